```python
import math
import jax, jax.numpy as jnp
from jax import lax
import numpy as np


D_MODEL = 1024
BATCH = 8
SEQ = 2048
DEPTH = 2

CHUNK = 64
EPS = 1e-6

GM_WIDTH = 1024
GM_GROUPS = 4
GM_BLOCK = 128

MLA_HEADS = 8
MLA_NOPE = 128
MLA_ROPE = 64
MLA_VDIM = 128
MLA_QK_DIM = MLA_NOPE + MLA_ROPE
MLA_Q_RANK = 384
MLA_KV_RANK = 256
MLA_WIDTH = MLA_HEADS * MLA_VDIM
ROPE_THETA = 10000.0
Q_BLOCK = 128

LRU_WIDTH = 1280
LRU_BLOCKS = 16
LRU_BW = LRU_WIDTH // LRU_BLOCKS
LRU_C = 8.0
CONV_W = 4

IN_SIZES = (GM_WIDTH, GM_WIDTH, GM_WIDTH, MLA_Q_RANK, MLA_KV_RANK, MLA_ROPE, MLA_WIDTH,
            LRU_WIDTH, LRU_WIDTH, D_MODEL, D_MODEL, D_MODEL)
N_IN = sum(IN_SIZES)

kernel_name = "hybrid_gmlp_mla_rglru_chunk_causal"


def _split_points():
    return [int(s) for s in np.cumsum(IN_SIZES)[:-1]]


def rmsnorm(x, g):
    xf = x.astype(jnp.float32)
    y = xf * lax.rsqrt(jnp.mean(xf * xf, axis=-1, keepdims=True) + EPS)
    return (y * g.astype(jnp.float32)).astype(x.dtype)


def layernorm(x, g, b):
    xf = x.astype(jnp.float32)
    mu = jnp.mean(xf, axis=-1, keepdims=True)
    var = jnp.mean(jnp.square(xf - mu), axis=-1, keepdims=True)
    y = (xf - mu) * lax.rsqrt(var + EPS)
    return (y * g.astype(jnp.float32) + b.astype(jnp.float32)).astype(x.dtype)


def rope(t, cos, sin):
    half = t.shape[-1] // 2
    t1, t2 = t[..., :half], t[..., half:]
    return jnp.concatenate([t1 * cos - t2 * sin, t2 * cos + t1 * sin], axis=-1)


def gmlp_spatial(u, v, ln_g, ln_b, ws, bs):
    B, S, _ = v.shape
    v = layernorm(v, ln_g, ln_b)
    vb = v.reshape(B, S // GM_BLOCK, GM_BLOCK, GM_GROUPS, GM_WIDTH // GM_GROUPS)
    idx = jnp.arange(GM_BLOCK)
    mask = (idx[None, :] // CHUNK) <= (idx[:, None] // CHUNK)
    ws_m = jnp.where(mask[None], ws, jnp.zeros_like(ws))
    sv = jnp.einsum('gij,bnjgc->bnigc', ws_m, vb) + bs.T[None, None, :, :, None]
    return u * sv.reshape(B, S, GM_WIDTH)


def mla_attention(c_q, c_kv, k_rope_in, q_norm_g, w_uq, kv_norm_g, w_ukv):
    B, S, _ = c_q.shape
    H = MLA_HEADS
    q = (rmsnorm(c_q, q_norm_g) @ w_uq).reshape(B, S, H, MLA_QK_DIM)
    kv = (rmsnorm(c_kv, kv_norm_g) @ w_ukv).reshape(B, S, H, MLA_NOPE + MLA_VDIM)
    q_nope, q_rope = q[..., :MLA_NOPE], q[..., MLA_NOPE:]
    k_nope, v = kv[..., :MLA_NOPE], kv[..., MLA_NOPE:]

    pos = jnp.arange(S, dtype=jnp.float32)
    inv_freq = ROPE_THETA ** (-jnp.arange(0, MLA_ROPE, 2, dtype=jnp.float32) / MLA_ROPE)
    ang = pos[:, None] * inv_freq[None, :]
    cos = jnp.cos(ang).astype(q.dtype)
    sin = jnp.sin(ang).astype(q.dtype)
    q_rope = rope(q_rope, cos[None, :, None, :], sin[None, :, None, :])
    k_rope = rope(k_rope_in, cos[None], sin[None])

    q = jnp.concatenate([q_nope, q_rope], axis=-1)
    k = jnp.concatenate([k_nope, jnp.broadcast_to(k_rope[:, :, None, :], (B, S, H, MLA_ROPE))], axis=-1)
    scale = 1.0 / math.sqrt(MLA_QK_DIM)
    nb = S // Q_BLOCK
    qb = q.reshape(B, nb, Q_BLOCK, H, MLA_QK_DIM).transpose(1, 0, 2, 3, 4)
    key_chunk = jnp.arange(S) // CHUNK

    def one_block(args):
        qi, bi = args
        s = jnp.einsum('bqhd,bkhd->bhqk', qi, k).astype(jnp.float32) * scale
        q_chunk = (bi * Q_BLOCK + jnp.arange(Q_BLOCK)) // CHUNK
        mask = key_chunk[None, :] <= q_chunk[:, None]
        s = jnp.where(mask[None, None], s, -1e30)
        p = jax.nn.softmax(s, axis=-1).astype(v.dtype)
        return jnp.einsum('bhqk,bkhd->bqhd', p, v)

    o = lax.map(one_block, (qb, jnp.arange(nb)))
    return o.transpose(1, 0, 2, 3, 4).reshape(B, S, MLA_WIDTH)


def rg_lru(x_c, conv_w, conv_b, w_a, b_a, w_x, b_x, lam):
    B, S, C = x_c.shape
    xc = lax.conv_general_dilated(
        x_c, conv_w[:, None, :].astype(x_c.dtype), window_strides=(1,), padding=[(CONV_W - 1, 0)],
        dimension_numbers=('NWC', 'WIO', 'NWC'), feature_group_count=C) + conv_b
    xb = xc.reshape(B, S, LRU_BLOCKS, LRU_BW)
    r = jax.nn.sigmoid(jnp.einsum('bshi,hij->bshj', xb, w_a).reshape(B, S, C) + b_a)
    i = jax.nn.sigmoid(jnp.einsum('bshi,hij->bshj', xb, w_x).reshape(B, S, C) + b_x)
    rf = r.astype(jnp.float32)
    log_a = -LRU_C * rf * jax.nn.softplus(-lam.astype(jnp.float32))
    a = jnp.exp(log_a)
    mult = jnp.sqrt(jnp.maximum(1.0 - jnp.exp(2.0 * log_a), 0.0))
    bterm = mult * (i.astype(jnp.float32) * xc.astype(jnp.float32))

    def combine(e1, e2):
        a1, b1 = e1
        a2, b2 = e2
        return a1 * a2, a2 * b1 + b2

    _, h = lax.associative_scan(combine, (a, bterm), axis=1)
    return h.astype(x_c.dtype)


def _fwd_setup_inputs(seed: int = 0) -> dict:
    key = jax.random.key(seed)
    ks = iter(jax.random.split(key, 32))
    L, D = DEPTH, D_MODEL

    def nrm(shape, scale):
        return jax.random.normal(next(ks), shape, jnp.float32) * scale

    def gain(shape):
        return 1.0 + nrm(shape, 0.02)

    a_init = jax.random.uniform(next(ks), (L, LRU_WIDTH), jnp.float32, 0.9, 0.999)
    s = a_init ** (1.0 / LRU_C)
    lam = jnp.log(s) - jnp.log1p(-s)

    return {
        "x": nrm((BATCH, SEQ, D), 1.0),
        "pre_norm_g": gain((L, D)),
        "w_in": nrm((L, D, N_IN), D ** -0.5),
        "gm_ln_g": gain((L, GM_WIDTH)),
        "gm_ln_b": nrm((L, GM_WIDTH), 0.02),
        "gm_ws": nrm((L, GM_GROUPS, GM_BLOCK, GM_BLOCK), GM_BLOCK ** -0.5),
        "gm_bs": 1.0 + nrm((L, GM_GROUPS, GM_BLOCK), 0.02),
        "mla_q_norm_g": gain((L, MLA_Q_RANK)),
        "mla_w_uq": nrm((L, MLA_Q_RANK, MLA_HEADS * MLA_QK_DIM), MLA_Q_RANK ** -0.5),
        "mla_kv_norm_g": gain((L, MLA_KV_RANK)),
        "mla_w_ukv": nrm((L, MLA_KV_RANK, MLA_HEADS * (MLA_NOPE + MLA_VDIM)), MLA_KV_RANK ** -0.5),
        "lru_conv_w": nrm((L, CONV_W, LRU_WIDTH), CONV_W ** -0.5),
        "lru_conv_b": nrm((L, LRU_WIDTH), 0.01),
        "lru_w_a": nrm((L, LRU_BLOCKS, LRU_BW, LRU_BW), LRU_BW ** -0.5),
        "lru_b_a": nrm((L, LRU_WIDTH), 0.01),
        "lru_w_x": nrm((L, LRU_BLOCKS, LRU_BW, LRU_BW), LRU_BW ** -0.5),
        "lru_b_x": nrm((L, LRU_WIDTH), 0.01),
        "lru_lambda": lam,
        "w_proj_a": nrm((L, GM_WIDTH, D), GM_WIDTH ** -0.5),
        "w_proj_b": nrm((L, MLA_WIDTH, D), MLA_WIDTH ** -0.5),
        "w_proj_c": nrm((L, LRU_WIDTH, D), LRU_WIDTH ** -0.5),
        "w_out": nrm((L, D, D), D ** -0.5),
        "post_norm_g": gain((L, D)),
    }


def _fwd_reference(x, pre_norm_g, w_in, gm_ln_g, gm_ln_b, gm_ws, gm_bs, mla_q_norm_g, mla_w_uq,
              mla_kv_norm_g, mla_w_ukv, lru_conv_w, lru_conv_b, lru_w_a, lru_b_a, lru_w_x,
              lru_b_x, lru_lambda, w_proj_a, w_proj_b, w_proj_c, w_out, post_norm_g):
    cuts = _split_points()
    for l in range(DEPTH):
        h = rmsnorm(x, pre_norm_g[l])
        proj = h @ w_in[l]
        (u, v, z_a, c_q, c_kv, k_rope, z_b, x_c, z_c,
         g_a, g_b, g_c) = jnp.split(proj, cuts, axis=-1)

        y_a = gmlp_spatial(u, v, gm_ln_g[l], gm_ln_b[l], gm_ws[l], gm_bs[l]) * jax.nn.silu(z_a)
        y_b = mla_attention(c_q, c_kv, k_rope, mla_q_norm_g[l], mla_w_uq[l],
                            mla_kv_norm_g[l], mla_w_ukv[l]) * jax.nn.silu(z_b)
        y_c = rg_lru(x_c, lru_conv_w[l], lru_conv_b[l], lru_w_a[l], lru_b_a[l],
                     lru_w_x[l], lru_b_x[l], lru_lambda[l]) * jax.nn.silu(z_c)

        merged = (jax.nn.sigmoid(g_a) * (y_a @ w_proj_a[l])
                  + jax.nn.sigmoid(g_b) * (y_b @ w_proj_b[l])
                  + jax.nn.sigmoid(g_c) * (y_c @ w_proj_c[l]))
        x = x + rmsnorm(merged @ w_out[l], post_norm_g[l])
    return x


import jax as _jax
import jax.numpy as _jnp

TWIN_FORMAT = 'train_step'
FWD_PARAMS = ['x', 'pre_norm_g', 'w_in', 'gm_ln_g', 'gm_ln_b', 'gm_ws', 'gm_bs', 'mla_q_norm_g', 'mla_w_uq', 'mla_kv_norm_g', 'mla_w_ukv', 'lru_conv_w', 'lru_conv_b', 'lru_w_a', 'lru_b_a', 'lru_w_x', 'lru_b_x', 'lru_lambda', 'w_proj_a', 'w_proj_b', 'w_proj_c', 'w_out', 'post_norm_g']
TWIN_WEIGHTS = ['pre_norm_g', 'w_in', 'gm_ln_g', 'gm_ln_b', 'gm_ws', 'gm_bs', 'mla_q_norm_g', 'mla_w_uq', 'mla_kv_norm_g', 'mla_w_ukv', 'lru_conv_w', 'lru_conv_b', 'lru_w_a', 'lru_b_a', 'lru_w_x', 'lru_b_x', 'lru_lambda', 'w_proj_a', 'w_proj_b', 'w_proj_c', 'w_out', 'post_norm_g']
TWIN_DIFF_INPUT = 'x'
TWIN_INPUTS = ['x', 'pre_norm_g', 'w_in', 'gm_ln_g', 'gm_ln_b', 'gm_ws', 'gm_bs', 'mla_q_norm_g', 'mla_w_uq', 'mla_kv_norm_g', 'mla_w_ukv', 'lru_conv_w', 'lru_conv_b', 'lru_w_a', 'lru_b_a', 'lru_w_x', 'lru_b_x', 'lru_lambda', 'w_proj_a', 'w_proj_b', 'w_proj_c', 'w_out', 'post_norm_g', 'loss_target', 'm_pre_norm_g', 'm_w_in', 'm_gm_ln_g', 'm_gm_ln_b', 'm_gm_ws', 'm_gm_bs', 'm_mla_q_norm_g', 'm_mla_w_uq', 'm_mla_kv_norm_g', 'm_mla_w_ukv', 'm_lru_conv_w', 'm_lru_conv_b', 'm_lru_w_a', 'm_lru_b_a', 'm_lru_w_x', 'm_lru_b_x', 'm_lru_lambda', 'm_w_proj_a', 'm_w_proj_b', 'm_w_proj_c', 'm_w_out', 'm_post_norm_g', 'v_pre_norm_g', 'v_w_in', 'v_gm_ln_g', 'v_gm_ln_b', 'v_gm_ws', 'v_gm_bs', 'v_mla_q_norm_g', 'v_mla_w_uq', 'v_mla_kv_norm_g', 'v_mla_w_ukv', 'v_lru_conv_w', 'v_lru_conv_b', 'v_lru_w_a', 'v_lru_b_a', 'v_lru_w_x', 'v_lru_b_x', 'v_lru_lambda', 'v_w_proj_a', 'v_w_proj_b', 'v_w_proj_c', 'v_w_out', 'v_post_norm_g']
TWIN_OUTPUTS = ['loss', 'grad_x', 'grad_pre_norm_g', 'grad_w_in', 'grad_gm_ln_g', 'grad_gm_ln_b', 'grad_gm_ws', 'grad_gm_bs', 'grad_mla_q_norm_g', 'grad_mla_w_uq', 'grad_mla_kv_norm_g', 'grad_mla_w_ukv', 'grad_lru_conv_w', 'grad_lru_conv_b', 'grad_lru_w_a', 'grad_lru_b_a', 'grad_lru_w_x', 'grad_lru_b_x', 'grad_lru_lambda', 'grad_w_proj_a', 'grad_w_proj_b', 'grad_w_proj_c', 'grad_w_out', 'grad_post_norm_g', 'delta_pre_norm_g', 'delta_w_in', 'delta_gm_ln_g', 'delta_gm_ln_b', 'delta_gm_ws', 'delta_gm_bs', 'delta_mla_q_norm_g', 'delta_mla_w_uq', 'delta_mla_kv_norm_g', 'delta_mla_w_ukv', 'delta_lru_conv_w', 'delta_lru_conv_b', 'delta_lru_w_a', 'delta_lru_b_a', 'delta_lru_w_x', 'delta_lru_b_x', 'delta_lru_lambda', 'delta_w_proj_a', 'delta_w_proj_b', 'delta_w_proj_c', 'delta_w_out', 'delta_post_norm_g', 'new_m_pre_norm_g', 'new_m_w_in', 'new_m_gm_ln_g', 'new_m_gm_ln_b', 'new_m_gm_ws', 'new_m_gm_bs', 'new_m_mla_q_norm_g', 'new_m_mla_w_uq', 'new_m_mla_kv_norm_g', 'new_m_mla_w_ukv', 'new_m_lru_conv_w', 'new_m_lru_conv_b', 'new_m_lru_w_a', 'new_m_lru_b_a', 'new_m_lru_w_x', 'new_m_lru_b_x', 'new_m_lru_lambda', 'new_m_w_proj_a', 'new_m_w_proj_b', 'new_m_w_proj_c', 'new_m_w_out', 'new_m_post_norm_g', 'new_v_pre_norm_g', 'new_v_w_in', 'new_v_gm_ln_g', 'new_v_gm_ln_b', 'new_v_gm_ws', 'new_v_gm_bs', 'new_v_mla_q_norm_g', 'new_v_mla_w_uq', 'new_v_mla_kv_norm_g', 'new_v_mla_w_ukv', 'new_v_lru_conv_w', 'new_v_lru_conv_b', 'new_v_lru_w_a', 'new_v_lru_b_a', 'new_v_lru_w_x', 'new_v_lru_b_x', 'new_v_lru_lambda', 'new_v_w_proj_a', 'new_v_w_proj_b', 'new_v_w_proj_c', 'new_v_w_out', 'new_v_post_norm_g']
TWIN_LEAF_KINDS = {'loss': 'loss', 'grad_x': 'grad_x', 'grad_pre_norm_g': 'grad_w', 'grad_w_in': 'grad_w', 'grad_gm_ln_g': 'grad_w', 'grad_gm_ln_b': 'grad_w', 'grad_gm_ws': 'grad_w', 'grad_gm_bs': 'grad_w', 'grad_mla_q_norm_g': 'grad_w', 'grad_mla_w_uq': 'grad_w', 'grad_mla_kv_norm_g': 'grad_w', 'grad_mla_w_ukv': 'grad_w', 'grad_lru_conv_w': 'grad_w', 'grad_lru_conv_b': 'grad_w', 'grad_lru_w_a': 'grad_w', 'grad_lru_b_a': 'grad_w', 'grad_lru_w_x': 'grad_w', 'grad_lru_b_x': 'grad_w', 'grad_lru_lambda': 'grad_w', 'grad_w_proj_a': 'grad_w', 'grad_w_proj_b': 'grad_w', 'grad_w_proj_c': 'grad_w', 'grad_w_out': 'grad_w', 'grad_post_norm_g': 'grad_w', 'delta_pre_norm_g': 'delta_w', 'delta_w_in': 'delta_w', 'delta_gm_ln_g': 'delta_w', 'delta_gm_ln_b': 'delta_w', 'delta_gm_ws': 'delta_w', 'delta_gm_bs': 'delta_w', 'delta_mla_q_norm_g': 'delta_w', 'delta_mla_w_uq': 'delta_w', 'delta_mla_kv_norm_g': 'delta_w', 'delta_mla_w_ukv': 'delta_w', 'delta_lru_conv_w': 'delta_w', 'delta_lru_conv_b': 'delta_w', 'delta_lru_w_a': 'delta_w', 'delta_lru_b_a': 'delta_w', 'delta_lru_w_x': 'delta_w', 'delta_lru_b_x': 'delta_w', 'delta_lru_lambda': 'delta_w', 'delta_w_proj_a': 'delta_w', 'delta_w_proj_b': 'delta_w', 'delta_w_proj_c': 'delta_w', 'delta_w_out': 'delta_w', 'delta_post_norm_g': 'delta_w', 'new_m_pre_norm_g': 'new_m', 'new_m_w_in': 'new_m', 'new_m_gm_ln_g': 'new_m', 'new_m_gm_ln_b': 'new_m', 'new_m_gm_ws': 'new_m', 'new_m_gm_bs': 'new_m', 'new_m_mla_q_norm_g': 'new_m', 'new_m_mla_w_uq': 'new_m', 'new_m_mla_kv_norm_g': 'new_m', 'new_m_mla_w_ukv': 'new_m', 'new_m_lru_conv_w': 'new_m', 'new_m_lru_conv_b': 'new_m', 'new_m_lru_w_a': 'new_m', 'new_m_lru_b_a': 'new_m', 'new_m_lru_w_x': 'new_m', 'new_m_lru_b_x': 'new_m', 'new_m_lru_lambda': 'new_m', 'new_m_w_proj_a': 'new_m', 'new_m_w_proj_b': 'new_m', 'new_m_w_proj_c': 'new_m', 'new_m_w_out': 'new_m', 'new_m_post_norm_g': 'new_m', 'new_v_pre_norm_g': 'new_v', 'new_v_w_in': 'new_v', 'new_v_gm_ln_g': 'new_v', 'new_v_gm_ln_b': 'new_v', 'new_v_gm_ws': 'new_v', 'new_v_gm_bs': 'new_v', 'new_v_mla_q_norm_g': 'new_v', 'new_v_mla_w_uq': 'new_v', 'new_v_mla_kv_norm_g': 'new_v', 'new_v_mla_w_ukv': 'new_v', 'new_v_lru_conv_w': 'new_v', 'new_v_lru_conv_b': 'new_v', 'new_v_lru_w_a': 'new_v', 'new_v_lru_b_a': 'new_v', 'new_v_lru_w_x': 'new_v', 'new_v_lru_b_x': 'new_v', 'new_v_lru_lambda': 'new_v', 'new_v_w_proj_a': 'new_v', 'new_v_w_proj_b': 'new_v', 'new_v_w_proj_c': 'new_v', 'new_v_w_out': 'new_v', 'new_v_post_norm_g': 'new_v'}


def _forward(args):
    return _fwd_reference(*[args[k] for k in FWD_PARAMS])


def _output_shape():
    out = _jax.eval_shape(lambda: _forward(_fwd_setup_inputs(0)))
    return out.shape, out.dtype

N_MICROBATCH = 1
ADAM_LR = 0.001
ADAM_B1 = 0.9
ADAM_B2 = 0.999
ADAM_EPS = 1e-08
ADAM_WD = 0.01
ADAM_STEP = 10
PER_EXAMPLE_BATCH_AXIS = {'x': 0, 'loss_target': 0}
SHARED_INPUTS = []
_WEIGHT_DTYPES = {'pre_norm_g': _jnp.float32, 'w_in': _jnp.float32, 'gm_ln_g': _jnp.float32, 'gm_ln_b': _jnp.float32, 'gm_ws': _jnp.float32, 'gm_bs': _jnp.float32, 'mla_q_norm_g': _jnp.float32, 'mla_w_uq': _jnp.float32, 'mla_kv_norm_g': _jnp.float32, 'mla_w_ukv': _jnp.float32, 'lru_conv_w': _jnp.float32, 'lru_conv_b': _jnp.float32, 'lru_w_a': _jnp.float32, 'lru_b_a': _jnp.float32, 'lru_w_x': _jnp.float32, 'lru_b_x': _jnp.float32, 'lru_lambda': _jnp.float32, 'w_proj_a': _jnp.float32, 'w_proj_b': _jnp.float32, 'w_proj_c': _jnp.float32, 'w_out': _jnp.float32, 'post_norm_g': _jnp.float32}
MOMENT_SCALE = {'pre_norm_g': 4.914428e-01, 'w_in': 1.476309e-01, 'gm_ln_g': 1.639734e-01, 'gm_ln_b': 1.688929e-01, 'gm_ws': 2.318485e-01, 'gm_bs': 2.544048e-01, 'mla_q_norm_g': 4.940534e-02, 'mla_w_uq': 2.473449e-02, 'mla_kv_norm_g': 8.571789e-02, 'mla_w_ukv': 2.961139e-02, 'lru_conv_w': 1.913273e-01, 'lru_conv_b': 3.199740e+00, 'lru_w_a': 9.490851e-02, 'lru_b_a': 5.856001e-02, 'lru_w_x': 1.756026e-01, 'lru_b_x': 5.960680e-02, 'lru_lambda': 9.113951e-02, 'w_proj_a': 2.652580e-01, 'w_proj_b': 3.334096e-02, 'w_proj_c': 2.420698e-01, 'w_out': 3.565962e-01, 'post_norm_g': 1.597121e+01}


def _to_microbatches(a, axis):
    t = _jnp.moveaxis(a, axis, 0)
    t = t.reshape((N_MICROBATCH, t.shape[0] // N_MICROBATCH) + t.shape[1:])
    return _jnp.moveaxis(t, 1, axis + 1)


def setup_inputs(seed: int = 0) -> dict:
    inp = _fwd_setup_inputs(seed)
    key = _jax.random.fold_in(_jax.random.key(seed), 7919)
    shape, _ = _output_shape()
    out = dict(inp)
    out["loss_target"] = _jax.random.normal(_jax.random.fold_in(key, 0), shape, _jnp.float32)
    for i, name in enumerate(TWIN_WEIGHTS):
        w = inp[name].astype(_jnp.float32)
        if MOMENT_SCALE is None:
            s = _jnp.sqrt(_jnp.mean(_jnp.square(w)) + 1e-30)
        else:
            s = MOMENT_SCALE[name]
        km, kv = _jax.random.split(_jax.random.fold_in(key, i + 1))
        out[name] = w
        out["m_" + name] = s * _jax.random.normal(km, w.shape, _jnp.float32)
        out["v_" + name] = (s * s) * _jax.random.uniform(kv, w.shape, _jnp.float32, 0.5, 1.5)
    if N_MICROBATCH > 1:
        for name, axis in PER_EXAMPLE_BATCH_AXIS.items():
            out[name] = _to_microbatches(out[name], axis)
    return {'x': out['x'], 'pre_norm_g': out['pre_norm_g'], 'w_in': out['w_in'], 'gm_ln_g': out['gm_ln_g'], 'gm_ln_b': out['gm_ln_b'], 'gm_ws': out['gm_ws'], 'gm_bs': out['gm_bs'], 'mla_q_norm_g': out['mla_q_norm_g'], 'mla_w_uq': out['mla_w_uq'], 'mla_kv_norm_g': out['mla_kv_norm_g'], 'mla_w_ukv': out['mla_w_ukv'], 'lru_conv_w': out['lru_conv_w'], 'lru_conv_b': out['lru_conv_b'], 'lru_w_a': out['lru_w_a'], 'lru_b_a': out['lru_b_a'], 'lru_w_x': out['lru_w_x'], 'lru_b_x': out['lru_b_x'], 'lru_lambda': out['lru_lambda'], 'w_proj_a': out['w_proj_a'], 'w_proj_b': out['w_proj_b'], 'w_proj_c': out['w_proj_c'], 'w_out': out['w_out'], 'post_norm_g': out['post_norm_g'], 'loss_target': out['loss_target'], 'm_pre_norm_g': out['m_pre_norm_g'], 'm_w_in': out['m_w_in'], 'm_gm_ln_g': out['m_gm_ln_g'], 'm_gm_ln_b': out['m_gm_ln_b'], 'm_gm_ws': out['m_gm_ws'], 'm_gm_bs': out['m_gm_bs'], 'm_mla_q_norm_g': out['m_mla_q_norm_g'], 'm_mla_w_uq': out['m_mla_w_uq'], 'm_mla_kv_norm_g': out['m_mla_kv_norm_g'], 'm_mla_w_ukv': out['m_mla_w_ukv'], 'm_lru_conv_w': out['m_lru_conv_w'], 'm_lru_conv_b': out['m_lru_conv_b'], 'm_lru_w_a': out['m_lru_w_a'], 'm_lru_b_a': out['m_lru_b_a'], 'm_lru_w_x': out['m_lru_w_x'], 'm_lru_b_x': out['m_lru_b_x'], 'm_lru_lambda': out['m_lru_lambda'], 'm_w_proj_a': out['m_w_proj_a'], 'm_w_proj_b': out['m_w_proj_b'], 'm_w_proj_c': out['m_w_proj_c'], 'm_w_out': out['m_w_out'], 'm_post_norm_g': out['m_post_norm_g'], 'v_pre_norm_g': out['v_pre_norm_g'], 'v_w_in': out['v_w_in'], 'v_gm_ln_g': out['v_gm_ln_g'], 'v_gm_ln_b': out['v_gm_ln_b'], 'v_gm_ws': out['v_gm_ws'], 'v_gm_bs': out['v_gm_bs'], 'v_mla_q_norm_g': out['v_mla_q_norm_g'], 'v_mla_w_uq': out['v_mla_w_uq'], 'v_mla_kv_norm_g': out['v_mla_kv_norm_g'], 'v_mla_w_ukv': out['v_mla_w_ukv'], 'v_lru_conv_w': out['v_lru_conv_w'], 'v_lru_conv_b': out['v_lru_conv_b'], 'v_lru_w_a': out['v_lru_w_a'], 'v_lru_b_a': out['v_lru_b_a'], 'v_lru_w_x': out['v_lru_w_x'], 'v_lru_b_x': out['v_lru_b_x'], 'v_lru_lambda': out['v_lru_lambda'], 'v_w_proj_a': out['v_w_proj_a'], 'v_w_proj_b': out['v_w_proj_b'], 'v_w_proj_c': out['v_w_proj_c'], 'v_w_out': out['v_w_out'], 'v_post_norm_g': out['v_post_norm_g']}


def _loss(weights, diff, rest, loss_target):
    with _jax.named_scope("forward"):
        args = {**rest, TWIN_DIFF_INPUT: diff, **{k: w.astype(_WEIGHT_DTYPES[k]) for k, w in weights.items()}}
        y = _forward(args)
    with _jax.named_scope("loss_head"):
        err = _jnp.square(y.astype(_jnp.float32) - loss_target)
        return 0.5 * _jnp.sum(_jnp.mean(err, axis=-1)) if err.ndim else 0.5 * err


def _adamw(w, g, m, v):
    m = ADAM_B1 * m + (1.0 - ADAM_B1) * g
    v = ADAM_B2 * v + (1.0 - ADAM_B2) * _jnp.square(g)
    m_hat = m / (1.0 - ADAM_B1 ** ADAM_STEP)
    v_hat = v / (1.0 - ADAM_B2 ** ADAM_STEP)
    delta = -ADAM_LR * (m_hat / (_jnp.sqrt(v_hat) + ADAM_EPS) + ADAM_WD * w)
    return delta, m, v


def reference(x, pre_norm_g, w_in, gm_ln_g, gm_ln_b, gm_ws, gm_bs, mla_q_norm_g, mla_w_uq, mla_kv_norm_g, mla_w_ukv, lru_conv_w, lru_conv_b, lru_w_a, lru_b_a, lru_w_x, lru_b_x, lru_lambda, w_proj_a, w_proj_b, w_proj_c, w_out, post_norm_g, loss_target, m_pre_norm_g, m_w_in, m_gm_ln_g, m_gm_ln_b, m_gm_ws, m_gm_bs, m_mla_q_norm_g, m_mla_w_uq, m_mla_kv_norm_g, m_mla_w_ukv, m_lru_conv_w, m_lru_conv_b, m_lru_w_a, m_lru_b_a, m_lru_w_x, m_lru_b_x, m_lru_lambda, m_w_proj_a, m_w_proj_b, m_w_proj_c, m_w_out, m_post_norm_g, v_pre_norm_g, v_w_in, v_gm_ln_g, v_gm_ln_b, v_gm_ws, v_gm_bs, v_mla_q_norm_g, v_mla_w_uq, v_mla_kv_norm_g, v_mla_w_ukv, v_lru_conv_w, v_lru_conv_b, v_lru_w_a, v_lru_b_a, v_lru_w_x, v_lru_b_x, v_lru_lambda, v_w_proj_a, v_w_proj_b, v_w_proj_c, v_w_out, v_post_norm_g):
    given = dict(x=x, pre_norm_g=pre_norm_g, w_in=w_in, gm_ln_g=gm_ln_g, gm_ln_b=gm_ln_b, gm_ws=gm_ws, gm_bs=gm_bs, mla_q_norm_g=mla_q_norm_g, mla_w_uq=mla_w_uq, mla_kv_norm_g=mla_kv_norm_g, mla_w_ukv=mla_w_ukv, lru_conv_w=lru_conv_w, lru_conv_b=lru_conv_b, lru_w_a=lru_w_a, lru_b_a=lru_b_a, lru_w_x=lru_w_x, lru_b_x=lru_b_x, lru_lambda=lru_lambda, w_proj_a=w_proj_a, w_proj_b=w_proj_b, w_proj_c=w_proj_c, w_out=w_out, post_norm_g=post_norm_g, loss_target=loss_target, m_pre_norm_g=m_pre_norm_g, m_w_in=m_w_in, m_gm_ln_g=m_gm_ln_g, m_gm_ln_b=m_gm_ln_b, m_gm_ws=m_gm_ws, m_gm_bs=m_gm_bs, m_mla_q_norm_g=m_mla_q_norm_g, m_mla_w_uq=m_mla_w_uq, m_mla_kv_norm_g=m_mla_kv_norm_g, m_mla_w_ukv=m_mla_w_ukv, m_lru_conv_w=m_lru_conv_w, m_lru_conv_b=m_lru_conv_b, m_lru_w_a=m_lru_w_a, m_lru_b_a=m_lru_b_a, m_lru_w_x=m_lru_w_x, m_lru_b_x=m_lru_b_x, m_lru_lambda=m_lru_lambda, m_w_proj_a=m_w_proj_a, m_w_proj_b=m_w_proj_b, m_w_proj_c=m_w_proj_c, m_w_out=m_w_out, m_post_norm_g=m_post_norm_g, v_pre_norm_g=v_pre_norm_g, v_w_in=v_w_in, v_gm_ln_g=v_gm_ln_g, v_gm_ln_b=v_gm_ln_b, v_gm_ws=v_gm_ws, v_gm_bs=v_gm_bs, v_mla_q_norm_g=v_mla_q_norm_g, v_mla_w_uq=v_mla_w_uq, v_mla_kv_norm_g=v_mla_kv_norm_g, v_mla_w_ukv=v_mla_w_ukv, v_lru_conv_w=v_lru_conv_w, v_lru_conv_b=v_lru_conv_b, v_lru_w_a=v_lru_w_a, v_lru_b_a=v_lru_b_a, v_lru_w_x=v_lru_w_x, v_lru_b_x=v_lru_b_x, v_lru_lambda=v_lru_lambda, v_w_proj_a=v_w_proj_a, v_w_proj_b=v_w_proj_b, v_w_proj_c=v_w_proj_c, v_w_out=v_w_out, v_post_norm_g=v_post_norm_g)
    weights = {n: given[n] for n in TWIN_WEIGHTS}
    shared = {n: given[n] for n in SHARED_INPUTS}
    per_example = {n: given[n] for n in ['x']}
    grad_fn = _jax.value_and_grad(_loss, argnums=(0, 1))

    def one_microbatch(ex, loss_target):
        ex = dict(ex)
        diff = ex.pop(TWIN_DIFF_INPUT)
        return grad_fn(weights, diff, {**shared, **ex}, loss_target)

    if N_MICROBATCH == 1:
        loss, (grad_w, grad_x) = one_microbatch(per_example, given["loss_target"])
    else:
        def body(carry, xs):
            loss_sum, grad_sum = carry
            l_k, (gw_k, gx_k) = one_microbatch(xs[0], xs[1])
            with _jax.named_scope("update"):
                return (loss_sum + l_k, _jax.tree.map(_jnp.add, grad_sum, gw_k)), gx_k

        init = (_jnp.zeros((), _jnp.float32), _jax.tree.map(_jnp.zeros_like, weights))
        (loss, grad_w), grad_x = _jax.lax.scan(body, init, (per_example, given["loss_target"]))
    with _jax.named_scope("update"):
        delta_w, new_m, new_v = {}, {}, {}
        for n in TWIN_WEIGHTS:
            delta_w[n], new_m[n], new_v[n] = _adamw(weights[n], grad_w[n], given["m_" + n], given["v_" + n])
    return (loss, grad_x, *[grad_w[n] for n in TWIN_WEIGHTS], *[delta_w[n] for n in TWIN_WEIGHTS],
            *[new_m[n] for n in TWIN_WEIGHTS], *[new_v[n] for n in TWIN_WEIGHTS])
```

```python
import functools
import math

import jax
import jax.numpy as jnp
from jax import lax
from jax.experimental import pallas as pl
from jax.experimental.pallas import tpu as pltpu

F32, BF16 = jnp.float32, jnp.bfloat16
MESH = pl.DeviceIdType.MESH

S, D, DEPTH = 2048, 1024, 2
CHUNK, EPS = 64, 1e-6
GM_W, GM_G, GM_B = 1024, 4, 128
H, NOPE, ROPE, VDIM = 8, 128, 64, 128
QR, KVR = 384, 256
MLA_W = H * VDIM
LRU_W, LRU_NB, LRU_BW, LRU_C, CONV_W = 1280, 16, 80, 8.0, 4
ROPE_THETA = 10000.0
IN_SIZES = (GM_W, GM_W, GM_W, QR, KVR, ROPE, MLA_W, LRU_W, LRU_W, D, D, D)
N_IN = sum(IN_SIZES)
N_CHIPS = 4
ADAM_LR, ADAM_B1, ADAM_B2, ADAM_EPS, ADAM_WD, ADAM_STEP = 0.001, 0.9, 0.999, 1e-08, 0.01, 10

HP = 256
O_U, O_V, O_ZA, O_ZB, O_GA, O_GB, O_GC = 0, 1024, 2048, 3072, 4096, 5120, 6144
O_CKV, O_KR, O_XC, O_ZC, O_CQ = 7168, 7424, 7680, 8960, 10368
NP = 10752
VMEM_LIMIT = 48 * 1024 * 1024


def _tile(dim, target):
    if dim <= target:
        return dim
    t = (target // 128) * 128
    while dim % t:
        t -= 128
    return t


def _sig(x):
    return jax.nn.sigmoid(x)


def _silu(x):
    return x * _sig(x)


def _dsilu(x):
    s = _sig(x)
    return s * (1.0 + x * (1.0 - s))


def _mm(a, b, mode, name, out_dtype=F32, tm=512, tn=512, tk=1024):
    if mode == "nn":
        (M, K), (K2, N) = a.shape, b.shape
    elif mode == "nt":
        (M, K), (N, K2) = a.shape, b.shape
    else:
        (K, M), (K2, N) = a.shape, b.shape
    assert K == K2, (name, a.shape, b.shape)
    tm, tn, tk = _tile(M, tm), _tile(N, tn), _tile(K, tk)
    nk = K // tk
    if mode == "tn":
        a_spec = pl.BlockSpec((tk, tm), lambda i, j, k: (k, i))
        lhs_c = 0
    else:
        a_spec = pl.BlockSpec((tm, tk), lambda i, j, k: (i, k))
        lhs_c = 1
    if mode == "nt":
        b_spec = pl.BlockSpec((tn, tk), lambda i, j, k: (j, k))
        rhs_c = 1
    else:
        b_spec = pl.BlockSpec((tk, tn), lambda i, j, k: (k, j))
        rhs_c = 0
    dims = (((lhs_c,), (rhs_c,)), ((), ()))

    def body(a_ref, b_ref, o_ref, acc_ref):
        k = pl.program_id(2)

        @pl.when(k == 0)
        def _():
            acc_ref[...] = jnp.zeros_like(acc_ref)

        acc_ref[...] += lax.dot_general(a_ref[...].astype(BF16), b_ref[...].astype(BF16), dims,
                                        preferred_element_type=F32)

        @pl.when(k == nk - 1)
        def _():
            o_ref[...] = acc_ref[...].astype(o_ref.dtype)

    return pl.pallas_call(
        body, name=name, grid=(M // tm, N // tn, nk),
        in_specs=[a_spec, b_spec],
        out_specs=pl.BlockSpec((tm, tn), lambda i, j, k: (i, j)),
        out_shape=jax.ShapeDtypeStruct((M, N), out_dtype),
        scratch_shapes=[pltpu.VMEM((tm, tn), F32)],
        compiler_params=pltpu.CompilerParams(dimension_semantics=("parallel", "parallel", "arbitrary"),
                                             vmem_limit_bytes=VMEM_LIMIT),
    )(a, b)


def _rows(fn, name, tm, rows, halos=(), fulls=(), outs=(), accs=()):
    n = S // tm
    in_specs, args = [], []
    for arr, w, cb in rows:
        in_specs.append(pl.BlockSpec((tm, w), functools.partial(lambda i, cb: (i, cb), cb=cb)))
        args.append(arr)
    for arr, w, cb, side in halos:
        if side == "prev":
            im = functools.partial(lambda i, cb: (jnp.maximum(i * (tm // 8) - 1, 0), cb), cb=cb)
        else:
            im = functools.partial(lambda i, cb: (jnp.minimum((i + 1) * (tm // 8), S // 8 - 1), cb), cb=cb)
        in_specs.append(pl.BlockSpec((8, w), im))
        args.append(arr)
    for arr in fulls:
        in_specs.append(pl.BlockSpec(arr.shape, functools.partial(lambda i, nd: (0,) * nd, nd=arr.ndim)))
        args.append(arr)
    out_shape = [jax.ShapeDtypeStruct((S, w), dt) for w, dt in outs]
    out_specs = [pl.BlockSpec((tm, w), lambda i: (i, 0)) for w, dt in outs]
    for shp in accs:
        out_shape.append(jax.ShapeDtypeStruct(shp, F32))
        out_specs.append(pl.BlockSpec(shp, functools.partial(lambda i, nd: (0,) * nd, nd=len(shp))))
    nr, nh, nf, no, na = len(rows), len(halos), len(fulls), len(outs), len(accs)

    def body(*refs):
        i = pl.program_id(0)
        ins, orefs = refs[:nr + nh + nf], refs[nr + nh + nf:]
        rv = [r[...] for r in ins[:nr]]
        hv = [r[...] for r in ins[nr:nr + nh]]
        fv = [r[...] for r in ins[nr + nh:]]
        o, a = fn(i, rv, hv, fv)
        assert len(o) == no and len(a) == na, name
        for ref, val in zip(orefs[:no], o):
            ref[...] = val.astype(ref.dtype)
        if na:
            @pl.when(i == 0)
            def _():
                for ref in orefs[no:]:
                    ref[...] = jnp.zeros_like(ref)

            for ref, val in zip(orefs[no:], a):
                ref[...] += val

    res = pl.pallas_call(
        body, name=name, grid=(n,), in_specs=in_specs, out_specs=out_specs, out_shape=out_shape,
        compiler_params=pltpu.CompilerParams(dimension_semantics=("arbitrary",), vmem_limit_bytes=VMEM_LIMIT),
    )(*args)
    return res


def _shift_down(xb, halo, s, row):
    fix = jnp.tile(pltpu.roll(halo, s, 0), (xb.shape[0] // 8, 1))
    return jnp.where(row >= s, pltpu.roll(xb, s, 0), fix)


def _shift_up(xb, halo, s, row):
    tm = xb.shape[0]
    fix = jnp.tile(pltpu.roll(halo, 8 - s, 0), (tm // 8, 1))
    return jnp.where(row < tm - s, pltpu.roll(xb, tm - s, 0), fix)


def _rms(x):
    return lax.rsqrt(jnp.mean(x * x, axis=-1, keepdims=True) + EPS)


def _rms_bwd(dy, x, g):
    r = _rms(x)
    xh = x * r
    dxh = dy * g
    dx = r * (dxh - xh * jnp.mean(dxh * xh, axis=-1, keepdims=True))
    return dx, dy * xh


def _colsum(x):
    return jnp.sum(x, axis=0, keepdims=True)


def _prenorm_fwd(x, g):
    def fn(i, rv, hv, fv):
        (xb,), (gg,) = rv, fv
        return [xb * _rms(xb) * gg], []
    return _rows(fn, "prenorm_fwd", 256, [(x, D, 0)], fulls=[g], outs=[(D, BF16)])[0]


def _gm_mask():
    r = lax.broadcasted_iota(jnp.int32, (GM_B, GM_B), 0) // CHUNK
    c = lax.broadcasted_iota(jnp.int32, (GM_B, GM_B), 1) // CHUNK
    return c <= r


def _gm_norm(v, g, b):
    mu = jnp.mean(v, axis=-1, keepdims=True)
    vc = v - mu
    rs = lax.rsqrt(jnp.mean(vc * vc, axis=-1, keepdims=True) + EPS)
    vh = vc * rs
    return vh, rs, vh * g + b


def _gm_sv(vn, ws, bst):
    mask = _gm_mask()
    gw = GM_W // GM_G
    parts = []
    for g in range(GM_G):
        wm = jnp.where(mask, ws[g], 0.0).astype(BF16)
        parts.append(jnp.dot(wm, vn[:, g * gw:(g + 1) * gw].astype(BF16), preferred_element_type=F32)
                     + bst[:, g:g + 1])
    return jnp.concatenate(parts, axis=1)


def _gmlp_fwd(proj, ln_g, ln_b, ws, bst):
    def fn(i, rv, hv, fv):
        u, v, z = rv
        g, b, w, bt = fv
        _, _, vn = _gm_norm(v, g, b)
        return [u * _gm_sv(vn, w, bt) * _silu(z)], []
    return _rows(fn, "gmlp_fwd", GM_B, [(proj, GM_W, 0), (proj, GM_W, 1), (proj, GM_W, 2)],
                 fulls=[ln_g, ln_b, ws, bst], outs=[(GM_W, BF16)])[0]


def _mla_prep_fwd(proj, qg, kvg):
    def fn(i, rv, hv, fv):
        cq, ckv = rv
        g1, g2 = fv
        return [cq * _rms(cq) * g1, ckv * _rms(ckv) * g2], []
    return _rows(fn, "mla_prep_fwd", 256, [(proj, QR, O_CQ // QR), (proj, KVR, O_CKV // KVR)],
                 fulls=[qg, kvg], outs=[(QR, BF16), (KVR, BF16)])


def _rot(t, cc, sa, sb):
    return t * cc + pltpu.roll(t, 32, 1) * sa + pltpu.roll(t, 96, 1) * sb


def _rot_t(g, cc, sa, sb):
    return g * cc + pltpu.roll(g * sa, 96, 1) + pltpu.roll(g * sb, 32, 1)


def _rope_tables():
    pos = jnp.arange(S, dtype=F32)
    inv_freq = ROPE_THETA ** (-jnp.arange(0, ROPE, 2, dtype=F32) / ROPE)
    ang = pos[:, None] * inv_freq[None, :]
    cos, sin, z = jnp.cos(ang), jnp.sin(ang), jnp.zeros((S, 32), F32)
    cc = jnp.concatenate([cos, cos, z, z], axis=1)
    sa = jnp.concatenate([z, sin, z, z], axis=1)
    sb = jnp.concatenate([-sin, z, z, z], axis=1)
    return cc, sa, sb


def _rope_fwd(q, kv, proj, tabs):
    def fn(i, rv, hv, fv):
        qb, kvb, kr, cc, sa, sb = rv
        krr = _rot(kr, cc, sa, sb)
        qs, ks = [], []
        for h in range(H):
            qs += [qb[:, h * HP:h * HP + 128], _rot(qb[:, h * HP + 128:(h + 1) * HP], cc, sa, sb)]
            ks += [kvb[:, h * 128:(h + 1) * 128], krr]
        return [jnp.concatenate(qs, axis=1), jnp.concatenate(ks, axis=1), kvb[:, H * NOPE:]], []
    cc, sa, sb = tabs
    return _rows(fn, "rope_fwd", 256,
                 [(q, H * HP, 0), (kv, H * 256, 0), (proj, 128, O_KR // 128), (cc, 128, 0), (sa, 128, 0), (sb, 128, 0)],
                 outs=[(H * HP, BF16), (H * HP, BF16), (MLA_W, BF16)])


TQ = 256


def _attn_mask(i, kb):
    qpos = i * TQ + lax.broadcasted_iota(jnp.int32, (TQ, TQ), 0)
    kpos = kb * TQ + lax.broadcasted_iota(jnp.int32, (TQ, TQ), 1)
    return (kpos // CHUNK) <= (qpos // CHUNK)


def _attn_fwd(qc, kc, vv):
    scale = 1.0 / math.sqrt(NOPE + ROPE)
    nt = (((1,), (1,)), ((), ()))

    def body(q_ref, k_ref, v_ref, o_ref, l_ref):
        i = pl.program_id(1)
        q = q_ref[...]

        def step(kb, carry):
            m, l, acc = carry
            t0 = pl.multiple_of(kb * TQ, TQ)
            k = k_ref[pl.ds(t0, TQ), :]
            v = v_ref[pl.ds(t0, TQ), :]
            s = lax.dot_general(q, k, nt, preferred_element_type=F32) * scale
            s = jnp.where(_attn_mask(i, kb), s, -1e30)
            m_new = jnp.maximum(m, jnp.max(s, axis=-1, keepdims=True))
            p = jnp.exp(s - m_new)
            alpha = jnp.exp(m - m_new)
            l = alpha * l + jnp.sum(p, axis=-1, keepdims=True)
            acc = alpha * acc + jnp.dot(p.astype(BF16), v, preferred_element_type=F32)
            return m_new, l, acc

        m, l, acc = lax.fori_loop(0, i + 1, step, (jnp.full((TQ, 1), -1e30, F32), jnp.zeros((TQ, 1), F32),
                                                   jnp.zeros((TQ, VDIM), F32)))
        o_ref[...] = acc / l
        l_ref[...] = jnp.broadcast_to(m + jnp.log(l), (TQ, 128))

    return pl.pallas_call(
        body, name="attn_fwd", grid=(H, S // TQ),
        in_specs=[pl.BlockSpec((TQ, HP), lambda h, i: (i, h)),
                  pl.BlockSpec((S, HP), lambda h, i: (0, h)),
                  pl.BlockSpec((S, VDIM), lambda h, i: (0, h))],
        out_specs=[pl.BlockSpec((TQ, VDIM), lambda h, i: (i, h)), pl.BlockSpec((TQ, 128), lambda h, i: (i, h))],
        out_shape=[jax.ShapeDtypeStruct((S, MLA_W), F32), jax.ShapeDtypeStruct((S, H * 128), F32)],
        compiler_params=pltpu.CompilerParams(dimension_semantics=("parallel", "arbitrary"),
                                             vmem_limit_bytes=VMEM_LIMIT),
    )(qc, kc, vv)


def _gate_mul_fwd(name, val, proj, width, cb):
    def fn(i, rv, hv, fv):
        o, z = rv
        return [o * _silu(z)], []
    return _rows(fn, name, 256, [(val, width, 0), (proj, width, cb)], outs=[(width, BF16)])[0]


def _conv_fwd(proj, w, b):
    def fn(i, rv, hv, fv):
        (xb,), (halo,), (ww, bb) = rv, hv, fv
        halo = jnp.where(i > 0, halo, 0.0)
        row = lax.broadcasted_iota(jnp.int32, xb.shape, 0)
        acc = bb + ww[3:4] * xb
        for s in range(1, CONV_W):
            acc = acc + ww[3 - s:4 - s] * _shift_down(xb, halo, s, row)
        return [acc, acc], []
    return _rows(fn, "conv_fwd", 128, [(proj, LRU_W, O_XC // LRU_W)], halos=[(proj, LRU_W, O_XC // LRU_W, "prev")],
                 fulls=[w, b], outs=[(LRU_W, F32), (LRU_W, BF16)])


def _lru_terms(ga, gx, xc, ba, bx, lam):
    r = _sig(ga + ba)
    ig = _sig(gx + bx)
    sp = jnp.maximum(-lam, 0.0) + jnp.log(1.0 + jnp.exp(-jnp.abs(lam)))
    log_a = -LRU_C * r * sp
    a = jnp.exp(log_a)
    e2 = jnp.exp(2.0 * log_a)
    om = 1.0 - e2
    mult = jnp.sqrt(jnp.maximum(om, 0.0))
    return r, ig, sp, a, e2, om, mult


def _lru_gates_fwd(gates, xc, ba, bx, lam):
    def fn(i, rv, hv, fv):
        ga, gx, x = rv
        r, ig, sp, a, e2, om, mult = _lru_terms(ga, gx, x, *fv)
        return [a, mult * (ig * x)], []
    return _rows(fn, "lru_gates_fwd", 128, [(gates, LRU_W, 0), (gates, LRU_W, 1), (xc, LRU_W, 0)],
                 fulls=[ba, bx, lam], outs=[(LRU_W, F32), (LRU_W, F32)])


SCAN_T, SCAN_CW = 64, 256


def _scan_fwd(a, b):
    def body(a_ref, b_ref, h_ref):
        row = lax.broadcasted_iota(jnp.int32, (SCAN_T, SCAN_CW), 0)

        def step(blk, hc):
            t0 = pl.multiple_of(blk * SCAN_T, SCAN_T)
            A = a_ref[pl.ds(t0, SCAN_T), :]
            B = b_ref[pl.ds(t0, SCAN_T), :]
            d = 1
            while d < SCAN_T:
                keep = row >= d
                A_s = jnp.where(keep, pltpu.roll(A, d, 0), 1.0)
                B_s = jnp.where(keep, pltpu.roll(B, d, 0), 0.0)
                B = A * B_s + B
                A = A * A_s
                d *= 2
            hh = A * hc + B
            h_ref[pl.ds(t0, SCAN_T), :] = hh
            return hh[SCAN_T - 1:SCAN_T, :]

        lax.fori_loop(0, S // SCAN_T, step, jnp.zeros((1, SCAN_CW), F32))

    spec = pl.BlockSpec((S, SCAN_CW), lambda j: (0, j))
    return pl.pallas_call(
        body, name="scan_fwd", grid=(LRU_W // SCAN_CW,), in_specs=[spec, spec], out_specs=spec,
        out_shape=jax.ShapeDtypeStruct((S, LRU_W), F32),
        compiler_params=pltpu.CompilerParams(dimension_semantics=("parallel",), vmem_limit_bytes=VMEM_LIMIT),
    )(a, b)


def _merge_fwd(pa, pb, pc, proj):
    def fn(i, rv, hv, fv):
        a, b, c, ga, gb, gc = rv
        return [_sig(ga) * a + _sig(gb) * b + _sig(gc) * c], []
    return _rows(fn, "merge_fwd", 256,
                 [(pa, D, 0), (pb, D, 0), (pc, D, 0), (proj, D, O_GA // D), (proj, D, O_GB // D), (proj, D, O_GC // D)],
                 outs=[(D, BF16)])[0]


def _post_fwd(x, o2, g):
    def fn(i, rv, hv, fv):
        xb, ob = rv
        return [xb + ob * _rms(ob) * fv[0]], []
    return _rows(fn, "post_fwd", 256, [(x, D, 0), (o2, D, 0)], fulls=[g], outs=[(D, F32)])[0]


def _layer_fwd(x, P, tabs):
    A = {"x": x}
    A["h"] = _prenorm_fwd(x, P["pre_g"])
    proj = A["proj"] = _mm(A["h"], P["wp"], "nn", "in_proj")
    A["ya"] = _gmlp_fwd(proj, P["ln_g"], P["ln_b"], P["ws"], P["bst"])
    A["cqn"], A["ckvn"] = _mla_prep_fwd(proj, P["qg"], P["kvg"])
    q = _mm(A["cqn"], P["wuq"], "nn", "q_up")
    kv = _mm(A["ckvn"], P["wukv"], "nn", "kv_up")
    A["qc"], A["kc"], A["vv"] = _rope_fwd(q, kv, proj, tabs)
    A["o"], A["lse"] = _attn_fwd(A["qc"], A["kc"], A["vv"])
    A["yb"] = _gate_mul_fwd("yb_fwd", A["o"], proj, MLA_W, O_ZB // MLA_W)
    A["xc"], A["xcb"] = _conv_fwd(proj, P["conv_w"], P["conv_b"])
    A["gates"] = _mm(A["xcb"], P["wlru"], "nn", "lru_gate_mm")
    A["a"], bterm = _lru_gates_fwd(A["gates"], A["xc"], P["ba"], P["bx"], P["lam"])
    A["hs"] = _scan_fwd(A["a"], bterm)
    A["yc"] = _gate_mul_fwd("yc_fwd", A["hs"], proj, LRU_W, O_ZC // LRU_W)
    A["pa"] = _mm(A["ya"], P["wpa"], "nn", "proj_a")
    A["pb"] = _mm(A["yb"], P["wpb"], "nn", "proj_b")
    A["pc"] = _mm(A["yc"], P["wpc"], "nn", "proj_c")
    A["merged"] = _merge_fwd(A["pa"], A["pb"], A["pc"], proj)
    A["o2"] = _mm(A["merged"], P["wout"], "nn", "out_proj")
    return _post_fwd(x, A["o2"], P["post_g"]), A


def _loss_fwd(y, tgt):
    def fn(i, rv, hv, fv):
        yb, tb = rv
        e = yb - tb
        part = 0.5 * jnp.sum(jnp.mean(e * e, axis=-1, keepdims=True), axis=0, keepdims=True)
        return [e * (1.0 / D)], [part]
    return _rows(fn, "loss", 256, [(y, D, 0), (tgt, D, 0)], outs=[(D, F32)], accs=[(1, 1)])


def _post_bwd(dxn, o2, g):
    def fn(i, rv, hv, fv):
        dy, ob = rv
        dx, dg = _rms_bwd(dy, ob, fv[0])
        return [dx], [_colsum(dg)]
    return _rows(fn, "post_bwd", 256, [(dxn, D, 0), (o2, D, 0)], fulls=[g], outs=[(D, BF16)], accs=[(1, D)])


def _merge_bwd(dm, pa, pb, pc, proj):
    def fn(i, rv, hv, fv):
        d, a, b, c, ga, gb, gc = rv
        outs_p, outs_g = [], []
        for p, gg in ((a, ga), (b, gb), (c, gc)):
            s = _sig(gg)
            outs_p.append(d * s)
            outs_g.append(d * p * s * (1.0 - s))
        return outs_p + outs_g, []
    return _rows(fn, "merge_bwd", 128,
                 [(dm, D, 0), (pa, D, 0), (pb, D, 0), (pc, D, 0),
                  (proj, D, O_GA // D), (proj, D, O_GB // D), (proj, D, O_GC // D)],
                 outs=[(D, BF16)] * 6)


def _gmlp_bwd(dya, proj, ln_g, ln_b, ws, bst):
    gw = GM_W // GM_G

    def fn(i, rv, hv, fv):
        dy, u, v, z = rv
        g, b, w, bt = fv
        vh, rs, vn = _gm_norm(v, g, b)
        sv = _gm_sv(vn, w, bt)
        sz = _silu(z)
        du = dy * sv * sz
        dsv = dy * u * sz
        dz = dy * u * sv * _dsilu(z)
        mask = _gm_mask()
        lane = lax.broadcasted_iota(jnp.int32, (GM_B, 128), 1)
        dvn_parts, dws, dbst = [], [], jnp.zeros((GM_B, 128), F32)
        for k in range(GM_G):
            wm = jnp.where(mask, w[k], 0.0).astype(BF16)
            dsk = dsv[:, k * gw:(k + 1) * gw]
            dskb = dsk.astype(BF16)
            dvn_parts.append(lax.dot_general(wm, dskb, (((0,), (0,)), ((), ())), preferred_element_type=F32))
            dwk = lax.dot_general(dskb, vn[:, k * gw:(k + 1) * gw].astype(BF16), (((1,), (1,)), ((), ())),
                                  preferred_element_type=F32)
            dws.append(jnp.where(mask, dwk, 0.0)[None])
            dbst = dbst + jnp.where(lane == k, jnp.sum(dsk, axis=1, keepdims=True), 0.0)
        dvn = jnp.concatenate(dvn_parts, axis=1)
        dvh = dvn * g
        dv = rs * (dvh - jnp.mean(dvh, axis=-1, keepdims=True) - vh * jnp.mean(dvh * vh, axis=-1, keepdims=True))
        return [du, dv, dz], [jnp.concatenate(dws, axis=0), dbst, _colsum(dvn * vh), _colsum(dvn)]
    return _rows(fn, "gmlp_bwd", GM_B, [(dya, GM_W, 0), (proj, GM_W, 0), (proj, GM_W, 1), (proj, GM_W, 2)],
                 fulls=[ln_g, ln_b, ws, bst], outs=[(GM_W, BF16)] * 3,
                 accs=[(GM_G, GM_B, GM_B), (GM_B, 128), (1, GM_W), (1, GM_W)])


def _yb_bwd(dyb, o, proj):
    def fn(i, rv, hv, fv):
        dy, ob, z = rv
        do = dy * _silu(z)
        prod = do * ob
        dl = [jnp.broadcast_to(jnp.sum(prod[:, h * VDIM:(h + 1) * VDIM], axis=1, keepdims=True), (dy.shape[0], 128))
              for h in range(H)]
        return [do, jnp.concatenate(dl, axis=1), dy * ob * _dsilu(z)], []
    return _rows(fn, "yb_bwd", 256, [(dyb, MLA_W, 0), (o, MLA_W, 0), (proj, MLA_W, O_ZB // MLA_W)],
                 outs=[(MLA_W, BF16), (H * 128, F32), (MLA_W, BF16)])


def _attn_bwd(qc, kc, vv, do, lse, dl):
    scale = 1.0 / math.sqrt(NOPE + ROPE)
    nt = (((1,), (1,)), ((), ()))
    tn = (((0,), (0,)), ((), ()))

    def body(q_ref, k_ref, v_ref, do_ref, l_ref, d_ref, dq_ref, dk_ref, dv_ref):
        i = pl.program_id(1)

        @pl.when(i == 0)
        def _():
            dk_ref[...] = jnp.zeros_like(dk_ref)
            dv_ref[...] = jnp.zeros_like(dv_ref)

        q = q_ref[...]
        dob = do_ref[...]
        lse_c = l_ref[...][:, 0:1]
        dl_c = d_ref[...][:, 0:1]

        def step(kb, dq):
            t0 = pl.multiple_of(kb * TQ, TQ)
            k = k_ref[pl.ds(t0, TQ), :]
            v = v_ref[pl.ds(t0, TQ), :]
            s = lax.dot_general(q, k, nt, preferred_element_type=F32) * scale
            p = jnp.where(_attn_mask(i, kb), jnp.exp(s - lse_c), 0.0)
            dp = lax.dot_general(dob, v, nt, preferred_element_type=F32)
            ds = (p * (dp - dl_c) * scale).astype(BF16)
            dk_ref[pl.ds(t0, TQ), :] += lax.dot_general(ds, q, tn, preferred_element_type=F32)
            dv_ref[pl.ds(t0, TQ), :] += lax.dot_general(p.astype(BF16), dob, tn, preferred_element_type=F32)
            return dq + jnp.dot(ds, k, preferred_element_type=F32)

        dq_ref[...] = lax.fori_loop(0, i + 1, step, jnp.zeros((TQ, HP), F32))

    blk = lambda w: pl.BlockSpec((TQ, w), lambda h, i: (i, h))
    head = lambda w: pl.BlockSpec((S, w), lambda h, i: (0, h))
    return pl.pallas_call(
        body, name="attn_bwd", grid=(H, S // TQ),
        in_specs=[blk(HP), head(HP), head(VDIM), blk(VDIM), blk(128), blk(128)],
        out_specs=[blk(HP), head(HP), head(VDIM)],
        out_shape=[jax.ShapeDtypeStruct((S, H * HP), F32), jax.ShapeDtypeStruct((S, H * HP), F32),
                   jax.ShapeDtypeStruct((S, MLA_W), F32)],
        compiler_params=pltpu.CompilerParams(dimension_semantics=("parallel", "arbitrary"),
                                             vmem_limit_bytes=VMEM_LIMIT),
    )(qc, kc, vv, do, lse, dl)


def _rope_bwd(dqc, dkc, dvv, tabs):
    def fn(i, rv, hv, fv):
        dq, dk, dv, cc, sa, sb = rv
        qs, ks = [], []
        dkr = jnp.zeros((dq.shape[0], 128), F32)
        for h in range(H):
            qs += [dq[:, h * HP:h * HP + 128], _rot_t(dq[:, h * HP + 128:(h + 1) * HP], cc, sa, sb)]
            ks.append(dk[:, h * HP:h * HP + 128])
            dkr = dkr + dk[:, h * HP + 128:(h + 1) * HP]
        return [jnp.concatenate(qs, axis=1), jnp.concatenate(ks + [dv], axis=1), _rot_t(dkr, cc, sa, sb)], []
    cc, sa, sb = tabs
    return _rows(fn, "rope_bwd", 256,
                 [(dqc, H * HP, 0), (dkc, H * HP, 0), (dvv, MLA_W, 0), (cc, 128, 0), (sa, 128, 0), (sb, 128, 0)],
                 outs=[(H * HP, BF16), (H * 256, BF16), (128, BF16)])


def _mla_prep_bwd(dcqn, dckvn, proj, qg, kvg):
    def fn(i, rv, hv, fv):
        d1, d2, cq, ckv = rv
        g1, g2 = fv
        dx1, dg1 = _rms_bwd(d1, cq, g1)
        dx2, dg2 = _rms_bwd(d2, ckv, g2)
        return [dx1, dx2], [_colsum(dg1), _colsum(dg2)]
    return _rows(fn, "mla_prep_bwd", 256,
                 [(dcqn, QR, 0), (dckvn, KVR, 0), (proj, QR, O_CQ // QR), (proj, KVR, O_CKV // KVR)],
                 fulls=[qg, kvg], outs=[(QR, BF16), (KVR, BF16)], accs=[(1, QR), (1, KVR)])


def _yc_bwd(dyc, hs, proj):
    def fn(i, rv, hv, fv):
        dy, hh, z = rv
        return [dy * _silu(z), dy * hh * _dsilu(z)], []
    return _rows(fn, "yc_bwd", 128, [(dyc, LRU_W, 0), (hs, LRU_W, 0), (proj, LRU_W, O_ZC // LRU_W)],
                 outs=[(LRU_W, F32), (LRU_W, BF16)])


def _scan_bwd(a, hs, dh):
    nblk = S // SCAN_T

    def body(a_ref, h_ref, dh_ref, da_ref, db_ref):
        row = lax.broadcasted_iota(jnp.int32, (SCAN_T, SCAN_CW), 0)

        def step(j, carry):
            gc, ac = carry
            blk = nblk - 1 - j
            t0 = pl.multiple_of(blk * SCAN_T, SCAN_T)
            av = a_ref[pl.ds(t0, SCAN_T), :]
            A = jnp.where(row < SCAN_T - 1, pltpu.roll(av, SCAN_T - 1, 0), ac)
            B = dh_ref[pl.ds(t0, SCAN_T), :]
            d = 1
            while d < SCAN_T:
                keep = row < SCAN_T - d
                A_s = jnp.where(keep, pltpu.roll(A, SCAN_T - d, 0), 1.0)
                B_s = jnp.where(keep, pltpu.roll(B, SCAN_T - d, 0), 0.0)
                B = A * B_s + B
                A = A * A_s
                d *= 2
            g = A * gc + B
            p0 = pl.multiple_of(jnp.maximum(t0 - 8, 0), 8)
            last = jnp.where(blk > 0, h_ref[pl.ds(p0, 8), :][7:8, :], 0.0)
            h_prev = jnp.where(row >= 1, pltpu.roll(h_ref[pl.ds(t0, SCAN_T), :], 1, 0), last)
            da_ref[pl.ds(t0, SCAN_T), :] = g * h_prev
            db_ref[pl.ds(t0, SCAN_T), :] = g
            return g[0:1, :], av[0:1, :]

        z = jnp.zeros((1, SCAN_CW), F32)
        lax.fori_loop(0, nblk, step, (z, z))

    spec = pl.BlockSpec((S, SCAN_CW), lambda j: (0, j))
    return pl.pallas_call(
        body, name="scan_bwd", grid=(LRU_W // SCAN_CW,), in_specs=[spec] * 3, out_specs=[spec] * 2,
        out_shape=[jax.ShapeDtypeStruct((S, LRU_W), F32)] * 2,
        compiler_params=pltpu.CompilerParams(dimension_semantics=("parallel",), vmem_limit_bytes=VMEM_LIMIT),
    )(a, hs, dh)


def _lru_gates_bwd(da, db, gates, xc, ba, bx, lam):
    def fn(i, rv, hv, fv):
        dav, dbv, ga, gx, x = rv
        bav, bxv, lamv = fv
        r, ig, sp, a, e2, om, mult = _lru_terms(ga, gx, x, bav, bxv, lamv)
        dmult = dbv * ig * x
        dig = dbv * mult * x
        dxc1 = dbv * mult * ig
        dlog_a = dav * a + jnp.where(om > 0.0, dmult * (-e2 / mult), 0.0)
        dr = dlog_a * (-LRU_C * sp)
        dga = dr * r * (1.0 - r)
        dgx = dig * ig * (1.0 - ig)
        dlam = _colsum(dlog_a * (-LRU_C * r)) * (-_sig(-lamv))
        return [jnp.concatenate([dga, dgx], axis=1), dxc1], [_colsum(dga), _colsum(dgx), dlam]
    return _rows(fn, "lru_gates_bwd", 128,
                 [(da, LRU_W, 0), (db, LRU_W, 0), (gates, LRU_W, 0), (gates, LRU_W, 1), (xc, LRU_W, 0)],
                 fulls=[ba, bx, lam], outs=[(2 * LRU_W, BF16), (LRU_W, F32)], accs=[(1, LRU_W)] * 3)


def _conv_bwd(dxc1, dxc2, proj, w):
    cb = O_XC // LRU_W

    def fn(i, rv, hv, fv):
        d1, d2, xb = rv
        n1, n2, xprev = hv
        ww = fv[0]
        last = i == S // 128 - 1
        dxc = d1 + d2
        nxt = jnp.where(last, 0.0, n1 + n2)
        xprev = jnp.where(i > 0, xprev, 0.0)
        row = lax.broadcasted_iota(jnp.int32, xb.shape, 0)
        dx = ww[3:4] * dxc
        dws = [None] * CONV_W
        dws[3] = _colsum(dxc * xb)
        for s in range(1, CONV_W):
            dx = dx + ww[3 - s:4 - s] * _shift_up(dxc, nxt, s, row)
            dws[3 - s] = _colsum(dxc * _shift_down(xb, xprev, s, row))
        return [dx], [jnp.concatenate(dws, axis=0), _colsum(dxc)]
    return _rows(fn, "conv_bwd", 128, [(dxc1, LRU_W, 0), (dxc2, LRU_W, 0), (proj, LRU_W, cb)],
                 halos=[(dxc1, LRU_W, 0, "next"), (dxc2, LRU_W, 0, "next"), (proj, LRU_W, cb, "prev")],
                 fulls=[w], outs=[(LRU_W, BF16)], accs=[(CONV_W, LRU_W), (1, LRU_W)])


def _prenorm_bwd(dxn, dh, x, g):
    def fn(i, rv, hv, fv):
        dy, dhh, xb = rv
        dx, dg = _rms_bwd(dhh, xb, fv[0])
        return [dy + dx], [_colsum(dg)]
    return _rows(fn, "prenorm_bwd", 256, [(dxn, D, 0), (dh, D, 0), (x, D, 0)], fulls=[g], outs=[(D, F32)],
                 accs=[(1, D)])


def _layer_bwd(dxn, A, P, tabs):
    G = {}
    proj = A["proj"]
    do2, G["post_g"] = _post_bwd(dxn, A["o2"], P["post_g"])
    dm = _mm(do2, P["wout"], "nt", "out_proj_dx")
    G["wout"] = _mm(A["merged"], do2, "tn", "out_proj_dw")
    dpa, dpb, dpc, dga, dgb, dgc = _merge_bwd(dm, A["pa"], A["pb"], A["pc"], proj)
    dya = _mm(dpa, P["wpa"], "nt", "proj_a_dx")
    G["wpa"] = _mm(A["ya"], dpa, "tn", "proj_a_dw")
    dyb = _mm(dpb, P["wpb"], "nt", "proj_b_dx")
    G["wpb"] = _mm(A["yb"], dpb, "tn", "proj_b_dw")
    dyc = _mm(dpc, P["wpc"], "nt", "proj_c_dx")
    G["wpc"] = _mm(A["yc"], dpc, "tn", "proj_c_dw")
    du, dv, dza, G["ws"], G["bst"], G["ln_g"], G["ln_b"] = _gmlp_bwd(dya, proj, P["ln_g"], P["ln_b"], P["ws"], P["bst"])
    do, dl, dzb = _yb_bwd(dyb, A["o"], proj)
    dqc, dkc, dvv = _attn_bwd(A["qc"], A["kc"], A["vv"], do, A["lse"], dl)
    dq, dkv, dkr = _rope_bwd(dqc, dkc, dvv, tabs)
    dcqn = _mm(dq, P["wuq"], "nt", "q_up_dx")
    G["wuq"] = _mm(A["cqn"], dq, "tn", "q_up_dw")
    dckvn = _mm(dkv, P["wukv"], "nt", "kv_up_dx")
    G["wukv"] = _mm(A["ckvn"], dkv, "tn", "kv_up_dw")
    dcq, dckv, G["qg"], G["kvg"] = _mla_prep_bwd(dcqn, dckvn, proj, P["qg"], P["kvg"])
    dhs, dzc = _yc_bwd(dyc, A["hs"], proj)
    da, db = _scan_bwd(A["a"], A["hs"], dhs)
    dgates, dxc1, G["ba"], G["bx"], G["lam"] = _lru_gates_bwd(da, db, A["gates"], A["xc"], P["ba"], P["bx"], P["lam"])
    dxc2 = _mm(dgates, P["wlru"], "nt", "lru_gate_dx")
    G["wlru"] = _mm(A["xcb"], dgates, "tn", "lru_gate_dw")
    dxcc, G["conv_w"], G["conv_b"] = _conv_bwd(dxc1, dxc2, proj, P["conv_w"])
    zpad = jnp.zeros((S, 128), BF16)
    dproj = jnp.concatenate([du, dv, dza, dzb, dga, dgb, dgc, dckv, dkr, zpad, dxcc, dzc, zpad, dcq], axis=1)
    dh = _mm(dproj, P["wp"], "nt", "in_proj_dx")
    G["wp"] = _mm(A["h"], dproj, "tn", "in_proj_dw")
    dx, G["pre_g"] = _prenorm_bwd(dxn, dh, A["x"], P["pre_g"])
    return dx, G


_ORIG_OFF = [0]
for _s in IN_SIZES:
    _ORIG_OFF.append(_ORIG_OFF[-1] + _s)
_SEG_ORDER = [0, 1, 2, 6, 9, 10, 11, 4, 5, None, 7, 8, None, 3]


def _pad_w_in(w):
    parts = []
    for seg in _SEG_ORDER:
        if seg is None:
            parts.append(jnp.zeros((w.shape[0], 128), w.dtype))
        else:
            parts.append(w[:, _ORIG_OFF[seg]:_ORIG_OFF[seg + 1]])
            if seg == 5:
                parts.append(jnp.zeros((w.shape[0], 64), w.dtype))
    return jnp.concatenate(parts, axis=1)


def _unpad_w_in(g):
    offs = {0: O_U, 1: O_V, 2: O_ZA, 3: O_CQ, 4: O_CKV, 5: O_KR, 6: O_ZB, 7: O_XC, 8: O_ZC, 9: O_GA, 10: O_GB, 11: O_GC}
    return jnp.concatenate([g[:, offs[k]:offs[k] + IN_SIZES[k]] for k in range(len(IN_SIZES))], axis=1)


def _pad_wuq(w):
    return jnp.pad(w.reshape(QR, H, NOPE + ROPE), ((0, 0), (0, 0), (0, HP - NOPE - ROPE))).reshape(QR, H * HP)


def _unpad_wuq(g):
    return g.reshape(QR, H, HP)[:, :, :NOPE + ROPE].reshape(QR, H * (NOPE + ROPE))


def _perm_wukv(w):
    return w.reshape(KVR, H, 2, 128).transpose(0, 2, 1, 3).reshape(KVR, 2 * H * 128)


def _unperm_wukv(g):
    return g.reshape(KVR, 2, H, 128).transpose(0, 2, 1, 3).reshape(KVR, 2 * H * 128)


def _blockdiag(w):
    eye = jnp.eye(LRU_NB, dtype=w.dtype)
    return (w[:, :, None, :] * eye[:, None, :, None]).reshape(LRU_W, LRU_W)


def _unblockdiag(g):
    g4 = g.reshape(LRU_NB, LRU_BW, LRU_NB, LRU_BW)
    return jnp.stack([g4[k, :, k, :] for k in range(LRU_NB)], axis=0)


_HBM = pl.BlockSpec(memory_space=pltpu.HBM)


def _position():
    return lax.axis_index("x"), lax.axis_index("y"), lax.axis_index("c")


def _allgather(blocks, name):
    n = len(blocks)

    def body(*refs):
        ins, outs = refs[:n], refs[n:2 * n]
        send, recv, lsem = refs[2 * n:]
        x, y, c = _position()
        me, sib = (x, y, c), (x, y, 1 - c)
        chips = [(1 - x, y), (x, 1 - y), (1 - x, 1 - y)]

        def cp(k, a, block, to, src=None):
            dst = outs[a].at[4 * block[0] + 2 * block[1] + block[2]]
            return pltpu.make_async_remote_copy(src_ref=dst if src is None else src, dst_ref=dst,
                                                send_sem=send.at[7 * a + k], recv_sem=recv.at[7 * a + k],
                                                device_id=to, device_id_type=MESH)

        mine = [pltpu.make_async_copy(ins[a], outs[a].at[4 * x + 2 * y + c], lsem.at[a]) for a in range(n)]
        for m in mine:
            m.start()
        first = []
        for a in range(n):
            first.append(cp(0, a, me, sib, src=ins[a]))
            first += [cp(1 + j, a, me, (*chip, c), src=ins[a]) for j, chip in enumerate(chips)]
        for f in first:
            f.start()
        passed = []
        for j, chip in enumerate(chips):
            for a in range(n):
                cp(1 + j, a, (*chip, c), me).wait_recv()
                p = cp(4 + j, a, (*chip, c), sib)
                p.start()
                passed.append(p)
        for a in range(n):
            cp(0, a, sib, me).wait_recv()
            for j, chip in enumerate(chips):
                cp(4 + j, a, (*chip, 1 - c), me).wait_recv()
        for f in first + passed:
            f.wait_send()
        for m in mine:
            m.wait()

    return pl.pallas_call(
        body, name=name,
        out_shape=[jax.ShapeDtypeStruct((8,) + b.shape, b.dtype) for b in blocks],
        in_specs=[_HBM] * n, out_specs=[_HBM] * n,
        scratch_shapes=[pltpu.SemaphoreType.DMA((7 * n,)), pltpu.SemaphoreType.DMA((7 * n,)),
                        pltpu.SemaphoreType.DMA((n,))],
    )(*blocks)


def _sibling_exchange(srcs, name, other_half):
    n = len(srcs)

    def body(*refs):
        ins, outs = refs[:n], refs[n:2 * n]
        send, recv = refs[2 * n:]
        x, y, c = _position()
        cps = []
        for a in range(n):
            src = ins[a].at[1 - c] if other_half else ins[a]
            cps.append(pltpu.make_async_remote_copy(src_ref=src, dst_ref=outs[a], send_sem=send.at[a],
                                                    recv_sem=recv.at[a], device_id=(x, y, 1 - c), device_id_type=MESH))
        for cpy in cps:
            cpy.start()
        for cpy in cps:
            cpy.wait()

    return pl.pallas_call(
        body, name=name,
        out_shape=[jax.ShapeDtypeStruct(s.shape[1:] if other_half else s.shape, s.dtype) for s in srcs],
        in_specs=[_HBM] * n, out_specs=[_HBM] * n,
        scratch_shapes=[pltpu.SemaphoreType.DMA((n,)), pltpu.SemaphoreType.DMA((n,))],
    )(*srcs)


def _chip_scatter(parts, name):
    n = len(parts)

    def body(*refs):
        ins, outs = refs[:n], refs[n:2 * n]
        send, recv, lsem = refs[2 * n:]
        x, y, c = _position()
        j = 2 * x + y
        chips = [(1 - x, y), (x, 1 - y), (1 - x, 1 - y)]
        local = [pltpu.make_async_copy(ins[a].at[j], outs[a].at[j], lsem.at[a]) for a in range(n)]
        for m in local:
            m.start()
        cps = []
        for a in range(n):
            for k, (px, py) in enumerate(chips):
                cps.append(pltpu.make_async_remote_copy(
                    src_ref=ins[a].at[2 * px + py], dst_ref=outs[a].at[j],
                    send_sem=send.at[3 * a + k], recv_sem=recv.at[3 * a + k],
                    device_id=(px, py, c), device_id_type=MESH))
        for cpy in cps:
            cpy.start()
        for cpy in cps:
            cpy.wait()
        for m in local:
            m.wait()

    return pl.pallas_call(
        body, name=name,
        out_shape=[jax.ShapeDtypeStruct(p.shape, p.dtype) for p in parts],
        in_specs=[_HBM] * n, out_specs=[_HBM] * n,
        scratch_shapes=[pltpu.SemaphoreType.DMA((3 * n,)), pltpu.SemaphoreType.DMA((3 * n,)),
                        pltpu.SemaphoreType.DMA((n,))],
    )(*parts)


def _row_tile(r):
    for t in (256, 128, 64, 32, 16, 8):
        if r % t == 0 and r > t:
            return t
    return r


def _pair_add(g, rb, c_arr, name):
    _, _, R, C = g.shape
    tr = _row_tile(R)

    def body(c_ref, g_ref, r_ref, o_ref):
        o_ref[...] = (g_ref[...] + r_ref[...]).astype(o_ref.dtype)

    return pl.pallas_call(
        body, name=name,
        grid_spec=pltpu.PrefetchScalarGridSpec(
            num_scalar_prefetch=1, grid=(N_CHIPS, R // tr),
            in_specs=[pl.BlockSpec((None, None, tr, C), lambda j, i, c_ref: (c_ref[0], j, i, 0)),
                      pl.BlockSpec((None, tr, C), lambda j, i, c_ref: (j, i, 0))],
            out_specs=pl.BlockSpec((None, tr, C), lambda j, i, c_ref: (j, i, 0))),
        out_shape=jax.ShapeDtypeStruct((N_CHIPS, R, C), BF16),
        compiler_params=pltpu.CompilerParams(dimension_semantics=("parallel", "parallel"), vmem_limit_bytes=VMEM_LIMIT),
    )(c_arr, g, rb)


def _sum_slabs(rb, name):
    n, R, C = rb.shape
    tr = _row_tile(R)

    def body(r_ref, o_ref):
        acc = r_ref[0].astype(F32)
        for k in range(1, n):
            acc = acc + r_ref[k].astype(F32)
        o_ref[...] = acc

    return pl.pallas_call(
        body, name=name, grid=(R // tr,),
        in_specs=[pl.BlockSpec((n, tr, C), lambda i: (0, i, 0))],
        out_specs=pl.BlockSpec((tr, C), lambda i: (i, 0)),
        out_shape=jax.ShapeDtypeStruct((R, C), F32),
        compiler_params=pltpu.CompilerParams(dimension_semantics=("parallel",), vmem_limit_bytes=VMEM_LIMIT),
    )(rb)


def _adamw(w, g, m, v, name):
    R, C = w.shape
    tr = _row_tile(R)
    c1 = 1.0 - ADAM_B1 ** ADAM_STEP
    c2 = 1.0 - ADAM_B2 ** ADAM_STEP

    def body(w_ref, g_ref, m_ref, v_ref, d_ref, mo_ref, vo_ref):
        gg = g_ref[...]
        mn = ADAM_B1 * m_ref[...] + (1.0 - ADAM_B1) * gg
        vn = ADAM_B2 * v_ref[...] + (1.0 - ADAM_B2) * (gg * gg)
        d_ref[...] = -ADAM_LR * ((mn / c1) / (jnp.sqrt(vn / c2) + ADAM_EPS) + ADAM_WD * w_ref[...])
        mo_ref[...] = mn
        vo_ref[...] = vn

    spec = pl.BlockSpec((tr, C), lambda i: (i, 0))
    return pl.pallas_call(
        body, name=name, grid=(R // tr,), in_specs=[spec] * 4, out_specs=[spec] * 3,
        out_shape=[jax.ShapeDtypeStruct((R, C), F32)] * 3,
        compiler_params=pltpu.CompilerParams(dimension_semantics=("parallel",), vmem_limit_bytes=VMEM_LIMIT),
    )(w, g, m, v)


SHARDED = ("w_in", "mla_w_uq", "mla_w_ukv", "lru_conv_w", "w_proj_a", "w_proj_b", "w_proj_c", "w_out")
COL_SHARDED = ("w_in", "mla_w_uq", "mla_w_ukv", "lru_conv_w")
SMALL = ("pre_norm_g", "gm_ln_g", "gm_ln_b", "gm_ws", "gm_bs", "mla_q_norm_g", "mla_kv_norm_g", "lru_conv_b",
         "lru_w_a", "lru_b_a", "lru_w_x", "lru_b_x", "lru_lambda", "post_norm_g")
WEIGHTS = ("pre_norm_g", "w_in", "gm_ln_g", "gm_ln_b", "gm_ws", "gm_bs", "mla_q_norm_g", "mla_w_uq",
           "mla_kv_norm_g", "mla_w_ukv", "lru_conv_w", "lru_conv_b", "lru_w_a", "lru_b_a", "lru_w_x", "lru_b_x",
           "lru_lambda", "w_proj_a", "w_proj_b", "w_proj_c", "w_out", "post_norm_g")


def _full_from_gathered(name, g):
    if name in COL_SHARDED:
        return g.transpose(1, 0, 2).reshape(g.shape[1], N_CHIPS * g.shape[2])
    return g.reshape(N_CHIPS * g.shape[1], g.shape[2])


def _to_shards(name, full):
    if name in COL_SHARDED:
        r, c = full.shape
        return full.reshape(r, N_CHIPS, c // N_CHIPS).transpose(1, 0, 2)
    r, c = full.shape
    return full.reshape(N_CHIPS, r // N_CHIPS, c)


def _prepare(l, full, small):
    P = {}
    P["wp"] = _pad_w_in(full["w_in"])
    P["wuq"] = _pad_wuq(full["mla_w_uq"])
    P["wukv"] = _perm_wukv(full["mla_w_ukv"])
    P["wpa"], P["wpb"], P["wpc"], P["wout"] = full["w_proj_a"], full["w_proj_b"], full["w_proj_c"], full["w_out"]
    P["conv_w"] = full["lru_conv_w"]
    P["wlru"] = jnp.concatenate([_blockdiag(small["lru_w_a"][l]), _blockdiag(small["lru_w_x"][l])], axis=1).astype(BF16)
    row = lambda n: small[n][l][None, :]
    P["pre_g"], P["post_g"] = row("pre_norm_g"), row("post_norm_g")
    P["ln_g"], P["ln_b"] = row("gm_ln_g"), row("gm_ln_b")
    P["ws"] = small["gm_ws"][l]
    P["bst"] = jnp.pad(small["gm_bs"][l].T, ((0, 0), (0, 128 - GM_G)))
    P["qg"], P["kvg"] = row("mla_q_norm_g"), row("mla_kv_norm_g")
    P["conv_b"], P["ba"], P["bx"], P["lam"] = row("lru_conv_b"), row("lru_b_a"), row("lru_b_x"), row("lru_lambda")
    return P


def _layer_grads(G):
    out = {
        "pre_norm_g": G["pre_g"][0], "post_norm_g": G["post_g"][0],
        "gm_ln_g": G["ln_g"][0], "gm_ln_b": G["ln_b"][0], "gm_ws": G["ws"], "gm_bs": G["bst"][:, :GM_G].T,
        "mla_q_norm_g": G["qg"][0], "mla_kv_norm_g": G["kvg"][0],
        "lru_conv_b": G["conv_b"][0], "lru_b_a": G["ba"][0], "lru_b_x": G["bx"][0], "lru_lambda": G["lam"][0],
        "lru_w_a": _unblockdiag(G["wlru"][:, :LRU_W]), "lru_w_x": _unblockdiag(G["wlru"][:, LRU_W:]),
        "w_in": _unpad_w_in(G["wp"]), "mla_w_uq": _unpad_wuq(G["wuq"]), "mla_w_ukv": _unperm_wukv(G["wukv"]),
        "lru_conv_w": G["conv_w"], "w_proj_a": G["wpa"], "w_proj_b": G["wpb"], "w_proj_c": G["wpc"], "w_out": G["wout"],
    }
    return out


def _pack_small(d):
    flat = jnp.concatenate([d[n].reshape(-1) for n in SMALL])
    n = flat.shape[0]
    rows = -(-n // (512 * 128)) * 512
    return jnp.pad(flat, (0, rows * 128 - n)).reshape(rows, 128)


def _unpack_small(packed, like):
    flat = packed.reshape(-1)
    out, off = {}, 0
    for n in SMALL:
        sz = like[n].size
        out[n] = flat[off:off + sz].reshape(like[n].shape)
        off += sz
    return out


def kernel(x, pre_norm_g, w_in, gm_ln_g, gm_ln_b, gm_ws, gm_bs, mla_q_norm_g, mla_w_uq, mla_kv_norm_g, mla_w_ukv, lru_conv_w, lru_conv_b, lru_w_a, lru_b_a, lru_w_x, lru_b_x, lru_lambda, w_proj_a, w_proj_b, w_proj_c, w_out, post_norm_g, loss_target, m_pre_norm_g, m_w_in, m_gm_ln_g, m_gm_ln_b, m_gm_ws, m_gm_bs, m_mla_q_norm_g, m_mla_w_uq, m_mla_kv_norm_g, m_mla_w_ukv, m_lru_conv_w, m_lru_conv_b, m_lru_w_a, m_lru_b_a, m_lru_w_x, m_lru_b_x, m_lru_lambda, m_w_proj_a, m_w_proj_b, m_w_proj_c, m_w_out, m_post_norm_g, v_pre_norm_g, v_w_in, v_gm_ln_g, v_gm_ln_b, v_gm_ws, v_gm_bs, v_mla_q_norm_g, v_mla_w_uq, v_mla_kv_norm_g, v_mla_w_ukv, v_lru_conv_w, v_lru_conv_b, v_lru_w_a, v_lru_b_a, v_lru_w_x, v_lru_b_x, v_lru_lambda, v_w_proj_a, v_w_proj_b, v_w_proj_c, v_w_out, v_post_norm_g):
    args = dict(locals())
    W = {n: args[n] for n in WEIGHTS}
    M = {n: args["m_" + n] for n in WEIGHTS}
    V = {n: args["v_" + n] for n in WEIGHTS}
    c = lax.axis_index("c")

    blocks = []
    for n in SHARDED:
        blk = lax.dynamic_index_in_dim(W[n], c, 0, keepdims=False)
        blocks.append(blk if n == "lru_conv_w" else blk.astype(BF16))
    gathered = _allgather(blocks, "weights_allgather")
    full = [{}, {}]
    for n, g in zip(SHARDED, gathered):
        g = g.reshape((N_CHIPS, DEPTH) + g.shape[1:])
        for l in range(DEPTH):
            full[l][n] = _full_from_gathered(n, g[:, l])
    small = {n: W[n] for n in SMALL}
    P = [_prepare(l, full[l], small) for l in range(DEPTH)]
    tabs = _rope_tables()

    h0 = x[0]
    h1, A0 = _layer_fwd(h0, P[0], tabs)
    h2, A1 = _layer_fwd(h1, P[1], tabs)
    dy, loss_part = _loss_fwd(h2, loss_target[0])
    d1, G1 = _layer_bwd(dy, A1, P[1], tabs)
    d0, G0 = _layer_bwd(d1, A0, P[0], tabs)
    LG = [_layer_grads(G0), _layer_grads(G1)]
    loss = lax.psum(loss_part[0, 0], ("x", "y", "c"))

    gl = [jnp.stack([_to_shards(n, LG[l][n]) for l in range(DEPTH)], axis=0) for n in SHARDED]
    from_sib = _sibling_exchange(gl, "grads_to_sibling", other_half=True)
    c_arr = jnp.reshape(c, (1,)).astype(jnp.int32)
    pair = [_pair_add(g, rb, c_arr, "pair_add_" + n) for n, g, rb in zip(SHARDED, gl, from_sib)]
    slabs = _chip_scatter(pair, "grads_chip_scatter")
    mine = [_sum_slabs(s, "sum_slabs_" + n) for n, s in zip(SHARDED, slabs)]
    theirs = _sibling_exchange(mine, "reduced_to_sibling", other_half=False)
    grads = {}
    for n, a, b in zip(SHARDED, mine, theirs):
        grads[n] = jnp.where(c == 0, jnp.stack([a, b]), jnp.stack([b, a]))

    packed = _pack_small({n: jnp.stack([LG[0][n], LG[1][n]]) for n in SMALL})
    (allp,) = _allgather([packed], "small_grads_allgather")
    small_sum = _sum_slabs(allp, "sum_small_grads")
    grads.update(_unpack_small(small_sum, small))

    delta, new_m, new_v = {}, {}, {}
    for n in SHARDED:
        shp = W[n].shape
        two_d = (shp[0] * shp[1], shp[2])
        d_, m_, v_ = _adamw(W[n].reshape(two_d), grads[n].reshape(two_d), M[n].reshape(two_d), V[n].reshape(two_d),
                            "adamw_" + n)
        delta[n], new_m[n], new_v[n] = d_.reshape(shp), m_.reshape(shp), v_.reshape(shp)
    d_, m_, v_ = _adamw(_pack_small(small), small_sum, _pack_small({n: M[n] for n in SMALL}),
                        _pack_small({n: V[n] for n in SMALL}), "adamw_small")
    delta.update(_unpack_small(d_, small))
    new_m.update(_unpack_small(m_, small))
    new_v.update(_unpack_small(v_, small))

    return (loss, d0[None], *[grads[n] for n in WEIGHTS], *[delta[n] for n in WEIGHTS],
            *[new_m[n] for n in WEIGHTS], *[new_v[n] for n in WEIGHTS])
```

```python
import functools
import math

import jax
import jax.numpy as jnp
from jax import lax
from jax.experimental import pallas as pl
from jax.experimental.pallas import tpu as pltpu

F32, BF16 = jnp.float32, jnp.bfloat16
MESH = pl.DeviceIdType.MESH

S, D, DEPTH = 2048, 1024, 2
CHUNK, EPS = 64, 1e-6
GM_W, GM_G, GM_B = 1024, 4, 128
H, NOPE, ROPE, VDIM = 8, 128, 64, 128
QR, KVR = 384, 256
MLA_W = H * VDIM
LRU_W, LRU_NB, LRU_BW, LRU_C, CONV_W = 1280, 16, 80, 8.0, 4
ROPE_THETA = 10000.0
IN_SIZES = (GM_W, GM_W, GM_W, QR, KVR, ROPE, MLA_W, LRU_W, LRU_W, D, D, D)
N_IN = sum(IN_SIZES)
N_CHIPS = 4
ADAM_LR, ADAM_B1, ADAM_B2, ADAM_EPS, ADAM_WD, ADAM_STEP = 0.001, 0.9, 0.999, 1e-08, 0.01, 10

HP = 256
O_U, O_V, O_ZA, O_ZB, O_GA, O_GB, O_GC = 0, 1024, 2048, 3072, 4096, 5120, 6144
O_CKV, O_KR, O_XC, O_ZC, O_CQ = 7168, 7424, 7680, 8960, 10368
NP = 10752
VMEM_LIMIT = 48 * 1024 * 1024


def _tile(dim, target):
    if dim <= target:
        return dim
    t = (target // 128) * 128
    while dim % t:
        t -= 128
    return t


def _sig(x):
    return jax.nn.sigmoid(x)


def _silu(x):
    return x * _sig(x)


def _dsilu(x):
    s = _sig(x)
    return s * (1.0 + x * (1.0 - s))


def _mm(a, b, mode, name, out_dtype=F32, tm=512, tn=512, tk=1024, b_lead=None, out_lead=None):
    b2 = b.shape[1:] if b_lead is not None else b.shape
    if mode == "nn":
        (M, K), (K2, N) = a.shape, b2
    elif mode == "nt":
        (M, K), (N, K2) = a.shape, b2
    else:
        (K, M), (K2, N) = a.shape, b2
    assert K == K2, (name, a.shape, b.shape)
    tm, tn, tk = _tile(M, tm), _tile(N, tn), _tile(K, tk)
    nk = K // tk
    if mode == "tn":
        a_spec = pl.BlockSpec((tk, tm), lambda i, j, k: (k, i))
        lhs_c = 0
    else:
        a_spec = pl.BlockSpec((tm, tk), lambda i, j, k: (i, k))
        lhs_c = 1
    b_blk, b_idx, rhs_c = ((tn, tk), (lambda i, j, k: (j, k)), 1) if mode == "nt" else ((tk, tn), (lambda i, j, k: (k, j)), 0)
    if b_lead is None:
        b_spec = pl.BlockSpec(b_blk, b_idx)
    else:
        b_spec = pl.BlockSpec((None,) + b_blk, functools.partial(lambda i, j, k, f, l: (l,) + f(i, j, k), f=b_idx, l=b_lead))
    dims = (((lhs_c,), (rhs_c,)), ((), ()))
    in_specs, args, aliases = [a_spec, b_spec], [a, b], {}
    if out_lead is None:
        out_spec = pl.BlockSpec((tm, tn), lambda i, j, k: (i, j))
        out_shape = jax.ShapeDtypeStruct((M, N), out_dtype)
    else:
        l_out, n_lead, buf = out_lead
        out_spec = pl.BlockSpec((None, tm, tn), functools.partial(lambda i, j, k, l: (l, i, j), l=l_out))
        out_shape = jax.ShapeDtypeStruct((n_lead, M, N), out_dtype)
        if buf is not None:
            in_specs.append(pl.BlockSpec(memory_space=pl.ANY))
            args.append(buf)
            aliases = {2: 0}

    def body(a_ref, b_ref, *rest):
        o_ref, acc_ref = rest[-2:]
        k = pl.program_id(2)

        @pl.when(k == 0)
        def _():
            acc_ref[...] = jnp.zeros_like(acc_ref)

        acc_ref[...] += lax.dot_general(a_ref[...].astype(BF16), b_ref[...].astype(BF16), dims,
                                        preferred_element_type=F32)

        @pl.when(k == nk - 1)
        def _():
            o_ref[...] = acc_ref[...].astype(o_ref.dtype)

    return pl.pallas_call(
        body, name=name, grid=(M // tm, N // tn, nk),
        in_specs=in_specs, out_specs=out_spec, out_shape=out_shape,
        scratch_shapes=[pltpu.VMEM((tm, tn), F32)], input_output_aliases=aliases,
        compiler_params=pltpu.CompilerParams(dimension_semantics=("parallel", "parallel", "arbitrary"),
                                             vmem_limit_bytes=VMEM_LIMIT),
    )(*args)


def _rows(fn, name, tm, rows, halos=(), fulls=(), outs=(), accs=()):
    n = S // tm
    in_specs, args = [], []
    for arr, w, cb in rows:
        in_specs.append(pl.BlockSpec((tm, w), functools.partial(lambda i, cb: (i, cb), cb=cb)))
        args.append(arr)
    for arr, w, cb, side in halos:
        if side == "prev":
            im = functools.partial(lambda i, cb: (jnp.maximum(i * (tm // 8) - 1, 0), cb), cb=cb)
        else:
            im = functools.partial(lambda i, cb: (jnp.minimum((i + 1) * (tm // 8), S // 8 - 1), cb), cb=cb)
        in_specs.append(pl.BlockSpec((8, w), im))
        args.append(arr)
    for arr in fulls:
        in_specs.append(pl.BlockSpec(arr.shape, functools.partial(lambda i, nd: (0,) * nd, nd=arr.ndim)))
        args.append(arr)
    out_shape = [jax.ShapeDtypeStruct((S, w), dt) for w, dt in outs]
    out_specs = [pl.BlockSpec((tm, w), lambda i: (i, 0)) for w, dt in outs]
    for shp in accs:
        out_shape.append(jax.ShapeDtypeStruct(shp, F32))
        out_specs.append(pl.BlockSpec(shp, functools.partial(lambda i, nd: (0,) * nd, nd=len(shp))))
    nr, nh, nf, no, na = len(rows), len(halos), len(fulls), len(outs), len(accs)

    def body(*refs):
        i = pl.program_id(0)
        ins, orefs = refs[:nr + nh + nf], refs[nr + nh + nf:]
        rv = [r[...] for r in ins[:nr]]
        hv = [r[...] for r in ins[nr:nr + nh]]
        fv = [r[...] for r in ins[nr + nh:]]
        o, a = fn(i, rv, hv, fv)
        assert len(o) == no and len(a) == na, name
        for ref, val in zip(orefs[:no], o):
            ref[...] = val.astype(ref.dtype)
        if na:
            @pl.when(i == 0)
            def _():
                for ref in orefs[no:]:
                    ref[...] = jnp.zeros_like(ref)

            for ref, val in zip(orefs[no:], a):
                ref[...] += val

    res = pl.pallas_call(
        body, name=name, grid=(n,), in_specs=in_specs, out_specs=out_specs, out_shape=out_shape,
        compiler_params=pltpu.CompilerParams(dimension_semantics=("arbitrary",), vmem_limit_bytes=VMEM_LIMIT),
    )(*args)
    return res


def _shift_down(xb, halo, s, row):
    fix = jnp.tile(pltpu.roll(halo, s, 0), (xb.shape[0] // 8, 1))
    return jnp.where(row >= s, pltpu.roll(xb, s, 0), fix)


def _shift_up(xb, halo, s, row):
    tm = xb.shape[0]
    fix = jnp.tile(pltpu.roll(halo, 8 - s, 0), (tm // 8, 1))
    return jnp.where(row < tm - s, pltpu.roll(xb, tm - s, 0), fix)


def _rms(x):
    return lax.rsqrt(jnp.mean(x * x, axis=-1, keepdims=True) + EPS)


def _rms_bwd(dy, x, g):
    r = _rms(x)
    xh = x * r
    dxh = dy * g
    dx = r * (dxh - xh * jnp.mean(dxh * xh, axis=-1, keepdims=True))
    return dx, dy * xh


def _colsum(x):
    return jnp.sum(x, axis=0, keepdims=True)


def _prenorm_fwd(x, g):
    def fn(i, rv, hv, fv):
        (xb,), (gg,) = rv, fv
        return [xb * _rms(xb) * gg], []
    return _rows(fn, "prenorm_fwd", 256, [(x, D, 0)], fulls=[g], outs=[(D, BF16)])[0]


def _gm_mask():
    r = lax.broadcasted_iota(jnp.int32, (GM_B, GM_B), 0) // CHUNK
    c = lax.broadcasted_iota(jnp.int32, (GM_B, GM_B), 1) // CHUNK
    return c <= r


def _gm_norm(v, g, b):
    mu = jnp.mean(v, axis=-1, keepdims=True)
    vc = v - mu
    rs = lax.rsqrt(jnp.mean(vc * vc, axis=-1, keepdims=True) + EPS)
    vh = vc * rs
    return vh, rs, vh * g + b


def _gm_sv(vn, ws, bst):
    mask = _gm_mask()
    gw = GM_W // GM_G
    parts = []
    for g in range(GM_G):
        wm = jnp.where(mask, ws[g], 0.0).astype(BF16)
        parts.append(jnp.dot(wm, vn[:, g * gw:(g + 1) * gw].astype(BF16), preferred_element_type=F32)
                     + bst[:, g:g + 1])
    return jnp.concatenate(parts, axis=1)


def _gmlp_fwd(proj, ln_g, ln_b, ws, bst):
    def fn(i, rv, hv, fv):
        u, v, z = rv
        g, b, w, bt = fv
        _, _, vn = _gm_norm(v, g, b)
        return [u * _gm_sv(vn, w, bt) * _silu(z)], []
    return _rows(fn, "gmlp_fwd", GM_B, [(proj, GM_W, 0), (proj, GM_W, 1), (proj, GM_W, 2)],
                 fulls=[ln_g, ln_b, ws, bst], outs=[(GM_W, BF16)])[0]


def _mla_prep_fwd(proj, qg, kvg):
    def fn(i, rv, hv, fv):
        cq, ckv = rv
        g1, g2 = fv
        return [cq * _rms(cq) * g1, ckv * _rms(ckv) * g2], []
    return _rows(fn, "mla_prep_fwd", 256, [(proj, QR, O_CQ // QR), (proj, KVR, O_CKV // KVR)],
                 fulls=[qg, kvg], outs=[(QR, BF16), (KVR, BF16)])


def _rot(t, cc, sa, sb):
    return t * cc + pltpu.roll(t, 32, 1) * sa + pltpu.roll(t, 96, 1) * sb


def _rot_t(g, cc, sa, sb):
    return g * cc + pltpu.roll(g * sa, 96, 1) + pltpu.roll(g * sb, 32, 1)


def _rope_tables():
    pos = jnp.arange(S, dtype=F32)
    inv_freq = ROPE_THETA ** (-jnp.arange(0, ROPE, 2, dtype=F32) / ROPE)
    ang = pos[:, None] * inv_freq[None, :]
    cos, sin, z = jnp.cos(ang), jnp.sin(ang), jnp.zeros((S, 32), F32)
    cc = jnp.concatenate([cos, cos, z, z], axis=1)
    sa = jnp.concatenate([z, sin, z, z], axis=1)
    sb = jnp.concatenate([-sin, z, z, z], axis=1)
    return cc, sa, sb


def _rope_fwd(q, kv, proj, tabs):
    def fn(i, rv, hv, fv):
        qb, kvb, kr, cc, sa, sb = rv
        krr = _rot(kr, cc, sa, sb)
        qs, ks = [], []
        for h in range(H):
            qs += [qb[:, h * HP:h * HP + 128], _rot(qb[:, h * HP + 128:(h + 1) * HP], cc, sa, sb)]
            ks += [kvb[:, h * 128:(h + 1) * 128], krr]
        return [jnp.concatenate(qs, axis=1), jnp.concatenate(ks, axis=1), kvb[:, H * NOPE:]], []
    cc, sa, sb = tabs
    return _rows(fn, "rope_fwd", 256,
                 [(q, H * HP, 0), (kv, H * 256, 0), (proj, 128, O_KR // 128), (cc, 128, 0), (sa, 128, 0), (sb, 128, 0)],
                 outs=[(H * HP, BF16), (H * HP, BF16), (MLA_W, BF16)])


TQ = 256


def _attn_mask(i, kb):
    qpos = i * TQ + lax.broadcasted_iota(jnp.int32, (TQ, TQ), 0)
    kpos = kb * TQ + lax.broadcasted_iota(jnp.int32, (TQ, TQ), 1)
    return (kpos // CHUNK) <= (qpos // CHUNK)


def _attn_fwd(qc, kc, vv):
    scale = 1.0 / math.sqrt(NOPE + ROPE)
    nt = (((1,), (1,)), ((), ()))

    def body(q_ref, k_ref, v_ref, o_ref, l_ref):
        i = pl.program_id(1)
        q = q_ref[...]

        def step(kb, carry):
            m, l, acc = carry
            t0 = pl.multiple_of(kb * TQ, TQ)
            k = k_ref[pl.ds(t0, TQ), :]
            v = v_ref[pl.ds(t0, TQ), :]
            s = lax.dot_general(q, k, nt, preferred_element_type=F32) * scale
            s = jnp.where(_attn_mask(i, kb), s, -1e30)
            m_new = jnp.maximum(m, jnp.max(s, axis=-1, keepdims=True))
            p = jnp.exp(s - m_new)
            alpha = jnp.exp(m - m_new)
            l = alpha * l + jnp.sum(p, axis=-1, keepdims=True)
            acc = alpha * acc + jnp.dot(p.astype(BF16), v, preferred_element_type=F32)
            return m_new, l, acc

        m, l, acc = lax.fori_loop(0, i + 1, step, (jnp.full((TQ, 1), -1e30, F32), jnp.zeros((TQ, 1), F32),
                                                   jnp.zeros((TQ, VDIM), F32)))
        o_ref[...] = acc / l
        l_ref[...] = jnp.broadcast_to(m + jnp.log(l), (TQ, 128))

    return pl.pallas_call(
        body, name="attn_fwd", grid=(H, S // TQ),
        in_specs=[pl.BlockSpec((TQ, HP), lambda h, i: (i, h)),
                  pl.BlockSpec((S, HP), lambda h, i: (0, h)),
                  pl.BlockSpec((S, VDIM), lambda h, i: (0, h))],
        out_specs=[pl.BlockSpec((TQ, VDIM), lambda h, i: (i, h)), pl.BlockSpec((TQ, 128), lambda h, i: (i, h))],
        out_shape=[jax.ShapeDtypeStruct((S, MLA_W), F32), jax.ShapeDtypeStruct((S, H * 128), F32)],
        compiler_params=pltpu.CompilerParams(dimension_semantics=("parallel", "arbitrary"),
                                             vmem_limit_bytes=VMEM_LIMIT),
    )(qc, kc, vv)


def _gate_mul_fwd(name, val, proj, width, cb):
    def fn(i, rv, hv, fv):
        o, z = rv
        return [o * _silu(z)], []
    return _rows(fn, name, 256, [(val, width, 0), (proj, width, cb)], outs=[(width, BF16)])[0]


def _conv_fwd(proj, w, b):
    def fn(i, rv, hv, fv):
        (xb,), (halo,), (ww, bb) = rv, hv, fv
        halo = jnp.where(i > 0, halo, 0.0)
        row = lax.broadcasted_iota(jnp.int32, xb.shape, 0)
        acc = bb + ww[3:4] * xb
        for s in range(1, CONV_W):
            acc = acc + ww[3 - s:4 - s] * _shift_down(xb, halo, s, row)
        return [acc, acc], []
    return _rows(fn, "conv_fwd", 128, [(proj, LRU_W, O_XC // LRU_W)], halos=[(proj, LRU_W, O_XC // LRU_W, "prev")],
                 fulls=[w, b], outs=[(LRU_W, F32), (LRU_W, BF16)])


def _lru_terms(ga, gx, xc, ba, bx, lam):
    r = _sig(ga + ba)
    ig = _sig(gx + bx)
    sp = jnp.maximum(-lam, 0.0) + jnp.log(1.0 + jnp.exp(-jnp.abs(lam)))
    log_a = -LRU_C * r * sp
    a = jnp.exp(log_a)
    e2 = jnp.exp(2.0 * log_a)
    om = 1.0 - e2
    mult = jnp.sqrt(jnp.maximum(om, 0.0))
    return r, ig, sp, a, e2, om, mult


def _lru_gates_fwd(gates, xc, ba, bx, lam):
    def fn(i, rv, hv, fv):
        ga, gx, x = rv
        r, ig, sp, a, e2, om, mult = _lru_terms(ga, gx, x, *fv)
        return [a, mult * (ig * x)], []
    return _rows(fn, "lru_gates_fwd", 128, [(gates, LRU_W, 0), (gates, LRU_W, 1), (xc, LRU_W, 0)],
                 fulls=[ba, bx, lam], outs=[(LRU_W, F32), (LRU_W, F32)])


SCAN_T, SCAN_CW = 64, 256


def _scan_fwd(a, b):
    def body(a_ref, b_ref, h_ref):
        row = lax.broadcasted_iota(jnp.int32, (SCAN_T, SCAN_CW), 0)

        def step(blk, hc):
            t0 = pl.multiple_of(blk * SCAN_T, SCAN_T)
            A = a_ref[pl.ds(t0, SCAN_T), :]
            B = b_ref[pl.ds(t0, SCAN_T), :]
            d = 1
            while d < SCAN_T:
                keep = row >= d
                A_s = jnp.where(keep, pltpu.roll(A, d, 0), 1.0)
                B_s = jnp.where(keep, pltpu.roll(B, d, 0), 0.0)
                B = A * B_s + B
                A = A * A_s
                d *= 2
            hh = A * hc + B
            h_ref[pl.ds(t0, SCAN_T), :] = hh
            return hh[SCAN_T - 1:SCAN_T, :]

        lax.fori_loop(0, S // SCAN_T, step, jnp.zeros((1, SCAN_CW), F32))

    spec = pl.BlockSpec((S, SCAN_CW), lambda j: (0, j))
    return pl.pallas_call(
        body, name="scan_fwd", grid=(LRU_W // SCAN_CW,), in_specs=[spec, spec], out_specs=spec,
        out_shape=jax.ShapeDtypeStruct((S, LRU_W), F32),
        compiler_params=pltpu.CompilerParams(dimension_semantics=("parallel",), vmem_limit_bytes=VMEM_LIMIT),
    )(a, b)


def _merge_fwd(pa, pb, pc, proj):
    def fn(i, rv, hv, fv):
        a, b, c, ga, gb, gc = rv
        return [_sig(ga) * a + _sig(gb) * b + _sig(gc) * c], []
    return _rows(fn, "merge_fwd", 256,
                 [(pa, D, 0), (pb, D, 0), (pc, D, 0), (proj, D, O_GA // D), (proj, D, O_GB // D), (proj, D, O_GC // D)],
                 outs=[(D, BF16)])[0]


def _post_fwd(x, o2, g):
    def fn(i, rv, hv, fv):
        xb, ob = rv
        return [xb + ob * _rms(ob) * fv[0]], []
    return _rows(fn, "post_fwd", 256, [(x, D, 0), (o2, D, 0)], fulls=[g], outs=[(D, F32)])[0]


SB = 640
BD_TM = 512


def _bd_fwd(xcb, wsb):
    def body(x_ref, w_ref, o_ref):
        o_ref[...] = jnp.dot(x_ref[...], w_ref[...], preferred_element_type=F32)

    return pl.pallas_call(
        body, name="lru_gate_mm", grid=(S // BD_TM, 4),
        in_specs=[pl.BlockSpec((BD_TM, SB), lambda i, q: (i, q % 2)), pl.BlockSpec((None, SB, SB), lambda i, q: (q, 0, 0))],
        out_specs=pl.BlockSpec((BD_TM, SB), lambda i, q: (i, q)),
        out_shape=jax.ShapeDtypeStruct((S, 2 * LRU_W), F32),
        compiler_params=pltpu.CompilerParams(dimension_semantics=("parallel", "parallel"), vmem_limit_bytes=VMEM_LIMIT),
    )(xcb, wsb)


def _bd_dx(dgates, wsb):
    def body(d_ref, w_ref, o_ref, acc_ref):
        g = pl.program_id(2)

        @pl.when(g == 0)
        def _():
            acc_ref[...] = jnp.zeros_like(acc_ref)

        acc_ref[...] += lax.dot_general(d_ref[...], w_ref[...], (((1,), (1,)), ((), ())), preferred_element_type=F32)

        @pl.when(g == 1)
        def _():
            o_ref[...] = acc_ref[...]

    return pl.pallas_call(
        body, name="lru_gate_dx", grid=(S // BD_TM, 2, 2),
        in_specs=[pl.BlockSpec((BD_TM, SB), lambda i, s, g: (i, 2 * g + s)),
                  pl.BlockSpec((None, SB, SB), lambda i, s, g: (2 * g + s, 0, 0))],
        out_specs=pl.BlockSpec((BD_TM, SB), lambda i, s, g: (i, s)),
        out_shape=jax.ShapeDtypeStruct((S, LRU_W), F32),
        scratch_shapes=[pltpu.VMEM((BD_TM, SB), F32)],
        compiler_params=pltpu.CompilerParams(dimension_semantics=("parallel", "parallel", "arbitrary"),
                                             vmem_limit_bytes=VMEM_LIMIT),
    )(dgates, wsb)


def _bd_dw(xcb, dgates):
    tk = 1024

    def body(x_ref, d_ref, o_ref):
        @pl.when(pl.program_id(1) == 0)
        def _():
            o_ref[...] = jnp.zeros_like(o_ref)

        o_ref[...] += lax.dot_general(x_ref[...], d_ref[...], (((0,), (0,)), ((), ())), preferred_element_type=F32)

    return pl.pallas_call(
        body, name="lru_gate_dw", grid=(4, S // tk),
        in_specs=[pl.BlockSpec((tk, SB), lambda q, k: (k, q % 2)), pl.BlockSpec((tk, SB), lambda q, k: (k, q))],
        out_specs=pl.BlockSpec((None, SB, SB), lambda q, k: (q, 0, 0)),
        out_shape=jax.ShapeDtypeStruct((4, SB, SB), F32),
        compiler_params=pltpu.CompilerParams(dimension_semantics=("parallel", "arbitrary"), vmem_limit_bytes=VMEM_LIMIT),
    )(xcb, dgates)


def _bd_extract(dwsb):
    def body(w_ref, o_ref):
        lane = lax.broadcasted_iota(jnp.int32, (LRU_BW, 128), 1)
        for q in range(4):
            for kk in range(8):
                c0 = LRU_BW * kk
                w0, off = (c0 // 128) * 128, c0 % 128
                rows = pl.ds(LRU_BW * kk, LRU_BW)
                blk = w_ref[q, rows, w0:w0 + 128]
                if off:
                    blk = pltpu.roll(blk, 128 - off, 1)
                    if off + LRU_BW > 128:
                        nxt = pltpu.roll(w_ref[q, rows, w0 + 128:w0 + 256], 128 - off, 1)
                        blk = jnp.where(lane < 128 - off, blk, nxt)
                o_ref[q // 2, 8 * (q % 2) + kk] = blk.astype(BF16)

    return pl.pallas_call(
        body, name="lru_gate_dw_blocks",
        in_specs=[pl.BlockSpec(memory_space=pltpu.VMEM)], out_specs=pl.BlockSpec(memory_space=pltpu.VMEM),
        out_shape=jax.ShapeDtypeStruct((2, LRU_NB, LRU_BW, 128), BF16),
        compiler_params=pltpu.CompilerParams(vmem_limit_bytes=VMEM_LIMIT),
    )(dwsb)


def _layer_fwd(x, P, l, tabs):
    A = {"x": x}
    A["h"] = _prenorm_fwd(x, P["pre_g"])
    proj = A["proj"] = _mm(A["h"], P["wp"], "nt", "in_proj", b_lead=l)
    A["ya"] = _gmlp_fwd(proj, P["ln_g"], P["ln_b"], P["ws"], P["bst"])
    A["cqn"], A["ckvn"] = _mla_prep_fwd(proj, P["qg"], P["kvg"])
    q = _mm(A["cqn"], P["wuq"], "nt", "q_up", b_lead=l)
    kv = _mm(A["ckvn"], P["wukv"], "nt", "kv_up", b_lead=l)
    A["qc"], A["kc"], A["vv"] = _rope_fwd(q, kv, proj, tabs)
    A["o"], A["lse"] = _attn_fwd(A["qc"], A["kc"], A["vv"])
    A["yb"] = _gate_mul_fwd("yb_fwd", A["o"], proj, MLA_W, O_ZB // MLA_W)
    A["xc"], A["xcb"] = _conv_fwd(proj, P["conv_w"], P["conv_b"])
    A["gates"] = _bd_fwd(A["xcb"], P["wsb"])
    A["a"], bterm = _lru_gates_fwd(A["gates"], A["xc"], P["ba"], P["bx"], P["lam"])
    A["hs"] = _scan_fwd(A["a"], bterm)
    A["yc"] = _gate_mul_fwd("yc_fwd", A["hs"], proj, LRU_W, O_ZC // LRU_W)
    A["pa"] = _mm(A["ya"], P["wpa"], "nn", "proj_a", b_lead=l)
    A["pb"] = _mm(A["yb"], P["wpb"], "nn", "proj_b", b_lead=l)
    A["pc"] = _mm(A["yc"], P["wpc"], "nn", "proj_c", b_lead=l)
    A["merged"] = _merge_fwd(A["pa"], A["pb"], A["pc"], proj)
    A["o2"] = _mm(A["merged"], P["wout"], "nn", "out_proj", b_lead=l)
    return _post_fwd(x, A["o2"], P["post_g"]), A


def _loss_fwd(y, tgt):
    def fn(i, rv, hv, fv):
        yb, tb = rv
        e = yb - tb
        part = 0.5 * jnp.sum(jnp.mean(e * e, axis=-1, keepdims=True), axis=0, keepdims=True)
        return [e * (1.0 / D)], [part]
    return _rows(fn, "loss", 256, [(y, D, 0), (tgt, D, 0)], outs=[(D, F32)], accs=[(1, 1)])


def _post_bwd(dxn, o2, g):
    def fn(i, rv, hv, fv):
        dy, ob = rv
        dx, dg = _rms_bwd(dy, ob, fv[0])
        return [dx], [_colsum(dg)]
    return _rows(fn, "post_bwd", 256, [(dxn, D, 0), (o2, D, 0)], fulls=[g], outs=[(D, BF16)], accs=[(1, D)])


def _merge_bwd(dm, pa, pb, pc, proj):
    def fn(i, rv, hv, fv):
        d, a, b, c, ga, gb, gc = rv
        outs_p, outs_g = [], []
        for p, gg in ((a, ga), (b, gb), (c, gc)):
            s = _sig(gg)
            outs_p.append(d * s)
            outs_g.append(d * p * s * (1.0 - s))
        return outs_p + outs_g, []
    return _rows(fn, "merge_bwd", 128,
                 [(dm, D, 0), (pa, D, 0), (pb, D, 0), (pc, D, 0),
                  (proj, D, O_GA // D), (proj, D, O_GB // D), (proj, D, O_GC // D)],
                 outs=[(D, BF16)] * 6)


def _gmlp_bwd(dya, proj, ln_g, ln_b, ws, bst):
    gw = GM_W // GM_G

    def fn(i, rv, hv, fv):
        dy, u, v, z = rv
        g, b, w, bt = fv
        vh, rs, vn = _gm_norm(v, g, b)
        sv = _gm_sv(vn, w, bt)
        sz = _silu(z)
        du = dy * sv * sz
        dsv = dy * u * sz
        dz = dy * u * sv * _dsilu(z)
        mask = _gm_mask()
        lane = lax.broadcasted_iota(jnp.int32, (GM_B, 128), 1)
        dvn_parts, dws, dbst = [], [], jnp.zeros((GM_B, 128), F32)
        for k in range(GM_G):
            wm = jnp.where(mask, w[k], 0.0).astype(BF16)
            dsk = dsv[:, k * gw:(k + 1) * gw]
            dskb = dsk.astype(BF16)
            dvn_parts.append(lax.dot_general(wm, dskb, (((0,), (0,)), ((), ())), preferred_element_type=F32))
            dwk = lax.dot_general(dskb, vn[:, k * gw:(k + 1) * gw].astype(BF16), (((1,), (1,)), ((), ())),
                                  preferred_element_type=F32)
            dws.append(jnp.where(mask, dwk, 0.0)[None])
            dbst = dbst + jnp.where(lane == k, jnp.sum(dsk, axis=1, keepdims=True), 0.0)
        dvn = jnp.concatenate(dvn_parts, axis=1)
        dvh = dvn * g
        dv = rs * (dvh - jnp.mean(dvh, axis=-1, keepdims=True) - vh * jnp.mean(dvh * vh, axis=-1, keepdims=True))
        return [du, dv, dz], [jnp.concatenate(dws, axis=0), dbst, _colsum(dvn * vh), _colsum(dvn)]
    return _rows(fn, "gmlp_bwd", GM_B, [(dya, GM_W, 0), (proj, GM_W, 0), (proj, GM_W, 1), (proj, GM_W, 2)],
                 fulls=[ln_g, ln_b, ws, bst], outs=[(GM_W, BF16)] * 3,
                 accs=[(GM_G, GM_B, GM_B), (GM_B, 128), (1, GM_W), (1, GM_W)])


def _yb_bwd(dyb, o, proj):
    def fn(i, rv, hv, fv):
        dy, ob, z = rv
        do = dy * _silu(z)
        prod = do * ob
        dl = [jnp.broadcast_to(jnp.sum(prod[:, h * VDIM:(h + 1) * VDIM], axis=1, keepdims=True), (dy.shape[0], 128))
              for h in range(H)]
        return [do, jnp.concatenate(dl, axis=1), dy * ob * _dsilu(z)], []
    return _rows(fn, "yb_bwd", 256, [(dyb, MLA_W, 0), (o, MLA_W, 0), (proj, MLA_W, O_ZB // MLA_W)],
                 outs=[(MLA_W, BF16), (H * 128, F32), (MLA_W, BF16)])


def _attn_bwd(qc, kc, vv, do, lse, dl):
    scale = 1.0 / math.sqrt(NOPE + ROPE)
    nt = (((1,), (1,)), ((), ()))
    tn = (((0,), (0,)), ((), ()))

    def body(q_ref, k_ref, v_ref, do_ref, l_ref, d_ref, dq_ref, dk_ref, dv_ref):
        i = pl.program_id(1)

        @pl.when(i == 0)
        def _():
            dk_ref[...] = jnp.zeros_like(dk_ref)
            dv_ref[...] = jnp.zeros_like(dv_ref)

        q = q_ref[...]
        dob = do_ref[...]
        lse_c = l_ref[...][:, 0:1]
        dl_c = d_ref[...][:, 0:1]

        def step(kb, dq):
            t0 = pl.multiple_of(kb * TQ, TQ)
            k = k_ref[pl.ds(t0, TQ), :]
            v = v_ref[pl.ds(t0, TQ), :]
            s = lax.dot_general(q, k, nt, preferred_element_type=F32) * scale
            p = jnp.where(_attn_mask(i, kb), jnp.exp(s - lse_c), 0.0)
            dp = lax.dot_general(dob, v, nt, preferred_element_type=F32)
            ds = (p * (dp - dl_c) * scale).astype(BF16)
            dk_ref[pl.ds(t0, TQ), :] += lax.dot_general(ds, q, tn, preferred_element_type=F32)
            dv_ref[pl.ds(t0, TQ), :] += lax.dot_general(p.astype(BF16), dob, tn, preferred_element_type=F32)
            return dq + jnp.dot(ds, k, preferred_element_type=F32)

        dq_ref[...] = lax.fori_loop(0, i + 1, step, jnp.zeros((TQ, HP), F32))

    blk = lambda w: pl.BlockSpec((TQ, w), lambda h, i: (i, h))
    head = lambda w: pl.BlockSpec((S, w), lambda h, i: (0, h))
    return pl.pallas_call(
        body, name="attn_bwd", grid=(H, S // TQ),
        in_specs=[blk(HP), head(HP), head(VDIM), blk(VDIM), blk(128), blk(128)],
        out_specs=[blk(HP), head(HP), head(VDIM)],
        out_shape=[jax.ShapeDtypeStruct((S, H * HP), F32), jax.ShapeDtypeStruct((S, H * HP), F32),
                   jax.ShapeDtypeStruct((S, MLA_W), F32)],
        compiler_params=pltpu.CompilerParams(dimension_semantics=("parallel", "arbitrary"),
                                             vmem_limit_bytes=VMEM_LIMIT),
    )(qc, kc, vv, do, lse, dl)


def _rope_bwd(dqc, dkc, dvv, tabs):
    def fn(i, rv, hv, fv):
        dq, dk, dv, cc, sa, sb = rv
        qs, ks = [], []
        dkr = jnp.zeros((dq.shape[0], 128), F32)
        for h in range(H):
            qs += [dq[:, h * HP:h * HP + 128], _rot_t(dq[:, h * HP + 128:(h + 1) * HP], cc, sa, sb)]
            ks.append(dk[:, h * HP:h * HP + 128])
            dkr = dkr + dk[:, h * HP + 128:(h + 1) * HP]
        return [jnp.concatenate(qs, axis=1), jnp.concatenate(ks + [dv], axis=1), _rot_t(dkr, cc, sa, sb)], []
    cc, sa, sb = tabs
    return _rows(fn, "rope_bwd", 256,
                 [(dqc, H * HP, 0), (dkc, H * HP, 0), (dvv, MLA_W, 0), (cc, 128, 0), (sa, 128, 0), (sb, 128, 0)],
                 outs=[(H * HP, BF16), (H * 256, BF16), (128, BF16)])


def _mla_prep_bwd(dcqn, dckvn, proj, qg, kvg):
    def fn(i, rv, hv, fv):
        d1, d2, cq, ckv = rv
        g1, g2 = fv
        dx1, dg1 = _rms_bwd(d1, cq, g1)
        dx2, dg2 = _rms_bwd(d2, ckv, g2)
        return [dx1, dx2], [_colsum(dg1), _colsum(dg2)]
    return _rows(fn, "mla_prep_bwd", 256,
                 [(dcqn, QR, 0), (dckvn, KVR, 0), (proj, QR, O_CQ // QR), (proj, KVR, O_CKV // KVR)],
                 fulls=[qg, kvg], outs=[(QR, BF16), (KVR, BF16)], accs=[(1, QR), (1, KVR)])


def _yc_bwd(dyc, hs, proj):
    def fn(i, rv, hv, fv):
        dy, hh, z = rv
        return [dy * _silu(z), dy * hh * _dsilu(z)], []
    return _rows(fn, "yc_bwd", 128, [(dyc, LRU_W, 0), (hs, LRU_W, 0), (proj, LRU_W, O_ZC // LRU_W)],
                 outs=[(LRU_W, F32), (LRU_W, BF16)])


def _scan_bwd(a, hs, dh):
    nblk = S // SCAN_T

    def body(a_ref, h_ref, dh_ref, da_ref, db_ref):
        row = lax.broadcasted_iota(jnp.int32, (SCAN_T, SCAN_CW), 0)

        def step(j, carry):
            gc, ac = carry
            blk = nblk - 1 - j
            t0 = pl.multiple_of(blk * SCAN_T, SCAN_T)
            av = a_ref[pl.ds(t0, SCAN_T), :]
            A = jnp.where(row < SCAN_T - 1, pltpu.roll(av, SCAN_T - 1, 0), ac)
            B = dh_ref[pl.ds(t0, SCAN_T), :]
            d = 1
            while d < SCAN_T:
                keep = row < SCAN_T - d
                A_s = jnp.where(keep, pltpu.roll(A, SCAN_T - d, 0), 1.0)
                B_s = jnp.where(keep, pltpu.roll(B, SCAN_T - d, 0), 0.0)
                B = A * B_s + B
                A = A * A_s
                d *= 2
            g = A * gc + B
            p0 = pl.multiple_of(jnp.maximum(t0 - 8, 0), 8)
            last = jnp.where(blk > 0, h_ref[pl.ds(p0, 8), :][7:8, :], 0.0)
            h_prev = jnp.where(row >= 1, pltpu.roll(h_ref[pl.ds(t0, SCAN_T), :], 1, 0), last)
            da_ref[pl.ds(t0, SCAN_T), :] = g * h_prev
            db_ref[pl.ds(t0, SCAN_T), :] = g
            return g[0:1, :], av[0:1, :]

        z = jnp.zeros((1, SCAN_CW), F32)
        lax.fori_loop(0, nblk, step, (z, z))

    spec = pl.BlockSpec((S, SCAN_CW), lambda j: (0, j))
    return pl.pallas_call(
        body, name="scan_bwd", grid=(LRU_W // SCAN_CW,), in_specs=[spec] * 3, out_specs=[spec] * 2,
        out_shape=[jax.ShapeDtypeStruct((S, LRU_W), F32)] * 2,
        compiler_params=pltpu.CompilerParams(dimension_semantics=("parallel",), vmem_limit_bytes=VMEM_LIMIT),
    )(a, hs, dh)


def _lru_gates_bwd(da, db, gates, xc, ba, bx, lam):
    def fn(i, rv, hv, fv):
        dav, dbv, ga, gx, x = rv
        bav, bxv, lamv = fv
        r, ig, sp, a, e2, om, mult = _lru_terms(ga, gx, x, bav, bxv, lamv)
        dmult = dbv * ig * x
        dig = dbv * mult * x
        dxc1 = dbv * mult * ig
        dlog_a = dav * a + jnp.where(om > 0.0, dmult * (-e2 / mult), 0.0)
        dr = dlog_a * (-LRU_C * sp)
        dga = dr * r * (1.0 - r)
        dgx = dig * ig * (1.0 - ig)
        dlam = _colsum(dlog_a * (-LRU_C * r)) * (-_sig(-lamv))
        return [jnp.concatenate([dga, dgx], axis=1), dxc1], [_colsum(dga), _colsum(dgx), dlam]
    return _rows(fn, "lru_gates_bwd", 128,
                 [(da, LRU_W, 0), (db, LRU_W, 0), (gates, LRU_W, 0), (gates, LRU_W, 1), (xc, LRU_W, 0)],
                 fulls=[ba, bx, lam], outs=[(2 * LRU_W, BF16), (LRU_W, F32)], accs=[(1, LRU_W)] * 3)


def _conv_bwd(dxc1, dxc2, proj, w):
    cb = O_XC // LRU_W

    def fn(i, rv, hv, fv):
        d1, d2, xb = rv
        n1, n2, xprev = hv
        ww = fv[0]
        last = i == S // 128 - 1
        dxc = d1 + d2
        nxt = jnp.where(last, 0.0, n1 + n2)
        xprev = jnp.where(i > 0, xprev, 0.0)
        row = lax.broadcasted_iota(jnp.int32, xb.shape, 0)
        dx = ww[3:4] * dxc
        dws = [None] * CONV_W
        dws[3] = _colsum(dxc * xb)
        for s in range(1, CONV_W):
            dx = dx + ww[3 - s:4 - s] * _shift_up(dxc, nxt, s, row)
            dws[3 - s] = _colsum(dxc * _shift_down(xb, xprev, s, row))
        return [dx], [jnp.concatenate(dws, axis=0), _colsum(dxc)]
    return _rows(fn, "conv_bwd", 128, [(dxc1, LRU_W, 0), (dxc2, LRU_W, 0), (proj, LRU_W, cb)],
                 halos=[(dxc1, LRU_W, 0, "next"), (dxc2, LRU_W, 0, "next"), (proj, LRU_W, cb, "prev")],
                 fulls=[w], outs=[(LRU_W, BF16)], accs=[(CONV_W, LRU_W), (1, LRU_W)])


def _prenorm_bwd(dxn, dh, x, g):
    def fn(i, rv, hv, fv):
        dy, dhh, xb = rv
        dx, dg = _rms_bwd(dhh, xb, fv[0])
        return [dy + dx], [_colsum(dg)]
    return _rows(fn, "prenorm_bwd", 256, [(dxn, D, 0), (dh, D, 0), (x, D, 0)], fulls=[g], outs=[(D, F32)],
                 accs=[(1, D)])


def _layer_bwd(dxn, A, P, l, tabs, GB):
    G = {}
    GB = dict(GB) if GB is not None else {}
    proj = A["proj"]

    def dw(key, a, b, name):
        GB[key] = _mm(a, b, "tn", name, out_dtype=BF16, out_lead=(l, DEPTH, GB.get(key)))

    do2, G["post_g"] = _post_bwd(dxn, A["o2"], P["post_g"])
    dm = _mm(do2, P["wout"], "nt", "out_proj_dx", b_lead=l)
    dw("wout", A["merged"], do2, "out_proj_dw")
    dpa, dpb, dpc, dga, dgb, dgc = _merge_bwd(dm, A["pa"], A["pb"], A["pc"], proj)
    dya = _mm(dpa, P["wpa"], "nt", "proj_a_dx", b_lead=l)
    dw("wpa", A["ya"], dpa, "proj_a_dw")
    dyb = _mm(dpb, P["wpb"], "nt", "proj_b_dx", b_lead=l)
    dw("wpb", A["yb"], dpb, "proj_b_dw")
    dyc = _mm(dpc, P["wpc"], "nt", "proj_c_dx", b_lead=l)
    dw("wpc", A["yc"], dpc, "proj_c_dw")
    du, dv, dza, G["ws"], G["bst"], G["ln_g"], G["ln_b"] = _gmlp_bwd(dya, proj, P["ln_g"], P["ln_b"], P["ws"], P["bst"])
    do, dl, dzb = _yb_bwd(dyb, A["o"], proj)
    dqc, dkc, dvv = _attn_bwd(A["qc"], A["kc"], A["vv"], do, A["lse"], dl)
    dq, dkv, dkr = _rope_bwd(dqc, dkc, dvv, tabs)
    dcqn = _mm(dq, P["wuq"], "nn", "q_up_dx", b_lead=l)
    dw("wuq", dq, A["cqn"], "q_up_dw")
    dckvn = _mm(dkv, P["wukv"], "nn", "kv_up_dx", b_lead=l)
    dw("wukv", dkv, A["ckvn"], "kv_up_dw")
    dcq, dckv, G["qg"], G["kvg"] = _mla_prep_bwd(dcqn, dckvn, proj, P["qg"], P["kvg"])
    dhs, dzc = _yc_bwd(dyc, A["hs"], proj)
    da, db = _scan_bwd(A["a"], A["hs"], dhs)
    dgates, dxc1, G["ba"], G["bx"], G["lam"] = _lru_gates_bwd(da, db, A["gates"], A["xc"], P["ba"], P["bx"], P["lam"])
    dxc2 = _bd_dx(dgates, P["wsb"])
    G["wab"] = _bd_extract(_bd_dw(A["xcb"], dgates))
    dxcc, G["conv_w"], G["conv_b"] = _conv_bwd(dxc1, dxc2, proj, P["conv_w"])
    zpad = jnp.zeros((S, 128), BF16)
    dproj = jnp.concatenate([du, dv, dza, dzb, dga, dgb, dgc, dckv, dkr, zpad, dxcc, dzc, zpad, dcq], axis=1)
    dh = _mm(dproj, P["wp"], "nn", "in_proj_dx", b_lead=l)
    dw("wp", dproj, A["h"], "in_proj_dw")
    dx, G["pre_g"] = _prenorm_bwd(dxn, dh, A["x"], P["pre_g"])
    return dx, G, GB


_ORIG_OFF = [0]
for _s in IN_SIZES:
    _ORIG_OFF.append(_ORIG_OFF[-1] + _s)
_PAD_OFF = {0: O_U, 1: O_V, 2: O_ZA, 3: O_CQ, 4: O_CKV, 5: O_KR, 6: O_ZB, 7: O_XC, 8: O_ZC, 9: O_GA, 10: O_GB, 11: O_GC}
SHARD_IN = N_IN // N_CHIPS


def _pieces_w_in(j):
    lo, hi = SHARD_IN * j, SHARD_IN * (j + 1)
    out = []
    for k in range(len(IN_SIZES)):
        a, b = max(lo, _ORIG_OFF[k]), min(hi, _ORIG_OFF[k + 1])
        if a < b:
            out.append((a - lo, _PAD_OFF[k] + a - _ORIG_OFF[k], b - a))
    return out


def _pieces_uq(j):
    return [(192 * hh, HP * (2 * j + hh), NOPE + ROPE) for hh in range(2)]


def _pieces_ukv(j):
    out = []
    for hh in range(2):
        h = 2 * j + hh
        out += [(256 * hh, NOPE * h, NOPE), (256 * hh + NOPE, H * NOPE + VDIM * h, VDIM)]
    return out


def _pieces_rows(r):
    return lambda j: [(0, r * j, r)]


LAYOUT = {
    "w_in": (SHARD_IN, NP, _pieces_w_in),
    "mla_w_uq": (2 * (NOPE + ROPE), H * HP, _pieces_uq),
    "mla_w_ukv": (2 * (NOPE + VDIM), 2 * H * 128, _pieces_ukv),
    "lru_conv_w": (1, N_CHIPS, _pieces_rows(1)),
    "w_proj_a": (GM_W // N_CHIPS, GM_W, _pieces_rows(GM_W // N_CHIPS)),
    "w_proj_b": (MLA_W // N_CHIPS, MLA_W, _pieces_rows(MLA_W // N_CHIPS)),
    "w_proj_c": (LRU_W // N_CHIPS, LRU_W, _pieces_rows(LRU_W // N_CHIPS)),
    "w_out": (D // N_CHIPS, D, _pieces_rows(D // N_CHIPS)),
}
TRANSPOSED = ("w_in", "mla_w_uq", "mla_w_ukv")


def _superblocks(w):
    w4 = w.reshape(2, 8, LRU_BW, LRU_BW)
    eye = jnp.eye(8, dtype=w.dtype)
    return (w4[:, :, :, None, :] * eye[None, :, None, :, None]).reshape(2, SB, SB)


_HBM = pl.BlockSpec(memory_space=pltpu.HBM)


def _position():
    return lax.axis_index("x"), lax.axis_index("y"), lax.axis_index("c")


def _allgather(blocks, name):
    n = len(blocks)

    def body(*refs):
        ins, outs = refs[:n], refs[n:2 * n]
        send, recv, lsem = refs[2 * n:]
        x, y, c = _position()
        me, sib = (x, y, c), (x, y, 1 - c)
        chips = [(1 - x, y), (x, 1 - y), (1 - x, 1 - y)]

        def cp(k, a, block, to, src=None):
            dst = outs[a].at[4 * block[0] + 2 * block[1] + block[2]]
            return pltpu.make_async_remote_copy(src_ref=dst if src is None else src, dst_ref=dst,
                                                send_sem=send.at[7 * a + k], recv_sem=recv.at[7 * a + k],
                                                device_id=to, device_id_type=MESH)

        mine = [pltpu.make_async_copy(ins[a], outs[a].at[4 * x + 2 * y + c], lsem.at[a]) for a in range(n)]
        for m in mine:
            m.start()
        first = []
        for a in range(n):
            first.append(cp(0, a, me, sib, src=ins[a]))
            first += [cp(1 + j, a, me, (*chip, c), src=ins[a]) for j, chip in enumerate(chips)]
        for f in first:
            f.start()
        passed = []
        for j, chip in enumerate(chips):
            for a in range(n):
                cp(1 + j, a, (*chip, c), me).wait_recv()
                p = cp(4 + j, a, (*chip, c), sib)
                p.start()
                passed.append(p)
        for a in range(n):
            cp(0, a, sib, me).wait_recv()
            for j, chip in enumerate(chips):
                cp(4 + j, a, (*chip, 1 - c), me).wait_recv()
        for f in first + passed:
            f.wait_send()
        for m in mine:
            m.wait()

    return pl.pallas_call(
        body, name=name,
        out_shape=[jax.ShapeDtypeStruct((8,) + b.shape, b.dtype) for b in blocks],
        in_specs=[_HBM] * n, out_specs=[_HBM] * n,
        scratch_shapes=[pltpu.SemaphoreType.DMA((7 * n,)), pltpu.SemaphoreType.DMA((7 * n,)),
                        pltpu.SemaphoreType.DMA((n,))],
    )(*blocks)


_REL = (2, 1, 3)


def _weights_allgather(names, srcs, name):
    n = len(srcs)
    lay = [LAYOUT[nm] for nm in names]
    zeros = [jnp.zeros((DEPTH, lay[a][1]) + srcs[a].shape[1:], srcs[a].dtype) for a in range(n)]

    def body(*refs):
        ins, outs = refs[:n], refs[2 * n:3 * n]
        send, recv, lsem = refs[3 * n:]
        x, y, c = _position()
        j = 2 * x + y
        sib = (x, y, 1 - c)
        chips = [(1 - x, y), (x, 1 - y), (1 - x, 1 - y)]

        def flow(a, k, jsrc, to, from_src):
            cps = []
            for s0, d0, nr in lay[a][2](jsrc):
                dst = outs[a].at[c, pl.ds(d0, nr)]
                src = ins[a].at[pl.ds(s0, nr)] if from_src else dst
                cps.append(pltpu.make_async_remote_copy(src_ref=src, dst_ref=dst, send_sem=send.at[7 * a + k],
                                                        recv_sem=recv.at[7 * a + k], device_id=to, device_id_type=MESH))
            return cps

        def whole(a, k):
            return pltpu.make_async_remote_copy(src_ref=ins[a], dst_ref=outs[a].at[0, pl.ds(0, lay[a][0])],
                                                send_sem=send.at[7 * a + k], recv_sem=recv.at[7 * a + k],
                                                device_id=sib, device_id_type=MESH)

        for j0 in range(N_CHIPS):
            @pl.when(j == j0)
            def _(j0=j0):
                for a in range(n):
                    for s0, d0, nr in lay[a][2](j0):
                        pltpu.make_async_copy(ins[a].at[pl.ds(s0, nr)], outs[a].at[c, pl.ds(d0, nr)], lsem.at[a]).start()
                for a in range(n):
                    for cp in flow(a, 0, j0, sib, True):
                        cp.start()
                    for k, chip in enumerate(chips):
                        for cp in flow(a, 1 + k, j0, (*chip, c), True):
                            cp.start()
                for k in range(3):
                    for a in range(n):
                        whole(a, 1 + k).wait_recv()
                        for cp in flow(a, 4 + k, j0 ^ _REL[k], sib, False):
                            cp.start()

        for a in range(n):
            whole(a, 0).wait_recv()
            for k in range(3):
                whole(a, 4 + k).wait_recv()
        for a in range(n):
            for k in range(7):
                whole(a, k).wait_send()
            pltpu.make_async_copy(ins[a], outs[a].at[0, pl.ds(0, lay[a][0])], lsem.at[a]).wait()

    return pl.pallas_call(
        body, name=name,
        out_shape=[jax.ShapeDtypeStruct(z.shape, z.dtype) for z in zeros],
        in_specs=[_HBM] * (2 * n), out_specs=[_HBM] * n,
        input_output_aliases={n + a: a for a in range(n)},
        scratch_shapes=[pltpu.SemaphoreType.DMA((7 * n,)), pltpu.SemaphoreType.DMA((7 * n,)),
                        pltpu.SemaphoreType.DMA((n,))],
    )(*srcs, *zeros)


def _grads_to_sibling(gb, name):
    n = len(gb)

    def body(*refs):
        ins, outs = refs[:n], refs[n:2 * n]
        send, recv = refs[2 * n:]
        x, y, c = _position()
        cps = [pltpu.make_async_remote_copy(src_ref=ins[a].at[1 - c], dst_ref=outs[a], send_sem=send.at[a],
                                            recv_sem=recv.at[a], device_id=(x, y, 1 - c), device_id_type=MESH)
               for a in range(n)]
        for cp in cps:
            cp.start()
        for cp in cps:
            cp.wait()

    return pl.pallas_call(
        body, name=name,
        out_shape=[jax.ShapeDtypeStruct(g.shape[1:], g.dtype) for g in gb],
        in_specs=[_HBM] * n, out_specs=[_HBM] * n,
        scratch_shapes=[pltpu.SemaphoreType.DMA((n,)), pltpu.SemaphoreType.DMA((n,))],
    )(*gb)


def _chip_scatter(names, parts, name):
    n = len(parts)
    lay = [LAYOUT[nm] for nm in names]

    def body(*refs):
        ins, outs = refs[:n], refs[n:2 * n]
        send, recv, lsem = refs[2 * n:]
        x, y, c = _position()
        j = 2 * x + y
        chips = [(1 - x, y), (x, 1 - y), (1 - x, 1 - y)]

        def whole(a):
            return outs[a].at[0, pl.ds(0, lay[a][0])]

        for j0 in range(N_CHIPS):
            @pl.when(j == j0)
            def _(j0=j0):
                for a in range(n):
                    for s0, d0, nr in lay[a][2](j0):
                        pltpu.make_async_copy(ins[a].at[pl.ds(d0, nr)], outs[a].at[j0, pl.ds(s0, nr)], lsem.at[a]).start()
                    for k, chip in enumerate(chips):
                        for s0, d0, nr in lay[a][2](j0 ^ _REL[k]):
                            pltpu.make_async_remote_copy(
                                src_ref=ins[a].at[pl.ds(d0, nr)], dst_ref=outs[a].at[j0, pl.ds(s0, nr)],
                                send_sem=send.at[3 * a + k], recv_sem=recv.at[3 * a + k],
                                device_id=(*chip, c), device_id_type=MESH).start()

        for a in range(n):
            for k in range(3):
                pltpu.make_async_remote_copy(src_ref=whole(a), dst_ref=whole(a), send_sem=send.at[3 * a + k],
                                             recv_sem=recv.at[3 * a + k], device_id=(x, y, c), device_id_type=MESH).wait()
            pltpu.make_async_copy(whole(a), whole(a), lsem.at[a]).wait()

    return pl.pallas_call(
        body, name=name,
        out_shape=[jax.ShapeDtypeStruct((N_CHIPS, lay[a][0]) + parts[a].shape[1:], parts[a].dtype) for a in range(n)],
        in_specs=[_HBM] * n, out_specs=[_HBM] * n,
        scratch_shapes=[pltpu.SemaphoreType.DMA((3 * n,)), pltpu.SemaphoreType.DMA((3 * n,)),
                        pltpu.SemaphoreType.DMA((n,))],
    )(*parts)


def _reduced_exchange(mine, name):
    n = len(mine)

    def body(*refs):
        ins, outs = refs[:n], refs[n:2 * n]
        send, recv, lsem = refs[2 * n:]
        x, y, c = _position()
        loc = [pltpu.make_async_copy(ins[a], outs[a].at[c], lsem.at[a]) for a in range(n)]
        cps = [pltpu.make_async_remote_copy(src_ref=ins[a], dst_ref=outs[a].at[c], send_sem=send.at[a],
                                            recv_sem=recv.at[a], device_id=(x, y, 1 - c), device_id_type=MESH)
               for a in range(n)]
        for cp in loc + cps:
            cp.start()
        for cp in cps + loc:
            cp.wait()

    return pl.pallas_call(
        body, name=name,
        out_shape=[jax.ShapeDtypeStruct((DEPTH,) + m.shape, m.dtype) for m in mine],
        in_specs=[_HBM] * n, out_specs=[_HBM] * n,
        scratch_shapes=[pltpu.SemaphoreType.DMA((n,)), pltpu.SemaphoreType.DMA((n,)), pltpu.SemaphoreType.DMA((n,))],
    )(*mine)


def _row_tile(r):
    for t in (256, 128, 64, 32, 16, 8):
        if r % t == 0 and r > t:
            return t
    return r


def _pair_add(g, rb, c_arr, name):
    R, rest = g.shape[1], g.shape[2:]
    tr = _row_tile(R)
    z = (0,) * len(rest)

    def body(c_ref, g_ref, r_ref, o_ref):
        o_ref[...] = (g_ref[...].astype(F32) + r_ref[...].astype(F32)).astype(o_ref.dtype)

    return pl.pallas_call(
        body, name=name,
        grid_spec=pltpu.PrefetchScalarGridSpec(
            num_scalar_prefetch=1, grid=(R // tr,),
            in_specs=[pl.BlockSpec((None, tr) + rest, lambda i, c_ref: (c_ref[0], i) + z),
                      pl.BlockSpec((tr,) + rest, lambda i, c_ref: (i,) + z)],
            out_specs=pl.BlockSpec((tr,) + rest, lambda i, c_ref: (i,) + z)),
        out_shape=jax.ShapeDtypeStruct((R,) + rest, BF16),
        compiler_params=pltpu.CompilerParams(dimension_semantics=("parallel",), vmem_limit_bytes=VMEM_LIMIT),
    )(c_arr, g, rb)


def _sum_slabs(rb, name):
    n, R, rest = rb.shape[0], rb.shape[1], rb.shape[2:]
    tr = _row_tile(R)
    z = (0,) * len(rest)

    def body(r_ref, o_ref):
        acc = r_ref[0].astype(F32)
        for k in range(1, n):
            acc = acc + r_ref[k].astype(F32)
        o_ref[...] = acc

    if R // tr > 64 and len(rest) == 1 and rest[0] % 256 == 0:
        return pl.pallas_call(
            body, name=name, grid=(rest[0] // 256,),
            in_specs=[pl.BlockSpec((n, R, 256), lambda i: (0, 0, i))],
            out_specs=pl.BlockSpec((R, 256), lambda i: (0, i)),
            out_shape=jax.ShapeDtypeStruct((R,) + rest, F32),
            compiler_params=pltpu.CompilerParams(dimension_semantics=("parallel",), vmem_limit_bytes=VMEM_LIMIT),
        )(rb)
    return pl.pallas_call(
        body, name=name, grid=(R // tr,),
        in_specs=[pl.BlockSpec((n, tr) + rest, lambda i: (0, i) + z)],
        out_specs=pl.BlockSpec((tr,) + rest, lambda i: (i,) + z),
        out_shape=jax.ShapeDtypeStruct((R,) + rest, F32),
        compiler_params=pltpu.CompilerParams(dimension_semantics=("parallel",), vmem_limit_bytes=VMEM_LIMIT),
    )(rb)


def _adam_math(w, g, m, v):
    mn = ADAM_B1 * m + (1.0 - ADAM_B1) * g
    vn = ADAM_B2 * v + (1.0 - ADAM_B2) * (g * g)
    m_hat = mn / (1.0 - ADAM_B1 ** ADAM_STEP)
    v_hat = vn / (1.0 - ADAM_B2 ** ADAM_STEP)
    return -ADAM_LR * (m_hat / (jnp.sqrt(v_hat) + ADAM_EPS) + ADAM_WD * w), mn, vn


def _adamw(w, g, m, v, name):
    L, R, C = w.shape
    tr = _row_tile(R)

    def body(w_ref, g_ref, m_ref, v_ref, d_ref, mo_ref, vo_ref):
        d_ref[...], mo_ref[...], vo_ref[...] = _adam_math(w_ref[...], g_ref[...], m_ref[...], v_ref[...])

    spec = pl.BlockSpec((None, tr, C), lambda l, i: (l, i, 0))
    return pl.pallas_call(
        body, name=name, grid=(L, R // tr), in_specs=[spec] * 4, out_specs=[spec] * 3,
        out_shape=[jax.ShapeDtypeStruct((L, R, C), F32)] * 3,
        compiler_params=pltpu.CompilerParams(dimension_semantics=("parallel", "parallel"), vmem_limit_bytes=VMEM_LIMIT),
    )(w, g, m, v)


_VMEM_WHOLE = pl.BlockSpec(memory_space=pltpu.VMEM)


def _matrix_update(gath, w, m, v, name):
    K = w.shape[1]

    def body(g0_ref, g1_ref, w_ref, m_ref, v_ref, go_ref, d_ref, mo_ref, vo_ref):
        for l, gr in enumerate((g0_ref, g1_ref)):
            for k in range(K):
                g = gr[0, k].astype(F32)
                for dev in range(1, 8):
                    g = g + gr[dev, k].astype(F32)
                go_ref[l, k] = g
                d_ref[l, k], mo_ref[l, k], vo_ref[l, k] = _adam_math(w_ref[l, k], g, m_ref[l, k], v_ref[l, k])

    return pl.pallas_call(
        body, name=name, in_specs=[_VMEM_WHOLE] * 5, out_specs=[_VMEM_WHOLE] * 4,
        out_shape=[jax.ShapeDtypeStruct(w.shape, F32)] * 4,
        compiler_params=pltpu.CompilerParams(vmem_limit_bytes=VMEM_LIMIT),
    )(gath[0], gath[1], w, m, v)


VECS = (("pre_norm_g", D), ("post_norm_g", D), ("gm_ln_g", GM_W), ("gm_ln_b", GM_W), ("mla_q_norm_g", QR),
        ("mla_kv_norm_g", KVR), ("lru_conv_b", LRU_W), ("lru_b_a", LRU_W), ("lru_b_x", LRU_W), ("lru_lambda", LRU_W))
VEC_ROWS, VEC_W = 16, LRU_W


def _vector_update(gath, W, M, V):
    names = [n for n, _ in VECS] + ["gm_bs"]
    nw = len(names)

    def body(*refs):
        g_ref = refs[0]
        wr, mr, vr = refs[1:1 + nw], refs[1 + nw:1 + 2 * nw], refs[1 + 2 * nw:1 + 3 * nw]
        outs = refs[1 + 3 * nw:]
        s = g_ref[0]
        for dev in range(1, 8):
            s = s + g_ref[dev]
        for t, (_, width) in enumerate(VECS):
            for l in range(DEPTH):
                g = s[VEC_ROWS * l + t:VEC_ROWS * l + t + 1, :width]
                row = (pl.ds(l, 1), slice(None))
                res = (g,) + _adam_math(wr[t][row], g, mr[t][row], vr[t][row])
                for q in range(4):
                    outs[4 * t + q][row] = res[q]
        t = len(VECS)
        for l in range(DEPTH):
            for k in range(GM_G):
                g = s[VEC_ROWS * l + t + k:VEC_ROWS * l + t + k + 1, :GM_B]
                row = (l, pl.ds(k, 1), slice(None))
                res = (g,) + _adam_math(wr[t][row], g, mr[t][row], vr[t][row])
                for q in range(4):
                    outs[4 * t + q][row] = res[q]

    ws = [W[n] for n in names]
    out_shape = []
    for w in ws:
        out_shape += [jax.ShapeDtypeStruct(w.shape, F32)] * 4
    res = pl.pallas_call(
        body, name="vector_update", in_specs=[_VMEM_WHOLE] * (1 + 3 * nw), out_specs=[_VMEM_WHOLE] * (4 * nw),
        out_shape=out_shape, compiler_params=pltpu.CompilerParams(vmem_limit_bytes=VMEM_LIMIT),
    )(gath, *ws, *[M[n] for n in names], *[V[n] for n in names])
    return {n: tuple(res[4 * t:4 * t + 4]) for t, n in enumerate(names)}


def _vector_rows(G):
    key = {"pre_norm_g": "pre_g", "post_norm_g": "post_g", "gm_ln_g": "ln_g", "gm_ln_b": "ln_b", "mla_q_norm_g": "qg",
           "mla_kv_norm_g": "kvg", "lru_conv_b": "conv_b", "lru_b_a": "ba", "lru_b_x": "bx", "lru_lambda": "lam"}
    rows = [jnp.pad(G[key[n]], ((0, 0), (0, VEC_W - w))) for n, w in VECS]
    rows.append(jnp.pad(G["bst"][:, :GM_G].T, ((0, 0), (0, VEC_W - GM_B))))
    rows.append(jnp.zeros((VEC_ROWS - len(VECS) - GM_G, VEC_W), F32))
    return jnp.concatenate(rows, axis=0)


SHARDED = ("w_in", "mla_w_uq", "mla_w_ukv", "lru_conv_w", "w_proj_a", "w_proj_b", "w_proj_c", "w_out")
COL_SHARDED = ("w_in", "mla_w_uq", "mla_w_ukv", "lru_conv_w")
SMALL = ("pre_norm_g", "gm_ln_g", "gm_ln_b", "gm_ws", "gm_bs", "mla_q_norm_g", "mla_kv_norm_g", "lru_conv_b",
         "lru_w_a", "lru_b_a", "lru_w_x", "lru_b_x", "lru_lambda", "post_norm_g")
WEIGHTS = ("pre_norm_g", "w_in", "gm_ln_g", "gm_ln_b", "gm_ws", "gm_bs", "mla_q_norm_g", "mla_w_uq",
           "mla_kv_norm_g", "mla_w_ukv", "lru_conv_w", "lru_conv_b", "lru_w_a", "lru_b_a", "lru_w_x", "lru_b_x",
           "lru_lambda", "w_proj_a", "w_proj_b", "w_proj_c", "w_out", "post_norm_g")


GB_KEY = {"w_in": "wp", "mla_w_uq": "wuq", "mla_w_ukv": "wukv", "w_proj_a": "wpa", "w_proj_b": "wpb",
          "w_proj_c": "wpc", "w_out": "wout"}


def _prepare(l, gathered, small):
    P = {GB_KEY[n]: gathered[n] for n in GB_KEY}
    P["conv_w"] = gathered["lru_conv_w"][l].transpose(1, 0, 2).reshape(CONV_W, LRU_W)
    P["wsb"] = jnp.concatenate([_superblocks(small["lru_w_a"][l]), _superblocks(small["lru_w_x"][l])], axis=0).astype(BF16)
    row = lambda n: small[n][l][None, :]
    P["pre_g"], P["post_g"] = row("pre_norm_g"), row("post_norm_g")
    P["ln_g"], P["ln_b"] = row("gm_ln_g"), row("gm_ln_b")
    P["ws"] = small["gm_ws"][l]
    P["bst"] = jnp.pad(small["gm_bs"][l].T, ((0, 0), (0, 128 - GM_G)))
    P["qg"], P["kvg"] = row("mla_q_norm_g"), row("mla_kv_norm_g")
    P["conv_b"], P["ba"], P["bx"], P["lam"] = row("lru_conv_b"), row("lru_b_a"), row("lru_b_x"), row("lru_lambda")
    return P


def kernel(x, pre_norm_g, w_in, gm_ln_g, gm_ln_b, gm_ws, gm_bs, mla_q_norm_g, mla_w_uq, mla_kv_norm_g, mla_w_ukv, lru_conv_w, lru_conv_b, lru_w_a, lru_b_a, lru_w_x, lru_b_x, lru_lambda, w_proj_a, w_proj_b, w_proj_c, w_out, post_norm_g, loss_target, m_pre_norm_g, m_w_in, m_gm_ln_g, m_gm_ln_b, m_gm_ws, m_gm_bs, m_mla_q_norm_g, m_mla_w_uq, m_mla_kv_norm_g, m_mla_w_ukv, m_lru_conv_w, m_lru_conv_b, m_lru_w_a, m_lru_b_a, m_lru_w_x, m_lru_b_x, m_lru_lambda, m_w_proj_a, m_w_proj_b, m_w_proj_c, m_w_out, m_post_norm_g, v_pre_norm_g, v_w_in, v_gm_ln_g, v_gm_ln_b, v_gm_ws, v_gm_bs, v_mla_q_norm_g, v_mla_w_uq, v_mla_kv_norm_g, v_mla_w_ukv, v_lru_conv_w, v_lru_conv_b, v_lru_w_a, v_lru_b_a, v_lru_w_x, v_lru_b_x, v_lru_lambda, v_w_proj_a, v_w_proj_b, v_w_proj_c, v_w_out, v_post_norm_g):
    args = dict(locals())
    W = {n: args[n] for n in WEIGHTS}
    M = {n: args["m_" + n] for n in WEIGHTS}
    V = {n: args["v_" + n] for n in WEIGHTS}
    c = lax.axis_index("c")

    srcs = []
    for n in SHARDED:
        blk = lax.dynamic_index_in_dim(W[n], c, 0, keepdims=False)
        if n in TRANSPOSED:
            blk = blk.T
        srcs.append(blk[None] if n == "lru_conv_w" else blk.astype(BF16))
    gathered = dict(zip(SHARDED, _weights_allgather(SHARDED, srcs, "weights_allgather")))
    small = {n: W[n] for n in SMALL}
    P = [_prepare(l, gathered, small) for l in range(DEPTH)]
    tabs = _rope_tables()

    h0 = x[0]
    h1, A0 = _layer_fwd(h0, P[0], 0, tabs)
    h2, A1 = _layer_fwd(h1, P[1], 1, tabs)
    dy, loss_part = _loss_fwd(h2, loss_target[0])
    d1, G1, GB = _layer_bwd(dy, A1, P[1], 1, tabs, None)
    d0, G0, GB = _layer_bwd(d1, A0, P[0], 0, tabs, GB)
    LG = (G0, G1)
    loss = lax.psum(loss_part[0, 0], ("x", "y", "c"))

    conv_g = jnp.stack([g["conv_w"].reshape(CONV_W, N_CHIPS, LRU_W // N_CHIPS).transpose(1, 0, 2) for g in LG])
    gb = [conv_g if n == "lru_conv_w" else GB[GB_KEY[n]] for n in SHARDED]
    from_sib = _grads_to_sibling(gb, "grads_to_sibling")
    c_arr = jnp.reshape(c, (1,)).astype(jnp.int32)
    pair = [_pair_add(g, rb, c_arr, "pair_add_" + n) for n, g, rb in zip(SHARDED, gb, from_sib)]
    slabs = _chip_scatter(SHARDED, pair, "grads_chip_scatter")
    mine = [_sum_slabs(s, "sum_slabs_" + n) for n, s in zip(SHARDED, slabs)]
    both = _reduced_exchange(mine, "reduced_to_sibling")
    grads = {}
    for n, b in zip(SHARDED, both):
        if n in TRANSPOSED:
            b = jnp.swapaxes(b, 1, 2)
        grads[n] = b.reshape(W[n].shape)

    rows = jnp.concatenate([_vector_rows(g) for g in LG], axis=0)
    mats = []
    for g in LG:
        mats += [g["ws"].astype(BF16), g["wab"][0, :, :, :LRU_BW], g["wab"][1, :, :, :LRU_BW]]
    gath = _allgather([rows] + mats, "small_grads_allgather")
    upd = _vector_update(gath[0], W, M, V)
    for k, n in enumerate(("gm_ws", "lru_w_a", "lru_w_x")):
        upd[n] = _matrix_update((gath[1 + k], gath[4 + k]), W[n], M[n], V[n], "update_" + n)

    for n in SHARDED:
        upd[n] = (grads[n],) + tuple(_adamw(W[n], grads[n], M[n], V[n], "adamw_" + n))

    return (loss, d0[None], *[upd[n][0] for n in WEIGHTS], *[upd[n][1] for n in WEIGHTS],
            *[upd[n][2] for n in WEIGHTS], *[upd[n][3] for n in WEIGHTS])
```

```python
import functools
import math

import jax
import jax.numpy as jnp
from jax import lax
from jax.experimental import pallas as pl
from jax.experimental.pallas import tpu as pltpu

F32, BF16 = jnp.float32, jnp.bfloat16
MESH = pl.DeviceIdType.MESH

S, D, DEPTH = 2048, 1024, 2
CHUNK, EPS = 64, 1e-6
GM_W, GM_G, GM_B = 1024, 4, 128
H, NOPE, ROPE, VDIM = 8, 128, 64, 128
QR, KVR = 384, 256
MLA_W = H * VDIM
LRU_W, LRU_NB, LRU_BW, LRU_C, CONV_W = 1280, 16, 80, 8.0, 4
ROPE_THETA = 10000.0
IN_SIZES = (GM_W, GM_W, GM_W, QR, KVR, ROPE, MLA_W, LRU_W, LRU_W, D, D, D)
N_IN = sum(IN_SIZES)
N_CHIPS = 4
ADAM_LR, ADAM_B1, ADAM_B2, ADAM_EPS, ADAM_WD, ADAM_STEP = 0.001, 0.9, 0.999, 1e-08, 0.01, 10

HP = 256
O_U, O_V, O_ZA, O_ZB, O_GA, O_GB, O_GC = 0, 1024, 2048, 3072, 4096, 5120, 6144
O_CKV, O_KR, O_XC, O_ZC, O_CQ = 7168, 7424, 7680, 8960, 10368
NP = 10752
VMEM_LIMIT = 48 * 1024 * 1024


def _tile(dim, target):
    if dim <= target:
        return dim
    t = (target // 128) * 128
    while dim % t:
        t -= 128
    return t


def _sig(x):
    return jax.nn.sigmoid(x)


def _silu(x):
    return x * _sig(x)


def _dsilu(x):
    s = _sig(x)
    return s * (1.0 + x * (1.0 - s))


def _mm(a, b, mode, name, out_dtype=F32, tm=512, tn=512, tk=1024, b_lead=None, out_lead=None):
    b2 = b.shape[1:] if b_lead is not None else b.shape
    if mode == "nn":
        (M, K), (K2, N) = a.shape, b2
    elif mode == "nt":
        (M, K), (N, K2) = a.shape, b2
    else:
        (K, M), (K2, N) = a.shape, b2
    assert K == K2, (name, a.shape, b.shape)
    tm, tn, tk = _tile(M, tm), _tile(N, tn), _tile(K, tk)
    nk = K // tk
    if mode == "tn":
        a_spec = pl.BlockSpec((tk, tm), lambda i, j, k: (k, i))
        lhs_c = 0
    else:
        a_spec = pl.BlockSpec((tm, tk), lambda i, j, k: (i, k))
        lhs_c = 1
    b_blk, b_idx, rhs_c = ((tn, tk), (lambda i, j, k: (j, k)), 1) if mode == "nt" else ((tk, tn), (lambda i, j, k: (k, j)), 0)
    if b_lead is None:
        b_spec = pl.BlockSpec(b_blk, b_idx)
    else:
        b_spec = pl.BlockSpec((None,) + b_blk, functools.partial(lambda i, j, k, f, l: (l,) + f(i, j, k), f=b_idx, l=b_lead))
    dims = (((lhs_c,), (rhs_c,)), ((), ()))
    in_specs, args, aliases = [a_spec, b_spec], [a, b], {}
    if out_lead is None:
        out_spec = pl.BlockSpec((tm, tn), lambda i, j, k: (i, j))
        out_shape = jax.ShapeDtypeStruct((M, N), out_dtype)
    else:
        l_out, n_lead, buf = out_lead
        out_spec = pl.BlockSpec((None, tm, tn), functools.partial(lambda i, j, k, l: (l, i, j), l=l_out))
        out_shape = jax.ShapeDtypeStruct((n_lead, M, N), out_dtype)
        if buf is not None:
            in_specs.append(pl.BlockSpec(memory_space=pl.ANY))
            args.append(buf)
            aliases = {2: 0}

    def body(a_ref, b_ref, *rest):
        o_ref, acc_ref = rest[-2:]
        k = pl.program_id(2)

        @pl.when(k == 0)
        def _():
            acc_ref[...] = jnp.zeros_like(acc_ref)

        acc_ref[...] += lax.dot_general(a_ref[...].astype(BF16), b_ref[...].astype(BF16), dims,
                                        preferred_element_type=F32)

        @pl.when(k == nk - 1)
        def _():
            o_ref[...] = acc_ref[...].astype(o_ref.dtype)

    return pl.pallas_call(
        body, name=name, grid=(M // tm, N // tn, nk),
        in_specs=in_specs, out_specs=out_spec, out_shape=out_shape,
        scratch_shapes=[pltpu.VMEM((tm, tn), F32)], input_output_aliases=aliases,
        compiler_params=pltpu.CompilerParams(dimension_semantics=("parallel", "parallel", "arbitrary"),
                                             vmem_limit_bytes=VMEM_LIMIT),
    )(*args)


def _rows(fn, name, tm, rows, halos=(), fulls=(), outs=(), accs=()):
    n = S // tm
    in_specs, args = [], []
    for arr, w, cb in rows:
        in_specs.append(pl.BlockSpec((tm, w), functools.partial(lambda i, cb: (i, cb), cb=cb)))
        args.append(arr)
    for arr, w, cb, side in halos:
        if side == "prev":
            im = functools.partial(lambda i, cb: (jnp.maximum(i * (tm // 8) - 1, 0), cb), cb=cb)
        else:
            im = functools.partial(lambda i, cb: (jnp.minimum((i + 1) * (tm // 8), S // 8 - 1), cb), cb=cb)
        in_specs.append(pl.BlockSpec((8, w), im))
        args.append(arr)
    for arr in fulls:
        in_specs.append(pl.BlockSpec(arr.shape, functools.partial(lambda i, nd: (0,) * nd, nd=arr.ndim)))
        args.append(arr)
    out_shape = [jax.ShapeDtypeStruct((S, w), dt) for w, dt in outs]
    out_specs = [pl.BlockSpec((tm, w), lambda i: (i, 0)) for w, dt in outs]
    for shp in accs:
        out_shape.append(jax.ShapeDtypeStruct(shp, F32))
        out_specs.append(pl.BlockSpec(shp, functools.partial(lambda i, nd: (0,) * nd, nd=len(shp))))
    nr, nh, nf, no, na = len(rows), len(halos), len(fulls), len(outs), len(accs)

    def body(*refs):
        i = pl.program_id(0)
        ins, orefs = refs[:nr + nh + nf], refs[nr + nh + nf:]
        rv = [r[...] for r in ins[:nr]]
        hv = [r[...] for r in ins[nr:nr + nh]]
        fv = [r[...] for r in ins[nr + nh:]]
        o, a = fn(i, rv, hv, fv)
        assert len(o) == no and len(a) == na, name
        for ref, val in zip(orefs[:no], o):
            ref[...] = val.astype(ref.dtype)
        if na:
            @pl.when(i == 0)
            def _():
                for ref in orefs[no:]:
                    ref[...] = jnp.zeros_like(ref)

            for ref, val in zip(orefs[no:], a):
                ref[...] += val

    res = pl.pallas_call(
        body, name=name, grid=(n,), in_specs=in_specs, out_specs=out_specs, out_shape=out_shape,
        compiler_params=pltpu.CompilerParams(dimension_semantics=("arbitrary",), vmem_limit_bytes=VMEM_LIMIT),
    )(*args)
    return res


def _shift_down(xb, halo, s, row):
    fix = jnp.tile(pltpu.roll(halo, s, 0), (xb.shape[0] // 8, 1))
    return jnp.where(row >= s, pltpu.roll(xb, s, 0), fix)


def _shift_up(xb, halo, s, row):
    tm = xb.shape[0]
    fix = jnp.tile(pltpu.roll(halo, 8 - s, 0), (tm // 8, 1))
    return jnp.where(row < tm - s, pltpu.roll(xb, tm - s, 0), fix)


def _rms(x):
    return lax.rsqrt(jnp.mean(x * x, axis=-1, keepdims=True) + EPS)


def _rms_bwd(dy, x, g):
    r = _rms(x)
    xh = x * r
    dxh = dy * g
    dx = r * (dxh - xh * jnp.mean(dxh * xh, axis=-1, keepdims=True))
    return dx, dy * xh


def _colsum(x):
    return jnp.sum(x, axis=0, keepdims=True)


def _prenorm_fwd(x, g):
    def fn(i, rv, hv, fv):
        (xb,), (gg,) = rv, fv
        return [xb * _rms(xb) * gg], []
    return _rows(fn, "prenorm_fwd", 256, [(x, D, 0)], fulls=[g], outs=[(D, BF16)])[0]


def _gm_mask():
    r = lax.broadcasted_iota(jnp.int32, (GM_B, GM_B), 0) // CHUNK
    c = lax.broadcasted_iota(jnp.int32, (GM_B, GM_B), 1) // CHUNK
    return c <= r


def _gm_norm(v, g, b):
    mu = jnp.mean(v, axis=-1, keepdims=True)
    vc = v - mu
    rs = lax.rsqrt(jnp.mean(vc * vc, axis=-1, keepdims=True) + EPS)
    vh = vc * rs
    return vh, rs, vh * g + b


def _gm_sv(vn, ws, bst):
    mask = _gm_mask()
    gw = GM_W // GM_G
    parts = []
    for g in range(GM_G):
        wm = jnp.where(mask, ws[g], 0.0).astype(BF16)
        parts.append(jnp.dot(wm, vn[:, g * gw:(g + 1) * gw].astype(BF16), preferred_element_type=F32)
                     + bst[:, g:g + 1])
    return jnp.concatenate(parts, axis=1)


def _gmlp_fwd(proj, ln_g, ln_b, ws, bst):
    def fn(i, rv, hv, fv):
        u, v, z = rv
        g, b, w, bt = fv
        _, _, vn = _gm_norm(v, g, b)
        return [u * _gm_sv(vn, w, bt) * _silu(z)], []
    return _rows(fn, "gmlp_fwd", GM_B, [(proj, GM_W, 0), (proj, GM_W, 1), (proj, GM_W, 2)],
                 fulls=[ln_g, ln_b, ws, bst], outs=[(GM_W, BF16)])[0]


def _mla_prep_fwd(proj, qg, kvg):
    def fn(i, rv, hv, fv):
        cq, ckv = rv
        g1, g2 = fv
        return [cq * _rms(cq) * g1, ckv * _rms(ckv) * g2], []
    return _rows(fn, "mla_prep_fwd", 256, [(proj, QR, O_CQ // QR), (proj, KVR, O_CKV // KVR)],
                 fulls=[qg, kvg], outs=[(QR, BF16), (KVR, BF16)])


def _rot(t, cc, sa, sb):
    return t * cc + pltpu.roll(t, 32, 1) * sa + pltpu.roll(t, 96, 1) * sb


def _rot_t(g, cc, sa, sb):
    return g * cc + pltpu.roll(g * sa, 96, 1) + pltpu.roll(g * sb, 32, 1)


def _rope_tables():
    pos = jnp.arange(S, dtype=F32)
    inv_freq = ROPE_THETA ** (-jnp.arange(0, ROPE, 2, dtype=F32) / ROPE)
    ang = pos[:, None] * inv_freq[None, :]
    cos, sin, z = jnp.cos(ang), jnp.sin(ang), jnp.zeros((S, 32), F32)
    cc = jnp.concatenate([cos, cos, z, z], axis=1)
    sa = jnp.concatenate([z, sin, z, z], axis=1)
    sb = jnp.concatenate([-sin, z, z, z], axis=1)
    return cc, sa, sb


def _rope_fwd(q, kv, proj, tabs):
    def fn(i, rv, hv, fv):
        qb, kvb, kr, cc, sa, sb = rv
        krr = _rot(kr, cc, sa, sb)
        qs, ks = [], []
        for h in range(H):
            qs += [qb[:, h * HP:h * HP + 128], _rot(qb[:, h * HP + 128:(h + 1) * HP], cc, sa, sb)]
            ks += [kvb[:, h * 128:(h + 1) * 128], krr]
        return [jnp.concatenate(qs, axis=1), jnp.concatenate(ks, axis=1), kvb[:, H * NOPE:]], []
    cc, sa, sb = tabs
    return _rows(fn, "rope_fwd", 256,
                 [(q, H * HP, 0), (kv, H * 256, 0), (proj, 128, O_KR // 128), (cc, 128, 0), (sa, 128, 0), (sb, 128, 0)],
                 outs=[(H * HP, BF16), (H * HP, BF16), (MLA_W, BF16)])


TQ = 256


def _attn_mask(i, kb):
    qpos = i * TQ + lax.broadcasted_iota(jnp.int32, (TQ, TQ), 0)
    kpos = kb * TQ + lax.broadcasted_iota(jnp.int32, (TQ, TQ), 1)
    return (kpos // CHUNK) <= (qpos // CHUNK)


def _attn_fwd(qc, kc, vv):
    scale = 1.0 / math.sqrt(NOPE + ROPE)
    nt = (((1,), (1,)), ((), ()))

    def body(q_ref, k_ref, v_ref, o_ref, l_ref):
        i = pl.program_id(1)
        q = q_ref[...]

        def step(kb, carry):
            m, l, acc = carry
            t0 = pl.multiple_of(kb * TQ, TQ)
            k = k_ref[pl.ds(t0, TQ), :]
            v = v_ref[pl.ds(t0, TQ), :]
            s = lax.dot_general(q, k, nt, preferred_element_type=F32) * scale
            s = jnp.where(_attn_mask(i, kb), s, -1e30)
            m_new = jnp.maximum(m, jnp.max(s, axis=-1, keepdims=True))
            p = jnp.exp(s - m_new)
            alpha = jnp.exp(m - m_new)
            l = alpha * l + jnp.sum(p, axis=-1, keepdims=True)
            acc = alpha * acc + jnp.dot(p.astype(BF16), v, preferred_element_type=F32)
            return m_new, l, acc

        m, l, acc = lax.fori_loop(0, i + 1, step, (jnp.full((TQ, 1), -1e30, F32), jnp.zeros((TQ, 1), F32),
                                                   jnp.zeros((TQ, VDIM), F32)))
        o_ref[...] = acc / l
        l_ref[...] = jnp.broadcast_to(m + jnp.log(l), (TQ, 128))

    return pl.pallas_call(
        body, name="attn_fwd", grid=(H, S // TQ),
        in_specs=[pl.BlockSpec((TQ, HP), lambda h, i: (i, h)),
                  pl.BlockSpec((S, HP), lambda h, i: (0, h)),
                  pl.BlockSpec((S, VDIM), lambda h, i: (0, h))],
        out_specs=[pl.BlockSpec((TQ, VDIM), lambda h, i: (i, h)), pl.BlockSpec((TQ, 128), lambda h, i: (i, h))],
        out_shape=[jax.ShapeDtypeStruct((S, MLA_W), F32), jax.ShapeDtypeStruct((S, H * 128), F32)],
        compiler_params=pltpu.CompilerParams(dimension_semantics=("parallel", "arbitrary"),
                                             vmem_limit_bytes=VMEM_LIMIT),
    )(qc, kc, vv)


def _gate_mul_fwd(name, val, proj, width, cb):
    def fn(i, rv, hv, fv):
        o, z = rv
        return [o * _silu(z)], []
    return _rows(fn, name, 256, [(val, width, 0), (proj, width, cb)], outs=[(width, BF16)])[0]


def _conv_fwd(proj, w, b):
    def fn(i, rv, hv, fv):
        (xb,), (halo,), (ww, bb) = rv, hv, fv
        halo = jnp.where(i > 0, halo, 0.0)
        row = lax.broadcasted_iota(jnp.int32, xb.shape, 0)
        acc = bb + ww[3:4] * xb
        for s in range(1, CONV_W):
            acc = acc + ww[3 - s:4 - s] * _shift_down(xb, halo, s, row)
        return [acc, acc], []
    return _rows(fn, "conv_fwd", 128, [(proj, LRU_W, O_XC // LRU_W)], halos=[(proj, LRU_W, O_XC // LRU_W, "prev")],
                 fulls=[w, b], outs=[(LRU_W, F32), (LRU_W, BF16)])


def _lru_terms(ga, gx, xc, ba, bx, lam):
    r = _sig(ga + ba)
    ig = _sig(gx + bx)
    sp = jnp.maximum(-lam, 0.0) + jnp.log(1.0 + jnp.exp(-jnp.abs(lam)))
    log_a = -LRU_C * r * sp
    a = jnp.exp(log_a)
    e2 = jnp.exp(2.0 * log_a)
    om = 1.0 - e2
    mult = jnp.sqrt(jnp.maximum(om, 0.0))
    return r, ig, sp, a, e2, om, mult


def _lru_gates_fwd(gates, xc, ba, bx, lam):
    def fn(i, rv, hv, fv):
        ga, gx, x = rv
        r, ig, sp, a, e2, om, mult = _lru_terms(ga, gx, x, *fv)
        return [a, mult * (ig * x)], []
    return _rows(fn, "lru_gates_fwd", 128, [(gates, LRU_W, 0), (gates, LRU_W, 1), (xc, LRU_W, 0)],
                 fulls=[ba, bx, lam], outs=[(LRU_W, F32), (LRU_W, F32)])


SCAN_T, SCAN_CW = 64, 256


def _scan_fwd(a, b):
    def body(a_ref, b_ref, h_ref):
        row = lax.broadcasted_iota(jnp.int32, (SCAN_T, SCAN_CW), 0)

        def step(blk, hc):
            t0 = pl.multiple_of(blk * SCAN_T, SCAN_T)
            A = a_ref[pl.ds(t0, SCAN_T), :]
            B = b_ref[pl.ds(t0, SCAN_T), :]
            d = 1
            while d < SCAN_T:
                keep = row >= d
                A_s = jnp.where(keep, pltpu.roll(A, d, 0), 1.0)
                B_s = jnp.where(keep, pltpu.roll(B, d, 0), 0.0)
                B = A * B_s + B
                A = A * A_s
                d *= 2
            hh = A * hc + B
            h_ref[pl.ds(t0, SCAN_T), :] = hh
            return hh[SCAN_T - 1:SCAN_T, :]

        lax.fori_loop(0, S // SCAN_T, step, jnp.zeros((1, SCAN_CW), F32))

    spec = pl.BlockSpec((S, SCAN_CW), lambda j: (0, j))
    return pl.pallas_call(
        body, name="scan_fwd", grid=(LRU_W // SCAN_CW,), in_specs=[spec, spec], out_specs=spec,
        out_shape=jax.ShapeDtypeStruct((S, LRU_W), F32),
        compiler_params=pltpu.CompilerParams(dimension_semantics=("parallel",), vmem_limit_bytes=VMEM_LIMIT),
    )(a, b)


def _merge_fwd(pa, pb, pc, proj):
    def fn(i, rv, hv, fv):
        a, b, c, ga, gb, gc = rv
        return [_sig(ga) * a + _sig(gb) * b + _sig(gc) * c], []
    return _rows(fn, "merge_fwd", 256,
                 [(pa, D, 0), (pb, D, 0), (pc, D, 0), (proj, D, O_GA // D), (proj, D, O_GB // D), (proj, D, O_GC // D)],
                 outs=[(D, BF16)])[0]


def _post_fwd(x, o2, g):
    def fn(i, rv, hv, fv):
        xb, ob = rv
        return [xb + ob * _rms(ob) * fv[0]], []
    return _rows(fn, "post_fwd", 256, [(x, D, 0), (o2, D, 0)], fulls=[g], outs=[(D, F32)])[0]


SB = 640
BD_TM = 512


def _bd_fwd(xcb, wsb, l):
    def body(x_ref, w_ref, o_ref):
        o_ref[...] = jnp.dot(x_ref[...], w_ref[...], preferred_element_type=F32)

    return pl.pallas_call(
        body, name="lru_gate_mm", grid=(S // BD_TM, 4),
        in_specs=[pl.BlockSpec((BD_TM, SB), lambda i, q: (i, q % 2)),
                  pl.BlockSpec((None, None, SB, SB), lambda i, q: (l, q, 0, 0))],
        out_specs=pl.BlockSpec((BD_TM, SB), lambda i, q: (i, q)),
        out_shape=jax.ShapeDtypeStruct((S, 2 * LRU_W), F32),
        compiler_params=pltpu.CompilerParams(dimension_semantics=("parallel", "parallel"), vmem_limit_bytes=VMEM_LIMIT),
    )(xcb, wsb)


def _bd_dx(dgates, wsb, l):
    def body(d_ref, w_ref, o_ref, acc_ref):
        g = pl.program_id(2)

        @pl.when(g == 0)
        def _():
            acc_ref[...] = jnp.zeros_like(acc_ref)

        acc_ref[...] += lax.dot_general(d_ref[...], w_ref[...], (((1,), (1,)), ((), ())), preferred_element_type=F32)

        @pl.when(g == 1)
        def _():
            o_ref[...] = acc_ref[...]

    return pl.pallas_call(
        body, name="lru_gate_dx", grid=(S // BD_TM, 2, 2),
        in_specs=[pl.BlockSpec((BD_TM, SB), lambda i, s, g: (i, 2 * g + s)),
                  pl.BlockSpec((None, None, SB, SB), lambda i, s, g: (l, 2 * g + s, 0, 0))],
        out_specs=pl.BlockSpec((BD_TM, SB), lambda i, s, g: (i, s)),
        out_shape=jax.ShapeDtypeStruct((S, LRU_W), F32),
        scratch_shapes=[pltpu.VMEM((BD_TM, SB), F32)],
        compiler_params=pltpu.CompilerParams(dimension_semantics=("parallel", "parallel", "arbitrary"),
                                             vmem_limit_bytes=VMEM_LIMIT),
    )(dgates, wsb)


def _bd_dw(xcb, dgates):
    tk = 1024

    def body(x_ref, d_ref, o_ref):
        @pl.when(pl.program_id(1) == 0)
        def _():
            o_ref[...] = jnp.zeros_like(o_ref)

        o_ref[...] += lax.dot_general(x_ref[...], d_ref[...], (((0,), (0,)), ((), ())), preferred_element_type=F32)

    return pl.pallas_call(
        body, name="lru_gate_dw", grid=(4, S // tk),
        in_specs=[pl.BlockSpec((tk, SB), lambda q, k: (k, q % 2)), pl.BlockSpec((tk, SB), lambda q, k: (k, q))],
        out_specs=pl.BlockSpec((None, SB, SB), lambda q, k: (q, 0, 0)),
        out_shape=jax.ShapeDtypeStruct((4, SB, SB), F32),
        compiler_params=pltpu.CompilerParams(dimension_semantics=("parallel", "arbitrary"), vmem_limit_bytes=VMEM_LIMIT),
    )(xcb, dgates)


def _bd_extract(dwsb):
    def body(w_ref, o_ref):
        lane = lax.broadcasted_iota(jnp.int32, (LRU_BW, 128), 1)
        for q in range(4):
            for kk in range(8):
                c0 = LRU_BW * kk
                w0, off = (c0 // 128) * 128, c0 % 128
                rows = pl.ds(LRU_BW * kk, LRU_BW)
                blk = w_ref[q, rows, w0:w0 + 128]
                if off:
                    blk = pltpu.roll(blk, 128 - off, 1)
                    if off + LRU_BW > 128:
                        nxt = pltpu.roll(w_ref[q, rows, w0 + 128:w0 + 256], 128 - off, 1)
                        blk = jnp.where(lane < 128 - off, blk, nxt)
                o_ref[q // 2, 8 * (q % 2) + kk] = blk.astype(BF16)

    return pl.pallas_call(
        body, name="lru_gate_dw_blocks",
        in_specs=[pl.BlockSpec(memory_space=pltpu.VMEM)], out_specs=pl.BlockSpec(memory_space=pltpu.VMEM),
        out_shape=jax.ShapeDtypeStruct((2, LRU_NB, LRU_BW, 128), BF16),
        compiler_params=pltpu.CompilerParams(vmem_limit_bytes=VMEM_LIMIT),
    )(dwsb)


def _layer_fwd(x, P, l, tabs):
    A = {"x": x}
    A["h"] = _prenorm_fwd(x, P["pre_g"])
    proj = A["proj"] = _mm(A["h"], P["wp"], "nt", "in_proj", b_lead=l)
    A["ya"] = _gmlp_fwd(proj, P["ln_g"], P["ln_b"], P["ws"], P["bst"])
    A["cqn"], A["ckvn"] = _mla_prep_fwd(proj, P["qg"], P["kvg"])
    q = _mm(A["cqn"], P["wuq"], "nt", "q_up", b_lead=l)
    kv = _mm(A["ckvn"], P["wukv"], "nt", "kv_up", b_lead=l)
    A["qc"], A["kc"], A["vv"] = _rope_fwd(q, kv, proj, tabs)
    A["o"], A["lse"] = _attn_fwd(A["qc"], A["kc"], A["vv"])
    A["yb"] = _gate_mul_fwd("yb_fwd", A["o"], proj, MLA_W, O_ZB // MLA_W)
    A["xc"], A["xcb"] = _conv_fwd(proj, P["conv_w"], P["conv_b"])
    A["gates"] = _bd_fwd(A["xcb"], P["wsb"], l)
    A["a"], bterm = _lru_gates_fwd(A["gates"], A["xc"], P["ba"], P["bx"], P["lam"])
    A["hs"] = _scan_fwd(A["a"], bterm)
    A["yc"] = _gate_mul_fwd("yc_fwd", A["hs"], proj, LRU_W, O_ZC // LRU_W)
    A["pa"] = _mm(A["ya"], P["wpa"], "nn", "proj_a", b_lead=l)
    A["pb"] = _mm(A["yb"], P["wpb"], "nn", "proj_b", b_lead=l)
    A["pc"] = _mm(A["yc"], P["wpc"], "nn", "proj_c", b_lead=l)
    A["merged"] = _merge_fwd(A["pa"], A["pb"], A["pc"], proj)
    A["o2"] = _mm(A["merged"], P["wout"], "nn", "out_proj", b_lead=l)
    return _post_fwd(x, A["o2"], P["post_g"]), A


def _loss_fwd(y, tgt):
    def fn(i, rv, hv, fv):
        yb, tb = rv
        e = yb - tb
        part = 0.5 * jnp.sum(jnp.mean(e * e, axis=-1, keepdims=True), axis=0, keepdims=True)
        return [e * (1.0 / D)], [part]
    return _rows(fn, "loss", 256, [(y, D, 0), (tgt, D, 0)], outs=[(D, F32)], accs=[(1, 1)])


def _post_bwd(dxn, o2, g):
    def fn(i, rv, hv, fv):
        dy, ob = rv
        dx, dg = _rms_bwd(dy, ob, fv[0])
        return [dx], [_colsum(dg)]
    return _rows(fn, "post_bwd", 256, [(dxn, D, 0), (o2, D, 0)], fulls=[g], outs=[(D, BF16)], accs=[(1, D)])


def _merge_bwd(dm, pa, pb, pc, proj):
    def fn(i, rv, hv, fv):
        d, a, b, c, ga, gb, gc = rv
        outs_p, outs_g = [], []
        for p, gg in ((a, ga), (b, gb), (c, gc)):
            s = _sig(gg)
            outs_p.append(d * s)
            outs_g.append(d * p * s * (1.0 - s))
        return outs_p + outs_g, []
    return _rows(fn, "merge_bwd", 128,
                 [(dm, D, 0), (pa, D, 0), (pb, D, 0), (pc, D, 0),
                  (proj, D, O_GA // D), (proj, D, O_GB // D), (proj, D, O_GC // D)],
                 outs=[(D, BF16)] * 6)


def _gmlp_bwd(dya, proj, ln_g, ln_b, ws, bst):
    gw = GM_W // GM_G

    def fn(i, rv, hv, fv):
        dy, u, v, z = rv
        g, b, w, bt = fv
        vh, rs, vn = _gm_norm(v, g, b)
        sv = _gm_sv(vn, w, bt)
        sz = _silu(z)
        du = dy * sv * sz
        dsv = dy * u * sz
        dz = dy * u * sv * _dsilu(z)
        mask = _gm_mask()
        lane = lax.broadcasted_iota(jnp.int32, (GM_B, 128), 1)
        dvn_parts, dws, dbst = [], [], jnp.zeros((GM_B, 128), F32)
        for k in range(GM_G):
            wm = jnp.where(mask, w[k], 0.0).astype(BF16)
            dsk = dsv[:, k * gw:(k + 1) * gw]
            dskb = dsk.astype(BF16)
            dvn_parts.append(lax.dot_general(wm, dskb, (((0,), (0,)), ((), ())), preferred_element_type=F32))
            dwk = lax.dot_general(dskb, vn[:, k * gw:(k + 1) * gw].astype(BF16), (((1,), (1,)), ((), ())),
                                  preferred_element_type=F32)
            dws.append(jnp.where(mask, dwk, 0.0)[None])
            dbst = dbst + jnp.where(lane == k, jnp.sum(dsk, axis=1, keepdims=True), 0.0)
        dvn = jnp.concatenate(dvn_parts, axis=1)
        dvh = dvn * g
        dv = rs * (dvh - jnp.mean(dvh, axis=-1, keepdims=True) - vh * jnp.mean(dvh * vh, axis=-1, keepdims=True))
        return [du, dv, dz], [jnp.concatenate(dws, axis=0), dbst, _colsum(dvn * vh), _colsum(dvn)]
    return _rows(fn, "gmlp_bwd", GM_B, [(dya, GM_W, 0), (proj, GM_W, 0), (proj, GM_W, 1), (proj, GM_W, 2)],
                 fulls=[ln_g, ln_b, ws, bst], outs=[(GM_W, BF16)] * 3,
                 accs=[(GM_G, GM_B, GM_B), (GM_B, 128), (1, GM_W), (1, GM_W)])


def _yb_bwd(dyb, o, proj):
    def fn(i, rv, hv, fv):
        dy, ob, z = rv
        do = dy * _silu(z)
        prod = do * ob
        dl = [jnp.broadcast_to(jnp.sum(prod[:, h * VDIM:(h + 1) * VDIM], axis=1, keepdims=True), (dy.shape[0], 128))
              for h in range(H)]
        return [do, jnp.concatenate(dl, axis=1), dy * ob * _dsilu(z)], []
    return _rows(fn, "yb_bwd", 256, [(dyb, MLA_W, 0), (o, MLA_W, 0), (proj, MLA_W, O_ZB // MLA_W)],
                 outs=[(MLA_W, BF16), (H * 128, F32), (MLA_W, BF16)])


def _attn_bwd(qc, kc, vv, do, lse, dl):
    scale = 1.0 / math.sqrt(NOPE + ROPE)
    nt = (((1,), (1,)), ((), ()))
    tn = (((0,), (0,)), ((), ()))

    def body(q_ref, k_ref, v_ref, do_ref, l_ref, d_ref, dq_ref, dk_ref, dv_ref):
        i = pl.program_id(1)

        @pl.when(i == 0)
        def _():
            dk_ref[...] = jnp.zeros_like(dk_ref)
            dv_ref[...] = jnp.zeros_like(dv_ref)

        q = q_ref[...]
        dob = do_ref[...]
        lse_c = l_ref[...][:, 0:1]
        dl_c = d_ref[...][:, 0:1]

        def step(kb, dq):
            t0 = pl.multiple_of(kb * TQ, TQ)
            k = k_ref[pl.ds(t0, TQ), :]
            v = v_ref[pl.ds(t0, TQ), :]
            s = lax.dot_general(q, k, nt, preferred_element_type=F32) * scale
            p = jnp.where(_attn_mask(i, kb), jnp.exp(s - lse_c), 0.0)
            dp = lax.dot_general(dob, v, nt, preferred_element_type=F32)
            ds = (p * (dp - dl_c) * scale).astype(BF16)
            dk_ref[pl.ds(t0, TQ), :] += lax.dot_general(ds, q, tn, preferred_element_type=F32)
            dv_ref[pl.ds(t0, TQ), :] += lax.dot_general(p.astype(BF16), dob, tn, preferred_element_type=F32)
            return dq + jnp.dot(ds, k, preferred_element_type=F32)

        dq_ref[...] = lax.fori_loop(0, i + 1, step, jnp.zeros((TQ, HP), F32))

    blk = lambda w: pl.BlockSpec((TQ, w), lambda h, i: (i, h))
    head = lambda w: pl.BlockSpec((S, w), lambda h, i: (0, h))
    return pl.pallas_call(
        body, name="attn_bwd", grid=(H, S // TQ),
        in_specs=[blk(HP), head(HP), head(VDIM), blk(VDIM), blk(128), blk(128)],
        out_specs=[blk(HP), head(HP), head(VDIM)],
        out_shape=[jax.ShapeDtypeStruct((S, H * HP), F32), jax.ShapeDtypeStruct((S, H * HP), F32),
                   jax.ShapeDtypeStruct((S, MLA_W), F32)],
        compiler_params=pltpu.CompilerParams(dimension_semantics=("parallel", "arbitrary"),
                                             vmem_limit_bytes=VMEM_LIMIT),
    )(qc, kc, vv, do, lse, dl)


def _rope_bwd(dqc, dkc, dvv, tabs):
    def fn(i, rv, hv, fv):
        dq, dk, dv, cc, sa, sb = rv
        qs, ks = [], []
        dkr = jnp.zeros((dq.shape[0], 128), F32)
        for h in range(H):
            qs += [dq[:, h * HP:h * HP + 128], _rot_t(dq[:, h * HP + 128:(h + 1) * HP], cc, sa, sb)]
            ks.append(dk[:, h * HP:h * HP + 128])
            dkr = dkr + dk[:, h * HP + 128:(h + 1) * HP]
        return [jnp.concatenate(qs, axis=1), jnp.concatenate(ks + [dv], axis=1), _rot_t(dkr, cc, sa, sb)], []
    cc, sa, sb = tabs
    return _rows(fn, "rope_bwd", 256,
                 [(dqc, H * HP, 0), (dkc, H * HP, 0), (dvv, MLA_W, 0), (cc, 128, 0), (sa, 128, 0), (sb, 128, 0)],
                 outs=[(H * HP, BF16), (H * 256, BF16), (128, BF16)])


def _mla_prep_bwd(dcqn, dckvn, proj, qg, kvg):
    def fn(i, rv, hv, fv):
        d1, d2, cq, ckv = rv
        g1, g2 = fv
        dx1, dg1 = _rms_bwd(d1, cq, g1)
        dx2, dg2 = _rms_bwd(d2, ckv, g2)
        return [dx1, dx2], [_colsum(dg1), _colsum(dg2)]
    return _rows(fn, "mla_prep_bwd", 256,
                 [(dcqn, QR, 0), (dckvn, KVR, 0), (proj, QR, O_CQ // QR), (proj, KVR, O_CKV // KVR)],
                 fulls=[qg, kvg], outs=[(QR, BF16), (KVR, BF16)], accs=[(1, QR), (1, KVR)])


def _yc_bwd(dyc, hs, proj):
    def fn(i, rv, hv, fv):
        dy, hh, z = rv
        return [dy * _silu(z), dy * hh * _dsilu(z)], []
    return _rows(fn, "yc_bwd", 128, [(dyc, LRU_W, 0), (hs, LRU_W, 0), (proj, LRU_W, O_ZC // LRU_W)],
                 outs=[(LRU_W, F32), (LRU_W, BF16)])


def _scan_bwd(a, hs, dh):
    nblk = S // SCAN_T

    def body(a_ref, h_ref, dh_ref, da_ref, db_ref):
        row = lax.broadcasted_iota(jnp.int32, (SCAN_T, SCAN_CW), 0)

        def step(j, carry):
            gc, ac = carry
            blk = nblk - 1 - j
            t0 = pl.multiple_of(blk * SCAN_T, SCAN_T)
            av = a_ref[pl.ds(t0, SCAN_T), :]
            A = jnp.where(row < SCAN_T - 1, pltpu.roll(av, SCAN_T - 1, 0), ac)
            B = dh_ref[pl.ds(t0, SCAN_T), :]
            d = 1
            while d < SCAN_T:
                keep = row < SCAN_T - d
                A_s = jnp.where(keep, pltpu.roll(A, SCAN_T - d, 0), 1.0)
                B_s = jnp.where(keep, pltpu.roll(B, SCAN_T - d, 0), 0.0)
                B = A * B_s + B
                A = A * A_s
                d *= 2
            g = A * gc + B
            p0 = pl.multiple_of(jnp.maximum(t0 - 8, 0), 8)
            last = jnp.where(blk > 0, h_ref[pl.ds(p0, 8), :][7:8, :], 0.0)
            h_prev = jnp.where(row >= 1, pltpu.roll(h_ref[pl.ds(t0, SCAN_T), :], 1, 0), last)
            da_ref[pl.ds(t0, SCAN_T), :] = g * h_prev
            db_ref[pl.ds(t0, SCAN_T), :] = g
            return g[0:1, :], av[0:1, :]

        z = jnp.zeros((1, SCAN_CW), F32)
        lax.fori_loop(0, nblk, step, (z, z))

    spec = pl.BlockSpec((S, SCAN_CW), lambda j: (0, j))
    return pl.pallas_call(
        body, name="scan_bwd", grid=(LRU_W // SCAN_CW,), in_specs=[spec] * 3, out_specs=[spec] * 2,
        out_shape=[jax.ShapeDtypeStruct((S, LRU_W), F32)] * 2,
        compiler_params=pltpu.CompilerParams(dimension_semantics=("parallel",), vmem_limit_bytes=VMEM_LIMIT),
    )(a, hs, dh)


def _lru_gates_bwd(da, db, gates, xc, ba, bx, lam):
    def fn(i, rv, hv, fv):
        dav, dbv, ga, gx, x = rv
        bav, bxv, lamv = fv
        r, ig, sp, a, e2, om, mult = _lru_terms(ga, gx, x, bav, bxv, lamv)
        dmult = dbv * ig * x
        dig = dbv * mult * x
        dxc1 = dbv * mult * ig
        dlog_a = dav * a + jnp.where(om > 0.0, dmult * (-e2 / mult), 0.0)
        dr = dlog_a * (-LRU_C * sp)
        dga = dr * r * (1.0 - r)
        dgx = dig * ig * (1.0 - ig)
        dlam = _colsum(dlog_a * (-LRU_C * r)) * (-_sig(-lamv))
        return [jnp.concatenate([dga, dgx], axis=1), dxc1], [_colsum(dga), _colsum(dgx), dlam]
    return _rows(fn, "lru_gates_bwd", 128,
                 [(da, LRU_W, 0), (db, LRU_W, 0), (gates, LRU_W, 0), (gates, LRU_W, 1), (xc, LRU_W, 0)],
                 fulls=[ba, bx, lam], outs=[(2 * LRU_W, BF16), (LRU_W, F32)], accs=[(1, LRU_W)] * 3)


def _conv_bwd(dxc1, dxc2, proj, w):
    cb = O_XC // LRU_W

    def fn(i, rv, hv, fv):
        d1, d2, xb = rv
        n1, n2, xprev = hv
        ww = fv[0]
        last = i == S // 128 - 1
        dxc = d1 + d2
        nxt = jnp.where(last, 0.0, n1 + n2)
        xprev = jnp.where(i > 0, xprev, 0.0)
        row = lax.broadcasted_iota(jnp.int32, xb.shape, 0)
        dx = ww[3:4] * dxc
        dws = [None] * CONV_W
        dws[3] = _colsum(dxc * xb)
        for s in range(1, CONV_W):
            dx = dx + ww[3 - s:4 - s] * _shift_up(dxc, nxt, s, row)
            dws[3 - s] = _colsum(dxc * _shift_down(xb, xprev, s, row))
        return [dx], [jnp.concatenate(dws, axis=0), _colsum(dxc)]
    return _rows(fn, "conv_bwd", 128, [(dxc1, LRU_W, 0), (dxc2, LRU_W, 0), (proj, LRU_W, cb)],
                 halos=[(dxc1, LRU_W, 0, "next"), (dxc2, LRU_W, 0, "next"), (proj, LRU_W, cb, "prev")],
                 fulls=[w], outs=[(LRU_W, BF16)], accs=[(CONV_W, LRU_W), (1, LRU_W)])


def _prenorm_bwd(dxn, dh, x, g):
    def fn(i, rv, hv, fv):
        dy, dhh, xb = rv
        dx, dg = _rms_bwd(dhh, xb, fv[0])
        return [dy + dx], [_colsum(dg)]
    return _rows(fn, "prenorm_bwd", 256, [(dxn, D, 0), (dh, D, 0), (x, D, 0)], fulls=[g], outs=[(D, F32)],
                 accs=[(1, D)])


def _layer_bwd(dxn, A, P, l, tabs, GB):
    G = {}
    GB = dict(GB) if GB is not None else {}
    proj = A["proj"]

    def dw(key, a, b, name):
        GB[key] = _mm(a, b, "tn", name, out_dtype=BF16, out_lead=(l, DEPTH, GB.get(key)))

    do2, G["post_g"] = _post_bwd(dxn, A["o2"], P["post_g"])
    dm = _mm(do2, P["wout"], "nt", "out_proj_dx", b_lead=l)
    dw("wout", A["merged"], do2, "out_proj_dw")
    dpa, dpb, dpc, dga, dgb, dgc = _merge_bwd(dm, A["pa"], A["pb"], A["pc"], proj)
    dya = _mm(dpa, P["wpa"], "nt", "proj_a_dx", b_lead=l)
    dw("wpa", A["ya"], dpa, "proj_a_dw")
    dyb = _mm(dpb, P["wpb"], "nt", "proj_b_dx", b_lead=l)
    dw("wpb", A["yb"], dpb, "proj_b_dw")
    dyc = _mm(dpc, P["wpc"], "nt", "proj_c_dx", b_lead=l)
    dw("wpc", A["yc"], dpc, "proj_c_dw")
    du, dv, dza, G["ws"], G["bst"], G["ln_g"], G["ln_b"] = _gmlp_bwd(dya, proj, P["ln_g"], P["ln_b"], P["ws"], P["bst"])
    do, dl, dzb = _yb_bwd(dyb, A["o"], proj)
    dqc, dkc, dvv = _attn_bwd(A["qc"], A["kc"], A["vv"], do, A["lse"], dl)
    dq, dkv, dkr = _rope_bwd(dqc, dkc, dvv, tabs)
    dcqn = _mm(dq, P["wuq"], "nn", "q_up_dx", b_lead=l)
    dw("wuq", dq, A["cqn"], "q_up_dw")
    dckvn = _mm(dkv, P["wukv"], "nn", "kv_up_dx", b_lead=l)
    dw("wukv", dkv, A["ckvn"], "kv_up_dw")
    dcq, dckv, G["qg"], G["kvg"] = _mla_prep_bwd(dcqn, dckvn, proj, P["qg"], P["kvg"])
    dhs, dzc = _yc_bwd(dyc, A["hs"], proj)
    da, db = _scan_bwd(A["a"], A["hs"], dhs)
    dgates, dxc1, G["ba"], G["bx"], G["lam"] = _lru_gates_bwd(da, db, A["gates"], A["xc"], P["ba"], P["bx"], P["lam"])
    dxc2 = _bd_dx(dgates, P["wsb"], l)
    G["wab"] = _bd_extract(_bd_dw(A["xcb"], dgates))
    dxcc, G["conv_w"], G["conv_b"] = _conv_bwd(dxc1, dxc2, proj, P["conv_w"])
    zpad = jnp.zeros((S, 128), BF16)
    dproj = jnp.concatenate([du, dv, dza, dzb, dga, dgb, dgc, dckv, dkr, zpad, dxcc, dzc, zpad, dcq], axis=1)
    dh = _mm(dproj, P["wp"], "nn", "in_proj_dx", b_lead=l)
    dw("wp", dproj, A["h"], "in_proj_dw")
    dx, G["pre_g"] = _prenorm_bwd(dxn, dh, A["x"], P["pre_g"])
    return dx, G, GB


_ORIG_OFF = [0]
for _s in IN_SIZES:
    _ORIG_OFF.append(_ORIG_OFF[-1] + _s)
_PAD_OFF = {0: O_U, 1: O_V, 2: O_ZA, 3: O_CQ, 4: O_CKV, 5: O_KR, 6: O_ZB, 7: O_XC, 8: O_ZC, 9: O_GA, 10: O_GB, 11: O_GC}
SHARD_IN = N_IN // N_CHIPS


def _pieces_w_in(j):
    lo, hi = SHARD_IN * j, SHARD_IN * (j + 1)
    out = []
    for k in range(len(IN_SIZES)):
        a, b = max(lo, _ORIG_OFF[k]), min(hi, _ORIG_OFF[k + 1])
        if a < b:
            out.append((a - lo, _PAD_OFF[k] + a - _ORIG_OFF[k], b - a))
    return out


def _pieces_uq(j):
    return [(192 * hh, HP * (2 * j + hh), NOPE + ROPE) for hh in range(2)]


def _pieces_ukv(j):
    out = []
    for hh in range(2):
        h = 2 * j + hh
        out += [(256 * hh, NOPE * h, NOPE), (256 * hh + NOPE, H * NOPE + VDIM * h, VDIM)]
    return out


def _pieces_rows(r):
    return lambda j: [(0, r * j, r)]


LAYOUT = {
    "w_in": (SHARD_IN, NP, _pieces_w_in),
    "mla_w_uq": (2 * (NOPE + ROPE), H * HP, _pieces_uq),
    "mla_w_ukv": (2 * (NOPE + VDIM), 2 * H * 128, _pieces_ukv),
    "lru_conv_w": (1, N_CHIPS, _pieces_rows(1)),
    "w_proj_a": (GM_W // N_CHIPS, GM_W, _pieces_rows(GM_W // N_CHIPS)),
    "w_proj_b": (MLA_W // N_CHIPS, MLA_W, _pieces_rows(MLA_W // N_CHIPS)),
    "w_proj_c": (LRU_W // N_CHIPS, LRU_W, _pieces_rows(LRU_W // N_CHIPS)),
    "w_out": (D // N_CHIPS, D, _pieces_rows(D // N_CHIPS)),
}
TRANSPOSED = ("w_in", "mla_w_uq", "mla_w_ukv")


def _superblocks(w_a, w_x):
    w6 = jnp.stack([w_a, w_x], axis=1).reshape(DEPTH, 4, 8, LRU_BW, LRU_BW)
    eye = jnp.eye(8, dtype=w_a.dtype)
    return (w6[:, :, :, :, None, :] * eye[None, None, :, None, :, None]).reshape(DEPTH, 4, SB, SB).astype(BF16)


_HBM = pl.BlockSpec(memory_space=pltpu.HBM)


def _position():
    return lax.axis_index("x"), lax.axis_index("y"), lax.axis_index("c")


def _allgather(blocks, name):
    n = len(blocks)

    def body(*refs):
        ins, outs = refs[:n], refs[n:2 * n]
        send, recv, lsem = refs[2 * n:]
        x, y, c = _position()
        me, sib = (x, y, c), (x, y, 1 - c)
        chips = [(1 - x, y), (x, 1 - y), (1 - x, 1 - y)]

        def cp(k, a, block, to, src=None):
            dst = outs[a].at[4 * block[0] + 2 * block[1] + block[2]]
            return pltpu.make_async_remote_copy(src_ref=dst if src is None else src, dst_ref=dst,
                                                send_sem=send.at[7 * a + k], recv_sem=recv.at[7 * a + k],
                                                device_id=to, device_id_type=MESH)

        mine = [pltpu.make_async_copy(ins[a], outs[a].at[4 * x + 2 * y + c], lsem.at[a]) for a in range(n)]
        for m in mine:
            m.start()
        first = []
        for a in range(n):
            first.append(cp(0, a, me, sib, src=ins[a]))
            first += [cp(1 + j, a, me, (*chip, c), src=ins[a]) for j, chip in enumerate(chips)]
        for f in first:
            f.start()
        passed = []
        for j, chip in enumerate(chips):
            for a in range(n):
                cp(1 + j, a, (*chip, c), me).wait_recv()
                p = cp(4 + j, a, (*chip, c), sib)
                p.start()
                passed.append(p)
        for a in range(n):
            cp(0, a, sib, me).wait_recv()
            for j, chip in enumerate(chips):
                cp(4 + j, a, (*chip, 1 - c), me).wait_recv()
        for f in first + passed:
            f.wait_send()
        for m in mine:
            m.wait()

    return pl.pallas_call(
        body, name=name,
        out_shape=[jax.ShapeDtypeStruct((8,) + b.shape, b.dtype) for b in blocks],
        in_specs=[_HBM] * n, out_specs=[_HBM] * n,
        scratch_shapes=[pltpu.SemaphoreType.DMA((7 * n,)), pltpu.SemaphoreType.DMA((7 * n,)),
                        pltpu.SemaphoreType.DMA((n,))],
    )(*blocks)


_REL = (2, 1, 3)


def _weights_allgather(names, srcs, name):
    n = len(srcs)
    lay = [LAYOUT[nm] for nm in names]
    zeros = [jnp.zeros((DEPTH, lay[a][1]) + srcs[a].shape[1:], srcs[a].dtype) for a in range(n)]

    def body(*refs):
        ins, outs = refs[:n], refs[2 * n:3 * n]
        send, recv, lsem = refs[3 * n:]
        x, y, c = _position()
        j = 2 * x + y
        sib = (x, y, 1 - c)
        chips = [(1 - x, y), (x, 1 - y), (1 - x, 1 - y)]

        def flow(a, k, jsrc, to, from_src):
            cps = []
            for s0, d0, nr in lay[a][2](jsrc):
                dst = outs[a].at[c, pl.ds(d0, nr)]
                src = ins[a].at[pl.ds(s0, nr)] if from_src else dst
                cps.append(pltpu.make_async_remote_copy(src_ref=src, dst_ref=dst, send_sem=send.at[7 * a + k],
                                                        recv_sem=recv.at[7 * a + k], device_id=to, device_id_type=MESH))
            return cps

        def whole(a, k):
            return pltpu.make_async_remote_copy(src_ref=ins[a], dst_ref=outs[a].at[0, pl.ds(0, lay[a][0])],
                                                send_sem=send.at[7 * a + k], recv_sem=recv.at[7 * a + k],
                                                device_id=sib, device_id_type=MESH)

        for j0 in range(N_CHIPS):
            @pl.when(j == j0)
            def _(j0=j0):
                for a in range(n):
                    for s0, d0, nr in lay[a][2](j0):
                        pltpu.make_async_copy(ins[a].at[pl.ds(s0, nr)], outs[a].at[c, pl.ds(d0, nr)], lsem.at[a]).start()
                for a in range(n):
                    for cp in flow(a, 0, j0, sib, True):
                        cp.start()
                    for k, chip in enumerate(chips):
                        for cp in flow(a, 1 + k, j0, (*chip, c), True):
                            cp.start()
                for k in range(3):
                    for a in range(n):
                        whole(a, 1 + k).wait_recv()
                        for cp in flow(a, 4 + k, j0 ^ _REL[k], sib, False):
                            cp.start()

        for a in range(n):
            whole(a, 0).wait_recv()
            for k in range(3):
                whole(a, 4 + k).wait_recv()
        for a in range(n):
            for k in range(7):
                whole(a, k).wait_send()
            pltpu.make_async_copy(ins[a], outs[a].at[0, pl.ds(0, lay[a][0])], lsem.at[a]).wait()

    return pl.pallas_call(
        body, name=name,
        out_shape=[jax.ShapeDtypeStruct(z.shape, z.dtype) for z in zeros],
        in_specs=[_HBM] * (2 * n), out_specs=[_HBM] * n,
        input_output_aliases={n + a: a for a in range(n)},
        scratch_shapes=[pltpu.SemaphoreType.DMA((7 * n,)), pltpu.SemaphoreType.DMA((7 * n,)),
                        pltpu.SemaphoreType.DMA((n,))],
    )(*srcs, *zeros)


def _grads_to_sibling(gb, name):
    n = len(gb)

    def body(*refs):
        ins, outs = refs[:n], refs[n:2 * n]
        send, recv = refs[2 * n:]
        x, y, c = _position()
        cps = [pltpu.make_async_remote_copy(src_ref=ins[a].at[1 - c], dst_ref=outs[a], send_sem=send.at[a],
                                            recv_sem=recv.at[a], device_id=(x, y, 1 - c), device_id_type=MESH)
               for a in range(n)]
        for cp in cps:
            cp.start()
        for cp in cps:
            cp.wait()

    return pl.pallas_call(
        body, name=name,
        out_shape=[jax.ShapeDtypeStruct(g.shape[1:], g.dtype) for g in gb],
        in_specs=[_HBM] * n, out_specs=[_HBM] * n,
        scratch_shapes=[pltpu.SemaphoreType.DMA((n,)), pltpu.SemaphoreType.DMA((n,))],
    )(*gb)


def _chip_scatter(names, parts, name):
    n = len(parts)
    lay = [LAYOUT[nm] for nm in names]

    def body(*refs):
        ins, outs = refs[:n], refs[n:2 * n]
        send, recv, lsem = refs[2 * n:]
        x, y, c = _position()
        j = 2 * x + y
        chips = [(1 - x, y), (x, 1 - y), (1 - x, 1 - y)]

        def whole(a):
            return outs[a].at[0, pl.ds(0, lay[a][0])]

        for j0 in range(N_CHIPS):
            @pl.when(j == j0)
            def _(j0=j0):
                for a in range(n):
                    for s0, d0, nr in lay[a][2](j0):
                        pltpu.make_async_copy(ins[a].at[pl.ds(d0, nr)], outs[a].at[j0, pl.ds(s0, nr)], lsem.at[a]).start()
                    for k, chip in enumerate(chips):
                        for s0, d0, nr in lay[a][2](j0 ^ _REL[k]):
                            pltpu.make_async_remote_copy(
                                src_ref=ins[a].at[pl.ds(d0, nr)], dst_ref=outs[a].at[j0, pl.ds(s0, nr)],
                                send_sem=send.at[3 * a + k], recv_sem=recv.at[3 * a + k],
                                device_id=(*chip, c), device_id_type=MESH).start()

        for a in range(n):
            for k in range(3):
                pltpu.make_async_remote_copy(src_ref=whole(a), dst_ref=whole(a), send_sem=send.at[3 * a + k],
                                             recv_sem=recv.at[3 * a + k], device_id=(x, y, c), device_id_type=MESH).wait()
            pltpu.make_async_copy(whole(a), whole(a), lsem.at[a]).wait()

    return pl.pallas_call(
        body, name=name,
        out_shape=[jax.ShapeDtypeStruct((N_CHIPS, lay[a][0]) + parts[a].shape[1:], parts[a].dtype) for a in range(n)],
        in_specs=[_HBM] * n, out_specs=[_HBM] * n,
        scratch_shapes=[pltpu.SemaphoreType.DMA((3 * n,)), pltpu.SemaphoreType.DMA((3 * n,)),
                        pltpu.SemaphoreType.DMA((n,))],
    )(*parts)


def _reduced_exchange(bufs, name):
    n = len(bufs)

    def body(*refs):
        ins, outs = refs[:n], refs[n:2 * n]
        send, recv = refs[2 * n:]
        x, y, c = _position()
        cps = [pltpu.make_async_remote_copy(src_ref=outs[a].at[c], dst_ref=outs[a].at[c], send_sem=send.at[a],
                                            recv_sem=recv.at[a], device_id=(x, y, 1 - c), device_id_type=MESH)
               for a in range(n)]
        for cp in cps:
            cp.start()
        for cp in cps:
            cp.wait()

    return pl.pallas_call(
        body, name=name,
        out_shape=[jax.ShapeDtypeStruct(b.shape, b.dtype) for b in bufs],
        in_specs=[_HBM] * n, out_specs=[_HBM] * n, input_output_aliases={a: a for a in range(n)},
        scratch_shapes=[pltpu.SemaphoreType.DMA((n,)), pltpu.SemaphoreType.DMA((n,))],
    )(*bufs)


def _row_tile(r):
    for t in (256, 128, 64, 32, 16, 8):
        if r % t == 0 and r > t:
            return t
    return r


def _pair_add(g, rb, c_arr, name):
    R, rest = g.shape[1], g.shape[2:]
    tr = _row_tile(R)
    z = (0,) * len(rest)

    def body(c_ref, g_ref, r_ref, o_ref):
        o_ref[...] = (g_ref[...].astype(F32) + r_ref[...].astype(F32)).astype(o_ref.dtype)

    return pl.pallas_call(
        body, name=name,
        grid_spec=pltpu.PrefetchScalarGridSpec(
            num_scalar_prefetch=1, grid=(R // tr,),
            in_specs=[pl.BlockSpec((None, tr) + rest, lambda i, c_ref: (c_ref[0], i) + z),
                      pl.BlockSpec((tr,) + rest, lambda i, c_ref: (i,) + z)],
            out_specs=pl.BlockSpec((tr,) + rest, lambda i, c_ref: (i,) + z)),
        out_shape=jax.ShapeDtypeStruct((R,) + rest, BF16),
        compiler_params=pltpu.CompilerParams(dimension_semantics=("parallel",), vmem_limit_bytes=VMEM_LIMIT),
    )(c_arr, g, rb)


def _sum_slabs(rb, c_arr, name):
    n, R, rest = rb.shape[0], rb.shape[1], rb.shape[2:]
    tr = _row_tile(R)
    z = (0,) * len(rest)

    def body(c_ref, r_ref, o_ref):
        acc = r_ref[0].astype(F32)
        for k in range(1, n):
            acc = acc + r_ref[k].astype(F32)
        o_ref[...] = acc

    if R // tr > 64 and len(rest) == 1 and rest[0] % 256 == 0:
        grid = (rest[0] // 256,)
        in_spec = pl.BlockSpec((n, R, 256), lambda i, c_ref: (0, 0, i))
        out_spec = pl.BlockSpec((None, R, 256), lambda i, c_ref: (c_ref[0], 0, i))
    else:
        grid = (R // tr,)
        in_spec = pl.BlockSpec((n, tr) + rest, lambda i, c_ref: (0, i) + z)
        out_spec = pl.BlockSpec((None, tr) + rest, lambda i, c_ref: (c_ref[0], i) + z)
    return pl.pallas_call(
        body, name=name,
        grid_spec=pltpu.PrefetchScalarGridSpec(num_scalar_prefetch=1, grid=grid, in_specs=[in_spec], out_specs=out_spec),
        out_shape=jax.ShapeDtypeStruct((DEPTH, R) + rest, F32),
        compiler_params=pltpu.CompilerParams(dimension_semantics=("parallel",), vmem_limit_bytes=VMEM_LIMIT),
    )(c_arr, rb)


def _adam_math(w, g, m, v):
    mn = ADAM_B1 * m + (1.0 - ADAM_B1) * g
    vn = ADAM_B2 * v + (1.0 - ADAM_B2) * (g * g)
    m_hat = mn / (1.0 - ADAM_B1 ** ADAM_STEP)
    v_hat = vn / (1.0 - ADAM_B2 ** ADAM_STEP)
    return -ADAM_LR * (m_hat / (jnp.sqrt(v_hat) + ADAM_EPS) + ADAM_WD * w), mn, vn


def _adamw(w, g, m, v, name):
    L, R, C = w.shape
    tr = _row_tile(R)

    def body(w_ref, g_ref, m_ref, v_ref, d_ref, mo_ref, vo_ref):
        d_ref[...], mo_ref[...], vo_ref[...] = _adam_math(w_ref[...], g_ref[...], m_ref[...], v_ref[...])

    spec = pl.BlockSpec((None, tr, C), lambda l, i: (l, i, 0))
    return pl.pallas_call(
        body, name=name, grid=(L, R // tr), in_specs=[spec] * 4, out_specs=[spec] * 3,
        out_shape=[jax.ShapeDtypeStruct((L, R, C), F32)] * 3,
        compiler_params=pltpu.CompilerParams(dimension_semantics=("parallel", "parallel"), vmem_limit_bytes=VMEM_LIMIT),
    )(w, g, m, v)


_VMEM_WHOLE = pl.BlockSpec(memory_space=pltpu.VMEM)


def _matrix_update(gath, w, m, v, name):
    K = w.shape[1]

    def body(g0_ref, g1_ref, w_ref, m_ref, v_ref, go_ref, d_ref, mo_ref, vo_ref):
        for l, gr in enumerate((g0_ref, g1_ref)):
            for k in range(K):
                g = gr[0, k].astype(F32)
                for dev in range(1, 8):
                    g = g + gr[dev, k].astype(F32)
                go_ref[l, k] = g
                d_ref[l, k], mo_ref[l, k], vo_ref[l, k] = _adam_math(w_ref[l, k], g, m_ref[l, k], v_ref[l, k])

    return pl.pallas_call(
        body, name=name, in_specs=[_VMEM_WHOLE] * 5, out_specs=[_VMEM_WHOLE] * 4,
        out_shape=[jax.ShapeDtypeStruct(w.shape, F32)] * 4,
        compiler_params=pltpu.CompilerParams(vmem_limit_bytes=VMEM_LIMIT),
    )(gath[0], gath[1], w, m, v)


VECS = (("pre_norm_g", D), ("post_norm_g", D), ("gm_ln_g", GM_W), ("gm_ln_b", GM_W), ("mla_q_norm_g", QR),
        ("mla_kv_norm_g", KVR), ("lru_conv_b", LRU_W), ("lru_b_a", LRU_W), ("lru_b_x", LRU_W), ("lru_lambda", LRU_W))
VEC_KEY = {"pre_norm_g": "pre_g", "post_norm_g": "post_g", "gm_ln_g": "ln_g", "gm_ln_b": "ln_b", "mla_q_norm_g": "qg",
           "mla_kv_norm_g": "kvg", "lru_conv_b": "conv_b", "lru_b_a": "ba", "lru_b_x": "bx", "lru_lambda": "lam"}
VEC_ROWS, VEC_W, VEC_ROW0, LOSS_ROW = 16, LRU_W, GM_G, 14


def _pack_rows(LG, loss_part):
    per = len(VECS) + 1
    ins = []
    for G in LG:
        ins += [G[VEC_KEY[n]] for n, _ in VECS] + [G["bst"]]
    ins.append(loss_part)

    def body(*refs):
        o_ref = refs[-1]
        o_ref[...] = jnp.zeros_like(o_ref)
        for l in range(DEPTH):
            base = VEC_ROWS * l
            o_ref[pl.ds(base, 8), pl.ds(0, GM_B)] = refs[per * l + len(VECS)][...].T[:8, :]
            for t, (_, width) in enumerate(VECS):
                o_ref[pl.ds(base + VEC_ROW0 + t, 1), pl.ds(0, width)] = refs[per * l + t][...]
        o_ref[pl.ds(LOSS_ROW, 1), pl.ds(0, 128)] = jnp.broadcast_to(refs[-2][...], (1, 128))

    return pl.pallas_call(
        body, name="pack_rows", in_specs=[_VMEM_WHOLE] * len(ins), out_specs=_VMEM_WHOLE,
        out_shape=jax.ShapeDtypeStruct((DEPTH * VEC_ROWS, VEC_W), F32),
    )(*ins)


def _vector_update(gath, W, M, V):
    names = [n for n, _ in VECS] + ["gm_bs"]
    nw = len(names)

    def body(*refs):
        g_ref = refs[0]
        wr, mr, vr = refs[1:1 + nw], refs[1 + nw:1 + 2 * nw], refs[1 + 2 * nw:1 + 3 * nw]
        outs = refs[1 + 3 * nw:]
        s = g_ref[0]
        for dev in range(1, 8):
            s = s + g_ref[dev]
        for t, (_, width) in enumerate(VECS):
            for l in range(DEPTH):
                r = VEC_ROWS * l + VEC_ROW0 + t
                g = s[r:r + 1, :width]
                row = (pl.ds(l, 1), slice(None))
                res = (g,) + _adam_math(wr[t][row], g, mr[t][row], vr[t][row])
                for q in range(4):
                    outs[4 * t + q][row] = res[q]
        t = len(VECS)
        for l in range(DEPTH):
            for k in range(GM_G):
                g = s[VEC_ROWS * l + k:VEC_ROWS * l + k + 1, :GM_B]
                row = (l, pl.ds(k, 1), slice(None))
                res = (g,) + _adam_math(wr[t][row], g, mr[t][row], vr[t][row])
                for q in range(4):
                    outs[4 * t + q][row] = res[q]
        outs[4 * nw][...] = s[LOSS_ROW:LOSS_ROW + 1, :128]

    ws = [W[n] for n in names]
    out_shape = []
    for w in ws:
        out_shape += [jax.ShapeDtypeStruct(w.shape, F32)] * 4
    out_shape.append(jax.ShapeDtypeStruct((1, 128), F32))
    res = pl.pallas_call(
        body, name="vector_update", in_specs=[_VMEM_WHOLE] * (1 + 3 * nw), out_specs=[_VMEM_WHOLE] * (4 * nw + 1),
        out_shape=out_shape, compiler_params=pltpu.CompilerParams(vmem_limit_bytes=VMEM_LIMIT),
    )(gath, *ws, *[M[n] for n in names], *[V[n] for n in names])
    return {n: tuple(res[4 * t:4 * t + 4]) for t, n in enumerate(names)}, res[4 * nw]


SHARDED = ("w_in", "mla_w_uq", "mla_w_ukv", "lru_conv_w", "w_proj_a", "w_proj_b", "w_proj_c", "w_out")
COL_SHARDED = ("w_in", "mla_w_uq", "mla_w_ukv", "lru_conv_w")
SMALL = ("pre_norm_g", "gm_ln_g", "gm_ln_b", "gm_ws", "gm_bs", "mla_q_norm_g", "mla_kv_norm_g", "lru_conv_b",
         "lru_w_a", "lru_b_a", "lru_w_x", "lru_b_x", "lru_lambda", "post_norm_g")
WEIGHTS = ("pre_norm_g", "w_in", "gm_ln_g", "gm_ln_b", "gm_ws", "gm_bs", "mla_q_norm_g", "mla_w_uq",
           "mla_kv_norm_g", "mla_w_ukv", "lru_conv_w", "lru_conv_b", "lru_w_a", "lru_b_a", "lru_w_x", "lru_b_x",
           "lru_lambda", "w_proj_a", "w_proj_b", "w_proj_c", "w_out", "post_norm_g")


GB_KEY = {"w_in": "wp", "mla_w_uq": "wuq", "mla_w_ukv": "wukv", "w_proj_a": "wpa", "w_proj_b": "wpb",
          "w_proj_c": "wpc", "w_out": "wout"}


def _prepare(l, gathered, small, wsb):
    P = {GB_KEY[n]: gathered[n] for n in GB_KEY}
    P["conv_w"] = gathered["lru_conv_w"][l].transpose(1, 0, 2).reshape(CONV_W, LRU_W)
    P["wsb"] = wsb
    row = lambda n: small[n][l][None, :]
    P["pre_g"], P["post_g"] = row("pre_norm_g"), row("post_norm_g")
    P["ln_g"], P["ln_b"] = row("gm_ln_g"), row("gm_ln_b")
    P["ws"] = small["gm_ws"][l]
    P["bst"] = jnp.pad(small["gm_bs"][l].T, ((0, 0), (0, 128 - GM_G)))
    P["qg"], P["kvg"] = row("mla_q_norm_g"), row("mla_kv_norm_g")
    P["conv_b"], P["ba"], P["bx"], P["lam"] = row("lru_conv_b"), row("lru_b_a"), row("lru_b_x"), row("lru_lambda")
    return P


def kernel(x, pre_norm_g, w_in, gm_ln_g, gm_ln_b, gm_ws, gm_bs, mla_q_norm_g, mla_w_uq, mla_kv_norm_g, mla_w_ukv, lru_conv_w, lru_conv_b, lru_w_a, lru_b_a, lru_w_x, lru_b_x, lru_lambda, w_proj_a, w_proj_b, w_proj_c, w_out, post_norm_g, loss_target, m_pre_norm_g, m_w_in, m_gm_ln_g, m_gm_ln_b, m_gm_ws, m_gm_bs, m_mla_q_norm_g, m_mla_w_uq, m_mla_kv_norm_g, m_mla_w_ukv, m_lru_conv_w, m_lru_conv_b, m_lru_w_a, m_lru_b_a, m_lru_w_x, m_lru_b_x, m_lru_lambda, m_w_proj_a, m_w_proj_b, m_w_proj_c, m_w_out, m_post_norm_g, v_pre_norm_g, v_w_in, v_gm_ln_g, v_gm_ln_b, v_gm_ws, v_gm_bs, v_mla_q_norm_g, v_mla_w_uq, v_mla_kv_norm_g, v_mla_w_ukv, v_lru_conv_w, v_lru_conv_b, v_lru_w_a, v_lru_b_a, v_lru_w_x, v_lru_b_x, v_lru_lambda, v_w_proj_a, v_w_proj_b, v_w_proj_c, v_w_out, v_post_norm_g):
    args = dict(locals())
    W = {n: args[n] for n in WEIGHTS}
    M = {n: args["m_" + n] for n in WEIGHTS}
    V = {n: args["v_" + n] for n in WEIGHTS}
    c = lax.axis_index("c")

    srcs = []
    for n in SHARDED:
        blk = lax.dynamic_index_in_dim(W[n], c, 0, keepdims=False)
        if n in TRANSPOSED:
            blk = blk.T
        srcs.append(blk[None] if n == "lru_conv_w" else blk.astype(BF16))
    gathered = dict(zip(SHARDED, _weights_allgather(SHARDED, srcs, "weights_allgather")))
    small = {n: W[n] for n in SMALL}
    wsb = _superblocks(W["lru_w_a"], W["lru_w_x"])
    P = [_prepare(l, gathered, small, wsb) for l in range(DEPTH)]
    tabs = _rope_tables()

    h0 = x[0]
    h1, A0 = _layer_fwd(h0, P[0], 0, tabs)
    h2, A1 = _layer_fwd(h1, P[1], 1, tabs)
    dy, loss_part = _loss_fwd(h2, loss_target[0])
    d1, G1, GB = _layer_bwd(dy, A1, P[1], 1, tabs, None)
    d0, G0, GB = _layer_bwd(d1, A0, P[0], 0, tabs, GB)
    LG = (G0, G1)

    conv_g = jnp.stack([g["conv_w"].reshape(CONV_W, N_CHIPS, LRU_W // N_CHIPS).transpose(1, 0, 2) for g in LG])
    gb = [conv_g if n == "lru_conv_w" else GB[GB_KEY[n]] for n in SHARDED]
    from_sib = _grads_to_sibling(gb, "grads_to_sibling")
    c_arr = jnp.reshape(c, (1,)).astype(jnp.int32)
    pair = [_pair_add(g, rb, c_arr, "pair_add_" + n) for n, g, rb in zip(SHARDED, gb, from_sib)]
    slabs = _chip_scatter(SHARDED, pair, "grads_chip_scatter")
    mine = [_sum_slabs(s, c_arr, "sum_slabs_" + n) for n, s in zip(SHARDED, slabs)]
    both = _reduced_exchange(mine, "reduced_to_sibling")
    grads = {}
    for n, b in zip(SHARDED, both):
        if n in TRANSPOSED:
            b = jnp.swapaxes(b, 1, 2)
        grads[n] = b.reshape(W[n].shape)

    rows = _pack_rows(LG, loss_part)
    mats = []
    for g in LG:
        mats += [g["ws"].astype(BF16), g["wab"][0, :, :, :LRU_BW], g["wab"][1, :, :, :LRU_BW]]
    gath = _allgather([rows] + mats, "small_grads_allgather")
    upd, loss_row = _vector_update(gath[0], W, M, V)
    loss = loss_row[0, 0]
    for k, n in enumerate(("gm_ws", "lru_w_a", "lru_w_x")):
        upd[n] = _matrix_update((gath[1 + k], gath[4 + k]), W[n], M[n], V[n], "update_" + n)

    for n in SHARDED:
        upd[n] = (grads[n],) + tuple(_adamw(W[n], grads[n], M[n], V[n], "adamw_" + n))

    return (loss, d0[None], *[upd[n][0] for n in WEIGHTS], *[upd[n][1] for n in WEIGHTS],
            *[upd[n][2] for n in WEIGHTS], *[upd[n][3] for n in WEIGHTS])
```

```python
import functools
import math

import jax
import jax.numpy as jnp
from jax import lax
from jax.experimental import pallas as pl
from jax.experimental.pallas import tpu as pltpu

F32, BF16 = jnp.float32, jnp.bfloat16
MESH = pl.DeviceIdType.MESH

S, D, DEPTH = 2048, 1024, 2
CHUNK, EPS = 64, 1e-6
GM_W, GM_G, GM_B = 1024, 4, 128
H, NOPE, ROPE, VDIM = 8, 128, 64, 128
QR, KVR = 384, 256
MLA_W = H * VDIM
LRU_W, LRU_NB, LRU_BW, LRU_C, CONV_W = 1280, 16, 80, 8.0, 4
ROPE_THETA = 10000.0
IN_SIZES = (GM_W, GM_W, GM_W, QR, KVR, ROPE, MLA_W, LRU_W, LRU_W, D, D, D)
N_IN = sum(IN_SIZES)
N_CHIPS = 4
ADAM_LR, ADAM_B1, ADAM_B2, ADAM_EPS, ADAM_WD, ADAM_STEP = 0.001, 0.9, 0.999, 1e-08, 0.01, 10

HP = 256
O_U, O_V, O_ZA, O_ZB, O_GA, O_GB, O_GC = 0, 1024, 2048, 3072, 4096, 5120, 6144
O_CKV, O_KR, O_XC, O_ZC, O_CQ = 7168, 7424, 7680, 8960, 10368
NP = 10752
VMEM_LIMIT = 48 * 1024 * 1024


def _tile(dim, target):
    if dim <= target:
        return dim
    t = (target // 128) * 128
    while dim % t:
        t -= 128
    return t


def _sig(x):
    return jax.nn.sigmoid(x)


def _silu(x):
    return x * _sig(x)


def _dsilu(x):
    s = _sig(x)
    return s * (1.0 + x * (1.0 - s))


def _mm(a, b, mode, name, out_dtype=F32, tm=512, tn=512, tk=1024, b_lead=None, out_lead=None):
    b2 = b.shape[1:] if b_lead is not None else b.shape
    if mode == "nn":
        (M, K), (K2, N) = a.shape, b2
    elif mode == "nt":
        (M, K), (N, K2) = a.shape, b2
    else:
        (K, M), (K2, N) = a.shape, b2
    assert K == K2, (name, a.shape, b.shape)
    tm, tn, tk = _tile(M, tm), _tile(N, tn), _tile(K, tk)
    nk = K // tk
    if mode == "tn":
        a_spec = pl.BlockSpec((tk, tm), lambda i, j, k: (k, i))
        lhs_c = 0
    else:
        a_spec = pl.BlockSpec((tm, tk), lambda i, j, k: (i, k))
        lhs_c = 1
    b_blk, b_idx, rhs_c = ((tn, tk), (lambda i, j, k: (j, k)), 1) if mode == "nt" else ((tk, tn), (lambda i, j, k: (k, j)), 0)
    if b_lead is None:
        b_spec = pl.BlockSpec(b_blk, b_idx)
    else:
        b_spec = pl.BlockSpec((None,) + b_blk, functools.partial(lambda i, j, k, f, l: (l,) + f(i, j, k), f=b_idx, l=b_lead))
    dims = (((lhs_c,), (rhs_c,)), ((), ()))
    in_specs, args, aliases = [a_spec, b_spec], [a, b], {}
    if out_lead is None:
        out_spec = pl.BlockSpec((tm, tn), lambda i, j, k: (i, j))
        out_shape = jax.ShapeDtypeStruct((M, N), out_dtype)
    else:
        l_out, n_lead, buf = out_lead
        out_spec = pl.BlockSpec((None, tm, tn), functools.partial(lambda i, j, k, l: (l, i, j), l=l_out))
        out_shape = jax.ShapeDtypeStruct((n_lead, M, N), out_dtype)
        if buf is not None:
            in_specs.append(pl.BlockSpec(memory_space=pl.ANY))
            args.append(buf)
            aliases = {2: 0}

    def body(a_ref, b_ref, *rest):
        o_ref, acc_ref = rest[-2:]
        k = pl.program_id(2)

        @pl.when(k == 0)
        def _():
            acc_ref[...] = jnp.zeros_like(acc_ref)

        acc_ref[...] += lax.dot_general(a_ref[...].astype(BF16), b_ref[...].astype(BF16), dims,
                                        preferred_element_type=F32)

        @pl.when(k == nk - 1)
        def _():
            o_ref[...] = acc_ref[...].astype(o_ref.dtype)

    return pl.pallas_call(
        body, name=name, grid=(M // tm, N // tn, nk),
        in_specs=in_specs, out_specs=out_spec, out_shape=out_shape,
        scratch_shapes=[pltpu.VMEM((tm, tn), F32)], input_output_aliases=aliases,
        compiler_params=pltpu.CompilerParams(dimension_semantics=("parallel", "parallel", "arbitrary"),
                                             vmem_limit_bytes=VMEM_LIMIT),
    )(*args)


def _rows(fn, name, tm, rows, halos=(), fulls=(), outs=(), accs=()):
    n = S // tm
    in_specs, args = [], []
    for arr, w, cb in rows:
        in_specs.append(pl.BlockSpec((tm, w), functools.partial(lambda i, cb: (i, cb), cb=cb)))
        args.append(arr)
    for arr, w, cb, side in halos:
        if side == "prev":
            im = functools.partial(lambda i, cb: (jnp.maximum(i * (tm // 8) - 1, 0), cb), cb=cb)
        else:
            im = functools.partial(lambda i, cb: (jnp.minimum((i + 1) * (tm // 8), S // 8 - 1), cb), cb=cb)
        in_specs.append(pl.BlockSpec((8, w), im))
        args.append(arr)
    for arr in fulls:
        in_specs.append(pl.BlockSpec(arr.shape, functools.partial(lambda i, nd: (0,) * nd, nd=arr.ndim)))
        args.append(arr)
    out_shape, out_specs = [], []
    for o in outs:
        if len(o) == 3:
            out_shape.append(jax.ShapeDtypeStruct((o[0], S), o[1]))
            out_specs.append(pl.BlockSpec((o[0], tm), lambda i: (0, i)))
        else:
            out_shape.append(jax.ShapeDtypeStruct((S, o[0]), o[1]))
            out_specs.append(pl.BlockSpec((tm, o[0]), lambda i: (i, 0)))
    for shp in accs:
        out_shape.append(jax.ShapeDtypeStruct(shp, F32))
        out_specs.append(pl.BlockSpec(shp, functools.partial(lambda i, nd: (0,) * nd, nd=len(shp))))
    nr, nh, nf, no, na = len(rows), len(halos), len(fulls), len(outs), len(accs)

    def body(*refs):
        i = pl.program_id(0)
        ins, orefs = refs[:nr + nh + nf], refs[nr + nh + nf:]
        rv = [r[...] for r in ins[:nr]]
        hv = [r[...] for r in ins[nr:nr + nh]]
        fv = [r[...] for r in ins[nr + nh:]]
        o, a = fn(i, rv, hv, fv)
        assert len(o) == no and len(a) == na, name
        for spec, ref, val in zip(outs, orefs[:no], o):
            ref[...] = (val.T if len(spec) == 3 else val).astype(ref.dtype)
        if na:
            @pl.when(i == 0)
            def _():
                for ref in orefs[no:]:
                    ref[...] = jnp.zeros_like(ref)

            for ref, val in zip(orefs[no:], a):
                ref[...] += val

    res = pl.pallas_call(
        body, name=name, grid=(n,), in_specs=in_specs, out_specs=out_specs, out_shape=out_shape,
        compiler_params=pltpu.CompilerParams(dimension_semantics=("arbitrary",), vmem_limit_bytes=VMEM_LIMIT),
    )(*args)
    return res


def _shift_down(xb, halo, s, row):
    fix = jnp.tile(pltpu.roll(halo, s, 0), (xb.shape[0] // 8, 1))
    return jnp.where(row >= s, pltpu.roll(xb, s, 0), fix)


def _shift_up(xb, halo, s, row):
    tm = xb.shape[0]
    fix = jnp.tile(pltpu.roll(halo, 8 - s, 0), (tm // 8, 1))
    return jnp.where(row < tm - s, pltpu.roll(xb, tm - s, 0), fix)


def _rms(x):
    return lax.rsqrt(jnp.mean(x * x, axis=-1, keepdims=True) + EPS)


def _rms_bwd(dy, x, g):
    r = _rms(x)
    xh = x * r
    dxh = dy * g
    dx = r * (dxh - xh * jnp.mean(dxh * xh, axis=-1, keepdims=True))
    return dx, dy * xh


def _colsum(x):
    return jnp.sum(x, axis=0, keepdims=True)


def _prenorm_fwd(x, g):
    def fn(i, rv, hv, fv):
        (xb,), (gg,) = rv, fv
        return [xb * _rms(xb) * gg], []
    return _rows(fn, "prenorm_fwd", 256, [(x, D, 0)], fulls=[g], outs=[(D, BF16)])[0]


def _gm_mask():
    r = lax.broadcasted_iota(jnp.int32, (GM_B, GM_B), 0) // CHUNK
    c = lax.broadcasted_iota(jnp.int32, (GM_B, GM_B), 1) // CHUNK
    return c <= r


def _gm_norm(v, g, b):
    mu = jnp.mean(v, axis=-1, keepdims=True)
    vc = v - mu
    rs = lax.rsqrt(jnp.mean(vc * vc, axis=-1, keepdims=True) + EPS)
    vh = vc * rs
    return vh, rs, vh * g + b


def _gm_sv(vn, ws, bst):
    mask = _gm_mask()
    gw = GM_W // GM_G
    parts = []
    for g in range(GM_G):
        wm = jnp.where(mask, ws[g], 0.0).astype(BF16)
        parts.append(jnp.dot(wm, vn[:, g * gw:(g + 1) * gw].astype(BF16), preferred_element_type=F32)
                     + bst[:, g:g + 1])
    return jnp.concatenate(parts, axis=1)


def _gmlp_fwd(proj, ln_g, ln_b, ws, bst):
    def fn(i, rv, hv, fv):
        u, v, z = rv
        g, b, w, bt = fv
        _, _, vn = _gm_norm(v, g, b)
        return [u * _gm_sv(vn, w, bt) * _silu(z)], []
    return _rows(fn, "gmlp_fwd", GM_B, [(proj, GM_W, 0), (proj, GM_W, 1), (proj, GM_W, 2)],
                 fulls=[ln_g, ln_b, ws, bst], outs=[(GM_W, BF16)])[0]


def _mla_prep_fwd(proj, qg, kvg):
    def fn(i, rv, hv, fv):
        cq, ckv = rv
        g1, g2 = fv
        return [cq * _rms(cq) * g1, ckv * _rms(ckv) * g2], []
    return _rows(fn, "mla_prep_fwd", 256, [(proj, QR, O_CQ // QR), (proj, KVR, O_CKV // KVR)],
                 fulls=[qg, kvg], outs=[(QR, BF16), (KVR, BF16)])


def _rot(t, cc, sa, sb):
    return t * cc + pltpu.roll(t, 32, 1) * sa + pltpu.roll(t, 96, 1) * sb


def _rot_t(g, cc, sa, sb):
    return g * cc + pltpu.roll(g * sa, 96, 1) + pltpu.roll(g * sb, 32, 1)


def _rope_tables():
    pos = jnp.arange(S, dtype=F32)
    inv_freq = ROPE_THETA ** (-jnp.arange(0, ROPE, 2, dtype=F32) / ROPE)
    ang = pos[:, None] * inv_freq[None, :]
    cos, sin, z = jnp.cos(ang), jnp.sin(ang), jnp.zeros((S, 32), F32)
    cc = jnp.concatenate([cos, cos, z, z], axis=1)
    sa = jnp.concatenate([z, sin, z, z], axis=1)
    sb = jnp.concatenate([-sin, z, z, z], axis=1)
    return cc, sa, sb


ATT_SCALE = 1.0 / math.sqrt(NOPE + ROPE)


def _rope_fwd(q, kv, proj, tabs):
    def fn(i, rv, hv, fv):
        qb, kvb, kr, cc, sa, sb = rv
        krr = _rot(kr, cc, sa, sb)
        qs, ks = [], []
        for h in range(H):
            qs += [qb[:, h * HP:h * HP + 128] * ATT_SCALE, _rot(qb[:, h * HP + 128:(h + 1) * HP], cc, sa, sb) * ATT_SCALE]
            ks += [kvb[:, h * 128:(h + 1) * 128], krr]
        kc = jnp.concatenate(ks, axis=1)
        vv = kvb[:, H * NOPE:]
        return [jnp.concatenate(qs, axis=1), kc, kc, vv, vv], []
    cc, sa, sb = tabs
    return _rows(fn, "rope_fwd", 256,
                 [(q, H * HP, 0), (kv, H * 256, 0), (proj, 128, O_KR // 128), (cc, 128, 0), (sa, 128, 0), (sb, 128, 0)],
                 outs=[(H * HP, BF16), (H * HP, BF16), (H * HP, BF16, "T"), (MLA_W, BF16), (MLA_W, BF16, "T")])


TQ, TC, ATT_NB = 256, 128, 4
_NT = (((1,), (1,)), ((), ()))


def _attn_allowed(i, kc):
    kpos = kc * TC + lax.broadcasted_iota(jnp.int32, (TC, TQ), 0)
    qpos = i * TQ + lax.broadcasted_iota(jnp.int32, (TC, TQ), 1)
    return (kpos // CHUNK) <= (qpos // CHUNK)


def _attn_fwd(qc, kc, vt):
    def body(q_ref, k_ref, vt_ref, o_ref, l_ref):
        i = pl.program_id(1)
        q = q_ref[...]

        def block(sb, carry, masked):
            m, l, acc = carry
            t0s = [pl.multiple_of((sb * ATT_NB + c) * TC, TC) for c in range(ATT_NB)]
            ss = [lax.dot_general(k_ref[pl.ds(t0, TC), :], q, _NT, preferred_element_type=F32) for t0 in t0s]
            if masked:
                ss = [jnp.where(_attn_allowed(i, sb * ATT_NB + c), s, -1e30) for c, s in enumerate(ss)]
            m_new = m
            for s in ss:
                m_new = jnp.maximum(m_new, jnp.max(s, axis=0, keepdims=True))
            alpha = jnp.exp(m - m_new)
            ps = [jnp.exp(s - m_new) for s in ss]
            l = alpha * l
            acc = alpha * acc
            for t0, p in zip(t0s, ps):
                l = l + jnp.sum(p, axis=0, keepdims=True)
                acc = acc + jnp.dot(vt_ref[:, pl.ds(t0, TC)], p.astype(BF16), preferred_element_type=F32)
            return m_new, l, acc

        nsb = (i + 2) // 2
        c = (jnp.full((1, TQ), -1e30, F32), jnp.zeros((1, TQ), F32), jnp.zeros((VDIM, TQ), F32))
        c = lax.fori_loop(0, nsb - 1, lambda sb, c_: block(sb, c_, False), c)
        m, l, acc = block(nsb - 1, c, True)
        o_ref[...] = (acc / l).T
        l_ref[...] = m + jnp.log(l)

    return pl.pallas_call(
        body, name="attn_fwd", grid=(H, S // TQ),
        in_specs=[pl.BlockSpec((TQ, HP), lambda h, i: (i, h)),
                  pl.BlockSpec((S, HP), lambda h, i: (0, h)),
                  pl.BlockSpec((VDIM, S), lambda h, i: (h, 0))],
        out_specs=[pl.BlockSpec((TQ, VDIM), lambda h, i: (i, h)), pl.BlockSpec((None, 1, TQ), lambda h, i: (h, 0, i))],
        out_shape=[jax.ShapeDtypeStruct((S, MLA_W), F32), jax.ShapeDtypeStruct((H, 1, S), F32)],
        compiler_params=pltpu.CompilerParams(dimension_semantics=("parallel", "arbitrary"),
                                             vmem_limit_bytes=VMEM_LIMIT),
    )(qc, kc, vt)


def _gate_mul_fwd(name, val, proj, width, cb):
    def fn(i, rv, hv, fv):
        o, z = rv
        return [o * _silu(z)], []
    return _rows(fn, name, 256, [(val, width, 0), (proj, width, cb)], outs=[(width, BF16)])[0]


def _conv_fwd(proj, w, b):
    def fn(i, rv, hv, fv):
        (xb,), (halo,), (ww, bb) = rv, hv, fv
        halo = jnp.where(i > 0, halo, 0.0)
        row = lax.broadcasted_iota(jnp.int32, xb.shape, 0)
        acc = bb + ww[3:4] * xb
        for s in range(1, CONV_W):
            acc = acc + ww[3 - s:4 - s] * _shift_down(xb, halo, s, row)
        return [acc, acc], []
    return _rows(fn, "conv_fwd", 128, [(proj, LRU_W, O_XC // LRU_W)], halos=[(proj, LRU_W, O_XC // LRU_W, "prev")],
                 fulls=[w, b], outs=[(LRU_W, F32), (LRU_W, BF16)])


def _lru_terms(ga, gx, xc, ba, bx, lam):
    r = _sig(ga + ba)
    ig = _sig(gx + bx)
    sp = jnp.maximum(-lam, 0.0) + jnp.log(1.0 + jnp.exp(-jnp.abs(lam)))
    log_a = -LRU_C * r * sp
    a = jnp.exp(log_a)
    e2 = jnp.exp(2.0 * log_a)
    om = 1.0 - e2
    mult = jnp.sqrt(jnp.maximum(om, 0.0))
    return r, ig, sp, a, e2, om, mult


def _lru_gates_fwd(gates, xc, ba, bx, lam):
    def fn(i, rv, hv, fv):
        ga, gx, x = rv
        r, ig, sp, a, e2, om, mult = _lru_terms(ga, gx, x, *fv)
        return [a, mult * (ig * x)], []
    return _rows(fn, "lru_gates_fwd", 128, [(gates, LRU_W, 0), (gates, LRU_W, 1), (xc, LRU_W, 0)],
                 fulls=[ba, bx, lam], outs=[(LRU_W, F32), (LRU_W, F32)])


SCAN_T, SCAN_CW = 64, 256


def _scan_fwd(a, b):
    def body(a_ref, b_ref, h_ref):
        row = lax.broadcasted_iota(jnp.int32, (SCAN_T, SCAN_CW), 0)

        def step(blk, hc):
            t0 = pl.multiple_of(blk * SCAN_T, SCAN_T)
            A = a_ref[pl.ds(t0, SCAN_T), :]
            B = b_ref[pl.ds(t0, SCAN_T), :]
            d = 1
            while d < SCAN_T:
                keep = row >= d
                A_s = jnp.where(keep, pltpu.roll(A, d, 0), 1.0)
                B_s = jnp.where(keep, pltpu.roll(B, d, 0), 0.0)
                B = A * B_s + B
                A = A * A_s
                d *= 2
            hh = A * hc + B
            h_ref[pl.ds(t0, SCAN_T), :] = hh
            return hh[SCAN_T - 1:SCAN_T, :]

        lax.fori_loop(0, S // SCAN_T, step, jnp.zeros((1, SCAN_CW), F32))

    spec = pl.BlockSpec((S, SCAN_CW), lambda j: (0, j))
    return pl.pallas_call(
        body, name="scan_fwd", grid=(LRU_W // SCAN_CW,), in_specs=[spec, spec], out_specs=spec,
        out_shape=jax.ShapeDtypeStruct((S, LRU_W), F32),
        compiler_params=pltpu.CompilerParams(dimension_semantics=("parallel",), vmem_limit_bytes=VMEM_LIMIT),
    )(a, b)


def _merge_fwd(pa, pb, pc, proj):
    def fn(i, rv, hv, fv):
        a, b, c, ga, gb, gc = rv
        return [_sig(ga) * a + _sig(gb) * b + _sig(gc) * c], []
    return _rows(fn, "merge_fwd", 256,
                 [(pa, D, 0), (pb, D, 0), (pc, D, 0), (proj, D, O_GA // D), (proj, D, O_GB // D), (proj, D, O_GC // D)],
                 outs=[(D, BF16)])[0]


def _post_fwd(x, o2, g):
    def fn(i, rv, hv, fv):
        xb, ob = rv
        return [xb + ob * _rms(ob) * fv[0]], []
    return _rows(fn, "post_fwd", 256, [(x, D, 0), (o2, D, 0)], fulls=[g], outs=[(D, F32)])[0]


SB = 640
BD_TM = 512


def _bd_fwd(xcb, wsb, l):
    def body(x_ref, w_ref, o_ref):
        o_ref[...] = jnp.dot(x_ref[...], w_ref[...], preferred_element_type=F32)

    return pl.pallas_call(
        body, name="lru_gate_mm", grid=(S // BD_TM, 4),
        in_specs=[pl.BlockSpec((BD_TM, SB), lambda i, q: (i, q % 2)),
                  pl.BlockSpec((None, None, SB, SB), lambda i, q: (l, q, 0, 0))],
        out_specs=pl.BlockSpec((BD_TM, SB), lambda i, q: (i, q)),
        out_shape=jax.ShapeDtypeStruct((S, 2 * LRU_W), F32),
        compiler_params=pltpu.CompilerParams(dimension_semantics=("parallel", "parallel"), vmem_limit_bytes=VMEM_LIMIT),
    )(xcb, wsb)


def _bd_dx(dgates, wsb, l):
    def body(d_ref, w_ref, o_ref, acc_ref):
        g = pl.program_id(2)

        @pl.when(g == 0)
        def _():
            acc_ref[...] = jnp.zeros_like(acc_ref)

        acc_ref[...] += lax.dot_general(d_ref[...], w_ref[...], (((1,), (1,)), ((), ())), preferred_element_type=F32)

        @pl.when(g == 1)
        def _():
            o_ref[...] = acc_ref[...]

    return pl.pallas_call(
        body, name="lru_gate_dx", grid=(S // BD_TM, 2, 2),
        in_specs=[pl.BlockSpec((BD_TM, SB), lambda i, s, g: (i, 2 * g + s)),
                  pl.BlockSpec((None, None, SB, SB), lambda i, s, g: (l, 2 * g + s, 0, 0))],
        out_specs=pl.BlockSpec((BD_TM, SB), lambda i, s, g: (i, s)),
        out_shape=jax.ShapeDtypeStruct((S, LRU_W), F32),
        scratch_shapes=[pltpu.VMEM((BD_TM, SB), F32)],
        compiler_params=pltpu.CompilerParams(dimension_semantics=("parallel", "parallel", "arbitrary"),
                                             vmem_limit_bytes=VMEM_LIMIT),
    )(dgates, wsb)


def _bd_dw(xcb, dgates):
    tk = 1024

    def body(x_ref, d_ref, o_ref):
        @pl.when(pl.program_id(1) == 0)
        def _():
            o_ref[...] = jnp.zeros_like(o_ref)

        o_ref[...] += lax.dot_general(x_ref[...], d_ref[...], (((0,), (0,)), ((), ())), preferred_element_type=F32)

    return pl.pallas_call(
        body, name="lru_gate_dw", grid=(4, S // tk),
        in_specs=[pl.BlockSpec((tk, SB), lambda q, k: (k, q % 2)), pl.BlockSpec((tk, SB), lambda q, k: (k, q))],
        out_specs=pl.BlockSpec((None, SB, SB), lambda q, k: (q, 0, 0)),
        out_shape=jax.ShapeDtypeStruct((4, SB, SB), F32),
        compiler_params=pltpu.CompilerParams(dimension_semantics=("parallel", "arbitrary"), vmem_limit_bytes=VMEM_LIMIT),
    )(xcb, dgates)


def _bd_extract(dwsb):
    def body(w_ref, o_ref):
        lane = lax.broadcasted_iota(jnp.int32, (LRU_BW, 128), 1)
        for q in range(4):
            for kk in range(8):
                c0 = LRU_BW * kk
                w0, off = (c0 // 128) * 128, c0 % 128
                rows = pl.ds(LRU_BW * kk, LRU_BW)
                blk = w_ref[q, rows, w0:w0 + 128]
                if off:
                    blk = pltpu.roll(blk, 128 - off, 1)
                    if off + LRU_BW > 128:
                        nxt = pltpu.roll(w_ref[q, rows, w0 + 128:w0 + 256], 128 - off, 1)
                        blk = jnp.where(lane < 128 - off, blk, nxt)
                o_ref[q // 2, 8 * (q % 2) + kk] = blk.astype(BF16)

    return pl.pallas_call(
        body, name="lru_gate_dw_blocks",
        in_specs=[pl.BlockSpec(memory_space=pltpu.VMEM)], out_specs=pl.BlockSpec(memory_space=pltpu.VMEM),
        out_shape=jax.ShapeDtypeStruct((2, LRU_NB, LRU_BW, 128), BF16),
        compiler_params=pltpu.CompilerParams(vmem_limit_bytes=VMEM_LIMIT),
    )(dwsb)


def _layer_fwd(x, P, l, tabs):
    A = {"x": x}
    A["h"] = _prenorm_fwd(x, P["pre_g"])
    proj = A["proj"] = _mm(A["h"], P["wp"], "nt", "in_proj", b_lead=l)
    A["ya"] = _gmlp_fwd(proj, P["ln_g"], P["ln_b"], P["ws"], P["bst"])
    A["cqn"], A["ckvn"] = _mla_prep_fwd(proj, P["qg"], P["kvg"])
    q = _mm(A["cqn"], P["wuq"], "nt", "q_up", b_lead=l)
    kv = _mm(A["ckvn"], P["wukv"], "nt", "kv_up", b_lead=l)
    A["qc"], A["kc"], A["kct"], A["vv"], vt = _rope_fwd(q, kv, proj, tabs)
    A["o"], A["lse"] = _attn_fwd(A["qc"], A["kc"], vt)
    A["yb"] = _gate_mul_fwd("yb_fwd", A["o"], proj, MLA_W, O_ZB // MLA_W)
    A["xc"], A["xcb"] = _conv_fwd(proj, P["conv_w"], P["conv_b"])
    A["gates"] = _bd_fwd(A["xcb"], P["wsb"], l)
    A["a"], bterm = _lru_gates_fwd(A["gates"], A["xc"], P["ba"], P["bx"], P["lam"])
    A["hs"] = _scan_fwd(A["a"], bterm)
    A["yc"] = _gate_mul_fwd("yc_fwd", A["hs"], proj, LRU_W, O_ZC // LRU_W)
    A["pa"] = _mm(A["ya"], P["wpa"], "nn", "proj_a", b_lead=l)
    A["pb"] = _mm(A["yb"], P["wpb"], "nn", "proj_b", b_lead=l)
    A["pc"] = _mm(A["yc"], P["wpc"], "nn", "proj_c", b_lead=l)
    A["merged"] = _merge_fwd(A["pa"], A["pb"], A["pc"], proj)
    A["o2"] = _mm(A["merged"], P["wout"], "nn", "out_proj", b_lead=l)
    return _post_fwd(x, A["o2"], P["post_g"]), A


def _loss_fwd(y, tgt):
    def fn(i, rv, hv, fv):
        yb, tb = rv
        e = yb - tb
        part = 0.5 * jnp.sum(jnp.mean(e * e, axis=-1, keepdims=True), axis=0, keepdims=True)
        return [e * (1.0 / D)], [part]
    return _rows(fn, "loss", 256, [(y, D, 0), (tgt, D, 0)], outs=[(D, F32)], accs=[(1, 1)])


def _post_bwd(dxn, o2, g):
    def fn(i, rv, hv, fv):
        dy, ob = rv
        dx, dg = _rms_bwd(dy, ob, fv[0])
        return [dx], [_colsum(dg)]
    return _rows(fn, "post_bwd", 256, [(dxn, D, 0), (o2, D, 0)], fulls=[g], outs=[(D, BF16)], accs=[(1, D)])


def _merge_bwd(dm, pa, pb, pc, proj):
    def fn(i, rv, hv, fv):
        d, a, b, c, ga, gb, gc = rv
        outs_p, outs_g = [], []
        for p, gg in ((a, ga), (b, gb), (c, gc)):
            s = _sig(gg)
            outs_p.append(d * s)
            outs_g.append(d * p * s * (1.0 - s))
        return outs_p + outs_g, []
    return _rows(fn, "merge_bwd", 128,
                 [(dm, D, 0), (pa, D, 0), (pb, D, 0), (pc, D, 0),
                  (proj, D, O_GA // D), (proj, D, O_GB // D), (proj, D, O_GC // D)],
                 outs=[(D, BF16)] * 6)


def _gmlp_bwd(dya, proj, ln_g, ln_b, ws, bst):
    gw = GM_W // GM_G

    def fn(i, rv, hv, fv):
        dy, u, v, z = rv
        g, b, w, bt = fv
        vh, rs, vn = _gm_norm(v, g, b)
        sv = _gm_sv(vn, w, bt)
        sz = _silu(z)
        du = dy * sv * sz
        dsv = dy * u * sz
        dz = dy * u * sv * _dsilu(z)
        mask = _gm_mask()
        lane = lax.broadcasted_iota(jnp.int32, (GM_B, 128), 1)
        dvn_parts, dws, dbst = [], [], jnp.zeros((GM_B, 128), F32)
        for k in range(GM_G):
            wm = jnp.where(mask, w[k], 0.0).astype(BF16)
            dsk = dsv[:, k * gw:(k + 1) * gw]
            dskb = dsk.astype(BF16)
            dvn_parts.append(lax.dot_general(wm, dskb, (((0,), (0,)), ((), ())), preferred_element_type=F32))
            dwk = lax.dot_general(dskb, vn[:, k * gw:(k + 1) * gw].astype(BF16), (((1,), (1,)), ((), ())),
                                  preferred_element_type=F32)
            dws.append(jnp.where(mask, dwk, 0.0)[None])
            dbst = dbst + jnp.where(lane == k, jnp.sum(dsk, axis=1, keepdims=True), 0.0)
        dvn = jnp.concatenate(dvn_parts, axis=1)
        dvh = dvn * g
        dv = rs * (dvh - jnp.mean(dvh, axis=-1, keepdims=True) - vh * jnp.mean(dvh * vh, axis=-1, keepdims=True))
        return [du, dv, dz], [jnp.concatenate(dws, axis=0), dbst, _colsum(dvn * vh), _colsum(dvn)]
    return _rows(fn, "gmlp_bwd", GM_B, [(dya, GM_W, 0), (proj, GM_W, 0), (proj, GM_W, 1), (proj, GM_W, 2)],
                 fulls=[ln_g, ln_b, ws, bst], outs=[(GM_W, BF16)] * 3,
                 accs=[(GM_G, GM_B, GM_B), (GM_B, 128), (1, GM_W), (1, GM_W)])


def _yb_bwd(dyb, o, proj):
    def fn(i, rv, hv, fv):
        dy, ob, z = rv
        do = dy * _silu(z)
        prod = do * ob
        lane = lax.broadcasted_iota(jnp.int32, (dy.shape[0], 128), 1)
        dl = jnp.zeros((dy.shape[0], 128), F32)
        for h in range(H):
            dl = dl + jnp.where(lane == h, jnp.sum(prod[:, h * VDIM:(h + 1) * VDIM], axis=1, keepdims=True), 0.0)
        return [do, dl, dy * ob * _dsilu(z)], []
    return _rows(fn, "yb_bwd", 256, [(dyb, MLA_W, 0), (o, MLA_W, 0), (proj, MLA_W, O_ZB // MLA_W)],
                 outs=[(MLA_W, BF16), (128, F32, "T"), (MLA_W, BF16)])


def _attn_bwd(qc, kc, kct, vv, do, lse, dlt):
    def body(q_ref, k_ref, kt_ref, v_ref, do_ref, l_ref, d_ref, dq_ref, dk_ref, dv_ref, dqt_ref):
        h, i = pl.program_id(0), pl.program_id(1)

        @pl.when(i == 0)
        def _():
            dk_ref[...] = jnp.zeros_like(dk_ref)
            dv_ref[...] = jnp.zeros_like(dv_ref)

        q = q_ref[...]
        dob = do_ref[...]
        lse = l_ref[...]
        dl = d_ref[pl.ds(h, 1), :]
        dqt_ref[...] = jnp.zeros_like(dqt_ref)

        def block(sb, masked):
            dqt = None
            for c in range(ATT_NB):
                kc_ = sb * ATT_NB + c
                rows = pl.ds(pl.multiple_of(kc_ * TC, TC), TC)
                s = lax.dot_general(k_ref[rows, :], q, _NT, preferred_element_type=F32)
                p = jnp.exp(s - lse)
                if masked:
                    p = jnp.where(_attn_allowed(i, kc_), p, 0.0)
                dp = lax.dot_general(v_ref[rows, :], dob, _NT, preferred_element_type=F32)
                ds = (p * (dp - dl)).astype(BF16)
                dk_ref[rows, :] += jnp.dot(ds, q, preferred_element_type=F32)
                dv_ref[rows, :] += jnp.dot(p.astype(BF16), dob, preferred_element_type=F32)
                part = jnp.dot(kt_ref[:, rows], ds, preferred_element_type=F32)
                dqt = part if dqt is None else dqt + part
            dqt_ref[...] += dqt

        def unmasked(sb, carry):
            block(sb, False)
            return carry

        nsb = (i + 2) // 2
        lax.fori_loop(0, nsb - 1, unmasked, 0)
        block(nsb - 1, True)
        dq_ref[...] = dqt_ref[...].T

    blk = lambda w: pl.BlockSpec((TQ, w), lambda h, i: (i, h))
    head = lambda w: pl.BlockSpec((S, w), lambda h, i: (0, h))
    return pl.pallas_call(
        body, name="attn_bwd", grid=(H, S // TQ),
        in_specs=[blk(HP), head(HP), pl.BlockSpec((HP, S), lambda h, i: (h, 0)), head(VDIM), blk(VDIM),
                  pl.BlockSpec((None, 1, TQ), lambda h, i: (h, 0, i)), pl.BlockSpec((8, TQ), lambda h, i: (0, i))],
        out_specs=[blk(HP), head(HP), head(VDIM)],
        out_shape=[jax.ShapeDtypeStruct((S, H * HP), F32), jax.ShapeDtypeStruct((S, H * HP), F32),
                   jax.ShapeDtypeStruct((S, MLA_W), F32)],
        scratch_shapes=[pltpu.VMEM((HP, TQ), F32)],
        compiler_params=pltpu.CompilerParams(dimension_semantics=("parallel", "arbitrary"),
                                             vmem_limit_bytes=VMEM_LIMIT),
    )(qc, kc, kct, vv, do, lse, dlt)


def _rope_bwd(dqc, dkc, dvv, tabs):
    def fn(i, rv, hv, fv):
        dq, dk, dv, cc, sa, sb = rv
        qs, ks = [], []
        dkr = jnp.zeros((dq.shape[0], 128), F32)
        for h in range(H):
            qs += [dq[:, h * HP:h * HP + 128] * ATT_SCALE, _rot_t(dq[:, h * HP + 128:(h + 1) * HP], cc, sa, sb) * ATT_SCALE]
            ks.append(dk[:, h * HP:h * HP + 128])
            dkr = dkr + dk[:, h * HP + 128:(h + 1) * HP]
        return [jnp.concatenate(qs, axis=1), jnp.concatenate(ks + [dv], axis=1), _rot_t(dkr, cc, sa, sb)], []
    cc, sa, sb = tabs
    return _rows(fn, "rope_bwd", 256,
                 [(dqc, H * HP, 0), (dkc, H * HP, 0), (dvv, MLA_W, 0), (cc, 128, 0), (sa, 128, 0), (sb, 128, 0)],
                 outs=[(H * HP, BF16), (H * 256, BF16), (128, BF16)])


def _mla_prep_bwd(dcqn, dckvn, proj, qg, kvg):
    def fn(i, rv, hv, fv):
        d1, d2, cq, ckv = rv
        g1, g2 = fv
        dx1, dg1 = _rms_bwd(d1, cq, g1)
        dx2, dg2 = _rms_bwd(d2, ckv, g2)
        return [dx1, dx2], [_colsum(dg1), _colsum(dg2)]
    return _rows(fn, "mla_prep_bwd", 256,
                 [(dcqn, QR, 0), (dckvn, KVR, 0), (proj, QR, O_CQ // QR), (proj, KVR, O_CKV // KVR)],
                 fulls=[qg, kvg], outs=[(QR, BF16), (KVR, BF16)], accs=[(1, QR), (1, KVR)])


def _yc_bwd(dyc, hs, proj):
    def fn(i, rv, hv, fv):
        dy, hh, z = rv
        return [dy * _silu(z), dy * hh * _dsilu(z)], []
    return _rows(fn, "yc_bwd", 128, [(dyc, LRU_W, 0), (hs, LRU_W, 0), (proj, LRU_W, O_ZC // LRU_W)],
                 outs=[(LRU_W, F32), (LRU_W, BF16)])


def _scan_bwd(a, hs, dh):
    nblk = S // SCAN_T

    def body(a_ref, h_ref, dh_ref, da_ref, db_ref):
        row = lax.broadcasted_iota(jnp.int32, (SCAN_T, SCAN_CW), 0)

        def step(j, carry):
            gc, ac = carry
            blk = nblk - 1 - j
            t0 = pl.multiple_of(blk * SCAN_T, SCAN_T)
            av = a_ref[pl.ds(t0, SCAN_T), :]
            A = jnp.where(row < SCAN_T - 1, pltpu.roll(av, SCAN_T - 1, 0), ac)
            B = dh_ref[pl.ds(t0, SCAN_T), :]
            d = 1
            while d < SCAN_T:
                keep = row < SCAN_T - d
                A_s = jnp.where(keep, pltpu.roll(A, SCAN_T - d, 0), 1.0)
                B_s = jnp.where(keep, pltpu.roll(B, SCAN_T - d, 0), 0.0)
                B = A * B_s + B
                A = A * A_s
                d *= 2
            g = A * gc + B
            p0 = pl.multiple_of(jnp.maximum(t0 - 8, 0), 8)
            last = jnp.where(blk > 0, h_ref[pl.ds(p0, 8), :][7:8, :], 0.0)
            h_prev = jnp.where(row >= 1, pltpu.roll(h_ref[pl.ds(t0, SCAN_T), :], 1, 0), last)
            da_ref[pl.ds(t0, SCAN_T), :] = g * h_prev
            db_ref[pl.ds(t0, SCAN_T), :] = g
            return g[0:1, :], av[0:1, :]

        z = jnp.zeros((1, SCAN_CW), F32)
        lax.fori_loop(0, nblk, step, (z, z))

    spec = pl.BlockSpec((S, SCAN_CW), lambda j: (0, j))
    return pl.pallas_call(
        body, name="scan_bwd", grid=(LRU_W // SCAN_CW,), in_specs=[spec] * 3, out_specs=[spec] * 2,
        out_shape=[jax.ShapeDtypeStruct((S, LRU_W), F32)] * 2,
        compiler_params=pltpu.CompilerParams(dimension_semantics=("parallel",), vmem_limit_bytes=VMEM_LIMIT),
    )(a, hs, dh)


def _lru_gates_bwd(da, db, gates, xc, ba, bx, lam):
    def fn(i, rv, hv, fv):
        dav, dbv, ga, gx, x = rv
        bav, bxv, lamv = fv
        r, ig, sp, a, e2, om, mult = _lru_terms(ga, gx, x, bav, bxv, lamv)
        dmult = dbv * ig * x
        dig = dbv * mult * x
        dxc1 = dbv * mult * ig
        dlog_a = dav * a + jnp.where(om > 0.0, dmult * (-e2 / mult), 0.0)
        dr = dlog_a * (-LRU_C * sp)
        dga = dr * r * (1.0 - r)
        dgx = dig * ig * (1.0 - ig)
        dlam = _colsum(dlog_a * (-LRU_C * r)) * (-_sig(-lamv))
        return [jnp.concatenate([dga, dgx], axis=1), dxc1], [_colsum(dga), _colsum(dgx), dlam]
    return _rows(fn, "lru_gates_bwd", 128,
                 [(da, LRU_W, 0), (db, LRU_W, 0), (gates, LRU_W, 0), (gates, LRU_W, 1), (xc, LRU_W, 0)],
                 fulls=[ba, bx, lam], outs=[(2 * LRU_W, BF16), (LRU_W, F32)], accs=[(1, LRU_W)] * 3)


def _conv_bwd(dxc1, dxc2, proj, w):
    cb = O_XC // LRU_W

    def fn(i, rv, hv, fv):
        d1, d2, xb = rv
        n1, n2, xprev = hv
        ww = fv[0]
        last = i == S // 128 - 1
        dxc = d1 + d2
        nxt = jnp.where(last, 0.0, n1 + n2)
        xprev = jnp.where(i > 0, xprev, 0.0)
        row = lax.broadcasted_iota(jnp.int32, xb.shape, 0)
        dx = ww[3:4] * dxc
        dws = [None] * CONV_W
        dws[3] = _colsum(dxc * xb)
        for s in range(1, CONV_W):
            dx = dx + ww[3 - s:4 - s] * _shift_up(dxc, nxt, s, row)
            dws[3 - s] = _colsum(dxc * _shift_down(xb, xprev, s, row))
        return [dx], [jnp.concatenate(dws, axis=0), _colsum(dxc)]
    return _rows(fn, "conv_bwd", 128, [(dxc1, LRU_W, 0), (dxc2, LRU_W, 0), (proj, LRU_W, cb)],
                 halos=[(dxc1, LRU_W, 0, "next"), (dxc2, LRU_W, 0, "next"), (proj, LRU_W, cb, "prev")],
                 fulls=[w], outs=[(LRU_W, BF16)], accs=[(CONV_W, LRU_W), (1, LRU_W)])


def _prenorm_bwd(dxn, dh, x, g):
    def fn(i, rv, hv, fv):
        dy, dhh, xb = rv
        dx, dg = _rms_bwd(dhh, xb, fv[0])
        return [dy + dx], [_colsum(dg)]
    return _rows(fn, "prenorm_bwd", 256, [(dxn, D, 0), (dh, D, 0), (x, D, 0)], fulls=[g], outs=[(D, F32)],
                 accs=[(1, D)])


def _layer_bwd(dxn, A, P, l, tabs, GB):
    G = {}
    GB = dict(GB) if GB is not None else {}
    proj = A["proj"]

    def dw(key, a, b, name):
        GB[key] = _mm(a, b, "tn", name, out_dtype=BF16, out_lead=(l, DEPTH, GB.get(key)))

    do2, G["post_g"] = _post_bwd(dxn, A["o2"], P["post_g"])
    dm = _mm(do2, P["wout"], "nt", "out_proj_dx", b_lead=l)
    dw("wout", A["merged"], do2, "out_proj_dw")
    dpa, dpb, dpc, dga, dgb, dgc = _merge_bwd(dm, A["pa"], A["pb"], A["pc"], proj)
    dya = _mm(dpa, P["wpa"], "nt", "proj_a_dx", b_lead=l)
    dw("wpa", A["ya"], dpa, "proj_a_dw")
    dyb = _mm(dpb, P["wpb"], "nt", "proj_b_dx", b_lead=l)
    dw("wpb", A["yb"], dpb, "proj_b_dw")
    dyc = _mm(dpc, P["wpc"], "nt", "proj_c_dx", b_lead=l)
    dw("wpc", A["yc"], dpc, "proj_c_dw")
    du, dv, dza, G["ws"], G["bst"], G["ln_g"], G["ln_b"] = _gmlp_bwd(dya, proj, P["ln_g"], P["ln_b"], P["ws"], P["bst"])
    do, dl, dzb = _yb_bwd(dyb, A["o"], proj)
    dqc, dkc, dvv = _attn_bwd(A["qc"], A["kc"], A["kct"], A["vv"], do, A["lse"], dl)
    dq, dkv, dkr = _rope_bwd(dqc, dkc, dvv, tabs)
    dcqn = _mm(dq, P["wuq"], "nn", "q_up_dx", b_lead=l)
    dw("wuq", dq, A["cqn"], "q_up_dw")
    dckvn = _mm(dkv, P["wukv"], "nn", "kv_up_dx", b_lead=l)
    dw("wukv", dkv, A["ckvn"], "kv_up_dw")
    dcq, dckv, G["qg"], G["kvg"] = _mla_prep_bwd(dcqn, dckvn, proj, P["qg"], P["kvg"])
    dhs, dzc = _yc_bwd(dyc, A["hs"], proj)
    da, db = _scan_bwd(A["a"], A["hs"], dhs)
    dgates, dxc1, G["ba"], G["bx"], G["lam"] = _lru_gates_bwd(da, db, A["gates"], A["xc"], P["ba"], P["bx"], P["lam"])
    dxc2 = _bd_dx(dgates, P["wsb"], l)
    G["wab"] = _bd_extract(_bd_dw(A["xcb"], dgates))
    dxcc, G["conv_w"], G["conv_b"] = _conv_bwd(dxc1, dxc2, proj, P["conv_w"])
    zpad = jnp.zeros((S, 128), BF16)
    dproj = jnp.concatenate([du, dv, dza, dzb, dga, dgb, dgc, dckv, dkr, zpad, dxcc, dzc, zpad, dcq], axis=1)
    dh = _mm(dproj, P["wp"], "nn", "in_proj_dx", b_lead=l)
    dw("wp", dproj, A["h"], "in_proj_dw")
    dx, G["pre_g"] = _prenorm_bwd(dxn, dh, A["x"], P["pre_g"])
    return dx, G, GB


_ORIG_OFF = [0]
for _s in IN_SIZES:
    _ORIG_OFF.append(_ORIG_OFF[-1] + _s)
_PAD_OFF = {0: O_U, 1: O_V, 2: O_ZA, 3: O_CQ, 4: O_CKV, 5: O_KR, 6: O_ZB, 7: O_XC, 8: O_ZC, 9: O_GA, 10: O_GB, 11: O_GC}
SHARD_IN = N_IN // N_CHIPS


def _pieces_w_in(j):
    lo, hi = SHARD_IN * j, SHARD_IN * (j + 1)
    out = []
    for k in range(len(IN_SIZES)):
        a, b = max(lo, _ORIG_OFF[k]), min(hi, _ORIG_OFF[k + 1])
        if a < b:
            out.append((a - lo, _PAD_OFF[k] + a - _ORIG_OFF[k], b - a))
    return out


def _pieces_uq(j):
    return [(192 * hh, HP * (2 * j + hh), NOPE + ROPE) for hh in range(2)]


def _pieces_ukv(j):
    out = []
    for hh in range(2):
        h = 2 * j + hh
        out += [(256 * hh, NOPE * h, NOPE), (256 * hh + NOPE, H * NOPE + VDIM * h, VDIM)]
    return out


def _pieces_rows(r):
    return lambda j: [(0, r * j, r)]


LAYOUT = {
    "w_in": (SHARD_IN, NP, _pieces_w_in),
    "mla_w_uq": (2 * (NOPE + ROPE), H * HP, _pieces_uq),
    "mla_w_ukv": (2 * (NOPE + VDIM), 2 * H * 128, _pieces_ukv),
    "lru_conv_w": (1, N_CHIPS, _pieces_rows(1)),
    "w_proj_a": (GM_W // N_CHIPS, GM_W, _pieces_rows(GM_W // N_CHIPS)),
    "w_proj_b": (MLA_W // N_CHIPS, MLA_W, _pieces_rows(MLA_W // N_CHIPS)),
    "w_proj_c": (LRU_W // N_CHIPS, LRU_W, _pieces_rows(LRU_W // N_CHIPS)),
    "w_out": (D // N_CHIPS, D, _pieces_rows(D // N_CHIPS)),
}
TRANSPOSED = ("w_in", "mla_w_uq", "mla_w_ukv")


def _superblocks(w_a, w_x):
    w6 = jnp.stack([w_a, w_x], axis=1).reshape(DEPTH, 4, 8, LRU_BW, LRU_BW).astype(BF16)
    bands = [jnp.pad(w6[:, :, k], ((0, 0), (0, 0), (0, 0), (LRU_BW * k, SB - LRU_BW * (k + 1)))) for k in range(8)]
    return jnp.concatenate(bands, axis=2)


_HBM = pl.BlockSpec(memory_space=pltpu.HBM)


def _position():
    return lax.axis_index("x"), lax.axis_index("y"), lax.axis_index("c")


def _allgather(blocks, name):
    n = len(blocks)

    def body(*refs):
        ins, outs = refs[:n], refs[n:2 * n]
        send, recv, lsem = refs[2 * n:]
        x, y, c = _position()
        me, sib = (x, y, c), (x, y, 1 - c)
        chips = [(1 - x, y), (x, 1 - y), (1 - x, 1 - y)]

        def cp(k, a, block, to, src=None):
            dst = outs[a].at[4 * block[0] + 2 * block[1] + block[2]]
            return pltpu.make_async_remote_copy(src_ref=dst if src is None else src, dst_ref=dst,
                                                send_sem=send.at[7 * a + k], recv_sem=recv.at[7 * a + k],
                                                device_id=to, device_id_type=MESH)

        mine = [pltpu.make_async_copy(ins[a], outs[a].at[4 * x + 2 * y + c], lsem.at[a]) for a in range(n)]
        for m in mine:
            m.start()
        first = []
        for a in range(n):
            first.append(cp(0, a, me, sib, src=ins[a]))
            first += [cp(1 + j, a, me, (*chip, c), src=ins[a]) for j, chip in enumerate(chips)]
        for f in first:
            f.start()
        passed = []
        for j, chip in enumerate(chips):
            for a in range(n):
                cp(1 + j, a, (*chip, c), me).wait_recv()
                p = cp(4 + j, a, (*chip, c), sib)
                p.start()
                passed.append(p)
        for a in range(n):
            cp(0, a, sib, me).wait_recv()
            for j, chip in enumerate(chips):
                cp(4 + j, a, (*chip, 1 - c), me).wait_recv()
        for f in first + passed:
            f.wait_send()
        for m in mine:
            m.wait()

    return pl.pallas_call(
        body, name=name,
        out_shape=[jax.ShapeDtypeStruct((8,) + b.shape, b.dtype) for b in blocks],
        in_specs=[_HBM] * n, out_specs=[_HBM] * n,
        scratch_shapes=[pltpu.SemaphoreType.DMA((7 * n,)), pltpu.SemaphoreType.DMA((7 * n,)),
                        pltpu.SemaphoreType.DMA((n,))],
    )(*blocks)


_REL = (2, 1, 3)


def _weights_allgather(names, srcs, name):
    n = len(srcs)
    lay = [LAYOUT[nm] for nm in names]
    zeros = [jnp.zeros((DEPTH, lay[a][1]) + srcs[a].shape[1:], srcs[a].dtype) for a in range(n)]

    def body(*refs):
        ins, outs = refs[:n], refs[2 * n:3 * n]
        send, recv, lsem = refs[3 * n:]
        x, y, c = _position()
        j = 2 * x + y
        sib = (x, y, 1 - c)
        chips = [(1 - x, y), (x, 1 - y), (1 - x, 1 - y)]

        def flow(a, k, jsrc, to, from_src):
            cps = []
            for s0, d0, nr in lay[a][2](jsrc):
                dst = outs[a].at[c, pl.ds(d0, nr)]
                src = ins[a].at[pl.ds(s0, nr)] if from_src else dst
                cps.append(pltpu.make_async_remote_copy(src_ref=src, dst_ref=dst, send_sem=send.at[7 * a + k],
                                                        recv_sem=recv.at[7 * a + k], device_id=to, device_id_type=MESH))
            return cps

        def whole(a, k):
            return pltpu.make_async_remote_copy(src_ref=ins[a], dst_ref=outs[a].at[0, pl.ds(0, lay[a][0])],
                                                send_sem=send.at[7 * a + k], recv_sem=recv.at[7 * a + k],
                                                device_id=sib, device_id_type=MESH)

        for j0 in range(N_CHIPS):
            @pl.when(j == j0)
            def _(j0=j0):
                for a in range(n):
                    for s0, d0, nr in lay[a][2](j0):
                        pltpu.make_async_copy(ins[a].at[pl.ds(s0, nr)], outs[a].at[c, pl.ds(d0, nr)], lsem.at[a]).start()
                for a in range(n):
                    for cp in flow(a, 0, j0, sib, True):
                        cp.start()
                    for k, chip in enumerate(chips):
                        for cp in flow(a, 1 + k, j0, (*chip, c), True):
                            cp.start()
                for k in range(3):
                    for a in range(n):
                        whole(a, 1 + k).wait_recv()
                        for cp in flow(a, 4 + k, j0 ^ _REL[k], sib, False):
                            cp.start()

        for a in range(n):
            whole(a, 0).wait_recv()
            for k in range(3):
                whole(a, 4 + k).wait_recv()
        for a in range(n):
            for k in range(7):
                whole(a, k).wait_send()
            pltpu.make_async_copy(ins[a], outs[a].at[0, pl.ds(0, lay[a][0])], lsem.at[a]).wait()

    return pl.pallas_call(
        body, name=name,
        out_shape=[jax.ShapeDtypeStruct(z.shape, z.dtype) for z in zeros],
        in_specs=[_HBM] * (2 * n), out_specs=[_HBM] * n,
        input_output_aliases={n + a: a for a in range(n)},
        scratch_shapes=[pltpu.SemaphoreType.DMA((7 * n,)), pltpu.SemaphoreType.DMA((7 * n,)),
                        pltpu.SemaphoreType.DMA((n,))],
    )(*srcs, *zeros)


def _grads_to_sibling(gb, name):
    n = len(gb)

    def body(*refs):
        ins, outs = refs[:n], refs[n:2 * n]
        send, recv = refs[2 * n:]
        x, y, c = _position()
        cps = [pltpu.make_async_remote_copy(src_ref=ins[a].at[1 - c], dst_ref=outs[a], send_sem=send.at[a],
                                            recv_sem=recv.at[a], device_id=(x, y, 1 - c), device_id_type=MESH)
               for a in range(n)]
        for cp in cps:
            cp.start()
        for cp in cps:
            cp.wait()

    return pl.pallas_call(
        body, name=name,
        out_shape=[jax.ShapeDtypeStruct(g.shape[1:], g.dtype) for g in gb],
        in_specs=[_HBM] * n, out_specs=[_HBM] * n,
        scratch_shapes=[pltpu.SemaphoreType.DMA((n,)), pltpu.SemaphoreType.DMA((n,))],
    )(*gb)


def _chip_scatter(names, parts, name):
    n = len(parts)
    lay = [LAYOUT[nm] for nm in names]

    def body(*refs):
        ins, outs = refs[:n], refs[n:2 * n]
        send, recv, lsem = refs[2 * n:]
        x, y, c = _position()
        j = 2 * x + y
        chips = [(1 - x, y), (x, 1 - y), (1 - x, 1 - y)]

        def whole(a):
            return outs[a].at[0, pl.ds(0, lay[a][0])]

        for j0 in range(N_CHIPS):
            @pl.when(j == j0)
            def _(j0=j0):
                for a in range(n):
                    for s0, d0, nr in lay[a][2](j0):
                        pltpu.make_async_copy(ins[a].at[pl.ds(d0, nr)], outs[a].at[j0, pl.ds(s0, nr)], lsem.at[a]).start()
                    for k, chip in enumerate(chips):
                        for s0, d0, nr in lay[a][2](j0 ^ _REL[k]):
                            pltpu.make_async_remote_copy(
                                src_ref=ins[a].at[pl.ds(d0, nr)], dst_ref=outs[a].at[j0, pl.ds(s0, nr)],
                                send_sem=send.at[3 * a + k], recv_sem=recv.at[3 * a + k],
                                device_id=(*chip, c), device_id_type=MESH).start()

        for a in range(n):
            for k in range(3):
                pltpu.make_async_remote_copy(src_ref=whole(a), dst_ref=whole(a), send_sem=send.at[3 * a + k],
                                             recv_sem=recv.at[3 * a + k], device_id=(x, y, c), device_id_type=MESH).wait()
            pltpu.make_async_copy(whole(a), whole(a), lsem.at[a]).wait()

    return pl.pallas_call(
        body, name=name,
        out_shape=[jax.ShapeDtypeStruct((N_CHIPS, lay[a][0]) + parts[a].shape[1:], parts[a].dtype) for a in range(n)],
        in_specs=[_HBM] * n, out_specs=[_HBM] * n,
        scratch_shapes=[pltpu.SemaphoreType.DMA((3 * n,)), pltpu.SemaphoreType.DMA((3 * n,)),
                        pltpu.SemaphoreType.DMA((n,))],
    )(*parts)


def _reduced_exchange(bufs, name):
    n = len(bufs)

    def body(*refs):
        ins, outs = refs[:n], refs[n:2 * n]
        send, recv = refs[2 * n:]
        x, y, c = _position()
        cps = [pltpu.make_async_remote_copy(src_ref=outs[a].at[c], dst_ref=outs[a].at[c], send_sem=send.at[a],
                                            recv_sem=recv.at[a], device_id=(x, y, 1 - c), device_id_type=MESH)
               for a in range(n)]
        for cp in cps:
            cp.start()
        for cp in cps:
            cp.wait()

    return pl.pallas_call(
        body, name=name,
        out_shape=[jax.ShapeDtypeStruct(b.shape, b.dtype) for b in bufs],
        in_specs=[_HBM] * n, out_specs=[_HBM] * n, input_output_aliases={a: a for a in range(n)},
        scratch_shapes=[pltpu.SemaphoreType.DMA((n,)), pltpu.SemaphoreType.DMA((n,))],
    )(*bufs)


def _row_tile(r):
    for t in (256, 128, 64, 32, 16, 8):
        if r % t == 0 and r > t:
            return t
    return r


def _pair_add(g, rb, c_arr, name):
    R, rest = g.shape[1], g.shape[2:]
    tr = _row_tile(R)
    z = (0,) * len(rest)

    def body(c_ref, g_ref, r_ref, o_ref):
        o_ref[...] = (g_ref[...].astype(F32) + r_ref[...].astype(F32)).astype(o_ref.dtype)

    return pl.pallas_call(
        body, name=name,
        grid_spec=pltpu.PrefetchScalarGridSpec(
            num_scalar_prefetch=1, grid=(R // tr,),
            in_specs=[pl.BlockSpec((None, tr) + rest, lambda i, c_ref: (c_ref[0], i) + z),
                      pl.BlockSpec((tr,) + rest, lambda i, c_ref: (i,) + z)],
            out_specs=pl.BlockSpec((tr,) + rest, lambda i, c_ref: (i,) + z)),
        out_shape=jax.ShapeDtypeStruct((R,) + rest, BF16),
        compiler_params=pltpu.CompilerParams(dimension_semantics=("parallel",), vmem_limit_bytes=VMEM_LIMIT),
    )(c_arr, g, rb)


def _sum_slabs(rb, c_arr, name):
    n, R, rest = rb.shape[0], rb.shape[1], rb.shape[2:]
    tr = _row_tile(R)
    z = (0,) * len(rest)

    def body(c_ref, r_ref, o_ref):
        acc = r_ref[0].astype(F32)
        for k in range(1, n):
            acc = acc + r_ref[k].astype(F32)
        o_ref[...] = acc

    if R // tr > 64 and len(rest) == 1 and rest[0] % 256 == 0:
        grid = (rest[0] // 256,)
        in_spec = pl.BlockSpec((n, R, 256), lambda i, c_ref: (0, 0, i))
        out_spec = pl.BlockSpec((None, R, 256), lambda i, c_ref: (c_ref[0], 0, i))
    else:
        grid = (R // tr,)
        in_spec = pl.BlockSpec((n, tr) + rest, lambda i, c_ref: (0, i) + z)
        out_spec = pl.BlockSpec((None, tr) + rest, lambda i, c_ref: (c_ref[0], i) + z)
    return pl.pallas_call(
        body, name=name,
        grid_spec=pltpu.PrefetchScalarGridSpec(num_scalar_prefetch=1, grid=grid, in_specs=[in_spec], out_specs=out_spec),
        out_shape=jax.ShapeDtypeStruct((DEPTH, R) + rest, F32),
        compiler_params=pltpu.CompilerParams(dimension_semantics=("parallel",), vmem_limit_bytes=VMEM_LIMIT),
    )(c_arr, rb)


def _adam_math(w, g, m, v):
    mn = ADAM_B1 * m + (1.0 - ADAM_B1) * g
    vn = ADAM_B2 * v + (1.0 - ADAM_B2) * (g * g)
    m_hat = mn / (1.0 - ADAM_B1 ** ADAM_STEP)
    v_hat = vn / (1.0 - ADAM_B2 ** ADAM_STEP)
    return -ADAM_LR * (m_hat / (jnp.sqrt(v_hat) + ADAM_EPS) + ADAM_WD * w), mn, vn


def _adamw(w, g, m, v, name):
    L, R, C = w.shape
    tr = _row_tile(R)

    def body(w_ref, g_ref, m_ref, v_ref, d_ref, mo_ref, vo_ref):
        d_ref[...], mo_ref[...], vo_ref[...] = _adam_math(w_ref[...], g_ref[...], m_ref[...], v_ref[...])

    if R // tr > 64 and C % 128 == 0:
        spec, grid = pl.BlockSpec((None, R, 128), lambda l, i: (l, 0, i)), (L, C // 128)
    else:
        spec, grid = pl.BlockSpec((None, tr, C), lambda l, i: (l, i, 0)), (L, R // tr)
    return pl.pallas_call(
        body, name=name, grid=grid, in_specs=[spec] * 4, out_specs=[spec] * 3,
        out_shape=[jax.ShapeDtypeStruct((L, R, C), F32)] * 3,
        compiler_params=pltpu.CompilerParams(dimension_semantics=("parallel", "parallel"), vmem_limit_bytes=VMEM_LIMIT),
    )(w, g, m, v)


_VMEM_WHOLE = pl.BlockSpec(memory_space=pltpu.VMEM)


def _matrix_update(gath, w, m, v, name):
    K = w.shape[1]

    def body(g0_ref, g1_ref, w_ref, m_ref, v_ref, go_ref, d_ref, mo_ref, vo_ref):
        for l, gr in enumerate((g0_ref, g1_ref)):
            for k in range(K):
                g = gr[0, k].astype(F32)
                for dev in range(1, 8):
                    g = g + gr[dev, k].astype(F32)
                go_ref[l, k] = g
                d_ref[l, k], mo_ref[l, k], vo_ref[l, k] = _adam_math(w_ref[l, k], g, m_ref[l, k], v_ref[l, k])

    return pl.pallas_call(
        body, name=name, in_specs=[_VMEM_WHOLE] * 5, out_specs=[_VMEM_WHOLE] * 4,
        out_shape=[jax.ShapeDtypeStruct(w.shape, F32)] * 4,
        compiler_params=pltpu.CompilerParams(vmem_limit_bytes=VMEM_LIMIT),
    )(gath[0], gath[1], w, m, v)


VECS = (("pre_norm_g", D), ("post_norm_g", D), ("gm_ln_g", GM_W), ("gm_ln_b", GM_W), ("mla_q_norm_g", QR),
        ("mla_kv_norm_g", KVR), ("lru_conv_b", LRU_W), ("lru_b_a", LRU_W), ("lru_b_x", LRU_W), ("lru_lambda", LRU_W))
VEC_KEY = {"pre_norm_g": "pre_g", "post_norm_g": "post_g", "gm_ln_g": "ln_g", "gm_ln_b": "ln_b", "mla_q_norm_g": "qg",
           "mla_kv_norm_g": "kvg", "lru_conv_b": "conv_b", "lru_b_a": "ba", "lru_b_x": "bx", "lru_lambda": "lam"}
VEC_ROWS, VEC_W, VEC_ROW0, LOSS_ROW = 16, LRU_W, GM_G, 14


def _pack_rows(LG, loss_part):
    per = len(VECS) + 1
    ins = []
    for G in LG:
        ins += [G[VEC_KEY[n]] for n, _ in VECS] + [G["bst"]]
    ins.append(loss_part)

    def body(*refs):
        o_ref = refs[-1]
        o_ref[...] = jnp.zeros_like(o_ref)
        for l in range(DEPTH):
            base = VEC_ROWS * l
            o_ref[pl.ds(base, 8), pl.ds(0, GM_B)] = refs[per * l + len(VECS)][...].T[:8, :]
            for t, (_, width) in enumerate(VECS):
                o_ref[pl.ds(base + VEC_ROW0 + t, 1), pl.ds(0, width)] = refs[per * l + t][...]
        o_ref[pl.ds(LOSS_ROW, 1), pl.ds(0, 128)] = jnp.broadcast_to(refs[-2][...], (1, 128))

    return pl.pallas_call(
        body, name="pack_rows", in_specs=[_VMEM_WHOLE] * len(ins), out_specs=_VMEM_WHOLE,
        out_shape=jax.ShapeDtypeStruct((DEPTH * VEC_ROWS, VEC_W), F32),
    )(*ins)


def _vector_update(gath, W, M, V):
    names = [n for n, _ in VECS] + ["gm_bs"]
    nw = len(names)

    def body(*refs):
        g_ref = refs[0]
        wr, mr, vr = refs[1:1 + nw], refs[1 + nw:1 + 2 * nw], refs[1 + 2 * nw:1 + 3 * nw]
        outs = refs[1 + 3 * nw:]
        s = g_ref[0]
        for dev in range(1, 8):
            s = s + g_ref[dev]
        for t, (_, width) in enumerate(VECS):
            for l in range(DEPTH):
                r = VEC_ROWS * l + VEC_ROW0 + t
                g = s[r:r + 1, :width]
                row = (pl.ds(l, 1), slice(None))
                res = (g,) + _adam_math(wr[t][row], g, mr[t][row], vr[t][row])
                for q in range(4):
                    outs[4 * t + q][row] = res[q]
        t = len(VECS)
        for l in range(DEPTH):
            for k in range(GM_G):
                g = s[VEC_ROWS * l + k:VEC_ROWS * l + k + 1, :GM_B]
                row = (l, pl.ds(k, 1), slice(None))
                res = (g,) + _adam_math(wr[t][row], g, mr[t][row], vr[t][row])
                for q in range(4):
                    outs[4 * t + q][row] = res[q]
        outs[4 * nw][...] = s[LOSS_ROW:LOSS_ROW + 1, :128]

    ws = [W[n] for n in names]
    out_shape = []
    for w in ws:
        out_shape += [jax.ShapeDtypeStruct(w.shape, F32)] * 4
    out_shape.append(jax.ShapeDtypeStruct((1, 128), F32))
    res = pl.pallas_call(
        body, name="vector_update", in_specs=[_VMEM_WHOLE] * (1 + 3 * nw), out_specs=[_VMEM_WHOLE] * (4 * nw + 1),
        out_shape=out_shape, compiler_params=pltpu.CompilerParams(vmem_limit_bytes=VMEM_LIMIT),
    )(gath, *ws, *[M[n] for n in names], *[V[n] for n in names])
    return {n: tuple(res[4 * t:4 * t + 4]) for t, n in enumerate(names)}, res[4 * nw]


SHARDED = ("w_in", "mla_w_uq", "mla_w_ukv", "lru_conv_w", "w_proj_a", "w_proj_b", "w_proj_c", "w_out")
COL_SHARDED = ("w_in", "mla_w_uq", "mla_w_ukv", "lru_conv_w")
SMALL = ("pre_norm_g", "gm_ln_g", "gm_ln_b", "gm_ws", "gm_bs", "mla_q_norm_g", "mla_kv_norm_g", "lru_conv_b",
         "lru_w_a", "lru_b_a", "lru_w_x", "lru_b_x", "lru_lambda", "post_norm_g")
WEIGHTS = ("pre_norm_g", "w_in", "gm_ln_g", "gm_ln_b", "gm_ws", "gm_bs", "mla_q_norm_g", "mla_w_uq",
           "mla_kv_norm_g", "mla_w_ukv", "lru_conv_w", "lru_conv_b", "lru_w_a", "lru_b_a", "lru_w_x", "lru_b_x",
           "lru_lambda", "w_proj_a", "w_proj_b", "w_proj_c", "w_out", "post_norm_g")


GB_KEY = {"w_in": "wp", "mla_w_uq": "wuq", "mla_w_ukv": "wukv", "w_proj_a": "wpa", "w_proj_b": "wpb",
          "w_proj_c": "wpc", "w_out": "wout"}


def _prepare(l, gathered, small, wsb):
    P = {GB_KEY[n]: gathered[n] for n in GB_KEY}
    P["conv_w"] = gathered["lru_conv_w"][l].transpose(1, 0, 2).reshape(CONV_W, LRU_W)
    P["wsb"] = wsb
    row = lambda n: small[n][l][None, :]
    P["pre_g"], P["post_g"] = row("pre_norm_g"), row("post_norm_g")
    P["ln_g"], P["ln_b"] = row("gm_ln_g"), row("gm_ln_b")
    P["ws"] = small["gm_ws"][l]
    P["bst"] = jnp.pad(small["gm_bs"][l].T, ((0, 0), (0, 128 - GM_G)))
    P["qg"], P["kvg"] = row("mla_q_norm_g"), row("mla_kv_norm_g")
    P["conv_b"], P["ba"], P["bx"], P["lam"] = row("lru_conv_b"), row("lru_b_a"), row("lru_b_x"), row("lru_lambda")
    return P


def kernel(x, pre_norm_g, w_in, gm_ln_g, gm_ln_b, gm_ws, gm_bs, mla_q_norm_g, mla_w_uq, mla_kv_norm_g, mla_w_ukv, lru_conv_w, lru_conv_b, lru_w_a, lru_b_a, lru_w_x, lru_b_x, lru_lambda, w_proj_a, w_proj_b, w_proj_c, w_out, post_norm_g, loss_target, m_pre_norm_g, m_w_in, m_gm_ln_g, m_gm_ln_b, m_gm_ws, m_gm_bs, m_mla_q_norm_g, m_mla_w_uq, m_mla_kv_norm_g, m_mla_w_ukv, m_lru_conv_w, m_lru_conv_b, m_lru_w_a, m_lru_b_a, m_lru_w_x, m_lru_b_x, m_lru_lambda, m_w_proj_a, m_w_proj_b, m_w_proj_c, m_w_out, m_post_norm_g, v_pre_norm_g, v_w_in, v_gm_ln_g, v_gm_ln_b, v_gm_ws, v_gm_bs, v_mla_q_norm_g, v_mla_w_uq, v_mla_kv_norm_g, v_mla_w_ukv, v_lru_conv_w, v_lru_conv_b, v_lru_w_a, v_lru_b_a, v_lru_w_x, v_lru_b_x, v_lru_lambda, v_w_proj_a, v_w_proj_b, v_w_proj_c, v_w_out, v_post_norm_g):
    args = dict(locals())
    W = {n: args[n] for n in WEIGHTS}
    M = {n: args["m_" + n] for n in WEIGHTS}
    V = {n: args["v_" + n] for n in WEIGHTS}
    c = lax.axis_index("c")

    srcs = []
    for n in SHARDED:
        blk = lax.dynamic_index_in_dim(W[n], c, 0, keepdims=False)
        if n in TRANSPOSED:
            blk = blk.T
        srcs.append(blk[None] if n == "lru_conv_w" else blk.astype(BF16))
    gathered = dict(zip(SHARDED, _weights_allgather(SHARDED, srcs, "weights_allgather")))
    small = {n: W[n] for n in SMALL}
    wsb = _superblocks(W["lru_w_a"], W["lru_w_x"])
    P = [_prepare(l, gathered, small, wsb) for l in range(DEPTH)]
    tabs = _rope_tables()

    h0 = x[0]
    h1, A0 = _layer_fwd(h0, P[0], 0, tabs)
    h2, A1 = _layer_fwd(h1, P[1], 1, tabs)
    dy, loss_part = _loss_fwd(h2, loss_target[0])
    d1, G1, GB = _layer_bwd(dy, A1, P[1], 1, tabs, None)
    d0, G0, GB = _layer_bwd(d1, A0, P[0], 0, tabs, GB)
    LG = (G0, G1)

    conv_g = jnp.stack([g["conv_w"].reshape(CONV_W, N_CHIPS, LRU_W // N_CHIPS).transpose(1, 0, 2) for g in LG])
    gb = [conv_g if n == "lru_conv_w" else GB[GB_KEY[n]] for n in SHARDED]
    from_sib = _grads_to_sibling(gb, "grads_to_sibling")
    c_arr = jnp.reshape(c, (1,)).astype(jnp.int32)
    pair = [_pair_add(g, rb, c_arr, "pair_add_" + n) for n, g, rb in zip(SHARDED, gb, from_sib)]
    slabs = _chip_scatter(SHARDED, pair, "grads_chip_scatter")
    mine = [_sum_slabs(s, c_arr, "sum_slabs_" + n) for n, s in zip(SHARDED, slabs)]
    both = _reduced_exchange(mine, "reduced_to_sibling")
    grads = {}
    for n, b in zip(SHARDED, both):
        if n in TRANSPOSED and n != "w_in":
            b = jnp.swapaxes(b, 1, 2)
        grads[n] = b if n == "w_in" else b.reshape(W[n].shape)

    rows = _pack_rows(LG, loss_part)
    mats = []
    for g in LG:
        mats += [g["ws"].astype(BF16), g["wab"][0, :, :, :LRU_BW], g["wab"][1, :, :, :LRU_BW]]
    gath = _allgather([rows] + mats, "small_grads_allgather")
    upd, loss_row = _vector_update(gath[0], W, M, V)
    loss = loss_row[0, 0]
    for k, n in enumerate(("gm_ws", "lru_w_a", "lru_w_x")):
        upd[n] = _matrix_update((gath[1 + k], gath[4 + k]), W[n], M[n], V[n], "update_" + n)

    for n in SHARDED:
        if n == "w_in":
            tr = lambda a: jnp.swapaxes(a, 1, 2)
            res = _adamw(tr(W[n]), grads[n], tr(M[n]), tr(V[n]), "adamw_" + n)
            upd[n] = tuple(tr(a) for a in (grads[n],) + tuple(res))
        else:
            upd[n] = (grads[n],) + tuple(_adamw(W[n], grads[n], M[n], V[n], "adamw_" + n))

    return (loss, d0[None], *[upd[n][0] for n in WEIGHTS], *[upd[n][1] for n in WEIGHTS],
            *[upd[n][2] for n in WEIGHTS], *[upd[n][3] for n in WEIGHTS])
```

```python
import functools
import math

import jax
import jax.numpy as jnp
from jax import lax
from jax.experimental import pallas as pl
from jax.experimental.pallas import tpu as pltpu

F32, BF16 = jnp.float32, jnp.bfloat16
MESH = pl.DeviceIdType.MESH

S, D, DEPTH = 2048, 1024, 2
CHUNK, EPS = 64, 1e-6
GM_W, GM_G, GM_B = 1024, 4, 128
H, NOPE, ROPE, VDIM = 8, 128, 64, 128
QR, KVR = 384, 256
MLA_W = H * VDIM
LRU_W, LRU_NB, LRU_BW, LRU_C, CONV_W = 1280, 16, 80, 8.0, 4
ROPE_THETA = 10000.0
IN_SIZES = (GM_W, GM_W, GM_W, QR, KVR, ROPE, MLA_W, LRU_W, LRU_W, D, D, D)
N_IN = sum(IN_SIZES)
N_CHIPS = 4
ADAM_LR, ADAM_B1, ADAM_B2, ADAM_EPS, ADAM_WD, ADAM_STEP = 0.001, 0.9, 0.999, 1e-08, 0.01, 10

HP = 256
O_U, O_V, O_ZA, O_GA, O_GB, O_GC = 0, 1024, 2048, 3072, 4096, 5120
O_CKV, O_KR, O_CQ, O_XC, O_ZC, O_ZB = 6144, 6400, 6528, 7680, 8960, 10240
NP = 11264
VMEM_LIMIT = 48 * 1024 * 1024


def _tile(dim, target):
    if dim <= target:
        return dim
    t = (target // 128) * 128
    while dim % t:
        t -= 128
    return t


def _sig(x):
    return jax.nn.sigmoid(x)


def _silu(x):
    return x * _sig(x)


def _dsilu(x):
    s = _sig(x)
    return s * (1.0 + x * (1.0 - s))


def _mm(a, b, mode, name, out_dtype=F32, tm=512, tn=512, tk=1024, b_lead=None, out_lead=None):
    b2 = b.shape[1:] if b_lead is not None else b.shape
    if mode == "nn":
        (M, K), (K2, N) = a.shape, b2
    elif mode == "nt":
        (M, K), (N, K2) = a.shape, b2
    else:
        (K, M), (K2, N) = a.shape, b2
    assert K == K2, (name, a.shape, b.shape)
    tm, tn, tk = _tile(M, tm), _tile(N, tn), _tile(K, tk)
    nk = K // tk
    if mode == "tn":
        a_spec = pl.BlockSpec((tk, tm), lambda i, j, k: (k, i))
        lhs_c = 0
    else:
        a_spec = pl.BlockSpec((tm, tk), lambda i, j, k: (i, k))
        lhs_c = 1
    b_blk, b_idx, rhs_c = ((tn, tk), (lambda i, j, k: (j, k)), 1) if mode == "nt" else ((tk, tn), (lambda i, j, k: (k, j)), 0)
    if b_lead is None:
        b_spec = pl.BlockSpec(b_blk, b_idx)
    else:
        b_spec = pl.BlockSpec((None,) + b_blk, functools.partial(lambda i, j, k, f, l: (l,) + f(i, j, k), f=b_idx, l=b_lead))
    dims = (((lhs_c,), (rhs_c,)), ((), ()))
    in_specs, args, aliases = [a_spec, b_spec], [a, b], {}
    if out_lead is None:
        out_spec = pl.BlockSpec((tm, tn), lambda i, j, k: (i, j))
        out_shape = jax.ShapeDtypeStruct((M, N), out_dtype)
    else:
        l_out, n_lead, buf = out_lead
        out_spec = pl.BlockSpec((None, tm, tn), functools.partial(lambda i, j, k, l: (l, i, j), l=l_out))
        out_shape = jax.ShapeDtypeStruct((n_lead, M, N), out_dtype)
        if buf is not None:
            in_specs.append(pl.BlockSpec(memory_space=pl.ANY))
            args.append(buf)
            aliases = {2: 0}

    def body(a_ref, b_ref, *rest):
        o_ref, acc_ref = rest[-2:]
        k = pl.program_id(2)

        @pl.when(k == 0)
        def _():
            acc_ref[...] = jnp.zeros_like(acc_ref)

        acc_ref[...] += lax.dot_general(a_ref[...].astype(BF16), b_ref[...].astype(BF16), dims,
                                        preferred_element_type=F32)

        @pl.when(k == nk - 1)
        def _():
            o_ref[...] = acc_ref[...].astype(o_ref.dtype)

    return pl.pallas_call(
        body, name=name, grid=(M // tm, N // tn, nk),
        in_specs=in_specs, out_specs=out_spec, out_shape=out_shape,
        scratch_shapes=[pltpu.VMEM((tm, tn), F32)], input_output_aliases=aliases,
        compiler_params=pltpu.CompilerParams(dimension_semantics=("parallel", "parallel", "arbitrary"),
                                             vmem_limit_bytes=VMEM_LIMIT),
    )(*args)


def _rows(fn, name, tm, rows, halos=(), fulls=(), outs=(), accs=()):
    n = S // tm
    in_specs, args = [], []
    for arr, w, cb in rows:
        in_specs.append(pl.BlockSpec((tm, w), functools.partial(lambda i, cb: (i, cb), cb=cb)))
        args.append(arr)
    for arr, w, cb, side in halos:
        if side == "prev":
            im = functools.partial(lambda i, cb: (jnp.maximum(i * (tm // 8) - 1, 0), cb), cb=cb)
        else:
            im = functools.partial(lambda i, cb: (jnp.minimum((i + 1) * (tm // 8), S // 8 - 1), cb), cb=cb)
        in_specs.append(pl.BlockSpec((8, w), im))
        args.append(arr)
    for arr in fulls:
        in_specs.append(pl.BlockSpec(arr.shape, functools.partial(lambda i, nd: (0,) * nd, nd=arr.ndim)))
        args.append(arr)
    out_shape, out_specs, aliases, n_alias = [], [], {}, 0
    for k, o in enumerate(outs):
        if len(o) == 3 and o[2] == "T":
            out_shape.append(jax.ShapeDtypeStruct((o[0], S), o[1]))
            out_specs.append(pl.BlockSpec((o[0], tm), lambda i: (0, i)))
        elif len(o) == 3:
            buf, total, cb = o[2]
            out_shape.append(jax.ShapeDtypeStruct((S, total), o[1]))
            out_specs.append(pl.BlockSpec((tm, o[0]), functools.partial(lambda i, cb: (i, cb), cb=cb)))
            if buf is not None:
                aliases[len(args)] = k
                in_specs.append(pl.BlockSpec(memory_space=pl.ANY))
                args.append(buf)
                n_alias += 1
        else:
            out_shape.append(jax.ShapeDtypeStruct((S, o[0]), o[1]))
            out_specs.append(pl.BlockSpec((tm, o[0]), lambda i: (i, 0)))
    for shp in accs:
        out_shape.append(jax.ShapeDtypeStruct(shp, F32))
        out_specs.append(pl.BlockSpec(shp, functools.partial(lambda i, nd: (0,) * nd, nd=len(shp))))
    nr, nh, nf, no, na = len(rows), len(halos), len(fulls), len(outs), len(accs)

    def body(*refs):
        i = pl.program_id(0)
        ins, orefs = refs[:nr + nh + nf], refs[nr + nh + nf + n_alias:]
        rv = [r[...] for r in ins[:nr]]
        hv = [r[...] for r in ins[nr:nr + nh]]
        fv = [r[...] for r in ins[nr + nh:]]
        o, a = fn(i, rv, hv, fv)
        assert len(o) == no and len(a) == na, name
        for spec, ref, val in zip(outs, orefs[:no], o):
            ref[...] = (val.T if len(spec) == 3 and spec[2] == "T" else val).astype(ref.dtype)
        if na:
            @pl.when(i == 0)
            def _():
                for ref in orefs[no:]:
                    ref[...] = jnp.zeros_like(ref)

            for ref, val in zip(orefs[no:], a):
                ref[...] += val

    res = pl.pallas_call(
        body, name=name, grid=(n,), in_specs=in_specs, out_specs=out_specs, out_shape=out_shape,
        input_output_aliases=aliases,
        compiler_params=pltpu.CompilerParams(dimension_semantics=("arbitrary",), vmem_limit_bytes=VMEM_LIMIT),
    )(*args)
    return res


def _shift_down(xb, halo, s, row):
    fix = jnp.tile(pltpu.roll(halo, s, 0), (xb.shape[0] // 8, 1))
    return jnp.where(row >= s, pltpu.roll(xb, s, 0), fix)


def _shift_up(xb, halo, s, row):
    tm = xb.shape[0]
    fix = jnp.tile(pltpu.roll(halo, 8 - s, 0), (tm // 8, 1))
    return jnp.where(row < tm - s, pltpu.roll(xb, tm - s, 0), fix)


def _rms(x):
    return lax.rsqrt(jnp.mean(x * x, axis=-1, keepdims=True) + EPS)


def _rms_bwd(dy, x, g):
    r = _rms(x)
    xh = x * r
    dxh = dy * g
    dx = r * (dxh - xh * jnp.mean(dxh * xh, axis=-1, keepdims=True))
    return dx, dy * xh


def _colsum(x):
    return jnp.sum(x, axis=0, keepdims=True)


def _prenorm_fwd(x, g):
    def fn(i, rv, hv, fv):
        (xb,), (gg,) = rv, fv
        return [xb * _rms(xb) * gg], []
    return _rows(fn, "prenorm_fwd", 256, [(x, D, 0)], fulls=[g], outs=[(D, BF16)])[0]


def _gm_mask():
    r = lax.broadcasted_iota(jnp.int32, (GM_B, GM_B), 0) // CHUNK
    c = lax.broadcasted_iota(jnp.int32, (GM_B, GM_B), 1) // CHUNK
    return c <= r


def _gm_norm(v, g, b):
    mu = jnp.mean(v, axis=-1, keepdims=True)
    vc = v - mu
    rs = lax.rsqrt(jnp.mean(vc * vc, axis=-1, keepdims=True) + EPS)
    vh = vc * rs
    return vh, rs, vh * g + b


def _gm_sv(vn, ws, bst):
    mask = _gm_mask()
    gw = GM_W // GM_G
    parts = []
    for g in range(GM_G):
        wm = jnp.where(mask, ws[g], 0.0).astype(BF16)
        parts.append(jnp.dot(wm, vn[:, g * gw:(g + 1) * gw].astype(BF16), preferred_element_type=F32)
                     + bst[:, g:g + 1])
    return jnp.concatenate(parts, axis=1)


def _gmlp_fwd(proj, ln_g, ln_b, ws, bst):
    def fn(i, rv, hv, fv):
        u, v, z = rv
        g, b, w, bt = fv
        _, _, vn = _gm_norm(v, g, b)
        return [u * _gm_sv(vn, w, bt) * _silu(z)], []
    return _rows(fn, "gmlp_fwd", GM_B, [(proj, GM_W, 0), (proj, GM_W, 1), (proj, GM_W, 2)],
                 fulls=[ln_g, ln_b, ws, bst], outs=[(GM_W, BF16)])[0]


def _mla_prep_fwd(proj, qg, kvg):
    def fn(i, rv, hv, fv):
        cq, ckv = rv
        g1, g2 = fv
        return [cq * _rms(cq) * g1, ckv * _rms(ckv) * g2], []
    return _rows(fn, "mla_prep_fwd", 256, [(proj, QR, O_CQ // QR), (proj, KVR, O_CKV // KVR)],
                 fulls=[qg, kvg], outs=[(QR, BF16), (KVR, BF16)])


def _rot(t, cc, sa, sb):
    return t * cc + pltpu.roll(t, 32, 1) * sa + pltpu.roll(t, 96, 1) * sb


def _rot_t(g, cc, sa, sb):
    return g * cc + pltpu.roll(g * sa, 96, 1) + pltpu.roll(g * sb, 32, 1)


def _rope_tables():
    pos = jnp.arange(S, dtype=F32)
    inv_freq = ROPE_THETA ** (-jnp.arange(0, ROPE, 2, dtype=F32) / ROPE)
    ang = pos[:, None] * inv_freq[None, :]
    cos, sin, z = jnp.cos(ang), jnp.sin(ang), jnp.zeros((S, 32), F32)
    cc = jnp.concatenate([cos, cos, z, z], axis=1)
    sa = jnp.concatenate([z, sin, z, z], axis=1)
    sb = jnp.concatenate([-sin, z, z, z], axis=1)
    return cc, sa, sb


ATT_SCALE = 1.0 / math.sqrt(NOPE + ROPE)


def _rope_fwd(q, kv, proj, tabs):
    def fn(i, rv, hv, fv):
        qb, kvb, kr, cc, sa, sb = rv
        krr = _rot(kr, cc, sa, sb)
        qs, ks = [], []
        for h in range(H):
            qs += [qb[:, h * HP:h * HP + 128] * ATT_SCALE, _rot(qb[:, h * HP + 128:(h + 1) * HP], cc, sa, sb) * ATT_SCALE]
            ks += [kvb[:, h * 128:(h + 1) * 128], krr]
        kc = jnp.concatenate(ks, axis=1)
        vv = kvb[:, H * NOPE:]
        return [jnp.concatenate(qs, axis=1), kc, kc, vv, vv], []
    cc, sa, sb = tabs
    return _rows(fn, "rope_fwd", 256,
                 [(q, H * HP, 0), (kv, H * 256, 0), (proj, 128, O_KR // 128), (cc, 128, 0), (sa, 128, 0), (sb, 128, 0)],
                 outs=[(H * HP, BF16), (H * HP, BF16), (H * HP, BF16, "T"), (MLA_W, BF16), (MLA_W, BF16, "T")])


TQ, TC, ATT_NB = 256, 128, 4
_NT = (((1,), (1,)), ((), ()))


def _attn_allowed(i, kc):
    kpos = kc * TC + lax.broadcasted_iota(jnp.int32, (TC, TQ), 0)
    qpos = i * TQ + lax.broadcasted_iota(jnp.int32, (TC, TQ), 1)
    return (kpos // CHUNK) <= (qpos // CHUNK)


def _attn_fwd(qc, kc, vt):
    def body(q_ref, k_ref, vt_ref, o_ref, l_ref):
        i = pl.program_id(1)
        q = q_ref[...]

        def scores(sb):
            t0s = [pl.multiple_of((sb * ATT_NB + c) * TC, TC) for c in range(ATT_NB)]
            return [lax.dot_general(k_ref[pl.ds(t0, TC), :], q, _NT, preferred_element_type=F32) for t0 in t0s]

        def block(sb, ss, carry, masked):
            m, l, acc = carry
            t0s = [pl.multiple_of((sb * ATT_NB + c) * TC, TC) for c in range(ATT_NB)]
            if masked:
                ss = [jnp.where(_attn_allowed(i, sb * ATT_NB + c), s, -1e30) for c, s in enumerate(ss)]
            m_new = m
            for s in ss:
                m_new = jnp.maximum(m_new, jnp.max(s, axis=0, keepdims=True))
            alpha = jnp.exp(m - m_new)
            ps = [jnp.exp(s - m_new) for s in ss]
            l = alpha * l
            acc = alpha * acc
            for t0, p in zip(t0s, ps):
                l = l + jnp.sum(p, axis=0, keepdims=True)
                acc = acc + jnp.dot(vt_ref[:, pl.ds(t0, TC)], p.astype(BF16), preferred_element_type=F32)
            return m_new, l, acc

        nsb = (i + 2) // 2
        c = (jnp.full((1, TQ), -1e30, F32), jnp.zeros((1, TQ), F32), jnp.zeros((VDIM, TQ), F32))

        def step(sb, sc):
            nxt = scores(sb + 1)
            return nxt, block(sb, sc[0], sc[1], False)

        ss, c = lax.fori_loop(0, nsb - 1, step, (scores(0), c))
        m, l, acc = block(nsb - 1, ss, c, True)
        o_ref[...] = (acc / l).T
        l_ref[...] = m + jnp.log(l)

    return pl.pallas_call(
        body, name="attn_fwd", grid=(H, S // TQ),
        in_specs=[pl.BlockSpec((TQ, HP), lambda h, i: (i, h)),
                  pl.BlockSpec((S, HP), lambda h, i: (0, h)),
                  pl.BlockSpec((VDIM, S), lambda h, i: (h, 0))],
        out_specs=[pl.BlockSpec((TQ, VDIM), lambda h, i: (i, h)), pl.BlockSpec((None, 1, TQ), lambda h, i: (h, 0, i))],
        out_shape=[jax.ShapeDtypeStruct((S, MLA_W), F32), jax.ShapeDtypeStruct((H, 1, S), F32)],
        compiler_params=pltpu.CompilerParams(dimension_semantics=("parallel", "arbitrary"),
                                             vmem_limit_bytes=VMEM_LIMIT),
    )(qc, kc, vt)


def _gate_mul_fwd(name, val, proj, width, cb):
    def fn(i, rv, hv, fv):
        o, z = rv
        return [o * _silu(z)], []
    return _rows(fn, name, 256, [(val, width, 0), (proj, width, cb)], outs=[(width, BF16)])[0]


def _conv_fwd(proj, w, b):
    def fn(i, rv, hv, fv):
        (xb,), (halo,), (ww, bb) = rv, hv, fv
        halo = jnp.where(i > 0, halo, 0.0)
        row = lax.broadcasted_iota(jnp.int32, xb.shape, 0)
        acc = bb + ww[3:4] * xb
        for s in range(1, CONV_W):
            acc = acc + ww[3 - s:4 - s] * _shift_down(xb, halo, s, row)
        return [acc, acc], []
    return _rows(fn, "conv_fwd", 128, [(proj, LRU_W, O_XC // LRU_W)], halos=[(proj, LRU_W, O_XC // LRU_W, "prev")],
                 fulls=[w, b], outs=[(LRU_W, F32), (LRU_W, BF16)])


def _lru_terms(ga, gx, xc, ba, bx, lam):
    r = _sig(ga + ba)
    ig = _sig(gx + bx)
    sp = jnp.maximum(-lam, 0.0) + jnp.log(1.0 + jnp.exp(-jnp.abs(lam)))
    log_a = -LRU_C * r * sp
    a = jnp.exp(log_a)
    e2 = jnp.exp(2.0 * log_a)
    om = 1.0 - e2
    mult = jnp.sqrt(jnp.maximum(om, 0.0))
    return r, ig, sp, a, e2, om, mult


def _lru_gates_fwd(gates, xc, ba, bx, lam):
    def fn(i, rv, hv, fv):
        ga, gx, x = rv
        r, ig, sp, a, e2, om, mult = _lru_terms(ga, gx, x, *fv)
        return [a, mult * (ig * x)], []
    return _rows(fn, "lru_gates_fwd", 128, [(gates, LRU_W, 0), (gates, LRU_W, 1), (xc, LRU_W, 0)],
                 fulls=[ba, bx, lam], outs=[(LRU_W, F32), (LRU_W, F32)])


SCAN_T, SCAN_CW = 64, 256


def _scan_fwd(a, b):
    def body(a_ref, b_ref, h_ref):
        row = lax.broadcasted_iota(jnp.int32, (SCAN_T, SCAN_CW), 0)

        def step(blk, hc):
            t0 = pl.multiple_of(blk * SCAN_T, SCAN_T)
            A = a_ref[pl.ds(t0, SCAN_T), :]
            B = b_ref[pl.ds(t0, SCAN_T), :]
            d = 1
            while d < SCAN_T:
                keep = row >= d
                A_s = jnp.where(keep, pltpu.roll(A, d, 0), 1.0)
                B_s = jnp.where(keep, pltpu.roll(B, d, 0), 0.0)
                B = A * B_s + B
                A = A * A_s
                d *= 2
            hh = A * hc + B
            h_ref[pl.ds(t0, SCAN_T), :] = hh
            return hh[SCAN_T - 1:SCAN_T, :]

        lax.fori_loop(0, S // SCAN_T, step, jnp.zeros((1, SCAN_CW), F32))

    spec = pl.BlockSpec((S, SCAN_CW), lambda j: (0, j))
    return pl.pallas_call(
        body, name="scan_fwd", grid=(LRU_W // SCAN_CW,), in_specs=[spec, spec], out_specs=spec,
        out_shape=jax.ShapeDtypeStruct((S, LRU_W), F32),
        compiler_params=pltpu.CompilerParams(dimension_semantics=("parallel",), vmem_limit_bytes=VMEM_LIMIT),
    )(a, b)


def _merge_fwd(pa, pb, pc, proj):
    def fn(i, rv, hv, fv):
        a, b, c, ga, gb, gc = rv
        return [_sig(ga) * a + _sig(gb) * b + _sig(gc) * c], []
    return _rows(fn, "merge_fwd", 256,
                 [(pa, D, 0), (pb, D, 0), (pc, D, 0), (proj, D, O_GA // D), (proj, D, O_GB // D), (proj, D, O_GC // D)],
                 outs=[(D, BF16)])[0]


def _post_fwd(x, o2, g):
    def fn(i, rv, hv, fv):
        xb, ob = rv
        return [xb + ob * _rms(ob) * fv[0]], []
    return _rows(fn, "post_fwd", 256, [(x, D, 0), (o2, D, 0)], fulls=[g], outs=[(D, F32)])[0]


SB = 640
BD_TM = 512


def _bd_fwd(xcb, wsb, l):
    def body(x_ref, w_ref, o_ref):
        o_ref[...] = jnp.dot(x_ref[...], w_ref[...], preferred_element_type=F32)

    return pl.pallas_call(
        body, name="lru_gate_mm", grid=(S // BD_TM, 4),
        in_specs=[pl.BlockSpec((BD_TM, SB), lambda i, q: (i, q % 2)),
                  pl.BlockSpec((None, None, SB, SB), lambda i, q: (l, q, 0, 0))],
        out_specs=pl.BlockSpec((BD_TM, SB), lambda i, q: (i, q)),
        out_shape=jax.ShapeDtypeStruct((S, 2 * LRU_W), F32),
        compiler_params=pltpu.CompilerParams(dimension_semantics=("parallel", "parallel"), vmem_limit_bytes=VMEM_LIMIT),
    )(xcb, wsb)


def _bd_dx(dgates, wsb, l):
    def body(d_ref, w_ref, o_ref, acc_ref):
        g = pl.program_id(2)

        @pl.when(g == 0)
        def _():
            acc_ref[...] = jnp.zeros_like(acc_ref)

        acc_ref[...] += lax.dot_general(d_ref[...], w_ref[...], (((1,), (1,)), ((), ())), preferred_element_type=F32)

        @pl.when(g == 1)
        def _():
            o_ref[...] = acc_ref[...]

    return pl.pallas_call(
        body, name="lru_gate_dx", grid=(S // BD_TM, 2, 2),
        in_specs=[pl.BlockSpec((BD_TM, SB), lambda i, s, g: (i, 2 * g + s)),
                  pl.BlockSpec((None, None, SB, SB), lambda i, s, g: (l, 2 * g + s, 0, 0))],
        out_specs=pl.BlockSpec((BD_TM, SB), lambda i, s, g: (i, s)),
        out_shape=jax.ShapeDtypeStruct((S, LRU_W), F32),
        scratch_shapes=[pltpu.VMEM((BD_TM, SB), F32)],
        compiler_params=pltpu.CompilerParams(dimension_semantics=("parallel", "parallel", "arbitrary"),
                                             vmem_limit_bytes=VMEM_LIMIT),
    )(dgates, wsb)


def _bd_dw(xcb, dgates):
    tk = 1024

    def body(x_ref, d_ref, o_ref):
        @pl.when(pl.program_id(1) == 0)
        def _():
            o_ref[...] = jnp.zeros_like(o_ref)

        o_ref[...] += lax.dot_general(x_ref[...], d_ref[...], (((0,), (0,)), ((), ())), preferred_element_type=F32)

    return pl.pallas_call(
        body, name="lru_gate_dw", grid=(4, S // tk),
        in_specs=[pl.BlockSpec((tk, SB), lambda q, k: (k, q % 2)), pl.BlockSpec((tk, SB), lambda q, k: (k, q))],
        out_specs=pl.BlockSpec((None, SB, SB), lambda q, k: (q, 0, 0)),
        out_shape=jax.ShapeDtypeStruct((4, SB, SB), F32),
        compiler_params=pltpu.CompilerParams(dimension_semantics=("parallel", "arbitrary"), vmem_limit_bytes=VMEM_LIMIT),
    )(xcb, dgates)


def _bd_extract(dwsb):
    def body(w_ref, o_ref):
        lane = lax.broadcasted_iota(jnp.int32, (LRU_BW, 128), 1)
        for q in range(4):
            for kk in range(8):
                c0 = LRU_BW * kk
                w0, off = (c0 // 128) * 128, c0 % 128
                rows = pl.ds(LRU_BW * kk, LRU_BW)
                blk = w_ref[q, rows, w0:w0 + 128]
                if off:
                    blk = pltpu.roll(blk, 128 - off, 1)
                    if off + LRU_BW > 128:
                        nxt = pltpu.roll(w_ref[q, rows, w0 + 128:w0 + 256], 128 - off, 1)
                        blk = jnp.where(lane < 128 - off, blk, nxt)
                o_ref[q // 2, 8 * (q % 2) + kk] = blk.astype(BF16)

    return pl.pallas_call(
        body, name="lru_gate_dw_blocks",
        in_specs=[pl.BlockSpec(memory_space=pltpu.VMEM)], out_specs=pl.BlockSpec(memory_space=pltpu.VMEM),
        out_shape=jax.ShapeDtypeStruct((2, LRU_NB, LRU_BW, 128), BF16),
        compiler_params=pltpu.CompilerParams(vmem_limit_bytes=VMEM_LIMIT),
    )(dwsb)


def _layer_fwd(x, P, l, tabs):
    A = {"x": x}
    A["h"] = _prenorm_fwd(x, P["pre_g"])
    proj = A["proj"] = _mm(A["h"], P["wp"], "nt", "in_proj", b_lead=l, tm=1024)
    A["ya"] = _gmlp_fwd(proj, P["ln_g"], P["ln_b"], P["ws"], P["bst"])
    A["cqn"], A["ckvn"] = _mla_prep_fwd(proj, P["qg"], P["kvg"])
    q = _mm(A["cqn"], P["wuq"], "nt", "q_up", b_lead=l)
    kv = _mm(A["ckvn"], P["wukv"], "nt", "kv_up", b_lead=l)
    A["qc"], A["kc"], A["kct"], A["vv"], vt = _rope_fwd(q, kv, proj, tabs)
    A["o"], A["lse"] = _attn_fwd(A["qc"], A["kc"], vt)
    A["yb"] = _gate_mul_fwd("yb_fwd", A["o"], proj, MLA_W, O_ZB // MLA_W)
    A["xc"], A["xcb"] = _conv_fwd(proj, P["conv_w"], P["conv_b"])
    A["gates"] = _bd_fwd(A["xcb"], P["wsb"], l)
    A["a"], bterm = _lru_gates_fwd(A["gates"], A["xc"], P["ba"], P["bx"], P["lam"])
    A["hs"] = _scan_fwd(A["a"], bterm)
    A["yc"] = _gate_mul_fwd("yc_fwd", A["hs"], proj, LRU_W, O_ZC // LRU_W)
    A["pa"] = _mm(A["ya"], P["wpa"], "nn", "proj_a", b_lead=l)
    A["pb"] = _mm(A["yb"], P["wpb"], "nn", "proj_b", b_lead=l)
    A["pc"] = _mm(A["yc"], P["wpc"], "nn", "proj_c", b_lead=l)
    A["merged"] = _merge_fwd(A["pa"], A["pb"], A["pc"], proj)
    A["o2"] = _mm(A["merged"], P["wout"], "nn", "out_proj", b_lead=l)
    return _post_fwd(x, A["o2"], P["post_g"]), A


def _loss_fwd(y, tgt):
    def fn(i, rv, hv, fv):
        yb, tb = rv
        e = yb - tb
        part = 0.5 * jnp.sum(jnp.mean(e * e, axis=-1, keepdims=True), axis=0, keepdims=True)
        return [e * (1.0 / D)], [part]
    return _rows(fn, "loss", 256, [(y, D, 0), (tgt, D, 0)], outs=[(D, F32)], accs=[(1, 1)])


def _post_bwd(dxn, o2, g):
    def fn(i, rv, hv, fv):
        dy, ob = rv
        dx, dg = _rms_bwd(dy, ob, fv[0])
        return [dx], [_colsum(dg)]
    return _rows(fn, "post_bwd", 256, [(dxn, D, 0), (o2, D, 0)], fulls=[g], outs=[(D, BF16)], accs=[(1, D)])


def _merge_bwd(dm, pa, pb, pc, proj, dproj):
    def fn(i, rv, hv, fv):
        d, a, b, c, ga, gb, gc = rv
        outs_p, outs_g = [], []
        for p, gg in ((a, ga), (b, gb), (c, gc)):
            s = _sig(gg)
            outs_p.append(d * s)
            outs_g.append(d * p * s * (1.0 - s))
        return outs_p + [jnp.concatenate(outs_g, axis=1)], []
    return _rows(fn, "merge_bwd", 128,
                 [(dm, D, 0), (pa, D, 0), (pb, D, 0), (pc, D, 0),
                  (proj, D, O_GA // D), (proj, D, O_GB // D), (proj, D, O_GC // D)],
                 outs=[(D, BF16)] * 3 + [(3 * D, BF16, (dproj, NP, O_GA // (3 * D)))])


def _gmlp_bwd(dya, proj, ln_g, ln_b, ws, bst, dproj):
    gw = GM_W // GM_G

    def fn(i, rv, hv, fv):
        dy, u, v, z = rv
        g, b, w, bt = fv
        vh, rs, vn = _gm_norm(v, g, b)
        sv = _gm_sv(vn, w, bt)
        sz = _silu(z)
        du = dy * sv * sz
        dsv = dy * u * sz
        dz = dy * u * sv * _dsilu(z)
        mask = _gm_mask()
        lane = lax.broadcasted_iota(jnp.int32, (GM_B, 128), 1)
        dvn_parts, dws, dbst = [], [], jnp.zeros((GM_B, 128), F32)
        for k in range(GM_G):
            wm = jnp.where(mask, w[k], 0.0).astype(BF16)
            dsk = dsv[:, k * gw:(k + 1) * gw]
            dskb = dsk.astype(BF16)
            dvn_parts.append(lax.dot_general(wm, dskb, (((0,), (0,)), ((), ())), preferred_element_type=F32))
            dwk = lax.dot_general(dskb, vn[:, k * gw:(k + 1) * gw].astype(BF16), (((1,), (1,)), ((), ())),
                                  preferred_element_type=F32)
            dws.append(jnp.where(mask, dwk, 0.0)[None])
            dbst = dbst + jnp.where(lane == k, jnp.sum(dsk, axis=1, keepdims=True), 0.0)
        dvn = jnp.concatenate(dvn_parts, axis=1)
        dvh = dvn * g
        dv = rs * (dvh - jnp.mean(dvh, axis=-1, keepdims=True) - vh * jnp.mean(dvh * vh, axis=-1, keepdims=True))
        return ([jnp.concatenate([du, dv, dz], axis=1)],
                [jnp.concatenate(dws, axis=0), dbst, _colsum(dvn * vh), _colsum(dvn)])
    return _rows(fn, "gmlp_bwd", GM_B, [(dya, GM_W, 0), (proj, GM_W, 0), (proj, GM_W, 1), (proj, GM_W, 2)],
                 fulls=[ln_g, ln_b, ws, bst], outs=[(3 * GM_W, BF16, (dproj, NP, O_U // (3 * GM_W)))],
                 accs=[(GM_G, GM_B, GM_B), (GM_B, 128), (1, GM_W), (1, GM_W)])


def _yb_bwd(dyb, o, proj, dproj):
    def fn(i, rv, hv, fv):
        dy, ob, z = rv
        do = dy * _silu(z)
        prod = do * ob
        lane = lax.broadcasted_iota(jnp.int32, (dy.shape[0], 128), 1)
        dl = jnp.zeros((dy.shape[0], 128), F32)
        for h in range(H):
            dl = dl + jnp.where(lane == h, jnp.sum(prod[:, h * VDIM:(h + 1) * VDIM], axis=1, keepdims=True), 0.0)
        return [do, dl, dy * ob * _dsilu(z)], []
    return _rows(fn, "yb_bwd", 256, [(dyb, MLA_W, 0), (o, MLA_W, 0), (proj, MLA_W, O_ZB // MLA_W)],
                 outs=[(MLA_W, BF16), (128, F32, "T"), (MLA_W, BF16, (dproj, NP, O_ZB // MLA_W))])


def _attn_bwd(qc, kc, kct, vv, do, lse, dlt):
    def body(q_ref, k_ref, kt_ref, v_ref, do_ref, l_ref, d_ref, dq_ref, dk_ref, dv_ref, dqt_ref):
        h, i = pl.program_id(0), pl.program_id(1)

        @pl.when(i == 0)
        def _():
            dk_ref[...] = jnp.zeros_like(dk_ref)
            dv_ref[...] = jnp.zeros_like(dv_ref)

        q = q_ref[...]
        dob = do_ref[...]
        lse = l_ref[...]
        dl = d_ref[pl.ds(h, 1), :]
        dqt_ref[...] = jnp.zeros_like(dqt_ref)

        def rows_of(sb, c):
            return pl.ds(pl.multiple_of((sb * ATT_NB + c) * TC, TC), TC)

        def front(sb):
            return [(lax.dot_general(k_ref[rows_of(sb, c), :], q, _NT, preferred_element_type=F32),
                     lax.dot_general(v_ref[rows_of(sb, c), :], dob, _NT, preferred_element_type=F32))
                    for c in range(ATT_NB)]

        def block(sb, sd, masked):
            dqt = None
            for c, (s, dp) in enumerate(sd):
                rows = rows_of(sb, c)
                p = jnp.exp(s - lse)
                if masked:
                    p = jnp.where(_attn_allowed(i, sb * ATT_NB + c), p, 0.0)
                ds = (p * (dp - dl)).astype(BF16)
                dk_ref[rows, :] += jnp.dot(ds, q, preferred_element_type=F32)
                dv_ref[rows, :] += jnp.dot(p.astype(BF16), dob, preferred_element_type=F32)
                part = jnp.dot(kt_ref[:, rows], ds, preferred_element_type=F32)
                dqt = part if dqt is None else dqt + part
            dqt_ref[...] += dqt

        def step(sb, sd):
            nxt = front(sb + 1)
            block(sb, sd, False)
            return nxt

        nsb = (i + 2) // 2
        sd = lax.fori_loop(0, nsb - 1, step, front(0))
        block(nsb - 1, sd, True)
        dq_ref[...] = dqt_ref[...].T

    blk = lambda w: pl.BlockSpec((TQ, w), lambda h, i: (i, h))
    head = lambda w: pl.BlockSpec((S, w), lambda h, i: (0, h))
    return pl.pallas_call(
        body, name="attn_bwd", grid=(H, S // TQ),
        in_specs=[blk(HP), head(HP), pl.BlockSpec((HP, S), lambda h, i: (h, 0)), head(VDIM), blk(VDIM),
                  pl.BlockSpec((None, 1, TQ), lambda h, i: (h, 0, i)), pl.BlockSpec((8, TQ), lambda h, i: (0, i))],
        out_specs=[blk(HP), head(HP), head(VDIM)],
        out_shape=[jax.ShapeDtypeStruct((S, H * HP), F32), jax.ShapeDtypeStruct((S, H * HP), F32),
                   jax.ShapeDtypeStruct((S, MLA_W), F32)],
        scratch_shapes=[pltpu.VMEM((HP, TQ), F32)],
        compiler_params=pltpu.CompilerParams(dimension_semantics=("parallel", "arbitrary"),
                                             vmem_limit_bytes=VMEM_LIMIT),
    )(qc, kc, kct, vv, do, lse, dlt)


def _rope_bwd(dqc, dkc, dvv, tabs):
    def fn(i, rv, hv, fv):
        dq, dk, dv, cc, sa, sb = rv
        qs, ks = [], []
        dkr = jnp.zeros((dq.shape[0], 128), F32)
        for h in range(H):
            qs += [dq[:, h * HP:h * HP + 128] * ATT_SCALE, _rot_t(dq[:, h * HP + 128:(h + 1) * HP], cc, sa, sb) * ATT_SCALE]
            ks.append(dk[:, h * HP:h * HP + 128])
            dkr = dkr + dk[:, h * HP + 128:(h + 1) * HP]
        return [jnp.concatenate(qs, axis=1), jnp.concatenate(ks + [dv], axis=1), _rot_t(dkr, cc, sa, sb)], []
    cc, sa, sb = tabs
    return _rows(fn, "rope_bwd", 256,
                 [(dqc, H * HP, 0), (dkc, H * HP, 0), (dvv, MLA_W, 0), (cc, 128, 0), (sa, 128, 0), (sb, 128, 0)],
                 outs=[(H * HP, BF16), (H * 256, BF16), (128, BF16)])


MLA_GROUP = 1536


def _mla_prep_bwd(dcqn, dckvn, dkr, proj, qg, kvg, dproj):
    def fn(i, rv, hv, fv):
        d1, d2, dk, cq, ckv = rv
        g1, g2 = fv
        dx1, dg1 = _rms_bwd(d1, cq, g1)
        dx2, dg2 = _rms_bwd(d2, ckv, g2)
        zeros = jnp.zeros((d1.shape[0], MLA_GROUP - KVR - 128 - QR), F32)
        return [jnp.concatenate([dx2, dk.astype(F32), dx1, zeros], axis=1)], [_colsum(dg1), _colsum(dg2)]
    return _rows(fn, "mla_prep_bwd", 256,
                 [(dcqn, QR, 0), (dckvn, KVR, 0), (dkr, 128, 0), (proj, QR, O_CQ // QR), (proj, KVR, O_CKV // KVR)],
                 fulls=[qg, kvg], outs=[(MLA_GROUP, BF16, (dproj, NP, O_CKV // MLA_GROUP))], accs=[(1, QR), (1, KVR)])


def _yc_bwd(dyc, hs, proj, dproj):
    def fn(i, rv, hv, fv):
        dy, hh, z = rv
        return [dy * _silu(z), dy * hh * _dsilu(z)], []
    return _rows(fn, "yc_bwd", 128, [(dyc, LRU_W, 0), (hs, LRU_W, 0), (proj, LRU_W, O_ZC // LRU_W)],
                 outs=[(LRU_W, F32), (LRU_W, BF16, (dproj, NP, O_ZC // LRU_W))])


def _scan_bwd(a, hs, dh):
    nblk = S // SCAN_T

    def body(a_ref, h_ref, dh_ref, da_ref, db_ref):
        row = lax.broadcasted_iota(jnp.int32, (SCAN_T, SCAN_CW), 0)

        def step(j, carry):
            gc, ac = carry
            blk = nblk - 1 - j
            t0 = pl.multiple_of(blk * SCAN_T, SCAN_T)
            av = a_ref[pl.ds(t0, SCAN_T), :]
            A = jnp.where(row < SCAN_T - 1, pltpu.roll(av, SCAN_T - 1, 0), ac)
            B = dh_ref[pl.ds(t0, SCAN_T), :]
            d = 1
            while d < SCAN_T:
                keep = row < SCAN_T - d
                A_s = jnp.where(keep, pltpu.roll(A, SCAN_T - d, 0), 1.0)
                B_s = jnp.where(keep, pltpu.roll(B, SCAN_T - d, 0), 0.0)
                B = A * B_s + B
                A = A * A_s
                d *= 2
            g = A * gc + B
            p0 = pl.multiple_of(jnp.maximum(t0 - 8, 0), 8)
            last = jnp.where(blk > 0, h_ref[pl.ds(p0, 8), :][7:8, :], 0.0)
            h_prev = jnp.where(row >= 1, pltpu.roll(h_ref[pl.ds(t0, SCAN_T), :], 1, 0), last)
            da_ref[pl.ds(t0, SCAN_T), :] = g * h_prev
            db_ref[pl.ds(t0, SCAN_T), :] = g
            return g[0:1, :], av[0:1, :]

        z = jnp.zeros((1, SCAN_CW), F32)
        lax.fori_loop(0, nblk, step, (z, z))

    spec = pl.BlockSpec((S, SCAN_CW), lambda j: (0, j))
    return pl.pallas_call(
        body, name="scan_bwd", grid=(LRU_W // SCAN_CW,), in_specs=[spec] * 3, out_specs=[spec] * 2,
        out_shape=[jax.ShapeDtypeStruct((S, LRU_W), F32)] * 2,
        compiler_params=pltpu.CompilerParams(dimension_semantics=("parallel",), vmem_limit_bytes=VMEM_LIMIT),
    )(a, hs, dh)


def _lru_gates_bwd(da, db, gates, xc, ba, bx, lam):
    def fn(i, rv, hv, fv):
        dav, dbv, ga, gx, x = rv
        bav, bxv, lamv = fv
        r, ig, sp, a, e2, om, mult = _lru_terms(ga, gx, x, bav, bxv, lamv)
        dmult = dbv * ig * x
        dig = dbv * mult * x
        dxc1 = dbv * mult * ig
        dlog_a = dav * a + jnp.where(om > 0.0, dmult * (-e2 / mult), 0.0)
        dr = dlog_a * (-LRU_C * sp)
        dga = dr * r * (1.0 - r)
        dgx = dig * ig * (1.0 - ig)
        dlam = _colsum(dlog_a * (-LRU_C * r)) * (-_sig(-lamv))
        return [jnp.concatenate([dga, dgx], axis=1), dxc1], [_colsum(dga), _colsum(dgx), dlam]
    return _rows(fn, "lru_gates_bwd", 128,
                 [(da, LRU_W, 0), (db, LRU_W, 0), (gates, LRU_W, 0), (gates, LRU_W, 1), (xc, LRU_W, 0)],
                 fulls=[ba, bx, lam], outs=[(2 * LRU_W, BF16), (LRU_W, F32)], accs=[(1, LRU_W)] * 3)


def _conv_bwd(dxc1, dxc2, proj, w, dproj):
    cb = O_XC // LRU_W

    def fn(i, rv, hv, fv):
        d1, d2, xb = rv
        n1, n2, xprev = hv
        ww = fv[0]
        last = i == S // 128 - 1
        dxc = d1 + d2
        nxt = jnp.where(last, 0.0, n1 + n2)
        xprev = jnp.where(i > 0, xprev, 0.0)
        row = lax.broadcasted_iota(jnp.int32, xb.shape, 0)
        dx = ww[3:4] * dxc
        dws = [None] * CONV_W
        dws[3] = _colsum(dxc * xb)
        for s in range(1, CONV_W):
            dx = dx + ww[3 - s:4 - s] * _shift_up(dxc, nxt, s, row)
            dws[3 - s] = _colsum(dxc * _shift_down(xb, xprev, s, row))
        return [dx], [jnp.concatenate(dws, axis=0), _colsum(dxc)]
    return _rows(fn, "conv_bwd", 128, [(dxc1, LRU_W, 0), (dxc2, LRU_W, 0), (proj, LRU_W, cb)],
                 halos=[(dxc1, LRU_W, 0, "next"), (dxc2, LRU_W, 0, "next"), (proj, LRU_W, cb, "prev")],
                 fulls=[w], outs=[(LRU_W, BF16, (dproj, NP, cb))], accs=[(CONV_W, LRU_W), (1, LRU_W)])


def _prenorm_bwd(dxn, dh, x, g):
    def fn(i, rv, hv, fv):
        dy, dhh, xb = rv
        dx, dg = _rms_bwd(dhh, xb, fv[0])
        return [dy + dx], [_colsum(dg)]
    return _rows(fn, "prenorm_bwd", 256, [(dxn, D, 0), (dh, D, 0), (x, D, 0)], fulls=[g], outs=[(D, F32)],
                 accs=[(1, D)])


def _layer_bwd(dxn, A, P, l, tabs, GB):
    G = {}
    GB = dict(GB) if GB is not None else {}
    proj = A["proj"]

    def dw(key, a, b, name, **tiles):
        GB[key] = _mm(a, b, "tn", name, out_dtype=BF16, out_lead=(l, DEPTH, GB.get(key)), **tiles)

    do2, G["post_g"] = _post_bwd(dxn, A["o2"], P["post_g"])
    dm = _mm(do2, P["wout"], "nt", "out_proj_dx", b_lead=l)
    dw("wout", A["merged"], do2, "out_proj_dw")
    dpa, dpb, dpc, dproj = _merge_bwd(dm, A["pa"], A["pb"], A["pc"], proj, None)
    dya = _mm(dpa, P["wpa"], "nt", "proj_a_dx", b_lead=l)
    dw("wpa", A["ya"], dpa, "proj_a_dw")
    dyb = _mm(dpb, P["wpb"], "nt", "proj_b_dx", b_lead=l)
    dw("wpb", A["yb"], dpb, "proj_b_dw")
    dyc = _mm(dpc, P["wpc"], "nt", "proj_c_dx", b_lead=l)
    dw("wpc", A["yc"], dpc, "proj_c_dw")
    dproj, G["ws"], G["bst"], G["ln_g"], G["ln_b"] = _gmlp_bwd(dya, proj, P["ln_g"], P["ln_b"], P["ws"], P["bst"], dproj)
    do, dl, dproj = _yb_bwd(dyb, A["o"], proj, dproj)
    dqc, dkc, dvv = _attn_bwd(A["qc"], A["kc"], A["kct"], A["vv"], do, A["lse"], dl)
    dq, dkv, dkr = _rope_bwd(dqc, dkc, dvv, tabs)
    dcqn = _mm(dq, P["wuq"], "nn", "q_up_dx", b_lead=l)
    dw("wuq", dq, A["cqn"], "q_up_dw")
    dckvn = _mm(dkv, P["wukv"], "nn", "kv_up_dx", b_lead=l)
    dw("wukv", dkv, A["ckvn"], "kv_up_dw")
    dproj, G["qg"], G["kvg"] = _mla_prep_bwd(dcqn, dckvn, dkr, proj, P["qg"], P["kvg"], dproj)
    dhs, dproj = _yc_bwd(dyc, A["hs"], proj, dproj)
    da, db = _scan_bwd(A["a"], A["hs"], dhs)
    dgates, dxc1, G["ba"], G["bx"], G["lam"] = _lru_gates_bwd(da, db, A["gates"], A["xc"], P["ba"], P["bx"], P["lam"])
    dxc2 = _bd_dx(dgates, P["wsb"], l)
    G["wab"] = _bd_extract(_bd_dw(A["xcb"], dgates))
    dproj, G["conv_w"], G["conv_b"] = _conv_bwd(dxc1, dxc2, proj, P["conv_w"], dproj)
    dh = _mm(dproj, P["wp"], "nn", "in_proj_dx", b_lead=l, tm=1024, tn=1024)
    dw("wp", dproj, A["h"], "in_proj_dw", tm=1536, tn=1024)
    dx, G["pre_g"] = _prenorm_bwd(dxn, dh, A["x"], P["pre_g"])
    return dx, G, GB


_ORIG_OFF = [0]
for _s in IN_SIZES:
    _ORIG_OFF.append(_ORIG_OFF[-1] + _s)
_PAD_OFF = {0: O_U, 1: O_V, 2: O_ZA, 3: O_CQ, 4: O_CKV, 5: O_KR, 6: O_ZB, 7: O_XC, 8: O_ZC, 9: O_GA, 10: O_GB, 11: O_GC}
SHARD_IN = N_IN // N_CHIPS


def _pieces_w_in(j):
    lo, hi = SHARD_IN * j, SHARD_IN * (j + 1)
    out = []
    for k in range(len(IN_SIZES)):
        a, b = max(lo, _ORIG_OFF[k]), min(hi, _ORIG_OFF[k + 1])
        if a < b:
            out.append((a - lo, _PAD_OFF[k] + a - _ORIG_OFF[k], b - a))
    return out


def _pieces_uq(j):
    return [(192 * hh, HP * (2 * j + hh), NOPE + ROPE) for hh in range(2)]


def _pieces_ukv(j):
    out = []
    for hh in range(2):
        h = 2 * j + hh
        out += [(256 * hh, NOPE * h, NOPE), (256 * hh + NOPE, H * NOPE + VDIM * h, VDIM)]
    return out


def _pieces_rows(r):
    return lambda j: [(0, r * j, r)]


LAYOUT = {
    "w_in": (SHARD_IN, NP, _pieces_w_in),
    "mla_w_uq": (2 * (NOPE + ROPE), H * HP, _pieces_uq),
    "mla_w_ukv": (2 * (NOPE + VDIM), 2 * H * 128, _pieces_ukv),
    "lru_conv_w": (1, N_CHIPS, _pieces_rows(1)),
    "w_proj_a": (GM_W // N_CHIPS, GM_W, _pieces_rows(GM_W // N_CHIPS)),
    "w_proj_b": (MLA_W // N_CHIPS, MLA_W, _pieces_rows(MLA_W // N_CHIPS)),
    "w_proj_c": (LRU_W // N_CHIPS, LRU_W, _pieces_rows(LRU_W // N_CHIPS)),
    "w_out": (D // N_CHIPS, D, _pieces_rows(D // N_CHIPS)),
}
TRANSPOSED = ("w_in", "mla_w_uq", "mla_w_ukv")


def _superblocks(w_a, w_x):
    w6 = jnp.stack([w_a, w_x], axis=1).reshape(DEPTH, 4, 8, LRU_BW, LRU_BW).astype(BF16)
    bands = [jnp.pad(w6[:, :, k], ((0, 0), (0, 0), (0, 0), (LRU_BW * k, SB - LRU_BW * (k + 1)))) for k in range(8)]
    return jnp.concatenate(bands, axis=2)


_HBM = pl.BlockSpec(memory_space=pltpu.HBM)


def _position():
    return lax.axis_index("x"), lax.axis_index("y"), lax.axis_index("c")


def _allgather(blocks, name):
    n = len(blocks)

    def body(*refs):
        ins, outs = refs[:n], refs[n:2 * n]
        send, recv, lsem = refs[2 * n:]
        x, y, c = _position()
        me, sib = (x, y, c), (x, y, 1 - c)
        chips = [(1 - x, y), (x, 1 - y), (1 - x, 1 - y)]

        def cp(k, a, block, to, src=None):
            dst = outs[a].at[4 * block[0] + 2 * block[1] + block[2]]
            return pltpu.make_async_remote_copy(src_ref=dst if src is None else src, dst_ref=dst,
                                                send_sem=send.at[7 * a + k], recv_sem=recv.at[7 * a + k],
                                                device_id=to, device_id_type=MESH)

        mine = [pltpu.make_async_copy(ins[a], outs[a].at[4 * x + 2 * y + c], lsem.at[a]) for a in range(n)]
        for m in mine:
            m.start()
        first = []
        for a in range(n):
            first.append(cp(0, a, me, sib, src=ins[a]))
            first += [cp(1 + j, a, me, (*chip, c), src=ins[a]) for j, chip in enumerate(chips)]
        for f in first:
            f.start()
        passed = []
        for j, chip in enumerate(chips):
            for a in range(n):
                cp(1 + j, a, (*chip, c), me).wait_recv()
                p = cp(4 + j, a, (*chip, c), sib)
                p.start()
                passed.append(p)
        for a in range(n):
            cp(0, a, sib, me).wait_recv()
            for j, chip in enumerate(chips):
                cp(4 + j, a, (*chip, 1 - c), me).wait_recv()
        for f in first + passed:
            f.wait_send()
        for m in mine:
            m.wait()

    return pl.pallas_call(
        body, name=name,
        out_shape=[jax.ShapeDtypeStruct((8,) + b.shape, b.dtype) for b in blocks],
        in_specs=[_HBM] * n, out_specs=[_HBM] * n,
        scratch_shapes=[pltpu.SemaphoreType.DMA((7 * n,)), pltpu.SemaphoreType.DMA((7 * n,)),
                        pltpu.SemaphoreType.DMA((n,))],
    )(*blocks)


_REL = (2, 1, 3)


def _weights_allgather(names, srcs, name):
    n = len(srcs)
    lay = [LAYOUT[nm] for nm in names]
    zeros = [jnp.zeros((DEPTH, lay[a][1]) + srcs[a].shape[1:], srcs[a].dtype) for a in range(n)]

    def body(*refs):
        ins, outs = refs[:n], refs[2 * n:3 * n]
        send, recv, lsem = refs[3 * n:]
        x, y, c = _position()
        j = 2 * x + y
        sib = (x, y, 1 - c)
        chips = [(1 - x, y), (x, 1 - y), (1 - x, 1 - y)]

        def flow(a, k, jsrc, to, from_src):
            cps = []
            for s0, d0, nr in lay[a][2](jsrc):
                dst = outs[a].at[c, pl.ds(d0, nr)]
                src = ins[a].at[pl.ds(s0, nr)] if from_src else dst
                cps.append(pltpu.make_async_remote_copy(src_ref=src, dst_ref=dst, send_sem=send.at[7 * a + k],
                                                        recv_sem=recv.at[7 * a + k], device_id=to, device_id_type=MESH))
            return cps

        def whole(a, k):
            return pltpu.make_async_remote_copy(src_ref=ins[a], dst_ref=outs[a].at[0, pl.ds(0, lay[a][0])],
                                                send_sem=send.at[7 * a + k], recv_sem=recv.at[7 * a + k],
                                                device_id=sib, device_id_type=MESH)

        for j0 in range(N_CHIPS):
            @pl.when(j == j0)
            def _(j0=j0):
                for a in range(n):
                    for s0, d0, nr in lay[a][2](j0):
                        pltpu.make_async_copy(ins[a].at[pl.ds(s0, nr)], outs[a].at[c, pl.ds(d0, nr)], lsem.at[a]).start()
                for a in range(n):
                    for cp in flow(a, 0, j0, sib, True):
                        cp.start()
                    for k, chip in enumerate(chips):
                        for cp in flow(a, 1 + k, j0, (*chip, c), True):
                            cp.start()
                for k in range(3):
                    for a in range(n):
                        whole(a, 1 + k).wait_recv()
                        for cp in flow(a, 4 + k, j0 ^ _REL[k], sib, False):
                            cp.start()

        for a in range(n):
            whole(a, 0).wait_recv()
            for k in range(3):
                whole(a, 4 + k).wait_recv()
        for a in range(n):
            for k in range(7):
                whole(a, k).wait_send()
            pltpu.make_async_copy(ins[a], outs[a].at[0, pl.ds(0, lay[a][0])], lsem.at[a]).wait()

    return pl.pallas_call(
        body, name=name,
        out_shape=[jax.ShapeDtypeStruct(z.shape, z.dtype) for z in zeros],
        in_specs=[_HBM] * (2 * n), out_specs=[_HBM] * n,
        input_output_aliases={n + a: a for a in range(n)},
        scratch_shapes=[pltpu.SemaphoreType.DMA((7 * n,)), pltpu.SemaphoreType.DMA((7 * n,)),
                        pltpu.SemaphoreType.DMA((n,))],
    )(*srcs, *zeros)


def _grads_to_sibling(gb, name):
    n = len(gb)

    def body(*refs):
        ins, outs = refs[:n], refs[n:2 * n]
        send, recv = refs[2 * n:]
        x, y, c = _position()
        cps = [pltpu.make_async_remote_copy(src_ref=ins[a].at[1 - c], dst_ref=outs[a], send_sem=send.at[a],
                                            recv_sem=recv.at[a], device_id=(x, y, 1 - c), device_id_type=MESH)
               for a in range(n)]
        for cp in cps:
            cp.start()
        for cp in cps:
            cp.wait()

    return pl.pallas_call(
        body, name=name,
        out_shape=[jax.ShapeDtypeStruct(g.shape[1:], g.dtype) for g in gb],
        in_specs=[_HBM] * n, out_specs=[_HBM] * n,
        scratch_shapes=[pltpu.SemaphoreType.DMA((n,)), pltpu.SemaphoreType.DMA((n,))],
    )(*gb)


def _chip_scatter(names, parts, name):
    n = len(parts)
    lay = [LAYOUT[nm] for nm in names]

    def body(*refs):
        ins, outs = refs[:n], refs[n:2 * n]
        send, recv, lsem = refs[2 * n:]
        x, y, c = _position()
        j = 2 * x + y
        chips = [(1 - x, y), (x, 1 - y), (1 - x, 1 - y)]

        def whole(a):
            return outs[a].at[0, pl.ds(0, lay[a][0])]

        for j0 in range(N_CHIPS):
            @pl.when(j == j0)
            def _(j0=j0):
                for a in range(n):
                    for s0, d0, nr in lay[a][2](j0):
                        pltpu.make_async_copy(ins[a].at[pl.ds(d0, nr)], outs[a].at[j0, pl.ds(s0, nr)], lsem.at[a]).start()
                    for k, chip in enumerate(chips):
                        for s0, d0, nr in lay[a][2](j0 ^ _REL[k]):
                            pltpu.make_async_remote_copy(
                                src_ref=ins[a].at[pl.ds(d0, nr)], dst_ref=outs[a].at[j0, pl.ds(s0, nr)],
                                send_sem=send.at[3 * a + k], recv_sem=recv.at[3 * a + k],
                                device_id=(*chip, c), device_id_type=MESH).start()

        for a in range(n):
            for k in range(3):
                pltpu.make_async_remote_copy(src_ref=whole(a), dst_ref=whole(a), send_sem=send.at[3 * a + k],
                                             recv_sem=recv.at[3 * a + k], device_id=(x, y, c), device_id_type=MESH).wait()
            pltpu.make_async_copy(whole(a), whole(a), lsem.at[a]).wait()

    return pl.pallas_call(
        body, name=name,
        out_shape=[jax.ShapeDtypeStruct((N_CHIPS, lay[a][0]) + parts[a].shape[1:], parts[a].dtype) for a in range(n)],
        in_specs=[_HBM] * n, out_specs=[_HBM] * n,
        scratch_shapes=[pltpu.SemaphoreType.DMA((3 * n,)), pltpu.SemaphoreType.DMA((3 * n,)),
                        pltpu.SemaphoreType.DMA((n,))],
    )(*parts)


def _reduced_exchange(bufs, name):
    n = len(bufs)

    def body(*refs):
        ins, outs = refs[:n], refs[n:2 * n]
        send, recv = refs[2 * n:]
        x, y, c = _position()
        cps = [pltpu.make_async_remote_copy(src_ref=outs[a].at[c], dst_ref=outs[a].at[c], send_sem=send.at[a],
                                            recv_sem=recv.at[a], device_id=(x, y, 1 - c), device_id_type=MESH)
               for a in range(n)]
        for cp in cps:
            cp.start()
        for cp in cps:
            cp.wait()

    return pl.pallas_call(
        body, name=name,
        out_shape=[jax.ShapeDtypeStruct(b.shape, b.dtype) for b in bufs],
        in_specs=[_HBM] * n, out_specs=[_HBM] * n, input_output_aliases={a: a for a in range(n)},
        scratch_shapes=[pltpu.SemaphoreType.DMA((n,)), pltpu.SemaphoreType.DMA((n,))],
    )(*bufs)


def _row_tile(r):
    for t in (256, 128, 64, 32, 16, 8):
        if r % t == 0 and r > t:
            return t
    return r


def _pair_add(g, rb, c_arr, name):
    R, rest = g.shape[1], g.shape[2:]
    tr = _row_tile(R)
    z = (0,) * len(rest)

    def body(c_ref, g_ref, r_ref, o_ref):
        o_ref[...] = (g_ref[...].astype(F32) + r_ref[...].astype(F32)).astype(o_ref.dtype)

    return pl.pallas_call(
        body, name=name,
        grid_spec=pltpu.PrefetchScalarGridSpec(
            num_scalar_prefetch=1, grid=(R // tr,),
            in_specs=[pl.BlockSpec((None, tr) + rest, lambda i, c_ref: (c_ref[0], i) + z),
                      pl.BlockSpec((tr,) + rest, lambda i, c_ref: (i,) + z)],
            out_specs=pl.BlockSpec((tr,) + rest, lambda i, c_ref: (i,) + z)),
        out_shape=jax.ShapeDtypeStruct((R,) + rest, BF16),
        compiler_params=pltpu.CompilerParams(dimension_semantics=("parallel",), vmem_limit_bytes=VMEM_LIMIT),
    )(c_arr, g, rb)


def _sum_slabs(rb, c_arr, name):
    n, R, rest = rb.shape[0], rb.shape[1], rb.shape[2:]
    tr = _row_tile(R)
    z = (0,) * len(rest)

    def body(c_ref, r_ref, o_ref):
        acc = r_ref[0].astype(F32)
        for k in range(1, n):
            acc = acc + r_ref[k].astype(F32)
        o_ref[...] = acc

    if R // tr > 64 and len(rest) == 1 and rest[0] % 256 == 0:
        grid = (rest[0] // 256,)
        in_spec = pl.BlockSpec((n, R, 256), lambda i, c_ref: (0, 0, i))
        out_spec = pl.BlockSpec((None, R, 256), lambda i, c_ref: (c_ref[0], 0, i))
    else:
        grid = (R // tr,)
        in_spec = pl.BlockSpec((n, tr) + rest, lambda i, c_ref: (0, i) + z)
        out_spec = pl.BlockSpec((None, tr) + rest, lambda i, c_ref: (c_ref[0], i) + z)
    return pl.pallas_call(
        body, name=name,
        grid_spec=pltpu.PrefetchScalarGridSpec(num_scalar_prefetch=1, grid=grid, in_specs=[in_spec], out_specs=out_spec),
        out_shape=jax.ShapeDtypeStruct((DEPTH, R) + rest, F32),
        compiler_params=pltpu.CompilerParams(dimension_semantics=("parallel",), vmem_limit_bytes=VMEM_LIMIT),
    )(c_arr, rb)


def _adam_math(w, g, m, v):
    mn = ADAM_B1 * m + (1.0 - ADAM_B1) * g
    vn = ADAM_B2 * v + (1.0 - ADAM_B2) * (g * g)
    m_hat = mn / (1.0 - ADAM_B1 ** ADAM_STEP)
    v_hat = vn / (1.0 - ADAM_B2 ** ADAM_STEP)
    return -ADAM_LR * (m_hat / (jnp.sqrt(v_hat) + ADAM_EPS) + ADAM_WD * w), mn, vn


def _adamw(w, g, m, v, name):
    L, R, C = w.shape
    tr = _row_tile(R)

    def body(w_ref, g_ref, m_ref, v_ref, d_ref, mo_ref, vo_ref):
        d_ref[...], mo_ref[...], vo_ref[...] = _adam_math(w_ref[...], g_ref[...], m_ref[...], v_ref[...])

    if R // tr > 64 and C % 128 == 0:
        spec, grid = pl.BlockSpec((None, R, 128), lambda l, i: (l, 0, i)), (L, C // 128)
    else:
        spec, grid = pl.BlockSpec((None, tr, C), lambda l, i: (l, i, 0)), (L, R // tr)
    return pl.pallas_call(
        body, name=name, grid=grid, in_specs=[spec] * 4, out_specs=[spec] * 3,
        out_shape=[jax.ShapeDtypeStruct((L, R, C), F32)] * 3,
        compiler_params=pltpu.CompilerParams(dimension_semantics=("parallel", "parallel"), vmem_limit_bytes=VMEM_LIMIT),
    )(w, g, m, v)


_VMEM_WHOLE = pl.BlockSpec(memory_space=pltpu.VMEM)


def _matrix_update(gath, w, m, v, name):
    K = w.shape[1]

    def body(g0_ref, g1_ref, w_ref, m_ref, v_ref, go_ref, d_ref, mo_ref, vo_ref):
        for l, gr in enumerate((g0_ref, g1_ref)):
            for k in range(K):
                g = gr[0, k].astype(F32)
                for dev in range(1, 8):
                    g = g + gr[dev, k].astype(F32)
                go_ref[l, k] = g
                d_ref[l, k], mo_ref[l, k], vo_ref[l, k] = _adam_math(w_ref[l, k], g, m_ref[l, k], v_ref[l, k])

    return pl.pallas_call(
        body, name=name, in_specs=[_VMEM_WHOLE] * 5, out_specs=[_VMEM_WHOLE] * 4,
        out_shape=[jax.ShapeDtypeStruct(w.shape, F32)] * 4,
        compiler_params=pltpu.CompilerParams(vmem_limit_bytes=VMEM_LIMIT),
    )(gath[0], gath[1], w, m, v)


VECS = (("pre_norm_g", D), ("post_norm_g", D), ("gm_ln_g", GM_W), ("gm_ln_b", GM_W), ("mla_q_norm_g", QR),
        ("mla_kv_norm_g", KVR), ("lru_conv_b", LRU_W), ("lru_b_a", LRU_W), ("lru_b_x", LRU_W), ("lru_lambda", LRU_W))
VEC_KEY = {"pre_norm_g": "pre_g", "post_norm_g": "post_g", "gm_ln_g": "ln_g", "gm_ln_b": "ln_b", "mla_q_norm_g": "qg",
           "mla_kv_norm_g": "kvg", "lru_conv_b": "conv_b", "lru_b_a": "ba", "lru_b_x": "bx", "lru_lambda": "lam"}
VEC_ROWS, VEC_W, VEC_ROW0, LOSS_ROW = 16, LRU_W, GM_G, 14


def _pack_rows(LG, loss_part):
    per = len(VECS) + 1
    ins = []
    for G in LG:
        ins += [G[VEC_KEY[n]] for n, _ in VECS] + [G["bst"]]
    ins.append(loss_part)

    def body(*refs):
        o_ref = refs[-1]
        o_ref[...] = jnp.zeros_like(o_ref)
        for l in range(DEPTH):
            base = VEC_ROWS * l
            o_ref[pl.ds(base, 8), pl.ds(0, GM_B)] = refs[per * l + len(VECS)][...].T[:8, :]
            for t, (_, width) in enumerate(VECS):
                o_ref[pl.ds(base + VEC_ROW0 + t, 1), pl.ds(0, width)] = refs[per * l + t][...]
        o_ref[pl.ds(LOSS_ROW, 1), pl.ds(0, 128)] = jnp.broadcast_to(refs[-2][...], (1, 128))

    return pl.pallas_call(
        body, name="pack_rows", in_specs=[_VMEM_WHOLE] * len(ins), out_specs=_VMEM_WHOLE,
        out_shape=jax.ShapeDtypeStruct((DEPTH * VEC_ROWS, VEC_W), F32),
    )(*ins)


def _vector_update(gath, W, M, V):
    names = [n for n, _ in VECS] + ["gm_bs"]
    nw = len(names)

    def body(*refs):
        g_ref = refs[0]
        wr, mr, vr = refs[1:1 + nw], refs[1 + nw:1 + 2 * nw], refs[1 + 2 * nw:1 + 3 * nw]
        outs = refs[1 + 3 * nw:]
        s = g_ref[0]
        for dev in range(1, 8):
            s = s + g_ref[dev]
        for t, (_, width) in enumerate(VECS):
            for l in range(DEPTH):
                r = VEC_ROWS * l + VEC_ROW0 + t
                g = s[r:r + 1, :width]
                row = (pl.ds(l, 1), slice(None))
                res = (g,) + _adam_math(wr[t][row], g, mr[t][row], vr[t][row])
                for q in range(4):
                    outs[4 * t + q][row] = res[q]
        t = len(VECS)
        for l in range(DEPTH):
            for k in range(GM_G):
                g = s[VEC_ROWS * l + k:VEC_ROWS * l + k + 1, :GM_B]
                row = (l, pl.ds(k, 1), slice(None))
                res = (g,) + _adam_math(wr[t][row], g, mr[t][row], vr[t][row])
                for q in range(4):
                    outs[4 * t + q][row] = res[q]
        outs[4 * nw][...] = s[LOSS_ROW:LOSS_ROW + 1, :128]

    ws = [W[n] for n in names]
    out_shape = []
    for w in ws:
        out_shape += [jax.ShapeDtypeStruct(w.shape, F32)] * 4
    out_shape.append(jax.ShapeDtypeStruct((1, 128), F32))
    res = pl.pallas_call(
        body, name="vector_update", in_specs=[_VMEM_WHOLE] * (1 + 3 * nw), out_specs=[_VMEM_WHOLE] * (4 * nw + 1),
        out_shape=out_shape, compiler_params=pltpu.CompilerParams(vmem_limit_bytes=VMEM_LIMIT),
    )(gath, *ws, *[M[n] for n in names], *[V[n] for n in names])
    return {n: tuple(res[4 * t:4 * t + 4]) for t, n in enumerate(names)}, res[4 * nw]


SHARDED = ("w_in", "mla_w_uq", "mla_w_ukv", "lru_conv_w", "w_proj_a", "w_proj_b", "w_proj_c", "w_out")
COL_SHARDED = ("w_in", "mla_w_uq", "mla_w_ukv", "lru_conv_w")
SMALL = ("pre_norm_g", "gm_ln_g", "gm_ln_b", "gm_ws", "gm_bs", "mla_q_norm_g", "mla_kv_norm_g", "lru_conv_b",
         "lru_w_a", "lru_b_a", "lru_w_x", "lru_b_x", "lru_lambda", "post_norm_g")
WEIGHTS = ("pre_norm_g", "w_in", "gm_ln_g", "gm_ln_b", "gm_ws", "gm_bs", "mla_q_norm_g", "mla_w_uq",
           "mla_kv_norm_g", "mla_w_ukv", "lru_conv_w", "lru_conv_b", "lru_w_a", "lru_b_a", "lru_w_x", "lru_b_x",
           "lru_lambda", "w_proj_a", "w_proj_b", "w_proj_c", "w_out", "post_norm_g")


GB_KEY = {"w_in": "wp", "mla_w_uq": "wuq", "mla_w_ukv": "wukv", "w_proj_a": "wpa", "w_proj_b": "wpb",
          "w_proj_c": "wpc", "w_out": "wout"}


def _prepare(l, gathered, small, wsb):
    P = {GB_KEY[n]: gathered[n] for n in GB_KEY}
    P["conv_w"] = gathered["lru_conv_w"][l].transpose(1, 0, 2).reshape(CONV_W, LRU_W)
    P["wsb"] = wsb
    row = lambda n: small[n][l][None, :]
    P["pre_g"], P["post_g"] = row("pre_norm_g"), row("post_norm_g")
    P["ln_g"], P["ln_b"] = row("gm_ln_g"), row("gm_ln_b")
    P["ws"] = small["gm_ws"][l]
    P["bst"] = jnp.pad(small["gm_bs"][l].T, ((0, 0), (0, 128 - GM_G)))
    P["qg"], P["kvg"] = row("mla_q_norm_g"), row("mla_kv_norm_g")
    P["conv_b"], P["ba"], P["bx"], P["lam"] = row("lru_conv_b"), row("lru_b_a"), row("lru_b_x"), row("lru_lambda")
    return P


def kernel(x, pre_norm_g, w_in, gm_ln_g, gm_ln_b, gm_ws, gm_bs, mla_q_norm_g, mla_w_uq, mla_kv_norm_g, mla_w_ukv, lru_conv_w, lru_conv_b, lru_w_a, lru_b_a, lru_w_x, lru_b_x, lru_lambda, w_proj_a, w_proj_b, w_proj_c, w_out, post_norm_g, loss_target, m_pre_norm_g, m_w_in, m_gm_ln_g, m_gm_ln_b, m_gm_ws, m_gm_bs, m_mla_q_norm_g, m_mla_w_uq, m_mla_kv_norm_g, m_mla_w_ukv, m_lru_conv_w, m_lru_conv_b, m_lru_w_a, m_lru_b_a, m_lru_w_x, m_lru_b_x, m_lru_lambda, m_w_proj_a, m_w_proj_b, m_w_proj_c, m_w_out, m_post_norm_g, v_pre_norm_g, v_w_in, v_gm_ln_g, v_gm_ln_b, v_gm_ws, v_gm_bs, v_mla_q_norm_g, v_mla_w_uq, v_mla_kv_norm_g, v_mla_w_ukv, v_lru_conv_w, v_lru_conv_b, v_lru_w_a, v_lru_b_a, v_lru_w_x, v_lru_b_x, v_lru_lambda, v_w_proj_a, v_w_proj_b, v_w_proj_c, v_w_out, v_post_norm_g):
    args = dict(locals())
    W = {n: args[n] for n in WEIGHTS}
    M = {n: args["m_" + n] for n in WEIGHTS}
    V = {n: args["v_" + n] for n in WEIGHTS}
    c = lax.axis_index("c")

    srcs = []
    for n in SHARDED:
        blk = lax.dynamic_index_in_dim(W[n], c, 0, keepdims=False)
        if n in TRANSPOSED:
            blk = blk.T
        srcs.append(blk[None] if n == "lru_conv_w" else blk.astype(BF16))
    gathered = dict(zip(SHARDED, _weights_allgather(SHARDED, srcs, "weights_allgather")))
    small = {n: W[n] for n in SMALL}
    wsb = _superblocks(W["lru_w_a"], W["lru_w_x"])
    P = [_prepare(l, gathered, small, wsb) for l in range(DEPTH)]
    tabs = _rope_tables()

    h0 = x[0]
    h1, A0 = _layer_fwd(h0, P[0], 0, tabs)
    h2, A1 = _layer_fwd(h1, P[1], 1, tabs)
    dy, loss_part = _loss_fwd(h2, loss_target[0])
    d1, G1, GB = _layer_bwd(dy, A1, P[1], 1, tabs, None)
    d0, G0, GB = _layer_bwd(d1, A0, P[0], 0, tabs, GB)
    LG = (G0, G1)

    conv_g = jnp.stack([g["conv_w"].reshape(CONV_W, N_CHIPS, LRU_W // N_CHIPS).transpose(1, 0, 2) for g in LG])
    gb = [conv_g if n == "lru_conv_w" else GB[GB_KEY[n]] for n in SHARDED]
    from_sib = _grads_to_sibling(gb, "grads_to_sibling")
    c_arr = jnp.reshape(c, (1,)).astype(jnp.int32)
    pair = [_pair_add(g, rb, c_arr, "pair_add_" + n) for n, g, rb in zip(SHARDED, gb, from_sib)]
    slabs = _chip_scatter(SHARDED, pair, "grads_chip_scatter")
    mine = [_sum_slabs(s, c_arr, "sum_slabs_" + n) for n, s in zip(SHARDED, slabs)]
    both = _reduced_exchange(mine, "reduced_to_sibling")
    grads = {}
    for n, b in zip(SHARDED, both):
        if n in TRANSPOSED and n != "w_in":
            b = jnp.swapaxes(b, 1, 2)
        grads[n] = b if n == "w_in" else b.reshape(W[n].shape)

    rows = _pack_rows(LG, loss_part)
    mats = []
    for g in LG:
        mats += [g["ws"].astype(BF16), g["wab"][0, :, :, :LRU_BW], g["wab"][1, :, :, :LRU_BW]]
    gath = _allgather([rows] + mats, "small_grads_allgather")
    upd, loss_row = _vector_update(gath[0], W, M, V)
    loss = loss_row[0, 0]
    for k, n in enumerate(("gm_ws", "lru_w_a", "lru_w_x")):
        upd[n] = _matrix_update((gath[1 + k], gath[4 + k]), W[n], M[n], V[n], "update_" + n)

    for n in SHARDED:
        if n == "w_in":
            tr = lambda a: jnp.swapaxes(a, 1, 2)
            res = _adamw(tr(W[n]), grads[n], tr(M[n]), tr(V[n]), "adamw_" + n)
            upd[n] = tuple(tr(a) for a in (grads[n],) + tuple(res))
        else:
            upd[n] = (grads[n],) + tuple(_adamw(W[n], grads[n], M[n], V[n], "adamw_" + n))

    return (loss, d0[None], *[upd[n][0] for n in WEIGHTS], *[upd[n][1] for n in WEIGHTS],
            *[upd[n][2] for n in WEIGHTS], *[upd[n][3] for n in WEIGHTS])
```

```python
import functools
import math

import jax
import jax.numpy as jnp
from jax import lax
from jax.experimental import pallas as pl
from jax.experimental.pallas import tpu as pltpu

F32, BF16 = jnp.float32, jnp.bfloat16
MESH = pl.DeviceIdType.MESH

S, D, DEPTH = 2048, 1024, 2
CHUNK, EPS = 64, 1e-6
GM_W, GM_G, GM_B = 1024, 4, 128
H, NOPE, ROPE, VDIM = 8, 128, 64, 128
QR, KVR = 384, 256
MLA_W = H * VDIM
LRU_W, LRU_NB, LRU_BW, LRU_C, CONV_W = 1280, 16, 80, 8.0, 4
ROPE_THETA = 10000.0
IN_SIZES = (GM_W, GM_W, GM_W, QR, KVR, ROPE, MLA_W, LRU_W, LRU_W, D, D, D)
N_IN = sum(IN_SIZES)
N_CHIPS = 4
ADAM_LR, ADAM_B1, ADAM_B2, ADAM_EPS, ADAM_WD, ADAM_STEP = 0.001, 0.9, 0.999, 1e-08, 0.01, 10

HP = 256
O_U, O_V, O_ZA, O_GA, O_GB, O_GC = 0, 1024, 2048, 3072, 4096, 5120
O_CKV, O_KR, O_CQ, O_XC, O_ZC, O_ZB = 6144, 6400, 6528, 7680, 8960, 10240
NP = 11264
VMEM_LIMIT = 48 * 1024 * 1024


def _tile(dim, target):
    if dim <= target:
        return dim
    t = (target // 128) * 128
    while dim % t:
        t -= 128
    return t


def _sig(x):
    return jax.nn.sigmoid(x)


def _silu(x):
    return x * _sig(x)


def _dsilu(x):
    s = _sig(x)
    return s * (1.0 + x * (1.0 - s))


def _mm(a, b, mode, name, out_dtype=F32, tm=512, tn=512, tk=1024, b_lead=None, out_lead=None):
    b2 = b.shape[1:] if b_lead is not None else b.shape
    if mode == "nn":
        (M, K), (K2, N) = a.shape, b2
    elif mode == "nt":
        (M, K), (N, K2) = a.shape, b2
    else:
        (K, M), (K2, N) = a.shape, b2
    assert K == K2, (name, a.shape, b.shape)
    tm, tn, tk = _tile(M, tm), _tile(N, tn), _tile(K, tk)
    nk = K // tk
    if mode == "tn":
        a_spec = pl.BlockSpec((tk, tm), lambda i, j, k: (k, i))
        lhs_c = 0
    else:
        a_spec = pl.BlockSpec((tm, tk), lambda i, j, k: (i, k))
        lhs_c = 1
    b_blk, b_idx, rhs_c = ((tn, tk), (lambda i, j, k: (j, k)), 1) if mode == "nt" else ((tk, tn), (lambda i, j, k: (k, j)), 0)
    if b_lead is None:
        b_spec = pl.BlockSpec(b_blk, b_idx)
    else:
        b_spec = pl.BlockSpec((None,) + b_blk, functools.partial(lambda i, j, k, f, l: (l,) + f(i, j, k), f=b_idx, l=b_lead))
    dims = (((lhs_c,), (rhs_c,)), ((), ()))
    in_specs, args, aliases = [a_spec, b_spec], [a, b], {}
    if out_lead is None:
        out_spec = pl.BlockSpec((tm, tn), lambda i, j, k: (i, j))
        out_shape = jax.ShapeDtypeStruct((M, N), out_dtype)
    else:
        l_out, n_lead, buf = out_lead
        out_spec = pl.BlockSpec((None, tm, tn), functools.partial(lambda i, j, k, l: (l, i, j), l=l_out))
        out_shape = jax.ShapeDtypeStruct((n_lead, M, N), out_dtype)
        if buf is not None:
            in_specs.append(pl.BlockSpec(memory_space=pl.ANY))
            args.append(buf)
            aliases = {2: 0}

    def body(a_ref, b_ref, *rest):
        o_ref, acc_ref = rest[-2:]
        k = pl.program_id(2)

        @pl.when(k == 0)
        def _():
            acc_ref[...] = jnp.zeros_like(acc_ref)

        acc_ref[...] += lax.dot_general(a_ref[...].astype(BF16), b_ref[...].astype(BF16), dims,
                                        preferred_element_type=F32)

        @pl.when(k == nk - 1)
        def _():
            o_ref[...] = acc_ref[...].astype(o_ref.dtype)

    return pl.pallas_call(
        body, name=name, grid=(M // tm, N // tn, nk),
        in_specs=in_specs, out_specs=out_spec, out_shape=out_shape,
        scratch_shapes=[pltpu.VMEM((tm, tn), F32)], input_output_aliases=aliases,
        compiler_params=pltpu.CompilerParams(dimension_semantics=("parallel", "parallel", "arbitrary"),
                                             vmem_limit_bytes=VMEM_LIMIT),
    )(*args)


def _rows(fn, name, tm, rows, halos=(), fulls=(), outs=(), accs=()):
    n = S // tm
    in_specs, args = [], []
    for arr, w, cb in rows:
        in_specs.append(pl.BlockSpec((tm, w), functools.partial(lambda i, cb: (i, cb), cb=cb)))
        args.append(arr)
    for arr, w, cb, side in halos:
        if side == "prev":
            im = functools.partial(lambda i, cb: (jnp.maximum(i * (tm // 16) - 1, 0), cb), cb=cb)
        else:
            im = functools.partial(lambda i, cb: (jnp.minimum((i + 1) * (tm // 16), S // 16 - 1), cb), cb=cb)
        in_specs.append(pl.BlockSpec((16, w), im))
        args.append(arr)
    for arr in fulls:
        in_specs.append(pl.BlockSpec(arr.shape, functools.partial(lambda i, nd: (0,) * nd, nd=arr.ndim)))
        args.append(arr)
    out_shape, out_specs, aliases, n_alias = [], [], {}, 0
    for k, o in enumerate(outs):
        if len(o) == 3 and o[2] == "T":
            out_shape.append(jax.ShapeDtypeStruct((o[0], S), o[1]))
            out_specs.append(pl.BlockSpec((o[0], tm), lambda i: (0, i)))
        elif len(o) == 3:
            buf, total, cb = o[2]
            out_shape.append(jax.ShapeDtypeStruct((S, total), o[1]))
            out_specs.append(pl.BlockSpec((tm, o[0]), functools.partial(lambda i, cb: (i, cb), cb=cb)))
            if buf is not None:
                aliases[len(args)] = k
                in_specs.append(pl.BlockSpec(memory_space=pl.ANY))
                args.append(buf)
                n_alias += 1
        else:
            out_shape.append(jax.ShapeDtypeStruct((S, o[0]), o[1]))
            out_specs.append(pl.BlockSpec((tm, o[0]), lambda i: (i, 0)))
    for shp in accs:
        out_shape.append(jax.ShapeDtypeStruct(shp, F32))
        out_specs.append(pl.BlockSpec(shp, functools.partial(lambda i, nd: (0,) * nd, nd=len(shp))))
    nr, nh, nf, no, na = len(rows), len(halos), len(fulls), len(outs), len(accs)

    def body(*refs):
        i = pl.program_id(0)
        ins, orefs = refs[:nr + nh + nf], refs[nr + nh + nf + n_alias:]
        rv = [r[...].astype(F32) for r in ins[:nr]]
        hv = [r[...].astype(F32)[8:] if h[3] == "prev" else r[...].astype(F32)[:8] for r, h in zip(ins[nr:nr + nh], halos)]
        fv = [r[...] for r in ins[nr + nh:]]
        o, a = fn(i, rv, hv, fv)
        assert len(o) == no and len(a) == na, name
        for spec, ref, val in zip(outs, orefs[:no], o):
            ref[...] = (val.T if len(spec) == 3 and spec[2] == "T" else val).astype(ref.dtype)
        if na:
            @pl.when(i == 0)
            def _():
                for ref in orefs[no:]:
                    ref[...] = jnp.zeros_like(ref)

            for ref, val in zip(orefs[no:], a):
                ref[...] += val

    res = pl.pallas_call(
        body, name=name, grid=(n,), in_specs=in_specs, out_specs=out_specs, out_shape=out_shape,
        input_output_aliases=aliases,
        compiler_params=pltpu.CompilerParams(dimension_semantics=("arbitrary",), vmem_limit_bytes=VMEM_LIMIT),
    )(*args)
    return res


def _shift_down(xb, halo, s, row):
    fix = jnp.tile(pltpu.roll(halo, s, 0), (xb.shape[0] // 8, 1))
    return jnp.where(row >= s, pltpu.roll(xb, s, 0), fix)


def _shift_up(xb, halo, s, row):
    tm = xb.shape[0]
    fix = jnp.tile(pltpu.roll(halo, 8 - s, 0), (tm // 8, 1))
    return jnp.where(row < tm - s, pltpu.roll(xb, tm - s, 0), fix)


def _rms(x):
    return lax.rsqrt(jnp.mean(x * x, axis=-1, keepdims=True) + EPS)


def _rms_bwd(dy, x, g):
    r = _rms(x)
    xh = x * r
    dxh = dy * g
    dx = r * (dxh - xh * jnp.mean(dxh * xh, axis=-1, keepdims=True))
    return dx, dy * xh


def _colsum(x):
    return jnp.sum(x, axis=0, keepdims=True)


def _prenorm_fwd(x, g):
    def fn(i, rv, hv, fv):
        (xb,), (gg,) = rv, fv
        return [xb * _rms(xb) * gg], []
    return _rows(fn, "prenorm_fwd", 256, [(x, D, 0)], fulls=[g], outs=[(D, BF16)])[0]


def _gm_mask():
    r = lax.broadcasted_iota(jnp.int32, (GM_B, GM_B), 0) // CHUNK
    c = lax.broadcasted_iota(jnp.int32, (GM_B, GM_B), 1) // CHUNK
    return c <= r


def _gm_norm(v, g, b):
    mu = jnp.mean(v, axis=-1, keepdims=True)
    vc = v - mu
    rs = lax.rsqrt(jnp.mean(vc * vc, axis=-1, keepdims=True) + EPS)
    vh = vc * rs
    return vh, rs, vh * g + b


def _gm_sv(vn, ws, bst):
    mask = _gm_mask()
    gw = GM_W // GM_G
    parts = []
    for g in range(GM_G):
        wm = jnp.where(mask, ws[g], 0.0).astype(BF16)
        parts.append(jnp.dot(wm, vn[:, g * gw:(g + 1) * gw].astype(BF16), preferred_element_type=F32)
                     + bst[:, g:g + 1])
    return jnp.concatenate(parts, axis=1)


def _gmlp_fwd(proj, ln_g, ln_b, ws, bst):
    def fn(i, rv, hv, fv):
        u, v, z = rv
        g, b, w, bt = fv
        _, _, vn = _gm_norm(v, g, b)
        return [u * _gm_sv(vn, w, bt) * _silu(z)], []
    return _rows(fn, "gmlp_fwd", GM_B, [(proj, GM_W, 0), (proj, GM_W, 1), (proj, GM_W, 2)],
                 fulls=[ln_g, ln_b, ws, bst], outs=[(GM_W, BF16)])[0]


def _mla_prep_fwd(proj, qg, kvg):
    def fn(i, rv, hv, fv):
        cq, ckv = rv
        g1, g2 = fv
        return [cq * _rms(cq) * g1, ckv * _rms(ckv) * g2], []
    return _rows(fn, "mla_prep_fwd", 256, [(proj, QR, O_CQ // QR), (proj, KVR, O_CKV // KVR)],
                 fulls=[qg, kvg], outs=[(QR, BF16), (KVR, BF16)])


def _rot(t, cc, sa, sb):
    return t * cc + pltpu.roll(t, 32, 1) * sa + pltpu.roll(t, 96, 1) * sb


def _rot_t(g, cc, sa, sb):
    return g * cc + pltpu.roll(g * sa, 96, 1) + pltpu.roll(g * sb, 32, 1)


def _rope_tables():
    pos = jnp.arange(S, dtype=F32)
    inv_freq = ROPE_THETA ** (-jnp.arange(0, ROPE, 2, dtype=F32) / ROPE)
    ang = pos[:, None] * inv_freq[None, :]
    cos, sin, z = jnp.cos(ang), jnp.sin(ang), jnp.zeros((S, 32), F32)
    cc = jnp.concatenate([cos, cos, z, z], axis=1)
    sa = jnp.concatenate([z, sin, z, z], axis=1)
    sb = jnp.concatenate([-sin, z, z, z], axis=1)
    return cc, sa, sb


ATT_SCALE = 1.0 / math.sqrt(NOPE + ROPE)


def _rope_fwd(q, kv, proj, tabs):
    def fn(i, rv, hv, fv):
        qb, kvb, kr, cc, sa, sb = rv
        krr = _rot(kr, cc, sa, sb)
        qs, ks = [], []
        for h in range(H):
            qs += [qb[:, h * HP:h * HP + 128] * ATT_SCALE, _rot(qb[:, h * HP + 128:(h + 1) * HP], cc, sa, sb) * ATT_SCALE]
            ks += [kvb[:, h * 128:(h + 1) * 128], krr]
        kc = jnp.concatenate(ks, axis=1)
        vv = kvb[:, H * NOPE:]
        return [jnp.concatenate(qs, axis=1), kc, kc, vv, vv], []
    cc, sa, sb = tabs
    return _rows(fn, "rope_fwd", 256,
                 [(q, H * HP, 0), (kv, H * 256, 0), (proj, 128, O_KR // 128), (cc, 128, 0), (sa, 128, 0), (sb, 128, 0)],
                 outs=[(H * HP, BF16), (H * HP, BF16), (H * HP, BF16, "T"), (MLA_W, BF16), (MLA_W, BF16, "T")])


TQ, TC, ATT_NB = 256, 128, 4
_NT = (((1,), (1,)), ((), ()))


def _attn_allowed(i, kc):
    kpos = kc * TC + lax.broadcasted_iota(jnp.int32, (TC, TQ), 0)
    qpos = i * TQ + lax.broadcasted_iota(jnp.int32, (TC, TQ), 1)
    return (kpos // CHUNK) <= (qpos // CHUNK)


def _attn_fwd(qc, kc, vt):
    def body(q_ref, k_ref, vt_ref, o_ref, l_ref):
        i = pl.program_id(1)
        q = q_ref[...]

        def scores(sb):
            t0s = [pl.multiple_of((sb * ATT_NB + c) * TC, TC) for c in range(ATT_NB)]
            return [lax.dot_general(k_ref[pl.ds(t0, TC), :], q, _NT, preferred_element_type=F32) for t0 in t0s]

        def block(sb, ss, carry, masked):
            m, l, acc = carry
            t0s = [pl.multiple_of((sb * ATT_NB + c) * TC, TC) for c in range(ATT_NB)]
            if masked:
                ss = [jnp.where(_attn_allowed(i, sb * ATT_NB + c), s, -1e30) for c, s in enumerate(ss)]
            m_new = m
            for s in ss:
                m_new = jnp.maximum(m_new, jnp.max(s, axis=0, keepdims=True))
            alpha = jnp.exp(m - m_new)
            ps = [jnp.exp(s - m_new) for s in ss]
            l = alpha * l
            acc = alpha * acc
            for t0, p in zip(t0s, ps):
                l = l + jnp.sum(p, axis=0, keepdims=True)
                acc = acc + jnp.dot(vt_ref[:, pl.ds(t0, TC)], p.astype(BF16), preferred_element_type=F32)
            return m_new, l, acc

        nsb = (i + 2) // 2
        c = (jnp.full((1, TQ), -1e30, F32), jnp.zeros((1, TQ), F32), jnp.zeros((VDIM, TQ), F32))

        def step(sb, sc):
            nxt = scores(sb + 1)
            return nxt, block(sb, sc[0], sc[1], False)

        ss, c = lax.fori_loop(0, nsb - 1, step, (scores(0), c))
        m, l, acc = block(nsb - 1, ss, c, True)
        o_ref[...] = (acc / l).T
        l_ref[...] = m + jnp.log(l)

    return pl.pallas_call(
        body, name="attn_fwd", grid=(H, S // TQ),
        in_specs=[pl.BlockSpec((TQ, HP), lambda h, i: (i, h)),
                  pl.BlockSpec((S, HP), lambda h, i: (0, h)),
                  pl.BlockSpec((VDIM, S), lambda h, i: (h, 0))],
        out_specs=[pl.BlockSpec((TQ, VDIM), lambda h, i: (i, h)), pl.BlockSpec((None, 1, TQ), lambda h, i: (h, 0, i))],
        out_shape=[jax.ShapeDtypeStruct((S, MLA_W), F32), jax.ShapeDtypeStruct((H, 1, S), F32)],
        compiler_params=pltpu.CompilerParams(dimension_semantics=("parallel", "arbitrary"),
                                             vmem_limit_bytes=VMEM_LIMIT),
    )(qc, kc, vt)


def _gate_mul_fwd(name, val, proj, width, cb):
    def fn(i, rv, hv, fv):
        o, z = rv
        return [o * _silu(z)], []
    return _rows(fn, name, 256, [(val, width, 0), (proj, width, cb)], outs=[(width, BF16)])[0]


def _conv_fwd(proj, w, b):
    def fn(i, rv, hv, fv):
        (xb,), (halo,), (ww, bb) = rv, hv, fv
        halo = jnp.where(i > 0, halo, 0.0)
        row = lax.broadcasted_iota(jnp.int32, xb.shape, 0)
        acc = bb + ww[3:4] * xb
        for s in range(1, CONV_W):
            acc = acc + ww[3 - s:4 - s] * _shift_down(xb, halo, s, row)
        return [acc, acc], []
    return _rows(fn, "conv_fwd", 128, [(proj, LRU_W, O_XC // LRU_W)], halos=[(proj, LRU_W, O_XC // LRU_W, "prev")],
                 fulls=[w, b], outs=[(LRU_W, F32), (LRU_W, BF16)])


def _lru_terms(ga, gx, xc, ba, bx, lam):
    r = _sig(ga + ba)
    ig = _sig(gx + bx)
    sp = jnp.maximum(-lam, 0.0) + jnp.log(1.0 + jnp.exp(-jnp.abs(lam)))
    log_a = -LRU_C * r * sp
    a = jnp.exp(log_a)
    e2 = jnp.exp(2.0 * log_a)
    om = 1.0 - e2
    mult = jnp.sqrt(jnp.maximum(om, 0.0))
    return r, ig, sp, a, e2, om, mult


def _lru_gates_fwd(gates, xc, ba, bx, lam):
    def fn(i, rv, hv, fv):
        ga, gx, x = rv
        r, ig, sp, a, e2, om, mult = _lru_terms(ga, gx, x, *fv)
        return [a, mult * (ig * x)], []
    return _rows(fn, "lru_gates_fwd", 128, [(gates, LRU_W, 0), (gates, LRU_W, 1), (xc, LRU_W, 0)],
                 fulls=[ba, bx, lam], outs=[(LRU_W, F32), (LRU_W, F32)])


SCAN_T, SCAN_CW = 64, 256


def _scan_fwd(a, b):
    def body(a_ref, b_ref, h_ref):
        row = lax.broadcasted_iota(jnp.int32, (SCAN_T, SCAN_CW), 0)

        def step(blk, hc):
            t0 = pl.multiple_of(blk * SCAN_T, SCAN_T)
            A = a_ref[pl.ds(t0, SCAN_T), :]
            B = b_ref[pl.ds(t0, SCAN_T), :]
            d = 1
            while d < SCAN_T:
                keep = row >= d
                A_s = jnp.where(keep, pltpu.roll(A, d, 0), 1.0)
                B_s = jnp.where(keep, pltpu.roll(B, d, 0), 0.0)
                B = A * B_s + B
                A = A * A_s
                d *= 2
            hh = A * hc + B
            h_ref[pl.ds(t0, SCAN_T), :] = hh
            return hh[SCAN_T - 1:SCAN_T, :]

        lax.fori_loop(0, S // SCAN_T, step, jnp.zeros((1, SCAN_CW), F32))

    spec = pl.BlockSpec((S, SCAN_CW), lambda j: (0, j))
    return pl.pallas_call(
        body, name="scan_fwd", grid=(LRU_W // SCAN_CW,), in_specs=[spec, spec], out_specs=spec,
        out_shape=jax.ShapeDtypeStruct((S, LRU_W), F32),
        compiler_params=pltpu.CompilerParams(dimension_semantics=("parallel",), vmem_limit_bytes=VMEM_LIMIT),
    )(a, b)


def _merge_fwd(pa, pb, pc, proj):
    def fn(i, rv, hv, fv):
        a, b, c, ga, gb, gc = rv
        return [_sig(ga) * a + _sig(gb) * b + _sig(gc) * c], []
    return _rows(fn, "merge_fwd", 256,
                 [(pa, D, 0), (pb, D, 0), (pc, D, 0), (proj, D, O_GA // D), (proj, D, O_GB // D), (proj, D, O_GC // D)],
                 outs=[(D, BF16)])[0]


def _post_fwd(x, o2, g):
    def fn(i, rv, hv, fv):
        xb, ob = rv
        return [xb + ob * _rms(ob) * fv[0]], []
    return _rows(fn, "post_fwd", 256, [(x, D, 0), (o2, D, 0)], fulls=[g], outs=[(D, F32)])[0]


SB = 640
BD_TM = 512


def _bd_fwd(xcb, wsb, l):
    def body(x_ref, w_ref, o_ref):
        o_ref[...] = jnp.dot(x_ref[...], w_ref[...], preferred_element_type=F32)

    return pl.pallas_call(
        body, name="lru_gate_mm", grid=(S // BD_TM, 4),
        in_specs=[pl.BlockSpec((BD_TM, SB), lambda i, q: (i, q % 2)),
                  pl.BlockSpec((None, None, SB, SB), lambda i, q: (l, q, 0, 0))],
        out_specs=pl.BlockSpec((BD_TM, SB), lambda i, q: (i, q)),
        out_shape=jax.ShapeDtypeStruct((S, 2 * LRU_W), F32),
        compiler_params=pltpu.CompilerParams(dimension_semantics=("parallel", "parallel"), vmem_limit_bytes=VMEM_LIMIT),
    )(xcb, wsb)


def _bd_dx(dgates, wsb, l):
    def body(d_ref, w_ref, o_ref, acc_ref):
        g = pl.program_id(2)

        @pl.when(g == 0)
        def _():
            acc_ref[...] = jnp.zeros_like(acc_ref)

        acc_ref[...] += lax.dot_general(d_ref[...], w_ref[...], (((1,), (1,)), ((), ())), preferred_element_type=F32)

        @pl.when(g == 1)
        def _():
            o_ref[...] = acc_ref[...]

    return pl.pallas_call(
        body, name="lru_gate_dx", grid=(S // BD_TM, 2, 2),
        in_specs=[pl.BlockSpec((BD_TM, SB), lambda i, s, g: (i, 2 * g + s)),
                  pl.BlockSpec((None, None, SB, SB), lambda i, s, g: (l, 2 * g + s, 0, 0))],
        out_specs=pl.BlockSpec((BD_TM, SB), lambda i, s, g: (i, s)),
        out_shape=jax.ShapeDtypeStruct((S, LRU_W), F32),
        scratch_shapes=[pltpu.VMEM((BD_TM, SB), F32)],
        compiler_params=pltpu.CompilerParams(dimension_semantics=("parallel", "parallel", "arbitrary"),
                                             vmem_limit_bytes=VMEM_LIMIT),
    )(dgates, wsb)


def _bd_dw(xcb, dgates):
    tk = 1024

    def body(x_ref, d_ref, o_ref):
        @pl.when(pl.program_id(1) == 0)
        def _():
            o_ref[...] = jnp.zeros_like(o_ref)

        o_ref[...] += lax.dot_general(x_ref[...], d_ref[...], (((0,), (0,)), ((), ())), preferred_element_type=F32)

    return pl.pallas_call(
        body, name="lru_gate_dw", grid=(4, S // tk),
        in_specs=[pl.BlockSpec((tk, SB), lambda q, k: (k, q % 2)), pl.BlockSpec((tk, SB), lambda q, k: (k, q))],
        out_specs=pl.BlockSpec((None, SB, SB), lambda q, k: (q, 0, 0)),
        out_shape=jax.ShapeDtypeStruct((4, SB, SB), F32),
        compiler_params=pltpu.CompilerParams(dimension_semantics=("parallel", "arbitrary"), vmem_limit_bytes=VMEM_LIMIT),
    )(xcb, dgates)


def _bd_extract(dwsb):
    def body(w_ref, o_ref):
        lane = lax.broadcasted_iota(jnp.int32, (LRU_BW, 128), 1)
        for q in range(4):
            for kk in range(8):
                c0 = LRU_BW * kk
                w0, off = (c0 // 128) * 128, c0 % 128
                rows = pl.ds(LRU_BW * kk, LRU_BW)
                blk = w_ref[q, rows, w0:w0 + 128]
                if off:
                    blk = pltpu.roll(blk, 128 - off, 1)
                    if off + LRU_BW > 128:
                        nxt = pltpu.roll(w_ref[q, rows, w0 + 128:w0 + 256], 128 - off, 1)
                        blk = jnp.where(lane < 128 - off, blk, nxt)
                o_ref[q // 2, 8 * (q % 2) + kk] = blk.astype(BF16)

    return pl.pallas_call(
        body, name="lru_gate_dw_blocks",
        in_specs=[pl.BlockSpec(memory_space=pltpu.VMEM)], out_specs=pl.BlockSpec(memory_space=pltpu.VMEM),
        out_shape=jax.ShapeDtypeStruct((2, LRU_NB, LRU_BW, 128), BF16),
        compiler_params=pltpu.CompilerParams(vmem_limit_bytes=VMEM_LIMIT),
    )(dwsb)


def _layer_fwd(x, P, l, tabs):
    A = {"x": x}
    A["h"] = _prenorm_fwd(x, P["pre_g"])
    proj = A["proj"] = _mm(A["h"], P["wp"], "nt", "in_proj", out_dtype=BF16, tm=1024)
    A["ya"] = _gmlp_fwd(proj, P["ln_g"], P["ln_b"], P["ws"], P["bst"])
    A["cqn"], A["ckvn"] = _mla_prep_fwd(proj, P["qg"], P["kvg"])
    q = _mm(A["cqn"], P["wuq"], "nt", "q_up")
    kv = _mm(A["ckvn"], P["wukv"], "nt", "kv_up")
    A["qc"], A["kc"], A["kct"], A["vv"], vt = _rope_fwd(q, kv, proj, tabs)
    A["o"], A["lse"] = _attn_fwd(A["qc"], A["kc"], vt)
    A["yb"] = _gate_mul_fwd("yb_fwd", A["o"], proj, MLA_W, O_ZB // MLA_W)
    A["xc"], A["xcb"] = _conv_fwd(proj, P["conv_w"], P["conv_b"])
    A["gates"] = _bd_fwd(A["xcb"], P["wsb"], l)
    A["a"], bterm = _lru_gates_fwd(A["gates"], A["xc"], P["ba"], P["bx"], P["lam"])
    A["hs"] = _scan_fwd(A["a"], bterm)
    A["yc"] = _gate_mul_fwd("yc_fwd", A["hs"], proj, LRU_W, O_ZC // LRU_W)
    A["pa"] = _mm(A["ya"], P["wpa"], "nn", "proj_a")
    A["pb"] = _mm(A["yb"], P["wpb"], "nn", "proj_b")
    A["pc"] = _mm(A["yc"], P["wpc"], "nn", "proj_c")
    A["merged"] = _merge_fwd(A["pa"], A["pb"], A["pc"], proj)
    A["o2"] = _mm(A["merged"], P["wout"], "nn", "out_proj")
    return _post_fwd(x, A["o2"], P["post_g"]), A


def _loss_fwd(y, tgt):
    def fn(i, rv, hv, fv):
        yb, tb = rv
        e = yb - tb
        part = 0.5 * jnp.sum(jnp.mean(e * e, axis=-1, keepdims=True), axis=0, keepdims=True)
        return [e * (1.0 / D)], [part]
    return _rows(fn, "loss", 256, [(y, D, 0), (tgt, D, 0)], outs=[(D, F32)], accs=[(1, 1)])


def _post_bwd(dxn, o2, g):
    def fn(i, rv, hv, fv):
        dy, ob = rv
        dx, dg = _rms_bwd(dy, ob, fv[0])
        return [dx], [_colsum(dg)]
    return _rows(fn, "post_bwd", 256, [(dxn, D, 0), (o2, D, 0)], fulls=[g], outs=[(D, BF16)], accs=[(1, D)])


def _merge_bwd(dm, pa, pb, pc, proj, dproj):
    def fn(i, rv, hv, fv):
        d, a, b, c, ga, gb, gc = rv
        outs_p, outs_g = [], []
        for p, gg in ((a, ga), (b, gb), (c, gc)):
            s = _sig(gg)
            outs_p.append(d * s)
            outs_g.append(d * p * s * (1.0 - s))
        return outs_p + [jnp.concatenate(outs_g, axis=1)], []
    return _rows(fn, "merge_bwd", 128,
                 [(dm, D, 0), (pa, D, 0), (pb, D, 0), (pc, D, 0),
                  (proj, D, O_GA // D), (proj, D, O_GB // D), (proj, D, O_GC // D)],
                 outs=[(D, BF16)] * 3 + [(3 * D, BF16, (dproj, NP, O_GA // (3 * D)))])


def _gmlp_bwd(dya, proj, ln_g, ln_b, ws, bst, dproj):
    gw = GM_W // GM_G

    def fn(i, rv, hv, fv):
        dy, u, v, z = rv
        g, b, w, bt = fv
        vh, rs, vn = _gm_norm(v, g, b)
        sv = _gm_sv(vn, w, bt)
        sz = _silu(z)
        du = dy * sv * sz
        dsv = dy * u * sz
        dz = dy * u * sv * _dsilu(z)
        mask = _gm_mask()
        lane = lax.broadcasted_iota(jnp.int32, (GM_B, 128), 1)
        dvn_parts, dws, dbst = [], [], jnp.zeros((GM_B, 128), F32)
        for k in range(GM_G):
            wm = jnp.where(mask, w[k], 0.0).astype(BF16)
            dsk = dsv[:, k * gw:(k + 1) * gw]
            dskb = dsk.astype(BF16)
            dvn_parts.append(lax.dot_general(wm, dskb, (((0,), (0,)), ((), ())), preferred_element_type=F32))
            dwk = lax.dot_general(dskb, vn[:, k * gw:(k + 1) * gw].astype(BF16), (((1,), (1,)), ((), ())),
                                  preferred_element_type=F32)
            dws.append(jnp.where(mask, dwk, 0.0)[None])
            dbst = dbst + jnp.where(lane == k, jnp.sum(dsk, axis=1, keepdims=True), 0.0)
        dvn = jnp.concatenate(dvn_parts, axis=1)
        dvh = dvn * g
        dv = rs * (dvh - jnp.mean(dvh, axis=-1, keepdims=True) - vh * jnp.mean(dvh * vh, axis=-1, keepdims=True))
        return ([jnp.concatenate([du, dv, dz], axis=1)],
                [jnp.concatenate(dws, axis=0), dbst, _colsum(dvn * vh), _colsum(dvn)])
    return _rows(fn, "gmlp_bwd", GM_B, [(dya, GM_W, 0), (proj, GM_W, 0), (proj, GM_W, 1), (proj, GM_W, 2)],
                 fulls=[ln_g, ln_b, ws, bst], outs=[(3 * GM_W, BF16, (dproj, NP, O_U // (3 * GM_W)))],
                 accs=[(GM_G, GM_B, GM_B), (GM_B, 128), (1, GM_W), (1, GM_W)])


def _yb_bwd(dyb, o, proj, dproj):
    def fn(i, rv, hv, fv):
        dy, ob, z = rv
        do = dy * _silu(z)
        prod = do * ob
        lane = lax.broadcasted_iota(jnp.int32, (dy.shape[0], 128), 1)
        dl = jnp.zeros((dy.shape[0], 128), F32)
        for h in range(H):
            dl = dl + jnp.where(lane == h, jnp.sum(prod[:, h * VDIM:(h + 1) * VDIM], axis=1, keepdims=True), 0.0)
        return [do, dl, dy * ob * _dsilu(z)], []
    return _rows(fn, "yb_bwd", 256, [(dyb, MLA_W, 0), (o, MLA_W, 0), (proj, MLA_W, O_ZB // MLA_W)],
                 outs=[(MLA_W, BF16), (128, F32, "T"), (MLA_W, BF16, (dproj, NP, O_ZB // MLA_W))])


def _attn_bwd(qc, kc, kct, vv, do, lse, dlt):
    def body(q_ref, k_ref, kt_ref, v_ref, do_ref, l_ref, d_ref, dq_ref, dk_ref, dv_ref, dqt_ref):
        h, i = pl.program_id(0), pl.program_id(1)

        @pl.when(i == 0)
        def _():
            dk_ref[...] = jnp.zeros_like(dk_ref)
            dv_ref[...] = jnp.zeros_like(dv_ref)

        q = q_ref[...]
        dob = do_ref[...]
        lse = l_ref[...]
        dl = d_ref[pl.ds(h, 1), :]
        dqt_ref[...] = jnp.zeros_like(dqt_ref)

        def rows_of(sb, c):
            return pl.ds(pl.multiple_of((sb * ATT_NB + c) * TC, TC), TC)

        def front(sb):
            return [(lax.dot_general(k_ref[rows_of(sb, c), :], q, _NT, preferred_element_type=F32),
                     lax.dot_general(v_ref[rows_of(sb, c), :], dob, _NT, preferred_element_type=F32))
                    for c in range(ATT_NB)]

        def block(sb, sd, masked):
            dqt = None
            for c, (s, dp) in enumerate(sd):
                rows = rows_of(sb, c)
                p = jnp.exp(s - lse)
                if masked:
                    p = jnp.where(_attn_allowed(i, sb * ATT_NB + c), p, 0.0)
                ds = (p * (dp - dl)).astype(BF16)
                dk_ref[rows, :] += jnp.dot(ds, q, preferred_element_type=F32)
                dv_ref[rows, :] += jnp.dot(p.astype(BF16), dob, preferred_element_type=F32)
                part = jnp.dot(kt_ref[:, rows], ds, preferred_element_type=F32)
                dqt = part if dqt is None else dqt + part
            dqt_ref[...] += dqt

        def step(sb, sd):
            nxt = front(sb + 1)
            block(sb, sd, False)
            return nxt

        nsb = (i + 2) // 2
        sd = lax.fori_loop(0, nsb - 1, step, front(0))
        block(nsb - 1, sd, True)
        dq_ref[...] = dqt_ref[...].T

    blk = lambda w: pl.BlockSpec((TQ, w), lambda h, i: (i, h))
    head = lambda w: pl.BlockSpec((S, w), lambda h, i: (0, h))
    return pl.pallas_call(
        body, name="attn_bwd", grid=(H, S // TQ),
        in_specs=[blk(HP), head(HP), pl.BlockSpec((HP, S), lambda h, i: (h, 0)), head(VDIM), blk(VDIM),
                  pl.BlockSpec((None, 1, TQ), lambda h, i: (h, 0, i)), pl.BlockSpec((8, TQ), lambda h, i: (0, i))],
        out_specs=[blk(HP), head(HP), head(VDIM)],
        out_shape=[jax.ShapeDtypeStruct((S, H * HP), F32), jax.ShapeDtypeStruct((S, H * HP), F32),
                   jax.ShapeDtypeStruct((S, MLA_W), F32)],
        scratch_shapes=[pltpu.VMEM((HP, TQ), F32)],
        compiler_params=pltpu.CompilerParams(dimension_semantics=("parallel", "arbitrary"),
                                             vmem_limit_bytes=VMEM_LIMIT),
    )(qc, kc, kct, vv, do, lse, dlt)


def _rope_bwd(dqc, dkc, dvv, tabs):
    def fn(i, rv, hv, fv):
        dq, dk, dv, cc, sa, sb = rv
        qs, ks = [], []
        dkr = jnp.zeros((dq.shape[0], 128), F32)
        for h in range(H):
            qs += [dq[:, h * HP:h * HP + 128] * ATT_SCALE, _rot_t(dq[:, h * HP + 128:(h + 1) * HP], cc, sa, sb) * ATT_SCALE]
            ks.append(dk[:, h * HP:h * HP + 128])
            dkr = dkr + dk[:, h * HP + 128:(h + 1) * HP]
        return [jnp.concatenate(qs, axis=1), jnp.concatenate(ks + [dv], axis=1), _rot_t(dkr, cc, sa, sb)], []
    cc, sa, sb = tabs
    return _rows(fn, "rope_bwd", 256,
                 [(dqc, H * HP, 0), (dkc, H * HP, 0), (dvv, MLA_W, 0), (cc, 128, 0), (sa, 128, 0), (sb, 128, 0)],
                 outs=[(H * HP, BF16), (H * 256, BF16), (128, BF16)])


MLA_GROUP = 1536


def _mla_prep_bwd(dcqn, dckvn, dkr, proj, qg, kvg, dproj):
    def fn(i, rv, hv, fv):
        d1, d2, dk, cq, ckv = rv
        g1, g2 = fv
        dx1, dg1 = _rms_bwd(d1, cq, g1)
        dx2, dg2 = _rms_bwd(d2, ckv, g2)
        zeros = jnp.zeros((d1.shape[0], MLA_GROUP - KVR - 128 - QR), F32)
        return [jnp.concatenate([dx2, dk.astype(F32), dx1, zeros], axis=1)], [_colsum(dg1), _colsum(dg2)]
    return _rows(fn, "mla_prep_bwd", 256,
                 [(dcqn, QR, 0), (dckvn, KVR, 0), (dkr, 128, 0), (proj, QR, O_CQ // QR), (proj, KVR, O_CKV // KVR)],
                 fulls=[qg, kvg], outs=[(MLA_GROUP, BF16, (dproj, NP, O_CKV // MLA_GROUP))], accs=[(1, QR), (1, KVR)])


def _yc_bwd(dyc, hs, proj, dproj):
    def fn(i, rv, hv, fv):
        dy, hh, z = rv
        return [dy * _silu(z), dy * hh * _dsilu(z)], []
    return _rows(fn, "yc_bwd", 128, [(dyc, LRU_W, 0), (hs, LRU_W, 0), (proj, LRU_W, O_ZC // LRU_W)],
                 outs=[(LRU_W, F32), (LRU_W, BF16, (dproj, NP, O_ZC // LRU_W))])


def _scan_bwd(a, hs, dh):
    nblk = S // SCAN_T

    def body(a_ref, h_ref, dh_ref, da_ref, db_ref):
        row = lax.broadcasted_iota(jnp.int32, (SCAN_T, SCAN_CW), 0)

        def step(j, carry):
            gc, ac = carry
            blk = nblk - 1 - j
            t0 = pl.multiple_of(blk * SCAN_T, SCAN_T)
            av = a_ref[pl.ds(t0, SCAN_T), :]
            A = jnp.where(row < SCAN_T - 1, pltpu.roll(av, SCAN_T - 1, 0), ac)
            B = dh_ref[pl.ds(t0, SCAN_T), :]
            d = 1
            while d < SCAN_T:
                keep = row < SCAN_T - d
                A_s = jnp.where(keep, pltpu.roll(A, SCAN_T - d, 0), 1.0)
                B_s = jnp.where(keep, pltpu.roll(B, SCAN_T - d, 0), 0.0)
                B = A * B_s + B
                A = A * A_s
                d *= 2
            g = A * gc + B
            p0 = pl.multiple_of(jnp.maximum(t0 - 8, 0), 8)
            last = jnp.where(blk > 0, h_ref[pl.ds(p0, 8), :][7:8, :], 0.0)
            h_prev = jnp.where(row >= 1, pltpu.roll(h_ref[pl.ds(t0, SCAN_T), :], 1, 0), last)
            da_ref[pl.ds(t0, SCAN_T), :] = g * h_prev
            db_ref[pl.ds(t0, SCAN_T), :] = g
            return g[0:1, :], av[0:1, :]

        z = jnp.zeros((1, SCAN_CW), F32)
        lax.fori_loop(0, nblk, step, (z, z))

    spec = pl.BlockSpec((S, SCAN_CW), lambda j: (0, j))
    return pl.pallas_call(
        body, name="scan_bwd", grid=(LRU_W // SCAN_CW,), in_specs=[spec] * 3, out_specs=[spec] * 2,
        out_shape=[jax.ShapeDtypeStruct((S, LRU_W), F32)] * 2,
        compiler_params=pltpu.CompilerParams(dimension_semantics=("parallel",), vmem_limit_bytes=VMEM_LIMIT),
    )(a, hs, dh)


def _lru_gates_bwd(da, db, gates, xc, ba, bx, lam):
    def fn(i, rv, hv, fv):
        dav, dbv, ga, gx, x = rv
        bav, bxv, lamv = fv
        r, ig, sp, a, e2, om, mult = _lru_terms(ga, gx, x, bav, bxv, lamv)
        dmult = dbv * ig * x
        dig = dbv * mult * x
        dxc1 = dbv * mult * ig
        dlog_a = dav * a + jnp.where(om > 0.0, dmult * (-e2 / mult), 0.0)
        dr = dlog_a * (-LRU_C * sp)
        dga = dr * r * (1.0 - r)
        dgx = dig * ig * (1.0 - ig)
        dlam = _colsum(dlog_a * (-LRU_C * r)) * (-_sig(-lamv))
        return [jnp.concatenate([dga, dgx], axis=1), dxc1], [_colsum(dga), _colsum(dgx), dlam]
    return _rows(fn, "lru_gates_bwd", 128,
                 [(da, LRU_W, 0), (db, LRU_W, 0), (gates, LRU_W, 0), (gates, LRU_W, 1), (xc, LRU_W, 0)],
                 fulls=[ba, bx, lam], outs=[(2 * LRU_W, BF16), (LRU_W, F32)], accs=[(1, LRU_W)] * 3)


def _conv_bwd(dxc1, dxc2, proj, w, dproj):
    cb = O_XC // LRU_W

    def fn(i, rv, hv, fv):
        d1, d2, xb = rv
        n1, n2, xprev = hv
        ww = fv[0]
        last = i == S // 128 - 1
        dxc = d1 + d2
        nxt = jnp.where(last, 0.0, n1 + n2)
        xprev = jnp.where(i > 0, xprev, 0.0)
        row = lax.broadcasted_iota(jnp.int32, xb.shape, 0)
        dx = ww[3:4] * dxc
        dws = [None] * CONV_W
        dws[3] = _colsum(dxc * xb)
        for s in range(1, CONV_W):
            dx = dx + ww[3 - s:4 - s] * _shift_up(dxc, nxt, s, row)
            dws[3 - s] = _colsum(dxc * _shift_down(xb, xprev, s, row))
        return [dx], [jnp.concatenate(dws, axis=0), _colsum(dxc)]
    return _rows(fn, "conv_bwd", 128, [(dxc1, LRU_W, 0), (dxc2, LRU_W, 0), (proj, LRU_W, cb)],
                 halos=[(dxc1, LRU_W, 0, "next"), (dxc2, LRU_W, 0, "next"), (proj, LRU_W, cb, "prev")],
                 fulls=[w], outs=[(LRU_W, BF16, (dproj, NP, cb))], accs=[(CONV_W, LRU_W), (1, LRU_W)])


def _prenorm_bwd(dxn, dh, x, g):
    def fn(i, rv, hv, fv):
        dy, dhh, xb = rv
        dx, dg = _rms_bwd(dhh, xb, fv[0])
        return [dy + dx], [_colsum(dg)]
    return _rows(fn, "prenorm_bwd", 256, [(dxn, D, 0), (dh, D, 0), (x, D, 0)], fulls=[g], outs=[(D, F32)],
                 accs=[(1, D)])


def _layer_bwd(dxn, A, P, l, tabs, GB):
    G = {}
    GB = dict(GB) if GB is not None else {}
    proj = A["proj"]

    def dw(key, a, b, name, **tiles):
        GB[key] = _mm(a, b, "tn", name, out_dtype=BF16, out_lead=(l, DEPTH, GB.get(key)), **tiles)

    do2, G["post_g"] = _post_bwd(dxn, A["o2"], P["post_g"])
    dm = _mm(do2, P["wout"], "nt", "out_proj_dx")
    dw("wout", A["merged"], do2, "out_proj_dw")
    dpa, dpb, dpc, dproj = _merge_bwd(dm, A["pa"], A["pb"], A["pc"], proj, None)
    dya = _mm(dpa, P["wpa"], "nt", "proj_a_dx")
    dw("wpa", A["ya"], dpa, "proj_a_dw")
    dyb = _mm(dpb, P["wpb"], "nt", "proj_b_dx")
    dw("wpb", A["yb"], dpb, "proj_b_dw")
    dyc = _mm(dpc, P["wpc"], "nt", "proj_c_dx")
    dw("wpc", A["yc"], dpc, "proj_c_dw")
    dproj, G["ws"], G["bst"], G["ln_g"], G["ln_b"] = _gmlp_bwd(dya, proj, P["ln_g"], P["ln_b"], P["ws"], P["bst"], dproj)
    do, dl, dproj = _yb_bwd(dyb, A["o"], proj, dproj)
    dqc, dkc, dvv = _attn_bwd(A["qc"], A["kc"], A["kct"], A["vv"], do, A["lse"], dl)
    dq, dkv, dkr = _rope_bwd(dqc, dkc, dvv, tabs)
    dcqn = _mm(dq, P["wuq"], "nn", "q_up_dx")
    dw("wuq", dq, A["cqn"], "q_up_dw")
    dckvn = _mm(dkv, P["wukv"], "nn", "kv_up_dx")
    dw("wukv", dkv, A["ckvn"], "kv_up_dw")
    dproj, G["qg"], G["kvg"] = _mla_prep_bwd(dcqn, dckvn, dkr, proj, P["qg"], P["kvg"], dproj)
    dhs, dproj = _yc_bwd(dyc, A["hs"], proj, dproj)
    da, db = _scan_bwd(A["a"], A["hs"], dhs)
    dgates, dxc1, G["ba"], G["bx"], G["lam"] = _lru_gates_bwd(da, db, A["gates"], A["xc"], P["ba"], P["bx"], P["lam"])
    dxc2 = _bd_dx(dgates, P["wsb"], l)
    G["wab"] = _bd_extract(_bd_dw(A["xcb"], dgates))
    dproj, G["conv_w"], G["conv_b"] = _conv_bwd(dxc1, dxc2, proj, P["conv_w"], dproj)
    dh = _mm(dproj, P["wp"], "nn", "in_proj_dx", tm=1024, tn=1024)
    dw("wp", dproj, A["h"], "in_proj_dw", tm=1536, tn=1024)
    dx, G["pre_g"] = _prenorm_bwd(dxn, dh, A["x"], P["pre_g"])
    return dx, G, GB


_ORIG_OFF = [0]
for _s in IN_SIZES:
    _ORIG_OFF.append(_ORIG_OFF[-1] + _s)
_PAD_OFF = {0: O_U, 1: O_V, 2: O_ZA, 3: O_CQ, 4: O_CKV, 5: O_KR, 6: O_ZB, 7: O_XC, 8: O_ZC, 9: O_GA, 10: O_GB, 11: O_GC}
SHARD_IN = N_IN // N_CHIPS


def _pieces_w_in(j):
    lo, hi = SHARD_IN * j, SHARD_IN * (j + 1)
    out = []
    for k in range(len(IN_SIZES)):
        a, b = max(lo, _ORIG_OFF[k]), min(hi, _ORIG_OFF[k + 1])
        if a < b:
            out.append((a - lo, _PAD_OFF[k] + a - _ORIG_OFF[k], b - a))
    return out


def _pieces_uq(j):
    return [(192 * hh, HP * (2 * j + hh), NOPE + ROPE) for hh in range(2)]


def _pieces_ukv(j):
    out = []
    for hh in range(2):
        h = 2 * j + hh
        out += [(256 * hh, NOPE * h, NOPE), (256 * hh + NOPE, H * NOPE + VDIM * h, VDIM)]
    return out


def _pieces_rows(r):
    return lambda j: [(0, r * j, r)]


LAYOUT = {
    "w_in": (SHARD_IN, NP, _pieces_w_in),
    "mla_w_uq": (2 * (NOPE + ROPE), H * HP, _pieces_uq),
    "mla_w_ukv": (2 * (NOPE + VDIM), 2 * H * 128, _pieces_ukv),
    "lru_conv_w": (1, N_CHIPS, _pieces_rows(1)),
    "w_proj_a": (GM_W // N_CHIPS, GM_W, _pieces_rows(GM_W // N_CHIPS)),
    "w_proj_b": (MLA_W // N_CHIPS, MLA_W, _pieces_rows(MLA_W // N_CHIPS)),
    "w_proj_c": (LRU_W // N_CHIPS, LRU_W, _pieces_rows(LRU_W // N_CHIPS)),
    "w_out": (D // N_CHIPS, D, _pieces_rows(D // N_CHIPS)),
}
TRANSPOSED = ("w_in", "mla_w_uq", "mla_w_ukv")


def _superblocks(w_a, w_x):
    w6 = jnp.stack([w_a, w_x], axis=1).reshape(DEPTH, 4, 8, LRU_BW, LRU_BW).astype(BF16)
    bands = [jnp.pad(w6[:, :, k], ((0, 0), (0, 0), (0, 0), (LRU_BW * k, SB - LRU_BW * (k + 1)))) for k in range(8)]
    return jnp.concatenate(bands, axis=2)


_HBM = pl.BlockSpec(memory_space=pltpu.HBM)


def _position():
    return lax.axis_index("x"), lax.axis_index("y"), lax.axis_index("c")


def _allgather(blocks, name):
    n = len(blocks)

    def body(*refs):
        ins, outs = refs[:n], refs[n:2 * n]
        send, recv, lsem = refs[2 * n:]
        x, y, c = _position()
        me, sib = (x, y, c), (x, y, 1 - c)
        chips = [(1 - x, y), (x, 1 - y), (1 - x, 1 - y)]

        def cp(k, a, block, to, src=None):
            dst = outs[a].at[4 * block[0] + 2 * block[1] + block[2]]
            return pltpu.make_async_remote_copy(src_ref=dst if src is None else src, dst_ref=dst,
                                                send_sem=send.at[7 * a + k], recv_sem=recv.at[7 * a + k],
                                                device_id=to, device_id_type=MESH)

        mine = [pltpu.make_async_copy(ins[a], outs[a].at[4 * x + 2 * y + c], lsem.at[a]) for a in range(n)]
        for m in mine:
            m.start()
        first = []
        for a in range(n):
            first.append(cp(0, a, me, sib, src=ins[a]))
            first += [cp(1 + j, a, me, (*chip, c), src=ins[a]) for j, chip in enumerate(chips)]
        for f in first:
            f.start()
        passed = []
        for j, chip in enumerate(chips):
            for a in range(n):
                cp(1 + j, a, (*chip, c), me).wait_recv()
                p = cp(4 + j, a, (*chip, c), sib)
                p.start()
                passed.append(p)
        for a in range(n):
            cp(0, a, sib, me).wait_recv()
            for j, chip in enumerate(chips):
                cp(4 + j, a, (*chip, 1 - c), me).wait_recv()
        for f in first + passed:
            f.wait_send()
        for m in mine:
            m.wait()

    return pl.pallas_call(
        body, name=name,
        out_shape=[jax.ShapeDtypeStruct((8,) + b.shape, b.dtype) for b in blocks],
        in_specs=[_HBM] * n, out_specs=[_HBM] * n,
        scratch_shapes=[pltpu.SemaphoreType.DMA((7 * n,)), pltpu.SemaphoreType.DMA((7 * n,)),
                        pltpu.SemaphoreType.DMA((n,))],
    )(*blocks)


_REL = (2, 1, 3)


def _cut(r):
    return r if r < 32 else (r // 2 + 15) // 16 * 16


def _half_rows(r, c0):
    return _cut(r) if c0 == 0 else r - _cut(r)


def _half_pieces(lay_a, jsrc, c0):
    r = lay_a[0]
    lo, hi = (0, _cut(r)) if c0 == 0 else (_cut(r), r)
    out = []
    for s0, d0, nr in lay_a[2](jsrc):
        a, b = max(s0, lo), min(s0 + nr, hi)
        if a < b:
            out.append((a, d0 + a - s0, b - a))
    return out


def _gather_zeros(names, srcs):
    return [jnp.zeros((LAYOUT[nm][1],) + s.shape[1:], s.dtype) for nm, s in zip(names, srcs)]


def _weights_allgather(names, srcs, name):
    n = len(srcs)
    lay = [LAYOUT[nm] for nm in names]
    zeros = _gather_zeros(names, srcs)

    def body(*refs):
        ins, outs = refs[:n], refs[2 * n:3 * n]
        send, recv, lsem = refs[3 * n:]
        x, y, c = _position()
        j = 2 * x + y
        sib = (x, y, 1 - c)
        chips = [(1 - x, y), (x, 1 - y), (1 - x, 1 - y)]

        def flow(a, k, jsrc, c0, to, from_src):
            cps = []
            for s0, d0, nr in _half_pieces(lay[a], jsrc, c0):
                dst = outs[a].at[pl.ds(d0, nr)]
                src = ins[a].at[pl.ds(s0, nr)] if from_src else dst
                cps.append(pltpu.make_async_remote_copy(src_ref=src, dst_ref=dst, send_sem=send.at[7 * a + k],
                                                        recv_sem=recv.at[7 * a + k], device_id=to, device_id_type=MESH))
            return cps

        def sized(a, k, rows):
            ref = ins[a].at[pl.ds(0, rows)]
            return pltpu.make_async_remote_copy(src_ref=ref, dst_ref=ref, send_sem=send.at[7 * a + k],
                                                recv_sem=recv.at[7 * a + k], device_id=sib, device_id_type=MESH)

        for j0 in range(N_CHIPS):
            for c0 in range(2):
                @pl.when((j == j0) & (c == c0))
                def _(j0=j0, c0=c0):
                    mine = [_half_rows(lay[a][0], c0) for a in range(n)]
                    theirs = [_half_rows(lay[a][0], 1 - c0) for a in range(n)]
                    for a in range(n):
                        for s0, d0, nr in _half_pieces(lay[a], j0, c0):
                            pltpu.make_async_copy(ins[a].at[pl.ds(s0, nr)], outs[a].at[pl.ds(d0, nr)], lsem.at[a]).start()
                    for a in range(n):
                        for cp in flow(a, 0, j0, c0, sib, True):
                            cp.start()
                        for k, chip in enumerate(chips):
                            for cp in flow(a, 1 + k, j0, c0, (*chip, c), True):
                                cp.start()
                    for k in range(3):
                        for a in range(n):
                            if mine[a]:
                                sized(a, 1 + k, mine[a]).wait_recv()
                                for cp in flow(a, 4 + k, j0 ^ _REL[k], c0, sib, False):
                                    cp.start()
                    for a in range(n):
                        if theirs[a]:
                            sized(a, 0, theirs[a]).wait_recv()
                            for k in range(3):
                                sized(a, 4 + k, theirs[a]).wait_recv()
                    for a in range(n):
                        if mine[a]:
                            for k in range(7):
                                sized(a, k, mine[a]).wait_send()
                            ref = ins[a].at[pl.ds(0, mine[a])]
                            pltpu.make_async_copy(ref, ref, lsem.at[a]).wait()

    return pl.pallas_call(
        body, name=name,
        out_shape=[jax.ShapeDtypeStruct(z.shape, z.dtype) for z in zeros],
        in_specs=[_HBM] * (2 * n), out_specs=[_HBM] * n,
        input_output_aliases={n + a: a for a in range(n)},
        scratch_shapes=[pltpu.SemaphoreType.DMA((7 * n,)), pltpu.SemaphoreType.DMA((7 * n,)),
                        pltpu.SemaphoreType.DMA((n,))],
    )(*srcs, *zeros)


_SEM = pl.BlockSpec(memory_space=pltpu.SEMAPHORE)
_EFFECT = pltpu.SideEffectType.DATAFLOW_SIDE_EFFECTING


def _gather_start(names, srcs, name):
    n = len(srcs)
    lay = [LAYOUT[nm] for nm in names]
    zeros = _gather_zeros(names, srcs)

    def body(*refs):
        ins, lands = refs[:n], refs[n:2 * n]
        send, recv, lsem = refs[2 * n:2 * n + 3]
        x, y, c = _position()
        j = 2 * x + y
        chips = [(1 - x, y), (x, 1 - y), (1 - x, 1 - y)]
        for j0 in range(N_CHIPS):
            @pl.when(j == j0)
            def _(j0=j0):
                for a in range(n):
                    for s0, d0, nr in lay[a][2](j0):
                        src, dst = ins[a].at[pl.ds(s0, nr)], lands[a].at[pl.ds(d0, nr)]
                        pltpu.make_async_copy(src, dst, lsem.at[a]).start()
                        for k, chip in enumerate(chips):
                            pltpu.make_async_remote_copy(src_ref=src, dst_ref=dst, send_sem=send.at[3 * a + k],
                                                         recv_sem=recv.at[3 * a + k], device_id=(*chip, c),
                                                         device_id_type=MESH).start()

    sems = [pltpu.SemaphoreType.DMA((3 * n,)), pltpu.SemaphoreType.DMA((3 * n,)), pltpu.SemaphoreType.DMA((n,))]
    hbm = lambda a: pltpu.HBM(a.shape, a.dtype)
    res = pl.pallas_call(
        body, name=name,
        out_shape=sems + [hbm(s) for s in srcs] + [hbm(z) for z in zeros],
        in_specs=[_HBM] * (2 * n), out_specs=[_SEM] * 3 + [_HBM] * (2 * n),
        input_output_aliases={a: 3 + a for a in range(2 * n)},
        compiler_params=pltpu.CompilerParams(has_side_effects=_EFFECT),
    )(*[pltpu.with_memory_space_constraint(s, pltpu.HBM) for s in srcs],
      *[pltpu.with_memory_space_constraint(z, pltpu.HBM) for z in zeros])
    return res[:3], res[3:3 + n], res[3 + n:]


def _gather_wait(names, sems, srcs, lands, after, name):
    n = len(srcs)
    lay = [LAYOUT[nm] for nm in names]

    def body(*refs):
        ins, zones = refs[:n], refs[n:2 * n]
        send, recv, lsem = refs[2 * n:2 * n + 3]
        x, y, c = _position()
        for a in range(n):
            whole = zones[a].at[pl.ds(0, lay[a][0])]
            for k in range(3):
                cp = pltpu.make_async_remote_copy(src_ref=ins[a], dst_ref=whole, send_sem=send.at[3 * a + k],
                                                  recv_sem=recv.at[3 * a + k], device_id=(x, y, 1 - c),
                                                  device_id_type=MESH)
                cp.wait_send()
                cp.wait_recv()
            pltpu.make_async_copy(ins[a], whole, lsem.at[a]).wait()

    hbm = lambda a: pltpu.HBM(a.shape, a.dtype)
    res = pl.pallas_call(
        body, name=name,
        out_shape=[hbm(s) for s in srcs] + [hbm(z) for z in lands],
        in_specs=[_HBM] * (2 * n) + [_SEM] * 3 + [pl.BlockSpec(memory_space=pl.ANY)], out_specs=[_HBM] * (2 * n),
        input_output_aliases={a: a for a in range(2 * n)},
        compiler_params=pltpu.CompilerParams(has_side_effects=_EFFECT),
    )(*srcs, *lands, *sems, after)
    return res[n:]


def _grads_to_sibling(gb, name):
    n = len(gb)

    def body(*refs):
        ins, outs = refs[:n], refs[n:2 * n]
        send, recv = refs[2 * n:]
        x, y, c = _position()
        cps = [pltpu.make_async_remote_copy(src_ref=ins[a].at[1 - c], dst_ref=outs[a], send_sem=send.at[a],
                                            recv_sem=recv.at[a], device_id=(x, y, 1 - c), device_id_type=MESH)
               for a in range(n)]
        for cp in cps:
            cp.start()
        for cp in cps:
            cp.wait()

    return pl.pallas_call(
        body, name=name,
        out_shape=[jax.ShapeDtypeStruct(g.shape[1:], g.dtype) for g in gb],
        in_specs=[_HBM] * n, out_specs=[_HBM] * n,
        scratch_shapes=[pltpu.SemaphoreType.DMA((n,)), pltpu.SemaphoreType.DMA((n,))],
    )(*gb)


def _chip_scatter(names, parts, name):
    n = len(parts)
    lay = [LAYOUT[nm] for nm in names]

    def body(*refs):
        ins, outs = refs[:n], refs[n:2 * n]
        send, recv, lsem = refs[2 * n:]
        x, y, c = _position()
        j = 2 * x + y
        chips = [(1 - x, y), (x, 1 - y), (1 - x, 1 - y)]

        def whole(a):
            return outs[a].at[0, pl.ds(0, lay[a][0])]

        for j0 in range(N_CHIPS):
            @pl.when(j == j0)
            def _(j0=j0):
                for a in range(n):
                    for s0, d0, nr in lay[a][2](j0):
                        pltpu.make_async_copy(ins[a].at[pl.ds(d0, nr)], outs[a].at[j0, pl.ds(s0, nr)], lsem.at[a]).start()
                    for k, chip in enumerate(chips):
                        for s0, d0, nr in lay[a][2](j0 ^ _REL[k]):
                            pltpu.make_async_remote_copy(
                                src_ref=ins[a].at[pl.ds(d0, nr)], dst_ref=outs[a].at[j0, pl.ds(s0, nr)],
                                send_sem=send.at[3 * a + k], recv_sem=recv.at[3 * a + k],
                                device_id=(*chip, c), device_id_type=MESH).start()

        for a in range(n):
            for k in range(3):
                pltpu.make_async_remote_copy(src_ref=whole(a), dst_ref=whole(a), send_sem=send.at[3 * a + k],
                                             recv_sem=recv.at[3 * a + k], device_id=(x, y, c), device_id_type=MESH).wait()
            pltpu.make_async_copy(whole(a), whole(a), lsem.at[a]).wait()

    return pl.pallas_call(
        body, name=name,
        out_shape=[jax.ShapeDtypeStruct((N_CHIPS, lay[a][0]) + parts[a].shape[1:], parts[a].dtype) for a in range(n)],
        in_specs=[_HBM] * n, out_specs=[_HBM] * n,
        scratch_shapes=[pltpu.SemaphoreType.DMA((3 * n,)), pltpu.SemaphoreType.DMA((3 * n,)),
                        pltpu.SemaphoreType.DMA((n,))],
    )(*parts)


def _reduced_exchange(bufs, name):
    n = len(bufs)

    def body(*refs):
        ins, outs = refs[:n], refs[n:2 * n]
        send, recv = refs[2 * n:]
        x, y, c = _position()
        cps = [pltpu.make_async_remote_copy(src_ref=outs[a].at[c], dst_ref=outs[a].at[c], send_sem=send.at[a],
                                            recv_sem=recv.at[a], device_id=(x, y, 1 - c), device_id_type=MESH)
               for a in range(n)]
        for cp in cps:
            cp.start()
        for cp in cps:
            cp.wait()

    return pl.pallas_call(
        body, name=name,
        out_shape=[jax.ShapeDtypeStruct(b.shape, b.dtype) for b in bufs],
        in_specs=[_HBM] * n, out_specs=[_HBM] * n, input_output_aliases={a: a for a in range(n)},
        scratch_shapes=[pltpu.SemaphoreType.DMA((n,)), pltpu.SemaphoreType.DMA((n,))],
    )(*bufs)


def _row_tile(r):
    for t in (256, 128, 64, 32, 16, 8):
        if r % t == 0 and r > t:
            return t
    return r


def _pair_add(g, rb, c_arr, name):
    R, rest = g.shape[1], g.shape[2:]
    tr = _row_tile(R)
    z = (0,) * len(rest)

    def body(c_ref, g_ref, r_ref, o_ref):
        o_ref[...] = (g_ref[...].astype(F32) + r_ref[...].astype(F32)).astype(o_ref.dtype)

    return pl.pallas_call(
        body, name=name,
        grid_spec=pltpu.PrefetchScalarGridSpec(
            num_scalar_prefetch=1, grid=(R // tr,),
            in_specs=[pl.BlockSpec((None, tr) + rest, lambda i, c_ref: (c_ref[0], i) + z),
                      pl.BlockSpec((tr,) + rest, lambda i, c_ref: (i,) + z)],
            out_specs=pl.BlockSpec((tr,) + rest, lambda i, c_ref: (i,) + z)),
        out_shape=jax.ShapeDtypeStruct((R,) + rest, BF16),
        compiler_params=pltpu.CompilerParams(dimension_semantics=("parallel",), vmem_limit_bytes=VMEM_LIMIT),
    )(c_arr, g, rb)


def _sum_slabs(rb, c_arr, name):
    n, R, rest = rb.shape[0], rb.shape[1], rb.shape[2:]
    tr = _row_tile(R)
    z = (0,) * len(rest)

    def body(c_ref, r_ref, o_ref):
        acc = r_ref[0].astype(F32)
        for k in range(1, n):
            acc = acc + r_ref[k].astype(F32)
        o_ref[...] = acc

    if R // tr > 64 and len(rest) == 1 and rest[0] % 256 == 0:
        grid = (rest[0] // 256,)
        in_spec = pl.BlockSpec((n, R, 256), lambda i, c_ref: (0, 0, i))
        out_spec = pl.BlockSpec((None, R, 256), lambda i, c_ref: (c_ref[0], 0, i))
    else:
        grid = (R // tr,)
        in_spec = pl.BlockSpec((n, tr) + rest, lambda i, c_ref: (0, i) + z)
        out_spec = pl.BlockSpec((None, tr) + rest, lambda i, c_ref: (c_ref[0], i) + z)
    return pl.pallas_call(
        body, name=name,
        grid_spec=pltpu.PrefetchScalarGridSpec(num_scalar_prefetch=1, grid=grid, in_specs=[in_spec], out_specs=out_spec),
        out_shape=jax.ShapeDtypeStruct((DEPTH, R) + rest, F32),
        compiler_params=pltpu.CompilerParams(dimension_semantics=("parallel",), vmem_limit_bytes=VMEM_LIMIT),
    )(c_arr, rb)


def _adam_math(w, g, m, v):
    mn = ADAM_B1 * m + (1.0 - ADAM_B1) * g
    vn = ADAM_B2 * v + (1.0 - ADAM_B2) * (g * g)
    m_hat = mn / (1.0 - ADAM_B1 ** ADAM_STEP)
    v_hat = vn / (1.0 - ADAM_B2 ** ADAM_STEP)
    return -ADAM_LR * (m_hat / (jnp.sqrt(v_hat) + ADAM_EPS) + ADAM_WD * w), mn, vn


def _adamw(w, g, m, v, name):
    L, R, C = w.shape
    tr = _row_tile(R)

    def body(w_ref, g_ref, m_ref, v_ref, d_ref, mo_ref, vo_ref):
        d_ref[...], mo_ref[...], vo_ref[...] = _adam_math(w_ref[...], g_ref[...], m_ref[...], v_ref[...])

    if R // tr > 64 and C % 128 == 0:
        spec, grid = pl.BlockSpec((None, R, 128), lambda l, i: (l, 0, i)), (L, C // 128)
    else:
        spec, grid = pl.BlockSpec((None, tr, C), lambda l, i: (l, i, 0)), (L, R // tr)
    return pl.pallas_call(
        body, name=name, grid=grid, in_specs=[spec] * 4, out_specs=[spec] * 3,
        out_shape=[jax.ShapeDtypeStruct((L, R, C), F32)] * 3,
        compiler_params=pltpu.CompilerParams(dimension_semantics=("parallel", "parallel"), vmem_limit_bytes=VMEM_LIMIT),
    )(w, g, m, v)


_VMEM_WHOLE = pl.BlockSpec(memory_space=pltpu.VMEM)


def _matrix_update(gath, w, m, v, name):
    K = w.shape[1]

    def body(g0_ref, g1_ref, w_ref, m_ref, v_ref, go_ref, d_ref, mo_ref, vo_ref):
        for l, gr in enumerate((g0_ref, g1_ref)):
            for k in range(K):
                g = gr[0, k].astype(F32)
                for dev in range(1, 8):
                    g = g + gr[dev, k].astype(F32)
                go_ref[l, k] = g
                d_ref[l, k], mo_ref[l, k], vo_ref[l, k] = _adam_math(w_ref[l, k], g, m_ref[l, k], v_ref[l, k])

    return pl.pallas_call(
        body, name=name, in_specs=[_VMEM_WHOLE] * 5, out_specs=[_VMEM_WHOLE] * 4,
        out_shape=[jax.ShapeDtypeStruct(w.shape, F32)] * 4,
        compiler_params=pltpu.CompilerParams(vmem_limit_bytes=VMEM_LIMIT),
    )(gath[0], gath[1], w, m, v)


VECS = (("pre_norm_g", D), ("post_norm_g", D), ("gm_ln_g", GM_W), ("gm_ln_b", GM_W), ("mla_q_norm_g", QR),
        ("mla_kv_norm_g", KVR), ("lru_conv_b", LRU_W), ("lru_b_a", LRU_W), ("lru_b_x", LRU_W), ("lru_lambda", LRU_W))
VEC_KEY = {"pre_norm_g": "pre_g", "post_norm_g": "post_g", "gm_ln_g": "ln_g", "gm_ln_b": "ln_b", "mla_q_norm_g": "qg",
           "mla_kv_norm_g": "kvg", "lru_conv_b": "conv_b", "lru_b_a": "ba", "lru_b_x": "bx", "lru_lambda": "lam"}
VEC_ROWS, VEC_W, VEC_ROW0, LOSS_ROW = 16, LRU_W, GM_G, 14


def _pack_rows(LG, loss_part):
    per = len(VECS) + 1
    ins = []
    for G in LG:
        ins += [G[VEC_KEY[n]] for n, _ in VECS] + [G["bst"]]
    ins.append(loss_part)

    def body(*refs):
        o_ref = refs[-1]
        o_ref[...] = jnp.zeros_like(o_ref)
        for l in range(DEPTH):
            base = VEC_ROWS * l
            o_ref[pl.ds(base, 8), pl.ds(0, GM_B)] = refs[per * l + len(VECS)][...].T[:8, :]
            for t, (_, width) in enumerate(VECS):
                o_ref[pl.ds(base + VEC_ROW0 + t, 1), pl.ds(0, width)] = refs[per * l + t][...]
        o_ref[pl.ds(LOSS_ROW, 1), pl.ds(0, 128)] = jnp.broadcast_to(refs[-2][...], (1, 128))

    return pl.pallas_call(
        body, name="pack_rows", in_specs=[_VMEM_WHOLE] * len(ins), out_specs=_VMEM_WHOLE,
        out_shape=jax.ShapeDtypeStruct((DEPTH * VEC_ROWS, VEC_W), F32),
    )(*ins)


def _vector_update(gath, W, M, V):
    names = [n for n, _ in VECS] + ["gm_bs"]
    nw = len(names)

    def body(*refs):
        g_ref = refs[0]
        wr, mr, vr = refs[1:1 + nw], refs[1 + nw:1 + 2 * nw], refs[1 + 2 * nw:1 + 3 * nw]
        outs = refs[1 + 3 * nw:]
        s = g_ref[0]
        for dev in range(1, 8):
            s = s + g_ref[dev]
        for t, (_, width) in enumerate(VECS):
            for l in range(DEPTH):
                r = VEC_ROWS * l + VEC_ROW0 + t
                g = s[r:r + 1, :width]
                row = (pl.ds(l, 1), slice(None))
                res = (g,) + _adam_math(wr[t][row], g, mr[t][row], vr[t][row])
                for q in range(4):
                    outs[4 * t + q][row] = res[q]
        t = len(VECS)
        for l in range(DEPTH):
            for k in range(GM_G):
                g = s[VEC_ROWS * l + k:VEC_ROWS * l + k + 1, :GM_B]
                row = (l, pl.ds(k, 1), slice(None))
                res = (g,) + _adam_math(wr[t][row], g, mr[t][row], vr[t][row])
                for q in range(4):
                    outs[4 * t + q][row] = res[q]
        outs[4 * nw][...] = s[LOSS_ROW:LOSS_ROW + 1, :128]

    ws = [W[n] for n in names]
    out_shape = []
    for w in ws:
        out_shape += [jax.ShapeDtypeStruct(w.shape, F32)] * 4
    out_shape.append(jax.ShapeDtypeStruct((1, 128), F32))
    res = pl.pallas_call(
        body, name="vector_update", in_specs=[_VMEM_WHOLE] * (1 + 3 * nw), out_specs=[_VMEM_WHOLE] * (4 * nw + 1),
        out_shape=out_shape, compiler_params=pltpu.CompilerParams(vmem_limit_bytes=VMEM_LIMIT),
    )(gath, *ws, *[M[n] for n in names], *[V[n] for n in names])
    return {n: tuple(res[4 * t:4 * t + 4]) for t, n in enumerate(names)}, res[4 * nw]


SHARDED = ("w_in", "mla_w_uq", "mla_w_ukv", "lru_conv_w", "w_proj_a", "w_proj_b", "w_proj_c", "w_out")
COL_SHARDED = ("w_in", "mla_w_uq", "mla_w_ukv", "lru_conv_w")
SMALL = ("pre_norm_g", "gm_ln_g", "gm_ln_b", "gm_ws", "gm_bs", "mla_q_norm_g", "mla_kv_norm_g", "lru_conv_b",
         "lru_w_a", "lru_b_a", "lru_w_x", "lru_b_x", "lru_lambda", "post_norm_g")
WEIGHTS = ("pre_norm_g", "w_in", "gm_ln_g", "gm_ln_b", "gm_ws", "gm_bs", "mla_q_norm_g", "mla_w_uq",
           "mla_kv_norm_g", "mla_w_ukv", "lru_conv_w", "lru_conv_b", "lru_w_a", "lru_b_a", "lru_w_x", "lru_b_x",
           "lru_lambda", "w_proj_a", "w_proj_b", "w_proj_c", "w_out", "post_norm_g")


GB_KEY = {"w_in": "wp", "mla_w_uq": "wuq", "mla_w_ukv": "wukv", "w_proj_a": "wpa", "w_proj_b": "wpb",
          "w_proj_c": "wpc", "w_out": "wout"}


def _prepare(l, gathered, small, wsb):
    P = {GB_KEY[n]: gathered[n] for n in GB_KEY}
    P["conv_w"] = gathered["lru_conv_w"].transpose(1, 0, 2).reshape(CONV_W, LRU_W)
    P["wsb"] = wsb
    row = lambda n: small[n][l][None, :]
    P["pre_g"], P["post_g"] = row("pre_norm_g"), row("post_norm_g")
    P["ln_g"], P["ln_b"] = row("gm_ln_g"), row("gm_ln_b")
    P["ws"] = small["gm_ws"][l]
    P["bst"] = jnp.pad(small["gm_bs"][l].T, ((0, 0), (0, 128 - GM_G)))
    P["qg"], P["kvg"] = row("mla_q_norm_g"), row("mla_kv_norm_g")
    P["conv_b"], P["ba"], P["bx"], P["lam"] = row("lru_conv_b"), row("lru_b_a"), row("lru_b_x"), row("lru_lambda")
    return P


def kernel(x, pre_norm_g, w_in, gm_ln_g, gm_ln_b, gm_ws, gm_bs, mla_q_norm_g, mla_w_uq, mla_kv_norm_g, mla_w_ukv, lru_conv_w, lru_conv_b, lru_w_a, lru_b_a, lru_w_x, lru_b_x, lru_lambda, w_proj_a, w_proj_b, w_proj_c, w_out, post_norm_g, loss_target, m_pre_norm_g, m_w_in, m_gm_ln_g, m_gm_ln_b, m_gm_ws, m_gm_bs, m_mla_q_norm_g, m_mla_w_uq, m_mla_kv_norm_g, m_mla_w_ukv, m_lru_conv_w, m_lru_conv_b, m_lru_w_a, m_lru_b_a, m_lru_w_x, m_lru_b_x, m_lru_lambda, m_w_proj_a, m_w_proj_b, m_w_proj_c, m_w_out, m_post_norm_g, v_pre_norm_g, v_w_in, v_gm_ln_g, v_gm_ln_b, v_gm_ws, v_gm_bs, v_mla_q_norm_g, v_mla_w_uq, v_mla_kv_norm_g, v_mla_w_ukv, v_lru_conv_w, v_lru_conv_b, v_lru_w_a, v_lru_b_a, v_lru_w_x, v_lru_b_x, v_lru_lambda, v_w_proj_a, v_w_proj_b, v_w_proj_c, v_w_out, v_post_norm_g):
    args = dict(locals())
    W = {n: args[n] for n in WEIGHTS}
    M = {n: args["m_" + n] for n in WEIGHTS}
    V = {n: args["v_" + n] for n in WEIGHTS}
    c = lax.axis_index("c")

    def shards(l):
        out = []
        for n in SHARDED:
            blk = W[n][l].T if n in TRANSPOSED else W[n][l]
            out.append(blk[None] if n == "lru_conv_w" else blk.astype(BF16))
        return out

    small = {n: W[n] for n in SMALL}
    wsb = _superblocks(W["lru_w_a"], W["lru_w_x"])
    tabs = _rope_tables()
    g0 = dict(zip(SHARDED, _weights_allgather(SHARDED, shards(0), "weights_allgather_l0")))
    sems, srcs1, lands1 = _gather_start(SHARDED, shards(1), "weights_gather_start_l1")

    P = [_prepare(0, g0, small, wsb), None]
    h0 = x[0]
    h1, A0 = _layer_fwd(h0, P[0], 0, tabs)
    g1 = dict(zip(SHARDED, _gather_wait(SHARDED, sems, srcs1, lands1, h1, "weights_gather_wait_l1")))
    P[1] = _prepare(1, g1, small, wsb)
    h2, A1 = _layer_fwd(h1, P[1], 1, tabs)
    dy, loss_part = _loss_fwd(h2, loss_target[0])
    d1, G1, GB = _layer_bwd(dy, A1, P[1], 1, tabs, None)
    d0, G0, GB = _layer_bwd(d1, A0, P[0], 0, tabs, GB)
    LG = (G0, G1)

    conv_g = jnp.stack([g["conv_w"].reshape(CONV_W, N_CHIPS, LRU_W // N_CHIPS).transpose(1, 0, 2) for g in LG])
    gb = [conv_g if n == "lru_conv_w" else GB[GB_KEY[n]] for n in SHARDED]
    from_sib = _grads_to_sibling(gb, "grads_to_sibling")
    c_arr = jnp.reshape(c, (1,)).astype(jnp.int32)
    pair = [_pair_add(g, rb, c_arr, "pair_add_" + n) for n, g, rb in zip(SHARDED, gb, from_sib)]
    slabs = _chip_scatter(SHARDED, pair, "grads_chip_scatter")
    mine = [_sum_slabs(s, c_arr, "sum_slabs_" + n) for n, s in zip(SHARDED, slabs)]
    both = _reduced_exchange(mine, "reduced_to_sibling")
    grads = {}
    for n, b in zip(SHARDED, both):
        if n in TRANSPOSED and n != "w_in":
            b = jnp.swapaxes(b, 1, 2)
        grads[n] = b if n == "w_in" else b.reshape(W[n].shape)

    rows = _pack_rows(LG, loss_part)
    mats = []
    for g in LG:
        mats += [g["ws"].astype(BF16), g["wab"][0, :, :, :LRU_BW], g["wab"][1, :, :, :LRU_BW]]
    gath = _allgather([rows] + mats, "small_grads_allgather")
    upd, loss_row = _vector_update(gath[0], W, M, V)
    loss = loss_row[0, 0]
    for k, n in enumerate(("gm_ws", "lru_w_a", "lru_w_x")):
        upd[n] = _matrix_update((gath[1 + k], gath[4 + k]), W[n], M[n], V[n], "update_" + n)

    for n in SHARDED:
        if n == "w_in":
            tr = lambda a: jnp.swapaxes(a, 1, 2)
            res = _adamw(tr(W[n]), grads[n], tr(M[n]), tr(V[n]), "adamw_" + n)
            upd[n] = tuple(tr(a) for a in (grads[n],) + tuple(res))
        else:
            upd[n] = (grads[n],) + tuple(_adamw(W[n], grads[n], M[n], V[n], "adamw_" + n))

    return (loss, d0[None], *[upd[n][0] for n in WEIGHTS], *[upd[n][1] for n in WEIGHTS],
            *[upd[n][2] for n in WEIGHTS], *[upd[n][3] for n in WEIGHTS])
```

```python
import functools
import math

import jax
import jax.numpy as jnp
from jax import lax
from jax.experimental import pallas as pl
from jax.experimental.pallas import tpu as pltpu

F32, BF16 = jnp.float32, jnp.bfloat16
MESH = pl.DeviceIdType.MESH

S, D, DEPTH = 2048, 1024, 2
CHUNK, EPS = 64, 1e-6
GM_W, GM_G, GM_B = 1024, 4, 128
H, NOPE, ROPE, VDIM = 8, 128, 64, 128
QR, KVR = 384, 256
MLA_W = H * VDIM
LRU_W, LRU_NB, LRU_BW, LRU_C, CONV_W = 1280, 16, 80, 8.0, 4
ROPE_THETA = 10000.0
IN_SIZES = (GM_W, GM_W, GM_W, QR, KVR, ROPE, MLA_W, LRU_W, LRU_W, D, D, D)
N_IN = sum(IN_SIZES)
N_CHIPS = 4
ADAM_LR, ADAM_B1, ADAM_B2, ADAM_EPS, ADAM_WD, ADAM_STEP = 0.001, 0.9, 0.999, 1e-08, 0.01, 10

HP = 256
O_U, O_V, O_ZA, O_GA, O_GB, O_GC = 0, 1024, 2048, 3072, 4096, 5120
O_CKV, O_KR, O_CQ, O_XC, O_ZC, O_ZB = 6144, 6400, 6528, 7680, 8960, 10240
NP = 11264
VMEM_LIMIT = 48 * 1024 * 1024


def _tile(dim, target):
    if dim <= target:
        return dim
    t = (target // 128) * 128
    while dim % t:
        t -= 128
    return t


def _sig(x):
    return jax.nn.sigmoid(x)


def _silu(x):
    return x * _sig(x)


def _dsilu(x):
    s = _sig(x)
    return s * (1.0 + x * (1.0 - s))


def _mm(a, b, mode, name, out_dtype=F32, tm=512, tn=512, tk=1024, b_lead=None, out_lead=None):
    b2 = b.shape[1:] if b_lead is not None else b.shape
    if mode == "nn":
        (M, K), (K2, N) = a.shape, b2
    elif mode == "nt":
        (M, K), (N, K2) = a.shape, b2
    else:
        (K, M), (K2, N) = a.shape, b2
    assert K == K2, (name, a.shape, b.shape)
    tm, tn, tk = _tile(M, tm), _tile(N, tn), _tile(K, tk)
    nk = K // tk
    if mode == "tn":
        a_spec = pl.BlockSpec((tk, tm), lambda i, j, k: (k, i))
        lhs_c = 0
    else:
        a_spec = pl.BlockSpec((tm, tk), lambda i, j, k: (i, k))
        lhs_c = 1
    b_blk, b_idx, rhs_c = ((tn, tk), (lambda i, j, k: (j, k)), 1) if mode == "nt" else ((tk, tn), (lambda i, j, k: (k, j)), 0)
    if b_lead is None:
        b_spec = pl.BlockSpec(b_blk, b_idx)
    else:
        b_spec = pl.BlockSpec((None,) + b_blk, functools.partial(lambda i, j, k, f, l: (l,) + f(i, j, k), f=b_idx, l=b_lead))
    dims = (((lhs_c,), (rhs_c,)), ((), ()))
    in_specs, args, aliases = [a_spec, b_spec], [a, b], {}
    if out_lead is None:
        out_spec = pl.BlockSpec((tm, tn), lambda i, j, k: (i, j))
        out_shape = jax.ShapeDtypeStruct((M, N), out_dtype)
    else:
        l_out, n_lead, buf = out_lead
        out_spec = pl.BlockSpec((None, tm, tn), functools.partial(lambda i, j, k, l: (l, i, j), l=l_out))
        out_shape = jax.ShapeDtypeStruct((n_lead, M, N), out_dtype)
        if buf is not None:
            in_specs.append(pl.BlockSpec(memory_space=pl.ANY))
            args.append(buf)
            aliases = {2: 0}

    def body(a_ref, b_ref, *rest):
        o_ref, acc_ref = rest[-2:]
        k = pl.program_id(2)

        @pl.when(k == 0)
        def _():
            acc_ref[...] = jnp.zeros_like(acc_ref)

        acc_ref[...] += lax.dot_general(a_ref[...].astype(BF16), b_ref[...].astype(BF16), dims,
                                        preferred_element_type=F32)

        @pl.when(k == nk - 1)
        def _():
            o_ref[...] = acc_ref[...].astype(o_ref.dtype)

    return pl.pallas_call(
        body, name=name, grid=(M // tm, N // tn, nk),
        in_specs=in_specs, out_specs=out_spec, out_shape=out_shape,
        scratch_shapes=[pltpu.VMEM((tm, tn), F32)], input_output_aliases=aliases,
        compiler_params=pltpu.CompilerParams(dimension_semantics=("parallel", "parallel", "arbitrary"),
                                             vmem_limit_bytes=VMEM_LIMIT),
    )(*args)


def _rows(fn, name, tm, rows, halos=(), fulls=(), outs=(), accs=()):
    n = S // tm
    in_specs, args = [], []
    for arr, w, cb in rows:
        in_specs.append(pl.BlockSpec((tm, w), functools.partial(lambda i, cb: (i, cb), cb=cb)))
        args.append(arr)
    for arr, w, cb, side in halos:
        if side == "prev":
            im = functools.partial(lambda i, cb: (jnp.maximum(i * (tm // 16) - 1, 0), cb), cb=cb)
        else:
            im = functools.partial(lambda i, cb: (jnp.minimum((i + 1) * (tm // 16), S // 16 - 1), cb), cb=cb)
        in_specs.append(pl.BlockSpec((16, w), im))
        args.append(arr)
    for arr in fulls:
        in_specs.append(pl.BlockSpec(arr.shape, functools.partial(lambda i, nd: (0,) * nd, nd=arr.ndim)))
        args.append(arr)
    out_shape, out_specs, aliases, n_alias = [], [], {}, 0
    for k, o in enumerate(outs):
        if len(o) == 3 and o[2] == "T":
            out_shape.append(jax.ShapeDtypeStruct((o[0], S), o[1]))
            out_specs.append(pl.BlockSpec((o[0], tm), lambda i: (0, i)))
        elif len(o) == 3:
            buf, total, cb = o[2]
            out_shape.append(jax.ShapeDtypeStruct((S, total), o[1]))
            out_specs.append(pl.BlockSpec((tm, o[0]), functools.partial(lambda i, cb: (i, cb), cb=cb)))
            if buf is not None:
                aliases[len(args)] = k
                in_specs.append(pl.BlockSpec(memory_space=pl.ANY))
                args.append(buf)
                n_alias += 1
        else:
            out_shape.append(jax.ShapeDtypeStruct((S, o[0]), o[1]))
            out_specs.append(pl.BlockSpec((tm, o[0]), lambda i: (i, 0)))
    for shp in accs:
        out_shape.append(jax.ShapeDtypeStruct(shp, F32))
        out_specs.append(pl.BlockSpec(shp, functools.partial(lambda i, nd: (0,) * nd, nd=len(shp))))
    nr, nh, nf, no, na = len(rows), len(halos), len(fulls), len(outs), len(accs)

    def body(*refs):
        i = pl.program_id(0)
        ins, orefs = refs[:nr + nh + nf], refs[nr + nh + nf + n_alias:]
        rv = [r[...].astype(F32) for r in ins[:nr]]
        hv = [r[...].astype(F32)[8:] if h[3] == "prev" else r[...].astype(F32)[:8] for r, h in zip(ins[nr:nr + nh], halos)]
        fv = [r[...] for r in ins[nr + nh:]]
        o, a = fn(i, rv, hv, fv)
        assert len(o) == no and len(a) == na, name
        for spec, ref, val in zip(outs, orefs[:no], o):
            ref[...] = (val.T if len(spec) == 3 and spec[2] == "T" else val).astype(ref.dtype)
        if na:
            @pl.when(i == 0)
            def _():
                for ref in orefs[no:]:
                    ref[...] = jnp.zeros_like(ref)

            for ref, val in zip(orefs[no:], a):
                ref[...] += val

    res = pl.pallas_call(
        body, name=name, grid=(n,), in_specs=in_specs, out_specs=out_specs, out_shape=out_shape,
        input_output_aliases=aliases,
        compiler_params=pltpu.CompilerParams(dimension_semantics=("arbitrary",), vmem_limit_bytes=VMEM_LIMIT),
    )(*args)
    return res


def _shift_down(xb, halo, s, row):
    fix = jnp.tile(pltpu.roll(halo, s, 0), (xb.shape[0] // 8, 1))
    return jnp.where(row >= s, pltpu.roll(xb, s, 0), fix)


def _shift_up(xb, halo, s, row):
    tm = xb.shape[0]
    fix = jnp.tile(pltpu.roll(halo, 8 - s, 0), (tm // 8, 1))
    return jnp.where(row < tm - s, pltpu.roll(xb, tm - s, 0), fix)


def _rms(x):
    return lax.rsqrt(jnp.mean(x * x, axis=-1, keepdims=True) + EPS)


def _rms_bwd(dy, x, g):
    r = _rms(x)
    xh = x * r
    dxh = dy * g
    dx = r * (dxh - xh * jnp.mean(dxh * xh, axis=-1, keepdims=True))
    return dx, dy * xh


def _colsum(x):
    return jnp.sum(x, axis=0, keepdims=True)


def _prenorm_fwd(x, g):
    def fn(i, rv, hv, fv):
        (xb,), (gg,) = rv, fv
        return [xb * _rms(xb) * gg], []
    return _rows(fn, "prenorm_fwd", 256, [(x, D, 0)], fulls=[g], outs=[(D, BF16)])[0]


def _gm_mask():
    r = lax.broadcasted_iota(jnp.int32, (GM_B, GM_B), 0) // CHUNK
    c = lax.broadcasted_iota(jnp.int32, (GM_B, GM_B), 1) // CHUNK
    return c <= r


def _gm_norm(v, g, b):
    mu = jnp.mean(v, axis=-1, keepdims=True)
    vc = v - mu
    rs = lax.rsqrt(jnp.mean(vc * vc, axis=-1, keepdims=True) + EPS)
    vh = vc * rs
    return vh, rs, vh * g + b


def _gm_sv(vn, ws, bst):
    mask = _gm_mask()
    gw = GM_W // GM_G
    parts = []
    for g in range(GM_G):
        wm = jnp.where(mask, ws[g], 0.0).astype(BF16)
        parts.append(jnp.dot(wm, vn[:, g * gw:(g + 1) * gw].astype(BF16), preferred_element_type=F32)
                     + bst[:, g:g + 1])
    return jnp.concatenate(parts, axis=1)


def _gmlp_fwd(proj, ln_g, ln_b, ws, bst):
    def fn(i, rv, hv, fv):
        u, v, z = rv
        g, b, w, bt = fv
        _, _, vn = _gm_norm(v, g, b)
        return [u * _gm_sv(vn, w, bt) * _silu(z)], []
    return _rows(fn, "gmlp_fwd", GM_B, [(proj, GM_W, 0), (proj, GM_W, 1), (proj, GM_W, 2)],
                 fulls=[ln_g, ln_b, ws, bst], outs=[(GM_W, BF16)])[0]


def _mla_prep_fwd(proj, qg, kvg):
    def fn(i, rv, hv, fv):
        cq, ckv = rv
        g1, g2 = fv
        return [cq * _rms(cq) * g1, ckv * _rms(ckv) * g2], []
    return _rows(fn, "mla_prep_fwd", 256, [(proj, QR, O_CQ // QR), (proj, KVR, O_CKV // KVR)],
                 fulls=[qg, kvg], outs=[(QR, BF16), (KVR, BF16)])


def _rot(t, cc, sa, sb):
    return t * cc + pltpu.roll(t, 32, 1) * sa + pltpu.roll(t, 96, 1) * sb


def _rot_t(g, cc, sa, sb):
    return g * cc + pltpu.roll(g * sa, 96, 1) + pltpu.roll(g * sb, 32, 1)


def _rope_tables():
    pos = jnp.arange(S, dtype=F32)
    inv_freq = ROPE_THETA ** (-jnp.arange(0, ROPE, 2, dtype=F32) / ROPE)
    ang = pos[:, None] * inv_freq[None, :]
    cos, sin, z = jnp.cos(ang), jnp.sin(ang), jnp.zeros((S, 32), F32)
    cc = jnp.concatenate([cos, cos, z, z], axis=1)
    sa = jnp.concatenate([z, sin, z, z], axis=1)
    sb = jnp.concatenate([-sin, z, z, z], axis=1)
    return cc, sa, sb


ATT_SCALE = 1.0 / math.sqrt(NOPE + ROPE)


def _rope_fwd(q, kv, proj, tabs):
    def fn(i, rv, hv, fv):
        qb, kvb, kr, cc, sa, sb = rv
        krr = _rot(kr, cc, sa, sb)
        qs, ks = [], []
        for h in range(H):
            qs += [qb[:, h * HP:h * HP + 128] * ATT_SCALE, _rot(qb[:, h * HP + 128:(h + 1) * HP], cc, sa, sb) * ATT_SCALE]
            ks += [kvb[:, h * 128:(h + 1) * 128], krr]
        kc = jnp.concatenate(ks, axis=1)
        vv = kvb[:, H * NOPE:]
        return [jnp.concatenate(qs, axis=1), kc, kc, vv, vv], []
    cc, sa, sb = tabs
    return _rows(fn, "rope_fwd", 256,
                 [(q, H * HP, 0), (kv, H * 256, 0), (proj, 128, O_KR // 128), (cc, 128, 0), (sa, 128, 0), (sb, 128, 0)],
                 outs=[(H * HP, BF16), (H * HP, BF16), (H * HP, BF16, "T"), (MLA_W, BF16), (MLA_W, BF16, "T")])


TQ, TC, ATT_NB = 256, 128, 4
_NT = (((1,), (1,)), ((), ()))


def _attn_allowed(i, kc):
    kpos = kc * TC + lax.broadcasted_iota(jnp.int32, (TC, TQ), 0)
    qpos = i * TQ + lax.broadcasted_iota(jnp.int32, (TC, TQ), 1)
    return (kpos // CHUNK) <= (qpos // CHUNK)


def _attn_fwd(qc, kc, vt):
    def body(q_ref, k_ref, vt_ref, o_ref, l_ref):
        i = pl.program_id(1)
        q = q_ref[...]

        def scores(sb):
            t0s = [pl.multiple_of((sb * ATT_NB + c) * TC, TC) for c in range(ATT_NB)]
            return [lax.dot_general(k_ref[pl.ds(t0, TC), :], q, _NT, preferred_element_type=F32) for t0 in t0s]

        def block(sb, ss, carry, masked):
            m, l, acc = carry
            t0s = [pl.multiple_of((sb * ATT_NB + c) * TC, TC) for c in range(ATT_NB)]
            if masked:
                ss = [jnp.where(_attn_allowed(i, sb * ATT_NB + c), s, -1e30) for c, s in enumerate(ss)]
            m_new = m
            for s in ss:
                m_new = jnp.maximum(m_new, jnp.max(s, axis=0, keepdims=True))
            alpha = jnp.exp(m - m_new)
            ps = [jnp.exp(s - m_new) for s in ss]
            l = alpha * l
            acc = alpha * acc
            for t0, p in zip(t0s, ps):
                l = l + jnp.sum(p, axis=0, keepdims=True)
                acc = acc + jnp.dot(vt_ref[:, pl.ds(t0, TC)], p.astype(BF16), preferred_element_type=F32)
            return m_new, l, acc

        nsb = (i + 2) // 2
        c = (jnp.full((1, TQ), -1e30, F32), jnp.zeros((1, TQ), F32), jnp.zeros((VDIM, TQ), F32))

        def step(sb, sc):
            nxt = scores(sb + 1)
            return nxt, block(sb, sc[0], sc[1], False)

        ss, c = lax.fori_loop(0, nsb - 1, step, (scores(0), c))
        m, l, acc = block(nsb - 1, ss, c, True)
        o_ref[...] = (acc / l).T
        l_ref[...] = m + jnp.log(l)

    return pl.pallas_call(
        body, name="attn_fwd", grid=(H, S // TQ),
        in_specs=[pl.BlockSpec((TQ, HP), lambda h, i: (i, h)),
                  pl.BlockSpec((S, HP), lambda h, i: (0, h)),
                  pl.BlockSpec((VDIM, S), lambda h, i: (h, 0))],
        out_specs=[pl.BlockSpec((TQ, VDIM), lambda h, i: (i, h)), pl.BlockSpec((None, 1, TQ), lambda h, i: (h, 0, i))],
        out_shape=[jax.ShapeDtypeStruct((S, MLA_W), F32), jax.ShapeDtypeStruct((H, 1, S), F32)],
        compiler_params=pltpu.CompilerParams(dimension_semantics=("parallel", "arbitrary"),
                                             vmem_limit_bytes=VMEM_LIMIT),
    )(qc, kc, vt)


def _gate_mul_fwd(name, val, proj, width, cb):
    def fn(i, rv, hv, fv):
        o, z = rv
        return [o * _silu(z)], []
    return _rows(fn, name, 256, [(val, width, 0), (proj, width, cb)], outs=[(width, BF16)])[0]


def _conv_fwd(proj, w, b):
    def fn(i, rv, hv, fv):
        (xb,), (halo,), (ww, bb) = rv, hv, fv
        halo = jnp.where(i > 0, halo, 0.0)
        row = lax.broadcasted_iota(jnp.int32, xb.shape, 0)
        acc = bb + ww[3:4] * xb
        for s in range(1, CONV_W):
            acc = acc + ww[3 - s:4 - s] * _shift_down(xb, halo, s, row)
        return [acc, acc], []
    return _rows(fn, "conv_fwd", 128, [(proj, LRU_W, O_XC // LRU_W)], halos=[(proj, LRU_W, O_XC // LRU_W, "prev")],
                 fulls=[w, b], outs=[(LRU_W, F32), (LRU_W, BF16)])


def _lru_terms(ga, gx, xc, ba, bx, lam):
    r = _sig(ga + ba)
    ig = _sig(gx + bx)
    sp = jnp.maximum(-lam, 0.0) + jnp.log(1.0 + jnp.exp(-jnp.abs(lam)))
    log_a = -LRU_C * r * sp
    a = jnp.exp(log_a)
    e2 = jnp.exp(2.0 * log_a)
    om = 1.0 - e2
    mult = jnp.sqrt(jnp.maximum(om, 0.0))
    return r, ig, sp, a, e2, om, mult


def _lru_gates_fwd(gates, xc, ba, bx, lam):
    def fn(i, rv, hv, fv):
        ga, gx, x = rv
        r, ig, sp, a, e2, om, mult = _lru_terms(ga, gx, x, *fv)
        return [a, mult * (ig * x)], []
    return _rows(fn, "lru_gates_fwd", 128, [(gates, LRU_W, 0), (gates, LRU_W, 1), (xc, LRU_W, 0)],
                 fulls=[ba, bx, lam], outs=[(LRU_W, F32), (LRU_W, F32)])


SCAN_T, SCAN_CW = 64, 256


def _scan_fwd(a, b):
    def body(a_ref, b_ref, h_ref):
        row = lax.broadcasted_iota(jnp.int32, (SCAN_T, SCAN_CW), 0)

        def step(blk, hc):
            t0 = pl.multiple_of(blk * SCAN_T, SCAN_T)
            A = a_ref[pl.ds(t0, SCAN_T), :]
            B = b_ref[pl.ds(t0, SCAN_T), :]
            d = 1
            while d < SCAN_T:
                keep = row >= d
                A_s = jnp.where(keep, pltpu.roll(A, d, 0), 1.0)
                B_s = jnp.where(keep, pltpu.roll(B, d, 0), 0.0)
                B = A * B_s + B
                A = A * A_s
                d *= 2
            hh = A * hc + B
            h_ref[pl.ds(t0, SCAN_T), :] = hh
            return hh[SCAN_T - 1:SCAN_T, :]

        lax.fori_loop(0, S // SCAN_T, step, jnp.zeros((1, SCAN_CW), F32))

    spec = pl.BlockSpec((S, SCAN_CW), lambda j: (0, j))
    return pl.pallas_call(
        body, name="scan_fwd", grid=(LRU_W // SCAN_CW,), in_specs=[spec, spec], out_specs=spec,
        out_shape=jax.ShapeDtypeStruct((S, LRU_W), F32),
        compiler_params=pltpu.CompilerParams(dimension_semantics=("parallel",), vmem_limit_bytes=VMEM_LIMIT),
    )(a, b)


def _merge_fwd(pa, pb, pc, proj):
    def fn(i, rv, hv, fv):
        a, b, c, ga, gb, gc = rv
        return [_sig(ga) * a + _sig(gb) * b + _sig(gc) * c], []
    return _rows(fn, "merge_fwd", 256,
                 [(pa, D, 0), (pb, D, 0), (pc, D, 0), (proj, D, O_GA // D), (proj, D, O_GB // D), (proj, D, O_GC // D)],
                 outs=[(D, BF16)])[0]


def _post_fwd(x, o2, g):
    def fn(i, rv, hv, fv):
        xb, ob = rv
        return [xb + ob * _rms(ob) * fv[0]], []
    return _rows(fn, "post_fwd", 256, [(x, D, 0), (o2, D, 0)], fulls=[g], outs=[(D, F32)])[0]


SB = 640
BD_TM = 512


def _bd_fwd(xcb, wsb, l):
    def body(x_ref, w_ref, o_ref):
        o_ref[...] = jnp.dot(x_ref[...], w_ref[...], preferred_element_type=F32)

    return pl.pallas_call(
        body, name="lru_gate_mm", grid=(S // BD_TM, 4),
        in_specs=[pl.BlockSpec((BD_TM, SB), lambda i, q: (i, q % 2)),
                  pl.BlockSpec((None, None, SB, SB), lambda i, q: (l, q, 0, 0))],
        out_specs=pl.BlockSpec((BD_TM, SB), lambda i, q: (i, q)),
        out_shape=jax.ShapeDtypeStruct((S, 2 * LRU_W), F32),
        compiler_params=pltpu.CompilerParams(dimension_semantics=("parallel", "parallel"), vmem_limit_bytes=VMEM_LIMIT),
    )(xcb, wsb)


def _bd_dx(dgates, wsb, l):
    def body(d_ref, w_ref, o_ref, acc_ref):
        g = pl.program_id(2)

        @pl.when(g == 0)
        def _():
            acc_ref[...] = jnp.zeros_like(acc_ref)

        acc_ref[...] += lax.dot_general(d_ref[...], w_ref[...], (((1,), (1,)), ((), ())), preferred_element_type=F32)

        @pl.when(g == 1)
        def _():
            o_ref[...] = acc_ref[...]

    return pl.pallas_call(
        body, name="lru_gate_dx", grid=(S // BD_TM, 2, 2),
        in_specs=[pl.BlockSpec((BD_TM, SB), lambda i, s, g: (i, 2 * g + s)),
                  pl.BlockSpec((None, None, SB, SB), lambda i, s, g: (l, 2 * g + s, 0, 0))],
        out_specs=pl.BlockSpec((BD_TM, SB), lambda i, s, g: (i, s)),
        out_shape=jax.ShapeDtypeStruct((S, LRU_W), F32),
        scratch_shapes=[pltpu.VMEM((BD_TM, SB), F32)],
        compiler_params=pltpu.CompilerParams(dimension_semantics=("parallel", "parallel", "arbitrary"),
                                             vmem_limit_bytes=VMEM_LIMIT),
    )(dgates, wsb)


def _bd_dw(xcb, dgates):
    tk = 1024

    def body(x_ref, d_ref, o_ref):
        @pl.when(pl.program_id(1) == 0)
        def _():
            o_ref[...] = jnp.zeros_like(o_ref)

        o_ref[...] += lax.dot_general(x_ref[...], d_ref[...], (((0,), (0,)), ((), ())), preferred_element_type=F32)

    return pl.pallas_call(
        body, name="lru_gate_dw", grid=(4, S // tk),
        in_specs=[pl.BlockSpec((tk, SB), lambda q, k: (k, q % 2)), pl.BlockSpec((tk, SB), lambda q, k: (k, q))],
        out_specs=pl.BlockSpec((None, SB, SB), lambda q, k: (q, 0, 0)),
        out_shape=jax.ShapeDtypeStruct((4, SB, SB), F32),
        compiler_params=pltpu.CompilerParams(dimension_semantics=("parallel", "arbitrary"), vmem_limit_bytes=VMEM_LIMIT),
    )(xcb, dgates)


def _bd_extract(dwsb):
    def body(w_ref, o_ref):
        lane = lax.broadcasted_iota(jnp.int32, (LRU_BW, 128), 1)
        for q in range(4):
            for kk in range(8):
                c0 = LRU_BW * kk
                w0, off = (c0 // 128) * 128, c0 % 128
                rows = pl.ds(LRU_BW * kk, LRU_BW)
                blk = w_ref[q, rows, w0:w0 + 128]
                if off:
                    blk = pltpu.roll(blk, 128 - off, 1)
                    if off + LRU_BW > 128:
                        nxt = pltpu.roll(w_ref[q, rows, w0 + 128:w0 + 256], 128 - off, 1)
                        blk = jnp.where(lane < 128 - off, blk, nxt)
                o_ref[q // 2, 8 * (q % 2) + kk] = blk.astype(BF16)

    return pl.pallas_call(
        body, name="lru_gate_dw_blocks",
        in_specs=[pl.BlockSpec(memory_space=pltpu.VMEM)], out_specs=pl.BlockSpec(memory_space=pltpu.VMEM),
        out_shape=jax.ShapeDtypeStruct((2, LRU_NB, LRU_BW, 128), BF16),
        compiler_params=pltpu.CompilerParams(vmem_limit_bytes=VMEM_LIMIT),
    )(dwsb)


def _layer_fwd(x, P, l, tabs):
    A = {"x": x}
    A["h"] = _prenorm_fwd(x, P["pre_g"])
    proj = A["proj"] = _mm(A["h"], P["wp"], "nt", "in_proj", out_dtype=BF16, tm=1024)
    A["ya"] = _gmlp_fwd(proj, P["ln_g"], P["ln_b"], P["ws"], P["bst"])
    A["cqn"], A["ckvn"] = _mla_prep_fwd(proj, P["qg"], P["kvg"])
    q = _mm(A["cqn"], P["wuq"], "nt", "q_up")
    kv = _mm(A["ckvn"], P["wukv"], "nt", "kv_up")
    A["qc"], A["kc"], A["kct"], A["vv"], vt = _rope_fwd(q, kv, proj, tabs)
    A["o"], A["lse"] = _attn_fwd(A["qc"], A["kc"], vt)
    A["yb"] = _gate_mul_fwd("yb_fwd", A["o"], proj, MLA_W, O_ZB // MLA_W)
    A["xc"], A["xcb"] = _conv_fwd(proj, P["conv_w"], P["conv_b"])
    A["gates"] = _bd_fwd(A["xcb"], P["wsb"], l)
    A["a"], bterm = _lru_gates_fwd(A["gates"], A["xc"], P["ba"], P["bx"], P["lam"])
    A["hs"] = _scan_fwd(A["a"], bterm)
    A["yc"] = _gate_mul_fwd("yc_fwd", A["hs"], proj, LRU_W, O_ZC // LRU_W)
    A["pa"] = _mm(A["ya"], P["wpa"], "nn", "proj_a")
    A["pb"] = _mm(A["yb"], P["wpb"], "nn", "proj_b")
    A["pc"] = _mm(A["yc"], P["wpc"], "nn", "proj_c")
    A["merged"] = _merge_fwd(A["pa"], A["pb"], A["pc"], proj)
    A["o2"] = _mm(A["merged"], P["wout"], "nn", "out_proj")
    return _post_fwd(x, A["o2"], P["post_g"]), A


def _loss_fwd(y, tgt):
    def fn(i, rv, hv, fv):
        yb, tb = rv
        e = yb - tb
        part = 0.5 * jnp.sum(jnp.mean(e * e, axis=-1, keepdims=True), axis=0, keepdims=True)
        return [e * (1.0 / D)], [part]
    return _rows(fn, "loss", 256, [(y, D, 0), (tgt, D, 0)], outs=[(D, F32)], accs=[(1, 1)])


def _post_bwd(dxn, o2, g):
    def fn(i, rv, hv, fv):
        dy, ob = rv
        dx, dg = _rms_bwd(dy, ob, fv[0])
        return [dx], [_colsum(dg)]
    return _rows(fn, "post_bwd", 256, [(dxn, D, 0), (o2, D, 0)], fulls=[g], outs=[(D, BF16)], accs=[(1, D)])


def _merge_bwd(dm, pa, pb, pc, proj, dproj):
    def fn(i, rv, hv, fv):
        d, a, b, c, ga, gb, gc = rv
        outs_p, outs_g = [], []
        for p, gg in ((a, ga), (b, gb), (c, gc)):
            s = _sig(gg)
            outs_p.append(d * s)
            outs_g.append(d * p * s * (1.0 - s))
        return outs_p + [jnp.concatenate(outs_g, axis=1)], []
    return _rows(fn, "merge_bwd", 128,
                 [(dm, D, 0), (pa, D, 0), (pb, D, 0), (pc, D, 0),
                  (proj, D, O_GA // D), (proj, D, O_GB // D), (proj, D, O_GC // D)],
                 outs=[(D, BF16)] * 3 + [(3 * D, BF16, (dproj, NP, O_GA // (3 * D)))])


def _gmlp_bwd(dya, proj, ln_g, ln_b, ws, bst, dproj):
    gw = GM_W // GM_G

    def fn(i, rv, hv, fv):
        dy, u, v, z = rv
        g, b, w, bt = fv
        vh, rs, vn = _gm_norm(v, g, b)
        sv = _gm_sv(vn, w, bt)
        sz = _silu(z)
        du = dy * sv * sz
        dsv = dy * u * sz
        dz = dy * u * sv * _dsilu(z)
        mask = _gm_mask()
        lane = lax.broadcasted_iota(jnp.int32, (GM_B, 128), 1)
        dvn_parts, dws, dbst = [], [], jnp.zeros((GM_B, 128), F32)
        for k in range(GM_G):
            wm = jnp.where(mask, w[k], 0.0).astype(BF16)
            dsk = dsv[:, k * gw:(k + 1) * gw]
            dskb = dsk.astype(BF16)
            dvn_parts.append(lax.dot_general(wm, dskb, (((0,), (0,)), ((), ())), preferred_element_type=F32))
            dwk = lax.dot_general(dskb, vn[:, k * gw:(k + 1) * gw].astype(BF16), (((1,), (1,)), ((), ())),
                                  preferred_element_type=F32)
            dws.append(jnp.where(mask, dwk, 0.0)[None])
            dbst = dbst + jnp.where(lane == k, jnp.sum(dsk, axis=1, keepdims=True), 0.0)
        dvn = jnp.concatenate(dvn_parts, axis=1)
        dvh = dvn * g
        dv = rs * (dvh - jnp.mean(dvh, axis=-1, keepdims=True) - vh * jnp.mean(dvh * vh, axis=-1, keepdims=True))
        return ([jnp.concatenate([du, dv, dz], axis=1)],
                [jnp.concatenate(dws, axis=0), dbst, _colsum(dvn * vh), _colsum(dvn)])
    return _rows(fn, "gmlp_bwd", GM_B, [(dya, GM_W, 0), (proj, GM_W, 0), (proj, GM_W, 1), (proj, GM_W, 2)],
                 fulls=[ln_g, ln_b, ws, bst], outs=[(3 * GM_W, BF16, (dproj, NP, O_U // (3 * GM_W)))],
                 accs=[(GM_G, GM_B, GM_B), (GM_B, 128), (1, GM_W), (1, GM_W)])


def _yb_bwd(dyb, o, proj, dproj):
    def fn(i, rv, hv, fv):
        dy, ob, z = rv
        do = dy * _silu(z)
        prod = do * ob
        lane = lax.broadcasted_iota(jnp.int32, (dy.shape[0], 128), 1)
        dl = jnp.zeros((dy.shape[0], 128), F32)
        for h in range(H):
            dl = dl + jnp.where(lane == h, jnp.sum(prod[:, h * VDIM:(h + 1) * VDIM], axis=1, keepdims=True), 0.0)
        return [do, dl, dy * ob * _dsilu(z)], []
    return _rows(fn, "yb_bwd", 256, [(dyb, MLA_W, 0), (o, MLA_W, 0), (proj, MLA_W, O_ZB // MLA_W)],
                 outs=[(MLA_W, BF16), (128, F32, "T"), (MLA_W, BF16, (dproj, NP, O_ZB // MLA_W))])


def _attn_bwd(qc, kc, kct, vv, do, lse, dlt):
    def body(q_ref, k_ref, kt_ref, v_ref, do_ref, l_ref, d_ref, dq_ref, dk_ref, dv_ref, dqt_ref):
        h, i = pl.program_id(0), pl.program_id(1)

        @pl.when(i == 0)
        def _():
            dk_ref[...] = jnp.zeros_like(dk_ref)
            dv_ref[...] = jnp.zeros_like(dv_ref)

        q = q_ref[...]
        dob = do_ref[...]
        lse = l_ref[...]
        dl = d_ref[pl.ds(h, 1), :]
        dqt_ref[...] = jnp.zeros_like(dqt_ref)

        def rows_of(sb, c):
            return pl.ds(pl.multiple_of((sb * ATT_NB + c) * TC, TC), TC)

        def front(sb):
            return [(lax.dot_general(k_ref[rows_of(sb, c), :], q, _NT, preferred_element_type=F32),
                     lax.dot_general(v_ref[rows_of(sb, c), :], dob, _NT, preferred_element_type=F32))
                    for c in range(ATT_NB)]

        def block(sb, sd, masked):
            dqt = None
            for c, (s, dp) in enumerate(sd):
                rows = rows_of(sb, c)
                p = jnp.exp(s - lse)
                if masked:
                    p = jnp.where(_attn_allowed(i, sb * ATT_NB + c), p, 0.0)
                ds = (p * (dp - dl)).astype(BF16)
                dk_ref[rows, :] += jnp.dot(ds, q, preferred_element_type=F32)
                dv_ref[rows, :] += jnp.dot(p.astype(BF16), dob, preferred_element_type=F32)
                part = jnp.dot(kt_ref[:, rows], ds, preferred_element_type=F32)
                dqt = part if dqt is None else dqt + part
            dqt_ref[...] += dqt

        def step(sb, sd):
            nxt = front(sb + 1)
            block(sb, sd, False)
            return nxt

        nsb = (i + 2) // 2
        sd = lax.fori_loop(0, nsb - 1, step, front(0))
        block(nsb - 1, sd, True)
        dq_ref[...] = dqt_ref[...].T

    blk = lambda w: pl.BlockSpec((TQ, w), lambda h, i: (i, h))
    head = lambda w: pl.BlockSpec((S, w), lambda h, i: (0, h))
    return pl.pallas_call(
        body, name="attn_bwd", grid=(H, S // TQ),
        in_specs=[blk(HP), head(HP), pl.BlockSpec((HP, S), lambda h, i: (h, 0)), head(VDIM), blk(VDIM),
                  pl.BlockSpec((None, 1, TQ), lambda h, i: (h, 0, i)), pl.BlockSpec((8, TQ), lambda h, i: (0, i))],
        out_specs=[blk(HP), head(HP), head(VDIM)],
        out_shape=[jax.ShapeDtypeStruct((S, H * HP), F32), jax.ShapeDtypeStruct((S, H * HP), F32),
                   jax.ShapeDtypeStruct((S, MLA_W), F32)],
        scratch_shapes=[pltpu.VMEM((HP, TQ), F32)],
        compiler_params=pltpu.CompilerParams(dimension_semantics=("parallel", "arbitrary"),
                                             vmem_limit_bytes=VMEM_LIMIT),
    )(qc, kc, kct, vv, do, lse, dlt)


def _rope_bwd(dqc, dkc, dvv, tabs):
    def fn(i, rv, hv, fv):
        dq, dk, dv, cc, sa, sb = rv
        qs, ks = [], []
        dkr = jnp.zeros((dq.shape[0], 128), F32)
        for h in range(H):
            qs += [dq[:, h * HP:h * HP + 128] * ATT_SCALE, _rot_t(dq[:, h * HP + 128:(h + 1) * HP], cc, sa, sb) * ATT_SCALE]
            ks.append(dk[:, h * HP:h * HP + 128])
            dkr = dkr + dk[:, h * HP + 128:(h + 1) * HP]
        return [jnp.concatenate(qs, axis=1), jnp.concatenate(ks + [dv], axis=1), _rot_t(dkr, cc, sa, sb)], []
    cc, sa, sb = tabs
    return _rows(fn, "rope_bwd", 256,
                 [(dqc, H * HP, 0), (dkc, H * HP, 0), (dvv, MLA_W, 0), (cc, 128, 0), (sa, 128, 0), (sb, 128, 0)],
                 outs=[(H * HP, BF16), (H * 256, BF16), (128, BF16)])


MLA_GROUP = 1536


def _mla_prep_bwd(dcqn, dckvn, dkr, proj, qg, kvg, dproj):
    def fn(i, rv, hv, fv):
        d1, d2, dk, cq, ckv = rv
        g1, g2 = fv
        dx1, dg1 = _rms_bwd(d1, cq, g1)
        dx2, dg2 = _rms_bwd(d2, ckv, g2)
        zeros = jnp.zeros((d1.shape[0], MLA_GROUP - KVR - 128 - QR), F32)
        return [jnp.concatenate([dx2, dk.astype(F32), dx1, zeros], axis=1)], [_colsum(dg1), _colsum(dg2)]
    return _rows(fn, "mla_prep_bwd", 256,
                 [(dcqn, QR, 0), (dckvn, KVR, 0), (dkr, 128, 0), (proj, QR, O_CQ // QR), (proj, KVR, O_CKV // KVR)],
                 fulls=[qg, kvg], outs=[(MLA_GROUP, BF16, (dproj, NP, O_CKV // MLA_GROUP))], accs=[(1, QR), (1, KVR)])


def _yc_bwd(dyc, hs, proj, dproj):
    def fn(i, rv, hv, fv):
        dy, hh, z = rv
        return [dy * _silu(z), dy * hh * _dsilu(z)], []
    return _rows(fn, "yc_bwd", 128, [(dyc, LRU_W, 0), (hs, LRU_W, 0), (proj, LRU_W, O_ZC // LRU_W)],
                 outs=[(LRU_W, F32), (LRU_W, BF16, (dproj, NP, O_ZC // LRU_W))])


def _scan_bwd(a, hs, dh):
    nblk = S // SCAN_T

    def body(a_ref, h_ref, dh_ref, da_ref, db_ref):
        row = lax.broadcasted_iota(jnp.int32, (SCAN_T, SCAN_CW), 0)

        def step(j, carry):
            gc, ac = carry
            blk = nblk - 1 - j
            t0 = pl.multiple_of(blk * SCAN_T, SCAN_T)
            av = a_ref[pl.ds(t0, SCAN_T), :]
            A = jnp.where(row < SCAN_T - 1, pltpu.roll(av, SCAN_T - 1, 0), ac)
            B = dh_ref[pl.ds(t0, SCAN_T), :]
            d = 1
            while d < SCAN_T:
                keep = row < SCAN_T - d
                A_s = jnp.where(keep, pltpu.roll(A, SCAN_T - d, 0), 1.0)
                B_s = jnp.where(keep, pltpu.roll(B, SCAN_T - d, 0), 0.0)
                B = A * B_s + B
                A = A * A_s
                d *= 2
            g = A * gc + B
            p0 = pl.multiple_of(jnp.maximum(t0 - 8, 0), 8)
            last = jnp.where(blk > 0, h_ref[pl.ds(p0, 8), :][7:8, :], 0.0)
            h_prev = jnp.where(row >= 1, pltpu.roll(h_ref[pl.ds(t0, SCAN_T), :], 1, 0), last)
            da_ref[pl.ds(t0, SCAN_T), :] = g * h_prev
            db_ref[pl.ds(t0, SCAN_T), :] = g
            return g[0:1, :], av[0:1, :]

        z = jnp.zeros((1, SCAN_CW), F32)
        lax.fori_loop(0, nblk, step, (z, z))

    spec = pl.BlockSpec((S, SCAN_CW), lambda j: (0, j))
    return pl.pallas_call(
        body, name="scan_bwd", grid=(LRU_W // SCAN_CW,), in_specs=[spec] * 3, out_specs=[spec] * 2,
        out_shape=[jax.ShapeDtypeStruct((S, LRU_W), F32)] * 2,
        compiler_params=pltpu.CompilerParams(dimension_semantics=("parallel",), vmem_limit_bytes=VMEM_LIMIT),
    )(a, hs, dh)


def _lru_gates_bwd(da, db, gates, xc, ba, bx, lam):
    def fn(i, rv, hv, fv):
        dav, dbv, ga, gx, x = rv
        bav, bxv, lamv = fv
        r, ig, sp, a, e2, om, mult = _lru_terms(ga, gx, x, bav, bxv, lamv)
        dmult = dbv * ig * x
        dig = dbv * mult * x
        dxc1 = dbv * mult * ig
        dlog_a = dav * a + jnp.where(om > 0.0, dmult * (-e2 / mult), 0.0)
        dr = dlog_a * (-LRU_C * sp)
        dga = dr * r * (1.0 - r)
        dgx = dig * ig * (1.0 - ig)
        dlam = _colsum(dlog_a * (-LRU_C * r)) * (-_sig(-lamv))
        return [jnp.concatenate([dga, dgx], axis=1), dxc1], [_colsum(dga), _colsum(dgx), dlam]
    return _rows(fn, "lru_gates_bwd", 128,
                 [(da, LRU_W, 0), (db, LRU_W, 0), (gates, LRU_W, 0), (gates, LRU_W, 1), (xc, LRU_W, 0)],
                 fulls=[ba, bx, lam], outs=[(2 * LRU_W, BF16), (LRU_W, F32)], accs=[(1, LRU_W)] * 3)


def _conv_bwd(dxc1, dxc2, proj, w, dproj):
    cb = O_XC // LRU_W

    def fn(i, rv, hv, fv):
        d1, d2, xb = rv
        n1, n2, xprev = hv
        ww = fv[0]
        last = i == S // 128 - 1
        dxc = d1 + d2
        nxt = jnp.where(last, 0.0, n1 + n2)
        xprev = jnp.where(i > 0, xprev, 0.0)
        row = lax.broadcasted_iota(jnp.int32, xb.shape, 0)
        dx = ww[3:4] * dxc
        dws = [None] * CONV_W
        dws[3] = _colsum(dxc * xb)
        for s in range(1, CONV_W):
            dx = dx + ww[3 - s:4 - s] * _shift_up(dxc, nxt, s, row)
            dws[3 - s] = _colsum(dxc * _shift_down(xb, xprev, s, row))
        return [dx], [jnp.concatenate(dws, axis=0), _colsum(dxc)]
    return _rows(fn, "conv_bwd", 128, [(dxc1, LRU_W, 0), (dxc2, LRU_W, 0), (proj, LRU_W, cb)],
                 halos=[(dxc1, LRU_W, 0, "next"), (dxc2, LRU_W, 0, "next"), (proj, LRU_W, cb, "prev")],
                 fulls=[w], outs=[(LRU_W, BF16, (dproj, NP, cb))], accs=[(CONV_W, LRU_W), (1, LRU_W)])


def _prenorm_bwd(dxn, dh, x, g):
    def fn(i, rv, hv, fv):
        dy, dhh, xb = rv
        dx, dg = _rms_bwd(dhh, xb, fv[0])
        return [dy + dx], [_colsum(dg)]
    return _rows(fn, "prenorm_bwd", 256, [(dxn, D, 0), (dh, D, 0), (x, D, 0)], fulls=[g], outs=[(D, F32)],
                 accs=[(1, D)])


def _layer_bwd(dxn, A, P, l, tabs):
    G, GB = {}, {}
    proj = A["proj"]

    def dw(key, a, b, name, **tiles):
        GB[key] = _mm(a, b, "tn", name, out_dtype=BF16, **tiles)

    do2, G["post_g"] = _post_bwd(dxn, A["o2"], P["post_g"])
    dm = _mm(do2, P["wout"], "nt", "out_proj_dx")
    dw("wout", A["merged"], do2, "out_proj_dw")
    dpa, dpb, dpc, dproj = _merge_bwd(dm, A["pa"], A["pb"], A["pc"], proj, None)
    dya = _mm(dpa, P["wpa"], "nt", "proj_a_dx")
    dw("wpa", A["ya"], dpa, "proj_a_dw")
    dyb = _mm(dpb, P["wpb"], "nt", "proj_b_dx")
    dw("wpb", A["yb"], dpb, "proj_b_dw")
    dyc = _mm(dpc, P["wpc"], "nt", "proj_c_dx")
    dw("wpc", A["yc"], dpc, "proj_c_dw")
    dproj, G["ws"], G["bst"], G["ln_g"], G["ln_b"] = _gmlp_bwd(dya, proj, P["ln_g"], P["ln_b"], P["ws"], P["bst"], dproj)
    do, dl, dproj = _yb_bwd(dyb, A["o"], proj, dproj)
    dqc, dkc, dvv = _attn_bwd(A["qc"], A["kc"], A["kct"], A["vv"], do, A["lse"], dl)
    dq, dkv, dkr = _rope_bwd(dqc, dkc, dvv, tabs)
    dcqn = _mm(dq, P["wuq"], "nn", "q_up_dx")
    dw("wuq", dq, A["cqn"], "q_up_dw")
    dckvn = _mm(dkv, P["wukv"], "nn", "kv_up_dx")
    dw("wukv", dkv, A["ckvn"], "kv_up_dw")
    dproj, G["qg"], G["kvg"] = _mla_prep_bwd(dcqn, dckvn, dkr, proj, P["qg"], P["kvg"], dproj)
    dhs, dproj = _yc_bwd(dyc, A["hs"], proj, dproj)
    da, db = _scan_bwd(A["a"], A["hs"], dhs)
    dgates, dxc1, G["ba"], G["bx"], G["lam"] = _lru_gates_bwd(da, db, A["gates"], A["xc"], P["ba"], P["bx"], P["lam"])
    dxc2 = _bd_dx(dgates, P["wsb"], l)
    G["wab"] = _bd_extract(_bd_dw(A["xcb"], dgates))
    dproj, G["conv_w"], G["conv_b"] = _conv_bwd(dxc1, dxc2, proj, P["conv_w"], dproj)
    dh = _mm(dproj, P["wp"], "nn", "in_proj_dx", tm=1024, tn=1024)
    dw("wp", dproj, A["h"], "in_proj_dw", tm=1536, tn=1024)
    dx, G["pre_g"] = _prenorm_bwd(dxn, dh, A["x"], P["pre_g"])
    return dx, G, GB


_ORIG_OFF = [0]
for _s in IN_SIZES:
    _ORIG_OFF.append(_ORIG_OFF[-1] + _s)
_PAD_OFF = {0: O_U, 1: O_V, 2: O_ZA, 3: O_CQ, 4: O_CKV, 5: O_KR, 6: O_ZB, 7: O_XC, 8: O_ZC, 9: O_GA, 10: O_GB, 11: O_GC}
SHARD_IN = N_IN // N_CHIPS


def _pieces_w_in(j):
    lo, hi = SHARD_IN * j, SHARD_IN * (j + 1)
    out = []
    for k in range(len(IN_SIZES)):
        a, b = max(lo, _ORIG_OFF[k]), min(hi, _ORIG_OFF[k + 1])
        if a < b:
            out.append((a - lo, _PAD_OFF[k] + a - _ORIG_OFF[k], b - a))
    return out


def _pieces_uq(j):
    return [(192 * hh, HP * (2 * j + hh), NOPE + ROPE) for hh in range(2)]


def _pieces_ukv(j):
    out = []
    for hh in range(2):
        h = 2 * j + hh
        out += [(256 * hh, NOPE * h, NOPE), (256 * hh + NOPE, H * NOPE + VDIM * h, VDIM)]
    return out


def _pieces_rows(r):
    return lambda j: [(0, r * j, r)]


LAYOUT = {
    "w_in": (SHARD_IN, NP, _pieces_w_in),
    "mla_w_uq": (2 * (NOPE + ROPE), H * HP, _pieces_uq),
    "mla_w_ukv": (2 * (NOPE + VDIM), 2 * H * 128, _pieces_ukv),
    "lru_conv_w": (1, N_CHIPS, _pieces_rows(1)),
    "w_proj_a": (GM_W // N_CHIPS, GM_W, _pieces_rows(GM_W // N_CHIPS)),
    "w_proj_b": (MLA_W // N_CHIPS, MLA_W, _pieces_rows(MLA_W // N_CHIPS)),
    "w_proj_c": (LRU_W // N_CHIPS, LRU_W, _pieces_rows(LRU_W // N_CHIPS)),
    "w_out": (D // N_CHIPS, D, _pieces_rows(D // N_CHIPS)),
}
TRANSPOSED = ("w_in", "mla_w_uq", "mla_w_ukv")


def _superblocks(w_a, w_x):
    w6 = jnp.stack([w_a, w_x], axis=1).reshape(DEPTH, 4, 8, LRU_BW, LRU_BW).astype(BF16)
    bands = [jnp.pad(w6[:, :, k], ((0, 0), (0, 0), (0, 0), (LRU_BW * k, SB - LRU_BW * (k + 1)))) for k in range(8)]
    return jnp.concatenate(bands, axis=2)


_HBM = pl.BlockSpec(memory_space=pltpu.HBM)


def _position():
    return lax.axis_index("x"), lax.axis_index("y"), lax.axis_index("c")


def _allgather(blocks, name):
    n = len(blocks)

    def body(*refs):
        ins, outs = refs[:n], refs[n:2 * n]
        send, recv, lsem = refs[2 * n:]
        x, y, c = _position()
        me, sib = (x, y, c), (x, y, 1 - c)
        chips = [(1 - x, y), (x, 1 - y), (1 - x, 1 - y)]

        def cp(k, a, block, to, src=None):
            dst = outs[a].at[4 * block[0] + 2 * block[1] + block[2]]
            return pltpu.make_async_remote_copy(src_ref=dst if src is None else src, dst_ref=dst,
                                                send_sem=send.at[7 * a + k], recv_sem=recv.at[7 * a + k],
                                                device_id=to, device_id_type=MESH)

        mine = [pltpu.make_async_copy(ins[a], outs[a].at[4 * x + 2 * y + c], lsem.at[a]) for a in range(n)]
        for m in mine:
            m.start()
        first = []
        for a in range(n):
            first.append(cp(0, a, me, sib, src=ins[a]))
            first += [cp(1 + j, a, me, (*chip, c), src=ins[a]) for j, chip in enumerate(chips)]
        for f in first:
            f.start()
        passed = []
        for j, chip in enumerate(chips):
            for a in range(n):
                cp(1 + j, a, (*chip, c), me).wait_recv()
                p = cp(4 + j, a, (*chip, c), sib)
                p.start()
                passed.append(p)
        for a in range(n):
            cp(0, a, sib, me).wait_recv()
            for j, chip in enumerate(chips):
                cp(4 + j, a, (*chip, 1 - c), me).wait_recv()
        for f in first + passed:
            f.wait_send()
        for m in mine:
            m.wait()

    return pl.pallas_call(
        body, name=name,
        out_shape=[jax.ShapeDtypeStruct((8,) + b.shape, b.dtype) for b in blocks],
        in_specs=[_HBM] * n, out_specs=[_HBM] * n,
        scratch_shapes=[pltpu.SemaphoreType.DMA((7 * n,)), pltpu.SemaphoreType.DMA((7 * n,)),
                        pltpu.SemaphoreType.DMA((n,))],
    )(*blocks)


_REL = (2, 1, 3)


def _cut(r):
    return r if r < 32 else (r // 2 + 15) // 16 * 16


def _half_rows(r, c0):
    return _cut(r) if c0 == 0 else r - _cut(r)


def _half_pieces(lay_a, jsrc, c0):
    r = lay_a[0]
    lo, hi = (0, _cut(r)) if c0 == 0 else (_cut(r), r)
    out = []
    for s0, d0, nr in lay_a[2](jsrc):
        a, b = max(s0, lo), min(s0 + nr, hi)
        if a < b:
            out.append((a, d0 + a - s0, b - a))
    return out


def _gather_zeros(names, srcs):
    return [jnp.zeros((LAYOUT[nm][1],) + s.shape[1:], s.dtype) for nm, s in zip(names, srcs)]


def _weights_allgather(names, srcs, name):
    n = len(srcs)
    lay = [LAYOUT[nm] for nm in names]
    zeros = _gather_zeros(names, srcs)

    def body(*refs):
        ins, outs = refs[:n], refs[2 * n:3 * n]
        send, recv, lsem = refs[3 * n:]
        x, y, c = _position()
        j = 2 * x + y
        sib = (x, y, 1 - c)
        chips = [(1 - x, y), (x, 1 - y), (1 - x, 1 - y)]

        def flow(a, k, jsrc, c0, to, from_src):
            cps = []
            for s0, d0, nr in _half_pieces(lay[a], jsrc, c0):
                dst = outs[a].at[pl.ds(d0, nr)]
                src = ins[a].at[pl.ds(s0, nr)] if from_src else dst
                cps.append(pltpu.make_async_remote_copy(src_ref=src, dst_ref=dst, send_sem=send.at[7 * a + k],
                                                        recv_sem=recv.at[7 * a + k], device_id=to, device_id_type=MESH))
            return cps

        def sized(a, k, rows):
            ref = ins[a].at[pl.ds(0, rows)]
            return pltpu.make_async_remote_copy(src_ref=ref, dst_ref=ref, send_sem=send.at[7 * a + k],
                                                recv_sem=recv.at[7 * a + k], device_id=sib, device_id_type=MESH)

        for j0 in range(N_CHIPS):
            for c0 in range(2):
                @pl.when((j == j0) & (c == c0))
                def _(j0=j0, c0=c0):
                    mine = [_half_rows(lay[a][0], c0) for a in range(n)]
                    theirs = [_half_rows(lay[a][0], 1 - c0) for a in range(n)]
                    for a in range(n):
                        for s0, d0, nr in _half_pieces(lay[a], j0, c0):
                            pltpu.make_async_copy(ins[a].at[pl.ds(s0, nr)], outs[a].at[pl.ds(d0, nr)], lsem.at[a]).start()
                    for a in range(n):
                        for cp in flow(a, 0, j0, c0, sib, True):
                            cp.start()
                        for k, chip in enumerate(chips):
                            for cp in flow(a, 1 + k, j0, c0, (*chip, c), True):
                                cp.start()
                    for k in range(3):
                        for a in range(n):
                            if mine[a]:
                                sized(a, 1 + k, mine[a]).wait_recv()
                                for cp in flow(a, 4 + k, j0 ^ _REL[k], c0, sib, False):
                                    cp.start()
                    for a in range(n):
                        if theirs[a]:
                            sized(a, 0, theirs[a]).wait_recv()
                            for k in range(3):
                                sized(a, 4 + k, theirs[a]).wait_recv()
                    for a in range(n):
                        if mine[a]:
                            for k in range(7):
                                sized(a, k, mine[a]).wait_send()
                            ref = ins[a].at[pl.ds(0, mine[a])]
                            pltpu.make_async_copy(ref, ref, lsem.at[a]).wait()

    return pl.pallas_call(
        body, name=name,
        out_shape=[jax.ShapeDtypeStruct(z.shape, z.dtype) for z in zeros],
        in_specs=[_HBM] * (2 * n), out_specs=[_HBM] * n,
        input_output_aliases={n + a: a for a in range(n)},
        scratch_shapes=[pltpu.SemaphoreType.DMA((7 * n,)), pltpu.SemaphoreType.DMA((7 * n,)),
                        pltpu.SemaphoreType.DMA((n,))],
    )(*srcs, *zeros)


_SEM = pl.BlockSpec(memory_space=pltpu.SEMAPHORE)
_EFFECT = pltpu.SideEffectType.DATAFLOW_SIDE_EFFECTING


def _gather_start(names, srcs, name):
    n = len(srcs)
    lay = [LAYOUT[nm] for nm in names]
    zeros = _gather_zeros(names, srcs)

    def body(*refs):
        ins, lands = refs[:n], refs[n:2 * n]
        send, recv, lsem = refs[2 * n:2 * n + 3]
        x, y, c = _position()
        j = 2 * x + y
        chips = [(1 - x, y), (x, 1 - y), (1 - x, 1 - y)]
        for j0 in range(N_CHIPS):
            @pl.when(j == j0)
            def _(j0=j0):
                for a in range(n):
                    for s0, d0, nr in lay[a][2](j0):
                        src, dst = ins[a].at[pl.ds(s0, nr)], lands[a].at[pl.ds(d0, nr)]
                        pltpu.make_async_copy(src, dst, lsem.at[a]).start()
                        for k, chip in enumerate(chips):
                            pltpu.make_async_remote_copy(src_ref=src, dst_ref=dst, send_sem=send.at[3 * a + k],
                                                         recv_sem=recv.at[3 * a + k], device_id=(*chip, c),
                                                         device_id_type=MESH).start()

    sems = [pltpu.SemaphoreType.DMA((3 * n,)), pltpu.SemaphoreType.DMA((3 * n,)), pltpu.SemaphoreType.DMA((n,))]
    hbm = lambda a: pltpu.HBM(a.shape, a.dtype)
    res = pl.pallas_call(
        body, name=name,
        out_shape=sems + [hbm(s) for s in srcs] + [hbm(z) for z in zeros],
        in_specs=[_HBM] * (2 * n), out_specs=[_SEM] * 3 + [_HBM] * (2 * n),
        input_output_aliases={a: 3 + a for a in range(2 * n)},
        compiler_params=pltpu.CompilerParams(has_side_effects=_EFFECT),
    )(*[pltpu.with_memory_space_constraint(s, pltpu.HBM) for s in srcs],
      *[pltpu.with_memory_space_constraint(z, pltpu.HBM) for z in zeros])
    return res[:3], res[3:3 + n], res[3 + n:]


def _gather_wait(names, sems, srcs, lands, after, name):
    n = len(srcs)
    lay = [LAYOUT[nm] for nm in names]

    def body(*refs):
        ins, zones = refs[:n], refs[n:2 * n]
        send, recv, lsem = refs[2 * n:2 * n + 3]
        x, y, c = _position()
        for a in range(n):
            whole = zones[a].at[pl.ds(0, lay[a][0])]
            for k in range(3):
                cp = pltpu.make_async_remote_copy(src_ref=ins[a], dst_ref=whole, send_sem=send.at[3 * a + k],
                                                  recv_sem=recv.at[3 * a + k], device_id=(x, y, 1 - c),
                                                  device_id_type=MESH)
                cp.wait_send()
                cp.wait_recv()
            pltpu.make_async_copy(ins[a], whole, lsem.at[a]).wait()

    hbm = lambda a: pltpu.HBM(a.shape, a.dtype)
    res = pl.pallas_call(
        body, name=name,
        out_shape=[hbm(s) for s in srcs] + [hbm(z) for z in lands],
        in_specs=[_HBM] * (2 * n) + [_SEM] * 3 + [pl.BlockSpec(memory_space=pl.ANY)], out_specs=[_HBM] * (2 * n),
        input_output_aliases={a: a for a in range(2 * n)},
        compiler_params=pltpu.CompilerParams(has_side_effects=_EFFECT),
    )(*srcs, *lands, *sems, after)
    return res[n:]


def _clip_pieces(lay_a, jsrc, c0):
    h = lay_a[1] // 2
    lo, hi = c0 * h, (c0 + 1) * h
    out = []
    for s0, d0, nr in lay_a[2](jsrc):
        a, b = max(d0, lo), min(d0 + nr, hi)
        if a < b:
            out.append((s0 + a - d0, a, b - a))
    return out


def _rows_of(pieces):
    return sum(nr for _, _, nr in pieces)


def _both_cores(body_for):
    x, y, c = _position()
    j = 2 * x + y
    for j0 in range(N_CHIPS):
        for c0 in range(2):
            @pl.when((j == j0) & (c == c0))
            def _(j0=j0, c0=c0):
                body_for(j0, c0)


def _half_to_sibling(names, gl, name):
    n = len(gl)
    halves = [LAYOUT[nm][1] // 2 for nm in names]

    def body(*refs):
        ins, outs = refs[:n], refs[n:2 * n]
        send, recv = refs[2 * n:]
        x, y, c = _position()

        def run(j0, c0):
            cps = [pltpu.make_async_remote_copy(src_ref=ins[a].at[pl.ds((1 - c0) * halves[a], halves[a])], dst_ref=outs[a],
                                                send_sem=send.at[a], recv_sem=recv.at[a], device_id=(x, y, 1 - c),
                                                device_id_type=MESH) for a in range(n)]
            for cp in cps:
                cp.start()
            for cp in cps:
                cp.wait()

        _both_cores(run)

    return pl.pallas_call(
        body, name=name,
        out_shape=[jax.ShapeDtypeStruct((halves[a],) + gl[a].shape[1:], gl[a].dtype) for a in range(n)],
        in_specs=[_HBM] * n, out_specs=[_HBM] * n,
        scratch_shapes=[pltpu.SemaphoreType.DMA((n,)), pltpu.SemaphoreType.DMA((n,))],
    )(*gl)


def _chip_scatter_half(names, parts, name):
    n = len(parts)
    lay = [LAYOUT[nm] for nm in names]
    zeros = [jnp.zeros((N_CHIPS, lay[a][0]) + parts[a].shape[1:], parts[a].dtype) for a in range(n)]

    def body(*refs):
        ins, outs = refs[:n], refs[2 * n:3 * n]
        send, recv, lsem = refs[3 * n:]
        x, y, c = _position()
        chips = [(1 - x, y), (x, 1 - y), (1 - x, 1 - y)]

        def run(j0, c0):
            def sized(a, rows):
                return outs[a].at[0, pl.ds(0, rows)]

            for a in range(n):
                base = c0 * (lay[a][1] // 2)
                for s0, d0, nr in _clip_pieces(lay[a], j0, c0):
                    pltpu.make_async_copy(ins[a].at[pl.ds(d0 - base, nr)], outs[a].at[j0, pl.ds(s0, nr)], lsem.at[a]).start()
                for k, chip in enumerate(chips):
                    for s0, d0, nr in _clip_pieces(lay[a], j0 ^ _REL[k], c0):
                        pltpu.make_async_remote_copy(
                            src_ref=ins[a].at[pl.ds(d0 - base, nr)], dst_ref=outs[a].at[j0, pl.ds(s0, nr)],
                            send_sem=send.at[3 * a + k], recv_sem=recv.at[3 * a + k],
                            device_id=(*chip, c), device_id_type=MESH).start()
            for a in range(n):
                got = _rows_of(_clip_pieces(lay[a], j0, c0))
                for k in range(3):
                    sent = _rows_of(_clip_pieces(lay[a], j0 ^ _REL[k], c0))
                    if sent:
                        pltpu.make_async_remote_copy(src_ref=sized(a, sent), dst_ref=sized(a, sent),
                                                     send_sem=send.at[3 * a + k], recv_sem=recv.at[3 * a + k],
                                                     device_id=(x, y, c), device_id_type=MESH).wait_send()
                    if got:
                        pltpu.make_async_remote_copy(src_ref=sized(a, got), dst_ref=sized(a, got),
                                                     send_sem=send.at[3 * a + k], recv_sem=recv.at[3 * a + k],
                                                     device_id=(x, y, c), device_id_type=MESH).wait_recv()
                if got:
                    pltpu.make_async_copy(sized(a, got), sized(a, got), lsem.at[a]).wait()

        _both_cores(run)

    return pl.pallas_call(
        body, name=name,
        out_shape=[jax.ShapeDtypeStruct(z.shape, z.dtype) for z in zeros],
        in_specs=[_HBM] * (2 * n), out_specs=[_HBM] * n, input_output_aliases={n + a: a for a in range(n)},
        scratch_shapes=[pltpu.SemaphoreType.DMA((3 * n,)), pltpu.SemaphoreType.DMA((3 * n,)),
                        pltpu.SemaphoreType.DMA((n,))],
    )(*parts, *zeros)


def _subset_exchange(names, bufs, l, name):
    n = len(bufs)
    lay = [LAYOUT[nm] for nm in names]

    def body(*refs):
        outs = refs[n:2 * n]
        send, recv = refs[2 * n:]
        x, y, c = _position()

        def run(j0, c0):
            for a in range(n):
                for s0, _, nr in _clip_pieces(lay[a], j0, c0):
                    rows = outs[a].at[l, pl.ds(s0, nr)]
                    pltpu.make_async_remote_copy(src_ref=rows, dst_ref=rows, send_sem=send.at[a], recv_sem=recv.at[a],
                                                 device_id=(x, y, 1 - c), device_id_type=MESH).start()
            for a in range(n):
                for c_half, wait_send in ((c0, True), (1 - c0, False)):
                    rows = _rows_of(_clip_pieces(lay[a], j0, c_half))
                    if rows:
                        ref = outs[a].at[l, pl.ds(0, rows)]
                        cp = pltpu.make_async_remote_copy(src_ref=ref, dst_ref=ref, send_sem=send.at[a], recv_sem=recv.at[a],
                                                          device_id=(x, y, 1 - c), device_id_type=MESH)
                        if wait_send:
                            cp.wait_send()
                        else:
                            cp.wait_recv()

        _both_cores(run)

    return pl.pallas_call(
        body, name=name,
        out_shape=[jax.ShapeDtypeStruct(b.shape, b.dtype) for b in bufs],
        in_specs=[_HBM] * n, out_specs=[_HBM] * n, input_output_aliases={a: a for a in range(n)},
        scratch_shapes=[pltpu.SemaphoreType.DMA((n,)), pltpu.SemaphoreType.DMA((n,))],
    )(*bufs)


def _scatter_start(names, gl, name):
    n = len(gl)
    lay = [LAYOUT[nm] for nm in names]
    zones = [lax.empty((N_CHIPS, lay[a][0]) + gl[a].shape[1:], gl[a].dtype) for a in range(n)]

    def body(*refs):
        ins, lands = refs[:n], refs[n:2 * n]
        send, recv, lsem = refs[2 * n:2 * n + 3]
        x, y, c = _position()
        j = 2 * x + y
        chips = [(1 - x, y), (x, 1 - y), (1 - x, 1 - y)]
        for j0 in range(N_CHIPS):
            @pl.when(j == j0)
            def _(j0=j0):
                for a in range(n):
                    for s0, d0, nr in lay[a][2](j0):
                        pltpu.make_async_copy(ins[a].at[pl.ds(d0, nr)], lands[a].at[j0, pl.ds(s0, nr)], lsem.at[a]).start()
                    for k, chip in enumerate(chips):
                        for s0, d0, nr in lay[a][2](j0 ^ _REL[k]):
                            pltpu.make_async_remote_copy(
                                src_ref=ins[a].at[pl.ds(d0, nr)], dst_ref=lands[a].at[j0, pl.ds(s0, nr)],
                                send_sem=send.at[3 * a + k], recv_sem=recv.at[3 * a + k],
                                device_id=(*chip, c), device_id_type=MESH).start()

    sems = [pltpu.SemaphoreType.DMA((3 * n,)), pltpu.SemaphoreType.DMA((3 * n,)), pltpu.SemaphoreType.DMA((n,))]
    hbm = lambda a: pltpu.HBM(a.shape, a.dtype)
    res = pl.pallas_call(
        body, name=name,
        out_shape=sems + [hbm(g) for g in gl] + [hbm(z) for z in zones],
        in_specs=[_HBM] * (2 * n), out_specs=[_SEM] * 3 + [_HBM] * (2 * n),
        input_output_aliases={a: 3 + a for a in range(2 * n)},
        compiler_params=pltpu.CompilerParams(has_side_effects=_EFFECT),
    )(*[pltpu.with_memory_space_constraint(g, pltpu.HBM) for g in gl],
      *[pltpu.with_memory_space_constraint(z, pltpu.HBM) for z in zones])
    return res[:3], res[3:3 + n], res[3 + n:]


def _scatter_wait(names, sems, srcs, lands, after, name):
    n = len(srcs)
    lay = [LAYOUT[nm] for nm in names]

    def body(*refs):
        zones = refs[n:2 * n]
        send, recv, lsem = refs[2 * n:2 * n + 3]
        x, y, c = _position()
        for a in range(n):
            whole = zones[a].at[0, pl.ds(0, lay[a][0])]
            for k in range(3):
                cp = pltpu.make_async_remote_copy(src_ref=whole, dst_ref=whole, send_sem=send.at[3 * a + k],
                                                  recv_sem=recv.at[3 * a + k], device_id=(x, y, 1 - c),
                                                  device_id_type=MESH)
                cp.wait_send()
                cp.wait_recv()
            pltpu.make_async_copy(whole, whole, lsem.at[a]).wait()

    hbm = lambda a: pltpu.HBM(a.shape, a.dtype)
    res = pl.pallas_call(
        body, name=name,
        out_shape=[hbm(s) for s in srcs] + [hbm(z) for z in lands],
        in_specs=[_HBM] * (2 * n) + [_SEM] * 3 + [pl.BlockSpec(memory_space=pl.ANY)], out_specs=[_HBM] * (2 * n),
        input_output_aliases={a: a for a in range(2 * n)},
        compiler_params=pltpu.CompilerParams(has_side_effects=_EFFECT),
    )(*srcs, *lands, *sems, after)
    return res[n:]


def _sibling_swap(arrs, name):
    n = len(arrs)

    def body(*refs):
        ins, outs = refs[:n], refs[n:2 * n]
        send, recv = refs[2 * n:]
        x, y, c = _position()
        cps = [pltpu.make_async_remote_copy(src_ref=ins[a], dst_ref=outs[a], send_sem=send.at[a], recv_sem=recv.at[a],
                                            device_id=(x, y, 1 - c), device_id_type=MESH) for a in range(n)]
        for cp in cps:
            cp.start()
        for cp in cps:
            cp.wait()

    return pl.pallas_call(
        body, name=name,
        out_shape=[jax.ShapeDtypeStruct(a.shape, a.dtype) for a in arrs],
        in_specs=[_HBM] * n, out_specs=[_HBM] * n,
        scratch_shapes=[pltpu.SemaphoreType.DMA((n,)), pltpu.SemaphoreType.DMA((n,))],
    )(*arrs)


def _row_tile(r):
    for t in (256, 128, 64, 32, 16, 8):
        if r % t == 0 and r > t:
            return t
    return r


def _pair_add_half(g, rb, c_arr, name):
    hrows, rest = rb.shape[0], rb.shape[1:]
    tr = _row_tile(hrows)
    nb = hrows // tr
    z = (0,) * len(rest)

    def body(c_ref, g_ref, r_ref, o_ref):
        o_ref[...] = (g_ref[...].astype(F32) + r_ref[...].astype(F32)).astype(o_ref.dtype)

    return pl.pallas_call(
        body, name=name,
        grid_spec=pltpu.PrefetchScalarGridSpec(
            num_scalar_prefetch=1, grid=(nb,),
            in_specs=[pl.BlockSpec((tr,) + rest, lambda i, c_ref: (c_ref[0] * nb + i,) + z),
                      pl.BlockSpec((tr,) + rest, lambda i, c_ref: (i,) + z)],
            out_specs=pl.BlockSpec((tr,) + rest, lambda i, c_ref: (i,) + z)),
        out_shape=jax.ShapeDtypeStruct((hrows,) + rest, BF16),
        compiler_params=pltpu.CompilerParams(dimension_semantics=("parallel",), vmem_limit_bytes=VMEM_LIMIT),
    )(c_arr, g, rb)


def _sum_slabs(slabs, l, buf, name):
    m = len(slabs)
    n, R, rest = slabs[0].shape[0], slabs[0].shape[1], slabs[0].shape[2:]
    tr = _row_tile(R)
    z = (0,) * len(rest)

    def body(*refs):
        total = None
        for r_ref in refs[:m]:
            acc = r_ref[0].astype(F32)
            for k in range(1, n):
                acc = acc + r_ref[k].astype(F32)
            total = acc if total is None else total + acc
        refs[-1][...] = total

    if R // tr > 64 and len(rest) == 1 and rest[0] % 256 == 0:
        grid = (rest[0] // 256,)
        in_spec = pl.BlockSpec((n, R, 256), lambda i: (0, 0, i))
        out_spec = pl.BlockSpec((None, R, 256), lambda i: (l, 0, i))
    else:
        grid = (R // tr,)
        in_spec = pl.BlockSpec((n, tr) + rest, lambda i: (0, i) + z)
        out_spec = pl.BlockSpec((None, tr) + rest, lambda i: (l, i) + z)
    in_specs, args, aliases = [in_spec] * m, list(slabs), {}
    if buf is not None:
        in_specs.append(pl.BlockSpec(memory_space=pl.ANY))
        args.append(buf)
        aliases = {m: 0}
    return pl.pallas_call(
        body, name=name, grid=grid, in_specs=in_specs, out_specs=out_spec,
        out_shape=jax.ShapeDtypeStruct((DEPTH, R) + rest, F32), input_output_aliases=aliases,
        compiler_params=pltpu.CompilerParams(dimension_semantics=("parallel",), vmem_limit_bytes=VMEM_LIMIT),
    )(*args)


def _adam_math(w, g, m, v):
    mn = ADAM_B1 * m + (1.0 - ADAM_B1) * g
    vn = ADAM_B2 * v + (1.0 - ADAM_B2) * (g * g)
    m_hat = mn / (1.0 - ADAM_B1 ** ADAM_STEP)
    v_hat = vn / (1.0 - ADAM_B2 ** ADAM_STEP)
    return -ADAM_LR * (m_hat / (jnp.sqrt(v_hat) + ADAM_EPS) + ADAM_WD * w), mn, vn


def _adamw(w, g, m, v, name):
    L, R, C = w.shape
    tr = _row_tile(R)

    def body(w_ref, g_ref, m_ref, v_ref, d_ref, mo_ref, vo_ref):
        d_ref[...], mo_ref[...], vo_ref[...] = _adam_math(w_ref[...], g_ref[...], m_ref[...], v_ref[...])

    if R // tr > 64 and C % 128 == 0:
        spec, grid = pl.BlockSpec((None, R, 128), lambda l, i: (l, 0, i)), (L, C // 128)
    else:
        spec, grid = pl.BlockSpec((None, tr, C), lambda l, i: (l, i, 0)), (L, R // tr)
    return pl.pallas_call(
        body, name=name, grid=grid, in_specs=[spec] * 4, out_specs=[spec] * 3,
        out_shape=[jax.ShapeDtypeStruct((L, R, C), F32)] * 3,
        compiler_params=pltpu.CompilerParams(dimension_semantics=("parallel", "parallel"), vmem_limit_bytes=VMEM_LIMIT),
    )(w, g, m, v)


_VMEM_WHOLE = pl.BlockSpec(memory_space=pltpu.VMEM)


def _matrix_update(gath, w, m, v, name):
    K = w.shape[1]

    def body(g0_ref, g1_ref, w_ref, m_ref, v_ref, go_ref, d_ref, mo_ref, vo_ref):
        for l, gr in enumerate((g0_ref, g1_ref)):
            for k in range(K):
                g = gr[0, k].astype(F32)
                for dev in range(1, 8):
                    g = g + gr[dev, k].astype(F32)
                go_ref[l, k] = g
                d_ref[l, k], mo_ref[l, k], vo_ref[l, k] = _adam_math(w_ref[l, k], g, m_ref[l, k], v_ref[l, k])

    return pl.pallas_call(
        body, name=name, in_specs=[_VMEM_WHOLE] * 5, out_specs=[_VMEM_WHOLE] * 4,
        out_shape=[jax.ShapeDtypeStruct(w.shape, F32)] * 4,
        compiler_params=pltpu.CompilerParams(vmem_limit_bytes=VMEM_LIMIT),
    )(gath[0], gath[1], w, m, v)


VECS = (("pre_norm_g", D), ("post_norm_g", D), ("gm_ln_g", GM_W), ("gm_ln_b", GM_W), ("mla_q_norm_g", QR),
        ("mla_kv_norm_g", KVR), ("lru_conv_b", LRU_W), ("lru_b_a", LRU_W), ("lru_b_x", LRU_W), ("lru_lambda", LRU_W))
VEC_KEY = {"pre_norm_g": "pre_g", "post_norm_g": "post_g", "gm_ln_g": "ln_g", "gm_ln_b": "ln_b", "mla_q_norm_g": "qg",
           "mla_kv_norm_g": "kvg", "lru_conv_b": "conv_b", "lru_b_a": "ba", "lru_b_x": "bx", "lru_lambda": "lam"}
VEC_ROWS, VEC_W, VEC_ROW0, LOSS_ROW = 16, LRU_W, GM_G, 14


def _pack_rows(LG, loss_part):
    per = len(VECS) + 1
    ins = []
    for G in LG:
        ins += [G[VEC_KEY[n]] for n, _ in VECS] + [G["bst"]]
    ins.append(loss_part)

    def body(*refs):
        o_ref = refs[-1]
        o_ref[...] = jnp.zeros_like(o_ref)
        for l in range(DEPTH):
            base = VEC_ROWS * l
            o_ref[pl.ds(base, 8), pl.ds(0, GM_B)] = refs[per * l + len(VECS)][...].T[:8, :]
            for t, (_, width) in enumerate(VECS):
                o_ref[pl.ds(base + VEC_ROW0 + t, 1), pl.ds(0, width)] = refs[per * l + t][...]
        o_ref[pl.ds(LOSS_ROW, 1), pl.ds(0, 128)] = jnp.broadcast_to(refs[-2][...], (1, 128))

    return pl.pallas_call(
        body, name="pack_rows", in_specs=[_VMEM_WHOLE] * len(ins), out_specs=_VMEM_WHOLE,
        out_shape=jax.ShapeDtypeStruct((DEPTH * VEC_ROWS, VEC_W), F32),
    )(*ins)


def _vector_update(gath, W, M, V):
    names = [n for n, _ in VECS] + ["gm_bs"]
    nw = len(names)

    def body(*refs):
        g_ref = refs[0]
        wr, mr, vr = refs[1:1 + nw], refs[1 + nw:1 + 2 * nw], refs[1 + 2 * nw:1 + 3 * nw]
        outs = refs[1 + 3 * nw:]
        s = g_ref[0]
        for dev in range(1, 8):
            s = s + g_ref[dev]
        for t, (_, width) in enumerate(VECS):
            for l in range(DEPTH):
                r = VEC_ROWS * l + VEC_ROW0 + t
                g = s[r:r + 1, :width]
                row = (pl.ds(l, 1), slice(None))
                res = (g,) + _adam_math(wr[t][row], g, mr[t][row], vr[t][row])
                for q in range(4):
                    outs[4 * t + q][row] = res[q]
        t = len(VECS)
        for l in range(DEPTH):
            for k in range(GM_G):
                g = s[VEC_ROWS * l + k:VEC_ROWS * l + k + 1, :GM_B]
                row = (l, pl.ds(k, 1), slice(None))
                res = (g,) + _adam_math(wr[t][row], g, mr[t][row], vr[t][row])
                for q in range(4):
                    outs[4 * t + q][row] = res[q]
        outs[4 * nw][...] = s[LOSS_ROW:LOSS_ROW + 1, :128]

    ws = [W[n] for n in names]
    out_shape = []
    for w in ws:
        out_shape += [jax.ShapeDtypeStruct(w.shape, F32)] * 4
    out_shape.append(jax.ShapeDtypeStruct((1, 128), F32))
    res = pl.pallas_call(
        body, name="vector_update", in_specs=[_VMEM_WHOLE] * (1 + 3 * nw), out_specs=[_VMEM_WHOLE] * (4 * nw + 1),
        out_shape=out_shape, compiler_params=pltpu.CompilerParams(vmem_limit_bytes=VMEM_LIMIT),
    )(gath, *ws, *[M[n] for n in names], *[V[n] for n in names])
    return {n: tuple(res[4 * t:4 * t + 4]) for t, n in enumerate(names)}, res[4 * nw]


SHARDED = ("w_in", "mla_w_uq", "mla_w_ukv", "lru_conv_w", "w_proj_a", "w_proj_b", "w_proj_c", "w_out")
COL_SHARDED = ("w_in", "mla_w_uq", "mla_w_ukv", "lru_conv_w")
SMALL = ("pre_norm_g", "gm_ln_g", "gm_ln_b", "gm_ws", "gm_bs", "mla_q_norm_g", "mla_kv_norm_g", "lru_conv_b",
         "lru_w_a", "lru_b_a", "lru_w_x", "lru_b_x", "lru_lambda", "post_norm_g")
WEIGHTS = ("pre_norm_g", "w_in", "gm_ln_g", "gm_ln_b", "gm_ws", "gm_bs", "mla_q_norm_g", "mla_w_uq",
           "mla_kv_norm_g", "mla_w_ukv", "lru_conv_w", "lru_conv_b", "lru_w_a", "lru_b_a", "lru_w_x", "lru_b_x",
           "lru_lambda", "w_proj_a", "w_proj_b", "w_proj_c", "w_out", "post_norm_g")


GB_KEY = {"w_in": "wp", "mla_w_uq": "wuq", "mla_w_ukv": "wukv", "w_proj_a": "wpa", "w_proj_b": "wpb",
          "w_proj_c": "wpc", "w_out": "wout"}


def _prepare(l, gathered, small, wsb):
    P = {GB_KEY[n]: gathered[n] for n in GB_KEY}
    P["conv_w"] = gathered["lru_conv_w"].transpose(1, 0, 2).reshape(CONV_W, LRU_W)
    P["wsb"] = wsb
    row = lambda n: small[n][l][None, :]
    P["pre_g"], P["post_g"] = row("pre_norm_g"), row("post_norm_g")
    P["ln_g"], P["ln_b"] = row("gm_ln_g"), row("gm_ln_b")
    P["ws"] = small["gm_ws"][l]
    P["bst"] = jnp.pad(small["gm_bs"][l].T, ((0, 0), (0, 128 - GM_G)))
    P["qg"], P["kvg"] = row("mla_q_norm_g"), row("mla_kv_norm_g")
    P["conv_b"], P["ba"], P["bx"], P["lam"] = row("lru_conv_b"), row("lru_b_a"), row("lru_b_x"), row("lru_lambda")
    return P


def kernel(x, pre_norm_g, w_in, gm_ln_g, gm_ln_b, gm_ws, gm_bs, mla_q_norm_g, mla_w_uq, mla_kv_norm_g, mla_w_ukv, lru_conv_w, lru_conv_b, lru_w_a, lru_b_a, lru_w_x, lru_b_x, lru_lambda, w_proj_a, w_proj_b, w_proj_c, w_out, post_norm_g, loss_target, m_pre_norm_g, m_w_in, m_gm_ln_g, m_gm_ln_b, m_gm_ws, m_gm_bs, m_mla_q_norm_g, m_mla_w_uq, m_mla_kv_norm_g, m_mla_w_ukv, m_lru_conv_w, m_lru_conv_b, m_lru_w_a, m_lru_b_a, m_lru_w_x, m_lru_b_x, m_lru_lambda, m_w_proj_a, m_w_proj_b, m_w_proj_c, m_w_out, m_post_norm_g, v_pre_norm_g, v_w_in, v_gm_ln_g, v_gm_ln_b, v_gm_ws, v_gm_bs, v_mla_q_norm_g, v_mla_w_uq, v_mla_kv_norm_g, v_mla_w_ukv, v_lru_conv_w, v_lru_conv_b, v_lru_w_a, v_lru_b_a, v_lru_w_x, v_lru_b_x, v_lru_lambda, v_w_proj_a, v_w_proj_b, v_w_proj_c, v_w_out, v_post_norm_g):
    args = dict(locals())
    W = {n: args[n] for n in WEIGHTS}
    M = {n: args["m_" + n] for n in WEIGHTS}
    V = {n: args["v_" + n] for n in WEIGHTS}
    c = lax.axis_index("c")

    def shards(l):
        out = []
        for n in SHARDED:
            blk = W[n][l].T if n in TRANSPOSED else W[n][l]
            out.append(blk[None] if n == "lru_conv_w" else blk.astype(BF16))
        return out

    small = {n: W[n] for n in SMALL}
    wsb = _superblocks(W["lru_w_a"], W["lru_w_x"])
    tabs = _rope_tables()
    g0 = dict(zip(SHARDED, _weights_allgather(SHARDED, shards(0), "weights_allgather_l0")))
    sems, srcs1, lands1 = _gather_start(SHARDED, shards(1), "weights_gather_start_l1")

    P = [_prepare(0, g0, small, wsb), None]
    h0 = x[0]
    h1, A0 = _layer_fwd(h0, P[0], 0, tabs)
    g1 = dict(zip(SHARDED, _gather_wait(SHARDED, sems, srcs1, lands1, h1, "weights_gather_wait_l1")))
    P[1] = _prepare(1, g1, small, wsb)
    h2, A1 = _layer_fwd(h1, P[1], 1, tabs)
    dy, loss_part = _loss_fwd(h2, loss_target[0])
    def large_grads(G, GB):
        conv = G["conv_w"].reshape(CONV_W, N_CHIPS, LRU_W // N_CHIPS).transpose(1, 0, 2)
        return [conv if n == "lru_conv_w" else GB[GB_KEY[n]] for n in SHARDED]

    d1, G1, GB1 = _layer_bwd(dy, A1, P[1], 1, tabs)
    sems, srcs1, lands1 = _scatter_start(SHARDED, large_grads(G1, GB1), "grads_scatter_start_l1")
    d0, G0, GB0 = _layer_bwd(d1, A0, P[0], 0, tabs)
    LG = (G0, G1)
    mine1 = _scatter_wait(SHARDED, sems, srcs1, lands1, d0, "grads_scatter_wait_l1")
    theirs1 = _sibling_swap(mine1, "partials_to_sibling_l1")
    both = [_sum_slabs([a, b], 1, None, "sum_partials_l1_" + n) for n, a, b in zip(SHARDED, mine1, theirs1)]
    g0l = large_grads(G0, GB0)
    c_arr = jnp.reshape(c, (1,)).astype(jnp.int32)
    from_sib = _half_to_sibling(SHARDED, g0l, "grads_half_to_sibling_l0")
    pair = [_pair_add_half(g, rb, c_arr, "pair_add_" + n) for n, g, rb in zip(SHARDED, g0l, from_sib)]
    slabs = _chip_scatter_half(SHARDED, pair, "grads_chip_scatter_l0")
    both = [_sum_slabs([s], 0, b, "sum_slabs_l0_" + n) for n, s, b in zip(SHARDED, slabs, both)]
    both = _subset_exchange(SHARDED, both, 0, "reduced_rows_to_sibling_l0")
    grads = {}
    for n, b in zip(SHARDED, both):
        if n in TRANSPOSED and n != "w_in":
            b = jnp.swapaxes(b, 1, 2)
        grads[n] = b if n == "w_in" else b.reshape(W[n].shape)

    rows = _pack_rows(LG, loss_part)
    mats = []
    for g in LG:
        mats += [g["ws"].astype(BF16), g["wab"][0, :, :, :LRU_BW], g["wab"][1, :, :, :LRU_BW]]
    gath = _allgather([rows] + mats, "small_grads_allgather")
    upd, loss_row = _vector_update(gath[0], W, M, V)
    loss = loss_row[0, 0]
    for k, n in enumerate(("gm_ws", "lru_w_a", "lru_w_x")):
        upd[n] = _matrix_update((gath[1 + k], gath[4 + k]), W[n], M[n], V[n], "update_" + n)

    for n in SHARDED:
        if n == "w_in":
            tr = lambda a: jnp.swapaxes(a, 1, 2)
            res = _adamw(tr(W[n]), grads[n], tr(M[n]), tr(V[n]), "adamw_" + n)
            upd[n] = tuple(tr(a) for a in (grads[n],) + tuple(res))
        else:
            upd[n] = (grads[n],) + tuple(_adamw(W[n], grads[n], M[n], V[n], "adamw_" + n))

    return (loss, d0[None], *[upd[n][0] for n in WEIGHTS], *[upd[n][1] for n in WEIGHTS],
            *[upd[n][2] for n in WEIGHTS], *[upd[n][3] for n in WEIGHTS])
```

```python
import functools
import math

import jax
import jax.numpy as jnp
from jax import lax
from jax.experimental import pallas as pl
from jax.experimental.pallas import tpu as pltpu

F32, BF16 = jnp.float32, jnp.bfloat16
MESH = pl.DeviceIdType.MESH

S, D, DEPTH = 2048, 1024, 2
CHUNK, EPS = 64, 1e-6
GM_W, GM_G, GM_B = 1024, 4, 128
H, NOPE, ROPE, VDIM = 8, 128, 64, 128
QR, KVR = 384, 256
MLA_W = H * VDIM
LRU_W, LRU_NB, LRU_BW, LRU_C, CONV_W = 1280, 16, 80, 8.0, 4
ROPE_THETA = 10000.0
IN_SIZES = (GM_W, GM_W, GM_W, QR, KVR, ROPE, MLA_W, LRU_W, LRU_W, D, D, D)
N_IN = sum(IN_SIZES)
N_CHIPS = 4
ADAM_LR, ADAM_B1, ADAM_B2, ADAM_EPS, ADAM_WD, ADAM_STEP = 0.001, 0.9, 0.999, 1e-08, 0.01, 10

HP = 256
O_U, O_V, O_ZA, O_GA, O_GB, O_GC = 0, 1024, 2048, 3072, 4096, 5120
O_CKV, O_KR, O_CQ, O_XC, O_ZC, O_ZB = 6144, 6400, 6528, 7680, 8960, 10240
NP = 11264
VMEM_LIMIT = 48 * 1024 * 1024


def _tile(dim, target):
    if dim <= target:
        return dim
    t = (target // 128) * 128
    while dim % t:
        t -= 128
    return t


def _sig(x):
    return jax.nn.sigmoid(x)


def _silu(x):
    return x * _sig(x)


def _dsilu(x):
    s = _sig(x)
    return s * (1.0 + x * (1.0 - s))


def _mm(a, b, mode, name, out_dtype=F32, tm=512, tn=512, tk=1024, b_lead=None, out_lead=None):
    b2 = b.shape[1:] if b_lead is not None else b.shape
    if mode == "nn":
        (M, K), (K2, N) = a.shape, b2
    elif mode == "nt":
        (M, K), (N, K2) = a.shape, b2
    else:
        (K, M), (K2, N) = a.shape, b2
    assert K == K2, (name, a.shape, b.shape)
    tm, tn, tk = _tile(M, tm), _tile(N, tn), _tile(K, tk)
    nk = K // tk
    if mode == "tn":
        a_spec = pl.BlockSpec((tk, tm), lambda i, j, k: (k, i))
        lhs_c = 0
    else:
        a_spec = pl.BlockSpec((tm, tk), lambda i, j, k: (i, k))
        lhs_c = 1
    b_blk, b_idx, rhs_c = ((tn, tk), (lambda i, j, k: (j, k)), 1) if mode == "nt" else ((tk, tn), (lambda i, j, k: (k, j)), 0)
    if b_lead is None:
        b_spec = pl.BlockSpec(b_blk, b_idx)
    else:
        b_spec = pl.BlockSpec((None,) + b_blk, functools.partial(lambda i, j, k, f, l: (l,) + f(i, j, k), f=b_idx, l=b_lead))
    dims = (((lhs_c,), (rhs_c,)), ((), ()))
    in_specs, args, aliases = [a_spec, b_spec], [a, b], {}
    if out_lead is None:
        out_spec = pl.BlockSpec((tm, tn), lambda i, j, k: (i, j))
        out_shape = jax.ShapeDtypeStruct((M, N), out_dtype)
    else:
        l_out, n_lead, buf = out_lead
        out_spec = pl.BlockSpec((None, tm, tn), functools.partial(lambda i, j, k, l: (l, i, j), l=l_out))
        out_shape = jax.ShapeDtypeStruct((n_lead, M, N), out_dtype)
        if buf is not None:
            in_specs.append(pl.BlockSpec(memory_space=pl.ANY))
            args.append(buf)
            aliases = {2: 0}

    def body(a_ref, b_ref, *rest):
        o_ref, acc_ref = rest[-2:]
        k = pl.program_id(2)

        @pl.when(k == 0)
        def _():
            acc_ref[...] = jnp.zeros_like(acc_ref)

        acc_ref[...] += lax.dot_general(a_ref[...].astype(BF16), b_ref[...].astype(BF16), dims,
                                        preferred_element_type=F32)

        @pl.when(k == nk - 1)
        def _():
            o_ref[...] = acc_ref[...].astype(o_ref.dtype)

    return pl.pallas_call(
        body, name=name, grid=(M // tm, N // tn, nk),
        in_specs=in_specs, out_specs=out_spec, out_shape=out_shape,
        scratch_shapes=[pltpu.VMEM((tm, tn), F32)], input_output_aliases=aliases,
        compiler_params=pltpu.CompilerParams(dimension_semantics=("parallel", "parallel", "arbitrary"),
                                             vmem_limit_bytes=VMEM_LIMIT),
    )(*args)


def _rows(fn, name, tm, rows, halos=(), fulls=(), outs=(), accs=()):
    n = S // tm
    in_specs, args = [], []
    for arr, w, cb in rows:
        in_specs.append(pl.BlockSpec((tm, w), functools.partial(lambda i, cb: (i, cb), cb=cb)))
        args.append(arr)
    for arr, w, cb, side in halos:
        if side == "prev":
            im = functools.partial(lambda i, cb: (jnp.maximum(i * (tm // 16) - 1, 0), cb), cb=cb)
        else:
            im = functools.partial(lambda i, cb: (jnp.minimum((i + 1) * (tm // 16), S // 16 - 1), cb), cb=cb)
        in_specs.append(pl.BlockSpec((16, w), im))
        args.append(arr)
    for arr in fulls:
        in_specs.append(pl.BlockSpec(arr.shape, functools.partial(lambda i, nd: (0,) * nd, nd=arr.ndim)))
        args.append(arr)
    out_shape, out_specs, aliases, n_alias = [], [], {}, 0
    for k, o in enumerate(outs):
        if len(o) == 3 and o[2] == "T":
            out_shape.append(jax.ShapeDtypeStruct((o[0], S), o[1]))
            out_specs.append(pl.BlockSpec((o[0], tm), lambda i: (0, i)))
        elif len(o) == 3:
            buf, total, cb = o[2]
            out_shape.append(jax.ShapeDtypeStruct((S, total), o[1]))
            out_specs.append(pl.BlockSpec((tm, o[0]), functools.partial(lambda i, cb: (i, cb), cb=cb)))
            if buf is not None:
                aliases[len(args)] = k
                in_specs.append(pl.BlockSpec(memory_space=pl.ANY))
                args.append(buf)
                n_alias += 1
        else:
            out_shape.append(jax.ShapeDtypeStruct((S, o[0]), o[1]))
            out_specs.append(pl.BlockSpec((tm, o[0]), lambda i: (i, 0)))
    for shp in accs:
        out_shape.append(jax.ShapeDtypeStruct(shp, F32))
        out_specs.append(pl.BlockSpec(shp, functools.partial(lambda i, nd: (0,) * nd, nd=len(shp))))
    nr, nh, nf, no, na = len(rows), len(halos), len(fulls), len(outs), len(accs)

    def body(*refs):
        i = pl.program_id(0)
        ins, orefs = refs[:nr + nh + nf], refs[nr + nh + nf + n_alias:]
        rv = [r[...].astype(F32) for r in ins[:nr]]
        hv = [r[...].astype(F32)[8:] if h[3] == "prev" else r[...].astype(F32)[:8] for r, h in zip(ins[nr:nr + nh], halos)]
        fv = [r[...] for r in ins[nr + nh:]]
        o, a = fn(i, rv, hv, fv)
        assert len(o) == no and len(a) == na, name
        for spec, ref, val in zip(outs, orefs[:no], o):
            ref[...] = (val.T if len(spec) == 3 and spec[2] == "T" else val).astype(ref.dtype)
        if na:
            @pl.when(i == 0)
            def _():
                for ref in orefs[no:]:
                    ref[...] = jnp.zeros_like(ref)

            for ref, val in zip(orefs[no:], a):
                ref[...] += val

    res = pl.pallas_call(
        body, name=name, grid=(n,), in_specs=in_specs, out_specs=out_specs, out_shape=out_shape,
        input_output_aliases=aliases,
        compiler_params=pltpu.CompilerParams(dimension_semantics=("arbitrary",), vmem_limit_bytes=VMEM_LIMIT),
    )(*args)
    return res


def _shift_down(xb, halo, s, row):
    fix = jnp.tile(pltpu.roll(halo, s, 0), (xb.shape[0] // 8, 1))
    return jnp.where(row >= s, pltpu.roll(xb, s, 0), fix)


def _shift_up(xb, halo, s, row):
    tm = xb.shape[0]
    fix = jnp.tile(pltpu.roll(halo, 8 - s, 0), (tm // 8, 1))
    return jnp.where(row < tm - s, pltpu.roll(xb, tm - s, 0), fix)


def _rms(x):
    return lax.rsqrt(jnp.mean(x * x, axis=-1, keepdims=True) + EPS)


def _rms_bwd(dy, x, g):
    r = _rms(x)
    xh = x * r
    dxh = dy * g
    dx = r * (dxh - xh * jnp.mean(dxh * xh, axis=-1, keepdims=True))
    return dx, dy * xh


def _colsum(x):
    return jnp.sum(x, axis=0, keepdims=True)


def _prenorm_fwd(x, g, token=None):
    def fn(i, rv, hv, fv):
        return [rv[0] * _rms(rv[0]) * fv[0]], []
    return _rows(fn, "prenorm_fwd", 256, [(x, D, 0)], fulls=[g] + ([] if token is None else [token]), outs=[(D, BF16)])[0]


def _gm_mask():
    r = lax.broadcasted_iota(jnp.int32, (GM_B, GM_B), 0) // CHUNK
    c = lax.broadcasted_iota(jnp.int32, (GM_B, GM_B), 1) // CHUNK
    return c <= r


def _gm_norm(v, g, b):
    mu = jnp.mean(v, axis=-1, keepdims=True)
    vc = v - mu
    rs = lax.rsqrt(jnp.mean(vc * vc, axis=-1, keepdims=True) + EPS)
    vh = vc * rs
    return vh, rs, vh * g + b


def _gm_sv(vn, ws, bst):
    mask = _gm_mask()
    gw = GM_W // GM_G
    parts = []
    for g in range(GM_G):
        wm = jnp.where(mask, ws[g], 0.0).astype(BF16)
        parts.append(jnp.dot(wm, vn[:, g * gw:(g + 1) * gw].astype(BF16), preferred_element_type=F32)
                     + bst[:, g:g + 1])
    return jnp.concatenate(parts, axis=1)


def _gmlp_fwd(proj, ln_g, ln_b, ws, bst):
    def fn(i, rv, hv, fv):
        u, v, z = rv
        g, b, w, bt = fv
        _, _, vn = _gm_norm(v, g, b)
        return [u * _gm_sv(vn, w, bt) * _silu(z)], []
    return _rows(fn, "gmlp_fwd", GM_B, [(proj, GM_W, 0), (proj, GM_W, 1), (proj, GM_W, 2)],
                 fulls=[ln_g, ln_b, ws, bst], outs=[(GM_W, BF16)])[0]


def _mla_prep_fwd(proj, qg, kvg):
    def fn(i, rv, hv, fv):
        cq, ckv = rv
        g1, g2 = fv
        return [cq * _rms(cq) * g1, ckv * _rms(ckv) * g2], []
    return _rows(fn, "mla_prep_fwd", 256, [(proj, QR, O_CQ // QR), (proj, KVR, O_CKV // KVR)],
                 fulls=[qg, kvg], outs=[(QR, BF16), (KVR, BF16)])


def _rot(t, cc, sa, sb):
    return t * cc + pltpu.roll(t, 32, 1) * sa + pltpu.roll(t, 96, 1) * sb


def _rot_t(g, cc, sa, sb):
    return g * cc + pltpu.roll(g * sa, 96, 1) + pltpu.roll(g * sb, 32, 1)


def _rope_tables():
    pos = jnp.arange(S, dtype=F32)
    inv_freq = ROPE_THETA ** (-jnp.arange(0, ROPE, 2, dtype=F32) / ROPE)
    ang = pos[:, None] * inv_freq[None, :]
    cos, sin, z = jnp.cos(ang), jnp.sin(ang), jnp.zeros((S, 32), F32)
    cc = jnp.concatenate([cos, cos, z, z], axis=1)
    sa = jnp.concatenate([z, sin, z, z], axis=1)
    sb = jnp.concatenate([-sin, z, z, z], axis=1)
    return cc, sa, sb


ATT_SCALE = 1.0 / math.sqrt(NOPE + ROPE)


def _rope_fwd(q, kv, proj, tabs):
    def fn(i, rv, hv, fv):
        qb, kvb, kr, cc, sa, sb = rv
        krr = _rot(kr, cc, sa, sb)
        qs, ks = [], []
        for h in range(H):
            qs += [qb[:, h * HP:h * HP + 128] * ATT_SCALE, _rot(qb[:, h * HP + 128:(h + 1) * HP], cc, sa, sb) * ATT_SCALE]
            ks += [kvb[:, h * 128:(h + 1) * 128], krr]
        kc = jnp.concatenate(ks, axis=1)
        vv = kvb[:, H * NOPE:]
        return [jnp.concatenate(qs, axis=1), kc, kc, vv, vv], []
    cc, sa, sb = tabs
    return _rows(fn, "rope_fwd", 256,
                 [(q, H * HP, 0), (kv, H * 256, 0), (proj, 128, O_KR // 128), (cc, 128, 0), (sa, 128, 0), (sb, 128, 0)],
                 outs=[(H * HP, BF16), (H * HP, BF16), (H * HP, BF16, "T"), (MLA_W, BF16), (MLA_W, BF16, "T")])


TQ, TC, ATT_NB = 256, 128, 4
_NT = (((1,), (1,)), ((), ()))


def _attn_allowed(i, kc):
    kpos = kc * TC + lax.broadcasted_iota(jnp.int32, (TC, TQ), 0)
    qpos = i * TQ + lax.broadcasted_iota(jnp.int32, (TC, TQ), 1)
    return (kpos // CHUNK) <= (qpos // CHUNK)


def _attn_fwd(qc, kc, vt):
    def body(q_ref, k_ref, vt_ref, o_ref, l_ref):
        i = pl.program_id(1)
        q = q_ref[...]

        def scores(sb):
            t0s = [pl.multiple_of((sb * ATT_NB + c) * TC, TC) for c in range(ATT_NB)]
            return [lax.dot_general(k_ref[pl.ds(t0, TC), :], q, _NT, preferred_element_type=F32) for t0 in t0s]

        def block(sb, ss, carry, masked):
            m, l, acc = carry
            t0s = [pl.multiple_of((sb * ATT_NB + c) * TC, TC) for c in range(ATT_NB)]
            if masked:
                ss = [jnp.where(_attn_allowed(i, sb * ATT_NB + c), s, -1e30) for c, s in enumerate(ss)]
            m_new = m
            for s in ss:
                m_new = jnp.maximum(m_new, jnp.max(s, axis=0, keepdims=True))
            alpha = jnp.exp(m - m_new)
            ps = [jnp.exp(s - m_new) for s in ss]
            l = alpha * l
            acc = alpha * acc
            for t0, p in zip(t0s, ps):
                l = l + jnp.sum(p, axis=0, keepdims=True)
                acc = acc + jnp.dot(vt_ref[:, pl.ds(t0, TC)], p.astype(BF16), preferred_element_type=F32)
            return m_new, l, acc

        nsb = (i + 2) // 2
        c = (jnp.full((1, TQ), -1e30, F32), jnp.zeros((1, TQ), F32), jnp.zeros((VDIM, TQ), F32))

        def step(sb, sc):
            nxt = scores(sb + 1)
            return nxt, block(sb, sc[0], sc[1], False)

        ss, c = lax.fori_loop(0, nsb - 1, step, (scores(0), c))
        m, l, acc = block(nsb - 1, ss, c, True)
        o_ref[...] = (acc / l).T
        l_ref[...] = m + jnp.log(l)

    return pl.pallas_call(
        body, name="attn_fwd", grid=(H, S // TQ),
        in_specs=[pl.BlockSpec((TQ, HP), lambda h, i: (i, h)),
                  pl.BlockSpec((S, HP), lambda h, i: (0, h)),
                  pl.BlockSpec((VDIM, S), lambda h, i: (h, 0))],
        out_specs=[pl.BlockSpec((TQ, VDIM), lambda h, i: (i, h)), pl.BlockSpec((None, 1, TQ), lambda h, i: (h, 0, i))],
        out_shape=[jax.ShapeDtypeStruct((S, MLA_W), F32), jax.ShapeDtypeStruct((H, 1, S), F32)],
        compiler_params=pltpu.CompilerParams(dimension_semantics=("parallel", "arbitrary"),
                                             vmem_limit_bytes=VMEM_LIMIT),
    )(qc, kc, vt)


def _gate_mul_fwd(name, val, proj, width, cb):
    def fn(i, rv, hv, fv):
        o, z = rv
        return [o * _silu(z)], []
    return _rows(fn, name, 256, [(val, width, 0), (proj, width, cb)], outs=[(width, BF16)])[0]


def _conv_fwd(proj, w, b):
    def fn(i, rv, hv, fv):
        (xb,), (halo,), (ww, bb) = rv, hv, fv
        halo = jnp.where(i > 0, halo, 0.0)
        row = lax.broadcasted_iota(jnp.int32, xb.shape, 0)
        acc = bb + ww[3:4] * xb
        for s in range(1, CONV_W):
            acc = acc + ww[3 - s:4 - s] * _shift_down(xb, halo, s, row)
        return [acc, acc], []
    return _rows(fn, "conv_fwd", 128, [(proj, LRU_W, O_XC // LRU_W)], halos=[(proj, LRU_W, O_XC // LRU_W, "prev")],
                 fulls=[w, b], outs=[(LRU_W, F32), (LRU_W, BF16)])


def _lru_terms(ga, gx, xc, ba, bx, lam):
    r = _sig(ga + ba)
    ig = _sig(gx + bx)
    sp = jnp.maximum(-lam, 0.0) + jnp.log(1.0 + jnp.exp(-jnp.abs(lam)))
    log_a = -LRU_C * r * sp
    a = jnp.exp(log_a)
    e2 = jnp.exp(2.0 * log_a)
    om = 1.0 - e2
    mult = jnp.sqrt(jnp.maximum(om, 0.0))
    return r, ig, sp, a, e2, om, mult


def _lru_gates_fwd(gates, xc, ba, bx, lam):
    def fn(i, rv, hv, fv):
        ga, gx, x = rv
        r, ig, sp, a, e2, om, mult = _lru_terms(ga, gx, x, *fv)
        return [a, mult * (ig * x)], []
    return _rows(fn, "lru_gates_fwd", 128, [(gates, LRU_W, 0), (gates, LRU_W, 1), (xc, LRU_W, 0)],
                 fulls=[ba, bx, lam], outs=[(LRU_W, F32), (LRU_W, F32)])


SCAN_T, SCAN_CW = 64, 256


def _scan_fwd(a, b):
    def body(a_ref, b_ref, h_ref):
        row = lax.broadcasted_iota(jnp.int32, (SCAN_T, SCAN_CW), 0)

        def step(blk, hc):
            t0 = pl.multiple_of(blk * SCAN_T, SCAN_T)
            A = a_ref[pl.ds(t0, SCAN_T), :]
            B = b_ref[pl.ds(t0, SCAN_T), :]
            d = 1
            while d < SCAN_T:
                keep = row >= d
                A_s = jnp.where(keep, pltpu.roll(A, d, 0), 1.0)
                B_s = jnp.where(keep, pltpu.roll(B, d, 0), 0.0)
                B = A * B_s + B
                A = A * A_s
                d *= 2
            hh = A * hc + B
            h_ref[pl.ds(t0, SCAN_T), :] = hh
            return hh[SCAN_T - 1:SCAN_T, :]

        lax.fori_loop(0, S // SCAN_T, step, jnp.zeros((1, SCAN_CW), F32))

    spec = pl.BlockSpec((S, SCAN_CW), lambda j: (0, j))
    return pl.pallas_call(
        body, name="scan_fwd", grid=(LRU_W // SCAN_CW,), in_specs=[spec, spec], out_specs=spec,
        out_shape=jax.ShapeDtypeStruct((S, LRU_W), F32),
        compiler_params=pltpu.CompilerParams(dimension_semantics=("parallel",), vmem_limit_bytes=VMEM_LIMIT),
    )(a, b)


def _merge_fwd(pa, pb, pc, proj):
    def fn(i, rv, hv, fv):
        a, b, c, ga, gb, gc = rv
        return [_sig(ga) * a + _sig(gb) * b + _sig(gc) * c], []
    return _rows(fn, "merge_fwd", 256,
                 [(pa, D, 0), (pb, D, 0), (pc, D, 0), (proj, D, O_GA // D), (proj, D, O_GB // D), (proj, D, O_GC // D)],
                 outs=[(D, BF16)])[0]


def _post_fwd(x, o2, g):
    def fn(i, rv, hv, fv):
        xb, ob = rv
        return [xb + ob * _rms(ob) * fv[0]], []
    return _rows(fn, "post_fwd", 256, [(x, D, 0), (o2, D, 0)], fulls=[g], outs=[(D, F32)])[0]


SB = 640
BD_TM = 512


def _bd_fwd(xcb, wsb, l):
    def body(x_ref, w_ref, o_ref):
        o_ref[...] = jnp.dot(x_ref[...], w_ref[...], preferred_element_type=F32)

    return pl.pallas_call(
        body, name="lru_gate_mm", grid=(S // BD_TM, 4),
        in_specs=[pl.BlockSpec((BD_TM, SB), lambda i, q: (i, q % 2)),
                  pl.BlockSpec((None, None, SB, SB), lambda i, q: (l, q, 0, 0))],
        out_specs=pl.BlockSpec((BD_TM, SB), lambda i, q: (i, q)),
        out_shape=jax.ShapeDtypeStruct((S, 2 * LRU_W), F32),
        compiler_params=pltpu.CompilerParams(dimension_semantics=("parallel", "parallel"), vmem_limit_bytes=VMEM_LIMIT),
    )(xcb, wsb)


def _bd_dx(dgates, wsb, l):
    def body(d_ref, w_ref, o_ref, acc_ref):
        g = pl.program_id(2)

        @pl.when(g == 0)
        def _():
            acc_ref[...] = jnp.zeros_like(acc_ref)

        acc_ref[...] += lax.dot_general(d_ref[...], w_ref[...], (((1,), (1,)), ((), ())), preferred_element_type=F32)

        @pl.when(g == 1)
        def _():
            o_ref[...] = acc_ref[...]

    return pl.pallas_call(
        body, name="lru_gate_dx", grid=(S // BD_TM, 2, 2),
        in_specs=[pl.BlockSpec((BD_TM, SB), lambda i, s, g: (i, 2 * g + s)),
                  pl.BlockSpec((None, None, SB, SB), lambda i, s, g: (l, 2 * g + s, 0, 0))],
        out_specs=pl.BlockSpec((BD_TM, SB), lambda i, s, g: (i, s)),
        out_shape=jax.ShapeDtypeStruct((S, LRU_W), F32),
        scratch_shapes=[pltpu.VMEM((BD_TM, SB), F32)],
        compiler_params=pltpu.CompilerParams(dimension_semantics=("parallel", "parallel", "arbitrary"),
                                             vmem_limit_bytes=VMEM_LIMIT),
    )(dgates, wsb)


def _bd_dw(xcb, dgates):
    tk = 1024

    def body(x_ref, d_ref, o_ref):
        @pl.when(pl.program_id(1) == 0)
        def _():
            o_ref[...] = jnp.zeros_like(o_ref)

        o_ref[...] += lax.dot_general(x_ref[...], d_ref[...], (((0,), (0,)), ((), ())), preferred_element_type=F32)

    return pl.pallas_call(
        body, name="lru_gate_dw", grid=(4, S // tk),
        in_specs=[pl.BlockSpec((tk, SB), lambda q, k: (k, q % 2)), pl.BlockSpec((tk, SB), lambda q, k: (k, q))],
        out_specs=pl.BlockSpec((None, SB, SB), lambda q, k: (q, 0, 0)),
        out_shape=jax.ShapeDtypeStruct((4, SB, SB), F32),
        compiler_params=pltpu.CompilerParams(dimension_semantics=("parallel", "arbitrary"), vmem_limit_bytes=VMEM_LIMIT),
    )(xcb, dgates)


def _bd_extract(dwsb):
    def body(w_ref, o_ref):
        lane = lax.broadcasted_iota(jnp.int32, (LRU_BW, 128), 1)
        for q in range(4):
            for kk in range(8):
                c0 = LRU_BW * kk
                w0, off = (c0 // 128) * 128, c0 % 128
                rows = pl.ds(LRU_BW * kk, LRU_BW)
                blk = w_ref[q, rows, w0:w0 + 128]
                if off:
                    blk = pltpu.roll(blk, 128 - off, 1)
                    if off + LRU_BW > 128:
                        nxt = pltpu.roll(w_ref[q, rows, w0 + 128:w0 + 256], 128 - off, 1)
                        blk = jnp.where(lane < 128 - off, blk, nxt)
                o_ref[q // 2, 8 * (q % 2) + kk] = blk.astype(BF16)

    return pl.pallas_call(
        body, name="lru_gate_dw_blocks",
        in_specs=[pl.BlockSpec(memory_space=pltpu.VMEM)], out_specs=pl.BlockSpec(memory_space=pltpu.VMEM),
        out_shape=jax.ShapeDtypeStruct((2, LRU_NB, LRU_BW, 128), BF16),
        compiler_params=pltpu.CompilerParams(vmem_limit_bytes=VMEM_LIMIT),
    )(dwsb)


def _layer_fwd(x, P, l, tabs, token=None):
    A = {"x": x}
    A["h"] = _prenorm_fwd(x, P["pre_g"], token)
    proj = A["proj"] = _mm(A["h"], P["wp"], "nt", "in_proj", out_dtype=BF16, tm=1024)
    A["ya"] = _gmlp_fwd(proj, P["ln_g"], P["ln_b"], P["ws"], P["bst"])
    A["cqn"], A["ckvn"] = _mla_prep_fwd(proj, P["qg"], P["kvg"])
    q = _mm(A["cqn"], P["wuq"], "nt", "q_up")
    kv = _mm(A["ckvn"], P["wukv"], "nt", "kv_up")
    A["qc"], A["kc"], A["kct"], A["vv"], vt = _rope_fwd(q, kv, proj, tabs)
    A["o"], A["lse"] = _attn_fwd(A["qc"], A["kc"], vt)
    A["yb"] = _gate_mul_fwd("yb_fwd", A["o"], proj, MLA_W, O_ZB // MLA_W)
    A["xc"], A["xcb"] = _conv_fwd(proj, P["conv_w"], P["conv_b"])
    A["gates"] = _bd_fwd(A["xcb"], P["wsb"], l)
    A["a"], bterm = _lru_gates_fwd(A["gates"], A["xc"], P["ba"], P["bx"], P["lam"])
    A["hs"] = _scan_fwd(A["a"], bterm)
    A["yc"] = _gate_mul_fwd("yc_fwd", A["hs"], proj, LRU_W, O_ZC // LRU_W)
    A["pa"] = _mm(A["ya"], P["wpa"], "nn", "proj_a")
    A["pb"] = _mm(A["yb"], P["wpb"], "nn", "proj_b")
    A["pc"] = _mm(A["yc"], P["wpc"], "nn", "proj_c")
    A["merged"] = _merge_fwd(A["pa"], A["pb"], A["pc"], proj)
    A["o2"] = _mm(A["merged"], P["wout"], "nn", "out_proj")
    return _post_fwd(x, A["o2"], P["post_g"]), A


def _loss_fwd(y, tgt):
    def fn(i, rv, hv, fv):
        yb, tb = rv
        e = yb - tb
        part = 0.5 * jnp.sum(jnp.mean(e * e, axis=-1, keepdims=True), axis=0, keepdims=True)
        return [e * (1.0 / D)], [part]
    return _rows(fn, "loss", 256, [(y, D, 0), (tgt, D, 0)], outs=[(D, F32)], accs=[(1, 1)])


def _post_bwd(dxn, o2, g, token=None):
    def fn(i, rv, hv, fv):
        dy, ob = rv
        dx, dg = _rms_bwd(dy, ob, fv[0])
        return [dx], [_colsum(dg)]
    return _rows(fn, "post_bwd", 256, [(dxn, D, 0), (o2, D, 0)], fulls=[g] + ([] if token is None else [token]),
                 outs=[(D, BF16)], accs=[(1, D)])


def _merge_bwd(dm, pa, pb, pc, proj, dproj):
    def fn(i, rv, hv, fv):
        d, a, b, c, ga, gb, gc = rv
        outs_p, outs_g = [], []
        for p, gg in ((a, ga), (b, gb), (c, gc)):
            s = _sig(gg)
            outs_p.append(d * s)
            outs_g.append(d * p * s * (1.0 - s))
        return outs_p + [jnp.concatenate(outs_g, axis=1)], []
    return _rows(fn, "merge_bwd", 128,
                 [(dm, D, 0), (pa, D, 0), (pb, D, 0), (pc, D, 0),
                  (proj, D, O_GA // D), (proj, D, O_GB // D), (proj, D, O_GC // D)],
                 outs=[(D, BF16)] * 3 + [(3 * D, BF16, (dproj, NP, O_GA // (3 * D)))])


def _gmlp_bwd(dya, proj, ln_g, ln_b, ws, bst, dproj):
    gw = GM_W // GM_G

    def fn(i, rv, hv, fv):
        dy, u, v, z = rv
        g, b, w, bt = fv
        vh, rs, vn = _gm_norm(v, g, b)
        sv = _gm_sv(vn, w, bt)
        sz = _silu(z)
        du = dy * sv * sz
        dsv = dy * u * sz
        dz = dy * u * sv * _dsilu(z)
        mask = _gm_mask()
        lane = lax.broadcasted_iota(jnp.int32, (GM_B, 128), 1)
        dvn_parts, dws, dbst = [], [], jnp.zeros((GM_B, 128), F32)
        for k in range(GM_G):
            wm = jnp.where(mask, w[k], 0.0).astype(BF16)
            dsk = dsv[:, k * gw:(k + 1) * gw]
            dskb = dsk.astype(BF16)
            dvn_parts.append(lax.dot_general(wm, dskb, (((0,), (0,)), ((), ())), preferred_element_type=F32))
            dwk = lax.dot_general(dskb, vn[:, k * gw:(k + 1) * gw].astype(BF16), (((1,), (1,)), ((), ())),
                                  preferred_element_type=F32)
            dws.append(jnp.where(mask, dwk, 0.0)[None])
            dbst = dbst + jnp.where(lane == k, jnp.sum(dsk, axis=1, keepdims=True), 0.0)
        dvn = jnp.concatenate(dvn_parts, axis=1)
        dvh = dvn * g
        dv = rs * (dvh - jnp.mean(dvh, axis=-1, keepdims=True) - vh * jnp.mean(dvh * vh, axis=-1, keepdims=True))
        return ([jnp.concatenate([du, dv, dz], axis=1)],
                [jnp.concatenate(dws, axis=0), dbst, _colsum(dvn * vh), _colsum(dvn)])
    return _rows(fn, "gmlp_bwd", GM_B, [(dya, GM_W, 0), (proj, GM_W, 0), (proj, GM_W, 1), (proj, GM_W, 2)],
                 fulls=[ln_g, ln_b, ws, bst], outs=[(3 * GM_W, BF16, (dproj, NP, O_U // (3 * GM_W)))],
                 accs=[(GM_G, GM_B, GM_B), (GM_B, 128), (1, GM_W), (1, GM_W)])


def _yb_bwd(dyb, o, proj, dproj):
    def fn(i, rv, hv, fv):
        dy, ob, z = rv
        do = dy * _silu(z)
        prod = do * ob
        lane = lax.broadcasted_iota(jnp.int32, (dy.shape[0], 128), 1)
        dl = jnp.zeros((dy.shape[0], 128), F32)
        for h in range(H):
            dl = dl + jnp.where(lane == h, jnp.sum(prod[:, h * VDIM:(h + 1) * VDIM], axis=1, keepdims=True), 0.0)
        return [do, dl, dy * ob * _dsilu(z)], []
    return _rows(fn, "yb_bwd", 256, [(dyb, MLA_W, 0), (o, MLA_W, 0), (proj, MLA_W, O_ZB // MLA_W)],
                 outs=[(MLA_W, BF16), (128, F32, "T"), (MLA_W, BF16, (dproj, NP, O_ZB // MLA_W))])


def _attn_bwd(qc, kc, kct, vv, do, lse, dlt):
    def body(q_ref, k_ref, kt_ref, v_ref, do_ref, l_ref, d_ref, dq_ref, dk_ref, dv_ref, dqt_ref):
        h, i = pl.program_id(0), pl.program_id(1)

        @pl.when(i == 0)
        def _():
            dk_ref[...] = jnp.zeros_like(dk_ref)
            dv_ref[...] = jnp.zeros_like(dv_ref)

        q = q_ref[...]
        dob = do_ref[...]
        lse = l_ref[...]
        dl = d_ref[pl.ds(h, 1), :]
        dqt_ref[...] = jnp.zeros_like(dqt_ref)

        def rows_of(sb, c):
            return pl.ds(pl.multiple_of((sb * ATT_NB + c) * TC, TC), TC)

        def front(sb):
            return [(lax.dot_general(k_ref[rows_of(sb, c), :], q, _NT, preferred_element_type=F32),
                     lax.dot_general(v_ref[rows_of(sb, c), :], dob, _NT, preferred_element_type=F32))
                    for c in range(ATT_NB)]

        def block(sb, sd, masked):
            dqt = None
            for c, (s, dp) in enumerate(sd):
                rows = rows_of(sb, c)
                p = jnp.exp(s - lse)
                if masked:
                    p = jnp.where(_attn_allowed(i, sb * ATT_NB + c), p, 0.0)
                ds = (p * (dp - dl)).astype(BF16)
                dk_ref[rows, :] += jnp.dot(ds, q, preferred_element_type=F32)
                dv_ref[rows, :] += jnp.dot(p.astype(BF16), dob, preferred_element_type=F32)
                part = jnp.dot(kt_ref[:, rows], ds, preferred_element_type=F32)
                dqt = part if dqt is None else dqt + part
            dqt_ref[...] += dqt

        def step(sb, sd):
            nxt = front(sb + 1)
            block(sb, sd, False)
            return nxt

        nsb = (i + 2) // 2
        sd = lax.fori_loop(0, nsb - 1, step, front(0))
        block(nsb - 1, sd, True)
        dq_ref[...] = dqt_ref[...].T

    blk = lambda w: pl.BlockSpec((TQ, w), lambda h, i: (i, h))
    head = lambda w: pl.BlockSpec((S, w), lambda h, i: (0, h))
    return pl.pallas_call(
        body, name="attn_bwd", grid=(H, S // TQ),
        in_specs=[blk(HP), head(HP), pl.BlockSpec((HP, S), lambda h, i: (h, 0)), head(VDIM), blk(VDIM),
                  pl.BlockSpec((None, 1, TQ), lambda h, i: (h, 0, i)), pl.BlockSpec((8, TQ), lambda h, i: (0, i))],
        out_specs=[blk(HP), head(HP), head(VDIM)],
        out_shape=[jax.ShapeDtypeStruct((S, H * HP), F32), jax.ShapeDtypeStruct((S, H * HP), F32),
                   jax.ShapeDtypeStruct((S, MLA_W), F32)],
        scratch_shapes=[pltpu.VMEM((HP, TQ), F32)],
        compiler_params=pltpu.CompilerParams(dimension_semantics=("parallel", "arbitrary"),
                                             vmem_limit_bytes=VMEM_LIMIT),
    )(qc, kc, kct, vv, do, lse, dlt)


def _rope_bwd(dqc, dkc, dvv, tabs):
    def fn(i, rv, hv, fv):
        dq, dk, dv, cc, sa, sb = rv
        qs, ks = [], []
        dkr = jnp.zeros((dq.shape[0], 128), F32)
        for h in range(H):
            qs += [dq[:, h * HP:h * HP + 128] * ATT_SCALE, _rot_t(dq[:, h * HP + 128:(h + 1) * HP], cc, sa, sb) * ATT_SCALE]
            ks.append(dk[:, h * HP:h * HP + 128])
            dkr = dkr + dk[:, h * HP + 128:(h + 1) * HP]
        return [jnp.concatenate(qs, axis=1), jnp.concatenate(ks + [dv], axis=1), _rot_t(dkr, cc, sa, sb)], []
    cc, sa, sb = tabs
    return _rows(fn, "rope_bwd", 256,
                 [(dqc, H * HP, 0), (dkc, H * HP, 0), (dvv, MLA_W, 0), (cc, 128, 0), (sa, 128, 0), (sb, 128, 0)],
                 outs=[(H * HP, BF16), (H * 256, BF16), (128, BF16)])


MLA_GROUP = 1536


def _mla_prep_bwd(dcqn, dckvn, dkr, proj, qg, kvg, dproj):
    def fn(i, rv, hv, fv):
        d1, d2, dk, cq, ckv = rv
        g1, g2 = fv
        dx1, dg1 = _rms_bwd(d1, cq, g1)
        dx2, dg2 = _rms_bwd(d2, ckv, g2)
        zeros = jnp.zeros((d1.shape[0], MLA_GROUP - KVR - 128 - QR), F32)
        return [jnp.concatenate([dx2, dk.astype(F32), dx1, zeros], axis=1)], [_colsum(dg1), _colsum(dg2)]
    return _rows(fn, "mla_prep_bwd", 256,
                 [(dcqn, QR, 0), (dckvn, KVR, 0), (dkr, 128, 0), (proj, QR, O_CQ // QR), (proj, KVR, O_CKV // KVR)],
                 fulls=[qg, kvg], outs=[(MLA_GROUP, BF16, (dproj, NP, O_CKV // MLA_GROUP))], accs=[(1, QR), (1, KVR)])


def _yc_bwd(dyc, hs, proj, dproj):
    def fn(i, rv, hv, fv):
        dy, hh, z = rv
        return [dy * _silu(z), dy * hh * _dsilu(z)], []
    return _rows(fn, "yc_bwd", 128, [(dyc, LRU_W, 0), (hs, LRU_W, 0), (proj, LRU_W, O_ZC // LRU_W)],
                 outs=[(LRU_W, F32), (LRU_W, BF16, (dproj, NP, O_ZC // LRU_W))])


def _scan_bwd(a, hs, dh):
    nblk = S // SCAN_T

    def body(a_ref, h_ref, dh_ref, da_ref, db_ref):
        row = lax.broadcasted_iota(jnp.int32, (SCAN_T, SCAN_CW), 0)

        def step(j, carry):
            gc, ac = carry
            blk = nblk - 1 - j
            t0 = pl.multiple_of(blk * SCAN_T, SCAN_T)
            av = a_ref[pl.ds(t0, SCAN_T), :]
            A = jnp.where(row < SCAN_T - 1, pltpu.roll(av, SCAN_T - 1, 0), ac)
            B = dh_ref[pl.ds(t0, SCAN_T), :]
            d = 1
            while d < SCAN_T:
                keep = row < SCAN_T - d
                A_s = jnp.where(keep, pltpu.roll(A, SCAN_T - d, 0), 1.0)
                B_s = jnp.where(keep, pltpu.roll(B, SCAN_T - d, 0), 0.0)
                B = A * B_s + B
                A = A * A_s
                d *= 2
            g = A * gc + B
            p0 = pl.multiple_of(jnp.maximum(t0 - 8, 0), 8)
            last = jnp.where(blk > 0, h_ref[pl.ds(p0, 8), :][7:8, :], 0.0)
            h_prev = jnp.where(row >= 1, pltpu.roll(h_ref[pl.ds(t0, SCAN_T), :], 1, 0), last)
            da_ref[pl.ds(t0, SCAN_T), :] = g * h_prev
            db_ref[pl.ds(t0, SCAN_T), :] = g
            return g[0:1, :], av[0:1, :]

        z = jnp.zeros((1, SCAN_CW), F32)
        lax.fori_loop(0, nblk, step, (z, z))

    spec = pl.BlockSpec((S, SCAN_CW), lambda j: (0, j))
    return pl.pallas_call(
        body, name="scan_bwd", grid=(LRU_W // SCAN_CW,), in_specs=[spec] * 3, out_specs=[spec] * 2,
        out_shape=[jax.ShapeDtypeStruct((S, LRU_W), F32)] * 2,
        compiler_params=pltpu.CompilerParams(dimension_semantics=("parallel",), vmem_limit_bytes=VMEM_LIMIT),
    )(a, hs, dh)


def _lru_gates_bwd(da, db, gates, xc, ba, bx, lam):
    def fn(i, rv, hv, fv):
        dav, dbv, ga, gx, x = rv
        bav, bxv, lamv = fv
        r, ig, sp, a, e2, om, mult = _lru_terms(ga, gx, x, bav, bxv, lamv)
        dmult = dbv * ig * x
        dig = dbv * mult * x
        dxc1 = dbv * mult * ig
        dlog_a = dav * a + jnp.where(om > 0.0, dmult * (-e2 / mult), 0.0)
        dr = dlog_a * (-LRU_C * sp)
        dga = dr * r * (1.0 - r)
        dgx = dig * ig * (1.0 - ig)
        dlam = _colsum(dlog_a * (-LRU_C * r)) * (-_sig(-lamv))
        return [jnp.concatenate([dga, dgx], axis=1), dxc1], [_colsum(dga), _colsum(dgx), dlam]
    return _rows(fn, "lru_gates_bwd", 128,
                 [(da, LRU_W, 0), (db, LRU_W, 0), (gates, LRU_W, 0), (gates, LRU_W, 1), (xc, LRU_W, 0)],
                 fulls=[ba, bx, lam], outs=[(2 * LRU_W, BF16), (LRU_W, F32)], accs=[(1, LRU_W)] * 3)


def _conv_bwd(dxc1, dxc2, proj, w, dproj):
    cb = O_XC // LRU_W

    def fn(i, rv, hv, fv):
        d1, d2, xb = rv
        n1, n2, xprev = hv
        ww = fv[0]
        last = i == S // 128 - 1
        dxc = d1 + d2
        nxt = jnp.where(last, 0.0, n1 + n2)
        xprev = jnp.where(i > 0, xprev, 0.0)
        row = lax.broadcasted_iota(jnp.int32, xb.shape, 0)
        dx = ww[3:4] * dxc
        dws = [None] * CONV_W
        dws[3] = _colsum(dxc * xb)
        for s in range(1, CONV_W):
            dx = dx + ww[3 - s:4 - s] * _shift_up(dxc, nxt, s, row)
            dws[3 - s] = _colsum(dxc * _shift_down(xb, xprev, s, row))
        return [dx], [jnp.concatenate(dws, axis=0), _colsum(dxc)]
    return _rows(fn, "conv_bwd", 128, [(dxc1, LRU_W, 0), (dxc2, LRU_W, 0), (proj, LRU_W, cb)],
                 halos=[(dxc1, LRU_W, 0, "next"), (dxc2, LRU_W, 0, "next"), (proj, LRU_W, cb, "prev")],
                 fulls=[w], outs=[(LRU_W, BF16, (dproj, NP, cb))], accs=[(CONV_W, LRU_W), (1, LRU_W)])


def _prenorm_bwd(dxn, dh, x, g):
    def fn(i, rv, hv, fv):
        dy, dhh, xb = rv
        dx, dg = _rms_bwd(dhh, xb, fv[0])
        return [dy + dx], [_colsum(dg)]
    return _rows(fn, "prenorm_bwd", 256, [(dxn, D, 0), (dh, D, 0), (x, D, 0)], fulls=[g], outs=[(D, F32)],
                 accs=[(1, D)])


def _layer_bwd(dxn, A, P, l, tabs, token=None):
    G, GB = {}, {}
    proj = A["proj"]

    def dw(key, a, b, name, **tiles):
        GB[key] = _mm(a, b, "tn", name, out_dtype=BF16, **tiles)

    do2, G["post_g"] = _post_bwd(dxn, A["o2"], P["post_g"], token)
    dm = _mm(do2, P["wout"], "nt", "out_proj_dx")
    dw("wout", A["merged"], do2, "out_proj_dw")
    dpa, dpb, dpc, dproj = _merge_bwd(dm, A["pa"], A["pb"], A["pc"], proj, None)
    dya = _mm(dpa, P["wpa"], "nt", "proj_a_dx")
    dw("wpa", A["ya"], dpa, "proj_a_dw")
    dyb = _mm(dpb, P["wpb"], "nt", "proj_b_dx")
    dw("wpb", A["yb"], dpb, "proj_b_dw")
    dyc = _mm(dpc, P["wpc"], "nt", "proj_c_dx")
    dw("wpc", A["yc"], dpc, "proj_c_dw")
    dproj, G["ws"], G["bst"], G["ln_g"], G["ln_b"] = _gmlp_bwd(dya, proj, P["ln_g"], P["ln_b"], P["ws"], P["bst"], dproj)
    do, dl, dproj = _yb_bwd(dyb, A["o"], proj, dproj)
    dqc, dkc, dvv = _attn_bwd(A["qc"], A["kc"], A["kct"], A["vv"], do, A["lse"], dl)
    dq, dkv, dkr = _rope_bwd(dqc, dkc, dvv, tabs)
    dcqn = _mm(dq, P["wuq"], "nn", "q_up_dx")
    dw("wuq", dq, A["cqn"], "q_up_dw")
    dckvn = _mm(dkv, P["wukv"], "nn", "kv_up_dx")
    dw("wukv", dkv, A["ckvn"], "kv_up_dw")
    dproj, G["qg"], G["kvg"] = _mla_prep_bwd(dcqn, dckvn, dkr, proj, P["qg"], P["kvg"], dproj)
    dhs, dproj = _yc_bwd(dyc, A["hs"], proj, dproj)
    da, db = _scan_bwd(A["a"], A["hs"], dhs)
    dgates, dxc1, G["ba"], G["bx"], G["lam"] = _lru_gates_bwd(da, db, A["gates"], A["xc"], P["ba"], P["bx"], P["lam"])
    dxc2 = _bd_dx(dgates, P["wsb"], l)
    G["wab"] = _bd_extract(_bd_dw(A["xcb"], dgates))
    dproj, G["conv_w"], G["conv_b"] = _conv_bwd(dxc1, dxc2, proj, P["conv_w"], dproj)
    dh = _mm(dproj, P["wp"], "nn", "in_proj_dx", tm=1024, tn=1024)
    dw("wp", dproj, A["h"], "in_proj_dw", tm=1536, tn=1024)
    dx, G["pre_g"] = _prenorm_bwd(dxn, dh, A["x"], P["pre_g"])
    return dx, G, GB


_ORIG_OFF = [0]
for _s in IN_SIZES:
    _ORIG_OFF.append(_ORIG_OFF[-1] + _s)
_PAD_OFF = {0: O_U, 1: O_V, 2: O_ZA, 3: O_CQ, 4: O_CKV, 5: O_KR, 6: O_ZB, 7: O_XC, 8: O_ZC, 9: O_GA, 10: O_GB, 11: O_GC}
SHARD_IN = N_IN // N_CHIPS


def _pieces_w_in(j):
    lo, hi = SHARD_IN * j, SHARD_IN * (j + 1)
    out = []
    for k in range(len(IN_SIZES)):
        a, b = max(lo, _ORIG_OFF[k]), min(hi, _ORIG_OFF[k + 1])
        if a < b:
            out.append((a - lo, _PAD_OFF[k] + a - _ORIG_OFF[k], b - a))
    return out


def _pieces_uq(j):
    return [(192 * hh, HP * (2 * j + hh), NOPE + ROPE) for hh in range(2)]


def _pieces_ukv(j):
    out = []
    for hh in range(2):
        h = 2 * j + hh
        out += [(256 * hh, NOPE * h, NOPE), (256 * hh + NOPE, H * NOPE + VDIM * h, VDIM)]
    return out


def _pieces_rows(r):
    return lambda j: [(0, r * j, r)]


LAYOUT = {
    "w_in": (SHARD_IN, NP, _pieces_w_in),
    "mla_w_uq": (2 * (NOPE + ROPE), H * HP, _pieces_uq),
    "mla_w_ukv": (2 * (NOPE + VDIM), 2 * H * 128, _pieces_ukv),
    "lru_conv_w": (1, N_CHIPS, _pieces_rows(1)),
    "w_proj_a": (GM_W // N_CHIPS, GM_W, _pieces_rows(GM_W // N_CHIPS)),
    "w_proj_b": (MLA_W // N_CHIPS, MLA_W, _pieces_rows(MLA_W // N_CHIPS)),
    "w_proj_c": (LRU_W // N_CHIPS, LRU_W, _pieces_rows(LRU_W // N_CHIPS)),
    "w_out": (D // N_CHIPS, D, _pieces_rows(D // N_CHIPS)),
}
TRANSPOSED = ("w_in", "mla_w_uq", "mla_w_ukv")


def _superblocks(w_a, w_x):
    w6 = jnp.stack([w_a, w_x], axis=1).reshape(DEPTH, 4, 8, LRU_BW, LRU_BW).astype(BF16)
    bands = [jnp.pad(w6[:, :, k], ((0, 0), (0, 0), (0, 0), (LRU_BW * k, SB - LRU_BW * (k + 1)))) for k in range(8)]
    return jnp.concatenate(bands, axis=2)


_HBM = pl.BlockSpec(memory_space=pltpu.HBM)


def _position():
    return lax.axis_index("x"), lax.axis_index("y"), lax.axis_index("c")


def _allgather(blocks, name):
    n = len(blocks)

    def body(*refs):
        ins, outs = refs[:n], refs[n:2 * n]
        send, recv, lsem = refs[2 * n:]
        x, y, c = _position()
        me, sib = (x, y, c), (x, y, 1 - c)
        chips = [(1 - x, y), (x, 1 - y), (1 - x, 1 - y)]

        def cp(k, a, block, to, src=None):
            dst = outs[a].at[4 * block[0] + 2 * block[1] + block[2]]
            return pltpu.make_async_remote_copy(src_ref=dst if src is None else src, dst_ref=dst,
                                                send_sem=send.at[7 * a + k], recv_sem=recv.at[7 * a + k],
                                                device_id=to, device_id_type=MESH)

        mine = [pltpu.make_async_copy(ins[a], outs[a].at[4 * x + 2 * y + c], lsem.at[a]) for a in range(n)]
        for m in mine:
            m.start()
        first = []
        for a in range(n):
            first.append(cp(0, a, me, sib, src=ins[a]))
            first += [cp(1 + j, a, me, (*chip, c), src=ins[a]) for j, chip in enumerate(chips)]
        for f in first:
            f.start()
        passed = []
        for j, chip in enumerate(chips):
            for a in range(n):
                cp(1 + j, a, (*chip, c), me).wait_recv()
                p = cp(4 + j, a, (*chip, c), sib)
                p.start()
                passed.append(p)
        for a in range(n):
            cp(0, a, sib, me).wait_recv()
            for j, chip in enumerate(chips):
                cp(4 + j, a, (*chip, 1 - c), me).wait_recv()
        for f in first + passed:
            f.wait_send()
        for m in mine:
            m.wait()

    return pl.pallas_call(
        body, name=name,
        out_shape=[jax.ShapeDtypeStruct((8,) + b.shape, b.dtype) for b in blocks],
        in_specs=[_HBM] * n, out_specs=[_HBM] * n,
        scratch_shapes=[pltpu.SemaphoreType.DMA((7 * n,)), pltpu.SemaphoreType.DMA((7 * n,)),
                        pltpu.SemaphoreType.DMA((n,))],
    )(*blocks)


_REL = (2, 1, 3)


def _cut(r):
    return r if r < 32 else (r // 2 + 15) // 16 * 16


def _half_rows(r, c0):
    return _cut(r) if c0 == 0 else r - _cut(r)


def _half_pieces(lay_a, jsrc, c0):
    r = lay_a[0]
    lo, hi = (0, _cut(r)) if c0 == 0 else (_cut(r), r)
    out = []
    for s0, d0, nr in lay_a[2](jsrc):
        a, b = max(s0, lo), min(s0 + nr, hi)
        if a < b:
            out.append((a, d0 + a - s0, b - a))
    return out


def _gather_zeros(names, srcs):
    return [jnp.zeros((LAYOUT[nm][1],) + s.shape[1:], s.dtype) for nm, s in zip(names, srcs)]


def _weights_allgather(names, srcs, name):
    n = len(srcs)
    lay = [LAYOUT[nm] for nm in names]
    zeros = _gather_zeros(names, srcs)

    def body(*refs):
        ins, outs = refs[:n], refs[2 * n:3 * n]
        send, recv, lsem = refs[3 * n:]
        x, y, c = _position()
        j = 2 * x + y
        sib = (x, y, 1 - c)
        chips = [(1 - x, y), (x, 1 - y), (1 - x, 1 - y)]

        def flow(a, k, jsrc, c0, to, from_src):
            cps = []
            for s0, d0, nr in _half_pieces(lay[a], jsrc, c0):
                dst = outs[a].at[pl.ds(d0, nr)]
                src = ins[a].at[pl.ds(s0, nr)] if from_src else dst
                cps.append(pltpu.make_async_remote_copy(src_ref=src, dst_ref=dst, send_sem=send.at[7 * a + k],
                                                        recv_sem=recv.at[7 * a + k], device_id=to, device_id_type=MESH))
            return cps

        def sized(a, k, rows):
            ref = ins[a].at[pl.ds(0, rows)]
            return pltpu.make_async_remote_copy(src_ref=ref, dst_ref=ref, send_sem=send.at[7 * a + k],
                                                recv_sem=recv.at[7 * a + k], device_id=sib, device_id_type=MESH)

        for j0 in range(N_CHIPS):
            for c0 in range(2):
                @pl.when((j == j0) & (c == c0))
                def _(j0=j0, c0=c0):
                    mine = [_half_rows(lay[a][0], c0) for a in range(n)]
                    theirs = [_half_rows(lay[a][0], 1 - c0) for a in range(n)]
                    for a in range(n):
                        for s0, d0, nr in _half_pieces(lay[a], j0, c0):
                            pltpu.make_async_copy(ins[a].at[pl.ds(s0, nr)], outs[a].at[pl.ds(d0, nr)], lsem.at[a]).start()
                    for a in range(n):
                        for cp in flow(a, 0, j0, c0, sib, True):
                            cp.start()
                        for k, chip in enumerate(chips):
                            for cp in flow(a, 1 + k, j0, c0, (*chip, c), True):
                                cp.start()
                    for k in range(3):
                        for a in range(n):
                            if mine[a]:
                                sized(a, 1 + k, mine[a]).wait_recv()
                                for cp in flow(a, 4 + k, j0 ^ _REL[k], c0, sib, False):
                                    cp.start()
                    for a in range(n):
                        if theirs[a]:
                            sized(a, 0, theirs[a]).wait_recv()
                            for k in range(3):
                                sized(a, 4 + k, theirs[a]).wait_recv()
                    for a in range(n):
                        if mine[a]:
                            for k in range(7):
                                sized(a, k, mine[a]).wait_send()
                            ref = ins[a].at[pl.ds(0, mine[a])]
                            pltpu.make_async_copy(ref, ref, lsem.at[a]).wait()

    return pl.pallas_call(
        body, name=name,
        out_shape=[jax.ShapeDtypeStruct(z.shape, z.dtype) for z in zeros],
        in_specs=[_HBM] * (2 * n), out_specs=[_HBM] * n,
        input_output_aliases={n + a: a for a in range(n)},
        scratch_shapes=[pltpu.SemaphoreType.DMA((7 * n,)), pltpu.SemaphoreType.DMA((7 * n,)),
                        pltpu.SemaphoreType.DMA((n,))],
    )(*srcs, *zeros)


_SEM = pl.BlockSpec(memory_space=pltpu.SEMAPHORE)
_VMEM_TOKEN = pl.BlockSpec(memory_space=pltpu.VMEM)
_TOKEN = jax.ShapeDtypeStruct((8, 128), F32)
_EFFECT = pltpu.SideEffectType.DATAFLOW_SIDE_EFFECTING


def _gather_start(names, srcs, name):
    n = len(srcs)
    lay = [LAYOUT[nm] for nm in names]
    zeros = _gather_zeros(names, srcs)

    def body(*refs):
        ins, lands = refs[:n], refs[n:2 * n]
        send, recv, lsem = refs[2 * n:2 * n + 3]
        refs[-1][...] = jnp.zeros_like(refs[-1])
        x, y, c = _position()
        j = 2 * x + y
        chips = [(1 - x, y), (x, 1 - y), (1 - x, 1 - y)]
        for j0 in range(N_CHIPS):
            @pl.when(j == j0)
            def _(j0=j0):
                for a in range(n):
                    for s0, d0, nr in lay[a][2](j0):
                        src, dst = ins[a].at[pl.ds(s0, nr)], lands[a].at[pl.ds(d0, nr)]
                        pltpu.make_async_copy(src, dst, lsem.at[a]).start()
                        for k, chip in enumerate(chips):
                            pltpu.make_async_remote_copy(src_ref=src, dst_ref=dst, send_sem=send.at[3 * a + k],
                                                         recv_sem=recv.at[3 * a + k], device_id=(*chip, c),
                                                         device_id_type=MESH).start()

    sems = [pltpu.SemaphoreType.DMA((3 * n,)), pltpu.SemaphoreType.DMA((3 * n,)), pltpu.SemaphoreType.DMA((n,))]
    hbm = lambda a: pltpu.HBM(a.shape, a.dtype)
    res = pl.pallas_call(
        body, name=name,
        out_shape=sems + [hbm(s) for s in srcs] + [hbm(z) for z in zeros] + [_TOKEN],
        in_specs=[_HBM] * (2 * n), out_specs=[_SEM] * 3 + [_HBM] * (2 * n) + [_VMEM_TOKEN],
        input_output_aliases={a: 3 + a for a in range(2 * n)},
        compiler_params=pltpu.CompilerParams(has_side_effects=_EFFECT),
    )(*[pltpu.with_memory_space_constraint(s, pltpu.HBM) for s in srcs],
      *[pltpu.with_memory_space_constraint(z, pltpu.HBM) for z in zeros])
    return res[:3], res[3:3 + n], res[3 + n:3 + 2 * n], res[-1]


def _gather_wait(names, sems, srcs, lands, after, name):
    n = len(srcs)
    lay = [LAYOUT[nm] for nm in names]

    def body(*refs):
        ins, zones = refs[:n], refs[n:2 * n]
        send, recv, lsem = refs[2 * n:2 * n + 3]
        x, y, c = _position()
        for a in range(n):
            whole = zones[a].at[pl.ds(0, lay[a][0])]
            for k in range(3):
                cp = pltpu.make_async_remote_copy(src_ref=ins[a], dst_ref=whole, send_sem=send.at[3 * a + k],
                                                  recv_sem=recv.at[3 * a + k], device_id=(x, y, 1 - c),
                                                  device_id_type=MESH)
                cp.wait_send()
                cp.wait_recv()
            pltpu.make_async_copy(ins[a], whole, lsem.at[a]).wait()

    hbm = lambda a: pltpu.HBM(a.shape, a.dtype)
    res = pl.pallas_call(
        body, name=name,
        out_shape=[hbm(s) for s in srcs] + [hbm(z) for z in lands],
        in_specs=[_HBM] * (2 * n) + [_SEM] * 3 + [pl.BlockSpec(memory_space=pl.ANY)], out_specs=[_HBM] * (2 * n),
        input_output_aliases={a: a for a in range(2 * n)},
        compiler_params=pltpu.CompilerParams(has_side_effects=_EFFECT),
    )(*srcs, *lands, *sems, after)
    return res[n:]


def _clip_pieces(lay_a, jsrc, c0):
    h = lay_a[1] // 2
    lo, hi = c0 * h, (c0 + 1) * h
    out = []
    for s0, d0, nr in lay_a[2](jsrc):
        a, b = max(d0, lo), min(d0 + nr, hi)
        if a < b:
            out.append((s0 + a - d0, a, b - a))
    return out


def _rows_of(pieces):
    return sum(nr for _, _, nr in pieces)


def _both_cores(body_for):
    x, y, c = _position()
    j = 2 * x + y
    for j0 in range(N_CHIPS):
        for c0 in range(2):
            @pl.when((j == j0) & (c == c0))
            def _(j0=j0, c0=c0):
                body_for(j0, c0)


STAGE_ROWS = 512


def _staged_copy(src, dst, buf, sem_in, sem_out, rows):
    ch = buf.shape[0]
    for r in range(0, rows, ch):
        nr = min(ch, rows - r)
        stage = buf.at[pl.ds(0, nr)]
        cin = pltpu.make_async_copy(src.at[pl.ds(r, nr)], stage, sem_in)
        cin.start()
        cin.wait()
        cout = pltpu.make_async_copy(stage, dst.at[pl.ds(r, nr)], sem_out)
        cout.start()
        cout.wait()


def _half_to_sibling(names, gl, name):
    n = len(gl)
    halves = [LAYOUT[nm][1] // 2 for nm in names]

    def body(*refs):
        ins, outs = refs[:n], refs[n:2 * n]
        send, recv = refs[2 * n:]
        x, y, c = _position()

        def run(j0, c0):
            cps = [pltpu.make_async_remote_copy(src_ref=ins[a].at[pl.ds((1 - c0) * halves[a], halves[a])], dst_ref=outs[a],
                                                send_sem=send.at[a], recv_sem=recv.at[a], device_id=(x, y, 1 - c),
                                                device_id_type=MESH) for a in range(n)]
            for cp in cps:
                cp.start()
            for cp in cps:
                cp.wait()

        _both_cores(run)

    return pl.pallas_call(
        body, name=name,
        out_shape=[jax.ShapeDtypeStruct((halves[a],) + gl[a].shape[1:], gl[a].dtype) for a in range(n)],
        in_specs=[_HBM] * n, out_specs=[_HBM] * n,
        scratch_shapes=[pltpu.SemaphoreType.DMA((n,)), pltpu.SemaphoreType.DMA((n,))],
    )(*gl)


def _chip_scatter_half(names, parts, name):
    n = len(parts)
    lay = [LAYOUT[nm] for nm in names]
    zeros = [jnp.zeros((N_CHIPS, lay[a][0]) + parts[a].shape[1:], parts[a].dtype) for a in range(n)]

    def body(*refs):
        ins, outs = refs[:n], refs[2 * n:3 * n]
        send, recv = refs[3 * n:3 * n + 2]
        stage, sem_in, sem_out = refs[3 * n + 2:4 * n + 2], refs[4 * n + 2], refs[4 * n + 3]
        x, y, c = _position()
        chips = [(1 - x, y), (x, 1 - y), (1 - x, 1 - y)]

        def run(j0, c0):
            def sized(a, rows):
                return outs[a].at[0, pl.ds(0, rows)]

            for a in range(n):
                base = c0 * (lay[a][1] // 2)
                for k, chip in enumerate(chips):
                    for s0, d0, nr in _clip_pieces(lay[a], j0 ^ _REL[k], c0):
                        pltpu.make_async_remote_copy(
                            src_ref=ins[a].at[pl.ds(d0 - base, nr)], dst_ref=outs[a].at[j0, pl.ds(s0, nr)],
                            send_sem=send.at[3 * a + k], recv_sem=recv.at[3 * a + k],
                            device_id=(*chip, c), device_id_type=MESH).start()
            for a in range(n):
                base = c0 * (lay[a][1] // 2)
                for s0, d0, nr in _clip_pieces(lay[a], j0, c0):
                    _staged_copy(ins[a].at[pl.ds(d0 - base, nr)], outs[a].at[j0, pl.ds(s0, nr)], stage[a],
                                 sem_in.at[a], sem_out.at[a], nr)
            for a in range(n):
                got = _rows_of(_clip_pieces(lay[a], j0, c0))
                for k in range(3):
                    sent = _rows_of(_clip_pieces(lay[a], j0 ^ _REL[k], c0))
                    if sent:
                        pltpu.make_async_remote_copy(src_ref=sized(a, sent), dst_ref=sized(a, sent),
                                                     send_sem=send.at[3 * a + k], recv_sem=recv.at[3 * a + k],
                                                     device_id=(x, y, c), device_id_type=MESH).wait_send()
                    if got:
                        pltpu.make_async_remote_copy(src_ref=sized(a, got), dst_ref=sized(a, got),
                                                     send_sem=send.at[3 * a + k], recv_sem=recv.at[3 * a + k],
                                                     device_id=(x, y, c), device_id_type=MESH).wait_recv()

        _both_cores(run)

    return pl.pallas_call(
        body, name=name,
        out_shape=[jax.ShapeDtypeStruct(z.shape, z.dtype) for z in zeros],
        in_specs=[_HBM] * (2 * n), out_specs=[_HBM] * n, input_output_aliases={n + a: a for a in range(n)},
        scratch_shapes=[pltpu.SemaphoreType.DMA((3 * n,)), pltpu.SemaphoreType.DMA((3 * n,))]
        + [pltpu.VMEM((min(STAGE_ROWS, p.shape[0]),) + p.shape[1:], p.dtype) for p in parts]
        + [pltpu.SemaphoreType.DMA((n,)), pltpu.SemaphoreType.DMA((n,))],
    )(*parts, *zeros)


def _subset_exchange(names, bufs, l, name):
    n = len(bufs)
    lay = [LAYOUT[nm] for nm in names]

    def body(*refs):
        outs = refs[n:2 * n]
        send, recv = refs[2 * n:]
        x, y, c = _position()

        def run(j0, c0):
            for a in range(n):
                for s0, _, nr in _clip_pieces(lay[a], j0, c0):
                    rows = outs[a].at[l, pl.ds(s0, nr)]
                    pltpu.make_async_remote_copy(src_ref=rows, dst_ref=rows, send_sem=send.at[a], recv_sem=recv.at[a],
                                                 device_id=(x, y, 1 - c), device_id_type=MESH).start()
            for a in range(n):
                for c_half, wait_send in ((c0, True), (1 - c0, False)):
                    rows = _rows_of(_clip_pieces(lay[a], j0, c_half))
                    if rows:
                        ref = outs[a].at[l, pl.ds(0, rows)]
                        cp = pltpu.make_async_remote_copy(src_ref=ref, dst_ref=ref, send_sem=send.at[a], recv_sem=recv.at[a],
                                                          device_id=(x, y, 1 - c), device_id_type=MESH)
                        if wait_send:
                            cp.wait_send()
                        else:
                            cp.wait_recv()

        _both_cores(run)

    return pl.pallas_call(
        body, name=name,
        out_shape=[jax.ShapeDtypeStruct(b.shape, b.dtype) for b in bufs],
        in_specs=[_HBM] * n, out_specs=[_HBM] * n, input_output_aliases={a: a for a in range(n)},
        scratch_shapes=[pltpu.SemaphoreType.DMA((n,)), pltpu.SemaphoreType.DMA((n,))],
    )(*bufs)


def _scatter_start(names, gl, name):
    n = len(gl)
    lay = [LAYOUT[nm] for nm in names]
    zones = [lax.empty((N_CHIPS, lay[a][0]) + gl[a].shape[1:], gl[a].dtype) for a in range(n)]

    def body(*refs):
        ins, lands = refs[:n], refs[n:2 * n]
        send, recv, lsem = refs[2 * n:2 * n + 3]
        refs[-1][...] = jnp.zeros_like(refs[-1])
        x, y, c = _position()
        j = 2 * x + y
        chips = [(1 - x, y), (x, 1 - y), (1 - x, 1 - y)]
        for j0 in range(N_CHIPS):
            @pl.when(j == j0)
            def _(j0=j0):
                for a in range(n):
                    for s0, d0, nr in lay[a][2](j0):
                        pltpu.make_async_copy(ins[a].at[pl.ds(d0, nr)], lands[a].at[j0, pl.ds(s0, nr)], lsem.at[a]).start()
                    for k, chip in enumerate(chips):
                        for s0, d0, nr in lay[a][2](j0 ^ _REL[k]):
                            pltpu.make_async_remote_copy(
                                src_ref=ins[a].at[pl.ds(d0, nr)], dst_ref=lands[a].at[j0, pl.ds(s0, nr)],
                                send_sem=send.at[3 * a + k], recv_sem=recv.at[3 * a + k],
                                device_id=(*chip, c), device_id_type=MESH).start()

    sems = [pltpu.SemaphoreType.DMA((3 * n,)), pltpu.SemaphoreType.DMA((3 * n,)), pltpu.SemaphoreType.DMA((n,))]
    hbm = lambda a: pltpu.HBM(a.shape, a.dtype)
    res = pl.pallas_call(
        body, name=name,
        out_shape=sems + [hbm(g) for g in gl] + [hbm(z) for z in zones] + [_TOKEN],
        in_specs=[_HBM] * (2 * n), out_specs=[_SEM] * 3 + [_HBM] * (2 * n) + [_VMEM_TOKEN],
        input_output_aliases={a: 3 + a for a in range(2 * n)},
        compiler_params=pltpu.CompilerParams(has_side_effects=_EFFECT),
    )(*[pltpu.with_memory_space_constraint(g, pltpu.HBM) for g in gl],
      *[pltpu.with_memory_space_constraint(z, pltpu.HBM) for z in zones])
    return res[:3], res[3:3 + n], res[3 + n:3 + 2 * n], res[-1]


def _scatter_wait(names, sems, srcs, lands, after, name):
    n = len(srcs)
    lay = [LAYOUT[nm] for nm in names]

    def body(*refs):
        zones = refs[n:2 * n]
        send, recv, lsem = refs[2 * n:2 * n + 3]
        x, y, c = _position()
        for a in range(n):
            whole = zones[a].at[0, pl.ds(0, lay[a][0])]
            for k in range(3):
                cp = pltpu.make_async_remote_copy(src_ref=whole, dst_ref=whole, send_sem=send.at[3 * a + k],
                                                  recv_sem=recv.at[3 * a + k], device_id=(x, y, 1 - c),
                                                  device_id_type=MESH)
                cp.wait_send()
                cp.wait_recv()
            pltpu.make_async_copy(whole, whole, lsem.at[a]).wait()

    hbm = lambda a: pltpu.HBM(a.shape, a.dtype)
    res = pl.pallas_call(
        body, name=name,
        out_shape=[hbm(s) for s in srcs] + [hbm(z) for z in lands],
        in_specs=[_HBM] * (2 * n) + [_SEM] * 3 + [pl.BlockSpec(memory_space=pl.ANY)], out_specs=[_HBM] * (2 * n),
        input_output_aliases={a: a for a in range(2 * n)},
        compiler_params=pltpu.CompilerParams(has_side_effects=_EFFECT),
    )(*srcs, *lands, *sems, after)
    return res[n:]


def _sibling_swap(arrs, name):
    n = len(arrs)

    def body(*refs):
        ins, outs = refs[:n], refs[n:2 * n]
        send, recv = refs[2 * n:]
        x, y, c = _position()
        cps = [pltpu.make_async_remote_copy(src_ref=ins[a], dst_ref=outs[a], send_sem=send.at[a], recv_sem=recv.at[a],
                                            device_id=(x, y, 1 - c), device_id_type=MESH) for a in range(n)]
        for cp in cps:
            cp.start()
        for cp in cps:
            cp.wait()

    return pl.pallas_call(
        body, name=name,
        out_shape=[jax.ShapeDtypeStruct(a.shape, a.dtype) for a in arrs],
        in_specs=[_HBM] * n, out_specs=[_HBM] * n,
        scratch_shapes=[pltpu.SemaphoreType.DMA((n,)), pltpu.SemaphoreType.DMA((n,))],
    )(*arrs)


def _row_tile(r):
    for t in (256, 128, 64, 32, 16, 8):
        if r % t == 0 and r > t:
            return t
    return r


def _pair_add_half(g, rb, c_arr, name):
    hrows, rest = rb.shape[0], rb.shape[1:]
    tr = _row_tile(hrows)
    nb = hrows // tr
    z = (0,) * len(rest)

    def body(c_ref, g_ref, r_ref, o_ref):
        o_ref[...] = (g_ref[...].astype(F32) + r_ref[...].astype(F32)).astype(o_ref.dtype)

    return pl.pallas_call(
        body, name=name,
        grid_spec=pltpu.PrefetchScalarGridSpec(
            num_scalar_prefetch=1, grid=(nb,),
            in_specs=[pl.BlockSpec((tr,) + rest, lambda i, c_ref: (c_ref[0] * nb + i,) + z),
                      pl.BlockSpec((tr,) + rest, lambda i, c_ref: (i,) + z)],
            out_specs=pl.BlockSpec((tr,) + rest, lambda i, c_ref: (i,) + z)),
        out_shape=jax.ShapeDtypeStruct((hrows,) + rest, BF16),
        compiler_params=pltpu.CompilerParams(dimension_semantics=("parallel",), vmem_limit_bytes=VMEM_LIMIT),
    )(c_arr, g, rb)


def _sum_slabs(slabs, l, buf, name):
    m = len(slabs)
    n, R, rest = slabs[0].shape[0], slabs[0].shape[1], slabs[0].shape[2:]
    tr = _row_tile(R)
    z = (0,) * len(rest)

    def body(*refs):
        total = None
        for r_ref in refs[:m]:
            acc = r_ref[0].astype(F32)
            for k in range(1, n):
                acc = acc + r_ref[k].astype(F32)
            total = acc if total is None else total + acc
        refs[-1][...] = total

    if R // tr > 64 and len(rest) == 1 and rest[0] % 256 == 0:
        grid = (rest[0] // 256,)
        in_spec = pl.BlockSpec((n, R, 256), lambda i: (0, 0, i))
        out_spec = pl.BlockSpec((None, R, 256), lambda i: (l, 0, i))
    else:
        grid = (R // tr,)
        in_spec = pl.BlockSpec((n, tr) + rest, lambda i: (0, i) + z)
        out_spec = pl.BlockSpec((None, tr) + rest, lambda i: (l, i) + z)
    in_specs, args, aliases = [in_spec] * m, list(slabs), {}
    if buf is not None:
        in_specs.append(pl.BlockSpec(memory_space=pl.ANY))
        args.append(buf)
        aliases = {m: 0}
    return pl.pallas_call(
        body, name=name, grid=grid, in_specs=in_specs, out_specs=out_spec,
        out_shape=jax.ShapeDtypeStruct((DEPTH, R) + rest, F32), input_output_aliases=aliases,
        compiler_params=pltpu.CompilerParams(dimension_semantics=("parallel",), vmem_limit_bytes=VMEM_LIMIT),
    )(*args)


def _adam_math(w, g, m, v):
    mn = ADAM_B1 * m + (1.0 - ADAM_B1) * g
    vn = ADAM_B2 * v + (1.0 - ADAM_B2) * (g * g)
    m_hat = mn / (1.0 - ADAM_B1 ** ADAM_STEP)
    v_hat = vn / (1.0 - ADAM_B2 ** ADAM_STEP)
    return -ADAM_LR * (m_hat / (jnp.sqrt(v_hat) + ADAM_EPS) + ADAM_WD * w), mn, vn


def _adamw(w, g, m, v, name):
    L, R, C = w.shape
    tr = _row_tile(R)

    def body(w_ref, g_ref, m_ref, v_ref, d_ref, mo_ref, vo_ref):
        d_ref[...], mo_ref[...], vo_ref[...] = _adam_math(w_ref[...], g_ref[...], m_ref[...], v_ref[...])

    if R // tr > 64 and C % 128 == 0:
        spec, grid = pl.BlockSpec((None, R, 128), lambda l, i: (l, 0, i)), (L, C // 128)
    else:
        spec, grid = pl.BlockSpec((None, tr, C), lambda l, i: (l, i, 0)), (L, R // tr)
    return pl.pallas_call(
        body, name=name, grid=grid, in_specs=[spec] * 4, out_specs=[spec] * 3,
        out_shape=[jax.ShapeDtypeStruct((L, R, C), F32)] * 3,
        compiler_params=pltpu.CompilerParams(dimension_semantics=("parallel", "parallel"), vmem_limit_bytes=VMEM_LIMIT),
    )(w, g, m, v)


_VMEM_WHOLE = pl.BlockSpec(memory_space=pltpu.VMEM)


def _matrix_update(gath, w, m, v, name):
    K = w.shape[1]

    def body(g0_ref, g1_ref, w_ref, m_ref, v_ref, go_ref, d_ref, mo_ref, vo_ref):
        for l, gr in enumerate((g0_ref, g1_ref)):
            for k in range(K):
                g = gr[0, k].astype(F32)
                for dev in range(1, 8):
                    g = g + gr[dev, k].astype(F32)
                go_ref[l, k] = g
                d_ref[l, k], mo_ref[l, k], vo_ref[l, k] = _adam_math(w_ref[l, k], g, m_ref[l, k], v_ref[l, k])

    return pl.pallas_call(
        body, name=name, in_specs=[_VMEM_WHOLE] * 5, out_specs=[_VMEM_WHOLE] * 4,
        out_shape=[jax.ShapeDtypeStruct(w.shape, F32)] * 4,
        compiler_params=pltpu.CompilerParams(vmem_limit_bytes=VMEM_LIMIT),
    )(gath[0], gath[1], w, m, v)


VECS = (("pre_norm_g", D), ("post_norm_g", D), ("gm_ln_g", GM_W), ("gm_ln_b", GM_W), ("mla_q_norm_g", QR),
        ("mla_kv_norm_g", KVR), ("lru_conv_b", LRU_W), ("lru_b_a", LRU_W), ("lru_b_x", LRU_W), ("lru_lambda", LRU_W))
VEC_KEY = {"pre_norm_g": "pre_g", "post_norm_g": "post_g", "gm_ln_g": "ln_g", "gm_ln_b": "ln_b", "mla_q_norm_g": "qg",
           "mla_kv_norm_g": "kvg", "lru_conv_b": "conv_b", "lru_b_a": "ba", "lru_b_x": "bx", "lru_lambda": "lam"}
VEC_ROWS, VEC_W, VEC_ROW0, LOSS_ROW = 16, LRU_W, GM_G, 14


def _pack_rows(LG, loss_part):
    per = len(VECS) + 1
    ins = []
    for G in LG:
        ins += [G[VEC_KEY[n]] for n, _ in VECS] + [G["bst"]]
    ins.append(loss_part)

    def body(*refs):
        o_ref = refs[-1]
        o_ref[...] = jnp.zeros_like(o_ref)
        for l in range(DEPTH):
            base = VEC_ROWS * l
            o_ref[pl.ds(base, 8), pl.ds(0, GM_B)] = refs[per * l + len(VECS)][...].T[:8, :]
            for t, (_, width) in enumerate(VECS):
                o_ref[pl.ds(base + VEC_ROW0 + t, 1), pl.ds(0, width)] = refs[per * l + t][...]
        o_ref[pl.ds(LOSS_ROW, 1), pl.ds(0, 128)] = jnp.broadcast_to(refs[-2][...], (1, 128))

    return pl.pallas_call(
        body, name="pack_rows", in_specs=[_VMEM_WHOLE] * len(ins), out_specs=_VMEM_WHOLE,
        out_shape=jax.ShapeDtypeStruct((DEPTH * VEC_ROWS, VEC_W), F32),
    )(*ins)


def _vector_update(gath, W, M, V):
    names = [n for n, _ in VECS] + ["gm_bs"]
    nw = len(names)

    def body(*refs):
        g_ref = refs[0]
        wr, mr, vr = refs[1:1 + nw], refs[1 + nw:1 + 2 * nw], refs[1 + 2 * nw:1 + 3 * nw]
        outs = refs[1 + 3 * nw:]
        s = g_ref[0]
        for dev in range(1, 8):
            s = s + g_ref[dev]
        for t, (_, width) in enumerate(VECS):
            for l in range(DEPTH):
                r = VEC_ROWS * l + VEC_ROW0 + t
                g = s[r:r + 1, :width]
                row = (pl.ds(l, 1), slice(None))
                res = (g,) + _adam_math(wr[t][row], g, mr[t][row], vr[t][row])
                for q in range(4):
                    outs[4 * t + q][row] = res[q]
        t = len(VECS)
        for l in range(DEPTH):
            for k in range(GM_G):
                g = s[VEC_ROWS * l + k:VEC_ROWS * l + k + 1, :GM_B]
                row = (l, pl.ds(k, 1), slice(None))
                res = (g,) + _adam_math(wr[t][row], g, mr[t][row], vr[t][row])
                for q in range(4):
                    outs[4 * t + q][row] = res[q]
        outs[4 * nw][...] = s[LOSS_ROW:LOSS_ROW + 1, :128]

    ws = [W[n] for n in names]
    out_shape = []
    for w in ws:
        out_shape += [jax.ShapeDtypeStruct(w.shape, F32)] * 4
    out_shape.append(jax.ShapeDtypeStruct((1, 128), F32))
    res = pl.pallas_call(
        body, name="vector_update", in_specs=[_VMEM_WHOLE] * (1 + 3 * nw), out_specs=[_VMEM_WHOLE] * (4 * nw + 1),
        out_shape=out_shape, compiler_params=pltpu.CompilerParams(vmem_limit_bytes=VMEM_LIMIT),
    )(gath, *ws, *[M[n] for n in names], *[V[n] for n in names])
    return {n: tuple(res[4 * t:4 * t + 4]) for t, n in enumerate(names)}, res[4 * nw]


SHARDED = ("w_in", "mla_w_uq", "mla_w_ukv", "lru_conv_w", "w_proj_a", "w_proj_b", "w_proj_c", "w_out")
COL_SHARDED = ("w_in", "mla_w_uq", "mla_w_ukv", "lru_conv_w")
SMALL = ("pre_norm_g", "gm_ln_g", "gm_ln_b", "gm_ws", "gm_bs", "mla_q_norm_g", "mla_kv_norm_g", "lru_conv_b",
         "lru_w_a", "lru_b_a", "lru_w_x", "lru_b_x", "lru_lambda", "post_norm_g")
WEIGHTS = ("pre_norm_g", "w_in", "gm_ln_g", "gm_ln_b", "gm_ws", "gm_bs", "mla_q_norm_g", "mla_w_uq",
           "mla_kv_norm_g", "mla_w_ukv", "lru_conv_w", "lru_conv_b", "lru_w_a", "lru_b_a", "lru_w_x", "lru_b_x",
           "lru_lambda", "w_proj_a", "w_proj_b", "w_proj_c", "w_out", "post_norm_g")


GB_KEY = {"w_in": "wp", "mla_w_uq": "wuq", "mla_w_ukv": "wukv", "w_proj_a": "wpa", "w_proj_b": "wpb",
          "w_proj_c": "wpc", "w_out": "wout"}


def _prepare(l, gathered, small, wsb):
    P = {GB_KEY[n]: gathered[n] for n in GB_KEY}
    P["conv_w"] = gathered["lru_conv_w"].transpose(1, 0, 2).reshape(CONV_W, LRU_W)
    P["wsb"] = wsb
    row = lambda n: small[n][l][None, :]
    P["pre_g"], P["post_g"] = row("pre_norm_g"), row("post_norm_g")
    P["ln_g"], P["ln_b"] = row("gm_ln_g"), row("gm_ln_b")
    P["ws"] = small["gm_ws"][l]
    P["bst"] = jnp.pad(small["gm_bs"][l].T, ((0, 0), (0, 128 - GM_G)))
    P["qg"], P["kvg"] = row("mla_q_norm_g"), row("mla_kv_norm_g")
    P["conv_b"], P["ba"], P["bx"], P["lam"] = row("lru_conv_b"), row("lru_b_a"), row("lru_b_x"), row("lru_lambda")
    return P


def kernel(x, pre_norm_g, w_in, gm_ln_g, gm_ln_b, gm_ws, gm_bs, mla_q_norm_g, mla_w_uq, mla_kv_norm_g, mla_w_ukv, lru_conv_w, lru_conv_b, lru_w_a, lru_b_a, lru_w_x, lru_b_x, lru_lambda, w_proj_a, w_proj_b, w_proj_c, w_out, post_norm_g, loss_target, m_pre_norm_g, m_w_in, m_gm_ln_g, m_gm_ln_b, m_gm_ws, m_gm_bs, m_mla_q_norm_g, m_mla_w_uq, m_mla_kv_norm_g, m_mla_w_ukv, m_lru_conv_w, m_lru_conv_b, m_lru_w_a, m_lru_b_a, m_lru_w_x, m_lru_b_x, m_lru_lambda, m_w_proj_a, m_w_proj_b, m_w_proj_c, m_w_out, m_post_norm_g, v_pre_norm_g, v_w_in, v_gm_ln_g, v_gm_ln_b, v_gm_ws, v_gm_bs, v_mla_q_norm_g, v_mla_w_uq, v_mla_kv_norm_g, v_mla_w_ukv, v_lru_conv_w, v_lru_conv_b, v_lru_w_a, v_lru_b_a, v_lru_w_x, v_lru_b_x, v_lru_lambda, v_w_proj_a, v_w_proj_b, v_w_proj_c, v_w_out, v_post_norm_g):
    args = dict(locals())
    W = {n: args[n] for n in WEIGHTS}
    M = {n: args["m_" + n] for n in WEIGHTS}
    V = {n: args["v_" + n] for n in WEIGHTS}
    c = lax.axis_index("c")

    def shards(l):
        out = []
        for n in SHARDED:
            blk = W[n][l].T if n in TRANSPOSED else W[n][l]
            out.append(blk[None] if n == "lru_conv_w" else blk.astype(BF16))
        return out

    small = {n: W[n] for n in SMALL}
    wsb = _superblocks(W["lru_w_a"], W["lru_w_x"])
    tabs = _rope_tables()
    g0 = dict(zip(SHARDED, _weights_allgather(SHARDED, shards(0), "weights_allgather_l0")))
    sems, srcs1, lands1, token = _gather_start(SHARDED, shards(1), "weights_gather_start_l1")

    P = [_prepare(0, g0, small, wsb), None]
    h0 = x[0]
    h1, A0 = _layer_fwd(h0, P[0], 0, tabs, token)
    g1 = dict(zip(SHARDED, _gather_wait(SHARDED, sems, srcs1, lands1, h1, "weights_gather_wait_l1")))
    P[1] = _prepare(1, g1, small, wsb)
    h2, A1 = _layer_fwd(h1, P[1], 1, tabs)
    dy, loss_part = _loss_fwd(h2, loss_target[0])
    def large_grads(G, GB):
        conv = G["conv_w"].reshape(CONV_W, N_CHIPS, LRU_W // N_CHIPS).transpose(1, 0, 2)
        return [conv if n == "lru_conv_w" else GB[GB_KEY[n]] for n in SHARDED]

    d1, G1, GB1 = _layer_bwd(dy, A1, P[1], 1, tabs)
    sems, srcs1, lands1, token = _scatter_start(SHARDED, large_grads(G1, GB1), "grads_scatter_start_l1")
    d0, G0, GB0 = _layer_bwd(d1, A0, P[0], 0, tabs, token)
    LG = (G0, G1)
    mine1 = _scatter_wait(SHARDED, sems, srcs1, lands1, d0, "grads_scatter_wait_l1")
    theirs1 = _sibling_swap(mine1, "partials_to_sibling_l1")
    both = [_sum_slabs([a, b], 1, None, "sum_partials_l1_" + n) for n, a, b in zip(SHARDED, mine1, theirs1)]
    g0l = large_grads(G0, GB0)
    c_arr = jnp.reshape(c, (1,)).astype(jnp.int32)
    from_sib = _half_to_sibling(SHARDED, g0l, "grads_half_to_sibling_l0")
    pair = [_pair_add_half(g, rb, c_arr, "pair_add_" + n) for n, g, rb in zip(SHARDED, g0l, from_sib)]
    slabs = _chip_scatter_half(SHARDED, pair, "grads_chip_scatter_l0")
    both = [_sum_slabs([s], 0, b, "sum_slabs_l0_" + n) for n, s, b in zip(SHARDED, slabs, both)]
    both = _subset_exchange(SHARDED, both, 0, "reduced_rows_to_sibling_l0")
    grads = {}
    for n, b in zip(SHARDED, both):
        if n in TRANSPOSED and n != "w_in":
            b = jnp.swapaxes(b, 1, 2)
        grads[n] = b if n == "w_in" else b.reshape(W[n].shape)

    rows = _pack_rows(LG, loss_part)
    mats = []
    for g in LG:
        mats += [g["ws"].astype(BF16), g["wab"][0, :, :, :LRU_BW], g["wab"][1, :, :, :LRU_BW]]
    gath = _allgather([rows] + mats, "small_grads_allgather")
    upd, loss_row = _vector_update(gath[0], W, M, V)
    loss = loss_row[0, 0]
    for k, n in enumerate(("gm_ws", "lru_w_a", "lru_w_x")):
        upd[n] = _matrix_update((gath[1 + k], gath[4 + k]), W[n], M[n], V[n], "update_" + n)

    for n in SHARDED:
        if n == "w_in":
            tr = lambda a: jnp.swapaxes(a, 1, 2)
            res = _adamw(tr(W[n]), grads[n], tr(M[n]), tr(V[n]), "adamw_" + n)
            upd[n] = tuple(tr(a) for a in (grads[n],) + tuple(res))
        else:
            upd[n] = (grads[n],) + tuple(_adamw(W[n], grads[n], M[n], V[n], "adamw_" + n))

    return (loss, d0[None], *[upd[n][0] for n in WEIGHTS], *[upd[n][1] for n in WEIGHTS],
            *[upd[n][2] for n in WEIGHTS], *[upd[n][3] for n in WEIGHTS])
```

```python
import functools
import math

import jax
import jax.numpy as jnp
from jax import lax
from jax.experimental import pallas as pl
from jax.experimental.pallas import tpu as pltpu

F32, BF16 = jnp.float32, jnp.bfloat16
MESH = pl.DeviceIdType.MESH

S, D, DEPTH = 2048, 1024, 2
CHUNK, EPS = 64, 1e-6
GM_W, GM_G, GM_B = 1024, 4, 128
H, NOPE, ROPE, VDIM = 8, 128, 64, 128
QR, KVR = 384, 256
MLA_W = H * VDIM
LRU_W, LRU_NB, LRU_BW, LRU_C, CONV_W = 1280, 16, 80, 8.0, 4
ROPE_THETA = 10000.0
IN_SIZES = (GM_W, GM_W, GM_W, QR, KVR, ROPE, MLA_W, LRU_W, LRU_W, D, D, D)
N_IN = sum(IN_SIZES)
N_CHIPS = 4
ADAM_LR, ADAM_B1, ADAM_B2, ADAM_EPS, ADAM_WD, ADAM_STEP = 0.001, 0.9, 0.999, 1e-08, 0.01, 10

HP = 256
O_U, O_V, O_ZA, O_GA, O_GB, O_GC = 0, 1024, 2048, 3072, 4096, 5120
O_CKV, O_KR, O_CQ, O_XC, O_ZC, O_ZB = 6144, 6400, 6528, 7680, 8960, 10240
NP = 11264
VMEM_LIMIT = 48 * 1024 * 1024


def _tile(dim, target):
    if dim <= target:
        return dim
    t = (target // 128) * 128
    while dim % t:
        t -= 128
    return t


def _sig(x):
    return jax.nn.sigmoid(x)


def _silu(x):
    return x * _sig(x)


def _dsilu(x):
    s = _sig(x)
    return s * (1.0 + x * (1.0 - s))


def _mm(a, b, mode, name, out_dtype=F32, tm=512, tn=512, tk=1024, b_lead=None, out_lead=None):
    b2 = b.shape[1:] if b_lead is not None else b.shape
    if mode == "nn":
        (M, K), (K2, N) = a.shape, b2
    elif mode == "nt":
        (M, K), (N, K2) = a.shape, b2
    else:
        (K, M), (K2, N) = a.shape, b2
    assert K == K2, (name, a.shape, b.shape)
    tm, tn, tk = _tile(M, tm), _tile(N, tn), _tile(K, tk)
    nk = K // tk
    if mode == "tn":
        a_spec = pl.BlockSpec((tk, tm), lambda i, j, k: (k, i))
        lhs_c = 0
    else:
        a_spec = pl.BlockSpec((tm, tk), lambda i, j, k: (i, k))
        lhs_c = 1
    b_blk, b_idx, rhs_c = ((tn, tk), (lambda i, j, k: (j, k)), 1) if mode == "nt" else ((tk, tn), (lambda i, j, k: (k, j)), 0)
    if b_lead is None:
        b_spec = pl.BlockSpec(b_blk, b_idx)
    else:
        b_spec = pl.BlockSpec((None,) + b_blk, functools.partial(lambda i, j, k, f, l: (l,) + f(i, j, k), f=b_idx, l=b_lead))
    dims = (((lhs_c,), (rhs_c,)), ((), ()))
    in_specs, args, aliases = [a_spec, b_spec], [a, b], {}
    if out_lead is None:
        out_spec = pl.BlockSpec((tm, tn), lambda i, j, k: (i, j))
        out_shape = jax.ShapeDtypeStruct((M, N), out_dtype)
    else:
        l_out, n_lead, buf = out_lead
        out_spec = pl.BlockSpec((None, tm, tn), functools.partial(lambda i, j, k, l: (l, i, j), l=l_out))
        out_shape = jax.ShapeDtypeStruct((n_lead, M, N), out_dtype)
        if buf is not None:
            in_specs.append(pl.BlockSpec(memory_space=pl.ANY))
            args.append(buf)
            aliases = {2: 0}

    def body(a_ref, b_ref, *rest):
        o_ref, acc_ref = rest[-2:]
        k = pl.program_id(2)

        @pl.when(k == 0)
        def _():
            acc_ref[...] = jnp.zeros_like(acc_ref)

        acc_ref[...] += lax.dot_general(a_ref[...].astype(BF16), b_ref[...].astype(BF16), dims,
                                        preferred_element_type=F32)

        @pl.when(k == nk - 1)
        def _():
            o_ref[...] = acc_ref[...].astype(o_ref.dtype)

    return pl.pallas_call(
        body, name=name, grid=(M // tm, N // tn, nk),
        in_specs=in_specs, out_specs=out_spec, out_shape=out_shape,
        scratch_shapes=[pltpu.VMEM((tm, tn), F32)], input_output_aliases=aliases,
        compiler_params=pltpu.CompilerParams(dimension_semantics=("parallel", "parallel", "arbitrary"),
                                             vmem_limit_bytes=VMEM_LIMIT),
    )(*args)


def _rows(fn, name, tm, rows, halos=(), fulls=(), outs=(), accs=()):
    n = S // tm
    in_specs, args = [], []
    for arr, w, cb in rows:
        in_specs.append(pl.BlockSpec((tm, w), functools.partial(lambda i, cb: (i, cb), cb=cb)))
        args.append(arr)
    for arr, w, cb, side in halos:
        if side == "prev":
            im = functools.partial(lambda i, cb: (jnp.maximum(i * (tm // 16) - 1, 0), cb), cb=cb)
        else:
            im = functools.partial(lambda i, cb: (jnp.minimum((i + 1) * (tm // 16), S // 16 - 1), cb), cb=cb)
        in_specs.append(pl.BlockSpec((16, w), im))
        args.append(arr)
    for arr in fulls:
        in_specs.append(pl.BlockSpec(arr.shape, functools.partial(lambda i, nd: (0,) * nd, nd=arr.ndim)))
        args.append(arr)
    out_shape, out_specs, aliases, n_alias = [], [], {}, 0
    for k, o in enumerate(outs):
        if len(o) == 3 and o[2] == "T":
            out_shape.append(jax.ShapeDtypeStruct((o[0], S), o[1]))
            out_specs.append(pl.BlockSpec((o[0], tm), lambda i: (0, i)))
        elif len(o) == 3:
            buf, total, cb = o[2]
            out_shape.append(jax.ShapeDtypeStruct((S, total), o[1]))
            out_specs.append(pl.BlockSpec((tm, o[0]), functools.partial(lambda i, cb: (i, cb), cb=cb)))
            if buf is not None:
                aliases[len(args)] = k
                in_specs.append(pl.BlockSpec(memory_space=pl.ANY))
                args.append(buf)
                n_alias += 1
        else:
            out_shape.append(jax.ShapeDtypeStruct((S, o[0]), o[1]))
            out_specs.append(pl.BlockSpec((tm, o[0]), lambda i: (i, 0)))
    for shp in accs:
        out_shape.append(jax.ShapeDtypeStruct(shp, F32))
        out_specs.append(pl.BlockSpec(shp, functools.partial(lambda i, nd: (0,) * nd, nd=len(shp))))
    nr, nh, nf, no, na = len(rows), len(halos), len(fulls), len(outs), len(accs)

    def body(*refs):
        i = pl.program_id(0)
        ins, orefs = refs[:nr + nh + nf], refs[nr + nh + nf + n_alias:]
        rv = [r[...].astype(F32) for r in ins[:nr]]
        hv = [r[...].astype(F32)[8:] if h[3] == "prev" else r[...].astype(F32)[:8] for r, h in zip(ins[nr:nr + nh], halos)]
        fv = [r[...] for r in ins[nr + nh:]]
        o, a = fn(i, rv, hv, fv)
        assert len(o) == no and len(a) == na, name
        for spec, ref, val in zip(outs, orefs[:no], o):
            ref[...] = (val.T if len(spec) == 3 and spec[2] == "T" else val).astype(ref.dtype)
        if na:
            @pl.when(i == 0)
            def _():
                for ref in orefs[no:]:
                    ref[...] = jnp.zeros_like(ref)

            for ref, val in zip(orefs[no:], a):
                ref[...] += val

    res = pl.pallas_call(
        body, name=name, grid=(n,), in_specs=in_specs, out_specs=out_specs, out_shape=out_shape,
        input_output_aliases=aliases,
        compiler_params=pltpu.CompilerParams(dimension_semantics=("arbitrary",), vmem_limit_bytes=VMEM_LIMIT),
    )(*args)
    return res


def _shift_down(xb, halo, s, row):
    fix = jnp.tile(pltpu.roll(halo, s, 0), (xb.shape[0] // 8, 1))
    return jnp.where(row >= s, pltpu.roll(xb, s, 0), fix)


def _shift_up(xb, halo, s, row):
    tm = xb.shape[0]
    fix = jnp.tile(pltpu.roll(halo, 8 - s, 0), (tm // 8, 1))
    return jnp.where(row < tm - s, pltpu.roll(xb, tm - s, 0), fix)


def _rms(x):
    return lax.rsqrt(jnp.mean(x * x, axis=-1, keepdims=True) + EPS)


def _rms_bwd(dy, x, g):
    r = _rms(x)
    xh = x * r
    dxh = dy * g
    dx = r * (dxh - xh * jnp.mean(dxh * xh, axis=-1, keepdims=True))
    return dx, dy * xh


def _colsum(x):
    return jnp.sum(x, axis=0, keepdims=True)


def _prenorm_fwd(x, g, token=None):
    def fn(i, rv, hv, fv):
        return [rv[0] * _rms(rv[0]) * fv[0]], []
    return _rows(fn, "prenorm_fwd", 256, [(x, D, 0)], fulls=[g] + ([] if token is None else [token]), outs=[(D, BF16)])[0]


def _gm_mask():
    r = lax.broadcasted_iota(jnp.int32, (GM_B, GM_B), 0) // CHUNK
    c = lax.broadcasted_iota(jnp.int32, (GM_B, GM_B), 1) // CHUNK
    return c <= r


def _gm_norm(v, g, b):
    mu = jnp.mean(v, axis=-1, keepdims=True)
    vc = v - mu
    rs = lax.rsqrt(jnp.mean(vc * vc, axis=-1, keepdims=True) + EPS)
    vh = vc * rs
    return vh, rs, vh * g + b


def _gm_sv(vn, ws, bst):
    mask = _gm_mask()
    gw = GM_W // GM_G
    parts = []
    for g in range(GM_G):
        wm = jnp.where(mask, ws[g], 0.0).astype(BF16)
        parts.append(jnp.dot(wm, vn[:, g * gw:(g + 1) * gw].astype(BF16), preferred_element_type=F32)
                     + bst[:, g:g + 1])
    return jnp.concatenate(parts, axis=1)


def _gmlp_fwd(proj, ln_g, ln_b, ws, bst):
    def fn(i, rv, hv, fv):
        u, v, z = rv
        g, b, w, bt = fv
        _, _, vn = _gm_norm(v, g, b)
        return [u * _gm_sv(vn, w, bt) * _silu(z)], []
    return _rows(fn, "gmlp_fwd", GM_B, [(proj, GM_W, 0), (proj, GM_W, 1), (proj, GM_W, 2)],
                 fulls=[ln_g, ln_b, ws, bst], outs=[(GM_W, BF16)])[0]


def _mla_prep_fwd(proj, qg, kvg):
    def fn(i, rv, hv, fv):
        cq, ckv = rv
        g1, g2 = fv
        return [cq * _rms(cq) * g1, ckv * _rms(ckv) * g2], []
    return _rows(fn, "mla_prep_fwd", 256, [(proj, QR, O_CQ // QR), (proj, KVR, O_CKV // KVR)],
                 fulls=[qg, kvg], outs=[(QR, BF16), (KVR, BF16)])


def _rot(t, cc, sa, sb):
    return t * cc + pltpu.roll(t, 32, 1) * sa + pltpu.roll(t, 96, 1) * sb


def _rot_t(g, cc, sa, sb):
    return g * cc + pltpu.roll(g * sa, 96, 1) + pltpu.roll(g * sb, 32, 1)


def _rope_tables():
    pos = jnp.arange(S, dtype=F32)
    inv_freq = ROPE_THETA ** (-jnp.arange(0, ROPE, 2, dtype=F32) / ROPE)
    ang = pos[:, None] * inv_freq[None, :]
    cos, sin, z = jnp.cos(ang), jnp.sin(ang), jnp.zeros((S, 32), F32)
    cc = jnp.concatenate([cos, cos, z, z], axis=1)
    sa = jnp.concatenate([z, sin, z, z], axis=1)
    sb = jnp.concatenate([-sin, z, z, z], axis=1)
    return cc, sa, sb


ATT_SCALE = 1.0 / math.sqrt(NOPE + ROPE)


def _rope_fwd(q, kv, proj, tabs):
    def fn(i, rv, hv, fv):
        qb, kvb, kr, cc, sa, sb = rv
        krr = _rot(kr, cc, sa, sb)
        qs, ks = [], []
        for h in range(H):
            qs += [qb[:, h * HP:h * HP + 128] * ATT_SCALE, _rot(qb[:, h * HP + 128:(h + 1) * HP], cc, sa, sb) * ATT_SCALE]
            ks += [kvb[:, h * 128:(h + 1) * 128], krr]
        kc = jnp.concatenate(ks, axis=1)
        vv = kvb[:, H * NOPE:]
        return [jnp.concatenate(qs, axis=1), kc, kc, vv, vv], []
    cc, sa, sb = tabs
    return _rows(fn, "rope_fwd", 256,
                 [(q, H * HP, 0), (kv, H * 256, 0), (proj, 128, O_KR // 128), (cc, 128, 0), (sa, 128, 0), (sb, 128, 0)],
                 outs=[(H * HP, BF16), (H * HP, BF16), (H * HP, BF16, "T"), (MLA_W, BF16), (MLA_W, BF16, "T")])


TQ, TC, ATT_NB = 256, 128, 4
_NT = (((1,), (1,)), ((), ()))


def _attn_allowed(i, kc):
    kpos = kc * TC + lax.broadcasted_iota(jnp.int32, (TC, TQ), 0)
    qpos = i * TQ + lax.broadcasted_iota(jnp.int32, (TC, TQ), 1)
    return (kpos // CHUNK) <= (qpos // CHUNK)


def _attn_fwd(qc, kc, vt):
    def body(q_ref, k_ref, vt_ref, o_ref, l_ref):
        i = pl.program_id(1)
        q = q_ref[...]

        def scores(sb):
            t0s = [pl.multiple_of((sb * ATT_NB + c) * TC, TC) for c in range(ATT_NB)]
            return [lax.dot_general(k_ref[pl.ds(t0, TC), :], q, _NT, preferred_element_type=F32) for t0 in t0s]

        def block(sb, ss, carry, masked):
            m, l, acc = carry
            t0s = [pl.multiple_of((sb * ATT_NB + c) * TC, TC) for c in range(ATT_NB)]
            if masked:
                ss = [jnp.where(_attn_allowed(i, sb * ATT_NB + c), s, -1e30) for c, s in enumerate(ss)]
            m_new = m
            for s in ss:
                m_new = jnp.maximum(m_new, jnp.max(s, axis=0, keepdims=True))
            alpha = jnp.exp(m - m_new)
            ps = [jnp.exp(s - m_new) for s in ss]
            l = alpha * l
            acc = alpha * acc
            for t0, p in zip(t0s, ps):
                l = l + jnp.sum(p, axis=0, keepdims=True)
                acc = acc + jnp.dot(vt_ref[:, pl.ds(t0, TC)], p.astype(BF16), preferred_element_type=F32)
            return m_new, l, acc

        nsb = (i + 2) // 2
        c = (jnp.full((1, TQ), -1e30, F32), jnp.zeros((1, TQ), F32), jnp.zeros((VDIM, TQ), F32))

        def step(sb, sc):
            nxt = scores(sb + 1)
            return nxt, block(sb, sc[0], sc[1], False)

        ss, c = lax.fori_loop(0, nsb - 1, step, (scores(0), c))
        m, l, acc = block(nsb - 1, ss, c, True)
        o_ref[...] = (acc / l).T
        l_ref[...] = m + jnp.log(l)

    return pl.pallas_call(
        body, name="attn_fwd", grid=(H, S // TQ),
        in_specs=[pl.BlockSpec((TQ, HP), lambda h, i: (i, h)),
                  pl.BlockSpec((S, HP), lambda h, i: (0, h)),
                  pl.BlockSpec((VDIM, S), lambda h, i: (h, 0))],
        out_specs=[pl.BlockSpec((TQ, VDIM), lambda h, i: (i, h)), pl.BlockSpec((None, 1, TQ), lambda h, i: (h, 0, i))],
        out_shape=[jax.ShapeDtypeStruct((S, MLA_W), F32), jax.ShapeDtypeStruct((H, 1, S), F32)],
        compiler_params=pltpu.CompilerParams(dimension_semantics=("parallel", "arbitrary"),
                                             vmem_limit_bytes=VMEM_LIMIT),
    )(qc, kc, vt)


def _gate_mul_fwd(name, val, proj, width, cb):
    def fn(i, rv, hv, fv):
        o, z = rv
        return [o * _silu(z)], []
    return _rows(fn, name, 256, [(val, width, 0), (proj, width, cb)], outs=[(width, BF16)])[0]


def _conv_fwd(proj, w, b):
    def fn(i, rv, hv, fv):
        (xb,), (halo,), (ww, bb) = rv, hv, fv
        halo = jnp.where(i > 0, halo, 0.0)
        row = lax.broadcasted_iota(jnp.int32, xb.shape, 0)
        acc = bb + ww[3:4] * xb
        for s in range(1, CONV_W):
            acc = acc + ww[3 - s:4 - s] * _shift_down(xb, halo, s, row)
        return [acc, acc], []
    return _rows(fn, "conv_fwd", 128, [(proj, LRU_W, O_XC // LRU_W)], halos=[(proj, LRU_W, O_XC // LRU_W, "prev")],
                 fulls=[w, b], outs=[(LRU_W, F32), (LRU_W, BF16)])


def _lru_terms(ga, gx, xc, ba, bx, lam):
    r = _sig(ga + ba)
    ig = _sig(gx + bx)
    sp = jnp.maximum(-lam, 0.0) + jnp.log(1.0 + jnp.exp(-jnp.abs(lam)))
    log_a = -LRU_C * r * sp
    a = jnp.exp(log_a)
    e2 = jnp.exp(2.0 * log_a)
    om = 1.0 - e2
    mult = jnp.sqrt(jnp.maximum(om, 0.0))
    return r, ig, sp, a, e2, om, mult


def _lru_gates_fwd(gates, xc, ba, bx, lam):
    def fn(i, rv, hv, fv):
        ga, gx, x = rv
        r, ig, sp, a, e2, om, mult = _lru_terms(ga, gx, x, *fv)
        return [a, mult * (ig * x)], []
    return _rows(fn, "lru_gates_fwd", 128, [(gates, LRU_W, 0), (gates, LRU_W, 1), (xc, LRU_W, 0)],
                 fulls=[ba, bx, lam], outs=[(LRU_W, F32), (LRU_W, F32)])


SCAN_T, SCAN_CW = 64, 256


def _scan_fwd(a, b):
    def body(a_ref, b_ref, h_ref):
        row = lax.broadcasted_iota(jnp.int32, (SCAN_T, SCAN_CW), 0)

        def step(blk, hc):
            t0 = pl.multiple_of(blk * SCAN_T, SCAN_T)
            A = a_ref[pl.ds(t0, SCAN_T), :]
            B = b_ref[pl.ds(t0, SCAN_T), :]
            d = 1
            while d < SCAN_T:
                keep = row >= d
                A_s = jnp.where(keep, pltpu.roll(A, d, 0), 1.0)
                B_s = jnp.where(keep, pltpu.roll(B, d, 0), 0.0)
                B = A * B_s + B
                A = A * A_s
                d *= 2
            hh = A * hc + B
            h_ref[pl.ds(t0, SCAN_T), :] = hh
            return hh[SCAN_T - 1:SCAN_T, :]

        lax.fori_loop(0, S // SCAN_T, step, jnp.zeros((1, SCAN_CW), F32))

    spec = pl.BlockSpec((S, SCAN_CW), lambda j: (0, j))
    return pl.pallas_call(
        body, name="scan_fwd", grid=(LRU_W // SCAN_CW,), in_specs=[spec, spec], out_specs=spec,
        out_shape=jax.ShapeDtypeStruct((S, LRU_W), F32),
        compiler_params=pltpu.CompilerParams(dimension_semantics=("parallel",), vmem_limit_bytes=VMEM_LIMIT),
    )(a, b)


def _merge_fwd(pa, pb, pc, proj):
    def fn(i, rv, hv, fv):
        a, b, c, ga, gb, gc = rv
        return [_sig(ga) * a + _sig(gb) * b + _sig(gc) * c], []
    return _rows(fn, "merge_fwd", 256,
                 [(pa, D, 0), (pb, D, 0), (pc, D, 0), (proj, D, O_GA // D), (proj, D, O_GB // D), (proj, D, O_GC // D)],
                 outs=[(D, BF16)])[0]


def _post_fwd(x, o2, g):
    def fn(i, rv, hv, fv):
        xb, ob = rv
        return [xb + ob * _rms(ob) * fv[0]], []
    return _rows(fn, "post_fwd", 256, [(x, D, 0), (o2, D, 0)], fulls=[g], outs=[(D, F32)])[0]


SB = 640
BD_TM = 512


def _bd_fwd(xcb, wsb, l):
    def body(x_ref, w_ref, o_ref):
        o_ref[...] = jnp.dot(x_ref[...], w_ref[...], preferred_element_type=F32)

    return pl.pallas_call(
        body, name="lru_gate_mm", grid=(S // BD_TM, 4),
        in_specs=[pl.BlockSpec((BD_TM, SB), lambda i, q: (i, q % 2)),
                  pl.BlockSpec((None, None, SB, SB), lambda i, q: (l, q, 0, 0))],
        out_specs=pl.BlockSpec((BD_TM, SB), lambda i, q: (i, q)),
        out_shape=jax.ShapeDtypeStruct((S, 2 * LRU_W), F32),
        compiler_params=pltpu.CompilerParams(dimension_semantics=("parallel", "parallel"), vmem_limit_bytes=VMEM_LIMIT),
    )(xcb, wsb)


def _bd_dx(dgates, wsb, l):
    def body(d_ref, w_ref, o_ref, acc_ref):
        g = pl.program_id(2)

        @pl.when(g == 0)
        def _():
            acc_ref[...] = jnp.zeros_like(acc_ref)

        acc_ref[...] += lax.dot_general(d_ref[...], w_ref[...], (((1,), (1,)), ((), ())), preferred_element_type=F32)

        @pl.when(g == 1)
        def _():
            o_ref[...] = acc_ref[...]

    return pl.pallas_call(
        body, name="lru_gate_dx", grid=(S // BD_TM, 2, 2),
        in_specs=[pl.BlockSpec((BD_TM, SB), lambda i, s, g: (i, 2 * g + s)),
                  pl.BlockSpec((None, None, SB, SB), lambda i, s, g: (l, 2 * g + s, 0, 0))],
        out_specs=pl.BlockSpec((BD_TM, SB), lambda i, s, g: (i, s)),
        out_shape=jax.ShapeDtypeStruct((S, LRU_W), F32),
        scratch_shapes=[pltpu.VMEM((BD_TM, SB), F32)],
        compiler_params=pltpu.CompilerParams(dimension_semantics=("parallel", "parallel", "arbitrary"),
                                             vmem_limit_bytes=VMEM_LIMIT),
    )(dgates, wsb)


def _bd_dw(xcb, dgates):
    tk = 1024

    def body(x_ref, d_ref, o_ref):
        @pl.when(pl.program_id(1) == 0)
        def _():
            o_ref[...] = jnp.zeros_like(o_ref)

        o_ref[...] += lax.dot_general(x_ref[...], d_ref[...], (((0,), (0,)), ((), ())), preferred_element_type=F32)

    return pl.pallas_call(
        body, name="lru_gate_dw", grid=(4, S // tk),
        in_specs=[pl.BlockSpec((tk, SB), lambda q, k: (k, q % 2)), pl.BlockSpec((tk, SB), lambda q, k: (k, q))],
        out_specs=pl.BlockSpec((None, SB, SB), lambda q, k: (q, 0, 0)),
        out_shape=jax.ShapeDtypeStruct((4, SB, SB), F32),
        compiler_params=pltpu.CompilerParams(dimension_semantics=("parallel", "arbitrary"), vmem_limit_bytes=VMEM_LIMIT),
    )(xcb, dgates)


def _bd_extract(dwsb):
    def body(w_ref, o_ref):
        lane = lax.broadcasted_iota(jnp.int32, (LRU_BW, 128), 1)
        for q in range(4):
            for kk in range(8):
                c0 = LRU_BW * kk
                w0, off = (c0 // 128) * 128, c0 % 128
                rows = pl.ds(LRU_BW * kk, LRU_BW)
                blk = w_ref[q, rows, w0:w0 + 128]
                if off:
                    blk = pltpu.roll(blk, 128 - off, 1)
                    if off + LRU_BW > 128:
                        nxt = pltpu.roll(w_ref[q, rows, w0 + 128:w0 + 256], 128 - off, 1)
                        blk = jnp.where(lane < 128 - off, blk, nxt)
                o_ref[q // 2, 8 * (q % 2) + kk] = blk.astype(BF16)

    return pl.pallas_call(
        body, name="lru_gate_dw_blocks",
        in_specs=[pl.BlockSpec(memory_space=pltpu.VMEM)], out_specs=pl.BlockSpec(memory_space=pltpu.VMEM),
        out_shape=jax.ShapeDtypeStruct((2, LRU_NB, LRU_BW, 128), BF16),
        compiler_params=pltpu.CompilerParams(vmem_limit_bytes=VMEM_LIMIT),
    )(dwsb)


def _layer_fwd(x, P, l, tabs, token=None):
    A = {"x": x}
    A["h"] = _prenorm_fwd(x, P["pre_g"], token)
    proj = A["proj"] = _mm(A["h"], P["wp"], "nt", "in_proj", out_dtype=BF16, tm=1024)
    A["ya"] = _gmlp_fwd(proj, P["ln_g"], P["ln_b"], P["ws"], P["bst"])
    A["cqn"], A["ckvn"] = _mla_prep_fwd(proj, P["qg"], P["kvg"])
    q = _mm(A["cqn"], P["wuq"], "nt", "q_up")
    kv = _mm(A["ckvn"], P["wukv"], "nt", "kv_up")
    A["qc"], A["kc"], A["kct"], A["vv"], vt = _rope_fwd(q, kv, proj, tabs)
    A["o"], A["lse"] = _attn_fwd(A["qc"], A["kc"], vt)
    A["yb"] = _gate_mul_fwd("yb_fwd", A["o"], proj, MLA_W, O_ZB // MLA_W)
    A["xc"], A["xcb"] = _conv_fwd(proj, P["conv_w"], P["conv_b"])
    A["gates"] = _bd_fwd(A["xcb"], P["wsb"], l)
    A["a"], bterm = _lru_gates_fwd(A["gates"], A["xc"], P["ba"], P["bx"], P["lam"])
    A["hs"] = _scan_fwd(A["a"], bterm)
    A["yc"] = _gate_mul_fwd("yc_fwd", A["hs"], proj, LRU_W, O_ZC // LRU_W)
    A["pa"] = _mm(A["ya"], P["wpa"], "nn", "proj_a")
    A["pb"] = _mm(A["yb"], P["wpb"], "nn", "proj_b")
    A["pc"] = _mm(A["yc"], P["wpc"], "nn", "proj_c")
    A["merged"] = _merge_fwd(A["pa"], A["pb"], A["pc"], proj)
    A["o2"] = _mm(A["merged"], P["wout"], "nn", "out_proj")
    return _post_fwd(x, A["o2"], P["post_g"]), A


def _loss_fwd(y, tgt):
    def fn(i, rv, hv, fv):
        yb, tb = rv
        e = yb - tb
        part = 0.5 * jnp.sum(jnp.mean(e * e, axis=-1, keepdims=True), axis=0, keepdims=True)
        return [e * (1.0 / D)], [part]
    return _rows(fn, "loss", 256, [(y, D, 0), (tgt, D, 0)], outs=[(D, F32)], accs=[(1, 1)])


def _post_bwd(dxn, o2, g, token=None):
    def fn(i, rv, hv, fv):
        dy, ob = rv
        dx, dg = _rms_bwd(dy, ob, fv[0])
        return [dx], [_colsum(dg)]
    return _rows(fn, "post_bwd", 256, [(dxn, D, 0), (o2, D, 0)], fulls=[g] + ([] if token is None else [token]),
                 outs=[(D, BF16)], accs=[(1, D)])


def _merge_bwd(dm, pa, pb, pc, proj, dproj):
    def fn(i, rv, hv, fv):
        d, a, b, c, ga, gb, gc = rv
        outs_p, outs_g = [], []
        for p, gg in ((a, ga), (b, gb), (c, gc)):
            s = _sig(gg)
            outs_p.append(d * s)
            outs_g.append(d * p * s * (1.0 - s))
        return outs_p + [jnp.concatenate(outs_g, axis=1)], []
    return _rows(fn, "merge_bwd", 128,
                 [(dm, D, 0), (pa, D, 0), (pb, D, 0), (pc, D, 0),
                  (proj, D, O_GA // D), (proj, D, O_GB // D), (proj, D, O_GC // D)],
                 outs=[(D, BF16)] * 3 + [(3 * D, BF16, (dproj, NP, O_GA // (3 * D)))])


def _gmlp_bwd(dya, proj, ln_g, ln_b, ws, bst, dproj):
    gw = GM_W // GM_G

    def fn(i, rv, hv, fv):
        dy, u, v, z = rv
        g, b, w, bt = fv
        vh, rs, vn = _gm_norm(v, g, b)
        sv = _gm_sv(vn, w, bt)
        sz = _silu(z)
        du = dy * sv * sz
        dsv = dy * u * sz
        dz = dy * u * sv * _dsilu(z)
        mask = _gm_mask()
        lane = lax.broadcasted_iota(jnp.int32, (GM_B, 128), 1)
        dvn_parts, dws, dbst = [], [], jnp.zeros((GM_B, 128), F32)
        for k in range(GM_G):
            wm = jnp.where(mask, w[k], 0.0).astype(BF16)
            dsk = dsv[:, k * gw:(k + 1) * gw]
            dskb = dsk.astype(BF16)
            dvn_parts.append(lax.dot_general(wm, dskb, (((0,), (0,)), ((), ())), preferred_element_type=F32))
            dwk = lax.dot_general(dskb, vn[:, k * gw:(k + 1) * gw].astype(BF16), (((1,), (1,)), ((), ())),
                                  preferred_element_type=F32)
            dws.append(jnp.where(mask, dwk, 0.0)[None])
            dbst = dbst + jnp.where(lane == k, jnp.sum(dsk, axis=1, keepdims=True), 0.0)
        dvn = jnp.concatenate(dvn_parts, axis=1)
        dvh = dvn * g
        dv = rs * (dvh - jnp.mean(dvh, axis=-1, keepdims=True) - vh * jnp.mean(dvh * vh, axis=-1, keepdims=True))
        return ([jnp.concatenate([du, dv, dz], axis=1)],
                [jnp.concatenate(dws, axis=0), dbst, _colsum(dvn * vh), _colsum(dvn)])
    return _rows(fn, "gmlp_bwd", GM_B, [(dya, GM_W, 0), (proj, GM_W, 0), (proj, GM_W, 1), (proj, GM_W, 2)],
                 fulls=[ln_g, ln_b, ws, bst], outs=[(3 * GM_W, BF16, (dproj, NP, O_U // (3 * GM_W)))],
                 accs=[(GM_G, GM_B, GM_B), (GM_B, 128), (1, GM_W), (1, GM_W)])


def _yb_bwd(dyb, o, proj, dproj):
    def fn(i, rv, hv, fv):
        dy, ob, z = rv
        do = dy * _silu(z)
        prod = do * ob
        lane = lax.broadcasted_iota(jnp.int32, (dy.shape[0], 128), 1)
        dl = jnp.zeros((dy.shape[0], 128), F32)
        for h in range(H):
            dl = dl + jnp.where(lane == h, jnp.sum(prod[:, h * VDIM:(h + 1) * VDIM], axis=1, keepdims=True), 0.0)
        return [do, dl, dy * ob * _dsilu(z)], []
    return _rows(fn, "yb_bwd", 256, [(dyb, MLA_W, 0), (o, MLA_W, 0), (proj, MLA_W, O_ZB // MLA_W)],
                 outs=[(MLA_W, BF16), (128, F32, "T"), (MLA_W, BF16, (dproj, NP, O_ZB // MLA_W))])


def _attn_bwd(qc, kc, kct, vv, do, lse, dlt):
    def body(q_ref, k_ref, kt_ref, v_ref, do_ref, l_ref, d_ref, dq_ref, dk_ref, dv_ref, dqt_ref):
        h, i = pl.program_id(0), pl.program_id(1)

        @pl.when(i == 0)
        def _():
            dk_ref[...] = jnp.zeros_like(dk_ref)
            dv_ref[...] = jnp.zeros_like(dv_ref)

        q = q_ref[...]
        dob = do_ref[...]
        lse = l_ref[...]
        dl = d_ref[pl.ds(h, 1), :]
        dqt_ref[...] = jnp.zeros_like(dqt_ref)

        def rows_of(sb, c):
            return pl.ds(pl.multiple_of((sb * ATT_NB + c) * TC, TC), TC)

        def front(sb):
            return [(lax.dot_general(k_ref[rows_of(sb, c), :], q, _NT, preferred_element_type=F32),
                     lax.dot_general(v_ref[rows_of(sb, c), :], dob, _NT, preferred_element_type=F32))
                    for c in range(ATT_NB)]

        def block(sb, sd, masked):
            dqt = None
            for c, (s, dp) in enumerate(sd):
                rows = rows_of(sb, c)
                p = jnp.exp(s - lse)
                if masked:
                    p = jnp.where(_attn_allowed(i, sb * ATT_NB + c), p, 0.0)
                ds = (p * (dp - dl)).astype(BF16)
                dk_ref[rows, :] += jnp.dot(ds, q, preferred_element_type=F32)
                dv_ref[rows, :] += jnp.dot(p.astype(BF16), dob, preferred_element_type=F32)
                part = jnp.dot(kt_ref[:, rows], ds, preferred_element_type=F32)
                dqt = part if dqt is None else dqt + part
            dqt_ref[...] += dqt

        def step(sb, sd):
            nxt = front(sb + 1)
            block(sb, sd, False)
            return nxt

        nsb = (i + 2) // 2
        sd = lax.fori_loop(0, nsb - 1, step, front(0))
        block(nsb - 1, sd, True)
        dq_ref[...] = dqt_ref[...].T

    blk = lambda w: pl.BlockSpec((TQ, w), lambda h, i: (i, h))
    head = lambda w: pl.BlockSpec((S, w), lambda h, i: (0, h))
    return pl.pallas_call(
        body, name="attn_bwd", grid=(H, S // TQ),
        in_specs=[blk(HP), head(HP), pl.BlockSpec((HP, S), lambda h, i: (h, 0)), head(VDIM), blk(VDIM),
                  pl.BlockSpec((None, 1, TQ), lambda h, i: (h, 0, i)), pl.BlockSpec((8, TQ), lambda h, i: (0, i))],
        out_specs=[blk(HP), head(HP), head(VDIM)],
        out_shape=[jax.ShapeDtypeStruct((S, H * HP), F32), jax.ShapeDtypeStruct((S, H * HP), F32),
                   jax.ShapeDtypeStruct((S, MLA_W), F32)],
        scratch_shapes=[pltpu.VMEM((HP, TQ), F32)],
        compiler_params=pltpu.CompilerParams(dimension_semantics=("parallel", "arbitrary"),
                                             vmem_limit_bytes=VMEM_LIMIT),
    )(qc, kc, kct, vv, do, lse, dlt)


def _rope_bwd(dqc, dkc, dvv, tabs):
    def fn(i, rv, hv, fv):
        dq, dk, dv, cc, sa, sb = rv
        qs, ks = [], []
        dkr = jnp.zeros((dq.shape[0], 128), F32)
        for h in range(H):
            qs += [dq[:, h * HP:h * HP + 128] * ATT_SCALE, _rot_t(dq[:, h * HP + 128:(h + 1) * HP], cc, sa, sb) * ATT_SCALE]
            ks.append(dk[:, h * HP:h * HP + 128])
            dkr = dkr + dk[:, h * HP + 128:(h + 1) * HP]
        return [jnp.concatenate(qs, axis=1), jnp.concatenate(ks + [dv], axis=1), _rot_t(dkr, cc, sa, sb)], []
    cc, sa, sb = tabs
    return _rows(fn, "rope_bwd", 256,
                 [(dqc, H * HP, 0), (dkc, H * HP, 0), (dvv, MLA_W, 0), (cc, 128, 0), (sa, 128, 0), (sb, 128, 0)],
                 outs=[(H * HP, BF16), (H * 256, BF16), (128, BF16)])


MLA_GROUP = 1536


def _mla_prep_bwd(dcqn, dckvn, dkr, proj, qg, kvg, dproj):
    def fn(i, rv, hv, fv):
        d1, d2, dk, cq, ckv = rv
        g1, g2 = fv
        dx1, dg1 = _rms_bwd(d1, cq, g1)
        dx2, dg2 = _rms_bwd(d2, ckv, g2)
        zeros = jnp.zeros((d1.shape[0], MLA_GROUP - KVR - 128 - QR), F32)
        return [jnp.concatenate([dx2, dk.astype(F32), dx1, zeros], axis=1)], [_colsum(dg1), _colsum(dg2)]
    return _rows(fn, "mla_prep_bwd", 256,
                 [(dcqn, QR, 0), (dckvn, KVR, 0), (dkr, 128, 0), (proj, QR, O_CQ // QR), (proj, KVR, O_CKV // KVR)],
                 fulls=[qg, kvg], outs=[(MLA_GROUP, BF16, (dproj, NP, O_CKV // MLA_GROUP))], accs=[(1, QR), (1, KVR)])


def _yc_bwd(dyc, hs, proj, dproj):
    def fn(i, rv, hv, fv):
        dy, hh, z = rv
        return [dy * _silu(z), dy * hh * _dsilu(z)], []
    return _rows(fn, "yc_bwd", 128, [(dyc, LRU_W, 0), (hs, LRU_W, 0), (proj, LRU_W, O_ZC // LRU_W)],
                 outs=[(LRU_W, F32), (LRU_W, BF16, (dproj, NP, O_ZC // LRU_W))])


def _scan_bwd(a, hs, dh):
    nblk = S // SCAN_T

    def body(a_ref, h_ref, dh_ref, da_ref, db_ref):
        row = lax.broadcasted_iota(jnp.int32, (SCAN_T, SCAN_CW), 0)

        def step(j, carry):
            gc, ac = carry
            blk = nblk - 1 - j
            t0 = pl.multiple_of(blk * SCAN_T, SCAN_T)
            av = a_ref[pl.ds(t0, SCAN_T), :]
            A = jnp.where(row < SCAN_T - 1, pltpu.roll(av, SCAN_T - 1, 0), ac)
            B = dh_ref[pl.ds(t0, SCAN_T), :]
            d = 1
            while d < SCAN_T:
                keep = row < SCAN_T - d
                A_s = jnp.where(keep, pltpu.roll(A, SCAN_T - d, 0), 1.0)
                B_s = jnp.where(keep, pltpu.roll(B, SCAN_T - d, 0), 0.0)
                B = A * B_s + B
                A = A * A_s
                d *= 2
            g = A * gc + B
            p0 = pl.multiple_of(jnp.maximum(t0 - 8, 0), 8)
            last = jnp.where(blk > 0, h_ref[pl.ds(p0, 8), :][7:8, :], 0.0)
            h_prev = jnp.where(row >= 1, pltpu.roll(h_ref[pl.ds(t0, SCAN_T), :], 1, 0), last)
            da_ref[pl.ds(t0, SCAN_T), :] = g * h_prev
            db_ref[pl.ds(t0, SCAN_T), :] = g
            return g[0:1, :], av[0:1, :]

        z = jnp.zeros((1, SCAN_CW), F32)
        lax.fori_loop(0, nblk, step, (z, z))

    spec = pl.BlockSpec((S, SCAN_CW), lambda j: (0, j))
    return pl.pallas_call(
        body, name="scan_bwd", grid=(LRU_W // SCAN_CW,), in_specs=[spec] * 3, out_specs=[spec] * 2,
        out_shape=[jax.ShapeDtypeStruct((S, LRU_W), F32)] * 2,
        compiler_params=pltpu.CompilerParams(dimension_semantics=("parallel",), vmem_limit_bytes=VMEM_LIMIT),
    )(a, hs, dh)


def _lru_gates_bwd(da, db, gates, xc, ba, bx, lam):
    def fn(i, rv, hv, fv):
        dav, dbv, ga, gx, x = rv
        bav, bxv, lamv = fv
        r, ig, sp, a, e2, om, mult = _lru_terms(ga, gx, x, bav, bxv, lamv)
        dmult = dbv * ig * x
        dig = dbv * mult * x
        dxc1 = dbv * mult * ig
        dlog_a = dav * a + jnp.where(om > 0.0, dmult * (-e2 / mult), 0.0)
        dr = dlog_a * (-LRU_C * sp)
        dga = dr * r * (1.0 - r)
        dgx = dig * ig * (1.0 - ig)
        dlam = _colsum(dlog_a * (-LRU_C * r)) * (-_sig(-lamv))
        return [jnp.concatenate([dga, dgx], axis=1), dxc1], [_colsum(dga), _colsum(dgx), dlam]
    return _rows(fn, "lru_gates_bwd", 128,
                 [(da, LRU_W, 0), (db, LRU_W, 0), (gates, LRU_W, 0), (gates, LRU_W, 1), (xc, LRU_W, 0)],
                 fulls=[ba, bx, lam], outs=[(2 * LRU_W, BF16), (LRU_W, F32)], accs=[(1, LRU_W)] * 3)


def _conv_bwd(dxc1, dxc2, proj, w, dproj):
    cb = O_XC // LRU_W

    def fn(i, rv, hv, fv):
        d1, d2, xb = rv
        n1, n2, xprev = hv
        ww = fv[0]
        last = i == S // 128 - 1
        dxc = d1 + d2
        nxt = jnp.where(last, 0.0, n1 + n2)
        xprev = jnp.where(i > 0, xprev, 0.0)
        row = lax.broadcasted_iota(jnp.int32, xb.shape, 0)
        dx = ww[3:4] * dxc
        dws = [None] * CONV_W
        dws[3] = _colsum(dxc * xb)
        for s in range(1, CONV_W):
            dx = dx + ww[3 - s:4 - s] * _shift_up(dxc, nxt, s, row)
            dws[3 - s] = _colsum(dxc * _shift_down(xb, xprev, s, row))
        return [dx], [jnp.concatenate(dws, axis=0), _colsum(dxc)]
    return _rows(fn, "conv_bwd", 128, [(dxc1, LRU_W, 0), (dxc2, LRU_W, 0), (proj, LRU_W, cb)],
                 halos=[(dxc1, LRU_W, 0, "next"), (dxc2, LRU_W, 0, "next"), (proj, LRU_W, cb, "prev")],
                 fulls=[w], outs=[(LRU_W, BF16, (dproj, NP, cb))], accs=[(CONV_W, LRU_W), (1, LRU_W)])


def _prenorm_bwd(dxn, dh, x, g):
    def fn(i, rv, hv, fv):
        dy, dhh, xb = rv
        dx, dg = _rms_bwd(dhh, xb, fv[0])
        return [dy + dx], [_colsum(dg)]
    return _rows(fn, "prenorm_bwd", 256, [(dxn, D, 0), (dh, D, 0), (x, D, 0)], fulls=[g], outs=[(D, F32)],
                 accs=[(1, D)])


def _layer_bwd(dxn, A, P, l, tabs, token=None):
    G, GB = {}, {}
    proj = A["proj"]

    def dw(key, a, b, name, **tiles):
        GB[key] = _mm(a, b, "tn", name, out_dtype=BF16, **tiles)

    do2, G["post_g"] = _post_bwd(dxn, A["o2"], P["post_g"], token)
    dm = _mm(do2, P["wout"], "nt", "out_proj_dx")
    dw("wout", A["merged"], do2, "out_proj_dw")
    dpa, dpb, dpc, dproj = _merge_bwd(dm, A["pa"], A["pb"], A["pc"], proj, None)
    dya = _mm(dpa, P["wpa"], "nt", "proj_a_dx")
    dw("wpa", A["ya"], dpa, "proj_a_dw")
    dyb = _mm(dpb, P["wpb"], "nt", "proj_b_dx")
    dw("wpb", A["yb"], dpb, "proj_b_dw")
    dyc = _mm(dpc, P["wpc"], "nt", "proj_c_dx")
    dw("wpc", A["yc"], dpc, "proj_c_dw")
    dproj, G["ws"], G["bst"], G["ln_g"], G["ln_b"] = _gmlp_bwd(dya, proj, P["ln_g"], P["ln_b"], P["ws"], P["bst"], dproj)
    do, dl, dproj = _yb_bwd(dyb, A["o"], proj, dproj)
    dqc, dkc, dvv = _attn_bwd(A["qc"], A["kc"], A["kct"], A["vv"], do, A["lse"], dl)
    dq, dkv, dkr = _rope_bwd(dqc, dkc, dvv, tabs)
    dcqn = _mm(dq, P["wuq"], "nn", "q_up_dx")
    dw("wuq", dq, A["cqn"], "q_up_dw")
    dckvn = _mm(dkv, P["wukv"], "nn", "kv_up_dx")
    dw("wukv", dkv, A["ckvn"], "kv_up_dw")
    dproj, G["qg"], G["kvg"] = _mla_prep_bwd(dcqn, dckvn, dkr, proj, P["qg"], P["kvg"], dproj)
    dhs, dproj = _yc_bwd(dyc, A["hs"], proj, dproj)
    da, db = _scan_bwd(A["a"], A["hs"], dhs)
    dgates, dxc1, G["ba"], G["bx"], G["lam"] = _lru_gates_bwd(da, db, A["gates"], A["xc"], P["ba"], P["bx"], P["lam"])
    dxc2 = _bd_dx(dgates, P["wsb"], l)
    G["wab"] = _bd_extract(_bd_dw(A["xcb"], dgates))
    dproj, G["conv_w"], G["conv_b"] = _conv_bwd(dxc1, dxc2, proj, P["conv_w"], dproj)
    dh = _mm(dproj, P["wp"], "nn", "in_proj_dx", tm=1024, tn=1024)
    dw("wp", dproj, A["h"], "in_proj_dw", tm=1536, tn=1024)
    dx, G["pre_g"] = _prenorm_bwd(dxn, dh, A["x"], P["pre_g"])
    return dx, G, GB


_ORIG_OFF = [0]
for _s in IN_SIZES:
    _ORIG_OFF.append(_ORIG_OFF[-1] + _s)
_PAD_OFF = {0: O_U, 1: O_V, 2: O_ZA, 3: O_CQ, 4: O_CKV, 5: O_KR, 6: O_ZB, 7: O_XC, 8: O_ZC, 9: O_GA, 10: O_GB, 11: O_GC}
SHARD_IN = N_IN // N_CHIPS


def _pieces_w_in(j):
    lo, hi = SHARD_IN * j, SHARD_IN * (j + 1)
    out = []
    for k in range(len(IN_SIZES)):
        a, b = max(lo, _ORIG_OFF[k]), min(hi, _ORIG_OFF[k + 1])
        if a < b:
            out.append((a - lo, _PAD_OFF[k] + a - _ORIG_OFF[k], b - a))
    return out


def _pieces_uq(j):
    return [(192 * hh, HP * (2 * j + hh), NOPE + ROPE) for hh in range(2)]


def _pieces_ukv(j):
    out = []
    for hh in range(2):
        h = 2 * j + hh
        out += [(256 * hh, NOPE * h, NOPE), (256 * hh + NOPE, H * NOPE + VDIM * h, VDIM)]
    return out


def _pieces_rows(r):
    return lambda j: [(0, r * j, r)]


LAYOUT = {
    "w_in": (SHARD_IN, NP, _pieces_w_in),
    "mla_w_uq": (2 * (NOPE + ROPE), H * HP, _pieces_uq),
    "mla_w_ukv": (2 * (NOPE + VDIM), 2 * H * 128, _pieces_ukv),
    "lru_conv_w": (1, N_CHIPS, _pieces_rows(1)),
    "w_proj_a": (GM_W // N_CHIPS, GM_W, _pieces_rows(GM_W // N_CHIPS)),
    "w_proj_b": (MLA_W // N_CHIPS, MLA_W, _pieces_rows(MLA_W // N_CHIPS)),
    "w_proj_c": (LRU_W // N_CHIPS, LRU_W, _pieces_rows(LRU_W // N_CHIPS)),
    "w_out": (D // N_CHIPS, D, _pieces_rows(D // N_CHIPS)),
}
TRANSPOSED = ("w_in", "mla_w_uq", "mla_w_ukv")


def _superblocks(w_a, w_x):
    w6 = jnp.stack([w_a, w_x], axis=1).reshape(DEPTH, 4, 8, LRU_BW, LRU_BW).astype(BF16)
    bands = [jnp.pad(w6[:, :, k], ((0, 0), (0, 0), (0, 0), (LRU_BW * k, SB - LRU_BW * (k + 1)))) for k in range(8)]
    return jnp.concatenate(bands, axis=2)


_HBM = pl.BlockSpec(memory_space=pltpu.HBM)


def _position():
    return lax.axis_index("x"), lax.axis_index("y"), lax.axis_index("c")


def _allgather(blocks, name):
    n = len(blocks)

    def body(*refs):
        ins, outs = refs[:n], refs[n:2 * n]
        send, recv, lsem = refs[2 * n:]
        x, y, c = _position()
        me, sib = (x, y, c), (x, y, 1 - c)
        chips = [(1 - x, y), (x, 1 - y), (1 - x, 1 - y)]

        def cp(k, a, block, to, src=None):
            dst = outs[a].at[4 * block[0] + 2 * block[1] + block[2]]
            return pltpu.make_async_remote_copy(src_ref=dst if src is None else src, dst_ref=dst,
                                                send_sem=send.at[7 * a + k], recv_sem=recv.at[7 * a + k],
                                                device_id=to, device_id_type=MESH)

        mine = [pltpu.make_async_copy(ins[a], outs[a].at[4 * x + 2 * y + c], lsem.at[a]) for a in range(n)]
        for m in mine:
            m.start()
        first = []
        for a in range(n):
            first.append(cp(0, a, me, sib, src=ins[a]))
            first += [cp(1 + j, a, me, (*chip, c), src=ins[a]) for j, chip in enumerate(chips)]
        for f in first:
            f.start()
        passed = []
        for j, chip in enumerate(chips):
            for a in range(n):
                cp(1 + j, a, (*chip, c), me).wait_recv()
                p = cp(4 + j, a, (*chip, c), sib)
                p.start()
                passed.append(p)
        for a in range(n):
            cp(0, a, sib, me).wait_recv()
            for j, chip in enumerate(chips):
                cp(4 + j, a, (*chip, 1 - c), me).wait_recv()
        for f in first + passed:
            f.wait_send()
        for m in mine:
            m.wait()

    return pl.pallas_call(
        body, name=name,
        out_shape=[jax.ShapeDtypeStruct((8,) + b.shape, b.dtype) for b in blocks],
        in_specs=[_HBM] * n, out_specs=[_HBM] * n,
        scratch_shapes=[pltpu.SemaphoreType.DMA((7 * n,)), pltpu.SemaphoreType.DMA((7 * n,)),
                        pltpu.SemaphoreType.DMA((n,))],
    )(*blocks)


_REL = (2, 1, 3)


def _cut(r):
    return r if r < 32 else (r // 2 + 15) // 16 * 16


def _half_rows(r, c0):
    return _cut(r) if c0 == 0 else r - _cut(r)


def _half_pieces(lay_a, jsrc, c0):
    r = lay_a[0]
    lo, hi = (0, _cut(r)) if c0 == 0 else (_cut(r), r)
    out = []
    for s0, d0, nr in lay_a[2](jsrc):
        a, b = max(s0, lo), min(s0 + nr, hi)
        if a < b:
            out.append((a, d0 + a - s0, b - a))
    return out


def _gather_zeros(names, srcs):
    return [jnp.zeros((LAYOUT[nm][1],) + s.shape[1:], s.dtype) for nm, s in zip(names, srcs)]


def _weights_allgather(names, srcs, name, carry=()):
    n = len(srcs)
    lay = [LAYOUT[nm] for nm in names]
    zeros = _gather_zeros(names, srcs)
    m = len(carry)

    def body(*refs):
        ins, outs = refs[:n], refs[2 * n + m:3 * n + m]
        send, recv, lsem = refs[3 * n + 2 * m:]
        x, y, c = _position()
        j = 2 * x + y
        sib = (x, y, 1 - c)
        chips = [(1 - x, y), (x, 1 - y), (1 - x, 1 - y)]

        def flow(a, k, jsrc, c0, to, from_src):
            cps = []
            for s0, d0, nr in _half_pieces(lay[a], jsrc, c0):
                dst = outs[a].at[pl.ds(d0, nr)]
                src = ins[a].at[pl.ds(s0, nr)] if from_src else dst
                cps.append(pltpu.make_async_remote_copy(src_ref=src, dst_ref=dst, send_sem=send.at[7 * a + k],
                                                        recv_sem=recv.at[7 * a + k], device_id=to, device_id_type=MESH))
            return cps

        def sized(a, k, rows):
            ref = ins[a].at[pl.ds(0, rows)]
            return pltpu.make_async_remote_copy(src_ref=ref, dst_ref=ref, send_sem=send.at[7 * a + k],
                                                recv_sem=recv.at[7 * a + k], device_id=sib, device_id_type=MESH)

        for j0 in range(N_CHIPS):
            for c0 in range(2):
                @pl.when((j == j0) & (c == c0))
                def _(j0=j0, c0=c0):
                    mine = [_half_rows(lay[a][0], c0) for a in range(n)]
                    theirs = [_half_rows(lay[a][0], 1 - c0) for a in range(n)]
                    for a in range(n):
                        for s0, d0, nr in _half_pieces(lay[a], j0, c0):
                            pltpu.make_async_copy(ins[a].at[pl.ds(s0, nr)], outs[a].at[pl.ds(d0, nr)], lsem.at[a]).start()
                    for a in range(n):
                        for cp in flow(a, 0, j0, c0, sib, True):
                            cp.start()
                        for k, chip in enumerate(chips):
                            for cp in flow(a, 1 + k, j0, c0, (*chip, c), True):
                                cp.start()
                    for k in range(3):
                        for a in range(n):
                            if mine[a]:
                                sized(a, 1 + k, mine[a]).wait_recv()
                                for cp in flow(a, 4 + k, j0 ^ _REL[k], c0, sib, False):
                                    cp.start()
                    for a in range(n):
                        if theirs[a]:
                            sized(a, 0, theirs[a]).wait_recv()
                            for k in range(3):
                                sized(a, 4 + k, theirs[a]).wait_recv()
                    for a in range(n):
                        if mine[a]:
                            for k in range(7):
                                sized(a, k, mine[a]).wait_send()
                            ref = ins[a].at[pl.ds(0, mine[a])]
                            pltpu.make_async_copy(ref, ref, lsem.at[a]).wait()

    res = pl.pallas_call(
        body, name=name,
        out_shape=[jax.ShapeDtypeStruct(z.shape, z.dtype) for z in list(zeros) + list(carry)],
        in_specs=[_HBM] * (2 * n + m), out_specs=[_HBM] * (n + m),
        input_output_aliases={n + a: a for a in range(n + m)},
        scratch_shapes=[pltpu.SemaphoreType.DMA((7 * n,)), pltpu.SemaphoreType.DMA((7 * n,)),
                        pltpu.SemaphoreType.DMA((n,))],
    )(*srcs, *zeros, *carry)
    return res[:n], res[n:]


_SEM = pl.BlockSpec(memory_space=pltpu.SEMAPHORE)
_VMEM_TOKEN = pl.BlockSpec(memory_space=pltpu.VMEM)
_TOKEN = jax.ShapeDtypeStruct((8, 128), F32)
_EFFECT = pltpu.SideEffectType.DATAFLOW_SIDE_EFFECTING


def _gather_start(names, srcs, zeros, name):
    n = len(srcs)
    lay = [LAYOUT[nm] for nm in names]

    def body(*refs):
        ins, lands = refs[:n], refs[n:2 * n]
        send, recv, lsem = refs[2 * n:2 * n + 3]
        refs[-1][...] = jnp.zeros_like(refs[-1])
        x, y, c = _position()
        j = 2 * x + y
        chips = [(1 - x, y), (x, 1 - y), (1 - x, 1 - y)]
        for j0 in range(N_CHIPS):
            @pl.when(j == j0)
            def _(j0=j0):
                for a in range(n):
                    for s0, d0, nr in lay[a][2](j0):
                        src, dst = ins[a].at[pl.ds(s0, nr)], lands[a].at[pl.ds(d0, nr)]
                        pltpu.make_async_copy(src, dst, lsem.at[a]).start()
                        for k, chip in enumerate(chips):
                            pltpu.make_async_remote_copy(src_ref=src, dst_ref=dst, send_sem=send.at[3 * a + k],
                                                         recv_sem=recv.at[3 * a + k], device_id=(*chip, c),
                                                         device_id_type=MESH).start()

    sems = [pltpu.SemaphoreType.DMA((3 * n,)), pltpu.SemaphoreType.DMA((3 * n,)), pltpu.SemaphoreType.DMA((n,))]
    hbm = lambda a: pltpu.HBM(a.shape, a.dtype)
    res = pl.pallas_call(
        body, name=name,
        out_shape=sems + [hbm(s) for s in srcs] + [hbm(z) for z in zeros] + [_TOKEN],
        in_specs=[_HBM] * (2 * n), out_specs=[_SEM] * 3 + [_HBM] * (2 * n) + [_VMEM_TOKEN],
        input_output_aliases={a: 3 + a for a in range(2 * n)},
        compiler_params=pltpu.CompilerParams(has_side_effects=_EFFECT),
    )(*[pltpu.with_memory_space_constraint(s, pltpu.HBM) for s in srcs],
      *[pltpu.with_memory_space_constraint(z, pltpu.HBM) for z in zeros])
    return res[:3], res[3:3 + n], res[3 + n:3 + 2 * n], res[-1]


def _gather_wait(names, sems, srcs, lands, after, name):
    n = len(srcs)
    lay = [LAYOUT[nm] for nm in names]

    def body(*refs):
        ins, zones = refs[:n], refs[n:2 * n]
        send, recv, lsem = refs[2 * n:2 * n + 3]
        x, y, c = _position()
        for a in range(n):
            whole = zones[a].at[pl.ds(0, lay[a][0])]
            for k in range(3):
                cp = pltpu.make_async_remote_copy(src_ref=ins[a], dst_ref=whole, send_sem=send.at[3 * a + k],
                                                  recv_sem=recv.at[3 * a + k], device_id=(x, y, 1 - c),
                                                  device_id_type=MESH)
                cp.wait_send()
                cp.wait_recv()
            pltpu.make_async_copy(ins[a], whole, lsem.at[a]).wait()

    hbm = lambda a: pltpu.HBM(a.shape, a.dtype)
    res = pl.pallas_call(
        body, name=name,
        out_shape=[hbm(s) for s in srcs] + [hbm(z) for z in lands],
        in_specs=[_HBM] * (2 * n) + [_SEM] * 3 + [pl.BlockSpec(memory_space=pl.ANY)], out_specs=[_HBM] * (2 * n),
        input_output_aliases={a: a for a in range(2 * n)},
        compiler_params=pltpu.CompilerParams(has_side_effects=_EFFECT),
    )(*srcs, *lands, *sems, after)
    return res[n:]


def _clip_pieces(lay_a, jsrc, c0):
    h = lay_a[1] // 2
    lo, hi = c0 * h, (c0 + 1) * h
    out = []
    for s0, d0, nr in lay_a[2](jsrc):
        a, b = max(d0, lo), min(d0 + nr, hi)
        if a < b:
            out.append((s0 + a - d0, a, b - a))
    return out


def _rows_of(pieces):
    return sum(nr for _, _, nr in pieces)


def _both_cores(body_for):
    x, y, c = _position()
    j = 2 * x + y
    for j0 in range(N_CHIPS):
        for c0 in range(2):
            @pl.when((j == j0) & (c == c0))
            def _(j0=j0, c0=c0):
                body_for(j0, c0)


STAGE_ROWS = 512


def _staged_copy(src, dst, buf, sem_in, sem_out, rows):
    ch = buf.shape[0]
    for r in range(0, rows, ch):
        nr = min(ch, rows - r)
        stage = buf.at[pl.ds(0, nr)]
        cin = pltpu.make_async_copy(src.at[pl.ds(r, nr)], stage, sem_in)
        cin.start()
        cin.wait()
        cout = pltpu.make_async_copy(stage, dst.at[pl.ds(r, nr)], sem_out)
        cout.start()
        cout.wait()


def _half_to_sibling(names, gl, name):
    n = len(gl)
    halves = [LAYOUT[nm][1] // 2 for nm in names]

    def body(*refs):
        ins, outs = refs[:n], refs[n:2 * n]
        send, recv = refs[2 * n:]
        x, y, c = _position()

        def run(j0, c0):
            cps = [pltpu.make_async_remote_copy(src_ref=ins[a].at[pl.ds((1 - c0) * halves[a], halves[a])], dst_ref=outs[a],
                                                send_sem=send.at[a], recv_sem=recv.at[a], device_id=(x, y, 1 - c),
                                                device_id_type=MESH) for a in range(n)]
            for cp in cps:
                cp.start()
            for cp in cps:
                cp.wait()

        _both_cores(run)

    return pl.pallas_call(
        body, name=name,
        out_shape=[jax.ShapeDtypeStruct((halves[a],) + gl[a].shape[1:], gl[a].dtype) for a in range(n)],
        in_specs=[_HBM] * n, out_specs=[_HBM] * n,
        scratch_shapes=[pltpu.SemaphoreType.DMA((n,)), pltpu.SemaphoreType.DMA((n,))],
    )(*gl)


def _chip_scatter_half(names, parts, name):
    n = len(parts)
    lay = [LAYOUT[nm] for nm in names]
    zeros = [jnp.zeros((N_CHIPS, lay[a][0]) + parts[a].shape[1:], parts[a].dtype) for a in range(n)]

    def body(*refs):
        ins, outs = refs[:n], refs[2 * n:3 * n]
        send, recv = refs[3 * n:3 * n + 2]
        stage, sem_in, sem_out = refs[3 * n + 2:4 * n + 2], refs[4 * n + 2], refs[4 * n + 3]
        x, y, c = _position()
        chips = [(1 - x, y), (x, 1 - y), (1 - x, 1 - y)]

        def run(j0, c0):
            def sized(a, rows):
                return outs[a].at[0, pl.ds(0, rows)]

            for a in range(n):
                base = c0 * (lay[a][1] // 2)
                for k, chip in enumerate(chips):
                    for s0, d0, nr in _clip_pieces(lay[a], j0 ^ _REL[k], c0):
                        pltpu.make_async_remote_copy(
                            src_ref=ins[a].at[pl.ds(d0 - base, nr)], dst_ref=outs[a].at[j0, pl.ds(s0, nr)],
                            send_sem=send.at[3 * a + k], recv_sem=recv.at[3 * a + k],
                            device_id=(*chip, c), device_id_type=MESH).start()
            for a in range(n):
                base = c0 * (lay[a][1] // 2)
                for s0, d0, nr in _clip_pieces(lay[a], j0, c0):
                    _staged_copy(ins[a].at[pl.ds(d0 - base, nr)], outs[a].at[j0, pl.ds(s0, nr)], stage[a],
                                 sem_in.at[a], sem_out.at[a], nr)
            for a in range(n):
                got = _rows_of(_clip_pieces(lay[a], j0, c0))
                for k in range(3):
                    sent = _rows_of(_clip_pieces(lay[a], j0 ^ _REL[k], c0))
                    if sent:
                        pltpu.make_async_remote_copy(src_ref=sized(a, sent), dst_ref=sized(a, sent),
                                                     send_sem=send.at[3 * a + k], recv_sem=recv.at[3 * a + k],
                                                     device_id=(x, y, c), device_id_type=MESH).wait_send()
                    if got:
                        pltpu.make_async_remote_copy(src_ref=sized(a, got), dst_ref=sized(a, got),
                                                     send_sem=send.at[3 * a + k], recv_sem=recv.at[3 * a + k],
                                                     device_id=(x, y, c), device_id_type=MESH).wait_recv()

        _both_cores(run)

    return pl.pallas_call(
        body, name=name,
        out_shape=[jax.ShapeDtypeStruct(z.shape, z.dtype) for z in zeros],
        in_specs=[_HBM] * (2 * n), out_specs=[_HBM] * n, input_output_aliases={n + a: a for a in range(n)},
        scratch_shapes=[pltpu.SemaphoreType.DMA((3 * n,)), pltpu.SemaphoreType.DMA((3 * n,))]
        + [pltpu.VMEM((min(STAGE_ROWS, p.shape[0]),) + p.shape[1:], p.dtype) for p in parts]
        + [pltpu.SemaphoreType.DMA((n,)), pltpu.SemaphoreType.DMA((n,))],
    )(*parts, *zeros)


def _subset_exchange(names, bufs, l, name):
    n = len(bufs)
    lay = [LAYOUT[nm] for nm in names]

    def body(*refs):
        outs = refs[n:2 * n]
        send, recv = refs[2 * n:]
        x, y, c = _position()

        def run(j0, c0):
            for a in range(n):
                for s0, _, nr in _clip_pieces(lay[a], j0, c0):
                    rows = outs[a].at[l, pl.ds(s0, nr)]
                    pltpu.make_async_remote_copy(src_ref=rows, dst_ref=rows, send_sem=send.at[a], recv_sem=recv.at[a],
                                                 device_id=(x, y, 1 - c), device_id_type=MESH).start()
            for a in range(n):
                for c_half, wait_send in ((c0, True), (1 - c0, False)):
                    rows = _rows_of(_clip_pieces(lay[a], j0, c_half))
                    if rows:
                        ref = outs[a].at[l, pl.ds(0, rows)]
                        cp = pltpu.make_async_remote_copy(src_ref=ref, dst_ref=ref, send_sem=send.at[a], recv_sem=recv.at[a],
                                                          device_id=(x, y, 1 - c), device_id_type=MESH)
                        if wait_send:
                            cp.wait_send()
                        else:
                            cp.wait_recv()

        _both_cores(run)

    return pl.pallas_call(
        body, name=name,
        out_shape=[jax.ShapeDtypeStruct(b.shape, b.dtype) for b in bufs],
        in_specs=[_HBM] * n, out_specs=[_HBM] * n, input_output_aliases={a: a for a in range(n)},
        scratch_shapes=[pltpu.SemaphoreType.DMA((n,)), pltpu.SemaphoreType.DMA((n,))],
    )(*bufs)


def _scatter_start(names, gl, name):
    n = len(gl)
    lay = [LAYOUT[nm] for nm in names]
    zones = [lax.empty((N_CHIPS, lay[a][0]) + gl[a].shape[1:], gl[a].dtype) for a in range(n)]

    def body(*refs):
        ins, lands = refs[:n], refs[n:2 * n]
        send, recv, lsem = refs[2 * n:2 * n + 3]
        refs[-1][...] = jnp.zeros_like(refs[-1])
        x, y, c = _position()
        j = 2 * x + y
        chips = [(1 - x, y), (x, 1 - y), (1 - x, 1 - y)]
        for j0 in range(N_CHIPS):
            @pl.when(j == j0)
            def _(j0=j0):
                for a in range(n):
                    for s0, d0, nr in lay[a][2](j0):
                        pltpu.make_async_copy(ins[a].at[pl.ds(d0, nr)], lands[a].at[j0, pl.ds(s0, nr)], lsem.at[a]).start()
                    for k, chip in enumerate(chips):
                        for s0, d0, nr in lay[a][2](j0 ^ _REL[k]):
                            pltpu.make_async_remote_copy(
                                src_ref=ins[a].at[pl.ds(d0, nr)], dst_ref=lands[a].at[j0, pl.ds(s0, nr)],
                                send_sem=send.at[3 * a + k], recv_sem=recv.at[3 * a + k],
                                device_id=(*chip, c), device_id_type=MESH).start()

    sems = [pltpu.SemaphoreType.DMA((3 * n,)), pltpu.SemaphoreType.DMA((3 * n,)), pltpu.SemaphoreType.DMA((n,))]
    hbm = lambda a: pltpu.HBM(a.shape, a.dtype)
    res = pl.pallas_call(
        body, name=name,
        out_shape=sems + [hbm(g) for g in gl] + [hbm(z) for z in zones] + [_TOKEN],
        in_specs=[_HBM] * (2 * n), out_specs=[_SEM] * 3 + [_HBM] * (2 * n) + [_VMEM_TOKEN],
        input_output_aliases={a: 3 + a for a in range(2 * n)},
        compiler_params=pltpu.CompilerParams(has_side_effects=_EFFECT),
    )(*[pltpu.with_memory_space_constraint(g, pltpu.HBM) for g in gl],
      *[pltpu.with_memory_space_constraint(z, pltpu.HBM) for z in zones])
    return res[:3], res[3:3 + n], res[3 + n:3 + 2 * n], res[-1]


def _scatter_wait(names, sems, srcs, lands, after, name):
    n = len(srcs)
    lay = [LAYOUT[nm] for nm in names]

    def body(*refs):
        zones = refs[n:2 * n]
        send, recv, lsem = refs[2 * n:2 * n + 3]
        x, y, c = _position()
        for a in range(n):
            whole = zones[a].at[0, pl.ds(0, lay[a][0])]
            for k in range(3):
                cp = pltpu.make_async_remote_copy(src_ref=whole, dst_ref=whole, send_sem=send.at[3 * a + k],
                                                  recv_sem=recv.at[3 * a + k], device_id=(x, y, 1 - c),
                                                  device_id_type=MESH)
                cp.wait_send()
                cp.wait_recv()
            pltpu.make_async_copy(whole, whole, lsem.at[a]).wait()

    hbm = lambda a: pltpu.HBM(a.shape, a.dtype)
    res = pl.pallas_call(
        body, name=name,
        out_shape=[hbm(s) for s in srcs] + [hbm(z) for z in lands],
        in_specs=[_HBM] * (2 * n) + [_SEM] * 3 + [pl.BlockSpec(memory_space=pl.ANY)], out_specs=[_HBM] * (2 * n),
        input_output_aliases={a: a for a in range(2 * n)},
        compiler_params=pltpu.CompilerParams(has_side_effects=_EFFECT),
    )(*srcs, *lands, *sems, after)
    return res[n:]


def _sibling_swap(arrs, name):
    n = len(arrs)

    def body(*refs):
        ins, outs = refs[:n], refs[n:2 * n]
        send, recv = refs[2 * n:]
        x, y, c = _position()
        cps = [pltpu.make_async_remote_copy(src_ref=ins[a], dst_ref=outs[a], send_sem=send.at[a], recv_sem=recv.at[a],
                                            device_id=(x, y, 1 - c), device_id_type=MESH) for a in range(n)]
        for cp in cps:
            cp.start()
        for cp in cps:
            cp.wait()

    return pl.pallas_call(
        body, name=name,
        out_shape=[jax.ShapeDtypeStruct(a.shape, a.dtype) for a in arrs],
        in_specs=[_HBM] * n, out_specs=[_HBM] * n,
        scratch_shapes=[pltpu.SemaphoreType.DMA((n,)), pltpu.SemaphoreType.DMA((n,))],
    )(*arrs)


def _row_tile(r):
    for t in (256, 128, 64, 32, 16, 8):
        if r % t == 0 and r > t:
            return t
    return r


def _pair_add_half(g, rb, c_arr, name):
    hrows, rest = rb.shape[0], rb.shape[1:]
    tr = _row_tile(hrows)
    nb = hrows // tr
    z = (0,) * len(rest)

    def body(c_ref, g_ref, r_ref, o_ref):
        o_ref[...] = (g_ref[...].astype(F32) + r_ref[...].astype(F32)).astype(o_ref.dtype)

    return pl.pallas_call(
        body, name=name,
        grid_spec=pltpu.PrefetchScalarGridSpec(
            num_scalar_prefetch=1, grid=(nb,),
            in_specs=[pl.BlockSpec((tr,) + rest, lambda i, c_ref: (c_ref[0] * nb + i,) + z),
                      pl.BlockSpec((tr,) + rest, lambda i, c_ref: (i,) + z)],
            out_specs=pl.BlockSpec((tr,) + rest, lambda i, c_ref: (i,) + z)),
        out_shape=jax.ShapeDtypeStruct((hrows,) + rest, BF16),
        compiler_params=pltpu.CompilerParams(dimension_semantics=("parallel",), vmem_limit_bytes=VMEM_LIMIT),
    )(c_arr, g, rb)


def _sum_slabs(slabs, l, buf, name):
    m = len(slabs)
    n, R, rest = slabs[0].shape[0], slabs[0].shape[1], slabs[0].shape[2:]
    tr = _row_tile(R)
    z = (0,) * len(rest)

    def body(*refs):
        total = None
        for r_ref in refs[:m]:
            acc = r_ref[0].astype(F32)
            for k in range(1, n):
                acc = acc + r_ref[k].astype(F32)
            total = acc if total is None else total + acc
        refs[-1][...] = total

    if R // tr > 64 and len(rest) == 1 and rest[0] % 256 == 0:
        grid = (rest[0] // 256,)
        in_spec = pl.BlockSpec((n, R, 256), lambda i: (0, 0, i))
        out_spec = pl.BlockSpec((None, R, 256), lambda i: (l, 0, i))
    else:
        grid = (R // tr,)
        in_spec = pl.BlockSpec((n, tr) + rest, lambda i: (0, i) + z)
        out_spec = pl.BlockSpec((None, tr) + rest, lambda i: (l, i) + z)
    in_specs, args, aliases = [in_spec] * m, list(slabs), {}
    if buf is not None:
        in_specs.append(pl.BlockSpec(memory_space=pl.ANY))
        args.append(buf)
        aliases = {m: 0}
    return pl.pallas_call(
        body, name=name, grid=grid, in_specs=in_specs, out_specs=out_spec,
        out_shape=jax.ShapeDtypeStruct((DEPTH, R) + rest, F32), input_output_aliases=aliases,
        compiler_params=pltpu.CompilerParams(dimension_semantics=("parallel",), vmem_limit_bytes=VMEM_LIMIT),
    )(*args)


def _adam_math(w, g, m, v):
    mn = ADAM_B1 * m + (1.0 - ADAM_B1) * g
    vn = ADAM_B2 * v + (1.0 - ADAM_B2) * (g * g)
    m_hat = mn / (1.0 - ADAM_B1 ** ADAM_STEP)
    v_hat = vn / (1.0 - ADAM_B2 ** ADAM_STEP)
    return -ADAM_LR * (m_hat / (jnp.sqrt(v_hat) + ADAM_EPS) + ADAM_WD * w), mn, vn


def _adamw(w, g, m, v, name):
    L, R, C = w.shape
    tr = _row_tile(R)

    def body(w_ref, g_ref, m_ref, v_ref, d_ref, mo_ref, vo_ref):
        d_ref[...], mo_ref[...], vo_ref[...] = _adam_math(w_ref[...], g_ref[...], m_ref[...], v_ref[...])

    if R // tr > 64 and C % 128 == 0:
        spec, grid = pl.BlockSpec((None, R, 128), lambda l, i: (l, 0, i)), (L, C // 128)
    else:
        spec, grid = pl.BlockSpec((None, tr, C), lambda l, i: (l, i, 0)), (L, R // tr)
    return pl.pallas_call(
        body, name=name, grid=grid, in_specs=[spec] * 4, out_specs=[spec] * 3,
        out_shape=[jax.ShapeDtypeStruct((L, R, C), F32)] * 3,
        compiler_params=pltpu.CompilerParams(dimension_semantics=("parallel", "parallel"), vmem_limit_bytes=VMEM_LIMIT),
    )(w, g, m, v)


_VMEM_WHOLE = pl.BlockSpec(memory_space=pltpu.VMEM)


def _matrix_update(gath, w, m, v, name):
    K = w.shape[1]

    def body(g0_ref, g1_ref, w_ref, m_ref, v_ref, go_ref, d_ref, mo_ref, vo_ref):
        for l, gr in enumerate((g0_ref, g1_ref)):
            for k in range(K):
                g = gr[0, k].astype(F32)
                for dev in range(1, 8):
                    g = g + gr[dev, k].astype(F32)
                go_ref[l, k] = g
                d_ref[l, k], mo_ref[l, k], vo_ref[l, k] = _adam_math(w_ref[l, k], g, m_ref[l, k], v_ref[l, k])

    return pl.pallas_call(
        body, name=name, in_specs=[_VMEM_WHOLE] * 5, out_specs=[_VMEM_WHOLE] * 4,
        out_shape=[jax.ShapeDtypeStruct(w.shape, F32)] * 4,
        compiler_params=pltpu.CompilerParams(vmem_limit_bytes=VMEM_LIMIT),
    )(gath[0], gath[1], w, m, v)


VECS = (("pre_norm_g", D), ("post_norm_g", D), ("gm_ln_g", GM_W), ("gm_ln_b", GM_W), ("mla_q_norm_g", QR),
        ("mla_kv_norm_g", KVR), ("lru_conv_b", LRU_W), ("lru_b_a", LRU_W), ("lru_b_x", LRU_W), ("lru_lambda", LRU_W))
VEC_KEY = {"pre_norm_g": "pre_g", "post_norm_g": "post_g", "gm_ln_g": "ln_g", "gm_ln_b": "ln_b", "mla_q_norm_g": "qg",
           "mla_kv_norm_g": "kvg", "lru_conv_b": "conv_b", "lru_b_a": "ba", "lru_b_x": "bx", "lru_lambda": "lam"}
VEC_ROWS, VEC_W, VEC_ROW0, LOSS_ROW = 16, LRU_W, GM_G, 14


def _pack_rows(LG, loss_part):
    per = len(VECS) + 1
    ins = []
    for G in LG:
        ins += [G[VEC_KEY[n]] for n, _ in VECS] + [G["bst"]]
    ins.append(loss_part)

    def body(*refs):
        o_ref = refs[-1]
        o_ref[...] = jnp.zeros_like(o_ref)
        for l in range(DEPTH):
            base = VEC_ROWS * l
            o_ref[pl.ds(base, 8), pl.ds(0, GM_B)] = refs[per * l + len(VECS)][...].T[:8, :]
            for t, (_, width) in enumerate(VECS):
                o_ref[pl.ds(base + VEC_ROW0 + t, 1), pl.ds(0, width)] = refs[per * l + t][...]
        o_ref[pl.ds(LOSS_ROW, 1), pl.ds(0, 128)] = jnp.broadcast_to(refs[-2][...], (1, 128))

    return pl.pallas_call(
        body, name="pack_rows", in_specs=[_VMEM_WHOLE] * len(ins), out_specs=_VMEM_WHOLE,
        out_shape=jax.ShapeDtypeStruct((DEPTH * VEC_ROWS, VEC_W), F32),
    )(*ins)


def _vector_update(gath, W, M, V):
    names = [n for n, _ in VECS] + ["gm_bs"]
    nw = len(names)

    def body(*refs):
        g_ref = refs[0]
        wr, mr, vr = refs[1:1 + nw], refs[1 + nw:1 + 2 * nw], refs[1 + 2 * nw:1 + 3 * nw]
        outs = refs[1 + 3 * nw:]
        s = g_ref[0]
        for dev in range(1, 8):
            s = s + g_ref[dev]
        for t, (_, width) in enumerate(VECS):
            for l in range(DEPTH):
                r = VEC_ROWS * l + VEC_ROW0 + t
                g = s[r:r + 1, :width]
                row = (pl.ds(l, 1), slice(None))
                res = (g,) + _adam_math(wr[t][row], g, mr[t][row], vr[t][row])
                for q in range(4):
                    outs[4 * t + q][row] = res[q]
        t = len(VECS)
        for l in range(DEPTH):
            for k in range(GM_G):
                g = s[VEC_ROWS * l + k:VEC_ROWS * l + k + 1, :GM_B]
                row = (l, pl.ds(k, 1), slice(None))
                res = (g,) + _adam_math(wr[t][row], g, mr[t][row], vr[t][row])
                for q in range(4):
                    outs[4 * t + q][row] = res[q]
        outs[4 * nw][...] = s[LOSS_ROW:LOSS_ROW + 1, :128]

    ws = [W[n] for n in names]
    out_shape = []
    for w in ws:
        out_shape += [jax.ShapeDtypeStruct(w.shape, F32)] * 4
    out_shape.append(jax.ShapeDtypeStruct((1, 128), F32))
    res = pl.pallas_call(
        body, name="vector_update", in_specs=[_VMEM_WHOLE] * (1 + 3 * nw), out_specs=[_VMEM_WHOLE] * (4 * nw + 1),
        out_shape=out_shape, compiler_params=pltpu.CompilerParams(vmem_limit_bytes=VMEM_LIMIT),
    )(gath, *ws, *[M[n] for n in names], *[V[n] for n in names])
    return {n: tuple(res[4 * t:4 * t + 4]) for t, n in enumerate(names)}, res[4 * nw]


SHARDED = ("w_in", "mla_w_uq", "mla_w_ukv", "lru_conv_w", "w_proj_a", "w_proj_b", "w_proj_c", "w_out")
COL_SHARDED = ("w_in", "mla_w_uq", "mla_w_ukv", "lru_conv_w")
SMALL = ("pre_norm_g", "gm_ln_g", "gm_ln_b", "gm_ws", "gm_bs", "mla_q_norm_g", "mla_kv_norm_g", "lru_conv_b",
         "lru_w_a", "lru_b_a", "lru_w_x", "lru_b_x", "lru_lambda", "post_norm_g")
WEIGHTS = ("pre_norm_g", "w_in", "gm_ln_g", "gm_ln_b", "gm_ws", "gm_bs", "mla_q_norm_g", "mla_w_uq",
           "mla_kv_norm_g", "mla_w_ukv", "lru_conv_w", "lru_conv_b", "lru_w_a", "lru_b_a", "lru_w_x", "lru_b_x",
           "lru_lambda", "w_proj_a", "w_proj_b", "w_proj_c", "w_out", "post_norm_g")


GB_KEY = {"w_in": "wp", "mla_w_uq": "wuq", "mla_w_ukv": "wukv", "w_proj_a": "wpa", "w_proj_b": "wpb",
          "w_proj_c": "wpc", "w_out": "wout"}


def _prepare(l, gathered, small, wsb):
    P = {GB_KEY[n]: gathered[n] for n in GB_KEY}
    P["conv_w"] = gathered["lru_conv_w"].transpose(1, 0, 2).reshape(CONV_W, LRU_W)
    P["wsb"] = wsb
    row = lambda n: small[n][l][None, :]
    P["pre_g"], P["post_g"] = row("pre_norm_g"), row("post_norm_g")
    P["ln_g"], P["ln_b"] = row("gm_ln_g"), row("gm_ln_b")
    P["ws"] = small["gm_ws"][l]
    P["bst"] = jnp.pad(small["gm_bs"][l].T, ((0, 0), (0, 128 - GM_G)))
    P["qg"], P["kvg"] = row("mla_q_norm_g"), row("mla_kv_norm_g")
    P["conv_b"], P["ba"], P["bx"], P["lam"] = row("lru_conv_b"), row("lru_b_a"), row("lru_b_x"), row("lru_lambda")
    return P


def kernel(x, pre_norm_g, w_in, gm_ln_g, gm_ln_b, gm_ws, gm_bs, mla_q_norm_g, mla_w_uq, mla_kv_norm_g, mla_w_ukv, lru_conv_w, lru_conv_b, lru_w_a, lru_b_a, lru_w_x, lru_b_x, lru_lambda, w_proj_a, w_proj_b, w_proj_c, w_out, post_norm_g, loss_target, m_pre_norm_g, m_w_in, m_gm_ln_g, m_gm_ln_b, m_gm_ws, m_gm_bs, m_mla_q_norm_g, m_mla_w_uq, m_mla_kv_norm_g, m_mla_w_ukv, m_lru_conv_w, m_lru_conv_b, m_lru_w_a, m_lru_b_a, m_lru_w_x, m_lru_b_x, m_lru_lambda, m_w_proj_a, m_w_proj_b, m_w_proj_c, m_w_out, m_post_norm_g, v_pre_norm_g, v_w_in, v_gm_ln_g, v_gm_ln_b, v_gm_ws, v_gm_bs, v_mla_q_norm_g, v_mla_w_uq, v_mla_kv_norm_g, v_mla_w_ukv, v_lru_conv_w, v_lru_conv_b, v_lru_w_a, v_lru_b_a, v_lru_w_x, v_lru_b_x, v_lru_lambda, v_w_proj_a, v_w_proj_b, v_w_proj_c, v_w_out, v_post_norm_g):
    args = dict(locals())
    W = {n: args[n] for n in WEIGHTS}
    M = {n: args["m_" + n] for n in WEIGHTS}
    V = {n: args["v_" + n] for n in WEIGHTS}
    c = lax.axis_index("c")

    def shards(l):
        out = []
        for n in SHARDED:
            blk = W[n][l].T if n in TRANSPOSED else W[n][l]
            out.append(blk[None] if n == "lru_conv_w" else blk.astype(BF16))
        return out

    small = {n: W[n] for n in SMALL}
    wsb = _superblocks(W["lru_w_a"], W["lru_w_x"])
    tabs = _rope_tables()
    s1 = shards(1)
    g0, zones1 = _weights_allgather(SHARDED, shards(0), "weights_allgather_l0", carry=_gather_zeros(SHARDED, s1))
    g0 = dict(zip(SHARDED, g0))
    sems, srcs1, lands1, token = _gather_start(SHARDED, s1, zones1, "weights_gather_start_l1")

    P = [_prepare(0, g0, small, wsb), None]
    h0 = x[0]
    h1, A0 = _layer_fwd(h0, P[0], 0, tabs, token)
    g1 = dict(zip(SHARDED, _gather_wait(SHARDED, sems, srcs1, lands1, h1, "weights_gather_wait_l1")))
    P[1] = _prepare(1, g1, small, wsb)
    h2, A1 = _layer_fwd(h1, P[1], 1, tabs)
    dy, loss_part = _loss_fwd(h2, loss_target[0])
    def large_grads(G, GB):
        conv = G["conv_w"].reshape(CONV_W, N_CHIPS, LRU_W // N_CHIPS).transpose(1, 0, 2)
        return [conv if n == "lru_conv_w" else GB[GB_KEY[n]] for n in SHARDED]

    d1, G1, GB1 = _layer_bwd(dy, A1, P[1], 1, tabs)
    sems, srcs1, lands1, token = _scatter_start(SHARDED, large_grads(G1, GB1), "grads_scatter_start_l1")
    d0, G0, GB0 = _layer_bwd(d1, A0, P[0], 0, tabs, token)
    LG = (G0, G1)
    mine1 = _scatter_wait(SHARDED, sems, srcs1, lands1, d0, "grads_scatter_wait_l1")
    theirs1 = _sibling_swap(mine1, "partials_to_sibling_l1")
    both = [_sum_slabs([a, b], 1, None, "sum_partials_l1_" + n) for n, a, b in zip(SHARDED, mine1, theirs1)]
    g0l = large_grads(G0, GB0)
    c_arr = jnp.reshape(c, (1,)).astype(jnp.int32)
    from_sib = _half_to_sibling(SHARDED, g0l, "grads_half_to_sibling_l0")
    pair = [_pair_add_half(g, rb, c_arr, "pair_add_" + n) for n, g, rb in zip(SHARDED, g0l, from_sib)]
    slabs = _chip_scatter_half(SHARDED, pair, "grads_chip_scatter_l0")
    both = [_sum_slabs([s], 0, b, "sum_slabs_l0_" + n) for n, s, b in zip(SHARDED, slabs, both)]
    both = _subset_exchange(SHARDED, both, 0, "reduced_rows_to_sibling_l0")
    grads = {}
    for n, b in zip(SHARDED, both):
        if n in TRANSPOSED and n != "w_in":
            b = jnp.swapaxes(b, 1, 2)
        grads[n] = b if n == "w_in" else b.reshape(W[n].shape)

    rows = _pack_rows(LG, loss_part)
    mats = []
    for g in LG:
        mats += [g["ws"].astype(BF16), g["wab"][0, :, :, :LRU_BW], g["wab"][1, :, :, :LRU_BW]]
    gath = _allgather([rows] + mats, "small_grads_allgather")
    upd, loss_row = _vector_update(gath[0], W, M, V)
    loss = loss_row[0, 0]
    for k, n in enumerate(("gm_ws", "lru_w_a", "lru_w_x")):
        upd[n] = _matrix_update((gath[1 + k], gath[4 + k]), W[n], M[n], V[n], "update_" + n)

    for n in SHARDED:
        if n == "w_in":
            tr = lambda a: jnp.swapaxes(a, 1, 2)
            res = _adamw(tr(W[n]), grads[n], tr(M[n]), tr(V[n]), "adamw_" + n)
            upd[n] = tuple(tr(a) for a in (grads[n],) + tuple(res))
        else:
            upd[n] = (grads[n],) + tuple(_adamw(W[n], grads[n], M[n], V[n], "adamw_" + n))

    return (loss, d0[None], *[upd[n][0] for n in WEIGHTS], *[upd[n][1] for n in WEIGHTS],
            *[upd[n][2] for n in WEIGHTS], *[upd[n][3] for n in WEIGHTS])
```

```python
import functools
import math

import jax
import jax.numpy as jnp
from jax import lax
from jax.experimental import pallas as pl
from jax.experimental.pallas import tpu as pltpu

F32, BF16 = jnp.float32, jnp.bfloat16
MESH = pl.DeviceIdType.MESH

S, D, DEPTH = 2048, 1024, 2
CHUNK, EPS = 64, 1e-6
GM_W, GM_G, GM_B = 1024, 4, 128
H, NOPE, ROPE, VDIM = 8, 128, 64, 128
QR, KVR = 384, 256
MLA_W = H * VDIM
LRU_W, LRU_NB, LRU_BW, LRU_C, CONV_W = 1280, 16, 80, 8.0, 4
ROPE_THETA = 10000.0
IN_SIZES = (GM_W, GM_W, GM_W, QR, KVR, ROPE, MLA_W, LRU_W, LRU_W, D, D, D)
N_IN = sum(IN_SIZES)
N_CHIPS = 4
ADAM_LR, ADAM_B1, ADAM_B2, ADAM_EPS, ADAM_WD, ADAM_STEP = 0.001, 0.9, 0.999, 1e-08, 0.01, 10

HP = 256
O_U, O_V, O_ZA, O_GA, O_GB, O_GC = 0, 1024, 2048, 3072, 4096, 5120
O_CKV, O_KR, O_CQ, O_XC, O_ZC, O_ZB = 6144, 6400, 6528, 7680, 8960, 10240
NP = 11264
VMEM_LIMIT = 48 * 1024 * 1024


def _tile(dim, target):
    if dim <= target:
        return dim
    t = (target // 128) * 128
    while dim % t:
        t -= 128
    return t


def _sig(x):
    return jax.nn.sigmoid(x)


def _silu(x):
    return x * _sig(x)


def _dsilu(x):
    s = _sig(x)
    return s * (1.0 + x * (1.0 - s))


def _mm(a, b, mode, name, out_dtype=F32, tm=512, tn=512, tk=1024, b_lead=None, out_lead=None):
    b2 = b.shape[1:] if b_lead is not None else b.shape
    if mode == "nn":
        (M, K), (K2, N) = a.shape, b2
    elif mode == "nt":
        (M, K), (N, K2) = a.shape, b2
    else:
        (K, M), (K2, N) = a.shape, b2
    assert K == K2, (name, a.shape, b.shape)
    tm, tn, tk = _tile(M, tm), _tile(N, tn), _tile(K, tk)
    nk = K // tk
    if mode == "tn":
        a_spec = pl.BlockSpec((tk, tm), lambda i, j, k: (k, i))
        lhs_c = 0
    else:
        a_spec = pl.BlockSpec((tm, tk), lambda i, j, k: (i, k))
        lhs_c = 1
    b_blk, b_idx, rhs_c = ((tn, tk), (lambda i, j, k: (j, k)), 1) if mode == "nt" else ((tk, tn), (lambda i, j, k: (k, j)), 0)
    if b_lead is None:
        b_spec = pl.BlockSpec(b_blk, b_idx)
    else:
        b_spec = pl.BlockSpec((None,) + b_blk, functools.partial(lambda i, j, k, f, l: (l,) + f(i, j, k), f=b_idx, l=b_lead))
    dims = (((lhs_c,), (rhs_c,)), ((), ()))
    in_specs, args, aliases = [a_spec, b_spec], [a, b], {}
    if out_lead is None:
        out_spec = pl.BlockSpec((tm, tn), lambda i, j, k: (i, j))
        out_shape = jax.ShapeDtypeStruct((M, N), out_dtype)
    else:
        l_out, n_lead, buf = out_lead
        out_spec = pl.BlockSpec((None, tm, tn), functools.partial(lambda i, j, k, l: (l, i, j), l=l_out))
        out_shape = jax.ShapeDtypeStruct((n_lead, M, N), out_dtype)
        if buf is not None:
            in_specs.append(pl.BlockSpec(memory_space=pl.ANY))
            args.append(buf)
            aliases = {2: 0}

    def body(a_ref, b_ref, *rest):
        o_ref, acc_ref = rest[-2:]
        k = pl.program_id(2)

        @pl.when(k == 0)
        def _():
            acc_ref[...] = jnp.zeros_like(acc_ref)

        acc_ref[...] += lax.dot_general(a_ref[...].astype(BF16), b_ref[...].astype(BF16), dims,
                                        preferred_element_type=F32)

        @pl.when(k == nk - 1)
        def _():
            o_ref[...] = acc_ref[...].astype(o_ref.dtype)

    return pl.pallas_call(
        body, name=name, grid=(M // tm, N // tn, nk),
        in_specs=in_specs, out_specs=out_spec, out_shape=out_shape,
        scratch_shapes=[pltpu.VMEM((tm, tn), F32)], input_output_aliases=aliases,
        compiler_params=pltpu.CompilerParams(dimension_semantics=("parallel", "parallel", "arbitrary"),
                                             vmem_limit_bytes=VMEM_LIMIT),
    )(*args)


def _rows(fn, name, tm, rows, halos=(), fulls=(), outs=(), accs=()):
    n = S // tm
    in_specs, args = [], []
    for arr, w, cb in rows:
        in_specs.append(pl.BlockSpec((tm, w), functools.partial(lambda i, cb: (i, cb), cb=cb)))
        args.append(arr)
    for arr, w, cb, side in halos:
        if side == "prev":
            im = functools.partial(lambda i, cb: (jnp.maximum(i * (tm // 16) - 1, 0), cb), cb=cb)
        else:
            im = functools.partial(lambda i, cb: (jnp.minimum((i + 1) * (tm // 16), S // 16 - 1), cb), cb=cb)
        in_specs.append(pl.BlockSpec((16, w), im))
        args.append(arr)
    for arr in fulls:
        in_specs.append(pl.BlockSpec(arr.shape, functools.partial(lambda i, nd: (0,) * nd, nd=arr.ndim)))
        args.append(arr)
    out_shape, out_specs, aliases, n_alias = [], [], {}, 0
    for k, o in enumerate(outs):
        if len(o) == 3 and o[2] == "T":
            out_shape.append(jax.ShapeDtypeStruct((o[0], S), o[1]))
            out_specs.append(pl.BlockSpec((o[0], tm), lambda i: (0, i)))
        elif len(o) == 3:
            buf, total, cb = o[2]
            out_shape.append(jax.ShapeDtypeStruct((S, total), o[1]))
            out_specs.append(pl.BlockSpec((tm, o[0]), functools.partial(lambda i, cb: (i, cb), cb=cb)))
            if buf is not None:
                aliases[len(args)] = k
                in_specs.append(pl.BlockSpec(memory_space=pl.ANY))
                args.append(buf)
                n_alias += 1
        else:
            out_shape.append(jax.ShapeDtypeStruct((S, o[0]), o[1]))
            out_specs.append(pl.BlockSpec((tm, o[0]), lambda i: (i, 0)))
    for shp in accs:
        out_shape.append(jax.ShapeDtypeStruct(shp, F32))
        out_specs.append(pl.BlockSpec(shp, functools.partial(lambda i, nd: (0,) * nd, nd=len(shp))))
    nr, nh, nf, no, na = len(rows), len(halos), len(fulls), len(outs), len(accs)

    def body(*refs):
        i = pl.program_id(0)
        ins, orefs = refs[:nr + nh + nf], refs[nr + nh + nf + n_alias:]
        rv = [r[...].astype(F32) for r in ins[:nr]]
        hv = [r[...].astype(F32)[8:] if h[3] == "prev" else r[...].astype(F32)[:8] for r, h in zip(ins[nr:nr + nh], halos)]
        fv = [r[...] for r in ins[nr + nh:]]
        o, a = fn(i, rv, hv, fv)
        assert len(o) == no and len(a) == na, name
        for spec, ref, val in zip(outs, orefs[:no], o):
            ref[...] = (val.T if len(spec) == 3 and spec[2] == "T" else val).astype(ref.dtype)
        if na:
            @pl.when(i == 0)
            def _():
                for ref in orefs[no:]:
                    ref[...] = jnp.zeros_like(ref)

            for ref, val in zip(orefs[no:], a):
                ref[...] += val

    res = pl.pallas_call(
        body, name=name, grid=(n,), in_specs=in_specs, out_specs=out_specs, out_shape=out_shape,
        input_output_aliases=aliases,
        compiler_params=pltpu.CompilerParams(dimension_semantics=("arbitrary",), vmem_limit_bytes=VMEM_LIMIT),
    )(*args)
    return res


def _shift_down(xb, halo, s, row):
    fix = jnp.tile(pltpu.roll(halo, s, 0), (xb.shape[0] // 8, 1))
    return jnp.where(row >= s, pltpu.roll(xb, s, 0), fix)


def _shift_up(xb, halo, s, row):
    tm = xb.shape[0]
    fix = jnp.tile(pltpu.roll(halo, 8 - s, 0), (tm // 8, 1))
    return jnp.where(row < tm - s, pltpu.roll(xb, tm - s, 0), fix)


def _rms(x):
    return lax.rsqrt(jnp.mean(x * x, axis=-1, keepdims=True) + EPS)


def _rms_bwd(dy, x, g):
    r = _rms(x)
    xh = x * r
    dxh = dy * g
    dx = r * (dxh - xh * jnp.mean(dxh * xh, axis=-1, keepdims=True))
    return dx, dy * xh


def _colsum(x):
    return jnp.sum(x, axis=0, keepdims=True)


def _prenorm_fwd(x, g, token=None):
    def fn(i, rv, hv, fv):
        return [rv[0] * _rms(rv[0]) * fv[0]], []
    return _rows(fn, "prenorm_fwd", 256, [(x, D, 0)], fulls=[g] + ([] if token is None else [token]), outs=[(D, BF16)])[0]


def _gm_mask():
    r = lax.broadcasted_iota(jnp.int32, (GM_B, GM_B), 0) // CHUNK
    c = lax.broadcasted_iota(jnp.int32, (GM_B, GM_B), 1) // CHUNK
    return c <= r


def _gm_norm(v, g, b):
    mu = jnp.mean(v, axis=-1, keepdims=True)
    vc = v - mu
    rs = lax.rsqrt(jnp.mean(vc * vc, axis=-1, keepdims=True) + EPS)
    vh = vc * rs
    return vh, rs, vh * g + b


def _gm_sv(vn, ws, bst):
    mask = _gm_mask()
    gw = GM_W // GM_G
    parts = []
    for g in range(GM_G):
        wm = jnp.where(mask, ws[g], 0.0).astype(BF16)
        parts.append(jnp.dot(wm, vn[:, g * gw:(g + 1) * gw].astype(BF16), preferred_element_type=F32)
                     + bst[:, g:g + 1])
    return jnp.concatenate(parts, axis=1)


def _gmlp_fwd(proj, ln_g, ln_b, ws, bst):
    def fn(i, rv, hv, fv):
        u, v, z = rv
        g, b, w, bt = fv
        _, _, vn = _gm_norm(v, g, b)
        return [u * _gm_sv(vn, w, bt) * _silu(z)], []
    return _rows(fn, "gmlp_fwd", GM_B, [(proj, GM_W, 0), (proj, GM_W, 1), (proj, GM_W, 2)],
                 fulls=[ln_g, ln_b, ws, bst], outs=[(GM_W, BF16)])[0]


def _mla_prep_fwd(proj, qg, kvg):
    def fn(i, rv, hv, fv):
        cq, ckv = rv
        g1, g2 = fv
        return [cq * _rms(cq) * g1, ckv * _rms(ckv) * g2], []
    return _rows(fn, "mla_prep_fwd", 256, [(proj, QR, O_CQ // QR), (proj, KVR, O_CKV // KVR)],
                 fulls=[qg, kvg], outs=[(QR, BF16), (KVR, BF16)])


def _rot(t, cc, sa, sb):
    return t * cc + pltpu.roll(t, 32, 1) * sa + pltpu.roll(t, 96, 1) * sb


def _rot_t(g, cc, sa, sb):
    return g * cc + pltpu.roll(g * sa, 96, 1) + pltpu.roll(g * sb, 32, 1)


def _rope_tables():
    pos = jnp.arange(S, dtype=F32)
    inv_freq = ROPE_THETA ** (-jnp.arange(0, ROPE, 2, dtype=F32) / ROPE)
    ang = pos[:, None] * inv_freq[None, :]
    cos, sin, z = jnp.cos(ang), jnp.sin(ang), jnp.zeros((S, 32), F32)
    cc = jnp.concatenate([cos, cos, z, z], axis=1)
    sa = jnp.concatenate([z, sin, z, z], axis=1)
    sb = jnp.concatenate([-sin, z, z, z], axis=1)
    return cc, sa, sb


ATT_SCALE = 1.0 / math.sqrt(NOPE + ROPE)


def _rope_fwd(q, kv, proj, tabs):
    def fn(i, rv, hv, fv):
        qb, kvb, kr, cc, sa, sb = rv
        krr = _rot(kr, cc, sa, sb)
        qs, ks = [], []
        for h in range(H):
            qs += [qb[:, h * HP:h * HP + 128] * ATT_SCALE, _rot(qb[:, h * HP + 128:(h + 1) * HP], cc, sa, sb) * ATT_SCALE]
            ks += [kvb[:, h * 128:(h + 1) * 128], krr]
        kc = jnp.concatenate(ks, axis=1)
        vv = kvb[:, H * NOPE:]
        return [jnp.concatenate(qs, axis=1), kc, kc, vv, vv], []
    cc, sa, sb = tabs
    return _rows(fn, "rope_fwd", 256,
                 [(q, H * HP, 0), (kv, H * 256, 0), (proj, 128, O_KR // 128), (cc, 128, 0), (sa, 128, 0), (sb, 128, 0)],
                 outs=[(H * HP, BF16), (H * HP, BF16), (H * HP, BF16, "T"), (MLA_W, BF16), (MLA_W, BF16, "T")])


TQ, TC, ATT_NB = 256, 128, 4
_NT = (((1,), (1,)), ((), ()))


def _attn_allowed(i, kc):
    kpos = kc * TC + lax.broadcasted_iota(jnp.int32, (TC, TQ), 0)
    qpos = i * TQ + lax.broadcasted_iota(jnp.int32, (TC, TQ), 1)
    return (kpos // CHUNK) <= (qpos // CHUNK)


def _attn_fwd(qc, kc, vt):
    def body(q_ref, k_ref, vt_ref, o_ref, l_ref):
        i = pl.program_id(1)
        q = q_ref[...]

        def scores(sb):
            t0s = [pl.multiple_of((sb * ATT_NB + c) * TC, TC) for c in range(ATT_NB)]
            return [lax.dot_general(k_ref[pl.ds(t0, TC), :], q, _NT, preferred_element_type=F32) for t0 in t0s]

        def block(sb, ss, carry, masked):
            m, l, acc = carry
            t0s = [pl.multiple_of((sb * ATT_NB + c) * TC, TC) for c in range(ATT_NB)]
            if masked:
                ss = [jnp.where(_attn_allowed(i, sb * ATT_NB + c), s, -1e30) for c, s in enumerate(ss)]
            m_new = m
            for s in ss:
                m_new = jnp.maximum(m_new, jnp.max(s, axis=0, keepdims=True))
            alpha = jnp.exp(m - m_new)
            ps = [jnp.exp(s - m_new) for s in ss]
            l = alpha * l
            acc = alpha * acc
            for t0, p in zip(t0s, ps):
                l = l + jnp.sum(p, axis=0, keepdims=True)
                acc = acc + jnp.dot(vt_ref[:, pl.ds(t0, TC)], p.astype(BF16), preferred_element_type=F32)
            return m_new, l, acc

        nsb = (i + 2) // 2
        c = (jnp.full((1, TQ), -1e30, F32), jnp.zeros((1, TQ), F32), jnp.zeros((VDIM, TQ), F32))

        def step(sb, sc):
            nxt = scores(sb + 1)
            return nxt, block(sb, sc[0], sc[1], False)

        ss, c = lax.fori_loop(0, nsb - 1, step, (scores(0), c))
        m, l, acc = block(nsb - 1, ss, c, True)
        o_ref[...] = (acc / l).T
        l_ref[...] = m + jnp.log(l)

    return pl.pallas_call(
        body, name="attn_fwd", grid=(H, S // TQ),
        in_specs=[pl.BlockSpec((TQ, HP), lambda h, i: (i, h)),
                  pl.BlockSpec((S, HP), lambda h, i: (0, h)),
                  pl.BlockSpec((VDIM, S), lambda h, i: (h, 0))],
        out_specs=[pl.BlockSpec((TQ, VDIM), lambda h, i: (i, h)), pl.BlockSpec((None, 1, TQ), lambda h, i: (h, 0, i))],
        out_shape=[jax.ShapeDtypeStruct((S, MLA_W), F32), jax.ShapeDtypeStruct((H, 1, S), F32)],
        compiler_params=pltpu.CompilerParams(dimension_semantics=("parallel", "arbitrary"),
                                             vmem_limit_bytes=VMEM_LIMIT),
    )(qc, kc, vt)


def _gate_mul_fwd(name, val, proj, width, cb):
    def fn(i, rv, hv, fv):
        o, z = rv
        return [o * _silu(z)], []
    return _rows(fn, name, 256, [(val, width, 0), (proj, width, cb)], outs=[(width, BF16)])[0]


def _conv_fwd(proj, w, b):
    def fn(i, rv, hv, fv):
        (xb,), (halo,), (ww, bb) = rv, hv, fv
        halo = jnp.where(i > 0, halo, 0.0)
        row = lax.broadcasted_iota(jnp.int32, xb.shape, 0)
        acc = bb + ww[3:4] * xb
        for s in range(1, CONV_W):
            acc = acc + ww[3 - s:4 - s] * _shift_down(xb, halo, s, row)
        return [acc, acc], []
    return _rows(fn, "conv_fwd", 128, [(proj, LRU_W, O_XC // LRU_W)], halos=[(proj, LRU_W, O_XC // LRU_W, "prev")],
                 fulls=[w, b], outs=[(LRU_W, F32), (LRU_W, BF16)])


def _lru_terms(ga, gx, xc, ba, bx, lam):
    r = _sig(ga + ba)
    ig = _sig(gx + bx)
    sp = jnp.maximum(-lam, 0.0) + jnp.log(1.0 + jnp.exp(-jnp.abs(lam)))
    log_a = -LRU_C * r * sp
    a = jnp.exp(log_a)
    e2 = jnp.exp(2.0 * log_a)
    om = 1.0 - e2
    mult = jnp.sqrt(jnp.maximum(om, 0.0))
    return r, ig, sp, a, e2, om, mult


def _lru_gates_fwd(gates, xc, ba, bx, lam):
    def fn(i, rv, hv, fv):
        ga, gx, x = rv
        r, ig, sp, a, e2, om, mult = _lru_terms(ga, gx, x, *fv)
        return [a, mult * (ig * x)], []
    return _rows(fn, "lru_gates_fwd", 128, [(gates, LRU_W, 0), (gates, LRU_W, 1), (xc, LRU_W, 0)],
                 fulls=[ba, bx, lam], outs=[(LRU_W, F32), (LRU_W, F32)])


SCAN_T, SCAN_CW = 64, 256


def _scan_fwd(a, b):
    def body(a_ref, b_ref, h_ref):
        row = lax.broadcasted_iota(jnp.int32, (SCAN_T, SCAN_CW), 0)

        def step(blk, hc):
            t0 = pl.multiple_of(blk * SCAN_T, SCAN_T)
            A = a_ref[pl.ds(t0, SCAN_T), :]
            B = b_ref[pl.ds(t0, SCAN_T), :]
            d = 1
            while d < SCAN_T:
                keep = row >= d
                A_s = jnp.where(keep, pltpu.roll(A, d, 0), 1.0)
                B_s = jnp.where(keep, pltpu.roll(B, d, 0), 0.0)
                B = A * B_s + B
                A = A * A_s
                d *= 2
            hh = A * hc + B
            h_ref[pl.ds(t0, SCAN_T), :] = hh
            return hh[SCAN_T - 1:SCAN_T, :]

        lax.fori_loop(0, S // SCAN_T, step, jnp.zeros((1, SCAN_CW), F32))

    spec = pl.BlockSpec((S, SCAN_CW), lambda j: (0, j))
    return pl.pallas_call(
        body, name="scan_fwd", grid=(LRU_W // SCAN_CW,), in_specs=[spec, spec], out_specs=spec,
        out_shape=jax.ShapeDtypeStruct((S, LRU_W), F32),
        compiler_params=pltpu.CompilerParams(dimension_semantics=("parallel",), vmem_limit_bytes=VMEM_LIMIT),
    )(a, b)


def _merge_fwd(pa, pb, pc, proj):
    def fn(i, rv, hv, fv):
        a, b, c, ga, gb, gc = rv
        return [_sig(ga) * a + _sig(gb) * b + _sig(gc) * c], []
    return _rows(fn, "merge_fwd", 256,
                 [(pa, D, 0), (pb, D, 0), (pc, D, 0), (proj, D, O_GA // D), (proj, D, O_GB // D), (proj, D, O_GC // D)],
                 outs=[(D, BF16)])[0]


def _post_fwd(x, o2, g):
    def fn(i, rv, hv, fv):
        xb, ob = rv
        return [xb + ob * _rms(ob) * fv[0]], []
    return _rows(fn, "post_fwd", 256, [(x, D, 0), (o2, D, 0)], fulls=[g], outs=[(D, F32)])[0]


SB = 640
BD_TM = 512


def _bd_fwd(xcb, wsb, l):
    def body(x_ref, w_ref, o_ref):
        o_ref[...] = jnp.dot(x_ref[...], w_ref[...], preferred_element_type=F32).astype(o_ref.dtype)

    return pl.pallas_call(
        body, name="lru_gate_mm", grid=(S // BD_TM, 4),
        in_specs=[pl.BlockSpec((BD_TM, SB), lambda i, q: (i, q % 2)),
                  pl.BlockSpec((None, None, SB, SB), lambda i, q: (l, q, 0, 0))],
        out_specs=pl.BlockSpec((BD_TM, SB), lambda i, q: (i, q)),
        out_shape=jax.ShapeDtypeStruct((S, 2 * LRU_W), BF16),
        compiler_params=pltpu.CompilerParams(dimension_semantics=("parallel", "parallel"), vmem_limit_bytes=VMEM_LIMIT),
    )(xcb, wsb)


def _bd_dx(dgates, wsb, l):
    def body(d_ref, w_ref, o_ref, acc_ref):
        g = pl.program_id(2)

        @pl.when(g == 0)
        def _():
            acc_ref[...] = jnp.zeros_like(acc_ref)

        acc_ref[...] += lax.dot_general(d_ref[...], w_ref[...], (((1,), (1,)), ((), ())), preferred_element_type=F32)

        @pl.when(g == 1)
        def _():
            o_ref[...] = acc_ref[...].astype(o_ref.dtype)

    return pl.pallas_call(
        body, name="lru_gate_dx", grid=(S // BD_TM, 2, 2),
        in_specs=[pl.BlockSpec((BD_TM, SB), lambda i, s, g: (i, 2 * g + s)),
                  pl.BlockSpec((None, None, SB, SB), lambda i, s, g: (l, 2 * g + s, 0, 0))],
        out_specs=pl.BlockSpec((BD_TM, SB), lambda i, s, g: (i, s)),
        out_shape=jax.ShapeDtypeStruct((S, LRU_W), BF16),
        scratch_shapes=[pltpu.VMEM((BD_TM, SB), F32)],
        compiler_params=pltpu.CompilerParams(dimension_semantics=("parallel", "parallel", "arbitrary"),
                                             vmem_limit_bytes=VMEM_LIMIT),
    )(dgates, wsb)


def _bd_dw(xcb, dgates):
    tk = 1024

    def body(x_ref, d_ref, o_ref):
        @pl.when(pl.program_id(1) == 0)
        def _():
            o_ref[...] = jnp.zeros_like(o_ref)

        o_ref[...] += lax.dot_general(x_ref[...], d_ref[...], (((0,), (0,)), ((), ())), preferred_element_type=F32)

    return pl.pallas_call(
        body, name="lru_gate_dw", grid=(4, S // tk),
        in_specs=[pl.BlockSpec((tk, SB), lambda q, k: (k, q % 2)), pl.BlockSpec((tk, SB), lambda q, k: (k, q))],
        out_specs=pl.BlockSpec((None, SB, SB), lambda q, k: (q, 0, 0)),
        out_shape=jax.ShapeDtypeStruct((4, SB, SB), F32),
        compiler_params=pltpu.CompilerParams(dimension_semantics=("parallel", "arbitrary"), vmem_limit_bytes=VMEM_LIMIT),
    )(xcb, dgates)


def _bd_extract(dwsb):
    def body(w_ref, o_ref):
        lane = lax.broadcasted_iota(jnp.int32, (LRU_BW, 128), 1)
        for q in range(4):
            for kk in range(8):
                c0 = LRU_BW * kk
                w0, off = (c0 // 128) * 128, c0 % 128
                rows = pl.ds(LRU_BW * kk, LRU_BW)
                blk = w_ref[q, rows, w0:w0 + 128]
                if off:
                    blk = pltpu.roll(blk, 128 - off, 1)
                    if off + LRU_BW > 128:
                        nxt = pltpu.roll(w_ref[q, rows, w0 + 128:w0 + 256], 128 - off, 1)
                        blk = jnp.where(lane < 128 - off, blk, nxt)
                o_ref[q // 2, 8 * (q % 2) + kk] = blk.astype(BF16)

    return pl.pallas_call(
        body, name="lru_gate_dw_blocks",
        in_specs=[pl.BlockSpec(memory_space=pltpu.VMEM)], out_specs=pl.BlockSpec(memory_space=pltpu.VMEM),
        out_shape=jax.ShapeDtypeStruct((2, LRU_NB, LRU_BW, 128), BF16),
        compiler_params=pltpu.CompilerParams(vmem_limit_bytes=VMEM_LIMIT),
    )(dwsb)


def _layer_fwd(x, P, l, tabs, token=None, late=None):
    A = {"x": x}
    A["h"] = _prenorm_fwd(x, P["pre_g"], token)
    proj = A["proj"] = _mm(A["h"], P["wp"], "nt", "in_proj", out_dtype=BF16, tm=1024)
    A["ya"] = _gmlp_fwd(proj, P["ln_g"], P["ln_b"], P["ws"], P["bst"])
    A["xc"], A["xcb"] = _conv_fwd(proj, P["conv_w"], P["conv_b"])
    A["gates"] = _bd_fwd(A["xcb"], P["wsb"], l)
    A["a"], bterm = _lru_gates_fwd(A["gates"], A["xc"], P["ba"], P["bx"], P["lam"])
    A["hs"] = _scan_fwd(A["a"], bterm)
    A["yc"] = _gate_mul_fwd("yc_fwd", A["hs"], proj, LRU_W, O_ZC // LRU_W)
    if late is not None:
        P.update(late(A["yc"]))
    A["cqn"], A["ckvn"] = _mla_prep_fwd(proj, P["qg"], P["kvg"])
    q = _mm(A["cqn"], P["wuq"], "nt", "q_up", out_dtype=BF16)
    kv = _mm(A["ckvn"], P["wukv"], "nt", "kv_up", out_dtype=BF16)
    A["qc"], A["kc"], A["kct"], A["vv"], vt = _rope_fwd(q, kv, proj, tabs)
    A["o"], A["lse"] = _attn_fwd(A["qc"], A["kc"], vt)
    A["yb"] = _gate_mul_fwd("yb_fwd", A["o"], proj, MLA_W, O_ZB // MLA_W)
    A["pa"] = _mm(A["ya"], P["wpa"], "nn", "proj_a", out_dtype=BF16)
    A["pb"] = _mm(A["yb"], P["wpb"], "nn", "proj_b", out_dtype=BF16)
    A["pc"] = _mm(A["yc"], P["wpc"], "nn", "proj_c", out_dtype=BF16)
    A["merged"] = _merge_fwd(A["pa"], A["pb"], A["pc"], proj)
    A["o2"] = _mm(A["merged"], P["wout"], "nn", "out_proj")
    return _post_fwd(x, A["o2"], P["post_g"]), A


def _loss_fwd(y, tgt):
    def fn(i, rv, hv, fv):
        yb, tb = rv
        e = yb - tb
        part = 0.5 * jnp.sum(jnp.mean(e * e, axis=-1, keepdims=True), axis=0, keepdims=True)
        return [e * (1.0 / D)], [part]
    return _rows(fn, "loss", 256, [(y, D, 0), (tgt, D, 0)], outs=[(D, F32)], accs=[(1, 1)])


def _post_bwd(dxn, o2, g, token=None):
    def fn(i, rv, hv, fv):
        dy, ob = rv
        dx, dg = _rms_bwd(dy, ob, fv[0])
        return [dx], [_colsum(dg)]
    return _rows(fn, "post_bwd", 256, [(dxn, D, 0), (o2, D, 0)], fulls=[g] + ([] if token is None else [token]),
                 outs=[(D, BF16)], accs=[(1, D)])


def _merge_bwd(dm, pa, pb, pc, proj, dproj):
    def fn(i, rv, hv, fv):
        d, a, b, c, ga, gb, gc = rv
        outs_p, outs_g = [], []
        for p, gg in ((a, ga), (b, gb), (c, gc)):
            s = _sig(gg)
            outs_p.append(d * s)
            outs_g.append(d * p * s * (1.0 - s))
        return outs_p + [jnp.concatenate(outs_g, axis=1)], []
    return _rows(fn, "merge_bwd", 128,
                 [(dm, D, 0), (pa, D, 0), (pb, D, 0), (pc, D, 0),
                  (proj, D, O_GA // D), (proj, D, O_GB // D), (proj, D, O_GC // D)],
                 outs=[(D, BF16)] * 3 + [(3 * D, BF16, (dproj, NP, O_GA // (3 * D)))])


def _gmlp_bwd(dya, proj, ln_g, ln_b, ws, bst, dproj):
    gw = GM_W // GM_G

    def fn(i, rv, hv, fv):
        dy, u, v, z = rv
        g, b, w, bt = fv
        vh, rs, vn = _gm_norm(v, g, b)
        sv = _gm_sv(vn, w, bt)
        sz = _silu(z)
        du = dy * sv * sz
        dsv = dy * u * sz
        dz = dy * u * sv * _dsilu(z)
        mask = _gm_mask()
        lane = lax.broadcasted_iota(jnp.int32, (GM_B, 128), 1)
        dvn_parts, dws, dbst = [], [], jnp.zeros((GM_B, 128), F32)
        for k in range(GM_G):
            wm = jnp.where(mask, w[k], 0.0).astype(BF16)
            dsk = dsv[:, k * gw:(k + 1) * gw]
            dskb = dsk.astype(BF16)
            dvn_parts.append(lax.dot_general(wm, dskb, (((0,), (0,)), ((), ())), preferred_element_type=F32))
            dwk = lax.dot_general(dskb, vn[:, k * gw:(k + 1) * gw].astype(BF16), (((1,), (1,)), ((), ())),
                                  preferred_element_type=F32)
            dws.append(jnp.where(mask, dwk, 0.0)[None])
            dbst = dbst + jnp.where(lane == k, jnp.sum(dsk, axis=1, keepdims=True), 0.0)
        dvn = jnp.concatenate(dvn_parts, axis=1)
        dvh = dvn * g
        dv = rs * (dvh - jnp.mean(dvh, axis=-1, keepdims=True) - vh * jnp.mean(dvh * vh, axis=-1, keepdims=True))
        return ([jnp.concatenate([du, dv, dz], axis=1)],
                [jnp.concatenate(dws, axis=0), dbst, _colsum(dvn * vh), _colsum(dvn)])
    return _rows(fn, "gmlp_bwd", GM_B, [(dya, GM_W, 0), (proj, GM_W, 0), (proj, GM_W, 1), (proj, GM_W, 2)],
                 fulls=[ln_g, ln_b, ws, bst], outs=[(3 * GM_W, BF16, (dproj, NP, O_U // (3 * GM_W)))],
                 accs=[(GM_G, GM_B, GM_B), (GM_B, 128), (1, GM_W), (1, GM_W)])


def _yb_bwd(dyb, o, proj, dproj):
    def fn(i, rv, hv, fv):
        dy, ob, z = rv
        do = dy * _silu(z)
        prod = do * ob
        lane = lax.broadcasted_iota(jnp.int32, (dy.shape[0], 128), 1)
        dl = jnp.zeros((dy.shape[0], 128), F32)
        for h in range(H):
            dl = dl + jnp.where(lane == h, jnp.sum(prod[:, h * VDIM:(h + 1) * VDIM], axis=1, keepdims=True), 0.0)
        return [do, dl, dy * ob * _dsilu(z)], []
    return _rows(fn, "yb_bwd", 256, [(dyb, MLA_W, 0), (o, MLA_W, 0), (proj, MLA_W, O_ZB // MLA_W)],
                 outs=[(MLA_W, BF16), (128, F32, "T"), (MLA_W, BF16, (dproj, NP, O_ZB // MLA_W))])


def _attn_bwd(qc, kc, kct, vv, do, lse, dlt):
    def body(q_ref, k_ref, kt_ref, v_ref, do_ref, l_ref, d_ref, dq_ref, dk_ref, dv_ref, dqt_ref):
        h, i = pl.program_id(0), pl.program_id(1)

        @pl.when(i == 0)
        def _():
            dk_ref[...] = jnp.zeros_like(dk_ref)
            dv_ref[...] = jnp.zeros_like(dv_ref)

        q = q_ref[...]
        dob = do_ref[...]
        lse = l_ref[...]
        dl = d_ref[pl.ds(h, 1), :]
        dqt_ref[...] = jnp.zeros_like(dqt_ref)

        def rows_of(sb, c):
            return pl.ds(pl.multiple_of((sb * ATT_NB + c) * TC, TC), TC)

        def front(sb):
            return [(lax.dot_general(k_ref[rows_of(sb, c), :], q, _NT, preferred_element_type=F32),
                     lax.dot_general(v_ref[rows_of(sb, c), :], dob, _NT, preferred_element_type=F32))
                    for c in range(ATT_NB)]

        def block(sb, sd, masked):
            dqt = None
            for c, (s, dp) in enumerate(sd):
                rows = rows_of(sb, c)
                p = jnp.exp(s - lse)
                if masked:
                    p = jnp.where(_attn_allowed(i, sb * ATT_NB + c), p, 0.0)
                ds = (p * (dp - dl)).astype(BF16)
                dk_ref[rows, :] += jnp.dot(ds, q, preferred_element_type=F32)
                dv_ref[rows, :] += jnp.dot(p.astype(BF16), dob, preferred_element_type=F32)
                part = jnp.dot(kt_ref[:, rows], ds, preferred_element_type=F32)
                dqt = part if dqt is None else dqt + part
            dqt_ref[...] += dqt

        def step(sb, sd):
            nxt = front(sb + 1)
            block(sb, sd, False)
            return nxt

        nsb = (i + 2) // 2
        sd = lax.fori_loop(0, nsb - 1, step, front(0))
        block(nsb - 1, sd, True)
        dq_ref[...] = dqt_ref[...].T.astype(dq_ref.dtype)

    blk = lambda w: pl.BlockSpec((TQ, w), lambda h, i: (i, h))
    head = lambda w: pl.BlockSpec((S, w), lambda h, i: (0, h))
    return pl.pallas_call(
        body, name="attn_bwd", grid=(H, S // TQ),
        in_specs=[blk(HP), head(HP), pl.BlockSpec((HP, S), lambda h, i: (h, 0)), head(VDIM), blk(VDIM),
                  pl.BlockSpec((None, 1, TQ), lambda h, i: (h, 0, i)), pl.BlockSpec((8, TQ), lambda h, i: (0, i))],
        out_specs=[blk(HP), head(HP), head(VDIM)],
        out_shape=[jax.ShapeDtypeStruct((S, H * HP), BF16), jax.ShapeDtypeStruct((S, H * HP), F32),
                   jax.ShapeDtypeStruct((S, MLA_W), F32)],
        scratch_shapes=[pltpu.VMEM((HP, TQ), F32)],
        compiler_params=pltpu.CompilerParams(dimension_semantics=("parallel", "arbitrary"),
                                             vmem_limit_bytes=VMEM_LIMIT),
    )(qc, kc, kct, vv, do, lse, dlt)


def _rope_bwd(dqc, dkc, dvv, tabs):
    def fn(i, rv, hv, fv):
        dq, dk, dv, cc, sa, sb = rv
        qs, ks = [], []
        dkr = jnp.zeros((dq.shape[0], 128), F32)
        for h in range(H):
            qs += [dq[:, h * HP:h * HP + 128] * ATT_SCALE, _rot_t(dq[:, h * HP + 128:(h + 1) * HP], cc, sa, sb) * ATT_SCALE]
            ks.append(dk[:, h * HP:h * HP + 128])
            dkr = dkr + dk[:, h * HP + 128:(h + 1) * HP]
        return [jnp.concatenate(qs, axis=1), jnp.concatenate(ks + [dv], axis=1), _rot_t(dkr, cc, sa, sb)], []
    cc, sa, sb = tabs
    return _rows(fn, "rope_bwd", 256,
                 [(dqc, H * HP, 0), (dkc, H * HP, 0), (dvv, MLA_W, 0), (cc, 128, 0), (sa, 128, 0), (sb, 128, 0)],
                 outs=[(H * HP, BF16), (H * 256, BF16), (128, BF16)])


MLA_GROUP = 1536


def _mla_prep_bwd(dcqn, dckvn, dkr, proj, qg, kvg, dproj):
    def fn(i, rv, hv, fv):
        d1, d2, dk, cq, ckv = rv
        g1, g2 = fv
        dx1, dg1 = _rms_bwd(d1, cq, g1)
        dx2, dg2 = _rms_bwd(d2, ckv, g2)
        zeros = jnp.zeros((d1.shape[0], MLA_GROUP - KVR - 128 - QR), F32)
        return [jnp.concatenate([dx2, dk.astype(F32), dx1, zeros], axis=1)], [_colsum(dg1), _colsum(dg2)]
    return _rows(fn, "mla_prep_bwd", 256,
                 [(dcqn, QR, 0), (dckvn, KVR, 0), (dkr, 128, 0), (proj, QR, O_CQ // QR), (proj, KVR, O_CKV // KVR)],
                 fulls=[qg, kvg], outs=[(MLA_GROUP, BF16, (dproj, NP, O_CKV // MLA_GROUP))], accs=[(1, QR), (1, KVR)])


def _yc_bwd(dyc, hs, proj, dproj):
    def fn(i, rv, hv, fv):
        dy, hh, z = rv
        return [dy * _silu(z), dy * hh * _dsilu(z)], []
    return _rows(fn, "yc_bwd", 128, [(dyc, LRU_W, 0), (hs, LRU_W, 0), (proj, LRU_W, O_ZC // LRU_W)],
                 outs=[(LRU_W, F32), (LRU_W, BF16, (dproj, NP, O_ZC // LRU_W))])


def _scan_bwd(a, hs, dh):
    nblk = S // SCAN_T

    def body(a_ref, h_ref, dh_ref, da_ref, db_ref):
        row = lax.broadcasted_iota(jnp.int32, (SCAN_T, SCAN_CW), 0)

        def step(j, carry):
            gc, ac = carry
            blk = nblk - 1 - j
            t0 = pl.multiple_of(blk * SCAN_T, SCAN_T)
            av = a_ref[pl.ds(t0, SCAN_T), :]
            A = jnp.where(row < SCAN_T - 1, pltpu.roll(av, SCAN_T - 1, 0), ac)
            B = dh_ref[pl.ds(t0, SCAN_T), :]
            d = 1
            while d < SCAN_T:
                keep = row < SCAN_T - d
                A_s = jnp.where(keep, pltpu.roll(A, SCAN_T - d, 0), 1.0)
                B_s = jnp.where(keep, pltpu.roll(B, SCAN_T - d, 0), 0.0)
                B = A * B_s + B
                A = A * A_s
                d *= 2
            g = A * gc + B
            p0 = pl.multiple_of(jnp.maximum(t0 - 8, 0), 8)
            last = jnp.where(blk > 0, h_ref[pl.ds(p0, 8), :][7:8, :], 0.0)
            h_prev = jnp.where(row >= 1, pltpu.roll(h_ref[pl.ds(t0, SCAN_T), :], 1, 0), last)
            da_ref[pl.ds(t0, SCAN_T), :] = g * h_prev
            db_ref[pl.ds(t0, SCAN_T), :] = g
            return g[0:1, :], av[0:1, :]

        z = jnp.zeros((1, SCAN_CW), F32)
        lax.fori_loop(0, nblk, step, (z, z))

    spec = pl.BlockSpec((S, SCAN_CW), lambda j: (0, j))
    return pl.pallas_call(
        body, name="scan_bwd", grid=(LRU_W // SCAN_CW,), in_specs=[spec] * 3, out_specs=[spec] * 2,
        out_shape=[jax.ShapeDtypeStruct((S, LRU_W), F32)] * 2,
        compiler_params=pltpu.CompilerParams(dimension_semantics=("parallel",), vmem_limit_bytes=VMEM_LIMIT),
    )(a, hs, dh)


def _lru_gates_bwd(da, db, gates, xc, ba, bx, lam):
    def fn(i, rv, hv, fv):
        dav, dbv, ga, gx, x = rv
        bav, bxv, lamv = fv
        r, ig, sp, a, e2, om, mult = _lru_terms(ga, gx, x, bav, bxv, lamv)
        dmult = dbv * ig * x
        dig = dbv * mult * x
        dxc1 = dbv * mult * ig
        dlog_a = dav * a + jnp.where(om > 0.0, dmult * (-e2 / mult), 0.0)
        dr = dlog_a * (-LRU_C * sp)
        dga = dr * r * (1.0 - r)
        dgx = dig * ig * (1.0 - ig)
        dlam = _colsum(dlog_a * (-LRU_C * r)) * (-_sig(-lamv))
        return [jnp.concatenate([dga, dgx], axis=1), dxc1], [_colsum(dga), _colsum(dgx), dlam]
    return _rows(fn, "lru_gates_bwd", 128,
                 [(da, LRU_W, 0), (db, LRU_W, 0), (gates, LRU_W, 0), (gates, LRU_W, 1), (xc, LRU_W, 0)],
                 fulls=[ba, bx, lam], outs=[(2 * LRU_W, BF16), (LRU_W, F32)], accs=[(1, LRU_W)] * 3)


def _conv_bwd(dxc1, dxc2, proj, w, dproj):
    cb = O_XC // LRU_W

    def fn(i, rv, hv, fv):
        d1, d2, xb = rv
        n1, n2, xprev = hv
        ww = fv[0]
        last = i == S // 128 - 1
        dxc = d1 + d2
        nxt = jnp.where(last, 0.0, n1 + n2)
        xprev = jnp.where(i > 0, xprev, 0.0)
        row = lax.broadcasted_iota(jnp.int32, xb.shape, 0)
        dx = ww[3:4] * dxc
        dws = [None] * CONV_W
        dws[3] = _colsum(dxc * xb)
        for s in range(1, CONV_W):
            dx = dx + ww[3 - s:4 - s] * _shift_up(dxc, nxt, s, row)
            dws[3 - s] = _colsum(dxc * _shift_down(xb, xprev, s, row))
        return [dx], [jnp.concatenate(dws, axis=0), _colsum(dxc)]
    return _rows(fn, "conv_bwd", 128, [(dxc1, LRU_W, 0), (dxc2, LRU_W, 0), (proj, LRU_W, cb)],
                 halos=[(dxc1, LRU_W, 0, "next"), (dxc2, LRU_W, 0, "next"), (proj, LRU_W, cb, "prev")],
                 fulls=[w], outs=[(LRU_W, BF16, (dproj, NP, cb))], accs=[(CONV_W, LRU_W), (1, LRU_W)])


def _prenorm_bwd(dxn, dh, x, g):
    def fn(i, rv, hv, fv):
        dy, dhh, xb = rv
        dx, dg = _rms_bwd(dhh, xb, fv[0])
        return [dy + dx], [_colsum(dg)]
    return _rows(fn, "prenorm_bwd", 256, [(dxn, D, 0), (dh, D, 0), (x, D, 0)], fulls=[g], outs=[(D, F32)],
                 accs=[(1, D)])


def _layer_bwd(dxn, A, P, l, tabs, token=None):
    G, GB = {}, {}
    proj = A["proj"]

    def dw(key, a, b, name, **tiles):
        GB[key] = _mm(a, b, "tn", name, out_dtype=BF16, **tiles)

    do2, G["post_g"] = _post_bwd(dxn, A["o2"], P["post_g"], token)
    dm = _mm(do2, P["wout"], "nt", "out_proj_dx", out_dtype=BF16)
    dw("wout", A["merged"], do2, "out_proj_dw")
    dpa, dpb, dpc, dproj = _merge_bwd(dm, A["pa"], A["pb"], A["pc"], proj, None)
    dya = _mm(dpa, P["wpa"], "nt", "proj_a_dx", out_dtype=BF16)
    dw("wpa", A["ya"], dpa, "proj_a_dw")
    dyb = _mm(dpb, P["wpb"], "nt", "proj_b_dx", out_dtype=BF16)
    dw("wpb", A["yb"], dpb, "proj_b_dw")
    dyc = _mm(dpc, P["wpc"], "nt", "proj_c_dx", out_dtype=BF16)
    dw("wpc", A["yc"], dpc, "proj_c_dw")
    dproj, G["ws"], G["bst"], G["ln_g"], G["ln_b"] = _gmlp_bwd(dya, proj, P["ln_g"], P["ln_b"], P["ws"], P["bst"], dproj)
    do, dl, dproj = _yb_bwd(dyb, A["o"], proj, dproj)
    dqc, dkc, dvv = _attn_bwd(A["qc"], A["kc"], A["kct"], A["vv"], do, A["lse"], dl)
    dq, dkv, dkr = _rope_bwd(dqc, dkc, dvv, tabs)
    dcqn = _mm(dq, P["wuq"], "nn", "q_up_dx", out_dtype=BF16)
    dw("wuq", dq, A["cqn"], "q_up_dw")
    dckvn = _mm(dkv, P["wukv"], "nn", "kv_up_dx", out_dtype=BF16)
    dw("wukv", dkv, A["ckvn"], "kv_up_dw")
    dproj, G["qg"], G["kvg"] = _mla_prep_bwd(dcqn, dckvn, dkr, proj, P["qg"], P["kvg"], dproj)
    dhs, dproj = _yc_bwd(dyc, A["hs"], proj, dproj)
    da, db = _scan_bwd(A["a"], A["hs"], dhs)
    dgates, dxc1, G["ba"], G["bx"], G["lam"] = _lru_gates_bwd(da, db, A["gates"], A["xc"], P["ba"], P["bx"], P["lam"])
    dxc2 = _bd_dx(dgates, P["wsb"], l)
    G["wab"] = _bd_extract(_bd_dw(A["xcb"], dgates))
    dproj, G["conv_w"], G["conv_b"] = _conv_bwd(dxc1, dxc2, proj, P["conv_w"], dproj)
    dh = _mm(dproj, P["wp"], "nn", "in_proj_dx", tm=1024, tn=1024)
    dw("wp", dproj, A["h"], "in_proj_dw", tm=1536, tn=1024)
    dx, G["pre_g"] = _prenorm_bwd(dxn, dh, A["x"], P["pre_g"])
    return dx, G, GB


_ORIG_OFF = [0]
for _s in IN_SIZES:
    _ORIG_OFF.append(_ORIG_OFF[-1] + _s)
_PAD_OFF = {0: O_U, 1: O_V, 2: O_ZA, 3: O_CQ, 4: O_CKV, 5: O_KR, 6: O_ZB, 7: O_XC, 8: O_ZC, 9: O_GA, 10: O_GB, 11: O_GC}
SHARD_IN = N_IN // N_CHIPS


def _pieces_w_in(j):
    lo, hi = SHARD_IN * j, SHARD_IN * (j + 1)
    out = []
    for k in range(len(IN_SIZES)):
        a, b = max(lo, _ORIG_OFF[k]), min(hi, _ORIG_OFF[k + 1])
        if a < b:
            out.append((a - lo, _PAD_OFF[k] + a - _ORIG_OFF[k], b - a))
    return out


def _pieces_uq(j):
    return [(192 * hh, HP * (2 * j + hh), NOPE + ROPE) for hh in range(2)]


def _pieces_ukv(j):
    out = []
    for hh in range(2):
        h = 2 * j + hh
        out += [(256 * hh, NOPE * h, NOPE), (256 * hh + NOPE, H * NOPE + VDIM * h, VDIM)]
    return out


def _pieces_rows(r):
    return lambda j: [(0, r * j, r)]


LAYOUT = {
    "w_in": (SHARD_IN, NP, _pieces_w_in),
    "mla_w_uq": (2 * (NOPE + ROPE), H * HP, _pieces_uq),
    "mla_w_ukv": (2 * (NOPE + VDIM), 2 * H * 128, _pieces_ukv),
    "lru_conv_w": (1, N_CHIPS, _pieces_rows(1)),
    "w_proj_a": (GM_W // N_CHIPS, GM_W, _pieces_rows(GM_W // N_CHIPS)),
    "w_proj_b": (MLA_W // N_CHIPS, MLA_W, _pieces_rows(MLA_W // N_CHIPS)),
    "w_proj_c": (LRU_W // N_CHIPS, LRU_W, _pieces_rows(LRU_W // N_CHIPS)),
    "w_out": (D // N_CHIPS, D, _pieces_rows(D // N_CHIPS)),
}
TRANSPOSED = ("w_in", "mla_w_uq", "mla_w_ukv")


def _superblocks(w_a, w_x):
    w6 = jnp.stack([w_a, w_x], axis=1).reshape(DEPTH, 4, 8, LRU_BW, LRU_BW).astype(BF16)
    bands = [jnp.pad(w6[:, :, k], ((0, 0), (0, 0), (0, 0), (LRU_BW * k, SB - LRU_BW * (k + 1)))) for k in range(8)]
    return jnp.concatenate(bands, axis=2)


_HBM = pl.BlockSpec(memory_space=pltpu.HBM)


def _position():
    return lax.axis_index("x"), lax.axis_index("y"), lax.axis_index("c")


def _allgather(blocks, name):
    n = len(blocks)

    def body(*refs):
        ins, outs = refs[:n], refs[n:2 * n]
        send, recv, lsem = refs[2 * n:]
        x, y, c = _position()
        me, sib = (x, y, c), (x, y, 1 - c)
        chips = [(1 - x, y), (x, 1 - y), (1 - x, 1 - y)]

        def cp(k, a, block, to, src=None):
            dst = outs[a].at[4 * block[0] + 2 * block[1] + block[2]]
            return pltpu.make_async_remote_copy(src_ref=dst if src is None else src, dst_ref=dst,
                                                send_sem=send.at[7 * a + k], recv_sem=recv.at[7 * a + k],
                                                device_id=to, device_id_type=MESH)

        mine = [pltpu.make_async_copy(ins[a], outs[a].at[4 * x + 2 * y + c], lsem.at[a]) for a in range(n)]
        for m in mine:
            m.start()
        first = []
        for a in range(n):
            first.append(cp(0, a, me, sib, src=ins[a]))
            first += [cp(1 + j, a, me, (*chip, c), src=ins[a]) for j, chip in enumerate(chips)]
        for f in first:
            f.start()
        passed = []
        for j, chip in enumerate(chips):
            for a in range(n):
                cp(1 + j, a, (*chip, c), me).wait_recv()
                p = cp(4 + j, a, (*chip, c), sib)
                p.start()
                passed.append(p)
        for a in range(n):
            cp(0, a, sib, me).wait_recv()
            for j, chip in enumerate(chips):
                cp(4 + j, a, (*chip, 1 - c), me).wait_recv()
        for f in first + passed:
            f.wait_send()
        for m in mine:
            m.wait()

    return pl.pallas_call(
        body, name=name,
        out_shape=[jax.ShapeDtypeStruct((8,) + b.shape, b.dtype) for b in blocks],
        in_specs=[_HBM] * n, out_specs=[_HBM] * n,
        scratch_shapes=[pltpu.SemaphoreType.DMA((7 * n,)), pltpu.SemaphoreType.DMA((7 * n,)),
                        pltpu.SemaphoreType.DMA((n,))],
    )(*blocks)


_REL = (2, 1, 3)


def _cut(r):
    return r if r < 32 else (r // 2 + 15) // 16 * 16


def _half_rows(r, c0):
    return _cut(r) if c0 == 0 else r - _cut(r)


def _half_pieces(lay_a, jsrc, c0):
    r = lay_a[0]
    lo, hi = (0, _cut(r)) if c0 == 0 else (_cut(r), r)
    out = []
    for s0, d0, nr in lay_a[2](jsrc):
        a, b = max(s0, lo), min(s0 + nr, hi)
        if a < b:
            out.append((a, d0 + a - s0, b - a))
    return out


def _gather_zeros(names, srcs):
    return [jnp.zeros((LAYOUT[nm][1],) + s.shape[1:], s.dtype) for nm, s in zip(names, srcs)]


def _weights_allgather(names, srcs, name, carry=()):
    n = len(srcs)
    lay = [LAYOUT[nm] for nm in names]
    zeros = _gather_zeros(names, srcs)
    m = len(carry)

    def body(*refs):
        ins, outs = refs[:n], refs[2 * n + m:3 * n + m]
        send, recv, lsem = refs[3 * n + 2 * m:]
        x, y, c = _position()
        j = 2 * x + y
        sib = (x, y, 1 - c)
        chips = [(1 - x, y), (x, 1 - y), (1 - x, 1 - y)]

        def flow(a, k, jsrc, c0, to, from_src):
            cps = []
            for s0, d0, nr in _half_pieces(lay[a], jsrc, c0):
                dst = outs[a].at[pl.ds(d0, nr)]
                src = ins[a].at[pl.ds(s0, nr)] if from_src else dst
                cps.append(pltpu.make_async_remote_copy(src_ref=src, dst_ref=dst, send_sem=send.at[7 * a + k],
                                                        recv_sem=recv.at[7 * a + k], device_id=to, device_id_type=MESH))
            return cps

        def sized(a, k, rows):
            ref = ins[a].at[pl.ds(0, rows)]
            return pltpu.make_async_remote_copy(src_ref=ref, dst_ref=ref, send_sem=send.at[7 * a + k],
                                                recv_sem=recv.at[7 * a + k], device_id=sib, device_id_type=MESH)

        for j0 in range(N_CHIPS):
            for c0 in range(2):
                @pl.when((j == j0) & (c == c0))
                def _(j0=j0, c0=c0):
                    mine = [_half_rows(lay[a][0], c0) for a in range(n)]
                    theirs = [_half_rows(lay[a][0], 1 - c0) for a in range(n)]
                    for a in range(n):
                        for s0, d0, nr in _half_pieces(lay[a], j0, c0):
                            pltpu.make_async_copy(ins[a].at[pl.ds(s0, nr)], outs[a].at[pl.ds(d0, nr)], lsem.at[a]).start()
                    for a in range(n):
                        for cp in flow(a, 0, j0, c0, sib, True):
                            cp.start()
                        for k, chip in enumerate(chips):
                            for cp in flow(a, 1 + k, j0, c0, (*chip, c), True):
                                cp.start()
                    for k in range(3):
                        for a in range(n):
                            if mine[a]:
                                sized(a, 1 + k, mine[a]).wait_recv()
                                for cp in flow(a, 4 + k, j0 ^ _REL[k], c0, sib, False):
                                    cp.start()
                    for a in range(n):
                        if theirs[a]:
                            sized(a, 0, theirs[a]).wait_recv()
                            for k in range(3):
                                sized(a, 4 + k, theirs[a]).wait_recv()
                    for a in range(n):
                        if mine[a]:
                            for k in range(7):
                                sized(a, k, mine[a]).wait_send()
                            ref = ins[a].at[pl.ds(0, mine[a])]
                            pltpu.make_async_copy(ref, ref, lsem.at[a]).wait()

    res = pl.pallas_call(
        body, name=name,
        out_shape=[jax.ShapeDtypeStruct(z.shape, z.dtype) for z in list(zeros) + list(carry)],
        in_specs=[_HBM] * (2 * n + m), out_specs=[_HBM] * (n + m),
        input_output_aliases={n + a: a for a in range(n + m)},
        scratch_shapes=[pltpu.SemaphoreType.DMA((7 * n,)), pltpu.SemaphoreType.DMA((7 * n,)),
                        pltpu.SemaphoreType.DMA((n,))],
    )(*srcs, *zeros, *carry)
    return res[:n], res[n:]


_SEM = pl.BlockSpec(memory_space=pltpu.SEMAPHORE)
_VMEM_TOKEN = pl.BlockSpec(memory_space=pltpu.VMEM)
_TOKEN = jax.ShapeDtypeStruct((8, 128), F32)
_EFFECT = pltpu.SideEffectType.DATAFLOW_SIDE_EFFECTING


def _gather_start(names, srcs, zeros, name, after=None):
    n = len(srcs)
    lay = [LAYOUT[nm] for nm in names]
    extra = [] if after is None else [after]

    def body(*refs):
        ins, lands = refs[:n], refs[n:2 * n]
        send, recv, lsem = refs[2 * n + len(extra):2 * n + len(extra) + 3]
        refs[-1][...] = jnp.zeros_like(refs[-1])
        x, y, c = _position()
        j = 2 * x + y
        chips = [(1 - x, y), (x, 1 - y), (1 - x, 1 - y)]
        for j0 in range(N_CHIPS):
            @pl.when(j == j0)
            def _(j0=j0):
                for a in range(n):
                    for s0, d0, nr in lay[a][2](j0):
                        src, dst = ins[a].at[pl.ds(s0, nr)], lands[a].at[pl.ds(d0, nr)]
                        pltpu.make_async_copy(src, dst, lsem.at[a]).start()
                        for k, chip in enumerate(chips):
                            pltpu.make_async_remote_copy(src_ref=src, dst_ref=dst, send_sem=send.at[3 * a + k],
                                                         recv_sem=recv.at[3 * a + k], device_id=(*chip, c),
                                                         device_id_type=MESH).start()

    sems = [pltpu.SemaphoreType.DMA((3 * n,)), pltpu.SemaphoreType.DMA((3 * n,)), pltpu.SemaphoreType.DMA((n,))]
    hbm = lambda a: pltpu.HBM(a.shape, a.dtype)
    res = pl.pallas_call(
        body, name=name,
        out_shape=sems + [hbm(s) for s in srcs] + [hbm(z) for z in zeros] + [_TOKEN],
        in_specs=[_HBM] * (2 * n) + [pl.BlockSpec(memory_space=pl.ANY)] * len(extra),
        out_specs=[_SEM] * 3 + [_HBM] * (2 * n) + [_VMEM_TOKEN],
        input_output_aliases={a: 3 + a for a in range(2 * n)},
        compiler_params=pltpu.CompilerParams(has_side_effects=_EFFECT),
    )(*[pltpu.with_memory_space_constraint(s, pltpu.HBM) for s in srcs],
      *[pltpu.with_memory_space_constraint(z, pltpu.HBM) for z in zeros], *extra)
    return res[:3], res[3:3 + n], res[3 + n:3 + 2 * n], res[-1]


def _gather_wait(names, sems, srcs, lands, after, name):
    n = len(srcs)
    lay = [LAYOUT[nm] for nm in names]

    def body(*refs):
        ins, zones = refs[:n], refs[n:2 * n]
        send, recv, lsem = refs[2 * n:2 * n + 3]
        x, y, c = _position()
        for a in range(n):
            whole = zones[a].at[pl.ds(0, lay[a][0])]
            for k in range(3):
                cp = pltpu.make_async_remote_copy(src_ref=ins[a], dst_ref=whole, send_sem=send.at[3 * a + k],
                                                  recv_sem=recv.at[3 * a + k], device_id=(x, y, 1 - c),
                                                  device_id_type=MESH)
                cp.wait_send()
                cp.wait_recv()
            pltpu.make_async_copy(ins[a], whole, lsem.at[a]).wait()

    hbm = lambda a: pltpu.HBM(a.shape, a.dtype)
    res = pl.pallas_call(
        body, name=name,
        out_shape=[hbm(s) for s in srcs] + [hbm(z) for z in lands],
        in_specs=[_HBM] * (2 * n) + [_SEM] * 3 + [pl.BlockSpec(memory_space=pl.ANY)], out_specs=[_HBM] * (2 * n),
        input_output_aliases={a: a for a in range(2 * n)},
        compiler_params=pltpu.CompilerParams(has_side_effects=_EFFECT),
    )(*srcs, *lands, *sems, after)
    return res[n:]


def _clip_pieces(lay_a, jsrc, c0):
    h = lay_a[1] // 2
    lo, hi = c0 * h, (c0 + 1) * h
    out = []
    for s0, d0, nr in lay_a[2](jsrc):
        a, b = max(d0, lo), min(d0 + nr, hi)
        if a < b:
            out.append((s0 + a - d0, a, b - a))
    return out


def _rows_of(pieces):
    return sum(nr for _, _, nr in pieces)


def _both_cores(body_for):
    x, y, c = _position()
    j = 2 * x + y
    for j0 in range(N_CHIPS):
        for c0 in range(2):
            @pl.when((j == j0) & (c == c0))
            def _(j0=j0, c0=c0):
                body_for(j0, c0)


STAGE_ROWS = 512


def _staged_copy(src, dst, buf, sem_in, sem_out, rows):
    ch = buf.shape[0]
    for r in range(0, rows, ch):
        nr = min(ch, rows - r)
        stage = buf.at[pl.ds(0, nr)]
        cin = pltpu.make_async_copy(src.at[pl.ds(r, nr)], stage, sem_in)
        cin.start()
        cin.wait()
        cout = pltpu.make_async_copy(stage, dst.at[pl.ds(r, nr)], sem_out)
        cout.start()
        cout.wait()


def _half_to_sibling(names, gl, name):
    n = len(gl)
    halves = [LAYOUT[nm][1] // 2 for nm in names]

    def body(*refs):
        ins, outs = refs[:n], refs[n:2 * n]
        send, recv = refs[2 * n:]
        x, y, c = _position()

        def run(j0, c0):
            cps = [pltpu.make_async_remote_copy(src_ref=ins[a].at[pl.ds((1 - c0) * halves[a], halves[a])], dst_ref=outs[a],
                                                send_sem=send.at[a], recv_sem=recv.at[a], device_id=(x, y, 1 - c),
                                                device_id_type=MESH) for a in range(n)]
            for cp in cps:
                cp.start()
            for cp in cps:
                cp.wait()

        _both_cores(run)

    return pl.pallas_call(
        body, name=name,
        out_shape=[jax.ShapeDtypeStruct((halves[a],) + gl[a].shape[1:], gl[a].dtype) for a in range(n)],
        in_specs=[_HBM] * n, out_specs=[_HBM] * n,
        scratch_shapes=[pltpu.SemaphoreType.DMA((n,)), pltpu.SemaphoreType.DMA((n,))],
    )(*gl)


def _chip_scatter_half(names, parts, name):
    n = len(parts)
    lay = [LAYOUT[nm] for nm in names]
    zeros = [jnp.zeros((N_CHIPS, lay[a][0]) + parts[a].shape[1:], parts[a].dtype) for a in range(n)]

    def body(*refs):
        ins, outs = refs[:n], refs[2 * n:3 * n]
        send, recv = refs[3 * n:3 * n + 2]
        stage, sem_in, sem_out = refs[3 * n + 2:4 * n + 2], refs[4 * n + 2], refs[4 * n + 3]
        x, y, c = _position()
        chips = [(1 - x, y), (x, 1 - y), (1 - x, 1 - y)]

        def run(j0, c0):
            def sized(a, rows):
                return outs[a].at[0, pl.ds(0, rows)]

            for a in range(n):
                base = c0 * (lay[a][1] // 2)
                for k, chip in enumerate(chips):
                    for s0, d0, nr in _clip_pieces(lay[a], j0 ^ _REL[k], c0):
                        pltpu.make_async_remote_copy(
                            src_ref=ins[a].at[pl.ds(d0 - base, nr)], dst_ref=outs[a].at[j0, pl.ds(s0, nr)],
                            send_sem=send.at[3 * a + k], recv_sem=recv.at[3 * a + k],
                            device_id=(*chip, c), device_id_type=MESH).start()
            for a in range(n):
                base = c0 * (lay[a][1] // 2)
                for s0, d0, nr in _clip_pieces(lay[a], j0, c0):
                    _staged_copy(ins[a].at[pl.ds(d0 - base, nr)], outs[a].at[j0, pl.ds(s0, nr)], stage[a],
                                 sem_in.at[a], sem_out.at[a], nr)
            for a in range(n):
                got = _rows_of(_clip_pieces(lay[a], j0, c0))
                for k in range(3):
                    sent = _rows_of(_clip_pieces(lay[a], j0 ^ _REL[k], c0))
                    if sent:
                        pltpu.make_async_remote_copy(src_ref=sized(a, sent), dst_ref=sized(a, sent),
                                                     send_sem=send.at[3 * a + k], recv_sem=recv.at[3 * a + k],
                                                     device_id=(x, y, c), device_id_type=MESH).wait_send()
                    if got:
                        pltpu.make_async_remote_copy(src_ref=sized(a, got), dst_ref=sized(a, got),
                                                     send_sem=send.at[3 * a + k], recv_sem=recv.at[3 * a + k],
                                                     device_id=(x, y, c), device_id_type=MESH).wait_recv()

        _both_cores(run)

    return pl.pallas_call(
        body, name=name,
        out_shape=[jax.ShapeDtypeStruct(z.shape, z.dtype) for z in zeros],
        in_specs=[_HBM] * (2 * n), out_specs=[_HBM] * n, input_output_aliases={n + a: a for a in range(n)},
        scratch_shapes=[pltpu.SemaphoreType.DMA((3 * n,)), pltpu.SemaphoreType.DMA((3 * n,))]
        + [pltpu.VMEM((min(STAGE_ROWS, p.shape[0]),) + p.shape[1:], p.dtype) for p in parts]
        + [pltpu.SemaphoreType.DMA((n,)), pltpu.SemaphoreType.DMA((n,))],
    )(*parts, *zeros)


def _subset_exchange(names, bufs, l, name):
    n = len(bufs)
    lay = [LAYOUT[nm] for nm in names]

    def body(*refs):
        outs = refs[n:2 * n]
        send, recv = refs[2 * n:]
        x, y, c = _position()

        def run(j0, c0):
            for a in range(n):
                for s0, _, nr in _clip_pieces(lay[a], j0, c0):
                    rows = outs[a].at[l, pl.ds(s0, nr)]
                    pltpu.make_async_remote_copy(src_ref=rows, dst_ref=rows, send_sem=send.at[a], recv_sem=recv.at[a],
                                                 device_id=(x, y, 1 - c), device_id_type=MESH).start()
            for a in range(n):
                for c_half, wait_send in ((c0, True), (1 - c0, False)):
                    rows = _rows_of(_clip_pieces(lay[a], j0, c_half))
                    if rows:
                        ref = outs[a].at[l, pl.ds(0, rows)]
                        cp = pltpu.make_async_remote_copy(src_ref=ref, dst_ref=ref, send_sem=send.at[a], recv_sem=recv.at[a],
                                                          device_id=(x, y, 1 - c), device_id_type=MESH)
                        if wait_send:
                            cp.wait_send()
                        else:
                            cp.wait_recv()

        _both_cores(run)

    return pl.pallas_call(
        body, name=name,
        out_shape=[jax.ShapeDtypeStruct(b.shape, b.dtype) for b in bufs],
        in_specs=[_HBM] * n, out_specs=[_HBM] * n, input_output_aliases={a: a for a in range(n)},
        scratch_shapes=[pltpu.SemaphoreType.DMA((n,)), pltpu.SemaphoreType.DMA((n,))],
    )(*bufs)


def _scatter_start(names, gl, name):
    n = len(gl)
    lay = [LAYOUT[nm] for nm in names]
    zones = [lax.empty((N_CHIPS, lay[a][0]) + gl[a].shape[1:], gl[a].dtype) for a in range(n)]

    def body(*refs):
        ins, lands = refs[:n], refs[n:2 * n]
        send, recv, lsem = refs[2 * n:2 * n + 3]
        refs[-1][...] = jnp.zeros_like(refs[-1])
        x, y, c = _position()
        j = 2 * x + y
        chips = [(1 - x, y), (x, 1 - y), (1 - x, 1 - y)]
        for j0 in range(N_CHIPS):
            @pl.when(j == j0)
            def _(j0=j0):
                for a in range(n):
                    for s0, d0, nr in lay[a][2](j0):
                        pltpu.make_async_copy(ins[a].at[pl.ds(d0, nr)], lands[a].at[j0, pl.ds(s0, nr)], lsem.at[a]).start()
                    for k, chip in enumerate(chips):
                        for s0, d0, nr in lay[a][2](j0 ^ _REL[k]):
                            pltpu.make_async_remote_copy(
                                src_ref=ins[a].at[pl.ds(d0, nr)], dst_ref=lands[a].at[j0, pl.ds(s0, nr)],
                                send_sem=send.at[3 * a + k], recv_sem=recv.at[3 * a + k],
                                device_id=(*chip, c), device_id_type=MESH).start()

    sems = [pltpu.SemaphoreType.DMA((3 * n,)), pltpu.SemaphoreType.DMA((3 * n,)), pltpu.SemaphoreType.DMA((n,))]
    hbm = lambda a: pltpu.HBM(a.shape, a.dtype)
    res = pl.pallas_call(
        body, name=name,
        out_shape=sems + [hbm(g) for g in gl] + [hbm(z) for z in zones] + [_TOKEN],
        in_specs=[_HBM] * (2 * n), out_specs=[_SEM] * 3 + [_HBM] * (2 * n) + [_VMEM_TOKEN],
        input_output_aliases={a: 3 + a for a in range(2 * n)},
        compiler_params=pltpu.CompilerParams(has_side_effects=_EFFECT),
    )(*[pltpu.with_memory_space_constraint(g, pltpu.HBM) for g in gl],
      *[pltpu.with_memory_space_constraint(z, pltpu.HBM) for z in zones])
    return res[:3], res[3:3 + n], res[3 + n:3 + 2 * n], res[-1]


def _scatter_wait(names, sems, srcs, lands, after, name):
    n = len(srcs)
    lay = [LAYOUT[nm] for nm in names]

    def body(*refs):
        zones = refs[n:2 * n]
        send, recv, lsem = refs[2 * n:2 * n + 3]
        x, y, c = _position()
        for a in range(n):
            whole = zones[a].at[0, pl.ds(0, lay[a][0])]
            for k in range(3):
                cp = pltpu.make_async_remote_copy(src_ref=whole, dst_ref=whole, send_sem=send.at[3 * a + k],
                                                  recv_sem=recv.at[3 * a + k], device_id=(x, y, 1 - c),
                                                  device_id_type=MESH)
                cp.wait_send()
                cp.wait_recv()
            pltpu.make_async_copy(whole, whole, lsem.at[a]).wait()

    hbm = lambda a: pltpu.HBM(a.shape, a.dtype)
    res = pl.pallas_call(
        body, name=name,
        out_shape=[hbm(s) for s in srcs] + [hbm(z) for z in lands],
        in_specs=[_HBM] * (2 * n) + [_SEM] * 3 + [pl.BlockSpec(memory_space=pl.ANY)], out_specs=[_HBM] * (2 * n),
        input_output_aliases={a: a for a in range(2 * n)},
        compiler_params=pltpu.CompilerParams(has_side_effects=_EFFECT),
    )(*srcs, *lands, *sems, after)
    return res[n:]


def _sibling_swap(arrs, name):
    n = len(arrs)

    def body(*refs):
        ins, outs = refs[:n], refs[n:2 * n]
        send, recv = refs[2 * n:]
        x, y, c = _position()
        cps = [pltpu.make_async_remote_copy(src_ref=ins[a], dst_ref=outs[a], send_sem=send.at[a], recv_sem=recv.at[a],
                                            device_id=(x, y, 1 - c), device_id_type=MESH) for a in range(n)]
        for cp in cps:
            cp.start()
        for cp in cps:
            cp.wait()

    return pl.pallas_call(
        body, name=name,
        out_shape=[jax.ShapeDtypeStruct(a.shape, a.dtype) for a in arrs],
        in_specs=[_HBM] * n, out_specs=[_HBM] * n,
        scratch_shapes=[pltpu.SemaphoreType.DMA((n,)), pltpu.SemaphoreType.DMA((n,))],
    )(*arrs)


def _row_tile(r):
    for t in (256, 128, 64, 32, 16, 8):
        if r % t == 0 and r > t:
            return t
    return r


def _pair_add_half(g, rb, c_arr, name):
    hrows, rest = rb.shape[0], rb.shape[1:]
    tr = _row_tile(hrows)
    nb = hrows // tr
    z = (0,) * len(rest)

    def body(c_ref, g_ref, r_ref, o_ref):
        o_ref[...] = (g_ref[...].astype(F32) + r_ref[...].astype(F32)).astype(o_ref.dtype)

    return pl.pallas_call(
        body, name=name,
        grid_spec=pltpu.PrefetchScalarGridSpec(
            num_scalar_prefetch=1, grid=(nb,),
            in_specs=[pl.BlockSpec((tr,) + rest, lambda i, c_ref: (c_ref[0] * nb + i,) + z),
                      pl.BlockSpec((tr,) + rest, lambda i, c_ref: (i,) + z)],
            out_specs=pl.BlockSpec((tr,) + rest, lambda i, c_ref: (i,) + z)),
        out_shape=jax.ShapeDtypeStruct((hrows,) + rest, BF16),
        compiler_params=pltpu.CompilerParams(dimension_semantics=("parallel",), vmem_limit_bytes=VMEM_LIMIT),
    )(c_arr, g, rb)


def _sum_slabs(slabs, l, buf, name):
    m = len(slabs)
    n, R, rest = slabs[0].shape[0], slabs[0].shape[1], slabs[0].shape[2:]
    tr = _row_tile(R)
    z = (0,) * len(rest)

    def body(*refs):
        total = None
        for r_ref in refs[:m]:
            acc = r_ref[0].astype(F32)
            for k in range(1, n):
                acc = acc + r_ref[k].astype(F32)
            total = acc if total is None else total + acc
        refs[-1][...] = total

    if R // tr > 64 and len(rest) == 1 and rest[0] % 256 == 0:
        grid = (rest[0] // 256,)
        in_spec = pl.BlockSpec((n, R, 256), lambda i: (0, 0, i))
        out_spec = pl.BlockSpec((None, R, 256), lambda i: (l, 0, i))
    else:
        grid = (R // tr,)
        in_spec = pl.BlockSpec((n, tr) + rest, lambda i: (0, i) + z)
        out_spec = pl.BlockSpec((None, tr) + rest, lambda i: (l, i) + z)
    in_specs, args, aliases = [in_spec] * m, list(slabs), {}
    if buf is not None:
        in_specs.append(pl.BlockSpec(memory_space=pl.ANY))
        args.append(buf)
        aliases = {m: 0}
    return pl.pallas_call(
        body, name=name, grid=grid, in_specs=in_specs, out_specs=out_spec,
        out_shape=jax.ShapeDtypeStruct((DEPTH, R) + rest, F32), input_output_aliases=aliases,
        compiler_params=pltpu.CompilerParams(dimension_semantics=("parallel",), vmem_limit_bytes=VMEM_LIMIT),
    )(*args)


def _adam_math(w, g, m, v):
    mn = ADAM_B1 * m + (1.0 - ADAM_B1) * g
    vn = ADAM_B2 * v + (1.0 - ADAM_B2) * (g * g)
    m_hat = mn / (1.0 - ADAM_B1 ** ADAM_STEP)
    v_hat = vn / (1.0 - ADAM_B2 ** ADAM_STEP)
    return -ADAM_LR * (m_hat / (jnp.sqrt(v_hat) + ADAM_EPS) + ADAM_WD * w), mn, vn


def _adamw(w, g, m, v, name):
    L, R, C = w.shape
    tr = _row_tile(R)

    def body(w_ref, g_ref, m_ref, v_ref, d_ref, mo_ref, vo_ref):
        d_ref[...], mo_ref[...], vo_ref[...] = _adam_math(w_ref[...], g_ref[...], m_ref[...], v_ref[...])

    if R // tr > 64 and C % 128 == 0:
        spec, grid = pl.BlockSpec((None, R, 128), lambda l, i: (l, 0, i)), (L, C // 128)
    else:
        spec, grid = pl.BlockSpec((None, tr, C), lambda l, i: (l, i, 0)), (L, R // tr)
    return pl.pallas_call(
        body, name=name, grid=grid, in_specs=[spec] * 4, out_specs=[spec] * 3,
        out_shape=[jax.ShapeDtypeStruct((L, R, C), F32)] * 3,
        compiler_params=pltpu.CompilerParams(dimension_semantics=("parallel", "parallel"), vmem_limit_bytes=VMEM_LIMIT),
    )(w, g, m, v)


_VMEM_WHOLE = pl.BlockSpec(memory_space=pltpu.VMEM)


def _matrix_update(gath, w, m, v, name):
    K = w.shape[1]

    def body(g0_ref, g1_ref, w_ref, m_ref, v_ref, go_ref, d_ref, mo_ref, vo_ref):
        for l, gr in enumerate((g0_ref, g1_ref)):
            for k in range(K):
                g = gr[0, k].astype(F32)
                for dev in range(1, 8):
                    g = g + gr[dev, k].astype(F32)
                go_ref[l, k] = g
                d_ref[l, k], mo_ref[l, k], vo_ref[l, k] = _adam_math(w_ref[l, k], g, m_ref[l, k], v_ref[l, k])

    return pl.pallas_call(
        body, name=name, in_specs=[_VMEM_WHOLE] * 5, out_specs=[_VMEM_WHOLE] * 4,
        out_shape=[jax.ShapeDtypeStruct(w.shape, F32)] * 4,
        compiler_params=pltpu.CompilerParams(vmem_limit_bytes=VMEM_LIMIT),
    )(gath[0], gath[1], w, m, v)


VECS = (("pre_norm_g", D), ("post_norm_g", D), ("gm_ln_g", GM_W), ("gm_ln_b", GM_W), ("mla_q_norm_g", QR),
        ("mla_kv_norm_g", KVR), ("lru_conv_b", LRU_W), ("lru_b_a", LRU_W), ("lru_b_x", LRU_W), ("lru_lambda", LRU_W))
VEC_KEY = {"pre_norm_g": "pre_g", "post_norm_g": "post_g", "gm_ln_g": "ln_g", "gm_ln_b": "ln_b", "mla_q_norm_g": "qg",
           "mla_kv_norm_g": "kvg", "lru_conv_b": "conv_b", "lru_b_a": "ba", "lru_b_x": "bx", "lru_lambda": "lam"}
VEC_ROWS, VEC_W, VEC_ROW0, LOSS_ROW = 16, LRU_W, GM_G, 14


def _pack_rows(LG, loss_part):
    per = len(VECS) + 1
    ins = []
    for G in LG:
        ins += [G[VEC_KEY[n]] for n, _ in VECS] + [G["bst"]]
    ins.append(loss_part)

    def body(*refs):
        o_ref = refs[-1]
        o_ref[...] = jnp.zeros_like(o_ref)
        for l in range(DEPTH):
            base = VEC_ROWS * l
            o_ref[pl.ds(base, 8), pl.ds(0, GM_B)] = refs[per * l + len(VECS)][...].T[:8, :]
            for t, (_, width) in enumerate(VECS):
                o_ref[pl.ds(base + VEC_ROW0 + t, 1), pl.ds(0, width)] = refs[per * l + t][...]
        o_ref[pl.ds(LOSS_ROW, 1), pl.ds(0, 128)] = jnp.broadcast_to(refs[-2][...], (1, 128))

    return pl.pallas_call(
        body, name="pack_rows", in_specs=[_VMEM_WHOLE] * len(ins), out_specs=_VMEM_WHOLE,
        out_shape=jax.ShapeDtypeStruct((DEPTH * VEC_ROWS, VEC_W), F32),
    )(*ins)


def _vector_update(gath, W, M, V):
    names = [n for n, _ in VECS] + ["gm_bs"]
    nw = len(names)

    def body(*refs):
        g_ref = refs[0]
        wr, mr, vr = refs[1:1 + nw], refs[1 + nw:1 + 2 * nw], refs[1 + 2 * nw:1 + 3 * nw]
        outs = refs[1 + 3 * nw:]
        s = g_ref[0]
        for dev in range(1, 8):
            s = s + g_ref[dev]
        for t, (_, width) in enumerate(VECS):
            for l in range(DEPTH):
                r = VEC_ROWS * l + VEC_ROW0 + t
                g = s[r:r + 1, :width]
                row = (pl.ds(l, 1), slice(None))
                res = (g,) + _adam_math(wr[t][row], g, mr[t][row], vr[t][row])
                for q in range(4):
                    outs[4 * t + q][row] = res[q]
        t = len(VECS)
        for l in range(DEPTH):
            for k in range(GM_G):
                g = s[VEC_ROWS * l + k:VEC_ROWS * l + k + 1, :GM_B]
                row = (l, pl.ds(k, 1), slice(None))
                res = (g,) + _adam_math(wr[t][row], g, mr[t][row], vr[t][row])
                for q in range(4):
                    outs[4 * t + q][row] = res[q]
        outs[4 * nw][...] = s[LOSS_ROW:LOSS_ROW + 1, :128]

    ws = [W[n] for n in names]
    out_shape = []
    for w in ws:
        out_shape += [jax.ShapeDtypeStruct(w.shape, F32)] * 4
    out_shape.append(jax.ShapeDtypeStruct((1, 128), F32))
    res = pl.pallas_call(
        body, name="vector_update", in_specs=[_VMEM_WHOLE] * (1 + 3 * nw), out_specs=[_VMEM_WHOLE] * (4 * nw + 1),
        out_shape=out_shape, compiler_params=pltpu.CompilerParams(vmem_limit_bytes=VMEM_LIMIT),
    )(gath, *ws, *[M[n] for n in names], *[V[n] for n in names])
    return {n: tuple(res[4 * t:4 * t + 4]) for t, n in enumerate(names)}, res[4 * nw]


SHARDED = ("w_in", "mla_w_uq", "mla_w_ukv", "lru_conv_w", "w_proj_a", "w_proj_b", "w_proj_c", "w_out")
FIRST = ("w_in", "lru_conv_w")
LATER = tuple(n for n in SHARDED if n not in FIRST)
COL_SHARDED = ("w_in", "mla_w_uq", "mla_w_ukv", "lru_conv_w")
SMALL = ("pre_norm_g", "gm_ln_g", "gm_ln_b", "gm_ws", "gm_bs", "mla_q_norm_g", "mla_kv_norm_g", "lru_conv_b",
         "lru_w_a", "lru_b_a", "lru_w_x", "lru_b_x", "lru_lambda", "post_norm_g")
WEIGHTS = ("pre_norm_g", "w_in", "gm_ln_g", "gm_ln_b", "gm_ws", "gm_bs", "mla_q_norm_g", "mla_w_uq",
           "mla_kv_norm_g", "mla_w_ukv", "lru_conv_w", "lru_conv_b", "lru_w_a", "lru_b_a", "lru_w_x", "lru_b_x",
           "lru_lambda", "w_proj_a", "w_proj_b", "w_proj_c", "w_out", "post_norm_g")


GB_KEY = {"w_in": "wp", "mla_w_uq": "wuq", "mla_w_ukv": "wukv", "w_proj_a": "wpa", "w_proj_b": "wpb",
          "w_proj_c": "wpc", "w_out": "wout"}


def _prepare(l, gathered, small, wsb):
    P = {GB_KEY[n]: gathered[n] for n in GB_KEY if n in gathered}
    P["conv_w"] = gathered["lru_conv_w"].transpose(1, 0, 2).reshape(CONV_W, LRU_W)
    P["wsb"] = wsb
    row = lambda n: small[n][l][None, :]
    P["pre_g"], P["post_g"] = row("pre_norm_g"), row("post_norm_g")
    P["ln_g"], P["ln_b"] = row("gm_ln_g"), row("gm_ln_b")
    P["ws"] = small["gm_ws"][l]
    P["bst"] = jnp.pad(small["gm_bs"][l].T, ((0, 0), (0, 128 - GM_G)))
    P["qg"], P["kvg"] = row("mla_q_norm_g"), row("mla_kv_norm_g")
    P["conv_b"], P["ba"], P["bx"], P["lam"] = row("lru_conv_b"), row("lru_b_a"), row("lru_b_x"), row("lru_lambda")
    return P


def kernel(x, pre_norm_g, w_in, gm_ln_g, gm_ln_b, gm_ws, gm_bs, mla_q_norm_g, mla_w_uq, mla_kv_norm_g, mla_w_ukv, lru_conv_w, lru_conv_b, lru_w_a, lru_b_a, lru_w_x, lru_b_x, lru_lambda, w_proj_a, w_proj_b, w_proj_c, w_out, post_norm_g, loss_target, m_pre_norm_g, m_w_in, m_gm_ln_g, m_gm_ln_b, m_gm_ws, m_gm_bs, m_mla_q_norm_g, m_mla_w_uq, m_mla_kv_norm_g, m_mla_w_ukv, m_lru_conv_w, m_lru_conv_b, m_lru_w_a, m_lru_b_a, m_lru_w_x, m_lru_b_x, m_lru_lambda, m_w_proj_a, m_w_proj_b, m_w_proj_c, m_w_out, m_post_norm_g, v_pre_norm_g, v_w_in, v_gm_ln_g, v_gm_ln_b, v_gm_ws, v_gm_bs, v_mla_q_norm_g, v_mla_w_uq, v_mla_kv_norm_g, v_mla_w_ukv, v_lru_conv_w, v_lru_conv_b, v_lru_w_a, v_lru_b_a, v_lru_w_x, v_lru_b_x, v_lru_lambda, v_w_proj_a, v_w_proj_b, v_w_proj_c, v_w_out, v_post_norm_g):
    args = dict(locals())
    W = {n: args[n] for n in WEIGHTS}
    M = {n: args["m_" + n] for n in WEIGHTS}
    V = {n: args["v_" + n] for n in WEIGHTS}
    c = lax.axis_index("c")

    def shards(l, names):
        out = []
        for n in names:
            blk = W[n][l].T if n in TRANSPOSED else W[n][l]
            out.append(blk[None] if n == "lru_conv_w" else blk.astype(BF16))
        return out

    small = {n: W[n] for n in SMALL}
    wsb = _superblocks(W["lru_w_a"], W["lru_w_x"])
    tabs = _rope_tables()
    s0a, s0b, s1 = shards(0, FIRST), shards(0, LATER), shards(1, SHARDED)
    g0, zones = _weights_allgather(FIRST, s0a, "weights_allgather_l0",
                                   carry=_gather_zeros(LATER, s0b) + _gather_zeros(SHARDED, s1))
    sems0, srcs0, lands0, token0 = _gather_start(LATER, s0b, zones[:len(LATER)], "weights_gather_start_l0")
    sems, srcs1, lands1, token = _gather_start(SHARDED, s1, zones[len(LATER):], "weights_gather_start_l1", after=token0)

    def late0(after):
        got = _gather_wait(LATER, sems0, srcs0, lands0, after, "weights_gather_wait_l0")
        return {GB_KEY[n]: g for n, g in zip(LATER, got)}

    P = [_prepare(0, dict(zip(FIRST, g0)), small, wsb), None]
    h0 = x[0]
    h1, A0 = _layer_fwd(h0, P[0], 0, tabs, token, late0)
    g1 = dict(zip(SHARDED, _gather_wait(SHARDED, sems, srcs1, lands1, h1, "weights_gather_wait_l1")))
    P[1] = _prepare(1, g1, small, wsb)
    h2, A1 = _layer_fwd(h1, P[1], 1, tabs)
    dy, loss_part = _loss_fwd(h2, loss_target[0])
    def large_grads(G, GB):
        conv = G["conv_w"].reshape(CONV_W, N_CHIPS, LRU_W // N_CHIPS).transpose(1, 0, 2)
        return [conv if n == "lru_conv_w" else GB[GB_KEY[n]] for n in SHARDED]

    d1, G1, GB1 = _layer_bwd(dy, A1, P[1], 1, tabs)
    sems, srcs1, lands1, token = _scatter_start(SHARDED, large_grads(G1, GB1), "grads_scatter_start_l1")
    d0, G0, GB0 = _layer_bwd(d1, A0, P[0], 0, tabs, token)
    LG = (G0, G1)
    mine1 = _scatter_wait(SHARDED, sems, srcs1, lands1, d0, "grads_scatter_wait_l1")
    theirs1 = _sibling_swap(mine1, "partials_to_sibling_l1")
    both = [_sum_slabs([a, b], 1, None, "sum_partials_l1_" + n) for n, a, b in zip(SHARDED, mine1, theirs1)]
    g0l = large_grads(G0, GB0)
    c_arr = jnp.reshape(c, (1,)).astype(jnp.int32)
    from_sib = _half_to_sibling(SHARDED, g0l, "grads_half_to_sibling_l0")
    pair = [_pair_add_half(g, rb, c_arr, "pair_add_" + n) for n, g, rb in zip(SHARDED, g0l, from_sib)]
    slabs = _chip_scatter_half(SHARDED, pair, "grads_chip_scatter_l0")
    both = [_sum_slabs([s], 0, b, "sum_slabs_l0_" + n) for n, s, b in zip(SHARDED, slabs, both)]
    both = _subset_exchange(SHARDED, both, 0, "reduced_rows_to_sibling_l0")
    grads = {}
    for n, b in zip(SHARDED, both):
        if n in TRANSPOSED and n != "w_in":
            b = jnp.swapaxes(b, 1, 2)
        grads[n] = b if n == "w_in" else b.reshape(W[n].shape)

    rows = _pack_rows(LG, loss_part)
    mats = []
    for g in LG:
        mats += [g["ws"].astype(BF16), g["wab"][0, :, :, :LRU_BW], g["wab"][1, :, :, :LRU_BW]]
    gath = _allgather([rows] + mats, "small_grads_allgather")
    upd, loss_row = _vector_update(gath[0], W, M, V)
    loss = loss_row[0, 0]
    for k, n in enumerate(("gm_ws", "lru_w_a", "lru_w_x")):
        upd[n] = _matrix_update((gath[1 + k], gath[4 + k]), W[n], M[n], V[n], "update_" + n)

    for n in SHARDED:
        if n == "w_in":
            tr = lambda a: jnp.swapaxes(a, 1, 2)
            res = _adamw(tr(W[n]), grads[n], tr(M[n]), tr(V[n]), "adamw_" + n)
            upd[n] = tuple(tr(a) for a in (grads[n],) + tuple(res))
        else:
            upd[n] = (grads[n],) + tuple(_adamw(W[n], grads[n], M[n], V[n], "adamw_" + n))

    return (loss, d0[None], *[upd[n][0] for n in WEIGHTS], *[upd[n][1] for n in WEIGHTS],
            *[upd[n][2] for n in WEIGHTS], *[upd[n][3] for n in WEIGHTS])
```

```python
import functools
import math

import jax
import jax.numpy as jnp
from jax import lax
from jax.experimental import pallas as pl
from jax.experimental.pallas import tpu as pltpu

F32, BF16 = jnp.float32, jnp.bfloat16
MESH = pl.DeviceIdType.MESH

S, D, DEPTH = 2048, 1024, 2
CHUNK, EPS = 64, 1e-6
GM_W, GM_G, GM_B = 1024, 4, 128
H, NOPE, ROPE, VDIM = 8, 128, 64, 128
QR, KVR = 384, 256
MLA_W = H * VDIM
LRU_W, LRU_NB, LRU_BW, LRU_C, CONV_W = 1280, 16, 80, 8.0, 4
ROPE_THETA = 10000.0
IN_SIZES = (GM_W, GM_W, GM_W, QR, KVR, ROPE, MLA_W, LRU_W, LRU_W, D, D, D)
N_IN = sum(IN_SIZES)
N_CHIPS = 4
ADAM_LR, ADAM_B1, ADAM_B2, ADAM_EPS, ADAM_WD, ADAM_STEP = 0.001, 0.9, 0.999, 1e-08, 0.01, 10

HP = 256
O_U, O_V, O_ZA, O_GA, O_GB, O_GC = 0, 1024, 2048, 3072, 4096, 5120
O_CKV, O_KR, O_CQ, O_XC, O_ZC, O_ZB = 6144, 6400, 6528, 7680, 8960, 10240
NP = 11264
MIB = 1024 * 1024
VMEM_LIMIT = 16 * MIB


def _vmem(block_bytes, temp_bytes=0):
    return int(min(max(2 * block_bytes + temp_bytes + 4 * MIB, VMEM_LIMIT), 56 * MIB))


def _nbytes(shape, dtype):
    return math.prod(d for d in shape if d is not None) * jnp.dtype(dtype).itemsize


def _tile(dim, target):
    if dim <= target:
        return dim
    t = (target // 128) * 128
    while dim % t:
        t -= 128
    return t


def _sig(x):
    return jax.nn.sigmoid(x)


def _silu(x):
    return x * _sig(x)


def _dsilu(x):
    s = _sig(x)
    return s * (1.0 + x * (1.0 - s))


def _mm(a, b, mode, name, out_dtype=F32, tm=512, tn=512, tk=1024, b_lead=None, out_lead=None):
    b2 = b.shape[1:] if b_lead is not None else b.shape
    if mode == "nn":
        (M, K), (K2, N) = a.shape, b2
    elif mode == "nt":
        (M, K), (N, K2) = a.shape, b2
    else:
        (K, M), (K2, N) = a.shape, b2
    assert K == K2, (name, a.shape, b.shape)
    tm, tn, tk = _tile(M, tm), _tile(N, tn), _tile(K, tk)
    nk = K // tk
    if mode == "tn":
        a_spec = pl.BlockSpec((tk, tm), lambda i, j, k: (k, i))
        lhs_c = 0
    else:
        a_spec = pl.BlockSpec((tm, tk), lambda i, j, k: (i, k))
        lhs_c = 1
    b_blk, b_idx, rhs_c = ((tn, tk), (lambda i, j, k: (j, k)), 1) if mode == "nt" else ((tk, tn), (lambda i, j, k: (k, j)), 0)
    if b_lead is None:
        b_spec = pl.BlockSpec(b_blk, b_idx)
    else:
        b_spec = pl.BlockSpec((None,) + b_blk, functools.partial(lambda i, j, k, f, l: (l,) + f(i, j, k), f=b_idx, l=b_lead))
    dims = (((lhs_c,), (rhs_c,)), ((), ()))
    in_specs, args, aliases = [a_spec, b_spec], [a, b], {}
    if out_lead is None:
        out_spec = pl.BlockSpec((tm, tn), lambda i, j, k: (i, j))
        out_shape = jax.ShapeDtypeStruct((M, N), out_dtype)
    else:
        l_out, n_lead, buf = out_lead
        out_spec = pl.BlockSpec((None, tm, tn), functools.partial(lambda i, j, k, l: (l, i, j), l=l_out))
        out_shape = jax.ShapeDtypeStruct((n_lead, M, N), out_dtype)
        if buf is not None:
            in_specs.append(pl.BlockSpec(memory_space=pl.ANY))
            args.append(buf)
            aliases = {2: 0}

    def body(a_ref, b_ref, *rest):
        o_ref, acc_ref = rest[-2:]
        k = pl.program_id(2)

        @pl.when(k == 0)
        def _():
            acc_ref[...] = jnp.zeros_like(acc_ref)

        acc_ref[...] += lax.dot_general(a_ref[...].astype(BF16), b_ref[...].astype(BF16), dims,
                                        preferred_element_type=F32)

        @pl.when(k == nk - 1)
        def _():
            o_ref[...] = acc_ref[...].astype(o_ref.dtype)

    return pl.pallas_call(
        body, name=name, grid=(M // tm, N // tn, nk),
        in_specs=in_specs, out_specs=out_spec, out_shape=out_shape,
        scratch_shapes=[pltpu.VMEM((tm, tn), F32)], input_output_aliases=aliases,
        compiler_params=pltpu.CompilerParams(
            dimension_semantics=("parallel", "parallel", "arbitrary"),
            vmem_limit_bytes=_vmem(_nbytes((tm, tk), a.dtype) + _nbytes((tk, tn), b.dtype) + _nbytes((tm, tn), out_dtype),
                                   _nbytes((tm, tn), F32) + _nbytes((tm, tk), BF16) + _nbytes((tk, tn), BF16))),
    )(*args)


def _rows(fn, name, tm, rows, halos=(), fulls=(), outs=(), accs=()):
    n = S // tm
    in_specs, args = [], []
    for arr, w, cb in rows:
        in_specs.append(pl.BlockSpec((tm, w), functools.partial(lambda i, cb: (i, cb), cb=cb)))
        args.append(arr)
    for arr, w, cb, side in halos:
        if side == "prev":
            im = functools.partial(lambda i, cb: (jnp.maximum(i * (tm // 16) - 1, 0), cb), cb=cb)
        else:
            im = functools.partial(lambda i, cb: (jnp.minimum((i + 1) * (tm // 16), S // 16 - 1), cb), cb=cb)
        in_specs.append(pl.BlockSpec((16, w), im))
        args.append(arr)
    for arr in fulls:
        in_specs.append(pl.BlockSpec(arr.shape, functools.partial(lambda i, nd: (0,) * nd, nd=arr.ndim)))
        args.append(arr)
    out_shape, out_specs, aliases, n_alias = [], [], {}, 0
    for k, o in enumerate(outs):
        if len(o) == 3 and o[2] == "T":
            out_shape.append(jax.ShapeDtypeStruct((o[0], S), o[1]))
            out_specs.append(pl.BlockSpec((o[0], tm), lambda i: (0, i)))
        elif len(o) == 3:
            buf, total, cb = o[2]
            out_shape.append(jax.ShapeDtypeStruct((S, total), o[1]))
            out_specs.append(pl.BlockSpec((tm, o[0]), functools.partial(lambda i, cb: (i, cb), cb=cb)))
            if buf is not None:
                aliases[len(args)] = k
                in_specs.append(pl.BlockSpec(memory_space=pl.ANY))
                args.append(buf)
                n_alias += 1
        else:
            out_shape.append(jax.ShapeDtypeStruct((S, o[0]), o[1]))
            out_specs.append(pl.BlockSpec((tm, o[0]), lambda i: (i, 0)))
    for shp in accs:
        out_shape.append(jax.ShapeDtypeStruct(shp, F32))
        out_specs.append(pl.BlockSpec(shp, functools.partial(lambda i, nd: (0,) * nd, nd=len(shp))))
    nr, nh, nf, no, na = len(rows), len(halos), len(fulls), len(outs), len(accs)
    blocks = (sum(_nbytes((tm, w), arr.dtype) for arr, w, _ in rows) + sum(_nbytes(a.shape, a.dtype) for a in fulls)
              + sum(_nbytes((tm, o[0]), o[1]) for o in outs) + sum(_nbytes(shp, F32) for shp in accs))
    widest = _nbytes((tm, max([w for _, w, _ in rows] + [o[0] for o in outs])), F32)

    def body(*refs):
        i = pl.program_id(0)
        ins, orefs = refs[:nr + nh + nf], refs[nr + nh + nf + n_alias:]
        rv = [r[...].astype(F32) for r in ins[:nr]]
        hv = [r[...].astype(F32)[8:] if h[3] == "prev" else r[...].astype(F32)[:8] for r, h in zip(ins[nr:nr + nh], halos)]
        fv = [r[...] for r in ins[nr + nh:]]
        o, a = fn(i, rv, hv, fv)
        assert len(o) == no and len(a) == na, name
        for spec, ref, val in zip(outs, orefs[:no], o):
            ref[...] = (val.T if len(spec) == 3 and spec[2] == "T" else val).astype(ref.dtype)
        if na:
            @pl.when(i == 0)
            def _():
                for ref in orefs[no:]:
                    ref[...] = jnp.zeros_like(ref)

            for ref, val in zip(orefs[no:], a):
                ref[...] += val

    res = pl.pallas_call(
        body, name=name, grid=(n,), in_specs=in_specs, out_specs=out_specs, out_shape=out_shape,
        input_output_aliases=aliases,
        compiler_params=pltpu.CompilerParams(dimension_semantics=("arbitrary",), vmem_limit_bytes=_vmem(blocks, 6 * widest)),
    )(*args)
    return res


def _shift_down(xb, halo, s, row):
    fix = jnp.tile(pltpu.roll(halo, s, 0), (xb.shape[0] // 8, 1))
    return jnp.where(row >= s, pltpu.roll(xb, s, 0), fix)


def _shift_up(xb, halo, s, row):
    tm = xb.shape[0]
    fix = jnp.tile(pltpu.roll(halo, 8 - s, 0), (tm // 8, 1))
    return jnp.where(row < tm - s, pltpu.roll(xb, tm - s, 0), fix)


def _rms(x):
    return lax.rsqrt(jnp.mean(x * x, axis=-1, keepdims=True) + EPS)


def _rms_bwd(dy, x, g):
    r = _rms(x)
    xh = x * r
    dxh = dy * g
    dx = r * (dxh - xh * jnp.mean(dxh * xh, axis=-1, keepdims=True))
    return dx, dy * xh


def _colsum(x):
    return jnp.sum(x, axis=0, keepdims=True)


def _prenorm_fwd(x, g, token=None):
    def fn(i, rv, hv, fv):
        return [rv[0] * _rms(rv[0]) * fv[0]], []
    return _rows(fn, "prenorm_fwd", 256, [(x, D, 0)], fulls=[g] + ([] if token is None else [token]), outs=[(D, BF16)])[0]


def _gm_mask():
    r = lax.broadcasted_iota(jnp.int32, (GM_B, GM_B), 0) // CHUNK
    c = lax.broadcasted_iota(jnp.int32, (GM_B, GM_B), 1) // CHUNK
    return c <= r


def _gm_norm(v, g, b):
    mu = jnp.mean(v, axis=-1, keepdims=True)
    vc = v - mu
    rs = lax.rsqrt(jnp.mean(vc * vc, axis=-1, keepdims=True) + EPS)
    vh = vc * rs
    return vh, rs, vh * g + b


def _gm_sv(vn, ws, bst):
    mask = _gm_mask()
    gw = GM_W // GM_G
    parts = []
    for g in range(GM_G):
        wm = jnp.where(mask, ws[g], 0.0).astype(BF16)
        parts.append(jnp.dot(wm, vn[:, g * gw:(g + 1) * gw].astype(BF16), preferred_element_type=F32)
                     + bst[:, g:g + 1])
    return jnp.concatenate(parts, axis=1)


def _gmlp_fwd(proj, ln_g, ln_b, ws, bst):
    def fn(i, rv, hv, fv):
        u, v, z = rv
        g, b, w, bt = fv
        _, _, vn = _gm_norm(v, g, b)
        return [u * _gm_sv(vn, w, bt) * _silu(z)], []
    return _rows(fn, "gmlp_fwd", GM_B, [(proj, GM_W, 0), (proj, GM_W, 1), (proj, GM_W, 2)],
                 fulls=[ln_g, ln_b, ws, bst], outs=[(GM_W, BF16)])[0]


def _mla_prep_fwd(proj, qg, kvg):
    def fn(i, rv, hv, fv):
        cq, ckv = rv
        g1, g2 = fv
        return [cq * _rms(cq) * g1, ckv * _rms(ckv) * g2], []
    return _rows(fn, "mla_prep_fwd", 256, [(proj, QR, O_CQ // QR), (proj, KVR, O_CKV // KVR)],
                 fulls=[qg, kvg], outs=[(QR, BF16), (KVR, BF16)])


def _rot(t, cc, sa, sb):
    return t * cc + pltpu.roll(t, 32, 1) * sa + pltpu.roll(t, 96, 1) * sb


def _rot_t(g, cc, sa, sb):
    return g * cc + pltpu.roll(g * sa, 96, 1) + pltpu.roll(g * sb, 32, 1)


def _rope_tables():
    pos = jnp.arange(S, dtype=F32)
    inv_freq = ROPE_THETA ** (-jnp.arange(0, ROPE, 2, dtype=F32) / ROPE)
    ang = pos[:, None] * inv_freq[None, :]
    cos, sin, z = jnp.cos(ang), jnp.sin(ang), jnp.zeros((S, 32), F32)
    cc = jnp.concatenate([cos, cos, z, z], axis=1)
    sa = jnp.concatenate([z, sin, z, z], axis=1)
    sb = jnp.concatenate([-sin, z, z, z], axis=1)
    return cc, sa, sb


ATT_SCALE = 1.0 / math.sqrt(NOPE + ROPE)


def _rope_fwd(q, kv, proj, tabs):
    def fn(i, rv, hv, fv):
        qb, kvb, kr, cc, sa, sb = rv
        krr = _rot(kr, cc, sa, sb)
        qs, ks = [], []
        for h in range(H):
            qs += [qb[:, h * HP:h * HP + 128] * ATT_SCALE, _rot(qb[:, h * HP + 128:(h + 1) * HP], cc, sa, sb) * ATT_SCALE]
            ks += [kvb[:, h * 128:(h + 1) * 128], krr]
        kc = jnp.concatenate(ks, axis=1)
        vv = kvb[:, H * NOPE:]
        return [jnp.concatenate(qs, axis=1), kc, kc, vv, vv], []
    cc, sa, sb = tabs
    return _rows(fn, "rope_fwd", 256,
                 [(q, H * HP, 0), (kv, H * 256, 0), (proj, 128, O_KR // 128), (cc, 128, 0), (sa, 128, 0), (sb, 128, 0)],
                 outs=[(H * HP, BF16), (H * HP, BF16), (H * HP, BF16, "T"), (MLA_W, BF16), (MLA_W, BF16, "T")])


TQ, TC, ATT_NB = 256, 128, 4
_NT = (((1,), (1,)), ((), ()))


def _attn_allowed(i, kc):
    kpos = kc * TC + lax.broadcasted_iota(jnp.int32, (TC, TQ), 0)
    qpos = i * TQ + lax.broadcasted_iota(jnp.int32, (TC, TQ), 1)
    return (kpos // CHUNK) <= (qpos // CHUNK)


def _attn_fwd(qc, kc, vt):
    def body(q_ref, k_ref, vt_ref, o_ref, l_ref):
        i = pl.program_id(1)
        q = q_ref[...]

        def scores(sb):
            t0s = [pl.multiple_of((sb * ATT_NB + c) * TC, TC) for c in range(ATT_NB)]
            return [lax.dot_general(k_ref[pl.ds(t0, TC), :], q, _NT, preferred_element_type=F32) for t0 in t0s]

        def block(sb, ss, carry, masked):
            m, l, acc = carry
            t0s = [pl.multiple_of((sb * ATT_NB + c) * TC, TC) for c in range(ATT_NB)]
            if masked:
                ss = [jnp.where(_attn_allowed(i, sb * ATT_NB + c), s, -1e30) for c, s in enumerate(ss)]
            m_new = m
            for s in ss:
                m_new = jnp.maximum(m_new, jnp.max(s, axis=0, keepdims=True))
            alpha = jnp.exp(m - m_new)
            ps = [jnp.exp(s - m_new) for s in ss]
            l = alpha * l
            acc = alpha * acc
            for t0, p in zip(t0s, ps):
                l = l + jnp.sum(p, axis=0, keepdims=True)
                acc = acc + jnp.dot(vt_ref[:, pl.ds(t0, TC)], p.astype(BF16), preferred_element_type=F32)
            return m_new, l, acc

        nsb = (i + 2) // 2
        c = (jnp.full((1, TQ), -1e30, F32), jnp.zeros((1, TQ), F32), jnp.zeros((VDIM, TQ), F32))

        def step(sb, sc):
            nxt = scores(sb + 1)
            return nxt, block(sb, sc[0], sc[1], False)

        ss, c = lax.fori_loop(0, nsb - 1, step, (scores(0), c))
        m, l, acc = block(nsb - 1, ss, c, True)
        o_ref[...] = (acc / l).T
        l_ref[...] = m + jnp.log(l)

    return pl.pallas_call(
        body, name="attn_fwd", grid=(H, S // TQ),
        in_specs=[pl.BlockSpec((TQ, HP), lambda h, i: (i, h)),
                  pl.BlockSpec((S, HP), lambda h, i: (0, h)),
                  pl.BlockSpec((VDIM, S), lambda h, i: (h, 0))],
        out_specs=[pl.BlockSpec((TQ, VDIM), lambda h, i: (i, h)), pl.BlockSpec((None, 1, TQ), lambda h, i: (h, 0, i))],
        out_shape=[jax.ShapeDtypeStruct((S, MLA_W), F32), jax.ShapeDtypeStruct((H, 1, S), F32)],
        compiler_params=pltpu.CompilerParams(dimension_semantics=("parallel", "arbitrary"),
                                             vmem_limit_bytes=24 * MIB),
    )(qc, kc, vt)


def _gate_mul_fwd(name, val, proj, width, cb):
    def fn(i, rv, hv, fv):
        o, z = rv
        return [o * _silu(z)], []
    return _rows(fn, name, 256, [(val, width, 0), (proj, width, cb)], outs=[(width, BF16)])[0]


def _conv_fwd(proj, w, b):
    def fn(i, rv, hv, fv):
        (xb,), (halo,), (ww, bb) = rv, hv, fv
        halo = jnp.where(i > 0, halo, 0.0)
        row = lax.broadcasted_iota(jnp.int32, xb.shape, 0)
        acc = bb + ww[3:4] * xb
        for s in range(1, CONV_W):
            acc = acc + ww[3 - s:4 - s] * _shift_down(xb, halo, s, row)
        return [acc, acc], []
    return _rows(fn, "conv_fwd", 128, [(proj, LRU_W, O_XC // LRU_W)], halos=[(proj, LRU_W, O_XC // LRU_W, "prev")],
                 fulls=[w, b], outs=[(LRU_W, F32), (LRU_W, BF16)])


def _lru_terms(ga, gx, xc, ba, bx, lam):
    r = _sig(ga + ba)
    ig = _sig(gx + bx)
    sp = jnp.maximum(-lam, 0.0) + jnp.log(1.0 + jnp.exp(-jnp.abs(lam)))
    log_a = -LRU_C * r * sp
    a = jnp.exp(log_a)
    e2 = jnp.exp(2.0 * log_a)
    om = 1.0 - e2
    mult = jnp.sqrt(jnp.maximum(om, 0.0))
    return r, ig, sp, a, e2, om, mult


def _lru_gates_fwd(gates, xc, ba, bx, lam):
    def fn(i, rv, hv, fv):
        ga, gx, x = rv
        r, ig, sp, a, e2, om, mult = _lru_terms(ga, gx, x, *fv)
        return [a, mult * (ig * x)], []
    return _rows(fn, "lru_gates_fwd", 128, [(gates, LRU_W, 0), (gates, LRU_W, 1), (xc, LRU_W, 0)],
                 fulls=[ba, bx, lam], outs=[(LRU_W, F32), (LRU_W, F32)])


SCAN_T, SCAN_CW = 64, 256


def _scan_fwd(a, b):
    def body(a_ref, b_ref, h_ref):
        row = lax.broadcasted_iota(jnp.int32, (SCAN_T, SCAN_CW), 0)

        def step(blk, hc):
            t0 = pl.multiple_of(blk * SCAN_T, SCAN_T)
            A = a_ref[pl.ds(t0, SCAN_T), :]
            B = b_ref[pl.ds(t0, SCAN_T), :]
            d = 1
            while d < SCAN_T:
                keep = row >= d
                A_s = jnp.where(keep, pltpu.roll(A, d, 0), 1.0)
                B_s = jnp.where(keep, pltpu.roll(B, d, 0), 0.0)
                B = A * B_s + B
                A = A * A_s
                d *= 2
            hh = A * hc + B
            h_ref[pl.ds(t0, SCAN_T), :] = hh
            return hh[SCAN_T - 1:SCAN_T, :]

        lax.fori_loop(0, S // SCAN_T, step, jnp.zeros((1, SCAN_CW), F32))

    spec = pl.BlockSpec((S, SCAN_CW), lambda j: (0, j))
    return pl.pallas_call(
        body, name="scan_fwd", grid=(LRU_W // SCAN_CW,), in_specs=[spec, spec], out_specs=spec,
        out_shape=jax.ShapeDtypeStruct((S, LRU_W), F32),
        compiler_params=pltpu.CompilerParams(dimension_semantics=("parallel",),
                                             vmem_limit_bytes=_vmem(3 * _nbytes((S, SCAN_CW), F32))),
    )(a, b)


def _merge_fwd(pa, pb, pc, proj):
    def fn(i, rv, hv, fv):
        a, b, c, ga, gb, gc = rv
        return [_sig(ga) * a + _sig(gb) * b + _sig(gc) * c], []
    return _rows(fn, "merge_fwd", 256,
                 [(pa, D, 0), (pb, D, 0), (pc, D, 0), (proj, D, O_GA // D), (proj, D, O_GB // D), (proj, D, O_GC // D)],
                 outs=[(D, BF16)])[0]


def _post_fwd(x, o2, g):
    def fn(i, rv, hv, fv):
        xb, ob = rv
        return [xb + ob * _rms(ob) * fv[0]], []
    return _rows(fn, "post_fwd", 256, [(x, D, 0), (o2, D, 0)], fulls=[g], outs=[(D, F32)])[0]


SB = 640
BD_TM = 512


def _bd_fwd(xcb, wsb, l):
    def body(x_ref, w_ref, o_ref):
        o_ref[...] = jnp.dot(x_ref[...], w_ref[...], preferred_element_type=F32).astype(o_ref.dtype)

    return pl.pallas_call(
        body, name="lru_gate_mm", grid=(S // BD_TM, 4),
        in_specs=[pl.BlockSpec((BD_TM, SB), lambda i, q: (i, q % 2)),
                  pl.BlockSpec((None, None, SB, SB), lambda i, q: (l, q, 0, 0))],
        out_specs=pl.BlockSpec((BD_TM, SB), lambda i, q: (i, q)),
        out_shape=jax.ShapeDtypeStruct((S, 2 * LRU_W), BF16),
        compiler_params=pltpu.CompilerParams(dimension_semantics=("parallel", "parallel"), vmem_limit_bytes=VMEM_LIMIT),
    )(xcb, wsb)


def _bd_dx(dgates, wsb, l):
    def body(d_ref, w_ref, o_ref, acc_ref):
        g = pl.program_id(2)

        @pl.when(g == 0)
        def _():
            acc_ref[...] = jnp.zeros_like(acc_ref)

        acc_ref[...] += lax.dot_general(d_ref[...], w_ref[...], (((1,), (1,)), ((), ())), preferred_element_type=F32)

        @pl.when(g == 1)
        def _():
            o_ref[...] = acc_ref[...].astype(o_ref.dtype)

    return pl.pallas_call(
        body, name="lru_gate_dx", grid=(S // BD_TM, 2, 2),
        in_specs=[pl.BlockSpec((BD_TM, SB), lambda i, s, g: (i, 2 * g + s)),
                  pl.BlockSpec((None, None, SB, SB), lambda i, s, g: (l, 2 * g + s, 0, 0))],
        out_specs=pl.BlockSpec((BD_TM, SB), lambda i, s, g: (i, s)),
        out_shape=jax.ShapeDtypeStruct((S, LRU_W), BF16),
        scratch_shapes=[pltpu.VMEM((BD_TM, SB), F32)],
        compiler_params=pltpu.CompilerParams(dimension_semantics=("parallel", "parallel", "arbitrary"),
                                             vmem_limit_bytes=VMEM_LIMIT),
    )(dgates, wsb)


def _bd_dw(xcb, dgates):
    tk = 1024

    def body(x_ref, d_ref, o_ref):
        @pl.when(pl.program_id(1) == 0)
        def _():
            o_ref[...] = jnp.zeros_like(o_ref)

        o_ref[...] += lax.dot_general(x_ref[...], d_ref[...], (((0,), (0,)), ((), ())), preferred_element_type=F32)

    return pl.pallas_call(
        body, name="lru_gate_dw", grid=(4, S // tk),
        in_specs=[pl.BlockSpec((tk, SB), lambda q, k: (k, q % 2)), pl.BlockSpec((tk, SB), lambda q, k: (k, q))],
        out_specs=pl.BlockSpec((None, SB, SB), lambda q, k: (q, 0, 0)),
        out_shape=jax.ShapeDtypeStruct((4, SB, SB), F32),
        compiler_params=pltpu.CompilerParams(dimension_semantics=("parallel", "arbitrary"), vmem_limit_bytes=VMEM_LIMIT),
    )(xcb, dgates)


def _bd_extract(dwsb):
    def body(w_ref, o_ref):
        lane = lax.broadcasted_iota(jnp.int32, (LRU_BW, 128), 1)
        for q in range(4):
            for kk in range(8):
                c0 = LRU_BW * kk
                w0, off = (c0 // 128) * 128, c0 % 128
                rows = pl.ds(LRU_BW * kk, LRU_BW)
                blk = w_ref[q, rows, w0:w0 + 128]
                if off:
                    blk = pltpu.roll(blk, 128 - off, 1)
                    if off + LRU_BW > 128:
                        nxt = pltpu.roll(w_ref[q, rows, w0 + 128:w0 + 256], 128 - off, 1)
                        blk = jnp.where(lane < 128 - off, blk, nxt)
                o_ref[q // 2, 8 * (q % 2) + kk] = blk.astype(BF16)

    return pl.pallas_call(
        body, name="lru_gate_dw_blocks",
        in_specs=[pl.BlockSpec(memory_space=pltpu.VMEM)], out_specs=pl.BlockSpec(memory_space=pltpu.VMEM),
        out_shape=jax.ShapeDtypeStruct((2, LRU_NB, LRU_BW, 128), BF16),
        compiler_params=pltpu.CompilerParams(vmem_limit_bytes=VMEM_LIMIT),
    )(dwsb)


def _layer_fwd(x, P, l, tabs, token=None, late=None):
    A = {"x": x}
    A["h"] = _prenorm_fwd(x, P["pre_g"], token)
    proj = A["proj"] = _mm(A["h"], P["wp"], "nt", "in_proj", out_dtype=BF16, tm=1024)
    A["ya"] = _gmlp_fwd(proj, P["ln_g"], P["ln_b"], P["ws"], P["bst"])
    A["xc"], A["xcb"] = _conv_fwd(proj, P["conv_w"], P["conv_b"])
    A["gates"] = _bd_fwd(A["xcb"], P["wsb"], l)
    A["a"], bterm = _lru_gates_fwd(A["gates"], A["xc"], P["ba"], P["bx"], P["lam"])
    A["hs"] = _scan_fwd(A["a"], bterm)
    A["yc"] = _gate_mul_fwd("yc_fwd", A["hs"], proj, LRU_W, O_ZC // LRU_W)
    if late is not None:
        P.update(late(A["yc"]))
    A["cqn"], A["ckvn"] = _mla_prep_fwd(proj, P["qg"], P["kvg"])
    q = _mm(A["cqn"], P["wuq"], "nt", "q_up", out_dtype=BF16)
    kv = _mm(A["ckvn"], P["wukv"], "nt", "kv_up", out_dtype=BF16)
    A["qc"], A["kc"], A["kct"], A["vv"], vt = _rope_fwd(q, kv, proj, tabs)
    A["o"], A["lse"] = _attn_fwd(A["qc"], A["kc"], vt)
    A["yb"] = _gate_mul_fwd("yb_fwd", A["o"], proj, MLA_W, O_ZB // MLA_W)
    A["pa"] = _mm(A["ya"], P["wpa"], "nn", "proj_a", out_dtype=BF16)
    A["pb"] = _mm(A["yb"], P["wpb"], "nn", "proj_b", out_dtype=BF16)
    A["pc"] = _mm(A["yc"], P["wpc"], "nn", "proj_c", out_dtype=BF16)
    A["merged"] = _merge_fwd(A["pa"], A["pb"], A["pc"], proj)
    A["o2"] = _mm(A["merged"], P["wout"], "nn", "out_proj")
    return _post_fwd(x, A["o2"], P["post_g"]), A


def _loss_fwd(y, tgt):
    def fn(i, rv, hv, fv):
        yb, tb = rv
        e = yb - tb
        part = 0.5 * jnp.sum(jnp.mean(e * e, axis=-1, keepdims=True), axis=0, keepdims=True)
        return [e * (1.0 / D)], [part]
    return _rows(fn, "loss", 256, [(y, D, 0), (tgt, D, 0)], outs=[(D, F32)], accs=[(1, 1)])


def _post_bwd(dxn, o2, g, token=None):
    def fn(i, rv, hv, fv):
        dy, ob = rv
        dx, dg = _rms_bwd(dy, ob, fv[0])
        return [dx], [_colsum(dg)]
    return _rows(fn, "post_bwd", 256, [(dxn, D, 0), (o2, D, 0)], fulls=[g] + ([] if token is None else [token]),
                 outs=[(D, BF16)], accs=[(1, D)])


def _merge_bwd(dm, pa, pb, pc, proj, dproj):
    def fn(i, rv, hv, fv):
        d, a, b, c, ga, gb, gc = rv
        outs_p, outs_g = [], []
        for p, gg in ((a, ga), (b, gb), (c, gc)):
            s = _sig(gg)
            outs_p.append(d * s)
            outs_g.append(d * p * s * (1.0 - s))
        return outs_p + [jnp.concatenate(outs_g, axis=1)], []
    return _rows(fn, "merge_bwd", 128,
                 [(dm, D, 0), (pa, D, 0), (pb, D, 0), (pc, D, 0),
                  (proj, D, O_GA // D), (proj, D, O_GB // D), (proj, D, O_GC // D)],
                 outs=[(D, BF16)] * 3 + [(3 * D, BF16, (dproj, NP, O_GA // (3 * D)))])


def _gmlp_bwd(dya, proj, ln_g, ln_b, ws, bst, dproj):
    gw = GM_W // GM_G

    def fn(i, rv, hv, fv):
        dy, u, v, z = rv
        g, b, w, bt = fv
        vh, rs, vn = _gm_norm(v, g, b)
        sv = _gm_sv(vn, w, bt)
        sz = _silu(z)
        du = dy * sv * sz
        dsv = dy * u * sz
        dz = dy * u * sv * _dsilu(z)
        mask = _gm_mask()
        lane = lax.broadcasted_iota(jnp.int32, (GM_B, 128), 1)
        dvn_parts, dws, dbst = [], [], jnp.zeros((GM_B, 128), F32)
        for k in range(GM_G):
            wm = jnp.where(mask, w[k], 0.0).astype(BF16)
            dsk = dsv[:, k * gw:(k + 1) * gw]
            dskb = dsk.astype(BF16)
            dvn_parts.append(lax.dot_general(wm, dskb, (((0,), (0,)), ((), ())), preferred_element_type=F32))
            dwk = lax.dot_general(dskb, vn[:, k * gw:(k + 1) * gw].astype(BF16), (((1,), (1,)), ((), ())),
                                  preferred_element_type=F32)
            dws.append(jnp.where(mask, dwk, 0.0)[None])
            dbst = dbst + jnp.where(lane == k, jnp.sum(dsk, axis=1, keepdims=True), 0.0)
        dvn = jnp.concatenate(dvn_parts, axis=1)
        dvh = dvn * g
        dv = rs * (dvh - jnp.mean(dvh, axis=-1, keepdims=True) - vh * jnp.mean(dvh * vh, axis=-1, keepdims=True))
        return ([jnp.concatenate([du, dv, dz], axis=1)],
                [jnp.concatenate(dws, axis=0), dbst, _colsum(dvn * vh), _colsum(dvn)])
    return _rows(fn, "gmlp_bwd", GM_B, [(dya, GM_W, 0), (proj, GM_W, 0), (proj, GM_W, 1), (proj, GM_W, 2)],
                 fulls=[ln_g, ln_b, ws, bst], outs=[(3 * GM_W, BF16, (dproj, NP, O_U // (3 * GM_W)))],
                 accs=[(GM_G, GM_B, GM_B), (GM_B, 128), (1, GM_W), (1, GM_W)])


def _yb_bwd(dyb, o, proj, dproj):
    def fn(i, rv, hv, fv):
        dy, ob, z = rv
        do = dy * _silu(z)
        prod = do * ob
        lane = lax.broadcasted_iota(jnp.int32, (dy.shape[0], 128), 1)
        dl = jnp.zeros((dy.shape[0], 128), F32)
        for h in range(H):
            dl = dl + jnp.where(lane == h, jnp.sum(prod[:, h * VDIM:(h + 1) * VDIM], axis=1, keepdims=True), 0.0)
        return [do, dl, dy * ob * _dsilu(z)], []
    return _rows(fn, "yb_bwd", 256, [(dyb, MLA_W, 0), (o, MLA_W, 0), (proj, MLA_W, O_ZB // MLA_W)],
                 outs=[(MLA_W, BF16), (128, F32, "T"), (MLA_W, BF16, (dproj, NP, O_ZB // MLA_W))])


def _attn_bwd(qc, kc, kct, vv, do, lse, dlt):
    def body(q_ref, k_ref, kt_ref, v_ref, do_ref, l_ref, d_ref, dq_ref, dk_ref, dv_ref, dqt_ref):
        h, i = pl.program_id(0), pl.program_id(1)

        @pl.when(i == 0)
        def _():
            dk_ref[...] = jnp.zeros_like(dk_ref)
            dv_ref[...] = jnp.zeros_like(dv_ref)

        q = q_ref[...]
        dob = do_ref[...]
        lse = l_ref[...]
        dl = d_ref[pl.ds(h, 1), :]
        dqt_ref[...] = jnp.zeros_like(dqt_ref)

        def rows_of(sb, c):
            return pl.ds(pl.multiple_of((sb * ATT_NB + c) * TC, TC), TC)

        def front(sb):
            return [(lax.dot_general(k_ref[rows_of(sb, c), :], q, _NT, preferred_element_type=F32),
                     lax.dot_general(v_ref[rows_of(sb, c), :], dob, _NT, preferred_element_type=F32))
                    for c in range(ATT_NB)]

        def block(sb, sd, masked):
            dqt = None
            for c, (s, dp) in enumerate(sd):
                rows = rows_of(sb, c)
                p = jnp.exp(s - lse)
                if masked:
                    p = jnp.where(_attn_allowed(i, sb * ATT_NB + c), p, 0.0)
                ds = (p * (dp - dl)).astype(BF16)
                dk_ref[rows, :] += jnp.dot(ds, q, preferred_element_type=F32)
                dv_ref[rows, :] += jnp.dot(p.astype(BF16), dob, preferred_element_type=F32)
                part = jnp.dot(kt_ref[:, rows], ds, preferred_element_type=F32)
                dqt = part if dqt is None else dqt + part
            dqt_ref[...] += dqt

        def step(sb, sd):
            nxt = front(sb + 1)
            block(sb, sd, False)
            return nxt

        nsb = (i + 2) // 2
        sd = lax.fori_loop(0, nsb - 1, step, front(0))
        block(nsb - 1, sd, True)
        dq_ref[...] = dqt_ref[...].T.astype(dq_ref.dtype)

    blk = lambda w: pl.BlockSpec((TQ, w), lambda h, i: (i, h))
    head = lambda w: pl.BlockSpec((S, w), lambda h, i: (0, h))
    return pl.pallas_call(
        body, name="attn_bwd", grid=(H, S // TQ),
        in_specs=[blk(HP), head(HP), pl.BlockSpec((HP, S), lambda h, i: (h, 0)), head(VDIM), blk(VDIM),
                  pl.BlockSpec((None, 1, TQ), lambda h, i: (h, 0, i)), pl.BlockSpec((8, TQ), lambda h, i: (0, i))],
        out_specs=[blk(HP), head(HP), head(VDIM)],
        out_shape=[jax.ShapeDtypeStruct((S, H * HP), BF16), jax.ShapeDtypeStruct((S, H * HP), F32),
                   jax.ShapeDtypeStruct((S, MLA_W), F32)],
        scratch_shapes=[pltpu.VMEM((HP, TQ), F32)],
        compiler_params=pltpu.CompilerParams(dimension_semantics=("parallel", "arbitrary"),
                                             vmem_limit_bytes=28 * MIB),
    )(qc, kc, kct, vv, do, lse, dlt)


def _rope_bwd(dqc, dkc, dvv, tabs):
    def fn(i, rv, hv, fv):
        dq, dk, dv, cc, sa, sb = rv
        qs, ks = [], []
        dkr = jnp.zeros((dq.shape[0], 128), F32)
        for h in range(H):
            qs += [dq[:, h * HP:h * HP + 128] * ATT_SCALE, _rot_t(dq[:, h * HP + 128:(h + 1) * HP], cc, sa, sb) * ATT_SCALE]
            ks.append(dk[:, h * HP:h * HP + 128])
            dkr = dkr + dk[:, h * HP + 128:(h + 1) * HP]
        return [jnp.concatenate(qs, axis=1), jnp.concatenate(ks + [dv], axis=1), _rot_t(dkr, cc, sa, sb)], []
    cc, sa, sb = tabs
    return _rows(fn, "rope_bwd", 256,
                 [(dqc, H * HP, 0), (dkc, H * HP, 0), (dvv, MLA_W, 0), (cc, 128, 0), (sa, 128, 0), (sb, 128, 0)],
                 outs=[(H * HP, BF16), (H * 256, BF16), (128, BF16)])


MLA_GROUP = 1536


def _mla_prep_bwd(dcqn, dckvn, dkr, proj, qg, kvg, dproj):
    def fn(i, rv, hv, fv):
        d1, d2, dk, cq, ckv = rv
        g1, g2 = fv
        dx1, dg1 = _rms_bwd(d1, cq, g1)
        dx2, dg2 = _rms_bwd(d2, ckv, g2)
        zeros = jnp.zeros((d1.shape[0], MLA_GROUP - KVR - 128 - QR), F32)
        return [jnp.concatenate([dx2, dk.astype(F32), dx1, zeros], axis=1)], [_colsum(dg1), _colsum(dg2)]
    return _rows(fn, "mla_prep_bwd", 256,
                 [(dcqn, QR, 0), (dckvn, KVR, 0), (dkr, 128, 0), (proj, QR, O_CQ // QR), (proj, KVR, O_CKV // KVR)],
                 fulls=[qg, kvg], outs=[(MLA_GROUP, BF16, (dproj, NP, O_CKV // MLA_GROUP))], accs=[(1, QR), (1, KVR)])


def _yc_bwd(dyc, hs, proj, dproj):
    def fn(i, rv, hv, fv):
        dy, hh, z = rv
        return [dy * _silu(z), dy * hh * _dsilu(z)], []
    return _rows(fn, "yc_bwd", 128, [(dyc, LRU_W, 0), (hs, LRU_W, 0), (proj, LRU_W, O_ZC // LRU_W)],
                 outs=[(LRU_W, F32), (LRU_W, BF16, (dproj, NP, O_ZC // LRU_W))])


def _scan_bwd(a, hs, dh):
    nblk = S // SCAN_T

    def body(a_ref, h_ref, dh_ref, da_ref, db_ref):
        row = lax.broadcasted_iota(jnp.int32, (SCAN_T, SCAN_CW), 0)

        def step(j, carry):
            gc, ac = carry
            blk = nblk - 1 - j
            t0 = pl.multiple_of(blk * SCAN_T, SCAN_T)
            av = a_ref[pl.ds(t0, SCAN_T), :]
            A = jnp.where(row < SCAN_T - 1, pltpu.roll(av, SCAN_T - 1, 0), ac)
            B = dh_ref[pl.ds(t0, SCAN_T), :]
            d = 1
            while d < SCAN_T:
                keep = row < SCAN_T - d
                A_s = jnp.where(keep, pltpu.roll(A, SCAN_T - d, 0), 1.0)
                B_s = jnp.where(keep, pltpu.roll(B, SCAN_T - d, 0), 0.0)
                B = A * B_s + B
                A = A * A_s
                d *= 2
            g = A * gc + B
            p0 = pl.multiple_of(jnp.maximum(t0 - 8, 0), 8)
            last = jnp.where(blk > 0, h_ref[pl.ds(p0, 8), :][7:8, :], 0.0)
            h_prev = jnp.where(row >= 1, pltpu.roll(h_ref[pl.ds(t0, SCAN_T), :], 1, 0), last)
            da_ref[pl.ds(t0, SCAN_T), :] = g * h_prev
            db_ref[pl.ds(t0, SCAN_T), :] = g
            return g[0:1, :], av[0:1, :]

        z = jnp.zeros((1, SCAN_CW), F32)
        lax.fori_loop(0, nblk, step, (z, z))

    spec = pl.BlockSpec((S, SCAN_CW), lambda j: (0, j))
    return pl.pallas_call(
        body, name="scan_bwd", grid=(LRU_W // SCAN_CW,), in_specs=[spec] * 3, out_specs=[spec] * 2,
        out_shape=[jax.ShapeDtypeStruct((S, LRU_W), F32)] * 2,
        compiler_params=pltpu.CompilerParams(dimension_semantics=("parallel",),
                                             vmem_limit_bytes=_vmem(5 * _nbytes((S, SCAN_CW), F32))),
    )(a, hs, dh)


def _lru_gates_bwd(da, db, gates, xc, ba, bx, lam):
    def fn(i, rv, hv, fv):
        dav, dbv, ga, gx, x = rv
        bav, bxv, lamv = fv
        r, ig, sp, a, e2, om, mult = _lru_terms(ga, gx, x, bav, bxv, lamv)
        dmult = dbv * ig * x
        dig = dbv * mult * x
        dxc1 = dbv * mult * ig
        dlog_a = dav * a + jnp.where(om > 0.0, dmult * (-e2 / mult), 0.0)
        dr = dlog_a * (-LRU_C * sp)
        dga = dr * r * (1.0 - r)
        dgx = dig * ig * (1.0 - ig)
        dlam = _colsum(dlog_a * (-LRU_C * r)) * (-_sig(-lamv))
        return [jnp.concatenate([dga, dgx], axis=1), dxc1], [_colsum(dga), _colsum(dgx), dlam]
    return _rows(fn, "lru_gates_bwd", 128,
                 [(da, LRU_W, 0), (db, LRU_W, 0), (gates, LRU_W, 0), (gates, LRU_W, 1), (xc, LRU_W, 0)],
                 fulls=[ba, bx, lam], outs=[(2 * LRU_W, BF16), (LRU_W, F32)], accs=[(1, LRU_W)] * 3)


def _conv_bwd(dxc1, dxc2, proj, w, dproj):
    cb = O_XC // LRU_W

    def fn(i, rv, hv, fv):
        d1, d2, xb = rv
        n1, n2, xprev = hv
        ww = fv[0]
        last = i == S // 128 - 1
        dxc = d1 + d2
        nxt = jnp.where(last, 0.0, n1 + n2)
        xprev = jnp.where(i > 0, xprev, 0.0)
        row = lax.broadcasted_iota(jnp.int32, xb.shape, 0)
        dx = ww[3:4] * dxc
        dws = [None] * CONV_W
        dws[3] = _colsum(dxc * xb)
        for s in range(1, CONV_W):
            dx = dx + ww[3 - s:4 - s] * _shift_up(dxc, nxt, s, row)
            dws[3 - s] = _colsum(dxc * _shift_down(xb, xprev, s, row))
        return [dx], [jnp.concatenate(dws, axis=0), _colsum(dxc)]
    return _rows(fn, "conv_bwd", 128, [(dxc1, LRU_W, 0), (dxc2, LRU_W, 0), (proj, LRU_W, cb)],
                 halos=[(dxc1, LRU_W, 0, "next"), (dxc2, LRU_W, 0, "next"), (proj, LRU_W, cb, "prev")],
                 fulls=[w], outs=[(LRU_W, BF16, (dproj, NP, cb))], accs=[(CONV_W, LRU_W), (1, LRU_W)])


def _prenorm_bwd(dxn, dh, x, g):
    def fn(i, rv, hv, fv):
        dy, dhh, xb = rv
        dx, dg = _rms_bwd(dhh, xb, fv[0])
        return [dy + dx], [_colsum(dg)]
    return _rows(fn, "prenorm_bwd", 256, [(dxn, D, 0), (dh, D, 0), (x, D, 0)], fulls=[g], outs=[(D, F32)],
                 accs=[(1, D)])


def _layer_bwd(dxn, A, P, l, tabs, token=None):
    G, GB = {}, {}
    proj = A["proj"]

    def dw(key, a, b, name, **tiles):
        GB[key] = _mm(a, b, "tn", name, out_dtype=BF16, **tiles)

    do2, G["post_g"] = _post_bwd(dxn, A["o2"], P["post_g"], token)
    dm = _mm(do2, P["wout"], "nt", "out_proj_dx", out_dtype=BF16)
    dw("wout", A["merged"], do2, "out_proj_dw")
    dpa, dpb, dpc, dproj = _merge_bwd(dm, A["pa"], A["pb"], A["pc"], proj, None)
    dya = _mm(dpa, P["wpa"], "nt", "proj_a_dx", out_dtype=BF16)
    dw("wpa", A["ya"], dpa, "proj_a_dw")
    dyb = _mm(dpb, P["wpb"], "nt", "proj_b_dx", out_dtype=BF16)
    dw("wpb", A["yb"], dpb, "proj_b_dw")
    dyc = _mm(dpc, P["wpc"], "nt", "proj_c_dx", out_dtype=BF16)
    dw("wpc", A["yc"], dpc, "proj_c_dw")
    dproj, G["ws"], G["bst"], G["ln_g"], G["ln_b"] = _gmlp_bwd(dya, proj, P["ln_g"], P["ln_b"], P["ws"], P["bst"], dproj)
    do, dl, dproj = _yb_bwd(dyb, A["o"], proj, dproj)
    dqc, dkc, dvv = _attn_bwd(A["qc"], A["kc"], A["kct"], A["vv"], do, A["lse"], dl)
    dq, dkv, dkr = _rope_bwd(dqc, dkc, dvv, tabs)
    dcqn = _mm(dq, P["wuq"], "nn", "q_up_dx", out_dtype=BF16)
    dw("wuq", dq, A["cqn"], "q_up_dw")
    dckvn = _mm(dkv, P["wukv"], "nn", "kv_up_dx", out_dtype=BF16)
    dw("wukv", dkv, A["ckvn"], "kv_up_dw")
    dproj, G["qg"], G["kvg"] = _mla_prep_bwd(dcqn, dckvn, dkr, proj, P["qg"], P["kvg"], dproj)
    dhs, dproj = _yc_bwd(dyc, A["hs"], proj, dproj)
    da, db = _scan_bwd(A["a"], A["hs"], dhs)
    dgates, dxc1, G["ba"], G["bx"], G["lam"] = _lru_gates_bwd(da, db, A["gates"], A["xc"], P["ba"], P["bx"], P["lam"])
    dxc2 = _bd_dx(dgates, P["wsb"], l)
    G["wab"] = _bd_extract(_bd_dw(A["xcb"], dgates))
    dproj, G["conv_w"], G["conv_b"] = _conv_bwd(dxc1, dxc2, proj, P["conv_w"], dproj)
    dh = _mm(dproj, P["wp"], "nn", "in_proj_dx", tm=1024, tn=1024)
    dw("wp", dproj, A["h"], "in_proj_dw", tm=1536, tn=1024)
    dx, G["pre_g"] = _prenorm_bwd(dxn, dh, A["x"], P["pre_g"])
    return dx, G, GB


_ORIG_OFF = [0]
for _s in IN_SIZES:
    _ORIG_OFF.append(_ORIG_OFF[-1] + _s)
_PAD_OFF = {0: O_U, 1: O_V, 2: O_ZA, 3: O_CQ, 4: O_CKV, 5: O_KR, 6: O_ZB, 7: O_XC, 8: O_ZC, 9: O_GA, 10: O_GB, 11: O_GC}
SHARD_IN = N_IN // N_CHIPS


def _pieces_w_in(j):
    lo, hi = SHARD_IN * j, SHARD_IN * (j + 1)
    out = []
    for k in range(len(IN_SIZES)):
        a, b = max(lo, _ORIG_OFF[k]), min(hi, _ORIG_OFF[k + 1])
        if a < b:
            out.append((a - lo, _PAD_OFF[k] + a - _ORIG_OFF[k], b - a))
    return out


def _pieces_uq(j):
    return [(192 * hh, HP * (2 * j + hh), NOPE + ROPE) for hh in range(2)]


def _pieces_ukv(j):
    out = []
    for hh in range(2):
        h = 2 * j + hh
        out += [(256 * hh, NOPE * h, NOPE), (256 * hh + NOPE, H * NOPE + VDIM * h, VDIM)]
    return out


def _pieces_rows(r):
    return lambda j: [(0, r * j, r)]


LAYOUT = {
    "w_in": (SHARD_IN, NP, _pieces_w_in),
    "mla_w_uq": (2 * (NOPE + ROPE), H * HP, _pieces_uq),
    "mla_w_ukv": (2 * (NOPE + VDIM), 2 * H * 128, _pieces_ukv),
    "lru_conv_w": (1, N_CHIPS, _pieces_rows(1)),
    "w_proj_a": (GM_W // N_CHIPS, GM_W, _pieces_rows(GM_W // N_CHIPS)),
    "w_proj_b": (MLA_W // N_CHIPS, MLA_W, _pieces_rows(MLA_W // N_CHIPS)),
    "w_proj_c": (LRU_W // N_CHIPS, LRU_W, _pieces_rows(LRU_W // N_CHIPS)),
    "w_out": (D // N_CHIPS, D, _pieces_rows(D // N_CHIPS)),
}
TRANSPOSED = ("w_in", "mla_w_uq", "mla_w_ukv")


def _superblocks(w_a, w_x):
    w6 = jnp.stack([w_a, w_x], axis=1).reshape(DEPTH, 4, 8, LRU_BW, LRU_BW).astype(BF16)
    bands = [jnp.pad(w6[:, :, k], ((0, 0), (0, 0), (0, 0), (LRU_BW * k, SB - LRU_BW * (k + 1)))) for k in range(8)]
    return jnp.concatenate(bands, axis=2)


_HBM = pl.BlockSpec(memory_space=pltpu.HBM)


def _position():
    return lax.axis_index("x"), lax.axis_index("y"), lax.axis_index("c")


def _allgather(blocks, name):
    n = len(blocks)

    def body(*refs):
        ins, outs = refs[:n], refs[n:2 * n]
        send, recv, lsem = refs[2 * n:]
        x, y, c = _position()
        me, sib = (x, y, c), (x, y, 1 - c)
        chips = [(1 - x, y), (x, 1 - y), (1 - x, 1 - y)]

        def cp(k, a, block, to, src=None):
            dst = outs[a].at[4 * block[0] + 2 * block[1] + block[2]]
            return pltpu.make_async_remote_copy(src_ref=dst if src is None else src, dst_ref=dst,
                                                send_sem=send.at[7 * a + k], recv_sem=recv.at[7 * a + k],
                                                device_id=to, device_id_type=MESH)

        mine = [pltpu.make_async_copy(ins[a], outs[a].at[4 * x + 2 * y + c], lsem.at[a]) for a in range(n)]
        for m in mine:
            m.start()
        first = []
        for a in range(n):
            first.append(cp(0, a, me, sib, src=ins[a]))
            first += [cp(1 + j, a, me, (*chip, c), src=ins[a]) for j, chip in enumerate(chips)]
        for f in first:
            f.start()
        passed = []
        for j, chip in enumerate(chips):
            for a in range(n):
                cp(1 + j, a, (*chip, c), me).wait_recv()
                p = cp(4 + j, a, (*chip, c), sib)
                p.start()
                passed.append(p)
        for a in range(n):
            cp(0, a, sib, me).wait_recv()
            for j, chip in enumerate(chips):
                cp(4 + j, a, (*chip, 1 - c), me).wait_recv()
        for f in first + passed:
            f.wait_send()
        for m in mine:
            m.wait()

    return pl.pallas_call(
        body, name=name,
        out_shape=[jax.ShapeDtypeStruct((8,) + b.shape, b.dtype) for b in blocks],
        in_specs=[_HBM] * n, out_specs=[_HBM] * n,
        scratch_shapes=[pltpu.SemaphoreType.DMA((7 * n,)), pltpu.SemaphoreType.DMA((7 * n,)),
                        pltpu.SemaphoreType.DMA((n,))],
    )(*blocks)


_REL = (2, 1, 3)


def _cut(r):
    return r if r < 32 else (r // 2 + 15) // 16 * 16


def _half_rows(r, c0):
    return _cut(r) if c0 == 0 else r - _cut(r)


def _half_pieces(lay_a, jsrc, c0):
    r = lay_a[0]
    lo, hi = (0, _cut(r)) if c0 == 0 else (_cut(r), r)
    out = []
    for s0, d0, nr in lay_a[2](jsrc):
        a, b = max(s0, lo), min(s0 + nr, hi)
        if a < b:
            out.append((a, d0 + a - s0, b - a))
    return out


def _gather_zeros(names, srcs):
    return [jnp.zeros((LAYOUT[nm][1],) + s.shape[1:], s.dtype) for nm, s in zip(names, srcs)]


def _weights_allgather(names, srcs, name, carry=()):
    n = len(srcs)
    lay = [LAYOUT[nm] for nm in names]
    zeros = _gather_zeros(names, srcs)
    m = len(carry)

    def body(*refs):
        ins, outs = refs[:n], refs[2 * n + m:3 * n + m]
        send, recv, lsem = refs[3 * n + 2 * m:]
        x, y, c = _position()
        j = 2 * x + y
        sib = (x, y, 1 - c)
        chips = [(1 - x, y), (x, 1 - y), (1 - x, 1 - y)]

        def flow(a, k, jsrc, c0, to, from_src):
            cps = []
            for s0, d0, nr in _half_pieces(lay[a], jsrc, c0):
                dst = outs[a].at[pl.ds(d0, nr)]
                src = ins[a].at[pl.ds(s0, nr)] if from_src else dst
                cps.append(pltpu.make_async_remote_copy(src_ref=src, dst_ref=dst, send_sem=send.at[7 * a + k],
                                                        recv_sem=recv.at[7 * a + k], device_id=to, device_id_type=MESH))
            return cps

        def sized(a, k, rows):
            ref = ins[a].at[pl.ds(0, rows)]
            return pltpu.make_async_remote_copy(src_ref=ref, dst_ref=ref, send_sem=send.at[7 * a + k],
                                                recv_sem=recv.at[7 * a + k], device_id=sib, device_id_type=MESH)

        for j0 in range(N_CHIPS):
            for c0 in range(2):
                @pl.when((j == j0) & (c == c0))
                def _(j0=j0, c0=c0):
                    mine = [_half_rows(lay[a][0], c0) for a in range(n)]
                    theirs = [_half_rows(lay[a][0], 1 - c0) for a in range(n)]
                    for a in range(n):
                        for s0, d0, nr in _half_pieces(lay[a], j0, c0):
                            pltpu.make_async_copy(ins[a].at[pl.ds(s0, nr)], outs[a].at[pl.ds(d0, nr)], lsem.at[a]).start()
                    for a in range(n):
                        for cp in flow(a, 0, j0, c0, sib, True):
                            cp.start()
                        for k, chip in enumerate(chips):
                            for cp in flow(a, 1 + k, j0, c0, (*chip, c), True):
                                cp.start()
                    for k in range(3):
                        for a in range(n):
                            if mine[a]:
                                sized(a, 1 + k, mine[a]).wait_recv()
                                for cp in flow(a, 4 + k, j0 ^ _REL[k], c0, sib, False):
                                    cp.start()
                    for a in range(n):
                        if theirs[a]:
                            sized(a, 0, theirs[a]).wait_recv()
                            for k in range(3):
                                sized(a, 4 + k, theirs[a]).wait_recv()
                    for a in range(n):
                        if mine[a]:
                            for k in range(7):
                                sized(a, k, mine[a]).wait_send()
                            ref = ins[a].at[pl.ds(0, mine[a])]
                            pltpu.make_async_copy(ref, ref, lsem.at[a]).wait()

    res = pl.pallas_call(
        body, name=name,
        out_shape=[jax.ShapeDtypeStruct(z.shape, z.dtype) for z in list(zeros) + list(carry)],
        in_specs=[_HBM] * (2 * n + m), out_specs=[_HBM] * (n + m),
        input_output_aliases={n + a: a for a in range(n + m)},
        scratch_shapes=[pltpu.SemaphoreType.DMA((7 * n,)), pltpu.SemaphoreType.DMA((7 * n,)),
                        pltpu.SemaphoreType.DMA((n,))],
    )(*srcs, *zeros, *carry)
    return res[:n], res[n:]


_SEM = pl.BlockSpec(memory_space=pltpu.SEMAPHORE)
_VMEM_TOKEN = pl.BlockSpec(memory_space=pltpu.VMEM)
_TOKEN = jax.ShapeDtypeStruct((8, 128), F32)
_EFFECT = pltpu.SideEffectType.DATAFLOW_SIDE_EFFECTING


def _gather_start(names, srcs, zeros, name, after=None):
    n = len(srcs)
    lay = [LAYOUT[nm] for nm in names]
    extra = [] if after is None else [after]

    def body(*refs):
        ins, lands = refs[:n], refs[n:2 * n]
        send, recv, lsem = refs[2 * n + len(extra):2 * n + len(extra) + 3]
        refs[-1][...] = jnp.zeros_like(refs[-1])
        x, y, c = _position()
        j = 2 * x + y
        chips = [(1 - x, y), (x, 1 - y), (1 - x, 1 - y)]
        for j0 in range(N_CHIPS):
            @pl.when(j == j0)
            def _(j0=j0):
                for a in range(n):
                    for s0, d0, nr in lay[a][2](j0):
                        src, dst = ins[a].at[pl.ds(s0, nr)], lands[a].at[pl.ds(d0, nr)]
                        pltpu.make_async_copy(src, dst, lsem.at[a]).start()
                        for k, chip in enumerate(chips):
                            pltpu.make_async_remote_copy(src_ref=src, dst_ref=dst, send_sem=send.at[3 * a + k],
                                                         recv_sem=recv.at[3 * a + k], device_id=(*chip, c),
                                                         device_id_type=MESH).start()

    sems = [pltpu.SemaphoreType.DMA((3 * n,)), pltpu.SemaphoreType.DMA((3 * n,)), pltpu.SemaphoreType.DMA((n,))]
    hbm = lambda a: pltpu.HBM(a.shape, a.dtype)
    res = pl.pallas_call(
        body, name=name,
        out_shape=sems + [hbm(s) for s in srcs] + [hbm(z) for z in zeros] + [_TOKEN],
        in_specs=[_HBM] * (2 * n) + [pl.BlockSpec(memory_space=pl.ANY)] * len(extra),
        out_specs=[_SEM] * 3 + [_HBM] * (2 * n) + [_VMEM_TOKEN],
        input_output_aliases={a: 3 + a for a in range(2 * n)},
        compiler_params=pltpu.CompilerParams(has_side_effects=_EFFECT),
    )(*[pltpu.with_memory_space_constraint(s, pltpu.HBM) for s in srcs],
      *[pltpu.with_memory_space_constraint(z, pltpu.HBM) for z in zeros], *extra)
    return res[:3], res[3:3 + n], res[3 + n:3 + 2 * n], res[-1]


def _gather_wait(names, sems, srcs, lands, after, name):
    n = len(srcs)
    lay = [LAYOUT[nm] for nm in names]

    def body(*refs):
        ins, zones = refs[:n], refs[n:2 * n]
        send, recv, lsem = refs[2 * n:2 * n + 3]
        x, y, c = _position()
        for a in range(n):
            whole = zones[a].at[pl.ds(0, lay[a][0])]
            for k in range(3):
                cp = pltpu.make_async_remote_copy(src_ref=ins[a], dst_ref=whole, send_sem=send.at[3 * a + k],
                                                  recv_sem=recv.at[3 * a + k], device_id=(x, y, 1 - c),
                                                  device_id_type=MESH)
                cp.wait_send()
                cp.wait_recv()
            pltpu.make_async_copy(ins[a], whole, lsem.at[a]).wait()

    hbm = lambda a: pltpu.HBM(a.shape, a.dtype)
    res = pl.pallas_call(
        body, name=name,
        out_shape=[hbm(s) for s in srcs] + [hbm(z) for z in lands],
        in_specs=[_HBM] * (2 * n) + [_SEM] * 3 + [pl.BlockSpec(memory_space=pl.ANY)], out_specs=[_HBM] * (2 * n),
        input_output_aliases={a: a for a in range(2 * n)},
        compiler_params=pltpu.CompilerParams(has_side_effects=_EFFECT),
    )(*srcs, *lands, *sems, after)
    return res[n:]


def _clip_pieces(lay_a, jsrc, c0):
    h = lay_a[1] // 2
    lo, hi = c0 * h, (c0 + 1) * h
    out = []
    for s0, d0, nr in lay_a[2](jsrc):
        a, b = max(d0, lo), min(d0 + nr, hi)
        if a < b:
            out.append((s0 + a - d0, a, b - a))
    return out


def _rows_of(pieces):
    return sum(nr for _, _, nr in pieces)


def _both_cores(body_for):
    x, y, c = _position()
    j = 2 * x + y
    for j0 in range(N_CHIPS):
        for c0 in range(2):
            @pl.when((j == j0) & (c == c0))
            def _(j0=j0, c0=c0):
                body_for(j0, c0)


STAGE_ROWS = 512


def _staged_copy(src, dst, buf, sem_in, sem_out, rows):
    ch = buf.shape[0]
    for r in range(0, rows, ch):
        nr = min(ch, rows - r)
        stage = buf.at[pl.ds(0, nr)]
        cin = pltpu.make_async_copy(src.at[pl.ds(r, nr)], stage, sem_in)
        cin.start()
        cin.wait()
        cout = pltpu.make_async_copy(stage, dst.at[pl.ds(r, nr)], sem_out)
        cout.start()
        cout.wait()


def _half_to_sibling(names, gl, name):
    n = len(gl)
    halves = [LAYOUT[nm][1] // 2 for nm in names]

    def body(*refs):
        ins, outs = refs[:n], refs[n:2 * n]
        send, recv = refs[2 * n:]
        x, y, c = _position()

        def run(j0, c0):
            cps = [pltpu.make_async_remote_copy(src_ref=ins[a].at[pl.ds((1 - c0) * halves[a], halves[a])], dst_ref=outs[a],
                                                send_sem=send.at[a], recv_sem=recv.at[a], device_id=(x, y, 1 - c),
                                                device_id_type=MESH) for a in range(n)]
            for cp in cps:
                cp.start()
            for cp in cps:
                cp.wait()

        _both_cores(run)

    return pl.pallas_call(
        body, name=name,
        out_shape=[jax.ShapeDtypeStruct((halves[a],) + gl[a].shape[1:], gl[a].dtype) for a in range(n)],
        in_specs=[_HBM] * n, out_specs=[_HBM] * n,
        scratch_shapes=[pltpu.SemaphoreType.DMA((n,)), pltpu.SemaphoreType.DMA((n,))],
    )(*gl)


def _chip_scatter_half(names, parts, name):
    n = len(parts)
    lay = [LAYOUT[nm] for nm in names]
    zeros = [jnp.zeros((N_CHIPS, lay[a][0]) + parts[a].shape[1:], parts[a].dtype) for a in range(n)]

    def body(*refs):
        ins, outs = refs[:n], refs[2 * n:3 * n]
        send, recv = refs[3 * n:3 * n + 2]
        stage, sem_in, sem_out = refs[3 * n + 2:4 * n + 2], refs[4 * n + 2], refs[4 * n + 3]
        x, y, c = _position()
        chips = [(1 - x, y), (x, 1 - y), (1 - x, 1 - y)]

        def run(j0, c0):
            def sized(a, rows):
                return outs[a].at[0, pl.ds(0, rows)]

            for a in range(n):
                base = c0 * (lay[a][1] // 2)
                for k, chip in enumerate(chips):
                    for s0, d0, nr in _clip_pieces(lay[a], j0 ^ _REL[k], c0):
                        pltpu.make_async_remote_copy(
                            src_ref=ins[a].at[pl.ds(d0 - base, nr)], dst_ref=outs[a].at[j0, pl.ds(s0, nr)],
                            send_sem=send.at[3 * a + k], recv_sem=recv.at[3 * a + k],
                            device_id=(*chip, c), device_id_type=MESH).start()
            for a in range(n):
                base = c0 * (lay[a][1] // 2)
                for s0, d0, nr in _clip_pieces(lay[a], j0, c0):
                    _staged_copy(ins[a].at[pl.ds(d0 - base, nr)], outs[a].at[j0, pl.ds(s0, nr)], stage[a],
                                 sem_in.at[a], sem_out.at[a], nr)
            for a in range(n):
                got = _rows_of(_clip_pieces(lay[a], j0, c0))
                for k in range(3):
                    sent = _rows_of(_clip_pieces(lay[a], j0 ^ _REL[k], c0))
                    if sent:
                        pltpu.make_async_remote_copy(src_ref=sized(a, sent), dst_ref=sized(a, sent),
                                                     send_sem=send.at[3 * a + k], recv_sem=recv.at[3 * a + k],
                                                     device_id=(x, y, c), device_id_type=MESH).wait_send()
                    if got:
                        pltpu.make_async_remote_copy(src_ref=sized(a, got), dst_ref=sized(a, got),
                                                     send_sem=send.at[3 * a + k], recv_sem=recv.at[3 * a + k],
                                                     device_id=(x, y, c), device_id_type=MESH).wait_recv()

        _both_cores(run)

    return pl.pallas_call(
        body, name=name,
        out_shape=[jax.ShapeDtypeStruct(z.shape, z.dtype) for z in zeros],
        in_specs=[_HBM] * (2 * n), out_specs=[_HBM] * n, input_output_aliases={n + a: a for a in range(n)},
        scratch_shapes=[pltpu.SemaphoreType.DMA((3 * n,)), pltpu.SemaphoreType.DMA((3 * n,))]
        + [pltpu.VMEM((min(STAGE_ROWS, p.shape[0]),) + p.shape[1:], p.dtype) for p in parts]
        + [pltpu.SemaphoreType.DMA((n,)), pltpu.SemaphoreType.DMA((n,))],
    )(*parts, *zeros)


def _subset_exchange(names, bufs, l, name):
    n = len(bufs)
    lay = [LAYOUT[nm] for nm in names]

    def body(*refs):
        outs = refs[n:2 * n]
        send, recv = refs[2 * n:]
        x, y, c = _position()

        def run(j0, c0):
            for a in range(n):
                for s0, _, nr in _clip_pieces(lay[a], j0, c0):
                    rows = outs[a].at[l, pl.ds(s0, nr)]
                    pltpu.make_async_remote_copy(src_ref=rows, dst_ref=rows, send_sem=send.at[a], recv_sem=recv.at[a],
                                                 device_id=(x, y, 1 - c), device_id_type=MESH).start()
            for a in range(n):
                for c_half, wait_send in ((c0, True), (1 - c0, False)):
                    rows = _rows_of(_clip_pieces(lay[a], j0, c_half))
                    if rows:
                        ref = outs[a].at[l, pl.ds(0, rows)]
                        cp = pltpu.make_async_remote_copy(src_ref=ref, dst_ref=ref, send_sem=send.at[a], recv_sem=recv.at[a],
                                                          device_id=(x, y, 1 - c), device_id_type=MESH)
                        if wait_send:
                            cp.wait_send()
                        else:
                            cp.wait_recv()

        _both_cores(run)

    return pl.pallas_call(
        body, name=name,
        out_shape=[jax.ShapeDtypeStruct(b.shape, b.dtype) for b in bufs],
        in_specs=[_HBM] * n, out_specs=[_HBM] * n, input_output_aliases={a: a for a in range(n)},
        scratch_shapes=[pltpu.SemaphoreType.DMA((n,)), pltpu.SemaphoreType.DMA((n,))],
    )(*bufs)


def _scatter_start(names, gl, name):
    n = len(gl)
    lay = [LAYOUT[nm] for nm in names]
    zones = [lax.empty((N_CHIPS, lay[a][0]) + gl[a].shape[1:], gl[a].dtype) for a in range(n)]

    def body(*refs):
        ins, lands = refs[:n], refs[n:2 * n]
        send, recv, lsem = refs[2 * n:2 * n + 3]
        refs[-1][...] = jnp.zeros_like(refs[-1])
        x, y, c = _position()
        j = 2 * x + y
        chips = [(1 - x, y), (x, 1 - y), (1 - x, 1 - y)]
        for j0 in range(N_CHIPS):
            @pl.when(j == j0)
            def _(j0=j0):
                for a in range(n):
                    for s0, d0, nr in lay[a][2](j0):
                        pltpu.make_async_copy(ins[a].at[pl.ds(d0, nr)], lands[a].at[j0, pl.ds(s0, nr)], lsem.at[a]).start()
                    for k, chip in enumerate(chips):
                        for s0, d0, nr in lay[a][2](j0 ^ _REL[k]):
                            pltpu.make_async_remote_copy(
                                src_ref=ins[a].at[pl.ds(d0, nr)], dst_ref=lands[a].at[j0, pl.ds(s0, nr)],
                                send_sem=send.at[3 * a + k], recv_sem=recv.at[3 * a + k],
                                device_id=(*chip, c), device_id_type=MESH).start()

    sems = [pltpu.SemaphoreType.DMA((3 * n,)), pltpu.SemaphoreType.DMA((3 * n,)), pltpu.SemaphoreType.DMA((n,))]
    hbm = lambda a: pltpu.HBM(a.shape, a.dtype)
    res = pl.pallas_call(
        body, name=name,
        out_shape=sems + [hbm(g) for g in gl] + [hbm(z) for z in zones] + [_TOKEN],
        in_specs=[_HBM] * (2 * n), out_specs=[_SEM] * 3 + [_HBM] * (2 * n) + [_VMEM_TOKEN],
        input_output_aliases={a: 3 + a for a in range(2 * n)},
        compiler_params=pltpu.CompilerParams(has_side_effects=_EFFECT),
    )(*[pltpu.with_memory_space_constraint(g, pltpu.HBM) for g in gl],
      *[pltpu.with_memory_space_constraint(z, pltpu.HBM) for z in zones])
    return res[:3], res[3:3 + n], res[3 + n:3 + 2 * n], res[-1]


def _scatter_wait(names, sems, srcs, lands, after, name):
    n = len(srcs)
    lay = [LAYOUT[nm] for nm in names]

    def body(*refs):
        zones = refs[n:2 * n]
        send, recv, lsem = refs[2 * n:2 * n + 3]
        x, y, c = _position()
        for a in range(n):
            whole = zones[a].at[0, pl.ds(0, lay[a][0])]
            for k in range(3):
                cp = pltpu.make_async_remote_copy(src_ref=whole, dst_ref=whole, send_sem=send.at[3 * a + k],
                                                  recv_sem=recv.at[3 * a + k], device_id=(x, y, 1 - c),
                                                  device_id_type=MESH)
                cp.wait_send()
                cp.wait_recv()
            pltpu.make_async_copy(whole, whole, lsem.at[a]).wait()

    hbm = lambda a: pltpu.HBM(a.shape, a.dtype)
    res = pl.pallas_call(
        body, name=name,
        out_shape=[hbm(s) for s in srcs] + [hbm(z) for z in lands],
        in_specs=[_HBM] * (2 * n) + [_SEM] * 3 + [pl.BlockSpec(memory_space=pl.ANY)], out_specs=[_HBM] * (2 * n),
        input_output_aliases={a: a for a in range(2 * n)},
        compiler_params=pltpu.CompilerParams(has_side_effects=_EFFECT),
    )(*srcs, *lands, *sems, after)
    return res[n:]


def _sibling_swap(arrs, name):
    n = len(arrs)

    def body(*refs):
        ins, outs = refs[:n], refs[n:2 * n]
        send, recv = refs[2 * n:]
        x, y, c = _position()
        cps = [pltpu.make_async_remote_copy(src_ref=ins[a], dst_ref=outs[a], send_sem=send.at[a], recv_sem=recv.at[a],
                                            device_id=(x, y, 1 - c), device_id_type=MESH) for a in range(n)]
        for cp in cps:
            cp.start()
        for cp in cps:
            cp.wait()

    return pl.pallas_call(
        body, name=name,
        out_shape=[jax.ShapeDtypeStruct(a.shape, a.dtype) for a in arrs],
        in_specs=[_HBM] * n, out_specs=[_HBM] * n,
        scratch_shapes=[pltpu.SemaphoreType.DMA((n,)), pltpu.SemaphoreType.DMA((n,))],
    )(*arrs)


def _row_tile(r):
    for t in (256, 128, 64, 32, 16, 8):
        if r % t == 0 and r > t:
            return t
    return r


def _pair_add_half(g, rb, c_arr, name):
    hrows, rest = rb.shape[0], rb.shape[1:]
    tr = _row_tile(hrows)
    nb = hrows // tr
    z = (0,) * len(rest)

    def body(c_ref, g_ref, r_ref, o_ref):
        o_ref[...] = (g_ref[...].astype(F32) + r_ref[...].astype(F32)).astype(o_ref.dtype)

    return pl.pallas_call(
        body, name=name,
        grid_spec=pltpu.PrefetchScalarGridSpec(
            num_scalar_prefetch=1, grid=(nb,),
            in_specs=[pl.BlockSpec((tr,) + rest, lambda i, c_ref: (c_ref[0] * nb + i,) + z),
                      pl.BlockSpec((tr,) + rest, lambda i, c_ref: (i,) + z)],
            out_specs=pl.BlockSpec((tr,) + rest, lambda i, c_ref: (i,) + z)),
        out_shape=jax.ShapeDtypeStruct((hrows,) + rest, BF16),
        compiler_params=pltpu.CompilerParams(dimension_semantics=("parallel",), vmem_limit_bytes=VMEM_LIMIT),
    )(c_arr, g, rb)


def _sum_slabs(slabs, l, buf, name):
    m = len(slabs)
    n, R, rest = slabs[0].shape[0], slabs[0].shape[1], slabs[0].shape[2:]
    tr = _row_tile(R)
    z = (0,) * len(rest)

    def body(*refs):
        total = None
        for r_ref in refs[:m]:
            acc = r_ref[0].astype(F32)
            for k in range(1, n):
                acc = acc + r_ref[k].astype(F32)
            total = acc if total is None else total + acc
        refs[-1][...] = total

    if R // tr > 64 and len(rest) == 1 and rest[0] % 256 == 0:
        grid = (rest[0] // 256,)
        in_spec = pl.BlockSpec((n, R, 256), lambda i: (0, 0, i))
        out_spec = pl.BlockSpec((None, R, 256), lambda i: (l, 0, i))
    else:
        grid = (R // tr,)
        in_spec = pl.BlockSpec((n, tr) + rest, lambda i: (0, i) + z)
        out_spec = pl.BlockSpec((None, tr) + rest, lambda i: (l, i) + z)
    in_specs, args, aliases = [in_spec] * m, list(slabs), {}
    if buf is not None:
        in_specs.append(pl.BlockSpec(memory_space=pl.ANY))
        args.append(buf)
        aliases = {m: 0}
    return pl.pallas_call(
        body, name=name, grid=grid, in_specs=in_specs, out_specs=out_spec,
        out_shape=jax.ShapeDtypeStruct((DEPTH, R) + rest, F32), input_output_aliases=aliases,
        compiler_params=pltpu.CompilerParams(
            dimension_semantics=("parallel",),
            vmem_limit_bytes=_vmem(m * _nbytes(in_spec.block_shape, slabs[0].dtype) + _nbytes(out_spec.block_shape, F32),
                                   2 * _nbytes(out_spec.block_shape, F32))),
    )(*args)


def _adam_math(w, g, m, v):
    mn = ADAM_B1 * m + (1.0 - ADAM_B1) * g
    vn = ADAM_B2 * v + (1.0 - ADAM_B2) * (g * g)
    m_hat = mn / (1.0 - ADAM_B1 ** ADAM_STEP)
    v_hat = vn / (1.0 - ADAM_B2 ** ADAM_STEP)
    return -ADAM_LR * (m_hat / (jnp.sqrt(v_hat) + ADAM_EPS) + ADAM_WD * w), mn, vn


def _adamw(w, g, m, v, name):
    L, R, C = w.shape
    tr = _row_tile(R)

    def body(w_ref, g_ref, m_ref, v_ref, d_ref, mo_ref, vo_ref):
        d_ref[...], mo_ref[...], vo_ref[...] = _adam_math(w_ref[...], g_ref[...], m_ref[...], v_ref[...])

    if R // tr > 64 and C % 128 == 0:
        spec, grid = pl.BlockSpec((None, R, 128), lambda l, i: (l, 0, i)), (L, C // 128)
    else:
        spec, grid = pl.BlockSpec((None, tr, C), lambda l, i: (l, i, 0)), (L, R // tr)
    return pl.pallas_call(
        body, name=name, grid=grid, in_specs=[spec] * 4, out_specs=[spec] * 3,
        out_shape=[jax.ShapeDtypeStruct((L, R, C), F32)] * 3,
        compiler_params=pltpu.CompilerParams(dimension_semantics=("parallel", "parallel"),
                                             vmem_limit_bytes=_vmem(7 * _nbytes(spec.block_shape, F32))),
    )(w, g, m, v)


_VMEM_WHOLE = pl.BlockSpec(memory_space=pltpu.VMEM)


def _matrix_update(gath, w, m, v, name):
    K = w.shape[1]

    def body(g0_ref, g1_ref, w_ref, m_ref, v_ref, go_ref, d_ref, mo_ref, vo_ref):
        for l, gr in enumerate((g0_ref, g1_ref)):
            for k in range(K):
                g = gr[0, k].astype(F32)
                for dev in range(1, 8):
                    g = g + gr[dev, k].astype(F32)
                go_ref[l, k] = g
                d_ref[l, k], mo_ref[l, k], vo_ref[l, k] = _adam_math(w_ref[l, k], g, m_ref[l, k], v_ref[l, k])

    return pl.pallas_call(
        body, name=name, in_specs=[_VMEM_WHOLE] * 5, out_specs=[_VMEM_WHOLE] * 4,
        out_shape=[jax.ShapeDtypeStruct(w.shape, F32)] * 4,
        compiler_params=pltpu.CompilerParams(vmem_limit_bytes=32 * MIB),
    )(gath[0], gath[1], w, m, v)


VECS = (("pre_norm_g", D), ("post_norm_g", D), ("gm_ln_g", GM_W), ("gm_ln_b", GM_W), ("mla_q_norm_g", QR),
        ("mla_kv_norm_g", KVR), ("lru_conv_b", LRU_W), ("lru_b_a", LRU_W), ("lru_b_x", LRU_W), ("lru_lambda", LRU_W))
VEC_KEY = {"pre_norm_g": "pre_g", "post_norm_g": "post_g", "gm_ln_g": "ln_g", "gm_ln_b": "ln_b", "mla_q_norm_g": "qg",
           "mla_kv_norm_g": "kvg", "lru_conv_b": "conv_b", "lru_b_a": "ba", "lru_b_x": "bx", "lru_lambda": "lam"}
VEC_ROWS, VEC_W, VEC_ROW0, LOSS_ROW = 16, LRU_W, GM_G, 14


def _pack_rows(LG, loss_part):
    per = len(VECS) + 1
    ins = []
    for G in LG:
        ins += [G[VEC_KEY[n]] for n, _ in VECS] + [G["bst"]]
    ins.append(loss_part)

    def body(*refs):
        o_ref = refs[-1]
        o_ref[...] = jnp.zeros_like(o_ref)
        for l in range(DEPTH):
            base = VEC_ROWS * l
            o_ref[pl.ds(base, 8), pl.ds(0, GM_B)] = refs[per * l + len(VECS)][...].T[:8, :]
            for t, (_, width) in enumerate(VECS):
                o_ref[pl.ds(base + VEC_ROW0 + t, 1), pl.ds(0, width)] = refs[per * l + t][...]
        o_ref[pl.ds(LOSS_ROW, 1), pl.ds(0, 128)] = jnp.broadcast_to(refs[-2][...], (1, 128))

    return pl.pallas_call(
        body, name="pack_rows", in_specs=[_VMEM_WHOLE] * len(ins), out_specs=_VMEM_WHOLE,
        out_shape=jax.ShapeDtypeStruct((DEPTH * VEC_ROWS, VEC_W), F32),
    )(*ins)


def _vector_update(gath, W, M, V):
    names = [n for n, _ in VECS] + ["gm_bs"]
    nw = len(names)

    def body(*refs):
        g_ref = refs[0]
        wr, mr, vr = refs[1:1 + nw], refs[1 + nw:1 + 2 * nw], refs[1 + 2 * nw:1 + 3 * nw]
        outs = refs[1 + 3 * nw:]
        s = g_ref[0]
        for dev in range(1, 8):
            s = s + g_ref[dev]
        for t, (_, width) in enumerate(VECS):
            for l in range(DEPTH):
                r = VEC_ROWS * l + VEC_ROW0 + t
                g = s[r:r + 1, :width]
                row = (pl.ds(l, 1), slice(None))
                res = (g,) + _adam_math(wr[t][row], g, mr[t][row], vr[t][row])
                for q in range(4):
                    outs[4 * t + q][row] = res[q]
        t = len(VECS)
        for l in range(DEPTH):
            for k in range(GM_G):
                g = s[VEC_ROWS * l + k:VEC_ROWS * l + k + 1, :GM_B]
                row = (l, pl.ds(k, 1), slice(None))
                res = (g,) + _adam_math(wr[t][row], g, mr[t][row], vr[t][row])
                for q in range(4):
                    outs[4 * t + q][row] = res[q]
        outs[4 * nw][...] = s[LOSS_ROW:LOSS_ROW + 1, :128]

    ws = [W[n] for n in names]
    out_shape = []
    for w in ws:
        out_shape += [jax.ShapeDtypeStruct(w.shape, F32)] * 4
    out_shape.append(jax.ShapeDtypeStruct((1, 128), F32))
    res = pl.pallas_call(
        body, name="vector_update", in_specs=[_VMEM_WHOLE] * (1 + 3 * nw), out_specs=[_VMEM_WHOLE] * (4 * nw + 1),
        out_shape=out_shape, compiler_params=pltpu.CompilerParams(vmem_limit_bytes=VMEM_LIMIT),
    )(gath, *ws, *[M[n] for n in names], *[V[n] for n in names])
    return {n: tuple(res[4 * t:4 * t + 4]) for t, n in enumerate(names)}, res[4 * nw]


SHARDED = ("w_in", "mla_w_uq", "mla_w_ukv", "lru_conv_w", "w_proj_a", "w_proj_b", "w_proj_c", "w_out")
FIRST = ("w_in", "lru_conv_w")
LATER = tuple(n for n in SHARDED if n not in FIRST)
COL_SHARDED = ("w_in", "mla_w_uq", "mla_w_ukv", "lru_conv_w")
SMALL = ("pre_norm_g", "gm_ln_g", "gm_ln_b", "gm_ws", "gm_bs", "mla_q_norm_g", "mla_kv_norm_g", "lru_conv_b",
         "lru_w_a", "lru_b_a", "lru_w_x", "lru_b_x", "lru_lambda", "post_norm_g")
WEIGHTS = ("pre_norm_g", "w_in", "gm_ln_g", "gm_ln_b", "gm_ws", "gm_bs", "mla_q_norm_g", "mla_w_uq",
           "mla_kv_norm_g", "mla_w_ukv", "lru_conv_w", "lru_conv_b", "lru_w_a", "lru_b_a", "lru_w_x", "lru_b_x",
           "lru_lambda", "w_proj_a", "w_proj_b", "w_proj_c", "w_out", "post_norm_g")


GB_KEY = {"w_in": "wp", "mla_w_uq": "wuq", "mla_w_ukv": "wukv", "w_proj_a": "wpa", "w_proj_b": "wpb",
          "w_proj_c": "wpc", "w_out": "wout"}


def _prepare(l, gathered, small, wsb):
    P = {GB_KEY[n]: gathered[n] for n in GB_KEY if n in gathered}
    P["conv_w"] = gathered["lru_conv_w"].transpose(1, 0, 2).reshape(CONV_W, LRU_W)
    P["wsb"] = wsb
    row = lambda n: small[n][l][None, :]
    P["pre_g"], P["post_g"] = row("pre_norm_g"), row("post_norm_g")
    P["ln_g"], P["ln_b"] = row("gm_ln_g"), row("gm_ln_b")
    P["ws"] = small["gm_ws"][l]
    P["bst"] = jnp.pad(small["gm_bs"][l].T, ((0, 0), (0, 128 - GM_G)))
    P["qg"], P["kvg"] = row("mla_q_norm_g"), row("mla_kv_norm_g")
    P["conv_b"], P["ba"], P["bx"], P["lam"] = row("lru_conv_b"), row("lru_b_a"), row("lru_b_x"), row("lru_lambda")
    return P


def kernel(x, pre_norm_g, w_in, gm_ln_g, gm_ln_b, gm_ws, gm_bs, mla_q_norm_g, mla_w_uq, mla_kv_norm_g, mla_w_ukv, lru_conv_w, lru_conv_b, lru_w_a, lru_b_a, lru_w_x, lru_b_x, lru_lambda, w_proj_a, w_proj_b, w_proj_c, w_out, post_norm_g, loss_target, m_pre_norm_g, m_w_in, m_gm_ln_g, m_gm_ln_b, m_gm_ws, m_gm_bs, m_mla_q_norm_g, m_mla_w_uq, m_mla_kv_norm_g, m_mla_w_ukv, m_lru_conv_w, m_lru_conv_b, m_lru_w_a, m_lru_b_a, m_lru_w_x, m_lru_b_x, m_lru_lambda, m_w_proj_a, m_w_proj_b, m_w_proj_c, m_w_out, m_post_norm_g, v_pre_norm_g, v_w_in, v_gm_ln_g, v_gm_ln_b, v_gm_ws, v_gm_bs, v_mla_q_norm_g, v_mla_w_uq, v_mla_kv_norm_g, v_mla_w_ukv, v_lru_conv_w, v_lru_conv_b, v_lru_w_a, v_lru_b_a, v_lru_w_x, v_lru_b_x, v_lru_lambda, v_w_proj_a, v_w_proj_b, v_w_proj_c, v_w_out, v_post_norm_g):
    args = dict(locals())
    W = {n: args[n] for n in WEIGHTS}
    M = {n: args["m_" + n] for n in WEIGHTS}
    V = {n: args["v_" + n] for n in WEIGHTS}
    c = lax.axis_index("c")

    def shards(l, names):
        out = []
        for n in names:
            blk = W[n][l].T if n in TRANSPOSED else W[n][l]
            out.append(blk[None] if n == "lru_conv_w" else blk.astype(BF16))
        return out

    small = {n: W[n] for n in SMALL}
    wsb = _superblocks(W["lru_w_a"], W["lru_w_x"])
    tabs = _rope_tables()
    s0a, s0b, s1 = shards(0, FIRST), shards(0, LATER), shards(1, SHARDED)
    g0, zones = _weights_allgather(FIRST, s0a, "weights_allgather_l0",
                                   carry=_gather_zeros(LATER, s0b) + _gather_zeros(SHARDED, s1))
    sems0, srcs0, lands0, token0 = _gather_start(LATER, s0b, zones[:len(LATER)], "weights_gather_start_l0")
    sems, srcs1, lands1, token = _gather_start(SHARDED, s1, zones[len(LATER):], "weights_gather_start_l1", after=token0)

    def late0(after):
        got = _gather_wait(LATER, sems0, srcs0, lands0, after, "weights_gather_wait_l0")
        return {GB_KEY[n]: g for n, g in zip(LATER, got)}

    P = [_prepare(0, dict(zip(FIRST, g0)), small, wsb), None]
    h0 = x[0]
    h1, A0 = _layer_fwd(h0, P[0], 0, tabs, token, late0)
    g1 = dict(zip(SHARDED, _gather_wait(SHARDED, sems, srcs1, lands1, h1, "weights_gather_wait_l1")))
    P[1] = _prepare(1, g1, small, wsb)
    h2, A1 = _layer_fwd(h1, P[1], 1, tabs)
    dy, loss_part = _loss_fwd(h2, loss_target[0])
    def large_grads(G, GB):
        conv = G["conv_w"].reshape(CONV_W, N_CHIPS, LRU_W // N_CHIPS).transpose(1, 0, 2)
        return [conv if n == "lru_conv_w" else GB[GB_KEY[n]] for n in SHARDED]

    d1, G1, GB1 = _layer_bwd(dy, A1, P[1], 1, tabs)
    sems, srcs1, lands1, token = _scatter_start(SHARDED, large_grads(G1, GB1), "grads_scatter_start_l1")
    d0, G0, GB0 = _layer_bwd(d1, A0, P[0], 0, tabs, token)
    LG = (G0, G1)
    mine1 = _scatter_wait(SHARDED, sems, srcs1, lands1, d0, "grads_scatter_wait_l1")
    theirs1 = _sibling_swap(mine1, "partials_to_sibling_l1")
    both = [_sum_slabs([a, b], 1, None, "sum_partials_l1_" + n) for n, a, b in zip(SHARDED, mine1, theirs1)]
    g0l = large_grads(G0, GB0)
    c_arr = jnp.reshape(c, (1,)).astype(jnp.int32)
    from_sib = _half_to_sibling(SHARDED, g0l, "grads_half_to_sibling_l0")
    pair = [_pair_add_half(g, rb, c_arr, "pair_add_" + n) for n, g, rb in zip(SHARDED, g0l, from_sib)]
    slabs = _chip_scatter_half(SHARDED, pair, "grads_chip_scatter_l0")
    both = [_sum_slabs([s], 0, b, "sum_slabs_l0_" + n) for n, s, b in zip(SHARDED, slabs, both)]
    both = _subset_exchange(SHARDED, both, 0, "reduced_rows_to_sibling_l0")
    grads = {}
    for n, b in zip(SHARDED, both):
        if n in TRANSPOSED and n != "w_in":
            b = jnp.swapaxes(b, 1, 2)
        grads[n] = b if n == "w_in" else b.reshape(W[n].shape)

    rows = _pack_rows(LG, loss_part)
    mats = []
    for g in LG:
        mats += [g["ws"].astype(BF16), g["wab"][0, :, :, :LRU_BW], g["wab"][1, :, :, :LRU_BW]]
    gath = _allgather([rows] + mats, "small_grads_allgather")
    upd, loss_row = _vector_update(gath[0], W, M, V)
    loss = loss_row[0, 0]
    for k, n in enumerate(("gm_ws", "lru_w_a", "lru_w_x")):
        upd[n] = _matrix_update((gath[1 + k], gath[4 + k]), W[n], M[n], V[n], "update_" + n)

    for n in SHARDED:
        if n == "w_in":
            tr = lambda a: jnp.swapaxes(a, 1, 2)
            res = _adamw(tr(W[n]), grads[n], tr(M[n]), tr(V[n]), "adamw_" + n)
            upd[n] = tuple(tr(a) for a in (grads[n],) + tuple(res))
        else:
            upd[n] = (grads[n],) + tuple(_adamw(W[n], grads[n], M[n], V[n], "adamw_" + n))

    return (loss, d0[None], *[upd[n][0] for n in WEIGHTS], *[upd[n][1] for n in WEIGHTS],
            *[upd[n][2] for n in WEIGHTS], *[upd[n][3] for n in WEIGHTS])
```

```python
import functools
import math

import jax
import jax.numpy as jnp
from jax import lax
from jax.experimental import pallas as pl
from jax.experimental.pallas import tpu as pltpu

F32, BF16 = jnp.float32, jnp.bfloat16
MESH = pl.DeviceIdType.MESH

S, D, DEPTH = 2048, 1024, 2
CHUNK, EPS = 64, 1e-6
GM_W, GM_G, GM_B = 1024, 4, 128
H, NOPE, ROPE, VDIM = 8, 128, 64, 128
QR, KVR = 384, 256
MLA_W = H * VDIM
LRU_W, LRU_NB, LRU_BW, LRU_C, CONV_W = 1280, 16, 80, 8.0, 4
ROPE_THETA = 10000.0
IN_SIZES = (GM_W, GM_W, GM_W, QR, KVR, ROPE, MLA_W, LRU_W, LRU_W, D, D, D)
N_IN = sum(IN_SIZES)
N_CHIPS = 4
ADAM_LR, ADAM_B1, ADAM_B2, ADAM_EPS, ADAM_WD, ADAM_STEP = 0.001, 0.9, 0.999, 1e-08, 0.01, 10

HP = 256
O_U, O_V, O_ZA, O_GA, O_GB, O_GC = 0, 1024, 2048, 3072, 4096, 5120
O_CKV, O_KR, O_CQ, O_XC, O_ZC, O_ZB = 6144, 6400, 6528, 7680, 8960, 10240
NP = 11264
MIB = 1024 * 1024
VMEM_LIMIT = 16 * MIB


def _vmem(block_bytes, temp_bytes=0):
    return int(min(max(2 * block_bytes + temp_bytes + 4 * MIB, VMEM_LIMIT), 56 * MIB))


def _nbytes(shape, dtype):
    return math.prod(d for d in shape if d is not None) * jnp.dtype(dtype).itemsize


def _tile(dim, target):
    if dim <= target:
        return dim
    t = (target // 128) * 128
    while dim % t:
        t -= 128
    return t


def _sig(x):
    return jax.nn.sigmoid(x)


def _silu(x):
    return x * _sig(x)


def _dsilu(x):
    s = _sig(x)
    return s * (1.0 + x * (1.0 - s))


def _mm(a, b, mode, name, out_dtype=F32, tm=512, tn=512, tk=1024, b_lead=None, out_lead=None, token=None):
    b2 = b.shape[1:] if b_lead is not None else b.shape
    if mode == "nn":
        (M, K), (K2, N) = a.shape, b2
    elif mode == "nt":
        (M, K), (N, K2) = a.shape, b2
    else:
        (K, M), (K2, N) = a.shape, b2
    assert K == K2, (name, a.shape, b.shape)
    tm, tn, tk = _tile(M, tm), _tile(N, tn), _tile(K, tk)
    nk = K // tk
    if mode == "tn":
        a_spec = pl.BlockSpec((tk, tm), lambda i, j, k: (k, i))
        lhs_c = 0
    else:
        a_spec = pl.BlockSpec((tm, tk), lambda i, j, k: (i, k))
        lhs_c = 1
    b_blk, b_idx, rhs_c = ((tn, tk), (lambda i, j, k: (j, k)), 1) if mode == "nt" else ((tk, tn), (lambda i, j, k: (k, j)), 0)
    if b_lead is None:
        b_spec = pl.BlockSpec(b_blk, b_idx)
    else:
        b_spec = pl.BlockSpec((None,) + b_blk, functools.partial(lambda i, j, k, f, l: (l,) + f(i, j, k), f=b_idx, l=b_lead))
    dims = (((lhs_c,), (rhs_c,)), ((), ()))
    in_specs, args, aliases = [a_spec, b_spec], [a, b], {}
    if out_lead is None:
        out_spec = pl.BlockSpec((tm, tn), lambda i, j, k: (i, j))
        out_shape = jax.ShapeDtypeStruct((M, N), out_dtype)
    else:
        l_out, n_lead, buf = out_lead
        out_spec = pl.BlockSpec((None, tm, tn), functools.partial(lambda i, j, k, l: (l, i, j), l=l_out))
        out_shape = jax.ShapeDtypeStruct((n_lead, M, N), out_dtype)
        if buf is not None:
            in_specs.append(pl.BlockSpec(memory_space=pl.ANY))
            args.append(buf)
            aliases = {2: 0}
    if token is not None:
        in_specs.append(pl.BlockSpec(memory_space=pl.ANY))
        args.append(token)

    def body(a_ref, b_ref, *rest):
        o_ref, acc_ref = rest[-2:]
        k = pl.program_id(2)

        @pl.when(k == 0)
        def _():
            acc_ref[...] = jnp.zeros_like(acc_ref)

        acc_ref[...] += lax.dot_general(a_ref[...].astype(BF16), b_ref[...].astype(BF16), dims,
                                        preferred_element_type=F32)

        @pl.when(k == nk - 1)
        def _():
            o_ref[...] = acc_ref[...].astype(o_ref.dtype)

    return pl.pallas_call(
        body, name=name, grid=(M // tm, N // tn, nk),
        in_specs=in_specs, out_specs=out_spec, out_shape=out_shape,
        scratch_shapes=[pltpu.VMEM((tm, tn), F32)], input_output_aliases=aliases,
        compiler_params=pltpu.CompilerParams(
            dimension_semantics=("parallel", "parallel", "arbitrary"),
            vmem_limit_bytes=_vmem(_nbytes((tm, tk), a.dtype) + _nbytes((tk, tn), b.dtype) + _nbytes((tm, tn), out_dtype),
                                   _nbytes((tm, tn), F32) + _nbytes((tm, tk), BF16) + _nbytes((tk, tn), BF16))),
    )(*args)


def _rows(fn, name, tm, rows, halos=(), fulls=(), outs=(), accs=()):
    n = S // tm
    in_specs, args = [], []
    for arr, w, cb in rows:
        in_specs.append(pl.BlockSpec((tm, w), functools.partial(lambda i, cb: (i, cb), cb=cb)))
        args.append(arr)
    for arr, w, cb, side in halos:
        if side == "prev":
            im = functools.partial(lambda i, cb: (jnp.maximum(i * (tm // 16) - 1, 0), cb), cb=cb)
        else:
            im = functools.partial(lambda i, cb: (jnp.minimum((i + 1) * (tm // 16), S // 16 - 1), cb), cb=cb)
        in_specs.append(pl.BlockSpec((16, w), im))
        args.append(arr)
    for arr in fulls:
        in_specs.append(pl.BlockSpec(arr.shape, functools.partial(lambda i, nd: (0,) * nd, nd=arr.ndim)))
        args.append(arr)
    out_shape, out_specs, aliases, n_alias = [], [], {}, 0
    for k, o in enumerate(outs):
        if len(o) == 3 and o[2] == "T":
            out_shape.append(jax.ShapeDtypeStruct((o[0], S), o[1]))
            out_specs.append(pl.BlockSpec((o[0], tm), lambda i: (0, i)))
        elif len(o) == 3:
            buf, total, cb = o[2]
            out_shape.append(jax.ShapeDtypeStruct((S, total), o[1]))
            out_specs.append(pl.BlockSpec((tm, o[0]), functools.partial(lambda i, cb: (i, cb), cb=cb)))
            if buf is not None:
                aliases[len(args)] = k
                in_specs.append(pl.BlockSpec(memory_space=pl.ANY))
                args.append(buf)
                n_alias += 1
        else:
            out_shape.append(jax.ShapeDtypeStruct((S, o[0]), o[1]))
            out_specs.append(pl.BlockSpec((tm, o[0]), lambda i: (i, 0)))
    for shp in accs:
        out_shape.append(jax.ShapeDtypeStruct(shp, F32))
        out_specs.append(pl.BlockSpec(shp, functools.partial(lambda i, nd: (0,) * nd, nd=len(shp))))
    nr, nh, nf, no, na = len(rows), len(halos), len(fulls), len(outs), len(accs)
    blocks = (sum(_nbytes((tm, w), arr.dtype) for arr, w, _ in rows) + sum(_nbytes(a.shape, a.dtype) for a in fulls)
              + sum(_nbytes((tm, o[0]), o[1]) for o in outs) + sum(_nbytes(shp, F32) for shp in accs))
    widest = _nbytes((tm, max([w for _, w, _ in rows] + [o[0] for o in outs])), F32)

    def body(*refs):
        i = pl.program_id(0)
        ins, orefs = refs[:nr + nh + nf], refs[nr + nh + nf + n_alias:]
        rv = [r[...].astype(F32) for r in ins[:nr]]
        hv = [r[...].astype(F32)[8:] if h[3] == "prev" else r[...].astype(F32)[:8] for r, h in zip(ins[nr:nr + nh], halos)]
        fv = [r[...] for r in ins[nr + nh:]]
        o, a = fn(i, rv, hv, fv)
        assert len(o) == no and len(a) == na, name
        for spec, ref, val in zip(outs, orefs[:no], o):
            ref[...] = (val.T if len(spec) == 3 and spec[2] == "T" else val).astype(ref.dtype)
        if na:
            @pl.when(i == 0)
            def _():
                for ref in orefs[no:]:
                    ref[...] = jnp.zeros_like(ref)

            for ref, val in zip(orefs[no:], a):
                ref[...] += val

    res = pl.pallas_call(
        body, name=name, grid=(n,), in_specs=in_specs, out_specs=out_specs, out_shape=out_shape,
        input_output_aliases=aliases,
        compiler_params=pltpu.CompilerParams(dimension_semantics=("arbitrary",), vmem_limit_bytes=_vmem(blocks, 6 * widest)),
    )(*args)
    return res


def _shift_down(xb, halo, s, row):
    fix = jnp.tile(pltpu.roll(halo, s, 0), (xb.shape[0] // 8, 1))
    return jnp.where(row >= s, pltpu.roll(xb, s, 0), fix)


def _shift_up(xb, halo, s, row):
    tm = xb.shape[0]
    fix = jnp.tile(pltpu.roll(halo, 8 - s, 0), (tm // 8, 1))
    return jnp.where(row < tm - s, pltpu.roll(xb, tm - s, 0), fix)


def _rms(x):
    return lax.rsqrt(jnp.mean(x * x, axis=-1, keepdims=True) + EPS)


def _rms_bwd(dy, x, g):
    r = _rms(x)
    xh = x * r
    dxh = dy * g
    dx = r * (dxh - xh * jnp.mean(dxh * xh, axis=-1, keepdims=True))
    return dx, dy * xh


def _colsum(x):
    return jnp.sum(x, axis=0, keepdims=True)


def _prenorm_fwd(x, g, token=None):
    def fn(i, rv, hv, fv):
        return [rv[0] * _rms(rv[0]) * fv[0]], []
    return _rows(fn, "prenorm_fwd", 256, [(x, D, 0)], fulls=[g] + ([] if token is None else [token]), outs=[(D, BF16)])[0]


def _gm_mask():
    r = lax.broadcasted_iota(jnp.int32, (GM_B, GM_B), 0) // CHUNK
    c = lax.broadcasted_iota(jnp.int32, (GM_B, GM_B), 1) // CHUNK
    return c <= r


def _gm_norm(v, g, b):
    mu = jnp.mean(v, axis=-1, keepdims=True)
    vc = v - mu
    rs = lax.rsqrt(jnp.mean(vc * vc, axis=-1, keepdims=True) + EPS)
    vh = vc * rs
    return vh, rs, vh * g + b


def _gm_sv(vn, ws, bst):
    mask = _gm_mask()
    gw = GM_W // GM_G
    parts = []
    for g in range(GM_G):
        wm = jnp.where(mask, ws[g], 0.0).astype(BF16)
        parts.append(jnp.dot(wm, vn[:, g * gw:(g + 1) * gw].astype(BF16), preferred_element_type=F32)
                     + bst[:, g:g + 1])
    return jnp.concatenate(parts, axis=1)


def _gmlp_fwd(proj, ln_g, ln_b, ws, bst):
    def fn(i, rv, hv, fv):
        u, v, z = rv
        g, b, w, bt = fv
        _, _, vn = _gm_norm(v, g, b)
        return [u * _gm_sv(vn, w, bt) * _silu(z)], []
    return _rows(fn, "gmlp_fwd", GM_B, [(proj, GM_W, 0), (proj, GM_W, 1), (proj, GM_W, 2)],
                 fulls=[ln_g, ln_b, ws, bst], outs=[(GM_W, BF16)])[0]


def _mla_prep_fwd(proj, qg, kvg):
    def fn(i, rv, hv, fv):
        cq, ckv = rv
        g1, g2 = fv
        return [cq * _rms(cq) * g1, ckv * _rms(ckv) * g2], []
    return _rows(fn, "mla_prep_fwd", 256, [(proj, QR, O_CQ // QR), (proj, KVR, O_CKV // KVR)],
                 fulls=[qg, kvg], outs=[(QR, BF16), (KVR, BF16)])


def _rot(t, cc, sa, sb):
    return t * cc + pltpu.roll(t, 32, 1) * sa + pltpu.roll(t, 96, 1) * sb


def _rot_t(g, cc, sa, sb):
    return g * cc + pltpu.roll(g * sa, 96, 1) + pltpu.roll(g * sb, 32, 1)


def _rope_tables():
    pos = jnp.arange(S, dtype=F32)
    inv_freq = ROPE_THETA ** (-jnp.arange(0, ROPE, 2, dtype=F32) / ROPE)
    ang = pos[:, None] * inv_freq[None, :]
    cos, sin, z = jnp.cos(ang), jnp.sin(ang), jnp.zeros((S, 32), F32)
    cc = jnp.concatenate([cos, cos, z, z], axis=1)
    sa = jnp.concatenate([z, sin, z, z], axis=1)
    sb = jnp.concatenate([-sin, z, z, z], axis=1)
    return cc, sa, sb


ATT_SCALE = 1.0 / math.sqrt(NOPE + ROPE)


def _rope_fwd(q, kv, proj, tabs):
    def fn(i, rv, hv, fv):
        qb, kvb, kr, cc, sa, sb = rv
        krr = _rot(kr, cc, sa, sb)
        qs, ks = [], []
        for h in range(H):
            qs += [qb[:, h * HP:h * HP + 128] * ATT_SCALE, _rot(qb[:, h * HP + 128:(h + 1) * HP], cc, sa, sb) * ATT_SCALE]
            ks += [kvb[:, h * 128:(h + 1) * 128], krr]
        kc = jnp.concatenate(ks, axis=1)
        vv = kvb[:, H * NOPE:]
        return [jnp.concatenate(qs, axis=1), kc, kc, vv, vv], []
    cc, sa, sb = tabs
    return _rows(fn, "rope_fwd", 256,
                 [(q, H * HP, 0), (kv, H * 256, 0), (proj, 128, O_KR // 128), (cc, 128, 0), (sa, 128, 0), (sb, 128, 0)],
                 outs=[(H * HP, BF16), (H * HP, BF16), (H * HP, BF16, "T"), (MLA_W, BF16), (MLA_W, BF16, "T")])


TQ, TC, ATT_NB = 256, 128, 4
_NT = (((1,), (1,)), ((), ()))


def _attn_allowed(i, kc):
    kpos = kc * TC + lax.broadcasted_iota(jnp.int32, (TC, TQ), 0)
    qpos = i * TQ + lax.broadcasted_iota(jnp.int32, (TC, TQ), 1)
    return (kpos // CHUNK) <= (qpos // CHUNK)


def _attn_fwd(qc, kc, vt):
    def body(q_ref, k_ref, vt_ref, o_ref, l_ref):
        i = pl.program_id(1)
        q = q_ref[...]

        def scores(sb):
            t0s = [pl.multiple_of((sb * ATT_NB + c) * TC, TC) for c in range(ATT_NB)]
            return [lax.dot_general(k_ref[pl.ds(t0, TC), :], q, _NT, preferred_element_type=F32) for t0 in t0s]

        def block(sb, ss, carry, masked):
            m, l, acc = carry
            t0s = [pl.multiple_of((sb * ATT_NB + c) * TC, TC) for c in range(ATT_NB)]
            if masked:
                ss = [jnp.where(_attn_allowed(i, sb * ATT_NB + c), s, -1e30) for c, s in enumerate(ss)]
            m_new = m
            for s in ss:
                m_new = jnp.maximum(m_new, jnp.max(s, axis=0, keepdims=True))
            alpha = jnp.exp(m - m_new)
            ps = [jnp.exp(s - m_new) for s in ss]
            l = alpha * l
            acc = alpha * acc
            for t0, p in zip(t0s, ps):
                l = l + jnp.sum(p, axis=0, keepdims=True)
                acc = acc + jnp.dot(vt_ref[:, pl.ds(t0, TC)], p.astype(BF16), preferred_element_type=F32)
            return m_new, l, acc

        nsb = (i + 2) // 2
        c = (jnp.full((1, TQ), -1e30, F32), jnp.zeros((1, TQ), F32), jnp.zeros((VDIM, TQ), F32))

        def step(sb, sc):
            nxt = scores(sb + 1)
            return nxt, block(sb, sc[0], sc[1], False)

        ss, c = lax.fori_loop(0, nsb - 1, step, (scores(0), c))
        m, l, acc = block(nsb - 1, ss, c, True)
        o_ref[...] = (acc / l).T
        l_ref[...] = m + jnp.log(l)

    return pl.pallas_call(
        body, name="attn_fwd", grid=(H, S // TQ),
        in_specs=[pl.BlockSpec((TQ, HP), lambda h, i: (i, h)),
                  pl.BlockSpec((S, HP), lambda h, i: (0, h)),
                  pl.BlockSpec((VDIM, S), lambda h, i: (h, 0))],
        out_specs=[pl.BlockSpec((TQ, VDIM), lambda h, i: (i, h)), pl.BlockSpec((None, 1, TQ), lambda h, i: (h, 0, i))],
        out_shape=[jax.ShapeDtypeStruct((S, MLA_W), F32), jax.ShapeDtypeStruct((H, 1, S), F32)],
        compiler_params=pltpu.CompilerParams(dimension_semantics=("parallel", "arbitrary"),
                                             vmem_limit_bytes=24 * MIB),
    )(qc, kc, vt)


def _gate_mul_fwd(name, val, proj, width, cb):
    def fn(i, rv, hv, fv):
        o, z = rv
        return [o * _silu(z)], []
    return _rows(fn, name, 256, [(val, width, 0), (proj, width, cb)], outs=[(width, BF16)])[0]


def _conv_fwd(proj, w, b):
    def fn(i, rv, hv, fv):
        (xb,), (halo,), (ww, bb) = rv, hv, fv
        halo = jnp.where(i > 0, halo, 0.0)
        row = lax.broadcasted_iota(jnp.int32, xb.shape, 0)
        acc = bb + ww[3:4] * xb
        for s in range(1, CONV_W):
            acc = acc + ww[3 - s:4 - s] * _shift_down(xb, halo, s, row)
        return [acc, acc], []
    return _rows(fn, "conv_fwd", 128, [(proj, LRU_W, O_XC // LRU_W)], halos=[(proj, LRU_W, O_XC // LRU_W, "prev")],
                 fulls=[w, b], outs=[(LRU_W, F32), (LRU_W, BF16)])


def _lru_terms(ga, gx, xc, ba, bx, lam):
    r = _sig(ga + ba)
    ig = _sig(gx + bx)
    sp = jnp.maximum(-lam, 0.0) + jnp.log(1.0 + jnp.exp(-jnp.abs(lam)))
    log_a = -LRU_C * r * sp
    a = jnp.exp(log_a)
    e2 = jnp.exp(2.0 * log_a)
    om = 1.0 - e2
    mult = jnp.sqrt(jnp.maximum(om, 0.0))
    return r, ig, sp, a, e2, om, mult


def _lru_gates_fwd(gates, xc, ba, bx, lam):
    def fn(i, rv, hv, fv):
        ga, gx, x = rv
        r, ig, sp, a, e2, om, mult = _lru_terms(ga, gx, x, *fv)
        return [a, mult * (ig * x)], []
    return _rows(fn, "lru_gates_fwd", 128, [(gates, LRU_W, 0), (gates, LRU_W, 1), (xc, LRU_W, 0)],
                 fulls=[ba, bx, lam], outs=[(LRU_W, F32), (LRU_W, F32)])


SCAN_T, SCAN_CW = 64, 256


def _scan_fwd(a, b):
    def body(a_ref, b_ref, h_ref):
        row = lax.broadcasted_iota(jnp.int32, (SCAN_T, SCAN_CW), 0)

        def step(blk, hc):
            t0 = pl.multiple_of(blk * SCAN_T, SCAN_T)
            A = a_ref[pl.ds(t0, SCAN_T), :]
            B = b_ref[pl.ds(t0, SCAN_T), :]
            d = 1
            while d < SCAN_T:
                keep = row >= d
                A_s = jnp.where(keep, pltpu.roll(A, d, 0), 1.0)
                B_s = jnp.where(keep, pltpu.roll(B, d, 0), 0.0)
                B = A * B_s + B
                A = A * A_s
                d *= 2
            hh = A * hc + B
            h_ref[pl.ds(t0, SCAN_T), :] = hh
            return hh[SCAN_T - 1:SCAN_T, :]

        lax.fori_loop(0, S // SCAN_T, step, jnp.zeros((1, SCAN_CW), F32))

    spec = pl.BlockSpec((S, SCAN_CW), lambda j: (0, j))
    return pl.pallas_call(
        body, name="scan_fwd", grid=(LRU_W // SCAN_CW,), in_specs=[spec, spec], out_specs=spec,
        out_shape=jax.ShapeDtypeStruct((S, LRU_W), F32),
        compiler_params=pltpu.CompilerParams(dimension_semantics=("parallel",),
                                             vmem_limit_bytes=_vmem(3 * _nbytes((S, SCAN_CW), F32))),
    )(a, b)


def _merge_fwd(pa, pb, pc, proj):
    def fn(i, rv, hv, fv):
        a, b, c, ga, gb, gc = rv
        return [_sig(ga) * a + _sig(gb) * b + _sig(gc) * c], []
    return _rows(fn, "merge_fwd", 256,
                 [(pa, D, 0), (pb, D, 0), (pc, D, 0), (proj, D, O_GA // D), (proj, D, O_GB // D), (proj, D, O_GC // D)],
                 outs=[(D, BF16)])[0]


def _post_fwd(x, o2, g):
    def fn(i, rv, hv, fv):
        xb, ob = rv
        return [xb + ob * _rms(ob) * fv[0]], []
    return _rows(fn, "post_fwd", 256, [(x, D, 0), (o2, D, 0)], fulls=[g], outs=[(D, F32)])[0]


SB = 640
BD_TM = 512


def _bd_fwd(xcb, wsb, l):
    def body(x_ref, w_ref, o_ref):
        o_ref[...] = jnp.dot(x_ref[...], w_ref[...], preferred_element_type=F32).astype(o_ref.dtype)

    return pl.pallas_call(
        body, name="lru_gate_mm", grid=(S // BD_TM, 4),
        in_specs=[pl.BlockSpec((BD_TM, SB), lambda i, q: (i, q % 2)),
                  pl.BlockSpec((None, None, SB, SB), lambda i, q: (l, q, 0, 0))],
        out_specs=pl.BlockSpec((BD_TM, SB), lambda i, q: (i, q)),
        out_shape=jax.ShapeDtypeStruct((S, 2 * LRU_W), BF16),
        compiler_params=pltpu.CompilerParams(dimension_semantics=("parallel", "parallel"), vmem_limit_bytes=VMEM_LIMIT),
    )(xcb, wsb)


def _bd_dx(dgates, wsb, l):
    def body(d_ref, w_ref, o_ref, acc_ref):
        g = pl.program_id(2)

        @pl.when(g == 0)
        def _():
            acc_ref[...] = jnp.zeros_like(acc_ref)

        acc_ref[...] += lax.dot_general(d_ref[...], w_ref[...], (((1,), (1,)), ((), ())), preferred_element_type=F32)

        @pl.when(g == 1)
        def _():
            o_ref[...] = acc_ref[...].astype(o_ref.dtype)

    return pl.pallas_call(
        body, name="lru_gate_dx", grid=(S // BD_TM, 2, 2),
        in_specs=[pl.BlockSpec((BD_TM, SB), lambda i, s, g: (i, 2 * g + s)),
                  pl.BlockSpec((None, None, SB, SB), lambda i, s, g: (l, 2 * g + s, 0, 0))],
        out_specs=pl.BlockSpec((BD_TM, SB), lambda i, s, g: (i, s)),
        out_shape=jax.ShapeDtypeStruct((S, LRU_W), BF16),
        scratch_shapes=[pltpu.VMEM((BD_TM, SB), F32)],
        compiler_params=pltpu.CompilerParams(dimension_semantics=("parallel", "parallel", "arbitrary"),
                                             vmem_limit_bytes=VMEM_LIMIT),
    )(dgates, wsb)


def _bd_dw(xcb, dgates):
    tk = 1024

    def body(x_ref, d_ref, o_ref):
        @pl.when(pl.program_id(1) == 0)
        def _():
            o_ref[...] = jnp.zeros_like(o_ref)

        o_ref[...] += lax.dot_general(x_ref[...], d_ref[...], (((0,), (0,)), ((), ())), preferred_element_type=F32)

    return pl.pallas_call(
        body, name="lru_gate_dw", grid=(4, S // tk),
        in_specs=[pl.BlockSpec((tk, SB), lambda q, k: (k, q % 2)), pl.BlockSpec((tk, SB), lambda q, k: (k, q))],
        out_specs=pl.BlockSpec((None, SB, SB), lambda q, k: (q, 0, 0)),
        out_shape=jax.ShapeDtypeStruct((4, SB, SB), F32),
        compiler_params=pltpu.CompilerParams(dimension_semantics=("parallel", "arbitrary"), vmem_limit_bytes=VMEM_LIMIT),
    )(xcb, dgates)


def _bd_extract(dwsb):
    def body(w_ref, o_ref):
        lane = lax.broadcasted_iota(jnp.int32, (LRU_BW, 128), 1)
        for q in range(4):
            for kk in range(8):
                c0 = LRU_BW * kk
                w0, off = (c0 // 128) * 128, c0 % 128
                rows = pl.ds(LRU_BW * kk, LRU_BW)
                blk = w_ref[q, rows, w0:w0 + 128]
                if off:
                    blk = pltpu.roll(blk, 128 - off, 1)
                    if off + LRU_BW > 128:
                        nxt = pltpu.roll(w_ref[q, rows, w0 + 128:w0 + 256], 128 - off, 1)
                        blk = jnp.where(lane < 128 - off, blk, nxt)
                o_ref[q // 2, 8 * (q % 2) + kk] = blk.astype(BF16)

    return pl.pallas_call(
        body, name="lru_gate_dw_blocks",
        in_specs=[pl.BlockSpec(memory_space=pltpu.VMEM)], out_specs=pl.BlockSpec(memory_space=pltpu.VMEM),
        out_shape=jax.ShapeDtypeStruct((2, LRU_NB, LRU_BW, 128), BF16),
        compiler_params=pltpu.CompilerParams(vmem_limit_bytes=VMEM_LIMIT),
    )(dwsb)


def _layer_fwd(x, P, l, tabs, token=None, late=None):
    A = {"x": x}
    A["h"] = _prenorm_fwd(x, P["pre_g"], token)
    proj = A["proj"] = _mm(A["h"], P["wp"], "nt", "in_proj", out_dtype=BF16, tm=1024)
    A["ya"] = _gmlp_fwd(proj, P["ln_g"], P["ln_b"], P["ws"], P["bst"])
    A["xc"], A["xcb"] = _conv_fwd(proj, P["conv_w"], P["conv_b"])
    A["gates"] = _bd_fwd(A["xcb"], P["wsb"], l)
    A["a"], bterm = _lru_gates_fwd(A["gates"], A["xc"], P["ba"], P["bx"], P["lam"])
    A["hs"] = _scan_fwd(A["a"], bterm)
    A["yc"] = _gate_mul_fwd("yc_fwd", A["hs"], proj, LRU_W, O_ZC // LRU_W)
    if late is not None:
        P.update(late(A["yc"]))
    A["cqn"], A["ckvn"] = _mla_prep_fwd(proj, P["qg"], P["kvg"])
    q = _mm(A["cqn"], P["wuq"], "nt", "q_up", out_dtype=BF16)
    kv = _mm(A["ckvn"], P["wukv"], "nt", "kv_up", out_dtype=BF16)
    A["qc"], A["kc"], A["kct"], A["vv"], vt = _rope_fwd(q, kv, proj, tabs)
    A["o"], A["lse"] = _attn_fwd(A["qc"], A["kc"], vt)
    A["yb"] = _gate_mul_fwd("yb_fwd", A["o"], proj, MLA_W, O_ZB // MLA_W)
    A["pa"] = _mm(A["ya"], P["wpa"], "nn", "proj_a", out_dtype=BF16)
    A["pb"] = _mm(A["yb"], P["wpb"], "nn", "proj_b", out_dtype=BF16)
    A["pc"] = _mm(A["yc"], P["wpc"], "nn", "proj_c", out_dtype=BF16)
    A["merged"] = _merge_fwd(A["pa"], A["pb"], A["pc"], proj)
    A["o2"] = _mm(A["merged"], P["wout"], "nn", "out_proj")
    return _post_fwd(x, A["o2"], P["post_g"]), A


def _loss_fwd(y, tgt):
    def fn(i, rv, hv, fv):
        yb, tb = rv
        e = yb - tb
        part = 0.5 * jnp.sum(jnp.mean(e * e, axis=-1, keepdims=True), axis=0, keepdims=True)
        return [e * (1.0 / D)], [part]
    return _rows(fn, "loss", 256, [(y, D, 0), (tgt, D, 0)], outs=[(D, F32)], accs=[(1, 1)])


def _post_bwd(dxn, o2, g, token=None):
    def fn(i, rv, hv, fv):
        dy, ob = rv
        dx, dg = _rms_bwd(dy, ob, fv[0])
        return [dx], [_colsum(dg)]
    return _rows(fn, "post_bwd", 256, [(dxn, D, 0), (o2, D, 0)], fulls=[g] + ([] if token is None else [token]),
                 outs=[(D, BF16)], accs=[(1, D)])


def _merge_bwd(dm, pa, pb, pc, proj, dproj):
    def fn(i, rv, hv, fv):
        d, a, b, c, ga, gb, gc = rv
        outs_p, outs_g = [], []
        for p, gg in ((a, ga), (b, gb), (c, gc)):
            s = _sig(gg)
            outs_p.append(d * s)
            outs_g.append(d * p * s * (1.0 - s))
        return outs_p + [jnp.concatenate(outs_g, axis=1)], []
    return _rows(fn, "merge_bwd", 128,
                 [(dm, D, 0), (pa, D, 0), (pb, D, 0), (pc, D, 0),
                  (proj, D, O_GA // D), (proj, D, O_GB // D), (proj, D, O_GC // D)],
                 outs=[(D, BF16)] * 3 + [(3 * D, BF16, (dproj, NP, O_GA // (3 * D)))])


def _gmlp_bwd(dya, proj, ln_g, ln_b, ws, bst, dproj):
    gw = GM_W // GM_G

    def fn(i, rv, hv, fv):
        dy, u, v, z = rv
        g, b, w, bt = fv
        vh, rs, vn = _gm_norm(v, g, b)
        sv = _gm_sv(vn, w, bt)
        sz = _silu(z)
        du = dy * sv * sz
        dsv = dy * u * sz
        dz = dy * u * sv * _dsilu(z)
        mask = _gm_mask()
        lane = lax.broadcasted_iota(jnp.int32, (GM_B, 128), 1)
        dvn_parts, dws, dbst = [], [], jnp.zeros((GM_B, 128), F32)
        for k in range(GM_G):
            wm = jnp.where(mask, w[k], 0.0).astype(BF16)
            dsk = dsv[:, k * gw:(k + 1) * gw]
            dskb = dsk.astype(BF16)
            dvn_parts.append(lax.dot_general(wm, dskb, (((0,), (0,)), ((), ())), preferred_element_type=F32))
            dwk = lax.dot_general(dskb, vn[:, k * gw:(k + 1) * gw].astype(BF16), (((1,), (1,)), ((), ())),
                                  preferred_element_type=F32)
            dws.append(jnp.where(mask, dwk, 0.0)[None])
            dbst = dbst + jnp.where(lane == k, jnp.sum(dsk, axis=1, keepdims=True), 0.0)
        dvn = jnp.concatenate(dvn_parts, axis=1)
        dvh = dvn * g
        dv = rs * (dvh - jnp.mean(dvh, axis=-1, keepdims=True) - vh * jnp.mean(dvh * vh, axis=-1, keepdims=True))
        return ([jnp.concatenate([du, dv, dz], axis=1)],
                [jnp.concatenate(dws, axis=0), dbst, _colsum(dvn * vh), _colsum(dvn)])
    return _rows(fn, "gmlp_bwd", GM_B, [(dya, GM_W, 0), (proj, GM_W, 0), (proj, GM_W, 1), (proj, GM_W, 2)],
                 fulls=[ln_g, ln_b, ws, bst], outs=[(3 * GM_W, BF16, (dproj, NP, O_U // (3 * GM_W)))],
                 accs=[(GM_G, GM_B, GM_B), (GM_B, 128), (1, GM_W), (1, GM_W)])


def _yb_bwd(dyb, o, proj, dproj):
    def fn(i, rv, hv, fv):
        dy, ob, z = rv
        do = dy * _silu(z)
        prod = do * ob
        lane = lax.broadcasted_iota(jnp.int32, (dy.shape[0], 128), 1)
        dl = jnp.zeros((dy.shape[0], 128), F32)
        for h in range(H):
            dl = dl + jnp.where(lane == h, jnp.sum(prod[:, h * VDIM:(h + 1) * VDIM], axis=1, keepdims=True), 0.0)
        return [do, dl, dy * ob * _dsilu(z)], []
    return _rows(fn, "yb_bwd", 256, [(dyb, MLA_W, 0), (o, MLA_W, 0), (proj, MLA_W, O_ZB // MLA_W)],
                 outs=[(MLA_W, BF16), (128, F32, "T"), (MLA_W, BF16, (dproj, NP, O_ZB // MLA_W))])


def _attn_bwd(qc, kc, kct, vv, do, lse, dlt):
    def body(q_ref, k_ref, kt_ref, v_ref, do_ref, l_ref, d_ref, dq_ref, dk_ref, dv_ref, dqt_ref):
        h, i = pl.program_id(0), pl.program_id(1)

        @pl.when(i == 0)
        def _():
            dk_ref[...] = jnp.zeros_like(dk_ref)
            dv_ref[...] = jnp.zeros_like(dv_ref)

        q = q_ref[...]
        dob = do_ref[...]
        lse = l_ref[...]
        dl = d_ref[pl.ds(h, 1), :]
        dqt_ref[...] = jnp.zeros_like(dqt_ref)

        def rows_of(sb, c):
            return pl.ds(pl.multiple_of((sb * ATT_NB + c) * TC, TC), TC)

        def front(sb):
            return [(lax.dot_general(k_ref[rows_of(sb, c), :], q, _NT, preferred_element_type=F32),
                     lax.dot_general(v_ref[rows_of(sb, c), :], dob, _NT, preferred_element_type=F32))
                    for c in range(ATT_NB)]

        def block(sb, sd, masked):
            dqt = None
            for c, (s, dp) in enumerate(sd):
                rows = rows_of(sb, c)
                p = jnp.exp(s - lse)
                if masked:
                    p = jnp.where(_attn_allowed(i, sb * ATT_NB + c), p, 0.0)
                ds = (p * (dp - dl)).astype(BF16)
                dk_ref[rows, :] += jnp.dot(ds, q, preferred_element_type=F32)
                dv_ref[rows, :] += jnp.dot(p.astype(BF16), dob, preferred_element_type=F32)
                part = jnp.dot(kt_ref[:, rows], ds, preferred_element_type=F32)
                dqt = part if dqt is None else dqt + part
            dqt_ref[...] += dqt

        def step(sb, sd):
            nxt = front(sb + 1)
            block(sb, sd, False)
            return nxt

        nsb = (i + 2) // 2
        sd = lax.fori_loop(0, nsb - 1, step, front(0))
        block(nsb - 1, sd, True)
        dq_ref[...] = dqt_ref[...].T.astype(dq_ref.dtype)

    blk = lambda w: pl.BlockSpec((TQ, w), lambda h, i: (i, h))
    head = lambda w: pl.BlockSpec((S, w), lambda h, i: (0, h))
    return pl.pallas_call(
        body, name="attn_bwd", grid=(H, S // TQ),
        in_specs=[blk(HP), head(HP), pl.BlockSpec((HP, S), lambda h, i: (h, 0)), head(VDIM), blk(VDIM),
                  pl.BlockSpec((None, 1, TQ), lambda h, i: (h, 0, i)), pl.BlockSpec((8, TQ), lambda h, i: (0, i))],
        out_specs=[blk(HP), head(HP), head(VDIM)],
        out_shape=[jax.ShapeDtypeStruct((S, H * HP), BF16), jax.ShapeDtypeStruct((S, H * HP), F32),
                   jax.ShapeDtypeStruct((S, MLA_W), F32)],
        scratch_shapes=[pltpu.VMEM((HP, TQ), F32)],
        compiler_params=pltpu.CompilerParams(dimension_semantics=("parallel", "arbitrary"),
                                             vmem_limit_bytes=28 * MIB),
    )(qc, kc, kct, vv, do, lse, dlt)


def _rope_bwd(dqc, dkc, dvv, tabs):
    def fn(i, rv, hv, fv):
        dq, dk, dv, cc, sa, sb = rv
        qs, ks = [], []
        dkr = jnp.zeros((dq.shape[0], 128), F32)
        for h in range(H):
            qs += [dq[:, h * HP:h * HP + 128] * ATT_SCALE, _rot_t(dq[:, h * HP + 128:(h + 1) * HP], cc, sa, sb) * ATT_SCALE]
            ks.append(dk[:, h * HP:h * HP + 128])
            dkr = dkr + dk[:, h * HP + 128:(h + 1) * HP]
        return [jnp.concatenate(qs, axis=1), jnp.concatenate(ks + [dv], axis=1), _rot_t(dkr, cc, sa, sb)], []
    cc, sa, sb = tabs
    return _rows(fn, "rope_bwd", 256,
                 [(dqc, H * HP, 0), (dkc, H * HP, 0), (dvv, MLA_W, 0), (cc, 128, 0), (sa, 128, 0), (sb, 128, 0)],
                 outs=[(H * HP, BF16), (H * 256, BF16), (128, BF16)])


MLA_GROUP = 1536


def _mla_prep_bwd(dcqn, dckvn, dkr, proj, qg, kvg, dproj):
    def fn(i, rv, hv, fv):
        d1, d2, dk, cq, ckv = rv
        g1, g2 = fv
        dx1, dg1 = _rms_bwd(d1, cq, g1)
        dx2, dg2 = _rms_bwd(d2, ckv, g2)
        zeros = jnp.zeros((d1.shape[0], MLA_GROUP - KVR - 128 - QR), F32)
        return [jnp.concatenate([dx2, dk.astype(F32), dx1, zeros], axis=1)], [_colsum(dg1), _colsum(dg2)]
    return _rows(fn, "mla_prep_bwd", 256,
                 [(dcqn, QR, 0), (dckvn, KVR, 0), (dkr, 128, 0), (proj, QR, O_CQ // QR), (proj, KVR, O_CKV // KVR)],
                 fulls=[qg, kvg], outs=[(MLA_GROUP, BF16, (dproj, NP, O_CKV // MLA_GROUP))], accs=[(1, QR), (1, KVR)])


def _yc_bwd(dyc, hs, proj, dproj):
    def fn(i, rv, hv, fv):
        dy, hh, z = rv
        return [dy * _silu(z), dy * hh * _dsilu(z)], []
    return _rows(fn, "yc_bwd", 128, [(dyc, LRU_W, 0), (hs, LRU_W, 0), (proj, LRU_W, O_ZC // LRU_W)],
                 outs=[(LRU_W, F32), (LRU_W, BF16, (dproj, NP, O_ZC // LRU_W))])


def _scan_bwd(a, hs, dh):
    nblk = S // SCAN_T

    def body(a_ref, h_ref, dh_ref, da_ref, db_ref):
        row = lax.broadcasted_iota(jnp.int32, (SCAN_T, SCAN_CW), 0)

        def step(j, carry):
            gc, ac = carry
            blk = nblk - 1 - j
            t0 = pl.multiple_of(blk * SCAN_T, SCAN_T)
            av = a_ref[pl.ds(t0, SCAN_T), :]
            A = jnp.where(row < SCAN_T - 1, pltpu.roll(av, SCAN_T - 1, 0), ac)
            B = dh_ref[pl.ds(t0, SCAN_T), :]
            d = 1
            while d < SCAN_T:
                keep = row < SCAN_T - d
                A_s = jnp.where(keep, pltpu.roll(A, SCAN_T - d, 0), 1.0)
                B_s = jnp.where(keep, pltpu.roll(B, SCAN_T - d, 0), 0.0)
                B = A * B_s + B
                A = A * A_s
                d *= 2
            g = A * gc + B
            p0 = pl.multiple_of(jnp.maximum(t0 - 8, 0), 8)
            last = jnp.where(blk > 0, h_ref[pl.ds(p0, 8), :][7:8, :], 0.0)
            h_prev = jnp.where(row >= 1, pltpu.roll(h_ref[pl.ds(t0, SCAN_T), :], 1, 0), last)
            da_ref[pl.ds(t0, SCAN_T), :] = g * h_prev
            db_ref[pl.ds(t0, SCAN_T), :] = g
            return g[0:1, :], av[0:1, :]

        z = jnp.zeros((1, SCAN_CW), F32)
        lax.fori_loop(0, nblk, step, (z, z))

    spec = pl.BlockSpec((S, SCAN_CW), lambda j: (0, j))
    return pl.pallas_call(
        body, name="scan_bwd", grid=(LRU_W // SCAN_CW,), in_specs=[spec] * 3, out_specs=[spec] * 2,
        out_shape=[jax.ShapeDtypeStruct((S, LRU_W), F32)] * 2,
        compiler_params=pltpu.CompilerParams(dimension_semantics=("parallel",),
                                             vmem_limit_bytes=_vmem(5 * _nbytes((S, SCAN_CW), F32))),
    )(a, hs, dh)


def _lru_gates_bwd(da, db, gates, xc, ba, bx, lam):
    def fn(i, rv, hv, fv):
        dav, dbv, ga, gx, x = rv
        bav, bxv, lamv = fv
        r, ig, sp, a, e2, om, mult = _lru_terms(ga, gx, x, bav, bxv, lamv)
        dmult = dbv * ig * x
        dig = dbv * mult * x
        dxc1 = dbv * mult * ig
        dlog_a = dav * a + jnp.where(om > 0.0, dmult * (-e2 / mult), 0.0)
        dr = dlog_a * (-LRU_C * sp)
        dga = dr * r * (1.0 - r)
        dgx = dig * ig * (1.0 - ig)
        dlam = _colsum(dlog_a * (-LRU_C * r)) * (-_sig(-lamv))
        return [jnp.concatenate([dga, dgx], axis=1), dxc1], [_colsum(dga), _colsum(dgx), dlam]
    return _rows(fn, "lru_gates_bwd", 128,
                 [(da, LRU_W, 0), (db, LRU_W, 0), (gates, LRU_W, 0), (gates, LRU_W, 1), (xc, LRU_W, 0)],
                 fulls=[ba, bx, lam], outs=[(2 * LRU_W, BF16), (LRU_W, F32)], accs=[(1, LRU_W)] * 3)


def _conv_bwd(dxc1, dxc2, proj, w, dproj):
    cb = O_XC // LRU_W

    def fn(i, rv, hv, fv):
        d1, d2, xb = rv
        n1, n2, xprev = hv
        ww = fv[0]
        last = i == S // 128 - 1
        dxc = d1 + d2
        nxt = jnp.where(last, 0.0, n1 + n2)
        xprev = jnp.where(i > 0, xprev, 0.0)
        row = lax.broadcasted_iota(jnp.int32, xb.shape, 0)
        dx = ww[3:4] * dxc
        dws = [None] * CONV_W
        dws[3] = _colsum(dxc * xb)
        for s in range(1, CONV_W):
            dx = dx + ww[3 - s:4 - s] * _shift_up(dxc, nxt, s, row)
            dws[3 - s] = _colsum(dxc * _shift_down(xb, xprev, s, row))
        return [dx], [jnp.concatenate(dws, axis=0), _colsum(dxc)]
    return _rows(fn, "conv_bwd", 128, [(dxc1, LRU_W, 0), (dxc2, LRU_W, 0), (proj, LRU_W, cb)],
                 halos=[(dxc1, LRU_W, 0, "next"), (dxc2, LRU_W, 0, "next"), (proj, LRU_W, cb, "prev")],
                 fulls=[w], outs=[(LRU_W, BF16, (dproj, NP, cb))], accs=[(CONV_W, LRU_W), (1, LRU_W)])


def _prenorm_bwd(dxn, dh, x, g):
    def fn(i, rv, hv, fv):
        dy, dhh, xb = rv
        dx, dg = _rms_bwd(dhh, xb, fv[0])
        return [dy + dx], [_colsum(dg)]
    return _rows(fn, "prenorm_bwd", 256, [(dxn, D, 0), (dh, D, 0), (x, D, 0)], fulls=[g], outs=[(D, F32)],
                 accs=[(1, D)])


def _layer_bwd(dxn, A, P, l, tabs, token=None, early=None):
    G, GB = {}, {}
    proj = A["proj"]

    def dw(key, a, b, name, **tiles):
        GB[key] = _mm(a, b, "tn", name, out_dtype=BF16, **tiles)

    do2, G["post_g"] = _post_bwd(dxn, A["o2"], P["post_g"], token)
    dm = _mm(do2, P["wout"], "nt", "out_proj_dx", out_dtype=BF16)
    dw("wout", A["merged"], do2, "out_proj_dw")
    dpa, dpb, dpc, dproj = _merge_bwd(dm, A["pa"], A["pb"], A["pc"], proj, None)
    dya = _mm(dpa, P["wpa"], "nt", "proj_a_dx", out_dtype=BF16)
    dw("wpa", A["ya"], dpa, "proj_a_dw")
    dyb = _mm(dpb, P["wpb"], "nt", "proj_b_dx", out_dtype=BF16)
    dw("wpb", A["yb"], dpb, "proj_b_dw")
    dyc = _mm(dpc, P["wpc"], "nt", "proj_c_dx", out_dtype=BF16)
    dw("wpc", A["yc"], dpc, "proj_c_dw")
    dproj, G["ws"], G["bst"], G["ln_g"], G["ln_b"] = _gmlp_bwd(dya, proj, P["ln_g"], P["ln_b"], P["ws"], P["bst"], dproj)
    do, dl, dproj = _yb_bwd(dyb, A["o"], proj, dproj)
    dqc, dkc, dvv = _attn_bwd(A["qc"], A["kc"], A["kct"], A["vv"], do, A["lse"], dl)
    dq, dkv, dkr = _rope_bwd(dqc, dkc, dvv, tabs)
    dcqn = _mm(dq, P["wuq"], "nn", "q_up_dx", out_dtype=BF16)
    dw("wuq", dq, A["cqn"], "q_up_dw")
    dckvn = _mm(dkv, P["wukv"], "nn", "kv_up_dx", out_dtype=BF16)
    dw("wukv", dkv, A["ckvn"], "kv_up_dw")
    dproj, G["qg"], G["kvg"] = _mla_prep_bwd(dcqn, dckvn, dkr, proj, P["qg"], P["kvg"], dproj)
    dhs, dproj = _yc_bwd(dyc, A["hs"], proj, dproj)
    da, db = _scan_bwd(A["a"], A["hs"], dhs)
    dgates, dxc1, G["ba"], G["bx"], G["lam"] = _lru_gates_bwd(da, db, A["gates"], A["xc"], P["ba"], P["bx"], P["lam"])
    dxc2 = _bd_dx(dgates, P["wsb"], l)
    G["wab"] = _bd_extract(_bd_dw(A["xcb"], dgates))
    dproj, G["conv_w"], G["conv_b"] = _conv_bwd(dxc1, dxc2, proj, P["conv_w"], dproj)
    tok = (None, None) if early is None else early(GB)
    dh = _mm(dproj, P["wp"], "nn", "in_proj_dx", tm=1024, tn=1024, token=tok[0])
    dw("wp", dproj, A["h"], "in_proj_dw", tm=1536, tn=1024, token=tok[1])
    dx, G["pre_g"] = _prenorm_bwd(dxn, dh, A["x"], P["pre_g"])
    return dx, G, GB


_ORIG_OFF = [0]
for _s in IN_SIZES:
    _ORIG_OFF.append(_ORIG_OFF[-1] + _s)
_PAD_OFF = {0: O_U, 1: O_V, 2: O_ZA, 3: O_CQ, 4: O_CKV, 5: O_KR, 6: O_ZB, 7: O_XC, 8: O_ZC, 9: O_GA, 10: O_GB, 11: O_GC}
SHARD_IN = N_IN // N_CHIPS


def _pieces_w_in(j):
    lo, hi = SHARD_IN * j, SHARD_IN * (j + 1)
    out = []
    for k in range(len(IN_SIZES)):
        a, b = max(lo, _ORIG_OFF[k]), min(hi, _ORIG_OFF[k + 1])
        if a < b:
            out.append((a - lo, _PAD_OFF[k] + a - _ORIG_OFF[k], b - a))
    return out


def _pieces_uq(j):
    return [(192 * hh, HP * (2 * j + hh), NOPE + ROPE) for hh in range(2)]


def _pieces_ukv(j):
    out = []
    for hh in range(2):
        h = 2 * j + hh
        out += [(256 * hh, NOPE * h, NOPE), (256 * hh + NOPE, H * NOPE + VDIM * h, VDIM)]
    return out


def _pieces_rows(r):
    return lambda j: [(0, r * j, r)]


LAYOUT = {
    "w_in": (SHARD_IN, NP, _pieces_w_in),
    "mla_w_uq": (2 * (NOPE + ROPE), H * HP, _pieces_uq),
    "mla_w_ukv": (2 * (NOPE + VDIM), 2 * H * 128, _pieces_ukv),
    "lru_conv_w": (1, N_CHIPS, _pieces_rows(1)),
    "w_proj_a": (GM_W // N_CHIPS, GM_W, _pieces_rows(GM_W // N_CHIPS)),
    "w_proj_b": (MLA_W // N_CHIPS, MLA_W, _pieces_rows(MLA_W // N_CHIPS)),
    "w_proj_c": (LRU_W // N_CHIPS, LRU_W, _pieces_rows(LRU_W // N_CHIPS)),
    "w_out": (D // N_CHIPS, D, _pieces_rows(D // N_CHIPS)),
}
TRANSPOSED = ("w_in", "mla_w_uq", "mla_w_ukv")


def _superblocks(w_a, w_x):
    w6 = jnp.stack([w_a, w_x], axis=1).reshape(DEPTH, 4, 8, LRU_BW, LRU_BW).astype(BF16)
    bands = [jnp.pad(w6[:, :, k], ((0, 0), (0, 0), (0, 0), (LRU_BW * k, SB - LRU_BW * (k + 1)))) for k in range(8)]
    return jnp.concatenate(bands, axis=2)


_HBM = pl.BlockSpec(memory_space=pltpu.HBM)


def _position():
    return lax.axis_index("x"), lax.axis_index("y"), lax.axis_index("c")


def _allgather(blocks, name):
    n = len(blocks)

    def body(*refs):
        ins, outs = refs[:n], refs[n:2 * n]
        send, recv, lsem = refs[2 * n:]
        x, y, c = _position()
        me, sib = (x, y, c), (x, y, 1 - c)
        chips = [(1 - x, y), (x, 1 - y), (1 - x, 1 - y)]

        def cp(k, a, block, to, src=None):
            dst = outs[a].at[4 * block[0] + 2 * block[1] + block[2]]
            return pltpu.make_async_remote_copy(src_ref=dst if src is None else src, dst_ref=dst,
                                                send_sem=send.at[7 * a + k], recv_sem=recv.at[7 * a + k],
                                                device_id=to, device_id_type=MESH)

        mine = [pltpu.make_async_copy(ins[a], outs[a].at[4 * x + 2 * y + c], lsem.at[a]) for a in range(n)]
        for m in mine:
            m.start()
        first = []
        for a in range(n):
            first.append(cp(0, a, me, sib, src=ins[a]))
            first += [cp(1 + j, a, me, (*chip, c), src=ins[a]) for j, chip in enumerate(chips)]
        for f in first:
            f.start()
        passed = []
        for j, chip in enumerate(chips):
            for a in range(n):
                cp(1 + j, a, (*chip, c), me).wait_recv()
                p = cp(4 + j, a, (*chip, c), sib)
                p.start()
                passed.append(p)
        for a in range(n):
            cp(0, a, sib, me).wait_recv()
            for j, chip in enumerate(chips):
                cp(4 + j, a, (*chip, 1 - c), me).wait_recv()
        for f in first + passed:
            f.wait_send()
        for m in mine:
            m.wait()

    return pl.pallas_call(
        body, name=name,
        out_shape=[jax.ShapeDtypeStruct((8,) + b.shape, b.dtype) for b in blocks],
        in_specs=[_HBM] * n, out_specs=[_HBM] * n,
        scratch_shapes=[pltpu.SemaphoreType.DMA((7 * n,)), pltpu.SemaphoreType.DMA((7 * n,)),
                        pltpu.SemaphoreType.DMA((n,))],
    )(*blocks)


_REL = (2, 1, 3)


def _cut(r):
    return r if r < 32 else (r // 2 + 15) // 16 * 16


def _half_rows(r, c0):
    return _cut(r) if c0 == 0 else r - _cut(r)


def _half_pieces(lay_a, jsrc, c0):
    r = lay_a[0]
    lo, hi = (0, _cut(r)) if c0 == 0 else (_cut(r), r)
    out = []
    for s0, d0, nr in lay_a[2](jsrc):
        a, b = max(s0, lo), min(s0 + nr, hi)
        if a < b:
            out.append((a, d0 + a - s0, b - a))
    return out


def _gather_zeros(names, srcs):
    return [jnp.zeros((LAYOUT[nm][1],) + s.shape[1:], s.dtype) for nm, s in zip(names, srcs)]


def _weights_allgather(names, srcs, name, carry=()):
    n = len(srcs)
    lay = [LAYOUT[nm] for nm in names]
    zeros = _gather_zeros(names, srcs)
    m = len(carry)

    def body(*refs):
        ins, outs = refs[:n], refs[2 * n + m:3 * n + m]
        send, recv, lsem = refs[3 * n + 2 * m:]
        x, y, c = _position()
        j = 2 * x + y
        sib = (x, y, 1 - c)
        chips = [(1 - x, y), (x, 1 - y), (1 - x, 1 - y)]

        def flow(a, k, jsrc, c0, to, from_src):
            cps = []
            for s0, d0, nr in _half_pieces(lay[a], jsrc, c0):
                dst = outs[a].at[pl.ds(d0, nr)]
                src = ins[a].at[pl.ds(s0, nr)] if from_src else dst
                cps.append(pltpu.make_async_remote_copy(src_ref=src, dst_ref=dst, send_sem=send.at[7 * a + k],
                                                        recv_sem=recv.at[7 * a + k], device_id=to, device_id_type=MESH))
            return cps

        def sized(a, k, rows):
            ref = ins[a].at[pl.ds(0, rows)]
            return pltpu.make_async_remote_copy(src_ref=ref, dst_ref=ref, send_sem=send.at[7 * a + k],
                                                recv_sem=recv.at[7 * a + k], device_id=sib, device_id_type=MESH)

        for j0 in range(N_CHIPS):
            for c0 in range(2):
                @pl.when((j == j0) & (c == c0))
                def _(j0=j0, c0=c0):
                    mine = [_half_rows(lay[a][0], c0) for a in range(n)]
                    theirs = [_half_rows(lay[a][0], 1 - c0) for a in range(n)]
                    for a in range(n):
                        for s0, d0, nr in _half_pieces(lay[a], j0, c0):
                            pltpu.make_async_copy(ins[a].at[pl.ds(s0, nr)], outs[a].at[pl.ds(d0, nr)], lsem.at[a]).start()
                    for a in range(n):
                        for cp in flow(a, 0, j0, c0, sib, True):
                            cp.start()
                        for k, chip in enumerate(chips):
                            for cp in flow(a, 1 + k, j0, c0, (*chip, c), True):
                                cp.start()
                    for k in range(3):
                        for a in range(n):
                            if mine[a]:
                                sized(a, 1 + k, mine[a]).wait_recv()
                                for cp in flow(a, 4 + k, j0 ^ _REL[k], c0, sib, False):
                                    cp.start()
                    for a in range(n):
                        if theirs[a]:
                            sized(a, 0, theirs[a]).wait_recv()
                            for k in range(3):
                                sized(a, 4 + k, theirs[a]).wait_recv()
                    for a in range(n):
                        if mine[a]:
                            for k in range(7):
                                sized(a, k, mine[a]).wait_send()
                            ref = ins[a].at[pl.ds(0, mine[a])]
                            pltpu.make_async_copy(ref, ref, lsem.at[a]).wait()

    res = pl.pallas_call(
        body, name=name,
        out_shape=[jax.ShapeDtypeStruct(z.shape, z.dtype) for z in list(zeros) + list(carry)],
        in_specs=[_HBM] * (2 * n + m), out_specs=[_HBM] * (n + m),
        input_output_aliases={n + a: a for a in range(n + m)},
        scratch_shapes=[pltpu.SemaphoreType.DMA((7 * n,)), pltpu.SemaphoreType.DMA((7 * n,)),
                        pltpu.SemaphoreType.DMA((n,))],
    )(*srcs, *zeros, *carry)
    return res[:n], res[n:]


_SEM = pl.BlockSpec(memory_space=pltpu.SEMAPHORE)
_VMEM_TOKEN = pl.BlockSpec(memory_space=pltpu.VMEM)
_TOKEN = jax.ShapeDtypeStruct((8, 128), F32)
_EFFECT = pltpu.SideEffectType.DATAFLOW_SIDE_EFFECTING


def _gather_start(names, srcs, zeros, name, after=None):
    n = len(srcs)
    lay = [LAYOUT[nm] for nm in names]
    extra = [] if after is None else [after]

    def body(*refs):
        ins, lands = refs[:n], refs[n:2 * n]
        send, recv, lsem = refs[2 * n + len(extra):2 * n + len(extra) + 3]
        refs[-1][...] = jnp.zeros_like(refs[-1])
        x, y, c = _position()
        j = 2 * x + y
        chips = [(1 - x, y), (x, 1 - y), (1 - x, 1 - y)]
        for j0 in range(N_CHIPS):
            @pl.when(j == j0)
            def _(j0=j0):
                for a in range(n):
                    for s0, d0, nr in lay[a][2](j0):
                        src, dst = ins[a].at[pl.ds(s0, nr)], lands[a].at[pl.ds(d0, nr)]
                        pltpu.make_async_copy(src, dst, lsem.at[a]).start()
                        for k, chip in enumerate(chips):
                            pltpu.make_async_remote_copy(src_ref=src, dst_ref=dst, send_sem=send.at[3 * a + k],
                                                         recv_sem=recv.at[3 * a + k], device_id=(*chip, c),
                                                         device_id_type=MESH).start()

    sems = [pltpu.SemaphoreType.DMA((3 * n,)), pltpu.SemaphoreType.DMA((3 * n,)), pltpu.SemaphoreType.DMA((n,))]
    hbm = lambda a: pltpu.HBM(a.shape, a.dtype)
    res = pl.pallas_call(
        body, name=name,
        out_shape=sems + [hbm(s) for s in srcs] + [hbm(z) for z in zeros] + [_TOKEN],
        in_specs=[_HBM] * (2 * n) + [pl.BlockSpec(memory_space=pl.ANY)] * len(extra),
        out_specs=[_SEM] * 3 + [_HBM] * (2 * n) + [_VMEM_TOKEN],
        input_output_aliases={a: 3 + a for a in range(2 * n)},
        compiler_params=pltpu.CompilerParams(has_side_effects=_EFFECT),
    )(*[pltpu.with_memory_space_constraint(s, pltpu.HBM) for s in srcs],
      *[pltpu.with_memory_space_constraint(z, pltpu.HBM) for z in zeros], *extra)
    return res[:3], res[3:3 + n], res[3 + n:3 + 2 * n], res[-1]


def _gather_wait(names, sems, srcs, lands, after, name):
    n = len(srcs)
    lay = [LAYOUT[nm] for nm in names]

    def body(*refs):
        ins, zones = refs[:n], refs[n:2 * n]
        send, recv, lsem = refs[2 * n:2 * n + 3]
        x, y, c = _position()
        for a in range(n):
            whole = zones[a].at[pl.ds(0, lay[a][0])]
            for k in range(3):
                cp = pltpu.make_async_remote_copy(src_ref=ins[a], dst_ref=whole, send_sem=send.at[3 * a + k],
                                                  recv_sem=recv.at[3 * a + k], device_id=(x, y, 1 - c),
                                                  device_id_type=MESH)
                cp.wait_send()
                cp.wait_recv()
            pltpu.make_async_copy(ins[a], whole, lsem.at[a]).wait()

    hbm = lambda a: pltpu.HBM(a.shape, a.dtype)
    res = pl.pallas_call(
        body, name=name,
        out_shape=[hbm(s) for s in srcs] + [hbm(z) for z in lands],
        in_specs=[_HBM] * (2 * n) + [_SEM] * 3 + [pl.BlockSpec(memory_space=pl.ANY)], out_specs=[_HBM] * (2 * n),
        input_output_aliases={a: a for a in range(2 * n)},
        compiler_params=pltpu.CompilerParams(has_side_effects=_EFFECT),
    )(*srcs, *lands, *sems, after)
    return res[n:]


def _clip_pieces(lay_a, jsrc, c0):
    h = lay_a[1] // 2
    lo, hi = c0 * h, (c0 + 1) * h
    out = []
    for s0, d0, nr in lay_a[2](jsrc):
        a, b = max(d0, lo), min(d0 + nr, hi)
        if a < b:
            out.append((s0 + a - d0, a, b - a))
    return out


def _rows_of(pieces):
    return sum(nr for _, _, nr in pieces)


def _both_cores(body_for):
    x, y, c = _position()
    j = 2 * x + y
    for j0 in range(N_CHIPS):
        for c0 in range(2):
            @pl.when((j == j0) & (c == c0))
            def _(j0=j0, c0=c0):
                body_for(j0, c0)


STAGE_ROWS = 512


def _staged_copy(src, dst, buf, sem_in, sem_out, rows):
    ch = buf.shape[0]
    for r in range(0, rows, ch):
        nr = min(ch, rows - r)
        stage = buf.at[pl.ds(0, nr)]
        cin = pltpu.make_async_copy(src.at[pl.ds(r, nr)], stage, sem_in)
        cin.start()
        cin.wait()
        cout = pltpu.make_async_copy(stage, dst.at[pl.ds(r, nr)], sem_out)
        cout.start()
        cout.wait()


def _half_to_sibling(names, gl, name, after=None):
    n = len(gl)
    halves = [LAYOUT[nm][1] // 2 for nm in names]
    extra = [] if after is None else [after]

    def body(*refs):
        ins, outs = refs[:n], refs[n + len(extra):2 * n + len(extra)]
        send, recv = refs[2 * n + len(extra):]
        x, y, c = _position()

        def run(j0, c0):
            cps = [pltpu.make_async_remote_copy(src_ref=ins[a].at[pl.ds((1 - c0) * halves[a], halves[a])], dst_ref=outs[a],
                                                send_sem=send.at[a], recv_sem=recv.at[a], device_id=(x, y, 1 - c),
                                                device_id_type=MESH) for a in range(n)]
            for cp in cps:
                cp.start()
            for cp in cps:
                cp.wait()

        _both_cores(run)

    return pl.pallas_call(
        body, name=name,
        out_shape=[jax.ShapeDtypeStruct((halves[a],) + gl[a].shape[1:], gl[a].dtype) for a in range(n)],
        in_specs=[_HBM] * n + [pl.BlockSpec(memory_space=pl.ANY)] * len(extra), out_specs=[_HBM] * n,
        scratch_shapes=[pltpu.SemaphoreType.DMA((n,)), pltpu.SemaphoreType.DMA((n,))],
    )(*gl, *extra)


def _chip_scatter_half(names, parts, name):
    n = len(parts)
    lay = [LAYOUT[nm] for nm in names]
    zeros = [jnp.zeros((N_CHIPS, lay[a][0]) + parts[a].shape[1:], parts[a].dtype) for a in range(n)]

    def body(*refs):
        ins, outs = refs[:n], refs[2 * n:3 * n]
        send, recv = refs[3 * n:3 * n + 2]
        stage, sem_in, sem_out = refs[3 * n + 2:4 * n + 2], refs[4 * n + 2], refs[4 * n + 3]
        x, y, c = _position()
        chips = [(1 - x, y), (x, 1 - y), (1 - x, 1 - y)]

        def run(j0, c0):
            def sized(a, rows):
                return outs[a].at[0, pl.ds(0, rows)]

            for a in range(n):
                base = c0 * (lay[a][1] // 2)
                for k, chip in enumerate(chips):
                    for s0, d0, nr in _clip_pieces(lay[a], j0 ^ _REL[k], c0):
                        pltpu.make_async_remote_copy(
                            src_ref=ins[a].at[pl.ds(d0 - base, nr)], dst_ref=outs[a].at[j0, pl.ds(s0, nr)],
                            send_sem=send.at[3 * a + k], recv_sem=recv.at[3 * a + k],
                            device_id=(*chip, c), device_id_type=MESH).start()
            for a in range(n):
                base = c0 * (lay[a][1] // 2)
                for s0, d0, nr in _clip_pieces(lay[a], j0, c0):
                    _staged_copy(ins[a].at[pl.ds(d0 - base, nr)], outs[a].at[j0, pl.ds(s0, nr)], stage[a],
                                 sem_in.at[a], sem_out.at[a], nr)
            for a in range(n):
                got = _rows_of(_clip_pieces(lay[a], j0, c0))
                for k in range(3):
                    sent = _rows_of(_clip_pieces(lay[a], j0 ^ _REL[k], c0))
                    if sent:
                        pltpu.make_async_remote_copy(src_ref=sized(a, sent), dst_ref=sized(a, sent),
                                                     send_sem=send.at[3 * a + k], recv_sem=recv.at[3 * a + k],
                                                     device_id=(x, y, c), device_id_type=MESH).wait_send()
                    if got:
                        pltpu.make_async_remote_copy(src_ref=sized(a, got), dst_ref=sized(a, got),
                                                     send_sem=send.at[3 * a + k], recv_sem=recv.at[3 * a + k],
                                                     device_id=(x, y, c), device_id_type=MESH).wait_recv()

        _both_cores(run)

    return pl.pallas_call(
        body, name=name,
        out_shape=[jax.ShapeDtypeStruct(z.shape, z.dtype) for z in zeros],
        in_specs=[_HBM] * (2 * n), out_specs=[_HBM] * n, input_output_aliases={n + a: a for a in range(n)},
        scratch_shapes=[pltpu.SemaphoreType.DMA((3 * n,)), pltpu.SemaphoreType.DMA((3 * n,))]
        + [pltpu.VMEM((min(STAGE_ROWS, p.shape[0]),) + p.shape[1:], p.dtype) for p in parts]
        + [pltpu.SemaphoreType.DMA((n,)), pltpu.SemaphoreType.DMA((n,))],
    )(*parts, *zeros)


def _subset_exchange(names, bufs, l, name):
    n = len(bufs)
    lay = [LAYOUT[nm] for nm in names]

    def body(*refs):
        outs = refs[n:2 * n]
        send, recv = refs[2 * n:]
        x, y, c = _position()

        def run(j0, c0):
            for a in range(n):
                for s0, _, nr in _clip_pieces(lay[a], j0, c0):
                    rows = outs[a].at[l, pl.ds(s0, nr)]
                    pltpu.make_async_remote_copy(src_ref=rows, dst_ref=rows, send_sem=send.at[a], recv_sem=recv.at[a],
                                                 device_id=(x, y, 1 - c), device_id_type=MESH).start()
            for a in range(n):
                for c_half, wait_send in ((c0, True), (1 - c0, False)):
                    rows = _rows_of(_clip_pieces(lay[a], j0, c_half))
                    if rows:
                        ref = outs[a].at[l, pl.ds(0, rows)]
                        cp = pltpu.make_async_remote_copy(src_ref=ref, dst_ref=ref, send_sem=send.at[a], recv_sem=recv.at[a],
                                                          device_id=(x, y, 1 - c), device_id_type=MESH)
                        if wait_send:
                            cp.wait_send()
                        else:
                            cp.wait_recv()

        _both_cores(run)

    return pl.pallas_call(
        body, name=name,
        out_shape=[jax.ShapeDtypeStruct(b.shape, b.dtype) for b in bufs],
        in_specs=[_HBM] * n, out_specs=[_HBM] * n, input_output_aliases={a: a for a in range(n)},
        scratch_shapes=[pltpu.SemaphoreType.DMA((n,)), pltpu.SemaphoreType.DMA((n,))],
    )(*bufs)


def _scatter_start(names, gl, name):
    n = len(gl)
    lay = [LAYOUT[nm] for nm in names]
    zones = [lax.empty((N_CHIPS, lay[a][0]) + gl[a].shape[1:], gl[a].dtype) for a in range(n)]

    def body(*refs):
        ins, lands = refs[:n], refs[n:2 * n]
        send, recv, lsem = refs[2 * n:2 * n + 3]
        refs[-1][...] = jnp.zeros_like(refs[-1])
        x, y, c = _position()
        j = 2 * x + y
        chips = [(1 - x, y), (x, 1 - y), (1 - x, 1 - y)]
        for j0 in range(N_CHIPS):
            @pl.when(j == j0)
            def _(j0=j0):
                for a in range(n):
                    for s0, d0, nr in lay[a][2](j0):
                        pltpu.make_async_copy(ins[a].at[pl.ds(d0, nr)], lands[a].at[j0, pl.ds(s0, nr)], lsem.at[a]).start()
                    for k, chip in enumerate(chips):
                        for s0, d0, nr in lay[a][2](j0 ^ _REL[k]):
                            pltpu.make_async_remote_copy(
                                src_ref=ins[a].at[pl.ds(d0, nr)], dst_ref=lands[a].at[j0, pl.ds(s0, nr)],
                                send_sem=send.at[3 * a + k], recv_sem=recv.at[3 * a + k],
                                device_id=(*chip, c), device_id_type=MESH).start()

    sems = [pltpu.SemaphoreType.DMA((3 * n,)), pltpu.SemaphoreType.DMA((3 * n,)), pltpu.SemaphoreType.DMA((n,))]
    hbm = lambda a: pltpu.HBM(a.shape, a.dtype)
    res = pl.pallas_call(
        body, name=name,
        out_shape=sems + [hbm(g) for g in gl] + [hbm(z) for z in zones] + [_TOKEN],
        in_specs=[_HBM] * (2 * n), out_specs=[_SEM] * 3 + [_HBM] * (2 * n) + [_VMEM_TOKEN],
        input_output_aliases={a: 3 + a for a in range(2 * n)},
        compiler_params=pltpu.CompilerParams(has_side_effects=_EFFECT),
    )(*[pltpu.with_memory_space_constraint(g, pltpu.HBM) for g in gl],
      *[pltpu.with_memory_space_constraint(z, pltpu.HBM) for z in zones])
    return res[:3], res[3:3 + n], res[3 + n:3 + 2 * n], res[-1]


def _scatter_wait(names, sems, srcs, lands, after, name):
    n = len(srcs)
    lay = [LAYOUT[nm] for nm in names]

    def body(*refs):
        zones = refs[n:2 * n]
        send, recv, lsem = refs[2 * n:2 * n + 3]
        x, y, c = _position()
        for a in range(n):
            whole = zones[a].at[0, pl.ds(0, lay[a][0])]
            for k in range(3):
                cp = pltpu.make_async_remote_copy(src_ref=whole, dst_ref=whole, send_sem=send.at[3 * a + k],
                                                  recv_sem=recv.at[3 * a + k], device_id=(x, y, 1 - c),
                                                  device_id_type=MESH)
                cp.wait_send()
                cp.wait_recv()
            pltpu.make_async_copy(whole, whole, lsem.at[a]).wait()

    hbm = lambda a: pltpu.HBM(a.shape, a.dtype)
    res = pl.pallas_call(
        body, name=name,
        out_shape=[hbm(s) for s in srcs] + [hbm(z) for z in lands],
        in_specs=[_HBM] * (2 * n) + [_SEM] * 3 + [pl.BlockSpec(memory_space=pl.ANY)], out_specs=[_HBM] * (2 * n),
        input_output_aliases={a: a for a in range(2 * n)},
        compiler_params=pltpu.CompilerParams(has_side_effects=_EFFECT),
    )(*srcs, *lands, *sems, after)
    return res[n:]


def _swap_start(arrs, name):
    n = len(arrs)
    zones = [lax.empty(a.shape, a.dtype) for a in arrs]

    def body(*refs):
        ins, lands = refs[:n], refs[n:2 * n]
        send, recv = refs[2 * n:2 * n + 2]
        refs[-1][...] = jnp.zeros_like(refs[-1])
        x, y, c = _position()
        for a in range(n):
            pltpu.make_async_remote_copy(src_ref=ins[a], dst_ref=lands[a], send_sem=send.at[a], recv_sem=recv.at[a],
                                         device_id=(x, y, 1 - c), device_id_type=MESH).start()

    sems = [pltpu.SemaphoreType.DMA((n,)), pltpu.SemaphoreType.DMA((n,))]
    hbm = lambda a: pltpu.HBM(a.shape, a.dtype)
    res = pl.pallas_call(
        body, name=name,
        out_shape=sems + [hbm(a) for a in arrs] + [hbm(z) for z in zones] + [_TOKEN],
        in_specs=[_HBM] * (2 * n), out_specs=[_SEM] * 2 + [_HBM] * (2 * n) + [_VMEM_TOKEN],
        input_output_aliases={a: 2 + a for a in range(2 * n)},
        compiler_params=pltpu.CompilerParams(has_side_effects=_EFFECT),
    )(*[pltpu.with_memory_space_constraint(a, pltpu.HBM) for a in arrs],
      *[pltpu.with_memory_space_constraint(z, pltpu.HBM) for z in zones])
    return res[:2], res[2:2 + n], res[2 + n:2 + 2 * n], res[-1]


def _swap_wait(sems, srcs, lands, after, name):
    n = len(srcs)

    def body(*refs):
        ins, zones = refs[:n], refs[n:2 * n]
        send, recv = refs[2 * n:2 * n + 2]
        x, y, c = _position()
        for a in range(n):
            cp = pltpu.make_async_remote_copy(src_ref=ins[a], dst_ref=zones[a], send_sem=send.at[a], recv_sem=recv.at[a],
                                              device_id=(x, y, 1 - c), device_id_type=MESH)
            cp.wait_send()
            cp.wait_recv()

    hbm = lambda a: pltpu.HBM(a.shape, a.dtype)
    res = pl.pallas_call(
        body, name=name,
        out_shape=[hbm(s) for s in srcs] + [hbm(z) for z in lands],
        in_specs=[_HBM] * (2 * n) + [_SEM] * 2 + [pl.BlockSpec(memory_space=pl.ANY)], out_specs=[_HBM] * (2 * n),
        input_output_aliases={a: a for a in range(2 * n)},
        compiler_params=pltpu.CompilerParams(has_side_effects=_EFFECT),
    )(*srcs, *lands, *sems, after)
    return res[:n], res[n:]


def _row_tile(r):
    for t in (256, 128, 64, 32, 16, 8):
        if r % t == 0 and r > t:
            return t
    return r


def _pair_add_half(g, rb, c_arr, name):
    hrows, rest = rb.shape[0], rb.shape[1:]
    tr = _row_tile(hrows)
    nb = hrows // tr
    z = (0,) * len(rest)

    def body(c_ref, g_ref, r_ref, o_ref):
        o_ref[...] = (g_ref[...].astype(F32) + r_ref[...].astype(F32)).astype(o_ref.dtype)

    return pl.pallas_call(
        body, name=name,
        grid_spec=pltpu.PrefetchScalarGridSpec(
            num_scalar_prefetch=1, grid=(nb,),
            in_specs=[pl.BlockSpec((tr,) + rest, lambda i, c_ref: (c_ref[0] * nb + i,) + z),
                      pl.BlockSpec((tr,) + rest, lambda i, c_ref: (i,) + z)],
            out_specs=pl.BlockSpec((tr,) + rest, lambda i, c_ref: (i,) + z)),
        out_shape=jax.ShapeDtypeStruct((hrows,) + rest, BF16),
        compiler_params=pltpu.CompilerParams(dimension_semantics=("parallel",), vmem_limit_bytes=VMEM_LIMIT),
    )(c_arr, g, rb)


def _sum_slabs(slabs, l, buf, name):
    m = len(slabs)
    n, R, rest = slabs[0].shape[0], slabs[0].shape[1], slabs[0].shape[2:]
    tr = _row_tile(R)
    z = (0,) * len(rest)

    def body(*refs):
        total = None
        for r_ref in refs[:m]:
            acc = r_ref[0].astype(F32)
            for k in range(1, n):
                acc = acc + r_ref[k].astype(F32)
            total = acc if total is None else total + acc
        refs[-1][...] = total

    if R // tr > 64 and len(rest) == 1 and rest[0] % 256 == 0:
        grid = (rest[0] // 256,)
        in_spec = pl.BlockSpec((n, R, 256), lambda i: (0, 0, i))
        out_spec = pl.BlockSpec((None, R, 256), lambda i: (l, 0, i))
    else:
        grid = (R // tr,)
        in_spec = pl.BlockSpec((n, tr) + rest, lambda i: (0, i) + z)
        out_spec = pl.BlockSpec((None, tr) + rest, lambda i: (l, i) + z)
    in_specs, args, aliases = [in_spec] * m, list(slabs), {}
    if buf is not None:
        in_specs.append(pl.BlockSpec(memory_space=pl.ANY))
        args.append(buf)
        aliases = {m: 0}
    return pl.pallas_call(
        body, name=name, grid=grid, in_specs=in_specs, out_specs=out_spec,
        out_shape=jax.ShapeDtypeStruct((DEPTH, R) + rest, F32), input_output_aliases=aliases,
        compiler_params=pltpu.CompilerParams(
            dimension_semantics=("parallel",),
            vmem_limit_bytes=_vmem(m * _nbytes(in_spec.block_shape, slabs[0].dtype) + _nbytes(out_spec.block_shape, F32),
                                   2 * _nbytes(out_spec.block_shape, F32))),
    )(*args)


def _adam_math(w, g, m, v):
    mn = ADAM_B1 * m + (1.0 - ADAM_B1) * g
    vn = ADAM_B2 * v + (1.0 - ADAM_B2) * (g * g)
    m_hat = mn / (1.0 - ADAM_B1 ** ADAM_STEP)
    v_hat = vn / (1.0 - ADAM_B2 ** ADAM_STEP)
    return -ADAM_LR * (m_hat / (jnp.sqrt(v_hat) + ADAM_EPS) + ADAM_WD * w), mn, vn


def _adamw(w, g, m, v, name):
    L, R, C = w.shape
    tr = _row_tile(R)

    def body(w_ref, g_ref, m_ref, v_ref, d_ref, mo_ref, vo_ref):
        d_ref[...], mo_ref[...], vo_ref[...] = _adam_math(w_ref[...], g_ref[...], m_ref[...], v_ref[...])

    if R // tr > 64 and C % 128 == 0:
        spec, grid = pl.BlockSpec((None, R, 128), lambda l, i: (l, 0, i)), (L, C // 128)
    else:
        spec, grid = pl.BlockSpec((None, tr, C), lambda l, i: (l, i, 0)), (L, R // tr)
    return pl.pallas_call(
        body, name=name, grid=grid, in_specs=[spec] * 4, out_specs=[spec] * 3,
        out_shape=[jax.ShapeDtypeStruct((L, R, C), F32)] * 3,
        compiler_params=pltpu.CompilerParams(dimension_semantics=("parallel", "parallel"),
                                             vmem_limit_bytes=_vmem(7 * _nbytes(spec.block_shape, F32))),
    )(w, g, m, v)


_VMEM_WHOLE = pl.BlockSpec(memory_space=pltpu.VMEM)


def _matrix_update(gath, w, m, v, name):
    K = w.shape[1]

    def body(g0_ref, g1_ref, w_ref, m_ref, v_ref, go_ref, d_ref, mo_ref, vo_ref):
        for l, gr in enumerate((g0_ref, g1_ref)):
            for k in range(K):
                g = gr[0, k].astype(F32)
                for dev in range(1, 8):
                    g = g + gr[dev, k].astype(F32)
                go_ref[l, k] = g
                d_ref[l, k], mo_ref[l, k], vo_ref[l, k] = _adam_math(w_ref[l, k], g, m_ref[l, k], v_ref[l, k])

    return pl.pallas_call(
        body, name=name, in_specs=[_VMEM_WHOLE] * 5, out_specs=[_VMEM_WHOLE] * 4,
        out_shape=[jax.ShapeDtypeStruct(w.shape, F32)] * 4,
        compiler_params=pltpu.CompilerParams(vmem_limit_bytes=32 * MIB),
    )(gath[0], gath[1], w, m, v)


VECS = (("pre_norm_g", D), ("post_norm_g", D), ("gm_ln_g", GM_W), ("gm_ln_b", GM_W), ("mla_q_norm_g", QR),
        ("mla_kv_norm_g", KVR), ("lru_conv_b", LRU_W), ("lru_b_a", LRU_W), ("lru_b_x", LRU_W), ("lru_lambda", LRU_W))
VEC_KEY = {"pre_norm_g": "pre_g", "post_norm_g": "post_g", "gm_ln_g": "ln_g", "gm_ln_b": "ln_b", "mla_q_norm_g": "qg",
           "mla_kv_norm_g": "kvg", "lru_conv_b": "conv_b", "lru_b_a": "ba", "lru_b_x": "bx", "lru_lambda": "lam"}
VEC_ROWS, VEC_W, VEC_ROW0, LOSS_ROW = 16, LRU_W, GM_G, 14


def _pack_rows(LG, loss_part):
    per = len(VECS) + 1
    ins = []
    for G in LG:
        ins += [G[VEC_KEY[n]] for n, _ in VECS] + [G["bst"]]
    ins.append(loss_part)

    def body(*refs):
        o_ref = refs[-1]
        o_ref[...] = jnp.zeros_like(o_ref)
        for l in range(DEPTH):
            base = VEC_ROWS * l
            o_ref[pl.ds(base, 8), pl.ds(0, GM_B)] = refs[per * l + len(VECS)][...].T[:8, :]
            for t, (_, width) in enumerate(VECS):
                o_ref[pl.ds(base + VEC_ROW0 + t, 1), pl.ds(0, width)] = refs[per * l + t][...]
        o_ref[pl.ds(LOSS_ROW, 1), pl.ds(0, 128)] = jnp.broadcast_to(refs[-2][...], (1, 128))

    return pl.pallas_call(
        body, name="pack_rows", in_specs=[_VMEM_WHOLE] * len(ins), out_specs=_VMEM_WHOLE,
        out_shape=jax.ShapeDtypeStruct((DEPTH * VEC_ROWS, VEC_W), F32),
    )(*ins)


def _vector_update(gath, W, M, V):
    names = [n for n, _ in VECS] + ["gm_bs"]
    nw = len(names)

    def body(*refs):
        g_ref = refs[0]
        wr, mr, vr = refs[1:1 + nw], refs[1 + nw:1 + 2 * nw], refs[1 + 2 * nw:1 + 3 * nw]
        outs = refs[1 + 3 * nw:]
        s = g_ref[0]
        for dev in range(1, 8):
            s = s + g_ref[dev]
        for t, (_, width) in enumerate(VECS):
            for l in range(DEPTH):
                r = VEC_ROWS * l + VEC_ROW0 + t
                g = s[r:r + 1, :width]
                row = (pl.ds(l, 1), slice(None))
                res = (g,) + _adam_math(wr[t][row], g, mr[t][row], vr[t][row])
                for q in range(4):
                    outs[4 * t + q][row] = res[q]
        t = len(VECS)
        for l in range(DEPTH):
            for k in range(GM_G):
                g = s[VEC_ROWS * l + k:VEC_ROWS * l + k + 1, :GM_B]
                row = (l, pl.ds(k, 1), slice(None))
                res = (g,) + _adam_math(wr[t][row], g, mr[t][row], vr[t][row])
                for q in range(4):
                    outs[4 * t + q][row] = res[q]
        outs[4 * nw][...] = s[LOSS_ROW:LOSS_ROW + 1, :128]

    ws = [W[n] for n in names]
    out_shape = []
    for w in ws:
        out_shape += [jax.ShapeDtypeStruct(w.shape, F32)] * 4
    out_shape.append(jax.ShapeDtypeStruct((1, 128), F32))
    res = pl.pallas_call(
        body, name="vector_update", in_specs=[_VMEM_WHOLE] * (1 + 3 * nw), out_specs=[_VMEM_WHOLE] * (4 * nw + 1),
        out_shape=out_shape, compiler_params=pltpu.CompilerParams(vmem_limit_bytes=VMEM_LIMIT),
    )(gath, *ws, *[M[n] for n in names], *[V[n] for n in names])
    return {n: tuple(res[4 * t:4 * t + 4]) for t, n in enumerate(names)}, res[4 * nw]


SHARDED = ("w_in", "mla_w_uq", "mla_w_ukv", "lru_conv_w", "w_proj_a", "w_proj_b", "w_proj_c", "w_out")
FIRST = ("w_in", "lru_conv_w")
LATER = tuple(n for n in SHARDED if n not in FIRST)
COL_SHARDED = ("w_in", "mla_w_uq", "mla_w_ukv", "lru_conv_w")
SMALL = ("pre_norm_g", "gm_ln_g", "gm_ln_b", "gm_ws", "gm_bs", "mla_q_norm_g", "mla_kv_norm_g", "lru_conv_b",
         "lru_w_a", "lru_b_a", "lru_w_x", "lru_b_x", "lru_lambda", "post_norm_g")
WEIGHTS = ("pre_norm_g", "w_in", "gm_ln_g", "gm_ln_b", "gm_ws", "gm_bs", "mla_q_norm_g", "mla_w_uq",
           "mla_kv_norm_g", "mla_w_ukv", "lru_conv_w", "lru_conv_b", "lru_w_a", "lru_b_a", "lru_w_x", "lru_b_x",
           "lru_lambda", "w_proj_a", "w_proj_b", "w_proj_c", "w_out", "post_norm_g")


GB_KEY = {"w_in": "wp", "mla_w_uq": "wuq", "mla_w_ukv": "wukv", "w_proj_a": "wpa", "w_proj_b": "wpb",
          "w_proj_c": "wpc", "w_out": "wout"}


def _prepare(l, gathered, small, wsb):
    P = {GB_KEY[n]: gathered[n] for n in GB_KEY if n in gathered}
    P["conv_w"] = gathered["lru_conv_w"].transpose(1, 0, 2).reshape(CONV_W, LRU_W)
    P["wsb"] = wsb
    row = lambda n: small[n][l][None, :]
    P["pre_g"], P["post_g"] = row("pre_norm_g"), row("post_norm_g")
    P["ln_g"], P["ln_b"] = row("gm_ln_g"), row("gm_ln_b")
    P["ws"] = small["gm_ws"][l]
    P["bst"] = jnp.pad(small["gm_bs"][l].T, ((0, 0), (0, 128 - GM_G)))
    P["qg"], P["kvg"] = row("mla_q_norm_g"), row("mla_kv_norm_g")
    P["conv_b"], P["ba"], P["bx"], P["lam"] = row("lru_conv_b"), row("lru_b_a"), row("lru_b_x"), row("lru_lambda")
    return P


def kernel(x, pre_norm_g, w_in, gm_ln_g, gm_ln_b, gm_ws, gm_bs, mla_q_norm_g, mla_w_uq, mla_kv_norm_g, mla_w_ukv, lru_conv_w, lru_conv_b, lru_w_a, lru_b_a, lru_w_x, lru_b_x, lru_lambda, w_proj_a, w_proj_b, w_proj_c, w_out, post_norm_g, loss_target, m_pre_norm_g, m_w_in, m_gm_ln_g, m_gm_ln_b, m_gm_ws, m_gm_bs, m_mla_q_norm_g, m_mla_w_uq, m_mla_kv_norm_g, m_mla_w_ukv, m_lru_conv_w, m_lru_conv_b, m_lru_w_a, m_lru_b_a, m_lru_w_x, m_lru_b_x, m_lru_lambda, m_w_proj_a, m_w_proj_b, m_w_proj_c, m_w_out, m_post_norm_g, v_pre_norm_g, v_w_in, v_gm_ln_g, v_gm_ln_b, v_gm_ws, v_gm_bs, v_mla_q_norm_g, v_mla_w_uq, v_mla_kv_norm_g, v_mla_w_ukv, v_lru_conv_w, v_lru_conv_b, v_lru_w_a, v_lru_b_a, v_lru_w_x, v_lru_b_x, v_lru_lambda, v_w_proj_a, v_w_proj_b, v_w_proj_c, v_w_out, v_post_norm_g):
    args = dict(locals())
    W = {n: args[n] for n in WEIGHTS}
    M = {n: args["m_" + n] for n in WEIGHTS}
    V = {n: args["v_" + n] for n in WEIGHTS}
    c = lax.axis_index("c")

    def shards(l, names):
        out = []
        for n in names:
            blk = W[n][l].T if n in TRANSPOSED else W[n][l]
            out.append(blk[None] if n == "lru_conv_w" else blk.astype(BF16))
        return out

    small = {n: W[n] for n in SMALL}
    wsb = _superblocks(W["lru_w_a"], W["lru_w_x"])
    tabs = _rope_tables()
    s0a, s0b, s1a, s1b = shards(0, FIRST), shards(0, LATER), shards(1, FIRST), shards(1, LATER)
    g0, zones = _weights_allgather(FIRST, s0a, "weights_allgather_l0", carry=_gather_zeros(LATER, s0b)
                                   + _gather_zeros(FIRST, s1a) + _gather_zeros(LATER, s1b))
    nl, nf = len(LATER), len(FIRST)
    w0b = _gather_start(LATER, s0b, zones[:nl], "weights_gather_start_l0")
    w1a = _gather_start(FIRST, s1a, zones[nl:nl + nf], "weights_gather_start_l1_first", after=w0b[3])
    w1b = _gather_start(LATER, s1b, zones[nl + nf:], "weights_gather_start_l1_later", after=w1a[3])

    def late(started, name):
        def wait(after):
            got = _gather_wait(LATER, *started[:3], after, name)
            return {GB_KEY[n]: g for n, g in zip(LATER, got)}
        return wait

    P = [_prepare(0, dict(zip(FIRST, g0)), small, wsb), None]
    h0 = x[0]
    h1, A0 = _layer_fwd(h0, P[0], 0, tabs, w1b[3], late(w0b, "weights_gather_wait_l0"))
    g1 = _gather_wait(FIRST, *w1a[:3], h1, "weights_gather_wait_l1_first")
    P[1] = _prepare(1, dict(zip(FIRST, g1)), small, wsb)
    h2, A1 = _layer_fwd(h1, P[1], 1, tabs, None, late(w1b, "weights_gather_wait_l1_later"))
    dy, loss_part = _loss_fwd(h2, loss_target[0])

    def large_grads(G, GB, names):
        conv = G["conv_w"].reshape(CONV_W, N_CHIPS, LRU_W // N_CHIPS).transpose(1, 0, 2)
        return [conv if n == "lru_conv_w" else GB[GB_KEY[n]] for n in names]

    d1, G1, GB1 = _layer_bwd(dy, A1, P[1], 1, tabs)
    sc1 = _scatter_start(SHARDED, large_grads(G1, GB1, SHARDED), "grads_scatter_start_l1")
    started = {}

    def early0(GB):
        mine1 = _scatter_wait(SHARDED, *sc1[:3], GB["wukv"], "grads_scatter_wait_l1")
        started["swap1"] = _swap_start(mine1, "partials_swap_start_l1")
        started["sc0"] = _scatter_start(LATER, [GB[GB_KEY[n]] for n in LATER], "grads_scatter_start_l0")
        return started["sc0"][3], started["swap1"][3]

    d0, G0, GB0 = _layer_bwd(d1, A0, P[0], 0, tabs, sc1[3], early0)
    LG = (G0, G1)
    mine0 = _scatter_wait(LATER, *started["sc0"][:3], d0, "grads_scatter_wait_l0")
    swap0 = _swap_start(mine0, "partials_swap_start_l0")
    g0f = large_grads(G0, GB0, FIRST)
    c_arr = jnp.reshape(c, (1,)).astype(jnp.int32)
    from_sib = _half_to_sibling(FIRST, g0f, "grads_half_to_sibling_l0", after=swap0[3])
    pair = [_pair_add_half(g, rb, c_arr, "pair_add_" + n) for n, g, rb in zip(FIRST, g0f, from_sib)]
    slabs = _chip_scatter_half(FIRST, pair, "grads_chip_scatter_l0")
    mine1, theirs1 = _swap_wait(*started["swap1"][:3], slabs[0], "partials_swap_wait_l1")
    both = dict(zip(SHARDED, [_sum_slabs([a, b], 1, None, "sum_partials_l1_" + n)
                              for n, a, b in zip(SHARDED, mine1, theirs1)]))
    for n, s in zip(FIRST, slabs):
        both[n] = _sum_slabs([s], 0, both[n], "sum_slabs_l0_" + n)
    done = _subset_exchange(FIRST, [both[n] for n in FIRST], 0, "reduced_rows_to_sibling_l0")
    both.update(zip(FIRST, done))
    mine0, theirs0 = _swap_wait(*swap0[:3], done[0], "partials_swap_wait_l0")
    for n, a, b in zip(LATER, mine0, theirs0):
        both[n] = _sum_slabs([a, b], 0, both[n], "sum_partials_l0_" + n)
    both = [both[n] for n in SHARDED]
    grads = {}
    for n, b in zip(SHARDED, both):
        if n in TRANSPOSED and n != "w_in":
            b = jnp.swapaxes(b, 1, 2)
        grads[n] = b if n == "w_in" else b.reshape(W[n].shape)

    rows = _pack_rows(LG, loss_part)
    mats = []
    for g in LG:
        mats += [g["ws"].astype(BF16), g["wab"][0, :, :, :LRU_BW], g["wab"][1, :, :, :LRU_BW]]
    gath = _allgather([rows] + mats, "small_grads_allgather")
    upd, loss_row = _vector_update(gath[0], W, M, V)
    loss = loss_row[0, 0]
    for k, n in enumerate(("gm_ws", "lru_w_a", "lru_w_x")):
        upd[n] = _matrix_update((gath[1 + k], gath[4 + k]), W[n], M[n], V[n], "update_" + n)

    for n in SHARDED:
        if n == "w_in":
            tr = lambda a: jnp.swapaxes(a, 1, 2)
            res = _adamw(tr(W[n]), grads[n], tr(M[n]), tr(V[n]), "adamw_" + n)
            upd[n] = tuple(tr(a) for a in (grads[n],) + tuple(res))
        else:
            upd[n] = (grads[n],) + tuple(_adamw(W[n], grads[n], M[n], V[n], "adamw_" + n))

    return (loss, d0[None], *[upd[n][0] for n in WEIGHTS], *[upd[n][1] for n in WEIGHTS],
            *[upd[n][2] for n in WEIGHTS], *[upd[n][3] for n in WEIGHTS])
```

```python
import functools
import math

import jax
import jax.numpy as jnp
from jax import lax
from jax.experimental import pallas as pl
from jax.experimental.pallas import tpu as pltpu

F32, BF16 = jnp.float32, jnp.bfloat16
MESH = pl.DeviceIdType.MESH

S, D, DEPTH = 2048, 1024, 2
CHUNK, EPS = 64, 1e-6
GM_W, GM_G, GM_B = 1024, 4, 128
H, NOPE, ROPE, VDIM = 8, 128, 64, 128
QR, KVR = 384, 256
MLA_W = H * VDIM
LRU_W, LRU_NB, LRU_BW, LRU_C, CONV_W = 1280, 16, 80, 8.0, 4
ROPE_THETA = 10000.0
IN_SIZES = (GM_W, GM_W, GM_W, QR, KVR, ROPE, MLA_W, LRU_W, LRU_W, D, D, D)
N_IN = sum(IN_SIZES)
N_CHIPS = 4
ADAM_LR, ADAM_B1, ADAM_B2, ADAM_EPS, ADAM_WD, ADAM_STEP = 0.001, 0.9, 0.999, 1e-08, 0.01, 10

HP = 256
O_U, O_V, O_ZA, O_GA, O_GB, O_GC = 0, 1024, 2048, 3072, 4096, 5120
O_CKV, O_KR, O_CQ, O_XC, O_ZC, O_ZB = 6144, 6400, 6528, 7680, 8960, 10240
NP = 11264
MIB = 1024 * 1024
VMEM_LIMIT = 16 * MIB


def _vmem(block_bytes, temp_bytes=0):
    return int(min(max(2 * block_bytes + temp_bytes + 4 * MIB, VMEM_LIMIT), 56 * MIB))


def _nbytes(shape, dtype):
    return math.prod(d for d in shape if d is not None) * jnp.dtype(dtype).itemsize


def _tile(dim, target):
    if dim <= target:
        return dim
    t = (target // 128) * 128
    while dim % t:
        t -= 128
    return t


def _sig(x):
    return jax.nn.sigmoid(x)


def _silu(x):
    return x * _sig(x)


def _dsilu(x):
    s = _sig(x)
    return s * (1.0 + x * (1.0 - s))


def _mm(a, b, mode, name, out_dtype=F32, tm=512, tn=512, tk=1024, b_lead=None, out_lead=None, token=None):
    b2 = b.shape[1:] if b_lead is not None else b.shape
    if mode == "nn":
        (M, K), (K2, N) = a.shape, b2
    elif mode == "nt":
        (M, K), (N, K2) = a.shape, b2
    else:
        (K, M), (K2, N) = a.shape, b2
    assert K == K2, (name, a.shape, b.shape)
    tm, tn, tk = _tile(M, tm), _tile(N, tn), _tile(K, tk)
    nk = K // tk
    if mode == "tn":
        a_spec = pl.BlockSpec((tk, tm), lambda i, j, k: (k, i))
        lhs_c = 0
    else:
        a_spec = pl.BlockSpec((tm, tk), lambda i, j, k: (i, k))
        lhs_c = 1
    b_blk, b_idx, rhs_c = ((tn, tk), (lambda i, j, k: (j, k)), 1) if mode == "nt" else ((tk, tn), (lambda i, j, k: (k, j)), 0)
    if b_lead is None:
        b_spec = pl.BlockSpec(b_blk, b_idx)
    else:
        b_spec = pl.BlockSpec((None,) + b_blk, functools.partial(lambda i, j, k, f, l: (l,) + f(i, j, k), f=b_idx, l=b_lead))
    dims = (((lhs_c,), (rhs_c,)), ((), ()))
    in_specs, args, aliases = [a_spec, b_spec], [a, b], {}
    if out_lead is None:
        out_spec = pl.BlockSpec((tm, tn), lambda i, j, k: (i, j))
        out_shape = jax.ShapeDtypeStruct((M, N), out_dtype)
    else:
        l_out, n_lead, buf = out_lead
        out_spec = pl.BlockSpec((None, tm, tn), functools.partial(lambda i, j, k, l: (l, i, j), l=l_out))
        out_shape = jax.ShapeDtypeStruct((n_lead, M, N), out_dtype)
        if buf is not None:
            in_specs.append(pl.BlockSpec(memory_space=pl.ANY))
            args.append(buf)
            aliases = {2: 0}
    if token is not None:
        in_specs.append(pl.BlockSpec(memory_space=pl.ANY))
        args.append(token)

    def body(a_ref, b_ref, *rest):
        o_ref, acc_ref = rest[-2:]
        k = pl.program_id(2)

        @pl.when(k == 0)
        def _():
            acc_ref[...] = jnp.zeros_like(acc_ref)

        acc_ref[...] += lax.dot_general(a_ref[...].astype(BF16), b_ref[...].astype(BF16), dims,
                                        preferred_element_type=F32)

        @pl.when(k == nk - 1)
        def _():
            o_ref[...] = acc_ref[...].astype(o_ref.dtype)

    return pl.pallas_call(
        body, name=name, grid=(M // tm, N // tn, nk),
        in_specs=in_specs, out_specs=out_spec, out_shape=out_shape,
        scratch_shapes=[pltpu.VMEM((tm, tn), F32)], input_output_aliases=aliases,
        compiler_params=pltpu.CompilerParams(
            dimension_semantics=("parallel", "parallel", "arbitrary"),
            vmem_limit_bytes=_vmem(_nbytes((tm, tk), a.dtype) + _nbytes((tk, tn), b.dtype) + _nbytes((tm, tn), out_dtype),
                                   _nbytes((tm, tn), F32) + _nbytes((tm, tk), BF16) + _nbytes((tk, tn), BF16))),
    )(*args)


def _rows(fn, name, tm, rows, halos=(), fulls=(), outs=(), accs=()):
    n = S // tm
    in_specs, args = [], []
    for arr, w, cb in rows:
        in_specs.append(pl.BlockSpec((tm, w), functools.partial(lambda i, cb: (i, cb), cb=cb)))
        args.append(arr)
    for arr, w, cb, side in halos:
        if side == "prev":
            im = functools.partial(lambda i, cb: (jnp.maximum(i * (tm // 16) - 1, 0), cb), cb=cb)
        else:
            im = functools.partial(lambda i, cb: (jnp.minimum((i + 1) * (tm // 16), S // 16 - 1), cb), cb=cb)
        in_specs.append(pl.BlockSpec((16, w), im))
        args.append(arr)
    for arr in fulls:
        in_specs.append(pl.BlockSpec(arr.shape, functools.partial(lambda i, nd: (0,) * nd, nd=arr.ndim)))
        args.append(arr)
    out_shape, out_specs, aliases, n_alias = [], [], {}, 0
    for k, o in enumerate(outs):
        if len(o) == 3 and o[2] == "T":
            out_shape.append(jax.ShapeDtypeStruct((o[0], S), o[1]))
            out_specs.append(pl.BlockSpec((o[0], tm), lambda i: (0, i)))
        elif len(o) == 3:
            buf, total, cb = o[2]
            out_shape.append(jax.ShapeDtypeStruct((S, total), o[1]))
            out_specs.append(pl.BlockSpec((tm, o[0]), functools.partial(lambda i, cb: (i, cb), cb=cb)))
            if buf is not None:
                aliases[len(args)] = k
                in_specs.append(pl.BlockSpec(memory_space=pl.ANY))
                args.append(buf)
                n_alias += 1
        else:
            out_shape.append(jax.ShapeDtypeStruct((S, o[0]), o[1]))
            out_specs.append(pl.BlockSpec((tm, o[0]), lambda i: (i, 0)))
    for shp in accs:
        out_shape.append(jax.ShapeDtypeStruct(shp, F32))
        out_specs.append(pl.BlockSpec(shp, functools.partial(lambda i, nd: (0,) * nd, nd=len(shp))))
    nr, nh, nf, no, na = len(rows), len(halos), len(fulls), len(outs), len(accs)
    blocks = (sum(_nbytes((tm, w), arr.dtype) for arr, w, _ in rows) + sum(_nbytes(a.shape, a.dtype) for a in fulls)
              + sum(_nbytes((tm, o[0]), o[1]) for o in outs) + sum(_nbytes(shp, F32) for shp in accs))
    widest = _nbytes((tm, max([w for _, w, _ in rows] + [o[0] for o in outs])), F32)

    def body(*refs):
        i = pl.program_id(0)
        ins, orefs = refs[:nr + nh + nf], refs[nr + nh + nf + n_alias:]
        rv = [r[...].astype(F32) for r in ins[:nr]]
        hv = [r[...].astype(F32)[8:] if h[3] == "prev" else r[...].astype(F32)[:8] for r, h in zip(ins[nr:nr + nh], halos)]
        fv = [r[...] for r in ins[nr + nh:]]
        o, a = fn(i, rv, hv, fv)
        assert len(o) == no and len(a) == na, name
        for spec, ref, val in zip(outs, orefs[:no], o):
            ref[...] = (val.T if len(spec) == 3 and spec[2] == "T" else val).astype(ref.dtype)
        if na:
            @pl.when(i == 0)
            def _():
                for ref in orefs[no:]:
                    ref[...] = jnp.zeros_like(ref)

            for ref, val in zip(orefs[no:], a):
                ref[...] += val

    res = pl.pallas_call(
        body, name=name, grid=(n,), in_specs=in_specs, out_specs=out_specs, out_shape=out_shape,
        input_output_aliases=aliases,
        compiler_params=pltpu.CompilerParams(dimension_semantics=("arbitrary",), vmem_limit_bytes=_vmem(blocks, 6 * widest)),
    )(*args)
    return res


def _shift_down(xb, halo, s, row):
    fix = jnp.tile(pltpu.roll(halo, s, 0), (xb.shape[0] // 8, 1))
    return jnp.where(row >= s, pltpu.roll(xb, s, 0), fix)


def _shift_up(xb, halo, s, row):
    tm = xb.shape[0]
    fix = jnp.tile(pltpu.roll(halo, 8 - s, 0), (tm // 8, 1))
    return jnp.where(row < tm - s, pltpu.roll(xb, tm - s, 0), fix)


def _rms(x):
    return lax.rsqrt(jnp.mean(x * x, axis=-1, keepdims=True) + EPS)


def _rms_bwd(dy, x, g):
    r = _rms(x)
    xh = x * r
    dxh = dy * g
    dx = r * (dxh - xh * jnp.mean(dxh * xh, axis=-1, keepdims=True))
    return dx, dy * xh


def _colsum(x):
    return jnp.sum(x, axis=0, keepdims=True)


def _prenorm_fwd(x, g, token=None):
    def fn(i, rv, hv, fv):
        return [rv[0] * _rms(rv[0]) * fv[0]], []
    return _rows(fn, "prenorm_fwd", 256, [(x, D, 0)], fulls=[g] + ([] if token is None else [token]), outs=[(D, BF16)])[0]


def _gm_mask():
    r = lax.broadcasted_iota(jnp.int32, (GM_B, GM_B), 0) // CHUNK
    c = lax.broadcasted_iota(jnp.int32, (GM_B, GM_B), 1) // CHUNK
    return c <= r


def _gm_norm(v, g, b):
    mu = jnp.mean(v, axis=-1, keepdims=True)
    vc = v - mu
    rs = lax.rsqrt(jnp.mean(vc * vc, axis=-1, keepdims=True) + EPS)
    vh = vc * rs
    return vh, rs, vh * g + b


def _gm_sv(vn, ws, bst):
    mask = _gm_mask()
    gw = GM_W // GM_G
    parts = []
    for g in range(GM_G):
        wm = jnp.where(mask, ws[g], 0.0).astype(BF16)
        parts.append(jnp.dot(wm, vn[:, g * gw:(g + 1) * gw].astype(BF16), preferred_element_type=F32)
                     + bst[:, g:g + 1])
    return jnp.concatenate(parts, axis=1)


def _gmlp_fwd(proj, ln_g, ln_b, ws, bst):
    def fn(i, rv, hv, fv):
        u, v, z = rv
        g, b, w, bt = fv
        _, _, vn = _gm_norm(v, g, b)
        return [u * _gm_sv(vn, w, bt) * _silu(z)], []
    return _rows(fn, "gmlp_fwd", GM_B, [(proj, GM_W, 0), (proj, GM_W, 1), (proj, GM_W, 2)],
                 fulls=[ln_g, ln_b, ws, bst], outs=[(GM_W, BF16)])[0]


def _mla_prep_fwd(proj, qg, kvg):
    def fn(i, rv, hv, fv):
        cq, ckv = rv
        g1, g2 = fv
        return [cq * _rms(cq) * g1, ckv * _rms(ckv) * g2], []
    return _rows(fn, "mla_prep_fwd", 256, [(proj, QR, O_CQ // QR), (proj, KVR, O_CKV // KVR)],
                 fulls=[qg, kvg], outs=[(QR, BF16), (KVR, BF16)])


def _rot(t, cc, sa, sb):
    return t * cc + pltpu.roll(t, 32, 1) * sa + pltpu.roll(t, 96, 1) * sb


def _rot_t(g, cc, sa, sb):
    return g * cc + pltpu.roll(g * sa, 96, 1) + pltpu.roll(g * sb, 32, 1)


def _rope_tables():
    pos = jnp.arange(S, dtype=F32)
    inv_freq = ROPE_THETA ** (-jnp.arange(0, ROPE, 2, dtype=F32) / ROPE)
    ang = pos[:, None] * inv_freq[None, :]
    cos, sin, z = jnp.cos(ang), jnp.sin(ang), jnp.zeros((S, 32), F32)
    cc = jnp.concatenate([cos, cos, z, z], axis=1)
    sa = jnp.concatenate([z, sin, z, z], axis=1)
    sb = jnp.concatenate([-sin, z, z, z], axis=1)
    return cc, sa, sb


ATT_SCALE = 1.0 / math.sqrt(NOPE + ROPE)


def _rope_fwd(q, kv, proj, tabs):
    def fn(i, rv, hv, fv):
        qb, kvb, kr, cc, sa, sb = rv
        krr = _rot(kr, cc, sa, sb)
        qs, ks = [], []
        for h in range(H):
            qs += [qb[:, h * HP:h * HP + 128] * ATT_SCALE, _rot(qb[:, h * HP + 128:(h + 1) * HP], cc, sa, sb) * ATT_SCALE]
            ks += [kvb[:, h * 128:(h + 1) * 128], krr]
        kc = jnp.concatenate(ks, axis=1)
        vv = kvb[:, H * NOPE:]
        return [jnp.concatenate(qs, axis=1), kc, kc, vv, vv], []
    cc, sa, sb = tabs
    return _rows(fn, "rope_fwd", 256,
                 [(q, H * HP, 0), (kv, H * 256, 0), (proj, 128, O_KR // 128), (cc, 128, 0), (sa, 128, 0), (sb, 128, 0)],
                 outs=[(H * HP, BF16), (H * HP, BF16), (H * HP, BF16, "T"), (MLA_W, BF16), (MLA_W, BF16, "T")])


TQ, TC, ATT_NB = 512, 128, 4
ATT_KB = TC * ATT_NB
_NT = (((1,), (1,)), ((), ()))


def _attn_allowed(i, kc):
    kpos = kc * TC + lax.broadcasted_iota(jnp.int32, (TC, TQ), 0)
    qpos = i * TQ + lax.broadcasted_iota(jnp.int32, (TC, TQ), 1)
    return (kpos // CHUNK) <= (qpos // CHUNK)


def _attn_fwd(qc, kc, vt):
    def body(q_ref, k_ref, vt_ref, o_ref, l_ref):
        i = pl.program_id(1)
        q = q_ref[...]

        def scores(sb):
            t0s = [pl.multiple_of((sb * ATT_NB + c) * TC, TC) for c in range(ATT_NB)]
            return [lax.dot_general(k_ref[pl.ds(t0, TC), :], q, _NT, preferred_element_type=F32) for t0 in t0s]

        def block(sb, ss, carry, masked):
            m, l, acc = carry
            t0s = [pl.multiple_of((sb * ATT_NB + c) * TC, TC) for c in range(ATT_NB)]
            if masked:
                ss = [jnp.where(_attn_allowed(i, sb * ATT_NB + c), s, -1e30) for c, s in enumerate(ss)]
            m_new = m
            for s in ss:
                m_new = jnp.maximum(m_new, jnp.max(s, axis=0, keepdims=True))
            alpha = jnp.exp(m - m_new)
            ps = [jnp.exp(s - m_new) for s in ss]
            l = alpha * l
            acc = alpha * acc
            for t0, p in zip(t0s, ps):
                l = l + jnp.sum(p, axis=0, keepdims=True)
                acc = acc + jnp.dot(vt_ref[:, pl.ds(t0, TC)], p.astype(BF16), preferred_element_type=F32)
            return m_new, l, acc

        nsb = ((i + 1) * TQ + ATT_KB - 1) // ATT_KB
        c = (jnp.full((1, TQ), -1e30, F32), jnp.zeros((1, TQ), F32), jnp.zeros((VDIM, TQ), F32))

        def step(sb, sc):
            nxt = scores(sb + 1)
            return nxt, block(sb, sc[0], sc[1], False)

        ss, c = lax.fori_loop(0, nsb - 1, step, (scores(0), c))
        m, l, acc = block(nsb - 1, ss, c, True)
        o_ref[...] = (acc / l).T
        l_ref[...] = m + jnp.log(l)

    return pl.pallas_call(
        body, name="attn_fwd", grid=(H, S // TQ),
        in_specs=[pl.BlockSpec((TQ, HP), lambda h, i: (i, h)),
                  pl.BlockSpec((S, HP), lambda h, i: (0, h)),
                  pl.BlockSpec((VDIM, S), lambda h, i: (h, 0))],
        out_specs=[pl.BlockSpec((TQ, VDIM), lambda h, i: (i, h)), pl.BlockSpec((None, 1, TQ), lambda h, i: (h, 0, i))],
        out_shape=[jax.ShapeDtypeStruct((S, MLA_W), F32), jax.ShapeDtypeStruct((H, 1, S), F32)],
        compiler_params=pltpu.CompilerParams(dimension_semantics=("parallel", "arbitrary"),
                                             vmem_limit_bytes=24 * MIB),
    )(qc, kc, vt)


def _gate_mul_fwd(name, val, proj, width, cb):
    def fn(i, rv, hv, fv):
        o, z = rv
        return [o * _silu(z)], []
    return _rows(fn, name, 256, [(val, width, 0), (proj, width, cb)], outs=[(width, BF16)])[0]


def _conv_fwd(proj, w, b):
    def fn(i, rv, hv, fv):
        (xb,), (halo,), (ww, bb) = rv, hv, fv
        halo = jnp.where(i > 0, halo, 0.0)
        row = lax.broadcasted_iota(jnp.int32, xb.shape, 0)
        acc = bb + ww[3:4] * xb
        for s in range(1, CONV_W):
            acc = acc + ww[3 - s:4 - s] * _shift_down(xb, halo, s, row)
        return [acc, acc], []
    return _rows(fn, "conv_fwd", 128, [(proj, LRU_W, O_XC // LRU_W)], halos=[(proj, LRU_W, O_XC // LRU_W, "prev")],
                 fulls=[w, b], outs=[(LRU_W, F32), (LRU_W, BF16)])


def _lru_terms(ga, gx, xc, ba, bx, lam):
    r = _sig(ga + ba)
    ig = _sig(gx + bx)
    sp = jnp.maximum(-lam, 0.0) + jnp.log(1.0 + jnp.exp(-jnp.abs(lam)))
    log_a = -LRU_C * r * sp
    a = jnp.exp(log_a)
    e2 = jnp.exp(2.0 * log_a)
    om = 1.0 - e2
    mult = jnp.sqrt(jnp.maximum(om, 0.0))
    return r, ig, sp, a, e2, om, mult


def _lru_gates_fwd(gates, xc, ba, bx, lam):
    def fn(i, rv, hv, fv):
        ga, gx, x = rv
        r, ig, sp, a, e2, om, mult = _lru_terms(ga, gx, x, *fv)
        return [a, mult * (ig * x)], []
    return _rows(fn, "lru_gates_fwd", 128, [(gates, LRU_W, 0), (gates, LRU_W, 1), (xc, LRU_W, 0)],
                 fulls=[ba, bx, lam], outs=[(LRU_W, F32), (LRU_W, F32)])


SCAN_T, SCAN_CW = 64, 256


def _scan_fwd(a, b):
    def body(a_ref, b_ref, h_ref):
        row = lax.broadcasted_iota(jnp.int32, (SCAN_T, SCAN_CW), 0)

        def step(blk, hc):
            t0 = pl.multiple_of(blk * SCAN_T, SCAN_T)
            A = a_ref[pl.ds(t0, SCAN_T), :]
            B = b_ref[pl.ds(t0, SCAN_T), :]
            d = 1
            while d < SCAN_T:
                keep = row >= d
                A_s = jnp.where(keep, pltpu.roll(A, d, 0), 1.0)
                B_s = jnp.where(keep, pltpu.roll(B, d, 0), 0.0)
                B = A * B_s + B
                A = A * A_s
                d *= 2
            hh = A * hc + B
            h_ref[pl.ds(t0, SCAN_T), :] = hh
            return hh[SCAN_T - 1:SCAN_T, :]

        lax.fori_loop(0, S // SCAN_T, step, jnp.zeros((1, SCAN_CW), F32))

    spec = pl.BlockSpec((S, SCAN_CW), lambda j: (0, j))
    return pl.pallas_call(
        body, name="scan_fwd", grid=(LRU_W // SCAN_CW,), in_specs=[spec, spec], out_specs=spec,
        out_shape=jax.ShapeDtypeStruct((S, LRU_W), F32),
        compiler_params=pltpu.CompilerParams(dimension_semantics=("parallel",),
                                             vmem_limit_bytes=_vmem(3 * _nbytes((S, SCAN_CW), F32))),
    )(a, b)


def _merge_fwd(pa, pb, pc, proj):
    def fn(i, rv, hv, fv):
        a, b, c, ga, gb, gc = rv
        return [_sig(ga) * a + _sig(gb) * b + _sig(gc) * c], []
    return _rows(fn, "merge_fwd", 256,
                 [(pa, D, 0), (pb, D, 0), (pc, D, 0), (proj, D, O_GA // D), (proj, D, O_GB // D), (proj, D, O_GC // D)],
                 outs=[(D, BF16)])[0]


def _post_fwd(x, o2, g):
    def fn(i, rv, hv, fv):
        xb, ob = rv
        return [xb + ob * _rms(ob) * fv[0]], []
    return _rows(fn, "post_fwd", 256, [(x, D, 0), (o2, D, 0)], fulls=[g], outs=[(D, F32)])[0]


SB = 640
BD_TM = 512


def _bd_fwd(xcb, wsb, l):
    def body(x_ref, w_ref, o_ref):
        o_ref[...] = jnp.dot(x_ref[...], w_ref[...], preferred_element_type=F32).astype(o_ref.dtype)

    return pl.pallas_call(
        body, name="lru_gate_mm", grid=(S // BD_TM, 4),
        in_specs=[pl.BlockSpec((BD_TM, SB), lambda i, q: (i, q % 2)),
                  pl.BlockSpec((None, None, SB, SB), lambda i, q: (l, q, 0, 0))],
        out_specs=pl.BlockSpec((BD_TM, SB), lambda i, q: (i, q)),
        out_shape=jax.ShapeDtypeStruct((S, 2 * LRU_W), BF16),
        compiler_params=pltpu.CompilerParams(dimension_semantics=("parallel", "parallel"), vmem_limit_bytes=VMEM_LIMIT),
    )(xcb, wsb)


def _bd_dx(dgates, wsb, l):
    def body(d_ref, w_ref, o_ref, acc_ref):
        g = pl.program_id(2)

        @pl.when(g == 0)
        def _():
            acc_ref[...] = jnp.zeros_like(acc_ref)

        acc_ref[...] += lax.dot_general(d_ref[...], w_ref[...], (((1,), (1,)), ((), ())), preferred_element_type=F32)

        @pl.when(g == 1)
        def _():
            o_ref[...] = acc_ref[...].astype(o_ref.dtype)

    return pl.pallas_call(
        body, name="lru_gate_dx", grid=(S // BD_TM, 2, 2),
        in_specs=[pl.BlockSpec((BD_TM, SB), lambda i, s, g: (i, 2 * g + s)),
                  pl.BlockSpec((None, None, SB, SB), lambda i, s, g: (l, 2 * g + s, 0, 0))],
        out_specs=pl.BlockSpec((BD_TM, SB), lambda i, s, g: (i, s)),
        out_shape=jax.ShapeDtypeStruct((S, LRU_W), BF16),
        scratch_shapes=[pltpu.VMEM((BD_TM, SB), F32)],
        compiler_params=pltpu.CompilerParams(dimension_semantics=("parallel", "parallel", "arbitrary"),
                                             vmem_limit_bytes=VMEM_LIMIT),
    )(dgates, wsb)


def _bd_dw(xcb, dgates):
    tk = 1024

    def body(x_ref, d_ref, o_ref):
        @pl.when(pl.program_id(1) == 0)
        def _():
            o_ref[...] = jnp.zeros_like(o_ref)

        o_ref[...] += lax.dot_general(x_ref[...], d_ref[...], (((0,), (0,)), ((), ())), preferred_element_type=F32)

    return pl.pallas_call(
        body, name="lru_gate_dw", grid=(4, S // tk),
        in_specs=[pl.BlockSpec((tk, SB), lambda q, k: (k, q % 2)), pl.BlockSpec((tk, SB), lambda q, k: (k, q))],
        out_specs=pl.BlockSpec((None, SB, SB), lambda q, k: (q, 0, 0)),
        out_shape=jax.ShapeDtypeStruct((4, SB, SB), F32),
        compiler_params=pltpu.CompilerParams(dimension_semantics=("parallel", "arbitrary"), vmem_limit_bytes=VMEM_LIMIT),
    )(xcb, dgates)


def _bd_extract(dwsb):
    def body(w_ref, o_ref):
        lane = lax.broadcasted_iota(jnp.int32, (LRU_BW, 128), 1)
        for q in range(4):
            for kk in range(8):
                c0 = LRU_BW * kk
                w0, off = (c0 // 128) * 128, c0 % 128
                rows = pl.ds(LRU_BW * kk, LRU_BW)
                blk = w_ref[q, rows, w0:w0 + 128]
                if off:
                    blk = pltpu.roll(blk, 128 - off, 1)
                    if off + LRU_BW > 128:
                        nxt = pltpu.roll(w_ref[q, rows, w0 + 128:w0 + 256], 128 - off, 1)
                        blk = jnp.where(lane < 128 - off, blk, nxt)
                o_ref[q // 2, 8 * (q % 2) + kk] = blk.astype(BF16)

    return pl.pallas_call(
        body, name="lru_gate_dw_blocks",
        in_specs=[pl.BlockSpec(memory_space=pltpu.VMEM)], out_specs=pl.BlockSpec(memory_space=pltpu.VMEM),
        out_shape=jax.ShapeDtypeStruct((2, LRU_NB, LRU_BW, 128), BF16),
        compiler_params=pltpu.CompilerParams(vmem_limit_bytes=VMEM_LIMIT),
    )(dwsb)


def _layer_fwd(x, P, l, tabs, token=None, late=None):
    A = {"x": x}
    A["h"] = _prenorm_fwd(x, P["pre_g"], token)
    proj = A["proj"] = _mm(A["h"], P["wp"], "nt", "in_proj", out_dtype=BF16, tm=1024)
    A["ya"] = _gmlp_fwd(proj, P["ln_g"], P["ln_b"], P["ws"], P["bst"])
    A["xc"], A["xcb"] = _conv_fwd(proj, P["conv_w"], P["conv_b"])
    A["gates"] = _bd_fwd(A["xcb"], P["wsb"], l)
    A["a"], bterm = _lru_gates_fwd(A["gates"], A["xc"], P["ba"], P["bx"], P["lam"])
    A["hs"] = _scan_fwd(A["a"], bterm)
    A["yc"] = _gate_mul_fwd("yc_fwd", A["hs"], proj, LRU_W, O_ZC // LRU_W)
    if late is not None:
        P.update(late(A["yc"]))
    A["cqn"], A["ckvn"] = _mla_prep_fwd(proj, P["qg"], P["kvg"])
    q = _mm(A["cqn"], P["wuq"], "nt", "q_up", out_dtype=BF16)
    kv = _mm(A["ckvn"], P["wukv"], "nt", "kv_up", out_dtype=BF16)
    A["qc"], A["kc"], A["kct"], A["vv"], vt = _rope_fwd(q, kv, proj, tabs)
    A["o"], A["lse"] = _attn_fwd(A["qc"], A["kc"], vt)
    A["yb"] = _gate_mul_fwd("yb_fwd", A["o"], proj, MLA_W, O_ZB // MLA_W)
    A["pa"] = _mm(A["ya"], P["wpa"], "nn", "proj_a", out_dtype=BF16)
    A["pb"] = _mm(A["yb"], P["wpb"], "nn", "proj_b", out_dtype=BF16)
    A["pc"] = _mm(A["yc"], P["wpc"], "nn", "proj_c", out_dtype=BF16)
    A["merged"] = _merge_fwd(A["pa"], A["pb"], A["pc"], proj)
    A["o2"] = _mm(A["merged"], P["wout"], "nn", "out_proj")
    return _post_fwd(x, A["o2"], P["post_g"]), A


def _loss_fwd(y, tgt):
    def fn(i, rv, hv, fv):
        yb, tb = rv
        e = yb - tb
        part = 0.5 * jnp.sum(jnp.mean(e * e, axis=-1, keepdims=True), axis=0, keepdims=True)
        return [e * (1.0 / D)], [part]
    return _rows(fn, "loss", 256, [(y, D, 0), (tgt, D, 0)], outs=[(D, F32)], accs=[(1, 1)])


def _post_bwd(dxn, o2, g, token=None):
    def fn(i, rv, hv, fv):
        dy, ob = rv
        dx, dg = _rms_bwd(dy, ob, fv[0])
        return [dx], [_colsum(dg)]
    return _rows(fn, "post_bwd", 256, [(dxn, D, 0), (o2, D, 0)], fulls=[g] + ([] if token is None else [token]),
                 outs=[(D, BF16)], accs=[(1, D)])


def _merge_bwd(dm, pa, pb, pc, proj, dproj):
    def fn(i, rv, hv, fv):
        d, a, b, c, ga, gb, gc = rv
        outs_p, outs_g = [], []
        for p, gg in ((a, ga), (b, gb), (c, gc)):
            s = _sig(gg)
            outs_p.append(d * s)
            outs_g.append(d * p * s * (1.0 - s))
        return outs_p + [jnp.concatenate(outs_g, axis=1)], []
    return _rows(fn, "merge_bwd", 128,
                 [(dm, D, 0), (pa, D, 0), (pb, D, 0), (pc, D, 0),
                  (proj, D, O_GA // D), (proj, D, O_GB // D), (proj, D, O_GC // D)],
                 outs=[(D, BF16)] * 3 + [(3 * D, BF16, (dproj, NP, O_GA // (3 * D)))])


def _gmlp_bwd(dya, proj, ln_g, ln_b, ws, bst, dproj):
    gw = GM_W // GM_G

    def fn(i, rv, hv, fv):
        dy, u, v, z = rv
        g, b, w, bt = fv
        vh, rs, vn = _gm_norm(v, g, b)
        sv = _gm_sv(vn, w, bt)
        sz = _silu(z)
        du = dy * sv * sz
        dsv = dy * u * sz
        dz = dy * u * sv * _dsilu(z)
        mask = _gm_mask()
        lane = lax.broadcasted_iota(jnp.int32, (GM_B, 128), 1)
        dvn_parts, dws, dbst = [], [], jnp.zeros((GM_B, 128), F32)
        for k in range(GM_G):
            wm = jnp.where(mask, w[k], 0.0).astype(BF16)
            dsk = dsv[:, k * gw:(k + 1) * gw]
            dskb = dsk.astype(BF16)
            dvn_parts.append(lax.dot_general(wm, dskb, (((0,), (0,)), ((), ())), preferred_element_type=F32))
            dwk = lax.dot_general(dskb, vn[:, k * gw:(k + 1) * gw].astype(BF16), (((1,), (1,)), ((), ())),
                                  preferred_element_type=F32)
            dws.append(jnp.where(mask, dwk, 0.0)[None])
            dbst = dbst + jnp.where(lane == k, jnp.sum(dsk, axis=1, keepdims=True), 0.0)
        dvn = jnp.concatenate(dvn_parts, axis=1)
        dvh = dvn * g
        dv = rs * (dvh - jnp.mean(dvh, axis=-1, keepdims=True) - vh * jnp.mean(dvh * vh, axis=-1, keepdims=True))
        return ([jnp.concatenate([du, dv, dz], axis=1)],
                [jnp.concatenate(dws, axis=0), dbst, _colsum(dvn * vh), _colsum(dvn)])
    return _rows(fn, "gmlp_bwd", GM_B, [(dya, GM_W, 0), (proj, GM_W, 0), (proj, GM_W, 1), (proj, GM_W, 2)],
                 fulls=[ln_g, ln_b, ws, bst], outs=[(3 * GM_W, BF16, (dproj, NP, O_U // (3 * GM_W)))],
                 accs=[(GM_G, GM_B, GM_B), (GM_B, 128), (1, GM_W), (1, GM_W)])


def _yb_bwd(dyb, o, proj, dproj):
    def fn(i, rv, hv, fv):
        dy, ob, z = rv
        do = dy * _silu(z)
        prod = do * ob
        lane = lax.broadcasted_iota(jnp.int32, (dy.shape[0], 128), 1)
        dl = jnp.zeros((dy.shape[0], 128), F32)
        for h in range(H):
            dl = dl + jnp.where(lane == h, jnp.sum(prod[:, h * VDIM:(h + 1) * VDIM], axis=1, keepdims=True), 0.0)
        return [do, dl, dy * ob * _dsilu(z)], []
    return _rows(fn, "yb_bwd", 256, [(dyb, MLA_W, 0), (o, MLA_W, 0), (proj, MLA_W, O_ZB // MLA_W)],
                 outs=[(MLA_W, BF16), (128, F32, "T"), (MLA_W, BF16, (dproj, NP, O_ZB // MLA_W))])


def _attn_bwd(qc, kc, kct, vv, do, lse, dlt):
    def body(q_ref, k_ref, kt_ref, v_ref, do_ref, l_ref, d_ref, dq_ref, dk_ref, dv_ref, dqt_ref):
        h, i = pl.program_id(0), pl.program_id(1)

        @pl.when(i == 0)
        def _():
            dk_ref[...] = jnp.zeros_like(dk_ref)
            dv_ref[...] = jnp.zeros_like(dv_ref)

        q = q_ref[...]
        dob = do_ref[...]
        lse = l_ref[...]
        dl = d_ref[pl.ds(h, 1), :]
        dqt_ref[...] = jnp.zeros_like(dqt_ref)

        def rows_of(sb, c):
            return pl.ds(pl.multiple_of((sb * ATT_NB + c) * TC, TC), TC)

        def front(sb):
            return [(lax.dot_general(k_ref[rows_of(sb, c), :], q, _NT, preferred_element_type=F32),
                     lax.dot_general(v_ref[rows_of(sb, c), :], dob, _NT, preferred_element_type=F32))
                    for c in range(ATT_NB)]

        def block(sb, sd, masked):
            dqt = None
            for c, (s, dp) in enumerate(sd):
                rows = rows_of(sb, c)
                p = jnp.exp(s - lse)
                if masked:
                    p = jnp.where(_attn_allowed(i, sb * ATT_NB + c), p, 0.0)
                ds = (p * (dp - dl)).astype(BF16)
                dk_ref[rows, :] += jnp.dot(ds, q, preferred_element_type=F32)
                dv_ref[rows, :] += jnp.dot(p.astype(BF16), dob, preferred_element_type=F32)
                part = jnp.dot(kt_ref[:, rows], ds, preferred_element_type=F32)
                dqt = part if dqt is None else dqt + part
            dqt_ref[...] += dqt

        def step(sb, sd):
            nxt = front(sb + 1)
            block(sb, sd, False)
            return nxt

        nsb = ((i + 1) * TQ + ATT_KB - 1) // ATT_KB
        sd = lax.fori_loop(0, nsb - 1, step, front(0))
        block(nsb - 1, sd, True)
        dq_ref[...] = dqt_ref[...].T.astype(dq_ref.dtype)

    blk = lambda w: pl.BlockSpec((TQ, w), lambda h, i: (i, h))
    head = lambda w: pl.BlockSpec((S, w), lambda h, i: (0, h))
    return pl.pallas_call(
        body, name="attn_bwd", grid=(H, S // TQ),
        in_specs=[blk(HP), head(HP), pl.BlockSpec((HP, S), lambda h, i: (h, 0)), head(VDIM), blk(VDIM),
                  pl.BlockSpec((None, 1, TQ), lambda h, i: (h, 0, i)), pl.BlockSpec((8, TQ), lambda h, i: (0, i))],
        out_specs=[blk(HP), head(HP), head(VDIM)],
        out_shape=[jax.ShapeDtypeStruct((S, H * HP), BF16), jax.ShapeDtypeStruct((S, H * HP), F32),
                   jax.ShapeDtypeStruct((S, MLA_W), F32)],
        scratch_shapes=[pltpu.VMEM((HP, TQ), F32)],
        compiler_params=pltpu.CompilerParams(dimension_semantics=("parallel", "arbitrary"),
                                             vmem_limit_bytes=28 * MIB),
    )(qc, kc, kct, vv, do, lse, dlt)


def _rope_bwd(dqc, dkc, dvv, tabs):
    def fn(i, rv, hv, fv):
        dq, dk, dv, cc, sa, sb = rv
        qs, ks = [], []
        dkr = jnp.zeros((dq.shape[0], 128), F32)
        for h in range(H):
            qs += [dq[:, h * HP:h * HP + 128] * ATT_SCALE, _rot_t(dq[:, h * HP + 128:(h + 1) * HP], cc, sa, sb) * ATT_SCALE]
            ks.append(dk[:, h * HP:h * HP + 128])
            dkr = dkr + dk[:, h * HP + 128:(h + 1) * HP]
        return [jnp.concatenate(qs, axis=1), jnp.concatenate(ks + [dv], axis=1), _rot_t(dkr, cc, sa, sb)], []
    cc, sa, sb = tabs
    return _rows(fn, "rope_bwd", 256,
                 [(dqc, H * HP, 0), (dkc, H * HP, 0), (dvv, MLA_W, 0), (cc, 128, 0), (sa, 128, 0), (sb, 128, 0)],
                 outs=[(H * HP, BF16), (H * 256, BF16), (128, BF16)])


MLA_GROUP = 1536


def _mla_prep_bwd(dcqn, dckvn, dkr, proj, qg, kvg, dproj):
    def fn(i, rv, hv, fv):
        d1, d2, dk, cq, ckv = rv
        g1, g2 = fv
        dx1, dg1 = _rms_bwd(d1, cq, g1)
        dx2, dg2 = _rms_bwd(d2, ckv, g2)
        zeros = jnp.zeros((d1.shape[0], MLA_GROUP - KVR - 128 - QR), F32)
        return [jnp.concatenate([dx2, dk.astype(F32), dx1, zeros], axis=1)], [_colsum(dg1), _colsum(dg2)]
    return _rows(fn, "mla_prep_bwd", 256,
                 [(dcqn, QR, 0), (dckvn, KVR, 0), (dkr, 128, 0), (proj, QR, O_CQ // QR), (proj, KVR, O_CKV // KVR)],
                 fulls=[qg, kvg], outs=[(MLA_GROUP, BF16, (dproj, NP, O_CKV // MLA_GROUP))], accs=[(1, QR), (1, KVR)])


def _yc_bwd(dyc, hs, proj, dproj):
    def fn(i, rv, hv, fv):
        dy, hh, z = rv
        return [dy * _silu(z), dy * hh * _dsilu(z)], []
    return _rows(fn, "yc_bwd", 128, [(dyc, LRU_W, 0), (hs, LRU_W, 0), (proj, LRU_W, O_ZC // LRU_W)],
                 outs=[(LRU_W, F32), (LRU_W, BF16, (dproj, NP, O_ZC // LRU_W))])


def _scan_bwd(a, hs, dh):
    nblk = S // SCAN_T

    def body(a_ref, h_ref, dh_ref, da_ref, db_ref):
        row = lax.broadcasted_iota(jnp.int32, (SCAN_T, SCAN_CW), 0)

        def step(j, carry):
            gc, ac = carry
            blk = nblk - 1 - j
            t0 = pl.multiple_of(blk * SCAN_T, SCAN_T)
            av = a_ref[pl.ds(t0, SCAN_T), :]
            A = jnp.where(row < SCAN_T - 1, pltpu.roll(av, SCAN_T - 1, 0), ac)
            B = dh_ref[pl.ds(t0, SCAN_T), :]
            d = 1
            while d < SCAN_T:
                keep = row < SCAN_T - d
                A_s = jnp.where(keep, pltpu.roll(A, SCAN_T - d, 0), 1.0)
                B_s = jnp.where(keep, pltpu.roll(B, SCAN_T - d, 0), 0.0)
                B = A * B_s + B
                A = A * A_s
                d *= 2
            g = A * gc + B
            p0 = pl.multiple_of(jnp.maximum(t0 - 8, 0), 8)
            last = jnp.where(blk > 0, h_ref[pl.ds(p0, 8), :][7:8, :], 0.0)
            h_prev = jnp.where(row >= 1, pltpu.roll(h_ref[pl.ds(t0, SCAN_T), :], 1, 0), last)
            da_ref[pl.ds(t0, SCAN_T), :] = g * h_prev
            db_ref[pl.ds(t0, SCAN_T), :] = g
            return g[0:1, :], av[0:1, :]

        z = jnp.zeros((1, SCAN_CW), F32)
        lax.fori_loop(0, nblk, step, (z, z))

    spec = pl.BlockSpec((S, SCAN_CW), lambda j: (0, j))
    return pl.pallas_call(
        body, name="scan_bwd", grid=(LRU_W // SCAN_CW,), in_specs=[spec] * 3, out_specs=[spec] * 2,
        out_shape=[jax.ShapeDtypeStruct((S, LRU_W), F32)] * 2,
        compiler_params=pltpu.CompilerParams(dimension_semantics=("parallel",),
                                             vmem_limit_bytes=_vmem(5 * _nbytes((S, SCAN_CW), F32))),
    )(a, hs, dh)


def _lru_gates_bwd(da, db, gates, xc, ba, bx, lam):
    def fn(i, rv, hv, fv):
        dav, dbv, ga, gx, x = rv
        bav, bxv, lamv = fv
        r, ig, sp, a, e2, om, mult = _lru_terms(ga, gx, x, bav, bxv, lamv)
        dmult = dbv * ig * x
        dig = dbv * mult * x
        dxc1 = dbv * mult * ig
        dlog_a = dav * a + jnp.where(om > 0.0, dmult * (-e2 / mult), 0.0)
        dr = dlog_a * (-LRU_C * sp)
        dga = dr * r * (1.0 - r)
        dgx = dig * ig * (1.0 - ig)
        dlam = _colsum(dlog_a * (-LRU_C * r)) * (-_sig(-lamv))
        return [jnp.concatenate([dga, dgx], axis=1), dxc1], [_colsum(dga), _colsum(dgx), dlam]
    return _rows(fn, "lru_gates_bwd", 128,
                 [(da, LRU_W, 0), (db, LRU_W, 0), (gates, LRU_W, 0), (gates, LRU_W, 1), (xc, LRU_W, 0)],
                 fulls=[ba, bx, lam], outs=[(2 * LRU_W, BF16), (LRU_W, F32)], accs=[(1, LRU_W)] * 3)


def _conv_bwd(dxc1, dxc2, proj, w, dproj):
    cb = O_XC // LRU_W

    def fn(i, rv, hv, fv):
        d1, d2, xb = rv
        n1, n2, xprev = hv
        ww = fv[0]
        last = i == S // 128 - 1
        dxc = d1 + d2
        nxt = jnp.where(last, 0.0, n1 + n2)
        xprev = jnp.where(i > 0, xprev, 0.0)
        row = lax.broadcasted_iota(jnp.int32, xb.shape, 0)
        dx = ww[3:4] * dxc
        dws = [None] * CONV_W
        dws[3] = _colsum(dxc * xb)
        for s in range(1, CONV_W):
            dx = dx + ww[3 - s:4 - s] * _shift_up(dxc, nxt, s, row)
            dws[3 - s] = _colsum(dxc * _shift_down(xb, xprev, s, row))
        return [dx], [jnp.concatenate(dws, axis=0), _colsum(dxc)]
    return _rows(fn, "conv_bwd", 128, [(dxc1, LRU_W, 0), (dxc2, LRU_W, 0), (proj, LRU_W, cb)],
                 halos=[(dxc1, LRU_W, 0, "next"), (dxc2, LRU_W, 0, "next"), (proj, LRU_W, cb, "prev")],
                 fulls=[w], outs=[(LRU_W, BF16, (dproj, NP, cb))], accs=[(CONV_W, LRU_W), (1, LRU_W)])


def _prenorm_bwd(dxn, dh, x, g):
    def fn(i, rv, hv, fv):
        dy, dhh, xb = rv
        dx, dg = _rms_bwd(dhh, xb, fv[0])
        return [dy + dx], [_colsum(dg)]
    return _rows(fn, "prenorm_bwd", 256, [(dxn, D, 0), (dh, D, 0), (x, D, 0)], fulls=[g], outs=[(D, F32)],
                 accs=[(1, D)])


def _layer_bwd(dxn, A, P, l, tabs, token=None, early=None):
    G, GB = {}, {}
    proj = A["proj"]

    def dw(key, a, b, name, **tiles):
        GB[key] = _mm(a, b, "tn", name, out_dtype=BF16, **tiles)

    do2, G["post_g"] = _post_bwd(dxn, A["o2"], P["post_g"], token)
    dm = _mm(do2, P["wout"], "nt", "out_proj_dx", out_dtype=BF16)
    dw("wout", A["merged"], do2, "out_proj_dw")
    dpa, dpb, dpc, dproj = _merge_bwd(dm, A["pa"], A["pb"], A["pc"], proj, None)
    dya = _mm(dpa, P["wpa"], "nt", "proj_a_dx", out_dtype=BF16)
    dw("wpa", A["ya"], dpa, "proj_a_dw")
    dyb = _mm(dpb, P["wpb"], "nt", "proj_b_dx", out_dtype=BF16)
    dw("wpb", A["yb"], dpb, "proj_b_dw")
    dyc = _mm(dpc, P["wpc"], "nt", "proj_c_dx", out_dtype=BF16)
    dw("wpc", A["yc"], dpc, "proj_c_dw")
    dproj, G["ws"], G["bst"], G["ln_g"], G["ln_b"] = _gmlp_bwd(dya, proj, P["ln_g"], P["ln_b"], P["ws"], P["bst"], dproj)
    do, dl, dproj = _yb_bwd(dyb, A["o"], proj, dproj)
    dqc, dkc, dvv = _attn_bwd(A["qc"], A["kc"], A["kct"], A["vv"], do, A["lse"], dl)
    dq, dkv, dkr = _rope_bwd(dqc, dkc, dvv, tabs)
    dcqn = _mm(dq, P["wuq"], "nn", "q_up_dx", out_dtype=BF16)
    dw("wuq", dq, A["cqn"], "q_up_dw")
    dckvn = _mm(dkv, P["wukv"], "nn", "kv_up_dx", out_dtype=BF16)
    dw("wukv", dkv, A["ckvn"], "kv_up_dw")
    dproj, G["qg"], G["kvg"] = _mla_prep_bwd(dcqn, dckvn, dkr, proj, P["qg"], P["kvg"], dproj)
    dhs, dproj = _yc_bwd(dyc, A["hs"], proj, dproj)
    da, db = _scan_bwd(A["a"], A["hs"], dhs)
    dgates, dxc1, G["ba"], G["bx"], G["lam"] = _lru_gates_bwd(da, db, A["gates"], A["xc"], P["ba"], P["bx"], P["lam"])
    dxc2 = _bd_dx(dgates, P["wsb"], l)
    G["wab"] = _bd_extract(_bd_dw(A["xcb"], dgates))
    dproj, G["conv_w"], G["conv_b"] = _conv_bwd(dxc1, dxc2, proj, P["conv_w"], dproj)
    tok = (None, None) if early is None else early(GB)
    dh = _mm(dproj, P["wp"], "nn", "in_proj_dx", tm=1024, tn=1024, token=tok[0])
    dw("wp", dproj, A["h"], "in_proj_dw", tm=1536, tn=1024, token=tok[1])
    dx, G["pre_g"] = _prenorm_bwd(dxn, dh, A["x"], P["pre_g"])
    return dx, G, GB


_ORIG_OFF = [0]
for _s in IN_SIZES:
    _ORIG_OFF.append(_ORIG_OFF[-1] + _s)
_PAD_OFF = {0: O_U, 1: O_V, 2: O_ZA, 3: O_CQ, 4: O_CKV, 5: O_KR, 6: O_ZB, 7: O_XC, 8: O_ZC, 9: O_GA, 10: O_GB, 11: O_GC}
SHARD_IN = N_IN // N_CHIPS


def _pieces_w_in(j):
    lo, hi = SHARD_IN * j, SHARD_IN * (j + 1)
    out = []
    for k in range(len(IN_SIZES)):
        a, b = max(lo, _ORIG_OFF[k]), min(hi, _ORIG_OFF[k + 1])
        if a < b:
            out.append((a - lo, _PAD_OFF[k] + a - _ORIG_OFF[k], b - a))
    return out


def _pieces_uq(j):
    return [(192 * hh, HP * (2 * j + hh), NOPE + ROPE) for hh in range(2)]


def _pieces_ukv(j):
    out = []
    for hh in range(2):
        h = 2 * j + hh
        out += [(256 * hh, NOPE * h, NOPE), (256 * hh + NOPE, H * NOPE + VDIM * h, VDIM)]
    return out


def _pieces_rows(r):
    return lambda j: [(0, r * j, r)]


LAYOUT = {
    "w_in": (SHARD_IN, NP, _pieces_w_in),
    "mla_w_uq": (2 * (NOPE + ROPE), H * HP, _pieces_uq),
    "mla_w_ukv": (2 * (NOPE + VDIM), 2 * H * 128, _pieces_ukv),
    "lru_conv_w": (1, N_CHIPS, _pieces_rows(1)),
    "w_proj_a": (GM_W // N_CHIPS, GM_W, _pieces_rows(GM_W // N_CHIPS)),
    "w_proj_b": (MLA_W // N_CHIPS, MLA_W, _pieces_rows(MLA_W // N_CHIPS)),
    "w_proj_c": (LRU_W // N_CHIPS, LRU_W, _pieces_rows(LRU_W // N_CHIPS)),
    "w_out": (D // N_CHIPS, D, _pieces_rows(D // N_CHIPS)),
}
TRANSPOSED = ("w_in", "mla_w_uq", "mla_w_ukv")


def _superblocks(w_a, w_x):
    w6 = jnp.stack([w_a, w_x], axis=1).reshape(DEPTH, 4, 8, LRU_BW, LRU_BW).astype(BF16)
    bands = [jnp.pad(w6[:, :, k], ((0, 0), (0, 0), (0, 0), (LRU_BW * k, SB - LRU_BW * (k + 1)))) for k in range(8)]
    return jnp.concatenate(bands, axis=2)


_HBM = pl.BlockSpec(memory_space=pltpu.HBM)


def _position():
    return lax.axis_index("x"), lax.axis_index("y"), lax.axis_index("c")


def _allgather(blocks, name):
    n = len(blocks)

    def body(*refs):
        ins, outs = refs[:n], refs[n:2 * n]
        send, recv, lsem = refs[2 * n:]
        x, y, c = _position()
        me, sib = (x, y, c), (x, y, 1 - c)
        chips = [(1 - x, y), (x, 1 - y), (1 - x, 1 - y)]

        def cp(k, a, block, to, src=None):
            dst = outs[a].at[4 * block[0] + 2 * block[1] + block[2]]
            return pltpu.make_async_remote_copy(src_ref=dst if src is None else src, dst_ref=dst,
                                                send_sem=send.at[7 * a + k], recv_sem=recv.at[7 * a + k],
                                                device_id=to, device_id_type=MESH)

        mine = [pltpu.make_async_copy(ins[a], outs[a].at[4 * x + 2 * y + c], lsem.at[a]) for a in range(n)]
        for m in mine:
            m.start()
        first = []
        for a in range(n):
            first.append(cp(0, a, me, sib, src=ins[a]))
            first += [cp(1 + j, a, me, (*chip, c), src=ins[a]) for j, chip in enumerate(chips)]
        for f in first:
            f.start()
        passed = []
        for j, chip in enumerate(chips):
            for a in range(n):
                cp(1 + j, a, (*chip, c), me).wait_recv()
                p = cp(4 + j, a, (*chip, c), sib)
                p.start()
                passed.append(p)
        for a in range(n):
            cp(0, a, sib, me).wait_recv()
            for j, chip in enumerate(chips):
                cp(4 + j, a, (*chip, 1 - c), me).wait_recv()
        for f in first + passed:
            f.wait_send()
        for m in mine:
            m.wait()

    return pl.pallas_call(
        body, name=name,
        out_shape=[jax.ShapeDtypeStruct((8,) + b.shape, b.dtype) for b in blocks],
        in_specs=[_HBM] * n, out_specs=[_HBM] * n,
        scratch_shapes=[pltpu.SemaphoreType.DMA((7 * n,)), pltpu.SemaphoreType.DMA((7 * n,)),
                        pltpu.SemaphoreType.DMA((n,))],
    )(*blocks)


_REL = (2, 1, 3)


def _cut(r):
    return r if r < 32 else (r // 2 + 15) // 16 * 16


def _half_rows(r, c0):
    return _cut(r) if c0 == 0 else r - _cut(r)


def _half_pieces(lay_a, jsrc, c0):
    r = lay_a[0]
    lo, hi = (0, _cut(r)) if c0 == 0 else (_cut(r), r)
    out = []
    for s0, d0, nr in lay_a[2](jsrc):
        a, b = max(s0, lo), min(s0 + nr, hi)
        if a < b:
            out.append((a, d0 + a - s0, b - a))
    return out


def _gather_zeros(names, srcs):
    return [jnp.zeros((LAYOUT[nm][1],) + s.shape[1:], s.dtype) for nm, s in zip(names, srcs)]


def _weights_allgather(names, srcs, name, carry=()):
    n = len(srcs)
    lay = [LAYOUT[nm] for nm in names]
    zeros = _gather_zeros(names, srcs)
    m = len(carry)

    def body(*refs):
        ins, outs = refs[:n], refs[2 * n + m:3 * n + m]
        send, recv, lsem = refs[3 * n + 2 * m:]
        x, y, c = _position()
        j = 2 * x + y
        sib = (x, y, 1 - c)
        chips = [(1 - x, y), (x, 1 - y), (1 - x, 1 - y)]

        def flow(a, k, jsrc, c0, to, from_src):
            cps = []
            for s0, d0, nr in _half_pieces(lay[a], jsrc, c0):
                dst = outs[a].at[pl.ds(d0, nr)]
                src = ins[a].at[pl.ds(s0, nr)] if from_src else dst
                cps.append(pltpu.make_async_remote_copy(src_ref=src, dst_ref=dst, send_sem=send.at[7 * a + k],
                                                        recv_sem=recv.at[7 * a + k], device_id=to, device_id_type=MESH))
            return cps

        def sized(a, k, rows):
            ref = ins[a].at[pl.ds(0, rows)]
            return pltpu.make_async_remote_copy(src_ref=ref, dst_ref=ref, send_sem=send.at[7 * a + k],
                                                recv_sem=recv.at[7 * a + k], device_id=sib, device_id_type=MESH)

        for j0 in range(N_CHIPS):
            for c0 in range(2):
                @pl.when((j == j0) & (c == c0))
                def _(j0=j0, c0=c0):
                    mine = [_half_rows(lay[a][0], c0) for a in range(n)]
                    theirs = [_half_rows(lay[a][0], 1 - c0) for a in range(n)]
                    for a in range(n):
                        for s0, d0, nr in _half_pieces(lay[a], j0, c0):
                            pltpu.make_async_copy(ins[a].at[pl.ds(s0, nr)], outs[a].at[pl.ds(d0, nr)], lsem.at[a]).start()
                    for a in range(n):
                        for cp in flow(a, 0, j0, c0, sib, True):
                            cp.start()
                        for k, chip in enumerate(chips):
                            for cp in flow(a, 1 + k, j0, c0, (*chip, c), True):
                                cp.start()
                    for k in range(3):
                        for a in range(n):
                            if mine[a]:
                                sized(a, 1 + k, mine[a]).wait_recv()
                                for cp in flow(a, 4 + k, j0 ^ _REL[k], c0, sib, False):
                                    cp.start()
                    for a in range(n):
                        if theirs[a]:
                            sized(a, 0, theirs[a]).wait_recv()
                            for k in range(3):
                                sized(a, 4 + k, theirs[a]).wait_recv()
                    for a in range(n):
                        if mine[a]:
                            for k in range(7):
                                sized(a, k, mine[a]).wait_send()
                            ref = ins[a].at[pl.ds(0, mine[a])]
                            pltpu.make_async_copy(ref, ref, lsem.at[a]).wait()

    res = pl.pallas_call(
        body, name=name,
        out_shape=[jax.ShapeDtypeStruct(z.shape, z.dtype) for z in list(zeros) + list(carry)],
        in_specs=[_HBM] * (2 * n + m), out_specs=[_HBM] * (n + m),
        input_output_aliases={n + a: a for a in range(n + m)},
        scratch_shapes=[pltpu.SemaphoreType.DMA((7 * n,)), pltpu.SemaphoreType.DMA((7 * n,)),
                        pltpu.SemaphoreType.DMA((n,))],
    )(*srcs, *zeros, *carry)
    return res[:n], res[n:]


_SEM = pl.BlockSpec(memory_space=pltpu.SEMAPHORE)
_VMEM_TOKEN = pl.BlockSpec(memory_space=pltpu.VMEM)
_TOKEN = jax.ShapeDtypeStruct((8, 128), F32)
_EFFECT = pltpu.SideEffectType.DATAFLOW_SIDE_EFFECTING


def _gather_start(names, srcs, zeros, name, after=None):
    n = len(srcs)
    lay = [LAYOUT[nm] for nm in names]
    extra = [] if after is None else [after]

    def body(*refs):
        ins, lands = refs[:n], refs[n:2 * n]
        send, recv, lsem = refs[2 * n + len(extra):2 * n + len(extra) + 3]
        refs[-1][...] = jnp.zeros_like(refs[-1])
        x, y, c = _position()
        j = 2 * x + y
        chips = [(1 - x, y), (x, 1 - y), (1 - x, 1 - y)]
        for j0 in range(N_CHIPS):
            @pl.when(j == j0)
            def _(j0=j0):
                for a in range(n):
                    for s0, d0, nr in lay[a][2](j0):
                        src, dst = ins[a].at[pl.ds(s0, nr)], lands[a].at[pl.ds(d0, nr)]
                        pltpu.make_async_copy(src, dst, lsem.at[a]).start()
                        for k, chip in enumerate(chips):
                            pltpu.make_async_remote_copy(src_ref=src, dst_ref=dst, send_sem=send.at[3 * a + k],
                                                         recv_sem=recv.at[3 * a + k], device_id=(*chip, c),
                                                         device_id_type=MESH).start()

    sems = [pltpu.SemaphoreType.DMA((3 * n,)), pltpu.SemaphoreType.DMA((3 * n,)), pltpu.SemaphoreType.DMA((n,))]
    hbm = lambda a: pltpu.HBM(a.shape, a.dtype)
    res = pl.pallas_call(
        body, name=name,
        out_shape=sems + [hbm(s) for s in srcs] + [hbm(z) for z in zeros] + [_TOKEN],
        in_specs=[_HBM] * (2 * n) + [pl.BlockSpec(memory_space=pl.ANY)] * len(extra),
        out_specs=[_SEM] * 3 + [_HBM] * (2 * n) + [_VMEM_TOKEN],
        input_output_aliases={a: 3 + a for a in range(2 * n)},
        compiler_params=pltpu.CompilerParams(has_side_effects=_EFFECT),
    )(*[pltpu.with_memory_space_constraint(s, pltpu.HBM) for s in srcs],
      *[pltpu.with_memory_space_constraint(z, pltpu.HBM) for z in zeros], *extra)
    return res[:3], res[3:3 + n], res[3 + n:3 + 2 * n], res[-1]


def _gather_wait(names, sems, srcs, lands, after, name):
    n = len(srcs)
    lay = [LAYOUT[nm] for nm in names]

    def body(*refs):
        ins, zones = refs[:n], refs[n:2 * n]
        send, recv, lsem = refs[2 * n:2 * n + 3]
        x, y, c = _position()
        for a in range(n):
            whole = zones[a].at[pl.ds(0, lay[a][0])]
            for k in range(3):
                cp = pltpu.make_async_remote_copy(src_ref=ins[a], dst_ref=whole, send_sem=send.at[3 * a + k],
                                                  recv_sem=recv.at[3 * a + k], device_id=(x, y, 1 - c),
                                                  device_id_type=MESH)
                cp.wait_send()
                cp.wait_recv()
            pltpu.make_async_copy(ins[a], whole, lsem.at[a]).wait()

    hbm = lambda a: pltpu.HBM(a.shape, a.dtype)
    res = pl.pallas_call(
        body, name=name,
        out_shape=[hbm(s) for s in srcs] + [hbm(z) for z in lands],
        in_specs=[_HBM] * (2 * n) + [_SEM] * 3 + [pl.BlockSpec(memory_space=pl.ANY)], out_specs=[_HBM] * (2 * n),
        input_output_aliases={a: a for a in range(2 * n)},
        compiler_params=pltpu.CompilerParams(has_side_effects=_EFFECT),
    )(*srcs, *lands, *sems, after)
    return res[n:]


def _clip_pieces(lay_a, jsrc, c0):
    h = lay_a[1] // 2
    lo, hi = c0 * h, (c0 + 1) * h
    out = []
    for s0, d0, nr in lay_a[2](jsrc):
        a, b = max(d0, lo), min(d0 + nr, hi)
        if a < b:
            out.append((s0 + a - d0, a, b - a))
    return out


def _rows_of(pieces):
    return sum(nr for _, _, nr in pieces)


def _both_cores(body_for):
    x, y, c = _position()
    j = 2 * x + y
    for j0 in range(N_CHIPS):
        for c0 in range(2):
            @pl.when((j == j0) & (c == c0))
            def _(j0=j0, c0=c0):
                body_for(j0, c0)


STAGE_ROWS = 512


def _staged_copy(src, dst, buf, sem_in, sem_out, rows):
    ch = buf.shape[0]
    for r in range(0, rows, ch):
        nr = min(ch, rows - r)
        stage = buf.at[pl.ds(0, nr)]
        cin = pltpu.make_async_copy(src.at[pl.ds(r, nr)], stage, sem_in)
        cin.start()
        cin.wait()
        cout = pltpu.make_async_copy(stage, dst.at[pl.ds(r, nr)], sem_out)
        cout.start()
        cout.wait()


def _half_to_sibling(names, gl, name, after=None):
    n = len(gl)
    halves = [LAYOUT[nm][1] // 2 for nm in names]
    extra = [] if after is None else [after]

    def body(*refs):
        ins, outs = refs[:n], refs[n + len(extra):2 * n + len(extra)]
        send, recv = refs[2 * n + len(extra):]
        x, y, c = _position()

        def run(j0, c0):
            cps = [pltpu.make_async_remote_copy(src_ref=ins[a].at[pl.ds((1 - c0) * halves[a], halves[a])], dst_ref=outs[a],
                                                send_sem=send.at[a], recv_sem=recv.at[a], device_id=(x, y, 1 - c),
                                                device_id_type=MESH) for a in range(n)]
            for cp in cps:
                cp.start()
            for cp in cps:
                cp.wait()

        _both_cores(run)

    return pl.pallas_call(
        body, name=name,
        out_shape=[jax.ShapeDtypeStruct((halves[a],) + gl[a].shape[1:], gl[a].dtype) for a in range(n)],
        in_specs=[_HBM] * n + [pl.BlockSpec(memory_space=pl.ANY)] * len(extra), out_specs=[_HBM] * n,
        scratch_shapes=[pltpu.SemaphoreType.DMA((n,)), pltpu.SemaphoreType.DMA((n,))],
    )(*gl, *extra)


def _chip_scatter_half(names, parts, name):
    n = len(parts)
    lay = [LAYOUT[nm] for nm in names]
    zeros = [jnp.zeros((N_CHIPS, lay[a][0]) + parts[a].shape[1:], parts[a].dtype) for a in range(n)]

    def body(*refs):
        ins, outs = refs[:n], refs[2 * n:3 * n]
        send, recv = refs[3 * n:3 * n + 2]
        stage, sem_in, sem_out = refs[3 * n + 2:4 * n + 2], refs[4 * n + 2], refs[4 * n + 3]
        x, y, c = _position()
        chips = [(1 - x, y), (x, 1 - y), (1 - x, 1 - y)]

        def run(j0, c0):
            def sized(a, rows):
                return outs[a].at[0, pl.ds(0, rows)]

            for a in range(n):
                base = c0 * (lay[a][1] // 2)
                for k, chip in enumerate(chips):
                    for s0, d0, nr in _clip_pieces(lay[a], j0 ^ _REL[k], c0):
                        pltpu.make_async_remote_copy(
                            src_ref=ins[a].at[pl.ds(d0 - base, nr)], dst_ref=outs[a].at[j0, pl.ds(s0, nr)],
                            send_sem=send.at[3 * a + k], recv_sem=recv.at[3 * a + k],
                            device_id=(*chip, c), device_id_type=MESH).start()
            for a in range(n):
                base = c0 * (lay[a][1] // 2)
                for s0, d0, nr in _clip_pieces(lay[a], j0, c0):
                    _staged_copy(ins[a].at[pl.ds(d0 - base, nr)], outs[a].at[j0, pl.ds(s0, nr)], stage[a],
                                 sem_in.at[a], sem_out.at[a], nr)
            for a in range(n):
                got = _rows_of(_clip_pieces(lay[a], j0, c0))
                for k in range(3):
                    sent = _rows_of(_clip_pieces(lay[a], j0 ^ _REL[k], c0))
                    if sent:
                        pltpu.make_async_remote_copy(src_ref=sized(a, sent), dst_ref=sized(a, sent),
                                                     send_sem=send.at[3 * a + k], recv_sem=recv.at[3 * a + k],
                                                     device_id=(x, y, c), device_id_type=MESH).wait_send()
                    if got:
                        pltpu.make_async_remote_copy(src_ref=sized(a, got), dst_ref=sized(a, got),
                                                     send_sem=send.at[3 * a + k], recv_sem=recv.at[3 * a + k],
                                                     device_id=(x, y, c), device_id_type=MESH).wait_recv()

        _both_cores(run)

    return pl.pallas_call(
        body, name=name,
        out_shape=[jax.ShapeDtypeStruct(z.shape, z.dtype) for z in zeros],
        in_specs=[_HBM] * (2 * n), out_specs=[_HBM] * n, input_output_aliases={n + a: a for a in range(n)},
        scratch_shapes=[pltpu.SemaphoreType.DMA((3 * n,)), pltpu.SemaphoreType.DMA((3 * n,))]
        + [pltpu.VMEM((min(STAGE_ROWS, p.shape[0]),) + p.shape[1:], p.dtype) for p in parts]
        + [pltpu.SemaphoreType.DMA((n,)), pltpu.SemaphoreType.DMA((n,))],
    )(*parts, *zeros)


def _subset_exchange(names, bufs, l, name):
    n = len(bufs)
    lay = [LAYOUT[nm] for nm in names]

    def body(*refs):
        outs = refs[n:2 * n]
        send, recv = refs[2 * n:]
        x, y, c = _position()

        def run(j0, c0):
            for a in range(n):
                for s0, _, nr in _clip_pieces(lay[a], j0, c0):
                    rows = outs[a].at[l, pl.ds(s0, nr)]
                    pltpu.make_async_remote_copy(src_ref=rows, dst_ref=rows, send_sem=send.at[a], recv_sem=recv.at[a],
                                                 device_id=(x, y, 1 - c), device_id_type=MESH).start()
            for a in range(n):
                for c_half, wait_send in ((c0, True), (1 - c0, False)):
                    rows = _rows_of(_clip_pieces(lay[a], j0, c_half))
                    if rows:
                        ref = outs[a].at[l, pl.ds(0, rows)]
                        cp = pltpu.make_async_remote_copy(src_ref=ref, dst_ref=ref, send_sem=send.at[a], recv_sem=recv.at[a],
                                                          device_id=(x, y, 1 - c), device_id_type=MESH)
                        if wait_send:
                            cp.wait_send()
                        else:
                            cp.wait_recv()

        _both_cores(run)

    return pl.pallas_call(
        body, name=name,
        out_shape=[jax.ShapeDtypeStruct(b.shape, b.dtype) for b in bufs],
        in_specs=[_HBM] * n, out_specs=[_HBM] * n, input_output_aliases={a: a for a in range(n)},
        scratch_shapes=[pltpu.SemaphoreType.DMA((n,)), pltpu.SemaphoreType.DMA((n,))],
    )(*bufs)


def _scatter_start(names, gl, name):
    n = len(gl)
    lay = [LAYOUT[nm] for nm in names]
    zones = [lax.empty((N_CHIPS, lay[a][0]) + gl[a].shape[1:], gl[a].dtype) for a in range(n)]

    def body(*refs):
        ins, lands = refs[:n], refs[n:2 * n]
        send, recv, lsem = refs[2 * n:2 * n + 3]
        refs[-1][...] = jnp.zeros_like(refs[-1])
        x, y, c = _position()
        j = 2 * x + y
        chips = [(1 - x, y), (x, 1 - y), (1 - x, 1 - y)]
        for j0 in range(N_CHIPS):
            @pl.when(j == j0)
            def _(j0=j0):
                for a in range(n):
                    for s0, d0, nr in lay[a][2](j0):
                        pltpu.make_async_copy(ins[a].at[pl.ds(d0, nr)], lands[a].at[j0, pl.ds(s0, nr)], lsem.at[a]).start()
                    for k, chip in enumerate(chips):
                        for s0, d0, nr in lay[a][2](j0 ^ _REL[k]):
                            pltpu.make_async_remote_copy(
                                src_ref=ins[a].at[pl.ds(d0, nr)], dst_ref=lands[a].at[j0, pl.ds(s0, nr)],
                                send_sem=send.at[3 * a + k], recv_sem=recv.at[3 * a + k],
                                device_id=(*chip, c), device_id_type=MESH).start()

    sems = [pltpu.SemaphoreType.DMA((3 * n,)), pltpu.SemaphoreType.DMA((3 * n,)), pltpu.SemaphoreType.DMA((n,))]
    hbm = lambda a: pltpu.HBM(a.shape, a.dtype)
    res = pl.pallas_call(
        body, name=name,
        out_shape=sems + [hbm(g) for g in gl] + [hbm(z) for z in zones] + [_TOKEN],
        in_specs=[_HBM] * (2 * n), out_specs=[_SEM] * 3 + [_HBM] * (2 * n) + [_VMEM_TOKEN],
        input_output_aliases={a: 3 + a for a in range(2 * n)},
        compiler_params=pltpu.CompilerParams(has_side_effects=_EFFECT),
    )(*[pltpu.with_memory_space_constraint(g, pltpu.HBM) for g in gl],
      *[pltpu.with_memory_space_constraint(z, pltpu.HBM) for z in zones])
    return res[:3], res[3:3 + n], res[3 + n:3 + 2 * n], res[-1]


def _scatter_wait(names, sems, srcs, lands, after, name):
    n = len(srcs)
    lay = [LAYOUT[nm] for nm in names]

    def body(*refs):
        zones = refs[n:2 * n]
        send, recv, lsem = refs[2 * n:2 * n + 3]
        x, y, c = _position()
        for a in range(n):
            whole = zones[a].at[0, pl.ds(0, lay[a][0])]
            for k in range(3):
                cp = pltpu.make_async_remote_copy(src_ref=whole, dst_ref=whole, send_sem=send.at[3 * a + k],
                                                  recv_sem=recv.at[3 * a + k], device_id=(x, y, 1 - c),
                                                  device_id_type=MESH)
                cp.wait_send()
                cp.wait_recv()
            pltpu.make_async_copy(whole, whole, lsem.at[a]).wait()

    hbm = lambda a: pltpu.HBM(a.shape, a.dtype)
    res = pl.pallas_call(
        body, name=name,
        out_shape=[hbm(s) for s in srcs] + [hbm(z) for z in lands],
        in_specs=[_HBM] * (2 * n) + [_SEM] * 3 + [pl.BlockSpec(memory_space=pl.ANY)], out_specs=[_HBM] * (2 * n),
        input_output_aliases={a: a for a in range(2 * n)},
        compiler_params=pltpu.CompilerParams(has_side_effects=_EFFECT),
    )(*srcs, *lands, *sems, after)
    return res[n:]


def _swap_start(arrs, name):
    n = len(arrs)
    zones = [lax.empty(a.shape, a.dtype) for a in arrs]

    def body(*refs):
        ins, lands = refs[:n], refs[n:2 * n]
        send, recv = refs[2 * n:2 * n + 2]
        refs[-1][...] = jnp.zeros_like(refs[-1])
        x, y, c = _position()
        for a in range(n):
            pltpu.make_async_remote_copy(src_ref=ins[a], dst_ref=lands[a], send_sem=send.at[a], recv_sem=recv.at[a],
                                         device_id=(x, y, 1 - c), device_id_type=MESH).start()

    sems = [pltpu.SemaphoreType.DMA((n,)), pltpu.SemaphoreType.DMA((n,))]
    hbm = lambda a: pltpu.HBM(a.shape, a.dtype)
    res = pl.pallas_call(
        body, name=name,
        out_shape=sems + [hbm(a) for a in arrs] + [hbm(z) for z in zones] + [_TOKEN],
        in_specs=[_HBM] * (2 * n), out_specs=[_SEM] * 2 + [_HBM] * (2 * n) + [_VMEM_TOKEN],
        input_output_aliases={a: 2 + a for a in range(2 * n)},
        compiler_params=pltpu.CompilerParams(has_side_effects=_EFFECT),
    )(*[pltpu.with_memory_space_constraint(a, pltpu.HBM) for a in arrs],
      *[pltpu.with_memory_space_constraint(z, pltpu.HBM) for z in zones])
    return res[:2], res[2:2 + n], res[2 + n:2 + 2 * n], res[-1]


def _swap_wait(sems, srcs, lands, after, name):
    n = len(srcs)

    def body(*refs):
        ins, zones = refs[:n], refs[n:2 * n]
        send, recv = refs[2 * n:2 * n + 2]
        x, y, c = _position()
        for a in range(n):
            cp = pltpu.make_async_remote_copy(src_ref=ins[a], dst_ref=zones[a], send_sem=send.at[a], recv_sem=recv.at[a],
                                              device_id=(x, y, 1 - c), device_id_type=MESH)
            cp.wait_send()
            cp.wait_recv()

    hbm = lambda a: pltpu.HBM(a.shape, a.dtype)
    res = pl.pallas_call(
        body, name=name,
        out_shape=[hbm(s) for s in srcs] + [hbm(z) for z in lands],
        in_specs=[_HBM] * (2 * n) + [_SEM] * 2 + [pl.BlockSpec(memory_space=pl.ANY)], out_specs=[_HBM] * (2 * n),
        input_output_aliases={a: a for a in range(2 * n)},
        compiler_params=pltpu.CompilerParams(has_side_effects=_EFFECT),
    )(*srcs, *lands, *sems, after)
    return res[:n], res[n:]


def _row_tile(r):
    for t in (256, 128, 64, 32, 16, 8):
        if r % t == 0 and r > t:
            return t
    return r


def _pair_add_half(g, rb, c_arr, name):
    hrows, rest = rb.shape[0], rb.shape[1:]
    tr = _row_tile(hrows)
    nb = hrows // tr
    z = (0,) * len(rest)

    def body(c_ref, g_ref, r_ref, o_ref):
        o_ref[...] = (g_ref[...].astype(F32) + r_ref[...].astype(F32)).astype(o_ref.dtype)

    return pl.pallas_call(
        body, name=name,
        grid_spec=pltpu.PrefetchScalarGridSpec(
            num_scalar_prefetch=1, grid=(nb,),
            in_specs=[pl.BlockSpec((tr,) + rest, lambda i, c_ref: (c_ref[0] * nb + i,) + z),
                      pl.BlockSpec((tr,) + rest, lambda i, c_ref: (i,) + z)],
            out_specs=pl.BlockSpec((tr,) + rest, lambda i, c_ref: (i,) + z)),
        out_shape=jax.ShapeDtypeStruct((hrows,) + rest, BF16),
        compiler_params=pltpu.CompilerParams(dimension_semantics=("parallel",), vmem_limit_bytes=VMEM_LIMIT),
    )(c_arr, g, rb)


def _sum_slabs(slabs, l, buf, name):
    m = len(slabs)
    n, R, rest = slabs[0].shape[0], slabs[0].shape[1], slabs[0].shape[2:]
    tr = _row_tile(R)
    z = (0,) * len(rest)

    def body(*refs):
        total = None
        for r_ref in refs[:m]:
            acc = r_ref[0].astype(F32)
            for k in range(1, n):
                acc = acc + r_ref[k].astype(F32)
            total = acc if total is None else total + acc
        refs[-1][...] = total

    if R // tr > 64 and len(rest) == 1 and rest[0] % 256 == 0:
        grid = (rest[0] // 256,)
        in_spec = pl.BlockSpec((n, R, 256), lambda i: (0, 0, i))
        out_spec = pl.BlockSpec((None, R, 256), lambda i: (l, 0, i))
    else:
        grid = (R // tr,)
        in_spec = pl.BlockSpec((n, tr) + rest, lambda i: (0, i) + z)
        out_spec = pl.BlockSpec((None, tr) + rest, lambda i: (l, i) + z)
    in_specs, args, aliases = [in_spec] * m, list(slabs), {}
    if buf is not None:
        in_specs.append(pl.BlockSpec(memory_space=pl.ANY))
        args.append(buf)
        aliases = {m: 0}
    return pl.pallas_call(
        body, name=name, grid=grid, in_specs=in_specs, out_specs=out_spec,
        out_shape=jax.ShapeDtypeStruct((DEPTH, R) + rest, F32), input_output_aliases=aliases,
        compiler_params=pltpu.CompilerParams(
            dimension_semantics=("parallel",),
            vmem_limit_bytes=_vmem(m * _nbytes(in_spec.block_shape, slabs[0].dtype) + _nbytes(out_spec.block_shape, F32),
                                   2 * _nbytes(out_spec.block_shape, F32))),
    )(*args)


def _adam_math(w, g, m, v):
    mn = ADAM_B1 * m + (1.0 - ADAM_B1) * g
    vn = ADAM_B2 * v + (1.0 - ADAM_B2) * (g * g)
    m_hat = mn / (1.0 - ADAM_B1 ** ADAM_STEP)
    v_hat = vn / (1.0 - ADAM_B2 ** ADAM_STEP)
    return -ADAM_LR * (m_hat / (jnp.sqrt(v_hat) + ADAM_EPS) + ADAM_WD * w), mn, vn


def _adamw(w, g, m, v, name):
    L, R, C = w.shape
    tr = _row_tile(R)

    def body(w_ref, g_ref, m_ref, v_ref, d_ref, mo_ref, vo_ref):
        d_ref[...], mo_ref[...], vo_ref[...] = _adam_math(w_ref[...], g_ref[...], m_ref[...], v_ref[...])

    if R // tr > 64 and C % 128 == 0:
        spec, grid = pl.BlockSpec((None, R, 128), lambda l, i: (l, 0, i)), (L, C // 128)
    else:
        spec, grid = pl.BlockSpec((None, tr, C), lambda l, i: (l, i, 0)), (L, R // tr)
    return pl.pallas_call(
        body, name=name, grid=grid, in_specs=[spec] * 4, out_specs=[spec] * 3,
        out_shape=[jax.ShapeDtypeStruct((L, R, C), F32)] * 3,
        compiler_params=pltpu.CompilerParams(dimension_semantics=("parallel", "parallel"),
                                             vmem_limit_bytes=_vmem(7 * _nbytes(spec.block_shape, F32))),
    )(w, g, m, v)


_VMEM_WHOLE = pl.BlockSpec(memory_space=pltpu.VMEM)


def _matrix_update(gath, w, m, v, name):
    K = w.shape[1]

    def body(g0_ref, g1_ref, w_ref, m_ref, v_ref, go_ref, d_ref, mo_ref, vo_ref):
        for l, gr in enumerate((g0_ref, g1_ref)):
            for k in range(K):
                g = gr[0, k].astype(F32)
                for dev in range(1, 8):
                    g = g + gr[dev, k].astype(F32)
                go_ref[l, k] = g
                d_ref[l, k], mo_ref[l, k], vo_ref[l, k] = _adam_math(w_ref[l, k], g, m_ref[l, k], v_ref[l, k])

    return pl.pallas_call(
        body, name=name, in_specs=[_VMEM_WHOLE] * 5, out_specs=[_VMEM_WHOLE] * 4,
        out_shape=[jax.ShapeDtypeStruct(w.shape, F32)] * 4,
        compiler_params=pltpu.CompilerParams(vmem_limit_bytes=32 * MIB),
    )(gath[0], gath[1], w, m, v)


VECS = (("pre_norm_g", D), ("post_norm_g", D), ("gm_ln_g", GM_W), ("gm_ln_b", GM_W), ("mla_q_norm_g", QR),
        ("mla_kv_norm_g", KVR), ("lru_conv_b", LRU_W), ("lru_b_a", LRU_W), ("lru_b_x", LRU_W), ("lru_lambda", LRU_W))
VEC_KEY = {"pre_norm_g": "pre_g", "post_norm_g": "post_g", "gm_ln_g": "ln_g", "gm_ln_b": "ln_b", "mla_q_norm_g": "qg",
           "mla_kv_norm_g": "kvg", "lru_conv_b": "conv_b", "lru_b_a": "ba", "lru_b_x": "bx", "lru_lambda": "lam"}
VEC_ROWS, VEC_W, VEC_ROW0, LOSS_ROW = 16, LRU_W, GM_G, 14


def _pack_rows(LG, loss_part):
    per = len(VECS) + 1
    ins = []
    for G in LG:
        ins += [G[VEC_KEY[n]] for n, _ in VECS] + [G["bst"]]
    ins.append(loss_part)

    def body(*refs):
        o_ref = refs[-1]
        o_ref[...] = jnp.zeros_like(o_ref)
        for l in range(DEPTH):
            base = VEC_ROWS * l
            o_ref[pl.ds(base, 8), pl.ds(0, GM_B)] = refs[per * l + len(VECS)][...].T[:8, :]
            for t, (_, width) in enumerate(VECS):
                o_ref[pl.ds(base + VEC_ROW0 + t, 1), pl.ds(0, width)] = refs[per * l + t][...]
        o_ref[pl.ds(LOSS_ROW, 1), pl.ds(0, 128)] = jnp.broadcast_to(refs[-2][...], (1, 128))

    return pl.pallas_call(
        body, name="pack_rows", in_specs=[_VMEM_WHOLE] * len(ins), out_specs=_VMEM_WHOLE,
        out_shape=jax.ShapeDtypeStruct((DEPTH * VEC_ROWS, VEC_W), F32),
    )(*ins)


def _vector_update(gath, W, M, V):
    names = [n for n, _ in VECS] + ["gm_bs"]
    nw = len(names)

    def body(*refs):
        g_ref = refs[0]
        wr, mr, vr = refs[1:1 + nw], refs[1 + nw:1 + 2 * nw], refs[1 + 2 * nw:1 + 3 * nw]
        outs = refs[1 + 3 * nw:]
        s = g_ref[0]
        for dev in range(1, 8):
            s = s + g_ref[dev]
        for t, (_, width) in enumerate(VECS):
            for l in range(DEPTH):
                r = VEC_ROWS * l + VEC_ROW0 + t
                g = s[r:r + 1, :width]
                row = (pl.ds(l, 1), slice(None))
                res = (g,) + _adam_math(wr[t][row], g, mr[t][row], vr[t][row])
                for q in range(4):
                    outs[4 * t + q][row] = res[q]
        t = len(VECS)
        for l in range(DEPTH):
            for k in range(GM_G):
                g = s[VEC_ROWS * l + k:VEC_ROWS * l + k + 1, :GM_B]
                row = (l, pl.ds(k, 1), slice(None))
                res = (g,) + _adam_math(wr[t][row], g, mr[t][row], vr[t][row])
                for q in range(4):
                    outs[4 * t + q][row] = res[q]
        outs[4 * nw][...] = s[LOSS_ROW:LOSS_ROW + 1, :128]

    ws = [W[n] for n in names]
    out_shape = []
    for w in ws:
        out_shape += [jax.ShapeDtypeStruct(w.shape, F32)] * 4
    out_shape.append(jax.ShapeDtypeStruct((1, 128), F32))
    res = pl.pallas_call(
        body, name="vector_update", in_specs=[_VMEM_WHOLE] * (1 + 3 * nw), out_specs=[_VMEM_WHOLE] * (4 * nw + 1),
        out_shape=out_shape, compiler_params=pltpu.CompilerParams(vmem_limit_bytes=VMEM_LIMIT),
    )(gath, *ws, *[M[n] for n in names], *[V[n] for n in names])
    return {n: tuple(res[4 * t:4 * t + 4]) for t, n in enumerate(names)}, res[4 * nw]


SHARDED = ("w_in", "mla_w_uq", "mla_w_ukv", "lru_conv_w", "w_proj_a", "w_proj_b", "w_proj_c", "w_out")
FIRST = ("w_in", "lru_conv_w")
LATER = tuple(n for n in SHARDED if n not in FIRST)
COL_SHARDED = ("w_in", "mla_w_uq", "mla_w_ukv", "lru_conv_w")
SMALL = ("pre_norm_g", "gm_ln_g", "gm_ln_b", "gm_ws", "gm_bs", "mla_q_norm_g", "mla_kv_norm_g", "lru_conv_b",
         "lru_w_a", "lru_b_a", "lru_w_x", "lru_b_x", "lru_lambda", "post_norm_g")
WEIGHTS = ("pre_norm_g", "w_in", "gm_ln_g", "gm_ln_b", "gm_ws", "gm_bs", "mla_q_norm_g", "mla_w_uq",
           "mla_kv_norm_g", "mla_w_ukv", "lru_conv_w", "lru_conv_b", "lru_w_a", "lru_b_a", "lru_w_x", "lru_b_x",
           "lru_lambda", "w_proj_a", "w_proj_b", "w_proj_c", "w_out", "post_norm_g")


GB_KEY = {"w_in": "wp", "mla_w_uq": "wuq", "mla_w_ukv": "wukv", "w_proj_a": "wpa", "w_proj_b": "wpb",
          "w_proj_c": "wpc", "w_out": "wout"}


def _prepare(l, gathered, small, wsb):
    P = {GB_KEY[n]: gathered[n] for n in GB_KEY if n in gathered}
    P["conv_w"] = gathered["lru_conv_w"].transpose(1, 0, 2).reshape(CONV_W, LRU_W)
    P["wsb"] = wsb
    row = lambda n: small[n][l][None, :]
    P["pre_g"], P["post_g"] = row("pre_norm_g"), row("post_norm_g")
    P["ln_g"], P["ln_b"] = row("gm_ln_g"), row("gm_ln_b")
    P["ws"] = small["gm_ws"][l]
    P["bst"] = jnp.pad(small["gm_bs"][l].T, ((0, 0), (0, 128 - GM_G)))
    P["qg"], P["kvg"] = row("mla_q_norm_g"), row("mla_kv_norm_g")
    P["conv_b"], P["ba"], P["bx"], P["lam"] = row("lru_conv_b"), row("lru_b_a"), row("lru_b_x"), row("lru_lambda")
    return P


def kernel(x, pre_norm_g, w_in, gm_ln_g, gm_ln_b, gm_ws, gm_bs, mla_q_norm_g, mla_w_uq, mla_kv_norm_g, mla_w_ukv, lru_conv_w, lru_conv_b, lru_w_a, lru_b_a, lru_w_x, lru_b_x, lru_lambda, w_proj_a, w_proj_b, w_proj_c, w_out, post_norm_g, loss_target, m_pre_norm_g, m_w_in, m_gm_ln_g, m_gm_ln_b, m_gm_ws, m_gm_bs, m_mla_q_norm_g, m_mla_w_uq, m_mla_kv_norm_g, m_mla_w_ukv, m_lru_conv_w, m_lru_conv_b, m_lru_w_a, m_lru_b_a, m_lru_w_x, m_lru_b_x, m_lru_lambda, m_w_proj_a, m_w_proj_b, m_w_proj_c, m_w_out, m_post_norm_g, v_pre_norm_g, v_w_in, v_gm_ln_g, v_gm_ln_b, v_gm_ws, v_gm_bs, v_mla_q_norm_g, v_mla_w_uq, v_mla_kv_norm_g, v_mla_w_ukv, v_lru_conv_w, v_lru_conv_b, v_lru_w_a, v_lru_b_a, v_lru_w_x, v_lru_b_x, v_lru_lambda, v_w_proj_a, v_w_proj_b, v_w_proj_c, v_w_out, v_post_norm_g):
    args = dict(locals())
    W = {n: args[n] for n in WEIGHTS}
    M = {n: args["m_" + n] for n in WEIGHTS}
    V = {n: args["v_" + n] for n in WEIGHTS}
    c = lax.axis_index("c")

    def shards(l, names):
        out = []
        for n in names:
            blk = W[n][l].T if n in TRANSPOSED else W[n][l]
            out.append(blk[None] if n == "lru_conv_w" else blk.astype(BF16))
        return out

    small = {n: W[n] for n in SMALL}
    wsb = _superblocks(W["lru_w_a"], W["lru_w_x"])
    tabs = _rope_tables()
    s0a, s0b, s1a, s1b = shards(0, FIRST), shards(0, LATER), shards(1, FIRST), shards(1, LATER)
    g0, zones = _weights_allgather(FIRST, s0a, "weights_allgather_l0", carry=_gather_zeros(LATER, s0b)
                                   + _gather_zeros(FIRST, s1a) + _gather_zeros(LATER, s1b))
    nl, nf = len(LATER), len(FIRST)
    w0b = _gather_start(LATER, s0b, zones[:nl], "weights_gather_start_l0")
    w1a = _gather_start(FIRST, s1a, zones[nl:nl + nf], "weights_gather_start_l1_first", after=w0b[3])
    w1b = _gather_start(LATER, s1b, zones[nl + nf:], "weights_gather_start_l1_later", after=w1a[3])

    def late(started, name):
        def wait(after):
            got = _gather_wait(LATER, *started[:3], after, name)
            return {GB_KEY[n]: g for n, g in zip(LATER, got)}
        return wait

    P = [_prepare(0, dict(zip(FIRST, g0)), small, wsb), None]
    h0 = x[0]
    h1, A0 = _layer_fwd(h0, P[0], 0, tabs, w1b[3], late(w0b, "weights_gather_wait_l0"))
    g1 = _gather_wait(FIRST, *w1a[:3], h1, "weights_gather_wait_l1_first")
    P[1] = _prepare(1, dict(zip(FIRST, g1)), small, wsb)
    h2, A1 = _layer_fwd(h1, P[1], 1, tabs, None, late(w1b, "weights_gather_wait_l1_later"))
    dy, loss_part = _loss_fwd(h2, loss_target[0])

    def large_grads(G, GB, names):
        conv = G["conv_w"].reshape(CONV_W, N_CHIPS, LRU_W // N_CHIPS).transpose(1, 0, 2)
        return [conv if n == "lru_conv_w" else GB[GB_KEY[n]] for n in names]

    d1, G1, GB1 = _layer_bwd(dy, A1, P[1], 1, tabs)
    sc1 = _scatter_start(SHARDED, large_grads(G1, GB1, SHARDED), "grads_scatter_start_l1")
    started = {}

    def early0(GB):
        mine1 = _scatter_wait(SHARDED, *sc1[:3], GB["wukv"], "grads_scatter_wait_l1")
        started["swap1"] = _swap_start(mine1, "partials_swap_start_l1")
        started["sc0"] = _scatter_start(LATER, [GB[GB_KEY[n]] for n in LATER], "grads_scatter_start_l0")
        return started["sc0"][3], started["swap1"][3]

    d0, G0, GB0 = _layer_bwd(d1, A0, P[0], 0, tabs, sc1[3], early0)
    LG = (G0, G1)
    mine0 = _scatter_wait(LATER, *started["sc0"][:3], d0, "grads_scatter_wait_l0")
    swap0 = _swap_start(mine0, "partials_swap_start_l0")
    g0f = large_grads(G0, GB0, FIRST)
    c_arr = jnp.reshape(c, (1,)).astype(jnp.int32)
    from_sib = _half_to_sibling(FIRST, g0f, "grads_half_to_sibling_l0", after=swap0[3])
    pair = [_pair_add_half(g, rb, c_arr, "pair_add_" + n) for n, g, rb in zip(FIRST, g0f, from_sib)]
    slabs = _chip_scatter_half(FIRST, pair, "grads_chip_scatter_l0")
    mine1, theirs1 = _swap_wait(*started["swap1"][:3], slabs[0], "partials_swap_wait_l1")
    both = dict(zip(SHARDED, [_sum_slabs([a, b], 1, None, "sum_partials_l1_" + n)
                              for n, a, b in zip(SHARDED, mine1, theirs1)]))
    for n, s in zip(FIRST, slabs):
        both[n] = _sum_slabs([s], 0, both[n], "sum_slabs_l0_" + n)
    done = _subset_exchange(FIRST, [both[n] for n in FIRST], 0, "reduced_rows_to_sibling_l0")
    both.update(zip(FIRST, done))
    mine0, theirs0 = _swap_wait(*swap0[:3], done[0], "partials_swap_wait_l0")
    for n, a, b in zip(LATER, mine0, theirs0):
        both[n] = _sum_slabs([a, b], 0, both[n], "sum_partials_l0_" + n)
    both = [both[n] for n in SHARDED]
    grads = {}
    for n, b in zip(SHARDED, both):
        if n in TRANSPOSED and n != "w_in":
            b = jnp.swapaxes(b, 1, 2)
        grads[n] = b if n == "w_in" else b.reshape(W[n].shape)

    rows = _pack_rows(LG, loss_part)
    mats = []
    for g in LG:
        mats += [g["ws"].astype(BF16), g["wab"][0, :, :, :LRU_BW], g["wab"][1, :, :, :LRU_BW]]
    gath = _allgather([rows] + mats, "small_grads_allgather")
    upd, loss_row = _vector_update(gath[0], W, M, V)
    loss = loss_row[0, 0]
    for k, n in enumerate(("gm_ws", "lru_w_a", "lru_w_x")):
        upd[n] = _matrix_update((gath[1 + k], gath[4 + k]), W[n], M[n], V[n], "update_" + n)

    for n in SHARDED:
        if n == "w_in":
            tr = lambda a: jnp.swapaxes(a, 1, 2)
            res = _adamw(tr(W[n]), grads[n], tr(M[n]), tr(V[n]), "adamw_" + n)
            upd[n] = tuple(tr(a) for a in (grads[n],) + tuple(res))
        else:
            upd[n] = (grads[n],) + tuple(_adamw(W[n], grads[n], M[n], V[n], "adamw_" + n))

    return (loss, d0[None], *[upd[n][0] for n in WEIGHTS], *[upd[n][1] for n in WEIGHTS],
            *[upd[n][2] for n in WEIGHTS], *[upd[n][3] for n in WEIGHTS])
```

```python
import functools
import math

import jax
import jax.numpy as jnp
from jax import lax
from jax.experimental import pallas as pl
from jax.experimental.pallas import tpu as pltpu

F32, BF16 = jnp.float32, jnp.bfloat16
MESH = pl.DeviceIdType.MESH

S, D, DEPTH = 2048, 1024, 2
CHUNK, EPS = 64, 1e-6
GM_W, GM_G, GM_B = 1024, 4, 128
H, NOPE, ROPE, VDIM = 8, 128, 64, 128
QR, KVR = 384, 256
MLA_W = H * VDIM
LRU_W, LRU_NB, LRU_BW, LRU_C, CONV_W = 1280, 16, 80, 8.0, 4
ROPE_THETA = 10000.0
IN_SIZES = (GM_W, GM_W, GM_W, QR, KVR, ROPE, MLA_W, LRU_W, LRU_W, D, D, D)
N_IN = sum(IN_SIZES)
N_CHIPS = 4
ADAM_LR, ADAM_B1, ADAM_B2, ADAM_EPS, ADAM_WD, ADAM_STEP = 0.001, 0.9, 0.999, 1e-08, 0.01, 10

HP = 256
O_U, O_V, O_ZA, O_GA, O_GB, O_GC = 0, 1024, 2048, 3072, 4096, 5120
O_CKV, O_KR, O_CQ, O_XC, O_ZC, O_ZB = 6144, 6400, 6528, 7680, 8960, 10240
NP = 11264
MIB = 1024 * 1024
VMEM_LIMIT = 16 * MIB


def _vmem(block_bytes, temp_bytes=0):
    return int(min(max(2 * block_bytes + temp_bytes + 4 * MIB, VMEM_LIMIT), 56 * MIB))


def _nbytes(shape, dtype):
    return math.prod(d for d in shape if d is not None) * jnp.dtype(dtype).itemsize


def _tile(dim, target):
    if dim <= target:
        return dim
    t = (target // 128) * 128
    while dim % t:
        t -= 128
    return t


def _sig(x):
    return jax.nn.sigmoid(x)


def _silu(x):
    return x * _sig(x)


def _dsilu(x):
    s = _sig(x)
    return s * (1.0 + x * (1.0 - s))


def _mm(a, b, mode, name, out_dtype=F32, tm=512, tn=512, tk=1024, b_lead=None, out_lead=None, token=None):
    b2 = b.shape[1:] if b_lead is not None else b.shape
    if mode == "nn":
        (M, K), (K2, N) = a.shape, b2
    elif mode == "nt":
        (M, K), (N, K2) = a.shape, b2
    else:
        (K, M), (K2, N) = a.shape, b2
    assert K == K2, (name, a.shape, b.shape)
    tm, tn, tk = _tile(M, tm), _tile(N, tn), _tile(K, tk)
    nk = K // tk
    if mode == "tn":
        a_spec = pl.BlockSpec((tk, tm), lambda i, j, k: (k, i))
        lhs_c = 0
    else:
        a_spec = pl.BlockSpec((tm, tk), lambda i, j, k: (i, k))
        lhs_c = 1
    b_blk, b_idx, rhs_c = ((tn, tk), (lambda i, j, k: (j, k)), 1) if mode == "nt" else ((tk, tn), (lambda i, j, k: (k, j)), 0)
    if b_lead is None:
        b_spec = pl.BlockSpec(b_blk, b_idx)
    else:
        b_spec = pl.BlockSpec((None,) + b_blk, functools.partial(lambda i, j, k, f, l: (l,) + f(i, j, k), f=b_idx, l=b_lead))
    dims = (((lhs_c,), (rhs_c,)), ((), ()))
    in_specs, args, aliases = [a_spec, b_spec], [a, b], {}
    if out_lead is None:
        out_spec = pl.BlockSpec((tm, tn), lambda i, j, k: (i, j))
        out_shape = jax.ShapeDtypeStruct((M, N), out_dtype)
    else:
        l_out, n_lead, buf = out_lead
        out_spec = pl.BlockSpec((None, tm, tn), functools.partial(lambda i, j, k, l: (l, i, j), l=l_out))
        out_shape = jax.ShapeDtypeStruct((n_lead, M, N), out_dtype)
        if buf is not None:
            in_specs.append(pl.BlockSpec(memory_space=pl.ANY))
            args.append(buf)
            aliases = {2: 0}
    if token is not None:
        in_specs.append(pl.BlockSpec(memory_space=pl.ANY))
        args.append(token)

    def body(a_ref, b_ref, *rest):
        o_ref, acc_ref = rest[-2:]
        k = pl.program_id(2)

        @pl.when(k == 0)
        def _():
            acc_ref[...] = jnp.zeros_like(acc_ref)

        acc_ref[...] += lax.dot_general(a_ref[...].astype(BF16), b_ref[...].astype(BF16), dims,
                                        preferred_element_type=F32)

        @pl.when(k == nk - 1)
        def _():
            o_ref[...] = acc_ref[...].astype(o_ref.dtype)

    return pl.pallas_call(
        body, name=name, grid=(M // tm, N // tn, nk),
        in_specs=in_specs, out_specs=out_spec, out_shape=out_shape,
        scratch_shapes=[pltpu.VMEM((tm, tn), F32)], input_output_aliases=aliases,
        compiler_params=pltpu.CompilerParams(
            dimension_semantics=("parallel", "parallel", "arbitrary"),
            vmem_limit_bytes=_vmem(_nbytes((tm, tk), a.dtype) + _nbytes((tk, tn), b.dtype) + _nbytes((tm, tn), out_dtype),
                                   _nbytes((tm, tn), F32) + _nbytes((tm, tk), BF16) + _nbytes((tk, tn), BF16))),
    )(*args)


def _rows(fn, name, tm, rows, halos=(), fulls=(), outs=(), accs=()):
    n = S // tm
    in_specs, args = [], []
    for arr, w, cb in rows:
        in_specs.append(pl.BlockSpec((tm, w), functools.partial(lambda i, cb: (i, cb), cb=cb)))
        args.append(arr)
    for arr, w, cb, side in halos:
        if side == "prev":
            im = functools.partial(lambda i, cb: (jnp.maximum(i * (tm // 16) - 1, 0), cb), cb=cb)
        else:
            im = functools.partial(lambda i, cb: (jnp.minimum((i + 1) * (tm // 16), S // 16 - 1), cb), cb=cb)
        in_specs.append(pl.BlockSpec((16, w), im))
        args.append(arr)
    for arr in fulls:
        in_specs.append(pl.BlockSpec(arr.shape, functools.partial(lambda i, nd: (0,) * nd, nd=arr.ndim)))
        args.append(arr)
    out_shape, out_specs, aliases, n_alias = [], [], {}, 0
    for k, o in enumerate(outs):
        if len(o) == 3 and o[2] == "T":
            out_shape.append(jax.ShapeDtypeStruct((o[0], S), o[1]))
            out_specs.append(pl.BlockSpec((o[0], tm), lambda i: (0, i)))
        elif len(o) == 3:
            buf, total, cb = o[2]
            out_shape.append(jax.ShapeDtypeStruct((S, total), o[1]))
            out_specs.append(pl.BlockSpec((tm, o[0]), functools.partial(lambda i, cb: (i, cb), cb=cb)))
            if buf is not None:
                aliases[len(args)] = k
                in_specs.append(pl.BlockSpec(memory_space=pl.ANY))
                args.append(buf)
                n_alias += 1
        else:
            out_shape.append(jax.ShapeDtypeStruct((S, o[0]), o[1]))
            out_specs.append(pl.BlockSpec((tm, o[0]), lambda i: (i, 0)))
    for shp in accs:
        out_shape.append(jax.ShapeDtypeStruct(shp, F32))
        out_specs.append(pl.BlockSpec(shp, functools.partial(lambda i, nd: (0,) * nd, nd=len(shp))))
    nr, nh, nf, no, na = len(rows), len(halos), len(fulls), len(outs), len(accs)
    blocks = (sum(_nbytes((tm, w), arr.dtype) for arr, w, _ in rows) + sum(_nbytes(a.shape, a.dtype) for a in fulls)
              + sum(_nbytes((tm, o[0]), o[1]) for o in outs) + sum(_nbytes(shp, F32) for shp in accs))
    widest = _nbytes((tm, max([w for _, w, _ in rows] + [o[0] for o in outs])), F32)

    def body(*refs):
        i = pl.program_id(0)
        ins, orefs = refs[:nr + nh + nf], refs[nr + nh + nf + n_alias:]
        rv = [r[...].astype(F32) for r in ins[:nr]]
        hv = [r[...].astype(F32)[8:] if h[3] == "prev" else r[...].astype(F32)[:8] for r, h in zip(ins[nr:nr + nh], halos)]
        fv = [r[...] for r in ins[nr + nh:]]
        o, a = fn(i, rv, hv, fv)
        assert len(o) == no and len(a) == na, name
        for spec, ref, val in zip(outs, orefs[:no], o):
            ref[...] = (val.T if len(spec) == 3 and spec[2] == "T" else val).astype(ref.dtype)
        if na:
            @pl.when(i == 0)
            def _():
                for ref in orefs[no:]:
                    ref[...] = jnp.zeros_like(ref)

            for ref, val in zip(orefs[no:], a):
                ref[...] += val

    res = pl.pallas_call(
        body, name=name, grid=(n,), in_specs=in_specs, out_specs=out_specs, out_shape=out_shape,
        input_output_aliases=aliases,
        compiler_params=pltpu.CompilerParams(dimension_semantics=("arbitrary",), vmem_limit_bytes=_vmem(blocks, 6 * widest)),
    )(*args)
    return res


def _shift_down(xb, halo, s, row):
    fix = jnp.tile(pltpu.roll(halo, s, 0), (xb.shape[0] // 8, 1))
    return jnp.where(row >= s, pltpu.roll(xb, s, 0), fix)


def _shift_up(xb, halo, s, row):
    tm = xb.shape[0]
    fix = jnp.tile(pltpu.roll(halo, 8 - s, 0), (tm // 8, 1))
    return jnp.where(row < tm - s, pltpu.roll(xb, tm - s, 0), fix)


def _rms(x):
    return lax.rsqrt(jnp.mean(x * x, axis=-1, keepdims=True) + EPS)


def _rms_bwd(dy, x, g):
    r = _rms(x)
    xh = x * r
    dxh = dy * g
    dx = r * (dxh - xh * jnp.mean(dxh * xh, axis=-1, keepdims=True))
    return dx, dy * xh


def _colsum(x):
    return jnp.sum(x, axis=0, keepdims=True)


def _prenorm_fwd(x, g, token=None):
    def fn(i, rv, hv, fv):
        return [rv[0] * _rms(rv[0]) * fv[0]], []
    return _rows(fn, "prenorm_fwd", 256, [(x, D, 0)], fulls=[g] + ([] if token is None else [token]), outs=[(D, BF16)])[0]


def _gm_mask():
    r = lax.broadcasted_iota(jnp.int32, (GM_B, GM_B), 0) // CHUNK
    c = lax.broadcasted_iota(jnp.int32, (GM_B, GM_B), 1) // CHUNK
    return c <= r


def _gm_norm(v, g, b):
    mu = jnp.mean(v, axis=-1, keepdims=True)
    vc = v - mu
    rs = lax.rsqrt(jnp.mean(vc * vc, axis=-1, keepdims=True) + EPS)
    vh = vc * rs
    return vh, rs, vh * g + b


def _gm_sv(vn, ws, bst):
    mask = _gm_mask()
    gw = GM_W // GM_G
    parts = []
    for g in range(GM_G):
        wm = jnp.where(mask, ws[g], 0.0).astype(BF16)
        parts.append(jnp.dot(wm, vn[:, g * gw:(g + 1) * gw].astype(BF16), preferred_element_type=F32)
                     + bst[:, g:g + 1])
    return jnp.concatenate(parts, axis=1)


def _gmlp_fwd(proj, ln_g, ln_b, ws, bst):
    def fn(i, rv, hv, fv):
        u, v, z = rv
        g, b, w, bt = fv
        _, _, vn = _gm_norm(v, g, b)
        return [u * _gm_sv(vn, w, bt) * _silu(z)], []
    return _rows(fn, "gmlp_fwd", GM_B, [(proj, GM_W, 0), (proj, GM_W, 1), (proj, GM_W, 2)],
                 fulls=[ln_g, ln_b, ws, bst], outs=[(GM_W, BF16)])[0]


def _mla_prep_fwd(proj, qg, kvg):
    def fn(i, rv, hv, fv):
        cq, ckv = rv
        g1, g2 = fv
        return [cq * _rms(cq) * g1, ckv * _rms(ckv) * g2], []
    return _rows(fn, "mla_prep_fwd", 256, [(proj, QR, O_CQ // QR), (proj, KVR, O_CKV // KVR)],
                 fulls=[qg, kvg], outs=[(QR, BF16), (KVR, BF16)])


def _rot(t, cc, sa, sb):
    return t * cc + pltpu.roll(t, 32, 1) * sa + pltpu.roll(t, 96, 1) * sb


def _rot_t(g, cc, sa, sb):
    return g * cc + pltpu.roll(g * sa, 96, 1) + pltpu.roll(g * sb, 32, 1)


def _rope_tables():
    pos = jnp.arange(S, dtype=F32)
    inv_freq = ROPE_THETA ** (-jnp.arange(0, ROPE, 2, dtype=F32) / ROPE)
    ang = pos[:, None] * inv_freq[None, :]
    cos, sin, z = jnp.cos(ang), jnp.sin(ang), jnp.zeros((S, 32), F32)
    cc = jnp.concatenate([cos, cos, z, z], axis=1)
    sa = jnp.concatenate([z, sin, z, z], axis=1)
    sb = jnp.concatenate([-sin, z, z, z], axis=1)
    return cc, sa, sb


ATT_SCALE = 1.0 / math.sqrt(NOPE + ROPE)


def _rope_fwd(q, kv, proj, tabs):
    def fn(i, rv, hv, fv):
        qb, kvb, kr, cc, sa, sb = rv
        krr = _rot(kr, cc, sa, sb)
        qs, ks = [], []
        for h in range(H):
            qs += [qb[:, h * HP:h * HP + 128] * ATT_SCALE, _rot(qb[:, h * HP + 128:(h + 1) * HP], cc, sa, sb) * ATT_SCALE]
            ks += [kvb[:, h * 128:(h + 1) * 128], krr]
        kc = jnp.concatenate(ks, axis=1)
        vv = kvb[:, H * NOPE:]
        return [jnp.concatenate(qs, axis=1), kc, kc, vv, vv], []
    cc, sa, sb = tabs
    return _rows(fn, "rope_fwd", 256,
                 [(q, H * HP, 0), (kv, H * 256, 0), (proj, 128, O_KR // 128), (cc, 128, 0), (sa, 128, 0), (sb, 128, 0)],
                 outs=[(H * HP, BF16), (H * HP, BF16), (H * HP, BF16, "T"), (MLA_W, BF16), (MLA_W, BF16, "T")])


TQ, TC, ATT_NB = 512, 128, 4
ATT_KB = TC * ATT_NB
_NT = (((1,), (1,)), ((), ()))


def _attn_allowed(i, kc):
    kpos = kc * TC + lax.broadcasted_iota(jnp.int32, (TC, TQ), 0)
    qpos = i * TQ + lax.broadcasted_iota(jnp.int32, (TC, TQ), 1)
    return (kpos // CHUNK) <= (qpos // CHUNK)


def _attn_fwd(qc, kc, vt):
    def body(q_ref, k_ref, vt_ref, o_ref, l_ref):
        i = pl.program_id(1)
        q = q_ref[...]

        def scores(sb):
            t0s = [pl.multiple_of((sb * ATT_NB + c) * TC, TC) for c in range(ATT_NB)]
            return [lax.dot_general(k_ref[pl.ds(t0, TC), :], q, _NT, preferred_element_type=F32) for t0 in t0s]

        def block(sb, ss, carry, masked):
            m, l, acc = carry
            t0s = [pl.multiple_of((sb * ATT_NB + c) * TC, TC) for c in range(ATT_NB)]
            if masked:
                ss = [jnp.where(_attn_allowed(i, sb * ATT_NB + c), s, -1e30) for c, s in enumerate(ss)]
            m_new = m
            for s in ss:
                m_new = jnp.maximum(m_new, jnp.max(s, axis=0, keepdims=True))
            alpha = jnp.exp(m - m_new)
            ps = [jnp.exp(s - m_new) for s in ss]
            l = alpha * l
            acc = alpha * acc
            for t0, p in zip(t0s, ps):
                l = l + jnp.sum(p, axis=0, keepdims=True)
                acc = acc + jnp.dot(vt_ref[:, pl.ds(t0, TC)], p.astype(BF16), preferred_element_type=F32)
            return m_new, l, acc

        nsb = ((i + 1) * TQ + ATT_KB - 1) // ATT_KB
        c = (jnp.full((1, TQ), -1e30, F32), jnp.zeros((1, TQ), F32), jnp.zeros((VDIM, TQ), F32))

        def step(sb, sc):
            nxt = scores(sb + 1)
            return nxt, block(sb, sc[0], sc[1], False)

        ss, c = lax.fori_loop(0, nsb - 1, step, (scores(0), c))
        m, l, acc = block(nsb - 1, ss, c, True)
        o_ref[...] = (acc / l).T
        l_ref[...] = m + jnp.log(l)

    return pl.pallas_call(
        body, name="attn_fwd", grid=(H, S // TQ),
        in_specs=[pl.BlockSpec((TQ, HP), lambda h, i: (i, h)),
                  pl.BlockSpec((S, HP), lambda h, i: (0, h)),
                  pl.BlockSpec((VDIM, S), lambda h, i: (h, 0))],
        out_specs=[pl.BlockSpec((TQ, VDIM), lambda h, i: (i, h)), pl.BlockSpec((None, 1, TQ), lambda h, i: (h, 0, i))],
        out_shape=[jax.ShapeDtypeStruct((S, MLA_W), F32), jax.ShapeDtypeStruct((H, 1, S), F32)],
        compiler_params=pltpu.CompilerParams(dimension_semantics=("parallel", "arbitrary"),
                                             vmem_limit_bytes=24 * MIB),
    )(qc, kc, vt)


def _gate_mul_fwd(name, val, proj, width, cb):
    def fn(i, rv, hv, fv):
        o, z = rv
        return [o * _silu(z)], []
    return _rows(fn, name, 256, [(val, width, 0), (proj, width, cb)], outs=[(width, BF16)])[0]


def _conv_fwd(proj, w, b):
    def fn(i, rv, hv, fv):
        (xb,), (halo,), (ww, bb) = rv, hv, fv
        halo = jnp.where(i > 0, halo, 0.0)
        row = lax.broadcasted_iota(jnp.int32, xb.shape, 0)
        acc = bb + ww[3:4] * xb
        for s in range(1, CONV_W):
            acc = acc + ww[3 - s:4 - s] * _shift_down(xb, halo, s, row)
        return [acc, acc], []
    return _rows(fn, "conv_fwd", 128, [(proj, LRU_W, O_XC // LRU_W)], halos=[(proj, LRU_W, O_XC // LRU_W, "prev")],
                 fulls=[w, b], outs=[(LRU_W, F32), (LRU_W, BF16)])


def _lru_terms(ga, gx, xc, ba, bx, lam):
    r = _sig(ga + ba)
    ig = _sig(gx + bx)
    sp = jnp.maximum(-lam, 0.0) + jnp.log(1.0 + jnp.exp(-jnp.abs(lam)))
    log_a = -LRU_C * r * sp
    a = jnp.exp(log_a)
    e2 = jnp.exp(2.0 * log_a)
    om = 1.0 - e2
    mult = jnp.sqrt(jnp.maximum(om, 0.0))
    return r, ig, sp, a, e2, om, mult


def _lru_gates_fwd(gates, xc, ba, bx, lam):
    def fn(i, rv, hv, fv):
        ga, gx, x = rv
        r, ig, sp, a, e2, om, mult = _lru_terms(ga, gx, x, *fv)
        return [a, mult * (ig * x)], []
    return _rows(fn, "lru_gates_fwd", 128, [(gates, LRU_W, 0), (gates, LRU_W, 1), (xc, LRU_W, 0)],
                 fulls=[ba, bx, lam], outs=[(LRU_W, F32), (LRU_W, F32)])


SCAN_T, SCAN_CW = 64, 256


def _scan_fwd(a, b):
    def body(a_ref, b_ref, h_ref):
        row = lax.broadcasted_iota(jnp.int32, (SCAN_T, SCAN_CW), 0)

        def step(blk, hc):
            t0 = pl.multiple_of(blk * SCAN_T, SCAN_T)
            A = a_ref[pl.ds(t0, SCAN_T), :]
            B = b_ref[pl.ds(t0, SCAN_T), :]
            d = 1
            while d < SCAN_T:
                keep = row >= d
                A_s = jnp.where(keep, pltpu.roll(A, d, 0), 1.0)
                B_s = jnp.where(keep, pltpu.roll(B, d, 0), 0.0)
                B = A * B_s + B
                A = A * A_s
                d *= 2
            hh = A * hc + B
            h_ref[pl.ds(t0, SCAN_T), :] = hh
            return hh[SCAN_T - 1:SCAN_T, :]

        lax.fori_loop(0, S // SCAN_T, step, jnp.zeros((1, SCAN_CW), F32))

    spec = pl.BlockSpec((S, SCAN_CW), lambda j: (0, j))
    return pl.pallas_call(
        body, name="scan_fwd", grid=(LRU_W // SCAN_CW,), in_specs=[spec, spec], out_specs=spec,
        out_shape=jax.ShapeDtypeStruct((S, LRU_W), F32),
        compiler_params=pltpu.CompilerParams(dimension_semantics=("parallel",),
                                             vmem_limit_bytes=_vmem(3 * _nbytes((S, SCAN_CW), F32))),
    )(a, b)


def _merge_fwd(pa, pb, pc, proj):
    def fn(i, rv, hv, fv):
        a, b, c, ga, gb, gc = rv
        return [_sig(ga) * a + _sig(gb) * b + _sig(gc) * c], []
    return _rows(fn, "merge_fwd", 256,
                 [(pa, D, 0), (pb, D, 0), (pc, D, 0), (proj, D, O_GA // D), (proj, D, O_GB // D), (proj, D, O_GC // D)],
                 outs=[(D, BF16)])[0]


def _post_fwd(x, o2, g):
    def fn(i, rv, hv, fv):
        xb, ob = rv
        return [xb + ob * _rms(ob) * fv[0]], []
    return _rows(fn, "post_fwd", 256, [(x, D, 0), (o2, D, 0)], fulls=[g], outs=[(D, F32)])[0]


SB = 640
BD_TM = 512


def _bd_fwd(xcb, wsb, l):
    def body(x_ref, w_ref, o_ref):
        o_ref[...] = jnp.dot(x_ref[...], w_ref[...], preferred_element_type=F32).astype(o_ref.dtype)

    return pl.pallas_call(
        body, name="lru_gate_mm", grid=(S // BD_TM, 4),
        in_specs=[pl.BlockSpec((BD_TM, SB), lambda i, q: (i, q % 2)),
                  pl.BlockSpec((None, None, SB, SB), lambda i, q: (l, q, 0, 0))],
        out_specs=pl.BlockSpec((BD_TM, SB), lambda i, q: (i, q)),
        out_shape=jax.ShapeDtypeStruct((S, 2 * LRU_W), BF16),
        compiler_params=pltpu.CompilerParams(dimension_semantics=("parallel", "parallel"), vmem_limit_bytes=VMEM_LIMIT),
    )(xcb, wsb)


def _bd_dx(dgates, wsb, l):
    def body(d_ref, w_ref, o_ref, acc_ref):
        g = pl.program_id(2)

        @pl.when(g == 0)
        def _():
            acc_ref[...] = jnp.zeros_like(acc_ref)

        acc_ref[...] += lax.dot_general(d_ref[...], w_ref[...], (((1,), (1,)), ((), ())), preferred_element_type=F32)

        @pl.when(g == 1)
        def _():
            o_ref[...] = acc_ref[...].astype(o_ref.dtype)

    return pl.pallas_call(
        body, name="lru_gate_dx", grid=(S // BD_TM, 2, 2),
        in_specs=[pl.BlockSpec((BD_TM, SB), lambda i, s, g: (i, 2 * g + s)),
                  pl.BlockSpec((None, None, SB, SB), lambda i, s, g: (l, 2 * g + s, 0, 0))],
        out_specs=pl.BlockSpec((BD_TM, SB), lambda i, s, g: (i, s)),
        out_shape=jax.ShapeDtypeStruct((S, LRU_W), BF16),
        scratch_shapes=[pltpu.VMEM((BD_TM, SB), F32)],
        compiler_params=pltpu.CompilerParams(dimension_semantics=("parallel", "parallel", "arbitrary"),
                                             vmem_limit_bytes=VMEM_LIMIT),
    )(dgates, wsb)


def _bd_dw(xcb, dgates):
    tk = 1024

    def body(x_ref, d_ref, o_ref):
        @pl.when(pl.program_id(1) == 0)
        def _():
            o_ref[...] = jnp.zeros_like(o_ref)

        o_ref[...] += lax.dot_general(x_ref[...], d_ref[...], (((0,), (0,)), ((), ())), preferred_element_type=F32)

    return pl.pallas_call(
        body, name="lru_gate_dw", grid=(4, S // tk),
        in_specs=[pl.BlockSpec((tk, SB), lambda q, k: (k, q % 2)), pl.BlockSpec((tk, SB), lambda q, k: (k, q))],
        out_specs=pl.BlockSpec((None, SB, SB), lambda q, k: (q, 0, 0)),
        out_shape=jax.ShapeDtypeStruct((4, SB, SB), F32),
        compiler_params=pltpu.CompilerParams(dimension_semantics=("parallel", "arbitrary"), vmem_limit_bytes=VMEM_LIMIT),
    )(xcb, dgates)


def _bd_extract(dwsb):
    def body(w_ref, o_ref):
        lane = lax.broadcasted_iota(jnp.int32, (LRU_BW, 128), 1)
        for q in range(4):
            for kk in range(8):
                c0 = LRU_BW * kk
                w0, off = (c0 // 128) * 128, c0 % 128
                rows = pl.ds(LRU_BW * kk, LRU_BW)
                blk = w_ref[q, rows, w0:w0 + 128]
                if off:
                    blk = pltpu.roll(blk, 128 - off, 1)
                    if off + LRU_BW > 128:
                        nxt = pltpu.roll(w_ref[q, rows, w0 + 128:w0 + 256], 128 - off, 1)
                        blk = jnp.where(lane < 128 - off, blk, nxt)
                o_ref[q // 2, 8 * (q % 2) + kk] = blk.astype(BF16)

    return pl.pallas_call(
        body, name="lru_gate_dw_blocks",
        in_specs=[pl.BlockSpec(memory_space=pltpu.VMEM)], out_specs=pl.BlockSpec(memory_space=pltpu.VMEM),
        out_shape=jax.ShapeDtypeStruct((2, LRU_NB, LRU_BW, 128), BF16),
        compiler_params=pltpu.CompilerParams(vmem_limit_bytes=VMEM_LIMIT),
    )(dwsb)


def _layer_fwd(x, P, l, tabs, token=None, late=None):
    A = {"x": x}
    A["h"] = _prenorm_fwd(x, P["pre_g"], token)
    proj = A["proj"] = _mm(A["h"], P["wp"], "nt", "in_proj", out_dtype=BF16, tm=1024)
    A["ya"] = _gmlp_fwd(proj, P["ln_g"], P["ln_b"], P["ws"], P["bst"])
    A["xc"], A["xcb"] = _conv_fwd(proj, P["conv_w"], P["conv_b"])
    A["gates"] = _bd_fwd(A["xcb"], P["wsb"], l)
    A["a"], bterm = _lru_gates_fwd(A["gates"], A["xc"], P["ba"], P["bx"], P["lam"])
    A["hs"] = _scan_fwd(A["a"], bterm)
    A["yc"] = _gate_mul_fwd("yc_fwd", A["hs"], proj, LRU_W, O_ZC // LRU_W)
    if late is not None:
        P.update(late(A["yc"]))
    A["cqn"], A["ckvn"] = _mla_prep_fwd(proj, P["qg"], P["kvg"])
    q = _mm(A["cqn"], P["wuq"], "nt", "q_up", out_dtype=BF16)
    kv = _mm(A["ckvn"], P["wukv"], "nt", "kv_up", out_dtype=BF16)
    A["qc"], A["kc"], A["kct"], A["vv"], vt = _rope_fwd(q, kv, proj, tabs)
    A["o"], A["lse"] = _attn_fwd(A["qc"], A["kc"], vt)
    A["yb"] = _gate_mul_fwd("yb_fwd", A["o"], proj, MLA_W, O_ZB // MLA_W)
    A["pa"] = _mm(A["ya"], P["wpa"], "nn", "proj_a", out_dtype=BF16)
    A["pb"] = _mm(A["yb"], P["wpb"], "nn", "proj_b", out_dtype=BF16)
    A["pc"] = _mm(A["yc"], P["wpc"], "nn", "proj_c", out_dtype=BF16)
    A["merged"] = _merge_fwd(A["pa"], A["pb"], A["pc"], proj)
    A["o2"] = _mm(A["merged"], P["wout"], "nn", "out_proj")
    return _post_fwd(x, A["o2"], P["post_g"]), A


def _loss_fwd(y, tgt):
    def fn(i, rv, hv, fv):
        yb, tb = rv
        e = yb - tb
        part = 0.5 * jnp.sum(jnp.mean(e * e, axis=-1, keepdims=True), axis=0, keepdims=True)
        return [e * (1.0 / D)], [part]
    return _rows(fn, "loss", 256, [(y, D, 0), (tgt, D, 0)], outs=[(D, F32)], accs=[(1, 1)])


def _post_bwd(dxn, o2, g, token=None):
    def fn(i, rv, hv, fv):
        dy, ob = rv
        dx, dg = _rms_bwd(dy, ob, fv[0])
        return [dx], [_colsum(dg)]
    return _rows(fn, "post_bwd", 256, [(dxn, D, 0), (o2, D, 0)], fulls=[g] + ([] if token is None else [token]),
                 outs=[(D, BF16)], accs=[(1, D)])


def _merge_bwd(dm, pa, pb, pc, proj, dproj):
    def fn(i, rv, hv, fv):
        d, a, b, c, ga, gb, gc = rv
        outs_p, outs_g = [], []
        for p, gg in ((a, ga), (b, gb), (c, gc)):
            s = _sig(gg)
            outs_p.append(d * s)
            outs_g.append(d * p * s * (1.0 - s))
        return outs_p + [jnp.concatenate(outs_g, axis=1)], []
    return _rows(fn, "merge_bwd", 128,
                 [(dm, D, 0), (pa, D, 0), (pb, D, 0), (pc, D, 0),
                  (proj, D, O_GA // D), (proj, D, O_GB // D), (proj, D, O_GC // D)],
                 outs=[(D, BF16)] * 3 + [(3 * D, BF16, (dproj, NP, O_GA // (3 * D)))])


def _gmlp_bwd(dya, proj, ln_g, ln_b, ws, bst, dproj):
    gw = GM_W // GM_G

    def fn(i, rv, hv, fv):
        dy, u, v, z = rv
        g, b, w, bt = fv
        vh, rs, vn = _gm_norm(v, g, b)
        sv = _gm_sv(vn, w, bt)
        sz = _silu(z)
        du = dy * sv * sz
        dsv = dy * u * sz
        dz = dy * u * sv * _dsilu(z)
        mask = _gm_mask()
        lane = lax.broadcasted_iota(jnp.int32, (GM_B, 128), 1)
        dvn_parts, dws, dbst = [], [], jnp.zeros((GM_B, 128), F32)
        for k in range(GM_G):
            wm = jnp.where(mask, w[k], 0.0).astype(BF16)
            dsk = dsv[:, k * gw:(k + 1) * gw]
            dskb = dsk.astype(BF16)
            dvn_parts.append(lax.dot_general(wm, dskb, (((0,), (0,)), ((), ())), preferred_element_type=F32))
            dwk = lax.dot_general(dskb, vn[:, k * gw:(k + 1) * gw].astype(BF16), (((1,), (1,)), ((), ())),
                                  preferred_element_type=F32)
            dws.append(jnp.where(mask, dwk, 0.0)[None])
            dbst = dbst + jnp.where(lane == k, jnp.sum(dsk, axis=1, keepdims=True), 0.0)
        dvn = jnp.concatenate(dvn_parts, axis=1)
        dvh = dvn * g
        dv = rs * (dvh - jnp.mean(dvh, axis=-1, keepdims=True) - vh * jnp.mean(dvh * vh, axis=-1, keepdims=True))
        return ([jnp.concatenate([du, dv, dz], axis=1)],
                [jnp.concatenate(dws, axis=0), dbst, _colsum(dvn * vh), _colsum(dvn)])
    return _rows(fn, "gmlp_bwd", GM_B, [(dya, GM_W, 0), (proj, GM_W, 0), (proj, GM_W, 1), (proj, GM_W, 2)],
                 fulls=[ln_g, ln_b, ws, bst], outs=[(3 * GM_W, BF16, (dproj, NP, O_U // (3 * GM_W)))],
                 accs=[(GM_G, GM_B, GM_B), (GM_B, 128), (1, GM_W), (1, GM_W)])


def _yb_bwd(dyb, o, proj, dproj):
    def fn(i, rv, hv, fv):
        dy, ob, z = rv
        do = dy * _silu(z)
        prod = do * ob
        lane = lax.broadcasted_iota(jnp.int32, (dy.shape[0], 128), 1)
        dl = jnp.zeros((dy.shape[0], 128), F32)
        for h in range(H):
            dl = dl + jnp.where(lane == h, jnp.sum(prod[:, h * VDIM:(h + 1) * VDIM], axis=1, keepdims=True), 0.0)
        return [do, dl, dy * ob * _dsilu(z)], []
    return _rows(fn, "yb_bwd", 256, [(dyb, MLA_W, 0), (o, MLA_W, 0), (proj, MLA_W, O_ZB // MLA_W)],
                 outs=[(MLA_W, BF16), (128, F32, "T"), (MLA_W, BF16, (dproj, NP, O_ZB // MLA_W))])


def _attn_bwd(qc, kc, kct, vv, do, lse, dlt):
    def body(q_ref, k_ref, kt_ref, v_ref, do_ref, l_ref, d_ref, dq_ref, dk_ref, dv_ref, dqt_ref):
        h, i = pl.program_id(0), pl.program_id(1)

        @pl.when(i == 0)
        def _():
            dk_ref[...] = jnp.zeros_like(dk_ref)
            dv_ref[...] = jnp.zeros_like(dv_ref)

        q = q_ref[...]
        dob = do_ref[...]
        lse = l_ref[...]
        dl = d_ref[pl.ds(h, 1), :]
        dqt_ref[...] = jnp.zeros_like(dqt_ref)

        def rows_of(sb, c):
            return pl.ds(pl.multiple_of((sb * ATT_NB + c) * TC, TC), TC)

        def front(sb):
            return [(lax.dot_general(k_ref[rows_of(sb, c), :], q, _NT, preferred_element_type=F32),
                     lax.dot_general(v_ref[rows_of(sb, c), :], dob, _NT, preferred_element_type=F32))
                    for c in range(ATT_NB)]

        def block(sb, sd, masked):
            dqt = None
            for c, (s, dp) in enumerate(sd):
                rows = rows_of(sb, c)
                p = jnp.exp(s - lse)
                if masked:
                    p = jnp.where(_attn_allowed(i, sb * ATT_NB + c), p, 0.0)
                ds = (p * (dp - dl)).astype(BF16)
                dk_ref[rows, :] += jnp.dot(ds, q, preferred_element_type=F32)
                dv_ref[rows, :] += jnp.dot(p.astype(BF16), dob, preferred_element_type=F32)
                part = jnp.dot(kt_ref[:, rows], ds, preferred_element_type=F32)
                dqt = part if dqt is None else dqt + part
            dqt_ref[...] += dqt

        def step(sb, sd):
            nxt = front(sb + 1)
            block(sb, sd, False)
            return nxt

        nsb = ((i + 1) * TQ + ATT_KB - 1) // ATT_KB
        sd = lax.fori_loop(0, nsb - 1, step, front(0))
        block(nsb - 1, sd, True)
        dq_ref[...] = dqt_ref[...].T.astype(dq_ref.dtype)

    blk = lambda w: pl.BlockSpec((TQ, w), lambda h, i: (i, h))
    head = lambda w: pl.BlockSpec((S, w), lambda h, i: (0, h))
    return pl.pallas_call(
        body, name="attn_bwd", grid=(H, S // TQ),
        in_specs=[blk(HP), head(HP), pl.BlockSpec((HP, S), lambda h, i: (h, 0)), head(VDIM), blk(VDIM),
                  pl.BlockSpec((None, 1, TQ), lambda h, i: (h, 0, i)), pl.BlockSpec((8, TQ), lambda h, i: (0, i))],
        out_specs=[blk(HP), head(HP), head(VDIM)],
        out_shape=[jax.ShapeDtypeStruct((S, H * HP), BF16), jax.ShapeDtypeStruct((S, H * HP), F32),
                   jax.ShapeDtypeStruct((S, MLA_W), F32)],
        scratch_shapes=[pltpu.VMEM((HP, TQ), F32)],
        compiler_params=pltpu.CompilerParams(dimension_semantics=("parallel", "arbitrary"),
                                             vmem_limit_bytes=28 * MIB),
    )(qc, kc, kct, vv, do, lse, dlt)


def _rope_bwd(dqc, dkc, dvv, tabs):
    def fn(i, rv, hv, fv):
        dq, dk, dv, cc, sa, sb = rv
        qs, ks = [], []
        dkr = jnp.zeros((dq.shape[0], 128), F32)
        for h in range(H):
            qs += [dq[:, h * HP:h * HP + 128] * ATT_SCALE, _rot_t(dq[:, h * HP + 128:(h + 1) * HP], cc, sa, sb) * ATT_SCALE]
            ks.append(dk[:, h * HP:h * HP + 128])
            dkr = dkr + dk[:, h * HP + 128:(h + 1) * HP]
        return [jnp.concatenate(qs, axis=1), jnp.concatenate(ks + [dv], axis=1), _rot_t(dkr, cc, sa, sb)], []
    cc, sa, sb = tabs
    return _rows(fn, "rope_bwd", 256,
                 [(dqc, H * HP, 0), (dkc, H * HP, 0), (dvv, MLA_W, 0), (cc, 128, 0), (sa, 128, 0), (sb, 128, 0)],
                 outs=[(H * HP, BF16), (H * 256, BF16), (128, BF16)])


MLA_GROUP = 1536


def _mla_prep_bwd(dcqn, dckvn, dkr, proj, qg, kvg, dproj):
    def fn(i, rv, hv, fv):
        d1, d2, dk, cq, ckv = rv
        g1, g2 = fv
        dx1, dg1 = _rms_bwd(d1, cq, g1)
        dx2, dg2 = _rms_bwd(d2, ckv, g2)
        zeros = jnp.zeros((d1.shape[0], MLA_GROUP - KVR - 128 - QR), F32)
        return [jnp.concatenate([dx2, dk.astype(F32), dx1, zeros], axis=1)], [_colsum(dg1), _colsum(dg2)]
    return _rows(fn, "mla_prep_bwd", 256,
                 [(dcqn, QR, 0), (dckvn, KVR, 0), (dkr, 128, 0), (proj, QR, O_CQ // QR), (proj, KVR, O_CKV // KVR)],
                 fulls=[qg, kvg], outs=[(MLA_GROUP, BF16, (dproj, NP, O_CKV // MLA_GROUP))], accs=[(1, QR), (1, KVR)])


def _yc_bwd(dyc, hs, proj, dproj):
    def fn(i, rv, hv, fv):
        dy, hh, z = rv
        return [dy * _silu(z), dy * hh * _dsilu(z)], []
    return _rows(fn, "yc_bwd", 128, [(dyc, LRU_W, 0), (hs, LRU_W, 0), (proj, LRU_W, O_ZC // LRU_W)],
                 outs=[(LRU_W, F32), (LRU_W, BF16, (dproj, NP, O_ZC // LRU_W))])


def _scan_bwd(a, hs, dh):
    nblk = S // SCAN_T

    def body(a_ref, h_ref, dh_ref, da_ref, db_ref):
        row = lax.broadcasted_iota(jnp.int32, (SCAN_T, SCAN_CW), 0)

        def step(j, carry):
            gc, ac = carry
            blk = nblk - 1 - j
            t0 = pl.multiple_of(blk * SCAN_T, SCAN_T)
            av = a_ref[pl.ds(t0, SCAN_T), :]
            A = jnp.where(row < SCAN_T - 1, pltpu.roll(av, SCAN_T - 1, 0), ac)
            B = dh_ref[pl.ds(t0, SCAN_T), :]
            d = 1
            while d < SCAN_T:
                keep = row < SCAN_T - d
                A_s = jnp.where(keep, pltpu.roll(A, SCAN_T - d, 0), 1.0)
                B_s = jnp.where(keep, pltpu.roll(B, SCAN_T - d, 0), 0.0)
                B = A * B_s + B
                A = A * A_s
                d *= 2
            g = A * gc + B
            p0 = pl.multiple_of(jnp.maximum(t0 - 8, 0), 8)
            last = jnp.where(blk > 0, h_ref[pl.ds(p0, 8), :][7:8, :], 0.0)
            h_prev = jnp.where(row >= 1, pltpu.roll(h_ref[pl.ds(t0, SCAN_T), :], 1, 0), last)
            da_ref[pl.ds(t0, SCAN_T), :] = g * h_prev
            db_ref[pl.ds(t0, SCAN_T), :] = g
            return g[0:1, :], av[0:1, :]

        z = jnp.zeros((1, SCAN_CW), F32)
        lax.fori_loop(0, nblk, step, (z, z))

    spec = pl.BlockSpec((S, SCAN_CW), lambda j: (0, j))
    return pl.pallas_call(
        body, name="scan_bwd", grid=(LRU_W // SCAN_CW,), in_specs=[spec] * 3, out_specs=[spec] * 2,
        out_shape=[jax.ShapeDtypeStruct((S, LRU_W), F32)] * 2,
        compiler_params=pltpu.CompilerParams(dimension_semantics=("parallel",),
                                             vmem_limit_bytes=_vmem(5 * _nbytes((S, SCAN_CW), F32))),
    )(a, hs, dh)


def _lru_gates_bwd(da, db, gates, xc, ba, bx, lam):
    def fn(i, rv, hv, fv):
        dav, dbv, ga, gx, x = rv
        bav, bxv, lamv = fv
        r, ig, sp, a, e2, om, mult = _lru_terms(ga, gx, x, bav, bxv, lamv)
        dmult = dbv * ig * x
        dig = dbv * mult * x
        dxc1 = dbv * mult * ig
        dlog_a = dav * a + jnp.where(om > 0.0, dmult * (-e2 / mult), 0.0)
        dr = dlog_a * (-LRU_C * sp)
        dga = dr * r * (1.0 - r)
        dgx = dig * ig * (1.0 - ig)
        dlam = _colsum(dlog_a * (-LRU_C * r)) * (-_sig(-lamv))
        return [jnp.concatenate([dga, dgx], axis=1), dxc1], [_colsum(dga), _colsum(dgx), dlam]
    return _rows(fn, "lru_gates_bwd", 128,
                 [(da, LRU_W, 0), (db, LRU_W, 0), (gates, LRU_W, 0), (gates, LRU_W, 1), (xc, LRU_W, 0)],
                 fulls=[ba, bx, lam], outs=[(2 * LRU_W, BF16), (LRU_W, F32)], accs=[(1, LRU_W)] * 3)


def _conv_bwd(dxc1, dxc2, proj, w, dproj):
    cb = O_XC // LRU_W

    def fn(i, rv, hv, fv):
        d1, d2, xb = rv
        n1, n2, xprev = hv
        ww = fv[0]
        last = i == S // 128 - 1
        dxc = d1 + d2
        nxt = jnp.where(last, 0.0, n1 + n2)
        xprev = jnp.where(i > 0, xprev, 0.0)
        row = lax.broadcasted_iota(jnp.int32, xb.shape, 0)
        dx = ww[3:4] * dxc
        dws = [None] * CONV_W
        dws[3] = _colsum(dxc * xb)
        for s in range(1, CONV_W):
            dx = dx + ww[3 - s:4 - s] * _shift_up(dxc, nxt, s, row)
            dws[3 - s] = _colsum(dxc * _shift_down(xb, xprev, s, row))
        return [dx], [jnp.concatenate(dws, axis=0), _colsum(dxc)]
    return _rows(fn, "conv_bwd", 128, [(dxc1, LRU_W, 0), (dxc2, LRU_W, 0), (proj, LRU_W, cb)],
                 halos=[(dxc1, LRU_W, 0, "next"), (dxc2, LRU_W, 0, "next"), (proj, LRU_W, cb, "prev")],
                 fulls=[w], outs=[(LRU_W, BF16, (dproj, NP, cb))], accs=[(CONV_W, LRU_W), (1, LRU_W)])


def _prenorm_bwd(dxn, dh, x, g):
    def fn(i, rv, hv, fv):
        dy, dhh, xb = rv
        dx, dg = _rms_bwd(dhh, xb, fv[0])
        return [dy + dx], [_colsum(dg)]
    return _rows(fn, "prenorm_bwd", 256, [(dxn, D, 0), (dh, D, 0), (x, D, 0)], fulls=[g], outs=[(D, F32)],
                 accs=[(1, D)])


def _layer_bwd(dxn, A, P, l, tabs, token=None, early=None):
    G, GB = {}, {}
    proj = A["proj"]

    def dw(key, a, b, name, **tiles):
        GB[key] = _mm(a, b, "tn", name, out_dtype=BF16, **tiles)

    do2, G["post_g"] = _post_bwd(dxn, A["o2"], P["post_g"], token)
    dm = _mm(do2, P["wout"], "nt", "out_proj_dx", out_dtype=BF16)
    dw("wout", A["merged"], do2, "out_proj_dw")
    dpa, dpb, dpc, dproj = _merge_bwd(dm, A["pa"], A["pb"], A["pc"], proj, None)
    dya = _mm(dpa, P["wpa"], "nt", "proj_a_dx", out_dtype=BF16)
    dw("wpa", A["ya"], dpa, "proj_a_dw")
    dyb = _mm(dpb, P["wpb"], "nt", "proj_b_dx", out_dtype=BF16)
    dw("wpb", A["yb"], dpb, "proj_b_dw")
    dyc = _mm(dpc, P["wpc"], "nt", "proj_c_dx", out_dtype=BF16)
    dw("wpc", A["yc"], dpc, "proj_c_dw")
    dproj, G["ws"], G["bst"], G["ln_g"], G["ln_b"] = _gmlp_bwd(dya, proj, P["ln_g"], P["ln_b"], P["ws"], P["bst"], dproj)
    do, dl, dproj = _yb_bwd(dyb, A["o"], proj, dproj)
    dqc, dkc, dvv = _attn_bwd(A["qc"], A["kc"], A["kct"], A["vv"], do, A["lse"], dl)
    dq, dkv, dkr = _rope_bwd(dqc, dkc, dvv, tabs)
    dcqn = _mm(dq, P["wuq"], "nn", "q_up_dx", out_dtype=BF16)
    dw("wuq", dq, A["cqn"], "q_up_dw")
    dckvn = _mm(dkv, P["wukv"], "nn", "kv_up_dx", out_dtype=BF16)
    dw("wukv", dkv, A["ckvn"], "kv_up_dw")
    dproj, G["qg"], G["kvg"] = _mla_prep_bwd(dcqn, dckvn, dkr, proj, P["qg"], P["kvg"], dproj)
    dhs, dproj = _yc_bwd(dyc, A["hs"], proj, dproj)
    da, db = _scan_bwd(A["a"], A["hs"], dhs)
    dgates, dxc1, G["ba"], G["bx"], G["lam"] = _lru_gates_bwd(da, db, A["gates"], A["xc"], P["ba"], P["bx"], P["lam"])
    dxc2 = _bd_dx(dgates, P["wsb"], l)
    G["wab"] = _bd_extract(_bd_dw(A["xcb"], dgates))
    dproj, G["conv_w"], G["conv_b"] = _conv_bwd(dxc1, dxc2, proj, P["conv_w"], dproj)
    tok = (None, None) if early is None else early(GB)
    dh = _mm(dproj, P["wp"], "nn", "in_proj_dx", tm=1024, tn=1024, token=tok[0])
    dw("wp", dproj, A["h"], "in_proj_dw", tm=1536, tn=1024, token=tok[1])
    dx, G["pre_g"] = _prenorm_bwd(dxn, dh, A["x"], P["pre_g"])
    return dx, G, GB


_ORIG_OFF = [0]
for _s in IN_SIZES:
    _ORIG_OFF.append(_ORIG_OFF[-1] + _s)
_PAD_OFF = {0: O_U, 1: O_V, 2: O_ZA, 3: O_CQ, 4: O_CKV, 5: O_KR, 6: O_ZB, 7: O_XC, 8: O_ZC, 9: O_GA, 10: O_GB, 11: O_GC}
SHARD_IN = N_IN // N_CHIPS


def _pieces_w_in(j):
    lo, hi = SHARD_IN * j, SHARD_IN * (j + 1)
    out = []
    for k in range(len(IN_SIZES)):
        a, b = max(lo, _ORIG_OFF[k]), min(hi, _ORIG_OFF[k + 1])
        if a < b:
            out.append((a - lo, _PAD_OFF[k] + a - _ORIG_OFF[k], b - a))
    return out


def _pieces_uq(j):
    return [(192 * hh, HP * (2 * j + hh), NOPE + ROPE) for hh in range(2)]


def _pieces_ukv(j):
    out = []
    for hh in range(2):
        h = 2 * j + hh
        out += [(256 * hh, NOPE * h, NOPE), (256 * hh + NOPE, H * NOPE + VDIM * h, VDIM)]
    return out


def _pieces_rows(r):
    return lambda j: [(0, r * j, r)]


LAYOUT = {
    "w_in": (SHARD_IN, NP, _pieces_w_in),
    "mla_w_uq": (2 * (NOPE + ROPE), H * HP, _pieces_uq),
    "mla_w_ukv": (2 * (NOPE + VDIM), 2 * H * 128, _pieces_ukv),
    "lru_conv_w": (1, N_CHIPS, _pieces_rows(1)),
    "w_proj_a": (GM_W // N_CHIPS, GM_W, _pieces_rows(GM_W // N_CHIPS)),
    "w_proj_b": (MLA_W // N_CHIPS, MLA_W, _pieces_rows(MLA_W // N_CHIPS)),
    "w_proj_c": (LRU_W // N_CHIPS, LRU_W, _pieces_rows(LRU_W // N_CHIPS)),
    "w_out": (D // N_CHIPS, D, _pieces_rows(D // N_CHIPS)),
}
TRANSPOSED = ("w_in", "mla_w_uq", "mla_w_ukv")


def _superblocks(w_a, w_x):
    w6 = jnp.stack([w_a, w_x], axis=1).reshape(DEPTH, 4, 8, LRU_BW, LRU_BW).astype(BF16)
    bands = [jnp.pad(w6[:, :, k], ((0, 0), (0, 0), (0, 0), (LRU_BW * k, SB - LRU_BW * (k + 1)))) for k in range(8)]
    return jnp.concatenate(bands, axis=2)


_HBM = pl.BlockSpec(memory_space=pltpu.HBM)


def _position():
    return lax.axis_index("x"), lax.axis_index("y"), lax.axis_index("c")


def _allgather(blocks, name):
    n = len(blocks)

    def body(*refs):
        ins, outs = refs[:n], refs[n:2 * n]
        send, recv, lsem = refs[2 * n:]
        x, y, c = _position()
        me, sib = (x, y, c), (x, y, 1 - c)
        chips = [(1 - x, y), (x, 1 - y), (1 - x, 1 - y)]

        def cp(k, a, block, to, src=None):
            dst = outs[a].at[4 * block[0] + 2 * block[1] + block[2]]
            return pltpu.make_async_remote_copy(src_ref=dst if src is None else src, dst_ref=dst,
                                                send_sem=send.at[7 * a + k], recv_sem=recv.at[7 * a + k],
                                                device_id=to, device_id_type=MESH)

        mine = [pltpu.make_async_copy(ins[a], outs[a].at[4 * x + 2 * y + c], lsem.at[a]) for a in range(n)]
        for m in mine:
            m.start()
        first = []
        for a in range(n):
            first.append(cp(0, a, me, sib, src=ins[a]))
            first += [cp(1 + j, a, me, (*chip, c), src=ins[a]) for j, chip in enumerate(chips)]
        for f in first:
            f.start()
        passed = []
        for j, chip in enumerate(chips):
            for a in range(n):
                cp(1 + j, a, (*chip, c), me).wait_recv()
                p = cp(4 + j, a, (*chip, c), sib)
                p.start()
                passed.append(p)
        for a in range(n):
            cp(0, a, sib, me).wait_recv()
            for j, chip in enumerate(chips):
                cp(4 + j, a, (*chip, 1 - c), me).wait_recv()
        for f in first + passed:
            f.wait_send()
        for m in mine:
            m.wait()

    return pl.pallas_call(
        body, name=name,
        out_shape=[jax.ShapeDtypeStruct((8,) + b.shape, b.dtype) for b in blocks],
        in_specs=[_HBM] * n, out_specs=[_HBM] * n,
        scratch_shapes=[pltpu.SemaphoreType.DMA((7 * n,)), pltpu.SemaphoreType.DMA((7 * n,)),
                        pltpu.SemaphoreType.DMA((n,))],
    )(*blocks)


_REL = (2, 1, 3)


def _cut(r):
    return r if r < 32 else (r // 2 + 15) // 16 * 16


def _half_rows(r, c0):
    return _cut(r) if c0 == 0 else r - _cut(r)


def _half_pieces(lay_a, jsrc, c0):
    r = lay_a[0]
    lo, hi = (0, _cut(r)) if c0 == 0 else (_cut(r), r)
    out = []
    for s0, d0, nr in lay_a[2](jsrc):
        a, b = max(s0, lo), min(s0 + nr, hi)
        if a < b:
            out.append((a, d0 + a - s0, b - a))
    return out


def _gather_zeros(names, srcs):
    return [jnp.zeros((LAYOUT[nm][1],) + s.shape[1:], s.dtype) for nm, s in zip(names, srcs)]


def _weights_allgather(names, srcs, name, carry=()):
    n = len(srcs)
    lay = [LAYOUT[nm] for nm in names]
    zeros = _gather_zeros(names, srcs)
    m = len(carry)

    def body(*refs):
        ins, outs = refs[:n], refs[2 * n + m:3 * n + m]
        send, recv, lsem = refs[3 * n + 2 * m:]
        x, y, c = _position()
        j = 2 * x + y
        sib = (x, y, 1 - c)
        chips = [(1 - x, y), (x, 1 - y), (1 - x, 1 - y)]

        def flow(a, k, jsrc, c0, to, from_src):
            cps = []
            for s0, d0, nr in _half_pieces(lay[a], jsrc, c0):
                dst = outs[a].at[pl.ds(d0, nr)]
                src = ins[a].at[pl.ds(s0, nr)] if from_src else dst
                cps.append(pltpu.make_async_remote_copy(src_ref=src, dst_ref=dst, send_sem=send.at[7 * a + k],
                                                        recv_sem=recv.at[7 * a + k], device_id=to, device_id_type=MESH))
            return cps

        def sized(a, k, rows):
            ref = ins[a].at[pl.ds(0, rows)]
            return pltpu.make_async_remote_copy(src_ref=ref, dst_ref=ref, send_sem=send.at[7 * a + k],
                                                recv_sem=recv.at[7 * a + k], device_id=sib, device_id_type=MESH)

        for j0 in range(N_CHIPS):
            for c0 in range(2):
                @pl.when((j == j0) & (c == c0))
                def _(j0=j0, c0=c0):
                    mine = [_half_rows(lay[a][0], c0) for a in range(n)]
                    theirs = [_half_rows(lay[a][0], 1 - c0) for a in range(n)]
                    for a in range(n):
                        for s0, d0, nr in _half_pieces(lay[a], j0, c0):
                            pltpu.make_async_copy(ins[a].at[pl.ds(s0, nr)], outs[a].at[pl.ds(d0, nr)], lsem.at[a]).start()
                    for a in range(n):
                        for cp in flow(a, 0, j0, c0, sib, True):
                            cp.start()
                        for k, chip in enumerate(chips):
                            for cp in flow(a, 1 + k, j0, c0, (*chip, c), True):
                                cp.start()
                    for k in range(3):
                        for a in range(n):
                            if mine[a]:
                                sized(a, 1 + k, mine[a]).wait_recv()
                                for cp in flow(a, 4 + k, j0 ^ _REL[k], c0, sib, False):
                                    cp.start()
                    for a in range(n):
                        if theirs[a]:
                            sized(a, 0, theirs[a]).wait_recv()
                            for k in range(3):
                                sized(a, 4 + k, theirs[a]).wait_recv()
                    for a in range(n):
                        if mine[a]:
                            for k in range(7):
                                sized(a, k, mine[a]).wait_send()
                            ref = ins[a].at[pl.ds(0, mine[a])]
                            pltpu.make_async_copy(ref, ref, lsem.at[a]).wait()

    res = pl.pallas_call(
        body, name=name,
        out_shape=[jax.ShapeDtypeStruct(z.shape, z.dtype) for z in list(zeros) + list(carry)],
        in_specs=[_HBM] * (2 * n + m), out_specs=[_HBM] * (n + m),
        input_output_aliases={n + a: a for a in range(n + m)},
        scratch_shapes=[pltpu.SemaphoreType.DMA((7 * n,)), pltpu.SemaphoreType.DMA((7 * n,)),
                        pltpu.SemaphoreType.DMA((n,))],
    )(*srcs, *zeros, *carry)
    return res[:n], res[n:]


_SEM = pl.BlockSpec(memory_space=pltpu.SEMAPHORE)
_VMEM_TOKEN = pl.BlockSpec(memory_space=pltpu.VMEM)
_TOKEN = jax.ShapeDtypeStruct((8, 128), F32)
_EFFECT = pltpu.SideEffectType.DATAFLOW_SIDE_EFFECTING


def _gather_start(names, srcs, zeros, name, after=None):
    n = len(srcs)
    lay = [LAYOUT[nm] for nm in names]
    extra = [] if after is None else [after]

    def body(*refs):
        ins, lands = refs[:n], refs[n:2 * n]
        send, recv, lsem = refs[2 * n + len(extra):2 * n + len(extra) + 3]
        refs[-1][...] = jnp.zeros_like(refs[-1])
        x, y, c = _position()
        j = 2 * x + y
        chips = [(1 - x, y), (x, 1 - y), (1 - x, 1 - y)]
        for j0 in range(N_CHIPS):
            @pl.when(j == j0)
            def _(j0=j0):
                for a in range(n):
                    for s0, d0, nr in lay[a][2](j0):
                        src, dst = ins[a].at[pl.ds(s0, nr)], lands[a].at[pl.ds(d0, nr)]
                        pltpu.make_async_copy(src, dst, lsem.at[a]).start()
                        for k, chip in enumerate(chips):
                            pltpu.make_async_remote_copy(src_ref=src, dst_ref=dst, send_sem=send.at[3 * a + k],
                                                         recv_sem=recv.at[3 * a + k], device_id=(*chip, c),
                                                         device_id_type=MESH).start()

    sems = [pltpu.SemaphoreType.DMA((3 * n,)), pltpu.SemaphoreType.DMA((3 * n,)), pltpu.SemaphoreType.DMA((n,))]
    hbm = lambda a: pltpu.HBM(a.shape, a.dtype)
    res = pl.pallas_call(
        body, name=name,
        out_shape=sems + [hbm(s) for s in srcs] + [hbm(z) for z in zeros] + [_TOKEN],
        in_specs=[_HBM] * (2 * n) + [pl.BlockSpec(memory_space=pl.ANY)] * len(extra),
        out_specs=[_SEM] * 3 + [_HBM] * (2 * n) + [_VMEM_TOKEN],
        input_output_aliases={a: 3 + a for a in range(2 * n)},
        compiler_params=pltpu.CompilerParams(has_side_effects=_EFFECT),
    )(*[pltpu.with_memory_space_constraint(s, pltpu.HBM) for s in srcs],
      *[pltpu.with_memory_space_constraint(z, pltpu.HBM) for z in zeros], *extra)
    return res[:3], res[3:3 + n], res[3 + n:3 + 2 * n], res[-1]


def _gather_wait(names, sems, srcs, lands, after, name):
    n = len(srcs)
    lay = [LAYOUT[nm] for nm in names]

    def body(*refs):
        ins, zones = refs[:n], refs[n:2 * n]
        send, recv, lsem = refs[2 * n:2 * n + 3]
        x, y, c = _position()
        for a in range(n):
            whole = zones[a].at[pl.ds(0, lay[a][0])]
            for k in range(3):
                cp = pltpu.make_async_remote_copy(src_ref=ins[a], dst_ref=whole, send_sem=send.at[3 * a + k],
                                                  recv_sem=recv.at[3 * a + k], device_id=(x, y, 1 - c),
                                                  device_id_type=MESH)
                cp.wait_send()
                cp.wait_recv()
            pltpu.make_async_copy(ins[a], whole, lsem.at[a]).wait()

    hbm = lambda a: pltpu.HBM(a.shape, a.dtype)
    res = pl.pallas_call(
        body, name=name,
        out_shape=[hbm(s) for s in srcs] + [hbm(z) for z in lands],
        in_specs=[_HBM] * (2 * n) + [_SEM] * 3 + [pl.BlockSpec(memory_space=pl.ANY)], out_specs=[_HBM] * (2 * n),
        input_output_aliases={a: a for a in range(2 * n)},
        compiler_params=pltpu.CompilerParams(has_side_effects=_EFFECT),
    )(*srcs, *lands, *sems, after)
    return res[n:]


def _clip_pieces(lay_a, jsrc, c0):
    h = lay_a[1] // 2
    lo, hi = c0 * h, (c0 + 1) * h
    out = []
    for s0, d0, nr in lay_a[2](jsrc):
        a, b = max(d0, lo), min(d0 + nr, hi)
        if a < b:
            out.append((s0 + a - d0, a, b - a))
    return out


def _rows_of(pieces):
    return sum(nr for _, _, nr in pieces)


def _both_cores(body_for):
    x, y, c = _position()
    j = 2 * x + y
    for j0 in range(N_CHIPS):
        for c0 in range(2):
            @pl.when((j == j0) & (c == c0))
            def _(j0=j0, c0=c0):
                body_for(j0, c0)


STAGE_ROWS = 512


def _staged_copy(src, dst, buf, sem_in, sem_out, rows):
    ch = buf.shape[0]
    for r in range(0, rows, ch):
        nr = min(ch, rows - r)
        stage = buf.at[pl.ds(0, nr)]
        cin = pltpu.make_async_copy(src.at[pl.ds(r, nr)], stage, sem_in)
        cin.start()
        cin.wait()
        cout = pltpu.make_async_copy(stage, dst.at[pl.ds(r, nr)], sem_out)
        cout.start()
        cout.wait()


def _half_to_sibling(names, gl, name, after=None):
    n = len(gl)
    halves = [LAYOUT[nm][1] // 2 for nm in names]
    extra = [] if after is None else [after]

    def body(*refs):
        ins, outs = refs[:n], refs[n + len(extra):2 * n + len(extra)]
        send, recv = refs[2 * n + len(extra):]
        x, y, c = _position()

        def run(j0, c0):
            cps = [pltpu.make_async_remote_copy(src_ref=ins[a].at[pl.ds((1 - c0) * halves[a], halves[a])], dst_ref=outs[a],
                                                send_sem=send.at[a], recv_sem=recv.at[a], device_id=(x, y, 1 - c),
                                                device_id_type=MESH) for a in range(n)]
            for cp in cps:
                cp.start()
            for cp in cps:
                cp.wait()

        _both_cores(run)

    return pl.pallas_call(
        body, name=name,
        out_shape=[jax.ShapeDtypeStruct((halves[a],) + gl[a].shape[1:], gl[a].dtype) for a in range(n)],
        in_specs=[_HBM] * n + [pl.BlockSpec(memory_space=pl.ANY)] * len(extra), out_specs=[_HBM] * n,
        scratch_shapes=[pltpu.SemaphoreType.DMA((n,)), pltpu.SemaphoreType.DMA((n,))],
    )(*gl, *extra)


def _chip_scatter_half(names, parts, name):
    n = len(parts)
    lay = [LAYOUT[nm] for nm in names]
    zeros = [jnp.zeros((N_CHIPS, lay[a][0]) + parts[a].shape[1:], parts[a].dtype) for a in range(n)]

    def body(*refs):
        ins, outs = refs[:n], refs[2 * n:3 * n]
        send, recv = refs[3 * n:3 * n + 2]
        stage, sem_in, sem_out = refs[3 * n + 2:4 * n + 2], refs[4 * n + 2], refs[4 * n + 3]
        x, y, c = _position()
        chips = [(1 - x, y), (x, 1 - y), (1 - x, 1 - y)]

        def run(j0, c0):
            def sized(a, rows):
                return outs[a].at[0, pl.ds(0, rows)]

            for a in range(n):
                base = c0 * (lay[a][1] // 2)
                for k, chip in enumerate(chips):
                    for s0, d0, nr in _clip_pieces(lay[a], j0 ^ _REL[k], c0):
                        pltpu.make_async_remote_copy(
                            src_ref=ins[a].at[pl.ds(d0 - base, nr)], dst_ref=outs[a].at[j0, pl.ds(s0, nr)],
                            send_sem=send.at[3 * a + k], recv_sem=recv.at[3 * a + k],
                            device_id=(*chip, c), device_id_type=MESH).start()
            for a in range(n):
                base = c0 * (lay[a][1] // 2)
                for s0, d0, nr in _clip_pieces(lay[a], j0, c0):
                    _staged_copy(ins[a].at[pl.ds(d0 - base, nr)], outs[a].at[j0, pl.ds(s0, nr)], stage[a],
                                 sem_in.at[a], sem_out.at[a], nr)
            for a in range(n):
                got = _rows_of(_clip_pieces(lay[a], j0, c0))
                for k in range(3):
                    sent = _rows_of(_clip_pieces(lay[a], j0 ^ _REL[k], c0))
                    if sent:
                        pltpu.make_async_remote_copy(src_ref=sized(a, sent), dst_ref=sized(a, sent),
                                                     send_sem=send.at[3 * a + k], recv_sem=recv.at[3 * a + k],
                                                     device_id=(x, y, c), device_id_type=MESH).wait_send()
                    if got:
                        pltpu.make_async_remote_copy(src_ref=sized(a, got), dst_ref=sized(a, got),
                                                     send_sem=send.at[3 * a + k], recv_sem=recv.at[3 * a + k],
                                                     device_id=(x, y, c), device_id_type=MESH).wait_recv()

        _both_cores(run)

    return pl.pallas_call(
        body, name=name,
        out_shape=[jax.ShapeDtypeStruct(z.shape, z.dtype) for z in zeros],
        in_specs=[_HBM] * (2 * n), out_specs=[_HBM] * n, input_output_aliases={n + a: a for a in range(n)},
        scratch_shapes=[pltpu.SemaphoreType.DMA((3 * n,)), pltpu.SemaphoreType.DMA((3 * n,))]
        + [pltpu.VMEM((min(STAGE_ROWS, p.shape[0]),) + p.shape[1:], p.dtype) for p in parts]
        + [pltpu.SemaphoreType.DMA((n,)), pltpu.SemaphoreType.DMA((n,))],
    )(*parts, *zeros)


def _subset_exchange(names, bufs, l, name):
    n = len(bufs)
    lay = [LAYOUT[nm] for nm in names]

    def body(*refs):
        outs = refs[n:2 * n]
        send, recv = refs[2 * n:]
        x, y, c = _position()

        def run(j0, c0):
            for a in range(n):
                for s0, _, nr in _clip_pieces(lay[a], j0, c0):
                    rows = outs[a].at[l, pl.ds(s0, nr)]
                    pltpu.make_async_remote_copy(src_ref=rows, dst_ref=rows, send_sem=send.at[a], recv_sem=recv.at[a],
                                                 device_id=(x, y, 1 - c), device_id_type=MESH).start()
            for a in range(n):
                for c_half, wait_send in ((c0, True), (1 - c0, False)):
                    rows = _rows_of(_clip_pieces(lay[a], j0, c_half))
                    if rows:
                        ref = outs[a].at[l, pl.ds(0, rows)]
                        cp = pltpu.make_async_remote_copy(src_ref=ref, dst_ref=ref, send_sem=send.at[a], recv_sem=recv.at[a],
                                                          device_id=(x, y, 1 - c), device_id_type=MESH)
                        if wait_send:
                            cp.wait_send()
                        else:
                            cp.wait_recv()

        _both_cores(run)

    return pl.pallas_call(
        body, name=name,
        out_shape=[jax.ShapeDtypeStruct(b.shape, b.dtype) for b in bufs],
        in_specs=[_HBM] * n, out_specs=[_HBM] * n, input_output_aliases={a: a for a in range(n)},
        scratch_shapes=[pltpu.SemaphoreType.DMA((n,)), pltpu.SemaphoreType.DMA((n,))],
    )(*bufs)


def _scatter_start(names, gl, name):
    n = len(gl)
    lay = [LAYOUT[nm] for nm in names]
    zones = [lax.empty((N_CHIPS, lay[a][0]) + gl[a].shape[1:], gl[a].dtype) for a in range(n)]

    def body(*refs):
        ins, lands = refs[:n], refs[n:2 * n]
        send, recv, lsem = refs[2 * n:2 * n + 3]
        refs[-1][...] = jnp.zeros_like(refs[-1])
        x, y, c = _position()
        j = 2 * x + y
        chips = [(1 - x, y), (x, 1 - y), (1 - x, 1 - y)]
        for j0 in range(N_CHIPS):
            @pl.when(j == j0)
            def _(j0=j0):
                for a in range(n):
                    for s0, d0, nr in lay[a][2](j0):
                        pltpu.make_async_copy(ins[a].at[pl.ds(d0, nr)], lands[a].at[j0, pl.ds(s0, nr)], lsem.at[a]).start()
                    for k, chip in enumerate(chips):
                        for s0, d0, nr in lay[a][2](j0 ^ _REL[k]):
                            pltpu.make_async_remote_copy(
                                src_ref=ins[a].at[pl.ds(d0, nr)], dst_ref=lands[a].at[j0, pl.ds(s0, nr)],
                                send_sem=send.at[3 * a + k], recv_sem=recv.at[3 * a + k],
                                device_id=(*chip, c), device_id_type=MESH).start()

    sems = [pltpu.SemaphoreType.DMA((3 * n,)), pltpu.SemaphoreType.DMA((3 * n,)), pltpu.SemaphoreType.DMA((n,))]
    hbm = lambda a: pltpu.HBM(a.shape, a.dtype)
    res = pl.pallas_call(
        body, name=name,
        out_shape=sems + [hbm(g) for g in gl] + [hbm(z) for z in zones] + [_TOKEN],
        in_specs=[_HBM] * (2 * n), out_specs=[_SEM] * 3 + [_HBM] * (2 * n) + [_VMEM_TOKEN],
        input_output_aliases={a: 3 + a for a in range(2 * n)},
        compiler_params=pltpu.CompilerParams(has_side_effects=_EFFECT),
    )(*[pltpu.with_memory_space_constraint(g, pltpu.HBM) for g in gl],
      *[pltpu.with_memory_space_constraint(z, pltpu.HBM) for z in zones])
    return res[:3], res[3:3 + n], res[3 + n:3 + 2 * n], res[-1]


def _scatter_wait(names, sems, srcs, lands, after, name):
    n = len(srcs)
    lay = [LAYOUT[nm] for nm in names]

    def body(*refs):
        zones = refs[n:2 * n]
        send, recv, lsem = refs[2 * n:2 * n + 3]
        x, y, c = _position()
        for a in range(n):
            whole = zones[a].at[0, pl.ds(0, lay[a][0])]
            for k in range(3):
                cp = pltpu.make_async_remote_copy(src_ref=whole, dst_ref=whole, send_sem=send.at[3 * a + k],
                                                  recv_sem=recv.at[3 * a + k], device_id=(x, y, 1 - c),
                                                  device_id_type=MESH)
                cp.wait_send()
                cp.wait_recv()
            pltpu.make_async_copy(whole, whole, lsem.at[a]).wait()

    hbm = lambda a: pltpu.HBM(a.shape, a.dtype)
    res = pl.pallas_call(
        body, name=name,
        out_shape=[hbm(s) for s in srcs] + [hbm(z) for z in lands],
        in_specs=[_HBM] * (2 * n) + [_SEM] * 3 + [pl.BlockSpec(memory_space=pl.ANY)], out_specs=[_HBM] * (2 * n),
        input_output_aliases={a: a for a in range(2 * n)},
        compiler_params=pltpu.CompilerParams(has_side_effects=_EFFECT),
    )(*srcs, *lands, *sems, after)
    return res[n:]


def _peer(x, y, c, k):
    return (1 - x if k & 4 else x, 1 - y if k & 2 else y, 1 - c if k & 1 else c)


def _bcast_start(arrs, name, after=None):
    n = len(arrs)
    zones = [lax.empty((8,) + a.shape, a.dtype) for a in arrs]
    extra = [] if after is None else [after]

    def body(*refs):
        ins, lands = refs[:n], refs[n:2 * n]
        send, recv, lsem = refs[2 * n + len(extra):2 * n + len(extra) + 3]
        refs[-1][...] = jnp.zeros_like(refs[-1])
        x, y, c = _position()
        for a in range(n):
            dst = lands[a].at[4 * x + 2 * y + c]
            pltpu.make_async_copy(ins[a], dst, lsem.at[a]).start()
            for k in range(1, 8):
                pltpu.make_async_remote_copy(src_ref=ins[a], dst_ref=dst, send_sem=send.at[7 * a + k - 1],
                                             recv_sem=recv.at[7 * a + k - 1], device_id=_peer(x, y, c, k),
                                             device_id_type=MESH).start()

    sems = [pltpu.SemaphoreType.DMA((7 * n,)), pltpu.SemaphoreType.DMA((7 * n,)), pltpu.SemaphoreType.DMA((n,))]
    hbm = lambda a: pltpu.HBM(a.shape, a.dtype)
    res = pl.pallas_call(
        body, name=name,
        out_shape=sems + [hbm(a) for a in arrs] + [hbm(z) for z in zones] + [_TOKEN],
        in_specs=[_HBM] * (2 * n) + [pl.BlockSpec(memory_space=pl.ANY)] * len(extra),
        out_specs=[_SEM] * 3 + [_HBM] * (2 * n) + [_VMEM_TOKEN],
        input_output_aliases={a: 3 + a for a in range(2 * n)},
        compiler_params=pltpu.CompilerParams(has_side_effects=_EFFECT),
    )(*[pltpu.with_memory_space_constraint(a, pltpu.HBM) for a in arrs],
      *[pltpu.with_memory_space_constraint(z, pltpu.HBM) for z in zones], *extra)
    return res[:3], res[3:3 + n], res[3 + n:3 + 2 * n], res[-1]


def _bcast_wait(sems, srcs, lands, after, name):
    n = len(srcs)

    def body(*refs):
        ins, zones = refs[:n], refs[n:2 * n]
        send, recv, lsem = refs[2 * n:2 * n + 3]
        x, y, c = _position()
        for a in range(n):
            for k in range(1, 8):
                cp = pltpu.make_async_remote_copy(src_ref=ins[a], dst_ref=zones[a].at[0], send_sem=send.at[7 * a + k - 1],
                                                  recv_sem=recv.at[7 * a + k - 1], device_id=_peer(x, y, c, k),
                                                  device_id_type=MESH)
                cp.wait_send()
                cp.wait_recv()
            pltpu.make_async_copy(ins[a], zones[a].at[0], lsem.at[a]).wait()

    hbm = lambda a: pltpu.HBM(a.shape, a.dtype)
    res = pl.pallas_call(
        body, name=name,
        out_shape=[hbm(s) for s in srcs] + [hbm(z) for z in lands],
        in_specs=[_HBM] * (2 * n) + [_SEM] * 3 + [pl.BlockSpec(memory_space=pl.ANY)], out_specs=[_HBM] * (2 * n),
        input_output_aliases={a: a for a in range(2 * n)},
        compiler_params=pltpu.CompilerParams(has_side_effects=_EFFECT),
    )(*srcs, *lands, *sems, after)
    return res[n:]


def _swap_start(arrs, name):
    n = len(arrs)
    zones = [lax.empty(a.shape, a.dtype) for a in arrs]

    def body(*refs):
        ins, lands = refs[:n], refs[n:2 * n]
        send, recv = refs[2 * n:2 * n + 2]
        refs[-1][...] = jnp.zeros_like(refs[-1])
        x, y, c = _position()
        for a in range(n):
            pltpu.make_async_remote_copy(src_ref=ins[a], dst_ref=lands[a], send_sem=send.at[a], recv_sem=recv.at[a],
                                         device_id=(x, y, 1 - c), device_id_type=MESH).start()

    sems = [pltpu.SemaphoreType.DMA((n,)), pltpu.SemaphoreType.DMA((n,))]
    hbm = lambda a: pltpu.HBM(a.shape, a.dtype)
    res = pl.pallas_call(
        body, name=name,
        out_shape=sems + [hbm(a) for a in arrs] + [hbm(z) for z in zones] + [_TOKEN],
        in_specs=[_HBM] * (2 * n), out_specs=[_SEM] * 2 + [_HBM] * (2 * n) + [_VMEM_TOKEN],
        input_output_aliases={a: 2 + a for a in range(2 * n)},
        compiler_params=pltpu.CompilerParams(has_side_effects=_EFFECT),
    )(*[pltpu.with_memory_space_constraint(a, pltpu.HBM) for a in arrs],
      *[pltpu.with_memory_space_constraint(z, pltpu.HBM) for z in zones])
    return res[:2], res[2:2 + n], res[2 + n:2 + 2 * n], res[-1]


def _swap_wait(sems, srcs, lands, after, name):
    n = len(srcs)

    def body(*refs):
        ins, zones = refs[:n], refs[n:2 * n]
        send, recv = refs[2 * n:2 * n + 2]
        x, y, c = _position()
        for a in range(n):
            cp = pltpu.make_async_remote_copy(src_ref=ins[a], dst_ref=zones[a], send_sem=send.at[a], recv_sem=recv.at[a],
                                              device_id=(x, y, 1 - c), device_id_type=MESH)
            cp.wait_send()
            cp.wait_recv()

    hbm = lambda a: pltpu.HBM(a.shape, a.dtype)
    res = pl.pallas_call(
        body, name=name,
        out_shape=[hbm(s) for s in srcs] + [hbm(z) for z in lands],
        in_specs=[_HBM] * (2 * n) + [_SEM] * 2 + [pl.BlockSpec(memory_space=pl.ANY)], out_specs=[_HBM] * (2 * n),
        input_output_aliases={a: a for a in range(2 * n)},
        compiler_params=pltpu.CompilerParams(has_side_effects=_EFFECT),
    )(*srcs, *lands, *sems, after)
    return res[:n], res[n:]


def _row_tile(r):
    for t in (256, 128, 64, 32, 16, 8):
        if r % t == 0 and r > t:
            return t
    return r


def _pair_add_half(g, rb, c_arr, name):
    hrows, rest = rb.shape[0], rb.shape[1:]
    tr = _row_tile(hrows)
    nb = hrows // tr
    z = (0,) * len(rest)

    def body(c_ref, g_ref, r_ref, o_ref):
        o_ref[...] = (g_ref[...].astype(F32) + r_ref[...].astype(F32)).astype(o_ref.dtype)

    return pl.pallas_call(
        body, name=name,
        grid_spec=pltpu.PrefetchScalarGridSpec(
            num_scalar_prefetch=1, grid=(nb,),
            in_specs=[pl.BlockSpec((tr,) + rest, lambda i, c_ref: (c_ref[0] * nb + i,) + z),
                      pl.BlockSpec((tr,) + rest, lambda i, c_ref: (i,) + z)],
            out_specs=pl.BlockSpec((tr,) + rest, lambda i, c_ref: (i,) + z)),
        out_shape=jax.ShapeDtypeStruct((hrows,) + rest, BF16),
        compiler_params=pltpu.CompilerParams(dimension_semantics=("parallel",), vmem_limit_bytes=VMEM_LIMIT),
    )(c_arr, g, rb)


def _sum_slabs(slabs, l, buf, name):
    m = len(slabs)
    n, R, rest = slabs[0].shape[0], slabs[0].shape[1], slabs[0].shape[2:]
    tr = _row_tile(R)
    z = (0,) * len(rest)

    def body(*refs):
        total = None
        for r_ref in refs[:m]:
            acc = r_ref[0].astype(F32)
            for k in range(1, n):
                acc = acc + r_ref[k].astype(F32)
            total = acc if total is None else total + acc
        refs[-1][...] = total

    if R // tr > 64 and len(rest) == 1 and rest[0] % 256 == 0:
        grid = (rest[0] // 256,)
        in_spec = pl.BlockSpec((n, R, 256), lambda i: (0, 0, i))
        out_spec = pl.BlockSpec((None, R, 256), lambda i: (l, 0, i))
    else:
        grid = (R // tr,)
        in_spec = pl.BlockSpec((n, tr) + rest, lambda i: (0, i) + z)
        out_spec = pl.BlockSpec((None, tr) + rest, lambda i: (l, i) + z)
    in_specs, args, aliases = [in_spec] * m, list(slabs), {}
    if buf is not None:
        in_specs.append(pl.BlockSpec(memory_space=pl.ANY))
        args.append(buf)
        aliases = {m: 0}
    return pl.pallas_call(
        body, name=name, grid=grid, in_specs=in_specs, out_specs=out_spec,
        out_shape=jax.ShapeDtypeStruct((DEPTH, R) + rest, F32), input_output_aliases=aliases,
        compiler_params=pltpu.CompilerParams(
            dimension_semantics=("parallel",),
            vmem_limit_bytes=_vmem(m * _nbytes(in_spec.block_shape, slabs[0].dtype) + _nbytes(out_spec.block_shape, F32),
                                   2 * _nbytes(out_spec.block_shape, F32))),
    )(*args)


def _adam_math(w, g, m, v):
    mn = ADAM_B1 * m + (1.0 - ADAM_B1) * g
    vn = ADAM_B2 * v + (1.0 - ADAM_B2) * (g * g)
    m_hat = mn / (1.0 - ADAM_B1 ** ADAM_STEP)
    v_hat = vn / (1.0 - ADAM_B2 ** ADAM_STEP)
    return -ADAM_LR * (m_hat / (jnp.sqrt(v_hat) + ADAM_EPS) + ADAM_WD * w), mn, vn


def _adamw(w, g, m, v, name):
    L, R, C = w.shape
    tr = _row_tile(R)

    def body(w_ref, g_ref, m_ref, v_ref, d_ref, mo_ref, vo_ref):
        d_ref[...], mo_ref[...], vo_ref[...] = _adam_math(w_ref[...], g_ref[...], m_ref[...], v_ref[...])

    if R // tr > 64 and C % 128 == 0:
        spec, grid = pl.BlockSpec((None, R, 128), lambda l, i: (l, 0, i)), (L, C // 128)
    else:
        spec, grid = pl.BlockSpec((None, tr, C), lambda l, i: (l, i, 0)), (L, R // tr)
    return pl.pallas_call(
        body, name=name, grid=grid, in_specs=[spec] * 4, out_specs=[spec] * 3,
        out_shape=[jax.ShapeDtypeStruct((L, R, C), F32)] * 3,
        compiler_params=pltpu.CompilerParams(dimension_semantics=("parallel", "parallel"),
                                             vmem_limit_bytes=_vmem(7 * _nbytes(spec.block_shape, F32))),
    )(w, g, m, v)


_VMEM_WHOLE = pl.BlockSpec(memory_space=pltpu.VMEM)


def _matrix_update(gath, w, m, v, name):
    K = w.shape[1]

    def body(g0_ref, g1_ref, w_ref, m_ref, v_ref, go_ref, d_ref, mo_ref, vo_ref):
        for l, gr in enumerate((g0_ref, g1_ref)):
            for k in range(K):
                g = gr[0, k].astype(F32)
                for dev in range(1, 8):
                    g = g + gr[dev, k].astype(F32)
                go_ref[l, k] = g
                d_ref[l, k], mo_ref[l, k], vo_ref[l, k] = _adam_math(w_ref[l, k], g, m_ref[l, k], v_ref[l, k])

    return pl.pallas_call(
        body, name=name, in_specs=[_VMEM_WHOLE] * 5, out_specs=[_VMEM_WHOLE] * 4,
        out_shape=[jax.ShapeDtypeStruct(w.shape, F32)] * 4,
        compiler_params=pltpu.CompilerParams(vmem_limit_bytes=32 * MIB),
    )(gath[0], gath[1], w, m, v)


VECS = (("pre_norm_g", D), ("post_norm_g", D), ("gm_ln_g", GM_W), ("gm_ln_b", GM_W), ("mla_q_norm_g", QR),
        ("mla_kv_norm_g", KVR), ("lru_conv_b", LRU_W), ("lru_b_a", LRU_W), ("lru_b_x", LRU_W), ("lru_lambda", LRU_W))
VEC_KEY = {"pre_norm_g": "pre_g", "post_norm_g": "post_g", "gm_ln_g": "ln_g", "gm_ln_b": "ln_b", "mla_q_norm_g": "qg",
           "mla_kv_norm_g": "kvg", "lru_conv_b": "conv_b", "lru_b_a": "ba", "lru_b_x": "bx", "lru_lambda": "lam"}
VEC_ROWS, VEC_W, VEC_ROW0, LOSS_ROW = 16, LRU_W, GM_G, 14


def _pack_rows(LG, loss_part):
    per = len(VECS) + 1
    ins = []
    for G in LG:
        ins += [G[VEC_KEY[n]] for n, _ in VECS] + [G["bst"]]
    ins.append(loss_part)

    def body(*refs):
        o_ref = refs[-1]
        o_ref[...] = jnp.zeros_like(o_ref)
        for l in range(DEPTH):
            base = VEC_ROWS * l
            o_ref[pl.ds(base, 8), pl.ds(0, GM_B)] = refs[per * l + len(VECS)][...].T[:8, :]
            for t, (_, width) in enumerate(VECS):
                o_ref[pl.ds(base + VEC_ROW0 + t, 1), pl.ds(0, width)] = refs[per * l + t][...]
        o_ref[pl.ds(LOSS_ROW, 1), pl.ds(0, 128)] = jnp.broadcast_to(refs[-2][...], (1, 128))

    return pl.pallas_call(
        body, name="pack_rows", in_specs=[_VMEM_WHOLE] * len(ins), out_specs=_VMEM_WHOLE,
        out_shape=jax.ShapeDtypeStruct((DEPTH * VEC_ROWS, VEC_W), F32),
    )(*ins)


def _vector_update(gath, W, M, V):
    names = [n for n, _ in VECS] + ["gm_bs"]
    nw = len(names)

    def body(*refs):
        g_ref = refs[0]
        wr, mr, vr = refs[1:1 + nw], refs[1 + nw:1 + 2 * nw], refs[1 + 2 * nw:1 + 3 * nw]
        outs = refs[1 + 3 * nw:]
        s = g_ref[0]
        for dev in range(1, 8):
            s = s + g_ref[dev]
        for t, (_, width) in enumerate(VECS):
            for l in range(DEPTH):
                r = VEC_ROWS * l + VEC_ROW0 + t
                g = s[r:r + 1, :width]
                row = (pl.ds(l, 1), slice(None))
                res = (g,) + _adam_math(wr[t][row], g, mr[t][row], vr[t][row])
                for q in range(4):
                    outs[4 * t + q][row] = res[q]
        t = len(VECS)
        for l in range(DEPTH):
            for k in range(GM_G):
                g = s[VEC_ROWS * l + k:VEC_ROWS * l + k + 1, :GM_B]
                row = (l, pl.ds(k, 1), slice(None))
                res = (g,) + _adam_math(wr[t][row], g, mr[t][row], vr[t][row])
                for q in range(4):
                    outs[4 * t + q][row] = res[q]
        outs[4 * nw][...] = s[LOSS_ROW:LOSS_ROW + 1, :128]

    ws = [W[n] for n in names]
    out_shape = []
    for w in ws:
        out_shape += [jax.ShapeDtypeStruct(w.shape, F32)] * 4
    out_shape.append(jax.ShapeDtypeStruct((1, 128), F32))
    res = pl.pallas_call(
        body, name="vector_update", in_specs=[_VMEM_WHOLE] * (1 + 3 * nw), out_specs=[_VMEM_WHOLE] * (4 * nw + 1),
        out_shape=out_shape, compiler_params=pltpu.CompilerParams(vmem_limit_bytes=VMEM_LIMIT),
    )(gath, *ws, *[M[n] for n in names], *[V[n] for n in names])
    return {n: tuple(res[4 * t:4 * t + 4]) for t, n in enumerate(names)}, res[4 * nw]


SHARDED = ("w_in", "mla_w_uq", "mla_w_ukv", "lru_conv_w", "w_proj_a", "w_proj_b", "w_proj_c", "w_out")
FIRST = ("w_in", "lru_conv_w")
LATER = tuple(n for n in SHARDED if n not in FIRST)
COL_SHARDED = ("w_in", "mla_w_uq", "mla_w_ukv", "lru_conv_w")
SMALL = ("pre_norm_g", "gm_ln_g", "gm_ln_b", "gm_ws", "gm_bs", "mla_q_norm_g", "mla_kv_norm_g", "lru_conv_b",
         "lru_w_a", "lru_b_a", "lru_w_x", "lru_b_x", "lru_lambda", "post_norm_g")
WEIGHTS = ("pre_norm_g", "w_in", "gm_ln_g", "gm_ln_b", "gm_ws", "gm_bs", "mla_q_norm_g", "mla_w_uq",
           "mla_kv_norm_g", "mla_w_ukv", "lru_conv_w", "lru_conv_b", "lru_w_a", "lru_b_a", "lru_w_x", "lru_b_x",
           "lru_lambda", "w_proj_a", "w_proj_b", "w_proj_c", "w_out", "post_norm_g")


GB_KEY = {"w_in": "wp", "mla_w_uq": "wuq", "mla_w_ukv": "wukv", "w_proj_a": "wpa", "w_proj_b": "wpb",
          "w_proj_c": "wpc", "w_out": "wout"}


def _prepare(l, gathered, small, wsb):
    P = {GB_KEY[n]: gathered[n] for n in GB_KEY if n in gathered}
    P["conv_w"] = gathered["lru_conv_w"].transpose(1, 0, 2).reshape(CONV_W, LRU_W)
    P["wsb"] = wsb
    row = lambda n: small[n][l][None, :]
    P["pre_g"], P["post_g"] = row("pre_norm_g"), row("post_norm_g")
    P["ln_g"], P["ln_b"] = row("gm_ln_g"), row("gm_ln_b")
    P["ws"] = small["gm_ws"][l]
    P["bst"] = jnp.pad(small["gm_bs"][l].T, ((0, 0), (0, 128 - GM_G)))
    P["qg"], P["kvg"] = row("mla_q_norm_g"), row("mla_kv_norm_g")
    P["conv_b"], P["ba"], P["bx"], P["lam"] = row("lru_conv_b"), row("lru_b_a"), row("lru_b_x"), row("lru_lambda")
    return P


def kernel(x, pre_norm_g, w_in, gm_ln_g, gm_ln_b, gm_ws, gm_bs, mla_q_norm_g, mla_w_uq, mla_kv_norm_g, mla_w_ukv, lru_conv_w, lru_conv_b, lru_w_a, lru_b_a, lru_w_x, lru_b_x, lru_lambda, w_proj_a, w_proj_b, w_proj_c, w_out, post_norm_g, loss_target, m_pre_norm_g, m_w_in, m_gm_ln_g, m_gm_ln_b, m_gm_ws, m_gm_bs, m_mla_q_norm_g, m_mla_w_uq, m_mla_kv_norm_g, m_mla_w_ukv, m_lru_conv_w, m_lru_conv_b, m_lru_w_a, m_lru_b_a, m_lru_w_x, m_lru_b_x, m_lru_lambda, m_w_proj_a, m_w_proj_b, m_w_proj_c, m_w_out, m_post_norm_g, v_pre_norm_g, v_w_in, v_gm_ln_g, v_gm_ln_b, v_gm_ws, v_gm_bs, v_mla_q_norm_g, v_mla_w_uq, v_mla_kv_norm_g, v_mla_w_ukv, v_lru_conv_w, v_lru_conv_b, v_lru_w_a, v_lru_b_a, v_lru_w_x, v_lru_b_x, v_lru_lambda, v_w_proj_a, v_w_proj_b, v_w_proj_c, v_w_out, v_post_norm_g):
    args = dict(locals())
    W = {n: args[n] for n in WEIGHTS}
    M = {n: args["m_" + n] for n in WEIGHTS}
    V = {n: args["v_" + n] for n in WEIGHTS}
    c = lax.axis_index("c")

    def shards(l, names):
        out = []
        for n in names:
            blk = W[n][l].T if n in TRANSPOSED else W[n][l]
            out.append(blk[None] if n == "lru_conv_w" else blk.astype(BF16))
        return out

    small = {n: W[n] for n in SMALL}
    wsb = _superblocks(W["lru_w_a"], W["lru_w_x"])
    tabs = _rope_tables()
    s0a, s0b, s1a, s1b = shards(0, FIRST), shards(0, LATER), shards(1, FIRST), shards(1, LATER)
    g0, zones = _weights_allgather(FIRST, s0a, "weights_allgather_l0", carry=_gather_zeros(LATER, s0b)
                                   + _gather_zeros(FIRST, s1a) + _gather_zeros(LATER, s1b))
    nl, nf = len(LATER), len(FIRST)
    w0b = _gather_start(LATER, s0b, zones[:nl], "weights_gather_start_l0")
    w1a = _gather_start(FIRST, s1a, zones[nl:nl + nf], "weights_gather_start_l1_first", after=w0b[3])
    w1b = _gather_start(LATER, s1b, zones[nl + nf:], "weights_gather_start_l1_later", after=w1a[3])

    def late(started, name):
        def wait(after):
            got = _gather_wait(LATER, *started[:3], after, name)
            return {GB_KEY[n]: g for n, g in zip(LATER, got)}
        return wait

    P = [_prepare(0, dict(zip(FIRST, g0)), small, wsb), None]
    h0 = x[0]
    h1, A0 = _layer_fwd(h0, P[0], 0, tabs, w1b[3], late(w0b, "weights_gather_wait_l0"))
    g1 = _gather_wait(FIRST, *w1a[:3], h1, "weights_gather_wait_l1_first")
    P[1] = _prepare(1, dict(zip(FIRST, g1)), small, wsb)
    h2, A1 = _layer_fwd(h1, P[1], 1, tabs, None, late(w1b, "weights_gather_wait_l1_later"))
    dy, loss_part = _loss_fwd(h2, loss_target[0])

    def large_grads(G, GB, names):
        conv = G["conv_w"].reshape(CONV_W, N_CHIPS, LRU_W // N_CHIPS).transpose(1, 0, 2)
        return [conv if n == "lru_conv_w" else GB[GB_KEY[n]] for n in names]

    d1, G1, GB1 = _layer_bwd(dy, A1, P[1], 1, tabs)
    sc1 = _scatter_start(SHARDED, large_grads(G1, GB1, SHARDED), "grads_scatter_start_l1")

    def small_mats(g):
        return [g["ws"].astype(BF16), g["wab"][0, :, :, :LRU_BW], g["wab"][1, :, :, :LRU_BW]]

    bc1 = _bcast_start(small_mats(G1), "small_grads_start_l1", after=sc1[3])
    started = {}

    def early0(GB):
        mine1 = _scatter_wait(SHARDED, *sc1[:3], GB["wukv"], "grads_scatter_wait_l1")
        started["swap1"] = _swap_start(mine1, "partials_swap_start_l1")
        started["sc0"] = _scatter_start(LATER, [GB[GB_KEY[n]] for n in LATER], "grads_scatter_start_l0")
        return started["sc0"][3], started["swap1"][3]

    d0, G0, GB0 = _layer_bwd(d1, A0, P[0], 0, tabs, bc1[3], early0)
    LG = (G0, G1)
    mine0 = _scatter_wait(LATER, *started["sc0"][:3], d0, "grads_scatter_wait_l0")
    swap0 = _swap_start(mine0, "partials_swap_start_l0")
    bc0 = _bcast_start([_pack_rows(LG, loss_part)] + small_mats(G0), "small_grads_start_l0", after=swap0[3])
    g0f = large_grads(G0, GB0, FIRST)
    c_arr = jnp.reshape(c, (1,)).astype(jnp.int32)
    from_sib = _half_to_sibling(FIRST, g0f, "grads_half_to_sibling_l0", after=bc0[3])
    pair = [_pair_add_half(g, rb, c_arr, "pair_add_" + n) for n, g, rb in zip(FIRST, g0f, from_sib)]
    slabs = _chip_scatter_half(FIRST, pair, "grads_chip_scatter_l0")
    mine1, theirs1 = _swap_wait(*started["swap1"][:3], slabs[0], "partials_swap_wait_l1")
    both = dict(zip(SHARDED, [_sum_slabs([a, b], 1, None, "sum_partials_l1_" + n)
                              for n, a, b in zip(SHARDED, mine1, theirs1)]))
    for n, s in zip(FIRST, slabs):
        both[n] = _sum_slabs([s], 0, both[n], "sum_slabs_l0_" + n)
    done = _subset_exchange(FIRST, [both[n] for n in FIRST], 0, "reduced_rows_to_sibling_l0")
    both.update(zip(FIRST, done))
    mine0, theirs0 = _swap_wait(*swap0[:3], done[0], "partials_swap_wait_l0")
    for n, a, b in zip(LATER, mine0, theirs0):
        both[n] = _sum_slabs([a, b], 0, both[n], "sum_partials_l0_" + n)
    both = [both[n] for n in SHARDED]
    grads = {}
    for n, b in zip(SHARDED, both):
        if n in TRANSPOSED and n != "w_in":
            b = jnp.swapaxes(b, 1, 2)
        grads[n] = b if n == "w_in" else b.reshape(W[n].shape)

    gath1 = _bcast_wait(*bc1[:3], both[0], "small_grads_wait_l1")
    gath0 = _bcast_wait(*bc0[:3], gath1[0], "small_grads_wait_l0")
    upd, loss_row = _vector_update(gath0[0], W, M, V)
    loss = loss_row[0, 0]
    for k, n in enumerate(("gm_ws", "lru_w_a", "lru_w_x")):
        upd[n] = _matrix_update((gath0[1 + k], gath1[k]), W[n], M[n], V[n], "update_" + n)

    for n in SHARDED:
        if n == "w_in":
            tr = lambda a: jnp.swapaxes(a, 1, 2)
            res = _adamw(tr(W[n]), grads[n], tr(M[n]), tr(V[n]), "adamw_" + n)
            upd[n] = tuple(tr(a) for a in (grads[n],) + tuple(res))
        else:
            upd[n] = (grads[n],) + tuple(_adamw(W[n], grads[n], M[n], V[n], "adamw_" + n))

    return (loss, d0[None], *[upd[n][0] for n in WEIGHTS], *[upd[n][1] for n in WEIGHTS],
            *[upd[n][2] for n in WEIGHTS], *[upd[n][3] for n in WEIGHTS])
```

```python
import functools
import math

import jax
import jax.numpy as jnp
from jax import lax
from jax.experimental import pallas as pl
from jax.experimental.pallas import tpu as pltpu

F32, BF16 = jnp.float32, jnp.bfloat16
MESH = pl.DeviceIdType.MESH

S, D, DEPTH = 2048, 1024, 2
CHUNK, EPS = 64, 1e-6
GM_W, GM_G, GM_B = 1024, 4, 128
H, NOPE, ROPE, VDIM = 8, 128, 64, 128
QR, KVR = 384, 256
MLA_W = H * VDIM
LRU_W, LRU_NB, LRU_BW, LRU_C, CONV_W = 1280, 16, 80, 8.0, 4
ROPE_THETA = 10000.0
IN_SIZES = (GM_W, GM_W, GM_W, QR, KVR, ROPE, MLA_W, LRU_W, LRU_W, D, D, D)
N_IN = sum(IN_SIZES)
N_CHIPS = 4
ADAM_LR, ADAM_B1, ADAM_B2, ADAM_EPS, ADAM_WD, ADAM_STEP = 0.001, 0.9, 0.999, 1e-08, 0.01, 10

HP = 256
O_U, O_V, O_ZA, O_GA, O_GB, O_GC = 0, 1024, 2048, 3072, 4096, 5120
O_CKV, O_KR, O_CQ, O_XC, O_ZC, O_ZB = 6144, 6400, 6528, 7680, 8960, 10240
NP = 11264
MIB = 1024 * 1024
VMEM_LIMIT = 16 * MIB


def _vmem(block_bytes, temp_bytes=0):
    return int(min(max(2 * block_bytes + temp_bytes + 4 * MIB, VMEM_LIMIT), 56 * MIB))


def _nbytes(shape, dtype):
    return math.prod(d for d in shape if d is not None) * jnp.dtype(dtype).itemsize


def _tile(dim, target):
    if dim <= target:
        return dim
    t = (target // 128) * 128
    while dim % t:
        t -= 128
    return t


def _sig(x):
    return jax.nn.sigmoid(x)


def _silu(x):
    return x * _sig(x)


def _dsilu(x):
    s = _sig(x)
    return s * (1.0 + x * (1.0 - s))


def _mm(a, b, mode, name, out_dtype=F32, tm=512, tn=512, tk=1024, b_lead=None, out_lead=None, token=None):
    b2 = b.shape[1:] if b_lead is not None else b.shape
    if mode == "nn":
        (M, K), (K2, N) = a.shape, b2
    elif mode == "nt":
        (M, K), (N, K2) = a.shape, b2
    else:
        (K, M), (K2, N) = a.shape, b2
    assert K == K2, (name, a.shape, b.shape)
    tm, tn, tk = _tile(M, tm), _tile(N, tn), _tile(K, tk)
    nk = K // tk
    if mode == "tn":
        a_spec = pl.BlockSpec((tk, tm), lambda i, j, k: (k, i))
        lhs_c = 0
    else:
        a_spec = pl.BlockSpec((tm, tk), lambda i, j, k: (i, k))
        lhs_c = 1
    b_blk, b_idx, rhs_c = ((tn, tk), (lambda i, j, k: (j, k)), 1) if mode == "nt" else ((tk, tn), (lambda i, j, k: (k, j)), 0)
    if b_lead is None:
        b_spec = pl.BlockSpec(b_blk, b_idx)
    else:
        b_spec = pl.BlockSpec((None,) + b_blk, functools.partial(lambda i, j, k, f, l: (l,) + f(i, j, k), f=b_idx, l=b_lead))
    dims = (((lhs_c,), (rhs_c,)), ((), ()))
    in_specs, args, aliases = [a_spec, b_spec], [a, b], {}
    if out_lead is None:
        out_spec = pl.BlockSpec((tm, tn), lambda i, j, k: (i, j))
        out_shape = jax.ShapeDtypeStruct((M, N), out_dtype)
    else:
        l_out, n_lead, buf = out_lead
        out_spec = pl.BlockSpec((None, tm, tn), functools.partial(lambda i, j, k, l: (l, i, j), l=l_out))
        out_shape = jax.ShapeDtypeStruct((n_lead, M, N), out_dtype)
        if buf is not None:
            in_specs.append(pl.BlockSpec(memory_space=pl.ANY))
            args.append(buf)
            aliases = {2: 0}
    if token is not None:
        in_specs.append(pl.BlockSpec(memory_space=pl.ANY))
        args.append(token)

    def body(a_ref, b_ref, *rest):
        o_ref, acc_ref = rest[-2:]
        k = pl.program_id(2)

        @pl.when(k == 0)
        def _():
            acc_ref[...] = jnp.zeros_like(acc_ref)

        acc_ref[...] += lax.dot_general(a_ref[...].astype(BF16), b_ref[...].astype(BF16), dims,
                                        preferred_element_type=F32)

        @pl.when(k == nk - 1)
        def _():
            o_ref[...] = acc_ref[...].astype(o_ref.dtype)

    return pl.pallas_call(
        body, name=name, grid=(M // tm, N // tn, nk),
        in_specs=in_specs, out_specs=out_spec, out_shape=out_shape,
        scratch_shapes=[pltpu.VMEM((tm, tn), F32)], input_output_aliases=aliases,
        compiler_params=pltpu.CompilerParams(
            dimension_semantics=("parallel", "parallel", "arbitrary"),
            vmem_limit_bytes=_vmem(_nbytes((tm, tk), a.dtype) + _nbytes((tk, tn), b.dtype) + _nbytes((tm, tn), out_dtype),
                                   _nbytes((tm, tn), F32) + _nbytes((tm, tk), BF16) + _nbytes((tk, tn), BF16))),
    )(*args)


def _rows(fn, name, tm, rows, halos=(), fulls=(), outs=(), accs=()):
    n = S // tm
    in_specs, args = [], []
    for arr, w, cb in rows:
        in_specs.append(pl.BlockSpec((tm, w), functools.partial(lambda i, cb: (i, cb), cb=cb)))
        args.append(arr)
    for arr, w, cb, side in halos:
        if side == "prev":
            im = functools.partial(lambda i, cb: (jnp.maximum(i * (tm // 16) - 1, 0), cb), cb=cb)
        else:
            im = functools.partial(lambda i, cb: (jnp.minimum((i + 1) * (tm // 16), S // 16 - 1), cb), cb=cb)
        in_specs.append(pl.BlockSpec((16, w), im))
        args.append(arr)
    for arr in fulls:
        in_specs.append(pl.BlockSpec(arr.shape, functools.partial(lambda i, nd: (0,) * nd, nd=arr.ndim)))
        args.append(arr)
    out_shape, out_specs, aliases, n_alias = [], [], {}, 0
    for k, o in enumerate(outs):
        if len(o) == 3 and o[2] == "T":
            out_shape.append(jax.ShapeDtypeStruct((o[0], S), o[1]))
            out_specs.append(pl.BlockSpec((o[0], tm), lambda i: (0, i)))
        elif len(o) == 3:
            buf, total, cb = o[2]
            out_shape.append(jax.ShapeDtypeStruct((S, total), o[1]))
            out_specs.append(pl.BlockSpec((tm, o[0]), functools.partial(lambda i, cb: (i, cb), cb=cb)))
            if buf is not None:
                aliases[len(args)] = k
                in_specs.append(pl.BlockSpec(memory_space=pl.ANY))
                args.append(buf)
                n_alias += 1
        else:
            out_shape.append(jax.ShapeDtypeStruct((S, o[0]), o[1]))
            out_specs.append(pl.BlockSpec((tm, o[0]), lambda i: (i, 0)))
    for shp in accs:
        out_shape.append(jax.ShapeDtypeStruct(shp, F32))
        out_specs.append(pl.BlockSpec(shp, functools.partial(lambda i, nd: (0,) * nd, nd=len(shp))))
    nr, nh, nf, no, na = len(rows), len(halos), len(fulls), len(outs), len(accs)
    blocks = (sum(_nbytes((tm, w), arr.dtype) for arr, w, _ in rows) + sum(_nbytes(a.shape, a.dtype) for a in fulls)
              + sum(_nbytes((tm, o[0]), o[1]) for o in outs) + sum(_nbytes(shp, F32) for shp in accs))
    widest = _nbytes((tm, max([w for _, w, _ in rows] + [o[0] for o in outs])), F32)

    def body(*refs):
        i = pl.program_id(0)
        ins, orefs = refs[:nr + nh + nf], refs[nr + nh + nf + n_alias:]
        rv = [r[...].astype(F32) for r in ins[:nr]]
        hv = [r[...].astype(F32)[8:] if h[3] == "prev" else r[...].astype(F32)[:8] for r, h in zip(ins[nr:nr + nh], halos)]
        fv = [r[...] for r in ins[nr + nh:]]
        o, a = fn(i, rv, hv, fv)
        assert len(o) == no and len(a) == na, name
        for spec, ref, val in zip(outs, orefs[:no], o):
            ref[...] = (val.T if len(spec) == 3 and spec[2] == "T" else val).astype(ref.dtype)
        if na:
            @pl.when(i == 0)
            def _():
                for ref in orefs[no:]:
                    ref[...] = jnp.zeros_like(ref)

            for ref, val in zip(orefs[no:], a):
                ref[...] += val

    res = pl.pallas_call(
        body, name=name, grid=(n,), in_specs=in_specs, out_specs=out_specs, out_shape=out_shape,
        input_output_aliases=aliases,
        compiler_params=pltpu.CompilerParams(dimension_semantics=("arbitrary",), vmem_limit_bytes=_vmem(blocks, 6 * widest)),
    )(*args)
    return res


def _shift_down(xb, halo, s, row):
    fix = jnp.tile(pltpu.roll(halo, s, 0), (xb.shape[0] // 8, 1))
    return jnp.where(row >= s, pltpu.roll(xb, s, 0), fix)


def _shift_up(xb, halo, s, row):
    tm = xb.shape[0]
    fix = jnp.tile(pltpu.roll(halo, 8 - s, 0), (tm // 8, 1))
    return jnp.where(row < tm - s, pltpu.roll(xb, tm - s, 0), fix)


def _rms(x):
    return lax.rsqrt(jnp.mean(x * x, axis=-1, keepdims=True) + EPS)


def _rms_bwd(dy, x, g):
    r = _rms(x)
    xh = x * r
    dxh = dy * g
    dx = r * (dxh - xh * jnp.mean(dxh * xh, axis=-1, keepdims=True))
    return dx, dy * xh


def _colsum(x):
    return jnp.sum(x, axis=0, keepdims=True)


def _prenorm_fwd(x, g, token=None):
    def fn(i, rv, hv, fv):
        return [rv[0] * _rms(rv[0]) * fv[0]], []
    return _rows(fn, "prenorm_fwd", 256, [(x, D, 0)], fulls=[g] + ([] if token is None else [token]), outs=[(D, BF16)])[0]


def _gm_mask():
    r = lax.broadcasted_iota(jnp.int32, (GM_B, GM_B), 0) // CHUNK
    c = lax.broadcasted_iota(jnp.int32, (GM_B, GM_B), 1) // CHUNK
    return c <= r


def _gm_norm(v, g, b):
    mu = jnp.mean(v, axis=-1, keepdims=True)
    vc = v - mu
    rs = lax.rsqrt(jnp.mean(vc * vc, axis=-1, keepdims=True) + EPS)
    vh = vc * rs
    return vh, rs, vh * g + b


def _gm_sv(vn, ws, bst):
    mask = _gm_mask()
    gw = GM_W // GM_G
    parts = []
    for g in range(GM_G):
        wm = jnp.where(mask, ws[g], 0.0).astype(BF16)
        parts.append(jnp.dot(wm, vn[:, g * gw:(g + 1) * gw].astype(BF16), preferred_element_type=F32)
                     + bst[:, g:g + 1])
    return jnp.concatenate(parts, axis=1)


def _gmlp_fwd(proj, ln_g, ln_b, ws, bst):
    def fn(i, rv, hv, fv):
        u, v, z = rv
        g, b, w, bt = fv
        _, _, vn = _gm_norm(v, g, b)
        return [u * _gm_sv(vn, w, bt) * _silu(z)], []
    return _rows(fn, "gmlp_fwd", GM_B, [(proj, GM_W, 0), (proj, GM_W, 1), (proj, GM_W, 2)],
                 fulls=[ln_g, ln_b, ws, bst], outs=[(GM_W, BF16)])[0]


def _mla_prep_fwd(proj, qg, kvg):
    def fn(i, rv, hv, fv):
        cq, ckv = rv
        g1, g2 = fv
        return [cq * _rms(cq) * g1, ckv * _rms(ckv) * g2], []
    return _rows(fn, "mla_prep_fwd", 256, [(proj, QR, O_CQ // QR), (proj, KVR, O_CKV // KVR)],
                 fulls=[qg, kvg], outs=[(QR, BF16), (KVR, BF16)])


def _rot(t, cc, sa, sb):
    return t * cc + pltpu.roll(t, 32, 1) * sa + pltpu.roll(t, 96, 1) * sb


def _rot_t(g, cc, sa, sb):
    return g * cc + pltpu.roll(g * sa, 96, 1) + pltpu.roll(g * sb, 32, 1)


def _rope_tables():
    pos = jnp.arange(S, dtype=F32)
    inv_freq = ROPE_THETA ** (-jnp.arange(0, ROPE, 2, dtype=F32) / ROPE)
    ang = pos[:, None] * inv_freq[None, :]
    cos, sin, z = jnp.cos(ang), jnp.sin(ang), jnp.zeros((S, 32), F32)
    cc = jnp.concatenate([cos, cos, z, z], axis=1)
    sa = jnp.concatenate([z, sin, z, z], axis=1)
    sb = jnp.concatenate([-sin, z, z, z], axis=1)
    return cc, sa, sb


ATT_SCALE = 1.0 / math.sqrt(NOPE + ROPE)


def _rope_fwd(q, kv, proj, tabs):
    def fn(i, rv, hv, fv):
        qb, kvb, kr, cc, sa, sb = rv
        krr = _rot(kr, cc, sa, sb)
        qs, ks = [], []
        for h in range(H):
            qs += [qb[:, h * HP:h * HP + 128] * ATT_SCALE, _rot(qb[:, h * HP + 128:(h + 1) * HP], cc, sa, sb) * ATT_SCALE]
            ks += [kvb[:, h * 128:(h + 1) * 128], krr]
        kc = jnp.concatenate(ks, axis=1)
        vv = kvb[:, H * NOPE:]
        return [jnp.concatenate(qs, axis=1), kc, kc, vv, vv], []
    cc, sa, sb = tabs
    return _rows(fn, "rope_fwd", 256,
                 [(q, H * HP, 0), (kv, H * 256, 0), (proj, 128, O_KR // 128), (cc, 128, 0), (sa, 128, 0), (sb, 128, 0)],
                 outs=[(H * HP, BF16), (H * HP, BF16), (H * HP, BF16, "T"), (MLA_W, BF16), (MLA_W, BF16, "T")])


TQ, TC, ATT_NB = 512, 512, 1
ATT_KB = TC * ATT_NB
_NT = (((1,), (1,)), ((), ()))


def _attn_allowed(i, kc):
    kpos = kc * TC + lax.broadcasted_iota(jnp.int32, (TC, TQ), 0)
    qpos = i * TQ + lax.broadcasted_iota(jnp.int32, (TC, TQ), 1)
    return (kpos // CHUNK) <= (qpos // CHUNK)


def _attn_fwd(qc, kc, vt):
    def body(q_ref, k_ref, vt_ref, o_ref, l_ref):
        i = pl.program_id(1)
        q = q_ref[...]

        def scores(sb):
            t0s = [pl.multiple_of((sb * ATT_NB + c) * TC, TC) for c in range(ATT_NB)]
            return [lax.dot_general(k_ref[pl.ds(t0, TC), :], q, _NT, preferred_element_type=F32) for t0 in t0s]

        def block(sb, ss, carry, masked):
            m, l, acc = carry
            t0s = [pl.multiple_of((sb * ATT_NB + c) * TC, TC) for c in range(ATT_NB)]
            if masked:
                ss = [jnp.where(_attn_allowed(i, sb * ATT_NB + c), s, -1e30) for c, s in enumerate(ss)]
            m_new = m
            for s in ss:
                m_new = jnp.maximum(m_new, jnp.max(s, axis=0, keepdims=True))
            alpha = jnp.exp(m - m_new)
            ps = [jnp.exp(s - m_new) for s in ss]
            l = alpha * l
            acc = alpha * acc
            for t0, p in zip(t0s, ps):
                l = l + jnp.sum(p, axis=0, keepdims=True)
                acc = acc + jnp.dot(vt_ref[:, pl.ds(t0, TC)], p.astype(BF16), preferred_element_type=F32)
            return m_new, l, acc

        nsb = ((i + 1) * TQ + ATT_KB - 1) // ATT_KB
        c = (jnp.full((1, TQ), -1e30, F32), jnp.zeros((1, TQ), F32), jnp.zeros((VDIM, TQ), F32))

        def step(sb, sc):
            nxt = scores(sb + 1)
            return nxt, block(sb, sc[0], sc[1], False)

        ss, c = lax.fori_loop(0, nsb - 1, step, (scores(0), c))
        m, l, acc = block(nsb - 1, ss, c, True)
        o_ref[...] = (acc / l).T
        l_ref[...] = m + jnp.log(l)

    return pl.pallas_call(
        body, name="attn_fwd", grid=(H, S // TQ),
        in_specs=[pl.BlockSpec((TQ, HP), lambda h, i: (i, h)),
                  pl.BlockSpec((S, HP), lambda h, i: (0, h)),
                  pl.BlockSpec((VDIM, S), lambda h, i: (h, 0))],
        out_specs=[pl.BlockSpec((TQ, VDIM), lambda h, i: (i, h)), pl.BlockSpec((None, 1, TQ), lambda h, i: (h, 0, i))],
        out_shape=[jax.ShapeDtypeStruct((S, MLA_W), F32), jax.ShapeDtypeStruct((H, 1, S), F32)],
        compiler_params=pltpu.CompilerParams(dimension_semantics=("parallel", "arbitrary"),
                                             vmem_limit_bytes=24 * MIB),
    )(qc, kc, vt)


def _gate_mul_fwd(name, val, proj, width, cb):
    def fn(i, rv, hv, fv):
        o, z = rv
        return [o * _silu(z)], []
    return _rows(fn, name, 256, [(val, width, 0), (proj, width, cb)], outs=[(width, BF16)])[0]


def _conv_fwd(proj, w, b):
    def fn(i, rv, hv, fv):
        (xb,), (halo,), (ww, bb) = rv, hv, fv
        halo = jnp.where(i > 0, halo, 0.0)
        row = lax.broadcasted_iota(jnp.int32, xb.shape, 0)
        acc = bb + ww[3:4] * xb
        for s in range(1, CONV_W):
            acc = acc + ww[3 - s:4 - s] * _shift_down(xb, halo, s, row)
        return [acc, acc], []
    return _rows(fn, "conv_fwd", 128, [(proj, LRU_W, O_XC // LRU_W)], halos=[(proj, LRU_W, O_XC // LRU_W, "prev")],
                 fulls=[w, b], outs=[(LRU_W, F32), (LRU_W, BF16)])


def _lru_terms(ga, gx, xc, ba, bx, lam):
    r = _sig(ga + ba)
    ig = _sig(gx + bx)
    sp = jnp.maximum(-lam, 0.0) + jnp.log(1.0 + jnp.exp(-jnp.abs(lam)))
    log_a = -LRU_C * r * sp
    a = jnp.exp(log_a)
    e2 = jnp.exp(2.0 * log_a)
    om = 1.0 - e2
    mult = jnp.sqrt(jnp.maximum(om, 0.0))
    return r, ig, sp, a, e2, om, mult


def _lru_gates_fwd(gates, xc, ba, bx, lam):
    def fn(i, rv, hv, fv):
        ga, gx, x = rv
        r, ig, sp, a, e2, om, mult = _lru_terms(ga, gx, x, *fv)
        return [a, mult * (ig * x)], []
    return _rows(fn, "lru_gates_fwd", 128, [(gates, LRU_W, 0), (gates, LRU_W, 1), (xc, LRU_W, 0)],
                 fulls=[ba, bx, lam], outs=[(LRU_W, F32), (LRU_W, F32)])


SCAN_T, SCAN_CW = 64, 256


def _scan_fwd(a, b):
    def body(a_ref, b_ref, h_ref):
        row = lax.broadcasted_iota(jnp.int32, (SCAN_T, SCAN_CW), 0)

        def step(blk, hc):
            t0 = pl.multiple_of(blk * SCAN_T, SCAN_T)
            A = a_ref[pl.ds(t0, SCAN_T), :]
            B = b_ref[pl.ds(t0, SCAN_T), :]
            d = 1
            while d < SCAN_T:
                keep = row >= d
                A_s = jnp.where(keep, pltpu.roll(A, d, 0), 1.0)
                B_s = jnp.where(keep, pltpu.roll(B, d, 0), 0.0)
                B = A * B_s + B
                A = A * A_s
                d *= 2
            hh = A * hc + B
            h_ref[pl.ds(t0, SCAN_T), :] = hh
            return hh[SCAN_T - 1:SCAN_T, :]

        lax.fori_loop(0, S // SCAN_T, step, jnp.zeros((1, SCAN_CW), F32))

    spec = pl.BlockSpec((S, SCAN_CW), lambda j: (0, j))
    return pl.pallas_call(
        body, name="scan_fwd", grid=(LRU_W // SCAN_CW,), in_specs=[spec, spec], out_specs=spec,
        out_shape=jax.ShapeDtypeStruct((S, LRU_W), F32),
        compiler_params=pltpu.CompilerParams(dimension_semantics=("parallel",),
                                             vmem_limit_bytes=_vmem(3 * _nbytes((S, SCAN_CW), F32))),
    )(a, b)


def _merge_fwd(pa, pb, pc, proj):
    def fn(i, rv, hv, fv):
        a, b, c, ga, gb, gc = rv
        return [_sig(ga) * a + _sig(gb) * b + _sig(gc) * c], []
    return _rows(fn, "merge_fwd", 256,
                 [(pa, D, 0), (pb, D, 0), (pc, D, 0), (proj, D, O_GA // D), (proj, D, O_GB // D), (proj, D, O_GC // D)],
                 outs=[(D, BF16)])[0]


def _post_fwd(x, o2, g):
    def fn(i, rv, hv, fv):
        xb, ob = rv
        return [xb + ob * _rms(ob) * fv[0]], []
    return _rows(fn, "post_fwd", 256, [(x, D, 0), (o2, D, 0)], fulls=[g], outs=[(D, F32)])[0]


SB = 640
BD_TM = 512


def _bd_fwd(xcb, wsb, l):
    def body(x_ref, w_ref, o_ref):
        o_ref[...] = jnp.dot(x_ref[...], w_ref[...], preferred_element_type=F32).astype(o_ref.dtype)

    return pl.pallas_call(
        body, name="lru_gate_mm", grid=(S // BD_TM, 4),
        in_specs=[pl.BlockSpec((BD_TM, SB), lambda i, q: (i, q % 2)),
                  pl.BlockSpec((None, None, SB, SB), lambda i, q: (l, q, 0, 0))],
        out_specs=pl.BlockSpec((BD_TM, SB), lambda i, q: (i, q)),
        out_shape=jax.ShapeDtypeStruct((S, 2 * LRU_W), BF16),
        compiler_params=pltpu.CompilerParams(dimension_semantics=("parallel", "parallel"), vmem_limit_bytes=VMEM_LIMIT),
    )(xcb, wsb)


def _bd_dx(dgates, wsb, l):
    def body(d_ref, w_ref, o_ref, acc_ref):
        g = pl.program_id(2)

        @pl.when(g == 0)
        def _():
            acc_ref[...] = jnp.zeros_like(acc_ref)

        acc_ref[...] += lax.dot_general(d_ref[...], w_ref[...], (((1,), (1,)), ((), ())), preferred_element_type=F32)

        @pl.when(g == 1)
        def _():
            o_ref[...] = acc_ref[...].astype(o_ref.dtype)

    return pl.pallas_call(
        body, name="lru_gate_dx", grid=(S // BD_TM, 2, 2),
        in_specs=[pl.BlockSpec((BD_TM, SB), lambda i, s, g: (i, 2 * g + s)),
                  pl.BlockSpec((None, None, SB, SB), lambda i, s, g: (l, 2 * g + s, 0, 0))],
        out_specs=pl.BlockSpec((BD_TM, SB), lambda i, s, g: (i, s)),
        out_shape=jax.ShapeDtypeStruct((S, LRU_W), BF16),
        scratch_shapes=[pltpu.VMEM((BD_TM, SB), F32)],
        compiler_params=pltpu.CompilerParams(dimension_semantics=("parallel", "parallel", "arbitrary"),
                                             vmem_limit_bytes=VMEM_LIMIT),
    )(dgates, wsb)


def _bd_dw(xcb, dgates):
    tk = 1024

    def body(x_ref, d_ref, o_ref):
        @pl.when(pl.program_id(1) == 0)
        def _():
            o_ref[...] = jnp.zeros_like(o_ref)

        o_ref[...] += lax.dot_general(x_ref[...], d_ref[...], (((0,), (0,)), ((), ())), preferred_element_type=F32)

    return pl.pallas_call(
        body, name="lru_gate_dw", grid=(4, S // tk),
        in_specs=[pl.BlockSpec((tk, SB), lambda q, k: (k, q % 2)), pl.BlockSpec((tk, SB), lambda q, k: (k, q))],
        out_specs=pl.BlockSpec((None, SB, SB), lambda q, k: (q, 0, 0)),
        out_shape=jax.ShapeDtypeStruct((4, SB, SB), F32),
        compiler_params=pltpu.CompilerParams(dimension_semantics=("parallel", "arbitrary"), vmem_limit_bytes=VMEM_LIMIT),
    )(xcb, dgates)


def _bd_extract(dwsb):
    def body(w_ref, o_ref):
        lane = lax.broadcasted_iota(jnp.int32, (LRU_BW, 128), 1)
        for q in range(4):
            for kk in range(8):
                c0 = LRU_BW * kk
                w0, off = (c0 // 128) * 128, c0 % 128
                rows = pl.ds(LRU_BW * kk, LRU_BW)
                blk = w_ref[q, rows, w0:w0 + 128]
                if off:
                    blk = pltpu.roll(blk, 128 - off, 1)
                    if off + LRU_BW > 128:
                        nxt = pltpu.roll(w_ref[q, rows, w0 + 128:w0 + 256], 128 - off, 1)
                        blk = jnp.where(lane < 128 - off, blk, nxt)
                o_ref[q // 2, 8 * (q % 2) + kk] = blk.astype(BF16)

    return pl.pallas_call(
        body, name="lru_gate_dw_blocks",
        in_specs=[pl.BlockSpec(memory_space=pltpu.VMEM)], out_specs=pl.BlockSpec(memory_space=pltpu.VMEM),
        out_shape=jax.ShapeDtypeStruct((2, LRU_NB, LRU_BW, 128), BF16),
        compiler_params=pltpu.CompilerParams(vmem_limit_bytes=VMEM_LIMIT),
    )(dwsb)


def _layer_fwd(x, P, l, tabs, token=None, late=None):
    A = {"x": x}
    A["h"] = _prenorm_fwd(x, P["pre_g"], token)
    proj = A["proj"] = _mm(A["h"], P["wp"], "nt", "in_proj", out_dtype=BF16, tm=1024)
    A["ya"] = _gmlp_fwd(proj, P["ln_g"], P["ln_b"], P["ws"], P["bst"])
    A["xc"], A["xcb"] = _conv_fwd(proj, P["conv_w"], P["conv_b"])
    A["gates"] = _bd_fwd(A["xcb"], P["wsb"], l)
    A["a"], bterm = _lru_gates_fwd(A["gates"], A["xc"], P["ba"], P["bx"], P["lam"])
    A["hs"] = _scan_fwd(A["a"], bterm)
    A["yc"] = _gate_mul_fwd("yc_fwd", A["hs"], proj, LRU_W, O_ZC // LRU_W)
    if late is not None:
        P.update(late(A["yc"]))
    A["cqn"], A["ckvn"] = _mla_prep_fwd(proj, P["qg"], P["kvg"])
    q = _mm(A["cqn"], P["wuq"], "nt", "q_up", out_dtype=BF16)
    kv = _mm(A["ckvn"], P["wukv"], "nt", "kv_up", out_dtype=BF16)
    A["qc"], A["kc"], A["kct"], A["vv"], vt = _rope_fwd(q, kv, proj, tabs)
    A["o"], A["lse"] = _attn_fwd(A["qc"], A["kc"], vt)
    A["yb"] = _gate_mul_fwd("yb_fwd", A["o"], proj, MLA_W, O_ZB // MLA_W)
    A["pa"] = _mm(A["ya"], P["wpa"], "nn", "proj_a", out_dtype=BF16)
    A["pb"] = _mm(A["yb"], P["wpb"], "nn", "proj_b", out_dtype=BF16)
    A["pc"] = _mm(A["yc"], P["wpc"], "nn", "proj_c", out_dtype=BF16)
    A["merged"] = _merge_fwd(A["pa"], A["pb"], A["pc"], proj)
    A["o2"] = _mm(A["merged"], P["wout"], "nn", "out_proj")
    return _post_fwd(x, A["o2"], P["post_g"]), A


def _loss_fwd(y, tgt):
    def fn(i, rv, hv, fv):
        yb, tb = rv
        e = yb - tb
        part = 0.5 * jnp.sum(jnp.mean(e * e, axis=-1, keepdims=True), axis=0, keepdims=True)
        return [e * (1.0 / D)], [part]
    return _rows(fn, "loss", 256, [(y, D, 0), (tgt, D, 0)], outs=[(D, F32)], accs=[(1, 1)])


def _post_bwd(dxn, o2, g, token=None):
    def fn(i, rv, hv, fv):
        dy, ob = rv
        dx, dg = _rms_bwd(dy, ob, fv[0])
        return [dx], [_colsum(dg)]
    return _rows(fn, "post_bwd", 256, [(dxn, D, 0), (o2, D, 0)], fulls=[g] + ([] if token is None else [token]),
                 outs=[(D, BF16)], accs=[(1, D)])


def _merge_bwd(dm, pa, pb, pc, proj, dproj):
    def fn(i, rv, hv, fv):
        d, a, b, c, ga, gb, gc = rv
        outs_p, outs_g = [], []
        for p, gg in ((a, ga), (b, gb), (c, gc)):
            s = _sig(gg)
            outs_p.append(d * s)
            outs_g.append(d * p * s * (1.0 - s))
        return outs_p + [jnp.concatenate(outs_g, axis=1)], []
    return _rows(fn, "merge_bwd", 128,
                 [(dm, D, 0), (pa, D, 0), (pb, D, 0), (pc, D, 0),
                  (proj, D, O_GA // D), (proj, D, O_GB // D), (proj, D, O_GC // D)],
                 outs=[(D, BF16)] * 3 + [(3 * D, BF16, (dproj, NP, O_GA // (3 * D)))])


def _gmlp_bwd(dya, proj, ln_g, ln_b, ws, bst, dproj):
    gw = GM_W // GM_G

    def fn(i, rv, hv, fv):
        dy, u, v, z = rv
        g, b, w, bt = fv
        vh, rs, vn = _gm_norm(v, g, b)
        sv = _gm_sv(vn, w, bt)
        sz = _silu(z)
        du = dy * sv * sz
        dsv = dy * u * sz
        dz = dy * u * sv * _dsilu(z)
        mask = _gm_mask()
        lane = lax.broadcasted_iota(jnp.int32, (GM_B, 128), 1)
        dvn_parts, dws, dbst = [], [], jnp.zeros((GM_B, 128), F32)
        for k in range(GM_G):
            wm = jnp.where(mask, w[k], 0.0).astype(BF16)
            dsk = dsv[:, k * gw:(k + 1) * gw]
            dskb = dsk.astype(BF16)
            dvn_parts.append(lax.dot_general(wm, dskb, (((0,), (0,)), ((), ())), preferred_element_type=F32))
            dwk = lax.dot_general(dskb, vn[:, k * gw:(k + 1) * gw].astype(BF16), (((1,), (1,)), ((), ())),
                                  preferred_element_type=F32)
            dws.append(jnp.where(mask, dwk, 0.0)[None])
            dbst = dbst + jnp.where(lane == k, jnp.sum(dsk, axis=1, keepdims=True), 0.0)
        dvn = jnp.concatenate(dvn_parts, axis=1)
        dvh = dvn * g
        dv = rs * (dvh - jnp.mean(dvh, axis=-1, keepdims=True) - vh * jnp.mean(dvh * vh, axis=-1, keepdims=True))
        return ([jnp.concatenate([du, dv, dz], axis=1)],
                [jnp.concatenate(dws, axis=0), dbst, _colsum(dvn * vh), _colsum(dvn)])
    return _rows(fn, "gmlp_bwd", GM_B, [(dya, GM_W, 0), (proj, GM_W, 0), (proj, GM_W, 1), (proj, GM_W, 2)],
                 fulls=[ln_g, ln_b, ws, bst], outs=[(3 * GM_W, BF16, (dproj, NP, O_U // (3 * GM_W)))],
                 accs=[(GM_G, GM_B, GM_B), (GM_B, 128), (1, GM_W), (1, GM_W)])


def _yb_bwd(dyb, o, proj, dproj):
    def fn(i, rv, hv, fv):
        dy, ob, z = rv
        do = dy * _silu(z)
        prod = do * ob
        lane = lax.broadcasted_iota(jnp.int32, (dy.shape[0], 128), 1)
        dl = jnp.zeros((dy.shape[0], 128), F32)
        for h in range(H):
            dl = dl + jnp.where(lane == h, jnp.sum(prod[:, h * VDIM:(h + 1) * VDIM], axis=1, keepdims=True), 0.0)
        return [do, dl, dy * ob * _dsilu(z)], []
    return _rows(fn, "yb_bwd", 256, [(dyb, MLA_W, 0), (o, MLA_W, 0), (proj, MLA_W, O_ZB // MLA_W)],
                 outs=[(MLA_W, BF16), (128, F32, "T"), (MLA_W, BF16, (dproj, NP, O_ZB // MLA_W))])


def _attn_bwd(qc, kc, kct, vv, do, lse, dlt):
    def body(q_ref, k_ref, kt_ref, v_ref, do_ref, l_ref, d_ref, dq_ref, dk_ref, dv_ref, dqt_ref):
        h, i = pl.program_id(0), pl.program_id(1)

        @pl.when(i == 0)
        def _():
            dk_ref[...] = jnp.zeros_like(dk_ref)
            dv_ref[...] = jnp.zeros_like(dv_ref)

        q = q_ref[...]
        dob = do_ref[...]
        lse = l_ref[...]
        dl = d_ref[pl.ds(h, 1), :]
        dqt_ref[...] = jnp.zeros_like(dqt_ref)

        def rows_of(sb, c):
            return pl.ds(pl.multiple_of((sb * ATT_NB + c) * TC, TC), TC)

        def front(sb):
            return [(lax.dot_general(k_ref[rows_of(sb, c), :], q, _NT, preferred_element_type=F32),
                     lax.dot_general(v_ref[rows_of(sb, c), :], dob, _NT, preferred_element_type=F32))
                    for c in range(ATT_NB)]

        def block(sb, sd, masked):
            dqt = None
            for c, (s, dp) in enumerate(sd):
                rows = rows_of(sb, c)
                p = jnp.exp(s - lse)
                if masked:
                    p = jnp.where(_attn_allowed(i, sb * ATT_NB + c), p, 0.0)
                ds = (p * (dp - dl)).astype(BF16)
                dk_ref[rows, :] += jnp.dot(ds, q, preferred_element_type=F32)
                dv_ref[rows, :] += jnp.dot(p.astype(BF16), dob, preferred_element_type=F32)
                part = jnp.dot(kt_ref[:, rows], ds, preferred_element_type=F32)
                dqt = part if dqt is None else dqt + part
            dqt_ref[...] += dqt

        def step(sb, sd):
            nxt = front(sb + 1)
            block(sb, sd, False)
            return nxt

        nsb = ((i + 1) * TQ + ATT_KB - 1) // ATT_KB
        sd = lax.fori_loop(0, nsb - 1, step, front(0))
        block(nsb - 1, sd, True)
        dq_ref[...] = dqt_ref[...].T.astype(dq_ref.dtype)

    blk = lambda w: pl.BlockSpec((TQ, w), lambda h, i: (i, h))
    head = lambda w: pl.BlockSpec((S, w), lambda h, i: (0, h))
    return pl.pallas_call(
        body, name="attn_bwd", grid=(H, S // TQ),
        in_specs=[blk(HP), head(HP), pl.BlockSpec((HP, S), lambda h, i: (h, 0)), head(VDIM), blk(VDIM),
                  pl.BlockSpec((None, 1, TQ), lambda h, i: (h, 0, i)), pl.BlockSpec((8, TQ), lambda h, i: (0, i))],
        out_specs=[blk(HP), head(HP), head(VDIM)],
        out_shape=[jax.ShapeDtypeStruct((S, H * HP), BF16), jax.ShapeDtypeStruct((S, H * HP), F32),
                   jax.ShapeDtypeStruct((S, MLA_W), F32)],
        scratch_shapes=[pltpu.VMEM((HP, TQ), F32)],
        compiler_params=pltpu.CompilerParams(dimension_semantics=("parallel", "arbitrary"),
                                             vmem_limit_bytes=28 * MIB),
    )(qc, kc, kct, vv, do, lse, dlt)


def _rope_bwd(dqc, dkc, dvv, tabs):
    def fn(i, rv, hv, fv):
        dq, dk, dv, cc, sa, sb = rv
        qs, ks = [], []
        dkr = jnp.zeros((dq.shape[0], 128), F32)
        for h in range(H):
            qs += [dq[:, h * HP:h * HP + 128] * ATT_SCALE, _rot_t(dq[:, h * HP + 128:(h + 1) * HP], cc, sa, sb) * ATT_SCALE]
            ks.append(dk[:, h * HP:h * HP + 128])
            dkr = dkr + dk[:, h * HP + 128:(h + 1) * HP]
        return [jnp.concatenate(qs, axis=1), jnp.concatenate(ks + [dv], axis=1), _rot_t(dkr, cc, sa, sb)], []
    cc, sa, sb = tabs
    return _rows(fn, "rope_bwd", 256,
                 [(dqc, H * HP, 0), (dkc, H * HP, 0), (dvv, MLA_W, 0), (cc, 128, 0), (sa, 128, 0), (sb, 128, 0)],
                 outs=[(H * HP, BF16), (H * 256, BF16), (128, BF16)])


MLA_GROUP = 1536


def _mla_prep_bwd(dcqn, dckvn, dkr, proj, qg, kvg, dproj):
    def fn(i, rv, hv, fv):
        d1, d2, dk, cq, ckv = rv
        g1, g2 = fv
        dx1, dg1 = _rms_bwd(d1, cq, g1)
        dx2, dg2 = _rms_bwd(d2, ckv, g2)
        zeros = jnp.zeros((d1.shape[0], MLA_GROUP - KVR - 128 - QR), F32)
        return [jnp.concatenate([dx2, dk.astype(F32), dx1, zeros], axis=1)], [_colsum(dg1), _colsum(dg2)]
    return _rows(fn, "mla_prep_bwd", 256,
                 [(dcqn, QR, 0), (dckvn, KVR, 0), (dkr, 128, 0), (proj, QR, O_CQ // QR), (proj, KVR, O_CKV // KVR)],
                 fulls=[qg, kvg], outs=[(MLA_GROUP, BF16, (dproj, NP, O_CKV // MLA_GROUP))], accs=[(1, QR), (1, KVR)])


def _yc_bwd(dyc, hs, proj, dproj):
    def fn(i, rv, hv, fv):
        dy, hh, z = rv
        return [dy * _silu(z), dy * hh * _dsilu(z)], []
    return _rows(fn, "yc_bwd", 128, [(dyc, LRU_W, 0), (hs, LRU_W, 0), (proj, LRU_W, O_ZC // LRU_W)],
                 outs=[(LRU_W, F32), (LRU_W, BF16, (dproj, NP, O_ZC // LRU_W))])


def _scan_bwd(a, hs, dh):
    nblk = S // SCAN_T

    def body(a_ref, h_ref, dh_ref, da_ref, db_ref):
        row = lax.broadcasted_iota(jnp.int32, (SCAN_T, SCAN_CW), 0)

        def step(j, carry):
            gc, ac = carry
            blk = nblk - 1 - j
            t0 = pl.multiple_of(blk * SCAN_T, SCAN_T)
            av = a_ref[pl.ds(t0, SCAN_T), :]
            A = jnp.where(row < SCAN_T - 1, pltpu.roll(av, SCAN_T - 1, 0), ac)
            B = dh_ref[pl.ds(t0, SCAN_T), :]
            d = 1
            while d < SCAN_T:
                keep = row < SCAN_T - d
                A_s = jnp.where(keep, pltpu.roll(A, SCAN_T - d, 0), 1.0)
                B_s = jnp.where(keep, pltpu.roll(B, SCAN_T - d, 0), 0.0)
                B = A * B_s + B
                A = A * A_s
                d *= 2
            g = A * gc + B
            p0 = pl.multiple_of(jnp.maximum(t0 - 8, 0), 8)
            last = jnp.where(blk > 0, h_ref[pl.ds(p0, 8), :][7:8, :], 0.0)
            h_prev = jnp.where(row >= 1, pltpu.roll(h_ref[pl.ds(t0, SCAN_T), :], 1, 0), last)
            da_ref[pl.ds(t0, SCAN_T), :] = g * h_prev
            db_ref[pl.ds(t0, SCAN_T), :] = g
            return g[0:1, :], av[0:1, :]

        z = jnp.zeros((1, SCAN_CW), F32)
        lax.fori_loop(0, nblk, step, (z, z))

    spec = pl.BlockSpec((S, SCAN_CW), lambda j: (0, j))
    return pl.pallas_call(
        body, name="scan_bwd", grid=(LRU_W // SCAN_CW,), in_specs=[spec] * 3, out_specs=[spec] * 2,
        out_shape=[jax.ShapeDtypeStruct((S, LRU_W), F32)] * 2,
        compiler_params=pltpu.CompilerParams(dimension_semantics=("parallel",),
                                             vmem_limit_bytes=_vmem(5 * _nbytes((S, SCAN_CW), F32))),
    )(a, hs, dh)


def _lru_gates_bwd(da, db, gates, xc, ba, bx, lam):
    def fn(i, rv, hv, fv):
        dav, dbv, ga, gx, x = rv
        bav, bxv, lamv = fv
        r, ig, sp, a, e2, om, mult = _lru_terms(ga, gx, x, bav, bxv, lamv)
        dmult = dbv * ig * x
        dig = dbv * mult * x
        dxc1 = dbv * mult * ig
        dlog_a = dav * a + jnp.where(om > 0.0, dmult * (-e2 / mult), 0.0)
        dr = dlog_a * (-LRU_C * sp)
        dga = dr * r * (1.0 - r)
        dgx = dig * ig * (1.0 - ig)
        dlam = _colsum(dlog_a * (-LRU_C * r)) * (-_sig(-lamv))
        return [jnp.concatenate([dga, dgx], axis=1), dxc1], [_colsum(dga), _colsum(dgx), dlam]
    return _rows(fn, "lru_gates_bwd", 128,
                 [(da, LRU_W, 0), (db, LRU_W, 0), (gates, LRU_W, 0), (gates, LRU_W, 1), (xc, LRU_W, 0)],
                 fulls=[ba, bx, lam], outs=[(2 * LRU_W, BF16), (LRU_W, F32)], accs=[(1, LRU_W)] * 3)


def _conv_bwd(dxc1, dxc2, proj, w, dproj):
    cb = O_XC // LRU_W

    def fn(i, rv, hv, fv):
        d1, d2, xb = rv
        n1, n2, xprev = hv
        ww = fv[0]
        last = i == S // 128 - 1
        dxc = d1 + d2
        nxt = jnp.where(last, 0.0, n1 + n2)
        xprev = jnp.where(i > 0, xprev, 0.0)
        row = lax.broadcasted_iota(jnp.int32, xb.shape, 0)
        dx = ww[3:4] * dxc
        dws = [None] * CONV_W
        dws[3] = _colsum(dxc * xb)
        for s in range(1, CONV_W):
            dx = dx + ww[3 - s:4 - s] * _shift_up(dxc, nxt, s, row)
            dws[3 - s] = _colsum(dxc * _shift_down(xb, xprev, s, row))
        return [dx], [jnp.concatenate(dws, axis=0), _colsum(dxc)]
    return _rows(fn, "conv_bwd", 128, [(dxc1, LRU_W, 0), (dxc2, LRU_W, 0), (proj, LRU_W, cb)],
                 halos=[(dxc1, LRU_W, 0, "next"), (dxc2, LRU_W, 0, "next"), (proj, LRU_W, cb, "prev")],
                 fulls=[w], outs=[(LRU_W, BF16, (dproj, NP, cb))], accs=[(CONV_W, LRU_W), (1, LRU_W)])


def _prenorm_bwd(dxn, dh, x, g):
    def fn(i, rv, hv, fv):
        dy, dhh, xb = rv
        dx, dg = _rms_bwd(dhh, xb, fv[0])
        return [dy + dx], [_colsum(dg)]
    return _rows(fn, "prenorm_bwd", 256, [(dxn, D, 0), (dh, D, 0), (x, D, 0)], fulls=[g], outs=[(D, F32)],
                 accs=[(1, D)])


def _layer_bwd(dxn, A, P, l, tabs, token=None, early=None):
    G, GB = {}, {}
    proj = A["proj"]

    def dw(key, a, b, name, **tiles):
        GB[key] = _mm(a, b, "tn", name, out_dtype=BF16, **tiles)

    do2, G["post_g"] = _post_bwd(dxn, A["o2"], P["post_g"], token)
    dm = _mm(do2, P["wout"], "nt", "out_proj_dx", out_dtype=BF16)
    dw("wout", A["merged"], do2, "out_proj_dw")
    dpa, dpb, dpc, dproj = _merge_bwd(dm, A["pa"], A["pb"], A["pc"], proj, None)
    dya = _mm(dpa, P["wpa"], "nt", "proj_a_dx", out_dtype=BF16)
    dw("wpa", A["ya"], dpa, "proj_a_dw")
    dyb = _mm(dpb, P["wpb"], "nt", "proj_b_dx", out_dtype=BF16)
    dw("wpb", A["yb"], dpb, "proj_b_dw")
    dyc = _mm(dpc, P["wpc"], "nt", "proj_c_dx", out_dtype=BF16)
    dw("wpc", A["yc"], dpc, "proj_c_dw")
    dproj, G["ws"], G["bst"], G["ln_g"], G["ln_b"] = _gmlp_bwd(dya, proj, P["ln_g"], P["ln_b"], P["ws"], P["bst"], dproj)
    do, dl, dproj = _yb_bwd(dyb, A["o"], proj, dproj)
    dqc, dkc, dvv = _attn_bwd(A["qc"], A["kc"], A["kct"], A["vv"], do, A["lse"], dl)
    dq, dkv, dkr = _rope_bwd(dqc, dkc, dvv, tabs)
    dcqn = _mm(dq, P["wuq"], "nn", "q_up_dx", out_dtype=BF16)
    dw("wuq", dq, A["cqn"], "q_up_dw")
    dckvn = _mm(dkv, P["wukv"], "nn", "kv_up_dx", out_dtype=BF16)
    dw("wukv", dkv, A["ckvn"], "kv_up_dw")
    dproj, G["qg"], G["kvg"] = _mla_prep_bwd(dcqn, dckvn, dkr, proj, P["qg"], P["kvg"], dproj)
    dhs, dproj = _yc_bwd(dyc, A["hs"], proj, dproj)
    da, db = _scan_bwd(A["a"], A["hs"], dhs)
    dgates, dxc1, G["ba"], G["bx"], G["lam"] = _lru_gates_bwd(da, db, A["gates"], A["xc"], P["ba"], P["bx"], P["lam"])
    dxc2 = _bd_dx(dgates, P["wsb"], l)
    G["wab"] = _bd_extract(_bd_dw(A["xcb"], dgates))
    dproj, G["conv_w"], G["conv_b"] = _conv_bwd(dxc1, dxc2, proj, P["conv_w"], dproj)
    tok = (None, None) if early is None else early(GB)
    dh = _mm(dproj, P["wp"], "nn", "in_proj_dx", tm=1024, tn=1024, token=tok[0])
    dw("wp", dproj, A["h"], "in_proj_dw", tm=1536, tn=1024, token=tok[1])
    dx, G["pre_g"] = _prenorm_bwd(dxn, dh, A["x"], P["pre_g"])
    return dx, G, GB


_ORIG_OFF = [0]
for _s in IN_SIZES:
    _ORIG_OFF.append(_ORIG_OFF[-1] + _s)
_PAD_OFF = {0: O_U, 1: O_V, 2: O_ZA, 3: O_CQ, 4: O_CKV, 5: O_KR, 6: O_ZB, 7: O_XC, 8: O_ZC, 9: O_GA, 10: O_GB, 11: O_GC}
SHARD_IN = N_IN // N_CHIPS


def _pieces_w_in(j):
    lo, hi = SHARD_IN * j, SHARD_IN * (j + 1)
    out = []
    for k in range(len(IN_SIZES)):
        a, b = max(lo, _ORIG_OFF[k]), min(hi, _ORIG_OFF[k + 1])
        if a < b:
            out.append((a - lo, _PAD_OFF[k] + a - _ORIG_OFF[k], b - a))
    return out


def _pieces_uq(j):
    return [(192 * hh, HP * (2 * j + hh), NOPE + ROPE) for hh in range(2)]


def _pieces_ukv(j):
    out = []
    for hh in range(2):
        h = 2 * j + hh
        out += [(256 * hh, NOPE * h, NOPE), (256 * hh + NOPE, H * NOPE + VDIM * h, VDIM)]
    return out


def _pieces_rows(r):
    return lambda j: [(0, r * j, r)]


LAYOUT = {
    "w_in": (SHARD_IN, NP, _pieces_w_in),
    "mla_w_uq": (2 * (NOPE + ROPE), H * HP, _pieces_uq),
    "mla_w_ukv": (2 * (NOPE + VDIM), 2 * H * 128, _pieces_ukv),
    "lru_conv_w": (1, N_CHIPS, _pieces_rows(1)),
    "w_proj_a": (GM_W // N_CHIPS, GM_W, _pieces_rows(GM_W // N_CHIPS)),
    "w_proj_b": (MLA_W // N_CHIPS, MLA_W, _pieces_rows(MLA_W // N_CHIPS)),
    "w_proj_c": (LRU_W // N_CHIPS, LRU_W, _pieces_rows(LRU_W // N_CHIPS)),
    "w_out": (D // N_CHIPS, D, _pieces_rows(D // N_CHIPS)),
}
TRANSPOSED = ("w_in", "mla_w_uq", "mla_w_ukv")


def _superblocks(w_a, w_x):
    w6 = jnp.stack([w_a, w_x], axis=1).reshape(DEPTH, 4, 8, LRU_BW, LRU_BW).astype(BF16)
    bands = [jnp.pad(w6[:, :, k], ((0, 0), (0, 0), (0, 0), (LRU_BW * k, SB - LRU_BW * (k + 1)))) for k in range(8)]
    return jnp.concatenate(bands, axis=2)


_HBM = pl.BlockSpec(memory_space=pltpu.HBM)


def _position():
    return lax.axis_index("x"), lax.axis_index("y"), lax.axis_index("c")


def _allgather(blocks, name):
    n = len(blocks)

    def body(*refs):
        ins, outs = refs[:n], refs[n:2 * n]
        send, recv, lsem = refs[2 * n:]
        x, y, c = _position()
        me, sib = (x, y, c), (x, y, 1 - c)
        chips = [(1 - x, y), (x, 1 - y), (1 - x, 1 - y)]

        def cp(k, a, block, to, src=None):
            dst = outs[a].at[4 * block[0] + 2 * block[1] + block[2]]
            return pltpu.make_async_remote_copy(src_ref=dst if src is None else src, dst_ref=dst,
                                                send_sem=send.at[7 * a + k], recv_sem=recv.at[7 * a + k],
                                                device_id=to, device_id_type=MESH)

        mine = [pltpu.make_async_copy(ins[a], outs[a].at[4 * x + 2 * y + c], lsem.at[a]) for a in range(n)]
        for m in mine:
            m.start()
        first = []
        for a in range(n):
            first.append(cp(0, a, me, sib, src=ins[a]))
            first += [cp(1 + j, a, me, (*chip, c), src=ins[a]) for j, chip in enumerate(chips)]
        for f in first:
            f.start()
        passed = []
        for j, chip in enumerate(chips):
            for a in range(n):
                cp(1 + j, a, (*chip, c), me).wait_recv()
                p = cp(4 + j, a, (*chip, c), sib)
                p.start()
                passed.append(p)
        for a in range(n):
            cp(0, a, sib, me).wait_recv()
            for j, chip in enumerate(chips):
                cp(4 + j, a, (*chip, 1 - c), me).wait_recv()
        for f in first + passed:
            f.wait_send()
        for m in mine:
            m.wait()

    return pl.pallas_call(
        body, name=name,
        out_shape=[jax.ShapeDtypeStruct((8,) + b.shape, b.dtype) for b in blocks],
        in_specs=[_HBM] * n, out_specs=[_HBM] * n,
        scratch_shapes=[pltpu.SemaphoreType.DMA((7 * n,)), pltpu.SemaphoreType.DMA((7 * n,)),
                        pltpu.SemaphoreType.DMA((n,))],
    )(*blocks)


_REL = (2, 1, 3)


def _cut(r):
    return r if r < 32 else (r // 2 + 15) // 16 * 16


def _half_rows(r, c0):
    return _cut(r) if c0 == 0 else r - _cut(r)


def _half_pieces(lay_a, jsrc, c0):
    r = lay_a[0]
    lo, hi = (0, _cut(r)) if c0 == 0 else (_cut(r), r)
    out = []
    for s0, d0, nr in lay_a[2](jsrc):
        a, b = max(s0, lo), min(s0 + nr, hi)
        if a < b:
            out.append((a, d0 + a - s0, b - a))
    return out


def _gather_zeros(names, srcs):
    return [jnp.zeros((LAYOUT[nm][1],) + s.shape[1:], s.dtype) for nm, s in zip(names, srcs)]


def _weights_allgather(names, srcs, name, carry=()):
    n = len(srcs)
    lay = [LAYOUT[nm] for nm in names]
    zeros = _gather_zeros(names, srcs)
    m = len(carry)

    def body(*refs):
        ins, outs = refs[:n], refs[2 * n + m:3 * n + m]
        send, recv, lsem = refs[3 * n + 2 * m:]
        x, y, c = _position()
        j = 2 * x + y
        sib = (x, y, 1 - c)
        chips = [(1 - x, y), (x, 1 - y), (1 - x, 1 - y)]

        def flow(a, k, jsrc, c0, to, from_src):
            cps = []
            for s0, d0, nr in _half_pieces(lay[a], jsrc, c0):
                dst = outs[a].at[pl.ds(d0, nr)]
                src = ins[a].at[pl.ds(s0, nr)] if from_src else dst
                cps.append(pltpu.make_async_remote_copy(src_ref=src, dst_ref=dst, send_sem=send.at[7 * a + k],
                                                        recv_sem=recv.at[7 * a + k], device_id=to, device_id_type=MESH))
            return cps

        def sized(a, k, rows):
            ref = ins[a].at[pl.ds(0, rows)]
            return pltpu.make_async_remote_copy(src_ref=ref, dst_ref=ref, send_sem=send.at[7 * a + k],
                                                recv_sem=recv.at[7 * a + k], device_id=sib, device_id_type=MESH)

        for j0 in range(N_CHIPS):
            for c0 in range(2):
                @pl.when((j == j0) & (c == c0))
                def _(j0=j0, c0=c0):
                    mine = [_half_rows(lay[a][0], c0) for a in range(n)]
                    theirs = [_half_rows(lay[a][0], 1 - c0) for a in range(n)]
                    for a in range(n):
                        for s0, d0, nr in _half_pieces(lay[a], j0, c0):
                            pltpu.make_async_copy(ins[a].at[pl.ds(s0, nr)], outs[a].at[pl.ds(d0, nr)], lsem.at[a]).start()
                    for a in range(n):
                        for cp in flow(a, 0, j0, c0, sib, True):
                            cp.start()
                        for k, chip in enumerate(chips):
                            for cp in flow(a, 1 + k, j0, c0, (*chip, c), True):
                                cp.start()
                    for k in range(3):
                        for a in range(n):
                            if mine[a]:
                                sized(a, 1 + k, mine[a]).wait_recv()
                                for cp in flow(a, 4 + k, j0 ^ _REL[k], c0, sib, False):
                                    cp.start()
                    for a in range(n):
                        if theirs[a]:
                            sized(a, 0, theirs[a]).wait_recv()
                            for k in range(3):
                                sized(a, 4 + k, theirs[a]).wait_recv()
                    for a in range(n):
                        if mine[a]:
                            for k in range(7):
                                sized(a, k, mine[a]).wait_send()
                            ref = ins[a].at[pl.ds(0, mine[a])]
                            pltpu.make_async_copy(ref, ref, lsem.at[a]).wait()

    res = pl.pallas_call(
        body, name=name,
        out_shape=[jax.ShapeDtypeStruct(z.shape, z.dtype) for z in list(zeros) + list(carry)],
        in_specs=[_HBM] * (2 * n + m), out_specs=[_HBM] * (n + m),
        input_output_aliases={n + a: a for a in range(n + m)},
        scratch_shapes=[pltpu.SemaphoreType.DMA((7 * n,)), pltpu.SemaphoreType.DMA((7 * n,)),
                        pltpu.SemaphoreType.DMA((n,))],
    )(*srcs, *zeros, *carry)
    return res[:n], res[n:]


_SEM = pl.BlockSpec(memory_space=pltpu.SEMAPHORE)
_VMEM_TOKEN = pl.BlockSpec(memory_space=pltpu.VMEM)
_TOKEN = jax.ShapeDtypeStruct((8, 128), F32)
_EFFECT = pltpu.SideEffectType.DATAFLOW_SIDE_EFFECTING


def _gather_start(names, srcs, zeros, name, after=None):
    n = len(srcs)
    lay = [LAYOUT[nm] for nm in names]
    extra = [] if after is None else [after]

    def body(*refs):
        ins, lands = refs[:n], refs[n:2 * n]
        send, recv, lsem = refs[2 * n + len(extra):2 * n + len(extra) + 3]
        refs[-1][...] = jnp.zeros_like(refs[-1])
        x, y, c = _position()
        j = 2 * x + y
        chips = [(1 - x, y), (x, 1 - y), (1 - x, 1 - y)]
        for j0 in range(N_CHIPS):
            @pl.when(j == j0)
            def _(j0=j0):
                for a in range(n):
                    for s0, d0, nr in lay[a][2](j0):
                        src, dst = ins[a].at[pl.ds(s0, nr)], lands[a].at[pl.ds(d0, nr)]
                        pltpu.make_async_copy(src, dst, lsem.at[a]).start()
                        for k, chip in enumerate(chips):
                            pltpu.make_async_remote_copy(src_ref=src, dst_ref=dst, send_sem=send.at[3 * a + k],
                                                         recv_sem=recv.at[3 * a + k], device_id=(*chip, c),
                                                         device_id_type=MESH).start()

    sems = [pltpu.SemaphoreType.DMA((3 * n,)), pltpu.SemaphoreType.DMA((3 * n,)), pltpu.SemaphoreType.DMA((n,))]
    hbm = lambda a: pltpu.HBM(a.shape, a.dtype)
    res = pl.pallas_call(
        body, name=name,
        out_shape=sems + [hbm(s) for s in srcs] + [hbm(z) for z in zeros] + [_TOKEN],
        in_specs=[_HBM] * (2 * n) + [pl.BlockSpec(memory_space=pl.ANY)] * len(extra),
        out_specs=[_SEM] * 3 + [_HBM] * (2 * n) + [_VMEM_TOKEN],
        input_output_aliases={a: 3 + a for a in range(2 * n)},
        compiler_params=pltpu.CompilerParams(has_side_effects=_EFFECT),
    )(*[pltpu.with_memory_space_constraint(s, pltpu.HBM) for s in srcs],
      *[pltpu.with_memory_space_constraint(z, pltpu.HBM) for z in zeros], *extra)
    return res[:3], res[3:3 + n], res[3 + n:3 + 2 * n], res[-1]


def _gather_wait(names, sems, srcs, lands, after, name):
    n = len(srcs)
    lay = [LAYOUT[nm] for nm in names]

    def body(*refs):
        ins, zones = refs[:n], refs[n:2 * n]
        send, recv, lsem = refs[2 * n:2 * n + 3]
        x, y, c = _position()
        for a in range(n):
            whole = zones[a].at[pl.ds(0, lay[a][0])]
            for k in range(3):
                cp = pltpu.make_async_remote_copy(src_ref=ins[a], dst_ref=whole, send_sem=send.at[3 * a + k],
                                                  recv_sem=recv.at[3 * a + k], device_id=(x, y, 1 - c),
                                                  device_id_type=MESH)
                cp.wait_send()
                cp.wait_recv()
            pltpu.make_async_copy(ins[a], whole, lsem.at[a]).wait()

    hbm = lambda a: pltpu.HBM(a.shape, a.dtype)
    res = pl.pallas_call(
        body, name=name,
        out_shape=[hbm(s) for s in srcs] + [hbm(z) for z in lands],
        in_specs=[_HBM] * (2 * n) + [_SEM] * 3 + [pl.BlockSpec(memory_space=pl.ANY)], out_specs=[_HBM] * (2 * n),
        input_output_aliases={a: a for a in range(2 * n)},
        compiler_params=pltpu.CompilerParams(has_side_effects=_EFFECT),
    )(*srcs, *lands, *sems, after)
    return res[n:]


def _clip_pieces(lay_a, jsrc, c0):
    h = lay_a[1] // 2
    lo, hi = c0 * h, (c0 + 1) * h
    out = []
    for s0, d0, nr in lay_a[2](jsrc):
        a, b = max(d0, lo), min(d0 + nr, hi)
        if a < b:
            out.append((s0 + a - d0, a, b - a))
    return out


def _rows_of(pieces):
    return sum(nr for _, _, nr in pieces)


def _both_cores(body_for):
    x, y, c = _position()
    j = 2 * x + y
    for j0 in range(N_CHIPS):
        for c0 in range(2):
            @pl.when((j == j0) & (c == c0))
            def _(j0=j0, c0=c0):
                body_for(j0, c0)


STAGE_ROWS = 512


def _staged_copy(src, dst, buf, sem_in, sem_out, rows):
    ch = buf.shape[0]
    for r in range(0, rows, ch):
        nr = min(ch, rows - r)
        stage = buf.at[pl.ds(0, nr)]
        cin = pltpu.make_async_copy(src.at[pl.ds(r, nr)], stage, sem_in)
        cin.start()
        cin.wait()
        cout = pltpu.make_async_copy(stage, dst.at[pl.ds(r, nr)], sem_out)
        cout.start()
        cout.wait()


def _half_to_sibling(names, gl, name, after=None):
    n = len(gl)
    halves = [LAYOUT[nm][1] // 2 for nm in names]
    extra = [] if after is None else [after]

    def body(*refs):
        ins, outs = refs[:n], refs[n + len(extra):2 * n + len(extra)]
        send, recv = refs[2 * n + len(extra):]
        x, y, c = _position()

        def run(j0, c0):
            cps = [pltpu.make_async_remote_copy(src_ref=ins[a].at[pl.ds((1 - c0) * halves[a], halves[a])], dst_ref=outs[a],
                                                send_sem=send.at[a], recv_sem=recv.at[a], device_id=(x, y, 1 - c),
                                                device_id_type=MESH) for a in range(n)]
            for cp in cps:
                cp.start()
            for cp in cps:
                cp.wait()

        _both_cores(run)

    return pl.pallas_call(
        body, name=name,
        out_shape=[jax.ShapeDtypeStruct((halves[a],) + gl[a].shape[1:], gl[a].dtype) for a in range(n)],
        in_specs=[_HBM] * n + [pl.BlockSpec(memory_space=pl.ANY)] * len(extra), out_specs=[_HBM] * n,
        scratch_shapes=[pltpu.SemaphoreType.DMA((n,)), pltpu.SemaphoreType.DMA((n,))],
    )(*gl, *extra)


def _chip_scatter_half(names, parts, name):
    n = len(parts)
    lay = [LAYOUT[nm] for nm in names]
    zeros = [jnp.zeros((N_CHIPS, lay[a][0]) + parts[a].shape[1:], parts[a].dtype) for a in range(n)]

    def body(*refs):
        ins, outs = refs[:n], refs[2 * n:3 * n]
        send, recv = refs[3 * n:3 * n + 2]
        stage, sem_in, sem_out = refs[3 * n + 2:4 * n + 2], refs[4 * n + 2], refs[4 * n + 3]
        x, y, c = _position()
        chips = [(1 - x, y), (x, 1 - y), (1 - x, 1 - y)]

        def run(j0, c0):
            def sized(a, rows):
                return outs[a].at[0, pl.ds(0, rows)]

            for a in range(n):
                base = c0 * (lay[a][1] // 2)
                for k, chip in enumerate(chips):
                    for s0, d0, nr in _clip_pieces(lay[a], j0 ^ _REL[k], c0):
                        pltpu.make_async_remote_copy(
                            src_ref=ins[a].at[pl.ds(d0 - base, nr)], dst_ref=outs[a].at[j0, pl.ds(s0, nr)],
                            send_sem=send.at[3 * a + k], recv_sem=recv.at[3 * a + k],
                            device_id=(*chip, c), device_id_type=MESH).start()
            for a in range(n):
                base = c0 * (lay[a][1] // 2)
                for s0, d0, nr in _clip_pieces(lay[a], j0, c0):
                    _staged_copy(ins[a].at[pl.ds(d0 - base, nr)], outs[a].at[j0, pl.ds(s0, nr)], stage[a],
                                 sem_in.at[a], sem_out.at[a], nr)
            for a in range(n):
                got = _rows_of(_clip_pieces(lay[a], j0, c0))
                for k in range(3):
                    sent = _rows_of(_clip_pieces(lay[a], j0 ^ _REL[k], c0))
                    if sent:
                        pltpu.make_async_remote_copy(src_ref=sized(a, sent), dst_ref=sized(a, sent),
                                                     send_sem=send.at[3 * a + k], recv_sem=recv.at[3 * a + k],
                                                     device_id=(x, y, c), device_id_type=MESH).wait_send()
                    if got:
                        pltpu.make_async_remote_copy(src_ref=sized(a, got), dst_ref=sized(a, got),
                                                     send_sem=send.at[3 * a + k], recv_sem=recv.at[3 * a + k],
                                                     device_id=(x, y, c), device_id_type=MESH).wait_recv()

        _both_cores(run)

    return pl.pallas_call(
        body, name=name,
        out_shape=[jax.ShapeDtypeStruct(z.shape, z.dtype) for z in zeros],
        in_specs=[_HBM] * (2 * n), out_specs=[_HBM] * n, input_output_aliases={n + a: a for a in range(n)},
        scratch_shapes=[pltpu.SemaphoreType.DMA((3 * n,)), pltpu.SemaphoreType.DMA((3 * n,))]
        + [pltpu.VMEM((min(STAGE_ROWS, p.shape[0]),) + p.shape[1:], p.dtype) for p in parts]
        + [pltpu.SemaphoreType.DMA((n,)), pltpu.SemaphoreType.DMA((n,))],
    )(*parts, *zeros)


def _subset_exchange(names, bufs, l, name):
    n = len(bufs)
    lay = [LAYOUT[nm] for nm in names]

    def body(*refs):
        outs = refs[n:2 * n]
        send, recv = refs[2 * n:]
        x, y, c = _position()

        def run(j0, c0):
            for a in range(n):
                for s0, _, nr in _clip_pieces(lay[a], j0, c0):
                    rows = outs[a].at[l, pl.ds(s0, nr)]
                    pltpu.make_async_remote_copy(src_ref=rows, dst_ref=rows, send_sem=send.at[a], recv_sem=recv.at[a],
                                                 device_id=(x, y, 1 - c), device_id_type=MESH).start()
            for a in range(n):
                for c_half, wait_send in ((c0, True), (1 - c0, False)):
                    rows = _rows_of(_clip_pieces(lay[a], j0, c_half))
                    if rows:
                        ref = outs[a].at[l, pl.ds(0, rows)]
                        cp = pltpu.make_async_remote_copy(src_ref=ref, dst_ref=ref, send_sem=send.at[a], recv_sem=recv.at[a],
                                                          device_id=(x, y, 1 - c), device_id_type=MESH)
                        if wait_send:
                            cp.wait_send()
                        else:
                            cp.wait_recv()

        _both_cores(run)

    return pl.pallas_call(
        body, name=name,
        out_shape=[jax.ShapeDtypeStruct(b.shape, b.dtype) for b in bufs],
        in_specs=[_HBM] * n, out_specs=[_HBM] * n, input_output_aliases={a: a for a in range(n)},
        scratch_shapes=[pltpu.SemaphoreType.DMA((n,)), pltpu.SemaphoreType.DMA((n,))],
    )(*bufs)


def _scatter_start(names, gl, name):
    n = len(gl)
    lay = [LAYOUT[nm] for nm in names]
    zones = [lax.empty((N_CHIPS, lay[a][0]) + gl[a].shape[1:], gl[a].dtype) for a in range(n)]

    def body(*refs):
        ins, lands = refs[:n], refs[n:2 * n]
        send, recv, lsem = refs[2 * n:2 * n + 3]
        refs[-1][...] = jnp.zeros_like(refs[-1])
        x, y, c = _position()
        j = 2 * x + y
        chips = [(1 - x, y), (x, 1 - y), (1 - x, 1 - y)]
        for j0 in range(N_CHIPS):
            @pl.when(j == j0)
            def _(j0=j0):
                for a in range(n):
                    for s0, d0, nr in lay[a][2](j0):
                        pltpu.make_async_copy(ins[a].at[pl.ds(d0, nr)], lands[a].at[j0, pl.ds(s0, nr)], lsem.at[a]).start()
                    for k, chip in enumerate(chips):
                        for s0, d0, nr in lay[a][2](j0 ^ _REL[k]):
                            pltpu.make_async_remote_copy(
                                src_ref=ins[a].at[pl.ds(d0, nr)], dst_ref=lands[a].at[j0, pl.ds(s0, nr)],
                                send_sem=send.at[3 * a + k], recv_sem=recv.at[3 * a + k],
                                device_id=(*chip, c), device_id_type=MESH).start()

    sems = [pltpu.SemaphoreType.DMA((3 * n,)), pltpu.SemaphoreType.DMA((3 * n,)), pltpu.SemaphoreType.DMA((n,))]
    hbm = lambda a: pltpu.HBM(a.shape, a.dtype)
    res = pl.pallas_call(
        body, name=name,
        out_shape=sems + [hbm(g) for g in gl] + [hbm(z) for z in zones] + [_TOKEN],
        in_specs=[_HBM] * (2 * n), out_specs=[_SEM] * 3 + [_HBM] * (2 * n) + [_VMEM_TOKEN],
        input_output_aliases={a: 3 + a for a in range(2 * n)},
        compiler_params=pltpu.CompilerParams(has_side_effects=_EFFECT),
    )(*[pltpu.with_memory_space_constraint(g, pltpu.HBM) for g in gl],
      *[pltpu.with_memory_space_constraint(z, pltpu.HBM) for z in zones])
    return res[:3], res[3:3 + n], res[3 + n:3 + 2 * n], res[-1]


def _scatter_wait(names, sems, srcs, lands, after, name):
    n = len(srcs)
    lay = [LAYOUT[nm] for nm in names]

    def body(*refs):
        zones = refs[n:2 * n]
        send, recv, lsem = refs[2 * n:2 * n + 3]
        x, y, c = _position()
        for a in range(n):
            whole = zones[a].at[0, pl.ds(0, lay[a][0])]
            for k in range(3):
                cp = pltpu.make_async_remote_copy(src_ref=whole, dst_ref=whole, send_sem=send.at[3 * a + k],
                                                  recv_sem=recv.at[3 * a + k], device_id=(x, y, 1 - c),
                                                  device_id_type=MESH)
                cp.wait_send()
                cp.wait_recv()
            pltpu.make_async_copy(whole, whole, lsem.at[a]).wait()

    hbm = lambda a: pltpu.HBM(a.shape, a.dtype)
    res = pl.pallas_call(
        body, name=name,
        out_shape=[hbm(s) for s in srcs] + [hbm(z) for z in lands],
        in_specs=[_HBM] * (2 * n) + [_SEM] * 3 + [pl.BlockSpec(memory_space=pl.ANY)], out_specs=[_HBM] * (2 * n),
        input_output_aliases={a: a for a in range(2 * n)},
        compiler_params=pltpu.CompilerParams(has_side_effects=_EFFECT),
    )(*srcs, *lands, *sems, after)
    return res[n:]


def _swap_start(arrs, name):
    n = len(arrs)
    zones = [lax.empty(a.shape, a.dtype) for a in arrs]

    def body(*refs):
        ins, lands = refs[:n], refs[n:2 * n]
        send, recv = refs[2 * n:2 * n + 2]
        refs[-1][...] = jnp.zeros_like(refs[-1])
        x, y, c = _position()
        for a in range(n):
            pltpu.make_async_remote_copy(src_ref=ins[a], dst_ref=lands[a], send_sem=send.at[a], recv_sem=recv.at[a],
                                         device_id=(x, y, 1 - c), device_id_type=MESH).start()

    sems = [pltpu.SemaphoreType.DMA((n,)), pltpu.SemaphoreType.DMA((n,))]
    hbm = lambda a: pltpu.HBM(a.shape, a.dtype)
    res = pl.pallas_call(
        body, name=name,
        out_shape=sems + [hbm(a) for a in arrs] + [hbm(z) for z in zones] + [_TOKEN],
        in_specs=[_HBM] * (2 * n), out_specs=[_SEM] * 2 + [_HBM] * (2 * n) + [_VMEM_TOKEN],
        input_output_aliases={a: 2 + a for a in range(2 * n)},
        compiler_params=pltpu.CompilerParams(has_side_effects=_EFFECT),
    )(*[pltpu.with_memory_space_constraint(a, pltpu.HBM) for a in arrs],
      *[pltpu.with_memory_space_constraint(z, pltpu.HBM) for z in zones])
    return res[:2], res[2:2 + n], res[2 + n:2 + 2 * n], res[-1]


def _swap_wait(sems, srcs, lands, after, name):
    n = len(srcs)

    def body(*refs):
        ins, zones = refs[:n], refs[n:2 * n]
        send, recv = refs[2 * n:2 * n + 2]
        x, y, c = _position()
        for a in range(n):
            cp = pltpu.make_async_remote_copy(src_ref=ins[a], dst_ref=zones[a], send_sem=send.at[a], recv_sem=recv.at[a],
                                              device_id=(x, y, 1 - c), device_id_type=MESH)
            cp.wait_send()
            cp.wait_recv()

    hbm = lambda a: pltpu.HBM(a.shape, a.dtype)
    res = pl.pallas_call(
        body, name=name,
        out_shape=[hbm(s) for s in srcs] + [hbm(z) for z in lands],
        in_specs=[_HBM] * (2 * n) + [_SEM] * 2 + [pl.BlockSpec(memory_space=pl.ANY)], out_specs=[_HBM] * (2 * n),
        input_output_aliases={a: a for a in range(2 * n)},
        compiler_params=pltpu.CompilerParams(has_side_effects=_EFFECT),
    )(*srcs, *lands, *sems, after)
    return res[:n], res[n:]


def _row_tile(r):
    for t in (256, 128, 64, 32, 16, 8):
        if r % t == 0 and r > t:
            return t
    return r


def _pair_add_half(g, rb, c_arr, name):
    hrows, rest = rb.shape[0], rb.shape[1:]
    tr = _row_tile(hrows)
    nb = hrows // tr
    z = (0,) * len(rest)

    def body(c_ref, g_ref, r_ref, o_ref):
        o_ref[...] = (g_ref[...].astype(F32) + r_ref[...].astype(F32)).astype(o_ref.dtype)

    return pl.pallas_call(
        body, name=name,
        grid_spec=pltpu.PrefetchScalarGridSpec(
            num_scalar_prefetch=1, grid=(nb,),
            in_specs=[pl.BlockSpec((tr,) + rest, lambda i, c_ref: (c_ref[0] * nb + i,) + z),
                      pl.BlockSpec((tr,) + rest, lambda i, c_ref: (i,) + z)],
            out_specs=pl.BlockSpec((tr,) + rest, lambda i, c_ref: (i,) + z)),
        out_shape=jax.ShapeDtypeStruct((hrows,) + rest, BF16),
        compiler_params=pltpu.CompilerParams(dimension_semantics=("parallel",), vmem_limit_bytes=VMEM_LIMIT),
    )(c_arr, g, rb)


def _sum_slabs(slabs, l, buf, name):
    m = len(slabs)
    n, R, rest = slabs[0].shape[0], slabs[0].shape[1], slabs[0].shape[2:]
    tr = _row_tile(R)
    z = (0,) * len(rest)

    def body(*refs):
        total = None
        for r_ref in refs[:m]:
            acc = r_ref[0].astype(F32)
            for k in range(1, n):
                acc = acc + r_ref[k].astype(F32)
            total = acc if total is None else total + acc
        refs[-1][...] = total

    if R // tr > 64 and len(rest) == 1 and rest[0] % 256 == 0:
        grid = (rest[0] // 256,)
        in_spec = pl.BlockSpec((n, R, 256), lambda i: (0, 0, i))
        out_spec = pl.BlockSpec((None, R, 256), lambda i: (l, 0, i))
    else:
        grid = (R // tr,)
        in_spec = pl.BlockSpec((n, tr) + rest, lambda i: (0, i) + z)
        out_spec = pl.BlockSpec((None, tr) + rest, lambda i: (l, i) + z)
    in_specs, args, aliases = [in_spec] * m, list(slabs), {}
    if buf is not None:
        in_specs.append(pl.BlockSpec(memory_space=pl.ANY))
        args.append(buf)
        aliases = {m: 0}
    return pl.pallas_call(
        body, name=name, grid=grid, in_specs=in_specs, out_specs=out_spec,
        out_shape=jax.ShapeDtypeStruct((DEPTH, R) + rest, F32), input_output_aliases=aliases,
        compiler_params=pltpu.CompilerParams(
            dimension_semantics=("parallel",),
            vmem_limit_bytes=_vmem(m * _nbytes(in_spec.block_shape, slabs[0].dtype) + _nbytes(out_spec.block_shape, F32),
                                   2 * _nbytes(out_spec.block_shape, F32))),
    )(*args)


def _adam_math(w, g, m, v):
    mn = ADAM_B1 * m + (1.0 - ADAM_B1) * g
    vn = ADAM_B2 * v + (1.0 - ADAM_B2) * (g * g)
    m_hat = mn / (1.0 - ADAM_B1 ** ADAM_STEP)
    v_hat = vn / (1.0 - ADAM_B2 ** ADAM_STEP)
    return -ADAM_LR * (m_hat / (jnp.sqrt(v_hat) + ADAM_EPS) + ADAM_WD * w), mn, vn


def _adamw(w, g, m, v, name):
    L, R, C = w.shape
    tr = _row_tile(R)

    def body(w_ref, g_ref, m_ref, v_ref, d_ref, mo_ref, vo_ref):
        d_ref[...], mo_ref[...], vo_ref[...] = _adam_math(w_ref[...], g_ref[...], m_ref[...], v_ref[...])

    if R // tr > 64 and C % 128 == 0:
        spec, grid = pl.BlockSpec((None, R, 128), lambda l, i: (l, 0, i)), (L, C // 128)
    else:
        spec, grid = pl.BlockSpec((None, tr, C), lambda l, i: (l, i, 0)), (L, R // tr)
    return pl.pallas_call(
        body, name=name, grid=grid, in_specs=[spec] * 4, out_specs=[spec] * 3,
        out_shape=[jax.ShapeDtypeStruct((L, R, C), F32)] * 3,
        compiler_params=pltpu.CompilerParams(dimension_semantics=("parallel", "parallel"),
                                             vmem_limit_bytes=_vmem(7 * _nbytes(spec.block_shape, F32))),
    )(w, g, m, v)


_VMEM_WHOLE = pl.BlockSpec(memory_space=pltpu.VMEM)


def _matrix_update(gath, w, m, v, name):
    K = w.shape[1]

    def body(g0_ref, g1_ref, w_ref, m_ref, v_ref, go_ref, d_ref, mo_ref, vo_ref):
        for l, gr in enumerate((g0_ref, g1_ref)):
            for k in range(K):
                g = gr[0, k].astype(F32)
                for dev in range(1, 8):
                    g = g + gr[dev, k].astype(F32)
                go_ref[l, k] = g
                d_ref[l, k], mo_ref[l, k], vo_ref[l, k] = _adam_math(w_ref[l, k], g, m_ref[l, k], v_ref[l, k])

    return pl.pallas_call(
        body, name=name, in_specs=[_VMEM_WHOLE] * 5, out_specs=[_VMEM_WHOLE] * 4,
        out_shape=[jax.ShapeDtypeStruct(w.shape, F32)] * 4,
        compiler_params=pltpu.CompilerParams(vmem_limit_bytes=32 * MIB),
    )(gath[0], gath[1], w, m, v)


VECS = (("pre_norm_g", D), ("post_norm_g", D), ("gm_ln_g", GM_W), ("gm_ln_b", GM_W), ("mla_q_norm_g", QR),
        ("mla_kv_norm_g", KVR), ("lru_conv_b", LRU_W), ("lru_b_a", LRU_W), ("lru_b_x", LRU_W), ("lru_lambda", LRU_W))
VEC_KEY = {"pre_norm_g": "pre_g", "post_norm_g": "post_g", "gm_ln_g": "ln_g", "gm_ln_b": "ln_b", "mla_q_norm_g": "qg",
           "mla_kv_norm_g": "kvg", "lru_conv_b": "conv_b", "lru_b_a": "ba", "lru_b_x": "bx", "lru_lambda": "lam"}
VEC_ROWS, VEC_W, VEC_ROW0, LOSS_ROW = 16, LRU_W, GM_G, 14


def _pack_rows(LG, loss_part):
    per = len(VECS) + 1
    ins = []
    for G in LG:
        ins += [G[VEC_KEY[n]] for n, _ in VECS] + [G["bst"]]
    ins.append(loss_part)

    def body(*refs):
        o_ref = refs[-1]
        o_ref[...] = jnp.zeros_like(o_ref)
        for l in range(DEPTH):
            base = VEC_ROWS * l
            o_ref[pl.ds(base, 8), pl.ds(0, GM_B)] = refs[per * l + len(VECS)][...].T[:8, :]
            for t, (_, width) in enumerate(VECS):
                o_ref[pl.ds(base + VEC_ROW0 + t, 1), pl.ds(0, width)] = refs[per * l + t][...]
        o_ref[pl.ds(LOSS_ROW, 1), pl.ds(0, 128)] = jnp.broadcast_to(refs[-2][...], (1, 128))

    return pl.pallas_call(
        body, name="pack_rows", in_specs=[_VMEM_WHOLE] * len(ins), out_specs=_VMEM_WHOLE,
        out_shape=jax.ShapeDtypeStruct((DEPTH * VEC_ROWS, VEC_W), F32),
    )(*ins)


def _vector_update(gath, W, M, V):
    names = [n for n, _ in VECS] + ["gm_bs"]
    nw = len(names)

    def body(*refs):
        g_ref = refs[0]
        wr, mr, vr = refs[1:1 + nw], refs[1 + nw:1 + 2 * nw], refs[1 + 2 * nw:1 + 3 * nw]
        outs = refs[1 + 3 * nw:]
        s = g_ref[0]
        for dev in range(1, 8):
            s = s + g_ref[dev]
        for t, (_, width) in enumerate(VECS):
            for l in range(DEPTH):
                r = VEC_ROWS * l + VEC_ROW0 + t
                g = s[r:r + 1, :width]
                row = (pl.ds(l, 1), slice(None))
                res = (g,) + _adam_math(wr[t][row], g, mr[t][row], vr[t][row])
                for q in range(4):
                    outs[4 * t + q][row] = res[q]
        t = len(VECS)
        for l in range(DEPTH):
            for k in range(GM_G):
                g = s[VEC_ROWS * l + k:VEC_ROWS * l + k + 1, :GM_B]
                row = (l, pl.ds(k, 1), slice(None))
                res = (g,) + _adam_math(wr[t][row], g, mr[t][row], vr[t][row])
                for q in range(4):
                    outs[4 * t + q][row] = res[q]
        outs[4 * nw][...] = s[LOSS_ROW:LOSS_ROW + 1, :128]

    ws = [W[n] for n in names]
    out_shape = []
    for w in ws:
        out_shape += [jax.ShapeDtypeStruct(w.shape, F32)] * 4
    out_shape.append(jax.ShapeDtypeStruct((1, 128), F32))
    res = pl.pallas_call(
        body, name="vector_update", in_specs=[_VMEM_WHOLE] * (1 + 3 * nw), out_specs=[_VMEM_WHOLE] * (4 * nw + 1),
        out_shape=out_shape, compiler_params=pltpu.CompilerParams(vmem_limit_bytes=VMEM_LIMIT),
    )(gath, *ws, *[M[n] for n in names], *[V[n] for n in names])
    return {n: tuple(res[4 * t:4 * t + 4]) for t, n in enumerate(names)}, res[4 * nw]


SHARDED = ("w_in", "mla_w_uq", "mla_w_ukv", "lru_conv_w", "w_proj_a", "w_proj_b", "w_proj_c", "w_out")
FIRST = ("w_in", "lru_conv_w")
LATER = tuple(n for n in SHARDED if n not in FIRST)
COL_SHARDED = ("w_in", "mla_w_uq", "mla_w_ukv", "lru_conv_w")
SMALL = ("pre_norm_g", "gm_ln_g", "gm_ln_b", "gm_ws", "gm_bs", "mla_q_norm_g", "mla_kv_norm_g", "lru_conv_b",
         "lru_w_a", "lru_b_a", "lru_w_x", "lru_b_x", "lru_lambda", "post_norm_g")
WEIGHTS = ("pre_norm_g", "w_in", "gm_ln_g", "gm_ln_b", "gm_ws", "gm_bs", "mla_q_norm_g", "mla_w_uq",
           "mla_kv_norm_g", "mla_w_ukv", "lru_conv_w", "lru_conv_b", "lru_w_a", "lru_b_a", "lru_w_x", "lru_b_x",
           "lru_lambda", "w_proj_a", "w_proj_b", "w_proj_c", "w_out", "post_norm_g")


GB_KEY = {"w_in": "wp", "mla_w_uq": "wuq", "mla_w_ukv": "wukv", "w_proj_a": "wpa", "w_proj_b": "wpb",
          "w_proj_c": "wpc", "w_out": "wout"}


def _prepare(l, gathered, small, wsb):
    P = {GB_KEY[n]: gathered[n] for n in GB_KEY if n in gathered}
    P["conv_w"] = gathered["lru_conv_w"].transpose(1, 0, 2).reshape(CONV_W, LRU_W)
    P["wsb"] = wsb
    row = lambda n: small[n][l][None, :]
    P["pre_g"], P["post_g"] = row("pre_norm_g"), row("post_norm_g")
    P["ln_g"], P["ln_b"] = row("gm_ln_g"), row("gm_ln_b")
    P["ws"] = small["gm_ws"][l]
    P["bst"] = jnp.pad(small["gm_bs"][l].T, ((0, 0), (0, 128 - GM_G)))
    P["qg"], P["kvg"] = row("mla_q_norm_g"), row("mla_kv_norm_g")
    P["conv_b"], P["ba"], P["bx"], P["lam"] = row("lru_conv_b"), row("lru_b_a"), row("lru_b_x"), row("lru_lambda")
    return P


def kernel(x, pre_norm_g, w_in, gm_ln_g, gm_ln_b, gm_ws, gm_bs, mla_q_norm_g, mla_w_uq, mla_kv_norm_g, mla_w_ukv, lru_conv_w, lru_conv_b, lru_w_a, lru_b_a, lru_w_x, lru_b_x, lru_lambda, w_proj_a, w_proj_b, w_proj_c, w_out, post_norm_g, loss_target, m_pre_norm_g, m_w_in, m_gm_ln_g, m_gm_ln_b, m_gm_ws, m_gm_bs, m_mla_q_norm_g, m_mla_w_uq, m_mla_kv_norm_g, m_mla_w_ukv, m_lru_conv_w, m_lru_conv_b, m_lru_w_a, m_lru_b_a, m_lru_w_x, m_lru_b_x, m_lru_lambda, m_w_proj_a, m_w_proj_b, m_w_proj_c, m_w_out, m_post_norm_g, v_pre_norm_g, v_w_in, v_gm_ln_g, v_gm_ln_b, v_gm_ws, v_gm_bs, v_mla_q_norm_g, v_mla_w_uq, v_mla_kv_norm_g, v_mla_w_ukv, v_lru_conv_w, v_lru_conv_b, v_lru_w_a, v_lru_b_a, v_lru_w_x, v_lru_b_x, v_lru_lambda, v_w_proj_a, v_w_proj_b, v_w_proj_c, v_w_out, v_post_norm_g):
    args = dict(locals())
    W = {n: args[n] for n in WEIGHTS}
    M = {n: args["m_" + n] for n in WEIGHTS}
    V = {n: args["v_" + n] for n in WEIGHTS}
    c = lax.axis_index("c")

    def shards(l, names):
        out = []
        for n in names:
            blk = W[n][l].T if n in TRANSPOSED else W[n][l]
            out.append(blk[None] if n == "lru_conv_w" else blk.astype(BF16))
        return out

    small = {n: W[n] for n in SMALL}
    wsb = _superblocks(W["lru_w_a"], W["lru_w_x"])
    tabs = _rope_tables()
    s0a, s0b, s1a, s1b = shards(0, FIRST), shards(0, LATER), shards(1, FIRST), shards(1, LATER)
    g0, zones = _weights_allgather(FIRST, s0a, "weights_allgather_l0", carry=_gather_zeros(LATER, s0b)
                                   + _gather_zeros(FIRST, s1a) + _gather_zeros(LATER, s1b))
    nl, nf = len(LATER), len(FIRST)
    w0b = _gather_start(LATER, s0b, zones[:nl], "weights_gather_start_l0")
    w1a = _gather_start(FIRST, s1a, zones[nl:nl + nf], "weights_gather_start_l1_first", after=w0b[3])
    w1b = _gather_start(LATER, s1b, zones[nl + nf:], "weights_gather_start_l1_later", after=w1a[3])

    def late(started, name):
        def wait(after):
            got = _gather_wait(LATER, *started[:3], after, name)
            return {GB_KEY[n]: g for n, g in zip(LATER, got)}
        return wait

    P = [_prepare(0, dict(zip(FIRST, g0)), small, wsb), None]
    h0 = x[0]
    h1, A0 = _layer_fwd(h0, P[0], 0, tabs, w1b[3], late(w0b, "weights_gather_wait_l0"))
    g1 = _gather_wait(FIRST, *w1a[:3], h1, "weights_gather_wait_l1_first")
    P[1] = _prepare(1, dict(zip(FIRST, g1)), small, wsb)
    h2, A1 = _layer_fwd(h1, P[1], 1, tabs, None, late(w1b, "weights_gather_wait_l1_later"))
    dy, loss_part = _loss_fwd(h2, loss_target[0])

    def large_grads(G, GB, names):
        conv = G["conv_w"].reshape(CONV_W, N_CHIPS, LRU_W // N_CHIPS).transpose(1, 0, 2)
        return [conv if n == "lru_conv_w" else GB[GB_KEY[n]] for n in names]

    d1, G1, GB1 = _layer_bwd(dy, A1, P[1], 1, tabs)
    sc1 = _scatter_start(SHARDED, large_grads(G1, GB1, SHARDED), "grads_scatter_start_l1")
    started = {}

    def early0(GB):
        mine1 = _scatter_wait(SHARDED, *sc1[:3], GB["wukv"], "grads_scatter_wait_l1")
        started["swap1"] = _swap_start(mine1, "partials_swap_start_l1")
        started["sc0"] = _scatter_start(LATER, [GB[GB_KEY[n]] for n in LATER], "grads_scatter_start_l0")
        return started["sc0"][3], started["swap1"][3]

    d0, G0, GB0 = _layer_bwd(d1, A0, P[0], 0, tabs, sc1[3], early0)
    LG = (G0, G1)
    mine0 = _scatter_wait(LATER, *started["sc0"][:3], d0, "grads_scatter_wait_l0")
    swap0 = _swap_start(mine0, "partials_swap_start_l0")
    g0f = large_grads(G0, GB0, FIRST)
    c_arr = jnp.reshape(c, (1,)).astype(jnp.int32)
    from_sib = _half_to_sibling(FIRST, g0f, "grads_half_to_sibling_l0", after=swap0[3])
    pair = [_pair_add_half(g, rb, c_arr, "pair_add_" + n) for n, g, rb in zip(FIRST, g0f, from_sib)]
    slabs = _chip_scatter_half(FIRST, pair, "grads_chip_scatter_l0")
    mine1, theirs1 = _swap_wait(*started["swap1"][:3], slabs[0], "partials_swap_wait_l1")
    both = dict(zip(SHARDED, [_sum_slabs([a, b], 1, None, "sum_partials_l1_" + n)
                              for n, a, b in zip(SHARDED, mine1, theirs1)]))
    for n, s in zip(FIRST, slabs):
        both[n] = _sum_slabs([s], 0, both[n], "sum_slabs_l0_" + n)
    done = _subset_exchange(FIRST, [both[n] for n in FIRST], 0, "reduced_rows_to_sibling_l0")
    both.update(zip(FIRST, done))
    mine0, theirs0 = _swap_wait(*swap0[:3], done[0], "partials_swap_wait_l0")
    for n, a, b in zip(LATER, mine0, theirs0):
        both[n] = _sum_slabs([a, b], 0, both[n], "sum_partials_l0_" + n)
    both = [both[n] for n in SHARDED]
    grads = {}
    for n, b in zip(SHARDED, both):
        if n in TRANSPOSED and n != "w_in":
            b = jnp.swapaxes(b, 1, 2)
        grads[n] = b if n == "w_in" else b.reshape(W[n].shape)

    rows = _pack_rows(LG, loss_part)
    mats = []
    for g in LG:
        mats += [g["ws"].astype(BF16), g["wab"][0, :, :, :LRU_BW], g["wab"][1, :, :, :LRU_BW]]
    gath = _allgather([rows] + mats, "small_grads_allgather")
    upd, loss_row = _vector_update(gath[0], W, M, V)
    loss = loss_row[0, 0]
    for k, n in enumerate(("gm_ws", "lru_w_a", "lru_w_x")):
        upd[n] = _matrix_update((gath[1 + k], gath[4 + k]), W[n], M[n], V[n], "update_" + n)

    for n in SHARDED:
        if n == "w_in":
            tr = lambda a: jnp.swapaxes(a, 1, 2)
            res = _adamw(tr(W[n]), grads[n], tr(M[n]), tr(V[n]), "adamw_" + n)
            upd[n] = tuple(tr(a) for a in (grads[n],) + tuple(res))
        else:
            upd[n] = (grads[n],) + tuple(_adamw(W[n], grads[n], M[n], V[n], "adamw_" + n))

    return (loss, d0[None], *[upd[n][0] for n in WEIGHTS], *[upd[n][1] for n in WEIGHTS],
            *[upd[n][2] for n in WEIGHTS], *[upd[n][3] for n in WEIGHTS])
```

```python
import functools
import math

import jax
import jax.numpy as jnp
from jax import lax
from jax.experimental import pallas as pl
from jax.experimental.pallas import tpu as pltpu

F32, BF16 = jnp.float32, jnp.bfloat16
MESH = pl.DeviceIdType.MESH

S, D, DEPTH = 2048, 1024, 2
CHUNK, EPS = 64, 1e-6
GM_W, GM_G, GM_B = 1024, 4, 128
H, NOPE, ROPE, VDIM = 8, 128, 64, 128
QR, KVR = 384, 256
MLA_W = H * VDIM
LRU_W, LRU_NB, LRU_BW, LRU_C, CONV_W = 1280, 16, 80, 8.0, 4
ROPE_THETA = 10000.0
IN_SIZES = (GM_W, GM_W, GM_W, QR, KVR, ROPE, MLA_W, LRU_W, LRU_W, D, D, D)
N_IN = sum(IN_SIZES)
N_CHIPS = 4
ADAM_LR, ADAM_B1, ADAM_B2, ADAM_EPS, ADAM_WD, ADAM_STEP = 0.001, 0.9, 0.999, 1e-08, 0.01, 10

HP = 256
O_U, O_V, O_ZA, O_GA, O_GB, O_GC = 0, 1024, 2048, 3072, 4096, 5120
O_CKV, O_KR, O_CQ, O_XC, O_ZC, O_ZB = 6144, 6400, 6528, 7680, 8960, 10240
NP = 11264
MIB = 1024 * 1024
VMEM_LIMIT = 16 * MIB


def _vmem(block_bytes, temp_bytes=0):
    return int(min(max(2 * block_bytes + temp_bytes + 4 * MIB, VMEM_LIMIT), 56 * MIB))


def _nbytes(shape, dtype):
    return math.prod(d for d in shape if d is not None) * jnp.dtype(dtype).itemsize


def _tile(dim, target):
    if dim <= target:
        return dim
    t = (target // 128) * 128
    while dim % t:
        t -= 128
    return t


def _sig(x):
    return jax.nn.sigmoid(x)


def _silu(x):
    return x * _sig(x)


def _dsilu(x):
    s = _sig(x)
    return s * (1.0 + x * (1.0 - s))


def _mm(a, b, mode, name, out_dtype=F32, tm=1024, tn=1024, tk=1024, b_lead=None, out_lead=None, token=None):
    b2 = b.shape[1:] if b_lead is not None else b.shape
    if mode == "nn":
        (M, K), (K2, N) = a.shape, b2
    elif mode == "nt":
        (M, K), (N, K2) = a.shape, b2
    else:
        (K, M), (K2, N) = a.shape, b2
    assert K == K2, (name, a.shape, b.shape)
    tm, tn, tk = _tile(M, tm), _tile(N, tn), _tile(K, tk)
    nk = K // tk
    if mode == "tn":
        a_spec = pl.BlockSpec((tk, tm), lambda i, j, k: (k, i))
        lhs_c = 0
    else:
        a_spec = pl.BlockSpec((tm, tk), lambda i, j, k: (i, k))
        lhs_c = 1
    b_blk, b_idx, rhs_c = ((tn, tk), (lambda i, j, k: (j, k)), 1) if mode == "nt" else ((tk, tn), (lambda i, j, k: (k, j)), 0)
    if b_lead is None:
        b_spec = pl.BlockSpec(b_blk, b_idx)
    else:
        b_spec = pl.BlockSpec((None,) + b_blk, functools.partial(lambda i, j, k, f, l: (l,) + f(i, j, k), f=b_idx, l=b_lead))
    dims = (((lhs_c,), (rhs_c,)), ((), ()))
    in_specs, args, aliases = [a_spec, b_spec], [a, b], {}
    if out_lead is None:
        out_spec = pl.BlockSpec((tm, tn), lambda i, j, k: (i, j))
        out_shape = jax.ShapeDtypeStruct((M, N), out_dtype)
    else:
        l_out, n_lead, buf = out_lead
        out_spec = pl.BlockSpec((None, tm, tn), functools.partial(lambda i, j, k, l: (l, i, j), l=l_out))
        out_shape = jax.ShapeDtypeStruct((n_lead, M, N), out_dtype)
        if buf is not None:
            in_specs.append(pl.BlockSpec(memory_space=pl.ANY))
            args.append(buf)
            aliases = {2: 0}
    if token is not None:
        in_specs.append(pl.BlockSpec(memory_space=pl.ANY))
        args.append(token)

    def body(a_ref, b_ref, *rest):
        o_ref, acc_ref = rest[-2:]
        k = pl.program_id(2)

        @pl.when(k == 0)
        def _():
            acc_ref[...] = jnp.zeros_like(acc_ref)

        acc_ref[...] += lax.dot_general(a_ref[...].astype(BF16), b_ref[...].astype(BF16), dims,
                                        preferred_element_type=F32)

        @pl.when(k == nk - 1)
        def _():
            o_ref[...] = acc_ref[...].astype(o_ref.dtype)

    return pl.pallas_call(
        body, name=name, grid=(M // tm, N // tn, nk),
        in_specs=in_specs, out_specs=out_spec, out_shape=out_shape,
        scratch_shapes=[pltpu.VMEM((tm, tn), F32)], input_output_aliases=aliases,
        compiler_params=pltpu.CompilerParams(
            dimension_semantics=("parallel", "parallel", "arbitrary"),
            vmem_limit_bytes=_vmem(_nbytes((tm, tk), a.dtype) + _nbytes((tk, tn), b.dtype) + _nbytes((tm, tn), out_dtype),
                                   _nbytes((tm, tn), F32) + _nbytes((tm, tk), BF16) + _nbytes((tk, tn), BF16))),
    )(*args)


def _rows(fn, name, tm, rows, halos=(), fulls=(), outs=(), accs=()):
    n = S // tm
    in_specs, args = [], []
    for arr, w, cb in rows:
        in_specs.append(pl.BlockSpec((tm, w), functools.partial(lambda i, cb: (i, cb), cb=cb)))
        args.append(arr)
    for arr, w, cb, side in halos:
        if side == "prev":
            im = functools.partial(lambda i, cb: (jnp.maximum(i * (tm // 16) - 1, 0), cb), cb=cb)
        else:
            im = functools.partial(lambda i, cb: (jnp.minimum((i + 1) * (tm // 16), S // 16 - 1), cb), cb=cb)
        in_specs.append(pl.BlockSpec((16, w), im))
        args.append(arr)
    for arr in fulls:
        in_specs.append(pl.BlockSpec(arr.shape, functools.partial(lambda i, nd: (0,) * nd, nd=arr.ndim)))
        args.append(arr)
    out_shape, out_specs, aliases, n_alias = [], [], {}, 0
    for k, o in enumerate(outs):
        if len(o) == 3 and o[2] == "T":
            out_shape.append(jax.ShapeDtypeStruct((o[0], S), o[1]))
            out_specs.append(pl.BlockSpec((o[0], tm), lambda i: (0, i)))
        elif len(o) == 3:
            buf, total, cb = o[2]
            out_shape.append(jax.ShapeDtypeStruct((S, total), o[1]))
            out_specs.append(pl.BlockSpec((tm, o[0]), functools.partial(lambda i, cb: (i, cb), cb=cb)))
            if buf is not None:
                aliases[len(args)] = k
                in_specs.append(pl.BlockSpec(memory_space=pl.ANY))
                args.append(buf)
                n_alias += 1
        else:
            out_shape.append(jax.ShapeDtypeStruct((S, o[0]), o[1]))
            out_specs.append(pl.BlockSpec((tm, o[0]), lambda i: (i, 0)))
    for shp in accs:
        out_shape.append(jax.ShapeDtypeStruct(shp, F32))
        out_specs.append(pl.BlockSpec(shp, functools.partial(lambda i, nd: (0,) * nd, nd=len(shp))))
    nr, nh, nf, no, na = len(rows), len(halos), len(fulls), len(outs), len(accs)
    blocks = (sum(_nbytes((tm, w), arr.dtype) for arr, w, _ in rows) + sum(_nbytes(a.shape, a.dtype) for a in fulls)
              + sum(_nbytes((tm, o[0]), o[1]) for o in outs) + sum(_nbytes(shp, F32) for shp in accs))
    widest = _nbytes((tm, max([w for _, w, _ in rows] + [o[0] for o in outs])), F32)

    def body(*refs):
        i = pl.program_id(0)
        ins, orefs = refs[:nr + nh + nf], refs[nr + nh + nf + n_alias:]
        rv = [r[...].astype(F32) for r in ins[:nr]]
        hv = [r[...].astype(F32)[8:] if h[3] == "prev" else r[...].astype(F32)[:8] for r, h in zip(ins[nr:nr + nh], halos)]
        fv = [r[...] for r in ins[nr + nh:]]
        o, a = fn(i, rv, hv, fv)
        assert len(o) == no and len(a) == na, name
        for spec, ref, val in zip(outs, orefs[:no], o):
            ref[...] = (val.T if len(spec) == 3 and spec[2] == "T" else val).astype(ref.dtype)
        if na:
            @pl.when(i == 0)
            def _():
                for ref in orefs[no:]:
                    ref[...] = jnp.zeros_like(ref)

            for ref, val in zip(orefs[no:], a):
                ref[...] += val

    res = pl.pallas_call(
        body, name=name, grid=(n,), in_specs=in_specs, out_specs=out_specs, out_shape=out_shape,
        input_output_aliases=aliases,
        compiler_params=pltpu.CompilerParams(dimension_semantics=("arbitrary",), vmem_limit_bytes=_vmem(blocks, 6 * widest)),
    )(*args)
    return res


def _shift_down(xb, halo, s, row):
    fix = jnp.tile(pltpu.roll(halo, s, 0), (xb.shape[0] // 8, 1))
    return jnp.where(row >= s, pltpu.roll(xb, s, 0), fix)


def _shift_up(xb, halo, s, row):
    tm = xb.shape[0]
    fix = jnp.tile(pltpu.roll(halo, 8 - s, 0), (tm // 8, 1))
    return jnp.where(row < tm - s, pltpu.roll(xb, tm - s, 0), fix)


def _rms(x):
    return lax.rsqrt(jnp.mean(x * x, axis=-1, keepdims=True) + EPS)


def _rms_bwd(dy, x, g):
    r = _rms(x)
    xh = x * r
    dxh = dy * g
    dx = r * (dxh - xh * jnp.mean(dxh * xh, axis=-1, keepdims=True))
    return dx, dy * xh


def _colsum(x):
    return jnp.sum(x, axis=0, keepdims=True)


def _prenorm_fwd(x, g, token=None):
    def fn(i, rv, hv, fv):
        return [rv[0] * _rms(rv[0]) * fv[0]], []
    return _rows(fn, "prenorm_fwd", 256, [(x, D, 0)], fulls=[g] + ([] if token is None else [token]), outs=[(D, BF16)])[0]


def _gm_mask():
    r = lax.broadcasted_iota(jnp.int32, (GM_B, GM_B), 0) // CHUNK
    c = lax.broadcasted_iota(jnp.int32, (GM_B, GM_B), 1) // CHUNK
    return c <= r


def _gm_norm(v, g, b):
    mu = jnp.mean(v, axis=-1, keepdims=True)
    vc = v - mu
    rs = lax.rsqrt(jnp.mean(vc * vc, axis=-1, keepdims=True) + EPS)
    vh = vc * rs
    return vh, rs, vh * g + b


def _gm_sv(vn, ws, bst):
    mask = _gm_mask()
    gw = GM_W // GM_G
    parts = []
    for g in range(GM_G):
        wm = jnp.where(mask, ws[g], 0.0).astype(BF16)
        parts.append(jnp.dot(wm, vn[:, g * gw:(g + 1) * gw].astype(BF16), preferred_element_type=F32)
                     + bst[:, g:g + 1])
    return jnp.concatenate(parts, axis=1)


def _gmlp_fwd(proj, ln_g, ln_b, ws, bst):
    def fn(i, rv, hv, fv):
        u, v, z = rv
        g, b, w, bt = fv
        _, _, vn = _gm_norm(v, g, b)
        return [u * _gm_sv(vn, w, bt) * _silu(z)], []
    return _rows(fn, "gmlp_fwd", GM_B, [(proj, GM_W, 0), (proj, GM_W, 1), (proj, GM_W, 2)],
                 fulls=[ln_g, ln_b, ws, bst], outs=[(GM_W, BF16)])[0]


def _mla_prep_fwd(proj, qg, kvg):
    def fn(i, rv, hv, fv):
        cq, ckv = rv
        g1, g2 = fv
        return [cq * _rms(cq) * g1, ckv * _rms(ckv) * g2], []
    return _rows(fn, "mla_prep_fwd", 256, [(proj, QR, O_CQ // QR), (proj, KVR, O_CKV // KVR)],
                 fulls=[qg, kvg], outs=[(QR, BF16), (KVR, BF16)])


def _rot(t, cc, sa, sb):
    return t * cc + pltpu.roll(t, 32, 1) * sa + pltpu.roll(t, 96, 1) * sb


def _rot_t(g, cc, sa, sb):
    return g * cc + pltpu.roll(g * sa, 96, 1) + pltpu.roll(g * sb, 32, 1)


def _rope_tables():
    pos = jnp.arange(S, dtype=F32)
    inv_freq = ROPE_THETA ** (-jnp.arange(0, ROPE, 2, dtype=F32) / ROPE)
    ang = pos[:, None] * inv_freq[None, :]
    cos, sin, z = jnp.cos(ang), jnp.sin(ang), jnp.zeros((S, 32), F32)
    cc = jnp.concatenate([cos, cos, z, z], axis=1)
    sa = jnp.concatenate([z, sin, z, z], axis=1)
    sb = jnp.concatenate([-sin, z, z, z], axis=1)
    return cc, sa, sb


ATT_SCALE = 1.0 / math.sqrt(NOPE + ROPE)


def _rope_fwd(q, kv, proj, tabs):
    def fn(i, rv, hv, fv):
        qb, kvb, kr, cc, sa, sb = rv
        krr = _rot(kr, cc, sa, sb)
        qs, ks = [], []
        for h in range(H):
            qs += [qb[:, h * HP:h * HP + 128] * ATT_SCALE, _rot(qb[:, h * HP + 128:(h + 1) * HP], cc, sa, sb) * ATT_SCALE]
            ks += [kvb[:, h * 128:(h + 1) * 128], krr]
        kc = jnp.concatenate(ks, axis=1)
        vv = kvb[:, H * NOPE:]
        return [jnp.concatenate(qs, axis=1), kc, kc, vv, vv], []
    cc, sa, sb = tabs
    return _rows(fn, "rope_fwd", 256,
                 [(q, H * HP, 0), (kv, H * 256, 0), (proj, 128, O_KR // 128), (cc, 128, 0), (sa, 128, 0), (sb, 128, 0)],
                 outs=[(H * HP, BF16), (H * HP, BF16), (H * HP, BF16, "T"), (MLA_W, BF16), (MLA_W, BF16, "T")])


TQ, TC, ATT_NB = 512, 512, 1
ATT_KB = TC * ATT_NB
_NT = (((1,), (1,)), ((), ()))


def _attn_allowed(i, kc):
    kpos = kc * TC + lax.broadcasted_iota(jnp.int32, (TC, TQ), 0)
    qpos = i * TQ + lax.broadcasted_iota(jnp.int32, (TC, TQ), 1)
    return (kpos // CHUNK) <= (qpos // CHUNK)


def _attn_fwd(qc, kc, vt):
    def body(q_ref, k_ref, vt_ref, o_ref, l_ref):
        i = pl.program_id(1)
        q = q_ref[...]

        def scores(sb):
            t0s = [pl.multiple_of((sb * ATT_NB + c) * TC, TC) for c in range(ATT_NB)]
            return [lax.dot_general(k_ref[pl.ds(t0, TC), :], q, _NT, preferred_element_type=F32) for t0 in t0s]

        def block(sb, ss, carry, masked):
            m, l, acc = carry
            t0s = [pl.multiple_of((sb * ATT_NB + c) * TC, TC) for c in range(ATT_NB)]
            if masked:
                ss = [jnp.where(_attn_allowed(i, sb * ATT_NB + c), s, -1e30) for c, s in enumerate(ss)]
            m_new = m
            for s in ss:
                m_new = jnp.maximum(m_new, jnp.max(s, axis=0, keepdims=True))
            alpha = jnp.exp(m - m_new)
            ps = [jnp.exp(s - m_new) for s in ss]
            l = alpha * l
            acc = alpha * acc
            for t0, p in zip(t0s, ps):
                l = l + jnp.sum(p, axis=0, keepdims=True)
                acc = acc + jnp.dot(vt_ref[:, pl.ds(t0, TC)], p.astype(BF16), preferred_element_type=F32)
            return m_new, l, acc

        nsb = ((i + 1) * TQ + ATT_KB - 1) // ATT_KB
        c = (jnp.full((1, TQ), -1e30, F32), jnp.zeros((1, TQ), F32), jnp.zeros((VDIM, TQ), F32))

        def step(sb, sc):
            nxt = scores(sb + 1)
            return nxt, block(sb, sc[0], sc[1], False)

        ss, c = lax.fori_loop(0, nsb - 1, step, (scores(0), c))
        m, l, acc = block(nsb - 1, ss, c, True)
        o_ref[...] = (acc / l).T
        l_ref[...] = m + jnp.log(l)

    return pl.pallas_call(
        body, name="attn_fwd", grid=(H, S // TQ),
        in_specs=[pl.BlockSpec((TQ, HP), lambda h, i: (i, h)),
                  pl.BlockSpec((S, HP), lambda h, i: (0, h)),
                  pl.BlockSpec((VDIM, S), lambda h, i: (h, 0))],
        out_specs=[pl.BlockSpec((TQ, VDIM), lambda h, i: (i, h)), pl.BlockSpec((None, 1, TQ), lambda h, i: (h, 0, i))],
        out_shape=[jax.ShapeDtypeStruct((S, MLA_W), F32), jax.ShapeDtypeStruct((H, 1, S), F32)],
        compiler_params=pltpu.CompilerParams(dimension_semantics=("parallel", "arbitrary"),
                                             vmem_limit_bytes=24 * MIB),
    )(qc, kc, vt)


def _gate_mul_fwd(name, val, proj, width, cb):
    def fn(i, rv, hv, fv):
        o, z = rv
        return [o * _silu(z)], []
    return _rows(fn, name, 256, [(val, width, 0), (proj, width, cb)], outs=[(width, BF16)])[0]


def _conv_fwd(proj, w, b):
    def fn(i, rv, hv, fv):
        (xb,), (halo,), (ww, bb) = rv, hv, fv
        halo = jnp.where(i > 0, halo, 0.0)
        row = lax.broadcasted_iota(jnp.int32, xb.shape, 0)
        acc = bb + ww[3:4] * xb
        for s in range(1, CONV_W):
            acc = acc + ww[3 - s:4 - s] * _shift_down(xb, halo, s, row)
        return [acc, acc], []
    return _rows(fn, "conv_fwd", LRU_TM, [(proj, LRU_W, O_XC // LRU_W)], halos=[(proj, LRU_W, O_XC // LRU_W, "prev")],
                 fulls=[w, b], outs=[(LRU_W, F32), (LRU_W, BF16)])


def _lru_terms(ga, gx, xc, ba, bx, lam):
    r = _sig(ga + ba)
    ig = _sig(gx + bx)
    sp = jnp.maximum(-lam, 0.0) + jnp.log(1.0 + jnp.exp(-jnp.abs(lam)))
    log_a = -LRU_C * r * sp
    a = jnp.exp(log_a)
    e2 = jnp.exp(2.0 * log_a)
    om = 1.0 - e2
    mult = jnp.sqrt(jnp.maximum(om, 0.0))
    return r, ig, sp, a, e2, om, mult


def _lru_gates_fwd(gates, xc, ba, bx, lam):
    def fn(i, rv, hv, fv):
        ga, gx, x = rv
        r, ig, sp, a, e2, om, mult = _lru_terms(ga, gx, x, *fv)
        return [a, mult * (ig * x)], []
    return _rows(fn, "lru_gates_fwd", LRU_TM, [(gates, LRU_W, 0), (gates, LRU_W, 1), (xc, LRU_W, 0)],
                 fulls=[ba, bx, lam], outs=[(LRU_W, F32), (LRU_W, F32)])


SCAN_T, SCAN_CW = 64, 256
LRU_TM = 256


def _scan_fwd(a, b):
    def body(a_ref, b_ref, h_ref):
        row = lax.broadcasted_iota(jnp.int32, (SCAN_T, SCAN_CW), 0)

        def step(blk, hc):
            t0 = pl.multiple_of(blk * SCAN_T, SCAN_T)
            A = a_ref[pl.ds(t0, SCAN_T), :]
            B = b_ref[pl.ds(t0, SCAN_T), :]
            d = 1
            while d < SCAN_T:
                keep = row >= d
                A_s = jnp.where(keep, pltpu.roll(A, d, 0), 1.0)
                B_s = jnp.where(keep, pltpu.roll(B, d, 0), 0.0)
                B = A * B_s + B
                A = A * A_s
                d *= 2
            hh = A * hc + B
            h_ref[pl.ds(t0, SCAN_T), :] = hh
            return hh[SCAN_T - 1:SCAN_T, :]

        lax.fori_loop(0, S // SCAN_T, step, jnp.zeros((1, SCAN_CW), F32))

    spec = pl.BlockSpec((S, SCAN_CW), lambda j: (0, j))
    return pl.pallas_call(
        body, name="scan_fwd", grid=(LRU_W // SCAN_CW,), in_specs=[spec, spec], out_specs=spec,
        out_shape=jax.ShapeDtypeStruct((S, LRU_W), F32),
        compiler_params=pltpu.CompilerParams(dimension_semantics=("parallel",),
                                             vmem_limit_bytes=_vmem(3 * _nbytes((S, SCAN_CW), F32))),
    )(a, b)


def _merge_fwd(pa, pb, pc, proj):
    def fn(i, rv, hv, fv):
        a, b, c, ga, gb, gc = rv
        return [_sig(ga) * a + _sig(gb) * b + _sig(gc) * c], []
    return _rows(fn, "merge_fwd", 256,
                 [(pa, D, 0), (pb, D, 0), (pc, D, 0), (proj, D, O_GA // D), (proj, D, O_GB // D), (proj, D, O_GC // D)],
                 outs=[(D, BF16)])[0]


def _post_fwd(x, o2, g):
    def fn(i, rv, hv, fv):
        xb, ob = rv
        return [xb + ob * _rms(ob) * fv[0]], []
    return _rows(fn, "post_fwd", 256, [(x, D, 0), (o2, D, 0)], fulls=[g], outs=[(D, F32)])[0]


SB = 640
BD_TM = 512


def _bd_fwd(xcb, wsb, l):
    def body(x_ref, w_ref, o_ref):
        o_ref[...] = jnp.dot(x_ref[...], w_ref[...], preferred_element_type=F32).astype(o_ref.dtype)

    return pl.pallas_call(
        body, name="lru_gate_mm", grid=(S // BD_TM, 4),
        in_specs=[pl.BlockSpec((BD_TM, SB), lambda i, q: (i, q % 2)),
                  pl.BlockSpec((None, None, SB, SB), lambda i, q: (l, q, 0, 0))],
        out_specs=pl.BlockSpec((BD_TM, SB), lambda i, q: (i, q)),
        out_shape=jax.ShapeDtypeStruct((S, 2 * LRU_W), BF16),
        compiler_params=pltpu.CompilerParams(dimension_semantics=("parallel", "parallel"), vmem_limit_bytes=VMEM_LIMIT),
    )(xcb, wsb)


def _bd_dx(dgates, wsb, l):
    def body(d_ref, w_ref, o_ref, acc_ref):
        g = pl.program_id(2)

        @pl.when(g == 0)
        def _():
            acc_ref[...] = jnp.zeros_like(acc_ref)

        acc_ref[...] += lax.dot_general(d_ref[...], w_ref[...], (((1,), (1,)), ((), ())), preferred_element_type=F32)

        @pl.when(g == 1)
        def _():
            o_ref[...] = acc_ref[...].astype(o_ref.dtype)

    return pl.pallas_call(
        body, name="lru_gate_dx", grid=(S // BD_TM, 2, 2),
        in_specs=[pl.BlockSpec((BD_TM, SB), lambda i, s, g: (i, 2 * g + s)),
                  pl.BlockSpec((None, None, SB, SB), lambda i, s, g: (l, 2 * g + s, 0, 0))],
        out_specs=pl.BlockSpec((BD_TM, SB), lambda i, s, g: (i, s)),
        out_shape=jax.ShapeDtypeStruct((S, LRU_W), BF16),
        scratch_shapes=[pltpu.VMEM((BD_TM, SB), F32)],
        compiler_params=pltpu.CompilerParams(dimension_semantics=("parallel", "parallel", "arbitrary"),
                                             vmem_limit_bytes=VMEM_LIMIT),
    )(dgates, wsb)


def _bd_dw(xcb, dgates):
    tk = 1024

    def body(x_ref, d_ref, o_ref):
        @pl.when(pl.program_id(1) == 0)
        def _():
            o_ref[...] = jnp.zeros_like(o_ref)

        o_ref[...] += lax.dot_general(x_ref[...], d_ref[...], (((0,), (0,)), ((), ())), preferred_element_type=F32)

    return pl.pallas_call(
        body, name="lru_gate_dw", grid=(4, S // tk),
        in_specs=[pl.BlockSpec((tk, SB), lambda q, k: (k, q % 2)), pl.BlockSpec((tk, SB), lambda q, k: (k, q))],
        out_specs=pl.BlockSpec((None, SB, SB), lambda q, k: (q, 0, 0)),
        out_shape=jax.ShapeDtypeStruct((4, SB, SB), F32),
        compiler_params=pltpu.CompilerParams(dimension_semantics=("parallel", "arbitrary"), vmem_limit_bytes=VMEM_LIMIT),
    )(xcb, dgates)


def _bd_extract(dwsb):
    def body(w_ref, o_ref):
        lane = lax.broadcasted_iota(jnp.int32, (LRU_BW, 128), 1)
        for q in range(4):
            for kk in range(8):
                c0 = LRU_BW * kk
                w0, off = (c0 // 128) * 128, c0 % 128
                rows = pl.ds(LRU_BW * kk, LRU_BW)
                blk = w_ref[q, rows, w0:w0 + 128]
                if off:
                    blk = pltpu.roll(blk, 128 - off, 1)
                    if off + LRU_BW > 128:
                        nxt = pltpu.roll(w_ref[q, rows, w0 + 128:w0 + 256], 128 - off, 1)
                        blk = jnp.where(lane < 128 - off, blk, nxt)
                o_ref[q // 2, 8 * (q % 2) + kk] = blk.astype(BF16)

    return pl.pallas_call(
        body, name="lru_gate_dw_blocks",
        in_specs=[pl.BlockSpec(memory_space=pltpu.VMEM)], out_specs=pl.BlockSpec(memory_space=pltpu.VMEM),
        out_shape=jax.ShapeDtypeStruct((2, LRU_NB, LRU_BW, 128), BF16),
        compiler_params=pltpu.CompilerParams(vmem_limit_bytes=VMEM_LIMIT),
    )(dwsb)


def _layer_fwd(x, P, l, tabs, token=None, late=None):
    A = {"x": x}
    A["h"] = _prenorm_fwd(x, P["pre_g"], token)
    proj = A["proj"] = _mm(A["h"], P["wp"], "nt", "in_proj", out_dtype=BF16, tm=1024)
    A["ya"] = _gmlp_fwd(proj, P["ln_g"], P["ln_b"], P["ws"], P["bst"])
    A["xc"], A["xcb"] = _conv_fwd(proj, P["conv_w"], P["conv_b"])
    A["gates"] = _bd_fwd(A["xcb"], P["wsb"], l)
    A["a"], bterm = _lru_gates_fwd(A["gates"], A["xc"], P["ba"], P["bx"], P["lam"])
    A["hs"] = _scan_fwd(A["a"], bterm)
    A["yc"] = _gate_mul_fwd("yc_fwd", A["hs"], proj, LRU_W, O_ZC // LRU_W)
    if late is not None:
        P.update(late(A["yc"]))
    A["cqn"], A["ckvn"] = _mla_prep_fwd(proj, P["qg"], P["kvg"])
    q = _mm(A["cqn"], P["wuq"], "nt", "q_up", out_dtype=BF16)
    kv = _mm(A["ckvn"], P["wukv"], "nt", "kv_up", out_dtype=BF16)
    A["qc"], A["kc"], A["kct"], A["vv"], vt = _rope_fwd(q, kv, proj, tabs)
    A["o"], A["lse"] = _attn_fwd(A["qc"], A["kc"], vt)
    A["yb"] = _gate_mul_fwd("yb_fwd", A["o"], proj, MLA_W, O_ZB // MLA_W)
    A["pa"] = _mm(A["ya"], P["wpa"], "nn", "proj_a", out_dtype=BF16)
    A["pb"] = _mm(A["yb"], P["wpb"], "nn", "proj_b", out_dtype=BF16)
    A["pc"] = _mm(A["yc"], P["wpc"], "nn", "proj_c", out_dtype=BF16)
    A["merged"] = _merge_fwd(A["pa"], A["pb"], A["pc"], proj)
    A["o2"] = _mm(A["merged"], P["wout"], "nn", "out_proj")
    return _post_fwd(x, A["o2"], P["post_g"]), A


def _loss_fwd(y, tgt):
    def fn(i, rv, hv, fv):
        yb, tb = rv
        e = yb - tb
        part = 0.5 * jnp.sum(jnp.mean(e * e, axis=-1, keepdims=True), axis=0, keepdims=True)
        return [e * (1.0 / D)], [part]
    return _rows(fn, "loss", 256, [(y, D, 0), (tgt, D, 0)], outs=[(D, F32)], accs=[(1, 1)])


def _post_bwd(dxn, o2, g, token=None):
    def fn(i, rv, hv, fv):
        dy, ob = rv
        dx, dg = _rms_bwd(dy, ob, fv[0])
        return [dx], [_colsum(dg)]
    return _rows(fn, "post_bwd", 256, [(dxn, D, 0), (o2, D, 0)], fulls=[g] + ([] if token is None else [token]),
                 outs=[(D, BF16)], accs=[(1, D)])


def _merge_bwd(dm, pa, pb, pc, proj, dproj):
    def fn(i, rv, hv, fv):
        d, a, b, c, ga, gb, gc = rv
        outs_p, outs_g = [], []
        for p, gg in ((a, ga), (b, gb), (c, gc)):
            s = _sig(gg)
            outs_p.append(d * s)
            outs_g.append(d * p * s * (1.0 - s))
        return outs_p + [jnp.concatenate(outs_g, axis=1)], []
    return _rows(fn, "merge_bwd", 256,
                 [(dm, D, 0), (pa, D, 0), (pb, D, 0), (pc, D, 0),
                  (proj, D, O_GA // D), (proj, D, O_GB // D), (proj, D, O_GC // D)],
                 outs=[(D, BF16)] * 3 + [(3 * D, BF16, (dproj, NP, O_GA // (3 * D)))])


def _gmlp_bwd(dya, proj, ln_g, ln_b, ws, bst, dproj):
    gw = GM_W // GM_G

    def fn(i, rv, hv, fv):
        dy, u, v, z = rv
        g, b, w, bt = fv
        vh, rs, vn = _gm_norm(v, g, b)
        sv = _gm_sv(vn, w, bt)
        sz = _silu(z)
        du = dy * sv * sz
        dsv = dy * u * sz
        dz = dy * u * sv * _dsilu(z)
        mask = _gm_mask()
        lane = lax.broadcasted_iota(jnp.int32, (GM_B, 128), 1)
        dvn_parts, dws, dbst = [], [], jnp.zeros((GM_B, 128), F32)
        for k in range(GM_G):
            wm = jnp.where(mask, w[k], 0.0).astype(BF16)
            dsk = dsv[:, k * gw:(k + 1) * gw]
            dskb = dsk.astype(BF16)
            dvn_parts.append(lax.dot_general(wm, dskb, (((0,), (0,)), ((), ())), preferred_element_type=F32))
            dwk = lax.dot_general(dskb, vn[:, k * gw:(k + 1) * gw].astype(BF16), (((1,), (1,)), ((), ())),
                                  preferred_element_type=F32)
            dws.append(jnp.where(mask, dwk, 0.0)[None])
            dbst = dbst + jnp.where(lane == k, jnp.sum(dsk, axis=1, keepdims=True), 0.0)
        dvn = jnp.concatenate(dvn_parts, axis=1)
        dvh = dvn * g
        dv = rs * (dvh - jnp.mean(dvh, axis=-1, keepdims=True) - vh * jnp.mean(dvh * vh, axis=-1, keepdims=True))
        return ([jnp.concatenate([du, dv, dz], axis=1)],
                [jnp.concatenate(dws, axis=0), dbst, _colsum(dvn * vh), _colsum(dvn)])
    return _rows(fn, "gmlp_bwd", GM_B, [(dya, GM_W, 0), (proj, GM_W, 0), (proj, GM_W, 1), (proj, GM_W, 2)],
                 fulls=[ln_g, ln_b, ws, bst], outs=[(3 * GM_W, BF16, (dproj, NP, O_U // (3 * GM_W)))],
                 accs=[(GM_G, GM_B, GM_B), (GM_B, 128), (1, GM_W), (1, GM_W)])


def _yb_bwd(dyb, o, proj, dproj):
    def fn(i, rv, hv, fv):
        dy, ob, z = rv
        do = dy * _silu(z)
        prod = do * ob
        lane = lax.broadcasted_iota(jnp.int32, (dy.shape[0], 128), 1)
        dl = jnp.zeros((dy.shape[0], 128), F32)
        for h in range(H):
            dl = dl + jnp.where(lane == h, jnp.sum(prod[:, h * VDIM:(h + 1) * VDIM], axis=1, keepdims=True), 0.0)
        return [do, dl, dy * ob * _dsilu(z)], []
    return _rows(fn, "yb_bwd", 256, [(dyb, MLA_W, 0), (o, MLA_W, 0), (proj, MLA_W, O_ZB // MLA_W)],
                 outs=[(MLA_W, BF16), (128, F32, "T"), (MLA_W, BF16, (dproj, NP, O_ZB // MLA_W))])


def _attn_bwd(qc, kc, kct, vv, do, lse, dlt):
    def body(q_ref, k_ref, kt_ref, v_ref, do_ref, l_ref, d_ref, dq_ref, dk_ref, dv_ref, dqt_ref):
        h, i = pl.program_id(0), pl.program_id(1)

        @pl.when(i == 0)
        def _():
            dk_ref[...] = jnp.zeros_like(dk_ref)
            dv_ref[...] = jnp.zeros_like(dv_ref)

        q = q_ref[...]
        dob = do_ref[...]
        lse = l_ref[...]
        dl = d_ref[pl.ds(h, 1), :]
        dqt_ref[...] = jnp.zeros_like(dqt_ref)

        def rows_of(sb, c):
            return pl.ds(pl.multiple_of((sb * ATT_NB + c) * TC, TC), TC)

        def front(sb):
            return [(lax.dot_general(k_ref[rows_of(sb, c), :], q, _NT, preferred_element_type=F32),
                     lax.dot_general(v_ref[rows_of(sb, c), :], dob, _NT, preferred_element_type=F32))
                    for c in range(ATT_NB)]

        def block(sb, sd, masked):
            dqt = None
            for c, (s, dp) in enumerate(sd):
                rows = rows_of(sb, c)
                p = jnp.exp(s - lse)
                if masked:
                    p = jnp.where(_attn_allowed(i, sb * ATT_NB + c), p, 0.0)
                ds = (p * (dp - dl)).astype(BF16)
                dk_ref[rows, :] += jnp.dot(ds, q, preferred_element_type=F32)
                dv_ref[rows, :] += jnp.dot(p.astype(BF16), dob, preferred_element_type=F32)
                part = jnp.dot(kt_ref[:, rows], ds, preferred_element_type=F32)
                dqt = part if dqt is None else dqt + part
            dqt_ref[...] += dqt

        def step(sb, sd):
            nxt = front(sb + 1)
            block(sb, sd, False)
            return nxt

        nsb = ((i + 1) * TQ + ATT_KB - 1) // ATT_KB
        sd = lax.fori_loop(0, nsb - 1, step, front(0))
        block(nsb - 1, sd, True)
        dq_ref[...] = dqt_ref[...].T.astype(dq_ref.dtype)

    blk = lambda w: pl.BlockSpec((TQ, w), lambda h, i: (i, h))
    head = lambda w: pl.BlockSpec((S, w), lambda h, i: (0, h))
    return pl.pallas_call(
        body, name="attn_bwd", grid=(H, S // TQ),
        in_specs=[blk(HP), head(HP), pl.BlockSpec((HP, S), lambda h, i: (h, 0)), head(VDIM), blk(VDIM),
                  pl.BlockSpec((None, 1, TQ), lambda h, i: (h, 0, i)), pl.BlockSpec((8, TQ), lambda h, i: (0, i))],
        out_specs=[blk(HP), head(HP), head(VDIM)],
        out_shape=[jax.ShapeDtypeStruct((S, H * HP), BF16), jax.ShapeDtypeStruct((S, H * HP), F32),
                   jax.ShapeDtypeStruct((S, MLA_W), F32)],
        scratch_shapes=[pltpu.VMEM((HP, TQ), F32)],
        compiler_params=pltpu.CompilerParams(dimension_semantics=("parallel", "arbitrary"),
                                             vmem_limit_bytes=28 * MIB),
    )(qc, kc, kct, vv, do, lse, dlt)


def _rope_bwd(dqc, dkc, dvv, tabs):
    def fn(i, rv, hv, fv):
        dq, dk, dv, cc, sa, sb = rv
        qs, ks = [], []
        dkr = jnp.zeros((dq.shape[0], 128), F32)
        for h in range(H):
            qs += [dq[:, h * HP:h * HP + 128] * ATT_SCALE, _rot_t(dq[:, h * HP + 128:(h + 1) * HP], cc, sa, sb) * ATT_SCALE]
            ks.append(dk[:, h * HP:h * HP + 128])
            dkr = dkr + dk[:, h * HP + 128:(h + 1) * HP]
        return [jnp.concatenate(qs, axis=1), jnp.concatenate(ks + [dv], axis=1), _rot_t(dkr, cc, sa, sb)], []
    cc, sa, sb = tabs
    return _rows(fn, "rope_bwd", 256,
                 [(dqc, H * HP, 0), (dkc, H * HP, 0), (dvv, MLA_W, 0), (cc, 128, 0), (sa, 128, 0), (sb, 128, 0)],
                 outs=[(H * HP, BF16), (H * 256, BF16), (128, BF16)])


MLA_GROUP = 1536


def _mla_prep_bwd(dcqn, dckvn, dkr, proj, qg, kvg, dproj):
    def fn(i, rv, hv, fv):
        d1, d2, dk, cq, ckv = rv
        g1, g2 = fv
        dx1, dg1 = _rms_bwd(d1, cq, g1)
        dx2, dg2 = _rms_bwd(d2, ckv, g2)
        zeros = jnp.zeros((d1.shape[0], MLA_GROUP - KVR - 128 - QR), F32)
        return [jnp.concatenate([dx2, dk.astype(F32), dx1, zeros], axis=1)], [_colsum(dg1), _colsum(dg2)]
    return _rows(fn, "mla_prep_bwd", 256,
                 [(dcqn, QR, 0), (dckvn, KVR, 0), (dkr, 128, 0), (proj, QR, O_CQ // QR), (proj, KVR, O_CKV // KVR)],
                 fulls=[qg, kvg], outs=[(MLA_GROUP, BF16, (dproj, NP, O_CKV // MLA_GROUP))], accs=[(1, QR), (1, KVR)])


def _yc_bwd(dyc, hs, proj, dproj):
    def fn(i, rv, hv, fv):
        dy, hh, z = rv
        return [dy * _silu(z), dy * hh * _dsilu(z)], []
    return _rows(fn, "yc_bwd", LRU_TM, [(dyc, LRU_W, 0), (hs, LRU_W, 0), (proj, LRU_W, O_ZC // LRU_W)],
                 outs=[(LRU_W, F32), (LRU_W, BF16, (dproj, NP, O_ZC // LRU_W))])


def _scan_bwd(a, hs, dh):
    nblk = S // SCAN_T

    def body(a_ref, h_ref, dh_ref, da_ref, db_ref):
        row = lax.broadcasted_iota(jnp.int32, (SCAN_T, SCAN_CW), 0)

        def step(j, carry):
            gc, ac = carry
            blk = nblk - 1 - j
            t0 = pl.multiple_of(blk * SCAN_T, SCAN_T)
            av = a_ref[pl.ds(t0, SCAN_T), :]
            A = jnp.where(row < SCAN_T - 1, pltpu.roll(av, SCAN_T - 1, 0), ac)
            B = dh_ref[pl.ds(t0, SCAN_T), :]
            d = 1
            while d < SCAN_T:
                keep = row < SCAN_T - d
                A_s = jnp.where(keep, pltpu.roll(A, SCAN_T - d, 0), 1.0)
                B_s = jnp.where(keep, pltpu.roll(B, SCAN_T - d, 0), 0.0)
                B = A * B_s + B
                A = A * A_s
                d *= 2
            g = A * gc + B
            p0 = pl.multiple_of(jnp.maximum(t0 - 8, 0), 8)
            last = jnp.where(blk > 0, h_ref[pl.ds(p0, 8), :][7:8, :], 0.0)
            h_prev = jnp.where(row >= 1, pltpu.roll(h_ref[pl.ds(t0, SCAN_T), :], 1, 0), last)
            da_ref[pl.ds(t0, SCAN_T), :] = g * h_prev
            db_ref[pl.ds(t0, SCAN_T), :] = g
            return g[0:1, :], av[0:1, :]

        z = jnp.zeros((1, SCAN_CW), F32)
        lax.fori_loop(0, nblk, step, (z, z))

    spec = pl.BlockSpec((S, SCAN_CW), lambda j: (0, j))
    return pl.pallas_call(
        body, name="scan_bwd", grid=(LRU_W // SCAN_CW,), in_specs=[spec] * 3, out_specs=[spec] * 2,
        out_shape=[jax.ShapeDtypeStruct((S, LRU_W), F32)] * 2,
        compiler_params=pltpu.CompilerParams(dimension_semantics=("parallel",),
                                             vmem_limit_bytes=_vmem(5 * _nbytes((S, SCAN_CW), F32))),
    )(a, hs, dh)


def _lru_gates_bwd(da, db, gates, xc, ba, bx, lam):
    def fn(i, rv, hv, fv):
        dav, dbv, ga, gx, x = rv
        bav, bxv, lamv = fv
        r, ig, sp, a, e2, om, mult = _lru_terms(ga, gx, x, bav, bxv, lamv)
        dmult = dbv * ig * x
        dig = dbv * mult * x
        dxc1 = dbv * mult * ig
        dlog_a = dav * a + jnp.where(om > 0.0, dmult * (-e2 / mult), 0.0)
        dr = dlog_a * (-LRU_C * sp)
        dga = dr * r * (1.0 - r)
        dgx = dig * ig * (1.0 - ig)
        dlam = _colsum(dlog_a * (-LRU_C * r)) * (-_sig(-lamv))
        return [jnp.concatenate([dga, dgx], axis=1), dxc1], [_colsum(dga), _colsum(dgx), dlam]
    return _rows(fn, "lru_gates_bwd", LRU_TM,
                 [(da, LRU_W, 0), (db, LRU_W, 0), (gates, LRU_W, 0), (gates, LRU_W, 1), (xc, LRU_W, 0)],
                 fulls=[ba, bx, lam], outs=[(2 * LRU_W, BF16), (LRU_W, F32)], accs=[(1, LRU_W)] * 3)


def _conv_bwd(dxc1, dxc2, proj, w, dproj):
    cb = O_XC // LRU_W

    def fn(i, rv, hv, fv):
        d1, d2, xb = rv
        n1, n2, xprev = hv
        ww = fv[0]
        last = i == S // LRU_TM - 1
        dxc = d1 + d2
        nxt = jnp.where(last, 0.0, n1 + n2)
        xprev = jnp.where(i > 0, xprev, 0.0)
        row = lax.broadcasted_iota(jnp.int32, xb.shape, 0)
        dx = ww[3:4] * dxc
        dws = [None] * CONV_W
        dws[3] = _colsum(dxc * xb)
        for s in range(1, CONV_W):
            dx = dx + ww[3 - s:4 - s] * _shift_up(dxc, nxt, s, row)
            dws[3 - s] = _colsum(dxc * _shift_down(xb, xprev, s, row))
        return [dx], [jnp.concatenate(dws, axis=0), _colsum(dxc)]
    return _rows(fn, "conv_bwd", LRU_TM, [(dxc1, LRU_W, 0), (dxc2, LRU_W, 0), (proj, LRU_W, cb)],
                 halos=[(dxc1, LRU_W, 0, "next"), (dxc2, LRU_W, 0, "next"), (proj, LRU_W, cb, "prev")],
                 fulls=[w], outs=[(LRU_W, BF16, (dproj, NP, cb))], accs=[(CONV_W, LRU_W), (1, LRU_W)])


def _prenorm_bwd(dxn, dh, x, g):
    def fn(i, rv, hv, fv):
        dy, dhh, xb = rv
        dx, dg = _rms_bwd(dhh, xb, fv[0])
        return [dy + dx], [_colsum(dg)]
    return _rows(fn, "prenorm_bwd", 256, [(dxn, D, 0), (dh, D, 0), (x, D, 0)], fulls=[g], outs=[(D, F32)],
                 accs=[(1, D)])


def _layer_bwd(dxn, A, P, l, tabs, token=None, early=None):
    G, GB = {}, {}
    proj = A["proj"]

    def dw(key, a, b, name, **tiles):
        GB[key] = _mm(a, b, "tn", name, out_dtype=BF16, **tiles)

    do2, G["post_g"] = _post_bwd(dxn, A["o2"], P["post_g"], token)
    dm = _mm(do2, P["wout"], "nt", "out_proj_dx", out_dtype=BF16)
    dw("wout", A["merged"], do2, "out_proj_dw")
    dpa, dpb, dpc, dproj = _merge_bwd(dm, A["pa"], A["pb"], A["pc"], proj, None)
    dya = _mm(dpa, P["wpa"], "nt", "proj_a_dx", out_dtype=BF16)
    dw("wpa", A["ya"], dpa, "proj_a_dw")
    dyb = _mm(dpb, P["wpb"], "nt", "proj_b_dx", out_dtype=BF16)
    dw("wpb", A["yb"], dpb, "proj_b_dw")
    dyc = _mm(dpc, P["wpc"], "nt", "proj_c_dx", out_dtype=BF16)
    dw("wpc", A["yc"], dpc, "proj_c_dw")
    dproj, G["ws"], G["bst"], G["ln_g"], G["ln_b"] = _gmlp_bwd(dya, proj, P["ln_g"], P["ln_b"], P["ws"], P["bst"], dproj)
    do, dl, dproj = _yb_bwd(dyb, A["o"], proj, dproj)
    dqc, dkc, dvv = _attn_bwd(A["qc"], A["kc"], A["kct"], A["vv"], do, A["lse"], dl)
    dq, dkv, dkr = _rope_bwd(dqc, dkc, dvv, tabs)
    dcqn = _mm(dq, P["wuq"], "nn", "q_up_dx", out_dtype=BF16)
    dw("wuq", dq, A["cqn"], "q_up_dw")
    dckvn = _mm(dkv, P["wukv"], "nn", "kv_up_dx", out_dtype=BF16)
    dw("wukv", dkv, A["ckvn"], "kv_up_dw")
    dproj, G["qg"], G["kvg"] = _mla_prep_bwd(dcqn, dckvn, dkr, proj, P["qg"], P["kvg"], dproj)
    dhs, dproj = _yc_bwd(dyc, A["hs"], proj, dproj)
    da, db = _scan_bwd(A["a"], A["hs"], dhs)
    dgates, dxc1, G["ba"], G["bx"], G["lam"] = _lru_gates_bwd(da, db, A["gates"], A["xc"], P["ba"], P["bx"], P["lam"])
    dxc2 = _bd_dx(dgates, P["wsb"], l)
    G["wab"] = _bd_extract(_bd_dw(A["xcb"], dgates))
    dproj, G["conv_w"], G["conv_b"] = _conv_bwd(dxc1, dxc2, proj, P["conv_w"], dproj)
    tok = (None, None) if early is None else early(GB)
    dh = _mm(dproj, P["wp"], "nn", "in_proj_dx", tm=1024, tn=1024, token=tok[0])
    dw("wp", dproj, A["h"], "in_proj_dw", tm=1536, tn=1024, token=tok[1])
    dx, G["pre_g"] = _prenorm_bwd(dxn, dh, A["x"], P["pre_g"])
    return dx, G, GB


_ORIG_OFF = [0]
for _s in IN_SIZES:
    _ORIG_OFF.append(_ORIG_OFF[-1] + _s)
_PAD_OFF = {0: O_U, 1: O_V, 2: O_ZA, 3: O_CQ, 4: O_CKV, 5: O_KR, 6: O_ZB, 7: O_XC, 8: O_ZC, 9: O_GA, 10: O_GB, 11: O_GC}
SHARD_IN = N_IN // N_CHIPS


def _pieces_w_in(j):
    lo, hi = SHARD_IN * j, SHARD_IN * (j + 1)
    out = []
    for k in range(len(IN_SIZES)):
        a, b = max(lo, _ORIG_OFF[k]), min(hi, _ORIG_OFF[k + 1])
        if a < b:
            out.append((a - lo, _PAD_OFF[k] + a - _ORIG_OFF[k], b - a))
    return out


def _pieces_uq(j):
    return [(192 * hh, HP * (2 * j + hh), NOPE + ROPE) for hh in range(2)]


def _pieces_ukv(j):
    out = []
    for hh in range(2):
        h = 2 * j + hh
        out += [(256 * hh, NOPE * h, NOPE), (256 * hh + NOPE, H * NOPE + VDIM * h, VDIM)]
    return out


def _pieces_rows(r):
    return lambda j: [(0, r * j, r)]


LAYOUT = {
    "w_in": (SHARD_IN, NP, _pieces_w_in),
    "mla_w_uq": (2 * (NOPE + ROPE), H * HP, _pieces_uq),
    "mla_w_ukv": (2 * (NOPE + VDIM), 2 * H * 128, _pieces_ukv),
    "lru_conv_w": (1, N_CHIPS, _pieces_rows(1)),
    "w_proj_a": (GM_W // N_CHIPS, GM_W, _pieces_rows(GM_W // N_CHIPS)),
    "w_proj_b": (MLA_W // N_CHIPS, MLA_W, _pieces_rows(MLA_W // N_CHIPS)),
    "w_proj_c": (LRU_W // N_CHIPS, LRU_W, _pieces_rows(LRU_W // N_CHIPS)),
    "w_out": (D // N_CHIPS, D, _pieces_rows(D // N_CHIPS)),
}
TRANSPOSED = ("w_in", "mla_w_uq", "mla_w_ukv")


def _superblocks(w_a, w_x):
    w6 = jnp.stack([w_a, w_x], axis=1).reshape(DEPTH, 4, 8, LRU_BW, LRU_BW).astype(BF16)
    bands = [jnp.pad(w6[:, :, k], ((0, 0), (0, 0), (0, 0), (LRU_BW * k, SB - LRU_BW * (k + 1)))) for k in range(8)]
    return jnp.concatenate(bands, axis=2)


_HBM = pl.BlockSpec(memory_space=pltpu.HBM)


def _position():
    return lax.axis_index("x"), lax.axis_index("y"), lax.axis_index("c")


def _allgather(blocks, name):
    n = len(blocks)

    def body(*refs):
        ins, outs = refs[:n], refs[n:2 * n]
        send, recv, lsem = refs[2 * n:]
        x, y, c = _position()
        me, sib = (x, y, c), (x, y, 1 - c)
        chips = [(1 - x, y), (x, 1 - y), (1 - x, 1 - y)]

        def cp(k, a, block, to, src=None):
            dst = outs[a].at[4 * block[0] + 2 * block[1] + block[2]]
            return pltpu.make_async_remote_copy(src_ref=dst if src is None else src, dst_ref=dst,
                                                send_sem=send.at[7 * a + k], recv_sem=recv.at[7 * a + k],
                                                device_id=to, device_id_type=MESH)

        mine = [pltpu.make_async_copy(ins[a], outs[a].at[4 * x + 2 * y + c], lsem.at[a]) for a in range(n)]
        for m in mine:
            m.start()
        first = []
        for a in range(n):
            first.append(cp(0, a, me, sib, src=ins[a]))
            first += [cp(1 + j, a, me, (*chip, c), src=ins[a]) for j, chip in enumerate(chips)]
        for f in first:
            f.start()
        passed = []
        for j, chip in enumerate(chips):
            for a in range(n):
                cp(1 + j, a, (*chip, c), me).wait_recv()
                p = cp(4 + j, a, (*chip, c), sib)
                p.start()
                passed.append(p)
        for a in range(n):
            cp(0, a, sib, me).wait_recv()
            for j, chip in enumerate(chips):
                cp(4 + j, a, (*chip, 1 - c), me).wait_recv()
        for f in first + passed:
            f.wait_send()
        for m in mine:
            m.wait()

    return pl.pallas_call(
        body, name=name,
        out_shape=[jax.ShapeDtypeStruct((8,) + b.shape, b.dtype) for b in blocks],
        in_specs=[_HBM] * n, out_specs=[_HBM] * n,
        scratch_shapes=[pltpu.SemaphoreType.DMA((7 * n,)), pltpu.SemaphoreType.DMA((7 * n,)),
                        pltpu.SemaphoreType.DMA((n,))],
    )(*blocks)


_REL = (2, 1, 3)


def _cut(r):
    return r if r < 32 else (r // 2 + 15) // 16 * 16


def _half_rows(r, c0):
    return _cut(r) if c0 == 0 else r - _cut(r)


def _half_pieces(lay_a, jsrc, c0):
    r = lay_a[0]
    lo, hi = (0, _cut(r)) if c0 == 0 else (_cut(r), r)
    out = []
    for s0, d0, nr in lay_a[2](jsrc):
        a, b = max(s0, lo), min(s0 + nr, hi)
        if a < b:
            out.append((a, d0 + a - s0, b - a))
    return out


def _gather_zeros(names, srcs):
    return [jnp.zeros((LAYOUT[nm][1],) + s.shape[1:], s.dtype) for nm, s in zip(names, srcs)]


def _weights_allgather(names, srcs, name, carry=()):
    n = len(srcs)
    lay = [LAYOUT[nm] for nm in names]
    zeros = _gather_zeros(names, srcs)
    m = len(carry)

    def body(*refs):
        ins, outs = refs[:n], refs[2 * n + m:3 * n + m]
        send, recv, lsem = refs[3 * n + 2 * m:]
        x, y, c = _position()
        j = 2 * x + y
        sib = (x, y, 1 - c)
        chips = [(1 - x, y), (x, 1 - y), (1 - x, 1 - y)]

        def flow(a, k, jsrc, c0, to, from_src):
            cps = []
            for s0, d0, nr in _half_pieces(lay[a], jsrc, c0):
                dst = outs[a].at[pl.ds(d0, nr)]
                src = ins[a].at[pl.ds(s0, nr)] if from_src else dst
                cps.append(pltpu.make_async_remote_copy(src_ref=src, dst_ref=dst, send_sem=send.at[7 * a + k],
                                                        recv_sem=recv.at[7 * a + k], device_id=to, device_id_type=MESH))
            return cps

        def sized(a, k, rows):
            ref = ins[a].at[pl.ds(0, rows)]
            return pltpu.make_async_remote_copy(src_ref=ref, dst_ref=ref, send_sem=send.at[7 * a + k],
                                                recv_sem=recv.at[7 * a + k], device_id=sib, device_id_type=MESH)

        for j0 in range(N_CHIPS):
            for c0 in range(2):
                @pl.when((j == j0) & (c == c0))
                def _(j0=j0, c0=c0):
                    mine = [_half_rows(lay[a][0], c0) for a in range(n)]
                    theirs = [_half_rows(lay[a][0], 1 - c0) for a in range(n)]
                    for a in range(n):
                        for s0, d0, nr in _half_pieces(lay[a], j0, c0):
                            pltpu.make_async_copy(ins[a].at[pl.ds(s0, nr)], outs[a].at[pl.ds(d0, nr)], lsem.at[a]).start()
                    for a in range(n):
                        for cp in flow(a, 0, j0, c0, sib, True):
                            cp.start()
                        for k, chip in enumerate(chips):
                            for cp in flow(a, 1 + k, j0, c0, (*chip, c), True):
                                cp.start()
                    for k in range(3):
                        for a in range(n):
                            if mine[a]:
                                sized(a, 1 + k, mine[a]).wait_recv()
                                for cp in flow(a, 4 + k, j0 ^ _REL[k], c0, sib, False):
                                    cp.start()
                    for a in range(n):
                        if theirs[a]:
                            sized(a, 0, theirs[a]).wait_recv()
                            for k in range(3):
                                sized(a, 4 + k, theirs[a]).wait_recv()
                    for a in range(n):
                        if mine[a]:
                            for k in range(7):
                                sized(a, k, mine[a]).wait_send()
                            ref = ins[a].at[pl.ds(0, mine[a])]
                            pltpu.make_async_copy(ref, ref, lsem.at[a]).wait()

    res = pl.pallas_call(
        body, name=name,
        out_shape=[jax.ShapeDtypeStruct(z.shape, z.dtype) for z in list(zeros) + list(carry)],
        in_specs=[_HBM] * (2 * n + m), out_specs=[_HBM] * (n + m),
        input_output_aliases={n + a: a for a in range(n + m)},
        scratch_shapes=[pltpu.SemaphoreType.DMA((7 * n,)), pltpu.SemaphoreType.DMA((7 * n,)),
                        pltpu.SemaphoreType.DMA((n,))],
    )(*srcs, *zeros, *carry)
    return res[:n], res[n:]


_SEM = pl.BlockSpec(memory_space=pltpu.SEMAPHORE)
_VMEM_TOKEN = pl.BlockSpec(memory_space=pltpu.VMEM)
_TOKEN = jax.ShapeDtypeStruct((8, 128), F32)
_EFFECT = pltpu.SideEffectType.DATAFLOW_SIDE_EFFECTING


def _gather_start(names, srcs, zeros, name, after=None):
    n = len(srcs)
    lay = [LAYOUT[nm] for nm in names]
    extra = [] if after is None else [after]

    def body(*refs):
        ins, lands = refs[:n], refs[n:2 * n]
        send, recv, lsem = refs[2 * n + len(extra):2 * n + len(extra) + 3]
        refs[-1][...] = jnp.zeros_like(refs[-1])
        x, y, c = _position()
        j = 2 * x + y
        chips = [(1 - x, y), (x, 1 - y), (1 - x, 1 - y)]
        for j0 in range(N_CHIPS):
            @pl.when(j == j0)
            def _(j0=j0):
                for a in range(n):
                    for s0, d0, nr in lay[a][2](j0):
                        src, dst = ins[a].at[pl.ds(s0, nr)], lands[a].at[pl.ds(d0, nr)]
                        pltpu.make_async_copy(src, dst, lsem.at[a]).start()
                        for k, chip in enumerate(chips):
                            pltpu.make_async_remote_copy(src_ref=src, dst_ref=dst, send_sem=send.at[3 * a + k],
                                                         recv_sem=recv.at[3 * a + k], device_id=(*chip, c),
                                                         device_id_type=MESH).start()

    sems = [pltpu.SemaphoreType.DMA((3 * n,)), pltpu.SemaphoreType.DMA((3 * n,)), pltpu.SemaphoreType.DMA((n,))]
    hbm = lambda a: pltpu.HBM(a.shape, a.dtype)
    res = pl.pallas_call(
        body, name=name,
        out_shape=sems + [hbm(s) for s in srcs] + [hbm(z) for z in zeros] + [_TOKEN],
        in_specs=[_HBM] * (2 * n) + [pl.BlockSpec(memory_space=pl.ANY)] * len(extra),
        out_specs=[_SEM] * 3 + [_HBM] * (2 * n) + [_VMEM_TOKEN],
        input_output_aliases={a: 3 + a for a in range(2 * n)},
        compiler_params=pltpu.CompilerParams(has_side_effects=_EFFECT),
    )(*[pltpu.with_memory_space_constraint(s, pltpu.HBM) for s in srcs],
      *[pltpu.with_memory_space_constraint(z, pltpu.HBM) for z in zeros], *extra)
    return res[:3], res[3:3 + n], res[3 + n:3 + 2 * n], res[-1]


def _gather_wait(names, sems, srcs, lands, after, name):
    n = len(srcs)
    lay = [LAYOUT[nm] for nm in names]

    def body(*refs):
        ins, zones = refs[:n], refs[n:2 * n]
        send, recv, lsem = refs[2 * n:2 * n + 3]
        x, y, c = _position()
        for a in range(n):
            whole = zones[a].at[pl.ds(0, lay[a][0])]
            for k in range(3):
                cp = pltpu.make_async_remote_copy(src_ref=ins[a], dst_ref=whole, send_sem=send.at[3 * a + k],
                                                  recv_sem=recv.at[3 * a + k], device_id=(x, y, 1 - c),
                                                  device_id_type=MESH)
                cp.wait_send()
                cp.wait_recv()
            pltpu.make_async_copy(ins[a], whole, lsem.at[a]).wait()

    hbm = lambda a: pltpu.HBM(a.shape, a.dtype)
    res = pl.pallas_call(
        body, name=name,
        out_shape=[hbm(s) for s in srcs] + [hbm(z) for z in lands],
        in_specs=[_HBM] * (2 * n) + [_SEM] * 3 + [pl.BlockSpec(memory_space=pl.ANY)], out_specs=[_HBM] * (2 * n),
        input_output_aliases={a: a for a in range(2 * n)},
        compiler_params=pltpu.CompilerParams(has_side_effects=_EFFECT),
    )(*srcs, *lands, *sems, after)
    return res[n:]


def _clip_pieces(lay_a, jsrc, c0):
    h = lay_a[1] // 2
    lo, hi = c0 * h, (c0 + 1) * h
    out = []
    for s0, d0, nr in lay_a[2](jsrc):
        a, b = max(d0, lo), min(d0 + nr, hi)
        if a < b:
            out.append((s0 + a - d0, a, b - a))
    return out


def _rows_of(pieces):
    return sum(nr for _, _, nr in pieces)


def _both_cores(body_for):
    x, y, c = _position()
    j = 2 * x + y
    for j0 in range(N_CHIPS):
        for c0 in range(2):
            @pl.when((j == j0) & (c == c0))
            def _(j0=j0, c0=c0):
                body_for(j0, c0)


STAGE_ROWS = 512


def _staged_copy(src, dst, buf, sem_in, sem_out, rows):
    ch = buf.shape[0]
    for r in range(0, rows, ch):
        nr = min(ch, rows - r)
        stage = buf.at[pl.ds(0, nr)]
        cin = pltpu.make_async_copy(src.at[pl.ds(r, nr)], stage, sem_in)
        cin.start()
        cin.wait()
        cout = pltpu.make_async_copy(stage, dst.at[pl.ds(r, nr)], sem_out)
        cout.start()
        cout.wait()


def _half_to_sibling(names, gl, name, after=None):
    n = len(gl)
    halves = [LAYOUT[nm][1] // 2 for nm in names]
    extra = [] if after is None else [after]

    def body(*refs):
        ins, outs = refs[:n], refs[n + len(extra):2 * n + len(extra)]
        send, recv = refs[2 * n + len(extra):]
        x, y, c = _position()

        def run(j0, c0):
            cps = [pltpu.make_async_remote_copy(src_ref=ins[a].at[pl.ds((1 - c0) * halves[a], halves[a])], dst_ref=outs[a],
                                                send_sem=send.at[a], recv_sem=recv.at[a], device_id=(x, y, 1 - c),
                                                device_id_type=MESH) for a in range(n)]
            for cp in cps:
                cp.start()
            for cp in cps:
                cp.wait()

        _both_cores(run)

    return pl.pallas_call(
        body, name=name,
        out_shape=[jax.ShapeDtypeStruct((halves[a],) + gl[a].shape[1:], gl[a].dtype) for a in range(n)],
        in_specs=[_HBM] * n + [pl.BlockSpec(memory_space=pl.ANY)] * len(extra), out_specs=[_HBM] * n,
        scratch_shapes=[pltpu.SemaphoreType.DMA((n,)), pltpu.SemaphoreType.DMA((n,))],
    )(*gl, *extra)


def _chip_scatter_half(names, parts, name):
    n = len(parts)
    lay = [LAYOUT[nm] for nm in names]
    zeros = [jnp.zeros((N_CHIPS, lay[a][0]) + parts[a].shape[1:], parts[a].dtype) for a in range(n)]

    def body(*refs):
        ins, outs = refs[:n], refs[2 * n:3 * n]
        send, recv = refs[3 * n:3 * n + 2]
        stage, sem_in, sem_out = refs[3 * n + 2:4 * n + 2], refs[4 * n + 2], refs[4 * n + 3]
        x, y, c = _position()
        chips = [(1 - x, y), (x, 1 - y), (1 - x, 1 - y)]

        def run(j0, c0):
            def sized(a, rows):
                return outs[a].at[0, pl.ds(0, rows)]

            for a in range(n):
                base = c0 * (lay[a][1] // 2)
                for k, chip in enumerate(chips):
                    for s0, d0, nr in _clip_pieces(lay[a], j0 ^ _REL[k], c0):
                        pltpu.make_async_remote_copy(
                            src_ref=ins[a].at[pl.ds(d0 - base, nr)], dst_ref=outs[a].at[j0, pl.ds(s0, nr)],
                            send_sem=send.at[3 * a + k], recv_sem=recv.at[3 * a + k],
                            device_id=(*chip, c), device_id_type=MESH).start()
            for a in range(n):
                base = c0 * (lay[a][1] // 2)
                for s0, d0, nr in _clip_pieces(lay[a], j0, c0):
                    _staged_copy(ins[a].at[pl.ds(d0 - base, nr)], outs[a].at[j0, pl.ds(s0, nr)], stage[a],
                                 sem_in.at[a], sem_out.at[a], nr)
            for a in range(n):
                got = _rows_of(_clip_pieces(lay[a], j0, c0))
                for k in range(3):
                    sent = _rows_of(_clip_pieces(lay[a], j0 ^ _REL[k], c0))
                    if sent:
                        pltpu.make_async_remote_copy(src_ref=sized(a, sent), dst_ref=sized(a, sent),
                                                     send_sem=send.at[3 * a + k], recv_sem=recv.at[3 * a + k],
                                                     device_id=(x, y, c), device_id_type=MESH).wait_send()
                    if got:
                        pltpu.make_async_remote_copy(src_ref=sized(a, got), dst_ref=sized(a, got),
                                                     send_sem=send.at[3 * a + k], recv_sem=recv.at[3 * a + k],
                                                     device_id=(x, y, c), device_id_type=MESH).wait_recv()

        _both_cores(run)

    return pl.pallas_call(
        body, name=name,
        out_shape=[jax.ShapeDtypeStruct(z.shape, z.dtype) for z in zeros],
        in_specs=[_HBM] * (2 * n), out_specs=[_HBM] * n, input_output_aliases={n + a: a for a in range(n)},
        scratch_shapes=[pltpu.SemaphoreType.DMA((3 * n,)), pltpu.SemaphoreType.DMA((3 * n,))]
        + [pltpu.VMEM((min(STAGE_ROWS, p.shape[0]),) + p.shape[1:], p.dtype) for p in parts]
        + [pltpu.SemaphoreType.DMA((n,)), pltpu.SemaphoreType.DMA((n,))],
    )(*parts, *zeros)


def _subset_exchange(names, bufs, l, name):
    n = len(bufs)
    lay = [LAYOUT[nm] for nm in names]

    def body(*refs):
        outs = refs[n:2 * n]
        send, recv = refs[2 * n:]
        x, y, c = _position()

        def run(j0, c0):
            for a in range(n):
                for s0, _, nr in _clip_pieces(lay[a], j0, c0):
                    rows = outs[a].at[l, pl.ds(s0, nr)]
                    pltpu.make_async_remote_copy(src_ref=rows, dst_ref=rows, send_sem=send.at[a], recv_sem=recv.at[a],
                                                 device_id=(x, y, 1 - c), device_id_type=MESH).start()
            for a in range(n):
                for c_half, wait_send in ((c0, True), (1 - c0, False)):
                    rows = _rows_of(_clip_pieces(lay[a], j0, c_half))
                    if rows:
                        ref = outs[a].at[l, pl.ds(0, rows)]
                        cp = pltpu.make_async_remote_copy(src_ref=ref, dst_ref=ref, send_sem=send.at[a], recv_sem=recv.at[a],
                                                          device_id=(x, y, 1 - c), device_id_type=MESH)
                        if wait_send:
                            cp.wait_send()
                        else:
                            cp.wait_recv()

        _both_cores(run)

    return pl.pallas_call(
        body, name=name,
        out_shape=[jax.ShapeDtypeStruct(b.shape, b.dtype) for b in bufs],
        in_specs=[_HBM] * n, out_specs=[_HBM] * n, input_output_aliases={a: a for a in range(n)},
        scratch_shapes=[pltpu.SemaphoreType.DMA((n,)), pltpu.SemaphoreType.DMA((n,))],
    )(*bufs)


def _scatter_start(names, gl, name):
    n = len(gl)
    lay = [LAYOUT[nm] for nm in names]
    zones = [lax.empty((N_CHIPS, lay[a][0]) + gl[a].shape[1:], gl[a].dtype) for a in range(n)]

    def body(*refs):
        ins, lands = refs[:n], refs[n:2 * n]
        send, recv, lsem = refs[2 * n:2 * n + 3]
        refs[-1][...] = jnp.zeros_like(refs[-1])
        x, y, c = _position()
        j = 2 * x + y
        chips = [(1 - x, y), (x, 1 - y), (1 - x, 1 - y)]
        for j0 in range(N_CHIPS):
            @pl.when(j == j0)
            def _(j0=j0):
                for a in range(n):
                    for s0, d0, nr in lay[a][2](j0):
                        pltpu.make_async_copy(ins[a].at[pl.ds(d0, nr)], lands[a].at[j0, pl.ds(s0, nr)], lsem.at[a]).start()
                    for k, chip in enumerate(chips):
                        for s0, d0, nr in lay[a][2](j0 ^ _REL[k]):
                            pltpu.make_async_remote_copy(
                                src_ref=ins[a].at[pl.ds(d0, nr)], dst_ref=lands[a].at[j0, pl.ds(s0, nr)],
                                send_sem=send.at[3 * a + k], recv_sem=recv.at[3 * a + k],
                                device_id=(*chip, c), device_id_type=MESH).start()

    sems = [pltpu.SemaphoreType.DMA((3 * n,)), pltpu.SemaphoreType.DMA((3 * n,)), pltpu.SemaphoreType.DMA((n,))]
    hbm = lambda a: pltpu.HBM(a.shape, a.dtype)
    res = pl.pallas_call(
        body, name=name,
        out_shape=sems + [hbm(g) for g in gl] + [hbm(z) for z in zones] + [_TOKEN],
        in_specs=[_HBM] * (2 * n), out_specs=[_SEM] * 3 + [_HBM] * (2 * n) + [_VMEM_TOKEN],
        input_output_aliases={a: 3 + a for a in range(2 * n)},
        compiler_params=pltpu.CompilerParams(has_side_effects=_EFFECT),
    )(*[pltpu.with_memory_space_constraint(g, pltpu.HBM) for g in gl],
      *[pltpu.with_memory_space_constraint(z, pltpu.HBM) for z in zones])
    return res[:3], res[3:3 + n], res[3 + n:3 + 2 * n], res[-1]


def _scatter_wait(names, sems, srcs, lands, after, name):
    n = len(srcs)
    lay = [LAYOUT[nm] for nm in names]

    def body(*refs):
        zones = refs[n:2 * n]
        send, recv, lsem = refs[2 * n:2 * n + 3]
        x, y, c = _position()
        for a in range(n):
            whole = zones[a].at[0, pl.ds(0, lay[a][0])]
            for k in range(3):
                cp = pltpu.make_async_remote_copy(src_ref=whole, dst_ref=whole, send_sem=send.at[3 * a + k],
                                                  recv_sem=recv.at[3 * a + k], device_id=(x, y, 1 - c),
                                                  device_id_type=MESH)
                cp.wait_send()
                cp.wait_recv()
            pltpu.make_async_copy(whole, whole, lsem.at[a]).wait()

    hbm = lambda a: pltpu.HBM(a.shape, a.dtype)
    res = pl.pallas_call(
        body, name=name,
        out_shape=[hbm(s) for s in srcs] + [hbm(z) for z in lands],
        in_specs=[_HBM] * (2 * n) + [_SEM] * 3 + [pl.BlockSpec(memory_space=pl.ANY)], out_specs=[_HBM] * (2 * n),
        input_output_aliases={a: a for a in range(2 * n)},
        compiler_params=pltpu.CompilerParams(has_side_effects=_EFFECT),
    )(*srcs, *lands, *sems, after)
    return res[n:]


def _swap_start(arrs, name):
    n = len(arrs)
    zones = [lax.empty(a.shape, a.dtype) for a in arrs]

    def body(*refs):
        ins, lands = refs[:n], refs[n:2 * n]
        send, recv = refs[2 * n:2 * n + 2]
        refs[-1][...] = jnp.zeros_like(refs[-1])
        x, y, c = _position()
        for a in range(n):
            pltpu.make_async_remote_copy(src_ref=ins[a], dst_ref=lands[a], send_sem=send.at[a], recv_sem=recv.at[a],
                                         device_id=(x, y, 1 - c), device_id_type=MESH).start()

    sems = [pltpu.SemaphoreType.DMA((n,)), pltpu.SemaphoreType.DMA((n,))]
    hbm = lambda a: pltpu.HBM(a.shape, a.dtype)
    res = pl.pallas_call(
        body, name=name,
        out_shape=sems + [hbm(a) for a in arrs] + [hbm(z) for z in zones] + [_TOKEN],
        in_specs=[_HBM] * (2 * n), out_specs=[_SEM] * 2 + [_HBM] * (2 * n) + [_VMEM_TOKEN],
        input_output_aliases={a: 2 + a for a in range(2 * n)},
        compiler_params=pltpu.CompilerParams(has_side_effects=_EFFECT),
    )(*[pltpu.with_memory_space_constraint(a, pltpu.HBM) for a in arrs],
      *[pltpu.with_memory_space_constraint(z, pltpu.HBM) for z in zones])
    return res[:2], res[2:2 + n], res[2 + n:2 + 2 * n], res[-1]


def _swap_wait(sems, srcs, lands, after, name):
    n = len(srcs)

    def body(*refs):
        ins, zones = refs[:n], refs[n:2 * n]
        send, recv = refs[2 * n:2 * n + 2]
        x, y, c = _position()
        for a in range(n):
            cp = pltpu.make_async_remote_copy(src_ref=ins[a], dst_ref=zones[a], send_sem=send.at[a], recv_sem=recv.at[a],
                                              device_id=(x, y, 1 - c), device_id_type=MESH)
            cp.wait_send()
            cp.wait_recv()

    hbm = lambda a: pltpu.HBM(a.shape, a.dtype)
    res = pl.pallas_call(
        body, name=name,
        out_shape=[hbm(s) for s in srcs] + [hbm(z) for z in lands],
        in_specs=[_HBM] * (2 * n) + [_SEM] * 2 + [pl.BlockSpec(memory_space=pl.ANY)], out_specs=[_HBM] * (2 * n),
        input_output_aliases={a: a for a in range(2 * n)},
        compiler_params=pltpu.CompilerParams(has_side_effects=_EFFECT),
    )(*srcs, *lands, *sems, after)
    return res[:n], res[n:]


def _row_tile(r):
    for t in (256, 128, 64, 32, 16, 8):
        if r % t == 0 and r > t:
            return t
    return r


def _pair_add_half(g, rb, c_arr, name):
    hrows, rest = rb.shape[0], rb.shape[1:]
    tr = _row_tile(hrows)
    nb = hrows // tr
    z = (0,) * len(rest)

    def body(c_ref, g_ref, r_ref, o_ref):
        o_ref[...] = (g_ref[...].astype(F32) + r_ref[...].astype(F32)).astype(o_ref.dtype)

    return pl.pallas_call(
        body, name=name,
        grid_spec=pltpu.PrefetchScalarGridSpec(
            num_scalar_prefetch=1, grid=(nb,),
            in_specs=[pl.BlockSpec((tr,) + rest, lambda i, c_ref: (c_ref[0] * nb + i,) + z),
                      pl.BlockSpec((tr,) + rest, lambda i, c_ref: (i,) + z)],
            out_specs=pl.BlockSpec((tr,) + rest, lambda i, c_ref: (i,) + z)),
        out_shape=jax.ShapeDtypeStruct((hrows,) + rest, BF16),
        compiler_params=pltpu.CompilerParams(dimension_semantics=("parallel",), vmem_limit_bytes=VMEM_LIMIT),
    )(c_arr, g, rb)


def _sum_slabs(slabs, l, buf, name):
    m = len(slabs)
    n, R, rest = slabs[0].shape[0], slabs[0].shape[1], slabs[0].shape[2:]
    tr = _row_tile(R)
    z = (0,) * len(rest)

    def body(*refs):
        total = None
        for r_ref in refs[:m]:
            acc = r_ref[0].astype(F32)
            for k in range(1, n):
                acc = acc + r_ref[k].astype(F32)
            total = acc if total is None else total + acc
        refs[-1][...] = total

    if R // tr > 64 and len(rest) == 1 and rest[0] % 256 == 0:
        grid = (rest[0] // 256,)
        in_spec = pl.BlockSpec((n, R, 256), lambda i: (0, 0, i))
        out_spec = pl.BlockSpec((None, R, 256), lambda i: (l, 0, i))
    else:
        grid = (R // tr,)
        in_spec = pl.BlockSpec((n, tr) + rest, lambda i: (0, i) + z)
        out_spec = pl.BlockSpec((None, tr) + rest, lambda i: (l, i) + z)
    in_specs, args, aliases = [in_spec] * m, list(slabs), {}
    if buf is not None:
        in_specs.append(pl.BlockSpec(memory_space=pl.ANY))
        args.append(buf)
        aliases = {m: 0}
    return pl.pallas_call(
        body, name=name, grid=grid, in_specs=in_specs, out_specs=out_spec,
        out_shape=jax.ShapeDtypeStruct((DEPTH, R) + rest, F32), input_output_aliases=aliases,
        compiler_params=pltpu.CompilerParams(
            dimension_semantics=("parallel",),
            vmem_limit_bytes=_vmem(m * _nbytes(in_spec.block_shape, slabs[0].dtype) + _nbytes(out_spec.block_shape, F32),
                                   2 * _nbytes(out_spec.block_shape, F32))),
    )(*args)


def _adam_math(w, g, m, v):
    mn = ADAM_B1 * m + (1.0 - ADAM_B1) * g
    vn = ADAM_B2 * v + (1.0 - ADAM_B2) * (g * g)
    m_hat = mn / (1.0 - ADAM_B1 ** ADAM_STEP)
    v_hat = vn / (1.0 - ADAM_B2 ** ADAM_STEP)
    return -ADAM_LR * (m_hat / (jnp.sqrt(v_hat) + ADAM_EPS) + ADAM_WD * w), mn, vn


def _adamw(w, g, m, v, name):
    L, R, C = w.shape
    tr = _row_tile(R)

    def body(w_ref, g_ref, m_ref, v_ref, d_ref, mo_ref, vo_ref):
        d_ref[...], mo_ref[...], vo_ref[...] = _adam_math(w_ref[...], g_ref[...], m_ref[...], v_ref[...])

    if R // tr > 64 and C % 128 == 0:
        spec, grid = pl.BlockSpec((None, R, 128), lambda l, i: (l, 0, i)), (L, C // 128)
    else:
        spec, grid = pl.BlockSpec((None, tr, C), lambda l, i: (l, i, 0)), (L, R // tr)
    return pl.pallas_call(
        body, name=name, grid=grid, in_specs=[spec] * 4, out_specs=[spec] * 3,
        out_shape=[jax.ShapeDtypeStruct((L, R, C), F32)] * 3,
        compiler_params=pltpu.CompilerParams(dimension_semantics=("parallel", "parallel"),
                                             vmem_limit_bytes=_vmem(7 * _nbytes(spec.block_shape, F32))),
    )(w, g, m, v)


_VMEM_WHOLE = pl.BlockSpec(memory_space=pltpu.VMEM)


def _matrix_update(gath, w, m, v, name):
    K = w.shape[1]

    def body(g0_ref, g1_ref, w_ref, m_ref, v_ref, go_ref, d_ref, mo_ref, vo_ref):
        for l, gr in enumerate((g0_ref, g1_ref)):
            for k in range(K):
                g = gr[0, k].astype(F32)
                for dev in range(1, 8):
                    g = g + gr[dev, k].astype(F32)
                go_ref[l, k] = g
                d_ref[l, k], mo_ref[l, k], vo_ref[l, k] = _adam_math(w_ref[l, k], g, m_ref[l, k], v_ref[l, k])

    return pl.pallas_call(
        body, name=name, in_specs=[_VMEM_WHOLE] * 5, out_specs=[_VMEM_WHOLE] * 4,
        out_shape=[jax.ShapeDtypeStruct(w.shape, F32)] * 4,
        compiler_params=pltpu.CompilerParams(vmem_limit_bytes=32 * MIB),
    )(gath[0], gath[1], w, m, v)


VECS = (("pre_norm_g", D), ("post_norm_g", D), ("gm_ln_g", GM_W), ("gm_ln_b", GM_W), ("mla_q_norm_g", QR),
        ("mla_kv_norm_g", KVR), ("lru_conv_b", LRU_W), ("lru_b_a", LRU_W), ("lru_b_x", LRU_W), ("lru_lambda", LRU_W))
VEC_KEY = {"pre_norm_g": "pre_g", "post_norm_g": "post_g", "gm_ln_g": "ln_g", "gm_ln_b": "ln_b", "mla_q_norm_g": "qg",
           "mla_kv_norm_g": "kvg", "lru_conv_b": "conv_b", "lru_b_a": "ba", "lru_b_x": "bx", "lru_lambda": "lam"}
VEC_ROWS, VEC_W, VEC_ROW0, LOSS_ROW = 16, LRU_W, GM_G, 14


def _pack_rows(LG, loss_part):
    per = len(VECS) + 1
    ins = []
    for G in LG:
        ins += [G[VEC_KEY[n]] for n, _ in VECS] + [G["bst"]]
    ins.append(loss_part)

    def body(*refs):
        o_ref = refs[-1]
        o_ref[...] = jnp.zeros_like(o_ref)
        for l in range(DEPTH):
            base = VEC_ROWS * l
            o_ref[pl.ds(base, 8), pl.ds(0, GM_B)] = refs[per * l + len(VECS)][...].T[:8, :]
            for t, (_, width) in enumerate(VECS):
                o_ref[pl.ds(base + VEC_ROW0 + t, 1), pl.ds(0, width)] = refs[per * l + t][...]
        o_ref[pl.ds(LOSS_ROW, 1), pl.ds(0, 128)] = jnp.broadcast_to(refs[-2][...], (1, 128))

    return pl.pallas_call(
        body, name="pack_rows", in_specs=[_VMEM_WHOLE] * len(ins), out_specs=_VMEM_WHOLE,
        out_shape=jax.ShapeDtypeStruct((DEPTH * VEC_ROWS, VEC_W), F32),
    )(*ins)


def _vector_update(gath, W, M, V):
    names = [n for n, _ in VECS] + ["gm_bs"]
    nw = len(names)

    def body(*refs):
        g_ref = refs[0]
        wr, mr, vr = refs[1:1 + nw], refs[1 + nw:1 + 2 * nw], refs[1 + 2 * nw:1 + 3 * nw]
        outs = refs[1 + 3 * nw:]
        s = g_ref[0]
        for dev in range(1, 8):
            s = s + g_ref[dev]
        for t, (_, width) in enumerate(VECS):
            for l in range(DEPTH):
                r = VEC_ROWS * l + VEC_ROW0 + t
                g = s[r:r + 1, :width]
                row = (pl.ds(l, 1), slice(None))
                res = (g,) + _adam_math(wr[t][row], g, mr[t][row], vr[t][row])
                for q in range(4):
                    outs[4 * t + q][row] = res[q]
        t = len(VECS)
        for l in range(DEPTH):
            for k in range(GM_G):
                g = s[VEC_ROWS * l + k:VEC_ROWS * l + k + 1, :GM_B]
                row = (l, pl.ds(k, 1), slice(None))
                res = (g,) + _adam_math(wr[t][row], g, mr[t][row], vr[t][row])
                for q in range(4):
                    outs[4 * t + q][row] = res[q]
        outs[4 * nw][...] = s[LOSS_ROW:LOSS_ROW + 1, :128]

    ws = [W[n] for n in names]
    out_shape = []
    for w in ws:
        out_shape += [jax.ShapeDtypeStruct(w.shape, F32)] * 4
    out_shape.append(jax.ShapeDtypeStruct((1, 128), F32))
    res = pl.pallas_call(
        body, name="vector_update", in_specs=[_VMEM_WHOLE] * (1 + 3 * nw), out_specs=[_VMEM_WHOLE] * (4 * nw + 1),
        out_shape=out_shape, compiler_params=pltpu.CompilerParams(vmem_limit_bytes=VMEM_LIMIT),
    )(gath, *ws, *[M[n] for n in names], *[V[n] for n in names])
    return {n: tuple(res[4 * t:4 * t + 4]) for t, n in enumerate(names)}, res[4 * nw]


SHARDED = ("w_in", "mla_w_uq", "mla_w_ukv", "lru_conv_w", "w_proj_a", "w_proj_b", "w_proj_c", "w_out")
FIRST = ("w_in", "lru_conv_w")
LATER = tuple(n for n in SHARDED if n not in FIRST)
COL_SHARDED = ("w_in", "mla_w_uq", "mla_w_ukv", "lru_conv_w")
SMALL = ("pre_norm_g", "gm_ln_g", "gm_ln_b", "gm_ws", "gm_bs", "mla_q_norm_g", "mla_kv_norm_g", "lru_conv_b",
         "lru_w_a", "lru_b_a", "lru_w_x", "lru_b_x", "lru_lambda", "post_norm_g")
WEIGHTS = ("pre_norm_g", "w_in", "gm_ln_g", "gm_ln_b", "gm_ws", "gm_bs", "mla_q_norm_g", "mla_w_uq",
           "mla_kv_norm_g", "mla_w_ukv", "lru_conv_w", "lru_conv_b", "lru_w_a", "lru_b_a", "lru_w_x", "lru_b_x",
           "lru_lambda", "w_proj_a", "w_proj_b", "w_proj_c", "w_out", "post_norm_g")


GB_KEY = {"w_in": "wp", "mla_w_uq": "wuq", "mla_w_ukv": "wukv", "w_proj_a": "wpa", "w_proj_b": "wpb",
          "w_proj_c": "wpc", "w_out": "wout"}


def _prepare(l, gathered, small, wsb):
    P = {GB_KEY[n]: gathered[n] for n in GB_KEY if n in gathered}
    P["conv_w"] = gathered["lru_conv_w"].transpose(1, 0, 2).reshape(CONV_W, LRU_W)
    P["wsb"] = wsb
    row = lambda n: small[n][l][None, :]
    P["pre_g"], P["post_g"] = row("pre_norm_g"), row("post_norm_g")
    P["ln_g"], P["ln_b"] = row("gm_ln_g"), row("gm_ln_b")
    P["ws"] = small["gm_ws"][l]
    P["bst"] = jnp.pad(small["gm_bs"][l].T, ((0, 0), (0, 128 - GM_G)))
    P["qg"], P["kvg"] = row("mla_q_norm_g"), row("mla_kv_norm_g")
    P["conv_b"], P["ba"], P["bx"], P["lam"] = row("lru_conv_b"), row("lru_b_a"), row("lru_b_x"), row("lru_lambda")
    return P


def kernel(x, pre_norm_g, w_in, gm_ln_g, gm_ln_b, gm_ws, gm_bs, mla_q_norm_g, mla_w_uq, mla_kv_norm_g, mla_w_ukv, lru_conv_w, lru_conv_b, lru_w_a, lru_b_a, lru_w_x, lru_b_x, lru_lambda, w_proj_a, w_proj_b, w_proj_c, w_out, post_norm_g, loss_target, m_pre_norm_g, m_w_in, m_gm_ln_g, m_gm_ln_b, m_gm_ws, m_gm_bs, m_mla_q_norm_g, m_mla_w_uq, m_mla_kv_norm_g, m_mla_w_ukv, m_lru_conv_w, m_lru_conv_b, m_lru_w_a, m_lru_b_a, m_lru_w_x, m_lru_b_x, m_lru_lambda, m_w_proj_a, m_w_proj_b, m_w_proj_c, m_w_out, m_post_norm_g, v_pre_norm_g, v_w_in, v_gm_ln_g, v_gm_ln_b, v_gm_ws, v_gm_bs, v_mla_q_norm_g, v_mla_w_uq, v_mla_kv_norm_g, v_mla_w_ukv, v_lru_conv_w, v_lru_conv_b, v_lru_w_a, v_lru_b_a, v_lru_w_x, v_lru_b_x, v_lru_lambda, v_w_proj_a, v_w_proj_b, v_w_proj_c, v_w_out, v_post_norm_g):
    args = dict(locals())
    W = {n: args[n] for n in WEIGHTS}
    M = {n: args["m_" + n] for n in WEIGHTS}
    V = {n: args["v_" + n] for n in WEIGHTS}
    c = lax.axis_index("c")

    def shards(l, names):
        out = []
        for n in names:
            blk = W[n][l].T if n in TRANSPOSED else W[n][l]
            out.append(blk[None] if n == "lru_conv_w" else blk.astype(BF16))
        return out

    small = {n: W[n] for n in SMALL}
    wsb = _superblocks(W["lru_w_a"], W["lru_w_x"])
    tabs = _rope_tables()
    s0a, s0b, s1a, s1b = shards(0, FIRST), shards(0, LATER), shards(1, FIRST), shards(1, LATER)
    g0, zones = _weights_allgather(FIRST, s0a, "weights_allgather_l0", carry=_gather_zeros(LATER, s0b)
                                   + _gather_zeros(FIRST, s1a) + _gather_zeros(LATER, s1b))
    nl, nf = len(LATER), len(FIRST)
    w0b = _gather_start(LATER, s0b, zones[:nl], "weights_gather_start_l0")
    w1a = _gather_start(FIRST, s1a, zones[nl:nl + nf], "weights_gather_start_l1_first", after=w0b[3])
    w1b = _gather_start(LATER, s1b, zones[nl + nf:], "weights_gather_start_l1_later", after=w1a[3])

    def late(started, name):
        def wait(after):
            got = _gather_wait(LATER, *started[:3], after, name)
            return {GB_KEY[n]: g for n, g in zip(LATER, got)}
        return wait

    P = [_prepare(0, dict(zip(FIRST, g0)), small, wsb), None]
    h0 = x[0]
    h1, A0 = _layer_fwd(h0, P[0], 0, tabs, w1b[3], late(w0b, "weights_gather_wait_l0"))
    g1 = _gather_wait(FIRST, *w1a[:3], h1, "weights_gather_wait_l1_first")
    P[1] = _prepare(1, dict(zip(FIRST, g1)), small, wsb)
    h2, A1 = _layer_fwd(h1, P[1], 1, tabs, None, late(w1b, "weights_gather_wait_l1_later"))
    dy, loss_part = _loss_fwd(h2, loss_target[0])

    def large_grads(G, GB, names):
        conv = G["conv_w"].reshape(CONV_W, N_CHIPS, LRU_W // N_CHIPS).transpose(1, 0, 2)
        return [conv if n == "lru_conv_w" else GB[GB_KEY[n]] for n in names]

    d1, G1, GB1 = _layer_bwd(dy, A1, P[1], 1, tabs)
    sc1 = _scatter_start(SHARDED, large_grads(G1, GB1, SHARDED), "grads_scatter_start_l1")
    started = {}

    def early0(GB):
        mine1 = _scatter_wait(SHARDED, *sc1[:3], GB["wukv"], "grads_scatter_wait_l1")
        started["swap1"] = _swap_start(mine1, "partials_swap_start_l1")
        started["sc0"] = _scatter_start(LATER, [GB[GB_KEY[n]] for n in LATER], "grads_scatter_start_l0")
        return started["sc0"][3], started["swap1"][3]

    d0, G0, GB0 = _layer_bwd(d1, A0, P[0], 0, tabs, sc1[3], early0)
    LG = (G0, G1)
    mine0 = _scatter_wait(LATER, *started["sc0"][:3], d0, "grads_scatter_wait_l0")
    swap0 = _swap_start(mine0, "partials_swap_start_l0")
    g0f = large_grads(G0, GB0, FIRST)
    c_arr = jnp.reshape(c, (1,)).astype(jnp.int32)
    from_sib = _half_to_sibling(FIRST, g0f, "grads_half_to_sibling_l0", after=swap0[3])
    pair = [_pair_add_half(g, rb, c_arr, "pair_add_" + n) for n, g, rb in zip(FIRST, g0f, from_sib)]
    slabs = _chip_scatter_half(FIRST, pair, "grads_chip_scatter_l0")
    mine1, theirs1 = _swap_wait(*started["swap1"][:3], slabs[0], "partials_swap_wait_l1")
    both = dict(zip(SHARDED, [_sum_slabs([a, b], 1, None, "sum_partials_l1_" + n)
                              for n, a, b in zip(SHARDED, mine1, theirs1)]))
    for n, s in zip(FIRST, slabs):
        both[n] = _sum_slabs([s], 0, both[n], "sum_slabs_l0_" + n)
    done = _subset_exchange(FIRST, [both[n] for n in FIRST], 0, "reduced_rows_to_sibling_l0")
    both.update(zip(FIRST, done))
    mine0, theirs0 = _swap_wait(*swap0[:3], done[0], "partials_swap_wait_l0")
    for n, a, b in zip(LATER, mine0, theirs0):
        both[n] = _sum_slabs([a, b], 0, both[n], "sum_partials_l0_" + n)
    both = [both[n] for n in SHARDED]
    grads = {}
    for n, b in zip(SHARDED, both):
        if n in TRANSPOSED and n != "w_in":
            b = jnp.swapaxes(b, 1, 2)
        grads[n] = b if n == "w_in" else b.reshape(W[n].shape)

    rows = _pack_rows(LG, loss_part)
    mats = []
    for g in LG:
        mats += [g["ws"].astype(BF16), g["wab"][0, :, :, :LRU_BW], g["wab"][1, :, :, :LRU_BW]]
    gath = _allgather([rows] + mats, "small_grads_allgather")
    upd, loss_row = _vector_update(gath[0], W, M, V)
    loss = loss_row[0, 0]
    for k, n in enumerate(("gm_ws", "lru_w_a", "lru_w_x")):
        upd[n] = _matrix_update((gath[1 + k], gath[4 + k]), W[n], M[n], V[n], "update_" + n)

    for n in SHARDED:
        if n == "w_in":
            tr = lambda a: jnp.swapaxes(a, 1, 2)
            res = _adamw(tr(W[n]), grads[n], tr(M[n]), tr(V[n]), "adamw_" + n)
            upd[n] = tuple(tr(a) for a in (grads[n],) + tuple(res))
        else:
            upd[n] = (grads[n],) + tuple(_adamw(W[n], grads[n], M[n], V[n], "adamw_" + n))

    return (loss, d0[None], *[upd[n][0] for n in WEIGHTS], *[upd[n][1] for n in WEIGHTS],
            *[upd[n][2] for n in WEIGHTS], *[upd[n][3] for n in WEIGHTS])
```

```python
import functools
import math

import jax
import jax.numpy as jnp
from jax import lax
from jax.experimental import pallas as pl
from jax.experimental.pallas import tpu as pltpu

F32, BF16 = jnp.float32, jnp.bfloat16
MESH = pl.DeviceIdType.MESH

S, D, DEPTH = 2048, 1024, 2
CHUNK, EPS = 64, 1e-6
GM_W, GM_G, GM_B = 1024, 4, 128
H, NOPE, ROPE, VDIM = 8, 128, 64, 128
QR, KVR = 384, 256
MLA_W = H * VDIM
LRU_W, LRU_NB, LRU_BW, LRU_C, CONV_W = 1280, 16, 80, 8.0, 4
ROPE_THETA = 10000.0
IN_SIZES = (GM_W, GM_W, GM_W, QR, KVR, ROPE, MLA_W, LRU_W, LRU_W, D, D, D)
N_IN = sum(IN_SIZES)
N_CHIPS = 4
ADAM_LR, ADAM_B1, ADAM_B2, ADAM_EPS, ADAM_WD, ADAM_STEP = 0.001, 0.9, 0.999, 1e-08, 0.01, 10

HP = 256
O_U, O_V, O_ZA, O_GA, O_GB, O_GC = 0, 1024, 2048, 3072, 4096, 5120
O_CKV, O_KR, O_CQ, O_XC, O_ZC, O_ZB = 6144, 6400, 6528, 7680, 8960, 10240
NP = 11264
MIB = 1024 * 1024
VMEM_LIMIT = 16 * MIB


def _vmem(block_bytes, temp_bytes=0):
    return int(min(max(2 * block_bytes + temp_bytes + 4 * MIB, VMEM_LIMIT), 56 * MIB))


def _nbytes(shape, dtype):
    return math.prod(d for d in shape if d is not None) * jnp.dtype(dtype).itemsize


def _tile(dim, target):
    if dim <= target:
        return dim
    t = (target // 128) * 128
    while dim % t:
        t -= 128
    return t


def _sig(x):
    return jax.nn.sigmoid(x)


def _silu(x):
    return x * _sig(x)


def _dsilu(x):
    s = _sig(x)
    return s * (1.0 + x * (1.0 - s))


def _mm(a, b, mode, name, out_dtype=F32, tm=1024, tn=1024, tk=1024, b_lead=None, out_lead=None, token=None):
    b2 = b.shape[1:] if b_lead is not None else b.shape
    if mode == "nn":
        (M, K), (K2, N) = a.shape, b2
    elif mode == "nt":
        (M, K), (N, K2) = a.shape, b2
    else:
        (K, M), (K2, N) = a.shape, b2
    assert K == K2, (name, a.shape, b.shape)
    tm, tn, tk = _tile(M, tm), _tile(N, tn), _tile(K, tk)
    nk = K // tk
    if mode == "tn":
        a_spec = pl.BlockSpec((tk, tm), lambda i, j, k: (k, i))
        lhs_c = 0
    else:
        a_spec = pl.BlockSpec((tm, tk), lambda i, j, k: (i, k))
        lhs_c = 1
    b_blk, b_idx, rhs_c = ((tn, tk), (lambda i, j, k: (j, k)), 1) if mode == "nt" else ((tk, tn), (lambda i, j, k: (k, j)), 0)
    if b_lead is None:
        b_spec = pl.BlockSpec(b_blk, b_idx)
    else:
        b_spec = pl.BlockSpec((None,) + b_blk, functools.partial(lambda i, j, k, f, l: (l,) + f(i, j, k), f=b_idx, l=b_lead))
    dims = (((lhs_c,), (rhs_c,)), ((), ()))
    in_specs, args, aliases = [a_spec, b_spec], [a, b], {}
    if out_lead is None:
        out_spec = pl.BlockSpec((tm, tn), lambda i, j, k: (i, j))
        out_shape = jax.ShapeDtypeStruct((M, N), out_dtype)
    else:
        l_out, n_lead, buf = out_lead
        out_spec = pl.BlockSpec((None, tm, tn), functools.partial(lambda i, j, k, l: (l, i, j), l=l_out))
        out_shape = jax.ShapeDtypeStruct((n_lead, M, N), out_dtype)
        if buf is not None:
            in_specs.append(pl.BlockSpec(memory_space=pl.ANY))
            args.append(buf)
            aliases = {2: 0}
    if token is not None:
        in_specs.append(pl.BlockSpec(memory_space=pl.ANY))
        args.append(token)

    def body(a_ref, b_ref, *rest):
        o_ref, acc_ref = rest[-2:]
        k = pl.program_id(2)

        @pl.when(k == 0)
        def _():
            acc_ref[...] = jnp.zeros_like(acc_ref)

        acc_ref[...] += lax.dot_general(a_ref[...].astype(BF16), b_ref[...].astype(BF16), dims,
                                        preferred_element_type=F32)

        @pl.when(k == nk - 1)
        def _():
            o_ref[...] = acc_ref[...].astype(o_ref.dtype)

    return pl.pallas_call(
        body, name=name, grid=(M // tm, N // tn, nk),
        in_specs=in_specs, out_specs=out_spec, out_shape=out_shape,
        scratch_shapes=[pltpu.VMEM((tm, tn), F32)], input_output_aliases=aliases,
        compiler_params=pltpu.CompilerParams(
            dimension_semantics=("parallel", "parallel", "arbitrary"),
            vmem_limit_bytes=_vmem(_nbytes((tm, tk), a.dtype) + _nbytes((tk, tn), b.dtype) + _nbytes((tm, tn), out_dtype),
                                   _nbytes((tm, tn), F32) + _nbytes((tm, tk), BF16) + _nbytes((tk, tn), BF16))),
    )(*args)


def _rows(fn, name, tm, rows, halos=(), fulls=(), outs=(), accs=()):
    n = S // tm
    in_specs, args = [], []
    for arr, w, cb in rows:
        in_specs.append(pl.BlockSpec((tm, w), functools.partial(lambda i, cb: (i, cb), cb=cb)))
        args.append(arr)
    for arr, w, cb, side in halos:
        if side == "prev":
            im = functools.partial(lambda i, cb: (jnp.maximum(i * (tm // 16) - 1, 0), cb), cb=cb)
        else:
            im = functools.partial(lambda i, cb: (jnp.minimum((i + 1) * (tm // 16), S // 16 - 1), cb), cb=cb)
        in_specs.append(pl.BlockSpec((16, w), im))
        args.append(arr)
    for arr in fulls:
        in_specs.append(pl.BlockSpec(arr.shape, functools.partial(lambda i, nd: (0,) * nd, nd=arr.ndim)))
        args.append(arr)
    out_shape, out_specs, aliases, n_alias = [], [], {}, 0
    for k, o in enumerate(outs):
        if len(o) == 3 and o[2] == "T":
            out_shape.append(jax.ShapeDtypeStruct((o[0], S), o[1]))
            out_specs.append(pl.BlockSpec((o[0], tm), lambda i: (0, i)))
        elif len(o) == 3:
            buf, total, cb = o[2]
            out_shape.append(jax.ShapeDtypeStruct((S, total), o[1]))
            out_specs.append(pl.BlockSpec((tm, o[0]), functools.partial(lambda i, cb: (i, cb), cb=cb)))
            if buf is not None:
                aliases[len(args)] = k
                in_specs.append(pl.BlockSpec(memory_space=pl.ANY))
                args.append(buf)
                n_alias += 1
        else:
            out_shape.append(jax.ShapeDtypeStruct((S, o[0]), o[1]))
            out_specs.append(pl.BlockSpec((tm, o[0]), lambda i: (i, 0)))
    for shp in accs:
        out_shape.append(jax.ShapeDtypeStruct(shp, F32))
        out_specs.append(pl.BlockSpec(shp, functools.partial(lambda i, nd: (0,) * nd, nd=len(shp))))
    nr, nh, nf, no, na = len(rows), len(halos), len(fulls), len(outs), len(accs)
    blocks = (sum(_nbytes((tm, w), arr.dtype) for arr, w, _ in rows) + sum(_nbytes(a.shape, a.dtype) for a in fulls)
              + sum(_nbytes((tm, o[0]), o[1]) for o in outs) + sum(_nbytes(shp, F32) for shp in accs))
    widest = _nbytes((tm, max([w for _, w, _ in rows] + [o[0] for o in outs])), F32)

    def body(*refs):
        i = pl.program_id(0)
        ins, orefs = refs[:nr + nh + nf], refs[nr + nh + nf + n_alias:]
        rv = [r[...].astype(F32) for r in ins[:nr]]
        hv = [r[...].astype(F32)[8:] if h[3] == "prev" else r[...].astype(F32)[:8] for r, h in zip(ins[nr:nr + nh], halos)]
        fv = [r[...] for r in ins[nr + nh:]]
        o, a = fn(i, rv, hv, fv)
        assert len(o) == no and len(a) == na, name
        for spec, ref, val in zip(outs, orefs[:no], o):
            ref[...] = (val.T if len(spec) == 3 and spec[2] == "T" else val).astype(ref.dtype)
        if na:
            @pl.when(i == 0)
            def _():
                for ref in orefs[no:]:
                    ref[...] = jnp.zeros_like(ref)

            for ref, val in zip(orefs[no:], a):
                ref[...] += val

    res = pl.pallas_call(
        body, name=name, grid=(n,), in_specs=in_specs, out_specs=out_specs, out_shape=out_shape,
        input_output_aliases=aliases,
        compiler_params=pltpu.CompilerParams(dimension_semantics=("arbitrary",), vmem_limit_bytes=_vmem(blocks, 6 * widest)),
    )(*args)
    return res


def _shift_down(xb, halo, s, row):
    fix = jnp.tile(pltpu.roll(halo, s, 0), (xb.shape[0] // 8, 1))
    return jnp.where(row >= s, pltpu.roll(xb, s, 0), fix)


def _shift_up(xb, halo, s, row):
    tm = xb.shape[0]
    fix = jnp.tile(pltpu.roll(halo, 8 - s, 0), (tm // 8, 1))
    return jnp.where(row < tm - s, pltpu.roll(xb, tm - s, 0), fix)


def _rms(x):
    return lax.rsqrt(jnp.mean(x * x, axis=-1, keepdims=True) + EPS)


def _rms_bwd(dy, x, g):
    r = _rms(x)
    xh = x * r
    dxh = dy * g
    dx = r * (dxh - xh * jnp.mean(dxh * xh, axis=-1, keepdims=True))
    return dx, dy * xh


def _colsum(x):
    return jnp.sum(x, axis=0, keepdims=True)


def _prenorm_fwd(x, g, token=None):
    def fn(i, rv, hv, fv):
        return [rv[0] * _rms(rv[0]) * fv[0]], []
    return _rows(fn, "prenorm_fwd", 256, [(x, D, 0)], fulls=[g] + ([] if token is None else [token]), outs=[(D, BF16)])[0]


def _gm_mask():
    r = lax.broadcasted_iota(jnp.int32, (GM_B, GM_B), 0) // CHUNK
    c = lax.broadcasted_iota(jnp.int32, (GM_B, GM_B), 1) // CHUNK
    return c <= r


def _gm_norm(v, g, b):
    mu = jnp.mean(v, axis=-1, keepdims=True)
    vc = v - mu
    rs = lax.rsqrt(jnp.mean(vc * vc, axis=-1, keepdims=True) + EPS)
    vh = vc * rs
    return vh, rs, vh * g + b


def _gm_sv(vn, ws, bst):
    mask = _gm_mask()
    gw = GM_W // GM_G
    parts = []
    for g in range(GM_G):
        wm = jnp.where(mask, ws[g], 0.0).astype(BF16)
        parts.append(jnp.dot(wm, vn[:, g * gw:(g + 1) * gw].astype(BF16), preferred_element_type=F32)
                     + bst[:, g:g + 1])
    return jnp.concatenate(parts, axis=1)


def _gmlp_fwd(proj, ln_g, ln_b, ws, bst):
    def fn(i, rv, hv, fv):
        u, v, z = rv
        g, b, w, bt = fv
        _, _, vn = _gm_norm(v, g, b)
        return [u * _gm_sv(vn, w, bt) * _silu(z)], []
    return _rows(fn, "gmlp_fwd", GM_B, [(proj, GM_W, 0), (proj, GM_W, 1), (proj, GM_W, 2)],
                 fulls=[ln_g, ln_b, ws, bst], outs=[(GM_W, BF16)])[0]


def _mla_prep_fwd(proj, qg, kvg):
    def fn(i, rv, hv, fv):
        cq, ckv = rv
        g1, g2 = fv
        return [cq * _rms(cq) * g1, ckv * _rms(ckv) * g2], []
    return _rows(fn, "mla_prep_fwd", 256, [(proj, QR, O_CQ // QR), (proj, KVR, O_CKV // KVR)],
                 fulls=[qg, kvg], outs=[(QR, BF16), (KVR, BF16)])


def _rot(t, cc, sa, sb):
    return t * cc + pltpu.roll(t, 32, 1) * sa + pltpu.roll(t, 96, 1) * sb


def _rot_t(g, cc, sa, sb):
    return g * cc + pltpu.roll(g * sa, 96, 1) + pltpu.roll(g * sb, 32, 1)


def _rope_tables():
    pos = jnp.arange(S, dtype=F32)
    inv_freq = ROPE_THETA ** (-jnp.arange(0, ROPE, 2, dtype=F32) / ROPE)
    ang = pos[:, None] * inv_freq[None, :]
    cos, sin, z = jnp.cos(ang), jnp.sin(ang), jnp.zeros((S, 32), F32)
    cc = jnp.concatenate([cos, cos, z, z], axis=1)
    sa = jnp.concatenate([z, sin, z, z], axis=1)
    sb = jnp.concatenate([-sin, z, z, z], axis=1)
    return cc, sa, sb


ATT_SCALE = 1.0 / math.sqrt(NOPE + ROPE)


def _rope_fwd(q, kv, proj, tabs):
    def fn(i, rv, hv, fv):
        qb, kvb, kr, cc, sa, sb = rv
        krr = _rot(kr, cc, sa, sb)
        qs, ks = [], []
        for h in range(H):
            qs += [qb[:, h * HP:h * HP + 128] * ATT_SCALE, _rot(qb[:, h * HP + 128:(h + 1) * HP], cc, sa, sb) * ATT_SCALE]
            ks += [kvb[:, h * 128:(h + 1) * 128], krr]
        kc = jnp.concatenate(ks, axis=1)
        vv = kvb[:, H * NOPE:]
        return [jnp.concatenate(qs, axis=1), kc, kc, vv, vv], []
    cc, sa, sb = tabs
    return _rows(fn, "rope_fwd", 256,
                 [(q, H * HP, 0), (kv, H * 256, 0), (proj, 128, O_KR // 128), (cc, 128, 0), (sa, 128, 0), (sb, 128, 0)],
                 outs=[(H * HP, BF16), (H * HP, BF16), (H * HP, BF16, "T"), (MLA_W, BF16), (MLA_W, BF16, "T")])


TQ, TC, ATT_NB = 512, 512, 1
ATT_KB = TC * ATT_NB
_NT = (((1,), (1,)), ((), ()))


def _attn_allowed(i, kc):
    kpos = kc * TC + lax.broadcasted_iota(jnp.int32, (TC, TQ), 0)
    qpos = i * TQ + lax.broadcasted_iota(jnp.int32, (TC, TQ), 1)
    return (kpos // CHUNK) <= (qpos // CHUNK)


def _attn_fwd(qc, kc, vt):
    def body(q_ref, k_ref, vt_ref, o_ref, l_ref):
        i = pl.program_id(1)
        q = q_ref[...]

        def scores(sb):
            t0s = [pl.multiple_of((sb * ATT_NB + c) * TC, TC) for c in range(ATT_NB)]
            return [lax.dot_general(k_ref[pl.ds(t0, TC), :], q, _NT, preferred_element_type=F32) for t0 in t0s]

        def block(sb, ss, carry, masked):
            m, l, acc = carry
            t0s = [pl.multiple_of((sb * ATT_NB + c) * TC, TC) for c in range(ATT_NB)]
            if masked:
                ss = [jnp.where(_attn_allowed(i, sb * ATT_NB + c), s, -1e30) for c, s in enumerate(ss)]
            m_new = m
            for s in ss:
                m_new = jnp.maximum(m_new, jnp.max(s, axis=0, keepdims=True))
            alpha = jnp.exp(m - m_new)
            ps = [jnp.exp(s - m_new) for s in ss]
            l = alpha * l
            acc = alpha * acc
            for t0, p in zip(t0s, ps):
                l = l + jnp.sum(p, axis=0, keepdims=True)
                acc = acc + jnp.dot(vt_ref[:, pl.ds(t0, TC)], p.astype(BF16), preferred_element_type=F32)
            return m_new, l, acc

        nsb = ((i + 1) * TQ + ATT_KB - 1) // ATT_KB
        c = (jnp.full((1, TQ), -1e30, F32), jnp.zeros((1, TQ), F32), jnp.zeros((VDIM, TQ), F32))

        def step(sb, sc):
            nxt = scores(sb + 1)
            return nxt, block(sb, sc[0], sc[1], False)

        ss, c = lax.fori_loop(0, nsb - 1, step, (scores(0), c))
        m, l, acc = block(nsb - 1, ss, c, True)
        o_ref[...] = (acc / l).T
        l_ref[...] = m + jnp.log(l)

    return pl.pallas_call(
        body, name="attn_fwd", grid=(H, S // TQ),
        in_specs=[pl.BlockSpec((TQ, HP), lambda h, i: (i, h)),
                  pl.BlockSpec((S, HP), lambda h, i: (0, h)),
                  pl.BlockSpec((VDIM, S), lambda h, i: (h, 0))],
        out_specs=[pl.BlockSpec((TQ, VDIM), lambda h, i: (i, h)), pl.BlockSpec((None, 1, TQ), lambda h, i: (h, 0, i))],
        out_shape=[jax.ShapeDtypeStruct((S, MLA_W), F32), jax.ShapeDtypeStruct((H, 1, S), F32)],
        compiler_params=pltpu.CompilerParams(dimension_semantics=("parallel", "arbitrary"),
                                             vmem_limit_bytes=24 * MIB),
    )(qc, kc, vt)


def _gate_mul_fwd(name, val, proj, width, cb):
    def fn(i, rv, hv, fv):
        o, z = rv
        return [o * _silu(z)], []
    return _rows(fn, name, 256, [(val, width, 0), (proj, width, cb)], outs=[(width, BF16)])[0]


def _conv_fwd(proj, w, b):
    def fn(i, rv, hv, fv):
        (xb,), (halo,), (ww, bb) = rv, hv, fv
        halo = jnp.where(i > 0, halo, 0.0)
        row = lax.broadcasted_iota(jnp.int32, xb.shape, 0)
        acc = bb + ww[3:4] * xb
        for s in range(1, CONV_W):
            acc = acc + ww[3 - s:4 - s] * _shift_down(xb, halo, s, row)
        return [acc, acc], []
    return _rows(fn, "conv_fwd", LRU_TM, [(proj, LRU_W, O_XC // LRU_W)], halos=[(proj, LRU_W, O_XC // LRU_W, "prev")],
                 fulls=[w, b], outs=[(LRU_W, F32), (LRU_W, BF16)])


def _lru_terms(ga, gx, xc, ba, bx, lam):
    r = _sig(ga + ba)
    ig = _sig(gx + bx)
    sp = jnp.maximum(-lam, 0.0) + jnp.log(1.0 + jnp.exp(-jnp.abs(lam)))
    log_a = -LRU_C * r * sp
    a = jnp.exp(log_a)
    e2 = jnp.exp(2.0 * log_a)
    om = 1.0 - e2
    mult = jnp.sqrt(jnp.maximum(om, 0.0))
    return r, ig, sp, a, e2, om, mult


def _lru_gates_fwd(gates, xc, ba, bx, lam):
    def fn(i, rv, hv, fv):
        ga, gx, x = rv
        r, ig, sp, a, e2, om, mult = _lru_terms(ga, gx, x, *fv)
        return [a, mult * (ig * x)], []
    return _rows(fn, "lru_gates_fwd", LRU_TM, [(gates, LRU_W, 0), (gates, LRU_W, 1), (xc, LRU_W, 0)],
                 fulls=[ba, bx, lam], outs=[(LRU_W, F32), (LRU_W, F32)])


SCAN_T, SCAN_CW = 64, 256
LRU_TM = 256


def _scan_fwd(a, b):
    def body(a_ref, b_ref, h_ref):
        row = lax.broadcasted_iota(jnp.int32, (SCAN_T, SCAN_CW), 0)

        def step(blk, hc):
            t0 = pl.multiple_of(blk * SCAN_T, SCAN_T)
            A = a_ref[pl.ds(t0, SCAN_T), :]
            B = b_ref[pl.ds(t0, SCAN_T), :]
            d = 1
            while d < SCAN_T:
                keep = row >= d
                A_s = jnp.where(keep, pltpu.roll(A, d, 0), 1.0)
                B_s = jnp.where(keep, pltpu.roll(B, d, 0), 0.0)
                B = A * B_s + B
                A = A * A_s
                d *= 2
            hh = A * hc + B
            h_ref[pl.ds(t0, SCAN_T), :] = hh
            return hh[SCAN_T - 1:SCAN_T, :]

        lax.fori_loop(0, S // SCAN_T, step, jnp.zeros((1, SCAN_CW), F32))

    spec = pl.BlockSpec((S, SCAN_CW), lambda j: (0, j))
    return pl.pallas_call(
        body, name="scan_fwd", grid=(LRU_W // SCAN_CW,), in_specs=[spec, spec], out_specs=spec,
        out_shape=jax.ShapeDtypeStruct((S, LRU_W), F32),
        compiler_params=pltpu.CompilerParams(dimension_semantics=("parallel",),
                                             vmem_limit_bytes=_vmem(3 * _nbytes((S, SCAN_CW), F32))),
    )(a, b)


def _merge_fwd(pa, pb, pc, proj):
    def fn(i, rv, hv, fv):
        a, b, c, ga, gb, gc = rv
        return [_sig(ga) * a + _sig(gb) * b + _sig(gc) * c], []
    return _rows(fn, "merge_fwd", 256,
                 [(pa, D, 0), (pb, D, 0), (pc, D, 0), (proj, D, O_GA // D), (proj, D, O_GB // D), (proj, D, O_GC // D)],
                 outs=[(D, BF16)])[0]


def _post_fwd(x, o2, g):
    def fn(i, rv, hv, fv):
        xb, ob = rv
        return [xb + ob * _rms(ob) * fv[0]], []
    return _rows(fn, "post_fwd", 256, [(x, D, 0), (o2, D, 0)], fulls=[g], outs=[(D, F32)])[0]


SB = 640
BD_TM = 512


def _bd_fwd(xcb, wsb, l):
    def body(x_ref, w_ref, o_ref):
        o_ref[...] = jnp.dot(x_ref[...], w_ref[...], preferred_element_type=F32).astype(o_ref.dtype)

    return pl.pallas_call(
        body, name="lru_gate_mm", grid=(S // BD_TM, 4),
        in_specs=[pl.BlockSpec((BD_TM, SB), lambda i, q: (i, q % 2)),
                  pl.BlockSpec((None, None, SB, SB), lambda i, q: (l, q, 0, 0))],
        out_specs=pl.BlockSpec((BD_TM, SB), lambda i, q: (i, q)),
        out_shape=jax.ShapeDtypeStruct((S, 2 * LRU_W), BF16),
        compiler_params=pltpu.CompilerParams(dimension_semantics=("parallel", "parallel"), vmem_limit_bytes=VMEM_LIMIT),
    )(xcb, wsb)


def _bd_dx(dgates, wsb, l):
    def body(d_ref, w_ref, o_ref, acc_ref):
        g = pl.program_id(2)

        @pl.when(g == 0)
        def _():
            acc_ref[...] = jnp.zeros_like(acc_ref)

        acc_ref[...] += lax.dot_general(d_ref[...], w_ref[...], (((1,), (1,)), ((), ())), preferred_element_type=F32)

        @pl.when(g == 1)
        def _():
            o_ref[...] = acc_ref[...].astype(o_ref.dtype)

    return pl.pallas_call(
        body, name="lru_gate_dx", grid=(S // BD_TM, 2, 2),
        in_specs=[pl.BlockSpec((BD_TM, SB), lambda i, s, g: (i, 2 * g + s)),
                  pl.BlockSpec((None, None, SB, SB), lambda i, s, g: (l, 2 * g + s, 0, 0))],
        out_specs=pl.BlockSpec((BD_TM, SB), lambda i, s, g: (i, s)),
        out_shape=jax.ShapeDtypeStruct((S, LRU_W), BF16),
        scratch_shapes=[pltpu.VMEM((BD_TM, SB), F32)],
        compiler_params=pltpu.CompilerParams(dimension_semantics=("parallel", "parallel", "arbitrary"),
                                             vmem_limit_bytes=VMEM_LIMIT),
    )(dgates, wsb)


def _bd_dw(xcb, dgates):
    tk = 1024

    def body(x_ref, d_ref, o_ref):
        @pl.when(pl.program_id(1) == 0)
        def _():
            o_ref[...] = jnp.zeros_like(o_ref)

        o_ref[...] += lax.dot_general(x_ref[...], d_ref[...], (((0,), (0,)), ((), ())), preferred_element_type=F32)

    return pl.pallas_call(
        body, name="lru_gate_dw", grid=(4, S // tk),
        in_specs=[pl.BlockSpec((tk, SB), lambda q, k: (k, q % 2)), pl.BlockSpec((tk, SB), lambda q, k: (k, q))],
        out_specs=pl.BlockSpec((None, SB, SB), lambda q, k: (q, 0, 0)),
        out_shape=jax.ShapeDtypeStruct((4, SB, SB), F32),
        compiler_params=pltpu.CompilerParams(dimension_semantics=("parallel", "arbitrary"), vmem_limit_bytes=VMEM_LIMIT),
    )(xcb, dgates)


def _bd_extract(dwsb):
    def body(w_ref, o_ref):
        lane = lax.broadcasted_iota(jnp.int32, (LRU_BW, 128), 1)
        for q in range(4):
            for kk in range(8):
                c0 = LRU_BW * kk
                w0, off = (c0 // 128) * 128, c0 % 128
                rows = pl.ds(LRU_BW * kk, LRU_BW)
                blk = w_ref[q, rows, w0:w0 + 128]
                if off:
                    blk = pltpu.roll(blk, 128 - off, 1)
                    if off + LRU_BW > 128:
                        nxt = pltpu.roll(w_ref[q, rows, w0 + 128:w0 + 256], 128 - off, 1)
                        blk = jnp.where(lane < 128 - off, blk, nxt)
                o_ref[q // 2, 8 * (q % 2) + kk] = blk.astype(BF16)

    return pl.pallas_call(
        body, name="lru_gate_dw_blocks",
        in_specs=[pl.BlockSpec(memory_space=pltpu.VMEM)], out_specs=pl.BlockSpec(memory_space=pltpu.VMEM),
        out_shape=jax.ShapeDtypeStruct((2, LRU_NB, LRU_BW, 128), BF16),
        compiler_params=pltpu.CompilerParams(vmem_limit_bytes=VMEM_LIMIT),
    )(dwsb)


def _layer_fwd(x, P, l, tabs, token=None, late=None):
    A = {"x": x}
    A["h"] = _prenorm_fwd(x, P["pre_g"], token)
    proj = A["proj"] = _mm(A["h"], P["wp"], "nt", "in_proj", out_dtype=BF16, tm=1024)
    A["ya"] = _gmlp_fwd(proj, P["ln_g"], P["ln_b"], P["ws"], P["bst"])
    A["xc"], A["xcb"] = _conv_fwd(proj, P["conv_w"], P["conv_b"])
    A["gates"] = _bd_fwd(A["xcb"], P["wsb"], l)
    A["a"], bterm = _lru_gates_fwd(A["gates"], A["xc"], P["ba"], P["bx"], P["lam"])
    A["hs"] = _scan_fwd(A["a"], bterm)
    A["yc"] = _gate_mul_fwd("yc_fwd", A["hs"], proj, LRU_W, O_ZC // LRU_W)
    if late is not None:
        P.update(late(A["yc"]))
    A["cqn"], A["ckvn"] = _mla_prep_fwd(proj, P["qg"], P["kvg"])
    q = _mm(A["cqn"], P["wuq"], "nt", "q_up", out_dtype=BF16)
    kv = _mm(A["ckvn"], P["wukv"], "nt", "kv_up", out_dtype=BF16)
    A["qc"], A["kc"], A["kct"], A["vv"], vt = _rope_fwd(q, kv, proj, tabs)
    A["o"], A["lse"] = _attn_fwd(A["qc"], A["kc"], vt)
    A["yb"] = _gate_mul_fwd("yb_fwd", A["o"], proj, MLA_W, O_ZB // MLA_W)
    A["pa"] = _mm(A["ya"], P["wpa"], "nn", "proj_a", out_dtype=BF16)
    A["pb"] = _mm(A["yb"], P["wpb"], "nn", "proj_b", out_dtype=BF16)
    A["pc"] = _mm(A["yc"], P["wpc"], "nn", "proj_c", out_dtype=BF16)
    A["merged"] = _merge_fwd(A["pa"], A["pb"], A["pc"], proj)
    A["o2"] = _mm(A["merged"], P["wout"], "nn", "out_proj")
    return _post_fwd(x, A["o2"], P["post_g"]), A


def _loss_fwd(y, tgt):
    def fn(i, rv, hv, fv):
        yb, tb = rv
        e = yb - tb
        part = 0.5 * jnp.sum(jnp.mean(e * e, axis=-1, keepdims=True), axis=0, keepdims=True)
        return [e * (1.0 / D)], [part]
    return _rows(fn, "loss", 256, [(y, D, 0), (tgt, D, 0)], outs=[(D, F32)], accs=[(1, 1)])


def _post_bwd(dxn, o2, g, token=None):
    def fn(i, rv, hv, fv):
        dy, ob = rv
        dx, dg = _rms_bwd(dy, ob, fv[0])
        return [dx], [_colsum(dg)]
    return _rows(fn, "post_bwd", 256, [(dxn, D, 0), (o2, D, 0)], fulls=[g] + ([] if token is None else [token]),
                 outs=[(D, BF16)], accs=[(1, D)])


def _merge_bwd(dm, pa, pb, pc, proj, dproj):
    def fn(i, rv, hv, fv):
        d, a, b, c, ga, gb, gc = rv
        outs_p, outs_g = [], []
        for p, gg in ((a, ga), (b, gb), (c, gc)):
            s = _sig(gg)
            outs_p.append(d * s)
            outs_g.append(d * p * s * (1.0 - s))
        return outs_p + [jnp.concatenate(outs_g, axis=1)], []
    return _rows(fn, "merge_bwd", 256,
                 [(dm, D, 0), (pa, D, 0), (pb, D, 0), (pc, D, 0),
                  (proj, D, O_GA // D), (proj, D, O_GB // D), (proj, D, O_GC // D)],
                 outs=[(D, BF16)] * 3 + [(3 * D, BF16, (dproj, NP, O_GA // (3 * D)))])


def _gmlp_bwd(dya, proj, ln_g, ln_b, ws, bst, dproj):
    gw = GM_W // GM_G

    def fn(i, rv, hv, fv):
        dy, u, v, z = rv
        g, b, w, bt = fv
        vh, rs, vn = _gm_norm(v, g, b)
        sv = _gm_sv(vn, w, bt)
        sz = _silu(z)
        du = dy * sv * sz
        dsv = dy * u * sz
        dz = dy * u * sv * _dsilu(z)
        mask = _gm_mask()
        lane = lax.broadcasted_iota(jnp.int32, (GM_B, 128), 1)
        dvn_parts, dws, dbst = [], [], jnp.zeros((GM_B, 128), F32)
        for k in range(GM_G):
            wm = jnp.where(mask, w[k], 0.0).astype(BF16)
            dsk = dsv[:, k * gw:(k + 1) * gw]
            dskb = dsk.astype(BF16)
            dvn_parts.append(lax.dot_general(wm, dskb, (((0,), (0,)), ((), ())), preferred_element_type=F32))
            dwk = lax.dot_general(dskb, vn[:, k * gw:(k + 1) * gw].astype(BF16), (((1,), (1,)), ((), ())),
                                  preferred_element_type=F32)
            dws.append(jnp.where(mask, dwk, 0.0)[None])
            dbst = dbst + jnp.where(lane == k, jnp.sum(dsk, axis=1, keepdims=True), 0.0)
        dvn = jnp.concatenate(dvn_parts, axis=1)
        dvh = dvn * g
        dv = rs * (dvh - jnp.mean(dvh, axis=-1, keepdims=True) - vh * jnp.mean(dvh * vh, axis=-1, keepdims=True))
        return ([jnp.concatenate([du, dv, dz], axis=1)],
                [jnp.concatenate(dws, axis=0), dbst, _colsum(dvn * vh), _colsum(dvn)])
    return _rows(fn, "gmlp_bwd", GM_B, [(dya, GM_W, 0), (proj, GM_W, 0), (proj, GM_W, 1), (proj, GM_W, 2)],
                 fulls=[ln_g, ln_b, ws, bst], outs=[(3 * GM_W, BF16, (dproj, NP, O_U // (3 * GM_W)))],
                 accs=[(GM_G, GM_B, GM_B), (GM_B, 128), (1, GM_W), (1, GM_W)])


def _yb_bwd(dyb, o, proj, dproj):
    def fn(i, rv, hv, fv):
        dy, ob, z = rv
        do = dy * _silu(z)
        prod = do * ob
        lane = lax.broadcasted_iota(jnp.int32, (dy.shape[0], 128), 1)
        dl = jnp.zeros((dy.shape[0], 128), F32)
        for h in range(H):
            dl = dl + jnp.where(lane == h, jnp.sum(prod[:, h * VDIM:(h + 1) * VDIM], axis=1, keepdims=True), 0.0)
        return [do, dl, dy * ob * _dsilu(z)], []
    return _rows(fn, "yb_bwd", 256, [(dyb, MLA_W, 0), (o, MLA_W, 0), (proj, MLA_W, O_ZB // MLA_W)],
                 outs=[(MLA_W, BF16), (128, F32, "T"), (MLA_W, BF16, (dproj, NP, O_ZB // MLA_W))])


def _attn_bwd(qc, kc, kct, vv, do, lse, dlt):
    def body(q_ref, k_ref, kt_ref, v_ref, do_ref, l_ref, d_ref, dq_ref, dk_ref, dv_ref, dqt_ref):
        h, i = pl.program_id(0), pl.program_id(1)

        @pl.when(i == 0)
        def _():
            dk_ref[...] = jnp.zeros_like(dk_ref)
            dv_ref[...] = jnp.zeros_like(dv_ref)

        q = q_ref[...]
        dob = do_ref[...]
        lse = l_ref[...]
        dl = d_ref[pl.ds(h, 1), :]
        dqt_ref[...] = jnp.zeros_like(dqt_ref)

        def rows_of(sb, c):
            return pl.ds(pl.multiple_of((sb * ATT_NB + c) * TC, TC), TC)

        def front(sb):
            return [(lax.dot_general(k_ref[rows_of(sb, c), :], q, _NT, preferred_element_type=F32),
                     lax.dot_general(v_ref[rows_of(sb, c), :], dob, _NT, preferred_element_type=F32))
                    for c in range(ATT_NB)]

        def block(sb, sd, masked):
            dqt = None
            for c, (s, dp) in enumerate(sd):
                rows = rows_of(sb, c)
                p = jnp.exp(s - lse)
                if masked:
                    p = jnp.where(_attn_allowed(i, sb * ATT_NB + c), p, 0.0)
                ds = (p * (dp - dl)).astype(BF16)
                dk_ref[rows, :] += jnp.dot(ds, q, preferred_element_type=F32)
                dv_ref[rows, :] += jnp.dot(p.astype(BF16), dob, preferred_element_type=F32)
                part = jnp.dot(kt_ref[:, rows], ds, preferred_element_type=F32)
                dqt = part if dqt is None else dqt + part
            dqt_ref[...] += dqt

        def step(sb, sd):
            nxt = front(sb + 1)
            block(sb, sd, False)
            return nxt

        nsb = ((i + 1) * TQ + ATT_KB - 1) // ATT_KB
        sd = lax.fori_loop(0, nsb - 1, step, front(0))
        block(nsb - 1, sd, True)
        dq_ref[...] = dqt_ref[...].T.astype(dq_ref.dtype)

    blk = lambda w: pl.BlockSpec((TQ, w), lambda h, i: (i, h))
    head = lambda w: pl.BlockSpec((S, w), lambda h, i: (0, h))
    return pl.pallas_call(
        body, name="attn_bwd", grid=(H, S // TQ),
        in_specs=[blk(HP), head(HP), pl.BlockSpec((HP, S), lambda h, i: (h, 0)), head(VDIM), blk(VDIM),
                  pl.BlockSpec((None, 1, TQ), lambda h, i: (h, 0, i)), pl.BlockSpec((8, TQ), lambda h, i: (0, i))],
        out_specs=[blk(HP), head(HP), head(VDIM)],
        out_shape=[jax.ShapeDtypeStruct((S, H * HP), BF16), jax.ShapeDtypeStruct((S, H * HP), F32),
                   jax.ShapeDtypeStruct((S, MLA_W), F32)],
        scratch_shapes=[pltpu.VMEM((HP, TQ), F32)],
        compiler_params=pltpu.CompilerParams(dimension_semantics=("parallel", "arbitrary"),
                                             vmem_limit_bytes=28 * MIB),
    )(qc, kc, kct, vv, do, lse, dlt)


def _rope_bwd(dqc, dkc, dvv, tabs):
    def fn(i, rv, hv, fv):
        dq, dk, dv, cc, sa, sb = rv
        qs, ks = [], []
        dkr = jnp.zeros((dq.shape[0], 128), F32)
        for h in range(H):
            qs += [dq[:, h * HP:h * HP + 128] * ATT_SCALE, _rot_t(dq[:, h * HP + 128:(h + 1) * HP], cc, sa, sb) * ATT_SCALE]
            ks.append(dk[:, h * HP:h * HP + 128])
            dkr = dkr + dk[:, h * HP + 128:(h + 1) * HP]
        return [jnp.concatenate(qs, axis=1), jnp.concatenate(ks + [dv], axis=1), _rot_t(dkr, cc, sa, sb)], []
    cc, sa, sb = tabs
    return _rows(fn, "rope_bwd", 256,
                 [(dqc, H * HP, 0), (dkc, H * HP, 0), (dvv, MLA_W, 0), (cc, 128, 0), (sa, 128, 0), (sb, 128, 0)],
                 outs=[(H * HP, BF16), (H * 256, BF16), (128, BF16)])


MLA_GROUP = 1536


def _mla_prep_bwd(dcqn, dckvn, dkr, proj, qg, kvg, dproj):
    def fn(i, rv, hv, fv):
        d1, d2, dk, cq, ckv = rv
        g1, g2 = fv
        dx1, dg1 = _rms_bwd(d1, cq, g1)
        dx2, dg2 = _rms_bwd(d2, ckv, g2)
        zeros = jnp.zeros((d1.shape[0], MLA_GROUP - KVR - 128 - QR), F32)
        return [jnp.concatenate([dx2, dk.astype(F32), dx1, zeros], axis=1)], [_colsum(dg1), _colsum(dg2)]
    return _rows(fn, "mla_prep_bwd", 256,
                 [(dcqn, QR, 0), (dckvn, KVR, 0), (dkr, 128, 0), (proj, QR, O_CQ // QR), (proj, KVR, O_CKV // KVR)],
                 fulls=[qg, kvg], outs=[(MLA_GROUP, BF16, (dproj, NP, O_CKV // MLA_GROUP))], accs=[(1, QR), (1, KVR)])


def _yc_bwd(dyc, hs, proj, dproj):
    def fn(i, rv, hv, fv):
        dy, hh, z = rv
        return [dy * _silu(z), dy * hh * _dsilu(z)], []
    return _rows(fn, "yc_bwd", LRU_TM, [(dyc, LRU_W, 0), (hs, LRU_W, 0), (proj, LRU_W, O_ZC // LRU_W)],
                 outs=[(LRU_W, F32), (LRU_W, BF16, (dproj, NP, O_ZC // LRU_W))])


def _scan_bwd(a, hs, dh):
    nblk = S // SCAN_T

    def body(a_ref, h_ref, dh_ref, da_ref, db_ref):
        row = lax.broadcasted_iota(jnp.int32, (SCAN_T, SCAN_CW), 0)

        def step(j, carry):
            gc, ac = carry
            blk = nblk - 1 - j
            t0 = pl.multiple_of(blk * SCAN_T, SCAN_T)
            av = a_ref[pl.ds(t0, SCAN_T), :]
            A = jnp.where(row < SCAN_T - 1, pltpu.roll(av, SCAN_T - 1, 0), ac)
            B = dh_ref[pl.ds(t0, SCAN_T), :]
            d = 1
            while d < SCAN_T:
                keep = row < SCAN_T - d
                A_s = jnp.where(keep, pltpu.roll(A, SCAN_T - d, 0), 1.0)
                B_s = jnp.where(keep, pltpu.roll(B, SCAN_T - d, 0), 0.0)
                B = A * B_s + B
                A = A * A_s
                d *= 2
            g = A * gc + B
            p0 = pl.multiple_of(jnp.maximum(t0 - 8, 0), 8)
            last = jnp.where(blk > 0, h_ref[pl.ds(p0, 8), :][7:8, :], 0.0)
            h_prev = jnp.where(row >= 1, pltpu.roll(h_ref[pl.ds(t0, SCAN_T), :], 1, 0), last)
            da_ref[pl.ds(t0, SCAN_T), :] = g * h_prev
            db_ref[pl.ds(t0, SCAN_T), :] = g
            return g[0:1, :], av[0:1, :]

        z = jnp.zeros((1, SCAN_CW), F32)
        lax.fori_loop(0, nblk, step, (z, z))

    spec = pl.BlockSpec((S, SCAN_CW), lambda j: (0, j))
    return pl.pallas_call(
        body, name="scan_bwd", grid=(LRU_W // SCAN_CW,), in_specs=[spec] * 3, out_specs=[spec] * 2,
        out_shape=[jax.ShapeDtypeStruct((S, LRU_W), F32)] * 2,
        compiler_params=pltpu.CompilerParams(dimension_semantics=("parallel",),
                                             vmem_limit_bytes=_vmem(5 * _nbytes((S, SCAN_CW), F32))),
    )(a, hs, dh)


def _lru_gates_bwd(da, db, gates, xc, ba, bx, lam):
    def fn(i, rv, hv, fv):
        dav, dbv, ga, gx, x = rv
        bav, bxv, lamv = fv
        r, ig, sp, a, e2, om, mult = _lru_terms(ga, gx, x, bav, bxv, lamv)
        dmult = dbv * ig * x
        dig = dbv * mult * x
        dxc1 = dbv * mult * ig
        dlog_a = dav * a + jnp.where(om > 0.0, dmult * (-e2 / mult), 0.0)
        dr = dlog_a * (-LRU_C * sp)
        dga = dr * r * (1.0 - r)
        dgx = dig * ig * (1.0 - ig)
        dlam = _colsum(dlog_a * (-LRU_C * r)) * (-_sig(-lamv))
        return [jnp.concatenate([dga, dgx], axis=1), dxc1], [_colsum(dga), _colsum(dgx), dlam]
    return _rows(fn, "lru_gates_bwd", LRU_TM,
                 [(da, LRU_W, 0), (db, LRU_W, 0), (gates, LRU_W, 0), (gates, LRU_W, 1), (xc, LRU_W, 0)],
                 fulls=[ba, bx, lam], outs=[(2 * LRU_W, BF16), (LRU_W, F32)], accs=[(1, LRU_W)] * 3)


def _conv_bwd(dxc1, dxc2, proj, w, dproj):
    cb = O_XC // LRU_W

    def fn(i, rv, hv, fv):
        d1, d2, xb = rv
        n1, n2, xprev = hv
        ww = fv[0]
        last = i == S // LRU_TM - 1
        dxc = d1 + d2
        nxt = jnp.where(last, 0.0, n1 + n2)
        xprev = jnp.where(i > 0, xprev, 0.0)
        row = lax.broadcasted_iota(jnp.int32, xb.shape, 0)
        dx = ww[3:4] * dxc
        dws = [None] * CONV_W
        dws[3] = _colsum(dxc * xb)
        for s in range(1, CONV_W):
            dx = dx + ww[3 - s:4 - s] * _shift_up(dxc, nxt, s, row)
            dws[3 - s] = _colsum(dxc * _shift_down(xb, xprev, s, row))
        return [dx], [jnp.concatenate(dws, axis=0), _colsum(dxc)]
    return _rows(fn, "conv_bwd", LRU_TM, [(dxc1, LRU_W, 0), (dxc2, LRU_W, 0), (proj, LRU_W, cb)],
                 halos=[(dxc1, LRU_W, 0, "next"), (dxc2, LRU_W, 0, "next"), (proj, LRU_W, cb, "prev")],
                 fulls=[w], outs=[(LRU_W, BF16, (dproj, NP, cb))], accs=[(CONV_W, LRU_W), (1, LRU_W)])


def _prenorm_bwd(dxn, dh, x, g):
    def fn(i, rv, hv, fv):
        dy, dhh, xb = rv
        dx, dg = _rms_bwd(dhh, xb, fv[0])
        return [dy + dx], [_colsum(dg)]
    return _rows(fn, "prenorm_bwd", 256, [(dxn, D, 0), (dh, D, 0), (x, D, 0)], fulls=[g], outs=[(D, F32)],
                 accs=[(1, D)])


def _layer_bwd(dxn, A, P, l, tabs, token=None, early=None):
    G, GB = {}, {}
    proj = A["proj"]

    def dw(key, a, b, name, **tiles):
        GB[key] = _mm(a, b, "tn", name, out_dtype=BF16, **tiles)

    do2, G["post_g"] = _post_bwd(dxn, A["o2"], P["post_g"], token)
    dm = _mm(do2, P["wout"], "nt", "out_proj_dx", out_dtype=BF16)
    dw("wout", A["merged"], do2, "out_proj_dw")
    dpa, dpb, dpc, dproj = _merge_bwd(dm, A["pa"], A["pb"], A["pc"], proj, None)
    dya = _mm(dpa, P["wpa"], "nt", "proj_a_dx", out_dtype=BF16)
    dw("wpa", A["ya"], dpa, "proj_a_dw")
    dyb = _mm(dpb, P["wpb"], "nt", "proj_b_dx", out_dtype=BF16)
    dw("wpb", A["yb"], dpb, "proj_b_dw")
    dyc = _mm(dpc, P["wpc"], "nt", "proj_c_dx", out_dtype=BF16)
    dw("wpc", A["yc"], dpc, "proj_c_dw")
    dproj, G["ws"], G["bst"], G["ln_g"], G["ln_b"] = _gmlp_bwd(dya, proj, P["ln_g"], P["ln_b"], P["ws"], P["bst"], dproj)
    do, dl, dproj = _yb_bwd(dyb, A["o"], proj, dproj)
    dqc, dkc, dvv = _attn_bwd(A["qc"], A["kc"], A["kct"], A["vv"], do, A["lse"], dl)
    dq, dkv, dkr = _rope_bwd(dqc, dkc, dvv, tabs)
    dcqn = _mm(dq, P["wuq"], "nn", "q_up_dx", out_dtype=BF16)
    dw("wuq", dq, A["cqn"], "q_up_dw")
    dckvn = _mm(dkv, P["wukv"], "nn", "kv_up_dx", out_dtype=BF16)
    dw("wukv", dkv, A["ckvn"], "kv_up_dw")
    dproj, G["qg"], G["kvg"] = _mla_prep_bwd(dcqn, dckvn, dkr, proj, P["qg"], P["kvg"], dproj)
    dhs, dproj = _yc_bwd(dyc, A["hs"], proj, dproj)
    da, db = _scan_bwd(A["a"], A["hs"], dhs)
    dgates, dxc1, G["ba"], G["bx"], G["lam"] = _lru_gates_bwd(da, db, A["gates"], A["xc"], P["ba"], P["bx"], P["lam"])
    dxc2 = _bd_dx(dgates, P["wsb"], l)
    G["wab"] = _bd_extract(_bd_dw(A["xcb"], dgates))
    dproj, G["conv_w"], G["conv_b"] = _conv_bwd(dxc1, dxc2, proj, P["conv_w"], dproj)
    tok = (None, None) if early is None else early(GB)
    dh = _mm(dproj, P["wp"], "nn", "in_proj_dx", tm=1024, tn=1024, token=tok[0])
    dw("wp", dproj, A["h"], "in_proj_dw", tm=1536, tn=1024, token=tok[1])
    dx, G["pre_g"] = _prenorm_bwd(dxn, dh, A["x"], P["pre_g"])
    return dx, G, GB


_ORIG_OFF = [0]
for _s in IN_SIZES:
    _ORIG_OFF.append(_ORIG_OFF[-1] + _s)
_PAD_OFF = {0: O_U, 1: O_V, 2: O_ZA, 3: O_CQ, 4: O_CKV, 5: O_KR, 6: O_ZB, 7: O_XC, 8: O_ZC, 9: O_GA, 10: O_GB, 11: O_GC}
SHARD_IN = N_IN // N_CHIPS


def _pieces_w_in(j):
    lo, hi = SHARD_IN * j, SHARD_IN * (j + 1)
    out = []
    for k in range(len(IN_SIZES)):
        a, b = max(lo, _ORIG_OFF[k]), min(hi, _ORIG_OFF[k + 1])
        if a < b:
            out.append((a - lo, _PAD_OFF[k] + a - _ORIG_OFF[k], b - a))
    return out


def _pieces_uq(j):
    return [(192 * hh, HP * (2 * j + hh), NOPE + ROPE) for hh in range(2)]


def _pieces_ukv(j):
    out = []
    for hh in range(2):
        h = 2 * j + hh
        out += [(256 * hh, NOPE * h, NOPE), (256 * hh + NOPE, H * NOPE + VDIM * h, VDIM)]
    return out


def _pieces_rows(r):
    return lambda j: [(0, r * j, r)]


LAYOUT = {
    "w_in": (SHARD_IN, NP, _pieces_w_in),
    "mla_w_uq": (2 * (NOPE + ROPE), H * HP, _pieces_uq),
    "mla_w_ukv": (2 * (NOPE + VDIM), 2 * H * 128, _pieces_ukv),
    "lru_conv_w": (1, N_CHIPS, _pieces_rows(1)),
    "w_proj_a": (GM_W // N_CHIPS, GM_W, _pieces_rows(GM_W // N_CHIPS)),
    "w_proj_b": (MLA_W // N_CHIPS, MLA_W, _pieces_rows(MLA_W // N_CHIPS)),
    "w_proj_c": (LRU_W // N_CHIPS, LRU_W, _pieces_rows(LRU_W // N_CHIPS)),
    "w_out": (D // N_CHIPS, D, _pieces_rows(D // N_CHIPS)),
}
TRANSPOSED = ("w_in", "mla_w_uq", "mla_w_ukv")


def _superblocks(w_a, w_x):
    w6 = jnp.stack([w_a, w_x], axis=1).reshape(DEPTH, 4, 8, LRU_BW, LRU_BW).astype(BF16)
    bands = [jnp.pad(w6[:, :, k], ((0, 0), (0, 0), (0, 0), (LRU_BW * k, SB - LRU_BW * (k + 1)))) for k in range(8)]
    return jnp.concatenate(bands, axis=2)


_HBM = pl.BlockSpec(memory_space=pltpu.HBM)


def _position():
    return lax.axis_index("x"), lax.axis_index("y"), lax.axis_index("c")


def _allgather(blocks, name):
    n = len(blocks)

    def body(*refs):
        ins, outs = refs[:n], refs[n:2 * n]
        send, recv, lsem = refs[2 * n:]
        x, y, c = _position()
        me, sib = (x, y, c), (x, y, 1 - c)
        chips = [(1 - x, y), (x, 1 - y), (1 - x, 1 - y)]

        def cp(k, a, block, to, src=None):
            dst = outs[a].at[4 * block[0] + 2 * block[1] + block[2]]
            return pltpu.make_async_remote_copy(src_ref=dst if src is None else src, dst_ref=dst,
                                                send_sem=send.at[7 * a + k], recv_sem=recv.at[7 * a + k],
                                                device_id=to, device_id_type=MESH)

        mine = [pltpu.make_async_copy(ins[a], outs[a].at[4 * x + 2 * y + c], lsem.at[a]) for a in range(n)]
        for m in mine:
            m.start()
        first = []
        for a in range(n):
            first.append(cp(0, a, me, sib, src=ins[a]))
            first += [cp(1 + j, a, me, (*chip, c), src=ins[a]) for j, chip in enumerate(chips)]
        for f in first:
            f.start()
        passed = []
        for j, chip in enumerate(chips):
            for a in range(n):
                cp(1 + j, a, (*chip, c), me).wait_recv()
                p = cp(4 + j, a, (*chip, c), sib)
                p.start()
                passed.append(p)
        for a in range(n):
            cp(0, a, sib, me).wait_recv()
            for j, chip in enumerate(chips):
                cp(4 + j, a, (*chip, 1 - c), me).wait_recv()
        for f in first + passed:
            f.wait_send()
        for m in mine:
            m.wait()

    return pl.pallas_call(
        body, name=name,
        out_shape=[jax.ShapeDtypeStruct((8,) + b.shape, b.dtype) for b in blocks],
        in_specs=[_HBM] * n, out_specs=[_HBM] * n,
        scratch_shapes=[pltpu.SemaphoreType.DMA((7 * n,)), pltpu.SemaphoreType.DMA((7 * n,)),
                        pltpu.SemaphoreType.DMA((n,))],
    )(*blocks)


_REL = (2, 1, 3)


def _cut(r):
    return r if r < 32 else (r // 2 + 15) // 16 * 16


def _half_rows(r, c0):
    return _cut(r) if c0 == 0 else r - _cut(r)


def _half_pieces(lay_a, jsrc, c0):
    r = lay_a[0]
    lo, hi = (0, _cut(r)) if c0 == 0 else (_cut(r), r)
    out = []
    for s0, d0, nr in lay_a[2](jsrc):
        a, b = max(s0, lo), min(s0 + nr, hi)
        if a < b:
            out.append((a, d0 + a - s0, b - a))
    return out


def _gather_zeros(names, srcs):
    return [jnp.zeros((LAYOUT[nm][1],) + s.shape[1:], s.dtype) for nm, s in zip(names, srcs)]


def _weights_allgather(names, srcs, name, carry=()):
    n = len(srcs)
    lay = [LAYOUT[nm] for nm in names]
    zeros = _gather_zeros(names, srcs)
    m = len(carry)

    def body(*refs):
        ins, outs = refs[:n], refs[2 * n + m:3 * n + m]
        send, recv, lsem = refs[3 * n + 2 * m:]
        x, y, c = _position()
        j = 2 * x + y
        sib = (x, y, 1 - c)
        chips = [(1 - x, y), (x, 1 - y), (1 - x, 1 - y)]

        def flow(a, k, jsrc, c0, to, from_src):
            cps = []
            for s0, d0, nr in _half_pieces(lay[a], jsrc, c0):
                dst = outs[a].at[pl.ds(d0, nr)]
                src = ins[a].at[pl.ds(s0, nr)] if from_src else dst
                cps.append(pltpu.make_async_remote_copy(src_ref=src, dst_ref=dst, send_sem=send.at[7 * a + k],
                                                        recv_sem=recv.at[7 * a + k], device_id=to, device_id_type=MESH))
            return cps

        def sized(a, k, rows):
            ref = ins[a].at[pl.ds(0, rows)]
            return pltpu.make_async_remote_copy(src_ref=ref, dst_ref=ref, send_sem=send.at[7 * a + k],
                                                recv_sem=recv.at[7 * a + k], device_id=sib, device_id_type=MESH)

        for j0 in range(N_CHIPS):
            for c0 in range(2):
                @pl.when((j == j0) & (c == c0))
                def _(j0=j0, c0=c0):
                    mine = [_half_rows(lay[a][0], c0) for a in range(n)]
                    theirs = [_half_rows(lay[a][0], 1 - c0) for a in range(n)]
                    for a in range(n):
                        for s0, d0, nr in _half_pieces(lay[a], j0, c0):
                            pltpu.make_async_copy(ins[a].at[pl.ds(s0, nr)], outs[a].at[pl.ds(d0, nr)], lsem.at[a]).start()
                    for a in range(n):
                        for cp in flow(a, 0, j0, c0, sib, True):
                            cp.start()
                        for k, chip in enumerate(chips):
                            for cp in flow(a, 1 + k, j0, c0, (*chip, c), True):
                                cp.start()
                    for k in range(3):
                        for a in range(n):
                            if mine[a]:
                                sized(a, 1 + k, mine[a]).wait_recv()
                                for cp in flow(a, 4 + k, j0 ^ _REL[k], c0, sib, False):
                                    cp.start()
                    for a in range(n):
                        if theirs[a]:
                            sized(a, 0, theirs[a]).wait_recv()
                            for k in range(3):
                                sized(a, 4 + k, theirs[a]).wait_recv()
                    for a in range(n):
                        if mine[a]:
                            for k in range(7):
                                sized(a, k, mine[a]).wait_send()
                            ref = ins[a].at[pl.ds(0, mine[a])]
                            pltpu.make_async_copy(ref, ref, lsem.at[a]).wait()

    res = pl.pallas_call(
        body, name=name,
        out_shape=[jax.ShapeDtypeStruct(z.shape, z.dtype) for z in list(zeros) + list(carry)],
        in_specs=[_HBM] * (2 * n + m), out_specs=[_HBM] * (n + m),
        input_output_aliases={n + a: a for a in range(n + m)},
        scratch_shapes=[pltpu.SemaphoreType.DMA((7 * n,)), pltpu.SemaphoreType.DMA((7 * n,)),
                        pltpu.SemaphoreType.DMA((n,))],
    )(*srcs, *zeros, *carry)
    return res[:n], res[n:]


_SEM = pl.BlockSpec(memory_space=pltpu.SEMAPHORE)
_VMEM_TOKEN = pl.BlockSpec(memory_space=pltpu.VMEM)
_TOKEN = jax.ShapeDtypeStruct((8, 128), F32)
_EFFECT = pltpu.SideEffectType.DATAFLOW_SIDE_EFFECTING


def _gather_start(names, srcs, zeros, name, after=None):
    n = len(srcs)
    lay = [LAYOUT[nm] for nm in names]
    extra = [] if after is None else [after]

    def body(*refs):
        ins, lands = refs[:n], refs[n:2 * n]
        send, recv, lsem = refs[2 * n + len(extra):2 * n + len(extra) + 3]
        refs[-1][...] = jnp.zeros_like(refs[-1])
        x, y, c = _position()
        j = 2 * x + y
        chips = [(1 - x, y), (x, 1 - y), (1 - x, 1 - y)]
        for j0 in range(N_CHIPS):
            @pl.when(j == j0)
            def _(j0=j0):
                for a in range(n):
                    for s0, d0, nr in lay[a][2](j0):
                        src, dst = ins[a].at[pl.ds(s0, nr)], lands[a].at[pl.ds(d0, nr)]
                        pltpu.make_async_copy(src, dst, lsem.at[a]).start()
                        for k, chip in enumerate(chips):
                            pltpu.make_async_remote_copy(src_ref=src, dst_ref=dst, send_sem=send.at[3 * a + k],
                                                         recv_sem=recv.at[3 * a + k], device_id=(*chip, c),
                                                         device_id_type=MESH).start()

    sems = [pltpu.SemaphoreType.DMA((3 * n,)), pltpu.SemaphoreType.DMA((3 * n,)), pltpu.SemaphoreType.DMA((n,))]
    hbm = lambda a: pltpu.HBM(a.shape, a.dtype)
    res = pl.pallas_call(
        body, name=name,
        out_shape=sems + [hbm(s) for s in srcs] + [hbm(z) for z in zeros] + [_TOKEN],
        in_specs=[_HBM] * (2 * n) + [pl.BlockSpec(memory_space=pl.ANY)] * len(extra),
        out_specs=[_SEM] * 3 + [_HBM] * (2 * n) + [_VMEM_TOKEN],
        input_output_aliases={a: 3 + a for a in range(2 * n)},
        compiler_params=pltpu.CompilerParams(has_side_effects=_EFFECT),
    )(*[pltpu.with_memory_space_constraint(s, pltpu.HBM) for s in srcs],
      *[pltpu.with_memory_space_constraint(z, pltpu.HBM) for z in zeros], *extra)
    return res[:3], res[3:3 + n], res[3 + n:3 + 2 * n], res[-1]


def _gather_wait(names, sems, srcs, lands, after, name):
    n = len(srcs)
    lay = [LAYOUT[nm] for nm in names]

    def body(*refs):
        ins, zones = refs[:n], refs[n:2 * n]
        send, recv, lsem = refs[2 * n:2 * n + 3]
        x, y, c = _position()
        for a in range(n):
            whole = zones[a].at[pl.ds(0, lay[a][0])]
            for k in range(3):
                cp = pltpu.make_async_remote_copy(src_ref=ins[a], dst_ref=whole, send_sem=send.at[3 * a + k],
                                                  recv_sem=recv.at[3 * a + k], device_id=(x, y, 1 - c),
                                                  device_id_type=MESH)
                cp.wait_send()
                cp.wait_recv()
            pltpu.make_async_copy(ins[a], whole, lsem.at[a]).wait()

    hbm = lambda a: pltpu.HBM(a.shape, a.dtype)
    res = pl.pallas_call(
        body, name=name,
        out_shape=[hbm(s) for s in srcs] + [hbm(z) for z in lands],
        in_specs=[_HBM] * (2 * n) + [_SEM] * 3 + [pl.BlockSpec(memory_space=pl.ANY)], out_specs=[_HBM] * (2 * n),
        input_output_aliases={a: a for a in range(2 * n)},
        compiler_params=pltpu.CompilerParams(has_side_effects=_EFFECT),
    )(*srcs, *lands, *sems, after)
    return res[n:]


def _clip_pieces(lay_a, jsrc, c0):
    h = lay_a[1] // 2
    lo, hi = c0 * h, (c0 + 1) * h
    out = []
    for s0, d0, nr in lay_a[2](jsrc):
        a, b = max(d0, lo), min(d0 + nr, hi)
        if a < b:
            out.append((s0 + a - d0, a, b - a))
    return out


def _rows_of(pieces):
    return sum(nr for _, _, nr in pieces)


def _both_cores(body_for):
    x, y, c = _position()
    j = 2 * x + y
    for j0 in range(N_CHIPS):
        for c0 in range(2):
            @pl.when((j == j0) & (c == c0))
            def _(j0=j0, c0=c0):
                body_for(j0, c0)


STAGE_ROWS = 512


def _staged_copy(src, dst, buf, sem_in, sem_out, rows):
    ch = buf.shape[0]
    for r in range(0, rows, ch):
        nr = min(ch, rows - r)
        stage = buf.at[pl.ds(0, nr)]
        cin = pltpu.make_async_copy(src.at[pl.ds(r, nr)], stage, sem_in)
        cin.start()
        cin.wait()
        cout = pltpu.make_async_copy(stage, dst.at[pl.ds(r, nr)], sem_out)
        cout.start()
        cout.wait()


def _half_to_sibling(names, gl, name, after=None):
    n = len(gl)
    halves = [LAYOUT[nm][1] // 2 for nm in names]
    extra = [] if after is None else [after]

    def body(*refs):
        ins, outs = refs[:n], refs[n + len(extra):2 * n + len(extra)]
        send, recv = refs[2 * n + len(extra):]
        x, y, c = _position()

        def run(j0, c0):
            cps = [pltpu.make_async_remote_copy(src_ref=ins[a].at[pl.ds((1 - c0) * halves[a], halves[a])], dst_ref=outs[a],
                                                send_sem=send.at[a], recv_sem=recv.at[a], device_id=(x, y, 1 - c),
                                                device_id_type=MESH) for a in range(n)]
            for cp in cps:
                cp.start()
            for cp in cps:
                cp.wait()

        _both_cores(run)

    return pl.pallas_call(
        body, name=name,
        out_shape=[jax.ShapeDtypeStruct((halves[a],) + gl[a].shape[1:], gl[a].dtype) for a in range(n)],
        in_specs=[_HBM] * n + [pl.BlockSpec(memory_space=pl.ANY)] * len(extra), out_specs=[_HBM] * n,
        scratch_shapes=[pltpu.SemaphoreType.DMA((n,)), pltpu.SemaphoreType.DMA((n,))],
    )(*gl, *extra)


def _chip_scatter_half(names, parts, name):
    n = len(parts)
    lay = [LAYOUT[nm] for nm in names]
    zeros = [jnp.zeros((N_CHIPS, lay[a][0]) + parts[a].shape[1:], parts[a].dtype) for a in range(n)]

    def body(*refs):
        ins, outs = refs[:n], refs[2 * n:3 * n]
        send, recv = refs[3 * n:3 * n + 2]
        stage, sem_in, sem_out = refs[3 * n + 2:4 * n + 2], refs[4 * n + 2], refs[4 * n + 3]
        x, y, c = _position()
        chips = [(1 - x, y), (x, 1 - y), (1 - x, 1 - y)]

        def run(j0, c0):
            def sized(a, rows):
                return outs[a].at[0, pl.ds(0, rows)]

            for a in range(n):
                base = c0 * (lay[a][1] // 2)
                for k, chip in enumerate(chips):
                    for s0, d0, nr in _clip_pieces(lay[a], j0 ^ _REL[k], c0):
                        pltpu.make_async_remote_copy(
                            src_ref=ins[a].at[pl.ds(d0 - base, nr)], dst_ref=outs[a].at[j0, pl.ds(s0, nr)],
                            send_sem=send.at[3 * a + k], recv_sem=recv.at[3 * a + k],
                            device_id=(*chip, c), device_id_type=MESH).start()
            for a in range(n):
                base = c0 * (lay[a][1] // 2)
                for s0, d0, nr in _clip_pieces(lay[a], j0, c0):
                    _staged_copy(ins[a].at[pl.ds(d0 - base, nr)], outs[a].at[j0, pl.ds(s0, nr)], stage[a],
                                 sem_in.at[a], sem_out.at[a], nr)
            for a in range(n):
                got = _rows_of(_clip_pieces(lay[a], j0, c0))
                for k in range(3):
                    sent = _rows_of(_clip_pieces(lay[a], j0 ^ _REL[k], c0))
                    if sent:
                        pltpu.make_async_remote_copy(src_ref=sized(a, sent), dst_ref=sized(a, sent),
                                                     send_sem=send.at[3 * a + k], recv_sem=recv.at[3 * a + k],
                                                     device_id=(x, y, c), device_id_type=MESH).wait_send()
                    if got:
                        pltpu.make_async_remote_copy(src_ref=sized(a, got), dst_ref=sized(a, got),
                                                     send_sem=send.at[3 * a + k], recv_sem=recv.at[3 * a + k],
                                                     device_id=(x, y, c), device_id_type=MESH).wait_recv()

        _both_cores(run)

    return pl.pallas_call(
        body, name=name,
        out_shape=[jax.ShapeDtypeStruct(z.shape, z.dtype) for z in zeros],
        in_specs=[_HBM] * (2 * n), out_specs=[_HBM] * n, input_output_aliases={n + a: a for a in range(n)},
        scratch_shapes=[pltpu.SemaphoreType.DMA((3 * n,)), pltpu.SemaphoreType.DMA((3 * n,))]
        + [pltpu.VMEM((min(STAGE_ROWS, p.shape[0]),) + p.shape[1:], p.dtype) for p in parts]
        + [pltpu.SemaphoreType.DMA((n,)), pltpu.SemaphoreType.DMA((n,))],
    )(*parts, *zeros)


def _subset_exchange(names, bufs, l, name):
    n = len(bufs)
    lay = [LAYOUT[nm] for nm in names]

    def body(*refs):
        outs = refs[n:2 * n]
        send, recv = refs[2 * n:]
        x, y, c = _position()

        def run(j0, c0):
            for a in range(n):
                for s0, _, nr in _clip_pieces(lay[a], j0, c0):
                    rows = outs[a].at[l, pl.ds(s0, nr)]
                    pltpu.make_async_remote_copy(src_ref=rows, dst_ref=rows, send_sem=send.at[a], recv_sem=recv.at[a],
                                                 device_id=(x, y, 1 - c), device_id_type=MESH).start()
            for a in range(n):
                for c_half, wait_send in ((c0, True), (1 - c0, False)):
                    rows = _rows_of(_clip_pieces(lay[a], j0, c_half))
                    if rows:
                        ref = outs[a].at[l, pl.ds(0, rows)]
                        cp = pltpu.make_async_remote_copy(src_ref=ref, dst_ref=ref, send_sem=send.at[a], recv_sem=recv.at[a],
                                                          device_id=(x, y, 1 - c), device_id_type=MESH)
                        if wait_send:
                            cp.wait_send()
                        else:
                            cp.wait_recv()

        _both_cores(run)

    return pl.pallas_call(
        body, name=name,
        out_shape=[jax.ShapeDtypeStruct(b.shape, b.dtype) for b in bufs],
        in_specs=[_HBM] * n, out_specs=[_HBM] * n, input_output_aliases={a: a for a in range(n)},
        scratch_shapes=[pltpu.SemaphoreType.DMA((n,)), pltpu.SemaphoreType.DMA((n,))],
    )(*bufs)


def _scatter_start(names, gl, name):
    n = len(gl)
    lay = [LAYOUT[nm] for nm in names]
    zones = [lax.empty((N_CHIPS, lay[a][0]) + gl[a].shape[1:], gl[a].dtype) for a in range(n)]

    def body(*refs):
        ins, lands = refs[:n], refs[n:2 * n]
        send, recv, lsem = refs[2 * n:2 * n + 3]
        refs[-1][...] = jnp.zeros_like(refs[-1])
        x, y, c = _position()
        j = 2 * x + y
        chips = [(1 - x, y), (x, 1 - y), (1 - x, 1 - y)]
        for j0 in range(N_CHIPS):
            @pl.when(j == j0)
            def _(j0=j0):
                for a in range(n):
                    for s0, d0, nr in lay[a][2](j0):
                        pltpu.make_async_copy(ins[a].at[pl.ds(d0, nr)], lands[a].at[j0, pl.ds(s0, nr)], lsem.at[a]).start()
                    for k, chip in enumerate(chips):
                        for s0, d0, nr in lay[a][2](j0 ^ _REL[k]):
                            pltpu.make_async_remote_copy(
                                src_ref=ins[a].at[pl.ds(d0, nr)], dst_ref=lands[a].at[j0, pl.ds(s0, nr)],
                                send_sem=send.at[3 * a + k], recv_sem=recv.at[3 * a + k],
                                device_id=(*chip, c), device_id_type=MESH).start()

    sems = [pltpu.SemaphoreType.DMA((3 * n,)), pltpu.SemaphoreType.DMA((3 * n,)), pltpu.SemaphoreType.DMA((n,))]
    hbm = lambda a: pltpu.HBM(a.shape, a.dtype)
    res = pl.pallas_call(
        body, name=name,
        out_shape=sems + [hbm(g) for g in gl] + [hbm(z) for z in zones] + [_TOKEN],
        in_specs=[_HBM] * (2 * n), out_specs=[_SEM] * 3 + [_HBM] * (2 * n) + [_VMEM_TOKEN],
        input_output_aliases={a: 3 + a for a in range(2 * n)},
        compiler_params=pltpu.CompilerParams(has_side_effects=_EFFECT),
    )(*[pltpu.with_memory_space_constraint(g, pltpu.HBM) for g in gl],
      *[pltpu.with_memory_space_constraint(z, pltpu.HBM) for z in zones])
    return res[:3], res[3:3 + n], res[3 + n:3 + 2 * n], res[-1]


def _scatter_wait(names, sems, srcs, lands, after, name):
    n = len(srcs)
    lay = [LAYOUT[nm] for nm in names]

    def body(*refs):
        zones = refs[n:2 * n]
        send, recv, lsem = refs[2 * n:2 * n + 3]
        x, y, c = _position()
        for a in range(n):
            whole = zones[a].at[0, pl.ds(0, lay[a][0])]
            for k in range(3):
                cp = pltpu.make_async_remote_copy(src_ref=whole, dst_ref=whole, send_sem=send.at[3 * a + k],
                                                  recv_sem=recv.at[3 * a + k], device_id=(x, y, 1 - c),
                                                  device_id_type=MESH)
                cp.wait_send()
                cp.wait_recv()
            pltpu.make_async_copy(whole, whole, lsem.at[a]).wait()

    hbm = lambda a: pltpu.HBM(a.shape, a.dtype)
    res = pl.pallas_call(
        body, name=name,
        out_shape=[hbm(s) for s in srcs] + [hbm(z) for z in lands],
        in_specs=[_HBM] * (2 * n) + [_SEM] * 3 + [pl.BlockSpec(memory_space=pl.ANY)], out_specs=[_HBM] * (2 * n),
        input_output_aliases={a: a for a in range(2 * n)},
        compiler_params=pltpu.CompilerParams(has_side_effects=_EFFECT),
    )(*srcs, *lands, *sems, after)
    return res[n:]


def _peer(x, y, c, k):
    return (1 - x if k & 4 else x, 1 - y if k & 2 else y, 1 - c if k & 1 else c)


def _bcast_start(arrs, name, after=None):
    n = len(arrs)
    zones = [lax.empty((8,) + a.shape, a.dtype) for a in arrs]
    extra = [] if after is None else [after]

    def body(*refs):
        ins, lands = refs[:n], refs[n:2 * n]
        send, recv, lsem = refs[2 * n + len(extra):2 * n + len(extra) + 3]
        refs[-1][...] = jnp.zeros_like(refs[-1])
        x, y, c = _position()
        for a in range(n):
            dst = lands[a].at[4 * x + 2 * y + c]
            pltpu.make_async_copy(ins[a], dst, lsem.at[a]).start()
            for k in range(1, 8):
                pltpu.make_async_remote_copy(src_ref=ins[a], dst_ref=dst, send_sem=send.at[7 * a + k - 1],
                                             recv_sem=recv.at[7 * a + k - 1], device_id=_peer(x, y, c, k),
                                             device_id_type=MESH).start()

    sems = [pltpu.SemaphoreType.DMA((7 * n,)), pltpu.SemaphoreType.DMA((7 * n,)), pltpu.SemaphoreType.DMA((n,))]
    hbm = lambda a: pltpu.HBM(a.shape, a.dtype)
    res = pl.pallas_call(
        body, name=name,
        out_shape=sems + [hbm(a) for a in arrs] + [hbm(z) for z in zones] + [_TOKEN],
        in_specs=[_HBM] * (2 * n) + [pl.BlockSpec(memory_space=pl.ANY)] * len(extra),
        out_specs=[_SEM] * 3 + [_HBM] * (2 * n) + [_VMEM_TOKEN],
        input_output_aliases={a: 3 + a for a in range(2 * n)},
        compiler_params=pltpu.CompilerParams(has_side_effects=_EFFECT),
    )(*[pltpu.with_memory_space_constraint(a, pltpu.HBM) for a in arrs],
      *[pltpu.with_memory_space_constraint(z, pltpu.HBM) for z in zones], *extra)
    return res[:3], res[3:3 + n], res[3 + n:3 + 2 * n], res[-1]


def _bcast_wait(sems, srcs, lands, after, name):
    n = len(srcs)

    def body(*refs):
        ins, zones = refs[:n], refs[n:2 * n]
        send, recv, lsem = refs[2 * n:2 * n + 3]
        x, y, c = _position()
        for a in range(n):
            for k in range(1, 8):
                cp = pltpu.make_async_remote_copy(src_ref=ins[a], dst_ref=zones[a].at[0], send_sem=send.at[7 * a + k - 1],
                                                  recv_sem=recv.at[7 * a + k - 1], device_id=_peer(x, y, c, k),
                                                  device_id_type=MESH)
                cp.wait_send()
                cp.wait_recv()
            pltpu.make_async_copy(ins[a], zones[a].at[0], lsem.at[a]).wait()

    hbm = lambda a: pltpu.HBM(a.shape, a.dtype)
    res = pl.pallas_call(
        body, name=name,
        out_shape=[hbm(s) for s in srcs] + [hbm(z) for z in lands],
        in_specs=[_HBM] * (2 * n) + [_SEM] * 3 + [pl.BlockSpec(memory_space=pl.ANY)], out_specs=[_HBM] * (2 * n),
        input_output_aliases={a: a for a in range(2 * n)},
        compiler_params=pltpu.CompilerParams(has_side_effects=_EFFECT),
    )(*srcs, *lands, *sems, after)
    return res[n:]


def _swap_start(arrs, name):
    n = len(arrs)
    zones = [lax.empty(a.shape, a.dtype) for a in arrs]

    def body(*refs):
        ins, lands = refs[:n], refs[n:2 * n]
        send, recv = refs[2 * n:2 * n + 2]
        refs[-1][...] = jnp.zeros_like(refs[-1])
        x, y, c = _position()
        for a in range(n):
            pltpu.make_async_remote_copy(src_ref=ins[a], dst_ref=lands[a], send_sem=send.at[a], recv_sem=recv.at[a],
                                         device_id=(x, y, 1 - c), device_id_type=MESH).start()

    sems = [pltpu.SemaphoreType.DMA((n,)), pltpu.SemaphoreType.DMA((n,))]
    hbm = lambda a: pltpu.HBM(a.shape, a.dtype)
    res = pl.pallas_call(
        body, name=name,
        out_shape=sems + [hbm(a) for a in arrs] + [hbm(z) for z in zones] + [_TOKEN],
        in_specs=[_HBM] * (2 * n), out_specs=[_SEM] * 2 + [_HBM] * (2 * n) + [_VMEM_TOKEN],
        input_output_aliases={a: 2 + a for a in range(2 * n)},
        compiler_params=pltpu.CompilerParams(has_side_effects=_EFFECT),
    )(*[pltpu.with_memory_space_constraint(a, pltpu.HBM) for a in arrs],
      *[pltpu.with_memory_space_constraint(z, pltpu.HBM) for z in zones])
    return res[:2], res[2:2 + n], res[2 + n:2 + 2 * n], res[-1]


def _swap_wait(sems, srcs, lands, after, name):
    n = len(srcs)

    def body(*refs):
        ins, zones = refs[:n], refs[n:2 * n]
        send, recv = refs[2 * n:2 * n + 2]
        x, y, c = _position()
        for a in range(n):
            cp = pltpu.make_async_remote_copy(src_ref=ins[a], dst_ref=zones[a], send_sem=send.at[a], recv_sem=recv.at[a],
                                              device_id=(x, y, 1 - c), device_id_type=MESH)
            cp.wait_send()
            cp.wait_recv()

    hbm = lambda a: pltpu.HBM(a.shape, a.dtype)
    res = pl.pallas_call(
        body, name=name,
        out_shape=[hbm(s) for s in srcs] + [hbm(z) for z in lands],
        in_specs=[_HBM] * (2 * n) + [_SEM] * 2 + [pl.BlockSpec(memory_space=pl.ANY)], out_specs=[_HBM] * (2 * n),
        input_output_aliases={a: a for a in range(2 * n)},
        compiler_params=pltpu.CompilerParams(has_side_effects=_EFFECT),
    )(*srcs, *lands, *sems, after)
    return res[:n], res[n:]


def _row_tile(r):
    for t in (256, 128, 64, 32, 16, 8):
        if r % t == 0 and r > t:
            return t
    return r


def _pair_add_half(g, rb, c_arr, name):
    hrows, rest = rb.shape[0], rb.shape[1:]
    tr = _row_tile(hrows)
    nb = hrows // tr
    z = (0,) * len(rest)

    def body(c_ref, g_ref, r_ref, o_ref):
        o_ref[...] = (g_ref[...].astype(F32) + r_ref[...].astype(F32)).astype(o_ref.dtype)

    return pl.pallas_call(
        body, name=name,
        grid_spec=pltpu.PrefetchScalarGridSpec(
            num_scalar_prefetch=1, grid=(nb,),
            in_specs=[pl.BlockSpec((tr,) + rest, lambda i, c_ref: (c_ref[0] * nb + i,) + z),
                      pl.BlockSpec((tr,) + rest, lambda i, c_ref: (i,) + z)],
            out_specs=pl.BlockSpec((tr,) + rest, lambda i, c_ref: (i,) + z)),
        out_shape=jax.ShapeDtypeStruct((hrows,) + rest, BF16),
        compiler_params=pltpu.CompilerParams(dimension_semantics=("parallel",), vmem_limit_bytes=VMEM_LIMIT),
    )(c_arr, g, rb)


def _sum_slabs(slabs, l, buf, name):
    m = len(slabs)
    n, R, rest = slabs[0].shape[0], slabs[0].shape[1], slabs[0].shape[2:]
    tr = _row_tile(R)
    z = (0,) * len(rest)

    def body(*refs):
        total = None
        for r_ref in refs[:m]:
            acc = r_ref[0].astype(F32)
            for k in range(1, n):
                acc = acc + r_ref[k].astype(F32)
            total = acc if total is None else total + acc
        refs[-1][...] = total

    if R // tr > 64 and len(rest) == 1 and rest[0] % 256 == 0:
        grid = (rest[0] // 256,)
        in_spec = pl.BlockSpec((n, R, 256), lambda i: (0, 0, i))
        out_spec = pl.BlockSpec((None, R, 256), lambda i: (l, 0, i))
    else:
        grid = (R // tr,)
        in_spec = pl.BlockSpec((n, tr) + rest, lambda i: (0, i) + z)
        out_spec = pl.BlockSpec((None, tr) + rest, lambda i: (l, i) + z)
    in_specs, args, aliases = [in_spec] * m, list(slabs), {}
    if buf is not None:
        in_specs.append(pl.BlockSpec(memory_space=pl.ANY))
        args.append(buf)
        aliases = {m: 0}
    return pl.pallas_call(
        body, name=name, grid=grid, in_specs=in_specs, out_specs=out_spec,
        out_shape=jax.ShapeDtypeStruct((DEPTH, R) + rest, F32), input_output_aliases=aliases,
        compiler_params=pltpu.CompilerParams(
            dimension_semantics=("parallel",),
            vmem_limit_bytes=_vmem(m * _nbytes(in_spec.block_shape, slabs[0].dtype) + _nbytes(out_spec.block_shape, F32),
                                   2 * _nbytes(out_spec.block_shape, F32))),
    )(*args)


def _adam_math(w, g, m, v):
    mn = ADAM_B1 * m + (1.0 - ADAM_B1) * g
    vn = ADAM_B2 * v + (1.0 - ADAM_B2) * (g * g)
    m_hat = mn / (1.0 - ADAM_B1 ** ADAM_STEP)
    v_hat = vn / (1.0 - ADAM_B2 ** ADAM_STEP)
    return -ADAM_LR * (m_hat / (jnp.sqrt(v_hat) + ADAM_EPS) + ADAM_WD * w), mn, vn


def _adamw(w, g, m, v, name, token=None):
    L, R, C = w.shape
    tr = _row_tile(R)
    extra = [] if token is None else [token]

    def body(w_ref, g_ref, m_ref, v_ref, *rest):
        d_ref, mo_ref, vo_ref = rest[-3:]
        d_ref[...], mo_ref[...], vo_ref[...] = _adam_math(w_ref[...], g_ref[...], m_ref[...], v_ref[...])

    if R // tr > 64 and C % 128 == 0:
        spec, grid = pl.BlockSpec((None, R, 128), lambda l, i: (l, 0, i)), (L, C // 128)
    else:
        spec, grid = pl.BlockSpec((None, tr, C), lambda l, i: (l, i, 0)), (L, R // tr)
    return pl.pallas_call(
        body, name=name, grid=grid, in_specs=[spec] * 4 + [pl.BlockSpec(memory_space=pl.ANY)] * len(extra),
        out_specs=[spec] * 3, out_shape=[jax.ShapeDtypeStruct((L, R, C), F32)] * 3,
        compiler_params=pltpu.CompilerParams(dimension_semantics=("parallel", "parallel"),
                                             vmem_limit_bytes=_vmem(7 * _nbytes(spec.block_shape, F32))),
    )(w, g, m, v, *extra)


_VMEM_WHOLE = pl.BlockSpec(memory_space=pltpu.VMEM)


def _matrix_update(gath, w, m, v, name):
    K = w.shape[1]

    def body(g0_ref, g1_ref, w_ref, m_ref, v_ref, go_ref, d_ref, mo_ref, vo_ref):
        for l, gr in enumerate((g0_ref, g1_ref)):
            for k in range(K):
                g = gr[0, k].astype(F32)
                for dev in range(1, 8):
                    g = g + gr[dev, k].astype(F32)
                go_ref[l, k] = g
                d_ref[l, k], mo_ref[l, k], vo_ref[l, k] = _adam_math(w_ref[l, k], g, m_ref[l, k], v_ref[l, k])

    return pl.pallas_call(
        body, name=name, in_specs=[_VMEM_WHOLE] * 5, out_specs=[_VMEM_WHOLE] * 4,
        out_shape=[jax.ShapeDtypeStruct(w.shape, F32)] * 4,
        compiler_params=pltpu.CompilerParams(vmem_limit_bytes=32 * MIB),
    )(gath[0], gath[1], w, m, v)


VECS = (("pre_norm_g", D), ("post_norm_g", D), ("gm_ln_g", GM_W), ("gm_ln_b", GM_W), ("mla_q_norm_g", QR),
        ("mla_kv_norm_g", KVR), ("lru_conv_b", LRU_W), ("lru_b_a", LRU_W), ("lru_b_x", LRU_W), ("lru_lambda", LRU_W))
VEC_KEY = {"pre_norm_g": "pre_g", "post_norm_g": "post_g", "gm_ln_g": "ln_g", "gm_ln_b": "ln_b", "mla_q_norm_g": "qg",
           "mla_kv_norm_g": "kvg", "lru_conv_b": "conv_b", "lru_b_a": "ba", "lru_b_x": "bx", "lru_lambda": "lam"}
VEC_ROWS, VEC_W, VEC_ROW0, LOSS_ROW = 16, LRU_W, GM_G, 14


def _pack_rows(LG, loss_part):
    per = len(VECS) + 1
    ins = []
    for G in LG:
        ins += [G[VEC_KEY[n]] for n, _ in VECS] + [G["bst"]]
    ins.append(loss_part)

    def body(*refs):
        o_ref = refs[-1]
        o_ref[...] = jnp.zeros_like(o_ref)
        for l in range(DEPTH):
            base = VEC_ROWS * l
            o_ref[pl.ds(base, 8), pl.ds(0, GM_B)] = refs[per * l + len(VECS)][...].T[:8, :]
            for t, (_, width) in enumerate(VECS):
                o_ref[pl.ds(base + VEC_ROW0 + t, 1), pl.ds(0, width)] = refs[per * l + t][...]
        o_ref[pl.ds(LOSS_ROW, 1), pl.ds(0, 128)] = jnp.broadcast_to(refs[-2][...], (1, 128))

    return pl.pallas_call(
        body, name="pack_rows", in_specs=[_VMEM_WHOLE] * len(ins), out_specs=_VMEM_WHOLE,
        out_shape=jax.ShapeDtypeStruct((DEPTH * VEC_ROWS, VEC_W), F32),
    )(*ins)


def _vector_update(gath, W, M, V):
    names = [n for n, _ in VECS] + ["gm_bs"]
    nw = len(names)

    def body(*refs):
        g_ref = refs[0]
        wr, mr, vr = refs[1:1 + nw], refs[1 + nw:1 + 2 * nw], refs[1 + 2 * nw:1 + 3 * nw]
        outs = refs[1 + 3 * nw:]
        s = g_ref[0]
        for dev in range(1, 8):
            s = s + g_ref[dev]
        for t, (_, width) in enumerate(VECS):
            for l in range(DEPTH):
                r = VEC_ROWS * l + VEC_ROW0 + t
                g = s[r:r + 1, :width]
                row = (pl.ds(l, 1), slice(None))
                res = (g,) + _adam_math(wr[t][row], g, mr[t][row], vr[t][row])
                for q in range(4):
                    outs[4 * t + q][row] = res[q]
        t = len(VECS)
        for l in range(DEPTH):
            for k in range(GM_G):
                g = s[VEC_ROWS * l + k:VEC_ROWS * l + k + 1, :GM_B]
                row = (l, pl.ds(k, 1), slice(None))
                res = (g,) + _adam_math(wr[t][row], g, mr[t][row], vr[t][row])
                for q in range(4):
                    outs[4 * t + q][row] = res[q]
        outs[4 * nw][...] = s[LOSS_ROW:LOSS_ROW + 1, :128]

    ws = [W[n] for n in names]
    out_shape = []
    for w in ws:
        out_shape += [jax.ShapeDtypeStruct(w.shape, F32)] * 4
    out_shape.append(jax.ShapeDtypeStruct((1, 128), F32))
    res = pl.pallas_call(
        body, name="vector_update", in_specs=[_VMEM_WHOLE] * (1 + 3 * nw), out_specs=[_VMEM_WHOLE] * (4 * nw + 1),
        out_shape=out_shape, compiler_params=pltpu.CompilerParams(vmem_limit_bytes=VMEM_LIMIT),
    )(gath, *ws, *[M[n] for n in names], *[V[n] for n in names])
    return {n: tuple(res[4 * t:4 * t + 4]) for t, n in enumerate(names)}, res[4 * nw]


SHARDED = ("w_in", "mla_w_uq", "mla_w_ukv", "lru_conv_w", "w_proj_a", "w_proj_b", "w_proj_c", "w_out")
FIRST = ("w_in", "lru_conv_w")
LATER = tuple(n for n in SHARDED if n not in FIRST)
COL_SHARDED = ("w_in", "mla_w_uq", "mla_w_ukv", "lru_conv_w")
SMALL = ("pre_norm_g", "gm_ln_g", "gm_ln_b", "gm_ws", "gm_bs", "mla_q_norm_g", "mla_kv_norm_g", "lru_conv_b",
         "lru_w_a", "lru_b_a", "lru_w_x", "lru_b_x", "lru_lambda", "post_norm_g")
WEIGHTS = ("pre_norm_g", "w_in", "gm_ln_g", "gm_ln_b", "gm_ws", "gm_bs", "mla_q_norm_g", "mla_w_uq",
           "mla_kv_norm_g", "mla_w_ukv", "lru_conv_w", "lru_conv_b", "lru_w_a", "lru_b_a", "lru_w_x", "lru_b_x",
           "lru_lambda", "w_proj_a", "w_proj_b", "w_proj_c", "w_out", "post_norm_g")


GB_KEY = {"w_in": "wp", "mla_w_uq": "wuq", "mla_w_ukv": "wukv", "w_proj_a": "wpa", "w_proj_b": "wpb",
          "w_proj_c": "wpc", "w_out": "wout"}


def _prepare(l, gathered, small, wsb):
    P = {GB_KEY[n]: gathered[n] for n in GB_KEY if n in gathered}
    P["conv_w"] = gathered["lru_conv_w"].transpose(1, 0, 2).reshape(CONV_W, LRU_W)
    P["wsb"] = wsb
    row = lambda n: small[n][l][None, :]
    P["pre_g"], P["post_g"] = row("pre_norm_g"), row("post_norm_g")
    P["ln_g"], P["ln_b"] = row("gm_ln_g"), row("gm_ln_b")
    P["ws"] = small["gm_ws"][l]
    P["bst"] = jnp.pad(small["gm_bs"][l].T, ((0, 0), (0, 128 - GM_G)))
    P["qg"], P["kvg"] = row("mla_q_norm_g"), row("mla_kv_norm_g")
    P["conv_b"], P["ba"], P["bx"], P["lam"] = row("lru_conv_b"), row("lru_b_a"), row("lru_b_x"), row("lru_lambda")
    return P


def kernel(x, pre_norm_g, w_in, gm_ln_g, gm_ln_b, gm_ws, gm_bs, mla_q_norm_g, mla_w_uq, mla_kv_norm_g, mla_w_ukv, lru_conv_w, lru_conv_b, lru_w_a, lru_b_a, lru_w_x, lru_b_x, lru_lambda, w_proj_a, w_proj_b, w_proj_c, w_out, post_norm_g, loss_target, m_pre_norm_g, m_w_in, m_gm_ln_g, m_gm_ln_b, m_gm_ws, m_gm_bs, m_mla_q_norm_g, m_mla_w_uq, m_mla_kv_norm_g, m_mla_w_ukv, m_lru_conv_w, m_lru_conv_b, m_lru_w_a, m_lru_b_a, m_lru_w_x, m_lru_b_x, m_lru_lambda, m_w_proj_a, m_w_proj_b, m_w_proj_c, m_w_out, m_post_norm_g, v_pre_norm_g, v_w_in, v_gm_ln_g, v_gm_ln_b, v_gm_ws, v_gm_bs, v_mla_q_norm_g, v_mla_w_uq, v_mla_kv_norm_g, v_mla_w_ukv, v_lru_conv_w, v_lru_conv_b, v_lru_w_a, v_lru_b_a, v_lru_w_x, v_lru_b_x, v_lru_lambda, v_w_proj_a, v_w_proj_b, v_w_proj_c, v_w_out, v_post_norm_g):
    args = dict(locals())
    W = {n: args[n] for n in WEIGHTS}
    M = {n: args["m_" + n] for n in WEIGHTS}
    V = {n: args["v_" + n] for n in WEIGHTS}
    c = lax.axis_index("c")

    def shards(l, names):
        out = []
        for n in names:
            blk = W[n][l].T if n in TRANSPOSED else W[n][l]
            out.append(blk[None] if n == "lru_conv_w" else blk.astype(BF16))
        return out

    small = {n: W[n] for n in SMALL}
    wsb = _superblocks(W["lru_w_a"], W["lru_w_x"])
    tabs = _rope_tables()
    s0a, s0b, s1a, s1b = shards(0, FIRST), shards(0, LATER), shards(1, FIRST), shards(1, LATER)
    g0, zones = _weights_allgather(FIRST, s0a, "weights_allgather_l0", carry=_gather_zeros(LATER, s0b)
                                   + _gather_zeros(FIRST, s1a) + _gather_zeros(LATER, s1b))
    nl, nf = len(LATER), len(FIRST)
    w0b = _gather_start(LATER, s0b, zones[:nl], "weights_gather_start_l0")
    w1a = _gather_start(FIRST, s1a, zones[nl:nl + nf], "weights_gather_start_l1_first", after=w0b[3])
    w1b = _gather_start(LATER, s1b, zones[nl + nf:], "weights_gather_start_l1_later", after=w1a[3])

    def late(started, name):
        def wait(after):
            got = _gather_wait(LATER, *started[:3], after, name)
            return {GB_KEY[n]: g for n, g in zip(LATER, got)}
        return wait

    P = [_prepare(0, dict(zip(FIRST, g0)), small, wsb), None]
    h0 = x[0]
    h1, A0 = _layer_fwd(h0, P[0], 0, tabs, w1b[3], late(w0b, "weights_gather_wait_l0"))
    g1 = _gather_wait(FIRST, *w1a[:3], h1, "weights_gather_wait_l1_first")
    P[1] = _prepare(1, dict(zip(FIRST, g1)), small, wsb)
    h2, A1 = _layer_fwd(h1, P[1], 1, tabs, None, late(w1b, "weights_gather_wait_l1_later"))
    dy, loss_part = _loss_fwd(h2, loss_target[0])

    def large_grads(G, GB, names):
        conv = G["conv_w"].reshape(CONV_W, N_CHIPS, LRU_W // N_CHIPS).transpose(1, 0, 2)
        return [conv if n == "lru_conv_w" else GB[GB_KEY[n]] for n in names]

    started = {}

    def early1(GB):
        started["sc1b"] = _scatter_start(LATER, [GB[GB_KEY[n]] for n in LATER], "grads_scatter_start_l1_later")
        return started["sc1b"][3], None

    d1, G1, GB1 = _layer_bwd(dy, A1, P[1], 1, tabs, None, early1)
    sc1a = _scatter_start(FIRST, large_grads(G1, GB1, FIRST), "grads_scatter_start_l1_first")

    def early0(GB):
        got_b = _scatter_wait(LATER, *started["sc1b"][:3], GB["wukv"], "grads_scatter_wait_l1_later")
        got_a = _scatter_wait(FIRST, *sc1a[:3], got_b[0], "grads_scatter_wait_l1_first")
        got = dict(zip(LATER + FIRST, list(got_b) + list(got_a)))
        started["swap1"] = _swap_start([got[n] for n in SHARDED], "partials_swap_start_l1")
        started["sc0"] = _scatter_start(LATER, [GB[GB_KEY[n]] for n in LATER], "grads_scatter_start_l0")
        return started["sc0"][3], started["swap1"][3]

    d0, G0, GB0 = _layer_bwd(d1, A0, P[0], 0, tabs, sc1a[3], early0)
    LG = (G0, G1)
    mine0 = _scatter_wait(LATER, *started["sc0"][:3], d0, "grads_scatter_wait_l0")
    swap0 = _swap_start(mine0, "partials_swap_start_l0")
    g0f = large_grads(G0, GB0, FIRST)
    c_arr = jnp.reshape(c, (1,)).astype(jnp.int32)
    from_sib = _half_to_sibling(FIRST, g0f, "grads_half_to_sibling_l0", after=swap0[3])
    pair = [_pair_add_half(g, rb, c_arr, "pair_add_" + n) for n, g, rb in zip(FIRST, g0f, from_sib)]
    slabs = _chip_scatter_half(FIRST, pair, "grads_chip_scatter_l0")
    mine1, theirs1 = _swap_wait(*started["swap1"][:3], slabs[0], "partials_swap_wait_l1")
    both = dict(zip(SHARDED, [_sum_slabs([a, b], 1, None, "sum_partials_l1_" + n)
                              for n, a, b in zip(SHARDED, mine1, theirs1)]))
    for n, s in zip(FIRST, slabs):
        both[n] = _sum_slabs([s], 0, both[n], "sum_slabs_l0_" + n)
    done = _subset_exchange(FIRST, [both[n] for n in FIRST], 0, "reduced_rows_to_sibling_l0")
    both.update(zip(FIRST, done))
    mine0, theirs0 = _swap_wait(*swap0[:3], done[0], "partials_swap_wait_l0")
    for n, a, b in zip(LATER, mine0, theirs0):
        both[n] = _sum_slabs([a, b], 0, both[n], "sum_partials_l0_" + n)
    both = [both[n] for n in SHARDED]
    grads = {}
    for n, b in zip(SHARDED, both):
        if n in TRANSPOSED and n != "w_in":
            b = jnp.swapaxes(b, 1, 2)
        grads[n] = b if n == "w_in" else b.reshape(W[n].shape)

    rows = _pack_rows(LG, loss_part)
    mats = []
    for g in LG:
        mats += [g["ws"].astype(BF16), g["wab"][0, :, :, :LRU_BW], g["wab"][1, :, :, :LRU_BW]]
    bc = _bcast_start([rows] + mats, "small_grads_start", after=done[0])

    upd, last = {}, None
    for n in SHARDED:
        token = bc[3] if n == SHARDED[0] else None
        if n == "w_in":
            tr = lambda a: jnp.swapaxes(a, 1, 2)
            res = _adamw(tr(W[n]), grads[n], tr(M[n]), tr(V[n]), "adamw_" + n, token)
            upd[n] = tuple(tr(a) for a in (grads[n],) + tuple(res))
        else:
            res = _adamw(W[n], grads[n], M[n], V[n], "adamw_" + n, token)
            upd[n] = (grads[n],) + tuple(res)
        last = res[0]

    gath = _bcast_wait(*bc[:3], last, "small_grads_wait")
    vec_upd, loss_row = _vector_update(gath[0], W, M, V)
    upd.update(vec_upd)
    loss = loss_row[0, 0]
    for k, n in enumerate(("gm_ws", "lru_w_a", "lru_w_x")):
        upd[n] = _matrix_update((gath[1 + k], gath[4 + k]), W[n], M[n], V[n], "update_" + n)

    return (loss, d0[None], *[upd[n][0] for n in WEIGHTS], *[upd[n][1] for n in WEIGHTS],
            *[upd[n][2] for n in WEIGHTS], *[upd[n][3] for n in WEIGHTS])
```

```python
import functools
import math

import jax
import jax.numpy as jnp
from jax import lax
from jax.experimental import pallas as pl
from jax.experimental.pallas import tpu as pltpu

F32, BF16 = jnp.float32, jnp.bfloat16
MESH = pl.DeviceIdType.MESH

S, D, DEPTH = 2048, 1024, 2
CHUNK, EPS = 64, 1e-6
GM_W, GM_G, GM_B = 1024, 4, 128
H, NOPE, ROPE, VDIM = 8, 128, 64, 128
QR, KVR = 384, 256
MLA_W = H * VDIM
LRU_W, LRU_NB, LRU_BW, LRU_C, CONV_W = 1280, 16, 80, 8.0, 4
ROPE_THETA = 10000.0
IN_SIZES = (GM_W, GM_W, GM_W, QR, KVR, ROPE, MLA_W, LRU_W, LRU_W, D, D, D)
N_IN = sum(IN_SIZES)
N_CHIPS = 4
ADAM_LR, ADAM_B1, ADAM_B2, ADAM_EPS, ADAM_WD, ADAM_STEP = 0.001, 0.9, 0.999, 1e-08, 0.01, 10

HP = 256
O_U, O_V, O_ZA, O_GA, O_GB, O_GC = 0, 1024, 2048, 3072, 4096, 5120
O_CKV, O_KR, O_CQ, O_XC, O_ZC, O_ZB = 6144, 6400, 6528, 7680, 8960, 10240
NP = 11264
MIB = 1024 * 1024
VMEM_LIMIT = 16 * MIB


def _vmem(block_bytes, temp_bytes=0):
    return int(min(max(2 * block_bytes + temp_bytes + 4 * MIB, VMEM_LIMIT), 56 * MIB))


def _nbytes(shape, dtype):
    return math.prod(d for d in shape if d is not None) * jnp.dtype(dtype).itemsize


def _tile(dim, target):
    if dim <= target:
        return dim
    t = (target // 128) * 128
    while dim % t:
        t -= 128
    return t


def _sig(x):
    return jax.nn.sigmoid(x)


def _silu(x):
    return x * _sig(x)


def _dsilu(x):
    s = _sig(x)
    return s * (1.0 + x * (1.0 - s))


def _mm(a, b, mode, name, out_dtype=F32, tm=1024, tn=1024, tk=1024, b_lead=None, out_lead=None, token=None):
    b2 = b.shape[1:] if b_lead is not None else b.shape
    if mode == "nn":
        (M, K), (K2, N) = a.shape, b2
    elif mode == "nt":
        (M, K), (N, K2) = a.shape, b2
    else:
        (K, M), (K2, N) = a.shape, b2
    assert K == K2, (name, a.shape, b.shape)
    tm, tn, tk = _tile(M, tm), _tile(N, tn), _tile(K, tk)
    nk = K // tk
    if mode == "tn":
        a_spec = pl.BlockSpec((tk, tm), lambda i, j, k: (k, i))
        lhs_c = 0
    else:
        a_spec = pl.BlockSpec((tm, tk), lambda i, j, k: (i, k))
        lhs_c = 1
    b_blk, b_idx, rhs_c = ((tn, tk), (lambda i, j, k: (j, k)), 1) if mode == "nt" else ((tk, tn), (lambda i, j, k: (k, j)), 0)
    if b_lead is None:
        b_spec = pl.BlockSpec(b_blk, b_idx)
    else:
        b_spec = pl.BlockSpec((None,) + b_blk, functools.partial(lambda i, j, k, f, l: (l,) + f(i, j, k), f=b_idx, l=b_lead))
    dims = (((lhs_c,), (rhs_c,)), ((), ()))
    in_specs, args, aliases = [a_spec, b_spec], [a, b], {}
    if out_lead is None:
        out_spec = pl.BlockSpec((tm, tn), lambda i, j, k: (i, j))
        out_shape = jax.ShapeDtypeStruct((M, N), out_dtype)
    else:
        l_out, n_lead, buf = out_lead
        out_spec = pl.BlockSpec((None, tm, tn), functools.partial(lambda i, j, k, l: (l, i, j), l=l_out))
        out_shape = jax.ShapeDtypeStruct((n_lead, M, N), out_dtype)
        if buf is not None:
            in_specs.append(pl.BlockSpec(memory_space=pl.ANY))
            args.append(buf)
            aliases = {2: 0}
    if token is not None:
        in_specs.append(pl.BlockSpec(memory_space=pl.ANY))
        args.append(token)

    def body(a_ref, b_ref, *rest):
        o_ref, acc_ref = rest[-2:]
        k = pl.program_id(2)

        @pl.when(k == 0)
        def _():
            acc_ref[...] = jnp.zeros_like(acc_ref)

        acc_ref[...] += lax.dot_general(a_ref[...].astype(BF16), b_ref[...].astype(BF16), dims,
                                        preferred_element_type=F32)

        @pl.when(k == nk - 1)
        def _():
            o_ref[...] = acc_ref[...].astype(o_ref.dtype)

    return pl.pallas_call(
        body, name=name, grid=(M // tm, N // tn, nk),
        in_specs=in_specs, out_specs=out_spec, out_shape=out_shape,
        scratch_shapes=[pltpu.VMEM((tm, tn), F32)], input_output_aliases=aliases,
        compiler_params=pltpu.CompilerParams(
            dimension_semantics=("parallel", "parallel", "arbitrary"),
            vmem_limit_bytes=_vmem(_nbytes((tm, tk), a.dtype) + _nbytes((tk, tn), b.dtype) + _nbytes((tm, tn), out_dtype),
                                   _nbytes((tm, tn), F32) + _nbytes((tm, tk), BF16) + _nbytes((tk, tn), BF16))),
    )(*args)


def _rows(fn, name, tm, rows, halos=(), fulls=(), outs=(), accs=()):
    n = S // tm
    in_specs, args = [], []
    for arr, w, cb in rows:
        in_specs.append(pl.BlockSpec((tm, w), functools.partial(lambda i, cb: (i, cb), cb=cb)))
        args.append(arr)
    for arr, w, cb, side in halos:
        if side == "prev":
            im = functools.partial(lambda i, cb: (jnp.maximum(i * (tm // 16) - 1, 0), cb), cb=cb)
        else:
            im = functools.partial(lambda i, cb: (jnp.minimum((i + 1) * (tm // 16), S // 16 - 1), cb), cb=cb)
        in_specs.append(pl.BlockSpec((16, w), im))
        args.append(arr)
    for arr in fulls:
        in_specs.append(pl.BlockSpec(arr.shape, functools.partial(lambda i, nd: (0,) * nd, nd=arr.ndim)))
        args.append(arr)
    out_shape, out_specs, aliases, n_alias = [], [], {}, 0
    for k, o in enumerate(outs):
        if len(o) == 3 and o[2] == "T":
            out_shape.append(jax.ShapeDtypeStruct((o[0], S), o[1]))
            out_specs.append(pl.BlockSpec((o[0], tm), lambda i: (0, i)))
        elif len(o) == 3:
            buf, total, cb = o[2]
            out_shape.append(jax.ShapeDtypeStruct((S, total), o[1]))
            out_specs.append(pl.BlockSpec((tm, o[0]), functools.partial(lambda i, cb: (i, cb), cb=cb)))
            if buf is not None:
                aliases[len(args)] = k
                in_specs.append(pl.BlockSpec(memory_space=pl.ANY))
                args.append(buf)
                n_alias += 1
        else:
            out_shape.append(jax.ShapeDtypeStruct((S, o[0]), o[1]))
            out_specs.append(pl.BlockSpec((tm, o[0]), lambda i: (i, 0)))
    for shp in accs:
        out_shape.append(jax.ShapeDtypeStruct(shp, F32))
        out_specs.append(pl.BlockSpec(shp, functools.partial(lambda i, nd: (0,) * nd, nd=len(shp))))
    nr, nh, nf, no, na = len(rows), len(halos), len(fulls), len(outs), len(accs)
    blocks = (sum(_nbytes((tm, w), arr.dtype) for arr, w, _ in rows) + sum(_nbytes(a.shape, a.dtype) for a in fulls)
              + sum(_nbytes((tm, o[0]), o[1]) for o in outs) + sum(_nbytes(shp, F32) for shp in accs))
    widest = _nbytes((tm, max([w for _, w, _ in rows] + [o[0] for o in outs])), F32)

    def body(*refs):
        i = pl.program_id(0)
        ins, orefs = refs[:nr + nh + nf], refs[nr + nh + nf + n_alias:]
        rv = [r[...].astype(F32) for r in ins[:nr]]
        hv = [r[...].astype(F32)[8:] if h[3] == "prev" else r[...].astype(F32)[:8] for r, h in zip(ins[nr:nr + nh], halos)]
        fv = [r[...] for r in ins[nr + nh:]]
        o, a = fn(i, rv, hv, fv)
        assert len(o) == no and len(a) == na, name
        for spec, ref, val in zip(outs, orefs[:no], o):
            ref[...] = (val.T if len(spec) == 3 and spec[2] == "T" else val).astype(ref.dtype)
        if na:
            @pl.when(i == 0)
            def _():
                for ref in orefs[no:]:
                    ref[...] = jnp.zeros_like(ref)

            for ref, val in zip(orefs[no:], a):
                ref[...] += val

    res = pl.pallas_call(
        body, name=name, grid=(n,), in_specs=in_specs, out_specs=out_specs, out_shape=out_shape,
        input_output_aliases=aliases,
        compiler_params=pltpu.CompilerParams(dimension_semantics=("arbitrary",), vmem_limit_bytes=_vmem(blocks, 6 * widest)),
    )(*args)
    return res


def _shift_down(xb, halo, s, row):
    fix = jnp.tile(pltpu.roll(halo, s, 0), (xb.shape[0] // 8, 1))
    return jnp.where(row >= s, pltpu.roll(xb, s, 0), fix)


def _shift_up(xb, halo, s, row):
    tm = xb.shape[0]
    fix = jnp.tile(pltpu.roll(halo, 8 - s, 0), (tm // 8, 1))
    return jnp.where(row < tm - s, pltpu.roll(xb, tm - s, 0), fix)


def _rms(x):
    return lax.rsqrt(jnp.mean(x * x, axis=-1, keepdims=True) + EPS)


def _rms_bwd(dy, x, g):
    r = _rms(x)
    xh = x * r
    dxh = dy * g
    dx = r * (dxh - xh * jnp.mean(dxh * xh, axis=-1, keepdims=True))
    return dx, dy * xh


def _colsum(x):
    return jnp.sum(x, axis=0, keepdims=True)


def _prenorm_fwd(x, g, token=None):
    def fn(i, rv, hv, fv):
        return [rv[0] * _rms(rv[0]) * fv[0]], []
    return _rows(fn, "prenorm_fwd", 256, [(x, D, 0)], fulls=[g] + ([] if token is None else [token]), outs=[(D, BF16)])[0]


def _gm_mask():
    r = lax.broadcasted_iota(jnp.int32, (GM_B, GM_B), 0) // CHUNK
    c = lax.broadcasted_iota(jnp.int32, (GM_B, GM_B), 1) // CHUNK
    return c <= r


def _gm_norm(v, g, b):
    mu = jnp.mean(v, axis=-1, keepdims=True)
    vc = v - mu
    rs = lax.rsqrt(jnp.mean(vc * vc, axis=-1, keepdims=True) + EPS)
    vh = vc * rs
    return vh, rs, vh * g + b


def _gm_sv(vn, ws, bst):
    mask = _gm_mask()
    gw = GM_W // GM_G
    parts = []
    for g in range(GM_G):
        wm = jnp.where(mask, ws[g], 0.0).astype(BF16)
        parts.append(jnp.dot(wm, vn[:, g * gw:(g + 1) * gw].astype(BF16), preferred_element_type=F32)
                     + bst[:, g:g + 1])
    return jnp.concatenate(parts, axis=1)


def _gmlp_fwd(proj, ln_g, ln_b, ws, bst):
    def fn(i, rv, hv, fv):
        u, v, z = rv
        g, b, w, bt = fv
        _, _, vn = _gm_norm(v, g, b)
        return [u * _gm_sv(vn, w, bt) * _silu(z)], []
    return _rows(fn, "gmlp_fwd", GM_B, [(proj, GM_W, 0), (proj, GM_W, 1), (proj, GM_W, 2)],
                 fulls=[ln_g, ln_b, ws, bst], outs=[(GM_W, BF16)])[0]


def _mla_prep_fwd(proj, qg, kvg):
    def fn(i, rv, hv, fv):
        cq, ckv = rv
        g1, g2 = fv
        return [cq * _rms(cq) * g1, ckv * _rms(ckv) * g2], []
    return _rows(fn, "mla_prep_fwd", 256, [(proj, QR, O_CQ // QR), (proj, KVR, O_CKV // KVR)],
                 fulls=[qg, kvg], outs=[(QR, BF16), (KVR, BF16)])


def _rot(t, cc, sa, sb):
    return t * cc + pltpu.roll(t, 32, 1) * sa + pltpu.roll(t, 96, 1) * sb


def _rot_t(g, cc, sa, sb):
    return g * cc + pltpu.roll(g * sa, 96, 1) + pltpu.roll(g * sb, 32, 1)


def _rope_tables():
    pos = jnp.arange(S, dtype=F32)
    inv_freq = ROPE_THETA ** (-jnp.arange(0, ROPE, 2, dtype=F32) / ROPE)
    ang = pos[:, None] * inv_freq[None, :]
    cos, sin, z = jnp.cos(ang), jnp.sin(ang), jnp.zeros((S, 32), F32)
    cc = jnp.concatenate([cos, cos, z, z], axis=1)
    sa = jnp.concatenate([z, sin, z, z], axis=1)
    sb = jnp.concatenate([-sin, z, z, z], axis=1)
    return cc, sa, sb


ATT_SCALE = 1.0 / math.sqrt(NOPE + ROPE)


def _rope_fwd(q, kv, proj, tabs):
    def fn(i, rv, hv, fv):
        qb, kvb, kr, cc, sa, sb = rv
        krr = _rot(kr, cc, sa, sb)
        qs, ks = [], []
        for h in range(H):
            qs += [qb[:, h * HP:h * HP + 128] * ATT_SCALE, _rot(qb[:, h * HP + 128:(h + 1) * HP], cc, sa, sb) * ATT_SCALE]
            ks += [kvb[:, h * 128:(h + 1) * 128], krr]
        kc = jnp.concatenate(ks, axis=1)
        vv = kvb[:, H * NOPE:]
        return [jnp.concatenate(qs, axis=1), kc, kc, vv, vv], []
    cc, sa, sb = tabs
    return _rows(fn, "rope_fwd", 256,
                 [(q, H * HP, 0), (kv, H * 256, 0), (proj, 128, O_KR // 128), (cc, 128, 0), (sa, 128, 0), (sb, 128, 0)],
                 outs=[(H * HP, BF16), (H * HP, BF16), (H * HP, BF16, "T"), (MLA_W, BF16), (MLA_W, BF16, "T")])


TQ, TC, ATT_NB = 512, 512, 1
ATT_KB = TC * ATT_NB
_NT = (((1,), (1,)), ((), ()))


def _attn_allowed(i, kc):
    kpos = kc * TC + lax.broadcasted_iota(jnp.int32, (TC, TQ), 0)
    qpos = i * TQ + lax.broadcasted_iota(jnp.int32, (TC, TQ), 1)
    return (kpos // CHUNK) <= (qpos // CHUNK)


def _attn_fwd(qc, kc, vt):
    def body(q_ref, k_ref, vt_ref, o_ref, l_ref):
        i = pl.program_id(1)
        q = q_ref[...]

        def scores(sb):
            t0s = [pl.multiple_of((sb * ATT_NB + c) * TC, TC) for c in range(ATT_NB)]
            return [lax.dot_general(k_ref[pl.ds(t0, TC), :], q, _NT, preferred_element_type=F32) for t0 in t0s]

        def block(sb, ss, carry, masked):
            m, l, acc = carry
            t0s = [pl.multiple_of((sb * ATT_NB + c) * TC, TC) for c in range(ATT_NB)]
            if masked:
                ss = [jnp.where(_attn_allowed(i, sb * ATT_NB + c), s, -1e30) for c, s in enumerate(ss)]
            m_new = m
            for s in ss:
                m_new = jnp.maximum(m_new, jnp.max(s, axis=0, keepdims=True))
            alpha = jnp.exp(m - m_new)
            ps = [jnp.exp(s - m_new) for s in ss]
            l = alpha * l
            acc = alpha * acc
            for t0, p in zip(t0s, ps):
                l = l + jnp.sum(p, axis=0, keepdims=True)
                acc = acc + jnp.dot(vt_ref[:, pl.ds(t0, TC)], p.astype(BF16), preferred_element_type=F32)
            return m_new, l, acc

        nsb = ((i + 1) * TQ + ATT_KB - 1) // ATT_KB
        c = (jnp.full((1, TQ), -1e30, F32), jnp.zeros((1, TQ), F32), jnp.zeros((VDIM, TQ), F32))

        def step(sb, sc):
            nxt = scores(sb + 1)
            return nxt, block(sb, sc[0], sc[1], False)

        ss, c = lax.fori_loop(0, nsb - 1, step, (scores(0), c))
        m, l, acc = block(nsb - 1, ss, c, True)
        o_ref[...] = (acc / l).T
        l_ref[...] = m + jnp.log(l)

    return pl.pallas_call(
        body, name="attn_fwd", grid=(H, S // TQ),
        in_specs=[pl.BlockSpec((TQ, HP), lambda h, i: (i, h)),
                  pl.BlockSpec((S, HP), lambda h, i: (0, h)),
                  pl.BlockSpec((VDIM, S), lambda h, i: (h, 0))],
        out_specs=[pl.BlockSpec((TQ, VDIM), lambda h, i: (i, h)), pl.BlockSpec((None, 1, TQ), lambda h, i: (h, 0, i))],
        out_shape=[jax.ShapeDtypeStruct((S, MLA_W), F32), jax.ShapeDtypeStruct((H, 1, S), F32)],
        compiler_params=pltpu.CompilerParams(dimension_semantics=("parallel", "arbitrary"),
                                             vmem_limit_bytes=24 * MIB),
    )(qc, kc, vt)


def _gate_mul_fwd(name, val, proj, width, cb):
    def fn(i, rv, hv, fv):
        o, z = rv
        return [o * _silu(z)], []
    return _rows(fn, name, 256, [(val, width, 0), (proj, width, cb)], outs=[(width, BF16)])[0]


def _conv_fwd(proj, w, b):
    def fn(i, rv, hv, fv):
        (xb,), (halo,), (ww, bb) = rv, hv, fv
        halo = jnp.where(i > 0, halo, 0.0)
        row = lax.broadcasted_iota(jnp.int32, xb.shape, 0)
        acc = bb + ww[3:4] * xb
        for s in range(1, CONV_W):
            acc = acc + ww[3 - s:4 - s] * _shift_down(xb, halo, s, row)
        return [acc, acc], []
    return _rows(fn, "conv_fwd", LRU_TM, [(proj, LRU_W, O_XC // LRU_W)], halos=[(proj, LRU_W, O_XC // LRU_W, "prev")],
                 fulls=[w, b], outs=[(LRU_W, F32), (LRU_W, BF16)])


def _lru_terms(ga, gx, xc, ba, bx, lam):
    r = _sig(ga + ba)
    ig = _sig(gx + bx)
    sp = jnp.maximum(-lam, 0.0) + jnp.log(1.0 + jnp.exp(-jnp.abs(lam)))
    log_a = -LRU_C * r * sp
    a = jnp.exp(log_a)
    e2 = jnp.exp(2.0 * log_a)
    om = 1.0 - e2
    mult = jnp.sqrt(jnp.maximum(om, 0.0))
    return r, ig, sp, a, e2, om, mult


def _lru_gates_fwd(gates, xc, ba, bx, lam):
    def fn(i, rv, hv, fv):
        ga, gx, x = rv
        r, ig, sp, a, e2, om, mult = _lru_terms(ga, gx, x, *fv)
        return [a, mult * (ig * x)], []
    return _rows(fn, "lru_gates_fwd", LRU_TM, [(gates, LRU_W, 0), (gates, LRU_W, 1), (xc, LRU_W, 0)],
                 fulls=[ba, bx, lam], outs=[(LRU_W, F32), (LRU_W, F32)])


SCAN_T, SCAN_CW = 64, 256
LRU_TM = 256


def _scan_fwd(a, b):
    def body(a_ref, b_ref, h_ref):
        row = lax.broadcasted_iota(jnp.int32, (SCAN_T, SCAN_CW), 0)

        def step(blk, hc):
            t0 = pl.multiple_of(blk * SCAN_T, SCAN_T)
            A = a_ref[pl.ds(t0, SCAN_T), :]
            B = b_ref[pl.ds(t0, SCAN_T), :]
            d = 1
            while d < SCAN_T:
                keep = row >= d
                A_s = jnp.where(keep, pltpu.roll(A, d, 0), 1.0)
                B_s = jnp.where(keep, pltpu.roll(B, d, 0), 0.0)
                B = A * B_s + B
                A = A * A_s
                d *= 2
            hh = A * hc + B
            h_ref[pl.ds(t0, SCAN_T), :] = hh
            return hh[SCAN_T - 1:SCAN_T, :]

        lax.fori_loop(0, S // SCAN_T, step, jnp.zeros((1, SCAN_CW), F32))

    spec = pl.BlockSpec((S, SCAN_CW), lambda j: (0, j))
    return pl.pallas_call(
        body, name="scan_fwd", grid=(LRU_W // SCAN_CW,), in_specs=[spec, spec], out_specs=spec,
        out_shape=jax.ShapeDtypeStruct((S, LRU_W), F32),
        compiler_params=pltpu.CompilerParams(dimension_semantics=("parallel",),
                                             vmem_limit_bytes=_vmem(3 * _nbytes((S, SCAN_CW), F32))),
    )(a, b)


def _merge_fwd(pa, pb, pc, proj):
    def fn(i, rv, hv, fv):
        a, b, c, ga, gb, gc = rv
        return [_sig(ga) * a + _sig(gb) * b + _sig(gc) * c], []
    return _rows(fn, "merge_fwd", 256,
                 [(pa, D, 0), (pb, D, 0), (pc, D, 0), (proj, D, O_GA // D), (proj, D, O_GB // D), (proj, D, O_GC // D)],
                 outs=[(D, BF16)])[0]


def _post_fwd(x, o2, g):
    def fn(i, rv, hv, fv):
        xb, ob = rv
        return [xb + ob * _rms(ob) * fv[0]], []
    return _rows(fn, "post_fwd", 256, [(x, D, 0), (o2, D, 0)], fulls=[g], outs=[(D, F32)])[0]


SB = 640
BD_TM = 512


def _bd_fwd(xcb, wsb, l):
    def body(x_ref, w_ref, o_ref):
        o_ref[...] = jnp.dot(x_ref[...], w_ref[...], preferred_element_type=F32).astype(o_ref.dtype)

    return pl.pallas_call(
        body, name="lru_gate_mm", grid=(S // BD_TM, 4),
        in_specs=[pl.BlockSpec((BD_TM, SB), lambda i, q: (i, q % 2)),
                  pl.BlockSpec((None, None, SB, SB), lambda i, q: (l, q, 0, 0))],
        out_specs=pl.BlockSpec((BD_TM, SB), lambda i, q: (i, q)),
        out_shape=jax.ShapeDtypeStruct((S, 2 * LRU_W), BF16),
        compiler_params=pltpu.CompilerParams(dimension_semantics=("parallel", "parallel"), vmem_limit_bytes=VMEM_LIMIT),
    )(xcb, wsb)


def _bd_dx(dgates, wsb, l):
    def body(d_ref, w_ref, o_ref, acc_ref):
        g = pl.program_id(2)

        @pl.when(g == 0)
        def _():
            acc_ref[...] = jnp.zeros_like(acc_ref)

        acc_ref[...] += lax.dot_general(d_ref[...], w_ref[...], (((1,), (1,)), ((), ())), preferred_element_type=F32)

        @pl.when(g == 1)
        def _():
            o_ref[...] = acc_ref[...].astype(o_ref.dtype)

    return pl.pallas_call(
        body, name="lru_gate_dx", grid=(S // BD_TM, 2, 2),
        in_specs=[pl.BlockSpec((BD_TM, SB), lambda i, s, g: (i, 2 * g + s)),
                  pl.BlockSpec((None, None, SB, SB), lambda i, s, g: (l, 2 * g + s, 0, 0))],
        out_specs=pl.BlockSpec((BD_TM, SB), lambda i, s, g: (i, s)),
        out_shape=jax.ShapeDtypeStruct((S, LRU_W), BF16),
        scratch_shapes=[pltpu.VMEM((BD_TM, SB), F32)],
        compiler_params=pltpu.CompilerParams(dimension_semantics=("parallel", "parallel", "arbitrary"),
                                             vmem_limit_bytes=VMEM_LIMIT),
    )(dgates, wsb)


def _bd_dw(xcb, dgates):
    tk = 1024

    def body(x_ref, d_ref, o_ref):
        @pl.when(pl.program_id(1) == 0)
        def _():
            o_ref[...] = jnp.zeros_like(o_ref)

        o_ref[...] += lax.dot_general(x_ref[...], d_ref[...], (((0,), (0,)), ((), ())), preferred_element_type=F32)

    return pl.pallas_call(
        body, name="lru_gate_dw", grid=(4, S // tk),
        in_specs=[pl.BlockSpec((tk, SB), lambda q, k: (k, q % 2)), pl.BlockSpec((tk, SB), lambda q, k: (k, q))],
        out_specs=pl.BlockSpec((None, SB, SB), lambda q, k: (q, 0, 0)),
        out_shape=jax.ShapeDtypeStruct((4, SB, SB), F32),
        compiler_params=pltpu.CompilerParams(dimension_semantics=("parallel", "arbitrary"), vmem_limit_bytes=VMEM_LIMIT),
    )(xcb, dgates)


def _bd_extract(dwsb):
    def body(w_ref, o_ref):
        lane = lax.broadcasted_iota(jnp.int32, (LRU_BW, 128), 1)
        for q in range(4):
            for kk in range(8):
                c0 = LRU_BW * kk
                w0, off = (c0 // 128) * 128, c0 % 128
                rows = pl.ds(LRU_BW * kk, LRU_BW)
                blk = w_ref[q, rows, w0:w0 + 128]
                if off:
                    blk = pltpu.roll(blk, 128 - off, 1)
                    if off + LRU_BW > 128:
                        nxt = pltpu.roll(w_ref[q, rows, w0 + 128:w0 + 256], 128 - off, 1)
                        blk = jnp.where(lane < 128 - off, blk, nxt)
                o_ref[q // 2, 8 * (q % 2) + kk] = blk.astype(BF16)

    return pl.pallas_call(
        body, name="lru_gate_dw_blocks",
        in_specs=[pl.BlockSpec(memory_space=pltpu.VMEM)], out_specs=pl.BlockSpec(memory_space=pltpu.VMEM),
        out_shape=jax.ShapeDtypeStruct((2, LRU_NB, LRU_BW, 128), BF16),
        compiler_params=pltpu.CompilerParams(vmem_limit_bytes=VMEM_LIMIT),
    )(dwsb)


def _layer_fwd(x, P, l, tabs, token=None, late=None):
    A = {"x": x}
    A["h"] = _prenorm_fwd(x, P["pre_g"], token)
    proj = A["proj"] = _mm(A["h"], P["wp"], "nt", "in_proj", out_dtype=BF16, tm=1024)
    A["ya"] = _gmlp_fwd(proj, P["ln_g"], P["ln_b"], P["ws"], P["bst"])
    A["xc"], A["xcb"] = _conv_fwd(proj, P["conv_w"], P["conv_b"])
    A["gates"] = _bd_fwd(A["xcb"], P["wsb"], l)
    A["a"], bterm = _lru_gates_fwd(A["gates"], A["xc"], P["ba"], P["bx"], P["lam"])
    A["hs"] = _scan_fwd(A["a"], bterm)
    A["yc"] = _gate_mul_fwd("yc_fwd", A["hs"], proj, LRU_W, O_ZC // LRU_W)
    if late is not None:
        P.update(late(A["yc"]))
    A["cqn"], A["ckvn"] = _mla_prep_fwd(proj, P["qg"], P["kvg"])
    q = _mm(A["cqn"], P["wuq"], "nt", "q_up", out_dtype=BF16)
    kv = _mm(A["ckvn"], P["wukv"], "nt", "kv_up", out_dtype=BF16)
    A["qc"], A["kc"], A["kct"], A["vv"], vt = _rope_fwd(q, kv, proj, tabs)
    A["o"], A["lse"] = _attn_fwd(A["qc"], A["kc"], vt)
    A["yb"] = _gate_mul_fwd("yb_fwd", A["o"], proj, MLA_W, O_ZB // MLA_W)
    A["pa"] = _mm(A["ya"], P["wpa"], "nn", "proj_a", out_dtype=BF16)
    A["pb"] = _mm(A["yb"], P["wpb"], "nn", "proj_b", out_dtype=BF16)
    A["pc"] = _mm(A["yc"], P["wpc"], "nn", "proj_c", out_dtype=BF16)
    A["merged"] = _merge_fwd(A["pa"], A["pb"], A["pc"], proj)
    A["o2"] = _mm(A["merged"], P["wout"], "nn", "out_proj")
    return _post_fwd(x, A["o2"], P["post_g"]), A


def _loss_fwd(y, tgt):
    def fn(i, rv, hv, fv):
        yb, tb = rv
        e = yb - tb
        part = 0.5 * jnp.sum(jnp.mean(e * e, axis=-1, keepdims=True), axis=0, keepdims=True)
        return [e * (1.0 / D)], [part]
    return _rows(fn, "loss", 256, [(y, D, 0), (tgt, D, 0)], outs=[(D, F32)], accs=[(1, 1)])


def _post_bwd(dxn, o2, g, token=None):
    def fn(i, rv, hv, fv):
        dy, ob = rv
        dx, dg = _rms_bwd(dy, ob, fv[0])
        return [dx], [_colsum(dg)]
    return _rows(fn, "post_bwd", 256, [(dxn, D, 0), (o2, D, 0)], fulls=[g] + ([] if token is None else [token]),
                 outs=[(D, BF16)], accs=[(1, D)])


def _merge_bwd(dm, pa, pb, pc, proj, dproj):
    def fn(i, rv, hv, fv):
        d, a, b, c, ga, gb, gc = rv
        outs_p, outs_g = [], []
        for p, gg in ((a, ga), (b, gb), (c, gc)):
            s = _sig(gg)
            outs_p.append(d * s)
            outs_g.append(d * p * s * (1.0 - s))
        return outs_p + [jnp.concatenate(outs_g, axis=1)], []
    return _rows(fn, "merge_bwd", 256,
                 [(dm, D, 0), (pa, D, 0), (pb, D, 0), (pc, D, 0),
                  (proj, D, O_GA // D), (proj, D, O_GB // D), (proj, D, O_GC // D)],
                 outs=[(D, BF16)] * 3 + [(3 * D, BF16, (dproj, NP, O_GA // (3 * D)))])


def _gmlp_bwd(dya, proj, ln_g, ln_b, ws, bst, dproj):
    gw = GM_W // GM_G

    def fn(i, rv, hv, fv):
        dy, u, v, z = rv
        g, b, w, bt = fv
        vh, rs, vn = _gm_norm(v, g, b)
        sv = _gm_sv(vn, w, bt)
        sz = _silu(z)
        du = dy * sv * sz
        dsv = dy * u * sz
        dz = dy * u * sv * _dsilu(z)
        mask = _gm_mask()
        lane = lax.broadcasted_iota(jnp.int32, (GM_B, 128), 1)
        dvn_parts, dws, dbst = [], [], jnp.zeros((GM_B, 128), F32)
        for k in range(GM_G):
            wm = jnp.where(mask, w[k], 0.0).astype(BF16)
            dsk = dsv[:, k * gw:(k + 1) * gw]
            dskb = dsk.astype(BF16)
            dvn_parts.append(lax.dot_general(wm, dskb, (((0,), (0,)), ((), ())), preferred_element_type=F32))
            dwk = lax.dot_general(dskb, vn[:, k * gw:(k + 1) * gw].astype(BF16), (((1,), (1,)), ((), ())),
                                  preferred_element_type=F32)
            dws.append(jnp.where(mask, dwk, 0.0)[None])
            dbst = dbst + jnp.where(lane == k, jnp.sum(dsk, axis=1, keepdims=True), 0.0)
        dvn = jnp.concatenate(dvn_parts, axis=1)
        dvh = dvn * g
        dv = rs * (dvh - jnp.mean(dvh, axis=-1, keepdims=True) - vh * jnp.mean(dvh * vh, axis=-1, keepdims=True))
        return ([jnp.concatenate([du, dv, dz], axis=1)],
                [jnp.concatenate(dws, axis=0), dbst, _colsum(dvn * vh), _colsum(dvn)])
    return _rows(fn, "gmlp_bwd", GM_B, [(dya, GM_W, 0), (proj, GM_W, 0), (proj, GM_W, 1), (proj, GM_W, 2)],
                 fulls=[ln_g, ln_b, ws, bst], outs=[(3 * GM_W, BF16, (dproj, NP, O_U // (3 * GM_W)))],
                 accs=[(GM_G, GM_B, GM_B), (GM_B, 128), (1, GM_W), (1, GM_W)])


def _yb_bwd(dyb, o, proj, dproj):
    def fn(i, rv, hv, fv):
        dy, ob, z = rv
        do = dy * _silu(z)
        prod = do * ob
        lane = lax.broadcasted_iota(jnp.int32, (dy.shape[0], 128), 1)
        dl = jnp.zeros((dy.shape[0], 128), F32)
        for h in range(H):
            dl = dl + jnp.where(lane == h, jnp.sum(prod[:, h * VDIM:(h + 1) * VDIM], axis=1, keepdims=True), 0.0)
        return [do, dl, dy * ob * _dsilu(z)], []
    return _rows(fn, "yb_bwd", 256, [(dyb, MLA_W, 0), (o, MLA_W, 0), (proj, MLA_W, O_ZB // MLA_W)],
                 outs=[(MLA_W, BF16), (128, F32, "T"), (MLA_W, BF16, (dproj, NP, O_ZB // MLA_W))])


def _attn_bwd(qc, kc, kct, vv, do, lse, dlt):
    def body(q_ref, k_ref, kt_ref, v_ref, do_ref, l_ref, d_ref, dq_ref, dk_ref, dv_ref, dqt_ref):
        h, i = pl.program_id(0), pl.program_id(1)

        @pl.when(i == 0)
        def _():
            dk_ref[...] = jnp.zeros_like(dk_ref)
            dv_ref[...] = jnp.zeros_like(dv_ref)

        q = q_ref[...]
        dob = do_ref[...]
        lse = l_ref[...]
        dl = d_ref[pl.ds(h, 1), :]
        dqt_ref[...] = jnp.zeros_like(dqt_ref)

        def rows_of(sb, c):
            return pl.ds(pl.multiple_of((sb * ATT_NB + c) * TC, TC), TC)

        def front(sb):
            return [(lax.dot_general(k_ref[rows_of(sb, c), :], q, _NT, preferred_element_type=F32),
                     lax.dot_general(v_ref[rows_of(sb, c), :], dob, _NT, preferred_element_type=F32))
                    for c in range(ATT_NB)]

        def block(sb, sd, masked):
            dqt = None
            for c, (s, dp) in enumerate(sd):
                rows = rows_of(sb, c)
                p = jnp.exp(s - lse)
                if masked:
                    p = jnp.where(_attn_allowed(i, sb * ATT_NB + c), p, 0.0)
                ds = (p * (dp - dl)).astype(BF16)
                dk_ref[rows, :] += jnp.dot(ds, q, preferred_element_type=F32)
                dv_ref[rows, :] += jnp.dot(p.astype(BF16), dob, preferred_element_type=F32)
                part = jnp.dot(kt_ref[:, rows], ds, preferred_element_type=F32)
                dqt = part if dqt is None else dqt + part
            dqt_ref[...] += dqt

        def step(sb, sd):
            nxt = front(sb + 1)
            block(sb, sd, False)
            return nxt

        nsb = ((i + 1) * TQ + ATT_KB - 1) // ATT_KB
        sd = lax.fori_loop(0, nsb - 1, step, front(0))
        block(nsb - 1, sd, True)
        dq_ref[...] = dqt_ref[...].T.astype(dq_ref.dtype)

    blk = lambda w: pl.BlockSpec((TQ, w), lambda h, i: (i, h))
    head = lambda w: pl.BlockSpec((S, w), lambda h, i: (0, h))
    return pl.pallas_call(
        body, name="attn_bwd", grid=(H, S // TQ),
        in_specs=[blk(HP), head(HP), pl.BlockSpec((HP, S), lambda h, i: (h, 0)), head(VDIM), blk(VDIM),
                  pl.BlockSpec((None, 1, TQ), lambda h, i: (h, 0, i)), pl.BlockSpec((8, TQ), lambda h, i: (0, i))],
        out_specs=[blk(HP), head(HP), head(VDIM)],
        out_shape=[jax.ShapeDtypeStruct((S, H * HP), BF16), jax.ShapeDtypeStruct((S, H * HP), F32),
                   jax.ShapeDtypeStruct((S, MLA_W), F32)],
        scratch_shapes=[pltpu.VMEM((HP, TQ), F32)],
        compiler_params=pltpu.CompilerParams(dimension_semantics=("parallel", "arbitrary"),
                                             vmem_limit_bytes=28 * MIB),
    )(qc, kc, kct, vv, do, lse, dlt)


def _rope_bwd(dqc, dkc, dvv, tabs):
    def fn(i, rv, hv, fv):
        dq, dk, dv, cc, sa, sb = rv
        qs, ks = [], []
        dkr = jnp.zeros((dq.shape[0], 128), F32)
        for h in range(H):
            qs += [dq[:, h * HP:h * HP + 128] * ATT_SCALE, _rot_t(dq[:, h * HP + 128:(h + 1) * HP], cc, sa, sb) * ATT_SCALE]
            ks.append(dk[:, h * HP:h * HP + 128])
            dkr = dkr + dk[:, h * HP + 128:(h + 1) * HP]
        return [jnp.concatenate(qs, axis=1), jnp.concatenate(ks + [dv], axis=1), _rot_t(dkr, cc, sa, sb)], []
    cc, sa, sb = tabs
    return _rows(fn, "rope_bwd", 256,
                 [(dqc, H * HP, 0), (dkc, H * HP, 0), (dvv, MLA_W, 0), (cc, 128, 0), (sa, 128, 0), (sb, 128, 0)],
                 outs=[(H * HP, BF16), (H * 256, BF16), (128, BF16)])


MLA_GROUP = 1536


def _mla_prep_bwd(dcqn, dckvn, dkr, proj, qg, kvg, dproj):
    def fn(i, rv, hv, fv):
        d1, d2, dk, cq, ckv = rv
        g1, g2 = fv
        dx1, dg1 = _rms_bwd(d1, cq, g1)
        dx2, dg2 = _rms_bwd(d2, ckv, g2)
        zeros = jnp.zeros((d1.shape[0], MLA_GROUP - KVR - 128 - QR), F32)
        return [jnp.concatenate([dx2, dk.astype(F32), dx1, zeros], axis=1)], [_colsum(dg1), _colsum(dg2)]
    return _rows(fn, "mla_prep_bwd", 256,
                 [(dcqn, QR, 0), (dckvn, KVR, 0), (dkr, 128, 0), (proj, QR, O_CQ // QR), (proj, KVR, O_CKV // KVR)],
                 fulls=[qg, kvg], outs=[(MLA_GROUP, BF16, (dproj, NP, O_CKV // MLA_GROUP))], accs=[(1, QR), (1, KVR)])


def _yc_bwd(dyc, hs, proj, dproj):
    def fn(i, rv, hv, fv):
        dy, hh, z = rv
        return [dy * _silu(z), dy * hh * _dsilu(z)], []
    return _rows(fn, "yc_bwd", LRU_TM, [(dyc, LRU_W, 0), (hs, LRU_W, 0), (proj, LRU_W, O_ZC // LRU_W)],
                 outs=[(LRU_W, F32), (LRU_W, BF16, (dproj, NP, O_ZC // LRU_W))])


def _scan_bwd(a, hs, dh):
    nblk = S // SCAN_T

    def body(a_ref, h_ref, dh_ref, da_ref, db_ref):
        row = lax.broadcasted_iota(jnp.int32, (SCAN_T, SCAN_CW), 0)

        def step(j, carry):
            gc, ac = carry
            blk = nblk - 1 - j
            t0 = pl.multiple_of(blk * SCAN_T, SCAN_T)
            av = a_ref[pl.ds(t0, SCAN_T), :]
            A = jnp.where(row < SCAN_T - 1, pltpu.roll(av, SCAN_T - 1, 0), ac)
            B = dh_ref[pl.ds(t0, SCAN_T), :]
            d = 1
            while d < SCAN_T:
                keep = row < SCAN_T - d
                A_s = jnp.where(keep, pltpu.roll(A, SCAN_T - d, 0), 1.0)
                B_s = jnp.where(keep, pltpu.roll(B, SCAN_T - d, 0), 0.0)
                B = A * B_s + B
                A = A * A_s
                d *= 2
            g = A * gc + B
            p0 = pl.multiple_of(jnp.maximum(t0 - 8, 0), 8)
            last = jnp.where(blk > 0, h_ref[pl.ds(p0, 8), :][7:8, :], 0.0)
            h_prev = jnp.where(row >= 1, pltpu.roll(h_ref[pl.ds(t0, SCAN_T), :], 1, 0), last)
            da_ref[pl.ds(t0, SCAN_T), :] = g * h_prev
            db_ref[pl.ds(t0, SCAN_T), :] = g
            return g[0:1, :], av[0:1, :]

        z = jnp.zeros((1, SCAN_CW), F32)
        lax.fori_loop(0, nblk, step, (z, z))

    spec = pl.BlockSpec((S, SCAN_CW), lambda j: (0, j))
    return pl.pallas_call(
        body, name="scan_bwd", grid=(LRU_W // SCAN_CW,), in_specs=[spec] * 3, out_specs=[spec] * 2,
        out_shape=[jax.ShapeDtypeStruct((S, LRU_W), F32)] * 2,
        compiler_params=pltpu.CompilerParams(dimension_semantics=("parallel",),
                                             vmem_limit_bytes=_vmem(5 * _nbytes((S, SCAN_CW), F32))),
    )(a, hs, dh)


def _lru_gates_bwd(da, db, gates, xc, ba, bx, lam):
    def fn(i, rv, hv, fv):
        dav, dbv, ga, gx, x = rv
        bav, bxv, lamv = fv
        r, ig, sp, a, e2, om, mult = _lru_terms(ga, gx, x, bav, bxv, lamv)
        dmult = dbv * ig * x
        dig = dbv * mult * x
        dxc1 = dbv * mult * ig
        dlog_a = dav * a + jnp.where(om > 0.0, dmult * (-e2 / mult), 0.0)
        dr = dlog_a * (-LRU_C * sp)
        dga = dr * r * (1.0 - r)
        dgx = dig * ig * (1.0 - ig)
        dlam = _colsum(dlog_a * (-LRU_C * r)) * (-_sig(-lamv))
        return [jnp.concatenate([dga, dgx], axis=1), dxc1], [_colsum(dga), _colsum(dgx), dlam]
    return _rows(fn, "lru_gates_bwd", LRU_TM,
                 [(da, LRU_W, 0), (db, LRU_W, 0), (gates, LRU_W, 0), (gates, LRU_W, 1), (xc, LRU_W, 0)],
                 fulls=[ba, bx, lam], outs=[(2 * LRU_W, BF16), (LRU_W, F32)], accs=[(1, LRU_W)] * 3)


def _conv_bwd(dxc1, dxc2, proj, w, dproj):
    cb = O_XC // LRU_W

    def fn(i, rv, hv, fv):
        d1, d2, xb = rv
        n1, n2, xprev = hv
        ww = fv[0]
        last = i == S // LRU_TM - 1
        dxc = d1 + d2
        nxt = jnp.where(last, 0.0, n1 + n2)
        xprev = jnp.where(i > 0, xprev, 0.0)
        row = lax.broadcasted_iota(jnp.int32, xb.shape, 0)
        dx = ww[3:4] * dxc
        dws = [None] * CONV_W
        dws[3] = _colsum(dxc * xb)
        for s in range(1, CONV_W):
            dx = dx + ww[3 - s:4 - s] * _shift_up(dxc, nxt, s, row)
            dws[3 - s] = _colsum(dxc * _shift_down(xb, xprev, s, row))
        return [dx], [jnp.concatenate(dws, axis=0), _colsum(dxc)]
    return _rows(fn, "conv_bwd", LRU_TM, [(dxc1, LRU_W, 0), (dxc2, LRU_W, 0), (proj, LRU_W, cb)],
                 halos=[(dxc1, LRU_W, 0, "next"), (dxc2, LRU_W, 0, "next"), (proj, LRU_W, cb, "prev")],
                 fulls=[w], outs=[(LRU_W, BF16, (dproj, NP, cb))], accs=[(CONV_W, LRU_W), (1, LRU_W)])


def _prenorm_bwd(dxn, dh, x, g):
    def fn(i, rv, hv, fv):
        dy, dhh, xb = rv
        dx, dg = _rms_bwd(dhh, xb, fv[0])
        return [dy + dx], [_colsum(dg)]
    return _rows(fn, "prenorm_bwd", 256, [(dxn, D, 0), (dh, D, 0), (x, D, 0)], fulls=[g], outs=[(D, F32)],
                 accs=[(1, D)])


def _layer_bwd(dxn, A, P, l, tabs, token=None, early=None):
    G, GB = {}, {}
    proj = A["proj"]

    def dw(key, a, b, name, **tiles):
        GB[key] = _mm(a, b, "tn", name, out_dtype=BF16, **tiles)

    do2, G["post_g"] = _post_bwd(dxn, A["o2"], P["post_g"], token)
    dm = _mm(do2, P["wout"], "nt", "out_proj_dx", out_dtype=BF16)
    dw("wout", A["merged"], do2, "out_proj_dw")
    dpa, dpb, dpc, dproj = _merge_bwd(dm, A["pa"], A["pb"], A["pc"], proj, None)
    dya = _mm(dpa, P["wpa"], "nt", "proj_a_dx", out_dtype=BF16)
    dw("wpa", A["ya"], dpa, "proj_a_dw")
    dyb = _mm(dpb, P["wpb"], "nt", "proj_b_dx", out_dtype=BF16)
    dw("wpb", A["yb"], dpb, "proj_b_dw")
    dyc = _mm(dpc, P["wpc"], "nt", "proj_c_dx", out_dtype=BF16)
    dw("wpc", A["yc"], dpc, "proj_c_dw")
    dproj, G["ws"], G["bst"], G["ln_g"], G["ln_b"] = _gmlp_bwd(dya, proj, P["ln_g"], P["ln_b"], P["ws"], P["bst"], dproj)
    do, dl, dproj = _yb_bwd(dyb, A["o"], proj, dproj)
    dqc, dkc, dvv = _attn_bwd(A["qc"], A["kc"], A["kct"], A["vv"], do, A["lse"], dl)
    dq, dkv, dkr = _rope_bwd(dqc, dkc, dvv, tabs)
    dcqn = _mm(dq, P["wuq"], "nn", "q_up_dx", out_dtype=BF16)
    dw("wuq", dq, A["cqn"], "q_up_dw")
    dckvn = _mm(dkv, P["wukv"], "nn", "kv_up_dx", out_dtype=BF16)
    dw("wukv", dkv, A["ckvn"], "kv_up_dw")
    dproj, G["qg"], G["kvg"] = _mla_prep_bwd(dcqn, dckvn, dkr, proj, P["qg"], P["kvg"], dproj)
    dhs, dproj = _yc_bwd(dyc, A["hs"], proj, dproj)
    da, db = _scan_bwd(A["a"], A["hs"], dhs)
    dgates, dxc1, G["ba"], G["bx"], G["lam"] = _lru_gates_bwd(da, db, A["gates"], A["xc"], P["ba"], P["bx"], P["lam"])
    dxc2 = _bd_dx(dgates, P["wsb"], l)
    G["wab"] = _bd_extract(_bd_dw(A["xcb"], dgates))
    dproj, G["conv_w"], G["conv_b"] = _conv_bwd(dxc1, dxc2, proj, P["conv_w"], dproj)
    tok = (None, None) if early is None else early(GB)
    dh = _mm(dproj, P["wp"], "nn", "in_proj_dx", tm=1024, tn=1024, token=tok[0])
    dw("wp", dproj, A["h"], "in_proj_dw", tm=1536, tn=1024, token=tok[1])
    dx, G["pre_g"] = _prenorm_bwd(dxn, dh, A["x"], P["pre_g"])
    return dx, G, GB


_ORIG_OFF = [0]
for _s in IN_SIZES:
    _ORIG_OFF.append(_ORIG_OFF[-1] + _s)
_PAD_OFF = {0: O_U, 1: O_V, 2: O_ZA, 3: O_CQ, 4: O_CKV, 5: O_KR, 6: O_ZB, 7: O_XC, 8: O_ZC, 9: O_GA, 10: O_GB, 11: O_GC}
SHARD_IN = N_IN // N_CHIPS


def _pieces_w_in(j):
    lo, hi = SHARD_IN * j, SHARD_IN * (j + 1)
    out = []
    for k in range(len(IN_SIZES)):
        a, b = max(lo, _ORIG_OFF[k]), min(hi, _ORIG_OFF[k + 1])
        if a < b:
            out.append((a - lo, _PAD_OFF[k] + a - _ORIG_OFF[k], b - a))
    return out


def _pieces_uq(j):
    return [(192 * hh, HP * (2 * j + hh), NOPE + ROPE) for hh in range(2)]


def _pieces_ukv(j):
    out = []
    for hh in range(2):
        h = 2 * j + hh
        out += [(256 * hh, NOPE * h, NOPE), (256 * hh + NOPE, H * NOPE + VDIM * h, VDIM)]
    return out


def _pieces_rows(r):
    return lambda j: [(0, r * j, r)]


LAYOUT = {
    "w_in": (SHARD_IN, NP, _pieces_w_in),
    "mla_w_uq": (2 * (NOPE + ROPE), H * HP, _pieces_uq),
    "mla_w_ukv": (2 * (NOPE + VDIM), 2 * H * 128, _pieces_ukv),
    "lru_conv_w": (1, N_CHIPS, _pieces_rows(1)),
    "w_proj_a": (GM_W // N_CHIPS, GM_W, _pieces_rows(GM_W // N_CHIPS)),
    "w_proj_b": (MLA_W // N_CHIPS, MLA_W, _pieces_rows(MLA_W // N_CHIPS)),
    "w_proj_c": (LRU_W // N_CHIPS, LRU_W, _pieces_rows(LRU_W // N_CHIPS)),
    "w_out": (D // N_CHIPS, D, _pieces_rows(D // N_CHIPS)),
}
TRANSPOSED = ("w_in", "mla_w_uq", "mla_w_ukv")


def _superblocks(w_a, w_x):
    w6 = jnp.stack([w_a, w_x], axis=1).reshape(DEPTH, 4, 8, LRU_BW, LRU_BW).astype(BF16)
    bands = [jnp.pad(w6[:, :, k], ((0, 0), (0, 0), (0, 0), (LRU_BW * k, SB - LRU_BW * (k + 1)))) for k in range(8)]
    return jnp.concatenate(bands, axis=2)


_HBM = pl.BlockSpec(memory_space=pltpu.HBM)


def _position():
    return lax.axis_index("x"), lax.axis_index("y"), lax.axis_index("c")


_REL = (2, 1, 3)


def _cut(r):
    return r if r < 32 else (r // 2 + 15) // 16 * 16


def _half_rows(r, c0):
    return _cut(r) if c0 == 0 else r - _cut(r)


def _half_pieces(lay_a, jsrc, c0):
    r = lay_a[0]
    lo, hi = (0, _cut(r)) if c0 == 0 else (_cut(r), r)
    out = []
    for s0, d0, nr in lay_a[2](jsrc):
        a, b = max(s0, lo), min(s0 + nr, hi)
        if a < b:
            out.append((a, d0 + a - s0, b - a))
    return out


def _gather_zeros(names, srcs):
    return [jnp.zeros((LAYOUT[nm][1],) + s.shape[1:], s.dtype) for nm, s in zip(names, srcs)]


def _weights_allgather(names, srcs, name, carry=()):
    n = len(srcs)
    lay = [LAYOUT[nm] for nm in names]
    zeros = _gather_zeros(names, srcs)
    m = len(carry)

    def body(*refs):
        ins, outs = refs[:n], refs[2 * n + m:3 * n + m]
        send, recv, lsem = refs[3 * n + 2 * m:]
        x, y, c = _position()
        j = 2 * x + y
        sib = (x, y, 1 - c)
        chips = [(1 - x, y), (x, 1 - y), (1 - x, 1 - y)]

        def flow(a, k, jsrc, c0, to, from_src):
            cps = []
            for s0, d0, nr in _half_pieces(lay[a], jsrc, c0):
                dst = outs[a].at[pl.ds(d0, nr)]
                src = ins[a].at[pl.ds(s0, nr)] if from_src else dst
                cps.append(pltpu.make_async_remote_copy(src_ref=src, dst_ref=dst, send_sem=send.at[7 * a + k],
                                                        recv_sem=recv.at[7 * a + k], device_id=to, device_id_type=MESH))
            return cps

        def sized(a, k, rows):
            ref = ins[a].at[pl.ds(0, rows)]
            return pltpu.make_async_remote_copy(src_ref=ref, dst_ref=ref, send_sem=send.at[7 * a + k],
                                                recv_sem=recv.at[7 * a + k], device_id=sib, device_id_type=MESH)

        for j0 in range(N_CHIPS):
            for c0 in range(2):
                @pl.when((j == j0) & (c == c0))
                def _(j0=j0, c0=c0):
                    mine = [_half_rows(lay[a][0], c0) for a in range(n)]
                    theirs = [_half_rows(lay[a][0], 1 - c0) for a in range(n)]
                    for a in range(n):
                        for s0, d0, nr in _half_pieces(lay[a], j0, c0):
                            pltpu.make_async_copy(ins[a].at[pl.ds(s0, nr)], outs[a].at[pl.ds(d0, nr)], lsem.at[a]).start()
                    for a in range(n):
                        for cp in flow(a, 0, j0, c0, sib, True):
                            cp.start()
                        for k, chip in enumerate(chips):
                            for cp in flow(a, 1 + k, j0, c0, (*chip, c), True):
                                cp.start()
                    for k in range(3):
                        for a in range(n):
                            if mine[a]:
                                sized(a, 1 + k, mine[a]).wait_recv()
                                for cp in flow(a, 4 + k, j0 ^ _REL[k], c0, sib, False):
                                    cp.start()
                    for a in range(n):
                        if theirs[a]:
                            sized(a, 0, theirs[a]).wait_recv()
                            for k in range(3):
                                sized(a, 4 + k, theirs[a]).wait_recv()
                    for a in range(n):
                        if mine[a]:
                            for k in range(7):
                                sized(a, k, mine[a]).wait_send()
                            ref = ins[a].at[pl.ds(0, mine[a])]
                            pltpu.make_async_copy(ref, ref, lsem.at[a]).wait()

    res = pl.pallas_call(
        body, name=name,
        out_shape=[jax.ShapeDtypeStruct(z.shape, z.dtype) for z in list(zeros) + list(carry)],
        in_specs=[_HBM] * (2 * n + m), out_specs=[_HBM] * (n + m),
        input_output_aliases={n + a: a for a in range(n + m)},
        scratch_shapes=[pltpu.SemaphoreType.DMA((7 * n,)), pltpu.SemaphoreType.DMA((7 * n,)),
                        pltpu.SemaphoreType.DMA((n,))],
    )(*srcs, *zeros, *carry)
    return res[:n], res[n:]


_SEM = pl.BlockSpec(memory_space=pltpu.SEMAPHORE)
_VMEM_TOKEN = pl.BlockSpec(memory_space=pltpu.VMEM)
_TOKEN = jax.ShapeDtypeStruct((8, 128), F32)
_EFFECT = pltpu.SideEffectType.DATAFLOW_SIDE_EFFECTING


def _gather_start(names, srcs, zeros, name, after=None):
    n = len(srcs)
    lay = [LAYOUT[nm] for nm in names]
    extra = [] if after is None else [after]

    def body(*refs):
        ins, lands = refs[:n], refs[n:2 * n]
        send, recv, lsem = refs[2 * n + len(extra):2 * n + len(extra) + 3]
        refs[-1][...] = jnp.zeros_like(refs[-1])
        x, y, c = _position()
        j = 2 * x + y
        chips = [(1 - x, y), (x, 1 - y), (1 - x, 1 - y)]
        for j0 in range(N_CHIPS):
            @pl.when(j == j0)
            def _(j0=j0):
                for a in range(n):
                    for s0, d0, nr in lay[a][2](j0):
                        src, dst = ins[a].at[pl.ds(s0, nr)], lands[a].at[pl.ds(d0, nr)]
                        pltpu.make_async_copy(src, dst, lsem.at[a]).start()
                        for k, chip in enumerate(chips):
                            pltpu.make_async_remote_copy(src_ref=src, dst_ref=dst, send_sem=send.at[3 * a + k],
                                                         recv_sem=recv.at[3 * a + k], device_id=(*chip, c),
                                                         device_id_type=MESH).start()

    sems = [pltpu.SemaphoreType.DMA((3 * n,)), pltpu.SemaphoreType.DMA((3 * n,)), pltpu.SemaphoreType.DMA((n,))]
    hbm = lambda a: pltpu.HBM(a.shape, a.dtype)
    res = pl.pallas_call(
        body, name=name,
        out_shape=sems + [hbm(s) for s in srcs] + [hbm(z) for z in zeros] + [_TOKEN],
        in_specs=[_HBM] * (2 * n) + [pl.BlockSpec(memory_space=pl.ANY)] * len(extra),
        out_specs=[_SEM] * 3 + [_HBM] * (2 * n) + [_VMEM_TOKEN],
        input_output_aliases={a: 3 + a for a in range(2 * n)},
        compiler_params=pltpu.CompilerParams(has_side_effects=_EFFECT),
    )(*[pltpu.with_memory_space_constraint(s, pltpu.HBM) for s in srcs],
      *[pltpu.with_memory_space_constraint(z, pltpu.HBM) for z in zeros], *extra)
    return res[:3], res[3:3 + n], res[3 + n:3 + 2 * n], res[-1]


def _gather_wait(names, sems, srcs, lands, after, name):
    n = len(srcs)
    lay = [LAYOUT[nm] for nm in names]

    def body(*refs):
        ins, zones = refs[:n], refs[n:2 * n]
        send, recv, lsem = refs[2 * n:2 * n + 3]
        x, y, c = _position()
        for a in range(n):
            whole = zones[a].at[pl.ds(0, lay[a][0])]
            for k in range(3):
                cp = pltpu.make_async_remote_copy(src_ref=ins[a], dst_ref=whole, send_sem=send.at[3 * a + k],
                                                  recv_sem=recv.at[3 * a + k], device_id=(x, y, 1 - c),
                                                  device_id_type=MESH)
                cp.wait_send()
                cp.wait_recv()
            pltpu.make_async_copy(ins[a], whole, lsem.at[a]).wait()

    hbm = lambda a: pltpu.HBM(a.shape, a.dtype)
    res = pl.pallas_call(
        body, name=name,
        out_shape=[hbm(s) for s in srcs] + [hbm(z) for z in lands],
        in_specs=[_HBM] * (2 * n) + [_SEM] * 3 + [pl.BlockSpec(memory_space=pl.ANY)], out_specs=[_HBM] * (2 * n),
        input_output_aliases={a: a for a in range(2 * n)},
        compiler_params=pltpu.CompilerParams(has_side_effects=_EFFECT),
    )(*srcs, *lands, *sems, after)
    return res[n:]


def _clip_pieces(lay_a, jsrc, c0):
    h = lay_a[1] // 2
    lo, hi = c0 * h, (c0 + 1) * h
    out = []
    for s0, d0, nr in lay_a[2](jsrc):
        a, b = max(d0, lo), min(d0 + nr, hi)
        if a < b:
            out.append((s0 + a - d0, a, b - a))
    return out


def _rows_of(pieces):
    return sum(nr for _, _, nr in pieces)


def _both_cores(body_for):
    x, y, c = _position()
    j = 2 * x + y
    for j0 in range(N_CHIPS):
        for c0 in range(2):
            @pl.when((j == j0) & (c == c0))
            def _(j0=j0, c0=c0):
                body_for(j0, c0)


STAGE_ROWS = 512


def _staged_copy(src, dst, buf, sem_in, sem_out, rows):
    ch = buf.shape[0]
    for r in range(0, rows, ch):
        nr = min(ch, rows - r)
        stage = buf.at[pl.ds(0, nr)]
        cin = pltpu.make_async_copy(src.at[pl.ds(r, nr)], stage, sem_in)
        cin.start()
        cin.wait()
        cout = pltpu.make_async_copy(stage, dst.at[pl.ds(r, nr)], sem_out)
        cout.start()
        cout.wait()


def _half_to_sibling(names, gl, name, after=None):
    n = len(gl)
    halves = [LAYOUT[nm][1] // 2 for nm in names]
    extra = [] if after is None else [after]

    def body(*refs):
        ins, outs = refs[:n], refs[n + len(extra):2 * n + len(extra)]
        send, recv = refs[2 * n + len(extra):]
        x, y, c = _position()

        def run(j0, c0):
            cps = [pltpu.make_async_remote_copy(src_ref=ins[a].at[pl.ds((1 - c0) * halves[a], halves[a])], dst_ref=outs[a],
                                                send_sem=send.at[a], recv_sem=recv.at[a], device_id=(x, y, 1 - c),
                                                device_id_type=MESH) for a in range(n)]
            for cp in cps:
                cp.start()
            for cp in cps:
                cp.wait()

        _both_cores(run)

    return pl.pallas_call(
        body, name=name,
        out_shape=[jax.ShapeDtypeStruct((halves[a],) + gl[a].shape[1:], gl[a].dtype) for a in range(n)],
        in_specs=[_HBM] * n + [pl.BlockSpec(memory_space=pl.ANY)] * len(extra), out_specs=[_HBM] * n,
        scratch_shapes=[pltpu.SemaphoreType.DMA((n,)), pltpu.SemaphoreType.DMA((n,))],
    )(*gl, *extra)


def _chip_scatter_half(names, parts, name):
    n = len(parts)
    lay = [LAYOUT[nm] for nm in names]
    zeros = [jnp.zeros((N_CHIPS, lay[a][0]) + parts[a].shape[1:], parts[a].dtype) for a in range(n)]

    def body(*refs):
        ins, outs = refs[:n], refs[2 * n:3 * n]
        send, recv = refs[3 * n:3 * n + 2]
        stage, sem_in, sem_out = refs[3 * n + 2:4 * n + 2], refs[4 * n + 2], refs[4 * n + 3]
        x, y, c = _position()
        chips = [(1 - x, y), (x, 1 - y), (1 - x, 1 - y)]

        def run(j0, c0):
            def sized(a, rows):
                return outs[a].at[0, pl.ds(0, rows)]

            for a in range(n):
                base = c0 * (lay[a][1] // 2)
                for k, chip in enumerate(chips):
                    for s0, d0, nr in _clip_pieces(lay[a], j0 ^ _REL[k], c0):
                        pltpu.make_async_remote_copy(
                            src_ref=ins[a].at[pl.ds(d0 - base, nr)], dst_ref=outs[a].at[j0, pl.ds(s0, nr)],
                            send_sem=send.at[3 * a + k], recv_sem=recv.at[3 * a + k],
                            device_id=(*chip, c), device_id_type=MESH).start()
            for a in range(n):
                base = c0 * (lay[a][1] // 2)
                for s0, d0, nr in _clip_pieces(lay[a], j0, c0):
                    _staged_copy(ins[a].at[pl.ds(d0 - base, nr)], outs[a].at[j0, pl.ds(s0, nr)], stage[a],
                                 sem_in.at[a], sem_out.at[a], nr)
            for a in range(n):
                got = _rows_of(_clip_pieces(lay[a], j0, c0))
                for k in range(3):
                    sent = _rows_of(_clip_pieces(lay[a], j0 ^ _REL[k], c0))
                    if sent:
                        pltpu.make_async_remote_copy(src_ref=sized(a, sent), dst_ref=sized(a, sent),
                                                     send_sem=send.at[3 * a + k], recv_sem=recv.at[3 * a + k],
                                                     device_id=(x, y, c), device_id_type=MESH).wait_send()
                    if got:
                        pltpu.make_async_remote_copy(src_ref=sized(a, got), dst_ref=sized(a, got),
                                                     send_sem=send.at[3 * a + k], recv_sem=recv.at[3 * a + k],
                                                     device_id=(x, y, c), device_id_type=MESH).wait_recv()

        _both_cores(run)

    return pl.pallas_call(
        body, name=name,
        out_shape=[jax.ShapeDtypeStruct(z.shape, z.dtype) for z in zeros],
        in_specs=[_HBM] * (2 * n), out_specs=[_HBM] * n, input_output_aliases={n + a: a for a in range(n)},
        scratch_shapes=[pltpu.SemaphoreType.DMA((3 * n,)), pltpu.SemaphoreType.DMA((3 * n,))]
        + [pltpu.VMEM((min(STAGE_ROWS, p.shape[0]),) + p.shape[1:], p.dtype) for p in parts]
        + [pltpu.SemaphoreType.DMA((n,)), pltpu.SemaphoreType.DMA((n,))],
    )(*parts, *zeros)


def _subset_exchange(names, bufs, l, name):
    n = len(bufs)
    lay = [LAYOUT[nm] for nm in names]

    def body(*refs):
        outs = refs[n:2 * n]
        send, recv = refs[2 * n:]
        x, y, c = _position()

        def run(j0, c0):
            for a in range(n):
                for s0, _, nr in _clip_pieces(lay[a], j0, c0):
                    rows = outs[a].at[l, pl.ds(s0, nr)]
                    pltpu.make_async_remote_copy(src_ref=rows, dst_ref=rows, send_sem=send.at[a], recv_sem=recv.at[a],
                                                 device_id=(x, y, 1 - c), device_id_type=MESH).start()
            for a in range(n):
                for c_half, wait_send in ((c0, True), (1 - c0, False)):
                    rows = _rows_of(_clip_pieces(lay[a], j0, c_half))
                    if rows:
                        ref = outs[a].at[l, pl.ds(0, rows)]
                        cp = pltpu.make_async_remote_copy(src_ref=ref, dst_ref=ref, send_sem=send.at[a], recv_sem=recv.at[a],
                                                          device_id=(x, y, 1 - c), device_id_type=MESH)
                        if wait_send:
                            cp.wait_send()
                        else:
                            cp.wait_recv()

        _both_cores(run)

    return pl.pallas_call(
        body, name=name,
        out_shape=[jax.ShapeDtypeStruct(b.shape, b.dtype) for b in bufs],
        in_specs=[_HBM] * n, out_specs=[_HBM] * n, input_output_aliases={a: a for a in range(n)},
        scratch_shapes=[pltpu.SemaphoreType.DMA((n,)), pltpu.SemaphoreType.DMA((n,))],
    )(*bufs)


def _scatter_start(names, gl, name):
    n = len(gl)
    lay = [LAYOUT[nm] for nm in names]
    zones = [lax.empty((N_CHIPS, lay[a][0]) + gl[a].shape[1:], gl[a].dtype) for a in range(n)]

    def body(*refs):
        ins, lands = refs[:n], refs[n:2 * n]
        send, recv, lsem = refs[2 * n:2 * n + 3]
        refs[-1][...] = jnp.zeros_like(refs[-1])
        x, y, c = _position()
        j = 2 * x + y
        chips = [(1 - x, y), (x, 1 - y), (1 - x, 1 - y)]
        for j0 in range(N_CHIPS):
            @pl.when(j == j0)
            def _(j0=j0):
                for a in range(n):
                    for s0, d0, nr in lay[a][2](j0):
                        pltpu.make_async_copy(ins[a].at[pl.ds(d0, nr)], lands[a].at[j0, pl.ds(s0, nr)], lsem.at[a]).start()
                    for k, chip in enumerate(chips):
                        for s0, d0, nr in lay[a][2](j0 ^ _REL[k]):
                            pltpu.make_async_remote_copy(
                                src_ref=ins[a].at[pl.ds(d0, nr)], dst_ref=lands[a].at[j0, pl.ds(s0, nr)],
                                send_sem=send.at[3 * a + k], recv_sem=recv.at[3 * a + k],
                                device_id=(*chip, c), device_id_type=MESH).start()

    sems = [pltpu.SemaphoreType.DMA((3 * n,)), pltpu.SemaphoreType.DMA((3 * n,)), pltpu.SemaphoreType.DMA((n,))]
    hbm = lambda a: pltpu.HBM(a.shape, a.dtype)
    res = pl.pallas_call(
        body, name=name,
        out_shape=sems + [hbm(g) for g in gl] + [hbm(z) for z in zones] + [_TOKEN],
        in_specs=[_HBM] * (2 * n), out_specs=[_SEM] * 3 + [_HBM] * (2 * n) + [_VMEM_TOKEN],
        input_output_aliases={a: 3 + a for a in range(2 * n)},
        compiler_params=pltpu.CompilerParams(has_side_effects=_EFFECT),
    )(*[pltpu.with_memory_space_constraint(g, pltpu.HBM) for g in gl],
      *[pltpu.with_memory_space_constraint(z, pltpu.HBM) for z in zones])
    return res[:3], res[3:3 + n], res[3 + n:3 + 2 * n], res[-1]


def _scatter_wait(names, sems, srcs, lands, after, name):
    n = len(srcs)
    lay = [LAYOUT[nm] for nm in names]

    def body(*refs):
        zones = refs[n:2 * n]
        send, recv, lsem = refs[2 * n:2 * n + 3]
        x, y, c = _position()
        for a in range(n):
            whole = zones[a].at[0, pl.ds(0, lay[a][0])]
            for k in range(3):
                cp = pltpu.make_async_remote_copy(src_ref=whole, dst_ref=whole, send_sem=send.at[3 * a + k],
                                                  recv_sem=recv.at[3 * a + k], device_id=(x, y, 1 - c),
                                                  device_id_type=MESH)
                cp.wait_send()
                cp.wait_recv()
            pltpu.make_async_copy(whole, whole, lsem.at[a]).wait()

    hbm = lambda a: pltpu.HBM(a.shape, a.dtype)
    res = pl.pallas_call(
        body, name=name,
        out_shape=[hbm(s) for s in srcs] + [hbm(z) for z in lands],
        in_specs=[_HBM] * (2 * n) + [_SEM] * 3 + [pl.BlockSpec(memory_space=pl.ANY)], out_specs=[_HBM] * (2 * n),
        input_output_aliases={a: a for a in range(2 * n)},
        compiler_params=pltpu.CompilerParams(has_side_effects=_EFFECT),
    )(*srcs, *lands, *sems, after)
    return res[n:]


def _peer(x, y, c, k):
    return (1 - x if k & 4 else x, 1 - y if k & 2 else y, 1 - c if k & 1 else c)


def _bcast_start(arrs, name, after=None):
    n = len(arrs)
    zones = [lax.empty((8,) + a.shape, a.dtype) for a in arrs]
    extra = [] if after is None else [after]

    def body(*refs):
        ins, lands = refs[:n], refs[n:2 * n]
        send, recv, lsem = refs[2 * n + len(extra):2 * n + len(extra) + 3]
        refs[-1][...] = jnp.zeros_like(refs[-1])
        x, y, c = _position()
        for a in range(n):
            dst = lands[a].at[4 * x + 2 * y + c]
            pltpu.make_async_copy(ins[a], dst, lsem.at[a]).start()
            for k in range(1, 8):
                pltpu.make_async_remote_copy(src_ref=ins[a], dst_ref=dst, send_sem=send.at[7 * a + k - 1],
                                             recv_sem=recv.at[7 * a + k - 1], device_id=_peer(x, y, c, k),
                                             device_id_type=MESH).start()

    sems = [pltpu.SemaphoreType.DMA((7 * n,)), pltpu.SemaphoreType.DMA((7 * n,)), pltpu.SemaphoreType.DMA((n,))]
    hbm = lambda a: pltpu.HBM(a.shape, a.dtype)
    res = pl.pallas_call(
        body, name=name,
        out_shape=sems + [hbm(a) for a in arrs] + [hbm(z) for z in zones] + [_TOKEN],
        in_specs=[_HBM] * (2 * n) + [pl.BlockSpec(memory_space=pl.ANY)] * len(extra),
        out_specs=[_SEM] * 3 + [_HBM] * (2 * n) + [_VMEM_TOKEN],
        input_output_aliases={a: 3 + a for a in range(2 * n)},
        compiler_params=pltpu.CompilerParams(has_side_effects=_EFFECT),
    )(*[pltpu.with_memory_space_constraint(a, pltpu.HBM) for a in arrs],
      *[pltpu.with_memory_space_constraint(z, pltpu.HBM) for z in zones], *extra)
    return res[:3], res[3:3 + n], res[3 + n:3 + 2 * n], res[-1]


def _bcast_wait(sems, srcs, lands, after, name):
    n = len(srcs)

    def body(*refs):
        ins, zones = refs[:n], refs[n:2 * n]
        send, recv, lsem = refs[2 * n:2 * n + 3]
        x, y, c = _position()
        for a in range(n):
            for k in range(1, 8):
                cp = pltpu.make_async_remote_copy(src_ref=ins[a], dst_ref=zones[a].at[0], send_sem=send.at[7 * a + k - 1],
                                                  recv_sem=recv.at[7 * a + k - 1], device_id=_peer(x, y, c, k),
                                                  device_id_type=MESH)
                cp.wait_send()
                cp.wait_recv()
            pltpu.make_async_copy(ins[a], zones[a].at[0], lsem.at[a]).wait()

    hbm = lambda a: pltpu.HBM(a.shape, a.dtype)
    res = pl.pallas_call(
        body, name=name,
        out_shape=[hbm(s) for s in srcs] + [hbm(z) for z in lands],
        in_specs=[_HBM] * (2 * n) + [_SEM] * 3 + [pl.BlockSpec(memory_space=pl.ANY)], out_specs=[_HBM] * (2 * n),
        input_output_aliases={a: a for a in range(2 * n)},
        compiler_params=pltpu.CompilerParams(has_side_effects=_EFFECT),
    )(*srcs, *lands, *sems, after)
    return res[n:]


def _swap_start(arrs, name):
    n = len(arrs)
    zones = [lax.empty(a.shape, a.dtype) for a in arrs]

    def body(*refs):
        ins, lands = refs[:n], refs[n:2 * n]
        send, recv = refs[2 * n:2 * n + 2]
        refs[-1][...] = jnp.zeros_like(refs[-1])
        x, y, c = _position()
        for a in range(n):
            pltpu.make_async_remote_copy(src_ref=ins[a], dst_ref=lands[a], send_sem=send.at[a], recv_sem=recv.at[a],
                                         device_id=(x, y, 1 - c), device_id_type=MESH).start()

    sems = [pltpu.SemaphoreType.DMA((n,)), pltpu.SemaphoreType.DMA((n,))]
    hbm = lambda a: pltpu.HBM(a.shape, a.dtype)
    res = pl.pallas_call(
        body, name=name,
        out_shape=sems + [hbm(a) for a in arrs] + [hbm(z) for z in zones] + [_TOKEN],
        in_specs=[_HBM] * (2 * n), out_specs=[_SEM] * 2 + [_HBM] * (2 * n) + [_VMEM_TOKEN],
        input_output_aliases={a: 2 + a for a in range(2 * n)},
        compiler_params=pltpu.CompilerParams(has_side_effects=_EFFECT),
    )(*[pltpu.with_memory_space_constraint(a, pltpu.HBM) for a in arrs],
      *[pltpu.with_memory_space_constraint(z, pltpu.HBM) for z in zones])
    return res[:2], res[2:2 + n], res[2 + n:2 + 2 * n], res[-1]


def _swap_wait(sems, srcs, lands, after, name):
    n = len(srcs)

    def body(*refs):
        ins, zones = refs[:n], refs[n:2 * n]
        send, recv = refs[2 * n:2 * n + 2]
        x, y, c = _position()
        for a in range(n):
            cp = pltpu.make_async_remote_copy(src_ref=ins[a], dst_ref=zones[a], send_sem=send.at[a], recv_sem=recv.at[a],
                                              device_id=(x, y, 1 - c), device_id_type=MESH)
            cp.wait_send()
            cp.wait_recv()

    hbm = lambda a: pltpu.HBM(a.shape, a.dtype)
    res = pl.pallas_call(
        body, name=name,
        out_shape=[hbm(s) for s in srcs] + [hbm(z) for z in lands],
        in_specs=[_HBM] * (2 * n) + [_SEM] * 2 + [pl.BlockSpec(memory_space=pl.ANY)], out_specs=[_HBM] * (2 * n),
        input_output_aliases={a: a for a in range(2 * n)},
        compiler_params=pltpu.CompilerParams(has_side_effects=_EFFECT),
    )(*srcs, *lands, *sems, after)
    return res[:n], res[n:]


def _row_tile(r):
    for t in (256, 128, 64, 32, 16, 8):
        if r % t == 0 and r > t:
            return t
    return r


def _pair_add_half(g, rb, c_arr, name):
    hrows, rest = rb.shape[0], rb.shape[1:]
    tr = _row_tile(hrows)
    nb = hrows // tr
    z = (0,) * len(rest)

    def body(c_ref, g_ref, r_ref, o_ref):
        o_ref[...] = (g_ref[...].astype(F32) + r_ref[...].astype(F32)).astype(o_ref.dtype)

    return pl.pallas_call(
        body, name=name,
        grid_spec=pltpu.PrefetchScalarGridSpec(
            num_scalar_prefetch=1, grid=(nb,),
            in_specs=[pl.BlockSpec((tr,) + rest, lambda i, c_ref: (c_ref[0] * nb + i,) + z),
                      pl.BlockSpec((tr,) + rest, lambda i, c_ref: (i,) + z)],
            out_specs=pl.BlockSpec((tr,) + rest, lambda i, c_ref: (i,) + z)),
        out_shape=jax.ShapeDtypeStruct((hrows,) + rest, BF16),
        compiler_params=pltpu.CompilerParams(dimension_semantics=("parallel",), vmem_limit_bytes=VMEM_LIMIT),
    )(c_arr, g, rb)


def _sum_slabs(slabs, l, buf, name):
    m = len(slabs)
    n, R, rest = slabs[0].shape[0], slabs[0].shape[1], slabs[0].shape[2:]
    tr = _row_tile(R)
    z = (0,) * len(rest)

    def body(*refs):
        total = None
        for r_ref in refs[:m]:
            acc = r_ref[0].astype(F32)
            for k in range(1, n):
                acc = acc + r_ref[k].astype(F32)
            total = acc if total is None else total + acc
        refs[-1][...] = total

    if R // tr > 64 and len(rest) == 1 and rest[0] % 256 == 0:
        grid = (rest[0] // 256,)
        in_spec = pl.BlockSpec((n, R, 256), lambda i: (0, 0, i))
        out_spec = pl.BlockSpec((None, R, 256), lambda i: (l, 0, i))
    else:
        grid = (R // tr,)
        in_spec = pl.BlockSpec((n, tr) + rest, lambda i: (0, i) + z)
        out_spec = pl.BlockSpec((None, tr) + rest, lambda i: (l, i) + z)
    in_specs, args, aliases = [in_spec] * m, list(slabs), {}
    if buf is not None:
        in_specs.append(pl.BlockSpec(memory_space=pl.ANY))
        args.append(buf)
        aliases = {m: 0}
    return pl.pallas_call(
        body, name=name, grid=grid, in_specs=in_specs, out_specs=out_spec,
        out_shape=jax.ShapeDtypeStruct((DEPTH, R) + rest, F32), input_output_aliases=aliases,
        compiler_params=pltpu.CompilerParams(
            dimension_semantics=("parallel",),
            vmem_limit_bytes=_vmem(m * _nbytes(in_spec.block_shape, slabs[0].dtype) + _nbytes(out_spec.block_shape, F32),
                                   2 * _nbytes(out_spec.block_shape, F32))),
    )(*args)


def _adam_math(w, g, m, v):
    mn = ADAM_B1 * m + (1.0 - ADAM_B1) * g
    vn = ADAM_B2 * v + (1.0 - ADAM_B2) * (g * g)
    m_hat = mn / (1.0 - ADAM_B1 ** ADAM_STEP)
    v_hat = vn / (1.0 - ADAM_B2 ** ADAM_STEP)
    return -ADAM_LR * (m_hat / (jnp.sqrt(v_hat) + ADAM_EPS) + ADAM_WD * w), mn, vn


def _adamw(w, g, m, v, name, token=None):
    L, R, C = w.shape
    tr = _row_tile(R)
    extra = [] if token is None else [token]

    def body(w_ref, g_ref, m_ref, v_ref, *rest):
        d_ref, mo_ref, vo_ref = rest[-3:]
        d_ref[...], mo_ref[...], vo_ref[...] = _adam_math(w_ref[...], g_ref[...], m_ref[...], v_ref[...])

    if R // tr > 64 and C % 128 == 0:
        spec, grid = pl.BlockSpec((None, R, 128), lambda l, i: (l, 0, i)), (L, C // 128)
    else:
        spec, grid = pl.BlockSpec((None, tr, C), lambda l, i: (l, i, 0)), (L, R // tr)
    return pl.pallas_call(
        body, name=name, grid=grid, in_specs=[spec] * 4 + [pl.BlockSpec(memory_space=pl.ANY)] * len(extra),
        out_specs=[spec] * 3, out_shape=[jax.ShapeDtypeStruct((L, R, C), F32)] * 3,
        compiler_params=pltpu.CompilerParams(dimension_semantics=("parallel", "parallel"),
                                             vmem_limit_bytes=_vmem(7 * _nbytes(spec.block_shape, F32))),
    )(w, g, m, v, *extra)


_VMEM_WHOLE = pl.BlockSpec(memory_space=pltpu.VMEM)


def _matrix_update(gath, w, m, v, name):
    K = w.shape[1]

    def body(g0_ref, g1_ref, w_ref, m_ref, v_ref, go_ref, d_ref, mo_ref, vo_ref):
        for l, gr in enumerate((g0_ref, g1_ref)):
            for k in range(K):
                g = gr[0, k].astype(F32)
                for dev in range(1, 8):
                    g = g + gr[dev, k].astype(F32)
                go_ref[l, k] = g
                d_ref[l, k], mo_ref[l, k], vo_ref[l, k] = _adam_math(w_ref[l, k], g, m_ref[l, k], v_ref[l, k])

    return pl.pallas_call(
        body, name=name, in_specs=[_VMEM_WHOLE] * 5, out_specs=[_VMEM_WHOLE] * 4,
        out_shape=[jax.ShapeDtypeStruct(w.shape, F32)] * 4,
        compiler_params=pltpu.CompilerParams(vmem_limit_bytes=32 * MIB),
    )(gath[0], gath[1], w, m, v)


VECS = (("pre_norm_g", D), ("post_norm_g", D), ("gm_ln_g", GM_W), ("gm_ln_b", GM_W), ("mla_q_norm_g", QR),
        ("mla_kv_norm_g", KVR), ("lru_conv_b", LRU_W), ("lru_b_a", LRU_W), ("lru_b_x", LRU_W), ("lru_lambda", LRU_W))
VEC_KEY = {"pre_norm_g": "pre_g", "post_norm_g": "post_g", "gm_ln_g": "ln_g", "gm_ln_b": "ln_b", "mla_q_norm_g": "qg",
           "mla_kv_norm_g": "kvg", "lru_conv_b": "conv_b", "lru_b_a": "ba", "lru_b_x": "bx", "lru_lambda": "lam"}
VEC_ROWS, VEC_W, VEC_ROW0, LOSS_ROW = 16, LRU_W, GM_G, 14


def _pack_rows(LG, loss_part):
    per = len(VECS) + 1
    ins = []
    for G in LG:
        ins += [G[VEC_KEY[n]] for n, _ in VECS] + [G["bst"]]
    ins.append(loss_part)

    def body(*refs):
        o_ref = refs[-1]
        o_ref[...] = jnp.zeros_like(o_ref)
        for l in range(DEPTH):
            base = VEC_ROWS * l
            o_ref[pl.ds(base, 8), pl.ds(0, GM_B)] = refs[per * l + len(VECS)][...].T[:8, :]
            for t, (_, width) in enumerate(VECS):
                o_ref[pl.ds(base + VEC_ROW0 + t, 1), pl.ds(0, width)] = refs[per * l + t][...]
        o_ref[pl.ds(LOSS_ROW, 1), pl.ds(0, 128)] = jnp.broadcast_to(refs[-2][...], (1, 128))

    return pl.pallas_call(
        body, name="pack_rows", in_specs=[_VMEM_WHOLE] * len(ins), out_specs=_VMEM_WHOLE,
        out_shape=jax.ShapeDtypeStruct((DEPTH * VEC_ROWS, VEC_W), F32),
    )(*ins)


def _vector_update(gath, W, M, V):
    names = [n for n, _ in VECS] + ["gm_bs"]
    nw = len(names)

    def body(*refs):
        g_ref = refs[0]
        wr, mr, vr = refs[1:1 + nw], refs[1 + nw:1 + 2 * nw], refs[1 + 2 * nw:1 + 3 * nw]
        outs = refs[1 + 3 * nw:]
        s = g_ref[0]
        for dev in range(1, 8):
            s = s + g_ref[dev]
        for t, (_, width) in enumerate(VECS):
            for l in range(DEPTH):
                r = VEC_ROWS * l + VEC_ROW0 + t
                g = s[r:r + 1, :width]
                row = (pl.ds(l, 1), slice(None))
                res = (g,) + _adam_math(wr[t][row], g, mr[t][row], vr[t][row])
                for q in range(4):
                    outs[4 * t + q][row] = res[q]
        t = len(VECS)
        for l in range(DEPTH):
            for k in range(GM_G):
                g = s[VEC_ROWS * l + k:VEC_ROWS * l + k + 1, :GM_B]
                row = (l, pl.ds(k, 1), slice(None))
                res = (g,) + _adam_math(wr[t][row], g, mr[t][row], vr[t][row])
                for q in range(4):
                    outs[4 * t + q][row] = res[q]
        outs[4 * nw][...] = s[LOSS_ROW:LOSS_ROW + 1, :128]

    ws = [W[n] for n in names]
    out_shape = []
    for w in ws:
        out_shape += [jax.ShapeDtypeStruct(w.shape, F32)] * 4
    out_shape.append(jax.ShapeDtypeStruct((1, 128), F32))
    res = pl.pallas_call(
        body, name="vector_update", in_specs=[_VMEM_WHOLE] * (1 + 3 * nw), out_specs=[_VMEM_WHOLE] * (4 * nw + 1),
        out_shape=out_shape, compiler_params=pltpu.CompilerParams(vmem_limit_bytes=VMEM_LIMIT),
    )(gath, *ws, *[M[n] for n in names], *[V[n] for n in names])
    return {n: tuple(res[4 * t:4 * t + 4]) for t, n in enumerate(names)}, res[4 * nw]


SHARDED = ("w_in", "mla_w_uq", "mla_w_ukv", "lru_conv_w", "w_proj_a", "w_proj_b", "w_proj_c", "w_out")
FIRST = ("w_in", "lru_conv_w")
LATER = tuple(n for n in SHARDED if n not in FIRST)
COL_SHARDED = ("w_in", "mla_w_uq", "mla_w_ukv", "lru_conv_w")
SMALL = ("pre_norm_g", "gm_ln_g", "gm_ln_b", "gm_ws", "gm_bs", "mla_q_norm_g", "mla_kv_norm_g", "lru_conv_b",
         "lru_w_a", "lru_b_a", "lru_w_x", "lru_b_x", "lru_lambda", "post_norm_g")
WEIGHTS = ("pre_norm_g", "w_in", "gm_ln_g", "gm_ln_b", "gm_ws", "gm_bs", "mla_q_norm_g", "mla_w_uq",
           "mla_kv_norm_g", "mla_w_ukv", "lru_conv_w", "lru_conv_b", "lru_w_a", "lru_b_a", "lru_w_x", "lru_b_x",
           "lru_lambda", "w_proj_a", "w_proj_b", "w_proj_c", "w_out", "post_norm_g")


GB_KEY = {"w_in": "wp", "mla_w_uq": "wuq", "mla_w_ukv": "wukv", "w_proj_a": "wpa", "w_proj_b": "wpb",
          "w_proj_c": "wpc", "w_out": "wout"}


def _prepare(l, gathered, small, wsb):
    P = {GB_KEY[n]: gathered[n] for n in GB_KEY if n in gathered}
    P["conv_w"] = gathered["lru_conv_w"].transpose(1, 0, 2).reshape(CONV_W, LRU_W)
    P["wsb"] = wsb
    row = lambda n: small[n][l][None, :]
    P["pre_g"], P["post_g"] = row("pre_norm_g"), row("post_norm_g")
    P["ln_g"], P["ln_b"] = row("gm_ln_g"), row("gm_ln_b")
    P["ws"] = small["gm_ws"][l]
    P["bst"] = jnp.pad(small["gm_bs"][l].T, ((0, 0), (0, 128 - GM_G)))
    P["qg"], P["kvg"] = row("mla_q_norm_g"), row("mla_kv_norm_g")
    P["conv_b"], P["ba"], P["bx"], P["lam"] = row("lru_conv_b"), row("lru_b_a"), row("lru_b_x"), row("lru_lambda")
    return P


def kernel(x, pre_norm_g, w_in, gm_ln_g, gm_ln_b, gm_ws, gm_bs, mla_q_norm_g, mla_w_uq, mla_kv_norm_g, mla_w_ukv, lru_conv_w, lru_conv_b, lru_w_a, lru_b_a, lru_w_x, lru_b_x, lru_lambda, w_proj_a, w_proj_b, w_proj_c, w_out, post_norm_g, loss_target, m_pre_norm_g, m_w_in, m_gm_ln_g, m_gm_ln_b, m_gm_ws, m_gm_bs, m_mla_q_norm_g, m_mla_w_uq, m_mla_kv_norm_g, m_mla_w_ukv, m_lru_conv_w, m_lru_conv_b, m_lru_w_a, m_lru_b_a, m_lru_w_x, m_lru_b_x, m_lru_lambda, m_w_proj_a, m_w_proj_b, m_w_proj_c, m_w_out, m_post_norm_g, v_pre_norm_g, v_w_in, v_gm_ln_g, v_gm_ln_b, v_gm_ws, v_gm_bs, v_mla_q_norm_g, v_mla_w_uq, v_mla_kv_norm_g, v_mla_w_ukv, v_lru_conv_w, v_lru_conv_b, v_lru_w_a, v_lru_b_a, v_lru_w_x, v_lru_b_x, v_lru_lambda, v_w_proj_a, v_w_proj_b, v_w_proj_c, v_w_out, v_post_norm_g):
    args = dict(locals())
    W = {n: args[n] for n in WEIGHTS}
    M = {n: args["m_" + n] for n in WEIGHTS}
    V = {n: args["v_" + n] for n in WEIGHTS}
    c = lax.axis_index("c")

    def shards(l, names):
        out = []
        for n in names:
            blk = W[n][l].T if n in TRANSPOSED else W[n][l]
            out.append(blk[None] if n == "lru_conv_w" else blk.astype(BF16))
        return out

    small = {n: W[n] for n in SMALL}
    wsb = _superblocks(W["lru_w_a"], W["lru_w_x"])
    tabs = _rope_tables()
    s0a, s0b, s1a, s1b = shards(0, FIRST), shards(0, LATER), shards(1, FIRST), shards(1, LATER)
    g0, zones = _weights_allgather(FIRST, s0a, "weights_allgather_l0", carry=_gather_zeros(LATER, s0b)
                                   + _gather_zeros(FIRST, s1a) + _gather_zeros(LATER, s1b))
    nl, nf = len(LATER), len(FIRST)
    w0b = _gather_start(LATER, s0b, zones[:nl], "weights_gather_start_l0")
    w1a = _gather_start(FIRST, s1a, zones[nl:nl + nf], "weights_gather_start_l1_first", after=w0b[3])
    w1b = _gather_start(LATER, s1b, zones[nl + nf:], "weights_gather_start_l1_later", after=w1a[3])

    def late(started, name):
        def wait(after):
            got = _gather_wait(LATER, *started[:3], after, name)
            return {GB_KEY[n]: g for n, g in zip(LATER, got)}
        return wait

    P = [_prepare(0, dict(zip(FIRST, g0)), small, wsb), None]
    h0 = x[0]
    h1, A0 = _layer_fwd(h0, P[0], 0, tabs, w1b[3], late(w0b, "weights_gather_wait_l0"))
    g1 = _gather_wait(FIRST, *w1a[:3], h1, "weights_gather_wait_l1_first")
    P[1] = _prepare(1, dict(zip(FIRST, g1)), small, wsb)
    h2, A1 = _layer_fwd(h1, P[1], 1, tabs, None, late(w1b, "weights_gather_wait_l1_later"))
    dy, loss_part = _loss_fwd(h2, loss_target[0])

    def large_grads(G, GB, names):
        conv = G["conv_w"].reshape(CONV_W, N_CHIPS, LRU_W // N_CHIPS).transpose(1, 0, 2)
        return [conv if n == "lru_conv_w" else GB[GB_KEY[n]] for n in names]

    started = {}

    def early1(GB):
        started["sc1b"] = _scatter_start(LATER, [GB[GB_KEY[n]] for n in LATER], "grads_scatter_start_l1_later")
        return started["sc1b"][3], None

    d1, G1, GB1 = _layer_bwd(dy, A1, P[1], 1, tabs, None, early1)
    sc1a = _scatter_start(FIRST, large_grads(G1, GB1, FIRST), "grads_scatter_start_l1_first")

    def early0(GB):
        got_b = _scatter_wait(LATER, *started["sc1b"][:3], GB["wukv"], "grads_scatter_wait_l1_later")
        got_a = _scatter_wait(FIRST, *sc1a[:3], got_b[0], "grads_scatter_wait_l1_first")
        got = dict(zip(LATER + FIRST, list(got_b) + list(got_a)))
        started["swap1"] = _swap_start([got[n] for n in SHARDED], "partials_swap_start_l1")
        started["sc0"] = _scatter_start(LATER, [GB[GB_KEY[n]] for n in LATER], "grads_scatter_start_l0")
        return started["sc0"][3], started["swap1"][3]

    d0, G0, GB0 = _layer_bwd(d1, A0, P[0], 0, tabs, sc1a[3], early0)
    LG = (G0, G1)
    mine0 = _scatter_wait(LATER, *started["sc0"][:3], d0, "grads_scatter_wait_l0")
    swap0 = _swap_start(mine0, "partials_swap_start_l0")
    g0f = large_grads(G0, GB0, FIRST)
    c_arr = jnp.reshape(c, (1,)).astype(jnp.int32)
    from_sib = _half_to_sibling(FIRST, g0f, "grads_half_to_sibling_l0", after=swap0[3])
    pair = [_pair_add_half(g, rb, c_arr, "pair_add_" + n) for n, g, rb in zip(FIRST, g0f, from_sib)]
    slabs = _chip_scatter_half(FIRST, pair, "grads_chip_scatter_l0")
    mats = []
    for g in LG:
        mats += [g["ws"].astype(BF16), g["wab"][0, :, :, :LRU_BW], g["wab"][1, :, :, :LRU_BW]]
    bc = _bcast_start([_pack_rows(LG, loss_part)] + mats, "small_grads_start", after=slabs[0])
    mine1, theirs1 = _swap_wait(*started["swap1"][:3], bc[3], "partials_swap_wait_l1")
    both = dict(zip(SHARDED, [_sum_slabs([a, b], 1, None, "sum_partials_l1_" + n)
                              for n, a, b in zip(SHARDED, mine1, theirs1)]))
    for n, s in zip(FIRST, slabs):
        both[n] = _sum_slabs([s], 0, both[n], "sum_slabs_l0_" + n)
    done = _subset_exchange(FIRST, [both[n] for n in FIRST], 0, "reduced_rows_to_sibling_l0")
    both.update(zip(FIRST, done))
    mine0, theirs0 = _swap_wait(*swap0[:3], done[0], "partials_swap_wait_l0")
    for n, a, b in zip(LATER, mine0, theirs0):
        both[n] = _sum_slabs([a, b], 0, both[n], "sum_partials_l0_" + n)
    both = [both[n] for n in SHARDED]
    grads = {}
    for n, b in zip(SHARDED, both):
        if n in TRANSPOSED and n != "w_in":
            b = jnp.swapaxes(b, 1, 2)
        grads[n] = b if n == "w_in" else b.reshape(W[n].shape)

    upd, last = {}, None
    for n in SHARDED:
        token = bc[3] if n == SHARDED[0] else None
        if n == "w_in":
            tr = lambda a: jnp.swapaxes(a, 1, 2)
            res = _adamw(tr(W[n]), grads[n], tr(M[n]), tr(V[n]), "adamw_" + n, token)
            upd[n] = tuple(tr(a) for a in (grads[n],) + tuple(res))
        else:
            res = _adamw(W[n], grads[n], M[n], V[n], "adamw_" + n, token)
            upd[n] = (grads[n],) + tuple(res)
        last = res[0]

    gath = _bcast_wait(*bc[:3], last, "small_grads_wait")
    vec_upd, loss_row = _vector_update(gath[0], W, M, V)
    upd.update(vec_upd)
    loss = loss_row[0, 0]
    for k, n in enumerate(("gm_ws", "lru_w_a", "lru_w_x")):
        upd[n] = _matrix_update((gath[1 + k], gath[4 + k]), W[n], M[n], V[n], "update_" + n)

    return (loss, d0[None], *[upd[n][0] for n in WEIGHTS], *[upd[n][1] for n in WEIGHTS],
            *[upd[n][2] for n in WEIGHTS], *[upd[n][3] for n in WEIGHTS])
```

```python
import functools
import math

import jax
import jax.numpy as jnp
from jax import lax
from jax.experimental import pallas as pl
from jax.experimental.pallas import tpu as pltpu

F32, BF16 = jnp.float32, jnp.bfloat16
MESH = pl.DeviceIdType.MESH

S, D, DEPTH = 2048, 1024, 2
CHUNK, EPS = 64, 1e-6
GM_W, GM_G, GM_B = 1024, 4, 128
H, NOPE, ROPE, VDIM = 8, 128, 64, 128
QR, KVR = 384, 256
MLA_W = H * VDIM
LRU_W, LRU_NB, LRU_BW, LRU_C, CONV_W = 1280, 16, 80, 8.0, 4
ROPE_THETA = 10000.0
IN_SIZES = (GM_W, GM_W, GM_W, QR, KVR, ROPE, MLA_W, LRU_W, LRU_W, D, D, D)
N_IN = sum(IN_SIZES)
N_CHIPS = 4
ADAM_LR, ADAM_B1, ADAM_B2, ADAM_EPS, ADAM_WD, ADAM_STEP = 0.001, 0.9, 0.999, 1e-08, 0.01, 10

HP = 256
O_U, O_V, O_ZA, O_GA, O_GB, O_GC = 0, 1024, 2048, 3072, 4096, 5120
O_CKV, O_KR, O_CQ, O_XC, O_ZC, O_ZB = 6144, 6400, 6528, 7680, 8960, 10240
NP = 11264
MIB = 1024 * 1024
VMEM_LIMIT = 16 * MIB


def _vmem(block_bytes, temp_bytes=0):
    return int(min(max(2 * block_bytes + temp_bytes + 4 * MIB, VMEM_LIMIT), 56 * MIB))


def _nbytes(shape, dtype):
    return math.prod(d for d in shape if d is not None) * jnp.dtype(dtype).itemsize


def _tile(dim, target):
    if dim <= target:
        return dim
    t = (target // 128) * 128
    while dim % t:
        t -= 128
    return t


def _sig(x):
    return jax.nn.sigmoid(x)


def _silu(x):
    return x * _sig(x)


def _dsilu(x):
    s = _sig(x)
    return s * (1.0 + x * (1.0 - s))


def _mm(a, b, mode, name, out_dtype=F32, tm=1024, tn=1024, tk=1024, b_lead=None, out_lead=None, token=None):
    b2 = b.shape[1:] if b_lead is not None else b.shape
    if mode == "nn":
        (M, K), (K2, N) = a.shape, b2
    elif mode == "nt":
        (M, K), (N, K2) = a.shape, b2
    else:
        (K, M), (K2, N) = a.shape, b2
    assert K == K2, (name, a.shape, b.shape)
    tm, tn, tk = _tile(M, tm), _tile(N, tn), _tile(K, tk)
    nk = K // tk
    if mode == "tn":
        a_spec = pl.BlockSpec((tk, tm), lambda i, j, k: (k, i))
        lhs_c = 0
    else:
        a_spec = pl.BlockSpec((tm, tk), lambda i, j, k: (i, k))
        lhs_c = 1
    b_blk, b_idx, rhs_c = ((tn, tk), (lambda i, j, k: (j, k)), 1) if mode == "nt" else ((tk, tn), (lambda i, j, k: (k, j)), 0)
    if b_lead is None:
        b_spec = pl.BlockSpec(b_blk, b_idx)
    else:
        b_spec = pl.BlockSpec((None,) + b_blk, functools.partial(lambda i, j, k, f, l: (l,) + f(i, j, k), f=b_idx, l=b_lead))
    dims = (((lhs_c,), (rhs_c,)), ((), ()))
    in_specs, args, aliases = [a_spec, b_spec], [a, b], {}
    if out_lead is None:
        out_spec = pl.BlockSpec((tm, tn), lambda i, j, k: (i, j))
        out_shape = jax.ShapeDtypeStruct((M, N), out_dtype)
    else:
        l_out, n_lead, buf = out_lead
        out_spec = pl.BlockSpec((None, tm, tn), functools.partial(lambda i, j, k, l: (l, i, j), l=l_out))
        out_shape = jax.ShapeDtypeStruct((n_lead, M, N), out_dtype)
        if buf is not None:
            in_specs.append(pl.BlockSpec(memory_space=pl.ANY))
            args.append(buf)
            aliases = {2: 0}
    if token is not None:
        in_specs.append(pl.BlockSpec(memory_space=pl.ANY))
        args.append(token)

    def body(a_ref, b_ref, *rest):
        o_ref, acc_ref = rest[-2:]
        k = pl.program_id(2)

        @pl.when(k == 0)
        def _():
            acc_ref[...] = jnp.zeros_like(acc_ref)

        acc_ref[...] += lax.dot_general(a_ref[...].astype(BF16), b_ref[...].astype(BF16), dims,
                                        preferred_element_type=F32)

        @pl.when(k == nk - 1)
        def _():
            o_ref[...] = acc_ref[...].astype(o_ref.dtype)

    return pl.pallas_call(
        body, name=name, grid=(M // tm, N // tn, nk),
        in_specs=in_specs, out_specs=out_spec, out_shape=out_shape,
        scratch_shapes=[pltpu.VMEM((tm, tn), F32)], input_output_aliases=aliases,
        compiler_params=pltpu.CompilerParams(
            dimension_semantics=("parallel", "parallel", "arbitrary"),
            vmem_limit_bytes=_vmem(_nbytes((tm, tk), a.dtype) + _nbytes((tk, tn), b.dtype) + _nbytes((tm, tn), out_dtype),
                                   _nbytes((tm, tn), F32) + _nbytes((tm, tk), BF16) + _nbytes((tk, tn), BF16))),
    )(*args)


def _rows(fn, name, tm, rows, halos=(), fulls=(), outs=(), accs=()):
    n = S // tm
    in_specs, args = [], []
    for arr, w, cb in rows:
        in_specs.append(pl.BlockSpec((tm, w), functools.partial(lambda i, cb: (i, cb), cb=cb)))
        args.append(arr)
    for arr, w, cb, side in halos:
        if side == "prev":
            im = functools.partial(lambda i, cb: (jnp.maximum(i * (tm // 16) - 1, 0), cb), cb=cb)
        else:
            im = functools.partial(lambda i, cb: (jnp.minimum((i + 1) * (tm // 16), S // 16 - 1), cb), cb=cb)
        in_specs.append(pl.BlockSpec((16, w), im))
        args.append(arr)
    for arr in fulls:
        in_specs.append(pl.BlockSpec(arr.shape, functools.partial(lambda i, nd: (0,) * nd, nd=arr.ndim)))
        args.append(arr)
    out_shape, out_specs, aliases, n_alias = [], [], {}, 0
    for k, o in enumerate(outs):
        if len(o) == 3 and o[2] == "T":
            out_shape.append(jax.ShapeDtypeStruct((o[0], S), o[1]))
            out_specs.append(pl.BlockSpec((o[0], tm), lambda i: (0, i)))
        elif len(o) == 3:
            buf, total, cb = o[2]
            out_shape.append(jax.ShapeDtypeStruct((S, total), o[1]))
            out_specs.append(pl.BlockSpec((tm, o[0]), functools.partial(lambda i, cb: (i, cb), cb=cb)))
            if buf is not None:
                aliases[len(args)] = k
                in_specs.append(pl.BlockSpec(memory_space=pl.ANY))
                args.append(buf)
                n_alias += 1
        else:
            out_shape.append(jax.ShapeDtypeStruct((S, o[0]), o[1]))
            out_specs.append(pl.BlockSpec((tm, o[0]), lambda i: (i, 0)))
    for shp in accs:
        out_shape.append(jax.ShapeDtypeStruct(shp, F32))
        out_specs.append(pl.BlockSpec(shp, functools.partial(lambda i, nd: (0,) * nd, nd=len(shp))))
    nr, nh, nf, no, na = len(rows), len(halos), len(fulls), len(outs), len(accs)
    blocks = (sum(_nbytes((tm, w), arr.dtype) for arr, w, _ in rows) + sum(_nbytes(a.shape, a.dtype) for a in fulls)
              + sum(_nbytes((tm, o[0]), o[1]) for o in outs) + sum(_nbytes(shp, F32) for shp in accs))
    widest = _nbytes((tm, max([w for _, w, _ in rows] + [o[0] for o in outs])), F32)

    def body(*refs):
        i = pl.program_id(0)
        ins, orefs = refs[:nr + nh + nf], refs[nr + nh + nf + n_alias:]
        rv = [r[...].astype(F32) for r in ins[:nr]]
        hv = [r[...].astype(F32)[8:] if h[3] == "prev" else r[...].astype(F32)[:8] for r, h in zip(ins[nr:nr + nh], halos)]
        fv = [r[...] for r in ins[nr + nh:]]
        o, a = fn(i, rv, hv, fv)
        assert len(o) == no and len(a) == na, name
        for spec, ref, val in zip(outs, orefs[:no], o):
            ref[...] = (val.T if len(spec) == 3 and spec[2] == "T" else val).astype(ref.dtype)
        if na:
            @pl.when(i == 0)
            def _():
                for ref in orefs[no:]:
                    ref[...] = jnp.zeros_like(ref)

            for ref, val in zip(orefs[no:], a):
                ref[...] += val

    res = pl.pallas_call(
        body, name=name, grid=(n,), in_specs=in_specs, out_specs=out_specs, out_shape=out_shape,
        input_output_aliases=aliases,
        compiler_params=pltpu.CompilerParams(dimension_semantics=("arbitrary",), vmem_limit_bytes=_vmem(blocks, 6 * widest)),
    )(*args)
    return res


def _shift_down(xb, halo, s, row):
    fix = jnp.tile(pltpu.roll(halo, s, 0), (xb.shape[0] // 8, 1))
    return jnp.where(row >= s, pltpu.roll(xb, s, 0), fix)


def _shift_up(xb, halo, s, row):
    tm = xb.shape[0]
    fix = jnp.tile(pltpu.roll(halo, 8 - s, 0), (tm // 8, 1))
    return jnp.where(row < tm - s, pltpu.roll(xb, tm - s, 0), fix)


def _rms(x):
    return lax.rsqrt(jnp.mean(x * x, axis=-1, keepdims=True) + EPS)


def _rms_bwd(dy, x, g):
    r = _rms(x)
    xh = x * r
    dxh = dy * g
    dx = r * (dxh - xh * jnp.mean(dxh * xh, axis=-1, keepdims=True))
    return dx, dy * xh


def _colsum(x):
    return jnp.sum(x, axis=0, keepdims=True)


def _prenorm_fwd(x, g, token=None):
    def fn(i, rv, hv, fv):
        return [rv[0] * _rms(rv[0]) * fv[0]], []
    return _rows(fn, "prenorm_fwd", 256, [(x, D, 0)], fulls=[g] + ([] if token is None else [token]), outs=[(D, BF16)])[0]


def _gm_mask():
    r = lax.broadcasted_iota(jnp.int32, (GM_B, GM_B), 0) // CHUNK
    c = lax.broadcasted_iota(jnp.int32, (GM_B, GM_B), 1) // CHUNK
    return c <= r


def _gm_norm(v, g, b):
    mu = jnp.mean(v, axis=-1, keepdims=True)
    vc = v - mu
    rs = lax.rsqrt(jnp.mean(vc * vc, axis=-1, keepdims=True) + EPS)
    vh = vc * rs
    return vh, rs, vh * g + b


def _gm_sv(vn, ws, bst):
    mask = _gm_mask()
    gw = GM_W // GM_G
    parts = []
    for g in range(GM_G):
        wm = jnp.where(mask, ws[g], 0.0).astype(BF16)
        parts.append(jnp.dot(wm, vn[:, g * gw:(g + 1) * gw].astype(BF16), preferred_element_type=F32)
                     + bst[:, g:g + 1])
    return jnp.concatenate(parts, axis=1)


def _gmlp_fwd(proj, ln_g, ln_b, ws, bst):
    def fn(i, rv, hv, fv):
        u, v, z = rv
        g, b, w, bt = fv
        _, _, vn = _gm_norm(v, g, b)
        return [u * _gm_sv(vn, w, bt) * _silu(z)], []
    return _rows(fn, "gmlp_fwd", GM_B, [(proj, GM_W, 0), (proj, GM_W, 1), (proj, GM_W, 2)],
                 fulls=[ln_g, ln_b, ws, bst], outs=[(GM_W, BF16)])[0]


def _mla_prep_fwd(proj, qg, kvg):
    def fn(i, rv, hv, fv):
        cq, ckv = rv
        g1, g2 = fv
        return [cq * _rms(cq) * g1, ckv * _rms(ckv) * g2], []
    return _rows(fn, "mla_prep_fwd", 256, [(proj, QR, O_CQ // QR), (proj, KVR, O_CKV // KVR)],
                 fulls=[qg, kvg], outs=[(QR, BF16), (KVR, BF16)])


def _rot(t, cc, sa, sb):
    return t * cc + pltpu.roll(t, 32, 1) * sa + pltpu.roll(t, 96, 1) * sb


def _rot_t(g, cc, sa, sb):
    return g * cc + pltpu.roll(g * sa, 96, 1) + pltpu.roll(g * sb, 32, 1)


def _rope_tables():
    pos = jnp.arange(S, dtype=F32)
    inv_freq = ROPE_THETA ** (-jnp.arange(0, ROPE, 2, dtype=F32) / ROPE)
    ang = pos[:, None] * inv_freq[None, :]
    cos, sin, z = jnp.cos(ang), jnp.sin(ang), jnp.zeros((S, 32), F32)
    cc = jnp.concatenate([cos, cos, z, z], axis=1)
    sa = jnp.concatenate([z, sin, z, z], axis=1)
    sb = jnp.concatenate([-sin, z, z, z], axis=1)
    return cc, sa, sb


ATT_SCALE = 1.0 / math.sqrt(NOPE + ROPE)


def _rope_fwd(q, kv, proj, tabs):
    def fn(i, rv, hv, fv):
        qb, kvb, kr, cc, sa, sb = rv
        krr = _rot(kr, cc, sa, sb)
        qs, ks = [], []
        for h in range(H):
            qs += [qb[:, h * HP:h * HP + 128] * ATT_SCALE, _rot(qb[:, h * HP + 128:(h + 1) * HP], cc, sa, sb) * ATT_SCALE]
            ks += [kvb[:, h * 128:(h + 1) * 128], krr]
        kc = jnp.concatenate(ks, axis=1)
        vv = kvb[:, H * NOPE:]
        return [jnp.concatenate(qs, axis=1), kc, kc, vv, vv], []
    cc, sa, sb = tabs
    return _rows(fn, "rope_fwd", 256,
                 [(q, H * HP, 0), (kv, H * 256, 0), (proj, 128, O_KR // 128), (cc, 128, 0), (sa, 128, 0), (sb, 128, 0)],
                 outs=[(H * HP, BF16), (H * HP, BF16), (H * HP, BF16, "T"), (MLA_W, BF16), (MLA_W, BF16, "T")])


TQ, TC, ATT_NB = 512, 512, 1
ATT_KB = TC * ATT_NB
_NT = (((1,), (1,)), ((), ()))


def _attn_allowed(i, kc):
    kpos = kc * TC + lax.broadcasted_iota(jnp.int32, (TC, TQ), 0)
    qpos = i * TQ + lax.broadcasted_iota(jnp.int32, (TC, TQ), 1)
    return (kpos // CHUNK) <= (qpos // CHUNK)


def _attn_fwd(qc, kc, vt):
    def body(q_ref, k_ref, vt_ref, o_ref, l_ref):
        i = pl.program_id(1)
        q = q_ref[...]

        def scores(sb):
            t0s = [pl.multiple_of((sb * ATT_NB + c) * TC, TC) for c in range(ATT_NB)]
            return [lax.dot_general(k_ref[pl.ds(t0, TC), :], q, _NT, preferred_element_type=F32) for t0 in t0s]

        def block(sb, ss, carry, masked):
            m, l, acc = carry
            t0s = [pl.multiple_of((sb * ATT_NB + c) * TC, TC) for c in range(ATT_NB)]
            if masked:
                ss = [jnp.where(_attn_allowed(i, sb * ATT_NB + c), s, -1e30) for c, s in enumerate(ss)]
            m_new = m
            for s in ss:
                m_new = jnp.maximum(m_new, jnp.max(s, axis=0, keepdims=True))
            alpha = jnp.exp(m - m_new)
            ps = [jnp.exp(s - m_new) for s in ss]
            l = alpha * l
            acc = alpha * acc
            for t0, p in zip(t0s, ps):
                l = l + jnp.sum(p, axis=0, keepdims=True)
                acc = acc + jnp.dot(vt_ref[:, pl.ds(t0, TC)], p.astype(BF16), preferred_element_type=F32)
            return m_new, l, acc

        nsb = ((i + 1) * TQ + ATT_KB - 1) // ATT_KB
        c = (jnp.full((1, TQ), -1e30, F32), jnp.zeros((1, TQ), F32), jnp.zeros((VDIM, TQ), F32))

        def step(sb, sc):
            nxt = scores(sb + 1)
            return nxt, block(sb, sc[0], sc[1], False)

        ss, c = lax.fori_loop(0, nsb - 1, step, (scores(0), c))
        m, l, acc = block(nsb - 1, ss, c, True)
        o_ref[...] = (acc / l).T
        l_ref[...] = m + jnp.log(l)

    return pl.pallas_call(
        body, name="attn_fwd", grid=(H, S // TQ),
        in_specs=[pl.BlockSpec((TQ, HP), lambda h, i: (i, h)),
                  pl.BlockSpec((S, HP), lambda h, i: (0, h)),
                  pl.BlockSpec((VDIM, S), lambda h, i: (h, 0))],
        out_specs=[pl.BlockSpec((TQ, VDIM), lambda h, i: (i, h)), pl.BlockSpec((None, 1, TQ), lambda h, i: (h, 0, i))],
        out_shape=[jax.ShapeDtypeStruct((S, MLA_W), F32), jax.ShapeDtypeStruct((H, 1, S), F32)],
        compiler_params=pltpu.CompilerParams(dimension_semantics=("parallel", "arbitrary"),
                                             vmem_limit_bytes=24 * MIB),
    )(qc, kc, vt)


def _gate_mul_fwd(name, val, proj, width, cb):
    def fn(i, rv, hv, fv):
        o, z = rv
        return [o * _silu(z)], []
    return _rows(fn, name, 256, [(val, width, 0), (proj, width, cb)], outs=[(width, BF16)])[0]


def _conv_fwd(proj, w, b):
    def fn(i, rv, hv, fv):
        (xb,), (halo,), (ww, bb) = rv, hv, fv
        halo = jnp.where(i > 0, halo, 0.0)
        row = lax.broadcasted_iota(jnp.int32, xb.shape, 0)
        acc = bb + ww[3:4] * xb
        for s in range(1, CONV_W):
            acc = acc + ww[3 - s:4 - s] * _shift_down(xb, halo, s, row)
        return [acc, acc], []
    return _rows(fn, "conv_fwd", LRU_TM, [(proj, LRU_W, O_XC // LRU_W)], halos=[(proj, LRU_W, O_XC // LRU_W, "prev")],
                 fulls=[w, b], outs=[(LRU_W, F32), (LRU_W, BF16)])


def _lru_terms(ga, gx, xc, ba, bx, lam):
    r = _sig(ga + ba)
    ig = _sig(gx + bx)
    sp = jnp.maximum(-lam, 0.0) + jnp.log(1.0 + jnp.exp(-jnp.abs(lam)))
    log_a = -LRU_C * r * sp
    a = jnp.exp(log_a)
    e2 = jnp.exp(2.0 * log_a)
    om = 1.0 - e2
    mult = jnp.sqrt(jnp.maximum(om, 0.0))
    return r, ig, sp, a, e2, om, mult


def _lru_gates_fwd(gates, xc, ba, bx, lam):
    def fn(i, rv, hv, fv):
        ga, gx, x = rv
        r, ig, sp, a, e2, om, mult = _lru_terms(ga, gx, x, *fv)
        return [a, mult * (ig * x)], []
    return _rows(fn, "lru_gates_fwd", LRU_TM, [(gates, LRU_W, 0), (gates, LRU_W, 1), (xc, LRU_W, 0)],
                 fulls=[ba, bx, lam], outs=[(LRU_W, F32), (LRU_W, F32)])


SCAN_T, SCAN_CW = 64, 256
LRU_TM = 256


def _scan_fwd(a, b):
    def body(a_ref, b_ref, h_ref):
        row = lax.broadcasted_iota(jnp.int32, (SCAN_T, SCAN_CW), 0)

        def step(blk, hc):
            t0 = pl.multiple_of(blk * SCAN_T, SCAN_T)
            A = a_ref[pl.ds(t0, SCAN_T), :]
            B = b_ref[pl.ds(t0, SCAN_T), :]
            d = 1
            while d < SCAN_T:
                keep = row >= d
                A_s = jnp.where(keep, pltpu.roll(A, d, 0), 1.0)
                B_s = jnp.where(keep, pltpu.roll(B, d, 0), 0.0)
                B = A * B_s + B
                A = A * A_s
                d *= 2
            hh = A * hc + B
            h_ref[pl.ds(t0, SCAN_T), :] = hh
            return hh[SCAN_T - 1:SCAN_T, :]

        lax.fori_loop(0, S // SCAN_T, step, jnp.zeros((1, SCAN_CW), F32))

    spec = pl.BlockSpec((S, SCAN_CW), lambda j: (0, j))
    return pl.pallas_call(
        body, name="scan_fwd", grid=(LRU_W // SCAN_CW,), in_specs=[spec, spec], out_specs=spec,
        out_shape=jax.ShapeDtypeStruct((S, LRU_W), F32),
        compiler_params=pltpu.CompilerParams(dimension_semantics=("parallel",),
                                             vmem_limit_bytes=_vmem(3 * _nbytes((S, SCAN_CW), F32))),
    )(a, b)


def _merge_fwd(pa, pb, pc, proj):
    def fn(i, rv, hv, fv):
        a, b, c, ga, gb, gc = rv
        return [_sig(ga) * a + _sig(gb) * b + _sig(gc) * c], []
    return _rows(fn, "merge_fwd", 256,
                 [(pa, D, 0), (pb, D, 0), (pc, D, 0), (proj, D, O_GA // D), (proj, D, O_GB // D), (proj, D, O_GC // D)],
                 outs=[(D, BF16)])[0]


def _post_fwd(x, o2, g):
    def fn(i, rv, hv, fv):
        xb, ob = rv
        return [xb + ob * _rms(ob) * fv[0]], []
    return _rows(fn, "post_fwd", 256, [(x, D, 0), (o2, D, 0)], fulls=[g], outs=[(D, F32)])[0]


SB = 640
BD_TM = 512


def _bd_fwd(xcb, wsb, l):
    def body(x_ref, w_ref, o_ref):
        o_ref[...] = jnp.dot(x_ref[...], w_ref[...], preferred_element_type=F32).astype(o_ref.dtype)

    return pl.pallas_call(
        body, name="lru_gate_mm", grid=(S // BD_TM, 4),
        in_specs=[pl.BlockSpec((BD_TM, SB), lambda i, q: (i, q % 2)),
                  pl.BlockSpec((None, None, SB, SB), lambda i, q: (l, q, 0, 0))],
        out_specs=pl.BlockSpec((BD_TM, SB), lambda i, q: (i, q)),
        out_shape=jax.ShapeDtypeStruct((S, 2 * LRU_W), BF16),
        compiler_params=pltpu.CompilerParams(dimension_semantics=("parallel", "parallel"), vmem_limit_bytes=VMEM_LIMIT),
    )(xcb, wsb)


def _bd_dx(dgates, wsb, l):
    def body(d_ref, w_ref, o_ref, acc_ref):
        g = pl.program_id(2)

        @pl.when(g == 0)
        def _():
            acc_ref[...] = jnp.zeros_like(acc_ref)

        acc_ref[...] += lax.dot_general(d_ref[...], w_ref[...], (((1,), (1,)), ((), ())), preferred_element_type=F32)

        @pl.when(g == 1)
        def _():
            o_ref[...] = acc_ref[...].astype(o_ref.dtype)

    return pl.pallas_call(
        body, name="lru_gate_dx", grid=(S // BD_TM, 2, 2),
        in_specs=[pl.BlockSpec((BD_TM, SB), lambda i, s, g: (i, 2 * g + s)),
                  pl.BlockSpec((None, None, SB, SB), lambda i, s, g: (l, 2 * g + s, 0, 0))],
        out_specs=pl.BlockSpec((BD_TM, SB), lambda i, s, g: (i, s)),
        out_shape=jax.ShapeDtypeStruct((S, LRU_W), BF16),
        scratch_shapes=[pltpu.VMEM((BD_TM, SB), F32)],
        compiler_params=pltpu.CompilerParams(dimension_semantics=("parallel", "parallel", "arbitrary"),
                                             vmem_limit_bytes=VMEM_LIMIT),
    )(dgates, wsb)


def _bd_dw(xcb, dgates):
    tk = 1024

    def body(x_ref, d_ref, o_ref):
        @pl.when(pl.program_id(1) == 0)
        def _():
            o_ref[...] = jnp.zeros_like(o_ref)

        o_ref[...] += lax.dot_general(x_ref[...], d_ref[...], (((0,), (0,)), ((), ())), preferred_element_type=F32)

    return pl.pallas_call(
        body, name="lru_gate_dw", grid=(4, S // tk),
        in_specs=[pl.BlockSpec((tk, SB), lambda q, k: (k, q % 2)), pl.BlockSpec((tk, SB), lambda q, k: (k, q))],
        out_specs=pl.BlockSpec((None, SB, SB), lambda q, k: (q, 0, 0)),
        out_shape=jax.ShapeDtypeStruct((4, SB, SB), F32),
        compiler_params=pltpu.CompilerParams(dimension_semantics=("parallel", "arbitrary"), vmem_limit_bytes=VMEM_LIMIT),
    )(xcb, dgates)


def _bd_extract(dwsb):
    def body(w_ref, o_ref):
        lane = lax.broadcasted_iota(jnp.int32, (LRU_BW, 128), 1)
        for q in range(4):
            for kk in range(8):
                c0 = LRU_BW * kk
                w0, off = (c0 // 128) * 128, c0 % 128
                rows = pl.ds(LRU_BW * kk, LRU_BW)
                blk = w_ref[q, rows, w0:w0 + 128]
                if off:
                    blk = pltpu.roll(blk, 128 - off, 1)
                    if off + LRU_BW > 128:
                        nxt = pltpu.roll(w_ref[q, rows, w0 + 128:w0 + 256], 128 - off, 1)
                        blk = jnp.where(lane < 128 - off, blk, nxt)
                o_ref[q // 2, 8 * (q % 2) + kk] = blk.astype(BF16)

    return pl.pallas_call(
        body, name="lru_gate_dw_blocks",
        in_specs=[pl.BlockSpec(memory_space=pltpu.VMEM)], out_specs=pl.BlockSpec(memory_space=pltpu.VMEM),
        out_shape=jax.ShapeDtypeStruct((2, LRU_NB, LRU_BW, 128), BF16),
        compiler_params=pltpu.CompilerParams(vmem_limit_bytes=VMEM_LIMIT),
    )(dwsb)


def _layer_fwd(x, P, l, tabs, token=None, late=None):
    A = {"x": x}
    A["h"] = _prenorm_fwd(x, P["pre_g"], token)
    proj = A["proj"] = _mm(A["h"], P["wp"], "nt", "in_proj", out_dtype=BF16, tm=1024)
    A["ya"] = _gmlp_fwd(proj, P["ln_g"], P["ln_b"], P["ws"], P["bst"])
    A["xc"], A["xcb"] = _conv_fwd(proj, P["conv_w"], P["conv_b"])
    A["gates"] = _bd_fwd(A["xcb"], P["wsb"], l)
    A["a"], bterm = _lru_gates_fwd(A["gates"], A["xc"], P["ba"], P["bx"], P["lam"])
    A["hs"] = _scan_fwd(A["a"], bterm)
    A["yc"] = _gate_mul_fwd("yc_fwd", A["hs"], proj, LRU_W, O_ZC // LRU_W)
    if late is not None:
        P.update(late(A["yc"]))
    A["cqn"], A["ckvn"] = _mla_prep_fwd(proj, P["qg"], P["kvg"])
    q = _mm(A["cqn"], P["wuq"], "nt", "q_up", out_dtype=BF16)
    kv = _mm(A["ckvn"], P["wukv"], "nt", "kv_up", out_dtype=BF16)
    A["qc"], A["kc"], A["kct"], A["vv"], vt = _rope_fwd(q, kv, proj, tabs)
    A["o"], A["lse"] = _attn_fwd(A["qc"], A["kc"], vt)
    A["yb"] = _gate_mul_fwd("yb_fwd", A["o"], proj, MLA_W, O_ZB // MLA_W)
    A["pa"] = _mm(A["ya"], P["wpa"], "nn", "proj_a", out_dtype=BF16)
    A["pb"] = _mm(A["yb"], P["wpb"], "nn", "proj_b", out_dtype=BF16)
    A["pc"] = _mm(A["yc"], P["wpc"], "nn", "proj_c", out_dtype=BF16)
    A["merged"] = _merge_fwd(A["pa"], A["pb"], A["pc"], proj)
    A["o2"] = _mm(A["merged"], P["wout"], "nn", "out_proj")
    return _post_fwd(x, A["o2"], P["post_g"]), A


def _loss_fwd(y, tgt):
    def fn(i, rv, hv, fv):
        yb, tb = rv
        e = yb - tb
        part = 0.5 * jnp.sum(jnp.mean(e * e, axis=-1, keepdims=True), axis=0, keepdims=True)
        return [e * (1.0 / D)], [part]
    return _rows(fn, "loss", 256, [(y, D, 0), (tgt, D, 0)], outs=[(D, F32)], accs=[(1, 1)])


def _post_bwd(dxn, o2, g, token=None):
    def fn(i, rv, hv, fv):
        dy, ob = rv
        dx, dg = _rms_bwd(dy, ob, fv[0])
        return [dx], [_colsum(dg)]
    return _rows(fn, "post_bwd", 256, [(dxn, D, 0), (o2, D, 0)], fulls=[g] + ([] if token is None else [token]),
                 outs=[(D, BF16)], accs=[(1, D)])


def _merge_bwd(dm, pa, pb, pc, proj, dproj):
    def fn(i, rv, hv, fv):
        d, a, b, c, ga, gb, gc = rv
        outs_p, outs_g = [], []
        for p, gg in ((a, ga), (b, gb), (c, gc)):
            s = _sig(gg)
            outs_p.append(d * s)
            outs_g.append(d * p * s * (1.0 - s))
        return outs_p + [jnp.concatenate(outs_g, axis=1)], []
    return _rows(fn, "merge_bwd", 256,
                 [(dm, D, 0), (pa, D, 0), (pb, D, 0), (pc, D, 0),
                  (proj, D, O_GA // D), (proj, D, O_GB // D), (proj, D, O_GC // D)],
                 outs=[(D, BF16)] * 3 + [(3 * D, BF16, (dproj, NP, O_GA // (3 * D)))])


def _gmlp_bwd(dya, proj, ln_g, ln_b, ws, bst, dproj):
    gw = GM_W // GM_G

    def fn(i, rv, hv, fv):
        dy, u, v, z = rv
        g, b, w, bt = fv
        vh, rs, vn = _gm_norm(v, g, b)
        sv = _gm_sv(vn, w, bt)
        sz = _silu(z)
        du = dy * sv * sz
        dsv = dy * u * sz
        dz = dy * u * sv * _dsilu(z)
        mask = _gm_mask()
        lane = lax.broadcasted_iota(jnp.int32, (GM_B, 128), 1)
        dvn_parts, dws, dbst = [], [], jnp.zeros((GM_B, 128), F32)
        for k in range(GM_G):
            wm = jnp.where(mask, w[k], 0.0).astype(BF16)
            dsk = dsv[:, k * gw:(k + 1) * gw]
            dskb = dsk.astype(BF16)
            dvn_parts.append(lax.dot_general(wm, dskb, (((0,), (0,)), ((), ())), preferred_element_type=F32))
            dwk = lax.dot_general(dskb, vn[:, k * gw:(k + 1) * gw].astype(BF16), (((1,), (1,)), ((), ())),
                                  preferred_element_type=F32)
            dws.append(jnp.where(mask, dwk, 0.0)[None])
            dbst = dbst + jnp.where(lane == k, jnp.sum(dsk, axis=1, keepdims=True), 0.0)
        dvn = jnp.concatenate(dvn_parts, axis=1)
        dvh = dvn * g
        dv = rs * (dvh - jnp.mean(dvh, axis=-1, keepdims=True) - vh * jnp.mean(dvh * vh, axis=-1, keepdims=True))
        return ([jnp.concatenate([du, dv, dz], axis=1)],
                [jnp.concatenate(dws, axis=0), dbst, _colsum(dvn * vh), _colsum(dvn)])
    return _rows(fn, "gmlp_bwd", GM_B, [(dya, GM_W, 0), (proj, GM_W, 0), (proj, GM_W, 1), (proj, GM_W, 2)],
                 fulls=[ln_g, ln_b, ws, bst], outs=[(3 * GM_W, BF16, (dproj, NP, O_U // (3 * GM_W)))],
                 accs=[(GM_G, GM_B, GM_B), (GM_B, 128), (1, GM_W), (1, GM_W)])


def _yb_bwd(dyb, o, proj, dproj):
    def fn(i, rv, hv, fv):
        dy, ob, z = rv
        do = dy * _silu(z)
        prod = do * ob
        lane = lax.broadcasted_iota(jnp.int32, (dy.shape[0], 128), 1)
        dl = jnp.zeros((dy.shape[0], 128), F32)
        for h in range(H):
            dl = dl + jnp.where(lane == h, jnp.sum(prod[:, h * VDIM:(h + 1) * VDIM], axis=1, keepdims=True), 0.0)
        return [do, dl, dy * ob * _dsilu(z)], []
    return _rows(fn, "yb_bwd", 256, [(dyb, MLA_W, 0), (o, MLA_W, 0), (proj, MLA_W, O_ZB // MLA_W)],
                 outs=[(MLA_W, BF16), (128, F32, "T"), (MLA_W, BF16, (dproj, NP, O_ZB // MLA_W))])


def _attn_bwd(qc, kc, kct, vv, do, lse, dlt):
    def body(q_ref, k_ref, kt_ref, v_ref, do_ref, l_ref, d_ref, dq_ref, dk_ref, dv_ref, dqt_ref):
        h, i = pl.program_id(0), pl.program_id(1)

        @pl.when(i == 0)
        def _():
            dk_ref[...] = jnp.zeros_like(dk_ref)
            dv_ref[...] = jnp.zeros_like(dv_ref)

        q = q_ref[...]
        dob = do_ref[...]
        lse = l_ref[...]
        dl = d_ref[pl.ds(h, 1), :]
        dqt_ref[...] = jnp.zeros_like(dqt_ref)

        def rows_of(sb, c):
            return pl.ds(pl.multiple_of((sb * ATT_NB + c) * TC, TC), TC)

        def front(sb):
            return [(lax.dot_general(k_ref[rows_of(sb, c), :], q, _NT, preferred_element_type=F32),
                     lax.dot_general(v_ref[rows_of(sb, c), :], dob, _NT, preferred_element_type=F32))
                    for c in range(ATT_NB)]

        def block(sb, sd, masked):
            dqt = None
            for c, (s, dp) in enumerate(sd):
                rows = rows_of(sb, c)
                p = jnp.exp(s - lse)
                if masked:
                    p = jnp.where(_attn_allowed(i, sb * ATT_NB + c), p, 0.0)
                ds = (p * (dp - dl)).astype(BF16)
                dk_ref[rows, :] += jnp.dot(ds, q, preferred_element_type=F32)
                dv_ref[rows, :] += jnp.dot(p.astype(BF16), dob, preferred_element_type=F32)
                part = jnp.dot(kt_ref[:, rows], ds, preferred_element_type=F32)
                dqt = part if dqt is None else dqt + part
            dqt_ref[...] += dqt

        def step(sb, sd):
            nxt = front(sb + 1)
            block(sb, sd, False)
            return nxt

        nsb = ((i + 1) * TQ + ATT_KB - 1) // ATT_KB
        sd = lax.fori_loop(0, nsb - 1, step, front(0))
        block(nsb - 1, sd, True)
        dq_ref[...] = dqt_ref[...].T.astype(dq_ref.dtype)

    blk = lambda w: pl.BlockSpec((TQ, w), lambda h, i: (i, h))
    head = lambda w: pl.BlockSpec((S, w), lambda h, i: (0, h))
    return pl.pallas_call(
        body, name="attn_bwd", grid=(H, S // TQ),
        in_specs=[blk(HP), head(HP), pl.BlockSpec((HP, S), lambda h, i: (h, 0)), head(VDIM), blk(VDIM),
                  pl.BlockSpec((None, 1, TQ), lambda h, i: (h, 0, i)), pl.BlockSpec((8, TQ), lambda h, i: (0, i))],
        out_specs=[blk(HP), head(HP), head(VDIM)],
        out_shape=[jax.ShapeDtypeStruct((S, H * HP), BF16), jax.ShapeDtypeStruct((S, H * HP), F32),
                   jax.ShapeDtypeStruct((S, MLA_W), F32)],
        scratch_shapes=[pltpu.VMEM((HP, TQ), F32)],
        compiler_params=pltpu.CompilerParams(dimension_semantics=("parallel", "arbitrary"),
                                             vmem_limit_bytes=28 * MIB),
    )(qc, kc, kct, vv, do, lse, dlt)


def _rope_bwd(dqc, dkc, dvv, tabs):
    def fn(i, rv, hv, fv):
        dq, dk, dv, cc, sa, sb = rv
        qs, ks = [], []
        dkr = jnp.zeros((dq.shape[0], 128), F32)
        for h in range(H):
            qs += [dq[:, h * HP:h * HP + 128] * ATT_SCALE, _rot_t(dq[:, h * HP + 128:(h + 1) * HP], cc, sa, sb) * ATT_SCALE]
            ks.append(dk[:, h * HP:h * HP + 128])
            dkr = dkr + dk[:, h * HP + 128:(h + 1) * HP]
        return [jnp.concatenate(qs, axis=1), jnp.concatenate(ks + [dv], axis=1), _rot_t(dkr, cc, sa, sb)], []
    cc, sa, sb = tabs
    return _rows(fn, "rope_bwd", 256,
                 [(dqc, H * HP, 0), (dkc, H * HP, 0), (dvv, MLA_W, 0), (cc, 128, 0), (sa, 128, 0), (sb, 128, 0)],
                 outs=[(H * HP, BF16), (H * 256, BF16), (128, BF16)])


MLA_GROUP = 1536


def _mla_prep_bwd(dcqn, dckvn, dkr, proj, qg, kvg, dproj):
    def fn(i, rv, hv, fv):
        d1, d2, dk, cq, ckv = rv
        g1, g2 = fv
        dx1, dg1 = _rms_bwd(d1, cq, g1)
        dx2, dg2 = _rms_bwd(d2, ckv, g2)
        zeros = jnp.zeros((d1.shape[0], MLA_GROUP - KVR - 128 - QR), F32)
        return [jnp.concatenate([dx2, dk.astype(F32), dx1, zeros], axis=1)], [_colsum(dg1), _colsum(dg2)]
    return _rows(fn, "mla_prep_bwd", 256,
                 [(dcqn, QR, 0), (dckvn, KVR, 0), (dkr, 128, 0), (proj, QR, O_CQ // QR), (proj, KVR, O_CKV // KVR)],
                 fulls=[qg, kvg], outs=[(MLA_GROUP, BF16, (dproj, NP, O_CKV // MLA_GROUP))], accs=[(1, QR), (1, KVR)])


def _yc_bwd(dyc, hs, proj, dproj):
    def fn(i, rv, hv, fv):
        dy, hh, z = rv
        return [dy * _silu(z), dy * hh * _dsilu(z)], []
    return _rows(fn, "yc_bwd", LRU_TM, [(dyc, LRU_W, 0), (hs, LRU_W, 0), (proj, LRU_W, O_ZC // LRU_W)],
                 outs=[(LRU_W, F32), (LRU_W, BF16, (dproj, NP, O_ZC // LRU_W))])


def _scan_bwd(a, hs, dh):
    nblk = S // SCAN_T

    def body(a_ref, h_ref, dh_ref, da_ref, db_ref):
        row = lax.broadcasted_iota(jnp.int32, (SCAN_T, SCAN_CW), 0)

        def step(j, carry):
            gc, ac = carry
            blk = nblk - 1 - j
            t0 = pl.multiple_of(blk * SCAN_T, SCAN_T)
            av = a_ref[pl.ds(t0, SCAN_T), :]
            A = jnp.where(row < SCAN_T - 1, pltpu.roll(av, SCAN_T - 1, 0), ac)
            B = dh_ref[pl.ds(t0, SCAN_T), :]
            d = 1
            while d < SCAN_T:
                keep = row < SCAN_T - d
                A_s = jnp.where(keep, pltpu.roll(A, SCAN_T - d, 0), 1.0)
                B_s = jnp.where(keep, pltpu.roll(B, SCAN_T - d, 0), 0.0)
                B = A * B_s + B
                A = A * A_s
                d *= 2
            g = A * gc + B
            p0 = pl.multiple_of(jnp.maximum(t0 - 8, 0), 8)
            last = jnp.where(blk > 0, h_ref[pl.ds(p0, 8), :][7:8, :], 0.0)
            h_prev = jnp.where(row >= 1, pltpu.roll(h_ref[pl.ds(t0, SCAN_T), :], 1, 0), last)
            da_ref[pl.ds(t0, SCAN_T), :] = g * h_prev
            db_ref[pl.ds(t0, SCAN_T), :] = g
            return g[0:1, :], av[0:1, :]

        z = jnp.zeros((1, SCAN_CW), F32)
        lax.fori_loop(0, nblk, step, (z, z))

    spec = pl.BlockSpec((S, SCAN_CW), lambda j: (0, j))
    return pl.pallas_call(
        body, name="scan_bwd", grid=(LRU_W // SCAN_CW,), in_specs=[spec] * 3, out_specs=[spec] * 2,
        out_shape=[jax.ShapeDtypeStruct((S, LRU_W), F32)] * 2,
        compiler_params=pltpu.CompilerParams(dimension_semantics=("parallel",),
                                             vmem_limit_bytes=_vmem(5 * _nbytes((S, SCAN_CW), F32))),
    )(a, hs, dh)


def _lru_gates_bwd(da, db, gates, xc, ba, bx, lam):
    def fn(i, rv, hv, fv):
        dav, dbv, ga, gx, x = rv
        bav, bxv, lamv = fv
        r, ig, sp, a, e2, om, mult = _lru_terms(ga, gx, x, bav, bxv, lamv)
        dmult = dbv * ig * x
        dig = dbv * mult * x
        dxc1 = dbv * mult * ig
        dlog_a = dav * a + jnp.where(om > 0.0, dmult * (-e2 / mult), 0.0)
        dr = dlog_a * (-LRU_C * sp)
        dga = dr * r * (1.0 - r)
        dgx = dig * ig * (1.0 - ig)
        dlam = _colsum(dlog_a * (-LRU_C * r)) * (-_sig(-lamv))
        return [jnp.concatenate([dga, dgx], axis=1), dxc1], [_colsum(dga), _colsum(dgx), dlam]
    return _rows(fn, "lru_gates_bwd", LRU_TM,
                 [(da, LRU_W, 0), (db, LRU_W, 0), (gates, LRU_W, 0), (gates, LRU_W, 1), (xc, LRU_W, 0)],
                 fulls=[ba, bx, lam], outs=[(2 * LRU_W, BF16), (LRU_W, F32)], accs=[(1, LRU_W)] * 3)


def _conv_bwd(dxc1, dxc2, proj, w, dproj):
    cb = O_XC // LRU_W

    def fn(i, rv, hv, fv):
        d1, d2, xb = rv
        n1, n2, xprev = hv
        ww = fv[0]
        last = i == S // LRU_TM - 1
        dxc = d1 + d2
        nxt = jnp.where(last, 0.0, n1 + n2)
        xprev = jnp.where(i > 0, xprev, 0.0)
        row = lax.broadcasted_iota(jnp.int32, xb.shape, 0)
        dx = ww[3:4] * dxc
        dws = [None] * CONV_W
        dws[3] = _colsum(dxc * xb)
        for s in range(1, CONV_W):
            dx = dx + ww[3 - s:4 - s] * _shift_up(dxc, nxt, s, row)
            dws[3 - s] = _colsum(dxc * _shift_down(xb, xprev, s, row))
        return [dx], [jnp.concatenate(dws, axis=0), _colsum(dxc)]
    return _rows(fn, "conv_bwd", LRU_TM, [(dxc1, LRU_W, 0), (dxc2, LRU_W, 0), (proj, LRU_W, cb)],
                 halos=[(dxc1, LRU_W, 0, "next"), (dxc2, LRU_W, 0, "next"), (proj, LRU_W, cb, "prev")],
                 fulls=[w], outs=[(LRU_W, BF16, (dproj, NP, cb))], accs=[(CONV_W, LRU_W), (1, LRU_W)])


def _prenorm_bwd(dxn, dh, x, g):
    def fn(i, rv, hv, fv):
        dy, dhh, xb = rv
        dx, dg = _rms_bwd(dhh, xb, fv[0])
        return [dy + dx], [_colsum(dg)]
    return _rows(fn, "prenorm_bwd", 256, [(dxn, D, 0), (dh, D, 0), (x, D, 0)], fulls=[g], outs=[(D, F32)],
                 accs=[(1, D)])


def _layer_bwd(dxn, A, P, l, tabs, token=None, early=None):
    G, GB = {}, {}
    proj = A["proj"]

    def dw(key, a, b, name, **tiles):
        GB[key] = _mm(a, b, "tn", name, out_dtype=BF16, **tiles)

    do2, G["post_g"] = _post_bwd(dxn, A["o2"], P["post_g"], token)
    dm = _mm(do2, P["wout"], "nt", "out_proj_dx", out_dtype=BF16)
    dw("wout", A["merged"], do2, "out_proj_dw")
    dpa, dpb, dpc, dproj = _merge_bwd(dm, A["pa"], A["pb"], A["pc"], proj, None)
    dya = _mm(dpa, P["wpa"], "nt", "proj_a_dx", out_dtype=BF16)
    dw("wpa", A["ya"], dpa, "proj_a_dw")
    dyb = _mm(dpb, P["wpb"], "nt", "proj_b_dx", out_dtype=BF16)
    dw("wpb", A["yb"], dpb, "proj_b_dw")
    dyc = _mm(dpc, P["wpc"], "nt", "proj_c_dx", out_dtype=BF16)
    dw("wpc", A["yc"], dpc, "proj_c_dw")
    dproj, G["ws"], G["bst"], G["ln_g"], G["ln_b"] = _gmlp_bwd(dya, proj, P["ln_g"], P["ln_b"], P["ws"], P["bst"], dproj)
    do, dl, dproj = _yb_bwd(dyb, A["o"], proj, dproj)
    dqc, dkc, dvv = _attn_bwd(A["qc"], A["kc"], A["kct"], A["vv"], do, A["lse"], dl)
    dq, dkv, dkr = _rope_bwd(dqc, dkc, dvv, tabs)
    dcqn = _mm(dq, P["wuq"], "nn", "q_up_dx", out_dtype=BF16)
    dw("wuq", dq, A["cqn"], "q_up_dw")
    dckvn = _mm(dkv, P["wukv"], "nn", "kv_up_dx", out_dtype=BF16)
    dw("wukv", dkv, A["ckvn"], "kv_up_dw")
    dproj, G["qg"], G["kvg"] = _mla_prep_bwd(dcqn, dckvn, dkr, proj, P["qg"], P["kvg"], dproj)
    dhs, dproj = _yc_bwd(dyc, A["hs"], proj, dproj)
    da, db = _scan_bwd(A["a"], A["hs"], dhs)
    dgates, dxc1, G["ba"], G["bx"], G["lam"] = _lru_gates_bwd(da, db, A["gates"], A["xc"], P["ba"], P["bx"], P["lam"])
    dxc2 = _bd_dx(dgates, P["wsb"], l)
    G["wab"] = _bd_extract(_bd_dw(A["xcb"], dgates))
    dproj, G["conv_w"], G["conv_b"] = _conv_bwd(dxc1, dxc2, proj, P["conv_w"], dproj)
    tok = (None, None) if early is None else early(GB)
    dh = _mm(dproj, P["wp"], "nn", "in_proj_dx", tm=1024, tn=1024, token=tok[0])
    dw("wp", dproj, A["h"], "in_proj_dw", tm=1536, tn=1024, token=tok[1])
    dx, G["pre_g"] = _prenorm_bwd(dxn, dh, A["x"], P["pre_g"])
    return dx, G, GB


_ORIG_OFF = [0]
for _s in IN_SIZES:
    _ORIG_OFF.append(_ORIG_OFF[-1] + _s)
_PAD_OFF = {0: O_U, 1: O_V, 2: O_ZA, 3: O_CQ, 4: O_CKV, 5: O_KR, 6: O_ZB, 7: O_XC, 8: O_ZC, 9: O_GA, 10: O_GB, 11: O_GC}
SHARD_IN = N_IN // N_CHIPS


def _pieces_w_in(j):
    lo, hi = SHARD_IN * j, SHARD_IN * (j + 1)
    out = []
    for k in range(len(IN_SIZES)):
        a, b = max(lo, _ORIG_OFF[k]), min(hi, _ORIG_OFF[k + 1])
        if a < b:
            out.append((a - lo, _PAD_OFF[k] + a - _ORIG_OFF[k], b - a))
    return out


def _pieces_uq(j):
    return [(192 * hh, HP * (2 * j + hh), NOPE + ROPE) for hh in range(2)]


def _pieces_ukv(j):
    out = []
    for hh in range(2):
        h = 2 * j + hh
        out += [(256 * hh, NOPE * h, NOPE), (256 * hh + NOPE, H * NOPE + VDIM * h, VDIM)]
    return out


def _pieces_rows(r):
    return lambda j: [(0, r * j, r)]


LAYOUT = {
    "w_in": (SHARD_IN, NP, _pieces_w_in),
    "mla_w_uq": (2 * (NOPE + ROPE), H * HP, _pieces_uq),
    "mla_w_ukv": (2 * (NOPE + VDIM), 2 * H * 128, _pieces_ukv),
    "lru_conv_w": (1, N_CHIPS, _pieces_rows(1)),
    "w_proj_a": (GM_W // N_CHIPS, GM_W, _pieces_rows(GM_W // N_CHIPS)),
    "w_proj_b": (MLA_W // N_CHIPS, MLA_W, _pieces_rows(MLA_W // N_CHIPS)),
    "w_proj_c": (LRU_W // N_CHIPS, LRU_W, _pieces_rows(LRU_W // N_CHIPS)),
    "w_out": (D // N_CHIPS, D, _pieces_rows(D // N_CHIPS)),
}
TRANSPOSED = ("w_in", "mla_w_uq", "mla_w_ukv")


def _superblocks(w_a, w_x):
    w6 = jnp.stack([w_a, w_x], axis=1).reshape(DEPTH, 4, 8, LRU_BW, LRU_BW).astype(BF16)
    bands = [jnp.pad(w6[:, :, k], ((0, 0), (0, 0), (0, 0), (LRU_BW * k, SB - LRU_BW * (k + 1)))) for k in range(8)]
    return jnp.concatenate(bands, axis=2)


_HBM = pl.BlockSpec(memory_space=pltpu.HBM)


def _position():
    return lax.axis_index("x"), lax.axis_index("y"), lax.axis_index("c")


_REL = (2, 1, 3)


def _cut(r):
    return r if r < 32 else (r // 2 + 15) // 16 * 16


def _half_rows(r, c0):
    return _cut(r) if c0 == 0 else r - _cut(r)


def _half_pieces(lay_a, jsrc, c0):
    r = lay_a[0]
    lo, hi = (0, _cut(r)) if c0 == 0 else (_cut(r), r)
    out = []
    for s0, d0, nr in lay_a[2](jsrc):
        a, b = max(s0, lo), min(s0 + nr, hi)
        if a < b:
            out.append((a, d0 + a - s0, b - a))
    return out


PAD_BLOCKS = {"w_in": (64, [(O_KR + ROPE) // 64] + list(range((O_CQ + QR) // 64, O_XC // 64))),
              "mla_w_uq": (64, [(HP * h + NOPE + ROPE) // 64 for h in range(H)])}


def _zero_blocks(buf, rows, blocks, name):
    rest = buf.shape[1:]
    z = (0,) * len(rest)

    def body(ids_ref, buf_ref, o_ref):
        o_ref[...] = jnp.zeros_like(o_ref)

    return pl.pallas_call(
        body, name=name,
        grid_spec=pltpu.PrefetchScalarGridSpec(
            num_scalar_prefetch=1, grid=(len(blocks),), in_specs=[pl.BlockSpec(memory_space=pl.ANY)],
            out_specs=pl.BlockSpec((rows,) + rest, lambda i, ids: (ids[i],) + z)),
        out_shape=jax.ShapeDtypeStruct(buf.shape, buf.dtype), input_output_aliases={1: 0},
    )(jnp.asarray(blocks, jnp.int32), buf)


def _gather_zeros(names, srcs):
    out = []
    for nm, s in zip(names, srcs):
        zone = lax.empty((LAYOUT[nm][1],) + s.shape[1:], s.dtype)
        out.append(_zero_blocks(zone, *PAD_BLOCKS[nm], "zero_pad_" + nm) if nm in PAD_BLOCKS else zone)
    return out


def _weights_allgather(names, srcs, name, carry=()):
    n = len(srcs)
    lay = [LAYOUT[nm] for nm in names]
    zeros = _gather_zeros(names, srcs)
    m = len(carry)

    def body(*refs):
        ins, outs = refs[:n], refs[2 * n + m:3 * n + m]
        send, recv, lsem = refs[3 * n + 2 * m:]
        x, y, c = _position()
        j = 2 * x + y
        sib = (x, y, 1 - c)
        chips = [(1 - x, y), (x, 1 - y), (1 - x, 1 - y)]

        def flow(a, k, jsrc, c0, to, from_src):
            cps = []
            for s0, d0, nr in _half_pieces(lay[a], jsrc, c0):
                dst = outs[a].at[pl.ds(d0, nr)]
                src = ins[a].at[pl.ds(s0, nr)] if from_src else dst
                cps.append(pltpu.make_async_remote_copy(src_ref=src, dst_ref=dst, send_sem=send.at[7 * a + k],
                                                        recv_sem=recv.at[7 * a + k], device_id=to, device_id_type=MESH))
            return cps

        def sized(a, k, rows):
            ref = ins[a].at[pl.ds(0, rows)]
            return pltpu.make_async_remote_copy(src_ref=ref, dst_ref=ref, send_sem=send.at[7 * a + k],
                                                recv_sem=recv.at[7 * a + k], device_id=sib, device_id_type=MESH)

        for j0 in range(N_CHIPS):
            for c0 in range(2):
                @pl.when((j == j0) & (c == c0))
                def _(j0=j0, c0=c0):
                    mine = [_half_rows(lay[a][0], c0) for a in range(n)]
                    theirs = [_half_rows(lay[a][0], 1 - c0) for a in range(n)]
                    for a in range(n):
                        for s0, d0, nr in _half_pieces(lay[a], j0, c0):
                            pltpu.make_async_copy(ins[a].at[pl.ds(s0, nr)], outs[a].at[pl.ds(d0, nr)], lsem.at[a]).start()
                    for a in range(n):
                        for cp in flow(a, 0, j0, c0, sib, True):
                            cp.start()
                        for k, chip in enumerate(chips):
                            for cp in flow(a, 1 + k, j0, c0, (*chip, c), True):
                                cp.start()
                    for k in range(3):
                        for a in range(n):
                            if mine[a]:
                                sized(a, 1 + k, mine[a]).wait_recv()
                                for cp in flow(a, 4 + k, j0 ^ _REL[k], c0, sib, False):
                                    cp.start()
                    for a in range(n):
                        if theirs[a]:
                            sized(a, 0, theirs[a]).wait_recv()
                            for k in range(3):
                                sized(a, 4 + k, theirs[a]).wait_recv()
                    for a in range(n):
                        if mine[a]:
                            for k in range(7):
                                sized(a, k, mine[a]).wait_send()
                            ref = ins[a].at[pl.ds(0, mine[a])]
                            pltpu.make_async_copy(ref, ref, lsem.at[a]).wait()

    res = pl.pallas_call(
        body, name=name,
        out_shape=[jax.ShapeDtypeStruct(z.shape, z.dtype) for z in list(zeros) + list(carry)],
        in_specs=[_HBM] * (2 * n + m), out_specs=[_HBM] * (n + m),
        input_output_aliases={n + a: a for a in range(n + m)},
        scratch_shapes=[pltpu.SemaphoreType.DMA((7 * n,)), pltpu.SemaphoreType.DMA((7 * n,)),
                        pltpu.SemaphoreType.DMA((n,))],
    )(*srcs, *zeros, *carry)
    return res[:n], res[n:]


_SEM = pl.BlockSpec(memory_space=pltpu.SEMAPHORE)
_VMEM_TOKEN = pl.BlockSpec(memory_space=pltpu.VMEM)
_TOKEN = jax.ShapeDtypeStruct((8, 128), F32)
_EFFECT = pltpu.SideEffectType.DATAFLOW_SIDE_EFFECTING


def _gather_start(names, srcs, zeros, name, after=None):
    n = len(srcs)
    lay = [LAYOUT[nm] for nm in names]
    extra = [] if after is None else [after]

    def body(*refs):
        ins, lands = refs[:n], refs[n:2 * n]
        send, recv, lsem = refs[2 * n + len(extra):2 * n + len(extra) + 3]
        refs[-1][...] = jnp.zeros_like(refs[-1])
        x, y, c = _position()
        j = 2 * x + y
        chips = [(1 - x, y), (x, 1 - y), (1 - x, 1 - y)]
        for j0 in range(N_CHIPS):
            @pl.when(j == j0)
            def _(j0=j0):
                for a in range(n):
                    for s0, d0, nr in lay[a][2](j0):
                        src, dst = ins[a].at[pl.ds(s0, nr)], lands[a].at[pl.ds(d0, nr)]
                        pltpu.make_async_copy(src, dst, lsem.at[a]).start()
                        for k, chip in enumerate(chips):
                            pltpu.make_async_remote_copy(src_ref=src, dst_ref=dst, send_sem=send.at[3 * a + k],
                                                         recv_sem=recv.at[3 * a + k], device_id=(*chip, c),
                                                         device_id_type=MESH).start()

    sems = [pltpu.SemaphoreType.DMA((3 * n,)), pltpu.SemaphoreType.DMA((3 * n,)), pltpu.SemaphoreType.DMA((n,))]
    hbm = lambda a: pltpu.HBM(a.shape, a.dtype)
    res = pl.pallas_call(
        body, name=name,
        out_shape=sems + [hbm(s) for s in srcs] + [hbm(z) for z in zeros] + [_TOKEN],
        in_specs=[_HBM] * (2 * n) + [pl.BlockSpec(memory_space=pl.ANY)] * len(extra),
        out_specs=[_SEM] * 3 + [_HBM] * (2 * n) + [_VMEM_TOKEN],
        input_output_aliases={a: 3 + a for a in range(2 * n)},
        compiler_params=pltpu.CompilerParams(has_side_effects=_EFFECT),
    )(*[pltpu.with_memory_space_constraint(s, pltpu.HBM) for s in srcs],
      *[pltpu.with_memory_space_constraint(z, pltpu.HBM) for z in zeros], *extra)
    return res[:3], res[3:3 + n], res[3 + n:3 + 2 * n], res[-1]


def _gather_wait(names, sems, srcs, lands, after, name):
    n = len(srcs)
    lay = [LAYOUT[nm] for nm in names]

    def body(*refs):
        ins, zones = refs[:n], refs[n:2 * n]
        send, recv, lsem = refs[2 * n:2 * n + 3]
        x, y, c = _position()
        for a in range(n):
            whole = zones[a].at[pl.ds(0, lay[a][0])]
            for k in range(3):
                cp = pltpu.make_async_remote_copy(src_ref=ins[a], dst_ref=whole, send_sem=send.at[3 * a + k],
                                                  recv_sem=recv.at[3 * a + k], device_id=(x, y, 1 - c),
                                                  device_id_type=MESH)
                cp.wait_send()
                cp.wait_recv()
            pltpu.make_async_copy(ins[a], whole, lsem.at[a]).wait()

    hbm = lambda a: pltpu.HBM(a.shape, a.dtype)
    res = pl.pallas_call(
        body, name=name,
        out_shape=[hbm(s) for s in srcs] + [hbm(z) for z in lands],
        in_specs=[_HBM] * (2 * n) + [_SEM] * 3 + [pl.BlockSpec(memory_space=pl.ANY)], out_specs=[_HBM] * (2 * n),
        input_output_aliases={a: a for a in range(2 * n)},
        compiler_params=pltpu.CompilerParams(has_side_effects=_EFFECT),
    )(*srcs, *lands, *sems, after)
    return res[n:]


def _clip_pieces(lay_a, jsrc, c0):
    h = lay_a[1] // 2
    lo, hi = c0 * h, (c0 + 1) * h
    out = []
    for s0, d0, nr in lay_a[2](jsrc):
        a, b = max(d0, lo), min(d0 + nr, hi)
        if a < b:
            out.append((s0 + a - d0, a, b - a))
    return out


def _rows_of(pieces):
    return sum(nr for _, _, nr in pieces)


def _both_cores(body_for):
    x, y, c = _position()
    j = 2 * x + y
    for j0 in range(N_CHIPS):
        for c0 in range(2):
            @pl.when((j == j0) & (c == c0))
            def _(j0=j0, c0=c0):
                body_for(j0, c0)


STAGE_ROWS = 512


def _staged_copy(src, dst, buf, sem_in, sem_out, rows):
    ch = buf.shape[0]
    for r in range(0, rows, ch):
        nr = min(ch, rows - r)
        stage = buf.at[pl.ds(0, nr)]
        cin = pltpu.make_async_copy(src.at[pl.ds(r, nr)], stage, sem_in)
        cin.start()
        cin.wait()
        cout = pltpu.make_async_copy(stage, dst.at[pl.ds(r, nr)], sem_out)
        cout.start()
        cout.wait()


def _half_to_sibling(names, gl, name, after=None):
    n = len(gl)
    halves = [LAYOUT[nm][1] // 2 for nm in names]
    extra = [] if after is None else [after]

    def body(*refs):
        ins, outs = refs[:n], refs[n + len(extra):2 * n + len(extra)]
        send, recv = refs[2 * n + len(extra):]
        x, y, c = _position()

        def run(j0, c0):
            cps = [pltpu.make_async_remote_copy(src_ref=ins[a].at[pl.ds((1 - c0) * halves[a], halves[a])], dst_ref=outs[a],
                                                send_sem=send.at[a], recv_sem=recv.at[a], device_id=(x, y, 1 - c),
                                                device_id_type=MESH) for a in range(n)]
            for cp in cps:
                cp.start()
            for cp in cps:
                cp.wait()

        _both_cores(run)

    return pl.pallas_call(
        body, name=name,
        out_shape=[jax.ShapeDtypeStruct((halves[a],) + gl[a].shape[1:], gl[a].dtype) for a in range(n)],
        in_specs=[_HBM] * n + [pl.BlockSpec(memory_space=pl.ANY)] * len(extra), out_specs=[_HBM] * n,
        scratch_shapes=[pltpu.SemaphoreType.DMA((n,)), pltpu.SemaphoreType.DMA((n,))],
    )(*gl, *extra)


def _chip_scatter_half(names, parts, name):
    n = len(parts)
    lay = [LAYOUT[nm] for nm in names]
    zeros = [lax.empty((N_CHIPS, lay[a][0]) + parts[a].shape[1:], parts[a].dtype) for a in range(n)]

    def body(*refs):
        ins, outs = refs[:n], refs[2 * n:3 * n]
        send, recv = refs[3 * n:3 * n + 2]
        stage, sem_in, sem_out = refs[3 * n + 2:4 * n + 2], refs[4 * n + 2], refs[4 * n + 3]
        x, y, c = _position()
        chips = [(1 - x, y), (x, 1 - y), (1 - x, 1 - y)]

        def run(j0, c0):
            def sized(a, rows):
                return outs[a].at[0, pl.ds(0, rows)]

            for a in range(n):
                base = c0 * (lay[a][1] // 2)
                for k, chip in enumerate(chips):
                    for s0, d0, nr in _clip_pieces(lay[a], j0 ^ _REL[k], c0):
                        pltpu.make_async_remote_copy(
                            src_ref=ins[a].at[pl.ds(d0 - base, nr)], dst_ref=outs[a].at[j0, pl.ds(s0, nr)],
                            send_sem=send.at[3 * a + k], recv_sem=recv.at[3 * a + k],
                            device_id=(*chip, c), device_id_type=MESH).start()
            for a in range(n):
                base = c0 * (lay[a][1] // 2)
                for s0, d0, nr in _clip_pieces(lay[a], j0, c0):
                    _staged_copy(ins[a].at[pl.ds(d0 - base, nr)], outs[a].at[j0, pl.ds(s0, nr)], stage[a],
                                 sem_in.at[a], sem_out.at[a], nr)
            for a in range(n):
                got = _rows_of(_clip_pieces(lay[a], j0, c0))
                for k in range(3):
                    sent = _rows_of(_clip_pieces(lay[a], j0 ^ _REL[k], c0))
                    if sent:
                        pltpu.make_async_remote_copy(src_ref=sized(a, sent), dst_ref=sized(a, sent),
                                                     send_sem=send.at[3 * a + k], recv_sem=recv.at[3 * a + k],
                                                     device_id=(x, y, c), device_id_type=MESH).wait_send()
                    if got:
                        pltpu.make_async_remote_copy(src_ref=sized(a, got), dst_ref=sized(a, got),
                                                     send_sem=send.at[3 * a + k], recv_sem=recv.at[3 * a + k],
                                                     device_id=(x, y, c), device_id_type=MESH).wait_recv()

        _both_cores(run)

    return pl.pallas_call(
        body, name=name,
        out_shape=[jax.ShapeDtypeStruct(z.shape, z.dtype) for z in zeros],
        in_specs=[_HBM] * (2 * n), out_specs=[_HBM] * n, input_output_aliases={n + a: a for a in range(n)},
        scratch_shapes=[pltpu.SemaphoreType.DMA((3 * n,)), pltpu.SemaphoreType.DMA((3 * n,))]
        + [pltpu.VMEM((min(STAGE_ROWS, p.shape[0]),) + p.shape[1:], p.dtype) for p in parts]
        + [pltpu.SemaphoreType.DMA((n,)), pltpu.SemaphoreType.DMA((n,))],
    )(*parts, *zeros)


def _subset_exchange(names, bufs, l, name):
    n = len(bufs)
    lay = [LAYOUT[nm] for nm in names]

    def body(*refs):
        outs = refs[n:2 * n]
        send, recv = refs[2 * n:]
        x, y, c = _position()

        def run(j0, c0):
            for a in range(n):
                for s0, _, nr in _clip_pieces(lay[a], j0, c0):
                    rows = outs[a].at[l, pl.ds(s0, nr)]
                    pltpu.make_async_remote_copy(src_ref=rows, dst_ref=rows, send_sem=send.at[a], recv_sem=recv.at[a],
                                                 device_id=(x, y, 1 - c), device_id_type=MESH).start()
            for a in range(n):
                for c_half, wait_send in ((c0, True), (1 - c0, False)):
                    rows = _rows_of(_clip_pieces(lay[a], j0, c_half))
                    if rows:
                        ref = outs[a].at[l, pl.ds(0, rows)]
                        cp = pltpu.make_async_remote_copy(src_ref=ref, dst_ref=ref, send_sem=send.at[a], recv_sem=recv.at[a],
                                                          device_id=(x, y, 1 - c), device_id_type=MESH)
                        if wait_send:
                            cp.wait_send()
                        else:
                            cp.wait_recv()

        _both_cores(run)

    return pl.pallas_call(
        body, name=name,
        out_shape=[jax.ShapeDtypeStruct(b.shape, b.dtype) for b in bufs],
        in_specs=[_HBM] * n, out_specs=[_HBM] * n, input_output_aliases={a: a for a in range(n)},
        scratch_shapes=[pltpu.SemaphoreType.DMA((n,)), pltpu.SemaphoreType.DMA((n,))],
    )(*bufs)


def _scatter_start(names, gl, name):
    n = len(gl)
    lay = [LAYOUT[nm] for nm in names]
    zones = [lax.empty((N_CHIPS, lay[a][0]) + gl[a].shape[1:], gl[a].dtype) for a in range(n)]

    def body(*refs):
        ins, lands = refs[:n], refs[n:2 * n]
        send, recv, lsem = refs[2 * n:2 * n + 3]
        refs[-1][...] = jnp.zeros_like(refs[-1])
        x, y, c = _position()
        j = 2 * x + y
        chips = [(1 - x, y), (x, 1 - y), (1 - x, 1 - y)]
        for j0 in range(N_CHIPS):
            @pl.when(j == j0)
            def _(j0=j0):
                for a in range(n):
                    for s0, d0, nr in lay[a][2](j0):
                        pltpu.make_async_copy(ins[a].at[pl.ds(d0, nr)], lands[a].at[j0, pl.ds(s0, nr)], lsem.at[a]).start()
                    for k, chip in enumerate(chips):
                        for s0, d0, nr in lay[a][2](j0 ^ _REL[k]):
                            pltpu.make_async_remote_copy(
                                src_ref=ins[a].at[pl.ds(d0, nr)], dst_ref=lands[a].at[j0, pl.ds(s0, nr)],
                                send_sem=send.at[3 * a + k], recv_sem=recv.at[3 * a + k],
                                device_id=(*chip, c), device_id_type=MESH).start()

    sems = [pltpu.SemaphoreType.DMA((3 * n,)), pltpu.SemaphoreType.DMA((3 * n,)), pltpu.SemaphoreType.DMA((n,))]
    hbm = lambda a: pltpu.HBM(a.shape, a.dtype)
    res = pl.pallas_call(
        body, name=name,
        out_shape=sems + [hbm(g) for g in gl] + [hbm(z) for z in zones] + [_TOKEN],
        in_specs=[_HBM] * (2 * n), out_specs=[_SEM] * 3 + [_HBM] * (2 * n) + [_VMEM_TOKEN],
        input_output_aliases={a: 3 + a for a in range(2 * n)},
        compiler_params=pltpu.CompilerParams(has_side_effects=_EFFECT),
    )(*[pltpu.with_memory_space_constraint(g, pltpu.HBM) for g in gl],
      *[pltpu.with_memory_space_constraint(z, pltpu.HBM) for z in zones])
    return res[:3], res[3:3 + n], res[3 + n:3 + 2 * n], res[-1]


def _scatter_wait(names, sems, srcs, lands, after, name):
    n = len(srcs)
    lay = [LAYOUT[nm] for nm in names]

    def body(*refs):
        zones = refs[n:2 * n]
        send, recv, lsem = refs[2 * n:2 * n + 3]
        x, y, c = _position()
        for a in range(n):
            whole = zones[a].at[0, pl.ds(0, lay[a][0])]
            for k in range(3):
                cp = pltpu.make_async_remote_copy(src_ref=whole, dst_ref=whole, send_sem=send.at[3 * a + k],
                                                  recv_sem=recv.at[3 * a + k], device_id=(x, y, 1 - c),
                                                  device_id_type=MESH)
                cp.wait_send()
                cp.wait_recv()
            pltpu.make_async_copy(whole, whole, lsem.at[a]).wait()

    hbm = lambda a: pltpu.HBM(a.shape, a.dtype)
    res = pl.pallas_call(
        body, name=name,
        out_shape=[hbm(s) for s in srcs] + [hbm(z) for z in lands],
        in_specs=[_HBM] * (2 * n) + [_SEM] * 3 + [pl.BlockSpec(memory_space=pl.ANY)], out_specs=[_HBM] * (2 * n),
        input_output_aliases={a: a for a in range(2 * n)},
        compiler_params=pltpu.CompilerParams(has_side_effects=_EFFECT),
    )(*srcs, *lands, *sems, after)
    return res[n:]


def _peer(x, y, c, k):
    return (1 - x if k & 4 else x, 1 - y if k & 2 else y, 1 - c if k & 1 else c)


def _bcast_start(arrs, name, after=None):
    n = len(arrs)
    zones = [lax.empty((8,) + a.shape, a.dtype) for a in arrs]
    extra = [] if after is None else [after]

    def body(*refs):
        ins, lands = refs[:n], refs[n:2 * n]
        send, recv, lsem = refs[2 * n + len(extra):2 * n + len(extra) + 3]
        refs[-1][...] = jnp.zeros_like(refs[-1])
        x, y, c = _position()
        for a in range(n):
            dst = lands[a].at[4 * x + 2 * y + c]
            pltpu.make_async_copy(ins[a], dst, lsem.at[a]).start()
            for k in range(1, 8):
                pltpu.make_async_remote_copy(src_ref=ins[a], dst_ref=dst, send_sem=send.at[7 * a + k - 1],
                                             recv_sem=recv.at[7 * a + k - 1], device_id=_peer(x, y, c, k),
                                             device_id_type=MESH).start()

    sems = [pltpu.SemaphoreType.DMA((7 * n,)), pltpu.SemaphoreType.DMA((7 * n,)), pltpu.SemaphoreType.DMA((n,))]
    hbm = lambda a: pltpu.HBM(a.shape, a.dtype)
    res = pl.pallas_call(
        body, name=name,
        out_shape=sems + [hbm(a) for a in arrs] + [hbm(z) for z in zones] + [_TOKEN],
        in_specs=[_HBM] * (2 * n) + [pl.BlockSpec(memory_space=pl.ANY)] * len(extra),
        out_specs=[_SEM] * 3 + [_HBM] * (2 * n) + [_VMEM_TOKEN],
        input_output_aliases={a: 3 + a for a in range(2 * n)},
        compiler_params=pltpu.CompilerParams(has_side_effects=_EFFECT),
    )(*[pltpu.with_memory_space_constraint(a, pltpu.HBM) for a in arrs],
      *[pltpu.with_memory_space_constraint(z, pltpu.HBM) for z in zones], *extra)
    return res[:3], res[3:3 + n], res[3 + n:3 + 2 * n], res[-1]


def _bcast_wait(sems, srcs, lands, after, name):
    n = len(srcs)

    def body(*refs):
        ins, zones = refs[:n], refs[n:2 * n]
        send, recv, lsem = refs[2 * n:2 * n + 3]
        x, y, c = _position()
        for a in range(n):
            for k in range(1, 8):
                cp = pltpu.make_async_remote_copy(src_ref=ins[a], dst_ref=zones[a].at[0], send_sem=send.at[7 * a + k - 1],
                                                  recv_sem=recv.at[7 * a + k - 1], device_id=_peer(x, y, c, k),
                                                  device_id_type=MESH)
                cp.wait_send()
                cp.wait_recv()
            pltpu.make_async_copy(ins[a], zones[a].at[0], lsem.at[a]).wait()

    hbm = lambda a: pltpu.HBM(a.shape, a.dtype)
    res = pl.pallas_call(
        body, name=name,
        out_shape=[hbm(s) for s in srcs] + [hbm(z) for z in lands],
        in_specs=[_HBM] * (2 * n) + [_SEM] * 3 + [pl.BlockSpec(memory_space=pl.ANY)], out_specs=[_HBM] * (2 * n),
        input_output_aliases={a: a for a in range(2 * n)},
        compiler_params=pltpu.CompilerParams(has_side_effects=_EFFECT),
    )(*srcs, *lands, *sems, after)
    return res[n:]


def _swap_start(arrs, name):
    n = len(arrs)
    zones = [lax.empty(a.shape, a.dtype) for a in arrs]

    def body(*refs):
        ins, lands = refs[:n], refs[n:2 * n]
        send, recv = refs[2 * n:2 * n + 2]
        refs[-1][...] = jnp.zeros_like(refs[-1])
        x, y, c = _position()
        for a in range(n):
            pltpu.make_async_remote_copy(src_ref=ins[a], dst_ref=lands[a], send_sem=send.at[a], recv_sem=recv.at[a],
                                         device_id=(x, y, 1 - c), device_id_type=MESH).start()

    sems = [pltpu.SemaphoreType.DMA((n,)), pltpu.SemaphoreType.DMA((n,))]
    hbm = lambda a: pltpu.HBM(a.shape, a.dtype)
    res = pl.pallas_call(
        body, name=name,
        out_shape=sems + [hbm(a) for a in arrs] + [hbm(z) for z in zones] + [_TOKEN],
        in_specs=[_HBM] * (2 * n), out_specs=[_SEM] * 2 + [_HBM] * (2 * n) + [_VMEM_TOKEN],
        input_output_aliases={a: 2 + a for a in range(2 * n)},
        compiler_params=pltpu.CompilerParams(has_side_effects=_EFFECT),
    )(*[pltpu.with_memory_space_constraint(a, pltpu.HBM) for a in arrs],
      *[pltpu.with_memory_space_constraint(z, pltpu.HBM) for z in zones])
    return res[:2], res[2:2 + n], res[2 + n:2 + 2 * n], res[-1]


def _swap_wait(sems, srcs, lands, after, name):
    n = len(srcs)

    def body(*refs):
        ins, zones = refs[:n], refs[n:2 * n]
        send, recv = refs[2 * n:2 * n + 2]
        x, y, c = _position()
        for a in range(n):
            cp = pltpu.make_async_remote_copy(src_ref=ins[a], dst_ref=zones[a], send_sem=send.at[a], recv_sem=recv.at[a],
                                              device_id=(x, y, 1 - c), device_id_type=MESH)
            cp.wait_send()
            cp.wait_recv()

    hbm = lambda a: pltpu.HBM(a.shape, a.dtype)
    res = pl.pallas_call(
        body, name=name,
        out_shape=[hbm(s) for s in srcs] + [hbm(z) for z in lands],
        in_specs=[_HBM] * (2 * n) + [_SEM] * 2 + [pl.BlockSpec(memory_space=pl.ANY)], out_specs=[_HBM] * (2 * n),
        input_output_aliases={a: a for a in range(2 * n)},
        compiler_params=pltpu.CompilerParams(has_side_effects=_EFFECT),
    )(*srcs, *lands, *sems, after)
    return res[:n], res[n:]


def _row_tile(r):
    for t in (256, 128, 64, 32, 16, 8):
        if r % t == 0 and r > t:
            return t
    return r


def _pair_add_half(g, rb, c_arr, name):
    hrows, rest = rb.shape[0], rb.shape[1:]
    tr = _row_tile(hrows)
    nb = hrows // tr
    z = (0,) * len(rest)

    def body(c_ref, g_ref, r_ref, o_ref):
        o_ref[...] = (g_ref[...].astype(F32) + r_ref[...].astype(F32)).astype(o_ref.dtype)

    return pl.pallas_call(
        body, name=name,
        grid_spec=pltpu.PrefetchScalarGridSpec(
            num_scalar_prefetch=1, grid=(nb,),
            in_specs=[pl.BlockSpec((tr,) + rest, lambda i, c_ref: (c_ref[0] * nb + i,) + z),
                      pl.BlockSpec((tr,) + rest, lambda i, c_ref: (i,) + z)],
            out_specs=pl.BlockSpec((tr,) + rest, lambda i, c_ref: (i,) + z)),
        out_shape=jax.ShapeDtypeStruct((hrows,) + rest, BF16),
        compiler_params=pltpu.CompilerParams(dimension_semantics=("parallel",), vmem_limit_bytes=VMEM_LIMIT),
    )(c_arr, g, rb)


def _sum_slabs(slabs, l, buf, name):
    m = len(slabs)
    n, R, rest = slabs[0].shape[0], slabs[0].shape[1], slabs[0].shape[2:]
    tr = _row_tile(R)
    z = (0,) * len(rest)

    def body(*refs):
        total = None
        for r_ref in refs[:m]:
            acc = r_ref[0].astype(F32)
            for k in range(1, n):
                acc = acc + r_ref[k].astype(F32)
            total = acc if total is None else total + acc
        refs[-1][...] = total

    if R // tr > 64 and len(rest) == 1 and rest[0] % 256 == 0:
        grid = (rest[0] // 256,)
        in_spec = pl.BlockSpec((n, R, 256), lambda i: (0, 0, i))
        out_spec = pl.BlockSpec((None, R, 256), lambda i: (l, 0, i))
    else:
        grid = (R // tr,)
        in_spec = pl.BlockSpec((n, tr) + rest, lambda i: (0, i) + z)
        out_spec = pl.BlockSpec((None, tr) + rest, lambda i: (l, i) + z)
    in_specs, args, aliases = [in_spec] * m, list(slabs), {}
    if buf is not None:
        in_specs.append(pl.BlockSpec(memory_space=pl.ANY))
        args.append(buf)
        aliases = {m: 0}
    return pl.pallas_call(
        body, name=name, grid=grid, in_specs=in_specs, out_specs=out_spec,
        out_shape=jax.ShapeDtypeStruct((DEPTH, R) + rest, F32), input_output_aliases=aliases,
        compiler_params=pltpu.CompilerParams(
            dimension_semantics=("parallel",),
            vmem_limit_bytes=_vmem(m * _nbytes(in_spec.block_shape, slabs[0].dtype) + _nbytes(out_spec.block_shape, F32),
                                   2 * _nbytes(out_spec.block_shape, F32))),
    )(*args)


def _adam_math(w, g, m, v):
    mn = ADAM_B1 * m + (1.0 - ADAM_B1) * g
    vn = ADAM_B2 * v + (1.0 - ADAM_B2) * (g * g)
    m_hat = mn / (1.0 - ADAM_B1 ** ADAM_STEP)
    v_hat = vn / (1.0 - ADAM_B2 ** ADAM_STEP)
    return -ADAM_LR * (m_hat / (jnp.sqrt(v_hat) + ADAM_EPS) + ADAM_WD * w), mn, vn


def _adamw(w, g, m, v, name, token=None):
    L, R, C = w.shape
    tr = _row_tile(R)
    extra = [] if token is None else [token]

    def body(w_ref, g_ref, m_ref, v_ref, *rest):
        d_ref, mo_ref, vo_ref = rest[-3:]
        d_ref[...], mo_ref[...], vo_ref[...] = _adam_math(w_ref[...], g_ref[...], m_ref[...], v_ref[...])

    if R // tr > 64 and C % 128 == 0:
        spec, grid = pl.BlockSpec((None, R, 128), lambda l, i: (l, 0, i)), (L, C // 128)
    else:
        spec, grid = pl.BlockSpec((None, tr, C), lambda l, i: (l, i, 0)), (L, R // tr)
    return pl.pallas_call(
        body, name=name, grid=grid, in_specs=[spec] * 4 + [pl.BlockSpec(memory_space=pl.ANY)] * len(extra),
        out_specs=[spec] * 3, out_shape=[jax.ShapeDtypeStruct((L, R, C), F32)] * 3,
        compiler_params=pltpu.CompilerParams(dimension_semantics=("parallel", "parallel"),
                                             vmem_limit_bytes=_vmem(7 * _nbytes(spec.block_shape, F32))),
    )(w, g, m, v, *extra)


_VMEM_WHOLE = pl.BlockSpec(memory_space=pltpu.VMEM)


def _matrix_update(gath, w, m, v, name):
    K = w.shape[1]

    def body(g0_ref, g1_ref, w_ref, m_ref, v_ref, go_ref, d_ref, mo_ref, vo_ref):
        for l, gr in enumerate((g0_ref, g1_ref)):
            for k in range(K):
                g = gr[0, k].astype(F32)
                for dev in range(1, 8):
                    g = g + gr[dev, k].astype(F32)
                go_ref[l, k] = g
                d_ref[l, k], mo_ref[l, k], vo_ref[l, k] = _adam_math(w_ref[l, k], g, m_ref[l, k], v_ref[l, k])

    return pl.pallas_call(
        body, name=name, in_specs=[_VMEM_WHOLE] * 5, out_specs=[_VMEM_WHOLE] * 4,
        out_shape=[jax.ShapeDtypeStruct(w.shape, F32)] * 4,
        compiler_params=pltpu.CompilerParams(vmem_limit_bytes=32 * MIB),
    )(gath[0], gath[1], w, m, v)


VECS = (("pre_norm_g", D), ("post_norm_g", D), ("gm_ln_g", GM_W), ("gm_ln_b", GM_W), ("mla_q_norm_g", QR),
        ("mla_kv_norm_g", KVR), ("lru_conv_b", LRU_W), ("lru_b_a", LRU_W), ("lru_b_x", LRU_W), ("lru_lambda", LRU_W))
VEC_KEY = {"pre_norm_g": "pre_g", "post_norm_g": "post_g", "gm_ln_g": "ln_g", "gm_ln_b": "ln_b", "mla_q_norm_g": "qg",
           "mla_kv_norm_g": "kvg", "lru_conv_b": "conv_b", "lru_b_a": "ba", "lru_b_x": "bx", "lru_lambda": "lam"}
VEC_ROWS, VEC_W, VEC_ROW0, LOSS_ROW = 16, LRU_W, GM_G, 14


def _pack_rows(LG, loss_part):
    per = len(VECS) + 1
    ins = []
    for G in LG:
        ins += [G[VEC_KEY[n]] for n, _ in VECS] + [G["bst"]]
    ins.append(loss_part)

    def body(*refs):
        o_ref = refs[-1]
        o_ref[...] = jnp.zeros_like(o_ref)
        for l in range(DEPTH):
            base = VEC_ROWS * l
            o_ref[pl.ds(base, 8), pl.ds(0, GM_B)] = refs[per * l + len(VECS)][...].T[:8, :]
            for t, (_, width) in enumerate(VECS):
                o_ref[pl.ds(base + VEC_ROW0 + t, 1), pl.ds(0, width)] = refs[per * l + t][...]
        o_ref[pl.ds(LOSS_ROW, 1), pl.ds(0, 128)] = jnp.broadcast_to(refs[-2][...], (1, 128))

    return pl.pallas_call(
        body, name="pack_rows", in_specs=[_VMEM_WHOLE] * len(ins), out_specs=_VMEM_WHOLE,
        out_shape=jax.ShapeDtypeStruct((DEPTH * VEC_ROWS, VEC_W), F32),
    )(*ins)


def _vector_update(gath, W, M, V):
    names = [n for n, _ in VECS] + ["gm_bs"]
    nw = len(names)

    def body(*refs):
        g_ref = refs[0]
        wr, mr, vr = refs[1:1 + nw], refs[1 + nw:1 + 2 * nw], refs[1 + 2 * nw:1 + 3 * nw]
        outs = refs[1 + 3 * nw:]
        s = g_ref[0]
        for dev in range(1, 8):
            s = s + g_ref[dev]
        for t, (_, width) in enumerate(VECS):
            for l in range(DEPTH):
                r = VEC_ROWS * l + VEC_ROW0 + t
                g = s[r:r + 1, :width]
                row = (pl.ds(l, 1), slice(None))
                res = (g,) + _adam_math(wr[t][row], g, mr[t][row], vr[t][row])
                for q in range(4):
                    outs[4 * t + q][row] = res[q]
        t = len(VECS)
        for l in range(DEPTH):
            for k in range(GM_G):
                g = s[VEC_ROWS * l + k:VEC_ROWS * l + k + 1, :GM_B]
                row = (l, pl.ds(k, 1), slice(None))
                res = (g,) + _adam_math(wr[t][row], g, mr[t][row], vr[t][row])
                for q in range(4):
                    outs[4 * t + q][row] = res[q]
        outs[4 * nw][...] = s[LOSS_ROW:LOSS_ROW + 1, :128]

    ws = [W[n] for n in names]
    out_shape = []
    for w in ws:
        out_shape += [jax.ShapeDtypeStruct(w.shape, F32)] * 4
    out_shape.append(jax.ShapeDtypeStruct((1, 128), F32))
    res = pl.pallas_call(
        body, name="vector_update", in_specs=[_VMEM_WHOLE] * (1 + 3 * nw), out_specs=[_VMEM_WHOLE] * (4 * nw + 1),
        out_shape=out_shape, compiler_params=pltpu.CompilerParams(vmem_limit_bytes=VMEM_LIMIT),
    )(gath, *ws, *[M[n] for n in names], *[V[n] for n in names])
    return {n: tuple(res[4 * t:4 * t + 4]) for t, n in enumerate(names)}, res[4 * nw]


SHARDED = ("w_in", "mla_w_uq", "mla_w_ukv", "lru_conv_w", "w_proj_a", "w_proj_b", "w_proj_c", "w_out")
FIRST = ("w_in", "lru_conv_w")
LATER = tuple(n for n in SHARDED if n not in FIRST)
COL_SHARDED = ("w_in", "mla_w_uq", "mla_w_ukv", "lru_conv_w")
SMALL = ("pre_norm_g", "gm_ln_g", "gm_ln_b", "gm_ws", "gm_bs", "mla_q_norm_g", "mla_kv_norm_g", "lru_conv_b",
         "lru_w_a", "lru_b_a", "lru_w_x", "lru_b_x", "lru_lambda", "post_norm_g")
WEIGHTS = ("pre_norm_g", "w_in", "gm_ln_g", "gm_ln_b", "gm_ws", "gm_bs", "mla_q_norm_g", "mla_w_uq",
           "mla_kv_norm_g", "mla_w_ukv", "lru_conv_w", "lru_conv_b", "lru_w_a", "lru_b_a", "lru_w_x", "lru_b_x",
           "lru_lambda", "w_proj_a", "w_proj_b", "w_proj_c", "w_out", "post_norm_g")


GB_KEY = {"w_in": "wp", "mla_w_uq": "wuq", "mla_w_ukv": "wukv", "w_proj_a": "wpa", "w_proj_b": "wpb",
          "w_proj_c": "wpc", "w_out": "wout"}


def _prepare(l, gathered, small, wsb):
    P = {GB_KEY[n]: gathered[n] for n in GB_KEY if n in gathered}
    P["conv_w"] = gathered["lru_conv_w"].transpose(1, 0, 2).reshape(CONV_W, LRU_W)
    P["wsb"] = wsb
    row = lambda n: small[n][l][None, :]
    P["pre_g"], P["post_g"] = row("pre_norm_g"), row("post_norm_g")
    P["ln_g"], P["ln_b"] = row("gm_ln_g"), row("gm_ln_b")
    P["ws"] = small["gm_ws"][l]
    P["bst"] = jnp.pad(small["gm_bs"][l].T, ((0, 0), (0, 128 - GM_G)))
    P["qg"], P["kvg"] = row("mla_q_norm_g"), row("mla_kv_norm_g")
    P["conv_b"], P["ba"], P["bx"], P["lam"] = row("lru_conv_b"), row("lru_b_a"), row("lru_b_x"), row("lru_lambda")
    return P


def kernel(x, pre_norm_g, w_in, gm_ln_g, gm_ln_b, gm_ws, gm_bs, mla_q_norm_g, mla_w_uq, mla_kv_norm_g, mla_w_ukv, lru_conv_w, lru_conv_b, lru_w_a, lru_b_a, lru_w_x, lru_b_x, lru_lambda, w_proj_a, w_proj_b, w_proj_c, w_out, post_norm_g, loss_target, m_pre_norm_g, m_w_in, m_gm_ln_g, m_gm_ln_b, m_gm_ws, m_gm_bs, m_mla_q_norm_g, m_mla_w_uq, m_mla_kv_norm_g, m_mla_w_ukv, m_lru_conv_w, m_lru_conv_b, m_lru_w_a, m_lru_b_a, m_lru_w_x, m_lru_b_x, m_lru_lambda, m_w_proj_a, m_w_proj_b, m_w_proj_c, m_w_out, m_post_norm_g, v_pre_norm_g, v_w_in, v_gm_ln_g, v_gm_ln_b, v_gm_ws, v_gm_bs, v_mla_q_norm_g, v_mla_w_uq, v_mla_kv_norm_g, v_mla_w_ukv, v_lru_conv_w, v_lru_conv_b, v_lru_w_a, v_lru_b_a, v_lru_w_x, v_lru_b_x, v_lru_lambda, v_w_proj_a, v_w_proj_b, v_w_proj_c, v_w_out, v_post_norm_g):
    args = dict(locals())
    W = {n: args[n] for n in WEIGHTS}
    M = {n: args["m_" + n] for n in WEIGHTS}
    V = {n: args["v_" + n] for n in WEIGHTS}
    c = lax.axis_index("c")

    def shards(l, names):
        out = []
        for n in names:
            blk = W[n][l].T if n in TRANSPOSED else W[n][l]
            out.append(blk[None] if n == "lru_conv_w" else blk.astype(BF16))
        return out

    small = {n: W[n] for n in SMALL}
    wsb = _superblocks(W["lru_w_a"], W["lru_w_x"])
    tabs = _rope_tables()
    s0a, s0b, s1a, s1b = shards(0, FIRST), shards(0, LATER), shards(1, FIRST), shards(1, LATER)
    g0, zones = _weights_allgather(FIRST, s0a, "weights_allgather_l0", carry=_gather_zeros(LATER, s0b)
                                   + _gather_zeros(FIRST, s1a) + _gather_zeros(LATER, s1b))
    nl, nf = len(LATER), len(FIRST)
    w0b = _gather_start(LATER, s0b, zones[:nl], "weights_gather_start_l0")
    w1a = _gather_start(FIRST, s1a, zones[nl:nl + nf], "weights_gather_start_l1_first", after=w0b[3])
    w1b = _gather_start(LATER, s1b, zones[nl + nf:], "weights_gather_start_l1_later", after=w1a[3])

    def late(started, name):
        def wait(after):
            got = _gather_wait(LATER, *started[:3], after, name)
            return {GB_KEY[n]: g for n, g in zip(LATER, got)}
        return wait

    P = [_prepare(0, dict(zip(FIRST, g0)), small, wsb), None]
    h0 = x[0]
    h1, A0 = _layer_fwd(h0, P[0], 0, tabs, w1b[3], late(w0b, "weights_gather_wait_l0"))
    g1 = _gather_wait(FIRST, *w1a[:3], h1, "weights_gather_wait_l1_first")
    P[1] = _prepare(1, dict(zip(FIRST, g1)), small, wsb)
    h2, A1 = _layer_fwd(h1, P[1], 1, tabs, None, late(w1b, "weights_gather_wait_l1_later"))
    dy, loss_part = _loss_fwd(h2, loss_target[0])

    def large_grads(G, GB, names):
        conv = G["conv_w"].reshape(CONV_W, N_CHIPS, LRU_W // N_CHIPS).transpose(1, 0, 2)
        return [conv if n == "lru_conv_w" else GB[GB_KEY[n]] for n in names]

    started = {}

    def early1(GB):
        started["sc1b"] = _scatter_start(LATER, [GB[GB_KEY[n]] for n in LATER], "grads_scatter_start_l1_later")
        return started["sc1b"][3], None

    d1, G1, GB1 = _layer_bwd(dy, A1, P[1], 1, tabs, None, early1)
    sc1a = _scatter_start(FIRST, large_grads(G1, GB1, FIRST), "grads_scatter_start_l1_first")

    def early0(GB):
        got_b = _scatter_wait(LATER, *started["sc1b"][:3], GB["wukv"], "grads_scatter_wait_l1_later")
        got_a = _scatter_wait(FIRST, *sc1a[:3], got_b[0], "grads_scatter_wait_l1_first")
        got = dict(zip(LATER + FIRST, list(got_b) + list(got_a)))
        started["swap1"] = _swap_start([got[n] for n in SHARDED], "partials_swap_start_l1")
        started["sc0"] = _scatter_start(LATER, [GB[GB_KEY[n]] for n in LATER], "grads_scatter_start_l0")
        return started["sc0"][3], started["swap1"][3]

    d0, G0, GB0 = _layer_bwd(d1, A0, P[0], 0, tabs, sc1a[3], early0)
    LG = (G0, G1)
    mine0 = _scatter_wait(LATER, *started["sc0"][:3], d0, "grads_scatter_wait_l0")
    swap0 = _swap_start(mine0, "partials_swap_start_l0")
    g0f = large_grads(G0, GB0, FIRST)
    c_arr = jnp.reshape(c, (1,)).astype(jnp.int32)
    from_sib = _half_to_sibling(FIRST, g0f, "grads_half_to_sibling_l0", after=swap0[3])
    pair = [_pair_add_half(g, rb, c_arr, "pair_add_" + n) for n, g, rb in zip(FIRST, g0f, from_sib)]
    slabs = _chip_scatter_half(FIRST, pair, "grads_chip_scatter_l0")
    mats = []
    for g in LG:
        mats += [g["ws"].astype(BF16), g["wab"][0, :, :, :LRU_BW], g["wab"][1, :, :, :LRU_BW]]
    bc = _bcast_start([_pack_rows(LG, loss_part)] + mats, "small_grads_start", after=slabs[0])
    mine1, theirs1 = _swap_wait(*started["swap1"][:3], bc[3], "partials_swap_wait_l1")
    both = dict(zip(SHARDED, [_sum_slabs([a, b], 1, None, "sum_partials_l1_" + n)
                              for n, a, b in zip(SHARDED, mine1, theirs1)]))
    for n, s in zip(FIRST, slabs):
        both[n] = _sum_slabs([s], 0, both[n], "sum_slabs_l0_" + n)
    done = _subset_exchange(FIRST, [both[n] for n in FIRST], 0, "reduced_rows_to_sibling_l0")
    both.update(zip(FIRST, done))
    mine0, theirs0 = _swap_wait(*swap0[:3], done[0], "partials_swap_wait_l0")
    for n, a, b in zip(LATER, mine0, theirs0):
        both[n] = _sum_slabs([a, b], 0, both[n], "sum_partials_l0_" + n)
    both = [both[n] for n in SHARDED]
    grads = {}
    for n, b in zip(SHARDED, both):
        if n in TRANSPOSED and n != "w_in":
            b = jnp.swapaxes(b, 1, 2)
        grads[n] = b if n == "w_in" else b.reshape(W[n].shape)

    upd, last = {}, None
    for n in SHARDED:
        token = bc[3] if n == SHARDED[0] else None
        if n == "w_in":
            tr = lambda a: jnp.swapaxes(a, 1, 2)
            res = _adamw(tr(W[n]), grads[n], tr(M[n]), tr(V[n]), "adamw_" + n, token)
            upd[n] = tuple(tr(a) for a in (grads[n],) + tuple(res))
        else:
            res = _adamw(W[n], grads[n], M[n], V[n], "adamw_" + n, token)
            upd[n] = (grads[n],) + tuple(res)
        last = res[0]

    gath = _bcast_wait(*bc[:3], last, "small_grads_wait")
    vec_upd, loss_row = _vector_update(gath[0], W, M, V)
    upd.update(vec_upd)
    loss = loss_row[0, 0]
    for k, n in enumerate(("gm_ws", "lru_w_a", "lru_w_x")):
        upd[n] = _matrix_update((gath[1 + k], gath[4 + k]), W[n], M[n], V[n], "update_" + n)

    return (loss, d0[None], *[upd[n][0] for n in WEIGHTS], *[upd[n][1] for n in WEIGHTS],
            *[upd[n][2] for n in WEIGHTS], *[upd[n][3] for n in WEIGHTS])
```

```python
import functools
import math

import jax
import jax.numpy as jnp
from jax import lax
from jax.experimental import pallas as pl
from jax.experimental.pallas import tpu as pltpu

F32, BF16 = jnp.float32, jnp.bfloat16
MESH = pl.DeviceIdType.MESH

S, D, DEPTH = 2048, 1024, 2
CHUNK, EPS = 64, 1e-6
GM_W, GM_G, GM_B = 1024, 4, 128
H, NOPE, ROPE, VDIM = 8, 128, 64, 128
QR, KVR = 384, 256
MLA_W = H * VDIM
LRU_W, LRU_NB, LRU_BW, LRU_C, CONV_W = 1280, 16, 80, 8.0, 4
ROPE_THETA = 10000.0
IN_SIZES = (GM_W, GM_W, GM_W, QR, KVR, ROPE, MLA_W, LRU_W, LRU_W, D, D, D)
N_IN = sum(IN_SIZES)
N_CHIPS = 4
ADAM_LR, ADAM_B1, ADAM_B2, ADAM_EPS, ADAM_WD, ADAM_STEP = 0.001, 0.9, 0.999, 1e-08, 0.01, 10

HP = 256
O_U, O_V, O_ZA, O_GA, O_GB, O_GC = 0, 1024, 2048, 3072, 4096, 5120
O_CKV, O_KR, O_CQ, O_XC, O_ZC, O_ZB = 6144, 6400, 6528, 7680, 8960, 10240
NP = 11264
MIB = 1024 * 1024
VMEM_LIMIT = 16 * MIB


def _vmem(block_bytes, temp_bytes=0):
    return int(min(max(2 * block_bytes + temp_bytes + 4 * MIB, VMEM_LIMIT), 56 * MIB))


def _nbytes(shape, dtype):
    return math.prod(d for d in shape if d is not None) * jnp.dtype(dtype).itemsize


def _tile(dim, target):
    if dim <= target:
        return dim
    t = (target // 128) * 128
    while dim % t:
        t -= 128
    return t


def _sig(x):
    return jax.nn.sigmoid(x)


def _silu(x):
    return x * _sig(x)


def _dsilu(x):
    s = _sig(x)
    return s * (1.0 + x * (1.0 - s))


def _mm(a, b, mode, name, out_dtype=F32, tm=1024, tn=1024, tk=1024, b_lead=None, out_lead=None, token=None):
    b2 = b.shape[1:] if b_lead is not None else b.shape
    if mode == "nn":
        (M, K), (K2, N) = a.shape, b2
    elif mode == "nt":
        (M, K), (N, K2) = a.shape, b2
    else:
        (K, M), (K2, N) = a.shape, b2
    assert K == K2, (name, a.shape, b.shape)
    tm, tn, tk = _tile(M, tm), _tile(N, tn), _tile(K, tk)
    nk = K // tk
    if mode == "tn":
        a_spec = pl.BlockSpec((tk, tm), lambda i, j, k: (k, i))
        lhs_c = 0
    else:
        a_spec = pl.BlockSpec((tm, tk), lambda i, j, k: (i, k))
        lhs_c = 1
    b_blk, b_idx, rhs_c = ((tn, tk), (lambda i, j, k: (j, k)), 1) if mode == "nt" else ((tk, tn), (lambda i, j, k: (k, j)), 0)
    if b_lead is None:
        b_spec = pl.BlockSpec(b_blk, b_idx)
    else:
        b_spec = pl.BlockSpec((None,) + b_blk, functools.partial(lambda i, j, k, f, l: (l,) + f(i, j, k), f=b_idx, l=b_lead))
    dims = (((lhs_c,), (rhs_c,)), ((), ()))
    in_specs, args, aliases = [a_spec, b_spec], [a, b], {}
    if out_lead is None:
        out_spec = pl.BlockSpec((tm, tn), lambda i, j, k: (i, j))
        out_shape = jax.ShapeDtypeStruct((M, N), out_dtype)
    else:
        l_out, n_lead, buf = out_lead
        out_spec = pl.BlockSpec((None, tm, tn), functools.partial(lambda i, j, k, l: (l, i, j), l=l_out))
        out_shape = jax.ShapeDtypeStruct((n_lead, M, N), out_dtype)
        if buf is not None:
            in_specs.append(pl.BlockSpec(memory_space=pl.ANY))
            args.append(buf)
            aliases = {2: 0}
    if token is not None:
        in_specs.append(pl.BlockSpec(memory_space=pl.ANY))
        args.append(token)

    def body(a_ref, b_ref, *rest):
        o_ref, acc_ref = rest[-2:]
        k = pl.program_id(2)

        @pl.when(k == 0)
        def _():
            acc_ref[...] = jnp.zeros_like(acc_ref)

        acc_ref[...] += lax.dot_general(a_ref[...].astype(BF16), b_ref[...].astype(BF16), dims,
                                        preferred_element_type=F32)

        @pl.when(k == nk - 1)
        def _():
            o_ref[...] = acc_ref[...].astype(o_ref.dtype)

    return pl.pallas_call(
        body, name=name, grid=(M // tm, N // tn, nk),
        in_specs=in_specs, out_specs=out_spec, out_shape=out_shape,
        scratch_shapes=[pltpu.VMEM((tm, tn), F32)], input_output_aliases=aliases,
        compiler_params=pltpu.CompilerParams(
            dimension_semantics=("parallel", "parallel", "arbitrary"),
            vmem_limit_bytes=_vmem(_nbytes((tm, tk), a.dtype) + _nbytes((tk, tn), b.dtype) + _nbytes((tm, tn), out_dtype),
                                   _nbytes((tm, tn), F32) + _nbytes((tm, tk), BF16) + _nbytes((tk, tn), BF16))),
    )(*args)


def _rows(fn, name, tm, rows, halos=(), fulls=(), outs=(), accs=()):
    n = S // tm
    in_specs, args = [], []
    for arr, w, cb in rows:
        in_specs.append(pl.BlockSpec((tm, w), functools.partial(lambda i, cb: (i, cb), cb=cb)))
        args.append(arr)
    for arr, w, cb, side in halos:
        if side == "prev":
            im = functools.partial(lambda i, cb: (jnp.maximum(i * (tm // 16) - 1, 0), cb), cb=cb)
        else:
            im = functools.partial(lambda i, cb: (jnp.minimum((i + 1) * (tm // 16), S // 16 - 1), cb), cb=cb)
        in_specs.append(pl.BlockSpec((16, w), im))
        args.append(arr)
    for arr in fulls:
        in_specs.append(pl.BlockSpec(arr.shape, functools.partial(lambda i, nd: (0,) * nd, nd=arr.ndim)))
        args.append(arr)
    out_shape, out_specs, aliases, n_alias = [], [], {}, 0
    for k, o in enumerate(outs):
        if len(o) == 3 and o[2] == "T":
            out_shape.append(jax.ShapeDtypeStruct((o[0], S), o[1]))
            out_specs.append(pl.BlockSpec((o[0], tm), lambda i: (0, i)))
        elif len(o) == 3:
            buf, total, cb = o[2]
            out_shape.append(jax.ShapeDtypeStruct((S, total), o[1]))
            out_specs.append(pl.BlockSpec((tm, o[0]), functools.partial(lambda i, cb: (i, cb), cb=cb)))
            if buf is not None:
                aliases[len(args)] = k
                in_specs.append(pl.BlockSpec(memory_space=pl.ANY))
                args.append(buf)
                n_alias += 1
        else:
            out_shape.append(jax.ShapeDtypeStruct((S, o[0]), o[1]))
            out_specs.append(pl.BlockSpec((tm, o[0]), lambda i: (i, 0)))
    for shp in accs:
        out_shape.append(jax.ShapeDtypeStruct(shp, F32))
        out_specs.append(pl.BlockSpec(shp, functools.partial(lambda i, nd: (0,) * nd, nd=len(shp))))
    nr, nh, nf, no, na = len(rows), len(halos), len(fulls), len(outs), len(accs)
    blocks = (sum(_nbytes((tm, w), arr.dtype) for arr, w, _ in rows) + sum(_nbytes(a.shape, a.dtype) for a in fulls)
              + sum(_nbytes((tm, o[0]), o[1]) for o in outs) + sum(_nbytes(shp, F32) for shp in accs))
    widest = _nbytes((tm, max([w for _, w, _ in rows] + [o[0] for o in outs])), F32)

    def body(*refs):
        i = pl.program_id(0)
        ins, orefs = refs[:nr + nh + nf], refs[nr + nh + nf + n_alias:]
        rv = [r[...].astype(F32) for r in ins[:nr]]
        hv = [r[...].astype(F32)[8:] if h[3] == "prev" else r[...].astype(F32)[:8] for r, h in zip(ins[nr:nr + nh], halos)]
        fv = [r[...] for r in ins[nr + nh:]]
        o, a = fn(i, rv, hv, fv)
        assert len(o) == no and len(a) == na, name
        for spec, ref, val in zip(outs, orefs[:no], o):
            ref[...] = (val.T if len(spec) == 3 and spec[2] == "T" else val).astype(ref.dtype)
        if na:
            @pl.when(i == 0)
            def _():
                for ref in orefs[no:]:
                    ref[...] = jnp.zeros_like(ref)

            for ref, val in zip(orefs[no:], a):
                ref[...] += val

    res = pl.pallas_call(
        body, name=name, grid=(n,), in_specs=in_specs, out_specs=out_specs, out_shape=out_shape,
        input_output_aliases=aliases,
        compiler_params=pltpu.CompilerParams(dimension_semantics=("arbitrary",), vmem_limit_bytes=_vmem(blocks, 6 * widest)),
    )(*args)
    return res


def _shift_down(xb, halo, s, row):
    fix = jnp.tile(pltpu.roll(halo, s, 0), (xb.shape[0] // 8, 1))
    return jnp.where(row >= s, pltpu.roll(xb, s, 0), fix)


def _shift_up(xb, halo, s, row):
    tm = xb.shape[0]
    fix = jnp.tile(pltpu.roll(halo, 8 - s, 0), (tm // 8, 1))
    return jnp.where(row < tm - s, pltpu.roll(xb, tm - s, 0), fix)


def _rms(x):
    return lax.rsqrt(jnp.mean(x * x, axis=-1, keepdims=True) + EPS)


def _rms_bwd(dy, x, g):
    r = _rms(x)
    xh = x * r
    dxh = dy * g
    dx = r * (dxh - xh * jnp.mean(dxh * xh, axis=-1, keepdims=True))
    return dx, dy * xh


def _colsum(x):
    return jnp.sum(x, axis=0, keepdims=True)


def _prenorm_fwd(x, g, token=None):
    def fn(i, rv, hv, fv):
        return [rv[0] * _rms(rv[0]) * fv[0]], []
    return _rows(fn, "prenorm_fwd", 256, [(x, D, 0)], fulls=[g] + ([] if token is None else [token]), outs=[(D, BF16)])[0]


def _gm_mask():
    r = lax.broadcasted_iota(jnp.int32, (GM_B, GM_B), 0) // CHUNK
    c = lax.broadcasted_iota(jnp.int32, (GM_B, GM_B), 1) // CHUNK
    return c <= r


def _gm_norm(v, g, b):
    mu = jnp.mean(v, axis=-1, keepdims=True)
    vc = v - mu
    rs = lax.rsqrt(jnp.mean(vc * vc, axis=-1, keepdims=True) + EPS)
    vh = vc * rs
    return vh, rs, vh * g + b


def _gm_sv(vn, ws, bst):
    mask = _gm_mask()
    gw = GM_W // GM_G
    parts = []
    for g in range(GM_G):
        wm = jnp.where(mask, ws[g], 0.0).astype(BF16)
        parts.append(jnp.dot(wm, vn[:, g * gw:(g + 1) * gw].astype(BF16), preferred_element_type=F32)
                     + bst[:, g:g + 1])
    return jnp.concatenate(parts, axis=1)


def _gmlp_fwd(proj, ln_g, ln_b, ws, bst):
    def fn(i, rv, hv, fv):
        u, v, z = rv
        g, b, w, bt = fv
        _, _, vn = _gm_norm(v, g, b)
        return [u * _gm_sv(vn, w, bt) * _silu(z)], []
    return _rows(fn, "gmlp_fwd", GM_B, [(proj, GM_W, 0), (proj, GM_W, 1), (proj, GM_W, 2)],
                 fulls=[ln_g, ln_b, ws, bst], outs=[(GM_W, BF16)])[0]


def _mla_prep_fwd(proj, qg, kvg):
    def fn(i, rv, hv, fv):
        cq, ckv = rv
        g1, g2 = fv
        return [cq * _rms(cq) * g1, ckv * _rms(ckv) * g2], []
    return _rows(fn, "mla_prep_fwd", 256, [(proj, QR, O_CQ // QR), (proj, KVR, O_CKV // KVR)],
                 fulls=[qg, kvg], outs=[(QR, BF16), (KVR, BF16)])


def _rot(t, cc, sa, sb):
    return t * cc + pltpu.roll(t, 32, 1) * sa + pltpu.roll(t, 96, 1) * sb


def _rot_t(g, cc, sa, sb):
    return g * cc + pltpu.roll(g * sa, 96, 1) + pltpu.roll(g * sb, 32, 1)


def _rope_tables():
    pos = jnp.arange(S, dtype=F32)
    inv_freq = ROPE_THETA ** (-jnp.arange(0, ROPE, 2, dtype=F32) / ROPE)
    ang = pos[:, None] * inv_freq[None, :]
    cos, sin, z = jnp.cos(ang), jnp.sin(ang), jnp.zeros((S, 32), F32)
    cc = jnp.concatenate([cos, cos, z, z], axis=1)
    sa = jnp.concatenate([z, sin, z, z], axis=1)
    sb = jnp.concatenate([-sin, z, z, z], axis=1)
    return cc, sa, sb


ATT_SCALE = 1.0 / math.sqrt(NOPE + ROPE)


def _rope_fwd(q, kv, proj, tabs):
    def fn(i, rv, hv, fv):
        qb, kvb, kr, cc, sa, sb = rv
        krr = _rot(kr, cc, sa, sb)
        qs, ks = [], []
        for h in range(H):
            qs += [qb[:, h * HP:h * HP + 128] * ATT_SCALE, _rot(qb[:, h * HP + 128:(h + 1) * HP], cc, sa, sb) * ATT_SCALE]
            ks += [kvb[:, h * 128:(h + 1) * 128], krr]
        kc = jnp.concatenate(ks, axis=1)
        vv = kvb[:, H * NOPE:]
        return [jnp.concatenate(qs, axis=1), kc, kc, vv, vv], []
    cc, sa, sb = tabs
    return _rows(fn, "rope_fwd", 256,
                 [(q, H * HP, 0), (kv, H * 256, 0), (proj, 128, O_KR // 128), (cc, 128, 0), (sa, 128, 0), (sb, 128, 0)],
                 outs=[(H * HP, BF16), (H * HP, BF16), (H * HP, BF16, "T"), (MLA_W, BF16), (MLA_W, BF16, "T")])


TQ, TC, ATT_NB = 512, 512, 1
ATT_KB = TC * ATT_NB
_NT = (((1,), (1,)), ((), ()))


def _attn_allowed(i, kc):
    kpos = kc * TC + lax.broadcasted_iota(jnp.int32, (TC, TQ), 0)
    qpos = i * TQ + lax.broadcasted_iota(jnp.int32, (TC, TQ), 1)
    return (kpos // CHUNK) <= (qpos // CHUNK)


def _attn_fwd(qc, kc, vt):
    def body(q_ref, k_ref, vt_ref, o_ref, l_ref):
        i = pl.program_id(1)
        q = q_ref[...]

        def scores(sb):
            t0s = [pl.multiple_of((sb * ATT_NB + c) * TC, TC) for c in range(ATT_NB)]
            return [lax.dot_general(k_ref[pl.ds(t0, TC), :], q, _NT, preferred_element_type=F32) for t0 in t0s]

        def block(sb, ss, carry, masked):
            m, l, acc = carry
            t0s = [pl.multiple_of((sb * ATT_NB + c) * TC, TC) for c in range(ATT_NB)]
            if masked:
                ss = [jnp.where(_attn_allowed(i, sb * ATT_NB + c), s, -1e30) for c, s in enumerate(ss)]
            m_new = m
            for s in ss:
                m_new = jnp.maximum(m_new, jnp.max(s, axis=0, keepdims=True))
            alpha = jnp.exp(m - m_new)
            ps = [jnp.exp(s - m_new) for s in ss]
            l = alpha * l
            acc = alpha * acc
            for t0, p in zip(t0s, ps):
                l = l + jnp.sum(p, axis=0, keepdims=True)
                acc = acc + jnp.dot(vt_ref[:, pl.ds(t0, TC)], p.astype(BF16), preferred_element_type=F32)
            return m_new, l, acc

        nsb = ((i + 1) * TQ + ATT_KB - 1) // ATT_KB
        c = (jnp.full((1, TQ), -1e30, F32), jnp.zeros((1, TQ), F32), jnp.zeros((VDIM, TQ), F32))

        def step(sb, sc):
            nxt = scores(sb + 1)
            return nxt, block(sb, sc[0], sc[1], False)

        ss, c = lax.fori_loop(0, nsb - 1, step, (scores(0), c))
        m, l, acc = block(nsb - 1, ss, c, True)
        o_ref[...] = (acc / l).T.astype(o_ref.dtype)
        l_ref[...] = m + jnp.log(l)

    return pl.pallas_call(
        body, name="attn_fwd", grid=(H, S // TQ),
        in_specs=[pl.BlockSpec((TQ, HP), lambda h, i: (i, h)),
                  pl.BlockSpec((S, HP), lambda h, i: (0, h)),
                  pl.BlockSpec((VDIM, S), lambda h, i: (h, 0))],
        out_specs=[pl.BlockSpec((TQ, VDIM), lambda h, i: (i, h)), pl.BlockSpec((None, 1, TQ), lambda h, i: (h, 0, i))],
        out_shape=[jax.ShapeDtypeStruct((S, MLA_W), BF16), jax.ShapeDtypeStruct((H, 1, S), F32)],
        compiler_params=pltpu.CompilerParams(dimension_semantics=("parallel", "arbitrary"),
                                             vmem_limit_bytes=24 * MIB),
    )(qc, kc, vt)


def _gate_mul_fwd(name, val, proj, width, cb):
    def fn(i, rv, hv, fv):
        o, z = rv
        return [o * _silu(z)], []
    return _rows(fn, name, 256, [(val, width, 0), (proj, width, cb)], outs=[(width, BF16)])[0]


def _conv_fwd(proj, w, b):
    def fn(i, rv, hv, fv):
        (xb,), (halo,), (ww, bb) = rv, hv, fv
        halo = jnp.where(i > 0, halo, 0.0)
        row = lax.broadcasted_iota(jnp.int32, xb.shape, 0)
        acc = bb + ww[3:4] * xb
        for s in range(1, CONV_W):
            acc = acc + ww[3 - s:4 - s] * _shift_down(xb, halo, s, row)
        return [acc], []
    return _rows(fn, "conv_fwd", LRU_TM, [(proj, LRU_W, O_XC // LRU_W)], halos=[(proj, LRU_W, O_XC // LRU_W, "prev")],
                 fulls=[w, b], outs=[(LRU_W, BF16)])[0]


def _lru_terms(ga, gx, xc, ba, bx, lam):
    r = _sig(ga + ba)
    ig = _sig(gx + bx)
    sp = jnp.maximum(-lam, 0.0) + jnp.log(1.0 + jnp.exp(-jnp.abs(lam)))
    log_a = -LRU_C * r * sp
    a = jnp.exp(log_a)
    e2 = jnp.exp(2.0 * log_a)
    om = 1.0 - e2
    mult = jnp.sqrt(jnp.maximum(om, 0.0))
    return r, ig, sp, a, e2, om, mult


def _lru_gates_fwd(gates, xc, ba, bx, lam):
    def fn(i, rv, hv, fv):
        ga, gx, x = rv
        r, ig, sp, a, e2, om, mult = _lru_terms(ga, gx, x, *fv)
        return [a, mult * (ig * x)], []
    return _rows(fn, "lru_gates_fwd", LRU_TM, [(gates, LRU_W, 0), (gates, LRU_W, 1), (xc, LRU_W, 0)],
                 fulls=[ba, bx, lam], outs=[(LRU_W, F32), (LRU_W, F32)])


SCAN_T, SCAN_CW = 64, 256
LRU_TM = 256


def _scan_fwd(a, b):
    def body(a_ref, b_ref, h_ref):
        row = lax.broadcasted_iota(jnp.int32, (SCAN_T, SCAN_CW), 0)

        def step(blk, hc):
            t0 = pl.multiple_of(blk * SCAN_T, SCAN_T)
            A = a_ref[pl.ds(t0, SCAN_T), :]
            B = b_ref[pl.ds(t0, SCAN_T), :]
            d = 1
            while d < SCAN_T:
                keep = row >= d
                A_s = jnp.where(keep, pltpu.roll(A, d, 0), 1.0)
                B_s = jnp.where(keep, pltpu.roll(B, d, 0), 0.0)
                B = A * B_s + B
                A = A * A_s
                d *= 2
            hh = A * hc + B
            h_ref[pl.ds(t0, SCAN_T), :] = hh
            return hh[SCAN_T - 1:SCAN_T, :]

        lax.fori_loop(0, S // SCAN_T, step, jnp.zeros((1, SCAN_CW), F32))

    spec = pl.BlockSpec((S, SCAN_CW), lambda j: (0, j))
    return pl.pallas_call(
        body, name="scan_fwd", grid=(LRU_W // SCAN_CW,), in_specs=[spec, spec], out_specs=spec,
        out_shape=jax.ShapeDtypeStruct((S, LRU_W), F32),
        compiler_params=pltpu.CompilerParams(dimension_semantics=("parallel",),
                                             vmem_limit_bytes=_vmem(3 * _nbytes((S, SCAN_CW), F32))),
    )(a, b)


def _merge_fwd(pa, pb, pc, proj):
    def fn(i, rv, hv, fv):
        a, b, c, ga, gb, gc = rv
        return [_sig(ga) * a + _sig(gb) * b + _sig(gc) * c], []
    return _rows(fn, "merge_fwd", 256,
                 [(pa, D, 0), (pb, D, 0), (pc, D, 0), (proj, D, O_GA // D), (proj, D, O_GB // D), (proj, D, O_GC // D)],
                 outs=[(D, BF16)])[0]


def _post_fwd(x, o2, g):
    def fn(i, rv, hv, fv):
        xb, ob = rv
        return [xb + ob * _rms(ob) * fv[0]], []
    return _rows(fn, "post_fwd", 256, [(x, D, 0), (o2, D, 0)], fulls=[g], outs=[(D, F32)])[0]


SB = 640
BD_TM = 512


def _bd_fwd(xcb, wsb, l):
    def body(x_ref, w_ref, o_ref):
        o_ref[...] = jnp.dot(x_ref[...], w_ref[...], preferred_element_type=F32).astype(o_ref.dtype)

    return pl.pallas_call(
        body, name="lru_gate_mm", grid=(S // BD_TM, 4),
        in_specs=[pl.BlockSpec((BD_TM, SB), lambda i, q: (i, q % 2)),
                  pl.BlockSpec((None, None, SB, SB), lambda i, q: (l, q, 0, 0))],
        out_specs=pl.BlockSpec((BD_TM, SB), lambda i, q: (i, q)),
        out_shape=jax.ShapeDtypeStruct((S, 2 * LRU_W), BF16),
        compiler_params=pltpu.CompilerParams(dimension_semantics=("parallel", "parallel"), vmem_limit_bytes=VMEM_LIMIT),
    )(xcb, wsb)


def _bd_dx(dgates, wsb, l):
    def body(d_ref, w_ref, o_ref, acc_ref):
        g = pl.program_id(2)

        @pl.when(g == 0)
        def _():
            acc_ref[...] = jnp.zeros_like(acc_ref)

        acc_ref[...] += lax.dot_general(d_ref[...], w_ref[...], (((1,), (1,)), ((), ())), preferred_element_type=F32)

        @pl.when(g == 1)
        def _():
            o_ref[...] = acc_ref[...].astype(o_ref.dtype)

    return pl.pallas_call(
        body, name="lru_gate_dx", grid=(S // BD_TM, 2, 2),
        in_specs=[pl.BlockSpec((BD_TM, SB), lambda i, s, g: (i, 2 * g + s)),
                  pl.BlockSpec((None, None, SB, SB), lambda i, s, g: (l, 2 * g + s, 0, 0))],
        out_specs=pl.BlockSpec((BD_TM, SB), lambda i, s, g: (i, s)),
        out_shape=jax.ShapeDtypeStruct((S, LRU_W), BF16),
        scratch_shapes=[pltpu.VMEM((BD_TM, SB), F32)],
        compiler_params=pltpu.CompilerParams(dimension_semantics=("parallel", "parallel", "arbitrary"),
                                             vmem_limit_bytes=VMEM_LIMIT),
    )(dgates, wsb)


def _bd_dw(xcb, dgates):
    tk = 1024

    def body(x_ref, d_ref, o_ref):
        @pl.when(pl.program_id(1) == 0)
        def _():
            o_ref[...] = jnp.zeros_like(o_ref)

        o_ref[...] += lax.dot_general(x_ref[...], d_ref[...], (((0,), (0,)), ((), ())), preferred_element_type=F32)

    return pl.pallas_call(
        body, name="lru_gate_dw", grid=(4, S // tk),
        in_specs=[pl.BlockSpec((tk, SB), lambda q, k: (k, q % 2)), pl.BlockSpec((tk, SB), lambda q, k: (k, q))],
        out_specs=pl.BlockSpec((None, SB, SB), lambda q, k: (q, 0, 0)),
        out_shape=jax.ShapeDtypeStruct((4, SB, SB), F32),
        compiler_params=pltpu.CompilerParams(dimension_semantics=("parallel", "arbitrary"), vmem_limit_bytes=VMEM_LIMIT),
    )(xcb, dgates)


def _bd_extract(dwsb):
    def body(w_ref, o_ref):
        lane = lax.broadcasted_iota(jnp.int32, (LRU_BW, 128), 1)
        for q in range(4):
            for kk in range(8):
                c0 = LRU_BW * kk
                w0, off = (c0 // 128) * 128, c0 % 128
                rows = pl.ds(LRU_BW * kk, LRU_BW)
                blk = w_ref[q, rows, w0:w0 + 128]
                if off:
                    blk = pltpu.roll(blk, 128 - off, 1)
                    if off + LRU_BW > 128:
                        nxt = pltpu.roll(w_ref[q, rows, w0 + 128:w0 + 256], 128 - off, 1)
                        blk = jnp.where(lane < 128 - off, blk, nxt)
                o_ref[q // 2, 8 * (q % 2) + kk] = blk.astype(BF16)

    return pl.pallas_call(
        body, name="lru_gate_dw_blocks",
        in_specs=[pl.BlockSpec(memory_space=pltpu.VMEM)], out_specs=pl.BlockSpec(memory_space=pltpu.VMEM),
        out_shape=jax.ShapeDtypeStruct((2, LRU_NB, LRU_BW, 128), BF16),
        compiler_params=pltpu.CompilerParams(vmem_limit_bytes=VMEM_LIMIT),
    )(dwsb)


def _layer_fwd(x, P, l, tabs, token=None, late=None):
    A = {"x": x}
    A["h"] = _prenorm_fwd(x, P["pre_g"], token)
    proj = A["proj"] = _mm(A["h"], P["wp"], "nt", "in_proj", out_dtype=BF16, tm=1024)
    A["ya"] = _gmlp_fwd(proj, P["ln_g"], P["ln_b"], P["ws"], P["bst"])
    A["xcb"] = _conv_fwd(proj, P["conv_w"], P["conv_b"])
    A["gates"] = _bd_fwd(A["xcb"], P["wsb"], l)
    A["a"], bterm = _lru_gates_fwd(A["gates"], A["xcb"], P["ba"], P["bx"], P["lam"])
    A["hs"] = _scan_fwd(A["a"], bterm)
    A["yc"] = _gate_mul_fwd("yc_fwd", A["hs"], proj, LRU_W, O_ZC // LRU_W)
    if late is not None:
        P.update(late(A["yc"]))
    A["cqn"], A["ckvn"] = _mla_prep_fwd(proj, P["qg"], P["kvg"])
    q = _mm(A["cqn"], P["wuq"], "nt", "q_up", out_dtype=BF16)
    kv = _mm(A["ckvn"], P["wukv"], "nt", "kv_up", out_dtype=BF16)
    A["qc"], A["kc"], A["kct"], A["vv"], vt = _rope_fwd(q, kv, proj, tabs)
    A["o"], A["lse"] = _attn_fwd(A["qc"], A["kc"], vt)
    A["yb"] = _gate_mul_fwd("yb_fwd", A["o"], proj, MLA_W, O_ZB // MLA_W)
    A["pa"] = _mm(A["ya"], P["wpa"], "nn", "proj_a", out_dtype=BF16)
    A["pb"] = _mm(A["yb"], P["wpb"], "nn", "proj_b", out_dtype=BF16)
    A["pc"] = _mm(A["yc"], P["wpc"], "nn", "proj_c", out_dtype=BF16)
    A["merged"] = _merge_fwd(A["pa"], A["pb"], A["pc"], proj)
    A["o2"] = _mm(A["merged"], P["wout"], "nn", "out_proj", out_dtype=BF16)
    return _post_fwd(x, A["o2"], P["post_g"]), A


def _loss_fwd(y, tgt):
    def fn(i, rv, hv, fv):
        yb, tb = rv
        e = yb - tb
        part = 0.5 * jnp.sum(jnp.mean(e * e, axis=-1, keepdims=True), axis=0, keepdims=True)
        return [e * (1.0 / D)], [part]
    return _rows(fn, "loss", 256, [(y, D, 0), (tgt, D, 0)], outs=[(D, F32)], accs=[(1, 1)])


def _post_bwd(dxn, o2, g, token=None):
    def fn(i, rv, hv, fv):
        dy, ob = rv
        dx, dg = _rms_bwd(dy, ob, fv[0])
        return [dx], [_colsum(dg)]
    return _rows(fn, "post_bwd", 256, [(dxn, D, 0), (o2, D, 0)], fulls=[g] + ([] if token is None else [token]),
                 outs=[(D, BF16)], accs=[(1, D)])


def _merge_bwd(dm, pa, pb, pc, proj, dproj):
    def fn(i, rv, hv, fv):
        d, a, b, c, ga, gb, gc = rv
        outs_p, outs_g = [], []
        for p, gg in ((a, ga), (b, gb), (c, gc)):
            s = _sig(gg)
            outs_p.append(d * s)
            outs_g.append(d * p * s * (1.0 - s))
        return outs_p + [jnp.concatenate(outs_g, axis=1)], []
    return _rows(fn, "merge_bwd", 256,
                 [(dm, D, 0), (pa, D, 0), (pb, D, 0), (pc, D, 0),
                  (proj, D, O_GA // D), (proj, D, O_GB // D), (proj, D, O_GC // D)],
                 outs=[(D, BF16)] * 3 + [(3 * D, BF16, (dproj, NP, O_GA // (3 * D)))])


def _gmlp_bwd(dya, proj, ln_g, ln_b, ws, bst, dproj):
    gw = GM_W // GM_G

    def fn(i, rv, hv, fv):
        dy, u, v, z = rv
        g, b, w, bt = fv
        vh, rs, vn = _gm_norm(v, g, b)
        sv = _gm_sv(vn, w, bt)
        sz = _silu(z)
        du = dy * sv * sz
        dsv = dy * u * sz
        dz = dy * u * sv * _dsilu(z)
        mask = _gm_mask()
        lane = lax.broadcasted_iota(jnp.int32, (GM_B, 128), 1)
        dvn_parts, dws, dbst = [], [], jnp.zeros((GM_B, 128), F32)
        for k in range(GM_G):
            wm = jnp.where(mask, w[k], 0.0).astype(BF16)
            dsk = dsv[:, k * gw:(k + 1) * gw]
            dskb = dsk.astype(BF16)
            dvn_parts.append(lax.dot_general(wm, dskb, (((0,), (0,)), ((), ())), preferred_element_type=F32))
            dwk = lax.dot_general(dskb, vn[:, k * gw:(k + 1) * gw].astype(BF16), (((1,), (1,)), ((), ())),
                                  preferred_element_type=F32)
            dws.append(jnp.where(mask, dwk, 0.0)[None])
            dbst = dbst + jnp.where(lane == k, jnp.sum(dsk, axis=1, keepdims=True), 0.0)
        dvn = jnp.concatenate(dvn_parts, axis=1)
        dvh = dvn * g
        dv = rs * (dvh - jnp.mean(dvh, axis=-1, keepdims=True) - vh * jnp.mean(dvh * vh, axis=-1, keepdims=True))
        return ([jnp.concatenate([du, dv, dz], axis=1)],
                [jnp.concatenate(dws, axis=0), dbst, _colsum(dvn * vh), _colsum(dvn)])
    return _rows(fn, "gmlp_bwd", GM_B, [(dya, GM_W, 0), (proj, GM_W, 0), (proj, GM_W, 1), (proj, GM_W, 2)],
                 fulls=[ln_g, ln_b, ws, bst], outs=[(3 * GM_W, BF16, (dproj, NP, O_U // (3 * GM_W)))],
                 accs=[(GM_G, GM_B, GM_B), (GM_B, 128), (1, GM_W), (1, GM_W)])


def _yb_bwd(dyb, o, proj, dproj):
    def fn(i, rv, hv, fv):
        dy, ob, z = rv
        do = dy * _silu(z)
        prod = do * ob
        lane = lax.broadcasted_iota(jnp.int32, (dy.shape[0], 128), 1)
        dl = jnp.zeros((dy.shape[0], 128), F32)
        for h in range(H):
            dl = dl + jnp.where(lane == h, jnp.sum(prod[:, h * VDIM:(h + 1) * VDIM], axis=1, keepdims=True), 0.0)
        return [do, dl, dy * ob * _dsilu(z)], []
    return _rows(fn, "yb_bwd", 256, [(dyb, MLA_W, 0), (o, MLA_W, 0), (proj, MLA_W, O_ZB // MLA_W)],
                 outs=[(MLA_W, BF16), (128, F32, "T"), (MLA_W, BF16, (dproj, NP, O_ZB // MLA_W))])


def _attn_bwd(qc, kc, kct, vv, do, lse, dlt):
    def body(q_ref, k_ref, kt_ref, v_ref, do_ref, l_ref, d_ref, dq_ref, dk_ref, dv_ref, dqt_ref):
        h, i = pl.program_id(0), pl.program_id(1)

        @pl.when(i == 0)
        def _():
            dk_ref[...] = jnp.zeros_like(dk_ref)
            dv_ref[...] = jnp.zeros_like(dv_ref)

        q = q_ref[...]
        dob = do_ref[...]
        lse = l_ref[...]
        dl = d_ref[pl.ds(h, 1), :]
        dqt_ref[...] = jnp.zeros_like(dqt_ref)

        def rows_of(sb, c):
            return pl.ds(pl.multiple_of((sb * ATT_NB + c) * TC, TC), TC)

        def front(sb):
            return [(lax.dot_general(k_ref[rows_of(sb, c), :], q, _NT, preferred_element_type=F32),
                     lax.dot_general(v_ref[rows_of(sb, c), :], dob, _NT, preferred_element_type=F32))
                    for c in range(ATT_NB)]

        def block(sb, sd, masked):
            dqt = None
            for c, (s, dp) in enumerate(sd):
                rows = rows_of(sb, c)
                p = jnp.exp(s - lse)
                if masked:
                    p = jnp.where(_attn_allowed(i, sb * ATT_NB + c), p, 0.0)
                ds = (p * (dp - dl)).astype(BF16)
                dk_ref[rows, :] += jnp.dot(ds, q, preferred_element_type=F32)
                dv_ref[rows, :] += jnp.dot(p.astype(BF16), dob, preferred_element_type=F32)
                part = jnp.dot(kt_ref[:, rows], ds, preferred_element_type=F32)
                dqt = part if dqt is None else dqt + part
            dqt_ref[...] += dqt

        def step(sb, sd):
            nxt = front(sb + 1)
            block(sb, sd, False)
            return nxt

        nsb = ((i + 1) * TQ + ATT_KB - 1) // ATT_KB
        sd = lax.fori_loop(0, nsb - 1, step, front(0))
        block(nsb - 1, sd, True)
        dq_ref[...] = dqt_ref[...].T.astype(dq_ref.dtype)

    blk = lambda w: pl.BlockSpec((TQ, w), lambda h, i: (i, h))
    head = lambda w: pl.BlockSpec((S, w), lambda h, i: (0, h))
    return pl.pallas_call(
        body, name="attn_bwd", grid=(H, S // TQ),
        in_specs=[blk(HP), head(HP), pl.BlockSpec((HP, S), lambda h, i: (h, 0)), head(VDIM), blk(VDIM),
                  pl.BlockSpec((None, 1, TQ), lambda h, i: (h, 0, i)), pl.BlockSpec((8, TQ), lambda h, i: (0, i))],
        out_specs=[blk(HP), head(HP), head(VDIM)],
        out_shape=[jax.ShapeDtypeStruct((S, H * HP), BF16), jax.ShapeDtypeStruct((S, H * HP), F32),
                   jax.ShapeDtypeStruct((S, MLA_W), F32)],
        scratch_shapes=[pltpu.VMEM((HP, TQ), F32)],
        compiler_params=pltpu.CompilerParams(dimension_semantics=("parallel", "arbitrary"),
                                             vmem_limit_bytes=28 * MIB),
    )(qc, kc, kct, vv, do, lse, dlt)


def _rope_bwd(dqc, dkc, dvv, tabs):
    def fn(i, rv, hv, fv):
        dq, dk, dv, cc, sa, sb = rv
        qs, ks = [], []
        dkr = jnp.zeros((dq.shape[0], 128), F32)
        for h in range(H):
            qs += [dq[:, h * HP:h * HP + 128] * ATT_SCALE, _rot_t(dq[:, h * HP + 128:(h + 1) * HP], cc, sa, sb) * ATT_SCALE]
            ks.append(dk[:, h * HP:h * HP + 128])
            dkr = dkr + dk[:, h * HP + 128:(h + 1) * HP]
        return [jnp.concatenate(qs, axis=1), jnp.concatenate(ks + [dv], axis=1), _rot_t(dkr, cc, sa, sb)], []
    cc, sa, sb = tabs
    return _rows(fn, "rope_bwd", 256,
                 [(dqc, H * HP, 0), (dkc, H * HP, 0), (dvv, MLA_W, 0), (cc, 128, 0), (sa, 128, 0), (sb, 128, 0)],
                 outs=[(H * HP, BF16), (H * 256, BF16), (128, BF16)])


MLA_GROUP = 1536


def _mla_prep_bwd(dcqn, dckvn, dkr, proj, qg, kvg, dproj):
    def fn(i, rv, hv, fv):
        d1, d2, dk, cq, ckv = rv
        g1, g2 = fv
        dx1, dg1 = _rms_bwd(d1, cq, g1)
        dx2, dg2 = _rms_bwd(d2, ckv, g2)
        zeros = jnp.zeros((d1.shape[0], MLA_GROUP - KVR - 128 - QR), F32)
        return [jnp.concatenate([dx2, dk.astype(F32), dx1, zeros], axis=1)], [_colsum(dg1), _colsum(dg2)]
    return _rows(fn, "mla_prep_bwd", 256,
                 [(dcqn, QR, 0), (dckvn, KVR, 0), (dkr, 128, 0), (proj, QR, O_CQ // QR), (proj, KVR, O_CKV // KVR)],
                 fulls=[qg, kvg], outs=[(MLA_GROUP, BF16, (dproj, NP, O_CKV // MLA_GROUP))], accs=[(1, QR), (1, KVR)])


def _yc_bwd(dyc, hs, proj, dproj):
    def fn(i, rv, hv, fv):
        dy, hh, z = rv
        return [dy * _silu(z), dy * hh * _dsilu(z)], []
    return _rows(fn, "yc_bwd", LRU_TM, [(dyc, LRU_W, 0), (hs, LRU_W, 0), (proj, LRU_W, O_ZC // LRU_W)],
                 outs=[(LRU_W, BF16), (LRU_W, BF16, (dproj, NP, O_ZC // LRU_W))])


def _scan_bwd(a, hs, dh):
    nblk = S // SCAN_T

    def body(a_ref, h_ref, dh_ref, da_ref, db_ref):
        row = lax.broadcasted_iota(jnp.int32, (SCAN_T, SCAN_CW), 0)

        def step(j, carry):
            gc, ac = carry
            blk = nblk - 1 - j
            t0 = pl.multiple_of(blk * SCAN_T, SCAN_T)
            av = a_ref[pl.ds(t0, SCAN_T), :]
            A = jnp.where(row < SCAN_T - 1, pltpu.roll(av, SCAN_T - 1, 0), ac)
            B = dh_ref[pl.ds(t0, SCAN_T), :].astype(F32)
            d = 1
            while d < SCAN_T:
                keep = row < SCAN_T - d
                A_s = jnp.where(keep, pltpu.roll(A, SCAN_T - d, 0), 1.0)
                B_s = jnp.where(keep, pltpu.roll(B, SCAN_T - d, 0), 0.0)
                B = A * B_s + B
                A = A * A_s
                d *= 2
            g = A * gc + B
            p0 = pl.multiple_of(jnp.maximum(t0 - 8, 0), 8)
            last = jnp.where(blk > 0, h_ref[pl.ds(p0, 8), :][7:8, :], 0.0)
            h_prev = jnp.where(row >= 1, pltpu.roll(h_ref[pl.ds(t0, SCAN_T), :], 1, 0), last)
            da_ref[pl.ds(t0, SCAN_T), :] = (g * h_prev).astype(da_ref.dtype)
            db_ref[pl.ds(t0, SCAN_T), :] = g.astype(db_ref.dtype)
            return g[0:1, :], av[0:1, :]

        z = jnp.zeros((1, SCAN_CW), F32)
        lax.fori_loop(0, nblk, step, (z, z))

    spec = pl.BlockSpec((S, SCAN_CW), lambda j: (0, j))
    return pl.pallas_call(
        body, name="scan_bwd", grid=(LRU_W // SCAN_CW,), in_specs=[spec] * 3, out_specs=[spec] * 2,
        out_shape=[jax.ShapeDtypeStruct((S, LRU_W), BF16)] * 2,
        compiler_params=pltpu.CompilerParams(dimension_semantics=("parallel",),
                                             vmem_limit_bytes=_vmem(5 * _nbytes((S, SCAN_CW), F32))),
    )(a, hs, dh)


def _lru_gates_bwd(da, db, gates, xc, ba, bx, lam):
    def fn(i, rv, hv, fv):
        dav, dbv, ga, gx, x = rv
        bav, bxv, lamv = fv
        r, ig, sp, a, e2, om, mult = _lru_terms(ga, gx, x, bav, bxv, lamv)
        dmult = dbv * ig * x
        dig = dbv * mult * x
        dxc1 = dbv * mult * ig
        dlog_a = dav * a + jnp.where(om > 0.0, dmult * (-e2 / mult), 0.0)
        dr = dlog_a * (-LRU_C * sp)
        dga = dr * r * (1.0 - r)
        dgx = dig * ig * (1.0 - ig)
        dlam = _colsum(dlog_a * (-LRU_C * r)) * (-_sig(-lamv))
        return [jnp.concatenate([dga, dgx], axis=1), dxc1], [_colsum(dga), _colsum(dgx), dlam]
    return _rows(fn, "lru_gates_bwd", LRU_TM,
                 [(da, LRU_W, 0), (db, LRU_W, 0), (gates, LRU_W, 0), (gates, LRU_W, 1), (xc, LRU_W, 0)],
                 fulls=[ba, bx, lam], outs=[(2 * LRU_W, BF16), (LRU_W, BF16)], accs=[(1, LRU_W)] * 3)


def _conv_bwd(dxc1, dxc2, proj, w, dproj):
    cb = O_XC // LRU_W

    def fn(i, rv, hv, fv):
        d1, d2, xb = rv
        n1, n2, xprev = hv
        ww = fv[0]
        last = i == S // LRU_TM - 1
        dxc = d1 + d2
        nxt = jnp.where(last, 0.0, n1 + n2)
        xprev = jnp.where(i > 0, xprev, 0.0)
        row = lax.broadcasted_iota(jnp.int32, xb.shape, 0)
        dx = ww[3:4] * dxc
        dws = [None] * CONV_W
        dws[3] = _colsum(dxc * xb)
        for s in range(1, CONV_W):
            dx = dx + ww[3 - s:4 - s] * _shift_up(dxc, nxt, s, row)
            dws[3 - s] = _colsum(dxc * _shift_down(xb, xprev, s, row))
        return [dx], [jnp.concatenate(dws, axis=0), _colsum(dxc)]
    return _rows(fn, "conv_bwd", LRU_TM, [(dxc1, LRU_W, 0), (dxc2, LRU_W, 0), (proj, LRU_W, cb)],
                 halos=[(dxc1, LRU_W, 0, "next"), (dxc2, LRU_W, 0, "next"), (proj, LRU_W, cb, "prev")],
                 fulls=[w], outs=[(LRU_W, BF16, (dproj, NP, cb))], accs=[(CONV_W, LRU_W), (1, LRU_W)])


def _prenorm_bwd(dxn, dh, x, g):
    def fn(i, rv, hv, fv):
        dy, dhh, xb = rv
        dx, dg = _rms_bwd(dhh, xb, fv[0])
        return [dy + dx], [_colsum(dg)]
    return _rows(fn, "prenorm_bwd", 256, [(dxn, D, 0), (dh, D, 0), (x, D, 0)], fulls=[g], outs=[(D, F32)],
                 accs=[(1, D)])


def _layer_bwd(dxn, A, P, l, tabs, token=None, early=None):
    G, GB = {}, {}
    proj = A["proj"]

    def dw(key, a, b, name, **tiles):
        GB[key] = _mm(a, b, "tn", name, out_dtype=BF16, **tiles)

    do2, G["post_g"] = _post_bwd(dxn, A["o2"], P["post_g"], token)
    dm = _mm(do2, P["wout"], "nt", "out_proj_dx", out_dtype=BF16)
    dw("wout", A["merged"], do2, "out_proj_dw")
    dpa, dpb, dpc, dproj = _merge_bwd(dm, A["pa"], A["pb"], A["pc"], proj, None)
    dya = _mm(dpa, P["wpa"], "nt", "proj_a_dx", out_dtype=BF16)
    dw("wpa", A["ya"], dpa, "proj_a_dw")
    dyb = _mm(dpb, P["wpb"], "nt", "proj_b_dx", out_dtype=BF16)
    dw("wpb", A["yb"], dpb, "proj_b_dw")
    dyc = _mm(dpc, P["wpc"], "nt", "proj_c_dx", out_dtype=BF16)
    dw("wpc", A["yc"], dpc, "proj_c_dw")
    dproj, G["ws"], G["bst"], G["ln_g"], G["ln_b"] = _gmlp_bwd(dya, proj, P["ln_g"], P["ln_b"], P["ws"], P["bst"], dproj)
    do, dl, dproj = _yb_bwd(dyb, A["o"], proj, dproj)
    dqc, dkc, dvv = _attn_bwd(A["qc"], A["kc"], A["kct"], A["vv"], do, A["lse"], dl)
    dq, dkv, dkr = _rope_bwd(dqc, dkc, dvv, tabs)
    dcqn = _mm(dq, P["wuq"], "nn", "q_up_dx", out_dtype=BF16)
    dw("wuq", dq, A["cqn"], "q_up_dw")
    dckvn = _mm(dkv, P["wukv"], "nn", "kv_up_dx", out_dtype=BF16)
    dw("wukv", dkv, A["ckvn"], "kv_up_dw")
    dproj, G["qg"], G["kvg"] = _mla_prep_bwd(dcqn, dckvn, dkr, proj, P["qg"], P["kvg"], dproj)
    dhs, dproj = _yc_bwd(dyc, A["hs"], proj, dproj)
    da, db = _scan_bwd(A["a"], A["hs"], dhs)
    dgates, dxc1, G["ba"], G["bx"], G["lam"] = _lru_gates_bwd(da, db, A["gates"], A["xcb"], P["ba"], P["bx"], P["lam"])
    dxc2 = _bd_dx(dgates, P["wsb"], l)
    G["wab"] = _bd_extract(_bd_dw(A["xcb"], dgates))
    dproj, G["conv_w"], G["conv_b"] = _conv_bwd(dxc1, dxc2, proj, P["conv_w"], dproj)
    tok = (None, None) if early is None else early(GB)
    dh = _mm(dproj, P["wp"], "nn", "in_proj_dx", out_dtype=BF16, tm=1024, tn=1024, token=tok[0])
    dw("wp", dproj, A["h"], "in_proj_dw", tm=1536, tn=1024, token=tok[1])
    dx, G["pre_g"] = _prenorm_bwd(dxn, dh, A["x"], P["pre_g"])
    return dx, G, GB


_ORIG_OFF = [0]
for _s in IN_SIZES:
    _ORIG_OFF.append(_ORIG_OFF[-1] + _s)
_PAD_OFF = {0: O_U, 1: O_V, 2: O_ZA, 3: O_CQ, 4: O_CKV, 5: O_KR, 6: O_ZB, 7: O_XC, 8: O_ZC, 9: O_GA, 10: O_GB, 11: O_GC}
SHARD_IN = N_IN // N_CHIPS


def _pieces_w_in(j):
    lo, hi = SHARD_IN * j, SHARD_IN * (j + 1)
    out = []
    for k in range(len(IN_SIZES)):
        a, b = max(lo, _ORIG_OFF[k]), min(hi, _ORIG_OFF[k + 1])
        if a < b:
            out.append((a - lo, _PAD_OFF[k] + a - _ORIG_OFF[k], b - a))
    return out


def _pieces_uq(j):
    return [(192 * hh, HP * (2 * j + hh), NOPE + ROPE) for hh in range(2)]


def _pieces_ukv(j):
    out = []
    for hh in range(2):
        h = 2 * j + hh
        out += [(256 * hh, NOPE * h, NOPE), (256 * hh + NOPE, H * NOPE + VDIM * h, VDIM)]
    return out


def _pieces_rows(r):
    return lambda j: [(0, r * j, r)]


LAYOUT = {
    "w_in": (SHARD_IN, NP, _pieces_w_in),
    "mla_w_uq": (2 * (NOPE + ROPE), H * HP, _pieces_uq),
    "mla_w_ukv": (2 * (NOPE + VDIM), 2 * H * 128, _pieces_ukv),
    "lru_conv_w": (1, N_CHIPS, _pieces_rows(1)),
    "w_proj_a": (GM_W // N_CHIPS, GM_W, _pieces_rows(GM_W // N_CHIPS)),
    "w_proj_b": (MLA_W // N_CHIPS, MLA_W, _pieces_rows(MLA_W // N_CHIPS)),
    "w_proj_c": (LRU_W // N_CHIPS, LRU_W, _pieces_rows(LRU_W // N_CHIPS)),
    "w_out": (D // N_CHIPS, D, _pieces_rows(D // N_CHIPS)),
}
TRANSPOSED = ("w_in", "mla_w_uq", "mla_w_ukv")


def _superblocks(w_a, w_x):
    w6 = jnp.stack([w_a, w_x], axis=1).reshape(DEPTH, 4, 8, LRU_BW, LRU_BW).astype(BF16)
    bands = [jnp.pad(w6[:, :, k], ((0, 0), (0, 0), (0, 0), (LRU_BW * k, SB - LRU_BW * (k + 1)))) for k in range(8)]
    return jnp.concatenate(bands, axis=2)


_HBM = pl.BlockSpec(memory_space=pltpu.HBM)


def _position():
    return lax.axis_index("x"), lax.axis_index("y"), lax.axis_index("c")


_REL = (2, 1, 3)


def _cut(r):
    return r if r < 32 else (r // 2 + 15) // 16 * 16


def _half_rows(r, c0):
    return _cut(r) if c0 == 0 else r - _cut(r)


def _half_pieces(lay_a, jsrc, c0):
    r = lay_a[0]
    lo, hi = (0, _cut(r)) if c0 == 0 else (_cut(r), r)
    out = []
    for s0, d0, nr in lay_a[2](jsrc):
        a, b = max(s0, lo), min(s0 + nr, hi)
        if a < b:
            out.append((a, d0 + a - s0, b - a))
    return out


PAD_BLOCKS = {"w_in": (64, [(O_KR + ROPE) // 64] + list(range((O_CQ + QR) // 64, O_XC // 64))),
              "mla_w_uq": (64, [(HP * h + NOPE + ROPE) // 64 for h in range(H)])}


def _zero_blocks(buf, rows, blocks, name):
    rest = buf.shape[1:]
    z = (0,) * len(rest)

    def body(ids_ref, buf_ref, o_ref):
        o_ref[...] = jnp.zeros_like(o_ref)

    return pl.pallas_call(
        body, name=name,
        grid_spec=pltpu.PrefetchScalarGridSpec(
            num_scalar_prefetch=1, grid=(len(blocks),), in_specs=[pl.BlockSpec(memory_space=pl.ANY)],
            out_specs=pl.BlockSpec((rows,) + rest, lambda i, ids: (ids[i],) + z)),
        out_shape=jax.ShapeDtypeStruct(buf.shape, buf.dtype), input_output_aliases={1: 0},
    )(jnp.asarray(blocks, jnp.int32), buf)


def _gather_zeros(names, srcs):
    out = []
    for nm, s in zip(names, srcs):
        zone = lax.empty((LAYOUT[nm][1],) + s.shape[1:], s.dtype)
        out.append(_zero_blocks(zone, *PAD_BLOCKS[nm], "zero_pad_" + nm) if nm in PAD_BLOCKS else zone)
    return out


def _weights_allgather(names, srcs, name, carry=()):
    n = len(srcs)
    lay = [LAYOUT[nm] for nm in names]
    zeros = _gather_zeros(names, srcs)
    m = len(carry)

    def body(*refs):
        ins, outs = refs[:n], refs[2 * n + m:3 * n + m]
        send, recv, lsem = refs[3 * n + 2 * m:]
        x, y, c = _position()
        j = 2 * x + y
        sib = (x, y, 1 - c)
        chips = [(1 - x, y), (x, 1 - y), (1 - x, 1 - y)]

        def flow(a, k, jsrc, c0, to, from_src):
            cps = []
            for s0, d0, nr in _half_pieces(lay[a], jsrc, c0):
                dst = outs[a].at[pl.ds(d0, nr)]
                src = ins[a].at[pl.ds(s0, nr)] if from_src else dst
                cps.append(pltpu.make_async_remote_copy(src_ref=src, dst_ref=dst, send_sem=send.at[7 * a + k],
                                                        recv_sem=recv.at[7 * a + k], device_id=to, device_id_type=MESH))
            return cps

        def sized(a, k, rows):
            ref = ins[a].at[pl.ds(0, rows)]
            return pltpu.make_async_remote_copy(src_ref=ref, dst_ref=ref, send_sem=send.at[7 * a + k],
                                                recv_sem=recv.at[7 * a + k], device_id=sib, device_id_type=MESH)

        for j0 in range(N_CHIPS):
            for c0 in range(2):
                @pl.when((j == j0) & (c == c0))
                def _(j0=j0, c0=c0):
                    mine = [_half_rows(lay[a][0], c0) for a in range(n)]
                    theirs = [_half_rows(lay[a][0], 1 - c0) for a in range(n)]
                    for a in range(n):
                        for s0, d0, nr in _half_pieces(lay[a], j0, c0):
                            pltpu.make_async_copy(ins[a].at[pl.ds(s0, nr)], outs[a].at[pl.ds(d0, nr)], lsem.at[a]).start()
                    for a in range(n):
                        for cp in flow(a, 0, j0, c0, sib, True):
                            cp.start()
                        for k, chip in enumerate(chips):
                            for cp in flow(a, 1 + k, j0, c0, (*chip, c), True):
                                cp.start()
                    for k in range(3):
                        for a in range(n):
                            if mine[a]:
                                sized(a, 1 + k, mine[a]).wait_recv()
                                for cp in flow(a, 4 + k, j0 ^ _REL[k], c0, sib, False):
                                    cp.start()
                    for a in range(n):
                        if theirs[a]:
                            sized(a, 0, theirs[a]).wait_recv()
                            for k in range(3):
                                sized(a, 4 + k, theirs[a]).wait_recv()
                    for a in range(n):
                        if mine[a]:
                            for k in range(7):
                                sized(a, k, mine[a]).wait_send()
                            ref = ins[a].at[pl.ds(0, mine[a])]
                            pltpu.make_async_copy(ref, ref, lsem.at[a]).wait()

    res = pl.pallas_call(
        body, name=name,
        out_shape=[jax.ShapeDtypeStruct(z.shape, z.dtype) for z in list(zeros) + list(carry)],
        in_specs=[_HBM] * (2 * n + m), out_specs=[_HBM] * (n + m),
        input_output_aliases={n + a: a for a in range(n + m)},
        scratch_shapes=[pltpu.SemaphoreType.DMA((7 * n,)), pltpu.SemaphoreType.DMA((7 * n,)),
                        pltpu.SemaphoreType.DMA((n,))],
    )(*srcs, *zeros, *carry)
    return res[:n], res[n:]


_SEM = pl.BlockSpec(memory_space=pltpu.SEMAPHORE)
_VMEM_TOKEN = pl.BlockSpec(memory_space=pltpu.VMEM)
_TOKEN = jax.ShapeDtypeStruct((8, 128), F32)
_EFFECT = pltpu.SideEffectType.DATAFLOW_SIDE_EFFECTING


def _gather_start(names, srcs, zeros, name, after=None):
    n = len(srcs)
    lay = [LAYOUT[nm] for nm in names]
    extra = [] if after is None else [after]

    def body(*refs):
        ins, lands = refs[:n], refs[n:2 * n]
        send, recv, lsem = refs[2 * n + len(extra):2 * n + len(extra) + 3]
        refs[-1][...] = jnp.zeros_like(refs[-1])
        x, y, c = _position()
        j = 2 * x + y
        chips = [(1 - x, y), (x, 1 - y), (1 - x, 1 - y)]
        for j0 in range(N_CHIPS):
            @pl.when(j == j0)
            def _(j0=j0):
                for a in range(n):
                    for s0, d0, nr in lay[a][2](j0):
                        src, dst = ins[a].at[pl.ds(s0, nr)], lands[a].at[pl.ds(d0, nr)]
                        pltpu.make_async_copy(src, dst, lsem.at[a]).start()
                        for k, chip in enumerate(chips):
                            pltpu.make_async_remote_copy(src_ref=src, dst_ref=dst, send_sem=send.at[3 * a + k],
                                                         recv_sem=recv.at[3 * a + k], device_id=(*chip, c),
                                                         device_id_type=MESH).start()

    sems = [pltpu.SemaphoreType.DMA((3 * n,)), pltpu.SemaphoreType.DMA((3 * n,)), pltpu.SemaphoreType.DMA((n,))]
    hbm = lambda a: pltpu.HBM(a.shape, a.dtype)
    res = pl.pallas_call(
        body, name=name,
        out_shape=sems + [hbm(s) for s in srcs] + [hbm(z) for z in zeros] + [_TOKEN],
        in_specs=[_HBM] * (2 * n) + [pl.BlockSpec(memory_space=pl.ANY)] * len(extra),
        out_specs=[_SEM] * 3 + [_HBM] * (2 * n) + [_VMEM_TOKEN],
        input_output_aliases={a: 3 + a for a in range(2 * n)},
        compiler_params=pltpu.CompilerParams(has_side_effects=_EFFECT),
    )(*[pltpu.with_memory_space_constraint(s, pltpu.HBM) for s in srcs],
      *[pltpu.with_memory_space_constraint(z, pltpu.HBM) for z in zeros], *extra)
    return res[:3], res[3:3 + n], res[3 + n:3 + 2 * n], res[-1]


def _gather_wait(names, sems, srcs, lands, after, name):
    n = len(srcs)
    lay = [LAYOUT[nm] for nm in names]

    def body(*refs):
        ins, zones = refs[:n], refs[n:2 * n]
        send, recv, lsem = refs[2 * n:2 * n + 3]
        x, y, c = _position()
        for a in range(n):
            whole = zones[a].at[pl.ds(0, lay[a][0])]
            for k in range(3):
                cp = pltpu.make_async_remote_copy(src_ref=ins[a], dst_ref=whole, send_sem=send.at[3 * a + k],
                                                  recv_sem=recv.at[3 * a + k], device_id=(x, y, 1 - c),
                                                  device_id_type=MESH)
                cp.wait_send()
                cp.wait_recv()
            pltpu.make_async_copy(ins[a], whole, lsem.at[a]).wait()

    hbm = lambda a: pltpu.HBM(a.shape, a.dtype)
    res = pl.pallas_call(
        body, name=name,
        out_shape=[hbm(s) for s in srcs] + [hbm(z) for z in lands],
        in_specs=[_HBM] * (2 * n) + [_SEM] * 3 + [pl.BlockSpec(memory_space=pl.ANY)], out_specs=[_HBM] * (2 * n),
        input_output_aliases={a: a for a in range(2 * n)},
        compiler_params=pltpu.CompilerParams(has_side_effects=_EFFECT),
    )(*srcs, *lands, *sems, after)
    return res[n:]


def _clip_pieces(lay_a, jsrc, c0):
    h = lay_a[1] // 2
    lo, hi = c0 * h, (c0 + 1) * h
    out = []
    for s0, d0, nr in lay_a[2](jsrc):
        a, b = max(d0, lo), min(d0 + nr, hi)
        if a < b:
            out.append((s0 + a - d0, a, b - a))
    return out


def _rows_of(pieces):
    return sum(nr for _, _, nr in pieces)


def _both_cores(body_for):
    x, y, c = _position()
    j = 2 * x + y
    for j0 in range(N_CHIPS):
        for c0 in range(2):
            @pl.when((j == j0) & (c == c0))
            def _(j0=j0, c0=c0):
                body_for(j0, c0)


STAGE_ROWS = 512


def _staged_copy(src, dst, buf, sem_in, sem_out, rows):
    ch = buf.shape[0]
    for r in range(0, rows, ch):
        nr = min(ch, rows - r)
        stage = buf.at[pl.ds(0, nr)]
        cin = pltpu.make_async_copy(src.at[pl.ds(r, nr)], stage, sem_in)
        cin.start()
        cin.wait()
        cout = pltpu.make_async_copy(stage, dst.at[pl.ds(r, nr)], sem_out)
        cout.start()
        cout.wait()


def _half_to_sibling(names, gl, name, after=None):
    n = len(gl)
    halves = [LAYOUT[nm][1] // 2 for nm in names]
    extra = [] if after is None else [after]

    def body(*refs):
        ins, outs = refs[:n], refs[n + len(extra):2 * n + len(extra)]
        send, recv = refs[2 * n + len(extra):]
        x, y, c = _position()

        def run(j0, c0):
            cps = [pltpu.make_async_remote_copy(src_ref=ins[a].at[pl.ds((1 - c0) * halves[a], halves[a])], dst_ref=outs[a],
                                                send_sem=send.at[a], recv_sem=recv.at[a], device_id=(x, y, 1 - c),
                                                device_id_type=MESH) for a in range(n)]
            for cp in cps:
                cp.start()
            for cp in cps:
                cp.wait()

        _both_cores(run)

    return pl.pallas_call(
        body, name=name,
        out_shape=[jax.ShapeDtypeStruct((halves[a],) + gl[a].shape[1:], gl[a].dtype) for a in range(n)],
        in_specs=[_HBM] * n + [pl.BlockSpec(memory_space=pl.ANY)] * len(extra), out_specs=[_HBM] * n,
        scratch_shapes=[pltpu.SemaphoreType.DMA((n,)), pltpu.SemaphoreType.DMA((n,))],
    )(*gl, *extra)


def _chip_scatter_half(names, parts, name):
    n = len(parts)
    lay = [LAYOUT[nm] for nm in names]
    zeros = [lax.empty((N_CHIPS, lay[a][0]) + parts[a].shape[1:], parts[a].dtype) for a in range(n)]

    def body(*refs):
        ins, outs = refs[:n], refs[2 * n:3 * n]
        send, recv = refs[3 * n:3 * n + 2]
        stage, sem_in, sem_out = refs[3 * n + 2:4 * n + 2], refs[4 * n + 2], refs[4 * n + 3]
        x, y, c = _position()
        chips = [(1 - x, y), (x, 1 - y), (1 - x, 1 - y)]

        def run(j0, c0):
            def sized(a, rows):
                return outs[a].at[0, pl.ds(0, rows)]

            for a in range(n):
                base = c0 * (lay[a][1] // 2)
                for k, chip in enumerate(chips):
                    for s0, d0, nr in _clip_pieces(lay[a], j0 ^ _REL[k], c0):
                        pltpu.make_async_remote_copy(
                            src_ref=ins[a].at[pl.ds(d0 - base, nr)], dst_ref=outs[a].at[j0, pl.ds(s0, nr)],
                            send_sem=send.at[3 * a + k], recv_sem=recv.at[3 * a + k],
                            device_id=(*chip, c), device_id_type=MESH).start()
            for a in range(n):
                base = c0 * (lay[a][1] // 2)
                for s0, d0, nr in _clip_pieces(lay[a], j0, c0):
                    _staged_copy(ins[a].at[pl.ds(d0 - base, nr)], outs[a].at[j0, pl.ds(s0, nr)], stage[a],
                                 sem_in.at[a], sem_out.at[a], nr)
            for a in range(n):
                got = _rows_of(_clip_pieces(lay[a], j0, c0))
                for k in range(3):
                    sent = _rows_of(_clip_pieces(lay[a], j0 ^ _REL[k], c0))
                    if sent:
                        pltpu.make_async_remote_copy(src_ref=sized(a, sent), dst_ref=sized(a, sent),
                                                     send_sem=send.at[3 * a + k], recv_sem=recv.at[3 * a + k],
                                                     device_id=(x, y, c), device_id_type=MESH).wait_send()
                    if got:
                        pltpu.make_async_remote_copy(src_ref=sized(a, got), dst_ref=sized(a, got),
                                                     send_sem=send.at[3 * a + k], recv_sem=recv.at[3 * a + k],
                                                     device_id=(x, y, c), device_id_type=MESH).wait_recv()

        _both_cores(run)

    return pl.pallas_call(
        body, name=name,
        out_shape=[jax.ShapeDtypeStruct(z.shape, z.dtype) for z in zeros],
        in_specs=[_HBM] * (2 * n), out_specs=[_HBM] * n, input_output_aliases={n + a: a for a in range(n)},
        scratch_shapes=[pltpu.SemaphoreType.DMA((3 * n,)), pltpu.SemaphoreType.DMA((3 * n,))]
        + [pltpu.VMEM((min(STAGE_ROWS, p.shape[0]),) + p.shape[1:], p.dtype) for p in parts]
        + [pltpu.SemaphoreType.DMA((n,)), pltpu.SemaphoreType.DMA((n,))],
    )(*parts, *zeros)


def _subset_exchange(names, bufs, l, name):
    n = len(bufs)
    lay = [LAYOUT[nm] for nm in names]

    def body(*refs):
        outs = refs[n:2 * n]
        send, recv = refs[2 * n:]
        x, y, c = _position()

        def run(j0, c0):
            for a in range(n):
                for s0, _, nr in _clip_pieces(lay[a], j0, c0):
                    rows = outs[a].at[l, pl.ds(s0, nr)]
                    pltpu.make_async_remote_copy(src_ref=rows, dst_ref=rows, send_sem=send.at[a], recv_sem=recv.at[a],
                                                 device_id=(x, y, 1 - c), device_id_type=MESH).start()
            for a in range(n):
                for c_half, wait_send in ((c0, True), (1 - c0, False)):
                    rows = _rows_of(_clip_pieces(lay[a], j0, c_half))
                    if rows:
                        ref = outs[a].at[l, pl.ds(0, rows)]
                        cp = pltpu.make_async_remote_copy(src_ref=ref, dst_ref=ref, send_sem=send.at[a], recv_sem=recv.at[a],
                                                          device_id=(x, y, 1 - c), device_id_type=MESH)
                        if wait_send:
                            cp.wait_send()
                        else:
                            cp.wait_recv()

        _both_cores(run)

    return pl.pallas_call(
        body, name=name,
        out_shape=[jax.ShapeDtypeStruct(b.shape, b.dtype) for b in bufs],
        in_specs=[_HBM] * n, out_specs=[_HBM] * n, input_output_aliases={a: a for a in range(n)},
        scratch_shapes=[pltpu.SemaphoreType.DMA((n,)), pltpu.SemaphoreType.DMA((n,))],
    )(*bufs)


def _scatter_start(names, gl, name):
    n = len(gl)
    lay = [LAYOUT[nm] for nm in names]
    zones = [lax.empty((N_CHIPS, lay[a][0]) + gl[a].shape[1:], gl[a].dtype) for a in range(n)]

    def body(*refs):
        ins, lands = refs[:n], refs[n:2 * n]
        send, recv, lsem = refs[2 * n:2 * n + 3]
        refs[-1][...] = jnp.zeros_like(refs[-1])
        x, y, c = _position()
        j = 2 * x + y
        chips = [(1 - x, y), (x, 1 - y), (1 - x, 1 - y)]
        for j0 in range(N_CHIPS):
            @pl.when(j == j0)
            def _(j0=j0):
                for a in range(n):
                    for s0, d0, nr in lay[a][2](j0):
                        pltpu.make_async_copy(ins[a].at[pl.ds(d0, nr)], lands[a].at[j0, pl.ds(s0, nr)], lsem.at[a]).start()
                    for k, chip in enumerate(chips):
                        for s0, d0, nr in lay[a][2](j0 ^ _REL[k]):
                            pltpu.make_async_remote_copy(
                                src_ref=ins[a].at[pl.ds(d0, nr)], dst_ref=lands[a].at[j0, pl.ds(s0, nr)],
                                send_sem=send.at[3 * a + k], recv_sem=recv.at[3 * a + k],
                                device_id=(*chip, c), device_id_type=MESH).start()

    sems = [pltpu.SemaphoreType.DMA((3 * n,)), pltpu.SemaphoreType.DMA((3 * n,)), pltpu.SemaphoreType.DMA((n,))]
    hbm = lambda a: pltpu.HBM(a.shape, a.dtype)
    res = pl.pallas_call(
        body, name=name,
        out_shape=sems + [hbm(g) for g in gl] + [hbm(z) for z in zones] + [_TOKEN],
        in_specs=[_HBM] * (2 * n), out_specs=[_SEM] * 3 + [_HBM] * (2 * n) + [_VMEM_TOKEN],
        input_output_aliases={a: 3 + a for a in range(2 * n)},
        compiler_params=pltpu.CompilerParams(has_side_effects=_EFFECT),
    )(*[pltpu.with_memory_space_constraint(g, pltpu.HBM) for g in gl],
      *[pltpu.with_memory_space_constraint(z, pltpu.HBM) for z in zones])
    return res[:3], res[3:3 + n], res[3 + n:3 + 2 * n], res[-1]


def _scatter_wait(names, sems, srcs, lands, after, name):
    n = len(srcs)
    lay = [LAYOUT[nm] for nm in names]

    def body(*refs):
        zones = refs[n:2 * n]
        send, recv, lsem = refs[2 * n:2 * n + 3]
        x, y, c = _position()
        for a in range(n):
            whole = zones[a].at[0, pl.ds(0, lay[a][0])]
            for k in range(3):
                cp = pltpu.make_async_remote_copy(src_ref=whole, dst_ref=whole, send_sem=send.at[3 * a + k],
                                                  recv_sem=recv.at[3 * a + k], device_id=(x, y, 1 - c),
                                                  device_id_type=MESH)
                cp.wait_send()
                cp.wait_recv()
            pltpu.make_async_copy(whole, whole, lsem.at[a]).wait()

    hbm = lambda a: pltpu.HBM(a.shape, a.dtype)
    res = pl.pallas_call(
        body, name=name,
        out_shape=[hbm(s) for s in srcs] + [hbm(z) for z in lands],
        in_specs=[_HBM] * (2 * n) + [_SEM] * 3 + [pl.BlockSpec(memory_space=pl.ANY)], out_specs=[_HBM] * (2 * n),
        input_output_aliases={a: a for a in range(2 * n)},
        compiler_params=pltpu.CompilerParams(has_side_effects=_EFFECT),
    )(*srcs, *lands, *sems, after)
    return res[n:]


def _peer(x, y, c, k):
    return (1 - x if k & 4 else x, 1 - y if k & 2 else y, 1 - c if k & 1 else c)


def _bcast_start(arrs, name, after=None):
    n = len(arrs)
    zones = [lax.empty((8,) + a.shape, a.dtype) for a in arrs]
    extra = [] if after is None else [after]

    def body(*refs):
        ins, lands = refs[:n], refs[n:2 * n]
        send, recv, lsem = refs[2 * n + len(extra):2 * n + len(extra) + 3]
        refs[-1][...] = jnp.zeros_like(refs[-1])
        x, y, c = _position()
        for a in range(n):
            dst = lands[a].at[4 * x + 2 * y + c]
            pltpu.make_async_copy(ins[a], dst, lsem.at[a]).start()
            for k in range(1, 8):
                pltpu.make_async_remote_copy(src_ref=ins[a], dst_ref=dst, send_sem=send.at[7 * a + k - 1],
                                             recv_sem=recv.at[7 * a + k - 1], device_id=_peer(x, y, c, k),
                                             device_id_type=MESH).start()

    sems = [pltpu.SemaphoreType.DMA((7 * n,)), pltpu.SemaphoreType.DMA((7 * n,)), pltpu.SemaphoreType.DMA((n,))]
    hbm = lambda a: pltpu.HBM(a.shape, a.dtype)
    res = pl.pallas_call(
        body, name=name,
        out_shape=sems + [hbm(a) for a in arrs] + [hbm(z) for z in zones] + [_TOKEN],
        in_specs=[_HBM] * (2 * n) + [pl.BlockSpec(memory_space=pl.ANY)] * len(extra),
        out_specs=[_SEM] * 3 + [_HBM] * (2 * n) + [_VMEM_TOKEN],
        input_output_aliases={a: 3 + a for a in range(2 * n)},
        compiler_params=pltpu.CompilerParams(has_side_effects=_EFFECT),
    )(*[pltpu.with_memory_space_constraint(a, pltpu.HBM) for a in arrs],
      *[pltpu.with_memory_space_constraint(z, pltpu.HBM) for z in zones], *extra)
    return res[:3], res[3:3 + n], res[3 + n:3 + 2 * n], res[-1]


def _bcast_wait(sems, srcs, lands, after, name):
    n = len(srcs)

    def body(*refs):
        ins, zones = refs[:n], refs[n:2 * n]
        send, recv, lsem = refs[2 * n:2 * n + 3]
        x, y, c = _position()
        for a in range(n):
            for k in range(1, 8):
                cp = pltpu.make_async_remote_copy(src_ref=ins[a], dst_ref=zones[a].at[0], send_sem=send.at[7 * a + k - 1],
                                                  recv_sem=recv.at[7 * a + k - 1], device_id=_peer(x, y, c, k),
                                                  device_id_type=MESH)
                cp.wait_send()
                cp.wait_recv()
            pltpu.make_async_copy(ins[a], zones[a].at[0], lsem.at[a]).wait()

    hbm = lambda a: pltpu.HBM(a.shape, a.dtype)
    res = pl.pallas_call(
        body, name=name,
        out_shape=[hbm(s) for s in srcs] + [hbm(z) for z in lands],
        in_specs=[_HBM] * (2 * n) + [_SEM] * 3 + [pl.BlockSpec(memory_space=pl.ANY)], out_specs=[_HBM] * (2 * n),
        input_output_aliases={a: a for a in range(2 * n)},
        compiler_params=pltpu.CompilerParams(has_side_effects=_EFFECT),
    )(*srcs, *lands, *sems, after)
    return res[n:]


def _swap_start(arrs, name):
    n = len(arrs)
    zones = [lax.empty(a.shape, a.dtype) for a in arrs]

    def body(*refs):
        ins, lands = refs[:n], refs[n:2 * n]
        send, recv = refs[2 * n:2 * n + 2]
        refs[-1][...] = jnp.zeros_like(refs[-1])
        x, y, c = _position()
        for a in range(n):
            pltpu.make_async_remote_copy(src_ref=ins[a], dst_ref=lands[a], send_sem=send.at[a], recv_sem=recv.at[a],
                                         device_id=(x, y, 1 - c), device_id_type=MESH).start()

    sems = [pltpu.SemaphoreType.DMA((n,)), pltpu.SemaphoreType.DMA((n,))]
    hbm = lambda a: pltpu.HBM(a.shape, a.dtype)
    res = pl.pallas_call(
        body, name=name,
        out_shape=sems + [hbm(a) for a in arrs] + [hbm(z) for z in zones] + [_TOKEN],
        in_specs=[_HBM] * (2 * n), out_specs=[_SEM] * 2 + [_HBM] * (2 * n) + [_VMEM_TOKEN],
        input_output_aliases={a: 2 + a for a in range(2 * n)},
        compiler_params=pltpu.CompilerParams(has_side_effects=_EFFECT),
    )(*[pltpu.with_memory_space_constraint(a, pltpu.HBM) for a in arrs],
      *[pltpu.with_memory_space_constraint(z, pltpu.HBM) for z in zones])
    return res[:2], res[2:2 + n], res[2 + n:2 + 2 * n], res[-1]


def _swap_wait(sems, srcs, lands, after, name):
    n = len(srcs)

    def body(*refs):
        ins, zones = refs[:n], refs[n:2 * n]
        send, recv = refs[2 * n:2 * n + 2]
        x, y, c = _position()
        for a in range(n):
            cp = pltpu.make_async_remote_copy(src_ref=ins[a], dst_ref=zones[a], send_sem=send.at[a], recv_sem=recv.at[a],
                                              device_id=(x, y, 1 - c), device_id_type=MESH)
            cp.wait_send()
            cp.wait_recv()

    hbm = lambda a: pltpu.HBM(a.shape, a.dtype)
    res = pl.pallas_call(
        body, name=name,
        out_shape=[hbm(s) for s in srcs] + [hbm(z) for z in lands],
        in_specs=[_HBM] * (2 * n) + [_SEM] * 2 + [pl.BlockSpec(memory_space=pl.ANY)], out_specs=[_HBM] * (2 * n),
        input_output_aliases={a: a for a in range(2 * n)},
        compiler_params=pltpu.CompilerParams(has_side_effects=_EFFECT),
    )(*srcs, *lands, *sems, after)
    return res[:n], res[n:]


def _row_tile(r):
    for t in (256, 128, 64, 32, 16, 8):
        if r % t == 0 and r > t:
            return t
    return r


def _pair_add_half(g, rb, c_arr, name):
    hrows, rest = rb.shape[0], rb.shape[1:]
    tr = _row_tile(hrows)
    nb = hrows // tr
    z = (0,) * len(rest)

    def body(c_ref, g_ref, r_ref, o_ref):
        o_ref[...] = (g_ref[...].astype(F32) + r_ref[...].astype(F32)).astype(o_ref.dtype)

    return pl.pallas_call(
        body, name=name,
        grid_spec=pltpu.PrefetchScalarGridSpec(
            num_scalar_prefetch=1, grid=(nb,),
            in_specs=[pl.BlockSpec((tr,) + rest, lambda i, c_ref: (c_ref[0] * nb + i,) + z),
                      pl.BlockSpec((tr,) + rest, lambda i, c_ref: (i,) + z)],
            out_specs=pl.BlockSpec((tr,) + rest, lambda i, c_ref: (i,) + z)),
        out_shape=jax.ShapeDtypeStruct((hrows,) + rest, BF16),
        compiler_params=pltpu.CompilerParams(dimension_semantics=("parallel",), vmem_limit_bytes=VMEM_LIMIT),
    )(c_arr, g, rb)


def _sum_slabs(slabs, l, buf, name):
    m = len(slabs)
    n, R, rest = slabs[0].shape[0], slabs[0].shape[1], slabs[0].shape[2:]
    tr = _row_tile(R)
    z = (0,) * len(rest)

    def body(*refs):
        total = None
        for r_ref in refs[:m]:
            acc = r_ref[0].astype(F32)
            for k in range(1, n):
                acc = acc + r_ref[k].astype(F32)
            total = acc if total is None else total + acc
        refs[-1][...] = total

    if R // tr > 64 and len(rest) == 1 and rest[0] % 256 == 0:
        grid = (rest[0] // 256,)
        in_spec = pl.BlockSpec((n, R, 256), lambda i: (0, 0, i))
        out_spec = pl.BlockSpec((None, R, 256), lambda i: (l, 0, i))
    else:
        grid = (R // tr,)
        in_spec = pl.BlockSpec((n, tr) + rest, lambda i: (0, i) + z)
        out_spec = pl.BlockSpec((None, tr) + rest, lambda i: (l, i) + z)
    in_specs, args, aliases = [in_spec] * m, list(slabs), {}
    if buf is not None:
        in_specs.append(pl.BlockSpec(memory_space=pl.ANY))
        args.append(buf)
        aliases = {m: 0}
    return pl.pallas_call(
        body, name=name, grid=grid, in_specs=in_specs, out_specs=out_spec,
        out_shape=jax.ShapeDtypeStruct((DEPTH, R) + rest, F32), input_output_aliases=aliases,
        compiler_params=pltpu.CompilerParams(
            dimension_semantics=("parallel",),
            vmem_limit_bytes=_vmem(m * _nbytes(in_spec.block_shape, slabs[0].dtype) + _nbytes(out_spec.block_shape, F32),
                                   2 * _nbytes(out_spec.block_shape, F32))),
    )(*args)


def _adam_math(w, g, m, v):
    mn = ADAM_B1 * m + (1.0 - ADAM_B1) * g
    vn = ADAM_B2 * v + (1.0 - ADAM_B2) * (g * g)
    m_hat = mn / (1.0 - ADAM_B1 ** ADAM_STEP)
    v_hat = vn / (1.0 - ADAM_B2 ** ADAM_STEP)
    return -ADAM_LR * (m_hat / (jnp.sqrt(v_hat) + ADAM_EPS) + ADAM_WD * w), mn, vn


def _adamw(w, g, m, v, name, token=None):
    L, R, C = w.shape
    tr = _row_tile(R)
    extra = [] if token is None else [token]

    def body(w_ref, g_ref, m_ref, v_ref, *rest):
        d_ref, mo_ref, vo_ref = rest[-3:]
        d_ref[...], mo_ref[...], vo_ref[...] = _adam_math(w_ref[...], g_ref[...], m_ref[...], v_ref[...])

    if R // tr > 64 and C % 128 == 0:
        spec, grid = pl.BlockSpec((None, R, 128), lambda l, i: (l, 0, i)), (L, C // 128)
    else:
        spec, grid = pl.BlockSpec((None, tr, C), lambda l, i: (l, i, 0)), (L, R // tr)
    return pl.pallas_call(
        body, name=name, grid=grid, in_specs=[spec] * 4 + [pl.BlockSpec(memory_space=pl.ANY)] * len(extra),
        out_specs=[spec] * 3, out_shape=[jax.ShapeDtypeStruct((L, R, C), F32)] * 3,
        compiler_params=pltpu.CompilerParams(dimension_semantics=("parallel", "parallel"),
                                             vmem_limit_bytes=_vmem(7 * _nbytes(spec.block_shape, F32))),
    )(w, g, m, v, *extra)


_VMEM_WHOLE = pl.BlockSpec(memory_space=pltpu.VMEM)


def _matrix_update(gath, w, m, v, name):
    K = w.shape[1]

    def body(g0_ref, g1_ref, w_ref, m_ref, v_ref, go_ref, d_ref, mo_ref, vo_ref):
        for l, gr in enumerate((g0_ref, g1_ref)):
            for k in range(K):
                g = gr[0, k].astype(F32)
                for dev in range(1, 8):
                    g = g + gr[dev, k].astype(F32)
                go_ref[l, k] = g
                d_ref[l, k], mo_ref[l, k], vo_ref[l, k] = _adam_math(w_ref[l, k], g, m_ref[l, k], v_ref[l, k])

    return pl.pallas_call(
        body, name=name, in_specs=[_VMEM_WHOLE] * 5, out_specs=[_VMEM_WHOLE] * 4,
        out_shape=[jax.ShapeDtypeStruct(w.shape, F32)] * 4,
        compiler_params=pltpu.CompilerParams(vmem_limit_bytes=32 * MIB),
    )(gath[0], gath[1], w, m, v)


VECS = (("pre_norm_g", D), ("post_norm_g", D), ("gm_ln_g", GM_W), ("gm_ln_b", GM_W), ("mla_q_norm_g", QR),
        ("mla_kv_norm_g", KVR), ("lru_conv_b", LRU_W), ("lru_b_a", LRU_W), ("lru_b_x", LRU_W), ("lru_lambda", LRU_W))
VEC_KEY = {"pre_norm_g": "pre_g", "post_norm_g": "post_g", "gm_ln_g": "ln_g", "gm_ln_b": "ln_b", "mla_q_norm_g": "qg",
           "mla_kv_norm_g": "kvg", "lru_conv_b": "conv_b", "lru_b_a": "ba", "lru_b_x": "bx", "lru_lambda": "lam"}
VEC_ROWS, VEC_W, VEC_ROW0, LOSS_ROW = 16, LRU_W, GM_G, 14


def _pack_rows(LG, loss_part):
    per = len(VECS) + 1
    ins = []
    for G in LG:
        ins += [G[VEC_KEY[n]] for n, _ in VECS] + [G["bst"]]
    ins.append(loss_part)

    def body(*refs):
        o_ref = refs[-1]
        o_ref[...] = jnp.zeros_like(o_ref)
        for l in range(DEPTH):
            base = VEC_ROWS * l
            o_ref[pl.ds(base, 8), pl.ds(0, GM_B)] = refs[per * l + len(VECS)][...].T[:8, :]
            for t, (_, width) in enumerate(VECS):
                o_ref[pl.ds(base + VEC_ROW0 + t, 1), pl.ds(0, width)] = refs[per * l + t][...]
        o_ref[pl.ds(LOSS_ROW, 1), pl.ds(0, 128)] = jnp.broadcast_to(refs[-2][...], (1, 128))

    return pl.pallas_call(
        body, name="pack_rows", in_specs=[_VMEM_WHOLE] * len(ins), out_specs=_VMEM_WHOLE,
        out_shape=jax.ShapeDtypeStruct((DEPTH * VEC_ROWS, VEC_W), F32),
    )(*ins)


def _vector_update(gath, W, M, V):
    names = [n for n, _ in VECS] + ["gm_bs"]
    nw = len(names)

    def body(*refs):
        g_ref = refs[0]
        wr, mr, vr = refs[1:1 + nw], refs[1 + nw:1 + 2 * nw], refs[1 + 2 * nw:1 + 3 * nw]
        outs = refs[1 + 3 * nw:]
        s = g_ref[0]
        for dev in range(1, 8):
            s = s + g_ref[dev]
        for t, (_, width) in enumerate(VECS):
            for l in range(DEPTH):
                r = VEC_ROWS * l + VEC_ROW0 + t
                g = s[r:r + 1, :width]
                row = (pl.ds(l, 1), slice(None))
                res = (g,) + _adam_math(wr[t][row], g, mr[t][row], vr[t][row])
                for q in range(4):
                    outs[4 * t + q][row] = res[q]
        t = len(VECS)
        for l in range(DEPTH):
            for k in range(GM_G):
                g = s[VEC_ROWS * l + k:VEC_ROWS * l + k + 1, :GM_B]
                row = (l, pl.ds(k, 1), slice(None))
                res = (g,) + _adam_math(wr[t][row], g, mr[t][row], vr[t][row])
                for q in range(4):
                    outs[4 * t + q][row] = res[q]
        outs[4 * nw][...] = s[LOSS_ROW:LOSS_ROW + 1, :128]

    ws = [W[n] for n in names]
    out_shape = []
    for w in ws:
        out_shape += [jax.ShapeDtypeStruct(w.shape, F32)] * 4
    out_shape.append(jax.ShapeDtypeStruct((1, 128), F32))
    res = pl.pallas_call(
        body, name="vector_update", in_specs=[_VMEM_WHOLE] * (1 + 3 * nw), out_specs=[_VMEM_WHOLE] * (4 * nw + 1),
        out_shape=out_shape, compiler_params=pltpu.CompilerParams(vmem_limit_bytes=VMEM_LIMIT),
    )(gath, *ws, *[M[n] for n in names], *[V[n] for n in names])
    return {n: tuple(res[4 * t:4 * t + 4]) for t, n in enumerate(names)}, res[4 * nw]


SHARDED = ("w_in", "mla_w_uq", "mla_w_ukv", "lru_conv_w", "w_proj_a", "w_proj_b", "w_proj_c", "w_out")
FIRST = ("w_in", "lru_conv_w")
LATER = tuple(n for n in SHARDED if n not in FIRST)
COL_SHARDED = ("w_in", "mla_w_uq", "mla_w_ukv", "lru_conv_w")
SMALL = ("pre_norm_g", "gm_ln_g", "gm_ln_b", "gm_ws", "gm_bs", "mla_q_norm_g", "mla_kv_norm_g", "lru_conv_b",
         "lru_w_a", "lru_b_a", "lru_w_x", "lru_b_x", "lru_lambda", "post_norm_g")
WEIGHTS = ("pre_norm_g", "w_in", "gm_ln_g", "gm_ln_b", "gm_ws", "gm_bs", "mla_q_norm_g", "mla_w_uq",
           "mla_kv_norm_g", "mla_w_ukv", "lru_conv_w", "lru_conv_b", "lru_w_a", "lru_b_a", "lru_w_x", "lru_b_x",
           "lru_lambda", "w_proj_a", "w_proj_b", "w_proj_c", "w_out", "post_norm_g")


GB_KEY = {"w_in": "wp", "mla_w_uq": "wuq", "mla_w_ukv": "wukv", "w_proj_a": "wpa", "w_proj_b": "wpb",
          "w_proj_c": "wpc", "w_out": "wout"}


def _prepare(l, gathered, small, wsb):
    P = {GB_KEY[n]: gathered[n] for n in GB_KEY if n in gathered}
    P["conv_w"] = gathered["lru_conv_w"].transpose(1, 0, 2).reshape(CONV_W, LRU_W)
    P["wsb"] = wsb
    row = lambda n: small[n][l][None, :]
    P["pre_g"], P["post_g"] = row("pre_norm_g"), row("post_norm_g")
    P["ln_g"], P["ln_b"] = row("gm_ln_g"), row("gm_ln_b")
    P["ws"] = small["gm_ws"][l]
    P["bst"] = jnp.pad(small["gm_bs"][l].T, ((0, 0), (0, 128 - GM_G)))
    P["qg"], P["kvg"] = row("mla_q_norm_g"), row("mla_kv_norm_g")
    P["conv_b"], P["ba"], P["bx"], P["lam"] = row("lru_conv_b"), row("lru_b_a"), row("lru_b_x"), row("lru_lambda")
    return P


def kernel(x, pre_norm_g, w_in, gm_ln_g, gm_ln_b, gm_ws, gm_bs, mla_q_norm_g, mla_w_uq, mla_kv_norm_g, mla_w_ukv, lru_conv_w, lru_conv_b, lru_w_a, lru_b_a, lru_w_x, lru_b_x, lru_lambda, w_proj_a, w_proj_b, w_proj_c, w_out, post_norm_g, loss_target, m_pre_norm_g, m_w_in, m_gm_ln_g, m_gm_ln_b, m_gm_ws, m_gm_bs, m_mla_q_norm_g, m_mla_w_uq, m_mla_kv_norm_g, m_mla_w_ukv, m_lru_conv_w, m_lru_conv_b, m_lru_w_a, m_lru_b_a, m_lru_w_x, m_lru_b_x, m_lru_lambda, m_w_proj_a, m_w_proj_b, m_w_proj_c, m_w_out, m_post_norm_g, v_pre_norm_g, v_w_in, v_gm_ln_g, v_gm_ln_b, v_gm_ws, v_gm_bs, v_mla_q_norm_g, v_mla_w_uq, v_mla_kv_norm_g, v_mla_w_ukv, v_lru_conv_w, v_lru_conv_b, v_lru_w_a, v_lru_b_a, v_lru_w_x, v_lru_b_x, v_lru_lambda, v_w_proj_a, v_w_proj_b, v_w_proj_c, v_w_out, v_post_norm_g):
    args = dict(locals())
    W = {n: args[n] for n in WEIGHTS}
    M = {n: args["m_" + n] for n in WEIGHTS}
    V = {n: args["v_" + n] for n in WEIGHTS}
    c = lax.axis_index("c")

    def shards(l, names):
        out = []
        for n in names:
            blk = W[n][l].T if n in TRANSPOSED else W[n][l]
            out.append(blk[None] if n == "lru_conv_w" else blk.astype(BF16))
        return out

    small = {n: W[n] for n in SMALL}
    wsb = _superblocks(W["lru_w_a"], W["lru_w_x"])
    tabs = _rope_tables()
    s0a, s0b, s1a, s1b = shards(0, FIRST), shards(0, LATER), shards(1, FIRST), shards(1, LATER)
    g0, zones = _weights_allgather(FIRST, s0a, "weights_allgather_l0", carry=_gather_zeros(LATER, s0b)
                                   + _gather_zeros(FIRST, s1a) + _gather_zeros(LATER, s1b))
    nl, nf = len(LATER), len(FIRST)
    w0b = _gather_start(LATER, s0b, zones[:nl], "weights_gather_start_l0")
    w1a = _gather_start(FIRST, s1a, zones[nl:nl + nf], "weights_gather_start_l1_first", after=w0b[3])
    w1b = _gather_start(LATER, s1b, zones[nl + nf:], "weights_gather_start_l1_later", after=w1a[3])

    def late(started, name):
        def wait(after):
            got = _gather_wait(LATER, *started[:3], after, name)
            return {GB_KEY[n]: g for n, g in zip(LATER, got)}
        return wait

    P = [_prepare(0, dict(zip(FIRST, g0)), small, wsb), None]
    h0 = x[0]
    h1, A0 = _layer_fwd(h0, P[0], 0, tabs, w1b[3], late(w0b, "weights_gather_wait_l0"))
    g1 = _gather_wait(FIRST, *w1a[:3], h1, "weights_gather_wait_l1_first")
    P[1] = _prepare(1, dict(zip(FIRST, g1)), small, wsb)
    h2, A1 = _layer_fwd(h1, P[1], 1, tabs, None, late(w1b, "weights_gather_wait_l1_later"))
    dy, loss_part = _loss_fwd(h2, loss_target[0])

    def large_grads(G, GB, names):
        conv = G["conv_w"].reshape(CONV_W, N_CHIPS, LRU_W // N_CHIPS).transpose(1, 0, 2)
        return [conv if n == "lru_conv_w" else GB[GB_KEY[n]] for n in names]

    started = {}

    def early1(GB):
        started["sc1b"] = _scatter_start(LATER, [GB[GB_KEY[n]] for n in LATER], "grads_scatter_start_l1_later")
        return started["sc1b"][3], None

    d1, G1, GB1 = _layer_bwd(dy, A1, P[1], 1, tabs, None, early1)
    sc1a = _scatter_start(FIRST, large_grads(G1, GB1, FIRST), "grads_scatter_start_l1_first")

    def early0(GB):
        got_b = _scatter_wait(LATER, *started["sc1b"][:3], GB["wukv"], "grads_scatter_wait_l1_later")
        got_a = _scatter_wait(FIRST, *sc1a[:3], got_b[0], "grads_scatter_wait_l1_first")
        got = dict(zip(LATER + FIRST, list(got_b) + list(got_a)))
        started["swap1"] = _swap_start([got[n] for n in SHARDED], "partials_swap_start_l1")
        started["sc0"] = _scatter_start(LATER, [GB[GB_KEY[n]] for n in LATER], "grads_scatter_start_l0")
        return started["sc0"][3], started["swap1"][3]

    d0, G0, GB0 = _layer_bwd(d1, A0, P[0], 0, tabs, sc1a[3], early0)
    LG = (G0, G1)
    mine0 = _scatter_wait(LATER, *started["sc0"][:3], d0, "grads_scatter_wait_l0")
    swap0 = _swap_start(mine0, "partials_swap_start_l0")
    g0f = large_grads(G0, GB0, FIRST)
    c_arr = jnp.reshape(c, (1,)).astype(jnp.int32)
    from_sib = _half_to_sibling(FIRST, g0f, "grads_half_to_sibling_l0", after=swap0[3])
    pair = [_pair_add_half(g, rb, c_arr, "pair_add_" + n) for n, g, rb in zip(FIRST, g0f, from_sib)]
    slabs = _chip_scatter_half(FIRST, pair, "grads_chip_scatter_l0")
    mats = []
    for g in LG:
        mats += [g["ws"].astype(BF16), g["wab"][0, :, :, :LRU_BW], g["wab"][1, :, :, :LRU_BW]]
    bc = _bcast_start([_pack_rows(LG, loss_part)] + mats, "small_grads_start", after=slabs[0])
    mine1, theirs1 = _swap_wait(*started["swap1"][:3], bc[3], "partials_swap_wait_l1")
    both = dict(zip(SHARDED, [_sum_slabs([a, b], 1, None, "sum_partials_l1_" + n)
                              for n, a, b in zip(SHARDED, mine1, theirs1)]))
    for n, s in zip(FIRST, slabs):
        both[n] = _sum_slabs([s], 0, both[n], "sum_slabs_l0_" + n)
    done = _subset_exchange(FIRST, [both[n] for n in FIRST], 0, "reduced_rows_to_sibling_l0")
    both.update(zip(FIRST, done))
    mine0, theirs0 = _swap_wait(*swap0[:3], done[0], "partials_swap_wait_l0")
    for n, a, b in zip(LATER, mine0, theirs0):
        both[n] = _sum_slabs([a, b], 0, both[n], "sum_partials_l0_" + n)
    both = [both[n] for n in SHARDED]
    grads = {}
    for n, b in zip(SHARDED, both):
        if n in TRANSPOSED and n != "w_in":
            b = jnp.swapaxes(b, 1, 2)
        grads[n] = b if n == "w_in" else b.reshape(W[n].shape)

    upd, last = {}, None
    for n in SHARDED:
        token = bc[3] if n == SHARDED[0] else None
        if n == "w_in":
            tr = lambda a: jnp.swapaxes(a, 1, 2)
            res = _adamw(tr(W[n]), grads[n], tr(M[n]), tr(V[n]), "adamw_" + n, token)
            upd[n] = tuple(tr(a) for a in (grads[n],) + tuple(res))
        else:
            res = _adamw(W[n], grads[n], M[n], V[n], "adamw_" + n, token)
            upd[n] = (grads[n],) + tuple(res)
        last = res[0]

    gath = _bcast_wait(*bc[:3], last, "small_grads_wait")
    vec_upd, loss_row = _vector_update(gath[0], W, M, V)
    upd.update(vec_upd)
    loss = loss_row[0, 0]
    for k, n in enumerate(("gm_ws", "lru_w_a", "lru_w_x")):
        upd[n] = _matrix_update((gath[1 + k], gath[4 + k]), W[n], M[n], V[n], "update_" + n)

    return (loss, d0[None], *[upd[n][0] for n in WEIGHTS], *[upd[n][1] for n in WEIGHTS],
            *[upd[n][2] for n in WEIGHTS], *[upd[n][3] for n in WEIGHTS])
```

```python
import functools
import math

import jax
import jax.numpy as jnp
from jax import lax
from jax.experimental import pallas as pl
from jax.experimental.pallas import tpu as pltpu

F32, BF16 = jnp.float32, jnp.bfloat16
MESH = pl.DeviceIdType.MESH

S, D, DEPTH = 2048, 1024, 2
CHUNK, EPS = 64, 1e-6
GM_W, GM_G, GM_B = 1024, 4, 128
H, NOPE, ROPE, VDIM = 8, 128, 64, 128
QR, KVR = 384, 256
MLA_W = H * VDIM
LRU_W, LRU_NB, LRU_BW, LRU_C, CONV_W = 1280, 16, 80, 8.0, 4
ROPE_THETA = 10000.0
IN_SIZES = (GM_W, GM_W, GM_W, QR, KVR, ROPE, MLA_W, LRU_W, LRU_W, D, D, D)
N_IN = sum(IN_SIZES)
N_CHIPS = 4
ADAM_LR, ADAM_B1, ADAM_B2, ADAM_EPS, ADAM_WD, ADAM_STEP = 0.001, 0.9, 0.999, 1e-08, 0.01, 10

HP = 256
O_U, O_V, O_ZA, O_GA, O_GB, O_GC = 0, 1024, 2048, 3072, 4096, 5120
O_CKV, O_KR, O_CQ, O_XC, O_ZC, O_ZB = 6144, 6400, 6528, 7680, 8960, 10240
NP = 11264
MIB = 1024 * 1024
VMEM_LIMIT = 16 * MIB


def _vmem(block_bytes, temp_bytes=0):
    return int(min(max(2 * block_bytes + temp_bytes + 4 * MIB, VMEM_LIMIT), 56 * MIB))


def _nbytes(shape, dtype):
    return math.prod(d for d in shape if d is not None) * jnp.dtype(dtype).itemsize


def _tile(dim, target):
    if dim <= target:
        return dim
    t = (target // 128) * 128
    while dim % t:
        t -= 128
    return t


def _sig(x):
    return jax.nn.sigmoid(x)


def _silu(x):
    return x * _sig(x)


def _dsilu(x):
    s = _sig(x)
    return s * (1.0 + x * (1.0 - s))


def _mm(a, b, mode, name, out_dtype=F32, tm=1024, tn=1024, tk=1024, b_lead=None, out_lead=None, token=None):
    b2 = b.shape[1:] if b_lead is not None else b.shape
    if mode == "nn":
        (M, K), (K2, N) = a.shape, b2
    elif mode == "nt":
        (M, K), (N, K2) = a.shape, b2
    else:
        (K, M), (K2, N) = a.shape, b2
    assert K == K2, (name, a.shape, b.shape)
    tm, tn, tk = _tile(M, tm), _tile(N, tn), _tile(K, tk)
    nk = K // tk
    if mode == "tn":
        a_spec = pl.BlockSpec((tk, tm), lambda i, j, k: (k, i))
        lhs_c = 0
    else:
        a_spec = pl.BlockSpec((tm, tk), lambda i, j, k: (i, k))
        lhs_c = 1
    b_blk, b_idx, rhs_c = ((tn, tk), (lambda i, j, k: (j, k)), 1) if mode == "nt" else ((tk, tn), (lambda i, j, k: (k, j)), 0)
    if b_lead is None:
        b_spec = pl.BlockSpec(b_blk, b_idx)
    else:
        b_spec = pl.BlockSpec((None,) + b_blk, functools.partial(lambda i, j, k, f, l: (l,) + f(i, j, k), f=b_idx, l=b_lead))
    dims = (((lhs_c,), (rhs_c,)), ((), ()))
    in_specs, args, aliases = [a_spec, b_spec], [a, b], {}
    if out_lead is None:
        out_spec = pl.BlockSpec((tm, tn), lambda i, j, k: (i, j))
        out_shape = jax.ShapeDtypeStruct((M, N), out_dtype)
    else:
        l_out, n_lead, buf = out_lead
        out_spec = pl.BlockSpec((None, tm, tn), functools.partial(lambda i, j, k, l: (l, i, j), l=l_out))
        out_shape = jax.ShapeDtypeStruct((n_lead, M, N), out_dtype)
        if buf is not None:
            in_specs.append(pl.BlockSpec(memory_space=pl.ANY))
            args.append(buf)
            aliases = {2: 0}
    if token is not None:
        in_specs.append(pl.BlockSpec(memory_space=pl.ANY))
        args.append(token)

    def body(a_ref, b_ref, *rest):
        o_ref, acc_ref = rest[-2:]
        k = pl.program_id(2)

        @pl.when(k == 0)
        def _():
            acc_ref[...] = jnp.zeros_like(acc_ref)

        acc_ref[...] += lax.dot_general(a_ref[...].astype(BF16), b_ref[...].astype(BF16), dims,
                                        preferred_element_type=F32)

        @pl.when(k == nk - 1)
        def _():
            o_ref[...] = acc_ref[...].astype(o_ref.dtype)

    return pl.pallas_call(
        body, name=name, grid=(M // tm, N // tn, nk),
        in_specs=in_specs, out_specs=out_spec, out_shape=out_shape,
        scratch_shapes=[pltpu.VMEM((tm, tn), F32)], input_output_aliases=aliases,
        compiler_params=pltpu.CompilerParams(
            dimension_semantics=("parallel", "parallel", "arbitrary"),
            vmem_limit_bytes=_vmem(_nbytes((tm, tk), a.dtype) + _nbytes((tk, tn), b.dtype) + _nbytes((tm, tn), out_dtype),
                                   _nbytes((tm, tn), F32) + _nbytes((tm, tk), BF16) + _nbytes((tk, tn), BF16))),
    )(*args)


def _rows(fn, name, tm, rows, halos=(), fulls=(), outs=(), accs=()):
    n = S // tm
    in_specs, args = [], []
    for arr, w, cb in rows:
        in_specs.append(pl.BlockSpec((tm, w), functools.partial(lambda i, cb: (i, cb), cb=cb)))
        args.append(arr)
    for arr, w, cb, side in halos:
        if side == "prev":
            im = functools.partial(lambda i, cb: (jnp.maximum(i * (tm // 16) - 1, 0), cb), cb=cb)
        else:
            im = functools.partial(lambda i, cb: (jnp.minimum((i + 1) * (tm // 16), S // 16 - 1), cb), cb=cb)
        in_specs.append(pl.BlockSpec((16, w), im))
        args.append(arr)
    for arr in fulls:
        in_specs.append(pl.BlockSpec(arr.shape, functools.partial(lambda i, nd: (0,) * nd, nd=arr.ndim)))
        args.append(arr)
    out_shape, out_specs, aliases, n_alias = [], [], {}, 0
    for k, o in enumerate(outs):
        if len(o) == 3 and o[2] == "T":
            out_shape.append(jax.ShapeDtypeStruct((o[0], S), o[1]))
            out_specs.append(pl.BlockSpec((o[0], tm), lambda i: (0, i)))
        elif len(o) == 3:
            buf, total, cb = o[2]
            out_shape.append(jax.ShapeDtypeStruct((S, total), o[1]))
            out_specs.append(pl.BlockSpec((tm, o[0]), functools.partial(lambda i, cb: (i, cb), cb=cb)))
            if buf is not None:
                aliases[len(args)] = k
                in_specs.append(pl.BlockSpec(memory_space=pl.ANY))
                args.append(buf)
                n_alias += 1
        else:
            out_shape.append(jax.ShapeDtypeStruct((S, o[0]), o[1]))
            out_specs.append(pl.BlockSpec((tm, o[0]), lambda i: (i, 0)))
    for shp in accs:
        out_shape.append(jax.ShapeDtypeStruct(shp, F32))
        out_specs.append(pl.BlockSpec(shp, functools.partial(lambda i, nd: (0,) * nd, nd=len(shp))))
    nr, nh, nf, no, na = len(rows), len(halos), len(fulls), len(outs), len(accs)
    blocks = (sum(_nbytes((tm, w), arr.dtype) for arr, w, _ in rows) + sum(_nbytes(a.shape, a.dtype) for a in fulls)
              + sum(_nbytes((tm, o[0]), o[1]) for o in outs) + sum(_nbytes(shp, F32) for shp in accs))
    widest = _nbytes((tm, max([w for _, w, _ in rows] + [o[0] for o in outs])), F32)

    def body(*refs):
        i = pl.program_id(0)
        ins, orefs = refs[:nr + nh + nf], refs[nr + nh + nf + n_alias:]
        rv = [r[...].astype(F32) for r in ins[:nr]]
        hv = [r[...].astype(F32)[8:] if h[3] == "prev" else r[...].astype(F32)[:8] for r, h in zip(ins[nr:nr + nh], halos)]
        fv = [r[...] for r in ins[nr + nh:]]
        o, a = fn(i, rv, hv, fv)
        assert len(o) == no and len(a) == na, name
        for spec, ref, val in zip(outs, orefs[:no], o):
            ref[...] = (val.T if len(spec) == 3 and spec[2] == "T" else val).astype(ref.dtype)
        if na:
            @pl.when(i == 0)
            def _():
                for ref in orefs[no:]:
                    ref[...] = jnp.zeros_like(ref)

            for ref, val in zip(orefs[no:], a):
                ref[...] += val

    res = pl.pallas_call(
        body, name=name, grid=(n,), in_specs=in_specs, out_specs=out_specs, out_shape=out_shape,
        input_output_aliases=aliases,
        compiler_params=pltpu.CompilerParams(dimension_semantics=("arbitrary",), vmem_limit_bytes=_vmem(blocks, 6 * widest)),
    )(*args)
    return res


def _shift_down(xb, halo, s, row):
    fix = jnp.tile(pltpu.roll(halo, s, 0), (xb.shape[0] // 8, 1))
    return jnp.where(row >= s, pltpu.roll(xb, s, 0), fix)


def _shift_up(xb, halo, s, row):
    tm = xb.shape[0]
    fix = jnp.tile(pltpu.roll(halo, 8 - s, 0), (tm // 8, 1))
    return jnp.where(row < tm - s, pltpu.roll(xb, tm - s, 0), fix)


def _rms(x):
    return lax.rsqrt(jnp.mean(x * x, axis=-1, keepdims=True) + EPS)


def _rms_bwd(dy, x, g):
    r = _rms(x)
    xh = x * r
    dxh = dy * g
    dx = r * (dxh - xh * jnp.mean(dxh * xh, axis=-1, keepdims=True))
    return dx, dy * xh


def _colsum(x):
    return jnp.sum(x, axis=0, keepdims=True)


def _prenorm_fwd(x, g, token=None):
    def fn(i, rv, hv, fv):
        return [rv[0] * _rms(rv[0]) * fv[0]], []
    return _rows(fn, "prenorm_fwd", 256, [(x, D, 0)], fulls=[g] + ([] if token is None else [token]), outs=[(D, BF16)])[0]


def _gm_mask():
    r = lax.broadcasted_iota(jnp.int32, (GM_B, GM_B), 0) // CHUNK
    c = lax.broadcasted_iota(jnp.int32, (GM_B, GM_B), 1) // CHUNK
    return c <= r


def _gm_norm(v, g, b):
    mu = jnp.mean(v, axis=-1, keepdims=True)
    vc = v - mu
    rs = lax.rsqrt(jnp.mean(vc * vc, axis=-1, keepdims=True) + EPS)
    vh = vc * rs
    return vh, rs, vh * g + b


def _gm_sv(vn, ws, bst):
    mask = _gm_mask()
    gw = GM_W // GM_G
    parts = []
    for g in range(GM_G):
        wm = jnp.where(mask, ws[g], 0.0).astype(BF16)
        parts.append(jnp.dot(wm, vn[:, g * gw:(g + 1) * gw].astype(BF16), preferred_element_type=F32)
                     + bst[:, g:g + 1])
    return jnp.concatenate(parts, axis=1)


def _gmlp_fwd(proj, ln_g, ln_b, ws, bst):
    def fn(i, rv, hv, fv):
        u, v, z = rv
        g, b, w, bt = fv
        _, _, vn = _gm_norm(v, g, b)
        return [u * _gm_sv(vn, w, bt) * _silu(z)], []
    return _rows(fn, "gmlp_fwd", GM_B, [(proj, GM_W, 0), (proj, GM_W, 1), (proj, GM_W, 2)],
                 fulls=[ln_g, ln_b, ws, bst], outs=[(GM_W, BF16)])[0]


def _mla_prep_fwd(proj, qg, kvg):
    def fn(i, rv, hv, fv):
        cq, ckv = rv
        g1, g2 = fv
        return [cq * _rms(cq) * g1, ckv * _rms(ckv) * g2], []
    return _rows(fn, "mla_prep_fwd", 256, [(proj, QR, O_CQ // QR), (proj, KVR, O_CKV // KVR)],
                 fulls=[qg, kvg], outs=[(QR, BF16), (KVR, BF16)])


def _rot(t, cc, sa, sb):
    return t * cc + pltpu.roll(t, 32, 1) * sa + pltpu.roll(t, 96, 1) * sb


def _rot_t(g, cc, sa, sb):
    return g * cc + pltpu.roll(g * sa, 96, 1) + pltpu.roll(g * sb, 32, 1)


def _rope_tables():
    pos = jnp.arange(S, dtype=F32)
    inv_freq = ROPE_THETA ** (-jnp.arange(0, ROPE, 2, dtype=F32) / ROPE)
    ang = pos[:, None] * inv_freq[None, :]
    cos, sin, z = jnp.cos(ang), jnp.sin(ang), jnp.zeros((S, 32), F32)
    cc = jnp.concatenate([cos, cos, z, z], axis=1)
    sa = jnp.concatenate([z, sin, z, z], axis=1)
    sb = jnp.concatenate([-sin, z, z, z], axis=1)
    return cc, sa, sb


ATT_SCALE = 1.0 / math.sqrt(NOPE + ROPE)


def _rope_fwd(q, kv, proj, tabs):
    def fn(i, rv, hv, fv):
        qb, kvb, kr, cc, sa, sb = rv
        krr = _rot(kr, cc, sa, sb)
        qs, ks = [], []
        for h in range(H):
            qs += [qb[:, h * HP:h * HP + 128] * ATT_SCALE, _rot(qb[:, h * HP + 128:(h + 1) * HP], cc, sa, sb) * ATT_SCALE]
            ks += [kvb[:, h * 128:(h + 1) * 128], krr]
        kc = jnp.concatenate(ks, axis=1)
        vv = kvb[:, H * NOPE:]
        return [jnp.concatenate(qs, axis=1), kc, kc, vv, vv], []
    cc, sa, sb = tabs
    return _rows(fn, "rope_fwd", 256,
                 [(q, H * HP, 0), (kv, H * 256, 0), (proj, 128, O_KR // 128), (cc, 128, 0), (sa, 128, 0), (sb, 128, 0)],
                 outs=[(H * HP, BF16), (H * HP, BF16), (H * HP, BF16, "T"), (MLA_W, BF16), (MLA_W, BF16, "T")])


TQ, TC, ATT_NB = 512, 512, 1
ATT_KB = TC * ATT_NB
_NT = (((1,), (1,)), ((), ()))


def _attn_allowed(i, kc):
    kpos = kc * TC + lax.broadcasted_iota(jnp.int32, (TC, TQ), 0)
    qpos = i * TQ + lax.broadcasted_iota(jnp.int32, (TC, TQ), 1)
    return (kpos // CHUNK) <= (qpos // CHUNK)


def _attn_fwd(qc, kc, vt):
    def body(q_ref, k_ref, vt_ref, o_ref, l_ref):
        i = pl.program_id(1)
        q = q_ref[...]

        def scores(sb):
            t0s = [pl.multiple_of((sb * ATT_NB + c) * TC, TC) for c in range(ATT_NB)]
            return [lax.dot_general(k_ref[pl.ds(t0, TC), :], q, _NT, preferred_element_type=F32) for t0 in t0s]

        def block(sb, ss, carry, masked):
            m, l, acc = carry
            t0s = [pl.multiple_of((sb * ATT_NB + c) * TC, TC) for c in range(ATT_NB)]
            if masked:
                ss = [jnp.where(_attn_allowed(i, sb * ATT_NB + c), s, -1e30) for c, s in enumerate(ss)]
            m_new = m
            for s in ss:
                m_new = jnp.maximum(m_new, jnp.max(s, axis=0, keepdims=True))
            alpha = jnp.exp(m - m_new)
            ps = [jnp.exp(s - m_new) for s in ss]
            l = alpha * l
            acc = alpha * acc
            for t0, p in zip(t0s, ps):
                l = l + jnp.sum(p, axis=0, keepdims=True)
                acc = acc + jnp.dot(vt_ref[:, pl.ds(t0, TC)], p.astype(BF16), preferred_element_type=F32)
            return m_new, l, acc

        nsb = ((i + 1) * TQ + ATT_KB - 1) // ATT_KB
        c = (jnp.full((1, TQ), -1e30, F32), jnp.zeros((1, TQ), F32), jnp.zeros((VDIM, TQ), F32))

        def step(sb, sc):
            nxt = scores(sb + 1)
            return nxt, block(sb, sc[0], sc[1], False)

        ss, c = lax.fori_loop(0, nsb - 1, step, (scores(0), c))
        m, l, acc = block(nsb - 1, ss, c, True)
        o_ref[...] = (acc / l).T.astype(o_ref.dtype)
        l_ref[...] = m + jnp.log(l)

    return pl.pallas_call(
        body, name="attn_fwd", grid=(H, S // TQ),
        in_specs=[pl.BlockSpec((TQ, HP), lambda h, i: (i, h)),
                  pl.BlockSpec((S, HP), lambda h, i: (0, h)),
                  pl.BlockSpec((VDIM, S), lambda h, i: (h, 0))],
        out_specs=[pl.BlockSpec((TQ, VDIM), lambda h, i: (i, h)), pl.BlockSpec((None, 1, TQ), lambda h, i: (h, 0, i))],
        out_shape=[jax.ShapeDtypeStruct((S, MLA_W), F32), jax.ShapeDtypeStruct((H, 1, S), F32)],
        compiler_params=pltpu.CompilerParams(dimension_semantics=("parallel", "arbitrary"),
                                             vmem_limit_bytes=24 * MIB),
    )(qc, kc, vt)


def _gate_mul_fwd(name, val, proj, width, cb):
    def fn(i, rv, hv, fv):
        o, z = rv
        return [o * _silu(z)], []
    return _rows(fn, name, 256, [(val, width, 0), (proj, width, cb)], outs=[(width, BF16)])[0]


def _conv_fwd(proj, w, b):
    def fn(i, rv, hv, fv):
        (xb,), (halo,), (ww, bb) = rv, hv, fv
        halo = jnp.where(i > 0, halo, 0.0)
        row = lax.broadcasted_iota(jnp.int32, xb.shape, 0)
        acc = bb + ww[3:4] * xb
        for s in range(1, CONV_W):
            acc = acc + ww[3 - s:4 - s] * _shift_down(xb, halo, s, row)
        return [acc], []
    return _rows(fn, "conv_fwd", LRU_TM, [(proj, LRU_W, O_XC // LRU_W)], halos=[(proj, LRU_W, O_XC // LRU_W, "prev")],
                 fulls=[w, b], outs=[(LRU_W, BF16)])[0]


def _lru_terms(ga, gx, xc, ba, bx, lam):
    r = _sig(ga + ba)
    ig = _sig(gx + bx)
    sp = jnp.maximum(-lam, 0.0) + jnp.log(1.0 + jnp.exp(-jnp.abs(lam)))
    log_a = -LRU_C * r * sp
    a = jnp.exp(log_a)
    e2 = jnp.exp(2.0 * log_a)
    om = 1.0 - e2
    mult = jnp.sqrt(jnp.maximum(om, 0.0))
    return r, ig, sp, a, e2, om, mult


def _lru_gates_fwd(gates, xc, ba, bx, lam):
    def fn(i, rv, hv, fv):
        ga, gx, x = rv
        r, ig, sp, a, e2, om, mult = _lru_terms(ga, gx, x, *fv)
        return [a, mult * (ig * x)], []
    return _rows(fn, "lru_gates_fwd", LRU_TM, [(gates, LRU_W, 0), (gates, LRU_W, 1), (xc, LRU_W, 0)],
                 fulls=[ba, bx, lam], outs=[(LRU_W, F32), (LRU_W, F32)])


SCAN_T, SCAN_CW = 64, 256
LRU_TM = 256


def _scan_fwd(a, b):
    def body(a_ref, b_ref, h_ref):
        row = lax.broadcasted_iota(jnp.int32, (SCAN_T, SCAN_CW), 0)

        def step(blk, hc):
            t0 = pl.multiple_of(blk * SCAN_T, SCAN_T)
            A = a_ref[pl.ds(t0, SCAN_T), :]
            B = b_ref[pl.ds(t0, SCAN_T), :]
            d = 1
            while d < SCAN_T:
                keep = row >= d
                A_s = jnp.where(keep, pltpu.roll(A, d, 0), 1.0)
                B_s = jnp.where(keep, pltpu.roll(B, d, 0), 0.0)
                B = A * B_s + B
                A = A * A_s
                d *= 2
            hh = A * hc + B
            h_ref[pl.ds(t0, SCAN_T), :] = hh
            return hh[SCAN_T - 1:SCAN_T, :]

        lax.fori_loop(0, S // SCAN_T, step, jnp.zeros((1, SCAN_CW), F32))

    spec = pl.BlockSpec((S, SCAN_CW), lambda j: (0, j))
    return pl.pallas_call(
        body, name="scan_fwd", grid=(LRU_W // SCAN_CW,), in_specs=[spec, spec], out_specs=spec,
        out_shape=jax.ShapeDtypeStruct((S, LRU_W), F32),
        compiler_params=pltpu.CompilerParams(dimension_semantics=("parallel",),
                                             vmem_limit_bytes=_vmem(3 * _nbytes((S, SCAN_CW), F32))),
    )(a, b)


def _merge_fwd(pa, pb, pc, proj):
    def fn(i, rv, hv, fv):
        a, b, c, ga, gb, gc = rv
        return [_sig(ga) * a + _sig(gb) * b + _sig(gc) * c], []
    return _rows(fn, "merge_fwd", 256,
                 [(pa, D, 0), (pb, D, 0), (pc, D, 0), (proj, D, O_GA // D), (proj, D, O_GB // D), (proj, D, O_GC // D)],
                 outs=[(D, BF16)])[0]


def _post_fwd(x, o2, g):
    def fn(i, rv, hv, fv):
        xb, ob = rv
        return [xb + ob * _rms(ob) * fv[0]], []
    return _rows(fn, "post_fwd", 256, [(x, D, 0), (o2, D, 0)], fulls=[g], outs=[(D, F32)])[0]


SB = 640
BD_TM = 512


def _bd_fwd(xcb, wsb, l):
    def body(x_ref, w_ref, o_ref):
        o_ref[...] = jnp.dot(x_ref[...], w_ref[...], preferred_element_type=F32).astype(o_ref.dtype)

    return pl.pallas_call(
        body, name="lru_gate_mm", grid=(S // BD_TM, 4),
        in_specs=[pl.BlockSpec((BD_TM, SB), lambda i, q: (i, q % 2)),
                  pl.BlockSpec((None, None, SB, SB), lambda i, q: (l, q, 0, 0))],
        out_specs=pl.BlockSpec((BD_TM, SB), lambda i, q: (i, q)),
        out_shape=jax.ShapeDtypeStruct((S, 2 * LRU_W), BF16),
        compiler_params=pltpu.CompilerParams(dimension_semantics=("parallel", "parallel"), vmem_limit_bytes=VMEM_LIMIT),
    )(xcb, wsb)


def _bd_dx(dgates, wsb, l):
    def body(d_ref, w_ref, o_ref, acc_ref):
        g = pl.program_id(2)

        @pl.when(g == 0)
        def _():
            acc_ref[...] = jnp.zeros_like(acc_ref)

        acc_ref[...] += lax.dot_general(d_ref[...], w_ref[...], (((1,), (1,)), ((), ())), preferred_element_type=F32)

        @pl.when(g == 1)
        def _():
            o_ref[...] = acc_ref[...].astype(o_ref.dtype)

    return pl.pallas_call(
        body, name="lru_gate_dx", grid=(S // BD_TM, 2, 2),
        in_specs=[pl.BlockSpec((BD_TM, SB), lambda i, s, g: (i, 2 * g + s)),
                  pl.BlockSpec((None, None, SB, SB), lambda i, s, g: (l, 2 * g + s, 0, 0))],
        out_specs=pl.BlockSpec((BD_TM, SB), lambda i, s, g: (i, s)),
        out_shape=jax.ShapeDtypeStruct((S, LRU_W), BF16),
        scratch_shapes=[pltpu.VMEM((BD_TM, SB), F32)],
        compiler_params=pltpu.CompilerParams(dimension_semantics=("parallel", "parallel", "arbitrary"),
                                             vmem_limit_bytes=VMEM_LIMIT),
    )(dgates, wsb)


def _bd_dw(xcb, dgates):
    tk = 1024

    def body(x_ref, d_ref, o_ref):
        @pl.when(pl.program_id(1) == 0)
        def _():
            o_ref[...] = jnp.zeros_like(o_ref)

        o_ref[...] += lax.dot_general(x_ref[...], d_ref[...], (((0,), (0,)), ((), ())), preferred_element_type=F32)

    return pl.pallas_call(
        body, name="lru_gate_dw", grid=(4, S // tk),
        in_specs=[pl.BlockSpec((tk, SB), lambda q, k: (k, q % 2)), pl.BlockSpec((tk, SB), lambda q, k: (k, q))],
        out_specs=pl.BlockSpec((None, SB, SB), lambda q, k: (q, 0, 0)),
        out_shape=jax.ShapeDtypeStruct((4, SB, SB), F32),
        compiler_params=pltpu.CompilerParams(dimension_semantics=("parallel", "arbitrary"), vmem_limit_bytes=VMEM_LIMIT),
    )(xcb, dgates)


def _bd_extract(dwsb):
    def body(w_ref, o_ref):
        lane = lax.broadcasted_iota(jnp.int32, (LRU_BW, 128), 1)
        for q in range(4):
            for kk in range(8):
                c0 = LRU_BW * kk
                w0, off = (c0 // 128) * 128, c0 % 128
                rows = pl.ds(LRU_BW * kk, LRU_BW)
                blk = w_ref[q, rows, w0:w0 + 128]
                if off:
                    blk = pltpu.roll(blk, 128 - off, 1)
                    if off + LRU_BW > 128:
                        nxt = pltpu.roll(w_ref[q, rows, w0 + 128:w0 + 256], 128 - off, 1)
                        blk = jnp.where(lane < 128 - off, blk, nxt)
                o_ref[q // 2, 8 * (q % 2) + kk] = blk.astype(BF16)

    return pl.pallas_call(
        body, name="lru_gate_dw_blocks",
        in_specs=[pl.BlockSpec(memory_space=pltpu.VMEM)], out_specs=pl.BlockSpec(memory_space=pltpu.VMEM),
        out_shape=jax.ShapeDtypeStruct((2, LRU_NB, LRU_BW, 128), BF16),
        compiler_params=pltpu.CompilerParams(vmem_limit_bytes=VMEM_LIMIT),
    )(dwsb)


def _layer_fwd(x, P, l, tabs, token=None, late=None):
    A = {"x": x}
    A["h"] = _prenorm_fwd(x, P["pre_g"], token)
    proj = A["proj"] = _mm(A["h"], P["wp"], "nt", "in_proj", out_dtype=BF16, tm=1024)
    A["ya"] = _gmlp_fwd(proj, P["ln_g"], P["ln_b"], P["ws"], P["bst"])
    A["xcb"] = _conv_fwd(proj, P["conv_w"], P["conv_b"])
    A["gates"] = _bd_fwd(A["xcb"], P["wsb"], l)
    A["a"], bterm = _lru_gates_fwd(A["gates"], A["xcb"], P["ba"], P["bx"], P["lam"])
    A["hs"] = _scan_fwd(A["a"], bterm)
    A["yc"] = _gate_mul_fwd("yc_fwd", A["hs"], proj, LRU_W, O_ZC // LRU_W)
    if late is not None:
        P.update(late(A["yc"]))
    A["cqn"], A["ckvn"] = _mla_prep_fwd(proj, P["qg"], P["kvg"])
    q = _mm(A["cqn"], P["wuq"], "nt", "q_up", out_dtype=BF16)
    kv = _mm(A["ckvn"], P["wukv"], "nt", "kv_up", out_dtype=BF16)
    A["qc"], A["kc"], A["kct"], A["vv"], vt = _rope_fwd(q, kv, proj, tabs)
    A["o"], A["lse"] = _attn_fwd(A["qc"], A["kc"], vt)
    A["yb"] = _gate_mul_fwd("yb_fwd", A["o"], proj, MLA_W, O_ZB // MLA_W)
    A["pa"] = _mm(A["ya"], P["wpa"], "nn", "proj_a", out_dtype=BF16)
    A["pb"] = _mm(A["yb"], P["wpb"], "nn", "proj_b", out_dtype=BF16)
    A["pc"] = _mm(A["yc"], P["wpc"], "nn", "proj_c", out_dtype=BF16)
    A["merged"] = _merge_fwd(A["pa"], A["pb"], A["pc"], proj)
    A["o2"] = _mm(A["merged"], P["wout"], "nn", "out_proj")
    return _post_fwd(x, A["o2"], P["post_g"]), A


def _loss_fwd(y, tgt):
    def fn(i, rv, hv, fv):
        yb, tb = rv
        e = yb - tb
        part = 0.5 * jnp.sum(jnp.mean(e * e, axis=-1, keepdims=True), axis=0, keepdims=True)
        return [e * (1.0 / D)], [part]
    return _rows(fn, "loss", 256, [(y, D, 0), (tgt, D, 0)], outs=[(D, F32)], accs=[(1, 1)])


def _post_bwd(dxn, o2, g, token=None):
    def fn(i, rv, hv, fv):
        dy, ob = rv
        dx, dg = _rms_bwd(dy, ob, fv[0])
        return [dx], [_colsum(dg)]
    return _rows(fn, "post_bwd", 256, [(dxn, D, 0), (o2, D, 0)], fulls=[g] + ([] if token is None else [token]),
                 outs=[(D, BF16)], accs=[(1, D)])


def _merge_bwd(dm, pa, pb, pc, proj, dproj):
    def fn(i, rv, hv, fv):
        d, a, b, c, ga, gb, gc = rv
        outs_p, outs_g = [], []
        for p, gg in ((a, ga), (b, gb), (c, gc)):
            s = _sig(gg)
            outs_p.append(d * s)
            outs_g.append(d * p * s * (1.0 - s))
        return outs_p + [jnp.concatenate(outs_g, axis=1)], []
    return _rows(fn, "merge_bwd", 256,
                 [(dm, D, 0), (pa, D, 0), (pb, D, 0), (pc, D, 0),
                  (proj, D, O_GA // D), (proj, D, O_GB // D), (proj, D, O_GC // D)],
                 outs=[(D, BF16)] * 3 + [(3 * D, BF16, (dproj, NP, O_GA // (3 * D)))])


def _gmlp_bwd(dya, proj, ln_g, ln_b, ws, bst, dproj):
    gw = GM_W // GM_G

    def fn(i, rv, hv, fv):
        dy, u, v, z = rv
        g, b, w, bt = fv
        vh, rs, vn = _gm_norm(v, g, b)
        sv = _gm_sv(vn, w, bt)
        sz = _silu(z)
        du = dy * sv * sz
        dsv = dy * u * sz
        dz = dy * u * sv * _dsilu(z)
        mask = _gm_mask()
        lane = lax.broadcasted_iota(jnp.int32, (GM_B, 128), 1)
        dvn_parts, dws, dbst = [], [], jnp.zeros((GM_B, 128), F32)
        for k in range(GM_G):
            wm = jnp.where(mask, w[k], 0.0).astype(BF16)
            dsk = dsv[:, k * gw:(k + 1) * gw]
            dskb = dsk.astype(BF16)
            dvn_parts.append(lax.dot_general(wm, dskb, (((0,), (0,)), ((), ())), preferred_element_type=F32))
            dwk = lax.dot_general(dskb, vn[:, k * gw:(k + 1) * gw].astype(BF16), (((1,), (1,)), ((), ())),
                                  preferred_element_type=F32)
            dws.append(jnp.where(mask, dwk, 0.0)[None])
            dbst = dbst + jnp.where(lane == k, jnp.sum(dsk, axis=1, keepdims=True), 0.0)
        dvn = jnp.concatenate(dvn_parts, axis=1)
        dvh = dvn * g
        dv = rs * (dvh - jnp.mean(dvh, axis=-1, keepdims=True) - vh * jnp.mean(dvh * vh, axis=-1, keepdims=True))
        return ([jnp.concatenate([du, dv, dz], axis=1)],
                [jnp.concatenate(dws, axis=0), dbst, _colsum(dvn * vh), _colsum(dvn)])
    return _rows(fn, "gmlp_bwd", GM_B, [(dya, GM_W, 0), (proj, GM_W, 0), (proj, GM_W, 1), (proj, GM_W, 2)],
                 fulls=[ln_g, ln_b, ws, bst], outs=[(3 * GM_W, BF16, (dproj, NP, O_U // (3 * GM_W)))],
                 accs=[(GM_G, GM_B, GM_B), (GM_B, 128), (1, GM_W), (1, GM_W)])


def _yb_bwd(dyb, o, proj, dproj):
    def fn(i, rv, hv, fv):
        dy, ob, z = rv
        do = dy * _silu(z)
        prod = do * ob
        lane = lax.broadcasted_iota(jnp.int32, (dy.shape[0], 128), 1)
        dl = jnp.zeros((dy.shape[0], 128), F32)
        for h in range(H):
            dl = dl + jnp.where(lane == h, jnp.sum(prod[:, h * VDIM:(h + 1) * VDIM], axis=1, keepdims=True), 0.0)
        return [do, dl, dy * ob * _dsilu(z)], []
    return _rows(fn, "yb_bwd", 256, [(dyb, MLA_W, 0), (o, MLA_W, 0), (proj, MLA_W, O_ZB // MLA_W)],
                 outs=[(MLA_W, BF16), (128, F32, "T"), (MLA_W, BF16, (dproj, NP, O_ZB // MLA_W))])


def _attn_bwd(qc, kc, kct, vv, do, lse, dlt):
    def body(q_ref, k_ref, kt_ref, v_ref, do_ref, l_ref, d_ref, dq_ref, dk_ref, dv_ref, dqt_ref):
        h, i = pl.program_id(0), pl.program_id(1)

        @pl.when(i == 0)
        def _():
            dk_ref[...] = jnp.zeros_like(dk_ref)
            dv_ref[...] = jnp.zeros_like(dv_ref)

        q = q_ref[...]
        dob = do_ref[...]
        lse = l_ref[...]
        dl = d_ref[pl.ds(h, 1), :]
        dqt_ref[...] = jnp.zeros_like(dqt_ref)

        def rows_of(sb, c):
            return pl.ds(pl.multiple_of((sb * ATT_NB + c) * TC, TC), TC)

        def front(sb):
            return [(lax.dot_general(k_ref[rows_of(sb, c), :], q, _NT, preferred_element_type=F32),
                     lax.dot_general(v_ref[rows_of(sb, c), :], dob, _NT, preferred_element_type=F32))
                    for c in range(ATT_NB)]

        def block(sb, sd, masked):
            dqt = None
            for c, (s, dp) in enumerate(sd):
                rows = rows_of(sb, c)
                p = jnp.exp(s - lse)
                if masked:
                    p = jnp.where(_attn_allowed(i, sb * ATT_NB + c), p, 0.0)
                ds = (p * (dp - dl)).astype(BF16)
                dk_ref[rows, :] += jnp.dot(ds, q, preferred_element_type=F32)
                dv_ref[rows, :] += jnp.dot(p.astype(BF16), dob, preferred_element_type=F32)
                part = jnp.dot(kt_ref[:, rows], ds, preferred_element_type=F32)
                dqt = part if dqt is None else dqt + part
            dqt_ref[...] += dqt

        def step(sb, sd):
            nxt = front(sb + 1)
            block(sb, sd, False)
            return nxt

        nsb = ((i + 1) * TQ + ATT_KB - 1) // ATT_KB
        sd = lax.fori_loop(0, nsb - 1, step, front(0))
        block(nsb - 1, sd, True)
        dq_ref[...] = dqt_ref[...].T.astype(dq_ref.dtype)

    blk = lambda w: pl.BlockSpec((TQ, w), lambda h, i: (i, h))
    head = lambda w: pl.BlockSpec((S, w), lambda h, i: (0, h))
    return pl.pallas_call(
        body, name="attn_bwd", grid=(H, S // TQ),
        in_specs=[blk(HP), head(HP), pl.BlockSpec((HP, S), lambda h, i: (h, 0)), head(VDIM), blk(VDIM),
                  pl.BlockSpec((None, 1, TQ), lambda h, i: (h, 0, i)), pl.BlockSpec((8, TQ), lambda h, i: (0, i))],
        out_specs=[blk(HP), head(HP), head(VDIM)],
        out_shape=[jax.ShapeDtypeStruct((S, H * HP), BF16), jax.ShapeDtypeStruct((S, H * HP), F32),
                   jax.ShapeDtypeStruct((S, MLA_W), F32)],
        scratch_shapes=[pltpu.VMEM((HP, TQ), F32)],
        compiler_params=pltpu.CompilerParams(dimension_semantics=("parallel", "arbitrary"),
                                             vmem_limit_bytes=28 * MIB),
    )(qc, kc, kct, vv, do, lse, dlt)


def _rope_bwd(dqc, dkc, dvv, tabs):
    def fn(i, rv, hv, fv):
        dq, dk, dv, cc, sa, sb = rv
        qs, ks = [], []
        dkr = jnp.zeros((dq.shape[0], 128), F32)
        for h in range(H):
            qs += [dq[:, h * HP:h * HP + 128] * ATT_SCALE, _rot_t(dq[:, h * HP + 128:(h + 1) * HP], cc, sa, sb) * ATT_SCALE]
            ks.append(dk[:, h * HP:h * HP + 128])
            dkr = dkr + dk[:, h * HP + 128:(h + 1) * HP]
        return [jnp.concatenate(qs, axis=1), jnp.concatenate(ks + [dv], axis=1), _rot_t(dkr, cc, sa, sb)], []
    cc, sa, sb = tabs
    return _rows(fn, "rope_bwd", 256,
                 [(dqc, H * HP, 0), (dkc, H * HP, 0), (dvv, MLA_W, 0), (cc, 128, 0), (sa, 128, 0), (sb, 128, 0)],
                 outs=[(H * HP, BF16), (H * 256, BF16), (128, BF16)])


MLA_GROUP = 1536


def _mla_prep_bwd(dcqn, dckvn, dkr, proj, qg, kvg, dproj):
    def fn(i, rv, hv, fv):
        d1, d2, dk, cq, ckv = rv
        g1, g2 = fv
        dx1, dg1 = _rms_bwd(d1, cq, g1)
        dx2, dg2 = _rms_bwd(d2, ckv, g2)
        zeros = jnp.zeros((d1.shape[0], MLA_GROUP - KVR - 128 - QR), F32)
        return [jnp.concatenate([dx2, dk.astype(F32), dx1, zeros], axis=1)], [_colsum(dg1), _colsum(dg2)]
    return _rows(fn, "mla_prep_bwd", 256,
                 [(dcqn, QR, 0), (dckvn, KVR, 0), (dkr, 128, 0), (proj, QR, O_CQ // QR), (proj, KVR, O_CKV // KVR)],
                 fulls=[qg, kvg], outs=[(MLA_GROUP, BF16, (dproj, NP, O_CKV // MLA_GROUP))], accs=[(1, QR), (1, KVR)])


def _yc_bwd(dyc, hs, proj, dproj):
    def fn(i, rv, hv, fv):
        dy, hh, z = rv
        return [dy * _silu(z), dy * hh * _dsilu(z)], []
    return _rows(fn, "yc_bwd", LRU_TM, [(dyc, LRU_W, 0), (hs, LRU_W, 0), (proj, LRU_W, O_ZC // LRU_W)],
                 outs=[(LRU_W, F32), (LRU_W, BF16, (dproj, NP, O_ZC // LRU_W))])


def _scan_bwd(a, hs, dh):
    nblk = S // SCAN_T

    def body(a_ref, h_ref, dh_ref, da_ref, db_ref):
        row = lax.broadcasted_iota(jnp.int32, (SCAN_T, SCAN_CW), 0)

        def step(j, carry):
            gc, ac = carry
            blk = nblk - 1 - j
            t0 = pl.multiple_of(blk * SCAN_T, SCAN_T)
            av = a_ref[pl.ds(t0, SCAN_T), :]
            A = jnp.where(row < SCAN_T - 1, pltpu.roll(av, SCAN_T - 1, 0), ac)
            B = dh_ref[pl.ds(t0, SCAN_T), :].astype(F32)
            d = 1
            while d < SCAN_T:
                keep = row < SCAN_T - d
                A_s = jnp.where(keep, pltpu.roll(A, SCAN_T - d, 0), 1.0)
                B_s = jnp.where(keep, pltpu.roll(B, SCAN_T - d, 0), 0.0)
                B = A * B_s + B
                A = A * A_s
                d *= 2
            g = A * gc + B
            p0 = pl.multiple_of(jnp.maximum(t0 - 8, 0), 8)
            last = jnp.where(blk > 0, h_ref[pl.ds(p0, 8), :][7:8, :], 0.0)
            h_prev = jnp.where(row >= 1, pltpu.roll(h_ref[pl.ds(t0, SCAN_T), :], 1, 0), last)
            da_ref[pl.ds(t0, SCAN_T), :] = (g * h_prev).astype(da_ref.dtype)
            db_ref[pl.ds(t0, SCAN_T), :] = g.astype(db_ref.dtype)
            return g[0:1, :], av[0:1, :]

        z = jnp.zeros((1, SCAN_CW), F32)
        lax.fori_loop(0, nblk, step, (z, z))

    spec = pl.BlockSpec((S, SCAN_CW), lambda j: (0, j))
    return pl.pallas_call(
        body, name="scan_bwd", grid=(LRU_W // SCAN_CW,), in_specs=[spec] * 3, out_specs=[spec] * 2,
        out_shape=[jax.ShapeDtypeStruct((S, LRU_W), BF16)] * 2,
        compiler_params=pltpu.CompilerParams(dimension_semantics=("parallel",),
                                             vmem_limit_bytes=_vmem(5 * _nbytes((S, SCAN_CW), F32))),
    )(a, hs, dh)


def _lru_gates_bwd(da, db, gates, xc, ba, bx, lam):
    def fn(i, rv, hv, fv):
        dav, dbv, ga, gx, x = rv
        bav, bxv, lamv = fv
        r, ig, sp, a, e2, om, mult = _lru_terms(ga, gx, x, bav, bxv, lamv)
        dmult = dbv * ig * x
        dig = dbv * mult * x
        dxc1 = dbv * mult * ig
        dlog_a = dav * a + jnp.where(om > 0.0, dmult * (-e2 / mult), 0.0)
        dr = dlog_a * (-LRU_C * sp)
        dga = dr * r * (1.0 - r)
        dgx = dig * ig * (1.0 - ig)
        dlam = _colsum(dlog_a * (-LRU_C * r)) * (-_sig(-lamv))
        return [jnp.concatenate([dga, dgx], axis=1), dxc1], [_colsum(dga), _colsum(dgx), dlam]
    return _rows(fn, "lru_gates_bwd", LRU_TM,
                 [(da, LRU_W, 0), (db, LRU_W, 0), (gates, LRU_W, 0), (gates, LRU_W, 1), (xc, LRU_W, 0)],
                 fulls=[ba, bx, lam], outs=[(2 * LRU_W, BF16), (LRU_W, BF16)], accs=[(1, LRU_W)] * 3)


def _conv_bwd(dxc1, dxc2, proj, w, dproj):
    cb = O_XC // LRU_W

    def fn(i, rv, hv, fv):
        d1, d2, xb = rv
        n1, n2, xprev = hv
        ww = fv[0]
        last = i == S // LRU_TM - 1
        dxc = d1 + d2
        nxt = jnp.where(last, 0.0, n1 + n2)
        xprev = jnp.where(i > 0, xprev, 0.0)
        row = lax.broadcasted_iota(jnp.int32, xb.shape, 0)
        dx = ww[3:4] * dxc
        dws = [None] * CONV_W
        dws[3] = _colsum(dxc * xb)
        for s in range(1, CONV_W):
            dx = dx + ww[3 - s:4 - s] * _shift_up(dxc, nxt, s, row)
            dws[3 - s] = _colsum(dxc * _shift_down(xb, xprev, s, row))
        return [dx], [jnp.concatenate(dws, axis=0), _colsum(dxc)]
    return _rows(fn, "conv_bwd", LRU_TM, [(dxc1, LRU_W, 0), (dxc2, LRU_W, 0), (proj, LRU_W, cb)],
                 halos=[(dxc1, LRU_W, 0, "next"), (dxc2, LRU_W, 0, "next"), (proj, LRU_W, cb, "prev")],
                 fulls=[w], outs=[(LRU_W, BF16, (dproj, NP, cb))], accs=[(CONV_W, LRU_W), (1, LRU_W)])


def _prenorm_bwd(dxn, dh, x, g):
    def fn(i, rv, hv, fv):
        dy, dhh, xb = rv
        dx, dg = _rms_bwd(dhh, xb, fv[0])
        return [dy + dx], [_colsum(dg)]
    return _rows(fn, "prenorm_bwd", 256, [(dxn, D, 0), (dh, D, 0), (x, D, 0)], fulls=[g], outs=[(D, F32)],
                 accs=[(1, D)])


def _layer_bwd(dxn, A, P, l, tabs, token=None, early=None):
    G, GB = {}, {}
    proj = A["proj"]

    def dw(key, a, b, name, **tiles):
        GB[key] = _mm(a, b, "tn", name, out_dtype=BF16, **tiles)

    do2, G["post_g"] = _post_bwd(dxn, A["o2"], P["post_g"], token)
    dm = _mm(do2, P["wout"], "nt", "out_proj_dx", out_dtype=BF16)
    dw("wout", A["merged"], do2, "out_proj_dw")
    dpa, dpb, dpc, dproj = _merge_bwd(dm, A["pa"], A["pb"], A["pc"], proj, None)
    dya = _mm(dpa, P["wpa"], "nt", "proj_a_dx", out_dtype=BF16)
    dw("wpa", A["ya"], dpa, "proj_a_dw")
    dyb = _mm(dpb, P["wpb"], "nt", "proj_b_dx", out_dtype=BF16)
    dw("wpb", A["yb"], dpb, "proj_b_dw")
    dyc = _mm(dpc, P["wpc"], "nt", "proj_c_dx", out_dtype=BF16)
    dw("wpc", A["yc"], dpc, "proj_c_dw")
    dproj, G["ws"], G["bst"], G["ln_g"], G["ln_b"] = _gmlp_bwd(dya, proj, P["ln_g"], P["ln_b"], P["ws"], P["bst"], dproj)
    do, dl, dproj = _yb_bwd(dyb, A["o"], proj, dproj)
    dqc, dkc, dvv = _attn_bwd(A["qc"], A["kc"], A["kct"], A["vv"], do, A["lse"], dl)
    dq, dkv, dkr = _rope_bwd(dqc, dkc, dvv, tabs)
    dcqn = _mm(dq, P["wuq"], "nn", "q_up_dx", out_dtype=BF16)
    dw("wuq", dq, A["cqn"], "q_up_dw")
    dckvn = _mm(dkv, P["wukv"], "nn", "kv_up_dx", out_dtype=BF16)
    dw("wukv", dkv, A["ckvn"], "kv_up_dw")
    dproj, G["qg"], G["kvg"] = _mla_prep_bwd(dcqn, dckvn, dkr, proj, P["qg"], P["kvg"], dproj)
    dhs, dproj = _yc_bwd(dyc, A["hs"], proj, dproj)
    da, db = _scan_bwd(A["a"], A["hs"], dhs)
    dgates, dxc1, G["ba"], G["bx"], G["lam"] = _lru_gates_bwd(da, db, A["gates"], A["xcb"], P["ba"], P["bx"], P["lam"])
    dxc2 = _bd_dx(dgates, P["wsb"], l)
    G["wab"] = _bd_extract(_bd_dw(A["xcb"], dgates))
    dproj, G["conv_w"], G["conv_b"] = _conv_bwd(dxc1, dxc2, proj, P["conv_w"], dproj)
    tok = (None, None) if early is None else early(GB)
    dh = _mm(dproj, P["wp"], "nn", "in_proj_dx", tm=1024, tn=1024, token=tok[0])
    dw("wp", dproj, A["h"], "in_proj_dw", tm=1536, tn=1024, token=tok[1])
    dx, G["pre_g"] = _prenorm_bwd(dxn, dh, A["x"], P["pre_g"])
    return dx, G, GB


_ORIG_OFF = [0]
for _s in IN_SIZES:
    _ORIG_OFF.append(_ORIG_OFF[-1] + _s)
_PAD_OFF = {0: O_U, 1: O_V, 2: O_ZA, 3: O_CQ, 4: O_CKV, 5: O_KR, 6: O_ZB, 7: O_XC, 8: O_ZC, 9: O_GA, 10: O_GB, 11: O_GC}
SHARD_IN = N_IN // N_CHIPS


def _pieces_w_in(j):
    lo, hi = SHARD_IN * j, SHARD_IN * (j + 1)
    out = []
    for k in range(len(IN_SIZES)):
        a, b = max(lo, _ORIG_OFF[k]), min(hi, _ORIG_OFF[k + 1])
        if a < b:
            out.append((a - lo, _PAD_OFF[k] + a - _ORIG_OFF[k], b - a))
    return out


def _pieces_uq(j):
    return [(192 * hh, HP * (2 * j + hh), NOPE + ROPE) for hh in range(2)]


def _pieces_ukv(j):
    out = []
    for hh in range(2):
        h = 2 * j + hh
        out += [(256 * hh, NOPE * h, NOPE), (256 * hh + NOPE, H * NOPE + VDIM * h, VDIM)]
    return out


def _pieces_rows(r):
    return lambda j: [(0, r * j, r)]


LAYOUT = {
    "w_in": (SHARD_IN, NP, _pieces_w_in),
    "mla_w_uq": (2 * (NOPE + ROPE), H * HP, _pieces_uq),
    "mla_w_ukv": (2 * (NOPE + VDIM), 2 * H * 128, _pieces_ukv),
    "lru_conv_w": (1, N_CHIPS, _pieces_rows(1)),
    "w_proj_a": (GM_W // N_CHIPS, GM_W, _pieces_rows(GM_W // N_CHIPS)),
    "w_proj_b": (MLA_W // N_CHIPS, MLA_W, _pieces_rows(MLA_W // N_CHIPS)),
    "w_proj_c": (LRU_W // N_CHIPS, LRU_W, _pieces_rows(LRU_W // N_CHIPS)),
    "w_out": (D // N_CHIPS, D, _pieces_rows(D // N_CHIPS)),
}
TRANSPOSED = ("w_in", "mla_w_uq", "mla_w_ukv")


def _superblocks(w_a, w_x):
    w6 = jnp.stack([w_a, w_x], axis=1).reshape(DEPTH, 4, 8, LRU_BW, LRU_BW).astype(BF16)
    bands = [jnp.pad(w6[:, :, k], ((0, 0), (0, 0), (0, 0), (LRU_BW * k, SB - LRU_BW * (k + 1)))) for k in range(8)]
    return jnp.concatenate(bands, axis=2)


_HBM = pl.BlockSpec(memory_space=pltpu.HBM)


def _position():
    return lax.axis_index("x"), lax.axis_index("y"), lax.axis_index("c")


_REL = (2, 1, 3)


def _cut(r):
    return r if r < 32 else (r // 2 + 15) // 16 * 16


def _half_rows(r, c0):
    return _cut(r) if c0 == 0 else r - _cut(r)


def _half_pieces(lay_a, jsrc, c0):
    r = lay_a[0]
    lo, hi = (0, _cut(r)) if c0 == 0 else (_cut(r), r)
    out = []
    for s0, d0, nr in lay_a[2](jsrc):
        a, b = max(s0, lo), min(s0 + nr, hi)
        if a < b:
            out.append((a, d0 + a - s0, b - a))
    return out


PAD_BLOCKS = {"w_in": (64, [(O_KR + ROPE) // 64] + list(range((O_CQ + QR) // 64, O_XC // 64))),
              "mla_w_uq": (64, [(HP * h + NOPE + ROPE) // 64 for h in range(H)])}


def _zero_blocks(buf, rows, blocks, name):
    rest = buf.shape[1:]
    z = (0,) * len(rest)

    def body(ids_ref, buf_ref, o_ref):
        o_ref[...] = jnp.zeros_like(o_ref)

    return pl.pallas_call(
        body, name=name,
        grid_spec=pltpu.PrefetchScalarGridSpec(
            num_scalar_prefetch=1, grid=(len(blocks),), in_specs=[pl.BlockSpec(memory_space=pl.ANY)],
            out_specs=pl.BlockSpec((rows,) + rest, lambda i, ids: (ids[i],) + z)),
        out_shape=jax.ShapeDtypeStruct(buf.shape, buf.dtype), input_output_aliases={1: 0},
    )(jnp.asarray(blocks, jnp.int32), buf)


def _gather_zeros(names, srcs):
    out = []
    for nm, s in zip(names, srcs):
        zone = lax.empty((LAYOUT[nm][1],) + s.shape[1:], s.dtype)
        out.append(_zero_blocks(zone, *PAD_BLOCKS[nm], "zero_pad_" + nm) if nm in PAD_BLOCKS else zone)
    return out


def _weights_allgather(names, srcs, name, carry=()):
    n = len(srcs)
    lay = [LAYOUT[nm] for nm in names]
    zeros = _gather_zeros(names, srcs)
    m = len(carry)

    def body(*refs):
        ins, outs = refs[:n], refs[2 * n + m:3 * n + m]
        send, recv, lsem = refs[3 * n + 2 * m:]
        x, y, c = _position()
        j = 2 * x + y
        sib = (x, y, 1 - c)
        chips = [(1 - x, y), (x, 1 - y), (1 - x, 1 - y)]

        def flow(a, k, jsrc, c0, to, from_src):
            cps = []
            for s0, d0, nr in _half_pieces(lay[a], jsrc, c0):
                dst = outs[a].at[pl.ds(d0, nr)]
                src = ins[a].at[pl.ds(s0, nr)] if from_src else dst
                cps.append(pltpu.make_async_remote_copy(src_ref=src, dst_ref=dst, send_sem=send.at[7 * a + k],
                                                        recv_sem=recv.at[7 * a + k], device_id=to, device_id_type=MESH))
            return cps

        def sized(a, k, rows):
            ref = ins[a].at[pl.ds(0, rows)]
            return pltpu.make_async_remote_copy(src_ref=ref, dst_ref=ref, send_sem=send.at[7 * a + k],
                                                recv_sem=recv.at[7 * a + k], device_id=sib, device_id_type=MESH)

        for j0 in range(N_CHIPS):
            for c0 in range(2):
                @pl.when((j == j0) & (c == c0))
                def _(j0=j0, c0=c0):
                    mine = [_half_rows(lay[a][0], c0) for a in range(n)]
                    theirs = [_half_rows(lay[a][0], 1 - c0) for a in range(n)]
                    for a in range(n):
                        for s0, d0, nr in _half_pieces(lay[a], j0, c0):
                            pltpu.make_async_copy(ins[a].at[pl.ds(s0, nr)], outs[a].at[pl.ds(d0, nr)], lsem.at[a]).start()
                    for a in range(n):
                        for cp in flow(a, 0, j0, c0, sib, True):
                            cp.start()
                        for k, chip in enumerate(chips):
                            for cp in flow(a, 1 + k, j0, c0, (*chip, c), True):
                                cp.start()
                    for k in range(3):
                        for a in range(n):
                            if mine[a]:
                                sized(a, 1 + k, mine[a]).wait_recv()
                                for cp in flow(a, 4 + k, j0 ^ _REL[k], c0, sib, False):
                                    cp.start()
                    for a in range(n):
                        if theirs[a]:
                            sized(a, 0, theirs[a]).wait_recv()
                            for k in range(3):
                                sized(a, 4 + k, theirs[a]).wait_recv()
                    for a in range(n):
                        if mine[a]:
                            for k in range(7):
                                sized(a, k, mine[a]).wait_send()
                            ref = ins[a].at[pl.ds(0, mine[a])]
                            pltpu.make_async_copy(ref, ref, lsem.at[a]).wait()

    res = pl.pallas_call(
        body, name=name,
        out_shape=[jax.ShapeDtypeStruct(z.shape, z.dtype) for z in list(zeros) + list(carry)],
        in_specs=[_HBM] * (2 * n + m), out_specs=[_HBM] * (n + m),
        input_output_aliases={n + a: a for a in range(n + m)},
        scratch_shapes=[pltpu.SemaphoreType.DMA((7 * n,)), pltpu.SemaphoreType.DMA((7 * n,)),
                        pltpu.SemaphoreType.DMA((n,))],
    )(*srcs, *zeros, *carry)
    return res[:n], res[n:]


_SEM = pl.BlockSpec(memory_space=pltpu.SEMAPHORE)
_VMEM_TOKEN = pl.BlockSpec(memory_space=pltpu.VMEM)
_TOKEN = jax.ShapeDtypeStruct((8, 128), F32)
_EFFECT = pltpu.SideEffectType.DATAFLOW_SIDE_EFFECTING


def _gather_start(names, srcs, zeros, name, after=None):
    n = len(srcs)
    lay = [LAYOUT[nm] for nm in names]
    extra = [] if after is None else [after]

    def body(*refs):
        ins, lands = refs[:n], refs[n:2 * n]
        send, recv, lsem = refs[2 * n + len(extra):2 * n + len(extra) + 3]
        refs[-1][...] = jnp.zeros_like(refs[-1])
        x, y, c = _position()
        j = 2 * x + y
        chips = [(1 - x, y), (x, 1 - y), (1 - x, 1 - y)]
        for j0 in range(N_CHIPS):
            @pl.when(j == j0)
            def _(j0=j0):
                for a in range(n):
                    for s0, d0, nr in lay[a][2](j0):
                        src, dst = ins[a].at[pl.ds(s0, nr)], lands[a].at[pl.ds(d0, nr)]
                        pltpu.make_async_copy(src, dst, lsem.at[a]).start()
                        for k, chip in enumerate(chips):
                            pltpu.make_async_remote_copy(src_ref=src, dst_ref=dst, send_sem=send.at[3 * a + k],
                                                         recv_sem=recv.at[3 * a + k], device_id=(*chip, c),
                                                         device_id_type=MESH).start()

    sems = [pltpu.SemaphoreType.DMA((3 * n,)), pltpu.SemaphoreType.DMA((3 * n,)), pltpu.SemaphoreType.DMA((n,))]
    hbm = lambda a: pltpu.HBM(a.shape, a.dtype)
    res = pl.pallas_call(
        body, name=name,
        out_shape=sems + [hbm(s) for s in srcs] + [hbm(z) for z in zeros] + [_TOKEN],
        in_specs=[_HBM] * (2 * n) + [pl.BlockSpec(memory_space=pl.ANY)] * len(extra),
        out_specs=[_SEM] * 3 + [_HBM] * (2 * n) + [_VMEM_TOKEN],
        input_output_aliases={a: 3 + a for a in range(2 * n)},
        compiler_params=pltpu.CompilerParams(has_side_effects=_EFFECT),
    )(*[pltpu.with_memory_space_constraint(s, pltpu.HBM) for s in srcs],
      *[pltpu.with_memory_space_constraint(z, pltpu.HBM) for z in zeros], *extra)
    return res[:3], res[3:3 + n], res[3 + n:3 + 2 * n], res[-1]


def _gather_wait(names, sems, srcs, lands, after, name):
    n = len(srcs)
    lay = [LAYOUT[nm] for nm in names]

    def body(*refs):
        ins, zones = refs[:n], refs[n:2 * n]
        send, recv, lsem = refs[2 * n:2 * n + 3]
        x, y, c = _position()
        for a in range(n):
            whole = zones[a].at[pl.ds(0, lay[a][0])]
            for k in range(3):
                cp = pltpu.make_async_remote_copy(src_ref=ins[a], dst_ref=whole, send_sem=send.at[3 * a + k],
                                                  recv_sem=recv.at[3 * a + k], device_id=(x, y, 1 - c),
                                                  device_id_type=MESH)
                cp.wait_send()
                cp.wait_recv()
            pltpu.make_async_copy(ins[a], whole, lsem.at[a]).wait()

    hbm = lambda a: pltpu.HBM(a.shape, a.dtype)
    res = pl.pallas_call(
        body, name=name,
        out_shape=[hbm(s) for s in srcs] + [hbm(z) for z in lands],
        in_specs=[_HBM] * (2 * n) + [_SEM] * 3 + [pl.BlockSpec(memory_space=pl.ANY)], out_specs=[_HBM] * (2 * n),
        input_output_aliases={a: a for a in range(2 * n)},
        compiler_params=pltpu.CompilerParams(has_side_effects=_EFFECT),
    )(*srcs, *lands, *sems, after)
    return res[n:]


def _clip_pieces(lay_a, jsrc, c0):
    h = lay_a[1] // 2
    lo, hi = c0 * h, (c0 + 1) * h
    out = []
    for s0, d0, nr in lay_a[2](jsrc):
        a, b = max(d0, lo), min(d0 + nr, hi)
        if a < b:
            out.append((s0 + a - d0, a, b - a))
    return out


def _rows_of(pieces):
    return sum(nr for _, _, nr in pieces)


def _both_cores(body_for):
    x, y, c = _position()
    j = 2 * x + y
    for j0 in range(N_CHIPS):
        for c0 in range(2):
            @pl.when((j == j0) & (c == c0))
            def _(j0=j0, c0=c0):
                body_for(j0, c0)


STAGE_ROWS = 512


def _staged_copy(src, dst, buf, sem_in, sem_out, rows):
    ch = buf.shape[0]
    for r in range(0, rows, ch):
        nr = min(ch, rows - r)
        stage = buf.at[pl.ds(0, nr)]
        cin = pltpu.make_async_copy(src.at[pl.ds(r, nr)], stage, sem_in)
        cin.start()
        cin.wait()
        cout = pltpu.make_async_copy(stage, dst.at[pl.ds(r, nr)], sem_out)
        cout.start()
        cout.wait()


def _half_to_sibling(names, gl, name, after=None):
    n = len(gl)
    halves = [LAYOUT[nm][1] // 2 for nm in names]
    extra = [] if after is None else [after]

    def body(*refs):
        ins, outs = refs[:n], refs[n + len(extra):2 * n + len(extra)]
        send, recv = refs[2 * n + len(extra):]
        x, y, c = _position()

        def run(j0, c0):
            cps = [pltpu.make_async_remote_copy(src_ref=ins[a].at[pl.ds((1 - c0) * halves[a], halves[a])], dst_ref=outs[a],
                                                send_sem=send.at[a], recv_sem=recv.at[a], device_id=(x, y, 1 - c),
                                                device_id_type=MESH) for a in range(n)]
            for cp in cps:
                cp.start()
            for cp in cps:
                cp.wait()

        _both_cores(run)

    return pl.pallas_call(
        body, name=name,
        out_shape=[jax.ShapeDtypeStruct((halves[a],) + gl[a].shape[1:], gl[a].dtype) for a in range(n)],
        in_specs=[_HBM] * n + [pl.BlockSpec(memory_space=pl.ANY)] * len(extra), out_specs=[_HBM] * n,
        scratch_shapes=[pltpu.SemaphoreType.DMA((n,)), pltpu.SemaphoreType.DMA((n,))],
    )(*gl, *extra)


def _chip_scatter_half(names, parts, name):
    n = len(parts)
    lay = [LAYOUT[nm] for nm in names]
    zeros = [lax.empty((N_CHIPS, lay[a][0]) + parts[a].shape[1:], parts[a].dtype) for a in range(n)]

    def body(*refs):
        ins, outs = refs[:n], refs[2 * n:3 * n]
        send, recv = refs[3 * n:3 * n + 2]
        stage, sem_in, sem_out = refs[3 * n + 2:4 * n + 2], refs[4 * n + 2], refs[4 * n + 3]
        x, y, c = _position()
        chips = [(1 - x, y), (x, 1 - y), (1 - x, 1 - y)]

        def run(j0, c0):
            def sized(a, rows):
                return outs[a].at[0, pl.ds(0, rows)]

            for a in range(n):
                base = c0 * (lay[a][1] // 2)
                for k, chip in enumerate(chips):
                    for s0, d0, nr in _clip_pieces(lay[a], j0 ^ _REL[k], c0):
                        pltpu.make_async_remote_copy(
                            src_ref=ins[a].at[pl.ds(d0 - base, nr)], dst_ref=outs[a].at[j0, pl.ds(s0, nr)],
                            send_sem=send.at[3 * a + k], recv_sem=recv.at[3 * a + k],
                            device_id=(*chip, c), device_id_type=MESH).start()
            for a in range(n):
                base = c0 * (lay[a][1] // 2)
                for s0, d0, nr in _clip_pieces(lay[a], j0, c0):
                    _staged_copy(ins[a].at[pl.ds(d0 - base, nr)], outs[a].at[j0, pl.ds(s0, nr)], stage[a],
                                 sem_in.at[a], sem_out.at[a], nr)
            for a in range(n):
                got = _rows_of(_clip_pieces(lay[a], j0, c0))
                for k in range(3):
                    sent = _rows_of(_clip_pieces(lay[a], j0 ^ _REL[k], c0))
                    if sent:
                        pltpu.make_async_remote_copy(src_ref=sized(a, sent), dst_ref=sized(a, sent),
                                                     send_sem=send.at[3 * a + k], recv_sem=recv.at[3 * a + k],
                                                     device_id=(x, y, c), device_id_type=MESH).wait_send()
                    if got:
                        pltpu.make_async_remote_copy(src_ref=sized(a, got), dst_ref=sized(a, got),
                                                     send_sem=send.at[3 * a + k], recv_sem=recv.at[3 * a + k],
                                                     device_id=(x, y, c), device_id_type=MESH).wait_recv()

        _both_cores(run)

    return pl.pallas_call(
        body, name=name,
        out_shape=[jax.ShapeDtypeStruct(z.shape, z.dtype) for z in zeros],
        in_specs=[_HBM] * (2 * n), out_specs=[_HBM] * n, input_output_aliases={n + a: a for a in range(n)},
        scratch_shapes=[pltpu.SemaphoreType.DMA((3 * n,)), pltpu.SemaphoreType.DMA((3 * n,))]
        + [pltpu.VMEM((min(STAGE_ROWS, p.shape[0]),) + p.shape[1:], p.dtype) for p in parts]
        + [pltpu.SemaphoreType.DMA((n,)), pltpu.SemaphoreType.DMA((n,))],
    )(*parts, *zeros)


def _subset_exchange(names, bufs, l, name):
    n = len(bufs)
    lay = [LAYOUT[nm] for nm in names]

    def body(*refs):
        outs = refs[n:2 * n]
        send, recv = refs[2 * n:]
        x, y, c = _position()

        def run(j0, c0):
            for a in range(n):
                for s0, _, nr in _clip_pieces(lay[a], j0, c0):
                    rows = outs[a].at[l, pl.ds(s0, nr)]
                    pltpu.make_async_remote_copy(src_ref=rows, dst_ref=rows, send_sem=send.at[a], recv_sem=recv.at[a],
                                                 device_id=(x, y, 1 - c), device_id_type=MESH).start()
            for a in range(n):
                for c_half, wait_send in ((c0, True), (1 - c0, False)):
                    rows = _rows_of(_clip_pieces(lay[a], j0, c_half))
                    if rows:
                        ref = outs[a].at[l, pl.ds(0, rows)]
                        cp = pltpu.make_async_remote_copy(src_ref=ref, dst_ref=ref, send_sem=send.at[a], recv_sem=recv.at[a],
                                                          device_id=(x, y, 1 - c), device_id_type=MESH)
                        if wait_send:
                            cp.wait_send()
                        else:
                            cp.wait_recv()

        _both_cores(run)

    return pl.pallas_call(
        body, name=name,
        out_shape=[jax.ShapeDtypeStruct(b.shape, b.dtype) for b in bufs],
        in_specs=[_HBM] * n, out_specs=[_HBM] * n, input_output_aliases={a: a for a in range(n)},
        scratch_shapes=[pltpu.SemaphoreType.DMA((n,)), pltpu.SemaphoreType.DMA((n,))],
    )(*bufs)


def _scatter_start(names, gl, name):
    n = len(gl)
    lay = [LAYOUT[nm] for nm in names]
    zones = [lax.empty((N_CHIPS, lay[a][0]) + gl[a].shape[1:], gl[a].dtype) for a in range(n)]

    def body(*refs):
        ins, lands = refs[:n], refs[n:2 * n]
        send, recv, lsem = refs[2 * n:2 * n + 3]
        refs[-1][...] = jnp.zeros_like(refs[-1])
        x, y, c = _position()
        j = 2 * x + y
        chips = [(1 - x, y), (x, 1 - y), (1 - x, 1 - y)]
        for j0 in range(N_CHIPS):
            @pl.when(j == j0)
            def _(j0=j0):
                for a in range(n):
                    for s0, d0, nr in lay[a][2](j0):
                        pltpu.make_async_copy(ins[a].at[pl.ds(d0, nr)], lands[a].at[j0, pl.ds(s0, nr)], lsem.at[a]).start()
                    for k, chip in enumerate(chips):
                        for s0, d0, nr in lay[a][2](j0 ^ _REL[k]):
                            pltpu.make_async_remote_copy(
                                src_ref=ins[a].at[pl.ds(d0, nr)], dst_ref=lands[a].at[j0, pl.ds(s0, nr)],
                                send_sem=send.at[3 * a + k], recv_sem=recv.at[3 * a + k],
                                device_id=(*chip, c), device_id_type=MESH).start()

    sems = [pltpu.SemaphoreType.DMA((3 * n,)), pltpu.SemaphoreType.DMA((3 * n,)), pltpu.SemaphoreType.DMA((n,))]
    hbm = lambda a: pltpu.HBM(a.shape, a.dtype)
    res = pl.pallas_call(
        body, name=name,
        out_shape=sems + [hbm(g) for g in gl] + [hbm(z) for z in zones] + [_TOKEN],
        in_specs=[_HBM] * (2 * n), out_specs=[_SEM] * 3 + [_HBM] * (2 * n) + [_VMEM_TOKEN],
        input_output_aliases={a: 3 + a for a in range(2 * n)},
        compiler_params=pltpu.CompilerParams(has_side_effects=_EFFECT),
    )(*[pltpu.with_memory_space_constraint(g, pltpu.HBM) for g in gl],
      *[pltpu.with_memory_space_constraint(z, pltpu.HBM) for z in zones])
    return res[:3], res[3:3 + n], res[3 + n:3 + 2 * n], res[-1]


def _scatter_wait(names, sems, srcs, lands, after, name):
    n = len(srcs)
    lay = [LAYOUT[nm] for nm in names]

    def body(*refs):
        zones = refs[n:2 * n]
        send, recv, lsem = refs[2 * n:2 * n + 3]
        x, y, c = _position()
        for a in range(n):
            whole = zones[a].at[0, pl.ds(0, lay[a][0])]
            for k in range(3):
                cp = pltpu.make_async_remote_copy(src_ref=whole, dst_ref=whole, send_sem=send.at[3 * a + k],
                                                  recv_sem=recv.at[3 * a + k], device_id=(x, y, 1 - c),
                                                  device_id_type=MESH)
                cp.wait_send()
                cp.wait_recv()
            pltpu.make_async_copy(whole, whole, lsem.at[a]).wait()

    hbm = lambda a: pltpu.HBM(a.shape, a.dtype)
    res = pl.pallas_call(
        body, name=name,
        out_shape=[hbm(s) for s in srcs] + [hbm(z) for z in lands],
        in_specs=[_HBM] * (2 * n) + [_SEM] * 3 + [pl.BlockSpec(memory_space=pl.ANY)], out_specs=[_HBM] * (2 * n),
        input_output_aliases={a: a for a in range(2 * n)},
        compiler_params=pltpu.CompilerParams(has_side_effects=_EFFECT),
    )(*srcs, *lands, *sems, after)
    return res[n:]


def _peer(x, y, c, k):
    return (1 - x if k & 4 else x, 1 - y if k & 2 else y, 1 - c if k & 1 else c)


def _bcast_start(arrs, name, after=None):
    n = len(arrs)
    zones = [lax.empty((8,) + a.shape, a.dtype) for a in arrs]
    extra = [] if after is None else [after]

    def body(*refs):
        ins, lands = refs[:n], refs[n:2 * n]
        send, recv, lsem = refs[2 * n + len(extra):2 * n + len(extra) + 3]
        refs[-1][...] = jnp.zeros_like(refs[-1])
        x, y, c = _position()
        for a in range(n):
            dst = lands[a].at[4 * x + 2 * y + c]
            pltpu.make_async_copy(ins[a], dst, lsem.at[a]).start()
            for k in range(1, 8):
                pltpu.make_async_remote_copy(src_ref=ins[a], dst_ref=dst, send_sem=send.at[7 * a + k - 1],
                                             recv_sem=recv.at[7 * a + k - 1], device_id=_peer(x, y, c, k),
                                             device_id_type=MESH).start()

    sems = [pltpu.SemaphoreType.DMA((7 * n,)), pltpu.SemaphoreType.DMA((7 * n,)), pltpu.SemaphoreType.DMA((n,))]
    hbm = lambda a: pltpu.HBM(a.shape, a.dtype)
    res = pl.pallas_call(
        body, name=name,
        out_shape=sems + [hbm(a) for a in arrs] + [hbm(z) for z in zones] + [_TOKEN],
        in_specs=[_HBM] * (2 * n) + [pl.BlockSpec(memory_space=pl.ANY)] * len(extra),
        out_specs=[_SEM] * 3 + [_HBM] * (2 * n) + [_VMEM_TOKEN],
        input_output_aliases={a: 3 + a for a in range(2 * n)},
        compiler_params=pltpu.CompilerParams(has_side_effects=_EFFECT),
    )(*[pltpu.with_memory_space_constraint(a, pltpu.HBM) for a in arrs],
      *[pltpu.with_memory_space_constraint(z, pltpu.HBM) for z in zones], *extra)
    return res[:3], res[3:3 + n], res[3 + n:3 + 2 * n], res[-1]


def _bcast_wait(sems, srcs, lands, after, name):
    n = len(srcs)

    def body(*refs):
        ins, zones = refs[:n], refs[n:2 * n]
        send, recv, lsem = refs[2 * n:2 * n + 3]
        x, y, c = _position()
        for a in range(n):
            for k in range(1, 8):
                cp = pltpu.make_async_remote_copy(src_ref=ins[a], dst_ref=zones[a].at[0], send_sem=send.at[7 * a + k - 1],
                                                  recv_sem=recv.at[7 * a + k - 1], device_id=_peer(x, y, c, k),
                                                  device_id_type=MESH)
                cp.wait_send()
                cp.wait_recv()
            pltpu.make_async_copy(ins[a], zones[a].at[0], lsem.at[a]).wait()

    hbm = lambda a: pltpu.HBM(a.shape, a.dtype)
    res = pl.pallas_call(
        body, name=name,
        out_shape=[hbm(s) for s in srcs] + [hbm(z) for z in lands],
        in_specs=[_HBM] * (2 * n) + [_SEM] * 3 + [pl.BlockSpec(memory_space=pl.ANY)], out_specs=[_HBM] * (2 * n),
        input_output_aliases={a: a for a in range(2 * n)},
        compiler_params=pltpu.CompilerParams(has_side_effects=_EFFECT),
    )(*srcs, *lands, *sems, after)
    return res[n:]


def _swap_start(arrs, name):
    n = len(arrs)
    zones = [lax.empty(a.shape, a.dtype) for a in arrs]

    def body(*refs):
        ins, lands = refs[:n], refs[n:2 * n]
        send, recv = refs[2 * n:2 * n + 2]
        refs[-1][...] = jnp.zeros_like(refs[-1])
        x, y, c = _position()
        for a in range(n):
            pltpu.make_async_remote_copy(src_ref=ins[a], dst_ref=lands[a], send_sem=send.at[a], recv_sem=recv.at[a],
                                         device_id=(x, y, 1 - c), device_id_type=MESH).start()

    sems = [pltpu.SemaphoreType.DMA((n,)), pltpu.SemaphoreType.DMA((n,))]
    hbm = lambda a: pltpu.HBM(a.shape, a.dtype)
    res = pl.pallas_call(
        body, name=name,
        out_shape=sems + [hbm(a) for a in arrs] + [hbm(z) for z in zones] + [_TOKEN],
        in_specs=[_HBM] * (2 * n), out_specs=[_SEM] * 2 + [_HBM] * (2 * n) + [_VMEM_TOKEN],
        input_output_aliases={a: 2 + a for a in range(2 * n)},
        compiler_params=pltpu.CompilerParams(has_side_effects=_EFFECT),
    )(*[pltpu.with_memory_space_constraint(a, pltpu.HBM) for a in arrs],
      *[pltpu.with_memory_space_constraint(z, pltpu.HBM) for z in zones])
    return res[:2], res[2:2 + n], res[2 + n:2 + 2 * n], res[-1]


def _swap_wait(sems, srcs, lands, after, name):
    n = len(srcs)

    def body(*refs):
        ins, zones = refs[:n], refs[n:2 * n]
        send, recv = refs[2 * n:2 * n + 2]
        x, y, c = _position()
        for a in range(n):
            cp = pltpu.make_async_remote_copy(src_ref=ins[a], dst_ref=zones[a], send_sem=send.at[a], recv_sem=recv.at[a],
                                              device_id=(x, y, 1 - c), device_id_type=MESH)
            cp.wait_send()
            cp.wait_recv()

    hbm = lambda a: pltpu.HBM(a.shape, a.dtype)
    res = pl.pallas_call(
        body, name=name,
        out_shape=[hbm(s) for s in srcs] + [hbm(z) for z in lands],
        in_specs=[_HBM] * (2 * n) + [_SEM] * 2 + [pl.BlockSpec(memory_space=pl.ANY)], out_specs=[_HBM] * (2 * n),
        input_output_aliases={a: a for a in range(2 * n)},
        compiler_params=pltpu.CompilerParams(has_side_effects=_EFFECT),
    )(*srcs, *lands, *sems, after)
    return res[:n], res[n:]


def _row_tile(r):
    for t in (256, 128, 64, 32, 16, 8):
        if r % t == 0 and r > t:
            return t
    return r


def _pair_add_half(g, rb, c_arr, name):
    hrows, rest = rb.shape[0], rb.shape[1:]
    tr = _row_tile(hrows)
    nb = hrows // tr
    z = (0,) * len(rest)

    def body(c_ref, g_ref, r_ref, o_ref):
        o_ref[...] = (g_ref[...].astype(F32) + r_ref[...].astype(F32)).astype(o_ref.dtype)

    return pl.pallas_call(
        body, name=name,
        grid_spec=pltpu.PrefetchScalarGridSpec(
            num_scalar_prefetch=1, grid=(nb,),
            in_specs=[pl.BlockSpec((tr,) + rest, lambda i, c_ref: (c_ref[0] * nb + i,) + z),
                      pl.BlockSpec((tr,) + rest, lambda i, c_ref: (i,) + z)],
            out_specs=pl.BlockSpec((tr,) + rest, lambda i, c_ref: (i,) + z)),
        out_shape=jax.ShapeDtypeStruct((hrows,) + rest, BF16),
        compiler_params=pltpu.CompilerParams(dimension_semantics=("parallel",), vmem_limit_bytes=VMEM_LIMIT),
    )(c_arr, g, rb)


def _sum_slabs(slabs, l, buf, name):
    m = len(slabs)
    n, R, rest = slabs[0].shape[0], slabs[0].shape[1], slabs[0].shape[2:]
    tr = _row_tile(R)
    z = (0,) * len(rest)

    def body(*refs):
        total = None
        for r_ref in refs[:m]:
            acc = r_ref[0].astype(F32)
            for k in range(1, n):
                acc = acc + r_ref[k].astype(F32)
            total = acc if total is None else total + acc
        refs[-1][...] = total

    if R // tr > 64 and len(rest) == 1 and rest[0] % 256 == 0:
        grid = (rest[0] // 256,)
        in_spec = pl.BlockSpec((n, R, 256), lambda i: (0, 0, i))
        out_spec = pl.BlockSpec((None, R, 256), lambda i: (l, 0, i))
    else:
        grid = (R // tr,)
        in_spec = pl.BlockSpec((n, tr) + rest, lambda i: (0, i) + z)
        out_spec = pl.BlockSpec((None, tr) + rest, lambda i: (l, i) + z)
    in_specs, args, aliases = [in_spec] * m, list(slabs), {}
    if buf is not None:
        in_specs.append(pl.BlockSpec(memory_space=pl.ANY))
        args.append(buf)
        aliases = {m: 0}
    return pl.pallas_call(
        body, name=name, grid=grid, in_specs=in_specs, out_specs=out_spec,
        out_shape=jax.ShapeDtypeStruct((DEPTH, R) + rest, F32), input_output_aliases=aliases,
        compiler_params=pltpu.CompilerParams(
            dimension_semantics=("parallel",),
            vmem_limit_bytes=_vmem(m * _nbytes(in_spec.block_shape, slabs[0].dtype) + _nbytes(out_spec.block_shape, F32),
                                   2 * _nbytes(out_spec.block_shape, F32))),
    )(*args)


def _adam_math(w, g, m, v):
    mn = ADAM_B1 * m + (1.0 - ADAM_B1) * g
    vn = ADAM_B2 * v + (1.0 - ADAM_B2) * (g * g)
    m_hat = mn / (1.0 - ADAM_B1 ** ADAM_STEP)
    v_hat = vn / (1.0 - ADAM_B2 ** ADAM_STEP)
    return -ADAM_LR * (m_hat / (jnp.sqrt(v_hat) + ADAM_EPS) + ADAM_WD * w), mn, vn


def _adamw(w, g, m, v, name, token=None):
    L, R, C = w.shape
    tr = _row_tile(R)
    extra = [] if token is None else [token]

    def body(w_ref, g_ref, m_ref, v_ref, *rest):
        d_ref, mo_ref, vo_ref = rest[-3:]
        d_ref[...], mo_ref[...], vo_ref[...] = _adam_math(w_ref[...], g_ref[...], m_ref[...], v_ref[...])

    if R // tr > 64 and C % 128 == 0:
        spec, grid = pl.BlockSpec((None, R, 128), lambda l, i: (l, 0, i)), (L, C // 128)
    else:
        spec, grid = pl.BlockSpec((None, tr, C), lambda l, i: (l, i, 0)), (L, R // tr)
    return pl.pallas_call(
        body, name=name, grid=grid, in_specs=[spec] * 4 + [pl.BlockSpec(memory_space=pl.ANY)] * len(extra),
        out_specs=[spec] * 3, out_shape=[jax.ShapeDtypeStruct((L, R, C), F32)] * 3,
        compiler_params=pltpu.CompilerParams(dimension_semantics=("parallel", "parallel"),
                                             vmem_limit_bytes=_vmem(7 * _nbytes(spec.block_shape, F32))),
    )(w, g, m, v, *extra)


_VMEM_WHOLE = pl.BlockSpec(memory_space=pltpu.VMEM)


def _matrix_update(gath, w, m, v, name):
    K = w.shape[1]

    def body(g0_ref, g1_ref, w_ref, m_ref, v_ref, go_ref, d_ref, mo_ref, vo_ref):
        for l, gr in enumerate((g0_ref, g1_ref)):
            for k in range(K):
                g = gr[0, k].astype(F32)
                for dev in range(1, 8):
                    g = g + gr[dev, k].astype(F32)
                go_ref[l, k] = g
                d_ref[l, k], mo_ref[l, k], vo_ref[l, k] = _adam_math(w_ref[l, k], g, m_ref[l, k], v_ref[l, k])

    return pl.pallas_call(
        body, name=name, in_specs=[_VMEM_WHOLE] * 5, out_specs=[_VMEM_WHOLE] * 4,
        out_shape=[jax.ShapeDtypeStruct(w.shape, F32)] * 4,
        compiler_params=pltpu.CompilerParams(vmem_limit_bytes=32 * MIB),
    )(gath[0], gath[1], w, m, v)


VECS = (("pre_norm_g", D), ("post_norm_g", D), ("gm_ln_g", GM_W), ("gm_ln_b", GM_W), ("mla_q_norm_g", QR),
        ("mla_kv_norm_g", KVR), ("lru_conv_b", LRU_W), ("lru_b_a", LRU_W), ("lru_b_x", LRU_W), ("lru_lambda", LRU_W))
VEC_KEY = {"pre_norm_g": "pre_g", "post_norm_g": "post_g", "gm_ln_g": "ln_g", "gm_ln_b": "ln_b", "mla_q_norm_g": "qg",
           "mla_kv_norm_g": "kvg", "lru_conv_b": "conv_b", "lru_b_a": "ba", "lru_b_x": "bx", "lru_lambda": "lam"}
VEC_ROWS, VEC_W, VEC_ROW0, LOSS_ROW = 16, LRU_W, GM_G, 14


def _pack_rows(LG, loss_part):
    per = len(VECS) + 1
    ins = []
    for G in LG:
        ins += [G[VEC_KEY[n]] for n, _ in VECS] + [G["bst"]]
    ins.append(loss_part)

    def body(*refs):
        o_ref = refs[-1]
        o_ref[...] = jnp.zeros_like(o_ref)
        for l in range(DEPTH):
            base = VEC_ROWS * l
            o_ref[pl.ds(base, 8), pl.ds(0, GM_B)] = refs[per * l + len(VECS)][...].T[:8, :]
            for t, (_, width) in enumerate(VECS):
                o_ref[pl.ds(base + VEC_ROW0 + t, 1), pl.ds(0, width)] = refs[per * l + t][...]
        o_ref[pl.ds(LOSS_ROW, 1), pl.ds(0, 128)] = jnp.broadcast_to(refs[-2][...], (1, 128))

    return pl.pallas_call(
        body, name="pack_rows", in_specs=[_VMEM_WHOLE] * len(ins), out_specs=_VMEM_WHOLE,
        out_shape=jax.ShapeDtypeStruct((DEPTH * VEC_ROWS, VEC_W), F32),
    )(*ins)


def _vector_update(gath, W, M, V):
    names = [n for n, _ in VECS] + ["gm_bs"]
    nw = len(names)

    def body(*refs):
        g_ref = refs[0]
        wr, mr, vr = refs[1:1 + nw], refs[1 + nw:1 + 2 * nw], refs[1 + 2 * nw:1 + 3 * nw]
        outs = refs[1 + 3 * nw:]
        s = g_ref[0]
        for dev in range(1, 8):
            s = s + g_ref[dev]
        for t, (_, width) in enumerate(VECS):
            for l in range(DEPTH):
                r = VEC_ROWS * l + VEC_ROW0 + t
                g = s[r:r + 1, :width]
                row = (pl.ds(l, 1), slice(None))
                res = (g,) + _adam_math(wr[t][row], g, mr[t][row], vr[t][row])
                for q in range(4):
                    outs[4 * t + q][row] = res[q]
        t = len(VECS)
        for l in range(DEPTH):
            for k in range(GM_G):
                g = s[VEC_ROWS * l + k:VEC_ROWS * l + k + 1, :GM_B]
                row = (l, pl.ds(k, 1), slice(None))
                res = (g,) + _adam_math(wr[t][row], g, mr[t][row], vr[t][row])
                for q in range(4):
                    outs[4 * t + q][row] = res[q]
        outs[4 * nw][...] = s[LOSS_ROW:LOSS_ROW + 1, :128]

    ws = [W[n] for n in names]
    out_shape = []
    for w in ws:
        out_shape += [jax.ShapeDtypeStruct(w.shape, F32)] * 4
    out_shape.append(jax.ShapeDtypeStruct((1, 128), F32))
    res = pl.pallas_call(
        body, name="vector_update", in_specs=[_VMEM_WHOLE] * (1 + 3 * nw), out_specs=[_VMEM_WHOLE] * (4 * nw + 1),
        out_shape=out_shape, compiler_params=pltpu.CompilerParams(vmem_limit_bytes=VMEM_LIMIT),
    )(gath, *ws, *[M[n] for n in names], *[V[n] for n in names])
    return {n: tuple(res[4 * t:4 * t + 4]) for t, n in enumerate(names)}, res[4 * nw]


SHARDED = ("w_in", "mla_w_uq", "mla_w_ukv", "lru_conv_w", "w_proj_a", "w_proj_b", "w_proj_c", "w_out")
FIRST = ("w_in", "lru_conv_w")
LATER = tuple(n for n in SHARDED if n not in FIRST)
COL_SHARDED = ("w_in", "mla_w_uq", "mla_w_ukv", "lru_conv_w")
SMALL = ("pre_norm_g", "gm_ln_g", "gm_ln_b", "gm_ws", "gm_bs", "mla_q_norm_g", "mla_kv_norm_g", "lru_conv_b",
         "lru_w_a", "lru_b_a", "lru_w_x", "lru_b_x", "lru_lambda", "post_norm_g")
WEIGHTS = ("pre_norm_g", "w_in", "gm_ln_g", "gm_ln_b", "gm_ws", "gm_bs", "mla_q_norm_g", "mla_w_uq",
           "mla_kv_norm_g", "mla_w_ukv", "lru_conv_w", "lru_conv_b", "lru_w_a", "lru_b_a", "lru_w_x", "lru_b_x",
           "lru_lambda", "w_proj_a", "w_proj_b", "w_proj_c", "w_out", "post_norm_g")


GB_KEY = {"w_in": "wp", "mla_w_uq": "wuq", "mla_w_ukv": "wukv", "w_proj_a": "wpa", "w_proj_b": "wpb",
          "w_proj_c": "wpc", "w_out": "wout"}


def _prepare(l, gathered, small, wsb):
    P = {GB_KEY[n]: gathered[n] for n in GB_KEY if n in gathered}
    P["conv_w"] = gathered["lru_conv_w"].transpose(1, 0, 2).reshape(CONV_W, LRU_W)
    P["wsb"] = wsb
    row = lambda n: small[n][l][None, :]
    P["pre_g"], P["post_g"] = row("pre_norm_g"), row("post_norm_g")
    P["ln_g"], P["ln_b"] = row("gm_ln_g"), row("gm_ln_b")
    P["ws"] = small["gm_ws"][l]
    P["bst"] = jnp.pad(small["gm_bs"][l].T, ((0, 0), (0, 128 - GM_G)))
    P["qg"], P["kvg"] = row("mla_q_norm_g"), row("mla_kv_norm_g")
    P["conv_b"], P["ba"], P["bx"], P["lam"] = row("lru_conv_b"), row("lru_b_a"), row("lru_b_x"), row("lru_lambda")
    return P


def kernel(x, pre_norm_g, w_in, gm_ln_g, gm_ln_b, gm_ws, gm_bs, mla_q_norm_g, mla_w_uq, mla_kv_norm_g, mla_w_ukv, lru_conv_w, lru_conv_b, lru_w_a, lru_b_a, lru_w_x, lru_b_x, lru_lambda, w_proj_a, w_proj_b, w_proj_c, w_out, post_norm_g, loss_target, m_pre_norm_g, m_w_in, m_gm_ln_g, m_gm_ln_b, m_gm_ws, m_gm_bs, m_mla_q_norm_g, m_mla_w_uq, m_mla_kv_norm_g, m_mla_w_ukv, m_lru_conv_w, m_lru_conv_b, m_lru_w_a, m_lru_b_a, m_lru_w_x, m_lru_b_x, m_lru_lambda, m_w_proj_a, m_w_proj_b, m_w_proj_c, m_w_out, m_post_norm_g, v_pre_norm_g, v_w_in, v_gm_ln_g, v_gm_ln_b, v_gm_ws, v_gm_bs, v_mla_q_norm_g, v_mla_w_uq, v_mla_kv_norm_g, v_mla_w_ukv, v_lru_conv_w, v_lru_conv_b, v_lru_w_a, v_lru_b_a, v_lru_w_x, v_lru_b_x, v_lru_lambda, v_w_proj_a, v_w_proj_b, v_w_proj_c, v_w_out, v_post_norm_g):
    args = dict(locals())
    W = {n: args[n] for n in WEIGHTS}
    M = {n: args["m_" + n] for n in WEIGHTS}
    V = {n: args["v_" + n] for n in WEIGHTS}
    c = lax.axis_index("c")

    def shards(l, names):
        out = []
        for n in names:
            blk = W[n][l].T if n in TRANSPOSED else W[n][l]
            out.append(blk[None] if n == "lru_conv_w" else blk.astype(BF16))
        return out

    small = {n: W[n] for n in SMALL}
    wsb = _superblocks(W["lru_w_a"], W["lru_w_x"])
    tabs = _rope_tables()
    s0a, s0b, s1a, s1b = shards(0, FIRST), shards(0, LATER), shards(1, FIRST), shards(1, LATER)
    g0, zones = _weights_allgather(FIRST, s0a, "weights_allgather_l0", carry=_gather_zeros(LATER, s0b)
                                   + _gather_zeros(FIRST, s1a) + _gather_zeros(LATER, s1b))
    nl, nf = len(LATER), len(FIRST)
    w0b = _gather_start(LATER, s0b, zones[:nl], "weights_gather_start_l0")
    w1a = _gather_start(FIRST, s1a, zones[nl:nl + nf], "weights_gather_start_l1_first", after=w0b[3])
    w1b = _gather_start(LATER, s1b, zones[nl + nf:], "weights_gather_start_l1_later", after=w1a[3])

    def late(started, name):
        def wait(after):
            got = _gather_wait(LATER, *started[:3], after, name)
            return {GB_KEY[n]: g for n, g in zip(LATER, got)}
        return wait

    P = [_prepare(0, dict(zip(FIRST, g0)), small, wsb), None]
    h0 = x[0]
    h1, A0 = _layer_fwd(h0, P[0], 0, tabs, w1b[3], late(w0b, "weights_gather_wait_l0"))
    g1 = _gather_wait(FIRST, *w1a[:3], h1, "weights_gather_wait_l1_first")
    P[1] = _prepare(1, dict(zip(FIRST, g1)), small, wsb)
    h2, A1 = _layer_fwd(h1, P[1], 1, tabs, None, late(w1b, "weights_gather_wait_l1_later"))
    dy, loss_part = _loss_fwd(h2, loss_target[0])

    def large_grads(G, GB, names):
        conv = G["conv_w"].reshape(CONV_W, N_CHIPS, LRU_W // N_CHIPS).transpose(1, 0, 2)
        return [conv if n == "lru_conv_w" else GB[GB_KEY[n]] for n in names]

    started = {}

    def early1(GB):
        started["sc1b"] = _scatter_start(LATER, [GB[GB_KEY[n]] for n in LATER], "grads_scatter_start_l1_later")
        return started["sc1b"][3], None

    d1, G1, GB1 = _layer_bwd(dy, A1, P[1], 1, tabs, None, early1)
    sc1a = _scatter_start(FIRST, large_grads(G1, GB1, FIRST), "grads_scatter_start_l1_first")

    def early0(GB):
        got_b = _scatter_wait(LATER, *started["sc1b"][:3], GB["wukv"], "grads_scatter_wait_l1_later")
        got_a = _scatter_wait(FIRST, *sc1a[:3], got_b[0], "grads_scatter_wait_l1_first")
        got = dict(zip(LATER + FIRST, list(got_b) + list(got_a)))
        started["swap1"] = _swap_start([got[n] for n in SHARDED], "partials_swap_start_l1")
        started["sc0"] = _scatter_start(LATER, [GB[GB_KEY[n]] for n in LATER], "grads_scatter_start_l0")
        return started["sc0"][3], started["swap1"][3]

    d0, G0, GB0 = _layer_bwd(d1, A0, P[0], 0, tabs, sc1a[3], early0)
    LG = (G0, G1)
    mine0 = _scatter_wait(LATER, *started["sc0"][:3], d0, "grads_scatter_wait_l0")
    swap0 = _swap_start(mine0, "partials_swap_start_l0")
    g0f = large_grads(G0, GB0, FIRST)
    c_arr = jnp.reshape(c, (1,)).astype(jnp.int32)
    from_sib = _half_to_sibling(FIRST, g0f, "grads_half_to_sibling_l0", after=swap0[3])
    pair = [_pair_add_half(g, rb, c_arr, "pair_add_" + n) for n, g, rb in zip(FIRST, g0f, from_sib)]
    slabs = _chip_scatter_half(FIRST, pair, "grads_chip_scatter_l0")
    mats = []
    for g in LG:
        mats += [g["ws"].astype(BF16), g["wab"][0, :, :, :LRU_BW], g["wab"][1, :, :, :LRU_BW]]
    bc = _bcast_start([_pack_rows(LG, loss_part)] + mats, "small_grads_start", after=slabs[0])
    mine1, theirs1 = _swap_wait(*started["swap1"][:3], bc[3], "partials_swap_wait_l1")
    both = dict(zip(SHARDED, [_sum_slabs([a, b], 1, None, "sum_partials_l1_" + n)
                              for n, a, b in zip(SHARDED, mine1, theirs1)]))
    for n, s in zip(FIRST, slabs):
        both[n] = _sum_slabs([s], 0, both[n], "sum_slabs_l0_" + n)
    done = _subset_exchange(FIRST, [both[n] for n in FIRST], 0, "reduced_rows_to_sibling_l0")
    both.update(zip(FIRST, done))
    mine0, theirs0 = _swap_wait(*swap0[:3], done[0], "partials_swap_wait_l0")
    for n, a, b in zip(LATER, mine0, theirs0):
        both[n] = _sum_slabs([a, b], 0, both[n], "sum_partials_l0_" + n)
    both = [both[n] for n in SHARDED]
    grads = {}
    for n, b in zip(SHARDED, both):
        if n in TRANSPOSED and n != "w_in":
            b = jnp.swapaxes(b, 1, 2)
        grads[n] = b if n == "w_in" else b.reshape(W[n].shape)

    upd, last = {}, None
    for n in SHARDED:
        token = bc[3] if n == SHARDED[0] else None
        if n == "w_in":
            tr = lambda a: jnp.swapaxes(a, 1, 2)
            res = _adamw(tr(W[n]), grads[n], tr(M[n]), tr(V[n]), "adamw_" + n, token)
            upd[n] = tuple(tr(a) for a in (grads[n],) + tuple(res))
        else:
            res = _adamw(W[n], grads[n], M[n], V[n], "adamw_" + n, token)
            upd[n] = (grads[n],) + tuple(res)
        last = res[0]

    gath = _bcast_wait(*bc[:3], last, "small_grads_wait")
    vec_upd, loss_row = _vector_update(gath[0], W, M, V)
    upd.update(vec_upd)
    loss = loss_row[0, 0]
    for k, n in enumerate(("gm_ws", "lru_w_a", "lru_w_x")):
        upd[n] = _matrix_update((gath[1 + k], gath[4 + k]), W[n], M[n], V[n], "update_" + n)

    return (loss, d0[None], *[upd[n][0] for n in WEIGHTS], *[upd[n][1] for n in WEIGHTS],
            *[upd[n][2] for n in WEIGHTS], *[upd[n][3] for n in WEIGHTS])
```

```python
import functools
import math

import jax
import jax.numpy as jnp
from jax import lax
from jax.experimental import pallas as pl
from jax.experimental.pallas import tpu as pltpu

F32, BF16 = jnp.float32, jnp.bfloat16
MESH = pl.DeviceIdType.MESH

S, D, DEPTH = 2048, 1024, 2
CHUNK, EPS = 64, 1e-6
GM_W, GM_G, GM_B = 1024, 4, 128
H, NOPE, ROPE, VDIM = 8, 128, 64, 128
QR, KVR = 384, 256
MLA_W = H * VDIM
LRU_W, LRU_NB, LRU_BW, LRU_C, CONV_W = 1280, 16, 80, 8.0, 4
ROPE_THETA = 10000.0
IN_SIZES = (GM_W, GM_W, GM_W, QR, KVR, ROPE, MLA_W, LRU_W, LRU_W, D, D, D)
N_IN = sum(IN_SIZES)
N_CHIPS = 4
ADAM_LR, ADAM_B1, ADAM_B2, ADAM_EPS, ADAM_WD, ADAM_STEP = 0.001, 0.9, 0.999, 1e-08, 0.01, 10

HP = 256
O_U, O_V, O_ZA, O_GA, O_GB, O_GC = 0, 1024, 2048, 3072, 4096, 5120
O_CKV, O_KR, O_CQ, O_XC, O_ZC, O_ZB = 6144, 6400, 6528, 7680, 8960, 10240
NP = 11264
MIB = 1024 * 1024
VMEM_LIMIT = 16 * MIB


def _vmem(block_bytes, temp_bytes=0):
    return int(min(max(2 * block_bytes + temp_bytes + 4 * MIB, VMEM_LIMIT), 56 * MIB))


def _nbytes(shape, dtype):
    return math.prod(d for d in shape if d is not None) * jnp.dtype(dtype).itemsize


def _tile(dim, target):
    if dim <= target:
        return dim
    t = (target // 128) * 128
    while dim % t:
        t -= 128
    return t


def _sig(x):
    return jax.nn.sigmoid(x)


def _silu(x):
    return x * _sig(x)


def _dsilu(x):
    s = _sig(x)
    return s * (1.0 + x * (1.0 - s))


def _mm(a, b, mode, name, out_dtype=F32, tm=1024, tn=1024, tk=1024, b_lead=None, out_lead=None, token=None):
    b2 = b.shape[1:] if b_lead is not None else b.shape
    if mode == "nn":
        (M, K), (K2, N) = a.shape, b2
    elif mode == "nt":
        (M, K), (N, K2) = a.shape, b2
    else:
        (K, M), (K2, N) = a.shape, b2
    assert K == K2, (name, a.shape, b.shape)
    tm, tn, tk = _tile(M, tm), _tile(N, tn), _tile(K, tk)
    nk = K // tk
    if mode == "tn":
        a_spec = pl.BlockSpec((tk, tm), lambda i, j, k: (k, i))
        lhs_c = 0
    else:
        a_spec = pl.BlockSpec((tm, tk), lambda i, j, k: (i, k))
        lhs_c = 1
    b_blk, b_idx, rhs_c = ((tn, tk), (lambda i, j, k: (j, k)), 1) if mode == "nt" else ((tk, tn), (lambda i, j, k: (k, j)), 0)
    if b_lead is None:
        b_spec = pl.BlockSpec(b_blk, b_idx)
    else:
        b_spec = pl.BlockSpec((None,) + b_blk, functools.partial(lambda i, j, k, f, l: (l,) + f(i, j, k), f=b_idx, l=b_lead))
    dims = (((lhs_c,), (rhs_c,)), ((), ()))
    in_specs, args, aliases = [a_spec, b_spec], [a, b], {}
    if out_lead is None:
        out_spec = pl.BlockSpec((tm, tn), lambda i, j, k: (i, j))
        out_shape = jax.ShapeDtypeStruct((M, N), out_dtype)
    else:
        l_out, n_lead, buf = out_lead
        out_spec = pl.BlockSpec((None, tm, tn), functools.partial(lambda i, j, k, l: (l, i, j), l=l_out))
        out_shape = jax.ShapeDtypeStruct((n_lead, M, N), out_dtype)
        if buf is not None:
            in_specs.append(pl.BlockSpec(memory_space=pl.ANY))
            args.append(buf)
            aliases = {2: 0}
    if token is not None:
        in_specs.append(pl.BlockSpec(memory_space=pl.ANY))
        args.append(token)

    def body(a_ref, b_ref, *rest):
        o_ref, acc_ref = rest[-2:]
        k = pl.program_id(2)

        @pl.when(k == 0)
        def _():
            acc_ref[...] = jnp.zeros_like(acc_ref)

        acc_ref[...] += lax.dot_general(a_ref[...].astype(BF16), b_ref[...].astype(BF16), dims,
                                        preferred_element_type=F32)

        @pl.when(k == nk - 1)
        def _():
            o_ref[...] = acc_ref[...].astype(o_ref.dtype)

    return pl.pallas_call(
        body, name=name, grid=(M // tm, N // tn, nk),
        in_specs=in_specs, out_specs=out_spec, out_shape=out_shape,
        scratch_shapes=[pltpu.VMEM((tm, tn), F32)], input_output_aliases=aliases,
        compiler_params=pltpu.CompilerParams(
            dimension_semantics=("parallel", "parallel", "arbitrary"),
            vmem_limit_bytes=_vmem(_nbytes((tm, tk), a.dtype) + _nbytes((tk, tn), b.dtype) + _nbytes((tm, tn), out_dtype),
                                   _nbytes((tm, tn), F32) + _nbytes((tm, tk), BF16) + _nbytes((tk, tn), BF16))),
    )(*args)


def _rows(fn, name, tm, rows, halos=(), fulls=(), outs=(), accs=()):
    n = S // tm
    in_specs, args = [], []
    for arr, w, cb in rows:
        in_specs.append(pl.BlockSpec((tm, w), functools.partial(lambda i, cb: (i, cb), cb=cb)))
        args.append(arr)
    for arr, w, cb, side in halos:
        if side == "prev":
            im = functools.partial(lambda i, cb: (jnp.maximum(i * (tm // 16) - 1, 0), cb), cb=cb)
        else:
            im = functools.partial(lambda i, cb: (jnp.minimum((i + 1) * (tm // 16), S // 16 - 1), cb), cb=cb)
        in_specs.append(pl.BlockSpec((16, w), im))
        args.append(arr)
    for arr in fulls:
        in_specs.append(pl.BlockSpec(arr.shape, functools.partial(lambda i, nd: (0,) * nd, nd=arr.ndim)))
        args.append(arr)
    out_shape, out_specs, aliases, n_alias = [], [], {}, 0
    for k, o in enumerate(outs):
        if len(o) == 3 and o[2] == "T":
            out_shape.append(jax.ShapeDtypeStruct((o[0], S), o[1]))
            out_specs.append(pl.BlockSpec((o[0], tm), lambda i: (0, i)))
        elif len(o) == 3:
            buf, total, cb = o[2]
            out_shape.append(jax.ShapeDtypeStruct((S, total), o[1]))
            out_specs.append(pl.BlockSpec((tm, o[0]), functools.partial(lambda i, cb: (i, cb), cb=cb)))
            if buf is not None:
                aliases[len(args)] = k
                in_specs.append(pl.BlockSpec(memory_space=pl.ANY))
                args.append(buf)
                n_alias += 1
        else:
            out_shape.append(jax.ShapeDtypeStruct((S, o[0]), o[1]))
            out_specs.append(pl.BlockSpec((tm, o[0]), lambda i: (i, 0)))
    for shp in accs:
        out_shape.append(jax.ShapeDtypeStruct(shp, F32))
        out_specs.append(pl.BlockSpec(shp, functools.partial(lambda i, nd: (0,) * nd, nd=len(shp))))
    nr, nh, nf, no, na = len(rows), len(halos), len(fulls), len(outs), len(accs)
    blocks = (sum(_nbytes((tm, w), arr.dtype) for arr, w, _ in rows) + sum(_nbytes(a.shape, a.dtype) for a in fulls)
              + sum(_nbytes((tm, o[0]), o[1]) for o in outs) + sum(_nbytes(shp, F32) for shp in accs))
    widest = _nbytes((tm, max([w for _, w, _ in rows] + [o[0] for o in outs])), F32)

    def body(*refs):
        i = pl.program_id(0)
        ins, orefs = refs[:nr + nh + nf], refs[nr + nh + nf + n_alias:]
        rv = [r[...].astype(F32) for r in ins[:nr]]
        hv = [r[...].astype(F32)[8:] if h[3] == "prev" else r[...].astype(F32)[:8] for r, h in zip(ins[nr:nr + nh], halos)]
        fv = [r[...] for r in ins[nr + nh:]]
        o, a = fn(i, rv, hv, fv)
        assert len(o) == no and len(a) == na, name
        for spec, ref, val in zip(outs, orefs[:no], o):
            ref[...] = (val.T if len(spec) == 3 and spec[2] == "T" else val).astype(ref.dtype)
        if na:
            @pl.when(i == 0)
            def _():
                for ref in orefs[no:]:
                    ref[...] = jnp.zeros_like(ref)

            for ref, val in zip(orefs[no:], a):
                ref[...] += val

    res = pl.pallas_call(
        body, name=name, grid=(n,), in_specs=in_specs, out_specs=out_specs, out_shape=out_shape,
        input_output_aliases=aliases,
        compiler_params=pltpu.CompilerParams(dimension_semantics=("arbitrary",), vmem_limit_bytes=_vmem(blocks, 6 * widest)),
    )(*args)
    return res


def _shift_down(xb, halo, s, row):
    fix = jnp.tile(pltpu.roll(halo, s, 0), (xb.shape[0] // 8, 1))
    return jnp.where(row >= s, pltpu.roll(xb, s, 0), fix)


def _shift_up(xb, halo, s, row):
    tm = xb.shape[0]
    fix = jnp.tile(pltpu.roll(halo, 8 - s, 0), (tm // 8, 1))
    return jnp.where(row < tm - s, pltpu.roll(xb, tm - s, 0), fix)


def _rms(x):
    return lax.rsqrt(jnp.mean(x * x, axis=-1, keepdims=True) + EPS)


def _rms_bwd(dy, x, g):
    r = _rms(x)
    xh = x * r
    dxh = dy * g
    dx = r * (dxh - xh * jnp.mean(dxh * xh, axis=-1, keepdims=True))
    return dx, dy * xh


def _colsum(x):
    return jnp.sum(x, axis=0, keepdims=True)


def _prenorm_fwd(x, g, token=None):
    def fn(i, rv, hv, fv):
        return [rv[0] * _rms(rv[0]) * fv[0]], []
    return _rows(fn, "prenorm_fwd", 256, [(x, D, 0)], fulls=[g] + ([] if token is None else [token]), outs=[(D, BF16)])[0]


def _gm_mask():
    r = lax.broadcasted_iota(jnp.int32, (GM_B, GM_B), 0) // CHUNK
    c = lax.broadcasted_iota(jnp.int32, (GM_B, GM_B), 1) // CHUNK
    return c <= r


def _gm_norm(v, g, b):
    mu = jnp.mean(v, axis=-1, keepdims=True)
    vc = v - mu
    rs = lax.rsqrt(jnp.mean(vc * vc, axis=-1, keepdims=True) + EPS)
    vh = vc * rs
    return vh, rs, vh * g + b


def _gm_sv(vn, ws, bst):
    mask = _gm_mask()
    gw = GM_W // GM_G
    parts = []
    for g in range(GM_G):
        wm = jnp.where(mask, ws[g], 0.0).astype(BF16)
        parts.append(jnp.dot(wm, vn[:, g * gw:(g + 1) * gw].astype(BF16), preferred_element_type=F32)
                     + bst[:, g:g + 1])
    return jnp.concatenate(parts, axis=1)


def _gmlp_fwd(proj, ln_g, ln_b, ws, bst):
    def fn(i, rv, hv, fv):
        u, v, z = rv
        g, b, w, bt = fv
        _, _, vn = _gm_norm(v, g, b)
        return [u * _gm_sv(vn, w, bt) * _silu(z)], []
    return _rows(fn, "gmlp_fwd", GM_B, [(proj, GM_W, 0), (proj, GM_W, 1), (proj, GM_W, 2)],
                 fulls=[ln_g, ln_b, ws, bst], outs=[(GM_W, BF16)])[0]


def _mla_prep_fwd(proj, qg, kvg):
    def fn(i, rv, hv, fv):
        cq, ckv = rv
        g1, g2 = fv
        return [cq * _rms(cq) * g1, ckv * _rms(ckv) * g2], []
    return _rows(fn, "mla_prep_fwd", 256, [(proj, QR, O_CQ // QR), (proj, KVR, O_CKV // KVR)],
                 fulls=[qg, kvg], outs=[(QR, BF16), (KVR, BF16)])


def _rot(t, cc, sa, sb):
    return t * cc + pltpu.roll(t, 32, 1) * sa + pltpu.roll(t, 96, 1) * sb


def _rot_t(g, cc, sa, sb):
    return g * cc + pltpu.roll(g * sa, 96, 1) + pltpu.roll(g * sb, 32, 1)


def _rope_tables():
    pos = jnp.arange(S, dtype=F32)
    inv_freq = ROPE_THETA ** (-jnp.arange(0, ROPE, 2, dtype=F32) / ROPE)
    ang = pos[:, None] * inv_freq[None, :]
    cos, sin, z = jnp.cos(ang), jnp.sin(ang), jnp.zeros((S, 32), F32)
    cc = jnp.concatenate([cos, cos, z, z], axis=1)
    sa = jnp.concatenate([z, sin, z, z], axis=1)
    sb = jnp.concatenate([-sin, z, z, z], axis=1)
    return cc, sa, sb


ATT_SCALE = 1.0 / math.sqrt(NOPE + ROPE)


def _rope_fwd(q, kv, proj, tabs):
    def fn(i, rv, hv, fv):
        qb, kvb, kr, cc, sa, sb = rv
        krr = _rot(kr, cc, sa, sb)
        qs, ks = [], []
        for h in range(H):
            qs += [qb[:, h * HP:h * HP + 128] * ATT_SCALE, _rot(qb[:, h * HP + 128:(h + 1) * HP], cc, sa, sb) * ATT_SCALE]
            ks += [kvb[:, h * 128:(h + 1) * 128], krr]
        kc = jnp.concatenate(ks, axis=1)
        vv = kvb[:, H * NOPE:]
        return [jnp.concatenate(qs, axis=1), kc, kc, vv, vv], []
    cc, sa, sb = tabs
    return _rows(fn, "rope_fwd", 256,
                 [(q, H * HP, 0), (kv, H * 256, 0), (proj, 128, O_KR // 128), (cc, 128, 0), (sa, 128, 0), (sb, 128, 0)],
                 outs=[(H * HP, BF16), (H * HP, BF16), (H * HP, BF16, "T"), (MLA_W, BF16), (MLA_W, BF16, "T")])


TQ, TC, ATT_NB = 512, 512, 1
ATT_KB = TC * ATT_NB
_NT = (((1,), (1,)), ((), ()))


def _attn_allowed(i, kc):
    kpos = kc * TC + lax.broadcasted_iota(jnp.int32, (TC, TQ), 0)
    qpos = i * TQ + lax.broadcasted_iota(jnp.int32, (TC, TQ), 1)
    return (kpos // CHUNK) <= (qpos // CHUNK)


def _attn_fwd(qc, kc, vt):
    def body(q_ref, k_ref, vt_ref, o_ref, l_ref):
        i = pl.program_id(1)
        q = q_ref[...]

        def scores(sb):
            t0s = [pl.multiple_of((sb * ATT_NB + c) * TC, TC) for c in range(ATT_NB)]
            return [lax.dot_general(k_ref[pl.ds(t0, TC), :], q, _NT, preferred_element_type=F32) for t0 in t0s]

        def block(sb, ss, carry, masked):
            m, l, acc = carry
            t0s = [pl.multiple_of((sb * ATT_NB + c) * TC, TC) for c in range(ATT_NB)]
            if masked:
                ss = [jnp.where(_attn_allowed(i, sb * ATT_NB + c), s, -1e30) for c, s in enumerate(ss)]
            m_new = m
            for s in ss:
                m_new = jnp.maximum(m_new, jnp.max(s, axis=0, keepdims=True))
            alpha = jnp.exp(m - m_new)
            ps = [jnp.exp(s - m_new) for s in ss]
            l = alpha * l
            acc = alpha * acc
            for t0, p in zip(t0s, ps):
                l = l + jnp.sum(p, axis=0, keepdims=True)
                acc = acc + jnp.dot(vt_ref[:, pl.ds(t0, TC)], p.astype(BF16), preferred_element_type=F32)
            return m_new, l, acc

        nsb = ((i + 1) * TQ + ATT_KB - 1) // ATT_KB
        c = (jnp.full((1, TQ), -1e30, F32), jnp.zeros((1, TQ), F32), jnp.zeros((VDIM, TQ), F32))

        def step(sb, sc):
            nxt = scores(sb + 1)
            return nxt, block(sb, sc[0], sc[1], False)

        ss, c = lax.fori_loop(0, nsb - 1, step, (scores(0), c))
        m, l, acc = block(nsb - 1, ss, c, True)
        o_ref[...] = (acc / l).T.astype(o_ref.dtype)
        l_ref[...] = m + jnp.log(l)

    return pl.pallas_call(
        body, name="attn_fwd", grid=(H, S // TQ),
        in_specs=[pl.BlockSpec((TQ, HP), lambda h, i: (i, h)),
                  pl.BlockSpec((S, HP), lambda h, i: (0, h)),
                  pl.BlockSpec((VDIM, S), lambda h, i: (h, 0))],
        out_specs=[pl.BlockSpec((TQ, VDIM), lambda h, i: (i, h)), pl.BlockSpec((None, 1, TQ), lambda h, i: (h, 0, i))],
        out_shape=[jax.ShapeDtypeStruct((S, MLA_W), F32), jax.ShapeDtypeStruct((H, 1, S), F32)],
        compiler_params=pltpu.CompilerParams(dimension_semantics=("parallel", "arbitrary"),
                                             vmem_limit_bytes=24 * MIB),
    )(qc, kc, vt)


def _gate_mul_fwd(name, val, proj, width, cb):
    def fn(i, rv, hv, fv):
        o, z = rv
        return [o * _silu(z)], []
    return _rows(fn, name, 256, [(val, width, 0), (proj, width, cb)], outs=[(width, BF16)])[0]


def _conv_fwd(proj, w, b):
    def fn(i, rv, hv, fv):
        (xb,), (halo,), (ww, bb) = rv, hv, fv
        halo = jnp.where(i > 0, halo, 0.0)
        row = lax.broadcasted_iota(jnp.int32, xb.shape, 0)
        acc = bb + ww[3:4] * xb
        for s in range(1, CONV_W):
            acc = acc + ww[3 - s:4 - s] * _shift_down(xb, halo, s, row)
        return [acc], []
    return _rows(fn, "conv_fwd", LRU_TM, [(proj, LRU_W, O_XC // LRU_W)], halos=[(proj, LRU_W, O_XC // LRU_W, "prev")],
                 fulls=[w, b], outs=[(LRU_W, BF16)])[0]


def _lru_terms(ga, gx, xc, ba, bx, lam):
    r = _sig(ga + ba)
    ig = _sig(gx + bx)
    sp = jnp.maximum(-lam, 0.0) + jnp.log(1.0 + jnp.exp(-jnp.abs(lam)))
    log_a = -LRU_C * r * sp
    a = jnp.exp(log_a)
    e2 = jnp.exp(2.0 * log_a)
    om = 1.0 - e2
    mult = jnp.sqrt(jnp.maximum(om, 0.0))
    return r, ig, sp, a, e2, om, mult


def _lru_gates_fwd(gates, xc, ba, bx, lam):
    def fn(i, rv, hv, fv):
        ga, gx, x = rv
        r, ig, sp, a, e2, om, mult = _lru_terms(ga, gx, x, *fv)
        return [a, mult * (ig * x)], []
    return _rows(fn, "lru_gates_fwd", LRU_TM, [(gates, LRU_W, 0), (gates, LRU_W, 1), (xc, LRU_W, 0)],
                 fulls=[ba, bx, lam], outs=[(LRU_W, F32), (LRU_W, F32)])


SCAN_T, SCAN_CW = 64, 256
LRU_TM = 256


def _scan_fwd(a, b):
    def body(a_ref, b_ref, h_ref):
        row = lax.broadcasted_iota(jnp.int32, (SCAN_T, SCAN_CW), 0)

        def step(blk, hc):
            t0 = pl.multiple_of(blk * SCAN_T, SCAN_T)
            A = a_ref[pl.ds(t0, SCAN_T), :]
            B = b_ref[pl.ds(t0, SCAN_T), :]
            d = 1
            while d < SCAN_T:
                keep = row >= d
                A_s = jnp.where(keep, pltpu.roll(A, d, 0), 1.0)
                B_s = jnp.where(keep, pltpu.roll(B, d, 0), 0.0)
                B = A * B_s + B
                A = A * A_s
                d *= 2
            hh = A * hc + B
            h_ref[pl.ds(t0, SCAN_T), :] = hh
            return hh[SCAN_T - 1:SCAN_T, :]

        lax.fori_loop(0, S // SCAN_T, step, jnp.zeros((1, SCAN_CW), F32))

    spec = pl.BlockSpec((S, SCAN_CW), lambda j: (0, j))
    return pl.pallas_call(
        body, name="scan_fwd", grid=(LRU_W // SCAN_CW,), in_specs=[spec, spec], out_specs=spec,
        out_shape=jax.ShapeDtypeStruct((S, LRU_W), F32),
        compiler_params=pltpu.CompilerParams(dimension_semantics=("parallel",),
                                             vmem_limit_bytes=_vmem(3 * _nbytes((S, SCAN_CW), F32))),
    )(a, b)


def _merge_fwd(pa, pb, pc, proj):
    def fn(i, rv, hv, fv):
        a, b, c, ga, gb, gc = rv
        return [_sig(ga) * a + _sig(gb) * b + _sig(gc) * c], []
    return _rows(fn, "merge_fwd", 256,
                 [(pa, D, 0), (pb, D, 0), (pc, D, 0), (proj, D, O_GA // D), (proj, D, O_GB // D), (proj, D, O_GC // D)],
                 outs=[(D, BF16)])[0]


def _post_fwd(x, o2, g, next_g=None):
    def fn(i, rv, hv, fv):
        xb, ob = rv
        xn = xb + ob * _rms(ob) * fv[0]
        return ([xn] if next_g is None else [xn, xn * _rms(xn) * fv[1]]), []
    if next_g is None:
        return _rows(fn, "post_fwd", 256, [(x, D, 0), (o2, D, 0)], fulls=[g], outs=[(D, F32)])[0]
    return _rows(fn, "post_prenorm_fwd", 256, [(x, D, 0), (o2, D, 0)], fulls=[g, next_g], outs=[(D, F32), (D, BF16)])


SB = 640
BD_TM = 512


def _bd_fwd(xcb, wsb, l):
    def body(x_ref, w_ref, o_ref):
        o_ref[...] = jnp.dot(x_ref[...], w_ref[...], preferred_element_type=F32).astype(o_ref.dtype)

    return pl.pallas_call(
        body, name="lru_gate_mm", grid=(S // BD_TM, 4),
        in_specs=[pl.BlockSpec((BD_TM, SB), lambda i, q: (i, q % 2)),
                  pl.BlockSpec((None, None, SB, SB), lambda i, q: (l, q, 0, 0))],
        out_specs=pl.BlockSpec((BD_TM, SB), lambda i, q: (i, q)),
        out_shape=jax.ShapeDtypeStruct((S, 2 * LRU_W), BF16),
        compiler_params=pltpu.CompilerParams(dimension_semantics=("parallel", "parallel"), vmem_limit_bytes=VMEM_LIMIT),
    )(xcb, wsb)


def _bd_dx(dgates, wsb, l):
    def body(d_ref, w_ref, o_ref, acc_ref):
        g = pl.program_id(2)

        @pl.when(g == 0)
        def _():
            acc_ref[...] = jnp.zeros_like(acc_ref)

        acc_ref[...] += lax.dot_general(d_ref[...], w_ref[...], (((1,), (1,)), ((), ())), preferred_element_type=F32)

        @pl.when(g == 1)
        def _():
            o_ref[...] = acc_ref[...].astype(o_ref.dtype)

    return pl.pallas_call(
        body, name="lru_gate_dx", grid=(S // BD_TM, 2, 2),
        in_specs=[pl.BlockSpec((BD_TM, SB), lambda i, s, g: (i, 2 * g + s)),
                  pl.BlockSpec((None, None, SB, SB), lambda i, s, g: (l, 2 * g + s, 0, 0))],
        out_specs=pl.BlockSpec((BD_TM, SB), lambda i, s, g: (i, s)),
        out_shape=jax.ShapeDtypeStruct((S, LRU_W), BF16),
        scratch_shapes=[pltpu.VMEM((BD_TM, SB), F32)],
        compiler_params=pltpu.CompilerParams(dimension_semantics=("parallel", "parallel", "arbitrary"),
                                             vmem_limit_bytes=VMEM_LIMIT),
    )(dgates, wsb)


def _bd_dw(xcb, dgates):
    tk = 1024

    def body(x_ref, d_ref, o_ref):
        @pl.when(pl.program_id(1) == 0)
        def _():
            o_ref[...] = jnp.zeros_like(o_ref)

        o_ref[...] += lax.dot_general(x_ref[...], d_ref[...], (((0,), (0,)), ((), ())), preferred_element_type=F32)

    return pl.pallas_call(
        body, name="lru_gate_dw", grid=(4, S // tk),
        in_specs=[pl.BlockSpec((tk, SB), lambda q, k: (k, q % 2)), pl.BlockSpec((tk, SB), lambda q, k: (k, q))],
        out_specs=pl.BlockSpec((None, SB, SB), lambda q, k: (q, 0, 0)),
        out_shape=jax.ShapeDtypeStruct((4, SB, SB), F32),
        compiler_params=pltpu.CompilerParams(dimension_semantics=("parallel", "arbitrary"), vmem_limit_bytes=VMEM_LIMIT),
    )(xcb, dgates)


def _bd_extract(dwsb):
    def body(w_ref, o_ref):
        lane = lax.broadcasted_iota(jnp.int32, (LRU_BW, 128), 1)
        for q in range(4):
            for kk in range(8):
                c0 = LRU_BW * kk
                w0, off = (c0 // 128) * 128, c0 % 128
                rows = pl.ds(LRU_BW * kk, LRU_BW)
                blk = w_ref[q, rows, w0:w0 + 128]
                if off:
                    blk = pltpu.roll(blk, 128 - off, 1)
                    if off + LRU_BW > 128:
                        nxt = pltpu.roll(w_ref[q, rows, w0 + 128:w0 + 256], 128 - off, 1)
                        blk = jnp.where(lane < 128 - off, blk, nxt)
                o_ref[q // 2, 8 * (q % 2) + kk] = blk.astype(BF16)

    return pl.pallas_call(
        body, name="lru_gate_dw_blocks",
        in_specs=[pl.BlockSpec(memory_space=pltpu.VMEM)], out_specs=pl.BlockSpec(memory_space=pltpu.VMEM),
        out_shape=jax.ShapeDtypeStruct((2, LRU_NB, LRU_BW, 128), BF16),
        compiler_params=pltpu.CompilerParams(vmem_limit_bytes=VMEM_LIMIT),
    )(dwsb)


def _layer_fwd(x, P, l, tabs, token=None, late=None, h=None, next_g=None):
    A = {"x": x}
    A["h"] = _prenorm_fwd(x, P["pre_g"], token) if h is None else h
    proj = A["proj"] = _mm(A["h"], P["wp"], "nt", "in_proj", out_dtype=BF16, tm=1024)
    A["ya"] = _gmlp_fwd(proj, P["ln_g"], P["ln_b"], P["ws"], P["bst"])
    A["xcb"] = _conv_fwd(proj, P["conv_w"], P["conv_b"])
    A["gates"] = _bd_fwd(A["xcb"], P["wsb"], l)
    A["a"], bterm = _lru_gates_fwd(A["gates"], A["xcb"], P["ba"], P["bx"], P["lam"])
    A["hs"] = _scan_fwd(A["a"], bterm)
    A["yc"] = _gate_mul_fwd("yc_fwd", A["hs"], proj, LRU_W, O_ZC // LRU_W)
    if late is not None:
        P.update(late(A["yc"]))
    A["cqn"], A["ckvn"] = _mla_prep_fwd(proj, P["qg"], P["kvg"])
    q = _mm(A["cqn"], P["wuq"], "nt", "q_up", out_dtype=BF16)
    kv = _mm(A["ckvn"], P["wukv"], "nt", "kv_up", out_dtype=BF16)
    A["qc"], A["kc"], A["kct"], A["vv"], vt = _rope_fwd(q, kv, proj, tabs)
    A["o"], A["lse"] = _attn_fwd(A["qc"], A["kc"], vt)
    A["yb"] = _gate_mul_fwd("yb_fwd", A["o"], proj, MLA_W, O_ZB // MLA_W)
    A["pa"] = _mm(A["ya"], P["wpa"], "nn", "proj_a", out_dtype=BF16)
    A["pb"] = _mm(A["yb"], P["wpb"], "nn", "proj_b", out_dtype=BF16)
    A["pc"] = _mm(A["yc"], P["wpc"], "nn", "proj_c", out_dtype=BF16)
    A["merged"] = _merge_fwd(A["pa"], A["pb"], A["pc"], proj)
    A["o2"] = _mm(A["merged"], P["wout"], "nn", "out_proj")
    return _post_fwd(x, A["o2"], P["post_g"], next_g), A


def _loss_fwd(y, tgt):
    def fn(i, rv, hv, fv):
        yb, tb = rv
        e = yb - tb
        part = 0.5 * jnp.sum(jnp.mean(e * e, axis=-1, keepdims=True), axis=0, keepdims=True)
        return [e * (1.0 / D)], [part]
    return _rows(fn, "loss", 256, [(y, D, 0), (tgt, D, 0)], outs=[(D, F32)], accs=[(1, 1)])


def _post_bwd(dxn, o2, g, token=None):
    def fn(i, rv, hv, fv):
        dy, ob = rv
        dx, dg = _rms_bwd(dy, ob, fv[0])
        return [dx], [_colsum(dg)]
    return _rows(fn, "post_bwd", 256, [(dxn, D, 0), (o2, D, 0)], fulls=[g] + ([] if token is None else [token]),
                 outs=[(D, BF16)], accs=[(1, D)])


def _merge_bwd(dm, pa, pb, pc, proj, dproj):
    def fn(i, rv, hv, fv):
        d, a, b, c, ga, gb, gc = rv
        outs_p, outs_g = [], []
        for p, gg in ((a, ga), (b, gb), (c, gc)):
            s = _sig(gg)
            outs_p.append(d * s)
            outs_g.append(d * p * s * (1.0 - s))
        return outs_p + [jnp.concatenate(outs_g, axis=1)], []
    return _rows(fn, "merge_bwd", 256,
                 [(dm, D, 0), (pa, D, 0), (pb, D, 0), (pc, D, 0),
                  (proj, D, O_GA // D), (proj, D, O_GB // D), (proj, D, O_GC // D)],
                 outs=[(D, BF16)] * 3 + [(3 * D, BF16, (dproj, NP, O_GA // (3 * D)))])


def _gmlp_bwd(dya, proj, ln_g, ln_b, ws, bst, dproj):
    gw = GM_W // GM_G

    def fn(i, rv, hv, fv):
        dy, u, v, z = rv
        g, b, w, bt = fv
        vh, rs, vn = _gm_norm(v, g, b)
        sv = _gm_sv(vn, w, bt)
        sz = _silu(z)
        du = dy * sv * sz
        dsv = dy * u * sz
        dz = dy * u * sv * _dsilu(z)
        mask = _gm_mask()
        lane = lax.broadcasted_iota(jnp.int32, (GM_B, 128), 1)
        dvn_parts, dws, dbst = [], [], jnp.zeros((GM_B, 128), F32)
        for k in range(GM_G):
            wm = jnp.where(mask, w[k], 0.0).astype(BF16)
            dsk = dsv[:, k * gw:(k + 1) * gw]
            dskb = dsk.astype(BF16)
            dvn_parts.append(lax.dot_general(wm, dskb, (((0,), (0,)), ((), ())), preferred_element_type=F32))
            dwk = lax.dot_general(dskb, vn[:, k * gw:(k + 1) * gw].astype(BF16), (((1,), (1,)), ((), ())),
                                  preferred_element_type=F32)
            dws.append(jnp.where(mask, dwk, 0.0)[None])
            dbst = dbst + jnp.where(lane == k, jnp.sum(dsk, axis=1, keepdims=True), 0.0)
        dvn = jnp.concatenate(dvn_parts, axis=1)
        dvh = dvn * g
        dv = rs * (dvh - jnp.mean(dvh, axis=-1, keepdims=True) - vh * jnp.mean(dvh * vh, axis=-1, keepdims=True))
        return ([jnp.concatenate([du, dv, dz], axis=1)],
                [jnp.concatenate(dws, axis=0), dbst, _colsum(dvn * vh), _colsum(dvn)])
    return _rows(fn, "gmlp_bwd", GM_B, [(dya, GM_W, 0), (proj, GM_W, 0), (proj, GM_W, 1), (proj, GM_W, 2)],
                 fulls=[ln_g, ln_b, ws, bst], outs=[(3 * GM_W, BF16, (dproj, NP, O_U // (3 * GM_W)))],
                 accs=[(GM_G, GM_B, GM_B), (GM_B, 128), (1, GM_W), (1, GM_W)])


def _yb_bwd(dyb, o, proj, dproj):
    def fn(i, rv, hv, fv):
        dy, ob, z = rv
        do = dy * _silu(z)
        prod = do * ob
        lane = lax.broadcasted_iota(jnp.int32, (dy.shape[0], 128), 1)
        dl = jnp.zeros((dy.shape[0], 128), F32)
        for h in range(H):
            dl = dl + jnp.where(lane == h, jnp.sum(prod[:, h * VDIM:(h + 1) * VDIM], axis=1, keepdims=True), 0.0)
        return [do, dl, dy * ob * _dsilu(z)], []
    return _rows(fn, "yb_bwd", 256, [(dyb, MLA_W, 0), (o, MLA_W, 0), (proj, MLA_W, O_ZB // MLA_W)],
                 outs=[(MLA_W, BF16), (128, F32, "T"), (MLA_W, BF16, (dproj, NP, O_ZB // MLA_W))])


def _attn_bwd(qc, kc, kct, vv, do, lse, dlt):
    def body(q_ref, k_ref, kt_ref, v_ref, do_ref, l_ref, d_ref, dq_ref, dk_ref, dv_ref, dqt_ref):
        h, i = pl.program_id(0), pl.program_id(1)

        @pl.when(i == 0)
        def _():
            dk_ref[...] = jnp.zeros_like(dk_ref)
            dv_ref[...] = jnp.zeros_like(dv_ref)

        q = q_ref[...]
        dob = do_ref[...]
        lse = l_ref[...]
        dl = d_ref[pl.ds(h, 1), :]
        dqt_ref[...] = jnp.zeros_like(dqt_ref)

        def rows_of(sb, c):
            return pl.ds(pl.multiple_of((sb * ATT_NB + c) * TC, TC), TC)

        def front(sb):
            return [(lax.dot_general(k_ref[rows_of(sb, c), :], q, _NT, preferred_element_type=F32),
                     lax.dot_general(v_ref[rows_of(sb, c), :], dob, _NT, preferred_element_type=F32))
                    for c in range(ATT_NB)]

        def block(sb, sd, masked):
            dqt = None
            for c, (s, dp) in enumerate(sd):
                rows = rows_of(sb, c)
                p = jnp.exp(s - lse)
                if masked:
                    p = jnp.where(_attn_allowed(i, sb * ATT_NB + c), p, 0.0)
                ds = (p * (dp - dl)).astype(BF16)
                dk_ref[rows, :] += jnp.dot(ds, q, preferred_element_type=F32)
                dv_ref[rows, :] += jnp.dot(p.astype(BF16), dob, preferred_element_type=F32)
                part = jnp.dot(kt_ref[:, rows], ds, preferred_element_type=F32)
                dqt = part if dqt is None else dqt + part
            dqt_ref[...] += dqt

        def step(sb, sd):
            nxt = front(sb + 1)
            block(sb, sd, False)
            return nxt

        nsb = ((i + 1) * TQ + ATT_KB - 1) // ATT_KB
        sd = lax.fori_loop(0, nsb - 1, step, front(0))
        block(nsb - 1, sd, True)
        dq_ref[...] = dqt_ref[...].T.astype(dq_ref.dtype)

    blk = lambda w: pl.BlockSpec((TQ, w), lambda h, i: (i, h))
    head = lambda w: pl.BlockSpec((S, w), lambda h, i: (0, h))
    return pl.pallas_call(
        body, name="attn_bwd", grid=(H, S // TQ),
        in_specs=[blk(HP), head(HP), pl.BlockSpec((HP, S), lambda h, i: (h, 0)), head(VDIM), blk(VDIM),
                  pl.BlockSpec((None, 1, TQ), lambda h, i: (h, 0, i)), pl.BlockSpec((8, TQ), lambda h, i: (0, i))],
        out_specs=[blk(HP), head(HP), head(VDIM)],
        out_shape=[jax.ShapeDtypeStruct((S, H * HP), BF16), jax.ShapeDtypeStruct((S, H * HP), F32),
                   jax.ShapeDtypeStruct((S, MLA_W), F32)],
        scratch_shapes=[pltpu.VMEM((HP, TQ), F32)],
        compiler_params=pltpu.CompilerParams(dimension_semantics=("parallel", "arbitrary"),
                                             vmem_limit_bytes=28 * MIB),
    )(qc, kc, kct, vv, do, lse, dlt)


def _rope_bwd(dqc, dkc, dvv, tabs):
    def fn(i, rv, hv, fv):
        dq, dk, dv, cc, sa, sb = rv
        qs, ks = [], []
        dkr = jnp.zeros((dq.shape[0], 128), F32)
        for h in range(H):
            qs += [dq[:, h * HP:h * HP + 128] * ATT_SCALE, _rot_t(dq[:, h * HP + 128:(h + 1) * HP], cc, sa, sb) * ATT_SCALE]
            ks.append(dk[:, h * HP:h * HP + 128])
            dkr = dkr + dk[:, h * HP + 128:(h + 1) * HP]
        return [jnp.concatenate(qs, axis=1), jnp.concatenate(ks + [dv], axis=1), _rot_t(dkr, cc, sa, sb)], []
    cc, sa, sb = tabs
    return _rows(fn, "rope_bwd", 256,
                 [(dqc, H * HP, 0), (dkc, H * HP, 0), (dvv, MLA_W, 0), (cc, 128, 0), (sa, 128, 0), (sb, 128, 0)],
                 outs=[(H * HP, BF16), (H * 256, BF16), (128, BF16)])


MLA_GROUP = 1536


def _mla_prep_bwd(dcqn, dckvn, dkr, proj, qg, kvg, dproj):
    def fn(i, rv, hv, fv):
        d1, d2, dk, cq, ckv = rv
        g1, g2 = fv
        dx1, dg1 = _rms_bwd(d1, cq, g1)
        dx2, dg2 = _rms_bwd(d2, ckv, g2)
        zeros = jnp.zeros((d1.shape[0], MLA_GROUP - KVR - 128 - QR), F32)
        return [jnp.concatenate([dx2, dk.astype(F32), dx1, zeros], axis=1)], [_colsum(dg1), _colsum(dg2)]
    return _rows(fn, "mla_prep_bwd", 256,
                 [(dcqn, QR, 0), (dckvn, KVR, 0), (dkr, 128, 0), (proj, QR, O_CQ // QR), (proj, KVR, O_CKV // KVR)],
                 fulls=[qg, kvg], outs=[(MLA_GROUP, BF16, (dproj, NP, O_CKV // MLA_GROUP))], accs=[(1, QR), (1, KVR)])


def _yc_bwd(dyc, hs, proj, dproj):
    def fn(i, rv, hv, fv):
        dy, hh, z = rv
        return [dy * _silu(z), dy * hh * _dsilu(z)], []
    return _rows(fn, "yc_bwd", LRU_TM, [(dyc, LRU_W, 0), (hs, LRU_W, 0), (proj, LRU_W, O_ZC // LRU_W)],
                 outs=[(LRU_W, F32), (LRU_W, BF16, (dproj, NP, O_ZC // LRU_W))])


def _scan_bwd(a, hs, dh):
    nblk = S // SCAN_T

    def body(a_ref, h_ref, dh_ref, da_ref, db_ref):
        row = lax.broadcasted_iota(jnp.int32, (SCAN_T, SCAN_CW), 0)

        def step(j, carry):
            gc, ac = carry
            blk = nblk - 1 - j
            t0 = pl.multiple_of(blk * SCAN_T, SCAN_T)
            av = a_ref[pl.ds(t0, SCAN_T), :]
            A = jnp.where(row < SCAN_T - 1, pltpu.roll(av, SCAN_T - 1, 0), ac)
            B = dh_ref[pl.ds(t0, SCAN_T), :].astype(F32)
            d = 1
            while d < SCAN_T:
                keep = row < SCAN_T - d
                A_s = jnp.where(keep, pltpu.roll(A, SCAN_T - d, 0), 1.0)
                B_s = jnp.where(keep, pltpu.roll(B, SCAN_T - d, 0), 0.0)
                B = A * B_s + B
                A = A * A_s
                d *= 2
            g = A * gc + B
            p0 = pl.multiple_of(jnp.maximum(t0 - 8, 0), 8)
            last = jnp.where(blk > 0, h_ref[pl.ds(p0, 8), :][7:8, :], 0.0)
            h_prev = jnp.where(row >= 1, pltpu.roll(h_ref[pl.ds(t0, SCAN_T), :], 1, 0), last)
            da_ref[pl.ds(t0, SCAN_T), :] = (g * h_prev).astype(da_ref.dtype)
            db_ref[pl.ds(t0, SCAN_T), :] = g.astype(db_ref.dtype)
            return g[0:1, :], av[0:1, :]

        z = jnp.zeros((1, SCAN_CW), F32)
        lax.fori_loop(0, nblk, step, (z, z))

    spec = pl.BlockSpec((S, SCAN_CW), lambda j: (0, j))
    return pl.pallas_call(
        body, name="scan_bwd", grid=(LRU_W // SCAN_CW,), in_specs=[spec] * 3, out_specs=[spec] * 2,
        out_shape=[jax.ShapeDtypeStruct((S, LRU_W), BF16)] * 2,
        compiler_params=pltpu.CompilerParams(dimension_semantics=("parallel",),
                                             vmem_limit_bytes=_vmem(5 * _nbytes((S, SCAN_CW), F32))),
    )(a, hs, dh)


def _lru_gates_bwd(da, db, gates, xc, ba, bx, lam):
    def fn(i, rv, hv, fv):
        dav, dbv, ga, gx, x = rv
        bav, bxv, lamv = fv
        r, ig, sp, a, e2, om, mult = _lru_terms(ga, gx, x, bav, bxv, lamv)
        dmult = dbv * ig * x
        dig = dbv * mult * x
        dxc1 = dbv * mult * ig
        dlog_a = dav * a + jnp.where(om > 0.0, dmult * (-e2 / mult), 0.0)
        dr = dlog_a * (-LRU_C * sp)
        dga = dr * r * (1.0 - r)
        dgx = dig * ig * (1.0 - ig)
        dlam = _colsum(dlog_a * (-LRU_C * r)) * (-_sig(-lamv))
        return [jnp.concatenate([dga, dgx], axis=1), dxc1], [_colsum(dga), _colsum(dgx), dlam]
    return _rows(fn, "lru_gates_bwd", LRU_TM,
                 [(da, LRU_W, 0), (db, LRU_W, 0), (gates, LRU_W, 0), (gates, LRU_W, 1), (xc, LRU_W, 0)],
                 fulls=[ba, bx, lam], outs=[(2 * LRU_W, BF16), (LRU_W, BF16)], accs=[(1, LRU_W)] * 3)


def _conv_bwd(dxc1, dxc2, proj, w, dproj):
    cb = O_XC // LRU_W

    def fn(i, rv, hv, fv):
        d1, d2, xb = rv
        n1, n2, xprev = hv
        ww = fv[0]
        last = i == S // LRU_TM - 1
        dxc = d1 + d2
        nxt = jnp.where(last, 0.0, n1 + n2)
        xprev = jnp.where(i > 0, xprev, 0.0)
        row = lax.broadcasted_iota(jnp.int32, xb.shape, 0)
        dx = ww[3:4] * dxc
        dws = [None] * CONV_W
        dws[3] = _colsum(dxc * xb)
        for s in range(1, CONV_W):
            dx = dx + ww[3 - s:4 - s] * _shift_up(dxc, nxt, s, row)
            dws[3 - s] = _colsum(dxc * _shift_down(xb, xprev, s, row))
        return [dx], [jnp.concatenate(dws, axis=0), _colsum(dxc)]
    return _rows(fn, "conv_bwd", LRU_TM, [(dxc1, LRU_W, 0), (dxc2, LRU_W, 0), (proj, LRU_W, cb)],
                 halos=[(dxc1, LRU_W, 0, "next"), (dxc2, LRU_W, 0, "next"), (proj, LRU_W, cb, "prev")],
                 fulls=[w], outs=[(LRU_W, BF16, (dproj, NP, cb))], accs=[(CONV_W, LRU_W), (1, LRU_W)])


def _prenorm_bwd(dxn, dh, x, g):
    def fn(i, rv, hv, fv):
        dy, dhh, xb = rv
        dx, dg = _rms_bwd(dhh, xb, fv[0])
        return [dy + dx], [_colsum(dg)]
    return _rows(fn, "prenorm_bwd", 256, [(dxn, D, 0), (dh, D, 0), (x, D, 0)], fulls=[g], outs=[(D, F32)],
                 accs=[(1, D)])


def _layer_bwd(dxn, A, P, l, tabs, token=None, early=None):
    G, GB = {}, {}
    proj = A["proj"]

    def dw(key, a, b, name, **tiles):
        GB[key] = _mm(a, b, "tn", name, out_dtype=BF16, **tiles)

    do2, G["post_g"] = _post_bwd(dxn, A["o2"], P["post_g"], token)
    dm = _mm(do2, P["wout"], "nt", "out_proj_dx", out_dtype=BF16)
    dw("wout", A["merged"], do2, "out_proj_dw")
    dpa, dpb, dpc, dproj = _merge_bwd(dm, A["pa"], A["pb"], A["pc"], proj, None)
    dya = _mm(dpa, P["wpa"], "nt", "proj_a_dx", out_dtype=BF16)
    dw("wpa", A["ya"], dpa, "proj_a_dw")
    dyb = _mm(dpb, P["wpb"], "nt", "proj_b_dx", out_dtype=BF16)
    dw("wpb", A["yb"], dpb, "proj_b_dw")
    dyc = _mm(dpc, P["wpc"], "nt", "proj_c_dx", out_dtype=BF16)
    dw("wpc", A["yc"], dpc, "proj_c_dw")
    dproj, G["ws"], G["bst"], G["ln_g"], G["ln_b"] = _gmlp_bwd(dya, proj, P["ln_g"], P["ln_b"], P["ws"], P["bst"], dproj)
    do, dl, dproj = _yb_bwd(dyb, A["o"], proj, dproj)
    dqc, dkc, dvv = _attn_bwd(A["qc"], A["kc"], A["kct"], A["vv"], do, A["lse"], dl)
    dq, dkv, dkr = _rope_bwd(dqc, dkc, dvv, tabs)
    dcqn = _mm(dq, P["wuq"], "nn", "q_up_dx", out_dtype=BF16)
    dw("wuq", dq, A["cqn"], "q_up_dw")
    dckvn = _mm(dkv, P["wukv"], "nn", "kv_up_dx", out_dtype=BF16)
    dw("wukv", dkv, A["ckvn"], "kv_up_dw")
    dproj, G["qg"], G["kvg"] = _mla_prep_bwd(dcqn, dckvn, dkr, proj, P["qg"], P["kvg"], dproj)
    dhs, dproj = _yc_bwd(dyc, A["hs"], proj, dproj)
    da, db = _scan_bwd(A["a"], A["hs"], dhs)
    dgates, dxc1, G["ba"], G["bx"], G["lam"] = _lru_gates_bwd(da, db, A["gates"], A["xcb"], P["ba"], P["bx"], P["lam"])
    dxc2 = _bd_dx(dgates, P["wsb"], l)
    G["wab"] = _bd_extract(_bd_dw(A["xcb"], dgates))
    dproj, G["conv_w"], G["conv_b"] = _conv_bwd(dxc1, dxc2, proj, P["conv_w"], dproj)
    tok = (None, None) if early is None else early(GB)
    dh = _mm(dproj, P["wp"], "nn", "in_proj_dx", tm=1024, tn=1024, token=tok[0])
    dw("wp", dproj, A["h"], "in_proj_dw", tm=1536, tn=1024, token=tok[1])
    dx, G["pre_g"] = _prenorm_bwd(dxn, dh, A["x"], P["pre_g"])
    return dx, G, GB


_ORIG_OFF = [0]
for _s in IN_SIZES:
    _ORIG_OFF.append(_ORIG_OFF[-1] + _s)
_PAD_OFF = {0: O_U, 1: O_V, 2: O_ZA, 3: O_CQ, 4: O_CKV, 5: O_KR, 6: O_ZB, 7: O_XC, 8: O_ZC, 9: O_GA, 10: O_GB, 11: O_GC}
SHARD_IN = N_IN // N_CHIPS


def _pieces_w_in(j):
    lo, hi = SHARD_IN * j, SHARD_IN * (j + 1)
    out = []
    for k in range(len(IN_SIZES)):
        a, b = max(lo, _ORIG_OFF[k]), min(hi, _ORIG_OFF[k + 1])
        if a < b:
            out.append((a - lo, _PAD_OFF[k] + a - _ORIG_OFF[k], b - a))
    return out


def _pieces_uq(j):
    return [(192 * hh, HP * (2 * j + hh), NOPE + ROPE) for hh in range(2)]


def _pieces_ukv(j):
    out = []
    for hh in range(2):
        h = 2 * j + hh
        out += [(256 * hh, NOPE * h, NOPE), (256 * hh + NOPE, H * NOPE + VDIM * h, VDIM)]
    return out


def _pieces_rows(r):
    return lambda j: [(0, r * j, r)]


LAYOUT = {
    "w_in": (SHARD_IN, NP, _pieces_w_in),
    "mla_w_uq": (2 * (NOPE + ROPE), H * HP, _pieces_uq),
    "mla_w_ukv": (2 * (NOPE + VDIM), 2 * H * 128, _pieces_ukv),
    "lru_conv_w": (1, N_CHIPS, _pieces_rows(1)),
    "w_proj_a": (GM_W // N_CHIPS, GM_W, _pieces_rows(GM_W // N_CHIPS)),
    "w_proj_b": (MLA_W // N_CHIPS, MLA_W, _pieces_rows(MLA_W // N_CHIPS)),
    "w_proj_c": (LRU_W // N_CHIPS, LRU_W, _pieces_rows(LRU_W // N_CHIPS)),
    "w_out": (D // N_CHIPS, D, _pieces_rows(D // N_CHIPS)),
}
TRANSPOSED = ("w_in", "mla_w_uq", "mla_w_ukv")


def _superblocks(w_a, w_x):
    w6 = jnp.stack([w_a, w_x], axis=1).reshape(DEPTH, 4, 8, LRU_BW, LRU_BW).astype(BF16)
    bands = [jnp.pad(w6[:, :, k], ((0, 0), (0, 0), (0, 0), (LRU_BW * k, SB - LRU_BW * (k + 1)))) for k in range(8)]
    return jnp.concatenate(bands, axis=2)


_HBM = pl.BlockSpec(memory_space=pltpu.HBM)


def _position():
    return lax.axis_index("x"), lax.axis_index("y"), lax.axis_index("c")


_REL = (2, 1, 3)


def _cut(r):
    return r if r < 32 else (r // 2 + 15) // 16 * 16


def _half_rows(r, c0):
    return _cut(r) if c0 == 0 else r - _cut(r)


def _half_pieces(lay_a, jsrc, c0):
    r = lay_a[0]
    lo, hi = (0, _cut(r)) if c0 == 0 else (_cut(r), r)
    out = []
    for s0, d0, nr in lay_a[2](jsrc):
        a, b = max(s0, lo), min(s0 + nr, hi)
        if a < b:
            out.append((a, d0 + a - s0, b - a))
    return out


PAD_BLOCKS = {"w_in": (64, [(O_KR + ROPE) // 64] + list(range((O_CQ + QR) // 64, O_XC // 64))),
              "mla_w_uq": (64, [(HP * h + NOPE + ROPE) // 64 for h in range(H)])}


def _zero_blocks(buf, rows, blocks, name):
    rest = buf.shape[1:]
    z = (0,) * len(rest)

    def body(ids_ref, buf_ref, o_ref):
        o_ref[...] = jnp.zeros_like(o_ref)

    return pl.pallas_call(
        body, name=name,
        grid_spec=pltpu.PrefetchScalarGridSpec(
            num_scalar_prefetch=1, grid=(len(blocks),), in_specs=[pl.BlockSpec(memory_space=pl.ANY)],
            out_specs=pl.BlockSpec((rows,) + rest, lambda i, ids: (ids[i],) + z)),
        out_shape=jax.ShapeDtypeStruct(buf.shape, buf.dtype), input_output_aliases={1: 0},
    )(jnp.asarray(blocks, jnp.int32), buf)


def _gather_zeros(names, srcs):
    out = []
    for nm, s in zip(names, srcs):
        zone = lax.empty((LAYOUT[nm][1],) + s.shape[1:], s.dtype)
        out.append(_zero_blocks(zone, *PAD_BLOCKS[nm], "zero_pad_" + nm) if nm in PAD_BLOCKS else zone)
    return out


def _weights_allgather(names, srcs, name, carry=()):
    n = len(srcs)
    lay = [LAYOUT[nm] for nm in names]
    zeros = _gather_zeros(names, srcs)
    m = len(carry)

    def body(*refs):
        ins, outs = refs[:n], refs[2 * n + m:3 * n + m]
        send, recv, lsem = refs[3 * n + 2 * m:]
        x, y, c = _position()
        j = 2 * x + y
        sib = (x, y, 1 - c)
        chips = [(1 - x, y), (x, 1 - y), (1 - x, 1 - y)]

        def flow(a, k, jsrc, c0, to, from_src):
            cps = []
            for s0, d0, nr in _half_pieces(lay[a], jsrc, c0):
                dst = outs[a].at[pl.ds(d0, nr)]
                src = ins[a].at[pl.ds(s0, nr)] if from_src else dst
                cps.append(pltpu.make_async_remote_copy(src_ref=src, dst_ref=dst, send_sem=send.at[7 * a + k],
                                                        recv_sem=recv.at[7 * a + k], device_id=to, device_id_type=MESH))
            return cps

        def sized(a, k, rows):
            ref = ins[a].at[pl.ds(0, rows)]
            return pltpu.make_async_remote_copy(src_ref=ref, dst_ref=ref, send_sem=send.at[7 * a + k],
                                                recv_sem=recv.at[7 * a + k], device_id=sib, device_id_type=MESH)

        for j0 in range(N_CHIPS):
            for c0 in range(2):
                @pl.when((j == j0) & (c == c0))
                def _(j0=j0, c0=c0):
                    mine = [_half_rows(lay[a][0], c0) for a in range(n)]
                    theirs = [_half_rows(lay[a][0], 1 - c0) for a in range(n)]
                    for a in range(n):
                        for s0, d0, nr in _half_pieces(lay[a], j0, c0):
                            pltpu.make_async_copy(ins[a].at[pl.ds(s0, nr)], outs[a].at[pl.ds(d0, nr)], lsem.at[a]).start()
                    for a in range(n):
                        for cp in flow(a, 0, j0, c0, sib, True):
                            cp.start()
                        for k, chip in enumerate(chips):
                            for cp in flow(a, 1 + k, j0, c0, (*chip, c), True):
                                cp.start()
                    for k in range(3):
                        for a in range(n):
                            if mine[a]:
                                sized(a, 1 + k, mine[a]).wait_recv()
                                for cp in flow(a, 4 + k, j0 ^ _REL[k], c0, sib, False):
                                    cp.start()
                    for a in range(n):
                        if theirs[a]:
                            sized(a, 0, theirs[a]).wait_recv()
                            for k in range(3):
                                sized(a, 4 + k, theirs[a]).wait_recv()
                    for a in range(n):
                        if mine[a]:
                            for k in range(7):
                                sized(a, k, mine[a]).wait_send()
                            ref = ins[a].at[pl.ds(0, mine[a])]
                            pltpu.make_async_copy(ref, ref, lsem.at[a]).wait()

    res = pl.pallas_call(
        body, name=name,
        out_shape=[jax.ShapeDtypeStruct(z.shape, z.dtype) for z in list(zeros) + list(carry)],
        in_specs=[_HBM] * (2 * n + m), out_specs=[_HBM] * (n + m),
        input_output_aliases={n + a: a for a in range(n + m)},
        scratch_shapes=[pltpu.SemaphoreType.DMA((7 * n,)), pltpu.SemaphoreType.DMA((7 * n,)),
                        pltpu.SemaphoreType.DMA((n,))],
    )(*srcs, *zeros, *carry)
    return res[:n], res[n:]


_SEM = pl.BlockSpec(memory_space=pltpu.SEMAPHORE)
_VMEM_TOKEN = pl.BlockSpec(memory_space=pltpu.VMEM)
_TOKEN = jax.ShapeDtypeStruct((8, 128), F32)
_EFFECT = pltpu.SideEffectType.DATAFLOW_SIDE_EFFECTING


def _gather_start(names, srcs, zeros, name, after=None):
    n = len(srcs)
    lay = [LAYOUT[nm] for nm in names]
    extra = [] if after is None else [after]

    def body(*refs):
        ins, lands = refs[:n], refs[n:2 * n]
        send, recv, lsem = refs[2 * n + len(extra):2 * n + len(extra) + 3]
        refs[-1][...] = jnp.zeros_like(refs[-1])
        x, y, c = _position()
        j = 2 * x + y
        chips = [(1 - x, y), (x, 1 - y), (1 - x, 1 - y)]
        for j0 in range(N_CHIPS):
            @pl.when(j == j0)
            def _(j0=j0):
                for a in range(n):
                    for s0, d0, nr in lay[a][2](j0):
                        src, dst = ins[a].at[pl.ds(s0, nr)], lands[a].at[pl.ds(d0, nr)]
                        pltpu.make_async_copy(src, dst, lsem.at[a]).start()
                        for k, chip in enumerate(chips):
                            pltpu.make_async_remote_copy(src_ref=src, dst_ref=dst, send_sem=send.at[3 * a + k],
                                                         recv_sem=recv.at[3 * a + k], device_id=(*chip, c),
                                                         device_id_type=MESH).start()

    sems = [pltpu.SemaphoreType.DMA((3 * n,)), pltpu.SemaphoreType.DMA((3 * n,)), pltpu.SemaphoreType.DMA((n,))]
    hbm = lambda a: pltpu.HBM(a.shape, a.dtype)
    res = pl.pallas_call(
        body, name=name,
        out_shape=sems + [hbm(s) for s in srcs] + [hbm(z) for z in zeros] + [_TOKEN],
        in_specs=[_HBM] * (2 * n) + [pl.BlockSpec(memory_space=pl.ANY)] * len(extra),
        out_specs=[_SEM] * 3 + [_HBM] * (2 * n) + [_VMEM_TOKEN],
        input_output_aliases={a: 3 + a for a in range(2 * n)},
        compiler_params=pltpu.CompilerParams(has_side_effects=_EFFECT),
    )(*[pltpu.with_memory_space_constraint(s, pltpu.HBM) for s in srcs],
      *[pltpu.with_memory_space_constraint(z, pltpu.HBM) for z in zeros], *extra)
    return res[:3], res[3:3 + n], res[3 + n:3 + 2 * n], res[-1]


def _gather_wait(names, sems, srcs, lands, after, name):
    n = len(srcs)
    lay = [LAYOUT[nm] for nm in names]

    def body(*refs):
        ins, zones = refs[:n], refs[n:2 * n]
        send, recv, lsem = refs[2 * n:2 * n + 3]
        x, y, c = _position()
        for a in range(n):
            whole = zones[a].at[pl.ds(0, lay[a][0])]
            for k in range(3):
                cp = pltpu.make_async_remote_copy(src_ref=ins[a], dst_ref=whole, send_sem=send.at[3 * a + k],
                                                  recv_sem=recv.at[3 * a + k], device_id=(x, y, 1 - c),
                                                  device_id_type=MESH)
                cp.wait_send()
                cp.wait_recv()
            pltpu.make_async_copy(ins[a], whole, lsem.at[a]).wait()

    hbm = lambda a: pltpu.HBM(a.shape, a.dtype)
    res = pl.pallas_call(
        body, name=name,
        out_shape=[hbm(s) for s in srcs] + [hbm(z) for z in lands],
        in_specs=[_HBM] * (2 * n) + [_SEM] * 3 + [pl.BlockSpec(memory_space=pl.ANY)], out_specs=[_HBM] * (2 * n),
        input_output_aliases={a: a for a in range(2 * n)},
        compiler_params=pltpu.CompilerParams(has_side_effects=_EFFECT),
    )(*srcs, *lands, *sems, after)
    return res[n:]


def _clip_pieces(lay_a, jsrc, c0):
    h = lay_a[1] // 2
    lo, hi = c0 * h, (c0 + 1) * h
    out = []
    for s0, d0, nr in lay_a[2](jsrc):
        a, b = max(d0, lo), min(d0 + nr, hi)
        if a < b:
            out.append((s0 + a - d0, a, b - a))
    return out


def _rows_of(pieces):
    return sum(nr for _, _, nr in pieces)


def _both_cores(body_for):
    x, y, c = _position()
    j = 2 * x + y
    for j0 in range(N_CHIPS):
        for c0 in range(2):
            @pl.when((j == j0) & (c == c0))
            def _(j0=j0, c0=c0):
                body_for(j0, c0)


STAGE_ROWS = 512


def _staged_copy(src, dst, buf, sem_in, sem_out, rows):
    ch = buf.shape[0]
    for r in range(0, rows, ch):
        nr = min(ch, rows - r)
        stage = buf.at[pl.ds(0, nr)]
        cin = pltpu.make_async_copy(src.at[pl.ds(r, nr)], stage, sem_in)
        cin.start()
        cin.wait()
        cout = pltpu.make_async_copy(stage, dst.at[pl.ds(r, nr)], sem_out)
        cout.start()
        cout.wait()


def _half_to_sibling(names, gl, name, after=None):
    n = len(gl)
    halves = [LAYOUT[nm][1] // 2 for nm in names]
    extra = [] if after is None else [after]

    def body(*refs):
        ins, outs = refs[:n], refs[n + len(extra):2 * n + len(extra)]
        send, recv = refs[2 * n + len(extra):]
        x, y, c = _position()

        def run(j0, c0):
            cps = [pltpu.make_async_remote_copy(src_ref=ins[a].at[pl.ds((1 - c0) * halves[a], halves[a])], dst_ref=outs[a],
                                                send_sem=send.at[a], recv_sem=recv.at[a], device_id=(x, y, 1 - c),
                                                device_id_type=MESH) for a in range(n)]
            for cp in cps:
                cp.start()
            for cp in cps:
                cp.wait()

        _both_cores(run)

    return pl.pallas_call(
        body, name=name,
        out_shape=[jax.ShapeDtypeStruct((halves[a],) + gl[a].shape[1:], gl[a].dtype) for a in range(n)],
        in_specs=[_HBM] * n + [pl.BlockSpec(memory_space=pl.ANY)] * len(extra), out_specs=[_HBM] * n,
        scratch_shapes=[pltpu.SemaphoreType.DMA((n,)), pltpu.SemaphoreType.DMA((n,))],
    )(*gl, *extra)


def _chip_scatter_half(names, parts, name):
    n = len(parts)
    lay = [LAYOUT[nm] for nm in names]
    zeros = [lax.empty((N_CHIPS, lay[a][0]) + parts[a].shape[1:], parts[a].dtype) for a in range(n)]

    def body(*refs):
        ins, outs = refs[:n], refs[2 * n:3 * n]
        send, recv = refs[3 * n:3 * n + 2]
        stage, sem_in, sem_out = refs[3 * n + 2:4 * n + 2], refs[4 * n + 2], refs[4 * n + 3]
        x, y, c = _position()
        chips = [(1 - x, y), (x, 1 - y), (1 - x, 1 - y)]

        def run(j0, c0):
            def sized(a, rows):
                return outs[a].at[0, pl.ds(0, rows)]

            for a in range(n):
                base = c0 * (lay[a][1] // 2)
                for k, chip in enumerate(chips):
                    for s0, d0, nr in _clip_pieces(lay[a], j0 ^ _REL[k], c0):
                        pltpu.make_async_remote_copy(
                            src_ref=ins[a].at[pl.ds(d0 - base, nr)], dst_ref=outs[a].at[j0, pl.ds(s0, nr)],
                            send_sem=send.at[3 * a + k], recv_sem=recv.at[3 * a + k],
                            device_id=(*chip, c), device_id_type=MESH).start()
            for a in range(n):
                base = c0 * (lay[a][1] // 2)
                for s0, d0, nr in _clip_pieces(lay[a], j0, c0):
                    _staged_copy(ins[a].at[pl.ds(d0 - base, nr)], outs[a].at[j0, pl.ds(s0, nr)], stage[a],
                                 sem_in.at[a], sem_out.at[a], nr)
            for a in range(n):
                got = _rows_of(_clip_pieces(lay[a], j0, c0))
                for k in range(3):
                    sent = _rows_of(_clip_pieces(lay[a], j0 ^ _REL[k], c0))
                    if sent:
                        pltpu.make_async_remote_copy(src_ref=sized(a, sent), dst_ref=sized(a, sent),
                                                     send_sem=send.at[3 * a + k], recv_sem=recv.at[3 * a + k],
                                                     device_id=(x, y, c), device_id_type=MESH).wait_send()
                    if got:
                        pltpu.make_async_remote_copy(src_ref=sized(a, got), dst_ref=sized(a, got),
                                                     send_sem=send.at[3 * a + k], recv_sem=recv.at[3 * a + k],
                                                     device_id=(x, y, c), device_id_type=MESH).wait_recv()

        _both_cores(run)

    return pl.pallas_call(
        body, name=name,
        out_shape=[jax.ShapeDtypeStruct(z.shape, z.dtype) for z in zeros],
        in_specs=[_HBM] * (2 * n), out_specs=[_HBM] * n, input_output_aliases={n + a: a for a in range(n)},
        scratch_shapes=[pltpu.SemaphoreType.DMA((3 * n,)), pltpu.SemaphoreType.DMA((3 * n,))]
        + [pltpu.VMEM((min(STAGE_ROWS, p.shape[0]),) + p.shape[1:], p.dtype) for p in parts]
        + [pltpu.SemaphoreType.DMA((n,)), pltpu.SemaphoreType.DMA((n,))],
    )(*parts, *zeros)


def _subset_exchange(names, bufs, l, name):
    n = len(bufs)
    lay = [LAYOUT[nm] for nm in names]

    def body(*refs):
        outs = refs[n:2 * n]
        send, recv = refs[2 * n:]
        x, y, c = _position()

        def run(j0, c0):
            for a in range(n):
                for s0, _, nr in _clip_pieces(lay[a], j0, c0):
                    rows = outs[a].at[l, pl.ds(s0, nr)]
                    pltpu.make_async_remote_copy(src_ref=rows, dst_ref=rows, send_sem=send.at[a], recv_sem=recv.at[a],
                                                 device_id=(x, y, 1 - c), device_id_type=MESH).start()
            for a in range(n):
                for c_half, wait_send in ((c0, True), (1 - c0, False)):
                    rows = _rows_of(_clip_pieces(lay[a], j0, c_half))
                    if rows:
                        ref = outs[a].at[l, pl.ds(0, rows)]
                        cp = pltpu.make_async_remote_copy(src_ref=ref, dst_ref=ref, send_sem=send.at[a], recv_sem=recv.at[a],
                                                          device_id=(x, y, 1 - c), device_id_type=MESH)
                        if wait_send:
                            cp.wait_send()
                        else:
                            cp.wait_recv()

        _both_cores(run)

    return pl.pallas_call(
        body, name=name,
        out_shape=[jax.ShapeDtypeStruct(b.shape, b.dtype) for b in bufs],
        in_specs=[_HBM] * n, out_specs=[_HBM] * n, input_output_aliases={a: a for a in range(n)},
        scratch_shapes=[pltpu.SemaphoreType.DMA((n,)), pltpu.SemaphoreType.DMA((n,))],
    )(*bufs)


def _scatter_start(names, gl, name):
    n = len(gl)
    lay = [LAYOUT[nm] for nm in names]
    zones = [lax.empty((N_CHIPS, lay[a][0]) + gl[a].shape[1:], gl[a].dtype) for a in range(n)]

    def body(*refs):
        ins, lands = refs[:n], refs[n:2 * n]
        send, recv, lsem = refs[2 * n:2 * n + 3]
        refs[-1][...] = jnp.zeros_like(refs[-1])
        x, y, c = _position()
        j = 2 * x + y
        chips = [(1 - x, y), (x, 1 - y), (1 - x, 1 - y)]
        for j0 in range(N_CHIPS):
            @pl.when(j == j0)
            def _(j0=j0):
                for a in range(n):
                    for s0, d0, nr in lay[a][2](j0):
                        pltpu.make_async_copy(ins[a].at[pl.ds(d0, nr)], lands[a].at[j0, pl.ds(s0, nr)], lsem.at[a]).start()
                    for k, chip in enumerate(chips):
                        for s0, d0, nr in lay[a][2](j0 ^ _REL[k]):
                            pltpu.make_async_remote_copy(
                                src_ref=ins[a].at[pl.ds(d0, nr)], dst_ref=lands[a].at[j0, pl.ds(s0, nr)],
                                send_sem=send.at[3 * a + k], recv_sem=recv.at[3 * a + k],
                                device_id=(*chip, c), device_id_type=MESH).start()

    sems = [pltpu.SemaphoreType.DMA((3 * n,)), pltpu.SemaphoreType.DMA((3 * n,)), pltpu.SemaphoreType.DMA((n,))]
    hbm = lambda a: pltpu.HBM(a.shape, a.dtype)
    res = pl.pallas_call(
        body, name=name,
        out_shape=sems + [hbm(g) for g in gl] + [hbm(z) for z in zones] + [_TOKEN],
        in_specs=[_HBM] * (2 * n), out_specs=[_SEM] * 3 + [_HBM] * (2 * n) + [_VMEM_TOKEN],
        input_output_aliases={a: 3 + a for a in range(2 * n)},
        compiler_params=pltpu.CompilerParams(has_side_effects=_EFFECT),
    )(*[pltpu.with_memory_space_constraint(g, pltpu.HBM) for g in gl],
      *[pltpu.with_memory_space_constraint(z, pltpu.HBM) for z in zones])
    return res[:3], res[3:3 + n], res[3 + n:3 + 2 * n], res[-1]


def _scatter_wait(names, sems, srcs, lands, after, name):
    n = len(srcs)
    lay = [LAYOUT[nm] for nm in names]

    def body(*refs):
        zones = refs[n:2 * n]
        send, recv, lsem = refs[2 * n:2 * n + 3]
        x, y, c = _position()
        for a in range(n):
            whole = zones[a].at[0, pl.ds(0, lay[a][0])]
            for k in range(3):
                cp = pltpu.make_async_remote_copy(src_ref=whole, dst_ref=whole, send_sem=send.at[3 * a + k],
                                                  recv_sem=recv.at[3 * a + k], device_id=(x, y, 1 - c),
                                                  device_id_type=MESH)
                cp.wait_send()
                cp.wait_recv()
            pltpu.make_async_copy(whole, whole, lsem.at[a]).wait()

    hbm = lambda a: pltpu.HBM(a.shape, a.dtype)
    res = pl.pallas_call(
        body, name=name,
        out_shape=[hbm(s) for s in srcs] + [hbm(z) for z in lands],
        in_specs=[_HBM] * (2 * n) + [_SEM] * 3 + [pl.BlockSpec(memory_space=pl.ANY)], out_specs=[_HBM] * (2 * n),
        input_output_aliases={a: a for a in range(2 * n)},
        compiler_params=pltpu.CompilerParams(has_side_effects=_EFFECT),
    )(*srcs, *lands, *sems, after)
    return res[n:]


def _peer(x, y, c, k):
    return (1 - x if k & 4 else x, 1 - y if k & 2 else y, 1 - c if k & 1 else c)


def _bcast_start(arrs, name, after=None):
    n = len(arrs)
    zones = [lax.empty((8,) + a.shape, a.dtype) for a in arrs]
    extra = [] if after is None else [after]

    def body(*refs):
        ins, lands = refs[:n], refs[n:2 * n]
        send, recv, lsem = refs[2 * n + len(extra):2 * n + len(extra) + 3]
        refs[-1][...] = jnp.zeros_like(refs[-1])
        x, y, c = _position()
        for a in range(n):
            dst = lands[a].at[4 * x + 2 * y + c]
            pltpu.make_async_copy(ins[a], dst, lsem.at[a]).start()
            for k in range(1, 8):
                pltpu.make_async_remote_copy(src_ref=ins[a], dst_ref=dst, send_sem=send.at[7 * a + k - 1],
                                             recv_sem=recv.at[7 * a + k - 1], device_id=_peer(x, y, c, k),
                                             device_id_type=MESH).start()

    sems = [pltpu.SemaphoreType.DMA((7 * n,)), pltpu.SemaphoreType.DMA((7 * n,)), pltpu.SemaphoreType.DMA((n,))]
    hbm = lambda a: pltpu.HBM(a.shape, a.dtype)
    res = pl.pallas_call(
        body, name=name,
        out_shape=sems + [hbm(a) for a in arrs] + [hbm(z) for z in zones] + [_TOKEN],
        in_specs=[_HBM] * (2 * n) + [pl.BlockSpec(memory_space=pl.ANY)] * len(extra),
        out_specs=[_SEM] * 3 + [_HBM] * (2 * n) + [_VMEM_TOKEN],
        input_output_aliases={a: 3 + a for a in range(2 * n)},
        compiler_params=pltpu.CompilerParams(has_side_effects=_EFFECT),
    )(*[pltpu.with_memory_space_constraint(a, pltpu.HBM) for a in arrs],
      *[pltpu.with_memory_space_constraint(z, pltpu.HBM) for z in zones], *extra)
    return res[:3], res[3:3 + n], res[3 + n:3 + 2 * n], res[-1]


def _bcast_wait(sems, srcs, lands, after, name):
    n = len(srcs)

    def body(*refs):
        ins, zones = refs[:n], refs[n:2 * n]
        send, recv, lsem = refs[2 * n:2 * n + 3]
        x, y, c = _position()
        for a in range(n):
            for k in range(1, 8):
                cp = pltpu.make_async_remote_copy(src_ref=ins[a], dst_ref=zones[a].at[0], send_sem=send.at[7 * a + k - 1],
                                                  recv_sem=recv.at[7 * a + k - 1], device_id=_peer(x, y, c, k),
                                                  device_id_type=MESH)
                cp.wait_send()
                cp.wait_recv()
            pltpu.make_async_copy(ins[a], zones[a].at[0], lsem.at[a]).wait()

    hbm = lambda a: pltpu.HBM(a.shape, a.dtype)
    res = pl.pallas_call(
        body, name=name,
        out_shape=[hbm(s) for s in srcs] + [hbm(z) for z in lands],
        in_specs=[_HBM] * (2 * n) + [_SEM] * 3 + [pl.BlockSpec(memory_space=pl.ANY)], out_specs=[_HBM] * (2 * n),
        input_output_aliases={a: a for a in range(2 * n)},
        compiler_params=pltpu.CompilerParams(has_side_effects=_EFFECT),
    )(*srcs, *lands, *sems, after)
    return res[n:]


def _swap_start(arrs, name):
    n = len(arrs)
    zones = [lax.empty(a.shape, a.dtype) for a in arrs]

    def body(*refs):
        ins, lands = refs[:n], refs[n:2 * n]
        send, recv = refs[2 * n:2 * n + 2]
        refs[-1][...] = jnp.zeros_like(refs[-1])
        x, y, c = _position()
        for a in range(n):
            pltpu.make_async_remote_copy(src_ref=ins[a], dst_ref=lands[a], send_sem=send.at[a], recv_sem=recv.at[a],
                                         device_id=(x, y, 1 - c), device_id_type=MESH).start()

    sems = [pltpu.SemaphoreType.DMA((n,)), pltpu.SemaphoreType.DMA((n,))]
    hbm = lambda a: pltpu.HBM(a.shape, a.dtype)
    res = pl.pallas_call(
        body, name=name,
        out_shape=sems + [hbm(a) for a in arrs] + [hbm(z) for z in zones] + [_TOKEN],
        in_specs=[_HBM] * (2 * n), out_specs=[_SEM] * 2 + [_HBM] * (2 * n) + [_VMEM_TOKEN],
        input_output_aliases={a: 2 + a for a in range(2 * n)},
        compiler_params=pltpu.CompilerParams(has_side_effects=_EFFECT),
    )(*[pltpu.with_memory_space_constraint(a, pltpu.HBM) for a in arrs],
      *[pltpu.with_memory_space_constraint(z, pltpu.HBM) for z in zones])
    return res[:2], res[2:2 + n], res[2 + n:2 + 2 * n], res[-1]


def _swap_wait(sems, srcs, lands, after, name):
    n = len(srcs)

    def body(*refs):
        ins, zones = refs[:n], refs[n:2 * n]
        send, recv = refs[2 * n:2 * n + 2]
        x, y, c = _position()
        for a in range(n):
            cp = pltpu.make_async_remote_copy(src_ref=ins[a], dst_ref=zones[a], send_sem=send.at[a], recv_sem=recv.at[a],
                                              device_id=(x, y, 1 - c), device_id_type=MESH)
            cp.wait_send()
            cp.wait_recv()

    hbm = lambda a: pltpu.HBM(a.shape, a.dtype)
    res = pl.pallas_call(
        body, name=name,
        out_shape=[hbm(s) for s in srcs] + [hbm(z) for z in lands],
        in_specs=[_HBM] * (2 * n) + [_SEM] * 2 + [pl.BlockSpec(memory_space=pl.ANY)], out_specs=[_HBM] * (2 * n),
        input_output_aliases={a: a for a in range(2 * n)},
        compiler_params=pltpu.CompilerParams(has_side_effects=_EFFECT),
    )(*srcs, *lands, *sems, after)
    return res[:n], res[n:]


def _row_tile(r):
    for t in (256, 128, 64, 32, 16, 8):
        if r % t == 0 and r > t:
            return t
    return r


def _pair_add_half(g, rb, c_arr, name):
    hrows, rest = rb.shape[0], rb.shape[1:]
    tr = _row_tile(hrows)
    nb = hrows // tr
    z = (0,) * len(rest)

    def body(c_ref, g_ref, r_ref, o_ref):
        o_ref[...] = (g_ref[...].astype(F32) + r_ref[...].astype(F32)).astype(o_ref.dtype)

    return pl.pallas_call(
        body, name=name,
        grid_spec=pltpu.PrefetchScalarGridSpec(
            num_scalar_prefetch=1, grid=(nb,),
            in_specs=[pl.BlockSpec((tr,) + rest, lambda i, c_ref: (c_ref[0] * nb + i,) + z),
                      pl.BlockSpec((tr,) + rest, lambda i, c_ref: (i,) + z)],
            out_specs=pl.BlockSpec((tr,) + rest, lambda i, c_ref: (i,) + z)),
        out_shape=jax.ShapeDtypeStruct((hrows,) + rest, BF16),
        compiler_params=pltpu.CompilerParams(dimension_semantics=("parallel",), vmem_limit_bytes=VMEM_LIMIT),
    )(c_arr, g, rb)


def _sum_slabs(slabs, l, buf, name):
    m = len(slabs)
    n, R, rest = slabs[0].shape[0], slabs[0].shape[1], slabs[0].shape[2:]
    tr = _row_tile(R)
    z = (0,) * len(rest)

    def body(*refs):
        total = None
        for r_ref in refs[:m]:
            acc = r_ref[0].astype(F32)
            for k in range(1, n):
                acc = acc + r_ref[k].astype(F32)
            total = acc if total is None else total + acc
        refs[-1][...] = total

    if R // tr > 64 and len(rest) == 1 and rest[0] % 256 == 0:
        grid = (rest[0] // 256,)
        in_spec = pl.BlockSpec((n, R, 256), lambda i: (0, 0, i))
        out_spec = pl.BlockSpec((None, R, 256), lambda i: (l, 0, i))
    else:
        grid = (R // tr,)
        in_spec = pl.BlockSpec((n, tr) + rest, lambda i: (0, i) + z)
        out_spec = pl.BlockSpec((None, tr) + rest, lambda i: (l, i) + z)
    in_specs, args, aliases = [in_spec] * m, list(slabs), {}
    if buf is not None:
        in_specs.append(pl.BlockSpec(memory_space=pl.ANY))
        args.append(buf)
        aliases = {m: 0}
    return pl.pallas_call(
        body, name=name, grid=grid, in_specs=in_specs, out_specs=out_spec,
        out_shape=jax.ShapeDtypeStruct((DEPTH, R) + rest, F32), input_output_aliases=aliases,
        compiler_params=pltpu.CompilerParams(
            dimension_semantics=("parallel",),
            vmem_limit_bytes=_vmem(m * _nbytes(in_spec.block_shape, slabs[0].dtype) + _nbytes(out_spec.block_shape, F32),
                                   2 * _nbytes(out_spec.block_shape, F32))),
    )(*args)


def _adam_math(w, g, m, v):
    mn = ADAM_B1 * m + (1.0 - ADAM_B1) * g
    vn = ADAM_B2 * v + (1.0 - ADAM_B2) * (g * g)
    m_hat = mn / (1.0 - ADAM_B1 ** ADAM_STEP)
    v_hat = vn / (1.0 - ADAM_B2 ** ADAM_STEP)
    return -ADAM_LR * (m_hat / (jnp.sqrt(v_hat) + ADAM_EPS) + ADAM_WD * w), mn, vn


def _adamw(w, g, m, v, name, token=None):
    L, R, C = w.shape
    tr = _row_tile(R)
    extra = [] if token is None else [token]

    def body(w_ref, g_ref, m_ref, v_ref, *rest):
        d_ref, mo_ref, vo_ref = rest[-3:]
        d_ref[...], mo_ref[...], vo_ref[...] = _adam_math(w_ref[...], g_ref[...], m_ref[...], v_ref[...])

    if R // tr > 64 and C % 128 == 0:
        spec, grid = pl.BlockSpec((None, R, 128), lambda l, i: (l, 0, i)), (L, C // 128)
    else:
        spec, grid = pl.BlockSpec((None, tr, C), lambda l, i: (l, i, 0)), (L, R // tr)
    return pl.pallas_call(
        body, name=name, grid=grid, in_specs=[spec] * 4 + [pl.BlockSpec(memory_space=pl.ANY)] * len(extra),
        out_specs=[spec] * 3, out_shape=[jax.ShapeDtypeStruct((L, R, C), F32)] * 3,
        compiler_params=pltpu.CompilerParams(dimension_semantics=("parallel", "parallel"),
                                             vmem_limit_bytes=_vmem(7 * _nbytes(spec.block_shape, F32))),
    )(w, g, m, v, *extra)


_VMEM_WHOLE = pl.BlockSpec(memory_space=pltpu.VMEM)


def _matrix_update(gath, w, m, v, name):
    K = w.shape[1]

    def body(g0_ref, g1_ref, w_ref, m_ref, v_ref, go_ref, d_ref, mo_ref, vo_ref):
        for l, gr in enumerate((g0_ref, g1_ref)):
            for k in range(K):
                g = gr[0, k].astype(F32)
                for dev in range(1, 8):
                    g = g + gr[dev, k].astype(F32)
                go_ref[l, k] = g
                d_ref[l, k], mo_ref[l, k], vo_ref[l, k] = _adam_math(w_ref[l, k], g, m_ref[l, k], v_ref[l, k])

    return pl.pallas_call(
        body, name=name, in_specs=[_VMEM_WHOLE] * 5, out_specs=[_VMEM_WHOLE] * 4,
        out_shape=[jax.ShapeDtypeStruct(w.shape, F32)] * 4,
        compiler_params=pltpu.CompilerParams(vmem_limit_bytes=32 * MIB),
    )(gath[0], gath[1], w, m, v)


VECS = (("pre_norm_g", D), ("post_norm_g", D), ("gm_ln_g", GM_W), ("gm_ln_b", GM_W), ("mla_q_norm_g", QR),
        ("mla_kv_norm_g", KVR), ("lru_conv_b", LRU_W), ("lru_b_a", LRU_W), ("lru_b_x", LRU_W), ("lru_lambda", LRU_W))
VEC_KEY = {"pre_norm_g": "pre_g", "post_norm_g": "post_g", "gm_ln_g": "ln_g", "gm_ln_b": "ln_b", "mla_q_norm_g": "qg",
           "mla_kv_norm_g": "kvg", "lru_conv_b": "conv_b", "lru_b_a": "ba", "lru_b_x": "bx", "lru_lambda": "lam"}
VEC_ROWS, VEC_W, VEC_ROW0, LOSS_ROW = 16, LRU_W, GM_G, 14


def _pack_rows(LG, loss_part):
    per = len(VECS) + 1
    ins = []
    for G in LG:
        ins += [G[VEC_KEY[n]] for n, _ in VECS] + [G["bst"]]
    ins.append(loss_part)

    def body(*refs):
        o_ref = refs[-1]
        o_ref[...] = jnp.zeros_like(o_ref)
        for l in range(DEPTH):
            base = VEC_ROWS * l
            o_ref[pl.ds(base, 8), pl.ds(0, GM_B)] = refs[per * l + len(VECS)][...].T[:8, :]
            for t, (_, width) in enumerate(VECS):
                o_ref[pl.ds(base + VEC_ROW0 + t, 1), pl.ds(0, width)] = refs[per * l + t][...]
        o_ref[pl.ds(LOSS_ROW, 1), pl.ds(0, 128)] = jnp.broadcast_to(refs[-2][...], (1, 128))

    return pl.pallas_call(
        body, name="pack_rows", in_specs=[_VMEM_WHOLE] * len(ins), out_specs=_VMEM_WHOLE,
        out_shape=jax.ShapeDtypeStruct((DEPTH * VEC_ROWS, VEC_W), F32),
    )(*ins)


def _vector_update(gath, W, M, V):
    names = [n for n, _ in VECS] + ["gm_bs"]
    nw = len(names)

    def body(*refs):
        g_ref = refs[0]
        wr, mr, vr = refs[1:1 + nw], refs[1 + nw:1 + 2 * nw], refs[1 + 2 * nw:1 + 3 * nw]
        outs = refs[1 + 3 * nw:]
        s = g_ref[0]
        for dev in range(1, 8):
            s = s + g_ref[dev]
        for t, (_, width) in enumerate(VECS):
            for l in range(DEPTH):
                r = VEC_ROWS * l + VEC_ROW0 + t
                g = s[r:r + 1, :width]
                row = (pl.ds(l, 1), slice(None))
                res = (g,) + _adam_math(wr[t][row], g, mr[t][row], vr[t][row])
                for q in range(4):
                    outs[4 * t + q][row] = res[q]
        t = len(VECS)
        for l in range(DEPTH):
            for k in range(GM_G):
                g = s[VEC_ROWS * l + k:VEC_ROWS * l + k + 1, :GM_B]
                row = (l, pl.ds(k, 1), slice(None))
                res = (g,) + _adam_math(wr[t][row], g, mr[t][row], vr[t][row])
                for q in range(4):
                    outs[4 * t + q][row] = res[q]
        outs[4 * nw][...] = s[LOSS_ROW:LOSS_ROW + 1, :128]

    ws = [W[n] for n in names]
    out_shape = []
    for w in ws:
        out_shape += [jax.ShapeDtypeStruct(w.shape, F32)] * 4
    out_shape.append(jax.ShapeDtypeStruct((1, 128), F32))
    res = pl.pallas_call(
        body, name="vector_update", in_specs=[_VMEM_WHOLE] * (1 + 3 * nw), out_specs=[_VMEM_WHOLE] * (4 * nw + 1),
        out_shape=out_shape, compiler_params=pltpu.CompilerParams(vmem_limit_bytes=VMEM_LIMIT),
    )(gath, *ws, *[M[n] for n in names], *[V[n] for n in names])
    return {n: tuple(res[4 * t:4 * t + 4]) for t, n in enumerate(names)}, res[4 * nw]


SHARDED = ("w_in", "mla_w_uq", "mla_w_ukv", "lru_conv_w", "w_proj_a", "w_proj_b", "w_proj_c", "w_out")
FIRST = ("w_in", "lru_conv_w")
LATER = tuple(n for n in SHARDED if n not in FIRST)
COL_SHARDED = ("w_in", "mla_w_uq", "mla_w_ukv", "lru_conv_w")
SMALL = ("pre_norm_g", "gm_ln_g", "gm_ln_b", "gm_ws", "gm_bs", "mla_q_norm_g", "mla_kv_norm_g", "lru_conv_b",
         "lru_w_a", "lru_b_a", "lru_w_x", "lru_b_x", "lru_lambda", "post_norm_g")
WEIGHTS = ("pre_norm_g", "w_in", "gm_ln_g", "gm_ln_b", "gm_ws", "gm_bs", "mla_q_norm_g", "mla_w_uq",
           "mla_kv_norm_g", "mla_w_ukv", "lru_conv_w", "lru_conv_b", "lru_w_a", "lru_b_a", "lru_w_x", "lru_b_x",
           "lru_lambda", "w_proj_a", "w_proj_b", "w_proj_c", "w_out", "post_norm_g")


GB_KEY = {"w_in": "wp", "mla_w_uq": "wuq", "mla_w_ukv": "wukv", "w_proj_a": "wpa", "w_proj_b": "wpb",
          "w_proj_c": "wpc", "w_out": "wout"}


def _prepare(l, gathered, small, wsb):
    P = {GB_KEY[n]: gathered[n] for n in GB_KEY if n in gathered}
    P["conv_w"] = gathered["lru_conv_w"].transpose(1, 0, 2).reshape(CONV_W, LRU_W)
    P["wsb"] = wsb
    row = lambda n: small[n][l][None, :]
    P["pre_g"], P["post_g"] = row("pre_norm_g"), row("post_norm_g")
    P["ln_g"], P["ln_b"] = row("gm_ln_g"), row("gm_ln_b")
    P["ws"] = small["gm_ws"][l]
    P["bst"] = jnp.pad(small["gm_bs"][l].T, ((0, 0), (0, 128 - GM_G)))
    P["qg"], P["kvg"] = row("mla_q_norm_g"), row("mla_kv_norm_g")
    P["conv_b"], P["ba"], P["bx"], P["lam"] = row("lru_conv_b"), row("lru_b_a"), row("lru_b_x"), row("lru_lambda")
    return P


def kernel(x, pre_norm_g, w_in, gm_ln_g, gm_ln_b, gm_ws, gm_bs, mla_q_norm_g, mla_w_uq, mla_kv_norm_g, mla_w_ukv, lru_conv_w, lru_conv_b, lru_w_a, lru_b_a, lru_w_x, lru_b_x, lru_lambda, w_proj_a, w_proj_b, w_proj_c, w_out, post_norm_g, loss_target, m_pre_norm_g, m_w_in, m_gm_ln_g, m_gm_ln_b, m_gm_ws, m_gm_bs, m_mla_q_norm_g, m_mla_w_uq, m_mla_kv_norm_g, m_mla_w_ukv, m_lru_conv_w, m_lru_conv_b, m_lru_w_a, m_lru_b_a, m_lru_w_x, m_lru_b_x, m_lru_lambda, m_w_proj_a, m_w_proj_b, m_w_proj_c, m_w_out, m_post_norm_g, v_pre_norm_g, v_w_in, v_gm_ln_g, v_gm_ln_b, v_gm_ws, v_gm_bs, v_mla_q_norm_g, v_mla_w_uq, v_mla_kv_norm_g, v_mla_w_ukv, v_lru_conv_w, v_lru_conv_b, v_lru_w_a, v_lru_b_a, v_lru_w_x, v_lru_b_x, v_lru_lambda, v_w_proj_a, v_w_proj_b, v_w_proj_c, v_w_out, v_post_norm_g):
    args = dict(locals())
    W = {n: args[n] for n in WEIGHTS}
    M = {n: args["m_" + n] for n in WEIGHTS}
    V = {n: args["v_" + n] for n in WEIGHTS}
    c = lax.axis_index("c")

    def shards(l, names):
        out = []
        for n in names:
            blk = W[n][l].T if n in TRANSPOSED else W[n][l]
            out.append(blk[None] if n == "lru_conv_w" else blk.astype(BF16))
        return out

    small = {n: W[n] for n in SMALL}
    wsb = _superblocks(W["lru_w_a"], W["lru_w_x"])
    tabs = _rope_tables()
    s0a, s0b, s1a, s1b = shards(0, FIRST), shards(0, LATER), shards(1, FIRST), shards(1, LATER)
    g0, zones = _weights_allgather(FIRST, s0a, "weights_allgather_l0", carry=_gather_zeros(LATER, s0b)
                                   + _gather_zeros(FIRST, s1a) + _gather_zeros(LATER, s1b))
    nl, nf = len(LATER), len(FIRST)
    w0b = _gather_start(LATER, s0b, zones[:nl], "weights_gather_start_l0")
    w1a = _gather_start(FIRST, s1a, zones[nl:nl + nf], "weights_gather_start_l1_first", after=w0b[3])
    w1b = _gather_start(LATER, s1b, zones[nl + nf:], "weights_gather_start_l1_later", after=w1a[3])

    def late(started, name):
        def wait(after):
            got = _gather_wait(LATER, *started[:3], after, name)
            return {GB_KEY[n]: g for n, g in zip(LATER, got)}
        return wait

    P = [_prepare(0, dict(zip(FIRST, g0)), small, wsb), None]
    h0 = x[0]
    (h1, hn1), A0 = _layer_fwd(h0, P[0], 0, tabs, w1b[3], late(w0b, "weights_gather_wait_l0"),
                               next_g=small["pre_norm_g"][1][None, :])
    g1 = _gather_wait(FIRST, *w1a[:3], h1, "weights_gather_wait_l1_first")
    P[1] = _prepare(1, dict(zip(FIRST, g1)), small, wsb)
    h2, A1 = _layer_fwd(h1, P[1], 1, tabs, None, late(w1b, "weights_gather_wait_l1_later"), h=hn1)
    dy, loss_part = _loss_fwd(h2, loss_target[0])

    def large_grads(G, GB, names):
        conv = G["conv_w"].reshape(CONV_W, N_CHIPS, LRU_W // N_CHIPS).transpose(1, 0, 2)
        return [conv if n == "lru_conv_w" else GB[GB_KEY[n]] for n in names]

    started = {}

    def early1(GB):
        started["sc1b"] = _scatter_start(LATER, [GB[GB_KEY[n]] for n in LATER], "grads_scatter_start_l1_later")
        return started["sc1b"][3], None

    d1, G1, GB1 = _layer_bwd(dy, A1, P[1], 1, tabs, None, early1)
    sc1a = _scatter_start(FIRST, large_grads(G1, GB1, FIRST), "grads_scatter_start_l1_first")

    def early0(GB):
        got_b = _scatter_wait(LATER, *started["sc1b"][:3], GB["wukv"], "grads_scatter_wait_l1_later")
        got_a = _scatter_wait(FIRST, *sc1a[:3], got_b[0], "grads_scatter_wait_l1_first")
        got = dict(zip(LATER + FIRST, list(got_b) + list(got_a)))
        started["swap1"] = _swap_start([got[n] for n in SHARDED], "partials_swap_start_l1")
        started["sc0"] = _scatter_start(LATER, [GB[GB_KEY[n]] for n in LATER], "grads_scatter_start_l0")
        return started["sc0"][3], started["swap1"][3]

    d0, G0, GB0 = _layer_bwd(d1, A0, P[0], 0, tabs, sc1a[3], early0)
    LG = (G0, G1)
    mine0 = _scatter_wait(LATER, *started["sc0"][:3], d0, "grads_scatter_wait_l0")
    swap0 = _swap_start(mine0, "partials_swap_start_l0")
    g0f = large_grads(G0, GB0, FIRST)
    c_arr = jnp.reshape(c, (1,)).astype(jnp.int32)
    from_sib = _half_to_sibling(FIRST, g0f, "grads_half_to_sibling_l0", after=swap0[3])
    pair = [_pair_add_half(g, rb, c_arr, "pair_add_" + n) for n, g, rb in zip(FIRST, g0f, from_sib)]
    slabs = _chip_scatter_half(FIRST, pair, "grads_chip_scatter_l0")
    mats = []
    for g in LG:
        mats += [g["ws"].astype(BF16), g["wab"][0, :, :, :LRU_BW], g["wab"][1, :, :, :LRU_BW]]
    bc = _bcast_start([_pack_rows(LG, loss_part)] + mats, "small_grads_start", after=slabs[0])
    mine1, theirs1 = _swap_wait(*started["swap1"][:3], bc[3], "partials_swap_wait_l1")
    both = dict(zip(SHARDED, [_sum_slabs([a, b], 1, None, "sum_partials_l1_" + n)
                              for n, a, b in zip(SHARDED, mine1, theirs1)]))
    for n, s in zip(FIRST, slabs):
        both[n] = _sum_slabs([s], 0, both[n], "sum_slabs_l0_" + n)
    done = _subset_exchange(FIRST, [both[n] for n in FIRST], 0, "reduced_rows_to_sibling_l0")
    both.update(zip(FIRST, done))
    mine0, theirs0 = _swap_wait(*swap0[:3], done[0], "partials_swap_wait_l0")
    for n, a, b in zip(LATER, mine0, theirs0):
        both[n] = _sum_slabs([a, b], 0, both[n], "sum_partials_l0_" + n)
    both = [both[n] for n in SHARDED]
    grads = {}
    for n, b in zip(SHARDED, both):
        if n in TRANSPOSED and n != "w_in":
            b = jnp.swapaxes(b, 1, 2)
        grads[n] = b if n == "w_in" else b.reshape(W[n].shape)

    upd, last = {}, None
    for n in SHARDED:
        token = bc[3] if n == SHARDED[0] else None
        if n == "w_in":
            tr = lambda a: jnp.swapaxes(a, 1, 2)
            res = _adamw(tr(W[n]), grads[n], tr(M[n]), tr(V[n]), "adamw_" + n, token)
            upd[n] = tuple(tr(a) for a in (grads[n],) + tuple(res))
        else:
            res = _adamw(W[n], grads[n], M[n], V[n], "adamw_" + n, token)
            upd[n] = (grads[n],) + tuple(res)
        last = res[0]

    gath = _bcast_wait(*bc[:3], last, "small_grads_wait")
    vec_upd, loss_row = _vector_update(gath[0], W, M, V)
    upd.update(vec_upd)
    loss = loss_row[0, 0]
    for k, n in enumerate(("gm_ws", "lru_w_a", "lru_w_x")):
        upd[n] = _matrix_update((gath[1 + k], gath[4 + k]), W[n], M[n], V[n], "update_" + n)

    return (loss, d0[None], *[upd[n][0] for n in WEIGHTS], *[upd[n][1] for n in WEIGHTS],
            *[upd[n][2] for n in WEIGHTS], *[upd[n][3] for n in WEIGHTS])
```
